```python
import math
import jax
import jax.numpy as jnp
from jax import lax
import numpy as np

D_MODEL = 1024
BATCH = 8
SEQ = 4096
DEPTH = 1

HEAD_DIM = 64
SSD_HEADS = 16
SSD_GROUPS = 2
SSD_STATE = 128
SSD_CONV = 4
SSD_CHUNK = 128
SSD_WIDTH = SSD_HEADS * HEAD_DIM
SSD_BC_WIDTH = SSD_GROUPS * SSD_STATE
SSD_XBC_WIDTH = SSD_WIDTH + 2 * SSD_BC_WIDTH
FOX_HEADS = 16
FOX_WIDTH = FOX_HEADS * HEAD_DIM
Q_BLOCK = 128
MIX_WIDTH = SSD_WIDTH + FOX_WIDTH
D_FF = 2816
FFN_CONV = 3
NORM_EPS = 1e-6
Z_END = SSD_WIDTH
XBC_END = Z_END + SSD_XBC_WIDTH
DT_END = XBC_END + SSD_HEADS
Q_END = DT_END + FOX_WIDTH
K_END = Q_END + FOX_WIDTH
V_END = K_END + FOX_WIDTH
IN_COLS = V_END + FOX_HEADS

kernel_name = 'hybrid_ssd_fox_convffn_layer'


def rms_norm(x, w):
    xf = x.astype(jnp.float32)
    y = xf * lax.rsqrt(jnp.mean(xf * xf, axis=-1, keepdims=True) + NORM_EPS)
    return (y * w.astype(jnp.float32)).astype(x.dtype)


def causal_depthwise_conv(x, w, b):
    k = w.shape[0]
    y = lax.conv_general_dilated(
        x, w[:, None, :].astype(x.dtype), window_strides=(1,), padding=[(k - 1, 0)],
        dimension_numbers=('NWC', 'WIO', 'NWC'), feature_group_count=x.shape[-1])
    return y + b.astype(x.dtype)


def ssd_mixer(xbc_raw, z, dt_raw, conv_w, conv_b, dt_bias, a_log, d_skip, norm_w):
    b, s, _ = xbc_raw.shape
    nc = s // SSD_CHUNK
    e = SSD_HEADS // SSD_GROUPS
    xbc = jax.nn.silu(causal_depthwise_conv(xbc_raw, conv_w, conv_b))
    xs, bm, cm = jnp.split(xbc, [SSD_WIDTH, SSD_WIDTH + SSD_BC_WIDTH], axis=-1)
    x6 = xs.reshape(b, nc, SSD_CHUNK, SSD_GROUPS, e, HEAD_DIM)
    bm = bm.reshape(b, nc, SSD_CHUNK, SSD_GROUPS, SSD_STATE)
    cm = cm.reshape(b, nc, SSD_CHUNK, SSD_GROUPS, SSD_STATE)
    dt = jax.nn.softplus(dt_raw.astype(jnp.float32) + dt_bias.astype(jnp.float32))
    a = -jnp.exp(a_log.astype(jnp.float32))
    dt5 = dt.reshape(b, nc, SSD_CHUNK, SSD_GROUPS, e)
    a_cs = jnp.cumsum(dt5 * a.reshape(SSD_GROUPS, e), axis=2)
    xdt = x6 * dt5[..., None]
    at = jnp.moveaxis(a_cs, 2, -1)
    seg = at[..., :, None] - at[..., None, :]
    tri = jnp.tril(jnp.ones((SSD_CHUNK, SSD_CHUNK), dtype=bool))
    decay = jnp.exp(jnp.where(tri, seg, -jnp.inf))
    cb = jnp.einsum('bclgn,bcsgn->bcgls', cm, bm)
    scores = cb[:, :, :, None] * decay
    y_diag = jnp.einsum('bcgels,bcsgep->bclgep', scores, xdt)
    decay_to_end = jnp.exp(a_cs[:, :, -1:] - a_cs)
    chunk_states = jnp.einsum('bclgn,bclgep->bcgepn', bm, xdt * decay_to_end[..., None])
    chunk_decay = jnp.exp(a_cs[:, :, -1])

    def step(h, inp):
        dec, st = inp
        return h * dec[..., None, None] + st, h

    h0 = jnp.zeros((b, SSD_GROUPS, e, HEAD_DIM, SSD_STATE), jnp.float32)
    _, states_in = lax.scan(step, h0, (jnp.moveaxis(chunk_decay, 1, 0), jnp.moveaxis(chunk_states, 1, 0)))
    states_in = jnp.moveaxis(states_in, 0, 1)
    y_off = jnp.einsum('bclgn,bcgepn->bclgep', cm, states_in) * jnp.exp(a_cs)[..., None]
    y = y_diag + y_off + d_skip.astype(jnp.float32).reshape(SSD_GROUPS, e)[:, :, None] * x6
    y = y.reshape(b, s, SSD_WIDTH)
    yg = (y * jax.nn.silu(z.astype(jnp.float32))).reshape(b, s, SSD_GROUPS, SSD_WIDTH // SSD_GROUPS)
    yg = yg * lax.rsqrt(jnp.mean(yg * yg, axis=-1, keepdims=True) + NORM_EPS)
    return (yg.reshape(b, s, SSD_WIDTH) * norm_w.astype(jnp.float32)).astype(xbc_raw.dtype)


def fox_mixer(q, k, v, f_raw, f_bias, q_norm_w, k_norm_w):
    b, s, _ = q.shape

    def heads(t):
        return t.reshape(b, s, FOX_HEADS, HEAD_DIM).transpose(0, 2, 1, 3)

    qh = rms_norm(heads(q), q_norm_w)
    kh = rms_norm(heads(k), k_norm_w)
    vh = heads(v)
    log_f = jax.nn.log_sigmoid(f_raw.astype(jnp.float32) + f_bias.astype(jnp.float32))
    cum = jnp.cumsum(log_f, axis=1).transpose(0, 2, 1)
    scale = HEAD_DIM ** -0.5
    outs = []
    for start in range(0, s, Q_BLOCK):
        end = start + Q_BLOCK
        logits = jnp.einsum('bhqd,bhkd->bhqk', qh[:, :, start:end], kh[:, :, :end],
                            preferred_element_type=jnp.float32) * scale
        logits = logits + cum[:, :, start:end, None] - cum[:, :, None, :end]
        causal = (start + jnp.arange(Q_BLOCK))[:, None] >= jnp.arange(end)[None, :]
        p = jax.nn.softmax(jnp.where(causal, logits, -jnp.inf), axis=-1)
        outs.append(jnp.einsum('bhqk,bhkd->bhqd', p.astype(vh.dtype), vh[:, :, :end]))
    o = jnp.concatenate(outs, axis=2)
    return o.transpose(0, 2, 1, 3).reshape(b, s, FOX_WIDTH)


def conv_gated_mlp(x, w_up, conv_w, conv_b, w_down):
    h = x @ w_up.astype(x.dtype)
    h = causal_depthwise_conv(h, conv_w, conv_b)
    gate, val = jnp.split(h, 2, axis=-1)
    return (jax.nn.silu(gate) * val) @ w_down.astype(x.dtype)


def _fwd_setup_inputs(seed: int = 0) -> dict:
    key = jax.random.key(seed)
    ks = jax.random.split(key, 20)
    L = DEPTH

    def nrm(k, shape, scale):
        return jax.random.normal(k, shape, jnp.float32) * scale

    def gain(k, n):
        return 1.0 + 0.02 * jax.random.normal(k, (L, n), jnp.float32)

    dt = jnp.exp(jax.random.uniform(ks[5], (L, SSD_HEADS), jnp.float32, math.log(1e-3), math.log(1e-1)))
    dt_bias = dt + jnp.log(-jnp.expm1(-dt))
    return {
        'x': nrm(ks[0], (BATCH, SEQ, D_MODEL), 1.0),
        'norm_mix_w': gain(ks[1], D_MODEL),
        'w_in': nrm(ks[2], (L, D_MODEL, IN_COLS), D_MODEL ** -0.5),
        'ssd_conv_w': nrm(ks[3], (L, SSD_CONV, SSD_XBC_WIDTH), SSD_CONV ** -0.5),
        'ssd_conv_b': nrm(ks[4], (L, SSD_XBC_WIDTH), 0.02),
        'ssd_dt_bias': dt_bias,
        'ssd_a_log': jnp.log(jax.random.uniform(ks[6], (L, SSD_HEADS), jnp.float32, 1.0, 16.0)),
        'ssd_d': 1.0 + 0.1 * jax.random.normal(ks[7], (L, SSD_HEADS), jnp.float32),
        'ssd_norm_w': gain(ks[8], SSD_WIDTH),
        'fox_f_bias': jax.random.uniform(ks[9], (L, FOX_HEADS), jnp.float32, 2.0, 5.0),
        'fox_q_norm_w': gain(ks[10], HEAD_DIM),
        'fox_k_norm_w': gain(ks[11], HEAD_DIM),
        'w_out': nrm(ks[12], (L, MIX_WIDTH, D_MODEL), MIX_WIDTH ** -0.5),
        'norm_ffn_w': gain(ks[13], D_MODEL),
        'w_up': nrm(ks[14], (L, D_MODEL, 2 * D_FF), D_MODEL ** -0.5),
        'ffn_conv_w': nrm(ks[15], (L, FFN_CONV, 2 * D_FF), FFN_CONV ** -0.5),
        'ffn_conv_b': nrm(ks[16], (L, 2 * D_FF), 0.02),
        'w_down': nrm(ks[17], (L, D_FF, D_MODEL), D_FF ** -0.5),
    }


def _fwd_reference(x, norm_mix_w, w_in, ssd_conv_w, ssd_conv_b, ssd_dt_bias, ssd_a_log, ssd_d,
              ssd_norm_w, fox_f_bias, fox_q_norm_w, fox_k_norm_w, w_out, norm_ffn_w,
              w_up, ffn_conv_w, ffn_conv_b, w_down):
    for layer in range(DEPTH):
        h = rms_norm(x, norm_mix_w[layer])
        proj = h @ w_in[layer].astype(h.dtype)
        z, xbc, dt_raw, q, k, v, f_raw = jnp.split(
            proj, [Z_END, XBC_END, DT_END, Q_END, K_END, V_END], axis=-1)
        y_ssd = ssd_mixer(xbc, z, dt_raw, ssd_conv_w[layer], ssd_conv_b[layer], ssd_dt_bias[layer],
                          ssd_a_log[layer], ssd_d[layer], ssd_norm_w[layer])
        y_fox = fox_mixer(q, k, v, f_raw, fox_f_bias[layer], fox_q_norm_w[layer], fox_k_norm_w[layer])
        mixed = jnp.concatenate([y_ssd, y_fox.astype(y_ssd.dtype)], axis=-1)
        x = x + (mixed @ w_out[layer].astype(mixed.dtype)).astype(x.dtype)
        hf = rms_norm(x, norm_ffn_w[layer])
        x = x + conv_gated_mlp(hf, w_up[layer], ffn_conv_w[layer], ffn_conv_b[layer], w_down[layer]).astype(x.dtype)
    return x


import jax as _jax
import jax.numpy as _jnp

TWIN_FORMAT = 'train_step'
FWD_PARAMS = ['x', 'norm_mix_w', 'w_in', 'ssd_conv_w', 'ssd_conv_b', 'ssd_dt_bias', 'ssd_a_log', 'ssd_d', 'ssd_norm_w', 'fox_f_bias', 'fox_q_norm_w', 'fox_k_norm_w', 'w_out', 'norm_ffn_w', 'w_up', 'ffn_conv_w', 'ffn_conv_b', 'w_down']
TWIN_WEIGHTS = ['norm_mix_w', 'w_in', 'ssd_conv_w', 'ssd_conv_b', 'ssd_dt_bias', 'ssd_a_log', 'ssd_d', 'ssd_norm_w', 'fox_f_bias', 'fox_q_norm_w', 'fox_k_norm_w', 'w_out', 'norm_ffn_w', 'w_up', 'ffn_conv_w', 'ffn_conv_b', 'w_down']
TWIN_DIFF_INPUT = 'x'
TWIN_INPUTS = ['x', 'norm_mix_w', 'w_in', 'ssd_conv_w', 'ssd_conv_b', 'ssd_dt_bias', 'ssd_a_log', 'ssd_d', 'ssd_norm_w', 'fox_f_bias', 'fox_q_norm_w', 'fox_k_norm_w', 'w_out', 'norm_ffn_w', 'w_up', 'ffn_conv_w', 'ffn_conv_b', 'w_down', 'loss_target', 'm_norm_mix_w', 'm_w_in', 'm_ssd_conv_w', 'm_ssd_conv_b', 'm_ssd_dt_bias', 'm_ssd_a_log', 'm_ssd_d', 'm_ssd_norm_w', 'm_fox_f_bias', 'm_fox_q_norm_w', 'm_fox_k_norm_w', 'm_w_out', 'm_norm_ffn_w', 'm_w_up', 'm_ffn_conv_w', 'm_ffn_conv_b', 'm_w_down', 'v_norm_mix_w', 'v_w_in', 'v_ssd_conv_w', 'v_ssd_conv_b', 'v_ssd_dt_bias', 'v_ssd_a_log', 'v_ssd_d', 'v_ssd_norm_w', 'v_fox_f_bias', 'v_fox_q_norm_w', 'v_fox_k_norm_w', 'v_w_out', 'v_norm_ffn_w', 'v_w_up', 'v_ffn_conv_w', 'v_ffn_conv_b', 'v_w_down']
TWIN_OUTPUTS = ['loss', 'grad_x', 'grad_norm_mix_w', 'grad_w_in', 'grad_ssd_conv_w', 'grad_ssd_conv_b', 'grad_ssd_dt_bias', 'grad_ssd_a_log', 'grad_ssd_d', 'grad_ssd_norm_w', 'grad_fox_f_bias', 'grad_fox_q_norm_w', 'grad_fox_k_norm_w', 'grad_w_out', 'grad_norm_ffn_w', 'grad_w_up', 'grad_ffn_conv_w', 'grad_ffn_conv_b', 'grad_w_down', 'delta_norm_mix_w', 'delta_w_in', 'delta_ssd_conv_w', 'delta_ssd_conv_b', 'delta_ssd_dt_bias', 'delta_ssd_a_log', 'delta_ssd_d', 'delta_ssd_norm_w', 'delta_fox_f_bias', 'delta_fox_q_norm_w', 'delta_fox_k_norm_w', 'delta_w_out', 'delta_norm_ffn_w', 'delta_w_up', 'delta_ffn_conv_w', 'delta_ffn_conv_b', 'delta_w_down', 'new_m_norm_mix_w', 'new_m_w_in', 'new_m_ssd_conv_w', 'new_m_ssd_conv_b', 'new_m_ssd_dt_bias', 'new_m_ssd_a_log', 'new_m_ssd_d', 'new_m_ssd_norm_w', 'new_m_fox_f_bias', 'new_m_fox_q_norm_w', 'new_m_fox_k_norm_w', 'new_m_w_out', 'new_m_norm_ffn_w', 'new_m_w_up', 'new_m_ffn_conv_w', 'new_m_ffn_conv_b', 'new_m_w_down', 'new_v_norm_mix_w', 'new_v_w_in', 'new_v_ssd_conv_w', 'new_v_ssd_conv_b', 'new_v_ssd_dt_bias', 'new_v_ssd_a_log', 'new_v_ssd_d', 'new_v_ssd_norm_w', 'new_v_fox_f_bias', 'new_v_fox_q_norm_w', 'new_v_fox_k_norm_w', 'new_v_w_out', 'new_v_norm_ffn_w', 'new_v_w_up', 'new_v_ffn_conv_w', 'new_v_ffn_conv_b', 'new_v_w_down']
TWIN_LEAF_KINDS = {'loss': 'loss', 'grad_x': 'grad_x', 'grad_norm_mix_w': 'grad_w', 'grad_w_in': 'grad_w', 'grad_ssd_conv_w': 'grad_w', 'grad_ssd_conv_b': 'grad_w', 'grad_ssd_dt_bias': 'grad_w', 'grad_ssd_a_log': 'grad_w', 'grad_ssd_d': 'grad_w', 'grad_ssd_norm_w': 'grad_w', 'grad_fox_f_bias': 'grad_w', 'grad_fox_q_norm_w': 'grad_w', 'grad_fox_k_norm_w': 'grad_w', 'grad_w_out': 'grad_w', 'grad_norm_ffn_w': 'grad_w', 'grad_w_up': 'grad_w', 'grad_ffn_conv_w': 'grad_w', 'grad_ffn_conv_b': 'grad_w', 'grad_w_down': 'grad_w', 'delta_norm_mix_w': 'delta_w', 'delta_w_in': 'delta_w', 'delta_ssd_conv_w': 'delta_w', 'delta_ssd_conv_b': 'delta_w', 'delta_ssd_dt_bias': 'delta_w', 'delta_ssd_a_log': 'delta_w', 'delta_ssd_d': 'delta_w', 'delta_ssd_norm_w': 'delta_w', 'delta_fox_f_bias': 'delta_w', 'delta_fox_q_norm_w': 'delta_w', 'delta_fox_k_norm_w': 'delta_w', 'delta_w_out': 'delta_w', 'delta_norm_ffn_w': 'delta_w', 'delta_w_up': 'delta_w', 'delta_ffn_conv_w': 'delta_w', 'delta_ffn_conv_b': 'delta_w', 'delta_w_down': 'delta_w', 'new_m_norm_mix_w': 'new_m', 'new_m_w_in': 'new_m', 'new_m_ssd_conv_w': 'new_m', 'new_m_ssd_conv_b': 'new_m', 'new_m_ssd_dt_bias': 'new_m', 'new_m_ssd_a_log': 'new_m', 'new_m_ssd_d': 'new_m', 'new_m_ssd_norm_w': 'new_m', 'new_m_fox_f_bias': 'new_m', 'new_m_fox_q_norm_w': 'new_m', 'new_m_fox_k_norm_w': 'new_m', 'new_m_w_out': 'new_m', 'new_m_norm_ffn_w': 'new_m', 'new_m_w_up': 'new_m', 'new_m_ffn_conv_w': 'new_m', 'new_m_ffn_conv_b': 'new_m', 'new_m_w_down': 'new_m', 'new_v_norm_mix_w': 'new_v', 'new_v_w_in': 'new_v', 'new_v_ssd_conv_w': 'new_v', 'new_v_ssd_conv_b': 'new_v', 'new_v_ssd_dt_bias': 'new_v', 'new_v_ssd_a_log': 'new_v', 'new_v_ssd_d': 'new_v', 'new_v_ssd_norm_w': 'new_v', 'new_v_fox_f_bias': 'new_v', 'new_v_fox_q_norm_w': 'new_v', 'new_v_fox_k_norm_w': 'new_v', 'new_v_w_out': 'new_v', 'new_v_norm_ffn_w': 'new_v', 'new_v_w_up': 'new_v', 'new_v_ffn_conv_w': 'new_v', 'new_v_ffn_conv_b': 'new_v', 'new_v_w_down': 'new_v'}


def _forward(args):
    return _fwd_reference(*[args[k] for k in FWD_PARAMS])


def _output_shape():
    out = _jax.eval_shape(lambda: _forward(_fwd_setup_inputs(0)))
    return out.shape, out.dtype

N_MICROBATCH = 1
ADAM_LR = 0.001
ADAM_B1 = 0.9
ADAM_B2 = 0.999
ADAM_EPS = 1e-08
ADAM_WD = 0.01
ADAM_STEP = 10
PER_EXAMPLE_BATCH_AXIS = {'x': 0, 'loss_target': 0}
SHARED_INPUTS = []
_WEIGHT_DTYPES = {'norm_mix_w': _jnp.float32, 'w_in': _jnp.float32, 'ssd_conv_w': _jnp.float32, 'ssd_conv_b': _jnp.float32, 'ssd_dt_bias': _jnp.float32, 'ssd_a_log': _jnp.float32, 'ssd_d': _jnp.float32, 'ssd_norm_w': _jnp.float32, 'fox_f_bias': _jnp.float32, 'fox_q_norm_w': _jnp.float32, 'fox_k_norm_w': _jnp.float32, 'w_out': _jnp.float32, 'norm_ffn_w': _jnp.float32, 'w_up': _jnp.float32, 'ffn_conv_w': _jnp.float32, 'ffn_conv_b': _jnp.float32, 'w_down': _jnp.float32}
MOMENT_SCALE = {'norm_mix_w': 1.039888e+00, 'w_in': 2.289066e-01, 'ssd_conv_w': 6.111242e-01, 'ssd_conv_b': 1.987932e+00, 'ssd_dt_bias': 1.393820e+00, 'ssd_a_log': 5.611843e+00, 'ssd_d': 2.623655e+00, 'ssd_norm_w': 2.043766e+01, 'fox_f_bias': 2.447800e+01, 'fox_q_norm_w': 7.498460e+00, 'fox_k_norm_w': 7.489336e+00, 'w_out': 1.063073e+00, 'norm_ffn_w': 2.638639e+01, 'w_up': 4.142187e-01, 'ffn_conv_w': 3.661616e+00, 'ffn_conv_b': 3.328049e+00, 'w_down': 3.467604e-01}


def _to_microbatches(a, axis):
    t = _jnp.moveaxis(a, axis, 0)
    t = t.reshape((N_MICROBATCH, t.shape[0] // N_MICROBATCH) + t.shape[1:])
    return _jnp.moveaxis(t, 1, axis + 1)


def setup_inputs(seed: int = 0) -> dict:
    inp = _fwd_setup_inputs(seed)
    key = _jax.random.fold_in(_jax.random.key(seed), 7919)
    shape, _ = _output_shape()
    out = dict(inp)
    out["loss_target"] = _jax.random.normal(_jax.random.fold_in(key, 0), shape, _jnp.float32)
    for i, name in enumerate(TWIN_WEIGHTS):
        w = inp[name].astype(_jnp.float32)
        if MOMENT_SCALE is None:
            s = _jnp.sqrt(_jnp.mean(_jnp.square(w)) + 1e-30)
        else:
            s = MOMENT_SCALE[name]
        km, kv = _jax.random.split(_jax.random.fold_in(key, i + 1))
        out[name] = w
        out["m_" + name] = s * _jax.random.normal(km, w.shape, _jnp.float32)
        out["v_" + name] = (s * s) * _jax.random.uniform(kv, w.shape, _jnp.float32, 0.5, 1.5)
    if N_MICROBATCH > 1:
        for name, axis in PER_EXAMPLE_BATCH_AXIS.items():
            out[name] = _to_microbatches(out[name], axis)
    return {'x': out['x'], 'norm_mix_w': out['norm_mix_w'], 'w_in': out['w_in'], 'ssd_conv_w': out['ssd_conv_w'], 'ssd_conv_b': out['ssd_conv_b'], 'ssd_dt_bias': out['ssd_dt_bias'], 'ssd_a_log': out['ssd_a_log'], 'ssd_d': out['ssd_d'], 'ssd_norm_w': out['ssd_norm_w'], 'fox_f_bias': out['fox_f_bias'], 'fox_q_norm_w': out['fox_q_norm_w'], 'fox_k_norm_w': out['fox_k_norm_w'], 'w_out': out['w_out'], 'norm_ffn_w': out['norm_ffn_w'], 'w_up': out['w_up'], 'ffn_conv_w': out['ffn_conv_w'], 'ffn_conv_b': out['ffn_conv_b'], 'w_down': out['w_down'], 'loss_target': out['loss_target'], 'm_norm_mix_w': out['m_norm_mix_w'], 'm_w_in': out['m_w_in'], 'm_ssd_conv_w': out['m_ssd_conv_w'], 'm_ssd_conv_b': out['m_ssd_conv_b'], 'm_ssd_dt_bias': out['m_ssd_dt_bias'], 'm_ssd_a_log': out['m_ssd_a_log'], 'm_ssd_d': out['m_ssd_d'], 'm_ssd_norm_w': out['m_ssd_norm_w'], 'm_fox_f_bias': out['m_fox_f_bias'], 'm_fox_q_norm_w': out['m_fox_q_norm_w'], 'm_fox_k_norm_w': out['m_fox_k_norm_w'], 'm_w_out': out['m_w_out'], 'm_norm_ffn_w': out['m_norm_ffn_w'], 'm_w_up': out['m_w_up'], 'm_ffn_conv_w': out['m_ffn_conv_w'], 'm_ffn_conv_b': out['m_ffn_conv_b'], 'm_w_down': out['m_w_down'], 'v_norm_mix_w': out['v_norm_mix_w'], 'v_w_in': out['v_w_in'], 'v_ssd_conv_w': out['v_ssd_conv_w'], 'v_ssd_conv_b': out['v_ssd_conv_b'], 'v_ssd_dt_bias': out['v_ssd_dt_bias'], 'v_ssd_a_log': out['v_ssd_a_log'], 'v_ssd_d': out['v_ssd_d'], 'v_ssd_norm_w': out['v_ssd_norm_w'], 'v_fox_f_bias': out['v_fox_f_bias'], 'v_fox_q_norm_w': out['v_fox_q_norm_w'], 'v_fox_k_norm_w': out['v_fox_k_norm_w'], 'v_w_out': out['v_w_out'], 'v_norm_ffn_w': out['v_norm_ffn_w'], 'v_w_up': out['v_w_up'], 'v_ffn_conv_w': out['v_ffn_conv_w'], 'v_ffn_conv_b': out['v_ffn_conv_b'], 'v_w_down': out['v_w_down']}


def _loss(weights, diff, rest, loss_target):
    with _jax.named_scope("forward"):
        args = {**rest, TWIN_DIFF_INPUT: diff, **{k: w.astype(_WEIGHT_DTYPES[k]) for k, w in weights.items()}}
        y = _forward(args)
    with _jax.named_scope("loss_head"):
        err = _jnp.square(y.astype(_jnp.float32) - loss_target)
        return 0.5 * _jnp.sum(_jnp.mean(err, axis=-1)) if err.ndim else 0.5 * err


def _adamw(w, g, m, v):
    m = ADAM_B1 * m + (1.0 - ADAM_B1) * g
    v = ADAM_B2 * v + (1.0 - ADAM_B2) * _jnp.square(g)
    m_hat = m / (1.0 - ADAM_B1 ** ADAM_STEP)
    v_hat = v / (1.0 - ADAM_B2 ** ADAM_STEP)
    delta = -ADAM_LR * (m_hat / (_jnp.sqrt(v_hat) + ADAM_EPS) + ADAM_WD * w)
    return delta, m, v


def reference(x, norm_mix_w, w_in, ssd_conv_w, ssd_conv_b, ssd_dt_bias, ssd_a_log, ssd_d, ssd_norm_w, fox_f_bias, fox_q_norm_w, fox_k_norm_w, w_out, norm_ffn_w, w_up, ffn_conv_w, ffn_conv_b, w_down, loss_target, m_norm_mix_w, m_w_in, m_ssd_conv_w, m_ssd_conv_b, m_ssd_dt_bias, m_ssd_a_log, m_ssd_d, m_ssd_norm_w, m_fox_f_bias, m_fox_q_norm_w, m_fox_k_norm_w, m_w_out, m_norm_ffn_w, m_w_up, m_ffn_conv_w, m_ffn_conv_b, m_w_down, v_norm_mix_w, v_w_in, v_ssd_conv_w, v_ssd_conv_b, v_ssd_dt_bias, v_ssd_a_log, v_ssd_d, v_ssd_norm_w, v_fox_f_bias, v_fox_q_norm_w, v_fox_k_norm_w, v_w_out, v_norm_ffn_w, v_w_up, v_ffn_conv_w, v_ffn_conv_b, v_w_down):
    given = dict(x=x, norm_mix_w=norm_mix_w, w_in=w_in, ssd_conv_w=ssd_conv_w, ssd_conv_b=ssd_conv_b, ssd_dt_bias=ssd_dt_bias, ssd_a_log=ssd_a_log, ssd_d=ssd_d, ssd_norm_w=ssd_norm_w, fox_f_bias=fox_f_bias, fox_q_norm_w=fox_q_norm_w, fox_k_norm_w=fox_k_norm_w, w_out=w_out, norm_ffn_w=norm_ffn_w, w_up=w_up, ffn_conv_w=ffn_conv_w, ffn_conv_b=ffn_conv_b, w_down=w_down, loss_target=loss_target, m_norm_mix_w=m_norm_mix_w, m_w_in=m_w_in, m_ssd_conv_w=m_ssd_conv_w, m_ssd_conv_b=m_ssd_conv_b, m_ssd_dt_bias=m_ssd_dt_bias, m_ssd_a_log=m_ssd_a_log, m_ssd_d=m_ssd_d, m_ssd_norm_w=m_ssd_norm_w, m_fox_f_bias=m_fox_f_bias, m_fox_q_norm_w=m_fox_q_norm_w, m_fox_k_norm_w=m_fox_k_norm_w, m_w_out=m_w_out, m_norm_ffn_w=m_norm_ffn_w, m_w_up=m_w_up, m_ffn_conv_w=m_ffn_conv_w, m_ffn_conv_b=m_ffn_conv_b, m_w_down=m_w_down, v_norm_mix_w=v_norm_mix_w, v_w_in=v_w_in, v_ssd_conv_w=v_ssd_conv_w, v_ssd_conv_b=v_ssd_conv_b, v_ssd_dt_bias=v_ssd_dt_bias, v_ssd_a_log=v_ssd_a_log, v_ssd_d=v_ssd_d, v_ssd_norm_w=v_ssd_norm_w, v_fox_f_bias=v_fox_f_bias, v_fox_q_norm_w=v_fox_q_norm_w, v_fox_k_norm_w=v_fox_k_norm_w, v_w_out=v_w_out, v_norm_ffn_w=v_norm_ffn_w, v_w_up=v_w_up, v_ffn_conv_w=v_ffn_conv_w, v_ffn_conv_b=v_ffn_conv_b, v_w_down=v_w_down)
    weights = {n: given[n] for n in TWIN_WEIGHTS}
    shared = {n: given[n] for n in SHARED_INPUTS}
    per_example = {n: given[n] for n in ['x']}
    grad_fn = _jax.value_and_grad(_loss, argnums=(0, 1))

    def one_microbatch(ex, loss_target):
        ex = dict(ex)
        diff = ex.pop(TWIN_DIFF_INPUT)
        return grad_fn(weights, diff, {**shared, **ex}, loss_target)

    if N_MICROBATCH == 1:
        loss, (grad_w, grad_x) = one_microbatch(per_example, given["loss_target"])
    else:
        def body(carry, xs):
            loss_sum, grad_sum = carry
            l_k, (gw_k, gx_k) = one_microbatch(xs[0], xs[1])
            with _jax.named_scope("update"):
                return (loss_sum + l_k, _jax.tree.map(_jnp.add, grad_sum, gw_k)), gx_k

        init = (_jnp.zeros((), _jnp.float32), _jax.tree.map(_jnp.zeros_like, weights))
        (loss, grad_w), grad_x = _jax.lax.scan(body, init, (per_example, given["loss_target"]))
    with _jax.named_scope("update"):
        delta_w, new_m, new_v = {}, {}, {}
        for n in TWIN_WEIGHTS:
            delta_w[n], new_m[n], new_v[n] = _adamw(weights[n], grad_w[n], given["m_" + n], given["v_" + n])
    return (loss, grad_x, *[grad_w[n] for n in TWIN_WEIGHTS], *[delta_w[n] for n in TWIN_WEIGHTS],
            *[new_m[n] for n in TWIN_WEIGHTS], *[new_v[n] for n in TWIN_WEIGHTS])
```

```python
import functools

import jax
import jax.numpy as jnp
from jax import lax
from jax.experimental import pallas as pl
from jax.experimental.pallas import tpu as pltpu

F32 = jnp.float32
BF16 = jnp.bfloat16
MESH = pl.DeviceIdType.MESH

D_MODEL = 1024
HEAD_DIM = 64
N_HEADS = 16
N_PAIRS = N_HEADS // 2
SSD_CHUNK = 128
SSD_STATE = 128
SSD_CONV = 4
D_FF = 2816
FFN_CONV = 3
NORM_EPS = 1e-6
MAIN_COLS = 5632
SMALL_COLS = 128
F_LANE = 16
IN_COLS = 5664

ADAM_LR = 0.001
ADAM_B1 = 0.9
ADAM_B2 = 0.999
ADAM_EPS = 1e-08
ADAM_WD = 0.01
ADAM_STEP = 10

VMEM_LIMIT_V7X = 56 * 1024 * 1024
NEG_BIG = -1e30


def _params(sem=None):
    return pltpu.CompilerParams(dimension_semantics=sem, vmem_limit_bytes=VMEM_LIMIT_V7X)


def _sigmoid(x):
    return 1.0 / (1.0 + jnp.exp(-x))


def _silu_and_grad(x):
    s = _sigmoid(x)
    return x * s, s * (1.0 + x * (1.0 - s))


def _shift_down(v, j):
    return v if j == 0 else pltpu.roll(v, j, 0)


def _shift_up(v, j):
    return v if j == 0 else pltpu.roll(v, v.shape[0] - j, 0)


def _row_iota(shape):
    return lax.broadcasted_iota(jnp.int32, shape, 0)


def _lane_iota(shape):
    return lax.broadcasted_iota(jnp.int32, shape, 1)


def _dot(a, b, mode="nn"):
    dims = {"nn": (((1,), (0,)), ((), ())), "nt": (((1,), (1,)), ((), ())), "tn": (((0,), (0,)), ((), ()))}[mode]
    return lax.dot_general(a.astype(BF16), b.astype(BF16), dims, preferred_element_type=F32)


def _dot_f32(a, b):
    return jnp.dot(a, b, precision=lax.Precision.HIGHEST, preferred_element_type=F32)


def matmul(a, b, *, mode, tm, tn, tk, out_dtype, name, add=None, b_koff=0):
    if mode == "nn":
        (m, k), n = a.shape, b.shape[1]
    elif mode == "nt":
        (m, k), n = a.shape, b.shape[0]
    else:
        (k, m), n = a.shape, b.shape[1]
    assert m % tm == 0 and n % tn == 0 and k % tk == 0, (name, m, n, k, tm, tn, tk)
    nk = k // tk
    a_spec = pl.BlockSpec((tk, tm), lambda i, j, kk: (kk, i)) if mode == "tn" else pl.BlockSpec((tm, tk), lambda i, j, kk: (i, kk))
    b_spec = (pl.BlockSpec((tn, tk), lambda i, j, kk: (j, kk + b_koff)) if mode == "nt"
              else pl.BlockSpec((tk, tn), lambda i, j, kk: (kk + b_koff, j)))
    o_spec = pl.BlockSpec((tm, tn), lambda i, j, kk: (i, j))
    has_add = add is not None

    def body(*refs):
        if has_add:
            a_ref, b_ref, add_ref, o_ref, acc_ref = refs
        else:
            a_ref, b_ref, o_ref, acc_ref = refs
        kk = pl.program_id(2)
        part = _dot(a_ref[...], b_ref[...], mode)

        def finish(total):
            if has_add:
                total = total + add_ref[...]
            o_ref[...] = total.astype(out_dtype)

        if nk == 1:
            finish(part)
        else:
            @pl.when(kk == 0)
            def _():
                acc_ref[...] = part

            @pl.when(jnp.logical_and(kk > 0, kk < nk - 1))
            def _():
                acc_ref[...] += part

            @pl.when(kk == nk - 1)
            def _():
                finish(acc_ref[...] + part)

    in_specs = [a_spec, b_spec] + ([o_spec] if has_add else [])
    args = (a, b) + ((add,) if has_add else ())
    return pl.pallas_call(
        body, name=name, grid=(m // tm, n // tn, nk), in_specs=in_specs, out_specs=o_spec,
        out_shape=jax.ShapeDtypeStruct((m, n), out_dtype),
        scratch_shapes=[pltpu.VMEM((tm, tn) if nk > 1 else (8, 128), F32)],
        compiler_params=_params(("parallel", "parallel", "arbitrary")),
    )(*args)


def rms_fwd(x, w, *, name, tm=512):
    s, d = x.shape

    def body(x_ref, w_ref, h_ref):
        xv = x_ref[...]
        r = lax.rsqrt(jnp.mean(xv * xv, axis=-1, keepdims=True) + NORM_EPS)
        h_ref[...] = ((xv * r) * w_ref[...]).astype(BF16)

    return pl.pallas_call(
        body, name=name, grid=(s // tm,),
        in_specs=[pl.BlockSpec((tm, d), lambda i: (i, 0)), pl.BlockSpec((1, d), lambda i: (0, 0))],
        out_specs=pl.BlockSpec((tm, d), lambda i: (i, 0)),
        out_shape=jax.ShapeDtypeStruct((s, d), BF16), compiler_params=_params(("parallel",)),
    )(x, w)


def rms_bwd(dh, x, w, resid, *, name, tm=512):
    s, d = x.shape

    def body(dh_ref, x_ref, w_ref, res_ref, dx_ref, dw_ref):
        xv = x_ref[...]
        dhv = dh_ref[...]
        r = lax.rsqrt(jnp.mean(xv * xv, axis=-1, keepdims=True) + NORM_EPS)
        xh = xv * r
        g = dhv * w_ref[...]
        dx_ref[...] = res_ref[...] + r * (g - xh * jnp.mean(g * xh, axis=-1, keepdims=True))
        part = jnp.sum(dhv * xh, axis=0, keepdims=True)

        @pl.when(pl.program_id(0) == 0)
        def _():
            dw_ref[...] = part

        @pl.when(pl.program_id(0) > 0)
        def _():
            dw_ref[...] += part

    row = pl.BlockSpec((tm, d), lambda i: (i, 0))
    vec = pl.BlockSpec((1, d), lambda i: (0, 0))
    return pl.pallas_call(
        body, name=name, grid=(s // tm,), in_specs=[row, row, vec, row], out_specs=[row, vec],
        out_shape=[jax.ShapeDtypeStruct((s, d), F32), jax.ShapeDtypeStruct((1, d), F32)],
        compiler_params=_params(("arbitrary",)),
    )(dh, x, w, resid)


def loss_head(y, target, *, tm=512):
    s, d = y.shape

    def body(y_ref, t_ref, dy_ref, sq_ref):
        e = y_ref[...] - t_ref[...]
        dy_ref[...] = e / float(d)
        part = jnp.sum(e * e, axis=0, keepdims=True)

        @pl.when(pl.program_id(0) == 0)
        def _():
            sq_ref[...] = part

        @pl.when(pl.program_id(0) > 0)
        def _():
            sq_ref[...] += part

    row = pl.BlockSpec((tm, d), lambda i: (i, 0))
    vec = pl.BlockSpec((1, d), lambda i: (0, 0))
    return pl.pallas_call(
        body, name="loss_head", grid=(s // tm,), in_specs=[row, row], out_specs=[row, vec],
        out_shape=[jax.ShapeDtypeStruct((s, d), F32), jax.ShapeDtypeStruct((1, d), F32)],
        compiler_params=_params(("arbitrary",)),
    )(y, target)


def _conv_rows(ext, w, k_taps):
    acc = w[k_taps - 1:k_taps, :] * ext
    for k in range(k_taps - 1):
        acc = acc + w[k:k + 1, :] * _shift_down(ext, k_taps - 1 - k)
    return acc


def _conv_rows_transposed(dext, w, k_taps):
    acc = w[k_taps - 1:k_taps, :] * dext
    for k in range(k_taps - 1):
        acc = acc + w[k:k + 1, :] * _shift_up(dext, k_taps - 1 - k)
    return acc


def _stack_rows(rows, width):
    ri = _row_iota((8, width))
    out = jnp.zeros((8, width), F32)
    for k, r in enumerate(rows):
        out = out + jnp.where(ri == k, r, 0.0)
    return out


def ffn_mid_fwd(hu, conv_w8, conv_b, *, tm=512, tc=256):
    s = hu.shape[0]
    ncol = D_FF // tc
    r8 = tm // 8

    def body(g_ref, v_ref, gp_ref, vp_ref, wg_ref, wv_ref, bg_ref, bv_ref, o_ref):
        first = pl.program_id(1) == 0

        def conv(cur_ref, prev_ref, w_ref, b_ref):
            prev = jnp.where(first, 0.0, prev_ref[...])
            ext = jnp.concatenate([prev, cur_ref[...]], axis=0)
            return _conv_rows(ext, w_ref[...], FFN_CONV)[8:] + b_ref[...]

        gc = conv(g_ref, gp_ref, wg_ref, bg_ref)
        vc = conv(v_ref, vp_ref, wv_ref, bv_ref)
        o_ref[...] = (gc * _sigmoid(gc) * vc).astype(BF16)

    def prev_idx(i):
        return jnp.maximum(i * r8 - 1, 0)

    in_specs = [
        pl.BlockSpec((tm, tc), lambda j, i: (i, j)),
        pl.BlockSpec((tm, tc), lambda j, i: (i, j + ncol)),
        pl.BlockSpec((8, tc), lambda j, i: (prev_idx(i), j)),
        pl.BlockSpec((8, tc), lambda j, i: (prev_idx(i), j + ncol)),
        pl.BlockSpec((8, tc), lambda j, i: (0, j)),
        pl.BlockSpec((8, tc), lambda j, i: (0, j + ncol)),
        pl.BlockSpec((1, tc), lambda j, i: (0, j)),
        pl.BlockSpec((1, tc), lambda j, i: (0, j + ncol)),
    ]
    return pl.pallas_call(
        body, name="ffn_mid_fwd", grid=(ncol, s // tm), in_specs=in_specs,
        out_specs=pl.BlockSpec((tm, tc), lambda j, i: (i, j)),
        out_shape=jax.ShapeDtypeStruct((s, D_FF), BF16), compiler_params=_params(("parallel", "parallel")),
    )(hu, hu, hu, hu, conv_w8, conv_w8, conv_b, conv_b)


def ffn_mid_bwd(hu, dact, conv_w8, conv_b, *, tm=512, tc=256):
    s = hu.shape[0]
    ncol = D_FF // tc
    nrow = s // tm
    r8 = tm // 8

    def body(g_ref, v_ref, gp_ref, vp_ref, gn_ref, vn_ref, da_ref, dan_ref, wg_ref, wv_ref, bg_ref, bv_ref,
             dg_ref, dv_ref, wgo_ref, wvo_ref):
        i = pl.program_id(1)
        first = i == 0
        last = i == nrow - 1

        def ext_of(cur_ref, prev_ref, next_ref):
            prev = jnp.where(first, 0.0, prev_ref[...])
            return jnp.concatenate([prev, cur_ref[...], next_ref[...]], axis=0)

        g_ext = ext_of(g_ref, gp_ref, gn_ref)
        v_ext = ext_of(v_ref, vp_ref, vn_ref)
        gc = _conv_rows(g_ext, wg_ref[...], FFN_CONV) + bg_ref[...]
        vc = _conv_rows(v_ext, wv_ref[...], FFN_CONV) + bv_ref[...]
        da_ext = jnp.concatenate([jnp.zeros((8, tc), F32), da_ref[...], jnp.where(last, 0.0, dan_ref[...])], axis=0)
        silu, dsilu = _silu_and_grad(gc)
        dgc = da_ext * vc * dsilu
        dvc = da_ext * silu
        dg_ref[...] = _conv_rows_transposed(dgc, wg_ref[...], FFN_CONV)[8:8 + tm].astype(BF16)
        dv_ref[...] = _conv_rows_transposed(dvc, wv_ref[...], FFN_CONV)[8:8 + tm].astype(BF16)

        def wgrad(dcur, x_ext):
            rows = [jnp.sum(dcur * _shift_down(x_ext, FFN_CONV - 1 - k)[8:8 + tm], axis=0, keepdims=True)
                    for k in range(FFN_CONV)]
            rows.append(jnp.sum(dcur, axis=0, keepdims=True))
            return _stack_rows(rows, tc)

        pg = wgrad(dgc[8:8 + tm], g_ext)
        pv = wgrad(dvc[8:8 + tm], v_ext)

        @pl.when(first)
        def _():
            wgo_ref[...] = pg
            wvo_ref[...] = pv

        @pl.when(i > 0)
        def _():
            wgo_ref[...] += pg
            wvo_ref[...] += pv

    def prev_idx(i):
        return jnp.maximum(i * r8 - 1, 0)

    def next_idx(i):
        return jnp.minimum((i + 1) * r8, s // 8 - 1)

    cur_g = pl.BlockSpec((tm, tc), lambda j, i: (i, j))
    cur_v = pl.BlockSpec((tm, tc), lambda j, i: (i, j + ncol))
    in_specs = [
        cur_g, cur_v,
        pl.BlockSpec((8, tc), lambda j, i: (prev_idx(i), j)),
        pl.BlockSpec((8, tc), lambda j, i: (prev_idx(i), j + ncol)),
        pl.BlockSpec((8, tc), lambda j, i: (next_idx(i), j)),
        pl.BlockSpec((8, tc), lambda j, i: (next_idx(i), j + ncol)),
        cur_g,
        pl.BlockSpec((8, tc), lambda j, i: (next_idx(i), j)),
        pl.BlockSpec((8, tc), lambda j, i: (0, j)),
        pl.BlockSpec((8, tc), lambda j, i: (0, j + ncol)),
        pl.BlockSpec((1, tc), lambda j, i: (0, j)),
        pl.BlockSpec((1, tc), lambda j, i: (0, j + ncol)),
    ]
    out_specs = [cur_g, cur_g, pl.BlockSpec((8, tc), lambda j, i: (0, j)), pl.BlockSpec((8, tc), lambda j, i: (0, j))]
    out_shape = [jax.ShapeDtypeStruct((s, D_FF), BF16), jax.ShapeDtypeStruct((s, D_FF), BF16),
                 jax.ShapeDtypeStruct((8, D_FF), F32), jax.ShapeDtypeStruct((8, D_FF), F32)]
    return pl.pallas_call(
        body, name="ffn_mid_bwd", grid=(ncol, nrow), in_specs=in_specs, out_specs=out_specs, out_shape=out_shape,
        compiler_params=_params(("parallel", "arbitrary")),
    )(hu, hu, hu, hu, hu, hu, dact, dact, conv_w8, conv_w8, conv_b, conv_b)


def _softplus(x):
    return jnp.maximum(x, 0.0) + jnp.log(1.0 + jnp.exp(-jnp.abs(x)))


def _cumsum_rows(v):
    n = v.shape[0]
    ri = _row_iota(v.shape)
    sh = 1
    while sh < n:
        v = v + jnp.where(ri >= sh, _shift_down(v, sh), 0.0)
        sh *= 2
    return v


def _rev_cumsum_rows(v):
    n = v.shape[0]
    ri = _row_iota(v.shape)
    sh = 1
    while sh < n:
        v = v + jnp.where(ri < n - sh, _shift_up(v, sh), 0.0)
        sh *= 2
    return v


def _half_row_sums(v, lo):
    s0 = jnp.sum(jnp.where(lo, v, 0.0), axis=1, keepdims=True)
    return s0, jnp.sum(v, axis=1, keepdims=True) - s0


def _total(v):
    return jnp.sum(jnp.sum(v, axis=1, keepdims=True), axis=0, keepdims=True)


def _ssd_in_specs(rev_nc=None):
    def ch(c):
        return c if rev_nc is None else rev_nc - 1 - c

    def prev(c):
        return jnp.maximum(ch(c) * (SSD_CHUNK // 8) - 1, 0)

    L = SSD_CHUNK
    return [
        pl.BlockSpec((L, 1024), lambda c: (ch(c), 0)),
        pl.BlockSpec((L, 1024), lambda c: (ch(c), 1)),
        pl.BlockSpec((L, 256), lambda c: (ch(c), 20)),
        pl.BlockSpec((L, 256), lambda c: (ch(c), 21)),
        pl.BlockSpec((8, 1024), lambda c: (prev(c), 1)),
        pl.BlockSpec((8, 256), lambda c: (prev(c), 20)),
        pl.BlockSpec((8, 256), lambda c: (prev(c), 21)),
        pl.BlockSpec((8, 1024), lambda c: (0, 0)),
        pl.BlockSpec((8, 256), lambda c: (0, 4)),
        pl.BlockSpec((8, 256), lambda c: (0, 5)),
        pl.BlockSpec((1, 1024), lambda c: (0, 0)),
        pl.BlockSpec((1, 256), lambda c: (0, 4)),
        pl.BlockSpec((1, 256), lambda c: (0, 5)),
        pl.BlockSpec((L, SMALL_COLS), lambda c: (ch(c), 0)),
        pl.BlockSpec((8, 128), lambda c: (0, 0)),
        pl.BlockSpec((1, 1024), lambda c: (0, 0)),
    ]


def _ssd_conv_pre(cur_ref, prev_ref, w_ref, b_ref, first):
    prev = jnp.where(first, 0.0, prev_ref[...])
    ext = jnp.concatenate([prev, cur_ref[...]], axis=0)
    return ext, _conv_rows(ext, w_ref[...], SSD_CONV)[8:] + b_ref[...]


def _ssd_time_consts(small_ref, sp_ref):
    dt_pre = small_ref[...] + sp_ref[0:1, :]
    dt = _softplus(dt_pre)
    a = -jnp.exp(sp_ref[1:2, :])
    acs = _cumsum_rows(dt * a)
    return dt_pre, dt, a, acs


def ssd_fwd(proj, small, conv_w8, conv_b, smallp, norm_w):
    s = proj.shape[0]
    nc = s // SSD_CHUNK
    L = SSD_CHUNK

    def body(z_ref, xs_ref, b_ref, c_ref, xsp_ref, bp_ref, cp_ref, wx_ref, wb_ref, wc_ref, bx_ref, bb_ref, bc_ref,
             small_ref, sp_ref, nw_ref, y_ref, ypre_ref, st_ref, state):
        first = pl.program_id(0) == 0

        @pl.when(first)
        def _():
            state[...] = jnp.zeros_like(state)

        xs = _ssd_conv_pre(xs_ref, xsp_ref, wx_ref, bx_ref, first)[1]
        xs = xs * _sigmoid(xs)
        bm = _ssd_conv_pre(b_ref, bp_ref, wb_ref, bb_ref, first)[1]
        bm = bm * _sigmoid(bm)
        cm = _ssd_conv_pre(c_ref, cp_ref, wc_ref, bc_ref, first)[1]
        cm = cm * _sigmoid(cm)
        _, dt, _, acs = _ssd_time_consts(small_ref, sp_ref)
        acs_t = acs.T
        li = _lane_iota((L, L))
        ri = _row_iota((L, L))
        tri = ri >= li
        lo = li < HEAD_DIM
        st_ref[0] = state[...]
        for g in range(2):
            bg = bm[:, 128 * g:128 * g + 128]
            cg = cm[:, 128 * g:128 * g + 128]
            gmat = _dot(cg, bg, "nt")
            for pp in range(4):
                p = 4 * g + pp
                h0, h1 = 2 * p, 2 * p + 1
                x = xs[:, 128 * p:128 * p + 128]
                a0, a1 = acs[:, h0:h0 + 1], acs[:, h1:h1 + 1]
                xdt = x * jnp.where(lo, dt[:, h0:h0 + 1], dt[:, h1:h1 + 1])
                m0 = gmat * jnp.exp(jnp.where(tri, a0 - acs_t[h0:h0 + 1, :], NEG_BIG))
                m1 = gmat * jnp.exp(jnp.where(tri, a1 - acs_t[h1:h1 + 1, :], NEG_BIG))
                yd = _dot(m0, jnp.where(lo, xdt, 0.0)) + _dot(m1, jnp.where(lo, 0.0, xdt))
                hin = state[p]
                yo = _dot(cg, hin, "nt") * jnp.exp(jnp.where(lo, a0, a1))
                dskip = jnp.where(lo[0:1], sp_ref[2:3, h0:h0 + 1], sp_ref[2:3, h1:h1 + 1])
                ypre_ref[:, 128 * p:128 * p + 128] = yd + yo + dskip * x
                al0, al1 = acs[L - 1:L, h0:h0 + 1], acs[L - 1:L, h1:h1 + 1]
                w = jnp.exp(jnp.where(lo, al0 - a0, al1 - a1))
                dec = jnp.exp(jnp.where(ri < HEAD_DIM, al0, al1))
                state[p] = dec * hin + _dot(xdt * w, bg, "tn")
        z = z_ref[...]
        yg = ypre_ref[...] * (z * _sigmoid(z))
        for g in range(2):
            seg = yg[:, 512 * g:512 * g + 512]
            r = lax.rsqrt(jnp.mean(seg * seg, axis=-1, keepdims=True) + NORM_EPS)
            y_ref[:, 512 * g:512 * g + 512] = ((seg * r) * nw_ref[:, 512 * g:512 * g + 512]).astype(BF16)

    row = pl.BlockSpec((L, 1024), lambda c: (c, 0))
    return pl.pallas_call(
        body, name="ssd_fwd", grid=(nc,), in_specs=_ssd_in_specs(),
        out_specs=[row, row, pl.BlockSpec((1, N_PAIRS, 128, 128), lambda c: (c, 0, 0, 0))],
        out_shape=[jax.ShapeDtypeStruct((s, 1024), BF16), jax.ShapeDtypeStruct((s, 1024), F32),
                   jax.ShapeDtypeStruct((nc, N_PAIRS, 128, 128), F32)],
        scratch_shapes=[pltpu.VMEM((N_PAIRS, 128, 128), F32)],
        compiler_params=_params(("arbitrary",)),
    )(proj, proj, proj, proj, proj, proj, proj, conv_w8, conv_w8, conv_w8, conv_b, conv_b, conv_b, small, smallp, norm_w)


def ssd_bwd(proj, small, conv_w8, conv_b, smallp, norm_w, ypre, states, dy):
    s = proj.shape[0]
    nc = s // SSD_CHUNK
    L = SSD_CHUNK

    def body(z_ref, xs_ref, b_ref, c_ref, xsp_ref, bp_ref, cp_ref, wx_ref, wb_ref, wc_ref, bx_ref, bb_ref, bc_ref,
             small_ref, sp_ref, nw_ref, ypre_ref, st_ref, dy_ref,
             dz_ref, dxs_ref, db_ref, dc_ref, dsmall_ref, gwx_ref, gwb_ref, gwc_ref, gsp_ref, gnw_ref,
             dstate, carry_x, carry_b, carry_c, dxs_buf, dbm_buf, dcm_buf):
        step = pl.program_id(0)
        first_chunk = step == nc - 1
        start = step == 0

        @pl.when(start)
        def _():
            dstate[...] = jnp.zeros_like(dstate)
            carry_x[...] = jnp.zeros_like(carry_x)
            carry_b[...] = jnp.zeros_like(carry_b)
            carry_c[...] = jnp.zeros_like(carry_c)

        xs_ext, xs_pre = _ssd_conv_pre(xs_ref, xsp_ref, wx_ref, bx_ref, first_chunk)
        b_ext, b_pre = _ssd_conv_pre(b_ref, bp_ref, wb_ref, bb_ref, first_chunk)
        c_ext, c_pre = _ssd_conv_pre(c_ref, cp_ref, wc_ref, bc_ref, first_chunk)
        xs, xs_ds = _silu_and_grad(xs_pre)
        bm, b_ds = _silu_and_grad(b_pre)
        cm, c_ds = _silu_and_grad(c_pre)
        dt_pre, dt, a, acs = _ssd_time_consts(small_ref, sp_ref)
        acs_t = acs.T
        li = _lane_iota((L, L))
        ri = _row_iota((L, L))
        tri = ri >= li
        lo = li < HEAD_DIM
        lo_rows = ri < HEAD_DIM
        li1 = _lane_iota((1, L))

        z = z_ref[...]
        sz, dsz = _silu_and_grad(z)
        y = ypre_ref[...]
        yg = y * sz
        dout = dy_ref[...]
        dyg_parts = []
        gnw_parts = []
        for g in range(2):
            sl = slice(512 * g, 512 * g + 512)
            seg = yg[:, sl]
            r = lax.rsqrt(jnp.mean(seg * seg, axis=-1, keepdims=True) + NORM_EPS)
            n = seg * r
            gnw_parts.append(jnp.sum(dout[:, sl] * n, axis=0, keepdims=True))
            gg = dout[:, sl] * nw_ref[:, sl]
            dyg_parts.append(r * (gg - n * jnp.mean(gg * n, axis=-1, keepdims=True)))
        dyg = jnp.concatenate(dyg_parts, axis=1)
        gnw = jnp.concatenate(gnw_parts, axis=1)
        dz_ref[...] = (dyg * y * dsz).astype(BF16)
        dypre = dyg * sz

        ddt = jnp.zeros((L, L), F32)
        dacs = jnp.zeros((L, L), F32)
        dacs_t = jnp.zeros((L, L), F32)
        dalast = jnp.zeros((1, L), F32)
        dskip_g = jnp.zeros((1, L), F32)
        for g in range(2):
            bg = bm[:, 128 * g:128 * g + 128]
            cg = cm[:, 128 * g:128 * g + 128]
            gmat = _dot(cg, bg, "nt")
            dgmat = jnp.zeros((L, L), F32)
            dbg = jnp.zeros((L, L), F32)
            dcg = jnp.zeros((L, L), F32)
            for pp in range(4):
                p = 4 * g + pp
                h0, h1 = 2 * p, 2 * p + 1
                x = xs[:, 128 * p:128 * p + 128]
                dyp = dypre[:, 128 * p:128 * p + 128]
                a0, a1 = acs[:, h0:h0 + 1], acs[:, h1:h1 + 1]
                dtl = jnp.where(lo, dt[:, h0:h0 + 1], dt[:, h1:h1 + 1])
                xdt = x * dtl
                l0 = jnp.exp(jnp.where(tri, a0 - acs_t[h0:h0 + 1, :], NEG_BIG))
                l1 = jnp.exp(jnp.where(tri, a1 - acs_t[h1:h1 + 1, :], NEG_BIG))
                m0, m1 = gmat * l0, gmat * l1
                dskip = jnp.where(lo[0:1], sp_ref[2:3, h0:h0 + 1], sp_ref[2:3, h1:h1 + 1])
                s0, s1 = _half_row_sums(dyp * x, lo)
                dskip_g = dskip_g + jnp.where(li1 == h0, _total(s0), 0.0) + jnp.where(li1 == h1, _total(s1), 0.0)
                dx = dyp * dskip
                dy0, dy1 = jnp.where(lo, dyp, 0.0), jnp.where(lo, 0.0, dyp)
                x0, x1 = jnp.where(lo, xdt, 0.0), jnp.where(lo, 0.0, xdt)
                dm0, dm1 = _dot(dy0, x0, "nt"), _dot(dy1, x1, "nt")
                dxdt = _dot(m0, dy0, "tn") + _dot(m1, dy1, "tn")
                q0, q1 = dm0 * m0, dm1 * m1
                dacs = dacs + jnp.where(li == h0, jnp.sum(q0, axis=1, keepdims=True), 0.0) \
                            + jnp.where(li == h1, jnp.sum(q1, axis=1, keepdims=True), 0.0)
                dacs_t = dacs_t - jnp.where(ri == h0, jnp.sum(q0, axis=0, keepdims=True), 0.0) \
                                - jnp.where(ri == h1, jnp.sum(q1, axis=0, keepdims=True), 0.0)
                dgmat = dgmat + dm0 * l0 + dm1 * l1
                hin = st_ref[0, p]
                e = jnp.exp(jnp.where(lo, a0, a1))
                ch = _dot(cg, hin, "nt")
                dch = dyp * e
                dcg = dcg + _dot(dch, hin)
                dhin = _dot(dch, cg, "tn")
                s0, s1 = _half_row_sums(dch * ch, lo)
                dacs = dacs + jnp.where(li == h0, s0, 0.0) + jnp.where(li == h1, s1, 0.0)
                dhout = dstate[p]
                al0, al1 = acs[L - 1:L, h0:h0 + 1], acs[L - 1:L, h1:h1 + 1]
                dec = jnp.exp(jnp.where(lo_rows, al0, al1))
                dhin = dhin + dec * dhout
                dal = dhout * hin * dec
                dal0 = _total(jnp.where(lo_rows, dal, 0.0))
                dal1 = _total(dal) - dal0
                w = jnp.exp(jnp.where(lo, al0 - a0, al1 - a1))
                xw = xdt * w
                dxw = _dot(bg, dhout, "nt")
                dbg = dbg + _dot(xw, dhout)
                dxdt = dxdt + dxw * w
                s0, s1 = _half_row_sums(dxw * xw, lo)
                dacs = dacs - jnp.where(li == h0, s0, 0.0) - jnp.where(li == h1, s1, 0.0)
                dal0, dal1 = dal0 + _total(s0), dal1 + _total(s1)
                dalast = dalast + jnp.where(li1 == h0, dal0, 0.0) + jnp.where(li1 == h1, dal1, 0.0)
                dx = dx + dxdt * dtl
                s0, s1 = _half_row_sums(dxdt * x, lo)
                ddt = ddt + jnp.where(li == h0, s0, 0.0) + jnp.where(li == h1, s1, 0.0)
                dxs_buf[:, 128 * p:128 * p + 128] = dx
                dstate[p] = dhin
            dcg = dcg + _dot(dgmat, bg)
            dbg = dbg + _dot(dgmat, cg, "tn")
            dbm_buf[:, 128 * g:128 * g + 128] = dbg
            dcm_buf[:, 128 * g:128 * g + 128] = dcg

        dacs_tot = dacs + dacs_t.T + jnp.where(ri == L - 1, dalast, 0.0)
        dstep = _rev_cumsum_rows(dacs_tot)
        ddt = ddt + dstep * a
        head_lane = li < N_HEADS
        ddt_pre = jnp.where(head_lane, ddt * _sigmoid(dt_pre), 0.0)
        dsmall_ref[...] = ddt_pre
        da = jnp.sum(jnp.where(head_lane, dstep * dt, 0.0), axis=0, keepdims=True)
        gsp = _stack_rows([jnp.sum(ddt_pre, axis=0, keepdims=True), da * a, dskip_g], L)

        def conv_back(dpost, ds, ext, w_ref, carry, out_ref, width):
            dpre = dpost * ds
            dext = jnp.concatenate([dpre, carry[...]], axis=0)
            out_ref[...] = _conv_rows_transposed(dext, w_ref[...], SSD_CONV)[:L].astype(BF16)
            carry[...] = dpre[0:8]
            rows = [jnp.sum(dpre * _shift_down(ext, SSD_CONV - 1 - k)[8:], axis=0, keepdims=True) for k in range(SSD_CONV)]
            rows.append(jnp.sum(dpre, axis=0, keepdims=True))
            return _stack_rows(rows, width)

        gwx = conv_back(dxs_buf[...], xs_ds, xs_ext, wx_ref, carry_x, dxs_ref, 1024)
        gwb = conv_back(dbm_buf[...], b_ds, b_ext, wb_ref, carry_b, db_ref, 256)
        gwc = conv_back(dcm_buf[...], c_ds, c_ext, wc_ref, carry_c, dc_ref, 256)

        @pl.when(start)
        def _():
            gwx_ref[...] = gwx
            gwb_ref[...] = gwb
            gwc_ref[...] = gwc
            gsp_ref[...] = gsp
            gnw_ref[...] = gnw

        @pl.when(step > 0)
        def _():
            gwx_ref[...] += gwx
            gwb_ref[...] += gwb
            gwc_ref[...] += gwc
            gsp_ref[...] += gsp
            gnw_ref[...] += gnw

    def ch(c):
        return nc - 1 - c

    row = pl.BlockSpec((L, 1024), lambda c: (ch(c), 0))
    row256 = pl.BlockSpec((L, 256), lambda c: (ch(c), 0))
    in_specs = _ssd_in_specs(rev_nc=nc) + [row, pl.BlockSpec((1, N_PAIRS, 128, 128), lambda c: (ch(c), 0, 0, 0)), row]
    out_specs = [row, row, row256, row256, pl.BlockSpec((L, 128), lambda c: (ch(c), 0)),
                 pl.BlockSpec((8, 1024), lambda c: (0, 0)), pl.BlockSpec((8, 256), lambda c: (0, 0)),
                 pl.BlockSpec((8, 256), lambda c: (0, 0)), pl.BlockSpec((8, 128), lambda c: (0, 0)),
                 pl.BlockSpec((1, 1024), lambda c: (0, 0))]
    out_shape = [jax.ShapeDtypeStruct((s, 1024), BF16), jax.ShapeDtypeStruct((s, 1024), BF16),
                 jax.ShapeDtypeStruct((s, 256), BF16), jax.ShapeDtypeStruct((s, 256), BF16),
                 jax.ShapeDtypeStruct((s, 128), F32),
                 jax.ShapeDtypeStruct((8, 1024), F32), jax.ShapeDtypeStruct((8, 256), F32),
                 jax.ShapeDtypeStruct((8, 256), F32), jax.ShapeDtypeStruct((8, 128), F32),
                 jax.ShapeDtypeStruct((1, 1024), F32)]
    scratch = [pltpu.VMEM((N_PAIRS, 128, 128), F32), pltpu.VMEM((8, 1024), F32), pltpu.VMEM((8, 256), F32),
               pltpu.VMEM((8, 256), F32), pltpu.VMEM((L, 1024), F32), pltpu.VMEM((L, 256), F32), pltpu.VMEM((L, 256), F32)]
    return pl.pallas_call(
        body, name="ssd_bwd", grid=(nc,), in_specs=in_specs, out_specs=out_specs, out_shape=out_shape,
        scratch_shapes=scratch, compiler_params=_params(("arbitrary",)),
    )(proj, proj, proj, proj, proj, proj, proj, conv_w8, conv_w8, conv_w8, conv_b, conv_b, conv_b, small, smallp, norm_w,
      ypre, states, dy)


FOX_SCALE = HEAD_DIM ** -0.5
FOX_TQ = 256
FOX_TK = 256
Q_COL, K_COL, V_COL = 2, 3, 4


def _split3_dot(v, m):
    hi = v.astype(BF16)
    r1 = v - hi.astype(F32)
    mid = r1.astype(BF16)
    lo = (r1 - mid.astype(F32)).astype(BF16)
    return _dot(hi, m) + _dot(mid, m) + _dot(lo, m)


def _head_rstd(x, sel_ref, selt_ref):
    ms = _split3_dot(x * x, sel_ref[...]) * (1.0 / HEAD_DIM)
    return _split3_dot(lax.rsqrt(ms + NORM_EPS), selt_ref[...])


def fox_prep(proj, small, smallp, qw, kw, sel, selt, *, tm=256):
    s = proj.shape[0]

    def body(q_ref, k_ref, small_ref, sp_ref, qw_ref, kw_ref, sel_ref, selt_ref, qn_ref, kn_ref, cum_ref, cumt_ref, carry):
        @pl.when(pl.program_id(0) == 0)
        def _():
            carry[...] = jnp.zeros_like(carry)

        q = q_ref[...]
        qn_ref[...] = (((q * _head_rstd(q, sel_ref, selt_ref)) * qw_ref[...]) * FOX_SCALE).astype(BF16)
        k = k_ref[...]
        kn_ref[...] = ((k * _head_rstd(k, sel_ref, selt_ref)) * kw_ref[...]).astype(BF16)
        li = _lane_iota((tm, 128))
        f_lane = jnp.logical_and(li >= F_LANE, li < F_LANE + N_HEADS)
        logf = jnp.where(f_lane, -_softplus(-(small_ref[...] + sp_ref[3:4, :])), 0.0)
        cum = _cumsum_rows(logf) + carry[...]
        carry[...] = cum[tm - 1:tm, :]
        cum_ref[...] = cum
        cumt_ref[...] = cum.T

    row = pl.BlockSpec((tm, 1024), lambda i: (i, 0))
    vec = pl.BlockSpec((1, 1024), lambda i: (0, 0))
    return pl.pallas_call(
        body, name="fox_prep", grid=(s // tm,),
        in_specs=[pl.BlockSpec((tm, 1024), lambda i: (i, Q_COL)), pl.BlockSpec((tm, 1024), lambda i: (i, K_COL)),
                  pl.BlockSpec((tm, 128), lambda i: (i, 0)), pl.BlockSpec((8, 128), lambda i: (0, 0)), vec, vec,
                  pl.BlockSpec((1024, 128), lambda i: (0, 0)), pl.BlockSpec((128, 1024), lambda i: (0, 0))],
        out_specs=[row, row, pl.BlockSpec((tm, 128), lambda i: (i, 0)), pl.BlockSpec((128, tm), lambda i: (0, i))],
        out_shape=[jax.ShapeDtypeStruct((s, 1024), BF16), jax.ShapeDtypeStruct((s, 1024), BF16),
                   jax.ShapeDtypeStruct((s, 128), F32), jax.ShapeDtypeStruct((128, s), F32)],
        scratch_shapes=[pltpu.VMEM((1, 128), F32)], compiler_params=_params(("arbitrary",)),
    )(proj, proj, small, smallp, qw, kw, sel, selt)


def _pick_lane(block, lane):
    return jnp.sum(jnp.where(_lane_iota(block.shape) == lane, block, 0.0), axis=1, keepdims=True)


def _pick_row(block, row):
    return jnp.sum(jnp.where(_row_iota(block.shape) == row, block, 0.0), axis=0, keepdims=True)


def _causal_scores(qb, kb, cq, ck, q0, k0):
    sc = _dot(qb, kb, "nt") + cq - ck
    visible = (q0 + _row_iota(sc.shape)) >= (k0 + _lane_iota(sc.shape))
    return jnp.where(visible, sc, NEG_BIG)


def fox_fwd(qn, kn, proj, cum, cumt):
    s = qn.shape[0]
    tq, tk = FOX_TQ, FOX_TK
    nq = s // tq

    def body(q_ref, k_ref, v_ref, cum_ref, cumt_ref, o_ref, lse_ref, crow):
        p = pl.program_id(0)

        @pl.when(p == 0)
        def _():
            lse_ref[...] = jnp.zeros_like(lse_ref)

        lo = _lane_iota((tq, 128)) < HEAD_DIM
        for hh in range(2):
            h = 2 * p + hh
            mine = lo if hh == 0 else jnp.logical_not(lo)
            crow[hh:hh + 1, :] = _pick_row(cumt_ref[...], F_LANE + h)

            def q_loop(qi, _, hh=hh, h=h, mine=mine):
                q0 = pl.multiple_of(qi * tq, tq)
                qb = jnp.where(mine, q_ref[pl.ds(q0, tq), :], 0.0)
                cq = _pick_lane(cum_ref[pl.ds(q0, tq), :], F_LANE + h)

                def k_loop(kj, carry):
                    m, l, acc = carry
                    k0 = pl.multiple_of(kj * tk, tk)
                    sc = _causal_scores(qb, k_ref[pl.ds(k0, tk), :], cq, crow[hh:hh + 1, pl.ds(k0, tk)], q0, k0)
                    m_new = jnp.maximum(m, jnp.max(sc, axis=1, keepdims=True))
                    alpha = jnp.exp(m - m_new)
                    pe = jnp.exp(sc - m_new)
                    l = alpha * l + jnp.sum(pe, axis=1, keepdims=True)
                    acc = alpha * acc + _dot(pe, v_ref[pl.ds(k0, tk), :])
                    return m_new, l, acc

                init = (jnp.full((tq, 1), NEG_BIG, F32), jnp.zeros((tq, 1), F32), jnp.zeros((tq, 128), F32))
                m, l, acc = lax.fori_loop(0, qi + 1, k_loop, init)
                out = (acc / l).astype(BF16)
                if hh == 0:
                    o_ref[pl.ds(q0, tq), :] = out
                else:
                    o_ref[pl.ds(q0, tq), :] = jnp.where(lo, o_ref[pl.ds(q0, tq), :], out)
                old = lse_ref[pl.ds(q0, tq), :]
                lse_ref[pl.ds(q0, tq), :] = jnp.where(_lane_iota((tq, 128)) == h, m + jnp.log(l), old)
                return 0

            lax.fori_loop(0, nq, q_loop, 0)

    pair = lambda col0: pl.BlockSpec((s, 128), lambda p: (0, col0 + p))
    full = pl.BlockSpec((s, 128), lambda p: (0, 0))
    return pl.pallas_call(
        body, name="fox_fwd", grid=(N_PAIRS,),
        in_specs=[pair(0), pair(0), pair(V_COL * 8), full, pl.BlockSpec((128, s), lambda p: (0, 0))],
        out_specs=[pair(0), full],
        out_shape=[jax.ShapeDtypeStruct((s, 1024), BF16), jax.ShapeDtypeStruct((s, 128), F32)],
        scratch_shapes=[pltpu.VMEM((8, s), F32)], compiler_params=_params(("arbitrary",)),
    )(qn, kn, proj, cum, cumt)


def fox_bwd(qn, kn, proj, cum, cumt, lse, dmixed):
    s = qn.shape[0]
    tq, tk = FOX_TQ, FOX_TK
    nq = s // tq

    def body(q_ref, k_ref, v_ref, cum_ref, cumt_ref, lse_ref, do_ref, dq_ref, dk_ref, dv_ref, dcumt_ref,
             crow, dcs, pm_scr, dp_scr):
        p = pl.program_id(0)

        @pl.when(p == 0)
        def _():
            dcumt_ref[...] = jnp.zeros_like(dcumt_ref)

        dk_ref[...] = jnp.zeros_like(dk_ref)
        dv_ref[...] = jnp.zeros_like(dv_ref)
        dcs[...] = jnp.zeros_like(dcs)
        lo = _lane_iota((tq, 128)) < HEAD_DIM
        for hh in range(2):
            h = 2 * p + hh
            mine = lo if hh == 0 else jnp.logical_not(lo)
            crow[hh:hh + 1, :] = _pick_row(cumt_ref[...], F_LANE + h)

            def q_loop(qi, _, hh=hh, h=h, mine=mine):
                q0 = pl.multiple_of(qi * tq, tq)
                qb = jnp.where(mine, q_ref[pl.ds(q0, tq), :], 0.0)
                cq = _pick_lane(cum_ref[pl.ds(q0, tq), :], F_LANE + h)
                lse_q = _pick_lane(lse_ref[pl.ds(q0, tq), :], h)
                dob = jnp.where(mine, do_ref[pl.ds(q0, tq), :], 0.0).astype(BF16)

                def pass1(kj, delta):
                    k0 = pl.multiple_of(kj * tk, tk)
                    kb = jnp.where(mine, k_ref[pl.ds(k0, tk), :], 0.0)
                    vb = jnp.where(mine, v_ref[pl.ds(k0, tk), :], 0.0)
                    pm = jnp.exp(_causal_scores(qb, kb, cq, crow[hh:hh + 1, pl.ds(k0, tk)], q0, k0) - lse_q)
                    dp = _dot(dob, vb, "nt")
                    pm_scr[kj] = pm
                    dp_scr[kj] = dp
                    return delta + jnp.sum(pm * dp, axis=1, keepdims=True)

                delta = lax.fori_loop(0, qi + 1, pass1, jnp.zeros((tq, 1), F32))

                def pass2(kj, dq):
                    k0 = pl.multiple_of(kj * tk, tk)
                    kb = jnp.where(mine, k_ref[pl.ds(k0, tk), :], 0.0)
                    pm = pm_scr[kj]
                    ds = pm * (dp_scr[kj] - delta)
                    dk_ref[pl.ds(k0, tk), :] += _dot(ds, qb, "tn")
                    dv_ref[pl.ds(k0, tk), :] += _dot(pm, dob, "tn")
                    dcs[hh:hh + 1, pl.ds(k0, tk)] += jnp.sum(ds, axis=0, keepdims=True)
                    return dq + _dot(ds, kb)

                dq = lax.fori_loop(0, qi + 1, pass2, jnp.zeros((tq, 128), F32))
                if hh == 0:
                    dq_ref[pl.ds(q0, tq), :] = dq
                else:
                    dq_ref[pl.ds(q0, tq), :] += dq
                return 0

            lax.fori_loop(0, nq, q_loop, 0)
            dcumt_ref[...] = jnp.where(_row_iota((128, s)) == F_LANE + h, -dcs[hh:hh + 1, :], dcumt_ref[...])

    pair = lambda col0: pl.BlockSpec((s, 128), lambda p: (0, col0 + p))
    full = pl.BlockSpec((s, 128), lambda p: (0, 0))
    wide = pl.BlockSpec((128, s), lambda p: (0, 0))
    return pl.pallas_call(
        body, name="fox_bwd", grid=(N_PAIRS,),
        in_specs=[pair(0), pair(0), pair(V_COL * 8), full, wide, full, pair(8)],
        out_specs=[pair(0), pair(0), pair(0), wide],
        out_shape=[jax.ShapeDtypeStruct((s, 1024), F32), jax.ShapeDtypeStruct((s, 1024), F32),
                   jax.ShapeDtypeStruct((s, 1024), F32), jax.ShapeDtypeStruct((128, s), F32)],
        scratch_shapes=[pltpu.VMEM((8, s), F32), pltpu.VMEM((8, s), F32),
                        pltpu.VMEM((nq, tq, tk), F32), pltpu.VMEM((nq, tq, tk), F32)],
        compiler_params=_params(("arbitrary",)),
    )(qn, kn, proj, cum, cumt, lse, dmixed)


def fox_post(dqn, dkn, proj, small, smallp, qw, kw, sel, selt, dcumt, *, tm=256):
    s = proj.shape[0]
    nrow = s // tm

    def body(dqn_ref, dkn_ref, q_ref, k_ref, small_ref, sp_ref, qw_ref, kw_ref, sel_ref, selt_ref, dcumt_ref,
             dq_ref, dk_ref, dsmall_ref, gqw_ref, gkw_ref, gfb_ref, carry):
        step = pl.program_id(0)

        @pl.when(step == 0)
        def _():
            carry[...] = jnp.zeros_like(carry)

        def norm_bwd(x_ref, w_ref, dn, out_ref):
            x = x_ref[...]
            rf = _head_rstd(x, sel_ref, selt_ref)
            xh = x * rf
            g = dn * w_ref[...]
            mean_gx = _split3_dot(_split3_dot(g * xh, sel_ref[...]) * (1.0 / HEAD_DIM), selt_ref[...])
            out_ref[...] = (rf * (g - xh * mean_gx)).astype(BF16)
            return jnp.sum(dn * xh, axis=0, keepdims=True)

        gqw = norm_bwd(q_ref, qw_ref, dqn_ref[...] * FOX_SCALE, dq_ref)
        gkw = norm_bwd(k_ref, kw_ref, dkn_ref[...], dk_ref)
        li = _lane_iota((tm, 128))
        f_lane = jnp.logical_and(li >= F_LANE, li < F_LANE + N_HEADS)
        dlogf = _rev_cumsum_rows(dcumt_ref[...].T) + carry[...]
        carry[...] = dlogf[0:1, :]
        dfr = jnp.where(f_lane, dlogf * _sigmoid(-(small_ref[...] + sp_ref[3:4, :])), 0.0)
        dsmall_ref[...] = dfr
        gfb = jnp.sum(dfr, axis=0, keepdims=True)

        @pl.when(step == 0)
        def _():
            gqw_ref[...] = gqw
            gkw_ref[...] = gkw
            gfb_ref[...] = gfb

        @pl.when(step > 0)
        def _():
            gqw_ref[...] += gqw
            gkw_ref[...] += gkw
            gfb_ref[...] += gfb

    def rb(i):
        return nrow - 1 - i

    row = pl.BlockSpec((tm, 1024), lambda i: (rb(i), 0))
    vec = pl.BlockSpec((1, 1024), lambda i: (0, 0))
    return pl.pallas_call(
        body, name="fox_post", grid=(nrow,),
        in_specs=[row, row, pl.BlockSpec((tm, 1024), lambda i: (rb(i), Q_COL)), pl.BlockSpec((tm, 1024), lambda i: (rb(i), K_COL)),
                  pl.BlockSpec((tm, 128), lambda i: (rb(i), 0)), pl.BlockSpec((8, 128), lambda i: (0, 0)), vec, vec,
                  pl.BlockSpec((1024, 128), lambda i: (0, 0)), pl.BlockSpec((128, 1024), lambda i: (0, 0)),
                  pl.BlockSpec((128, tm), lambda i: (0, rb(i)))],
        out_specs=[row, row, pl.BlockSpec((tm, 128), lambda i: (rb(i), 0)), vec, vec, pl.BlockSpec((1, 128), lambda i: (0, 0))],
        out_shape=[jax.ShapeDtypeStruct((s, 1024), BF16), jax.ShapeDtypeStruct((s, 1024), BF16),
                   jax.ShapeDtypeStruct((s, 128), F32), jax.ShapeDtypeStruct((1, 1024), F32),
                   jax.ShapeDtypeStruct((1, 1024), F32), jax.ShapeDtypeStruct((1, 128), F32)],
        scratch_shapes=[pltpu.VMEM((1, 128), F32)], compiler_params=_params(("arbitrary",)),
    )(dqn, dkn, proj, proj, small, smallp, qw, kw, sel, selt, dcumt)


def local_step(x, target, wm, ws, w_out, w_up, w_down, ssd_cw8, ssd_cb, smallp, ssd_nw, qw_t, kw_t, sel, selt,
               norm_mix_w, norm_ffn_w, ffn_cw8, ffn_cb):
    h = rms_fwd(x, norm_mix_w, name="rms_mix_fwd")
    proj = matmul(h, wm, mode="nn", tm=1024, tn=512, tk=1024, out_dtype=F32, name="mm_in_proj")
    small = matmul(h, ws, mode="nn", tm=1024, tn=128, tk=1024, out_dtype=F32, name="mm_in_proj_small")
    y_ssd, ypre, states = ssd_fwd(proj, small, ssd_cw8, ssd_cb, smallp, ssd_nw)
    qn, kn, cum, cumt = fox_prep(proj, small, smallp, qw_t, kw_t, sel, selt)
    y_fox, lse = fox_fwd(qn, kn, proj, cum, cumt)
    x1 = matmul(y_ssd, w_out, mode="nn", tm=512, tn=1024, tk=1024, out_dtype=F32, name="mm_out_ssd", add=x)
    x1 = matmul(y_fox, w_out, mode="nn", tm=512, tn=1024, tk=1024, out_dtype=F32, name="mm_out_fox", add=x1, b_koff=1)
    hf = rms_fwd(x1, norm_ffn_w, name="rms_ffn_fwd")
    hu = matmul(hf, w_up, mode="nn", tm=1024, tn=512, tk=1024, out_dtype=F32, name="mm_up")
    act = ffn_mid_fwd(hu, ffn_cw8, ffn_cb)
    y = matmul(act, w_down, mode="nn", tm=512, tn=1024, tk=1408, out_dtype=F32, name="mm_down", add=x1)
    dy, sq = loss_head(y, target)

    dact = matmul(dy, w_down, mode="nt", tm=512, tn=1408, tk=1024, out_dtype=F32, name="mm_dact")
    g_down = matmul(act, dy, mode="tn", tm=1408, tn=1024, tk=512, out_dtype=BF16, name="mm_dw_down")
    dhu_g, dhu_v, gcw_g, gcw_v = ffn_mid_bwd(hu, dact, ffn_cw8, ffn_cb)
    dhf = matmul(dhu_g, w_up, mode="nt", tm=512, tn=1024, tk=1408, out_dtype=F32, name="mm_dhf_gate")
    dhf = matmul(dhu_v, w_up, mode="nt", tm=512, tn=1024, tk=1408, out_dtype=F32, name="mm_dhf_val", add=dhf, b_koff=2)
    g_up_g = matmul(hf, dhu_g, mode="tn", tm=1024, tn=1408, tk=512, out_dtype=BF16, name="mm_dw_up_gate")
    g_up_v = matmul(hf, dhu_v, mode="tn", tm=1024, tn=1408, tk=512, out_dtype=BF16, name="mm_dw_up_val")
    dx1, g_norm_ffn = rms_bwd(dhf, x1, norm_ffn_w, dy, name="rms_ffn_bwd")
    dmixed = matmul(dx1, w_out, mode="nt", tm=512, tn=1024, tk=1024, out_dtype=F32, name="mm_dmixed")
    g_out_a = matmul(y_ssd, dx1, mode="tn", tm=1024, tn=1024, tk=512, out_dtype=BF16, name="mm_dw_out_ssd")
    g_out_b = matmul(y_fox, dx1, mode="tn", tm=1024, tn=1024, tk=512, out_dtype=BF16, name="mm_dw_out_fox")
    dz, dxs, db, dc, dsmall_ssd, gcw_x, gcw_b, gcw_c, g_sp, g_ssd_nw = ssd_bwd(
        proj, small, ssd_cw8, ssd_cb, smallp, ssd_nw, ypre, states, dmixed)
    dqn, dkn, dv, dcumt = fox_bwd(qn, kn, proj, cum, cumt, lse, dmixed)
    dq, dk, dsmall_fox, g_qw, g_kw, g_fb = fox_post(dqn, dkn, proj, small, smallp, qw_t, kw_t, sel, selt, dcumt)
    dproj = jnp.concatenate([dz, dxs, dq, dk, dv.astype(BF16), db, dc], axis=1)
    dsmall = (dsmall_ssd + dsmall_fox).astype(BF16)
    dh = matmul(dproj, wm, mode="nt", tm=512, tn=1024, tk=512, out_dtype=F32, name="mm_dh")
    dh = matmul(dsmall, ws, mode="nt", tm=512, tn=1024, tk=128, out_dtype=F32, name="mm_dh_small", add=dh)
    g_wm = matmul(h, dproj, mode="tn", tm=1024, tn=1408, tk=512, out_dtype=BF16, name="mm_dw_in")
    g_ws = matmul(h, dsmall, mode="tn", tm=1024, tn=128, tk=512, out_dtype=BF16, name="mm_dw_in_small")
    grad_x, g_norm_mix = rms_bwd(dh, x, norm_mix_w, dx1, name="rms_mix_bwd")
    return dict(
        sq=sq, grad_x=grad_x, g_wm=g_wm, g_ws=g_ws, g_out=jnp.concatenate([g_out_a, g_out_b], axis=0),
        g_up=jnp.concatenate([g_up_g, g_up_v], axis=1), g_down=g_down,
        g_norm_mix=g_norm_mix, g_norm_ffn=g_norm_ffn, g_ssd_nw=g_ssd_nw,
        g_ssd_cw=jnp.concatenate([gcw_x, gcw_b, gcw_c], axis=1), g_sp=g_sp, g_fb=g_fb, g_qw=g_qw, g_kw=g_kw,
        g_ffn_cw=jnp.concatenate([gcw_g, gcw_v], axis=1))


def adamw(w, g, m, v, *, name, tr):
    rows, cols = w.shape

    def body(w_ref, g_ref, m_ref, v_ref, d_ref, mo_ref, vo_ref):
        gv = g_ref[...]
        mn = ADAM_B1 * m_ref[...] + (1.0 - ADAM_B1) * gv
        vn = ADAM_B2 * v_ref[...] + (1.0 - ADAM_B2) * (gv * gv)
        m_hat = mn / (1.0 - ADAM_B1 ** ADAM_STEP)
        v_hat = vn / (1.0 - ADAM_B2 ** ADAM_STEP)
        d_ref[...] = -ADAM_LR * (m_hat / (jnp.sqrt(v_hat) + ADAM_EPS) + ADAM_WD * w_ref[...])
        mo_ref[...] = mn
        vo_ref[...] = vn

    blk = pl.BlockSpec((tr, cols), lambda i: (i, 0))
    shp = jax.ShapeDtypeStruct((rows, cols), F32)
    return pl.pallas_call(
        body, name=name, grid=(rows // tr,), in_specs=[blk] * 4, out_specs=[blk] * 3, out_shape=[shp] * 3,
        compiler_params=_params(("parallel",)),
    )(w, g, m, v)


def add_pair(a, b, *, name, tr):
    _, rows, cols = a.shape

    def body(a_ref, b_ref, o_ref):
        o_ref[...] = (a_ref[...].astype(F32) + b_ref[...].astype(F32)).astype(BF16)

    blk = pl.BlockSpec((1, tr, cols), lambda j, i: (j, i, 0))
    return pl.pallas_call(
        body, name=name, grid=(4, rows // tr), in_specs=[blk, blk], out_specs=blk,
        out_shape=jax.ShapeDtypeStruct(a.shape, BF16), compiler_params=_params(("parallel", "parallel")),
    )(a, b)


def sum_chips(parts, *, name, tr):
    _, rows, cols = parts.shape

    def body(p_ref, o_ref):
        acc = p_ref[0].astype(F32)
        for k in range(1, 4):
            acc = acc + p_ref[k].astype(F32)
        o_ref[...] = acc

    return pl.pallas_call(
        body, name=name, grid=(rows // tr,), in_specs=[pl.BlockSpec((4, tr, cols), lambda i: (0, i, 0))],
        out_specs=pl.BlockSpec((tr, cols), lambda i: (i, 0)), out_shape=jax.ShapeDtypeStruct((rows, cols), F32),
        compiler_params=_params(("parallel",)),
    )(parts)


ANY = pl.BlockSpec(memory_space=pl.ANY)


def _place():
    x, y, c = lax.axis_index("x"), lax.axis_index("y"), lax.axis_index("c")
    chips = [(1 - x, y), (x, 1 - y), (1 - x, 1 - y)]
    return x, y, c, chips


def gather_weights(shards):
    n = len(shards)

    def body(*refs):
        ins, outs = refs[:n], refs[n:2 * n]
        send_sems, recv_sems, local_sems = refs[2 * n:]
        x, y, c, chips = _place()
        me = 2 * x + y
        sibling = (x, y, 1 - c)

        def half(a, blk):
            rows = ins[a].shape[0] // 2
            return outs[a].at[blk, pl.ds(c * rows, rows), :]

        def to_chip(a, t):
            rows = ins[a].shape[0] // 2
            return pltpu.make_async_remote_copy(
                src_ref=ins[a].at[pl.ds(c * rows, rows), :], dst_ref=half(a, me),
                send_sem=send_sems.at[a, t], recv_sem=recv_sems.at[a, t],
                device_id=(*chips[t], c), device_id_type=MESH)

        def from_chip(a, t):
            blk = 2 * chips[t][0] + chips[t][1]
            return pltpu.make_async_remote_copy(
                src_ref=half(a, blk), dst_ref=half(a, blk), send_sem=send_sems.at[a, t], recv_sem=recv_sems.at[a, t],
                device_id=(*chips[t], c), device_id_type=MESH)

        def to_sibling(a, t):
            blk = 2 * chips[t][0] + chips[t][1]
            return pltpu.make_async_remote_copy(
                src_ref=half(a, blk), dst_ref=half(a, blk), send_sem=send_sems.at[a, 3 + t], recv_sem=recv_sems.at[a, 3 + t],
                device_id=sibling, device_id_type=MESH)

        def from_sibling(a, t):
            blk = 2 * chips[t][0] + chips[t][1]
            rows = ins[a].shape[0] // 2
            dst = outs[a].at[blk, pl.ds((1 - c) * rows, rows), :]
            return pltpu.make_async_remote_copy(
                src_ref=dst, dst_ref=dst, send_sem=send_sems.at[a, 3 + t], recv_sem=recv_sems.at[a, 3 + t],
                device_id=sibling, device_id_type=MESH)

        own = [pltpu.make_async_copy(ins[a], outs[a].at[me], local_sems.at[a]) for a in range(n)]
        for cp in own:
            cp.start()
        first = [[to_chip(a, t) for t in range(3)] for a in range(n)]
        for a in range(n):
            for t in range(3):
                first[a][t].start()
        passed = [[to_sibling(a, t) for t in range(3)] for a in range(n)]
        for a in range(n):
            for t in range(3):
                from_chip(a, t).wait_recv()
                passed[a][t].start()
        for a in range(n):
            for t in range(3):
                from_sibling(a, t).wait_recv()
        for a in range(n):
            for t in range(3):
                first[a][t].wait_send()
                passed[a][t].wait_send()
            own[a].wait()

    return pl.pallas_call(
        body, name="gather_weights", in_specs=[ANY] * n, out_specs=[ANY] * n,
        out_shape=[jax.ShapeDtypeStruct((4,) + s.shape, s.dtype) for s in shards],
        scratch_shapes=[pltpu.SemaphoreType.DMA((n, 6)), pltpu.SemaphoreType.DMA((n, 6)), pltpu.SemaphoreType.DMA((n,))],
    )(*shards)


def pair_swap_halves(grads):
    n = len(grads)

    def body(*refs):
        ins, mine, theirs = refs[:n], refs[n:2 * n], refs[2 * n:3 * n]
        send_sems, recv_sems, local_sems = refs[3 * n:]
        x, y, c, _ = _place()
        copies = []
        for a in range(n):
            rows = ins[a].shape[1] // 2
            keep = pltpu.make_async_copy(ins[a].at[:, pl.ds(c * rows, rows), :], mine[a], local_sems.at[a])
            swap = pltpu.make_async_remote_copy(
                src_ref=ins[a].at[:, pl.ds((1 - c) * rows, rows), :], dst_ref=theirs[a],
                send_sem=send_sems.at[a], recv_sem=recv_sems.at[a], device_id=(x, y, 1 - c), device_id_type=MESH)
            keep.start()
            swap.start()
            copies.append((keep, swap))
        for keep, swap in copies:
            swap.wait()
            keep.wait()

    halves = [jax.ShapeDtypeStruct((4, g.shape[1] // 2, g.shape[2]), g.dtype) for g in grads]
    outs = pl.pallas_call(
        body, name="pair_swap_halves", in_specs=[ANY] * n, out_specs=[ANY] * (2 * n), out_shape=halves + halves,
        scratch_shapes=[pltpu.SemaphoreType.DMA((n,)), pltpu.SemaphoreType.DMA((n,)), pltpu.SemaphoreType.DMA((n,))],
    )(*grads)
    return outs[:n], outs[n:]


def scatter_to_chips(parts):
    n = len(parts)

    def body(*refs):
        ins, outs = refs[:n], refs[n:2 * n]
        send_sems, recv_sems, local_sems = refs[2 * n:]
        x, y, c, chips = _place()
        me = 2 * x + y
        keeps, sends = [], []
        for a in range(n):
            keep = pltpu.make_async_copy(ins[a].at[me], outs[a].at[me], local_sems.at[a])
            keep.start()
            keeps.append(keep)
            for t in range(3):
                blk = 2 * chips[t][0] + chips[t][1]
                cp = pltpu.make_async_remote_copy(
                    src_ref=ins[a].at[blk], dst_ref=outs[a].at[me], send_sem=send_sems.at[a, t], recv_sem=recv_sems.at[a, t],
                    device_id=(*chips[t], c), device_id_type=MESH)
                cp.start()
                sends.append(cp)
        for a in range(n):
            for t in range(3):
                blk = 2 * chips[t][0] + chips[t][1]
                pltpu.make_async_remote_copy(
                    src_ref=outs[a].at[blk], dst_ref=outs[a].at[blk], send_sem=send_sems.at[a, t], recv_sem=recv_sems.at[a, t],
                    device_id=(*chips[t], c), device_id_type=MESH).wait_recv()
        for cp in sends:
            cp.wait_send()
        for keep in keeps:
            keep.wait()

    return pl.pallas_call(
        body, name="scatter_to_chips", in_specs=[ANY] * n, out_specs=[ANY] * n,
        out_shape=[jax.ShapeDtypeStruct(p.shape, p.dtype) for p in parts],
        scratch_shapes=[pltpu.SemaphoreType.DMA((n, 3)), pltpu.SemaphoreType.DMA((n, 3)), pltpu.SemaphoreType.DMA((n,))],
    )(*parts)


def pair_join_halves(halves):
    n = len(halves)

    def body(*refs):
        ins, outs = refs[:n], refs[n:2 * n]
        send_sems, recv_sems, local_sems = refs[2 * n:]
        x, y, c, _ = _place()
        copies = []
        for a in range(n):
            rows = ins[a].shape[0]
            keep = pltpu.make_async_copy(ins[a], outs[a].at[pl.ds(c * rows, rows), :], local_sems.at[a])
            give = pltpu.make_async_remote_copy(
                src_ref=ins[a], dst_ref=outs[a].at[pl.ds(c * rows, rows), :],
                send_sem=send_sems.at[a], recv_sem=recv_sems.at[a], device_id=(x, y, 1 - c), device_id_type=MESH)
            keep.start()
            give.start()
            copies.append((keep, give))
        for a, (keep, give) in enumerate(copies):
            rows = ins[a].shape[0]
            got = outs[a].at[pl.ds((1 - c) * rows, rows), :]
            pltpu.make_async_remote_copy(
                src_ref=got, dst_ref=got, send_sem=send_sems.at[a], recv_sem=recv_sems.at[a],
                device_id=(x, y, 1 - c), device_id_type=MESH).wait_recv()
            give.wait_send()
            keep.wait()

    return pl.pallas_call(
        body, name="pair_join_halves", in_specs=[ANY] * n, out_specs=[ANY] * n,
        out_shape=[jax.ShapeDtypeStruct((2 * h.shape[0], h.shape[1]), h.dtype) for h in halves],
        scratch_shapes=[pltpu.SemaphoreType.DMA((n,)), pltpu.SemaphoreType.DMA((n,)), pltpu.SemaphoreType.DMA((n,))],
    )(*halves)


def allreduce_small(packed):
    rows = packed.shape[0]

    def body(in_ref, out_ref, gathered, send_sems, recv_sems):
        x, y, c, _ = _place()
        me = 4 * x + 2 * y + c
        gathered[me] = in_ref[...]
        flips = [(fx, fy, fc) for fx in (0, 1) for fy in (0, 1) for fc in (0, 1)][1:]
        peers = [((1 - x) if fx else x, (1 - y) if fy else y, (1 - c) if fc else c) for fx, fy, fc in flips]
        copies = []
        for t, peer in enumerate(peers):
            cp = pltpu.make_async_remote_copy(
                src_ref=in_ref, dst_ref=gathered.at[me], send_sem=send_sems.at[t], recv_sem=recv_sems.at[t],
                device_id=peer, device_id_type=MESH)
            cp.start()
            copies.append(cp)
        for t, (px, py, pc) in enumerate(peers):
            slot = gathered.at[4 * px + 2 * py + pc]
            pltpu.make_async_remote_copy(
                src_ref=slot, dst_ref=slot, send_sem=send_sems.at[t], recv_sem=recv_sems.at[t],
                device_id=(px, py, pc), device_id_type=MESH).wait_recv()
        for cp in copies:
            cp.wait_send()
        acc = gathered[0]
        for k in range(1, 8):
            acc = acc + gathered[k]
        out_ref[...] = acc

    vm = pl.BlockSpec(memory_space=pltpu.VMEM)
    return pl.pallas_call(
        body, name="allreduce_small", in_specs=[vm], out_specs=vm, out_shape=jax.ShapeDtypeStruct(packed.shape, F32),
        scratch_shapes=[pltpu.VMEM((8, rows, 128), F32), pltpu.SemaphoreType.DMA((7,)), pltpu.SemaphoreType.DMA((7,))],
    )(packed)


SMALL_NAMES = ("norm_mix_w", "ssd_conv_w", "ssd_conv_b", "ssd_dt_bias", "ssd_a_log", "ssd_d", "ssd_norm_w", "fox_f_bias",
               "fox_q_norm_w", "fox_k_norm_w", "norm_ffn_w", "ffn_conv_w", "ffn_conv_b")
BIG_NAMES = ("w_in", "w_out", "w_up", "w_down")
WEIGHT_ORDER = ("norm_mix_w", "w_in", "ssd_conv_w", "ssd_conv_b", "ssd_dt_bias", "ssd_a_log", "ssd_d", "ssd_norm_w",
                "fox_f_bias", "fox_q_norm_w", "fox_k_norm_w", "w_out", "norm_ffn_w", "w_up", "ffn_conv_w", "ffn_conv_b", "w_down")
ADAM_ROWS = {"w_in": 256, "w_out": 256, "w_up": 256, "w_down": 176}


def _pack(arrays):
    rows = []
    for a in arrays:
        flat = a.reshape(-1).astype(F32)
        rows.append(jnp.pad(flat, (0, (-flat.shape[0]) % 1024)).reshape(-1, 128))
    return jnp.concatenate(rows, axis=0)


def _unpack(packed, shapes):
    out, r = [], 0
    for shp in shapes:
        size = 1
        for d in shp:
            size *= d
        nrow = 8 * (-(-size // 1024))
        out.append(packed[r:r + nrow].reshape(-1)[:size].reshape(shp))
        r += nrow
    return out


def _pad_rows(a, rows):
    return jnp.pad(a, ((0, rows - a.shape[0]), (0, 0)))


def kernel(x, norm_mix_w, w_in, ssd_conv_w, ssd_conv_b, ssd_dt_bias, ssd_a_log, ssd_d, ssd_norm_w, fox_f_bias, fox_q_norm_w, fox_k_norm_w, w_out, norm_ffn_w, w_up, ffn_conv_w, ffn_conv_b, w_down, loss_target, m_norm_mix_w, m_w_in, m_ssd_conv_w, m_ssd_conv_b, m_ssd_dt_bias, m_ssd_a_log, m_ssd_d, m_ssd_norm_w, m_fox_f_bias, m_fox_q_norm_w, m_fox_k_norm_w, m_w_out, m_norm_ffn_w, m_w_up, m_ffn_conv_w, m_ffn_conv_b, m_w_down, v_norm_mix_w, v_w_in, v_ssd_conv_w, v_ssd_conv_b, v_ssd_dt_bias, v_ssd_a_log, v_ssd_d, v_ssd_norm_w, v_fox_f_bias, v_fox_q_norm_w, v_fox_k_norm_w, v_w_out, v_norm_ffn_w, v_w_up, v_ffn_conv_w, v_ffn_conv_b, v_w_down):
    w = dict(norm_mix_w=norm_mix_w, w_in=w_in, ssd_conv_w=ssd_conv_w, ssd_conv_b=ssd_conv_b, ssd_dt_bias=ssd_dt_bias,
             ssd_a_log=ssd_a_log, ssd_d=ssd_d, ssd_norm_w=ssd_norm_w, fox_f_bias=fox_f_bias, fox_q_norm_w=fox_q_norm_w,
             fox_k_norm_w=fox_k_norm_w, w_out=w_out, norm_ffn_w=norm_ffn_w, w_up=w_up, ffn_conv_w=ffn_conv_w,
             ffn_conv_b=ffn_conv_b, w_down=w_down)
    m = dict(norm_mix_w=m_norm_mix_w, w_in=m_w_in, ssd_conv_w=m_ssd_conv_w, ssd_conv_b=m_ssd_conv_b, ssd_dt_bias=m_ssd_dt_bias,
             ssd_a_log=m_ssd_a_log, ssd_d=m_ssd_d, ssd_norm_w=m_ssd_norm_w, fox_f_bias=m_fox_f_bias, fox_q_norm_w=m_fox_q_norm_w,
             fox_k_norm_w=m_fox_k_norm_w, w_out=m_w_out, norm_ffn_w=m_norm_ffn_w, w_up=m_w_up, ffn_conv_w=m_ffn_conv_w,
             ffn_conv_b=m_ffn_conv_b, w_down=m_w_down)
    v = dict(norm_mix_w=v_norm_mix_w, w_in=v_w_in, ssd_conv_w=v_ssd_conv_w, ssd_conv_b=v_ssd_conv_b, ssd_dt_bias=v_ssd_dt_bias,
             ssd_a_log=v_ssd_a_log, ssd_d=v_ssd_d, ssd_norm_w=v_ssd_norm_w, fox_f_bias=v_fox_f_bias, fox_q_norm_w=v_fox_q_norm_w,
             fox_k_norm_w=v_fox_k_norm_w, w_out=v_w_out, norm_ffn_w=v_norm_ffn_w, w_up=v_w_up, ffn_conv_w=v_ffn_conv_w,
             ffn_conv_b=v_ffn_conv_b, w_down=v_w_down)
    chip = 2 * lax.axis_index("x") + lax.axis_index("y")

    shards = [w_in[0].astype(BF16), w_out[0].astype(BF16), w_up[0].astype(BF16), w_down[0].astype(BF16),
              _pad_rows(ssd_conv_w[0], 16), _pad_rows(ffn_conv_w[0], 16)]
    a_in, a_out, a_up, a_down, a_scw, a_fcw = gather_weights(shards)
    w_full = a_in.transpose(1, 0, 2).reshape(D_MODEL, IN_COLS)
    wm = jnp.concatenate([w_full[:, :2048], w_full[:, 2576:5648], w_full[:, 2048:2560]], axis=1)
    ws = jnp.concatenate([w_full[:, 2560:2576], w_full[:, 5648:5664], jnp.zeros((D_MODEL, SMALL_COLS - 32), BF16)], axis=1)
    wo = a_out.reshape(2048, D_MODEL)
    wu = a_up.transpose(1, 0, 2).reshape(D_MODEL, 2 * D_FF)
    wd = a_down.reshape(D_FF, D_MODEL)
    ssd_cw8 = a_scw.transpose(1, 0, 2).reshape(16, 1536)[:8]
    ffn_cw8 = a_fcw.transpose(1, 0, 2).reshape(16, 2 * D_FF)[:8]
    smallp = jnp.zeros((8, 128), F32)
    smallp = smallp.at[0, :16].set(ssd_dt_bias[0]).at[1, :16].set(ssd_a_log[0]).at[2, :16].set(ssd_d[0])
    smallp = smallp.at[3, F_LANE:F_LANE + 16].set(fox_f_bias[0])
    qw_t = jnp.tile(fox_q_norm_w[0], N_HEADS)[None]
    kw_t = jnp.tile(fox_k_norm_w[0], N_HEADS)[None]
    sel = (jnp.arange(1024)[:, None] // HEAD_DIM == jnp.arange(128)[None, :]).astype(BF16)

    res = local_step(x[0], loss_target[0], wm, ws, wo, wu, wd, ssd_cw8, ssd_conv_b, smallp, ssd_norm_w, qw_t, kw_t,
                     sel, sel.T, norm_mix_w, norm_ffn_w, ffn_cw8, ffn_conv_b)

    full_shapes = [(1, 1024), (1, 4, 1536), (1, 1536), (1, 16), (1, 16), (1, 16), (1, 1024), (1, 16), (1, 64), (1, 64),
                   (1, 1024), (1, 3, 2 * D_FF), (1, 2 * D_FF), (1,)]
    local_small = [res["g_norm_mix"], res["g_ssd_cw"][:4], res["g_ssd_cw"][4], res["g_sp"][0, :16], res["g_sp"][1, :16],
                   res["g_sp"][2, :16], res["g_ssd_nw"], res["g_fb"][0, F_LANE:F_LANE + 16],
                   res["g_qw"].reshape(N_HEADS, HEAD_DIM).sum(0), res["g_kw"].reshape(N_HEADS, HEAD_DIM).sum(0),
                   res["g_norm_ffn"], res["g_ffn_cw"][:3], res["g_ffn_cw"][3], jnp.sum(res["sq"])]
    summed = _unpack(allreduce_small(_pack(local_small)), full_shapes)
    loss = (0.5 / D_MODEL) * summed[-1][0]
    g_small = dict(zip(SMALL_NAMES, summed[:-1]))
    g_small["ssd_conv_w"] = lax.dynamic_slice(g_small["ssd_conv_w"], (0, 0, 384 * chip), (1, 4, 384))
    g_small["ffn_conv_w"] = lax.dynamic_slice(g_small["ffn_conv_w"], (0, 0, 1408 * chip), (1, 3, 1408))

    g_wm, g_ws = res["g_wm"], res["g_ws"]
    g_in_full = jnp.concatenate([g_wm[:, :2048], g_wm[:, 5120:5632], g_ws[:, :16], g_wm[:, 2048:5120], g_ws[:, 16:32]], axis=1)
    big = [g_in_full.reshape(D_MODEL, 4, 1416).transpose(1, 0, 2), res["g_out"].reshape(4, 512, D_MODEL),
           res["g_up"].reshape(D_MODEL, 4, 1408).transpose(1, 0, 2), res["g_down"].reshape(4, 704, D_MODEL)]
    mine, theirs = pair_swap_halves(big)
    parts = [add_pair(a, b, name="add_pair_" + n, tr=ADAM_ROWS[n]) for a, b, n in zip(mine, theirs, BIG_NAMES)]
    landed = scatter_to_chips(parts)
    halves = [sum_chips(p, name="sum_chips_" + n, tr=ADAM_ROWS[n]) for p, n in zip(landed, BIG_NAMES)]
    g_big = dict(zip(BIG_NAMES, pair_join_halves(halves)))

    grads, deltas, new_m, new_v = {}, {}, {}, {}
    for n in BIG_NAMES:
        d, mn, vn = adamw(w[n][0], g_big[n], m[n][0], v[n][0], name="adamw_" + n, tr=ADAM_ROWS[n])
        grads[n], deltas[n], new_m[n], new_v[n] = g_big[n][None], d[None], mn[None], vn[None]
    shapes = [w[n].shape for n in SMALL_NAMES]
    d, mn, vn = adamw(_pack([w[n] for n in SMALL_NAMES]), _pack([g_small[n] for n in SMALL_NAMES]),
                      _pack([m[n] for n in SMALL_NAMES]), _pack([v[n] for n in SMALL_NAMES]), name="adamw_small", tr=8)
    for n, dd, mm, vv in zip(SMALL_NAMES, _unpack(d, shapes), _unpack(mn, shapes), _unpack(vn, shapes)):
        grads[n], deltas[n], new_m[n], new_v[n] = g_small[n].reshape(w[n].shape), dd, mm, vv
    return (loss, res["grad_x"][None], *[grads[n] for n in WEIGHT_ORDER], *[deltas[n] for n in WEIGHT_ORDER],
            *[new_m[n] for n in WEIGHT_ORDER], *[new_v[n] for n in WEIGHT_ORDER])
```

```python
import functools

import jax
import jax.numpy as jnp
from jax import lax
from jax.experimental import pallas as pl
from jax.experimental.pallas import tpu as pltpu

F32 = jnp.float32
BF16 = jnp.bfloat16
MESH = pl.DeviceIdType.MESH

D_MODEL = 1024
HEAD_DIM = 64
N_HEADS = 16
N_PAIRS = N_HEADS // 2
SSD_CHUNK = 128
SSD_STATE = 128
SSD_CONV = 4
D_FF = 2816
FFN_CONV = 3
NORM_EPS = 1e-6
MAIN_COLS = 5632
SMALL_COLS = 128
F_LANE = 16
IN_COLS = 5664

ADAM_LR = 0.001
ADAM_B1 = 0.9
ADAM_B2 = 0.999
ADAM_EPS = 1e-08
ADAM_WD = 0.01
ADAM_STEP = 10

VMEM_LIMIT_V7X = 56 * 1024 * 1024
NEG_BIG = -1e30


def _params(sem=None):
    return pltpu.CompilerParams(dimension_semantics=sem, vmem_limit_bytes=VMEM_LIMIT_V7X)


def _sigmoid(x):
    return 1.0 / (1.0 + jnp.exp(-x))


def _silu_and_grad(x):
    s = _sigmoid(x)
    return x * s, s * (1.0 + x * (1.0 - s))


def _shift_down(v, j):
    return v if j == 0 else pltpu.roll(v, j, 0)


def _shift_up(v, j):
    return v if j == 0 else pltpu.roll(v, v.shape[0] - j, 0)


def _row_iota(shape):
    return lax.broadcasted_iota(jnp.int32, shape, 0)


def _lane_iota(shape):
    return lax.broadcasted_iota(jnp.int32, shape, 1)


def _dot(a, b, mode="nn"):
    dims = {"nn": (((1,), (0,)), ((), ())), "nt": (((1,), (1,)), ((), ())), "tn": (((0,), (0,)), ((), ()))}[mode]
    return lax.dot_general(a.astype(BF16), b.astype(BF16), dims, preferred_element_type=F32)


def _dot_f32(a, b):
    return jnp.dot(a, b, precision=lax.Precision.HIGHEST, preferred_element_type=F32)


def matmul(a, b, *, mode, tm, tn, tk, out_dtype, name, add=None, b_koff=0):
    if mode == "nn":
        (m, k), n = a.shape, b.shape[1]
    elif mode == "nt":
        (m, k), n = a.shape, b.shape[0]
    else:
        (k, m), n = a.shape, b.shape[1]
    assert m % tm == 0 and n % tn == 0 and k % tk == 0, (name, m, n, k, tm, tn, tk)
    nk = k // tk
    a_spec = pl.BlockSpec((tk, tm), lambda i, j, kk: (kk, i)) if mode == "tn" else pl.BlockSpec((tm, tk), lambda i, j, kk: (i, kk))
    b_spec = (pl.BlockSpec((tn, tk), lambda i, j, kk: (j, kk + b_koff)) if mode == "nt"
              else pl.BlockSpec((tk, tn), lambda i, j, kk: (kk + b_koff, j)))
    o_spec = pl.BlockSpec((tm, tn), lambda i, j, kk: (i, j))
    has_add = add is not None

    def body(*refs):
        if has_add:
            a_ref, b_ref, add_ref, o_ref, acc_ref = refs
        else:
            a_ref, b_ref, o_ref, acc_ref = refs
        kk = pl.program_id(2)
        part = _dot(a_ref[...], b_ref[...], mode)

        def finish(total):
            if has_add:
                total = total + add_ref[...]
            o_ref[...] = total.astype(out_dtype)

        if nk == 1:
            finish(part)
        else:
            @pl.when(kk == 0)
            def _():
                acc_ref[...] = part

            @pl.when(jnp.logical_and(kk > 0, kk < nk - 1))
            def _():
                acc_ref[...] += part

            @pl.when(kk == nk - 1)
            def _():
                finish(acc_ref[...] + part)

    in_specs = [a_spec, b_spec] + ([o_spec] if has_add else [])
    args = (a, b) + ((add,) if has_add else ())
    return pl.pallas_call(
        body, name=name, grid=(m // tm, n // tn, nk), in_specs=in_specs, out_specs=o_spec,
        out_shape=jax.ShapeDtypeStruct((m, n), out_dtype),
        scratch_shapes=[pltpu.VMEM((tm, tn) if nk > 1 else (8, 128), F32)],
        compiler_params=_params(("parallel", "parallel", "arbitrary")),
    )(*args)


def rms_fwd(x, w, *, name, tm=512):
    s, d = x.shape

    def body(x_ref, w_ref, h_ref):
        xv = x_ref[...]
        r = lax.rsqrt(jnp.mean(xv * xv, axis=-1, keepdims=True) + NORM_EPS)
        h_ref[...] = ((xv * r) * w_ref[...]).astype(BF16)

    return pl.pallas_call(
        body, name=name, grid=(s // tm,),
        in_specs=[pl.BlockSpec((tm, d), lambda i: (i, 0)), pl.BlockSpec((1, d), lambda i: (0, 0))],
        out_specs=pl.BlockSpec((tm, d), lambda i: (i, 0)),
        out_shape=jax.ShapeDtypeStruct((s, d), BF16), compiler_params=_params(("parallel",)),
    )(x, w)


def rms_bwd(dh, x, w, resid, *, name, tm=512):
    s, d = x.shape

    def body(dh_ref, x_ref, w_ref, res_ref, dx_ref, dw_ref):
        xv = x_ref[...]
        dhv = dh_ref[...]
        r = lax.rsqrt(jnp.mean(xv * xv, axis=-1, keepdims=True) + NORM_EPS)
        xh = xv * r
        g = dhv * w_ref[...]
        dx_ref[...] = res_ref[...] + r * (g - xh * jnp.mean(g * xh, axis=-1, keepdims=True))
        part = jnp.sum(dhv * xh, axis=0, keepdims=True)

        @pl.when(pl.program_id(0) == 0)
        def _():
            dw_ref[...] = part

        @pl.when(pl.program_id(0) > 0)
        def _():
            dw_ref[...] += part

    row = pl.BlockSpec((tm, d), lambda i: (i, 0))
    vec = pl.BlockSpec((1, d), lambda i: (0, 0))
    return pl.pallas_call(
        body, name=name, grid=(s // tm,), in_specs=[row, row, vec, row], out_specs=[row, vec],
        out_shape=[jax.ShapeDtypeStruct((s, d), F32), jax.ShapeDtypeStruct((1, d), F32)],
        compiler_params=_params(("arbitrary",)),
    )(dh, x, w, resid)


def loss_head(y, target, *, tm=512):
    s, d = y.shape

    def body(y_ref, t_ref, dy_ref, sq_ref):
        e = y_ref[...] - t_ref[...]
        dy_ref[...] = e / float(d)
        part = jnp.sum(e * e, axis=0, keepdims=True)

        @pl.when(pl.program_id(0) == 0)
        def _():
            sq_ref[...] = part

        @pl.when(pl.program_id(0) > 0)
        def _():
            sq_ref[...] += part

    row = pl.BlockSpec((tm, d), lambda i: (i, 0))
    vec = pl.BlockSpec((1, d), lambda i: (0, 0))
    return pl.pallas_call(
        body, name="loss_head", grid=(s // tm,), in_specs=[row, row], out_specs=[row, vec],
        out_shape=[jax.ShapeDtypeStruct((s, d), F32), jax.ShapeDtypeStruct((1, d), F32)],
        compiler_params=_params(("arbitrary",)),
    )(y, target)


def _conv_rows(ext, w, k_taps):
    acc = w[k_taps - 1:k_taps, :] * ext
    for k in range(k_taps - 1):
        acc = acc + w[k:k + 1, :] * _shift_down(ext, k_taps - 1 - k)
    return acc


def _conv_rows_transposed(dext, w, k_taps):
    acc = w[k_taps - 1:k_taps, :] * dext
    for k in range(k_taps - 1):
        acc = acc + w[k:k + 1, :] * _shift_up(dext, k_taps - 1 - k)
    return acc


def _stack_rows(rows, width):
    ri = _row_iota((8, width))
    out = jnp.zeros((8, width), F32)
    for k, r in enumerate(rows):
        out = out + jnp.where(ri == k, r, 0.0)
    return out


def ffn_mid_fwd(hu, conv_w8, conv_b, *, tm=512, tc=256):
    s = hu.shape[0]
    ncol = D_FF // tc
    r8 = tm // 8

    def body(g_ref, v_ref, gp_ref, vp_ref, wg_ref, wv_ref, bg_ref, bv_ref, o_ref):
        first = pl.program_id(1) == 0

        def conv(cur_ref, prev_ref, w_ref, b_ref):
            prev = jnp.where(first, 0.0, prev_ref[...])
            ext = jnp.concatenate([prev, cur_ref[...]], axis=0)
            return _conv_rows(ext, w_ref[...], FFN_CONV)[8:] + b_ref[...]

        gc = conv(g_ref, gp_ref, wg_ref, bg_ref)
        vc = conv(v_ref, vp_ref, wv_ref, bv_ref)
        o_ref[...] = (gc * _sigmoid(gc) * vc).astype(BF16)

    def prev_idx(i):
        return jnp.maximum(i * r8 - 1, 0)

    in_specs = [
        pl.BlockSpec((tm, tc), lambda j, i: (i, j)),
        pl.BlockSpec((tm, tc), lambda j, i: (i, j + ncol)),
        pl.BlockSpec((8, tc), lambda j, i: (prev_idx(i), j)),
        pl.BlockSpec((8, tc), lambda j, i: (prev_idx(i), j + ncol)),
        pl.BlockSpec((8, tc), lambda j, i: (0, j)),
        pl.BlockSpec((8, tc), lambda j, i: (0, j + ncol)),
        pl.BlockSpec((1, tc), lambda j, i: (0, j)),
        pl.BlockSpec((1, tc), lambda j, i: (0, j + ncol)),
    ]
    return pl.pallas_call(
        body, name="ffn_mid_fwd", grid=(ncol, s // tm), in_specs=in_specs,
        out_specs=pl.BlockSpec((tm, tc), lambda j, i: (i, j)),
        out_shape=jax.ShapeDtypeStruct((s, D_FF), BF16), compiler_params=_params(("parallel", "parallel")),
    )(hu, hu, hu, hu, conv_w8, conv_w8, conv_b, conv_b)


def ffn_mid_bwd(hu, dact, conv_w8, conv_b, *, tm=512, tc=256):
    s = hu.shape[0]
    ncol = D_FF // tc
    nrow = s // tm
    r8 = tm // 8

    def body(g_ref, v_ref, gp_ref, vp_ref, gn_ref, vn_ref, da_ref, dan_ref, wg_ref, wv_ref, bg_ref, bv_ref,
             dg_ref, dv_ref, wgo_ref, wvo_ref):
        i = pl.program_id(1)
        first = i == 0
        last = i == nrow - 1

        def ext_of(cur_ref, prev_ref, next_ref):
            prev = jnp.where(first, 0.0, prev_ref[...])
            return jnp.concatenate([prev, cur_ref[...], next_ref[...]], axis=0)

        g_ext = ext_of(g_ref, gp_ref, gn_ref)
        v_ext = ext_of(v_ref, vp_ref, vn_ref)
        gc = _conv_rows(g_ext, wg_ref[...], FFN_CONV) + bg_ref[...]
        vc = _conv_rows(v_ext, wv_ref[...], FFN_CONV) + bv_ref[...]
        da_ext = jnp.concatenate([jnp.zeros((8, tc), F32), da_ref[...], jnp.where(last, 0.0, dan_ref[...])], axis=0)
        silu, dsilu = _silu_and_grad(gc)
        dgc = da_ext * vc * dsilu
        dvc = da_ext * silu
        dg_ref[...] = _conv_rows_transposed(dgc, wg_ref[...], FFN_CONV)[8:8 + tm].astype(BF16)
        dv_ref[...] = _conv_rows_transposed(dvc, wv_ref[...], FFN_CONV)[8:8 + tm].astype(BF16)

        def wgrad(dcur, x_ext):
            rows = [jnp.sum(dcur * _shift_down(x_ext, FFN_CONV - 1 - k)[8:8 + tm], axis=0, keepdims=True)
                    for k in range(FFN_CONV)]
            rows.append(jnp.sum(dcur, axis=0, keepdims=True))
            return _stack_rows(rows, tc)

        pg = wgrad(dgc[8:8 + tm], g_ext)
        pv = wgrad(dvc[8:8 + tm], v_ext)

        @pl.when(first)
        def _():
            wgo_ref[...] = pg
            wvo_ref[...] = pv

        @pl.when(i > 0)
        def _():
            wgo_ref[...] += pg
            wvo_ref[...] += pv

    def prev_idx(i):
        return jnp.maximum(i * r8 - 1, 0)

    def next_idx(i):
        return jnp.minimum((i + 1) * r8, s // 8 - 1)

    cur_g = pl.BlockSpec((tm, tc), lambda j, i: (i, j))
    cur_v = pl.BlockSpec((tm, tc), lambda j, i: (i, j + ncol))
    in_specs = [
        cur_g, cur_v,
        pl.BlockSpec((8, tc), lambda j, i: (prev_idx(i), j)),
        pl.BlockSpec((8, tc), lambda j, i: (prev_idx(i), j + ncol)),
        pl.BlockSpec((8, tc), lambda j, i: (next_idx(i), j)),
        pl.BlockSpec((8, tc), lambda j, i: (next_idx(i), j + ncol)),
        cur_g,
        pl.BlockSpec((8, tc), lambda j, i: (next_idx(i), j)),
        pl.BlockSpec((8, tc), lambda j, i: (0, j)),
        pl.BlockSpec((8, tc), lambda j, i: (0, j + ncol)),
        pl.BlockSpec((1, tc), lambda j, i: (0, j)),
        pl.BlockSpec((1, tc), lambda j, i: (0, j + ncol)),
    ]
    out_specs = [cur_g, cur_g, pl.BlockSpec((8, tc), lambda j, i: (0, j)), pl.BlockSpec((8, tc), lambda j, i: (0, j))]
    out_shape = [jax.ShapeDtypeStruct((s, D_FF), BF16), jax.ShapeDtypeStruct((s, D_FF), BF16),
                 jax.ShapeDtypeStruct((8, D_FF), F32), jax.ShapeDtypeStruct((8, D_FF), F32)]
    return pl.pallas_call(
        body, name="ffn_mid_bwd", grid=(ncol, nrow), in_specs=in_specs, out_specs=out_specs, out_shape=out_shape,
        compiler_params=_params(("parallel", "arbitrary")),
    )(hu, hu, hu, hu, hu, hu, dact, dact, conv_w8, conv_w8, conv_b, conv_b)


def _softplus(x):
    return jnp.maximum(x, 0.0) + jnp.log(1.0 + jnp.exp(-jnp.abs(x)))


def _cumsum_rows(v):
    n = v.shape[0]
    ri = _row_iota(v.shape)
    sh = 1
    while sh < n:
        v = v + jnp.where(ri >= sh, _shift_down(v, sh), 0.0)
        sh *= 2
    return v


def _rev_cumsum_rows(v):
    n = v.shape[0]
    ri = _row_iota(v.shape)
    sh = 1
    while sh < n:
        v = v + jnp.where(ri < n - sh, _shift_up(v, sh), 0.0)
        sh *= 2
    return v


def _half_row_sums(v, lo):
    s0 = jnp.sum(jnp.where(lo, v, 0.0), axis=1, keepdims=True)
    return s0, jnp.sum(v, axis=1, keepdims=True) - s0


def _total(v):
    return jnp.sum(jnp.sum(v, axis=1, keepdims=True), axis=0, keepdims=True)


def _ssd_in_specs(rev_nc=None):
    def ch(c):
        return c if rev_nc is None else rev_nc - 1 - c

    def prev(c):
        return jnp.maximum(ch(c) * (SSD_CHUNK // 8) - 1, 0)

    L = SSD_CHUNK
    return [
        pl.BlockSpec((L, 1024), lambda c: (ch(c), 0)),
        pl.BlockSpec((L, 1024), lambda c: (ch(c), 1)),
        pl.BlockSpec((L, 256), lambda c: (ch(c), 20)),
        pl.BlockSpec((L, 256), lambda c: (ch(c), 21)),
        pl.BlockSpec((8, 1024), lambda c: (prev(c), 1)),
        pl.BlockSpec((8, 256), lambda c: (prev(c), 20)),
        pl.BlockSpec((8, 256), lambda c: (prev(c), 21)),
        pl.BlockSpec((8, 1024), lambda c: (0, 0)),
        pl.BlockSpec((8, 256), lambda c: (0, 4)),
        pl.BlockSpec((8, 256), lambda c: (0, 5)),
        pl.BlockSpec((1, 1024), lambda c: (0, 0)),
        pl.BlockSpec((1, 256), lambda c: (0, 4)),
        pl.BlockSpec((1, 256), lambda c: (0, 5)),
        pl.BlockSpec((L, SMALL_COLS), lambda c: (ch(c), 0)),
        pl.BlockSpec((8, 128), lambda c: (0, 0)),
        pl.BlockSpec((1, 1024), lambda c: (0, 0)),
    ]


def _ssd_conv_pre(cur_ref, prev_ref, w_ref, b_ref, first):
    prev = jnp.where(first, 0.0, prev_ref[...])
    ext = jnp.concatenate([prev, cur_ref[...]], axis=0)
    return ext, _conv_rows(ext, w_ref[...], SSD_CONV)[8:] + b_ref[...]


def _ssd_time_consts(small_ref, sp_ref):
    dt_pre = small_ref[...] + sp_ref[0:1, :]
    dt = _softplus(dt_pre)
    a = -jnp.exp(sp_ref[1:2, :])
    acs = _cumsum_rows(dt * a)
    return dt_pre, dt, a, acs


def ssd_fwd(proj, small, conv_w8, conv_b, smallp, norm_w):
    s = proj.shape[0]
    nc = s // SSD_CHUNK
    L = SSD_CHUNK

    def body(z_ref, xs_ref, b_ref, c_ref, xsp_ref, bp_ref, cp_ref, wx_ref, wb_ref, wc_ref, bx_ref, bb_ref, bc_ref,
             small_ref, sp_ref, nw_ref, y_ref, ypre_ref, st_ref, state):
        first = pl.program_id(0) == 0

        @pl.when(first)
        def _():
            state[...] = jnp.zeros_like(state)

        xs = _ssd_conv_pre(xs_ref, xsp_ref, wx_ref, bx_ref, first)[1]
        xs = xs * _sigmoid(xs)
        bm = _ssd_conv_pre(b_ref, bp_ref, wb_ref, bb_ref, first)[1]
        bm = bm * _sigmoid(bm)
        cm = _ssd_conv_pre(c_ref, cp_ref, wc_ref, bc_ref, first)[1]
        cm = cm * _sigmoid(cm)
        _, dt, _, acs = _ssd_time_consts(small_ref, sp_ref)
        acs_t = acs.T
        li = _lane_iota((L, L))
        ri = _row_iota((L, L))
        tri = ri >= li
        lo = li < HEAD_DIM
        st_ref[0] = state[...]
        for g in range(2):
            bg = bm[:, 128 * g:128 * g + 128]
            cg = cm[:, 128 * g:128 * g + 128]
            gmat = _dot(cg, bg, "nt")
            for pp in range(4):
                p = 4 * g + pp
                h0, h1 = 2 * p, 2 * p + 1
                x = xs[:, 128 * p:128 * p + 128]
                a0, a1 = acs[:, h0:h0 + 1], acs[:, h1:h1 + 1]
                xdt = x * jnp.where(lo, dt[:, h0:h0 + 1], dt[:, h1:h1 + 1])
                m0 = gmat * jnp.exp(jnp.where(tri, a0 - acs_t[h0:h0 + 1, :], NEG_BIG))
                m1 = gmat * jnp.exp(jnp.where(tri, a1 - acs_t[h1:h1 + 1, :], NEG_BIG))
                yd = _dot(m0, jnp.where(lo, xdt, 0.0)) + _dot(m1, jnp.where(lo, 0.0, xdt))
                hin = state[p]
                yo = _dot(cg, hin, "nt") * jnp.exp(jnp.where(lo, a0, a1))
                dskip = jnp.where(lo[0:1], sp_ref[2:3, h0:h0 + 1], sp_ref[2:3, h1:h1 + 1])
                ypre_ref[:, 128 * p:128 * p + 128] = yd + yo + dskip * x
                al0, al1 = acs[L - 1:L, h0:h0 + 1], acs[L - 1:L, h1:h1 + 1]
                w = jnp.exp(jnp.where(lo, al0 - a0, al1 - a1))
                dec = jnp.exp(jnp.where(ri < HEAD_DIM, al0, al1))
                state[p] = dec * hin + _dot(xdt * w, bg, "tn")
        z = z_ref[...]
        yg = ypre_ref[...] * (z * _sigmoid(z))
        for g in range(2):
            seg = yg[:, 512 * g:512 * g + 512]
            r = lax.rsqrt(jnp.mean(seg * seg, axis=-1, keepdims=True) + NORM_EPS)
            y_ref[:, 512 * g:512 * g + 512] = ((seg * r) * nw_ref[:, 512 * g:512 * g + 512]).astype(BF16)

    row = pl.BlockSpec((L, 1024), lambda c: (c, 0))
    return pl.pallas_call(
        body, name="ssd_fwd", grid=(nc,), in_specs=_ssd_in_specs(),
        out_specs=[row, row, pl.BlockSpec((1, N_PAIRS, 128, 128), lambda c: (c, 0, 0, 0))],
        out_shape=[jax.ShapeDtypeStruct((s, 1024), BF16), jax.ShapeDtypeStruct((s, 1024), F32),
                   jax.ShapeDtypeStruct((nc, N_PAIRS, 128, 128), F32)],
        scratch_shapes=[pltpu.VMEM((N_PAIRS, 128, 128), F32)],
        compiler_params=_params(("arbitrary",)),
    )(proj, proj, proj, proj, proj, proj, proj, conv_w8, conv_w8, conv_w8, conv_b, conv_b, conv_b, small, smallp, norm_w)


def ssd_bwd(proj, small, conv_w8, conv_b, smallp, norm_w, ypre, states, dy):
    s = proj.shape[0]
    nc = s // SSD_CHUNK
    L = SSD_CHUNK

    def body(z_ref, xs_ref, b_ref, c_ref, xsp_ref, bp_ref, cp_ref, wx_ref, wb_ref, wc_ref, bx_ref, bb_ref, bc_ref,
             small_ref, sp_ref, nw_ref, ypre_ref, st_ref, dy_ref,
             dz_ref, dxs_ref, db_ref, dc_ref, dsmall_ref, gwx_ref, gwb_ref, gwc_ref, gsp_ref, gnw_ref,
             dstate, carry_x, carry_b, carry_c, dxs_buf, dbm_buf, dcm_buf):
        step = pl.program_id(0)
        first_chunk = step == nc - 1
        start = step == 0

        @pl.when(start)
        def _():
            dstate[...] = jnp.zeros_like(dstate)
            carry_x[...] = jnp.zeros_like(carry_x)
            carry_b[...] = jnp.zeros_like(carry_b)
            carry_c[...] = jnp.zeros_like(carry_c)

        xs_ext, xs_pre = _ssd_conv_pre(xs_ref, xsp_ref, wx_ref, bx_ref, first_chunk)
        b_ext, b_pre = _ssd_conv_pre(b_ref, bp_ref, wb_ref, bb_ref, first_chunk)
        c_ext, c_pre = _ssd_conv_pre(c_ref, cp_ref, wc_ref, bc_ref, first_chunk)
        xs, xs_ds = _silu_and_grad(xs_pre)
        bm, b_ds = _silu_and_grad(b_pre)
        cm, c_ds = _silu_and_grad(c_pre)
        dt_pre, dt, a, acs = _ssd_time_consts(small_ref, sp_ref)
        acs_t = acs.T
        li = _lane_iota((L, L))
        ri = _row_iota((L, L))
        tri = ri >= li
        lo = li < HEAD_DIM
        lo_rows = ri < HEAD_DIM
        li1 = _lane_iota((1, L))

        z = z_ref[...]
        sz, dsz = _silu_and_grad(z)
        y = ypre_ref[...]
        yg = y * sz
        dout = dy_ref[...]
        dyg_parts = []
        gnw_parts = []
        for g in range(2):
            sl = slice(512 * g, 512 * g + 512)
            seg = yg[:, sl]
            r = lax.rsqrt(jnp.mean(seg * seg, axis=-1, keepdims=True) + NORM_EPS)
            n = seg * r
            gnw_parts.append(jnp.sum(dout[:, sl] * n, axis=0, keepdims=True))
            gg = dout[:, sl] * nw_ref[:, sl]
            dyg_parts.append(r * (gg - n * jnp.mean(gg * n, axis=-1, keepdims=True)))
        dyg = jnp.concatenate(dyg_parts, axis=1)
        gnw = jnp.concatenate(gnw_parts, axis=1)
        dz_ref[...] = (dyg * y * dsz).astype(BF16)
        dypre = dyg * sz

        ddt = jnp.zeros((L, L), F32)
        dacs = jnp.zeros((L, L), F32)
        dacs_t = jnp.zeros((L, L), F32)
        dalast = jnp.zeros((1, L), F32)
        dskip_g = jnp.zeros((1, L), F32)
        for g in range(2):
            bg = bm[:, 128 * g:128 * g + 128]
            cg = cm[:, 128 * g:128 * g + 128]
            gmat = _dot(cg, bg, "nt")
            dgmat = jnp.zeros((L, L), F32)
            dbg = jnp.zeros((L, L), F32)
            dcg = jnp.zeros((L, L), F32)
            for pp in range(4):
                p = 4 * g + pp
                h0, h1 = 2 * p, 2 * p + 1
                x = xs[:, 128 * p:128 * p + 128]
                dyp = dypre[:, 128 * p:128 * p + 128]
                a0, a1 = acs[:, h0:h0 + 1], acs[:, h1:h1 + 1]
                dtl = jnp.where(lo, dt[:, h0:h0 + 1], dt[:, h1:h1 + 1])
                xdt = x * dtl
                l0 = jnp.exp(jnp.where(tri, a0 - acs_t[h0:h0 + 1, :], NEG_BIG))
                l1 = jnp.exp(jnp.where(tri, a1 - acs_t[h1:h1 + 1, :], NEG_BIG))
                m0, m1 = gmat * l0, gmat * l1
                dskip = jnp.where(lo[0:1], sp_ref[2:3, h0:h0 + 1], sp_ref[2:3, h1:h1 + 1])
                s0, s1 = _half_row_sums(dyp * x, lo)
                dskip_g = dskip_g + jnp.where(li1 == h0, _total(s0), 0.0) + jnp.where(li1 == h1, _total(s1), 0.0)
                dx = dyp * dskip
                dy0, dy1 = jnp.where(lo, dyp, 0.0), jnp.where(lo, 0.0, dyp)
                x0, x1 = jnp.where(lo, xdt, 0.0), jnp.where(lo, 0.0, xdt)
                dm0, dm1 = _dot(dy0, x0, "nt"), _dot(dy1, x1, "nt")
                dxdt = _dot(m0, dy0, "tn") + _dot(m1, dy1, "tn")
                q0, q1 = dm0 * m0, dm1 * m1
                dacs = dacs + jnp.where(li == h0, jnp.sum(q0, axis=1, keepdims=True), 0.0) \
                            + jnp.where(li == h1, jnp.sum(q1, axis=1, keepdims=True), 0.0)
                dacs_t = dacs_t - jnp.where(ri == h0, jnp.sum(q0, axis=0, keepdims=True), 0.0) \
                                - jnp.where(ri == h1, jnp.sum(q1, axis=0, keepdims=True), 0.0)
                dgmat = dgmat + dm0 * l0 + dm1 * l1
                hin = st_ref[0, p]
                e = jnp.exp(jnp.where(lo, a0, a1))
                ch = _dot(cg, hin, "nt")
                dch = dyp * e
                dcg = dcg + _dot(dch, hin)
                dhin = _dot(dch, cg, "tn")
                s0, s1 = _half_row_sums(dch * ch, lo)
                dacs = dacs + jnp.where(li == h0, s0, 0.0) + jnp.where(li == h1, s1, 0.0)
                dhout = dstate[p]
                al0, al1 = acs[L - 1:L, h0:h0 + 1], acs[L - 1:L, h1:h1 + 1]
                dec = jnp.exp(jnp.where(lo_rows, al0, al1))
                dhin = dhin + dec * dhout
                dal = dhout * hin * dec
                dal0 = _total(jnp.where(lo_rows, dal, 0.0))
                dal1 = _total(dal) - dal0
                w = jnp.exp(jnp.where(lo, al0 - a0, al1 - a1))
                xw = xdt * w
                dxw = _dot(bg, dhout, "nt")
                dbg = dbg + _dot(xw, dhout)
                dxdt = dxdt + dxw * w
                s0, s1 = _half_row_sums(dxw * xw, lo)
                dacs = dacs - jnp.where(li == h0, s0, 0.0) - jnp.where(li == h1, s1, 0.0)
                dal0, dal1 = dal0 + _total(s0), dal1 + _total(s1)
                dalast = dalast + jnp.where(li1 == h0, dal0, 0.0) + jnp.where(li1 == h1, dal1, 0.0)
                dx = dx + dxdt * dtl
                s0, s1 = _half_row_sums(dxdt * x, lo)
                ddt = ddt + jnp.where(li == h0, s0, 0.0) + jnp.where(li == h1, s1, 0.0)
                dxs_buf[:, 128 * p:128 * p + 128] = dx
                dstate[p] = dhin
            dcg = dcg + _dot(dgmat, bg)
            dbg = dbg + _dot(dgmat, cg, "tn")
            dbm_buf[:, 128 * g:128 * g + 128] = dbg
            dcm_buf[:, 128 * g:128 * g + 128] = dcg

        dacs_tot = dacs + dacs_t.T + jnp.where(ri == L - 1, dalast, 0.0)
        dstep = _rev_cumsum_rows(dacs_tot)
        ddt = ddt + dstep * a
        head_lane = li < N_HEADS
        ddt_pre = jnp.where(head_lane, ddt * _sigmoid(dt_pre), 0.0)
        dsmall_ref[...] = ddt_pre
        da = jnp.sum(jnp.where(head_lane, dstep * dt, 0.0), axis=0, keepdims=True)
        gsp = _stack_rows([jnp.sum(ddt_pre, axis=0, keepdims=True), da * a, dskip_g], L)

        def conv_back(dpost, ds, ext, w_ref, carry, out_ref, width):
            dpre = dpost * ds
            dext = jnp.concatenate([dpre, carry[...]], axis=0)
            out_ref[...] = _conv_rows_transposed(dext, w_ref[...], SSD_CONV)[:L].astype(BF16)
            carry[...] = dpre[0:8]
            rows = [jnp.sum(dpre * _shift_down(ext, SSD_CONV - 1 - k)[8:], axis=0, keepdims=True) for k in range(SSD_CONV)]
            rows.append(jnp.sum(dpre, axis=0, keepdims=True))
            return _stack_rows(rows, width)

        gwx = conv_back(dxs_buf[...], xs_ds, xs_ext, wx_ref, carry_x, dxs_ref, 1024)
        gwb = conv_back(dbm_buf[...], b_ds, b_ext, wb_ref, carry_b, db_ref, 256)
        gwc = conv_back(dcm_buf[...], c_ds, c_ext, wc_ref, carry_c, dc_ref, 256)

        @pl.when(start)
        def _():
            gwx_ref[...] = gwx
            gwb_ref[...] = gwb
            gwc_ref[...] = gwc
            gsp_ref[...] = gsp
            gnw_ref[...] = gnw

        @pl.when(step > 0)
        def _():
            gwx_ref[...] += gwx
            gwb_ref[...] += gwb
            gwc_ref[...] += gwc
            gsp_ref[...] += gsp
            gnw_ref[...] += gnw

    def ch(c):
        return nc - 1 - c

    row = pl.BlockSpec((L, 1024), lambda c: (ch(c), 0))
    row256 = pl.BlockSpec((L, 256), lambda c: (ch(c), 0))
    in_specs = _ssd_in_specs(rev_nc=nc) + [row, pl.BlockSpec((1, N_PAIRS, 128, 128), lambda c: (ch(c), 0, 0, 0)), row]
    out_specs = [row, row, row256, row256, pl.BlockSpec((L, 128), lambda c: (ch(c), 0)),
                 pl.BlockSpec((8, 1024), lambda c: (0, 0)), pl.BlockSpec((8, 256), lambda c: (0, 0)),
                 pl.BlockSpec((8, 256), lambda c: (0, 0)), pl.BlockSpec((8, 128), lambda c: (0, 0)),
                 pl.BlockSpec((1, 1024), lambda c: (0, 0))]
    out_shape = [jax.ShapeDtypeStruct((s, 1024), BF16), jax.ShapeDtypeStruct((s, 1024), BF16),
                 jax.ShapeDtypeStruct((s, 256), BF16), jax.ShapeDtypeStruct((s, 256), BF16),
                 jax.ShapeDtypeStruct((s, 128), F32),
                 jax.ShapeDtypeStruct((8, 1024), F32), jax.ShapeDtypeStruct((8, 256), F32),
                 jax.ShapeDtypeStruct((8, 256), F32), jax.ShapeDtypeStruct((8, 128), F32),
                 jax.ShapeDtypeStruct((1, 1024), F32)]
    scratch = [pltpu.VMEM((N_PAIRS, 128, 128), F32), pltpu.VMEM((8, 1024), F32), pltpu.VMEM((8, 256), F32),
               pltpu.VMEM((8, 256), F32), pltpu.VMEM((L, 1024), F32), pltpu.VMEM((L, 256), F32), pltpu.VMEM((L, 256), F32)]
    return pl.pallas_call(
        body, name="ssd_bwd", grid=(nc,), in_specs=in_specs, out_specs=out_specs, out_shape=out_shape,
        scratch_shapes=scratch, compiler_params=_params(("arbitrary",)),
    )(proj, proj, proj, proj, proj, proj, proj, conv_w8, conv_w8, conv_w8, conv_b, conv_b, conv_b, small, smallp, norm_w,
      ypre, states, dy)


FOX_SCALE = HEAD_DIM ** -0.5
FOX_TQ = 256
FOX_TK = 256
Q_COL, K_COL, V_COL = 2, 3, 4


def _split3_dot(v, m):
    hi = v.astype(BF16)
    r1 = v - hi.astype(F32)
    mid = r1.astype(BF16)
    lo = (r1 - mid.astype(F32)).astype(BF16)
    return _dot(hi, m) + _dot(mid, m) + _dot(lo, m)


def _head_rstd(x, sel_ref, selt_ref):
    ms = _split3_dot(x * x, sel_ref[...]) * (1.0 / HEAD_DIM)
    return _split3_dot(lax.rsqrt(ms + NORM_EPS), selt_ref[...])


def fox_prep(proj, small, smallp, qw, kw, sel, selt, *, tm=256):
    s = proj.shape[0]

    def body(q_ref, k_ref, small_ref, sp_ref, qw_ref, kw_ref, sel_ref, selt_ref, qn_ref, kn_ref, cum_ref, cumt_ref, carry):
        @pl.when(pl.program_id(0) == 0)
        def _():
            carry[...] = jnp.zeros_like(carry)

        q = q_ref[...]
        qn_ref[...] = (((q * _head_rstd(q, sel_ref, selt_ref)) * qw_ref[...]) * FOX_SCALE).astype(BF16)
        k = k_ref[...]
        kn_ref[...] = ((k * _head_rstd(k, sel_ref, selt_ref)) * kw_ref[...]).astype(BF16)
        li = _lane_iota((tm, 128))
        f_lane = jnp.logical_and(li >= F_LANE, li < F_LANE + N_HEADS)
        logf = jnp.where(f_lane, -_softplus(-(small_ref[...] + sp_ref[3:4, :])), 0.0)
        cum = _cumsum_rows(logf) + carry[...]
        carry[...] = cum[tm - 1:tm, :]
        cum_ref[...] = cum
        cumt_ref[...] = cum.T

    row = pl.BlockSpec((tm, 1024), lambda i: (i, 0))
    vec = pl.BlockSpec((1, 1024), lambda i: (0, 0))
    return pl.pallas_call(
        body, name="fox_prep", grid=(s // tm,),
        in_specs=[pl.BlockSpec((tm, 1024), lambda i: (i, Q_COL)), pl.BlockSpec((tm, 1024), lambda i: (i, K_COL)),
                  pl.BlockSpec((tm, 128), lambda i: (i, 0)), pl.BlockSpec((8, 128), lambda i: (0, 0)), vec, vec,
                  pl.BlockSpec((1024, 128), lambda i: (0, 0)), pl.BlockSpec((128, 1024), lambda i: (0, 0))],
        out_specs=[row, row, pl.BlockSpec((tm, 128), lambda i: (i, 0)), pl.BlockSpec((128, tm), lambda i: (0, i))],
        out_shape=[jax.ShapeDtypeStruct((s, 1024), BF16), jax.ShapeDtypeStruct((s, 1024), BF16),
                   jax.ShapeDtypeStruct((s, 128), F32), jax.ShapeDtypeStruct((128, s), F32)],
        scratch_shapes=[pltpu.VMEM((1, 128), F32)], compiler_params=_params(("arbitrary",)),
    )(proj, proj, small, smallp, qw, kw, sel, selt)


def _pick_lane(block, lane):
    return jnp.sum(jnp.where(_lane_iota(block.shape) == lane, block, 0.0), axis=1, keepdims=True)


def _pick_row(block, row):
    return jnp.sum(jnp.where(_row_iota(block.shape) == row, block, 0.0), axis=0, keepdims=True)


def _causal_scores(qb, kb, cq, ck, q0, k0):
    sc = _dot(qb, kb, "nt") + cq - ck
    visible = (q0 + _row_iota(sc.shape)) >= (k0 + _lane_iota(sc.shape))
    return jnp.where(visible, sc, NEG_BIG)


def fox_fwd(qn, kn, proj, cum, cumt):
    s = qn.shape[0]
    tq, tk = FOX_TQ, FOX_TK
    nq = s // tq

    def body(q_ref, k_ref, v_ref, cum_ref, cumt_ref, o_ref, lse_ref, crow):
        p = pl.program_id(0)

        @pl.when(p == 0)
        def _():
            lse_ref[...] = jnp.zeros_like(lse_ref)

        lo = _lane_iota((tq, 128)) < HEAD_DIM
        for hh in range(2):
            h = 2 * p + hh
            mine = lo if hh == 0 else jnp.logical_not(lo)
            crow[hh:hh + 1, :] = _pick_row(cumt_ref[...], F_LANE + h)

            def q_loop(qi, _, hh=hh, h=h, mine=mine):
                q0 = pl.multiple_of(qi * tq, tq)
                qb = jnp.where(mine, q_ref[pl.ds(q0, tq), :], 0.0)
                cq = _pick_lane(cum_ref[pl.ds(q0, tq), :], F_LANE + h)

                def k_loop(kj, carry):
                    m, l, acc = carry
                    k0 = pl.multiple_of(kj * tk, tk)
                    sc = _causal_scores(qb, k_ref[pl.ds(k0, tk), :], cq, crow[hh:hh + 1, pl.ds(k0, tk)], q0, k0)
                    m_new = jnp.maximum(m, jnp.max(sc, axis=1, keepdims=True))
                    alpha = jnp.exp(m - m_new)
                    pe = jnp.exp(sc - m_new)
                    l = alpha * l + jnp.sum(pe, axis=1, keepdims=True)
                    acc = alpha * acc + _dot(pe, v_ref[pl.ds(k0, tk), :])
                    return m_new, l, acc

                init = (jnp.full((tq, 1), NEG_BIG, F32), jnp.zeros((tq, 1), F32), jnp.zeros((tq, 128), F32))
                m, l, acc = lax.fori_loop(0, qi + 1, k_loop, init)
                out = (acc / l).astype(BF16)
                if hh == 0:
                    o_ref[pl.ds(q0, tq), :] = out
                else:
                    o_ref[pl.ds(q0, tq), :] = jnp.where(lo, o_ref[pl.ds(q0, tq), :], out)
                old = lse_ref[pl.ds(q0, tq), :]
                lse_ref[pl.ds(q0, tq), :] = jnp.where(_lane_iota((tq, 128)) == h, m + jnp.log(l), old)
                return 0

            lax.fori_loop(0, nq, q_loop, 0)

    pair = lambda col0: pl.BlockSpec((s, 128), lambda p: (0, col0 + p))
    full = pl.BlockSpec((s, 128), lambda p: (0, 0))
    return pl.pallas_call(
        body, name="fox_fwd", grid=(N_PAIRS,),
        in_specs=[pair(0), pair(0), pair(V_COL * 8), full, pl.BlockSpec((128, s), lambda p: (0, 0))],
        out_specs=[pair(0), full],
        out_shape=[jax.ShapeDtypeStruct((s, 1024), BF16), jax.ShapeDtypeStruct((s, 128), F32)],
        scratch_shapes=[pltpu.VMEM((8, s), F32)], compiler_params=_params(("arbitrary",)),
    )(qn, kn, proj, cum, cumt)


def fox_bwd(qn, kn, proj, cum, cumt, lse, dmixed):
    s = qn.shape[0]
    tq, tk = FOX_TQ, FOX_TK
    nq = s // tq

    def body(q_ref, k_ref, v_ref, cum_ref, cumt_ref, lse_ref, do_ref, dq_ref, dk_ref, dv_ref, dcumt_ref,
             crow, dcs, pm_scr, dp_scr):
        p = pl.program_id(0)

        @pl.when(p == 0)
        def _():
            dcumt_ref[...] = jnp.zeros_like(dcumt_ref)

        dk_ref[...] = jnp.zeros_like(dk_ref)
        dv_ref[...] = jnp.zeros_like(dv_ref)
        dcs[...] = jnp.zeros_like(dcs)
        lo = _lane_iota((tq, 128)) < HEAD_DIM
        for hh in range(2):
            h = 2 * p + hh
            mine = lo if hh == 0 else jnp.logical_not(lo)
            crow[hh:hh + 1, :] = _pick_row(cumt_ref[...], F_LANE + h)

            def q_loop(qi, _, hh=hh, h=h, mine=mine):
                q0 = pl.multiple_of(qi * tq, tq)
                qb = jnp.where(mine, q_ref[pl.ds(q0, tq), :], 0.0)
                cq = _pick_lane(cum_ref[pl.ds(q0, tq), :], F_LANE + h)
                lse_q = _pick_lane(lse_ref[pl.ds(q0, tq), :], h)
                dob = jnp.where(mine, do_ref[pl.ds(q0, tq), :], 0.0).astype(BF16)

                def pass1(kj, delta):
                    k0 = pl.multiple_of(kj * tk, tk)
                    kb = jnp.where(mine, k_ref[pl.ds(k0, tk), :], 0.0)
                    vb = jnp.where(mine, v_ref[pl.ds(k0, tk), :], 0.0)
                    pm = jnp.exp(_causal_scores(qb, kb, cq, crow[hh:hh + 1, pl.ds(k0, tk)], q0, k0) - lse_q)
                    dp = _dot(dob, vb, "nt")
                    pm_scr[kj] = pm
                    dp_scr[kj] = dp
                    return delta + jnp.sum(pm * dp, axis=1, keepdims=True)

                delta = lax.fori_loop(0, qi + 1, pass1, jnp.zeros((tq, 1), F32))

                def pass2(kj, dq):
                    k0 = pl.multiple_of(kj * tk, tk)
                    kb = jnp.where(mine, k_ref[pl.ds(k0, tk), :], 0.0)
                    pm = pm_scr[kj]
                    ds = pm * (dp_scr[kj] - delta)
                    dk_ref[pl.ds(k0, tk), :] += _dot(ds, qb, "tn")
                    dv_ref[pl.ds(k0, tk), :] += _dot(pm, dob, "tn")
                    dcs[hh:hh + 1, pl.ds(k0, tk)] += jnp.sum(ds, axis=0, keepdims=True)
                    return dq + _dot(ds, kb)

                dq = lax.fori_loop(0, qi + 1, pass2, jnp.zeros((tq, 128), F32))
                if hh == 0:
                    dq_ref[pl.ds(q0, tq), :] = dq
                else:
                    dq_ref[pl.ds(q0, tq), :] += dq
                return 0

            lax.fori_loop(0, nq, q_loop, 0)
            dcumt_ref[...] = jnp.where(_row_iota((128, s)) == F_LANE + h, -dcs[hh:hh + 1, :], dcumt_ref[...])

    pair = lambda col0: pl.BlockSpec((s, 128), lambda p: (0, col0 + p))
    full = pl.BlockSpec((s, 128), lambda p: (0, 0))
    wide = pl.BlockSpec((128, s), lambda p: (0, 0))
    return pl.pallas_call(
        body, name="fox_bwd", grid=(N_PAIRS,),
        in_specs=[pair(0), pair(0), pair(V_COL * 8), full, wide, full, pair(8)],
        out_specs=[pair(0), pair(0), pair(0), wide],
        out_shape=[jax.ShapeDtypeStruct((s, 1024), F32), jax.ShapeDtypeStruct((s, 1024), F32),
                   jax.ShapeDtypeStruct((s, 1024), F32), jax.ShapeDtypeStruct((128, s), F32)],
        scratch_shapes=[pltpu.VMEM((8, s), F32), pltpu.VMEM((8, s), F32),
                        pltpu.VMEM((nq, tq, tk), F32), pltpu.VMEM((nq, tq, tk), F32)],
        compiler_params=_params(("arbitrary",)),
    )(qn, kn, proj, cum, cumt, lse, dmixed)


def fox_post(dqn, dkn, proj, small, smallp, qw, kw, sel, selt, dcumt, *, tm=256):
    s = proj.shape[0]
    nrow = s // tm

    def body(dqn_ref, dkn_ref, q_ref, k_ref, small_ref, sp_ref, qw_ref, kw_ref, sel_ref, selt_ref, dcumt_ref,
             dq_ref, dk_ref, dsmall_ref, gqw_ref, gkw_ref, gfb_ref, carry):
        step = pl.program_id(0)

        @pl.when(step == 0)
        def _():
            carry[...] = jnp.zeros_like(carry)

        def norm_bwd(x_ref, w_ref, dn, out_ref):
            x = x_ref[...]
            rf = _head_rstd(x, sel_ref, selt_ref)
            xh = x * rf
            g = dn * w_ref[...]
            mean_gx = _split3_dot(_split3_dot(g * xh, sel_ref[...]) * (1.0 / HEAD_DIM), selt_ref[...])
            out_ref[...] = (rf * (g - xh * mean_gx)).astype(BF16)
            return jnp.sum(dn * xh, axis=0, keepdims=True)

        gqw = norm_bwd(q_ref, qw_ref, dqn_ref[...] * FOX_SCALE, dq_ref)
        gkw = norm_bwd(k_ref, kw_ref, dkn_ref[...], dk_ref)
        li = _lane_iota((tm, 128))
        f_lane = jnp.logical_and(li >= F_LANE, li < F_LANE + N_HEADS)
        dlogf = _rev_cumsum_rows(dcumt_ref[...].T) + carry[...]
        carry[...] = dlogf[0:1, :]
        dfr = jnp.where(f_lane, dlogf * _sigmoid(-(small_ref[...] + sp_ref[3:4, :])), 0.0)
        dsmall_ref[...] = dfr
        gfb = jnp.sum(dfr, axis=0, keepdims=True)

        @pl.when(step == 0)
        def _():
            gqw_ref[...] = gqw
            gkw_ref[...] = gkw
            gfb_ref[...] = gfb

        @pl.when(step > 0)
        def _():
            gqw_ref[...] += gqw
            gkw_ref[...] += gkw
            gfb_ref[...] += gfb

    def rb(i):
        return nrow - 1 - i

    row = pl.BlockSpec((tm, 1024), lambda i: (rb(i), 0))
    vec = pl.BlockSpec((1, 1024), lambda i: (0, 0))
    return pl.pallas_call(
        body, name="fox_post", grid=(nrow,),
        in_specs=[row, row, pl.BlockSpec((tm, 1024), lambda i: (rb(i), Q_COL)), pl.BlockSpec((tm, 1024), lambda i: (rb(i), K_COL)),
                  pl.BlockSpec((tm, 128), lambda i: (rb(i), 0)), pl.BlockSpec((8, 128), lambda i: (0, 0)), vec, vec,
                  pl.BlockSpec((1024, 128), lambda i: (0, 0)), pl.BlockSpec((128, 1024), lambda i: (0, 0)),
                  pl.BlockSpec((128, tm), lambda i: (0, rb(i)))],
        out_specs=[row, row, pl.BlockSpec((tm, 128), lambda i: (rb(i), 0)), vec, vec, pl.BlockSpec((1, 128), lambda i: (0, 0))],
        out_shape=[jax.ShapeDtypeStruct((s, 1024), BF16), jax.ShapeDtypeStruct((s, 1024), BF16),
                   jax.ShapeDtypeStruct((s, 128), F32), jax.ShapeDtypeStruct((1, 1024), F32),
                   jax.ShapeDtypeStruct((1, 1024), F32), jax.ShapeDtypeStruct((1, 128), F32)],
        scratch_shapes=[pltpu.VMEM((1, 128), F32)], compiler_params=_params(("arbitrary",)),
    )(dqn, dkn, proj, proj, small, smallp, qw, kw, sel, selt, dcumt)


def local_step(x, target, wm, ws, w_out, w_up, w_down, ssd_cw8, ssd_cb, smallp, ssd_nw, qw_t, kw_t, sel, selt,
               norm_mix_w, norm_ffn_w, ffn_cw8, ffn_cb):
    h = rms_fwd(x, norm_mix_w, name="rms_mix_fwd")
    proj = matmul(h, wm, mode="nn", tm=1024, tn=512, tk=1024, out_dtype=F32, name="mm_in_proj")
    small = matmul(h, ws, mode="nn", tm=1024, tn=128, tk=1024, out_dtype=F32, name="mm_in_proj_small")
    y_ssd, ypre, states = ssd_fwd(proj, small, ssd_cw8, ssd_cb, smallp, ssd_nw)
    qn, kn, cum, cumt = fox_prep(proj, small, smallp, qw_t, kw_t, sel, selt)
    y_fox, lse = fox_fwd(qn, kn, proj, cum, cumt)
    x1 = matmul(y_ssd, w_out, mode="nn", tm=512, tn=1024, tk=1024, out_dtype=F32, name="mm_out_ssd", add=x)
    x1 = matmul(y_fox, w_out, mode="nn", tm=512, tn=1024, tk=1024, out_dtype=F32, name="mm_out_fox", add=x1, b_koff=1)
    hf = rms_fwd(x1, norm_ffn_w, name="rms_ffn_fwd")
    hu = matmul(hf, w_up, mode="nn", tm=1024, tn=512, tk=1024, out_dtype=F32, name="mm_up")
    act = ffn_mid_fwd(hu, ffn_cw8, ffn_cb)
    y = matmul(act, w_down, mode="nn", tm=512, tn=1024, tk=1408, out_dtype=F32, name="mm_down", add=x1)
    dy, sq = loss_head(y, target)

    dact = matmul(dy, w_down, mode="nt", tm=512, tn=1408, tk=1024, out_dtype=F32, name="mm_dact")
    g_down = matmul(act, dy, mode="tn", tm=1408, tn=1024, tk=512, out_dtype=BF16, name="mm_dw_down")
    dhu_g, dhu_v, gcw_g, gcw_v = ffn_mid_bwd(hu, dact, ffn_cw8, ffn_cb)
    dhf = matmul(dhu_g, w_up, mode="nt", tm=512, tn=1024, tk=1408, out_dtype=F32, name="mm_dhf_gate")
    dhf = matmul(dhu_v, w_up, mode="nt", tm=512, tn=1024, tk=1408, out_dtype=F32, name="mm_dhf_val", add=dhf, b_koff=2)
    g_up_g = matmul(hf, dhu_g, mode="tn", tm=1024, tn=1408, tk=512, out_dtype=BF16, name="mm_dw_up_gate")
    g_up_v = matmul(hf, dhu_v, mode="tn", tm=1024, tn=1408, tk=512, out_dtype=BF16, name="mm_dw_up_val")
    dx1, g_norm_ffn = rms_bwd(dhf, x1, norm_ffn_w, dy, name="rms_ffn_bwd")
    dmixed = matmul(dx1, w_out, mode="nt", tm=512, tn=1024, tk=1024, out_dtype=F32, name="mm_dmixed")
    g_out_a = matmul(y_ssd, dx1, mode="tn", tm=1024, tn=1024, tk=512, out_dtype=BF16, name="mm_dw_out_ssd")
    g_out_b = matmul(y_fox, dx1, mode="tn", tm=1024, tn=1024, tk=512, out_dtype=BF16, name="mm_dw_out_fox")
    dz, dxs, db, dc, dsmall_ssd, gcw_x, gcw_b, gcw_c, g_sp, g_ssd_nw = ssd_bwd(
        proj, small, ssd_cw8, ssd_cb, smallp, ssd_nw, ypre, states, dmixed)
    dqn, dkn, dv, dcumt = fox_bwd(qn, kn, proj, cum, cumt, lse, dmixed)
    dq, dk, dsmall_fox, g_qw, g_kw, g_fb = fox_post(dqn, dkn, proj, small, smallp, qw_t, kw_t, sel, selt, dcumt)
    dproj = jnp.concatenate([dz, dxs, dq, dk, dv.astype(BF16), db, dc], axis=1)
    dsmall = (dsmall_ssd + dsmall_fox).astype(BF16)
    dh = matmul(dproj, wm, mode="nt", tm=512, tn=1024, tk=512, out_dtype=F32, name="mm_dh")
    dh = matmul(dsmall, ws, mode="nt", tm=512, tn=1024, tk=128, out_dtype=F32, name="mm_dh_small", add=dh)
    g_wm = matmul(h, dproj, mode="tn", tm=1024, tn=1408, tk=512, out_dtype=BF16, name="mm_dw_in")
    g_ws = matmul(h, dsmall, mode="tn", tm=1024, tn=128, tk=512, out_dtype=BF16, name="mm_dw_in_small")
    grad_x, g_norm_mix = rms_bwd(dh, x, norm_mix_w, dx1, name="rms_mix_bwd")
    return dict(
        sq=sq, grad_x=grad_x, g_wm=g_wm, g_ws=g_ws, g_out=jnp.concatenate([g_out_a, g_out_b], axis=0),
        g_up=jnp.concatenate([g_up_g, g_up_v], axis=1), g_down=g_down,
        g_norm_mix=g_norm_mix, g_norm_ffn=g_norm_ffn, g_ssd_nw=g_ssd_nw,
        g_ssd_cw=jnp.concatenate([gcw_x, gcw_b, gcw_c], axis=1), g_sp=g_sp, g_fb=g_fb, g_qw=g_qw, g_kw=g_kw,
        g_ffn_cw=jnp.concatenate([gcw_g, gcw_v], axis=1))


def adamw(w, g, m, v, *, name, tr):
    rows, cols = w.shape

    def body(w_ref, g_ref, m_ref, v_ref, d_ref, mo_ref, vo_ref):
        gv = g_ref[...]
        mn = ADAM_B1 * m_ref[...] + (1.0 - ADAM_B1) * gv
        vn = ADAM_B2 * v_ref[...] + (1.0 - ADAM_B2) * (gv * gv)
        m_hat = mn / (1.0 - ADAM_B1 ** ADAM_STEP)
        v_hat = vn / (1.0 - ADAM_B2 ** ADAM_STEP)
        d_ref[...] = -ADAM_LR * (m_hat / (jnp.sqrt(v_hat) + ADAM_EPS) + ADAM_WD * w_ref[...])
        mo_ref[...] = mn
        vo_ref[...] = vn

    blk = pl.BlockSpec((tr, cols), lambda i: (i, 0))
    shp = jax.ShapeDtypeStruct((rows, cols), F32)
    return pl.pallas_call(
        body, name=name, grid=(rows // tr,), in_specs=[blk] * 4, out_specs=[blk] * 3, out_shape=[shp] * 3,
        compiler_params=_params(("parallel",)),
    )(w, g, m, v)


def add_pair(a, b, *, name, tr):
    _, rows, cols = a.shape

    def body(a_ref, b_ref, o_ref):
        o_ref[...] = (a_ref[...].astype(F32) + b_ref[...].astype(F32)).astype(BF16)

    blk = pl.BlockSpec((1, tr, cols), lambda j, i: (j, i, 0))
    return pl.pallas_call(
        body, name=name, grid=(4, rows // tr), in_specs=[blk, blk], out_specs=blk,
        out_shape=jax.ShapeDtypeStruct(a.shape, BF16), compiler_params=_params(("parallel", "parallel")),
    )(a, b)


def sum_chips(parts, *, name, tr):
    _, rows, cols = parts.shape

    def body(p_ref, o_ref):
        acc = p_ref[0].astype(F32)
        for k in range(1, 4):
            acc = acc + p_ref[k].astype(F32)
        o_ref[...] = acc

    return pl.pallas_call(
        body, name=name, grid=(rows // tr,), in_specs=[pl.BlockSpec((4, tr, cols), lambda i: (0, i, 0))],
        out_specs=pl.BlockSpec((tr, cols), lambda i: (i, 0)), out_shape=jax.ShapeDtypeStruct((rows, cols), F32),
        compiler_params=_params(("parallel",)),
    )(parts)


ANY = pl.BlockSpec(memory_space=pl.ANY)


def _place():
    x, y, c = lax.axis_index("x"), lax.axis_index("y"), lax.axis_index("c")
    chips = [(1 - x, y), (x, 1 - y), (1 - x, 1 - y)]
    return x, y, c, chips


def _chunks(rows):
    size = next((c for c in (128, 176, 64, 32, 16, 8) if rows % c == 0), rows)
    return [(r, size) for r in range(0, rows, size)]


def gather_weights(shards):
    n = len(shards)

    def body(*refs):
        ins, outs = refs[:n], refs[n:2 * n]
        send_sems, recv_sems, local_sems = refs[2 * n:]
        x, y, c, chips = _place()
        me = 2 * x + y
        sibling = (x, y, 1 - c)

        def half(a, blk, r=0, nr=None):
            rows = ins[a].shape[0] // 2
            return outs[a].at[blk, pl.ds(c * rows + r, rows if nr is None else nr), :]

        def to_chip(a, t, r=0, nr=None):
            rows = ins[a].shape[0] // 2
            return pltpu.make_async_remote_copy(
                src_ref=ins[a].at[pl.ds(c * rows + r, rows if nr is None else nr), :], dst_ref=half(a, me, r, nr),
                send_sem=send_sems.at[a, t], recv_sem=recv_sems.at[a, t],
                device_id=(*chips[t], c), device_id_type=MESH)

        def from_chip(a, t):
            blk = 2 * chips[t][0] + chips[t][1]
            return pltpu.make_async_remote_copy(
                src_ref=half(a, blk), dst_ref=half(a, blk), send_sem=send_sems.at[a, t], recv_sem=recv_sems.at[a, t],
                device_id=(*chips[t], c), device_id_type=MESH)

        def to_sibling(a, t, r=0, nr=None):
            blk = 2 * chips[t][0] + chips[t][1]
            return pltpu.make_async_remote_copy(
                src_ref=half(a, blk, r, nr), dst_ref=half(a, blk, r, nr), send_sem=send_sems.at[a, 3 + t],
                recv_sem=recv_sems.at[a, 3 + t], device_id=sibling, device_id_type=MESH)

        def from_sibling(a, t):
            blk = 2 * chips[t][0] + chips[t][1]
            rows = ins[a].shape[0] // 2
            dst = outs[a].at[blk, pl.ds((1 - c) * rows, rows), :]
            return pltpu.make_async_remote_copy(
                src_ref=dst, dst_ref=dst, send_sem=send_sems.at[a, 3 + t], recv_sem=recv_sems.at[a, 3 + t],
                device_id=sibling, device_id_type=MESH)

        for a in range(n):
            for r, nr in _chunks(ins[a].shape[0]):
                pltpu.make_async_copy(ins[a].at[pl.ds(r, nr), :], outs[a].at[me, pl.ds(r, nr), :], local_sems.at[a]).start()
        for a in range(n):
            for t in range(3):
                for r, nr in _chunks(ins[a].shape[0] // 2):
                    to_chip(a, t, r, nr).start()
        for a in range(n):
            for t in range(3):
                from_chip(a, t).wait_recv()
                for r, nr in _chunks(ins[a].shape[0] // 2):
                    to_sibling(a, t, r, nr).start()
        for a in range(n):
            for t in range(3):
                from_sibling(a, t).wait_recv()
        for a in range(n):
            for t in range(3):
                to_chip(a, t).wait_send()
                to_sibling(a, t).wait_send()
            pltpu.make_async_copy(ins[a], outs[a].at[me], local_sems.at[a]).wait()

    return pl.pallas_call(
        body, name="gather_weights", in_specs=[ANY] * n, out_specs=[ANY] * n,
        out_shape=[jax.ShapeDtypeStruct((4,) + s.shape, s.dtype) for s in shards],
        scratch_shapes=[pltpu.SemaphoreType.DMA((n, 6)), pltpu.SemaphoreType.DMA((n, 6)), pltpu.SemaphoreType.DMA((n,))],
    )(*shards)


def pair_swap_halves(grads):
    n = len(grads)

    def body(*refs):
        ins, mine, theirs = refs[:n], refs[n:2 * n], refs[2 * n:3 * n]
        send_sems, recv_sems, local_sems = refs[3 * n:]
        x, y, c, _ = _place()
        sibling = (x, y, 1 - c)
        for a in range(n):
            rows = ins[a].shape[1] // 2
            for j in range(4):
                for r, nr in _chunks(rows):
                    pltpu.make_async_copy(ins[a].at[j, pl.ds(c * rows + r, nr), :], mine[a].at[j, pl.ds(r, nr), :],
                                          local_sems.at[a]).start()
                    pltpu.make_async_remote_copy(
                        src_ref=ins[a].at[j, pl.ds((1 - c) * rows + r, nr), :], dst_ref=theirs[a].at[j, pl.ds(r, nr), :],
                        send_sem=send_sems.at[a], recv_sem=recv_sems.at[a], device_id=sibling, device_id_type=MESH).start()
        for a in range(n):
            pltpu.make_async_remote_copy(src_ref=theirs[a], dst_ref=theirs[a], send_sem=send_sems.at[a],
                                         recv_sem=recv_sems.at[a], device_id=sibling, device_id_type=MESH).wait()
            pltpu.make_async_copy(mine[a], mine[a], local_sems.at[a]).wait()

    halves = [jax.ShapeDtypeStruct((4, g.shape[1] // 2, g.shape[2]), g.dtype) for g in grads]
    outs = pl.pallas_call(
        body, name="pair_swap_halves", in_specs=[ANY] * n, out_specs=[ANY] * (2 * n), out_shape=halves + halves,
        scratch_shapes=[pltpu.SemaphoreType.DMA((n,)), pltpu.SemaphoreType.DMA((n,)), pltpu.SemaphoreType.DMA((n,))],
    )(*grads)
    return outs[:n], outs[n:]


def scatter_to_chips(parts):
    n = len(parts)

    def body(*refs):
        ins, outs = refs[:n], refs[n:2 * n]
        send_sems, recv_sems, local_sems = refs[2 * n:]
        x, y, c, chips = _place()
        me = 2 * x + y
        blks = [2 * cx + cy for cx, cy in chips]
        for a in range(n):
            for r, nr in _chunks(ins[a].shape[1]):
                pltpu.make_async_copy(ins[a].at[me, pl.ds(r, nr), :], outs[a].at[me, pl.ds(r, nr), :], local_sems.at[a]).start()
                for t in range(3):
                    pltpu.make_async_remote_copy(
                        src_ref=ins[a].at[blks[t], pl.ds(r, nr), :], dst_ref=outs[a].at[me, pl.ds(r, nr), :],
                        send_sem=send_sems.at[a, t], recv_sem=recv_sems.at[a, t],
                        device_id=(*chips[t], c), device_id_type=MESH).start()
        for a in range(n):
            for t in range(3):
                pltpu.make_async_remote_copy(
                    src_ref=outs[a].at[blks[t]], dst_ref=outs[a].at[blks[t]], send_sem=send_sems.at[a, t],
                    recv_sem=recv_sems.at[a, t], device_id=(*chips[t], c), device_id_type=MESH).wait()
            pltpu.make_async_copy(ins[a].at[me], outs[a].at[me], local_sems.at[a]).wait()

    return pl.pallas_call(
        body, name="scatter_to_chips", in_specs=[ANY] * n, out_specs=[ANY] * n,
        out_shape=[jax.ShapeDtypeStruct(p.shape, p.dtype) for p in parts],
        scratch_shapes=[pltpu.SemaphoreType.DMA((n, 3)), pltpu.SemaphoreType.DMA((n, 3)), pltpu.SemaphoreType.DMA((n,))],
    )(*parts)


def pair_join_halves(halves):
    n = len(halves)

    def body(*refs):
        ins, outs = refs[:n], refs[n:2 * n]
        send_sems, recv_sems, local_sems = refs[2 * n:]
        x, y, c, _ = _place()
        sibling = (x, y, 1 - c)
        for a in range(n):
            rows = ins[a].shape[0]
            for r, nr in _chunks(rows):
                dst = outs[a].at[pl.ds(c * rows + r, nr), :]
                pltpu.make_async_copy(ins[a].at[pl.ds(r, nr), :], dst, local_sems.at[a]).start()
                pltpu.make_async_remote_copy(
                    src_ref=ins[a].at[pl.ds(r, nr), :], dst_ref=dst, send_sem=send_sems.at[a], recv_sem=recv_sems.at[a],
                    device_id=sibling, device_id_type=MESH).start()
        for a in range(n):
            rows = ins[a].shape[0]
            got = outs[a].at[pl.ds((1 - c) * rows, rows), :]
            pltpu.make_async_remote_copy(src_ref=ins[a], dst_ref=got, send_sem=send_sems.at[a], recv_sem=recv_sems.at[a],
                                         device_id=sibling, device_id_type=MESH).wait()
            pltpu.make_async_copy(ins[a], got, local_sems.at[a]).wait()

    return pl.pallas_call(
        body, name="pair_join_halves", in_specs=[ANY] * n, out_specs=[ANY] * n,
        out_shape=[jax.ShapeDtypeStruct((2 * h.shape[0], h.shape[1]), h.dtype) for h in halves],
        scratch_shapes=[pltpu.SemaphoreType.DMA((n,)), pltpu.SemaphoreType.DMA((n,)), pltpu.SemaphoreType.DMA((n,))],
    )(*halves)


def allreduce_small(packed):
    rows = packed.shape[0]

    def body(in_ref, out_ref, gathered, send_sems, recv_sems):
        x, y, c, _ = _place()
        me = 4 * x + 2 * y + c
        gathered[me] = in_ref[...]
        flips = [(fx, fy, fc) for fx in (0, 1) for fy in (0, 1) for fc in (0, 1)][1:]
        peers = [((1 - x) if fx else x, (1 - y) if fy else y, (1 - c) if fc else c) for fx, fy, fc in flips]
        copies = []
        for t, peer in enumerate(peers):
            cp = pltpu.make_async_remote_copy(
                src_ref=in_ref, dst_ref=gathered.at[me], send_sem=send_sems.at[t], recv_sem=recv_sems.at[t],
                device_id=peer, device_id_type=MESH)
            cp.start()
            copies.append(cp)
        for t, (px, py, pc) in enumerate(peers):
            slot = gathered.at[4 * px + 2 * py + pc]
            pltpu.make_async_remote_copy(
                src_ref=slot, dst_ref=slot, send_sem=send_sems.at[t], recv_sem=recv_sems.at[t],
                device_id=(px, py, pc), device_id_type=MESH).wait_recv()
        for cp in copies:
            cp.wait_send()
        acc = gathered[0]
        for k in range(1, 8):
            acc = acc + gathered[k]
        out_ref[...] = acc

    vm = pl.BlockSpec(memory_space=pltpu.VMEM)
    return pl.pallas_call(
        body, name="allreduce_small", in_specs=[vm], out_specs=vm, out_shape=jax.ShapeDtypeStruct(packed.shape, F32),
        scratch_shapes=[pltpu.VMEM((8, rows, 128), F32), pltpu.SemaphoreType.DMA((7,)), pltpu.SemaphoreType.DMA((7,))],
    )(packed)


SMALL_NAMES = ("norm_mix_w", "ssd_conv_w", "ssd_conv_b", "ssd_dt_bias", "ssd_a_log", "ssd_d", "ssd_norm_w", "fox_f_bias",
               "fox_q_norm_w", "fox_k_norm_w", "norm_ffn_w", "ffn_conv_w", "ffn_conv_b")
BIG_NAMES = ("w_in", "w_out", "w_up", "w_down")
WEIGHT_ORDER = ("norm_mix_w", "w_in", "ssd_conv_w", "ssd_conv_b", "ssd_dt_bias", "ssd_a_log", "ssd_d", "ssd_norm_w",
                "fox_f_bias", "fox_q_norm_w", "fox_k_norm_w", "w_out", "norm_ffn_w", "w_up", "ffn_conv_w", "ffn_conv_b", "w_down")
ADAM_ROWS = {"w_in": 256, "w_out": 256, "w_up": 256, "w_down": 176}


def _pack(arrays):
    rows = []
    for a in arrays:
        flat = a.reshape(-1).astype(F32)
        rows.append(jnp.pad(flat, (0, (-flat.shape[0]) % 1024)).reshape(-1, 128))
    return jnp.concatenate(rows, axis=0)


def _unpack(packed, shapes):
    out, r = [], 0
    for shp in shapes:
        size = 1
        for d in shp:
            size *= d
        nrow = 8 * (-(-size // 1024))
        out.append(packed[r:r + nrow].reshape(-1)[:size].reshape(shp))
        r += nrow
    return out


def _pad_rows(a, rows):
    return jnp.pad(a, ((0, rows - a.shape[0]), (0, 0)))


def kernel(x, norm_mix_w, w_in, ssd_conv_w, ssd_conv_b, ssd_dt_bias, ssd_a_log, ssd_d, ssd_norm_w, fox_f_bias, fox_q_norm_w, fox_k_norm_w, w_out, norm_ffn_w, w_up, ffn_conv_w, ffn_conv_b, w_down, loss_target, m_norm_mix_w, m_w_in, m_ssd_conv_w, m_ssd_conv_b, m_ssd_dt_bias, m_ssd_a_log, m_ssd_d, m_ssd_norm_w, m_fox_f_bias, m_fox_q_norm_w, m_fox_k_norm_w, m_w_out, m_norm_ffn_w, m_w_up, m_ffn_conv_w, m_ffn_conv_b, m_w_down, v_norm_mix_w, v_w_in, v_ssd_conv_w, v_ssd_conv_b, v_ssd_dt_bias, v_ssd_a_log, v_ssd_d, v_ssd_norm_w, v_fox_f_bias, v_fox_q_norm_w, v_fox_k_norm_w, v_w_out, v_norm_ffn_w, v_w_up, v_ffn_conv_w, v_ffn_conv_b, v_w_down):
    w = dict(norm_mix_w=norm_mix_w, w_in=w_in, ssd_conv_w=ssd_conv_w, ssd_conv_b=ssd_conv_b, ssd_dt_bias=ssd_dt_bias,
             ssd_a_log=ssd_a_log, ssd_d=ssd_d, ssd_norm_w=ssd_norm_w, fox_f_bias=fox_f_bias, fox_q_norm_w=fox_q_norm_w,
             fox_k_norm_w=fox_k_norm_w, w_out=w_out, norm_ffn_w=norm_ffn_w, w_up=w_up, ffn_conv_w=ffn_conv_w,
             ffn_conv_b=ffn_conv_b, w_down=w_down)
    m = dict(norm_mix_w=m_norm_mix_w, w_in=m_w_in, ssd_conv_w=m_ssd_conv_w, ssd_conv_b=m_ssd_conv_b, ssd_dt_bias=m_ssd_dt_bias,
             ssd_a_log=m_ssd_a_log, ssd_d=m_ssd_d, ssd_norm_w=m_ssd_norm_w, fox_f_bias=m_fox_f_bias, fox_q_norm_w=m_fox_q_norm_w,
             fox_k_norm_w=m_fox_k_norm_w, w_out=m_w_out, norm_ffn_w=m_norm_ffn_w, w_up=m_w_up, ffn_conv_w=m_ffn_conv_w,
             ffn_conv_b=m_ffn_conv_b, w_down=m_w_down)
    v = dict(norm_mix_w=v_norm_mix_w, w_in=v_w_in, ssd_conv_w=v_ssd_conv_w, ssd_conv_b=v_ssd_conv_b, ssd_dt_bias=v_ssd_dt_bias,
             ssd_a_log=v_ssd_a_log, ssd_d=v_ssd_d, ssd_norm_w=v_ssd_norm_w, fox_f_bias=v_fox_f_bias, fox_q_norm_w=v_fox_q_norm_w,
             fox_k_norm_w=v_fox_k_norm_w, w_out=v_w_out, norm_ffn_w=v_norm_ffn_w, w_up=v_w_up, ffn_conv_w=v_ffn_conv_w,
             ffn_conv_b=v_ffn_conv_b, w_down=v_w_down)
    chip = 2 * lax.axis_index("x") + lax.axis_index("y")

    shards = [w_in[0].astype(BF16), w_out[0].astype(BF16), w_up[0].astype(BF16), w_down[0].astype(BF16),
              _pad_rows(ssd_conv_w[0], 16), _pad_rows(ffn_conv_w[0], 16)]
    a_in, a_out, a_up, a_down, a_scw, a_fcw = gather_weights(shards)
    w_full = a_in.transpose(1, 0, 2).reshape(D_MODEL, IN_COLS)
    wm = jnp.concatenate([w_full[:, :2048], w_full[:, 2576:5648], w_full[:, 2048:2560]], axis=1)
    ws = jnp.concatenate([w_full[:, 2560:2576], w_full[:, 5648:5664], jnp.zeros((D_MODEL, SMALL_COLS - 32), BF16)], axis=1)
    wo = a_out.reshape(2048, D_MODEL)
    wu = a_up.transpose(1, 0, 2).reshape(D_MODEL, 2 * D_FF)
    wd = a_down.reshape(D_FF, D_MODEL)
    ssd_cw8 = a_scw.transpose(1, 0, 2).reshape(16, 1536)[:8]
    ffn_cw8 = a_fcw.transpose(1, 0, 2).reshape(16, 2 * D_FF)[:8]
    smallp = jnp.zeros((8, 128), F32)
    smallp = smallp.at[0, :16].set(ssd_dt_bias[0]).at[1, :16].set(ssd_a_log[0]).at[2, :16].set(ssd_d[0])
    smallp = smallp.at[3, F_LANE:F_LANE + 16].set(fox_f_bias[0])
    qw_t = jnp.tile(fox_q_norm_w[0], N_HEADS)[None]
    kw_t = jnp.tile(fox_k_norm_w[0], N_HEADS)[None]
    sel = (jnp.arange(1024)[:, None] // HEAD_DIM == jnp.arange(128)[None, :]).astype(BF16)

    res = local_step(x[0], loss_target[0], wm, ws, wo, wu, wd, ssd_cw8, ssd_conv_b, smallp, ssd_norm_w, qw_t, kw_t,
                     sel, sel.T, norm_mix_w, norm_ffn_w, ffn_cw8, ffn_conv_b)

    full_shapes = [(1, 1024), (1, 4, 1536), (1, 1536), (1, 16), (1, 16), (1, 16), (1, 1024), (1, 16), (1, 64), (1, 64),
                   (1, 1024), (1, 3, 2 * D_FF), (1, 2 * D_FF), (1,)]
    local_small = [res["g_norm_mix"], res["g_ssd_cw"][:4], res["g_ssd_cw"][4], res["g_sp"][0, :16], res["g_sp"][1, :16],
                   res["g_sp"][2, :16], res["g_ssd_nw"], res["g_fb"][0, F_LANE:F_LANE + 16],
                   res["g_qw"].reshape(N_HEADS, HEAD_DIM).sum(0), res["g_kw"].reshape(N_HEADS, HEAD_DIM).sum(0),
                   res["g_norm_ffn"], res["g_ffn_cw"][:3], res["g_ffn_cw"][3], jnp.sum(res["sq"])]
    summed = _unpack(allreduce_small(_pack(local_small)), full_shapes)
    loss = (0.5 / D_MODEL) * summed[-1][0]
    g_small = dict(zip(SMALL_NAMES, summed[:-1]))
    g_small["ssd_conv_w"] = lax.dynamic_slice(g_small["ssd_conv_w"], (0, 0, 384 * chip), (1, 4, 384))
    g_small["ffn_conv_w"] = lax.dynamic_slice(g_small["ffn_conv_w"], (0, 0, 1408 * chip), (1, 3, 1408))

    g_wm, g_ws = res["g_wm"], res["g_ws"]
    g_in_full = jnp.concatenate([g_wm[:, :2048], g_wm[:, 5120:5632], g_ws[:, :16], g_wm[:, 2048:5120], g_ws[:, 16:32]], axis=1)
    big = [g_in_full.reshape(D_MODEL, 4, 1416).transpose(1, 0, 2), res["g_out"].reshape(4, 512, D_MODEL),
           res["g_up"].reshape(D_MODEL, 4, 1408).transpose(1, 0, 2), res["g_down"].reshape(4, 704, D_MODEL)]
    mine, theirs = pair_swap_halves(big)
    parts = [add_pair(a, b, name="add_pair_" + n, tr=ADAM_ROWS[n]) for a, b, n in zip(mine, theirs, BIG_NAMES)]
    landed = scatter_to_chips(parts)
    halves = [sum_chips(p, name="sum_chips_" + n, tr=ADAM_ROWS[n]) for p, n in zip(landed, BIG_NAMES)]
    g_big = dict(zip(BIG_NAMES, pair_join_halves(halves)))

    grads, deltas, new_m, new_v = {}, {}, {}, {}
    for n in BIG_NAMES:
        d, mn, vn = adamw(w[n][0], g_big[n], m[n][0], v[n][0], name="adamw_" + n, tr=ADAM_ROWS[n])
        grads[n], deltas[n], new_m[n], new_v[n] = g_big[n][None], d[None], mn[None], vn[None]
    shapes = [w[n].shape for n in SMALL_NAMES]
    d, mn, vn = adamw(_pack([w[n] for n in SMALL_NAMES]), _pack([g_small[n] for n in SMALL_NAMES]),
                      _pack([m[n] for n in SMALL_NAMES]), _pack([v[n] for n in SMALL_NAMES]), name="adamw_small", tr=8)
    for n, dd, mm, vv in zip(SMALL_NAMES, _unpack(d, shapes), _unpack(mn, shapes), _unpack(vn, shapes)):
        grads[n], deltas[n], new_m[n], new_v[n] = g_small[n].reshape(w[n].shape), dd, mm, vv
    return (loss, res["grad_x"][None], *[grads[n] for n in WEIGHT_ORDER], *[deltas[n] for n in WEIGHT_ORDER],
            *[new_m[n] for n in WEIGHT_ORDER], *[new_v[n] for n in WEIGHT_ORDER])
```

```python
import functools

import jax
import jax.numpy as jnp
from jax import lax
from jax.experimental import pallas as pl
from jax.experimental.pallas import tpu as pltpu

F32 = jnp.float32
BF16 = jnp.bfloat16
MESH = pl.DeviceIdType.MESH

D_MODEL = 1024
HEAD_DIM = 64
N_HEADS = 16
N_PAIRS = N_HEADS // 2
SSD_CHUNK = 128
SSD_STATE = 128
SSD_CONV = 4
D_FF = 2816
FFN_CONV = 3
NORM_EPS = 1e-6
MAIN_COLS = 5632
SMALL_COLS = 128
F_LANE = 16
IN_COLS = 5664

ADAM_LR = 0.001
ADAM_B1 = 0.9
ADAM_B2 = 0.999
ADAM_EPS = 1e-08
ADAM_WD = 0.01
ADAM_STEP = 10

VMEM_LIMIT_V7X = 56 * 1024 * 1024
NEG_BIG = -1e30


def _params(sem=None):
    return pltpu.CompilerParams(dimension_semantics=sem, vmem_limit_bytes=VMEM_LIMIT_V7X)


def _sigmoid(x):
    return 1.0 / (1.0 + jnp.exp(-x))


def _silu_and_grad(x):
    s = _sigmoid(x)
    return x * s, s * (1.0 + x * (1.0 - s))


def _shift_down(v, j):
    return v if j == 0 else pltpu.roll(v, j, 0)


def _shift_up(v, j):
    return v if j == 0 else pltpu.roll(v, v.shape[0] - j, 0)


def _row_iota(shape):
    return lax.broadcasted_iota(jnp.int32, shape, 0)


def _lane_iota(shape):
    return lax.broadcasted_iota(jnp.int32, shape, 1)


def _dot(a, b, mode="nn"):
    dims = {"nn": (((1,), (0,)), ((), ())), "nt": (((1,), (1,)), ((), ())), "tn": (((0,), (0,)), ((), ()))}[mode]
    return lax.dot_general(a.astype(BF16), b.astype(BF16), dims, preferred_element_type=F32)


def _dot_f32(a, b):
    return jnp.dot(a, b, precision=lax.Precision.HIGHEST, preferred_element_type=F32)


def matmul(a, b, *, mode, tm, tn, tk, out_dtype, name, add=None, b_koff=0):
    if mode == "nn":
        (m, k), n = a.shape, b.shape[1]
    elif mode == "nt":
        (m, k), n = a.shape, b.shape[0]
    else:
        (k, m), n = a.shape, b.shape[1]
    assert m % tm == 0 and n % tn == 0 and k % tk == 0, (name, m, n, k, tm, tn, tk)
    nk = k // tk
    a_spec = pl.BlockSpec((tk, tm), lambda i, j, kk: (kk, i)) if mode == "tn" else pl.BlockSpec((tm, tk), lambda i, j, kk: (i, kk))
    b_spec = (pl.BlockSpec((tn, tk), lambda i, j, kk: (j, kk + b_koff)) if mode == "nt"
              else pl.BlockSpec((tk, tn), lambda i, j, kk: (kk + b_koff, j)))
    o_spec = pl.BlockSpec((tm, tn), lambda i, j, kk: (i, j))
    has_add = add is not None

    def body(*refs):
        if has_add:
            a_ref, b_ref, add_ref, o_ref, acc_ref = refs
        else:
            a_ref, b_ref, o_ref, acc_ref = refs
        kk = pl.program_id(2)
        part = _dot(a_ref[...], b_ref[...], mode)

        def finish(total):
            if has_add:
                total = total + add_ref[...]
            o_ref[...] = total.astype(out_dtype)

        if nk == 1:
            finish(part)
        else:
            @pl.when(kk == 0)
            def _():
                acc_ref[...] = part

            @pl.when(jnp.logical_and(kk > 0, kk < nk - 1))
            def _():
                acc_ref[...] += part

            @pl.when(kk == nk - 1)
            def _():
                finish(acc_ref[...] + part)

    in_specs = [a_spec, b_spec] + ([o_spec] if has_add else [])
    args = (a, b) + ((add,) if has_add else ())
    return pl.pallas_call(
        body, name=name, grid=(m // tm, n // tn, nk), in_specs=in_specs, out_specs=o_spec,
        out_shape=jax.ShapeDtypeStruct((m, n), out_dtype),
        scratch_shapes=[pltpu.VMEM((tm, tn) if nk > 1 else (8, 128), F32)],
        compiler_params=_params(("parallel", "parallel", "arbitrary")),
    )(*args)


def rms_fwd(x, w, *, name, tm=512):
    s, d = x.shape

    def body(x_ref, w_ref, h_ref):
        xv = x_ref[...]
        r = lax.rsqrt(jnp.mean(xv * xv, axis=-1, keepdims=True) + NORM_EPS)
        h_ref[...] = ((xv * r) * w_ref[...]).astype(BF16)

    return pl.pallas_call(
        body, name=name, grid=(s // tm,),
        in_specs=[pl.BlockSpec((tm, d), lambda i: (i, 0)), pl.BlockSpec((1, d), lambda i: (0, 0))],
        out_specs=pl.BlockSpec((tm, d), lambda i: (i, 0)),
        out_shape=jax.ShapeDtypeStruct((s, d), BF16), compiler_params=_params(("parallel",)),
    )(x, w)


def rms_bwd(dh, x, w, resid, *, name, tm=512):
    s, d = x.shape

    def body(dh_ref, x_ref, w_ref, res_ref, dx_ref, dw_ref):
        xv = x_ref[...]
        dhv = dh_ref[...]
        r = lax.rsqrt(jnp.mean(xv * xv, axis=-1, keepdims=True) + NORM_EPS)
        xh = xv * r
        g = dhv * w_ref[...]
        dx_ref[...] = res_ref[...] + r * (g - xh * jnp.mean(g * xh, axis=-1, keepdims=True))
        part = jnp.sum(dhv * xh, axis=0, keepdims=True)

        @pl.when(pl.program_id(0) == 0)
        def _():
            dw_ref[...] = part

        @pl.when(pl.program_id(0) > 0)
        def _():
            dw_ref[...] += part

    row = pl.BlockSpec((tm, d), lambda i: (i, 0))
    vec = pl.BlockSpec((1, d), lambda i: (0, 0))
    return pl.pallas_call(
        body, name=name, grid=(s // tm,), in_specs=[row, row, vec, row], out_specs=[row, vec],
        out_shape=[jax.ShapeDtypeStruct((s, d), F32), jax.ShapeDtypeStruct((1, d), F32)],
        compiler_params=_params(("arbitrary",)),
    )(dh, x, w, resid)


def loss_head(y, target, *, tm=512):
    s, d = y.shape

    def body(y_ref, t_ref, dy_ref, sq_ref):
        e = y_ref[...] - t_ref[...]
        dy_ref[...] = e / float(d)
        part = jnp.sum(e * e, axis=0, keepdims=True)

        @pl.when(pl.program_id(0) == 0)
        def _():
            sq_ref[...] = part

        @pl.when(pl.program_id(0) > 0)
        def _():
            sq_ref[...] += part

    row = pl.BlockSpec((tm, d), lambda i: (i, 0))
    vec = pl.BlockSpec((1, d), lambda i: (0, 0))
    return pl.pallas_call(
        body, name="loss_head", grid=(s // tm,), in_specs=[row, row], out_specs=[row, vec],
        out_shape=[jax.ShapeDtypeStruct((s, d), F32), jax.ShapeDtypeStruct((1, d), F32)],
        compiler_params=_params(("arbitrary",)),
    )(y, target)


def _conv_rows(ext, w, k_taps):
    acc = w[k_taps - 1:k_taps, :] * ext
    for k in range(k_taps - 1):
        acc = acc + w[k:k + 1, :] * _shift_down(ext, k_taps - 1 - k)
    return acc


def _conv_rows_transposed(dext, w, k_taps):
    acc = w[k_taps - 1:k_taps, :] * dext
    for k in range(k_taps - 1):
        acc = acc + w[k:k + 1, :] * _shift_up(dext, k_taps - 1 - k)
    return acc


def _stack_rows(rows, width):
    ri = _row_iota((8, width))
    out = jnp.zeros((8, width), F32)
    for k, r in enumerate(rows):
        out = out + jnp.where(ri == k, r, 0.0)
    return out


def ffn_mid_fwd(hu, conv_w8, conv_b, *, tm=512, tc=256):
    s = hu.shape[0]
    ncol = D_FF // tc
    r8 = tm // 8

    def body(g_ref, v_ref, gp_ref, vp_ref, wg_ref, wv_ref, bg_ref, bv_ref, o_ref):
        first = pl.program_id(1) == 0

        def conv(cur_ref, prev_ref, w_ref, b_ref):
            prev = jnp.where(first, 0.0, prev_ref[...])
            ext = jnp.concatenate([prev, cur_ref[...]], axis=0)
            return _conv_rows(ext, w_ref[...], FFN_CONV)[8:] + b_ref[...]

        gc = conv(g_ref, gp_ref, wg_ref, bg_ref)
        vc = conv(v_ref, vp_ref, wv_ref, bv_ref)
        o_ref[...] = (gc * _sigmoid(gc) * vc).astype(BF16)

    def prev_idx(i):
        return jnp.maximum(i * r8 - 1, 0)

    in_specs = [
        pl.BlockSpec((tm, tc), lambda j, i: (i, j)),
        pl.BlockSpec((tm, tc), lambda j, i: (i, j + ncol)),
        pl.BlockSpec((8, tc), lambda j, i: (prev_idx(i), j)),
        pl.BlockSpec((8, tc), lambda j, i: (prev_idx(i), j + ncol)),
        pl.BlockSpec((8, tc), lambda j, i: (0, j)),
        pl.BlockSpec((8, tc), lambda j, i: (0, j + ncol)),
        pl.BlockSpec((1, tc), lambda j, i: (0, j)),
        pl.BlockSpec((1, tc), lambda j, i: (0, j + ncol)),
    ]
    return pl.pallas_call(
        body, name="ffn_mid_fwd", grid=(ncol, s // tm), in_specs=in_specs,
        out_specs=pl.BlockSpec((tm, tc), lambda j, i: (i, j)),
        out_shape=jax.ShapeDtypeStruct((s, D_FF), BF16), compiler_params=_params(("parallel", "parallel")),
    )(hu, hu, hu, hu, conv_w8, conv_w8, conv_b, conv_b)


def ffn_mid_bwd(hu, dact, conv_w8, conv_b, *, tm=512, tc=256):
    s = hu.shape[0]
    ncol = D_FF // tc
    nrow = s // tm
    r8 = tm // 8

    def body(g_ref, v_ref, gp_ref, vp_ref, gn_ref, vn_ref, da_ref, dan_ref, wg_ref, wv_ref, bg_ref, bv_ref,
             dg_ref, dv_ref, wgo_ref, wvo_ref):
        i = pl.program_id(1)
        first = i == 0
        last = i == nrow - 1

        def ext_of(cur_ref, prev_ref, next_ref):
            prev = jnp.where(first, 0.0, prev_ref[...])
            return jnp.concatenate([prev, cur_ref[...], next_ref[...]], axis=0)

        g_ext = ext_of(g_ref, gp_ref, gn_ref)
        v_ext = ext_of(v_ref, vp_ref, vn_ref)
        gc = _conv_rows(g_ext, wg_ref[...], FFN_CONV) + bg_ref[...]
        vc = _conv_rows(v_ext, wv_ref[...], FFN_CONV) + bv_ref[...]
        da_ext = jnp.concatenate([jnp.zeros((8, tc), F32), da_ref[...], jnp.where(last, 0.0, dan_ref[...])], axis=0)
        silu, dsilu = _silu_and_grad(gc)
        dgc = da_ext * vc * dsilu
        dvc = da_ext * silu
        dg_ref[...] = _conv_rows_transposed(dgc, wg_ref[...], FFN_CONV)[8:8 + tm].astype(BF16)
        dv_ref[...] = _conv_rows_transposed(dvc, wv_ref[...], FFN_CONV)[8:8 + tm].astype(BF16)

        def wgrad(dcur, x_ext):
            rows = [jnp.sum(dcur * _shift_down(x_ext, FFN_CONV - 1 - k)[8:8 + tm], axis=0, keepdims=True)
                    for k in range(FFN_CONV)]
            rows.append(jnp.sum(dcur, axis=0, keepdims=True))
            return _stack_rows(rows, tc)

        pg = wgrad(dgc[8:8 + tm], g_ext)
        pv = wgrad(dvc[8:8 + tm], v_ext)

        @pl.when(first)
        def _():
            wgo_ref[...] = pg
            wvo_ref[...] = pv

        @pl.when(i > 0)
        def _():
            wgo_ref[...] += pg
            wvo_ref[...] += pv

    def prev_idx(i):
        return jnp.maximum(i * r8 - 1, 0)

    def next_idx(i):
        return jnp.minimum((i + 1) * r8, s // 8 - 1)

    cur_g = pl.BlockSpec((tm, tc), lambda j, i: (i, j))
    cur_v = pl.BlockSpec((tm, tc), lambda j, i: (i, j + ncol))
    in_specs = [
        cur_g, cur_v,
        pl.BlockSpec((8, tc), lambda j, i: (prev_idx(i), j)),
        pl.BlockSpec((8, tc), lambda j, i: (prev_idx(i), j + ncol)),
        pl.BlockSpec((8, tc), lambda j, i: (next_idx(i), j)),
        pl.BlockSpec((8, tc), lambda j, i: (next_idx(i), j + ncol)),
        cur_g,
        pl.BlockSpec((8, tc), lambda j, i: (next_idx(i), j)),
        pl.BlockSpec((8, tc), lambda j, i: (0, j)),
        pl.BlockSpec((8, tc), lambda j, i: (0, j + ncol)),
        pl.BlockSpec((1, tc), lambda j, i: (0, j)),
        pl.BlockSpec((1, tc), lambda j, i: (0, j + ncol)),
    ]
    out_specs = [cur_g, cur_g, pl.BlockSpec((8, tc), lambda j, i: (0, j)), pl.BlockSpec((8, tc), lambda j, i: (0, j))]
    out_shape = [jax.ShapeDtypeStruct((s, D_FF), BF16), jax.ShapeDtypeStruct((s, D_FF), BF16),
                 jax.ShapeDtypeStruct((8, D_FF), F32), jax.ShapeDtypeStruct((8, D_FF), F32)]
    return pl.pallas_call(
        body, name="ffn_mid_bwd", grid=(ncol, nrow), in_specs=in_specs, out_specs=out_specs, out_shape=out_shape,
        compiler_params=_params(("parallel", "arbitrary")),
    )(hu, hu, hu, hu, hu, hu, dact, dact, conv_w8, conv_w8, conv_b, conv_b)


def _softplus(x):
    return jnp.maximum(x, 0.0) + jnp.log(1.0 + jnp.exp(-jnp.abs(x)))


def _cumsum_rows(v):
    n = v.shape[0]
    ri = _row_iota(v.shape)
    sh = 1
    while sh < n:
        v = v + jnp.where(ri >= sh, _shift_down(v, sh), 0.0)
        sh *= 2
    return v


def _rev_cumsum_rows(v):
    n = v.shape[0]
    ri = _row_iota(v.shape)
    sh = 1
    while sh < n:
        v = v + jnp.where(ri < n - sh, _shift_up(v, sh), 0.0)
        sh *= 2
    return v


def _half_row_sums(v, lo):
    s0 = jnp.sum(jnp.where(lo, v, 0.0), axis=1, keepdims=True)
    return s0, jnp.sum(v, axis=1, keepdims=True) - s0


def _total(v):
    return jnp.sum(jnp.sum(v, axis=1, keepdims=True), axis=0, keepdims=True)


def _ssd_in_specs(rev_nc=None):
    def ch(c):
        return c if rev_nc is None else rev_nc - 1 - c

    def prev(c):
        return jnp.maximum(ch(c) * (SSD_CHUNK // 8) - 1, 0)

    L = SSD_CHUNK
    return [
        pl.BlockSpec((L, 1024), lambda c: (ch(c), 0)),
        pl.BlockSpec((L, 1024), lambda c: (ch(c), 1)),
        pl.BlockSpec((L, 256), lambda c: (ch(c), 20)),
        pl.BlockSpec((L, 256), lambda c: (ch(c), 21)),
        pl.BlockSpec((8, 1024), lambda c: (prev(c), 1)),
        pl.BlockSpec((8, 256), lambda c: (prev(c), 20)),
        pl.BlockSpec((8, 256), lambda c: (prev(c), 21)),
        pl.BlockSpec((8, 1024), lambda c: (0, 0)),
        pl.BlockSpec((8, 256), lambda c: (0, 4)),
        pl.BlockSpec((8, 256), lambda c: (0, 5)),
        pl.BlockSpec((1, 1024), lambda c: (0, 0)),
        pl.BlockSpec((1, 256), lambda c: (0, 4)),
        pl.BlockSpec((1, 256), lambda c: (0, 5)),
        pl.BlockSpec((L, SMALL_COLS), lambda c: (ch(c), 0)),
        pl.BlockSpec((8, 128), lambda c: (0, 0)),
        pl.BlockSpec((1, 1024), lambda c: (0, 0)),
    ]


def _ssd_conv_pre(cur_ref, prev_ref, w_ref, b_ref, first):
    prev = jnp.where(first, 0.0, prev_ref[...])
    ext = jnp.concatenate([prev, cur_ref[...]], axis=0)
    return ext, _conv_rows(ext, w_ref[...], SSD_CONV)[8:] + b_ref[...]


def _ssd_time_consts(small_ref, sp_ref):
    dt_pre = small_ref[...] + sp_ref[0:1, :]
    dt = _softplus(dt_pre)
    a = -jnp.exp(sp_ref[1:2, :])
    acs = _cumsum_rows(dt * a)
    return dt_pre, dt, a, acs


def ssd_fwd(proj, small, conv_w8, conv_b, smallp, norm_w):
    s = proj.shape[0]
    nc = s // SSD_CHUNK
    L = SSD_CHUNK

    def body(z_ref, xs_ref, b_ref, c_ref, xsp_ref, bp_ref, cp_ref, wx_ref, wb_ref, wc_ref, bx_ref, bb_ref, bc_ref,
             small_ref, sp_ref, nw_ref, y_ref, ypre_ref, st_ref, state):
        first = pl.program_id(0) == 0

        @pl.when(first)
        def _():
            state[...] = jnp.zeros_like(state)

        xs = _ssd_conv_pre(xs_ref, xsp_ref, wx_ref, bx_ref, first)[1]
        xs = xs * _sigmoid(xs)
        bm = _ssd_conv_pre(b_ref, bp_ref, wb_ref, bb_ref, first)[1]
        bm = bm * _sigmoid(bm)
        cm = _ssd_conv_pre(c_ref, cp_ref, wc_ref, bc_ref, first)[1]
        cm = cm * _sigmoid(cm)
        _, dt, _, acs = _ssd_time_consts(small_ref, sp_ref)
        acs_t = acs.T
        li = _lane_iota((L, L))
        ri = _row_iota((L, L))
        tri = ri >= li
        lo = li < HEAD_DIM
        st_ref[0] = state[...]
        for g in range(2):
            bg = bm[:, 128 * g:128 * g + 128]
            cg = cm[:, 128 * g:128 * g + 128]
            gmat = _dot(cg, bg, "nt")
            for pp in range(4):
                p = 4 * g + pp
                h0, h1 = 2 * p, 2 * p + 1
                x = xs[:, 128 * p:128 * p + 128]
                a0, a1 = acs[:, h0:h0 + 1], acs[:, h1:h1 + 1]
                xdt = x * jnp.where(lo, dt[:, h0:h0 + 1], dt[:, h1:h1 + 1])
                m0 = gmat * jnp.exp(jnp.where(tri, a0 - acs_t[h0:h0 + 1, :], NEG_BIG))
                m1 = gmat * jnp.exp(jnp.where(tri, a1 - acs_t[h1:h1 + 1, :], NEG_BIG))
                yd = _dot(m0, jnp.where(lo, xdt, 0.0)) + _dot(m1, jnp.where(lo, 0.0, xdt))
                hin = state[p]
                yo = _dot(cg, hin, "nt") * jnp.exp(jnp.where(lo, a0, a1))
                dskip = jnp.where(lo[0:1], sp_ref[2:3, h0:h0 + 1], sp_ref[2:3, h1:h1 + 1])
                ypre_ref[:, 128 * p:128 * p + 128] = yd + yo + dskip * x
                al0, al1 = acs[L - 1:L, h0:h0 + 1], acs[L - 1:L, h1:h1 + 1]
                w = jnp.exp(jnp.where(lo, al0 - a0, al1 - a1))
                dec = jnp.exp(jnp.where(ri < HEAD_DIM, al0, al1))
                state[p] = dec * hin + _dot(xdt * w, bg, "tn")
        z = z_ref[...]
        yg = ypre_ref[...] * (z * _sigmoid(z))
        for g in range(2):
            seg = yg[:, 512 * g:512 * g + 512]
            r = lax.rsqrt(jnp.mean(seg * seg, axis=-1, keepdims=True) + NORM_EPS)
            y_ref[:, 512 * g:512 * g + 512] = ((seg * r) * nw_ref[:, 512 * g:512 * g + 512]).astype(BF16)

    row = pl.BlockSpec((L, 1024), lambda c: (c, 0))
    return pl.pallas_call(
        body, name="ssd_fwd", grid=(nc,), in_specs=_ssd_in_specs(),
        out_specs=[row, row, pl.BlockSpec((1, N_PAIRS, 128, 128), lambda c: (c, 0, 0, 0))],
        out_shape=[jax.ShapeDtypeStruct((s, 1024), BF16), jax.ShapeDtypeStruct((s, 1024), F32),
                   jax.ShapeDtypeStruct((nc, N_PAIRS, 128, 128), F32)],
        scratch_shapes=[pltpu.VMEM((N_PAIRS, 128, 128), F32)],
        compiler_params=_params(("arbitrary",)),
    )(proj, proj, proj, proj, proj, proj, proj, conv_w8, conv_w8, conv_w8, conv_b, conv_b, conv_b, small, smallp, norm_w)


def ssd_bwd(proj, small, conv_w8, conv_b, smallp, norm_w, ypre, states, dy):
    s = proj.shape[0]
    nc = s // SSD_CHUNK
    L = SSD_CHUNK

    def body(z_ref, xs_ref, b_ref, c_ref, xsp_ref, bp_ref, cp_ref, wx_ref, wb_ref, wc_ref, bx_ref, bb_ref, bc_ref,
             small_ref, sp_ref, nw_ref, ypre_ref, st_ref, dy_ref,
             dz_ref, dxs_ref, db_ref, dc_ref, dsmall_ref, gwx_ref, gwb_ref, gwc_ref, gsp_ref, gnw_ref,
             dstate, carry_x, carry_b, carry_c, dxs_buf, dbm_buf, dcm_buf):
        step = pl.program_id(0)
        first_chunk = step == nc - 1
        start = step == 0

        @pl.when(start)
        def _():
            dstate[...] = jnp.zeros_like(dstate)
            carry_x[...] = jnp.zeros_like(carry_x)
            carry_b[...] = jnp.zeros_like(carry_b)
            carry_c[...] = jnp.zeros_like(carry_c)

        xs_ext, xs_pre = _ssd_conv_pre(xs_ref, xsp_ref, wx_ref, bx_ref, first_chunk)
        b_ext, b_pre = _ssd_conv_pre(b_ref, bp_ref, wb_ref, bb_ref, first_chunk)
        c_ext, c_pre = _ssd_conv_pre(c_ref, cp_ref, wc_ref, bc_ref, first_chunk)
        xs, xs_ds = _silu_and_grad(xs_pre)
        bm, b_ds = _silu_and_grad(b_pre)
        cm, c_ds = _silu_and_grad(c_pre)
        dt_pre, dt, a, acs = _ssd_time_consts(small_ref, sp_ref)
        acs_t = acs.T
        li = _lane_iota((L, L))
        ri = _row_iota((L, L))
        tri = ri >= li
        lo = li < HEAD_DIM
        lo_rows = ri < HEAD_DIM
        li1 = _lane_iota((1, L))

        z = z_ref[...]
        sz, dsz = _silu_and_grad(z)
        y = ypre_ref[...]
        yg = y * sz
        dout = dy_ref[...]
        dyg_parts = []
        gnw_parts = []
        for g in range(2):
            sl = slice(512 * g, 512 * g + 512)
            seg = yg[:, sl]
            r = lax.rsqrt(jnp.mean(seg * seg, axis=-1, keepdims=True) + NORM_EPS)
            n = seg * r
            gnw_parts.append(jnp.sum(dout[:, sl] * n, axis=0, keepdims=True))
            gg = dout[:, sl] * nw_ref[:, sl]
            dyg_parts.append(r * (gg - n * jnp.mean(gg * n, axis=-1, keepdims=True)))
        dyg = jnp.concatenate(dyg_parts, axis=1)
        gnw = jnp.concatenate(gnw_parts, axis=1)
        dz_ref[...] = (dyg * y * dsz).astype(BF16)
        dypre = dyg * sz

        ddt = jnp.zeros((L, L), F32)
        dacs = jnp.zeros((L, L), F32)
        dacs_t = jnp.zeros((L, L), F32)
        dalast = jnp.zeros((1, L), F32)
        dskip_g = jnp.zeros((1, L), F32)
        for g in range(2):
            bg = bm[:, 128 * g:128 * g + 128]
            cg = cm[:, 128 * g:128 * g + 128]
            gmat = _dot(cg, bg, "nt")
            dgmat = jnp.zeros((L, L), F32)
            dbg = jnp.zeros((L, L), F32)
            dcg = jnp.zeros((L, L), F32)
            for pp in range(4):
                p = 4 * g + pp
                h0, h1 = 2 * p, 2 * p + 1
                x = xs[:, 128 * p:128 * p + 128]
                dyp = dypre[:, 128 * p:128 * p + 128]
                a0, a1 = acs[:, h0:h0 + 1], acs[:, h1:h1 + 1]
                dtl = jnp.where(lo, dt[:, h0:h0 + 1], dt[:, h1:h1 + 1])
                xdt = x * dtl
                l0 = jnp.exp(jnp.where(tri, a0 - acs_t[h0:h0 + 1, :], NEG_BIG))
                l1 = jnp.exp(jnp.where(tri, a1 - acs_t[h1:h1 + 1, :], NEG_BIG))
                m0, m1 = gmat * l0, gmat * l1
                dskip = jnp.where(lo[0:1], sp_ref[2:3, h0:h0 + 1], sp_ref[2:3, h1:h1 + 1])
                s0, s1 = _half_row_sums(dyp * x, lo)
                dskip_g = dskip_g + jnp.where(li1 == h0, _total(s0), 0.0) + jnp.where(li1 == h1, _total(s1), 0.0)
                dx = dyp * dskip
                dy0, dy1 = jnp.where(lo, dyp, 0.0), jnp.where(lo, 0.0, dyp)
                x0, x1 = jnp.where(lo, xdt, 0.0), jnp.where(lo, 0.0, xdt)
                dm0, dm1 = _dot(dy0, x0, "nt"), _dot(dy1, x1, "nt")
                dxdt = _dot(m0, dy0, "tn") + _dot(m1, dy1, "tn")
                q0, q1 = dm0 * m0, dm1 * m1
                dacs = dacs + jnp.where(li == h0, jnp.sum(q0, axis=1, keepdims=True), 0.0) \
                            + jnp.where(li == h1, jnp.sum(q1, axis=1, keepdims=True), 0.0)
                dacs_t = dacs_t - jnp.where(ri == h0, jnp.sum(q0, axis=0, keepdims=True), 0.0) \
                                - jnp.where(ri == h1, jnp.sum(q1, axis=0, keepdims=True), 0.0)
                dgmat = dgmat + dm0 * l0 + dm1 * l1
                hin = st_ref[0, p]
                e = jnp.exp(jnp.where(lo, a0, a1))
                ch = _dot(cg, hin, "nt")
                dch = dyp * e
                dcg = dcg + _dot(dch, hin)
                dhin = _dot(dch, cg, "tn")
                s0, s1 = _half_row_sums(dch * ch, lo)
                dacs = dacs + jnp.where(li == h0, s0, 0.0) + jnp.where(li == h1, s1, 0.0)
                dhout = dstate[p]
                al0, al1 = acs[L - 1:L, h0:h0 + 1], acs[L - 1:L, h1:h1 + 1]
                dec = jnp.exp(jnp.where(lo_rows, al0, al1))
                dhin = dhin + dec * dhout
                dal = dhout * hin * dec
                dal0 = _total(jnp.where(lo_rows, dal, 0.0))
                dal1 = _total(dal) - dal0
                w = jnp.exp(jnp.where(lo, al0 - a0, al1 - a1))
                xw = xdt * w
                dxw = _dot(bg, dhout, "nt")
                dbg = dbg + _dot(xw, dhout)
                dxdt = dxdt + dxw * w
                s0, s1 = _half_row_sums(dxw * xw, lo)
                dacs = dacs - jnp.where(li == h0, s0, 0.0) - jnp.where(li == h1, s1, 0.0)
                dal0, dal1 = dal0 + _total(s0), dal1 + _total(s1)
                dalast = dalast + jnp.where(li1 == h0, dal0, 0.0) + jnp.where(li1 == h1, dal1, 0.0)
                dx = dx + dxdt * dtl
                s0, s1 = _half_row_sums(dxdt * x, lo)
                ddt = ddt + jnp.where(li == h0, s0, 0.0) + jnp.where(li == h1, s1, 0.0)
                dxs_buf[:, 128 * p:128 * p + 128] = dx
                dstate[p] = dhin
            dcg = dcg + _dot(dgmat, bg)
            dbg = dbg + _dot(dgmat, cg, "tn")
            dbm_buf[:, 128 * g:128 * g + 128] = dbg
            dcm_buf[:, 128 * g:128 * g + 128] = dcg

        dacs_tot = dacs + dacs_t.T + jnp.where(ri == L - 1, dalast, 0.0)
        dstep = _rev_cumsum_rows(dacs_tot)
        ddt = ddt + dstep * a
        head_lane = li < N_HEADS
        ddt_pre = jnp.where(head_lane, ddt * _sigmoid(dt_pre), 0.0)
        dsmall_ref[...] = ddt_pre
        da = jnp.sum(jnp.where(head_lane, dstep * dt, 0.0), axis=0, keepdims=True)
        gsp = _stack_rows([jnp.sum(ddt_pre, axis=0, keepdims=True), da * a, dskip_g], L)

        def conv_back(dpost, ds, ext, w_ref, carry, out_ref, width):
            dpre = dpost * ds
            dext = jnp.concatenate([dpre, carry[...]], axis=0)
            out_ref[...] = _conv_rows_transposed(dext, w_ref[...], SSD_CONV)[:L].astype(BF16)
            carry[...] = dpre[0:8]
            rows = [jnp.sum(dpre * _shift_down(ext, SSD_CONV - 1 - k)[8:], axis=0, keepdims=True) for k in range(SSD_CONV)]
            rows.append(jnp.sum(dpre, axis=0, keepdims=True))
            return _stack_rows(rows, width)

        gwx = conv_back(dxs_buf[...], xs_ds, xs_ext, wx_ref, carry_x, dxs_ref, 1024)
        gwb = conv_back(dbm_buf[...], b_ds, b_ext, wb_ref, carry_b, db_ref, 256)
        gwc = conv_back(dcm_buf[...], c_ds, c_ext, wc_ref, carry_c, dc_ref, 256)

        @pl.when(start)
        def _():
            gwx_ref[...] = gwx
            gwb_ref[...] = gwb
            gwc_ref[...] = gwc
            gsp_ref[...] = gsp
            gnw_ref[...] = gnw

        @pl.when(step > 0)
        def _():
            gwx_ref[...] += gwx
            gwb_ref[...] += gwb
            gwc_ref[...] += gwc
            gsp_ref[...] += gsp
            gnw_ref[...] += gnw

    def ch(c):
        return nc - 1 - c

    row = pl.BlockSpec((L, 1024), lambda c: (ch(c), 0))
    row256 = pl.BlockSpec((L, 256), lambda c: (ch(c), 0))
    in_specs = _ssd_in_specs(rev_nc=nc) + [row, pl.BlockSpec((1, N_PAIRS, 128, 128), lambda c: (ch(c), 0, 0, 0)), row]
    out_specs = [row, row, row256, row256, pl.BlockSpec((L, 128), lambda c: (ch(c), 0)),
                 pl.BlockSpec((8, 1024), lambda c: (0, 0)), pl.BlockSpec((8, 256), lambda c: (0, 0)),
                 pl.BlockSpec((8, 256), lambda c: (0, 0)), pl.BlockSpec((8, 128), lambda c: (0, 0)),
                 pl.BlockSpec((1, 1024), lambda c: (0, 0))]
    out_shape = [jax.ShapeDtypeStruct((s, 1024), BF16), jax.ShapeDtypeStruct((s, 1024), BF16),
                 jax.ShapeDtypeStruct((s, 256), BF16), jax.ShapeDtypeStruct((s, 256), BF16),
                 jax.ShapeDtypeStruct((s, 128), F32),
                 jax.ShapeDtypeStruct((8, 1024), F32), jax.ShapeDtypeStruct((8, 256), F32),
                 jax.ShapeDtypeStruct((8, 256), F32), jax.ShapeDtypeStruct((8, 128), F32),
                 jax.ShapeDtypeStruct((1, 1024), F32)]
    scratch = [pltpu.VMEM((N_PAIRS, 128, 128), F32), pltpu.VMEM((8, 1024), F32), pltpu.VMEM((8, 256), F32),
               pltpu.VMEM((8, 256), F32), pltpu.VMEM((L, 1024), F32), pltpu.VMEM((L, 256), F32), pltpu.VMEM((L, 256), F32)]
    return pl.pallas_call(
        body, name="ssd_bwd", grid=(nc,), in_specs=in_specs, out_specs=out_specs, out_shape=out_shape,
        scratch_shapes=scratch, compiler_params=_params(("arbitrary",)),
    )(proj, proj, proj, proj, proj, proj, proj, conv_w8, conv_w8, conv_w8, conv_b, conv_b, conv_b, small, smallp, norm_w,
      ypre, states, dy)


FOX_SCALE = HEAD_DIM ** -0.5
FOX_TQ = 256
FOX_TK = 256
Q_COL, K_COL, V_COL = 2, 3, 4


def _split3_dot(v, m):
    hi = v.astype(BF16)
    r1 = v - hi.astype(F32)
    mid = r1.astype(BF16)
    lo = (r1 - mid.astype(F32)).astype(BF16)
    return _dot(hi, m) + _dot(mid, m) + _dot(lo, m)


def _head_rstd(x, sel_ref, selt_ref):
    ms = _split3_dot(x * x, sel_ref[...]) * (1.0 / HEAD_DIM)
    return _split3_dot(lax.rsqrt(ms + NORM_EPS), selt_ref[...])


def fox_prep(proj, small, smallp, qw, kw, sel, selt, *, tm=256):
    s = proj.shape[0]

    def body(q_ref, k_ref, small_ref, sp_ref, qw_ref, kw_ref, sel_ref, selt_ref, qn_ref, kn_ref, cum_ref, cumt_ref, carry):
        @pl.when(pl.program_id(0) == 0)
        def _():
            carry[...] = jnp.zeros_like(carry)

        q = q_ref[...]
        qn_ref[...] = (((q * _head_rstd(q, sel_ref, selt_ref)) * qw_ref[...]) * FOX_SCALE).astype(BF16)
        k = k_ref[...]
        kn_ref[...] = ((k * _head_rstd(k, sel_ref, selt_ref)) * kw_ref[...]).astype(BF16)
        li = _lane_iota((tm, 128))
        f_lane = jnp.logical_and(li >= F_LANE, li < F_LANE + N_HEADS)
        logf = jnp.where(f_lane, -_softplus(-(small_ref[...] + sp_ref[3:4, :])), 0.0)
        cum = _cumsum_rows(logf) + carry[...]
        carry[...] = cum[tm - 1:tm, :]
        cum_ref[...] = cum
        cumt_ref[...] = cum.T

    row = pl.BlockSpec((tm, 1024), lambda i: (i, 0))
    vec = pl.BlockSpec((1, 1024), lambda i: (0, 0))
    return pl.pallas_call(
        body, name="fox_prep", grid=(s // tm,),
        in_specs=[pl.BlockSpec((tm, 1024), lambda i: (i, Q_COL)), pl.BlockSpec((tm, 1024), lambda i: (i, K_COL)),
                  pl.BlockSpec((tm, 128), lambda i: (i, 0)), pl.BlockSpec((8, 128), lambda i: (0, 0)), vec, vec,
                  pl.BlockSpec((1024, 128), lambda i: (0, 0)), pl.BlockSpec((128, 1024), lambda i: (0, 0))],
        out_specs=[row, row, pl.BlockSpec((tm, 128), lambda i: (i, 0)), pl.BlockSpec((128, tm), lambda i: (0, i))],
        out_shape=[jax.ShapeDtypeStruct((s, 1024), BF16), jax.ShapeDtypeStruct((s, 1024), BF16),
                   jax.ShapeDtypeStruct((s, 128), F32), jax.ShapeDtypeStruct((128, s), F32)],
        scratch_shapes=[pltpu.VMEM((1, 128), F32)], compiler_params=_params(("arbitrary",)),
    )(proj, proj, small, smallp, qw, kw, sel, selt)


def _pick_lane(block, lane):
    return jnp.sum(jnp.where(_lane_iota(block.shape) == lane, block, 0.0), axis=1, keepdims=True)


def _pick_row(block, row):
    return jnp.sum(jnp.where(_row_iota(block.shape) == row, block, 0.0), axis=0, keepdims=True)


def _causal_scores(qb, kb, cq, ck, q0, k0):
    sc = _dot(qb, kb, "nt") + cq - ck
    visible = (q0 + _row_iota(sc.shape)) >= (k0 + _lane_iota(sc.shape))
    return jnp.where(visible, sc, NEG_BIG)


def fox_fwd(qn, kn, proj, cum, cumt):
    s = qn.shape[0]
    tq, tk = FOX_TQ, FOX_TK
    nq = s // tq

    def body(q_ref, k_ref, v_ref, cum_ref, cumt_ref, o_ref, lse_ref, crow):
        p = pl.program_id(0)

        @pl.when(p == 0)
        def _():
            lse_ref[...] = jnp.zeros_like(lse_ref)

        lo = _lane_iota((tq, 128)) < HEAD_DIM
        for hh in range(2):
            h = 2 * p + hh
            mine = lo if hh == 0 else jnp.logical_not(lo)
            crow[hh:hh + 1, :] = _pick_row(cumt_ref[...], F_LANE + h)

            def q_loop(qi, _, hh=hh, h=h, mine=mine):
                q0 = pl.multiple_of(qi * tq, tq)
                qb = jnp.where(mine, q_ref[pl.ds(q0, tq), :], 0.0)
                cq = _pick_lane(cum_ref[pl.ds(q0, tq), :], F_LANE + h)

                def k_loop(kj, carry):
                    m, l, acc = carry
                    k0 = pl.multiple_of(kj * tk, tk)
                    sc = _causal_scores(qb, k_ref[pl.ds(k0, tk), :], cq, crow[hh:hh + 1, pl.ds(k0, tk)], q0, k0)
                    m_new = jnp.maximum(m, jnp.max(sc, axis=1, keepdims=True))
                    alpha = jnp.exp(m - m_new)
                    pe = jnp.exp(sc - m_new)
                    l = alpha * l + jnp.sum(pe, axis=1, keepdims=True)
                    acc = alpha * acc + _dot(pe, v_ref[pl.ds(k0, tk), :])
                    return m_new, l, acc

                init = (jnp.full((tq, 1), NEG_BIG, F32), jnp.zeros((tq, 1), F32), jnp.zeros((tq, 128), F32))
                m, l, acc = lax.fori_loop(0, qi + 1, k_loop, init)
                out = (acc / l).astype(BF16)
                if hh == 0:
                    o_ref[pl.ds(q0, tq), :] = out
                else:
                    o_ref[pl.ds(q0, tq), :] = jnp.where(lo, o_ref[pl.ds(q0, tq), :], out)
                old = lse_ref[pl.ds(q0, tq), :]
                lse_ref[pl.ds(q0, tq), :] = jnp.where(_lane_iota((tq, 128)) == h, m + jnp.log(l), old)
                return 0

            lax.fori_loop(0, nq, q_loop, 0)

    pair = lambda col0: pl.BlockSpec((s, 128), lambda p: (0, col0 + p))
    full = pl.BlockSpec((s, 128), lambda p: (0, 0))
    return pl.pallas_call(
        body, name="fox_fwd", grid=(N_PAIRS,),
        in_specs=[pair(0), pair(0), pair(V_COL * 8), full, pl.BlockSpec((128, s), lambda p: (0, 0))],
        out_specs=[pair(0), full],
        out_shape=[jax.ShapeDtypeStruct((s, 1024), BF16), jax.ShapeDtypeStruct((s, 128), F32)],
        scratch_shapes=[pltpu.VMEM((8, s), F32)], compiler_params=_params(("arbitrary",)),
    )(qn, kn, proj, cum, cumt)


def fox_bwd(qn, kn, proj, cum, cumt, lse, dmixed):
    s = qn.shape[0]
    tq, tk = FOX_TQ, FOX_TK
    nq = s // tq

    def body(q_ref, k_ref, v_ref, cum_ref, cumt_ref, lse_ref, do_ref, dq_ref, dk_ref, dv_ref, dcumt_ref,
             crow, dcs, pm_scr, dp_scr):
        p = pl.program_id(0)

        @pl.when(p == 0)
        def _():
            dcumt_ref[...] = jnp.zeros_like(dcumt_ref)

        dk_ref[...] = jnp.zeros_like(dk_ref)
        dv_ref[...] = jnp.zeros_like(dv_ref)
        dcs[...] = jnp.zeros_like(dcs)
        lo = _lane_iota((tq, 128)) < HEAD_DIM
        for hh in range(2):
            h = 2 * p + hh
            mine = lo if hh == 0 else jnp.logical_not(lo)
            crow[hh:hh + 1, :] = _pick_row(cumt_ref[...], F_LANE + h)

            def q_loop(qi, _, hh=hh, h=h, mine=mine):
                q0 = pl.multiple_of(qi * tq, tq)
                qb = jnp.where(mine, q_ref[pl.ds(q0, tq), :], 0.0)
                cq = _pick_lane(cum_ref[pl.ds(q0, tq), :], F_LANE + h)
                lse_q = _pick_lane(lse_ref[pl.ds(q0, tq), :], h)
                dob = jnp.where(mine, do_ref[pl.ds(q0, tq), :], 0.0).astype(BF16)

                def pass1(kj, delta):
                    k0 = pl.multiple_of(kj * tk, tk)
                    kb = jnp.where(mine, k_ref[pl.ds(k0, tk), :], 0.0)
                    vb = jnp.where(mine, v_ref[pl.ds(k0, tk), :], 0.0)
                    pm = jnp.exp(_causal_scores(qb, kb, cq, crow[hh:hh + 1, pl.ds(k0, tk)], q0, k0) - lse_q)
                    dp = _dot(dob, vb, "nt")
                    pm_scr[kj] = pm
                    dp_scr[kj] = dp
                    return delta + jnp.sum(pm * dp, axis=1, keepdims=True)

                delta = lax.fori_loop(0, qi + 1, pass1, jnp.zeros((tq, 1), F32))

                def pass2(kj, dq):
                    k0 = pl.multiple_of(kj * tk, tk)
                    kb = jnp.where(mine, k_ref[pl.ds(k0, tk), :], 0.0)
                    pm = pm_scr[kj]
                    ds = pm * (dp_scr[kj] - delta)
                    dk_ref[pl.ds(k0, tk), :] += _dot(ds, qb, "tn")
                    dv_ref[pl.ds(k0, tk), :] += _dot(pm, dob, "tn")
                    dcs[hh:hh + 1, pl.ds(k0, tk)] += jnp.sum(ds, axis=0, keepdims=True)
                    return dq + _dot(ds, kb)

                dq = lax.fori_loop(0, qi + 1, pass2, jnp.zeros((tq, 128), F32))
                if hh == 0:
                    dq_ref[pl.ds(q0, tq), :] = dq
                else:
                    dq_ref[pl.ds(q0, tq), :] += dq
                return 0

            lax.fori_loop(0, nq, q_loop, 0)
            dcumt_ref[...] = jnp.where(_row_iota((128, s)) == F_LANE + h, -dcs[hh:hh + 1, :], dcumt_ref[...])

    pair = lambda col0: pl.BlockSpec((s, 128), lambda p: (0, col0 + p))
    full = pl.BlockSpec((s, 128), lambda p: (0, 0))
    wide = pl.BlockSpec((128, s), lambda p: (0, 0))
    return pl.pallas_call(
        body, name="fox_bwd", grid=(N_PAIRS,),
        in_specs=[pair(0), pair(0), pair(V_COL * 8), full, wide, full, pair(8)],
        out_specs=[pair(0), pair(0), pair(0), wide],
        out_shape=[jax.ShapeDtypeStruct((s, 1024), F32), jax.ShapeDtypeStruct((s, 1024), F32),
                   jax.ShapeDtypeStruct((s, 1024), F32), jax.ShapeDtypeStruct((128, s), F32)],
        scratch_shapes=[pltpu.VMEM((8, s), F32), pltpu.VMEM((8, s), F32),
                        pltpu.VMEM((nq, tq, tk), F32), pltpu.VMEM((nq, tq, tk), F32)],
        compiler_params=_params(("arbitrary",)),
    )(qn, kn, proj, cum, cumt, lse, dmixed)


def fox_post(dqn, dkn, proj, small, smallp, qw, kw, sel, selt, dcumt, *, tm=256):
    s = proj.shape[0]
    nrow = s // tm

    def body(dqn_ref, dkn_ref, q_ref, k_ref, small_ref, sp_ref, qw_ref, kw_ref, sel_ref, selt_ref, dcumt_ref,
             dq_ref, dk_ref, dsmall_ref, gqw_ref, gkw_ref, gfb_ref, carry):
        step = pl.program_id(0)

        @pl.when(step == 0)
        def _():
            carry[...] = jnp.zeros_like(carry)

        def norm_bwd(x_ref, w_ref, dn, out_ref):
            x = x_ref[...]
            rf = _head_rstd(x, sel_ref, selt_ref)
            xh = x * rf
            g = dn * w_ref[...]
            mean_gx = _split3_dot(_split3_dot(g * xh, sel_ref[...]) * (1.0 / HEAD_DIM), selt_ref[...])
            out_ref[...] = (rf * (g - xh * mean_gx)).astype(BF16)
            return jnp.sum(dn * xh, axis=0, keepdims=True)

        gqw = norm_bwd(q_ref, qw_ref, dqn_ref[...] * FOX_SCALE, dq_ref)
        gkw = norm_bwd(k_ref, kw_ref, dkn_ref[...], dk_ref)
        li = _lane_iota((tm, 128))
        f_lane = jnp.logical_and(li >= F_LANE, li < F_LANE + N_HEADS)
        dlogf = _rev_cumsum_rows(dcumt_ref[...].T) + carry[...]
        carry[...] = dlogf[0:1, :]
        dfr = jnp.where(f_lane, dlogf * _sigmoid(-(small_ref[...] + sp_ref[3:4, :])), 0.0)
        dsmall_ref[...] = dfr
        gfb = jnp.sum(dfr, axis=0, keepdims=True)

        @pl.when(step == 0)
        def _():
            gqw_ref[...] = gqw
            gkw_ref[...] = gkw
            gfb_ref[...] = gfb

        @pl.when(step > 0)
        def _():
            gqw_ref[...] += gqw
            gkw_ref[...] += gkw
            gfb_ref[...] += gfb

    def rb(i):
        return nrow - 1 - i

    row = pl.BlockSpec((tm, 1024), lambda i: (rb(i), 0))
    vec = pl.BlockSpec((1, 1024), lambda i: (0, 0))
    return pl.pallas_call(
        body, name="fox_post", grid=(nrow,),
        in_specs=[row, row, pl.BlockSpec((tm, 1024), lambda i: (rb(i), Q_COL)), pl.BlockSpec((tm, 1024), lambda i: (rb(i), K_COL)),
                  pl.BlockSpec((tm, 128), lambda i: (rb(i), 0)), pl.BlockSpec((8, 128), lambda i: (0, 0)), vec, vec,
                  pl.BlockSpec((1024, 128), lambda i: (0, 0)), pl.BlockSpec((128, 1024), lambda i: (0, 0)),
                  pl.BlockSpec((128, tm), lambda i: (0, rb(i)))],
        out_specs=[row, row, pl.BlockSpec((tm, 128), lambda i: (rb(i), 0)), vec, vec, pl.BlockSpec((1, 128), lambda i: (0, 0))],
        out_shape=[jax.ShapeDtypeStruct((s, 1024), BF16), jax.ShapeDtypeStruct((s, 1024), BF16),
                   jax.ShapeDtypeStruct((s, 128), F32), jax.ShapeDtypeStruct((1, 1024), F32),
                   jax.ShapeDtypeStruct((1, 1024), F32), jax.ShapeDtypeStruct((1, 128), F32)],
        scratch_shapes=[pltpu.VMEM((1, 128), F32)], compiler_params=_params(("arbitrary",)),
    )(dqn, dkn, proj, proj, small, smallp, qw, kw, sel, selt, dcumt)


def local_step(x, target, wm, ws, w_out, w_up, w_down, ssd_cw8, ssd_cb, smallp, ssd_nw, qw_t, kw_t, sel, selt,
               norm_mix_w, norm_ffn_w, ffn_cw8, ffn_cb):
    h = rms_fwd(x, norm_mix_w, name="rms_mix_fwd")
    proj = matmul(h, wm, mode="nn", tm=1024, tn=512, tk=1024, out_dtype=F32, name="mm_in_proj")
    small = matmul(h, ws, mode="nn", tm=1024, tn=128, tk=1024, out_dtype=F32, name="mm_in_proj_small")
    y_ssd, ypre, states = ssd_fwd(proj, small, ssd_cw8, ssd_cb, smallp, ssd_nw)
    qn, kn, cum, cumt = fox_prep(proj, small, smallp, qw_t, kw_t, sel, selt)
    y_fox, lse = fox_fwd(qn, kn, proj, cum, cumt)
    x1 = matmul(y_ssd, w_out, mode="nn", tm=512, tn=1024, tk=1024, out_dtype=F32, name="mm_out_ssd", add=x)
    x1 = matmul(y_fox, w_out, mode="nn", tm=512, tn=1024, tk=1024, out_dtype=F32, name="mm_out_fox", add=x1, b_koff=1)
    hf = rms_fwd(x1, norm_ffn_w, name="rms_ffn_fwd")
    hu = matmul(hf, w_up, mode="nn", tm=1024, tn=512, tk=1024, out_dtype=F32, name="mm_up")
    act = ffn_mid_fwd(hu, ffn_cw8, ffn_cb)
    y = matmul(act, w_down, mode="nn", tm=512, tn=1024, tk=1408, out_dtype=F32, name="mm_down", add=x1)
    dy, sq = loss_head(y, target)

    dact = matmul(dy, w_down, mode="nt", tm=512, tn=1408, tk=1024, out_dtype=F32, name="mm_dact")
    g_down = matmul(act, dy, mode="tn", tm=1408, tn=1024, tk=512, out_dtype=BF16, name="mm_dw_down")
    dhu_g, dhu_v, gcw_g, gcw_v = ffn_mid_bwd(hu, dact, ffn_cw8, ffn_cb)
    dhf = matmul(dhu_g, w_up, mode="nt", tm=512, tn=1024, tk=1408, out_dtype=F32, name="mm_dhf_gate")
    dhf = matmul(dhu_v, w_up, mode="nt", tm=512, tn=1024, tk=1408, out_dtype=F32, name="mm_dhf_val", add=dhf, b_koff=2)
    g_up_g = matmul(hf, dhu_g, mode="tn", tm=1024, tn=1408, tk=512, out_dtype=BF16, name="mm_dw_up_gate")
    g_up_v = matmul(hf, dhu_v, mode="tn", tm=1024, tn=1408, tk=512, out_dtype=BF16, name="mm_dw_up_val")
    dx1, g_norm_ffn = rms_bwd(dhf, x1, norm_ffn_w, dy, name="rms_ffn_bwd")
    dmixed = matmul(dx1, w_out, mode="nt", tm=512, tn=1024, tk=1024, out_dtype=F32, name="mm_dmixed")
    g_out_a = matmul(y_ssd, dx1, mode="tn", tm=1024, tn=1024, tk=512, out_dtype=BF16, name="mm_dw_out_ssd")
    g_out_b = matmul(y_fox, dx1, mode="tn", tm=1024, tn=1024, tk=512, out_dtype=BF16, name="mm_dw_out_fox")
    dz, dxs, db, dc, dsmall_ssd, gcw_x, gcw_b, gcw_c, g_sp, g_ssd_nw = ssd_bwd(
        proj, small, ssd_cw8, ssd_cb, smallp, ssd_nw, ypre, states, dmixed)
    dqn, dkn, dv, dcumt = fox_bwd(qn, kn, proj, cum, cumt, lse, dmixed)
    dq, dk, dsmall_fox, g_qw, g_kw, g_fb = fox_post(dqn, dkn, proj, small, smallp, qw_t, kw_t, sel, selt, dcumt)
    dproj = jnp.concatenate([dz, dxs, dq, dk, dv.astype(BF16), db, dc], axis=1)
    dsmall = (dsmall_ssd + dsmall_fox).astype(BF16)
    dh = matmul(dproj, wm, mode="nt", tm=512, tn=1024, tk=512, out_dtype=F32, name="mm_dh")
    dh = matmul(dsmall, ws, mode="nt", tm=512, tn=1024, tk=128, out_dtype=F32, name="mm_dh_small", add=dh)
    g_wm = matmul(h, dproj, mode="tn", tm=1024, tn=1408, tk=512, out_dtype=BF16, name="mm_dw_in")
    g_ws = matmul(h, dsmall, mode="tn", tm=1024, tn=128, tk=512, out_dtype=BF16, name="mm_dw_in_small")
    grad_x, g_norm_mix = rms_bwd(dh, x, norm_mix_w, dx1, name="rms_mix_bwd")
    return dict(
        sq=sq, grad_x=grad_x, g_wm=g_wm, g_ws=g_ws, g_out=jnp.concatenate([g_out_a, g_out_b], axis=0),
        g_up=jnp.concatenate([g_up_g, g_up_v], axis=1), g_down=g_down,
        g_norm_mix=g_norm_mix, g_norm_ffn=g_norm_ffn, g_ssd_nw=g_ssd_nw,
        g_ssd_cw=jnp.concatenate([gcw_x, gcw_b, gcw_c], axis=1), g_sp=g_sp, g_fb=g_fb, g_qw=g_qw, g_kw=g_kw,
        g_ffn_cw=jnp.concatenate([gcw_g, gcw_v], axis=1))


def adamw(w, g, m, v, *, name, tr):
    rows, cols = w.shape

    def body(w_ref, g_ref, m_ref, v_ref, d_ref, mo_ref, vo_ref):
        gv = g_ref[...]
        mn = ADAM_B1 * m_ref[...] + (1.0 - ADAM_B1) * gv
        vn = ADAM_B2 * v_ref[...] + (1.0 - ADAM_B2) * (gv * gv)
        m_hat = mn / (1.0 - ADAM_B1 ** ADAM_STEP)
        v_hat = vn / (1.0 - ADAM_B2 ** ADAM_STEP)
        d_ref[...] = -ADAM_LR * (m_hat / (jnp.sqrt(v_hat) + ADAM_EPS) + ADAM_WD * w_ref[...])
        mo_ref[...] = mn
        vo_ref[...] = vn

    blk = pl.BlockSpec((tr, cols), lambda i: (i, 0))
    shp = jax.ShapeDtypeStruct((rows, cols), F32)
    return pl.pallas_call(
        body, name=name, grid=(rows // tr,), in_specs=[blk] * 4, out_specs=[blk] * 3, out_shape=[shp] * 3,
        compiler_params=_params(("parallel",)),
    )(w, g, m, v)


def add_pair(a, b, *, name, tr):
    _, rows, cols = a.shape

    def body(a_ref, b_ref, o_ref):
        o_ref[...] = (a_ref[...].astype(F32) + b_ref[...].astype(F32)).astype(BF16)

    blk = pl.BlockSpec((1, tr, cols), lambda j, i: (j, i, 0))
    return pl.pallas_call(
        body, name=name, grid=(4, rows // tr), in_specs=[blk, blk], out_specs=blk,
        out_shape=jax.ShapeDtypeStruct(a.shape, BF16), compiler_params=_params(("parallel", "parallel")),
    )(a, b)


def sum_chips(parts, *, name, tr):
    _, rows, cols = parts.shape

    def body(p_ref, o_ref):
        acc = p_ref[0].astype(F32)
        for k in range(1, 4):
            acc = acc + p_ref[k].astype(F32)
        o_ref[...] = acc

    return pl.pallas_call(
        body, name=name, grid=(rows // tr,), in_specs=[pl.BlockSpec((4, tr, cols), lambda i: (0, i, 0))],
        out_specs=pl.BlockSpec((tr, cols), lambda i: (i, 0)), out_shape=jax.ShapeDtypeStruct((rows, cols), F32),
        compiler_params=_params(("parallel",)),
    )(parts)


ANY = pl.BlockSpec(memory_space=pl.ANY)


def _place():
    x, y, c = lax.axis_index("x"), lax.axis_index("y"), lax.axis_index("c")
    chips = [(1 - x, y), (x, 1 - y), (1 - x, 1 - y)]
    return x, y, c, chips


def _chunks(rows):
    size = next((c for c in (128, 176, 64, 32, 16, 8) if rows % c == 0), rows)
    return [(r, size) for r in range(0, rows, size)]


def gather_weights(shards):
    n = len(shards)

    def body(*refs):
        ins, outs = refs[:n], refs[n:2 * n]
        send_sems, recv_sems = refs[2 * n:]
        x, y, c, chips = _place()
        me = 2 * x + y
        sibling = (x, y, 1 - c)

        def half(a, blk, r=0, nr=None):
            rows = ins[a].shape[0] // 2
            return outs[a].at[blk, pl.ds(c * rows + r, rows if nr is None else nr), :]

        def to_chip(a, t, r=0, nr=None):
            rows = ins[a].shape[0] // 2
            return pltpu.make_async_remote_copy(
                src_ref=ins[a].at[pl.ds(c * rows + r, rows if nr is None else nr), :], dst_ref=half(a, me, r, nr),
                send_sem=send_sems.at[a, t], recv_sem=recv_sems.at[a, t],
                device_id=(*chips[t], c), device_id_type=MESH)

        def from_chip(a, t):
            blk = 2 * chips[t][0] + chips[t][1]
            return pltpu.make_async_remote_copy(
                src_ref=half(a, blk), dst_ref=half(a, blk), send_sem=send_sems.at[a, t], recv_sem=recv_sems.at[a, t],
                device_id=(*chips[t], c), device_id_type=MESH)

        def to_sibling(a, t, r=0, nr=None):
            blk = 2 * chips[t][0] + chips[t][1]
            return pltpu.make_async_remote_copy(
                src_ref=half(a, blk, r, nr), dst_ref=half(a, blk, r, nr), send_sem=send_sems.at[a, 3 + t],
                recv_sem=recv_sems.at[a, 3 + t], device_id=sibling, device_id_type=MESH)

        def from_sibling(a, t):
            blk = 2 * chips[t][0] + chips[t][1]
            rows = ins[a].shape[0] // 2
            dst = outs[a].at[blk, pl.ds((1 - c) * rows, rows), :]
            return pltpu.make_async_remote_copy(
                src_ref=dst, dst_ref=dst, send_sem=send_sems.at[a, 3 + t], recv_sem=recv_sems.at[a, 3 + t],
                device_id=sibling, device_id_type=MESH)

        for a in range(n):
            for t in range(3):
                for r, nr in _chunks(ins[a].shape[0] // 2):
                    to_chip(a, t, r, nr).start()
        for a in range(n):
            for t in range(3):
                from_chip(a, t).wait_recv()
                for r, nr in _chunks(ins[a].shape[0] // 2):
                    to_sibling(a, t, r, nr).start()
        for a in range(n):
            for t in range(3):
                from_sibling(a, t).wait_recv()
        for a in range(n):
            for t in range(3):
                to_chip(a, t).wait_send()
                to_sibling(a, t).wait_send()

    chip = 2 * lax.axis_index("x") + lax.axis_index("y")
    gathered = pl.pallas_call(
        body, name="gather_weights", in_specs=[ANY] * n, out_specs=[ANY] * n,
        out_shape=[jax.ShapeDtypeStruct((4,) + s.shape, s.dtype) for s in shards],
        scratch_shapes=[pltpu.SemaphoreType.DMA((n, 6)), pltpu.SemaphoreType.DMA((n, 6))],
    )(*shards)
    return [lax.dynamic_update_slice(g, s[None], (chip, 0, 0)) for g, s in zip(gathered, shards)]


def pair_swap_halves(grads):
    n = len(grads)

    def body(*refs):
        ins, theirs = refs[:n], refs[n:2 * n]
        send_sems, recv_sems = refs[2 * n:]
        x, y, c, _ = _place()
        sibling = (x, y, 1 - c)
        for a in range(n):
            rows = ins[a].shape[1] // 2
            for j in range(4):
                for r, nr in _chunks(rows):
                    pltpu.make_async_remote_copy(
                        src_ref=ins[a].at[j, pl.ds((1 - c) * rows + r, nr), :], dst_ref=theirs[a].at[j, pl.ds(r, nr), :],
                        send_sem=send_sems.at[a], recv_sem=recv_sems.at[a], device_id=sibling, device_id_type=MESH).start()
        for a in range(n):
            pltpu.make_async_remote_copy(src_ref=theirs[a], dst_ref=theirs[a], send_sem=send_sems.at[a],
                                         recv_sem=recv_sems.at[a], device_id=sibling, device_id_type=MESH).wait()

    halves = [jax.ShapeDtypeStruct((4, g.shape[1] // 2, g.shape[2]), g.dtype) for g in grads]
    theirs = pl.pallas_call(
        body, name="pair_swap_halves", in_specs=[ANY] * n, out_specs=[ANY] * n, out_shape=halves,
        scratch_shapes=[pltpu.SemaphoreType.DMA((n,)), pltpu.SemaphoreType.DMA((n,))],
    )(*grads)
    c = lax.axis_index("c")
    mine = [lax.dynamic_slice_in_dim(g, c * (g.shape[1] // 2), g.shape[1] // 2, axis=1) for g in grads]
    return mine, theirs


def scatter_to_chips(parts):
    n = len(parts)

    def body(*refs):
        ins, outs = refs[:n], refs[n:2 * n]
        send_sems, recv_sems = refs[2 * n:]
        x, y, c, chips = _place()
        me = 2 * x + y
        blks = [2 * cx + cy for cx, cy in chips]
        for a in range(n):
            for r, nr in _chunks(ins[a].shape[1]):
                for t in range(3):
                    pltpu.make_async_remote_copy(
                        src_ref=ins[a].at[blks[t], pl.ds(r, nr), :], dst_ref=outs[a].at[me, pl.ds(r, nr), :],
                        send_sem=send_sems.at[a, t], recv_sem=recv_sems.at[a, t],
                        device_id=(*chips[t], c), device_id_type=MESH).start()
        for a in range(n):
            for t in range(3):
                pltpu.make_async_remote_copy(
                    src_ref=outs[a].at[blks[t]], dst_ref=outs[a].at[blks[t]], send_sem=send_sems.at[a, t],
                    recv_sem=recv_sems.at[a, t], device_id=(*chips[t], c), device_id_type=MESH).wait()

    landed = pl.pallas_call(
        body, name="scatter_to_chips", in_specs=[ANY] * n, out_specs=[ANY] * n,
        out_shape=[jax.ShapeDtypeStruct(p.shape, p.dtype) for p in parts],
        scratch_shapes=[pltpu.SemaphoreType.DMA((n, 3)), pltpu.SemaphoreType.DMA((n, 3))],
    )(*parts)
    chip = 2 * lax.axis_index("x") + lax.axis_index("y")
    return [lax.dynamic_update_slice(l, lax.dynamic_slice_in_dim(p, chip, 1, axis=0), (chip, 0, 0))
            for l, p in zip(landed, parts)]


def pair_join_halves(halves):
    n = len(halves)

    def body(*refs):
        ins, outs = refs[:n], refs[n:2 * n]
        send_sems, recv_sems = refs[2 * n:]
        x, y, c, _ = _place()
        sibling = (x, y, 1 - c)
        for a in range(n):
            rows = ins[a].shape[0]
            for r, nr in _chunks(rows):
                pltpu.make_async_remote_copy(
                    src_ref=ins[a].at[pl.ds(r, nr), :], dst_ref=outs[a].at[pl.ds(c * rows + r, nr), :],
                    send_sem=send_sems.at[a], recv_sem=recv_sems.at[a], device_id=sibling, device_id_type=MESH).start()
        for a in range(n):
            rows = ins[a].shape[0]
            got = outs[a].at[pl.ds((1 - c) * rows, rows), :]
            pltpu.make_async_remote_copy(src_ref=ins[a], dst_ref=got, send_sem=send_sems.at[a], recv_sem=recv_sems.at[a],
                                         device_id=sibling, device_id_type=MESH).wait()

    joined = pl.pallas_call(
        body, name="pair_join_halves", in_specs=[ANY] * n, out_specs=[ANY] * n,
        out_shape=[jax.ShapeDtypeStruct((2 * h.shape[0], h.shape[1]), h.dtype) for h in halves],
        scratch_shapes=[pltpu.SemaphoreType.DMA((n,)), pltpu.SemaphoreType.DMA((n,))],
    )(*halves)
    c = lax.axis_index("c")
    return [lax.dynamic_update_slice(j, h, (c * h.shape[0], 0)) for j, h in zip(joined, halves)]


def allreduce_small(packed):
    rows = packed.shape[0]

    def body(in_ref, out_ref, gathered, send_sems, recv_sems):
        x, y, c, _ = _place()
        me = 4 * x + 2 * y + c
        gathered[me] = in_ref[...]
        flips = [(fx, fy, fc) for fx in (0, 1) for fy in (0, 1) for fc in (0, 1)][1:]
        peers = [((1 - x) if fx else x, (1 - y) if fy else y, (1 - c) if fc else c) for fx, fy, fc in flips]
        copies = []
        for t, peer in enumerate(peers):
            cp = pltpu.make_async_remote_copy(
                src_ref=in_ref, dst_ref=gathered.at[me], send_sem=send_sems.at[t], recv_sem=recv_sems.at[t],
                device_id=peer, device_id_type=MESH)
            cp.start()
            copies.append(cp)
        for t, (px, py, pc) in enumerate(peers):
            slot = gathered.at[4 * px + 2 * py + pc]
            pltpu.make_async_remote_copy(
                src_ref=slot, dst_ref=slot, send_sem=send_sems.at[t], recv_sem=recv_sems.at[t],
                device_id=(px, py, pc), device_id_type=MESH).wait_recv()
        for cp in copies:
            cp.wait_send()
        acc = gathered[0]
        for k in range(1, 8):
            acc = acc + gathered[k]
        out_ref[...] = acc

    vm = pl.BlockSpec(memory_space=pltpu.VMEM)
    return pl.pallas_call(
        body, name="allreduce_small", in_specs=[vm], out_specs=vm, out_shape=jax.ShapeDtypeStruct(packed.shape, F32),
        scratch_shapes=[pltpu.VMEM((8, rows, 128), F32), pltpu.SemaphoreType.DMA((7,)), pltpu.SemaphoreType.DMA((7,))],
    )(packed)


SMALL_NAMES = ("norm_mix_w", "ssd_conv_w", "ssd_conv_b", "ssd_dt_bias", "ssd_a_log", "ssd_d", "ssd_norm_w", "fox_f_bias",
               "fox_q_norm_w", "fox_k_norm_w", "norm_ffn_w", "ffn_conv_w", "ffn_conv_b")
BIG_NAMES = ("w_in", "w_out", "w_up", "w_down")
WEIGHT_ORDER = ("norm_mix_w", "w_in", "ssd_conv_w", "ssd_conv_b", "ssd_dt_bias", "ssd_a_log", "ssd_d", "ssd_norm_w",
                "fox_f_bias", "fox_q_norm_w", "fox_k_norm_w", "w_out", "norm_ffn_w", "w_up", "ffn_conv_w", "ffn_conv_b", "w_down")
ADAM_ROWS = {"w_in": 256, "w_out": 256, "w_up": 256, "w_down": 176}


def _pack(arrays):
    rows = []
    for a in arrays:
        flat = a.reshape(-1).astype(F32)
        rows.append(jnp.pad(flat, (0, (-flat.shape[0]) % 1024)).reshape(-1, 128))
    return jnp.concatenate(rows, axis=0)


def _unpack(packed, shapes):
    out, r = [], 0
    for shp in shapes:
        size = 1
        for d in shp:
            size *= d
        nrow = 8 * (-(-size // 1024))
        out.append(packed[r:r + nrow].reshape(-1)[:size].reshape(shp))
        r += nrow
    return out


def _pad_rows(a, rows):
    return jnp.pad(a, ((0, rows - a.shape[0]), (0, 0)))


def kernel(x, norm_mix_w, w_in, ssd_conv_w, ssd_conv_b, ssd_dt_bias, ssd_a_log, ssd_d, ssd_norm_w, fox_f_bias, fox_q_norm_w, fox_k_norm_w, w_out, norm_ffn_w, w_up, ffn_conv_w, ffn_conv_b, w_down, loss_target, m_norm_mix_w, m_w_in, m_ssd_conv_w, m_ssd_conv_b, m_ssd_dt_bias, m_ssd_a_log, m_ssd_d, m_ssd_norm_w, m_fox_f_bias, m_fox_q_norm_w, m_fox_k_norm_w, m_w_out, m_norm_ffn_w, m_w_up, m_ffn_conv_w, m_ffn_conv_b, m_w_down, v_norm_mix_w, v_w_in, v_ssd_conv_w, v_ssd_conv_b, v_ssd_dt_bias, v_ssd_a_log, v_ssd_d, v_ssd_norm_w, v_fox_f_bias, v_fox_q_norm_w, v_fox_k_norm_w, v_w_out, v_norm_ffn_w, v_w_up, v_ffn_conv_w, v_ffn_conv_b, v_w_down):
    w = dict(norm_mix_w=norm_mix_w, w_in=w_in, ssd_conv_w=ssd_conv_w, ssd_conv_b=ssd_conv_b, ssd_dt_bias=ssd_dt_bias,
             ssd_a_log=ssd_a_log, ssd_d=ssd_d, ssd_norm_w=ssd_norm_w, fox_f_bias=fox_f_bias, fox_q_norm_w=fox_q_norm_w,
             fox_k_norm_w=fox_k_norm_w, w_out=w_out, norm_ffn_w=norm_ffn_w, w_up=w_up, ffn_conv_w=ffn_conv_w,
             ffn_conv_b=ffn_conv_b, w_down=w_down)
    m = dict(norm_mix_w=m_norm_mix_w, w_in=m_w_in, ssd_conv_w=m_ssd_conv_w, ssd_conv_b=m_ssd_conv_b, ssd_dt_bias=m_ssd_dt_bias,
             ssd_a_log=m_ssd_a_log, ssd_d=m_ssd_d, ssd_norm_w=m_ssd_norm_w, fox_f_bias=m_fox_f_bias, fox_q_norm_w=m_fox_q_norm_w,
             fox_k_norm_w=m_fox_k_norm_w, w_out=m_w_out, norm_ffn_w=m_norm_ffn_w, w_up=m_w_up, ffn_conv_w=m_ffn_conv_w,
             ffn_conv_b=m_ffn_conv_b, w_down=m_w_down)
    v = dict(norm_mix_w=v_norm_mix_w, w_in=v_w_in, ssd_conv_w=v_ssd_conv_w, ssd_conv_b=v_ssd_conv_b, ssd_dt_bias=v_ssd_dt_bias,
             ssd_a_log=v_ssd_a_log, ssd_d=v_ssd_d, ssd_norm_w=v_ssd_norm_w, fox_f_bias=v_fox_f_bias, fox_q_norm_w=v_fox_q_norm_w,
             fox_k_norm_w=v_fox_k_norm_w, w_out=v_w_out, norm_ffn_w=v_norm_ffn_w, w_up=v_w_up, ffn_conv_w=v_ffn_conv_w,
             ffn_conv_b=v_ffn_conv_b, w_down=v_w_down)
    chip = 2 * lax.axis_index("x") + lax.axis_index("y")

    shards = [w_in[0].astype(BF16), w_out[0].astype(BF16), w_up[0].astype(BF16), w_down[0].astype(BF16),
              _pad_rows(ssd_conv_w[0], 16), _pad_rows(ffn_conv_w[0], 16)]
    a_in, a_out, a_up, a_down, a_scw, a_fcw = gather_weights(shards)
    w_full = a_in.transpose(1, 0, 2).reshape(D_MODEL, IN_COLS)
    wm = jnp.concatenate([w_full[:, :2048], w_full[:, 2576:5648], w_full[:, 2048:2560]], axis=1)
    ws = jnp.concatenate([w_full[:, 2560:2576], w_full[:, 5648:5664], jnp.zeros((D_MODEL, SMALL_COLS - 32), BF16)], axis=1)
    wo = a_out.reshape(2048, D_MODEL)
    wu = a_up.transpose(1, 0, 2).reshape(D_MODEL, 2 * D_FF)
    wd = a_down.reshape(D_FF, D_MODEL)
    ssd_cw8 = a_scw.transpose(1, 0, 2).reshape(16, 1536)[:8]
    ffn_cw8 = a_fcw.transpose(1, 0, 2).reshape(16, 2 * D_FF)[:8]
    smallp = jnp.zeros((8, 128), F32)
    smallp = smallp.at[0, :16].set(ssd_dt_bias[0]).at[1, :16].set(ssd_a_log[0]).at[2, :16].set(ssd_d[0])
    smallp = smallp.at[3, F_LANE:F_LANE + 16].set(fox_f_bias[0])
    qw_t = jnp.tile(fox_q_norm_w[0], N_HEADS)[None]
    kw_t = jnp.tile(fox_k_norm_w[0], N_HEADS)[None]
    sel = (jnp.arange(1024)[:, None] // HEAD_DIM == jnp.arange(128)[None, :]).astype(BF16)

    res = local_step(x[0], loss_target[0], wm, ws, wo, wu, wd, ssd_cw8, ssd_conv_b, smallp, ssd_norm_w, qw_t, kw_t,
                     sel, sel.T, norm_mix_w, norm_ffn_w, ffn_cw8, ffn_conv_b)

    full_shapes = [(1, 1024), (1, 4, 1536), (1, 1536), (1, 16), (1, 16), (1, 16), (1, 1024), (1, 16), (1, 64), (1, 64),
                   (1, 1024), (1, 3, 2 * D_FF), (1, 2 * D_FF), (1,)]
    local_small = [res["g_norm_mix"], res["g_ssd_cw"][:4], res["g_ssd_cw"][4], res["g_sp"][0, :16], res["g_sp"][1, :16],
                   res["g_sp"][2, :16], res["g_ssd_nw"], res["g_fb"][0, F_LANE:F_LANE + 16],
                   res["g_qw"].reshape(N_HEADS, HEAD_DIM).sum(0), res["g_kw"].reshape(N_HEADS, HEAD_DIM).sum(0),
                   res["g_norm_ffn"], res["g_ffn_cw"][:3], res["g_ffn_cw"][3], jnp.sum(res["sq"])]
    summed = _unpack(allreduce_small(_pack(local_small)), full_shapes)
    loss = (0.5 / D_MODEL) * summed[-1][0]
    g_small = dict(zip(SMALL_NAMES, summed[:-1]))
    g_small["ssd_conv_w"] = lax.dynamic_slice(g_small["ssd_conv_w"], (0, 0, 384 * chip), (1, 4, 384))
    g_small["ffn_conv_w"] = lax.dynamic_slice(g_small["ffn_conv_w"], (0, 0, 1408 * chip), (1, 3, 1408))

    g_wm, g_ws = res["g_wm"], res["g_ws"]
    g_in_full = jnp.concatenate([g_wm[:, :2048], g_wm[:, 5120:5632], g_ws[:, :16], g_wm[:, 2048:5120], g_ws[:, 16:32]], axis=1)
    big = [g_in_full.reshape(D_MODEL, 4, 1416).transpose(1, 0, 2), res["g_out"].reshape(4, 512, D_MODEL),
           res["g_up"].reshape(D_MODEL, 4, 1408).transpose(1, 0, 2), res["g_down"].reshape(4, 704, D_MODEL)]
    mine, theirs = pair_swap_halves(big)
    parts = [add_pair(a, b, name="add_pair_" + n, tr=ADAM_ROWS[n]) for a, b, n in zip(mine, theirs, BIG_NAMES)]
    landed = scatter_to_chips(parts)
    halves = [sum_chips(p, name="sum_chips_" + n, tr=ADAM_ROWS[n]) for p, n in zip(landed, BIG_NAMES)]
    g_big = dict(zip(BIG_NAMES, pair_join_halves(halves)))

    grads, deltas, new_m, new_v = {}, {}, {}, {}
    for n in BIG_NAMES:
        d, mn, vn = adamw(w[n][0], g_big[n], m[n][0], v[n][0], name="adamw_" + n, tr=ADAM_ROWS[n])
        grads[n], deltas[n], new_m[n], new_v[n] = g_big[n][None], d[None], mn[None], vn[None]
    shapes = [w[n].shape for n in SMALL_NAMES]
    d, mn, vn = adamw(_pack([w[n] for n in SMALL_NAMES]), _pack([g_small[n] for n in SMALL_NAMES]),
                      _pack([m[n] for n in SMALL_NAMES]), _pack([v[n] for n in SMALL_NAMES]), name="adamw_small", tr=8)
    for n, dd, mm, vv in zip(SMALL_NAMES, _unpack(d, shapes), _unpack(mn, shapes), _unpack(vn, shapes)):
        grads[n], deltas[n], new_m[n], new_v[n] = g_small[n].reshape(w[n].shape), dd, mm, vv
    return (loss, res["grad_x"][None], *[grads[n] for n in WEIGHT_ORDER], *[deltas[n] for n in WEIGHT_ORDER],
            *[new_m[n] for n in WEIGHT_ORDER], *[new_v[n] for n in WEIGHT_ORDER])
```

```python
import functools

import jax
import jax.numpy as jnp
from jax import lax
from jax.experimental import pallas as pl
from jax.experimental.pallas import tpu as pltpu

F32 = jnp.float32
BF16 = jnp.bfloat16
MESH = pl.DeviceIdType.MESH

D_MODEL = 1024
HEAD_DIM = 64
N_HEADS = 16
N_PAIRS = N_HEADS // 2
SSD_CHUNK = 128
SSD_STATE = 128
SSD_CONV = 4
D_FF = 2816
FFN_CONV = 3
NORM_EPS = 1e-6
MAIN_COLS = 5632
SMALL_COLS = 128
F_LANE = 16
IN_COLS = 5664

ADAM_LR = 0.001
ADAM_B1 = 0.9
ADAM_B2 = 0.999
ADAM_EPS = 1e-08
ADAM_WD = 0.01
ADAM_STEP = 10

VMEM_LIMIT_V7X = 56 * 1024 * 1024
NEG_BIG = -1e30


def _params(sem=None):
    return pltpu.CompilerParams(dimension_semantics=sem, vmem_limit_bytes=VMEM_LIMIT_V7X)


def _sigmoid(x):
    return 1.0 / (1.0 + jnp.exp(-x))


def _silu_and_grad(x):
    s = _sigmoid(x)
    return x * s, s * (1.0 + x * (1.0 - s))


def _shift_down(v, j):
    return v if j == 0 else pltpu.roll(v, j, 0)


def _shift_up(v, j):
    return v if j == 0 else pltpu.roll(v, v.shape[0] - j, 0)


def _row_iota(shape):
    return lax.broadcasted_iota(jnp.int32, shape, 0)


def _lane_iota(shape):
    return lax.broadcasted_iota(jnp.int32, shape, 1)


def _dot(a, b, mode="nn"):
    dims = {"nn": (((1,), (0,)), ((), ())), "nt": (((1,), (1,)), ((), ())), "tn": (((0,), (0,)), ((), ()))}[mode]
    return lax.dot_general(a.astype(BF16), b.astype(BF16), dims, preferred_element_type=F32)


def _dot_f32(a, b):
    return jnp.dot(a, b, precision=lax.Precision.HIGHEST, preferred_element_type=F32)


def matmul(a, b, *, mode, tm, tn, tk, out_dtype, name, add=None, b_koff=0):
    if mode == "nn":
        (m, k), n = a.shape, b.shape[1]
    elif mode == "nt":
        (m, k), n = a.shape, b.shape[0]
    else:
        (k, m), n = a.shape, b.shape[1]
    assert m % tm == 0 and n % tn == 0 and k % tk == 0, (name, m, n, k, tm, tn, tk)
    nk = k // tk
    a_spec = pl.BlockSpec((tk, tm), lambda i, j, kk: (kk, i)) if mode == "tn" else pl.BlockSpec((tm, tk), lambda i, j, kk: (i, kk))
    b_spec = (pl.BlockSpec((tn, tk), lambda i, j, kk: (j, kk + b_koff)) if mode == "nt"
              else pl.BlockSpec((tk, tn), lambda i, j, kk: (kk + b_koff, j)))
    o_spec = pl.BlockSpec((tm, tn), lambda i, j, kk: (i, j))
    has_add = add is not None

    def body(*refs):
        if has_add:
            a_ref, b_ref, add_ref, o_ref, acc_ref = refs
        else:
            a_ref, b_ref, o_ref, acc_ref = refs
        kk = pl.program_id(2)
        part = _dot(a_ref[...], b_ref[...], mode)

        def finish(total):
            if has_add:
                total = total + add_ref[...]
            o_ref[...] = total.astype(out_dtype)

        if nk == 1:
            finish(part)
        else:
            @pl.when(kk == 0)
            def _():
                acc_ref[...] = part

            @pl.when(jnp.logical_and(kk > 0, kk < nk - 1))
            def _():
                acc_ref[...] += part

            @pl.when(kk == nk - 1)
            def _():
                finish(acc_ref[...] + part)

    in_specs = [a_spec, b_spec] + ([o_spec] if has_add else [])
    args = (a, b) + ((add,) if has_add else ())
    return pl.pallas_call(
        body, name=name, grid=(m // tm, n // tn, nk), in_specs=in_specs, out_specs=o_spec,
        out_shape=jax.ShapeDtypeStruct((m, n), out_dtype),
        scratch_shapes=[pltpu.VMEM((tm, tn) if nk > 1 else (8, 128), F32)],
        compiler_params=_params(("parallel", "parallel", "arbitrary")),
    )(*args)


def rms_fwd(x, w, *, name, tm=512):
    s, d = x.shape

    def body(x_ref, w_ref, h_ref):
        xv = x_ref[...]
        r = lax.rsqrt(jnp.mean(xv * xv, axis=-1, keepdims=True) + NORM_EPS)
        h_ref[...] = ((xv * r) * w_ref[...]).astype(BF16)

    return pl.pallas_call(
        body, name=name, grid=(s // tm,),
        in_specs=[pl.BlockSpec((tm, d), lambda i: (i, 0)), pl.BlockSpec((1, d), lambda i: (0, 0))],
        out_specs=pl.BlockSpec((tm, d), lambda i: (i, 0)),
        out_shape=jax.ShapeDtypeStruct((s, d), BF16), compiler_params=_params(("parallel",)),
    )(x, w)


def rms_bwd(dh, x, w, resid, *, name, tm=512):
    s, d = x.shape

    def body(dh_ref, x_ref, w_ref, res_ref, dx_ref, dw_ref):
        xv = x_ref[...]
        dhv = dh_ref[...]
        r = lax.rsqrt(jnp.mean(xv * xv, axis=-1, keepdims=True) + NORM_EPS)
        xh = xv * r
        g = dhv * w_ref[...]
        dx_ref[...] = res_ref[...] + r * (g - xh * jnp.mean(g * xh, axis=-1, keepdims=True))
        part = jnp.sum(dhv * xh, axis=0, keepdims=True)

        @pl.when(pl.program_id(0) == 0)
        def _():
            dw_ref[...] = part

        @pl.when(pl.program_id(0) > 0)
        def _():
            dw_ref[...] += part

    row = pl.BlockSpec((tm, d), lambda i: (i, 0))
    vec = pl.BlockSpec((1, d), lambda i: (0, 0))
    return pl.pallas_call(
        body, name=name, grid=(s // tm,), in_specs=[row, row, vec, row], out_specs=[row, vec],
        out_shape=[jax.ShapeDtypeStruct((s, d), F32), jax.ShapeDtypeStruct((1, d), F32)],
        compiler_params=_params(("arbitrary",)),
    )(dh, x, w, resid)


def loss_head(y, target, *, tm=512):
    s, d = y.shape

    def body(y_ref, t_ref, dy_ref, sq_ref):
        e = y_ref[...] - t_ref[...]
        dy_ref[...] = e / float(d)
        part = jnp.sum(e * e, axis=0, keepdims=True)

        @pl.when(pl.program_id(0) == 0)
        def _():
            sq_ref[...] = part

        @pl.when(pl.program_id(0) > 0)
        def _():
            sq_ref[...] += part

    row = pl.BlockSpec((tm, d), lambda i: (i, 0))
    vec = pl.BlockSpec((1, d), lambda i: (0, 0))
    return pl.pallas_call(
        body, name="loss_head", grid=(s // tm,), in_specs=[row, row], out_specs=[row, vec],
        out_shape=[jax.ShapeDtypeStruct((s, d), F32), jax.ShapeDtypeStruct((1, d), F32)],
        compiler_params=_params(("arbitrary",)),
    )(y, target)


def _conv_rows(ext, w, k_taps):
    acc = w[k_taps - 1:k_taps, :] * ext
    for k in range(k_taps - 1):
        acc = acc + w[k:k + 1, :] * _shift_down(ext, k_taps - 1 - k)
    return acc


def _conv_rows_transposed(dext, w, k_taps):
    acc = w[k_taps - 1:k_taps, :] * dext
    for k in range(k_taps - 1):
        acc = acc + w[k:k + 1, :] * _shift_up(dext, k_taps - 1 - k)
    return acc


def _stack_rows(rows, width):
    ri = _row_iota((8, width))
    out = jnp.zeros((8, width), F32)
    for k, r in enumerate(rows):
        out = out + jnp.where(ri == k, r, 0.0)
    return out


def ffn_mid_fwd(hu, conv_w8, conv_b, *, tm=512, tc=256):
    s = hu.shape[0]
    ncol = D_FF // tc
    r8 = tm // 8

    def body(g_ref, v_ref, gp_ref, vp_ref, wg_ref, wv_ref, bg_ref, bv_ref, o_ref):
        first = pl.program_id(1) == 0

        def conv(cur_ref, prev_ref, w_ref, b_ref):
            prev = jnp.where(first, 0.0, prev_ref[...])
            ext = jnp.concatenate([prev, cur_ref[...]], axis=0)
            return _conv_rows(ext, w_ref[...], FFN_CONV)[8:] + b_ref[...]

        gc = conv(g_ref, gp_ref, wg_ref, bg_ref)
        vc = conv(v_ref, vp_ref, wv_ref, bv_ref)
        o_ref[...] = (gc * _sigmoid(gc) * vc).astype(BF16)

    def prev_idx(i):
        return jnp.maximum(i * r8 - 1, 0)

    in_specs = [
        pl.BlockSpec((tm, tc), lambda j, i: (i, j)),
        pl.BlockSpec((tm, tc), lambda j, i: (i, j + ncol)),
        pl.BlockSpec((8, tc), lambda j, i: (prev_idx(i), j)),
        pl.BlockSpec((8, tc), lambda j, i: (prev_idx(i), j + ncol)),
        pl.BlockSpec((8, tc), lambda j, i: (0, j)),
        pl.BlockSpec((8, tc), lambda j, i: (0, j + ncol)),
        pl.BlockSpec((1, tc), lambda j, i: (0, j)),
        pl.BlockSpec((1, tc), lambda j, i: (0, j + ncol)),
    ]
    return pl.pallas_call(
        body, name="ffn_mid_fwd", grid=(ncol, s // tm), in_specs=in_specs,
        out_specs=pl.BlockSpec((tm, tc), lambda j, i: (i, j)),
        out_shape=jax.ShapeDtypeStruct((s, D_FF), BF16), compiler_params=_params(("parallel", "parallel")),
    )(hu, hu, hu, hu, conv_w8, conv_w8, conv_b, conv_b)


def ffn_mid_bwd(hu, dact, conv_w8, conv_b, *, tm=512, tc=256):
    s = hu.shape[0]
    ncol = D_FF // tc
    nrow = s // tm
    r8 = tm // 8

    def body(g_ref, v_ref, gp_ref, vp_ref, gn_ref, vn_ref, da_ref, dan_ref, wg_ref, wv_ref, bg_ref, bv_ref,
             dg_ref, dv_ref, wgo_ref, wvo_ref):
        i = pl.program_id(1)
        first = i == 0
        last = i == nrow - 1

        def ext_of(cur_ref, prev_ref, next_ref):
            prev = jnp.where(first, 0.0, prev_ref[...])
            return jnp.concatenate([prev, cur_ref[...], next_ref[...]], axis=0)

        g_ext = ext_of(g_ref, gp_ref, gn_ref)
        v_ext = ext_of(v_ref, vp_ref, vn_ref)
        gc = _conv_rows(g_ext, wg_ref[...], FFN_CONV) + bg_ref[...]
        vc = _conv_rows(v_ext, wv_ref[...], FFN_CONV) + bv_ref[...]
        da_ext = jnp.concatenate([jnp.zeros((8, tc), F32), da_ref[...], jnp.where(last, 0.0, dan_ref[...])], axis=0)
        silu, dsilu = _silu_and_grad(gc)
        dgc = da_ext * vc * dsilu
        dvc = da_ext * silu
        dg_ref[...] = _conv_rows_transposed(dgc, wg_ref[...], FFN_CONV)[8:8 + tm].astype(BF16)
        dv_ref[...] = _conv_rows_transposed(dvc, wv_ref[...], FFN_CONV)[8:8 + tm].astype(BF16)

        def wgrad(dcur, x_ext):
            rows = [jnp.sum(dcur * _shift_down(x_ext, FFN_CONV - 1 - k)[8:8 + tm], axis=0, keepdims=True)
                    for k in range(FFN_CONV)]
            rows.append(jnp.sum(dcur, axis=0, keepdims=True))
            return _stack_rows(rows, tc)

        pg = wgrad(dgc[8:8 + tm], g_ext)
        pv = wgrad(dvc[8:8 + tm], v_ext)

        @pl.when(first)
        def _():
            wgo_ref[...] = pg
            wvo_ref[...] = pv

        @pl.when(i > 0)
        def _():
            wgo_ref[...] += pg
            wvo_ref[...] += pv

    def prev_idx(i):
        return jnp.maximum(i * r8 - 1, 0)

    def next_idx(i):
        return jnp.minimum((i + 1) * r8, s // 8 - 1)

    cur_g = pl.BlockSpec((tm, tc), lambda j, i: (i, j))
    cur_v = pl.BlockSpec((tm, tc), lambda j, i: (i, j + ncol))
    in_specs = [
        cur_g, cur_v,
        pl.BlockSpec((8, tc), lambda j, i: (prev_idx(i), j)),
        pl.BlockSpec((8, tc), lambda j, i: (prev_idx(i), j + ncol)),
        pl.BlockSpec((8, tc), lambda j, i: (next_idx(i), j)),
        pl.BlockSpec((8, tc), lambda j, i: (next_idx(i), j + ncol)),
        cur_g,
        pl.BlockSpec((8, tc), lambda j, i: (next_idx(i), j)),
        pl.BlockSpec((8, tc), lambda j, i: (0, j)),
        pl.BlockSpec((8, tc), lambda j, i: (0, j + ncol)),
        pl.BlockSpec((1, tc), lambda j, i: (0, j)),
        pl.BlockSpec((1, tc), lambda j, i: (0, j + ncol)),
    ]
    out_specs = [cur_g, cur_g, pl.BlockSpec((8, tc), lambda j, i: (0, j)), pl.BlockSpec((8, tc), lambda j, i: (0, j))]
    out_shape = [jax.ShapeDtypeStruct((s, D_FF), BF16), jax.ShapeDtypeStruct((s, D_FF), BF16),
                 jax.ShapeDtypeStruct((8, D_FF), F32), jax.ShapeDtypeStruct((8, D_FF), F32)]
    return pl.pallas_call(
        body, name="ffn_mid_bwd", grid=(ncol, nrow), in_specs=in_specs, out_specs=out_specs, out_shape=out_shape,
        compiler_params=_params(("parallel", "arbitrary")),
    )(hu, hu, hu, hu, hu, hu, dact, dact, conv_w8, conv_w8, conv_b, conv_b)


def _softplus(x):
    return jnp.maximum(x, 0.0) + jnp.log(1.0 + jnp.exp(-jnp.abs(x)))


def _cumsum_rows(v):
    n = v.shape[0]
    ri = _row_iota(v.shape)
    sh = 1
    while sh < n:
        v = v + jnp.where(ri >= sh, _shift_down(v, sh), 0.0)
        sh *= 2
    return v


def _rev_cumsum_rows(v):
    n = v.shape[0]
    ri = _row_iota(v.shape)
    sh = 1
    while sh < n:
        v = v + jnp.where(ri < n - sh, _shift_up(v, sh), 0.0)
        sh *= 2
    return v


def _half_row_sums(v, lo):
    s0 = jnp.sum(jnp.where(lo, v, 0.0), axis=1, keepdims=True)
    return s0, jnp.sum(v, axis=1, keepdims=True) - s0


def _total(v):
    return jnp.sum(jnp.sum(v, axis=1, keepdims=True), axis=0, keepdims=True)


def _ssd_in_specs(rev_nc=None):
    def ch(c):
        return c if rev_nc is None else rev_nc - 1 - c

    def prev(c):
        return jnp.maximum(ch(c) * (SSD_CHUNK // 8) - 1, 0)

    L = SSD_CHUNK
    return [
        pl.BlockSpec((L, 1024), lambda c: (ch(c), 0)),
        pl.BlockSpec((L, 1024), lambda c: (ch(c), 1)),
        pl.BlockSpec((L, 256), lambda c: (ch(c), 20)),
        pl.BlockSpec((L, 256), lambda c: (ch(c), 21)),
        pl.BlockSpec((8, 1024), lambda c: (prev(c), 1)),
        pl.BlockSpec((8, 256), lambda c: (prev(c), 20)),
        pl.BlockSpec((8, 256), lambda c: (prev(c), 21)),
        pl.BlockSpec((8, 1024), lambda c: (0, 0)),
        pl.BlockSpec((8, 256), lambda c: (0, 4)),
        pl.BlockSpec((8, 256), lambda c: (0, 5)),
        pl.BlockSpec((1, 1024), lambda c: (0, 0)),
        pl.BlockSpec((1, 256), lambda c: (0, 4)),
        pl.BlockSpec((1, 256), lambda c: (0, 5)),
        pl.BlockSpec((L, SMALL_COLS), lambda c: (ch(c), 0)),
        pl.BlockSpec((8, 128), lambda c: (0, 0)),
        pl.BlockSpec((1, 1024), lambda c: (0, 0)),
    ]


def _ssd_conv_pre(cur_ref, prev_ref, w_ref, b_ref, first):
    prev = jnp.where(first, 0.0, prev_ref[...])
    ext = jnp.concatenate([prev, cur_ref[...]], axis=0)
    return ext, _conv_rows(ext, w_ref[...], SSD_CONV)[8:] + b_ref[...]


def _ssd_time_consts(small_ref, sp_ref):
    dt_pre = small_ref[...] + sp_ref[0:1, :]
    dt = _softplus(dt_pre)
    a = -jnp.exp(sp_ref[1:2, :])
    acs = _cumsum_rows(dt * a)
    return dt_pre, dt, a, acs


def ssd_fwd(proj, small, conv_w8, conv_b, smallp, norm_w):
    s = proj.shape[0]
    nc = s // SSD_CHUNK
    L = SSD_CHUNK

    def body(z_ref, xs_ref, b_ref, c_ref, xsp_ref, bp_ref, cp_ref, wx_ref, wb_ref, wc_ref, bx_ref, bb_ref, bc_ref,
             small_ref, sp_ref, nw_ref, y_ref, ypre_ref, st_ref, state):
        first = pl.program_id(0) == 0

        @pl.when(first)
        def _():
            state[...] = jnp.zeros_like(state)

        xs = _ssd_conv_pre(xs_ref, xsp_ref, wx_ref, bx_ref, first)[1]
        xs = xs * _sigmoid(xs)
        bm = _ssd_conv_pre(b_ref, bp_ref, wb_ref, bb_ref, first)[1]
        bm = bm * _sigmoid(bm)
        cm = _ssd_conv_pre(c_ref, cp_ref, wc_ref, bc_ref, first)[1]
        cm = cm * _sigmoid(cm)
        _, dt, _, acs = _ssd_time_consts(small_ref, sp_ref)
        acs_t = acs.T
        li = _lane_iota((L, L))
        ri = _row_iota((L, L))
        tri = ri >= li
        lo = li < HEAD_DIM
        st_ref[0] = state[...]
        for g in range(2):
            bg = bm[:, 128 * g:128 * g + 128]
            cg = cm[:, 128 * g:128 * g + 128]
            gmat = _dot(cg, bg, "nt")
            for pp in range(4):
                p = 4 * g + pp
                h0, h1 = 2 * p, 2 * p + 1
                x = xs[:, 128 * p:128 * p + 128]
                a0, a1 = acs[:, h0:h0 + 1], acs[:, h1:h1 + 1]
                xdt = x * jnp.where(lo, dt[:, h0:h0 + 1], dt[:, h1:h1 + 1])
                m0 = gmat * jnp.exp(jnp.where(tri, a0 - acs_t[h0:h0 + 1, :], NEG_BIG))
                m1 = gmat * jnp.exp(jnp.where(tri, a1 - acs_t[h1:h1 + 1, :], NEG_BIG))
                yd = _dot(m0, jnp.where(lo, xdt, 0.0)) + _dot(m1, jnp.where(lo, 0.0, xdt))
                hin = state[p]
                yo = _dot(cg, hin, "nt") * jnp.exp(jnp.where(lo, a0, a1))
                dskip = jnp.where(lo[0:1], sp_ref[2:3, h0:h0 + 1], sp_ref[2:3, h1:h1 + 1])
                ypre_ref[:, 128 * p:128 * p + 128] = yd + yo + dskip * x
                al0, al1 = acs[L - 1:L, h0:h0 + 1], acs[L - 1:L, h1:h1 + 1]
                w = jnp.exp(jnp.where(lo, al0 - a0, al1 - a1))
                dec = jnp.exp(jnp.where(ri < HEAD_DIM, al0, al1))
                state[p] = dec * hin + _dot(xdt * w, bg, "tn")
        z = z_ref[...]
        yg = ypre_ref[...] * (z * _sigmoid(z))
        for g in range(2):
            seg = yg[:, 512 * g:512 * g + 512]
            r = lax.rsqrt(jnp.mean(seg * seg, axis=-1, keepdims=True) + NORM_EPS)
            y_ref[:, 512 * g:512 * g + 512] = ((seg * r) * nw_ref[:, 512 * g:512 * g + 512]).astype(BF16)

    row = pl.BlockSpec((L, 1024), lambda c: (c, 0))
    return pl.pallas_call(
        body, name="ssd_fwd", grid=(nc,), in_specs=_ssd_in_specs(),
        out_specs=[row, row, pl.BlockSpec((1, N_PAIRS, 128, 128), lambda c: (c, 0, 0, 0))],
        out_shape=[jax.ShapeDtypeStruct((s, 1024), BF16), jax.ShapeDtypeStruct((s, 1024), F32),
                   jax.ShapeDtypeStruct((nc, N_PAIRS, 128, 128), F32)],
        scratch_shapes=[pltpu.VMEM((N_PAIRS, 128, 128), F32)],
        compiler_params=_params(("arbitrary",)),
    )(proj, proj, proj, proj, proj, proj, proj, conv_w8, conv_w8, conv_w8, conv_b, conv_b, conv_b, small, smallp, norm_w)


def ssd_bwd(proj, small, conv_w8, conv_b, smallp, norm_w, ypre, states, dy):
    s = proj.shape[0]
    nc = s // SSD_CHUNK
    L = SSD_CHUNK

    def body(z_ref, xs_ref, b_ref, c_ref, xsp_ref, bp_ref, cp_ref, wx_ref, wb_ref, wc_ref, bx_ref, bb_ref, bc_ref,
             small_ref, sp_ref, nw_ref, ypre_ref, st_ref, dy_ref,
             dz_ref, dxs_ref, db_ref, dc_ref, dsmall_ref, gwx_ref, gwb_ref, gwc_ref, gsp_ref, gnw_ref,
             dstate, carry_x, carry_b, carry_c, dxs_buf, dbm_buf, dcm_buf):
        step = pl.program_id(0)
        first_chunk = step == nc - 1
        start = step == 0

        @pl.when(start)
        def _():
            dstate[...] = jnp.zeros_like(dstate)
            carry_x[...] = jnp.zeros_like(carry_x)
            carry_b[...] = jnp.zeros_like(carry_b)
            carry_c[...] = jnp.zeros_like(carry_c)

        xs_ext, xs_pre = _ssd_conv_pre(xs_ref, xsp_ref, wx_ref, bx_ref, first_chunk)
        b_ext, b_pre = _ssd_conv_pre(b_ref, bp_ref, wb_ref, bb_ref, first_chunk)
        c_ext, c_pre = _ssd_conv_pre(c_ref, cp_ref, wc_ref, bc_ref, first_chunk)
        xs, xs_ds = _silu_and_grad(xs_pre)
        bm, b_ds = _silu_and_grad(b_pre)
        cm, c_ds = _silu_and_grad(c_pre)
        dt_pre, dt, a, acs = _ssd_time_consts(small_ref, sp_ref)
        acs_t = acs.T
        li = _lane_iota((L, L))
        ri = _row_iota((L, L))
        tri = ri >= li
        lo = li < HEAD_DIM
        lo_rows = ri < HEAD_DIM
        li1 = _lane_iota((1, L))

        z = z_ref[...]
        sz, dsz = _silu_and_grad(z)
        y = ypre_ref[...]
        yg = y * sz
        dout = dy_ref[...]
        dyg_parts = []
        gnw_parts = []
        for g in range(2):
            sl = slice(512 * g, 512 * g + 512)
            seg = yg[:, sl]
            r = lax.rsqrt(jnp.mean(seg * seg, axis=-1, keepdims=True) + NORM_EPS)
            n = seg * r
            gnw_parts.append(jnp.sum(dout[:, sl] * n, axis=0, keepdims=True))
            gg = dout[:, sl] * nw_ref[:, sl]
            dyg_parts.append(r * (gg - n * jnp.mean(gg * n, axis=-1, keepdims=True)))
        dyg = jnp.concatenate(dyg_parts, axis=1)
        gnw = jnp.concatenate(gnw_parts, axis=1)
        dz_ref[...] = (dyg * y * dsz).astype(BF16)
        dypre = dyg * sz

        ddt = jnp.zeros((L, L), F32)
        dacs = jnp.zeros((L, L), F32)
        dacs_t = jnp.zeros((L, L), F32)
        dalast = jnp.zeros((1, L), F32)
        dskip_g = jnp.zeros((1, L), F32)
        for g in range(2):
            bg = bm[:, 128 * g:128 * g + 128]
            cg = cm[:, 128 * g:128 * g + 128]
            gmat = _dot(cg, bg, "nt")
            dgmat = jnp.zeros((L, L), F32)
            dbg = jnp.zeros((L, L), F32)
            dcg = jnp.zeros((L, L), F32)
            for pp in range(4):
                p = 4 * g + pp
                h0, h1 = 2 * p, 2 * p + 1
                x = xs[:, 128 * p:128 * p + 128]
                dyp = dypre[:, 128 * p:128 * p + 128]
                a0, a1 = acs[:, h0:h0 + 1], acs[:, h1:h1 + 1]
                dtl = jnp.where(lo, dt[:, h0:h0 + 1], dt[:, h1:h1 + 1])
                xdt = x * dtl
                l0 = jnp.exp(jnp.where(tri, a0 - acs_t[h0:h0 + 1, :], NEG_BIG))
                l1 = jnp.exp(jnp.where(tri, a1 - acs_t[h1:h1 + 1, :], NEG_BIG))
                m0, m1 = gmat * l0, gmat * l1
                dskip = jnp.where(lo[0:1], sp_ref[2:3, h0:h0 + 1], sp_ref[2:3, h1:h1 + 1])
                s0, s1 = _half_row_sums(dyp * x, lo)
                dskip_g = dskip_g + jnp.where(li1 == h0, _total(s0), 0.0) + jnp.where(li1 == h1, _total(s1), 0.0)
                dx = dyp * dskip
                dy0, dy1 = jnp.where(lo, dyp, 0.0), jnp.where(lo, 0.0, dyp)
                x0, x1 = jnp.where(lo, xdt, 0.0), jnp.where(lo, 0.0, xdt)
                dm0, dm1 = _dot(dy0, x0, "nt"), _dot(dy1, x1, "nt")
                dxdt = _dot(m0, dy0, "tn") + _dot(m1, dy1, "tn")
                q0, q1 = dm0 * m0, dm1 * m1
                dacs = dacs + jnp.where(li == h0, jnp.sum(q0, axis=1, keepdims=True), 0.0) \
                            + jnp.where(li == h1, jnp.sum(q1, axis=1, keepdims=True), 0.0)
                dacs_t = dacs_t - jnp.where(ri == h0, jnp.sum(q0, axis=0, keepdims=True), 0.0) \
                                - jnp.where(ri == h1, jnp.sum(q1, axis=0, keepdims=True), 0.0)
                dgmat = dgmat + dm0 * l0 + dm1 * l1
                hin = st_ref[0, p]
                e = jnp.exp(jnp.where(lo, a0, a1))
                ch = _dot(cg, hin, "nt")
                dch = dyp * e
                dcg = dcg + _dot(dch, hin)
                dhin = _dot(dch, cg, "tn")
                s0, s1 = _half_row_sums(dch * ch, lo)
                dacs = dacs + jnp.where(li == h0, s0, 0.0) + jnp.where(li == h1, s1, 0.0)
                dhout = dstate[p]
                al0, al1 = acs[L - 1:L, h0:h0 + 1], acs[L - 1:L, h1:h1 + 1]
                dec = jnp.exp(jnp.where(lo_rows, al0, al1))
                dhin = dhin + dec * dhout
                dal = dhout * hin * dec
                dal0 = _total(jnp.where(lo_rows, dal, 0.0))
                dal1 = _total(dal) - dal0
                w = jnp.exp(jnp.where(lo, al0 - a0, al1 - a1))
                xw = xdt * w
                dxw = _dot(bg, dhout, "nt")
                dbg = dbg + _dot(xw, dhout)
                dxdt = dxdt + dxw * w
                s0, s1 = _half_row_sums(dxw * xw, lo)
                dacs = dacs - jnp.where(li == h0, s0, 0.0) - jnp.where(li == h1, s1, 0.0)
                dal0, dal1 = dal0 + _total(s0), dal1 + _total(s1)
                dalast = dalast + jnp.where(li1 == h0, dal0, 0.0) + jnp.where(li1 == h1, dal1, 0.0)
                dx = dx + dxdt * dtl
                s0, s1 = _half_row_sums(dxdt * x, lo)
                ddt = ddt + jnp.where(li == h0, s0, 0.0) + jnp.where(li == h1, s1, 0.0)
                dxs_buf[:, 128 * p:128 * p + 128] = dx
                dstate[p] = dhin
            dcg = dcg + _dot(dgmat, bg)
            dbg = dbg + _dot(dgmat, cg, "tn")
            dbm_buf[:, 128 * g:128 * g + 128] = dbg
            dcm_buf[:, 128 * g:128 * g + 128] = dcg

        dacs_tot = dacs + dacs_t.T + jnp.where(ri == L - 1, dalast, 0.0)
        dstep = _rev_cumsum_rows(dacs_tot)
        ddt = ddt + dstep * a
        head_lane = li < N_HEADS
        ddt_pre = jnp.where(head_lane, ddt * _sigmoid(dt_pre), 0.0)
        dsmall_ref[...] = ddt_pre
        da = jnp.sum(jnp.where(head_lane, dstep * dt, 0.0), axis=0, keepdims=True)
        gsp = _stack_rows([jnp.sum(ddt_pre, axis=0, keepdims=True), da * a, dskip_g], L)

        def conv_back(dpost, ds, ext, w_ref, carry, out_ref, width):
            dpre = dpost * ds
            dext = jnp.concatenate([dpre, carry[...]], axis=0)
            out_ref[...] = _conv_rows_transposed(dext, w_ref[...], SSD_CONV)[:L].astype(BF16)
            carry[...] = dpre[0:8]
            rows = [jnp.sum(dpre * _shift_down(ext, SSD_CONV - 1 - k)[8:], axis=0, keepdims=True) for k in range(SSD_CONV)]
            rows.append(jnp.sum(dpre, axis=0, keepdims=True))
            return _stack_rows(rows, width)

        gwx = conv_back(dxs_buf[...], xs_ds, xs_ext, wx_ref, carry_x, dxs_ref, 1024)
        gwb = conv_back(dbm_buf[...], b_ds, b_ext, wb_ref, carry_b, db_ref, 256)
        gwc = conv_back(dcm_buf[...], c_ds, c_ext, wc_ref, carry_c, dc_ref, 256)

        @pl.when(start)
        def _():
            gwx_ref[...] = gwx
            gwb_ref[...] = gwb
            gwc_ref[...] = gwc
            gsp_ref[...] = gsp
            gnw_ref[...] = gnw

        @pl.when(step > 0)
        def _():
            gwx_ref[...] += gwx
            gwb_ref[...] += gwb
            gwc_ref[...] += gwc
            gsp_ref[...] += gsp
            gnw_ref[...] += gnw

    def ch(c):
        return nc - 1 - c

    row = pl.BlockSpec((L, 1024), lambda c: (ch(c), 0))
    row256 = pl.BlockSpec((L, 256), lambda c: (ch(c), 0))
    in_specs = _ssd_in_specs(rev_nc=nc) + [row, pl.BlockSpec((1, N_PAIRS, 128, 128), lambda c: (ch(c), 0, 0, 0)), row]
    out_specs = [row, row, row256, row256, pl.BlockSpec((L, 128), lambda c: (ch(c), 0)),
                 pl.BlockSpec((8, 1024), lambda c: (0, 0)), pl.BlockSpec((8, 256), lambda c: (0, 0)),
                 pl.BlockSpec((8, 256), lambda c: (0, 0)), pl.BlockSpec((8, 128), lambda c: (0, 0)),
                 pl.BlockSpec((1, 1024), lambda c: (0, 0))]
    out_shape = [jax.ShapeDtypeStruct((s, 1024), BF16), jax.ShapeDtypeStruct((s, 1024), BF16),
                 jax.ShapeDtypeStruct((s, 256), BF16), jax.ShapeDtypeStruct((s, 256), BF16),
                 jax.ShapeDtypeStruct((s, 128), F32),
                 jax.ShapeDtypeStruct((8, 1024), F32), jax.ShapeDtypeStruct((8, 256), F32),
                 jax.ShapeDtypeStruct((8, 256), F32), jax.ShapeDtypeStruct((8, 128), F32),
                 jax.ShapeDtypeStruct((1, 1024), F32)]
    scratch = [pltpu.VMEM((N_PAIRS, 128, 128), F32), pltpu.VMEM((8, 1024), F32), pltpu.VMEM((8, 256), F32),
               pltpu.VMEM((8, 256), F32), pltpu.VMEM((L, 1024), F32), pltpu.VMEM((L, 256), F32), pltpu.VMEM((L, 256), F32)]
    return pl.pallas_call(
        body, name="ssd_bwd", grid=(nc,), in_specs=in_specs, out_specs=out_specs, out_shape=out_shape,
        scratch_shapes=scratch, compiler_params=_params(("arbitrary",)),
    )(proj, proj, proj, proj, proj, proj, proj, conv_w8, conv_w8, conv_w8, conv_b, conv_b, conv_b, small, smallp, norm_w,
      ypre, states, dy)


FOX_SCALE = HEAD_DIM ** -0.5
FOX_T = 256
Q_COL, K_COL, V_COL = 2, 3, 4


def _split3_dot(v, m):
    hi = v.astype(BF16)
    r1 = v - hi.astype(F32)
    mid = r1.astype(BF16)
    lo = (r1 - mid.astype(F32)).astype(BF16)
    return _dot(hi, m) + _dot(mid, m) + _dot(lo, m)


def _head_rstd(x, sel_ref, selt_ref):
    ms = _split3_dot(x * x, sel_ref[...]) * (1.0 / HEAD_DIM)
    return _split3_dot(lax.rsqrt(ms + NORM_EPS), selt_ref[...])


def fox_tables():
    r = jnp.arange(3 * 128)
    piece, lane = r // 128, r % 128
    head = lane - F_LANE
    is_head = jnp.logical_and(head >= 0, head < N_HEADS)
    col = 128 * (head // 2) + HEAD_DIM * (1 - head % 2) + piece
    cols = jnp.arange(1024)
    place_q = jnp.logical_and(is_head[:, None], cols[None, :] == col[:, None]).astype(BF16)
    place_k = jnp.logical_and(is_head[:, None], cols[None, :] == (col + 3)[:, None]).astype(BF16)
    ones_q = jnp.logical_and(cols % HEAD_DIM >= 3, cols % HEAD_DIM < 6).astype(F32)[None]
    ones_k = (cols % HEAD_DIM < 3).astype(F32)[None]
    h = jnp.arange(128) - F_LANE
    ok = jnp.logical_and(h >= 0, h < N_HEADS)
    same_pair = cols[:, None] // 128 == (h // 2)[None, :]
    fold_even = jnp.logical_and(jnp.logical_and(ok, h % 2 == 0)[None, :], same_pair).astype(BF16)
    fold_odd = jnp.logical_and(jnp.logical_and(ok, h % 2 == 1)[None, :], same_pair).astype(BF16)
    return place_q, place_k, ones_q, ones_k, fold_even, fold_odd


def fox_prep(proj, small, smallp, qw, kw, sel, selt, place_q, place_k, ones_q, ones_k, *, tm=256):
    s = proj.shape[0]

    def body(q_ref, k_ref, v_ref, small_ref, sp_ref, qw_ref, kw_ref, sel_ref, selt_ref, pq_ref, pk_ref, oq_ref, ok_ref,
             qn_ref, kn_ref, aq_ref, ak_ref, vb_ref, knt_ref, akt_ref, carry):
        @pl.when(pl.program_id(0) == 0)
        def _():
            carry[...] = jnp.zeros_like(carry)

        q = q_ref[...]
        qn_ref[...] = (((q * _head_rstd(q, sel_ref, selt_ref)) * qw_ref[...]) * FOX_SCALE).astype(BF16)
        k = k_ref[...]
        kn = ((k * _head_rstd(k, sel_ref, selt_ref)) * kw_ref[...]).astype(BF16)
        kn_ref[...] = kn
        knt_ref[...] = kn.astype(F32).T.astype(BF16)
        vb_ref[...] = v_ref[...].astype(BF16)
        li = _lane_iota((tm, 128))
        f_lane = jnp.logical_and(li >= F_LANE, li < F_LANE + N_HEADS)
        logf = jnp.where(f_lane, -_softplus(-(small_ref[...] + sp_ref[3:4, :])), 0.0)
        cum = _cumsum_rows(logf) + carry[...]
        carry[...] = cum[tm - 1:tm, :]
        hi = cum.astype(BF16)
        r1 = cum - hi.astype(F32)
        mid = r1.astype(BF16)
        lo = (r1 - mid.astype(F32)).astype(BF16)
        pieces = jnp.concatenate([hi, mid, lo], axis=1)
        aq_ref[...] = (_dot(pieces, pq_ref[...]) + oq_ref[...]).astype(BF16)
        ak = ok_ref[...] - _dot(pieces, pk_ref[...])
        ak_ref[...] = ak.astype(BF16)
        akt_ref[...] = ak.T.astype(BF16)

    row = pl.BlockSpec((tm, 1024), lambda i: (i, 0))
    col = pl.BlockSpec((1024, tm), lambda i: (0, i))
    vec = pl.BlockSpec((1, 1024), lambda i: (0, 0))
    table = pl.BlockSpec((384, 1024), lambda i: (0, 0))
    wide = jax.ShapeDtypeStruct((s, 1024), BF16)
    tall = jax.ShapeDtypeStruct((1024, s), BF16)
    return pl.pallas_call(
        body, name="fox_prep", grid=(s // tm,),
        in_specs=[pl.BlockSpec((tm, 1024), lambda i: (i, Q_COL)), pl.BlockSpec((tm, 1024), lambda i: (i, K_COL)),
                  pl.BlockSpec((tm, 1024), lambda i: (i, V_COL)),
                  pl.BlockSpec((tm, 128), lambda i: (i, 0)), pl.BlockSpec((8, 128), lambda i: (0, 0)), vec, vec,
                  pl.BlockSpec((1024, 128), lambda i: (0, 0)), pl.BlockSpec((128, 1024), lambda i: (0, 0)),
                  table, table, vec, vec],
        out_specs=[row, row, row, row, row, col, col],
        out_shape=[wide, wide, wide, wide, wide, tall, tall],
        scratch_shapes=[pltpu.VMEM((1, 128), F32)], compiler_params=_params(("arbitrary",)),
    )(proj, proj, proj, small, smallp, qw, kw, sel, selt, place_q, place_k, ones_q, ones_k)


def fox_fwd(qn, kn, aq, ak, vb):
    s = qn.shape[0]
    t = FOX_T
    nq = s // t

    def body(q_ref, k_ref, aq_ref, ak_ref, v_ref, o_ref, lse_ref):
        p = pl.program_id(0)

        @pl.when(p == 0)
        def _():
            lse_ref[...] = jnp.zeros_like(lse_ref)

        lo = _lane_iota((t, 128)) < HEAD_DIM
        causal = _row_iota((t, t)) >= _lane_iota((t, t))

        def q_loop(qi, _):
            q0 = pl.multiple_of(qi * t, t)
            qv, aqv = q_ref[pl.ds(q0, t), :], aq_ref[pl.ds(q0, t), :]
            qa, qb = jnp.where(lo, qv, aqv), jnp.where(lo, aqv, qv)

            def step(kj, carry, diagonal):
                m0, l0, m1, l1, acc = carry
                k0 = pl.multiple_of(kj * t, t)
                kv, akv, vv = k_ref[pl.ds(k0, t), :], ak_ref[pl.ds(k0, t), :], v_ref[pl.ds(k0, t), :]
                s0 = _dot(qa, jnp.where(lo, kv, akv), "nt")
                s1 = _dot(qb, jnp.where(lo, akv, kv), "nt")
                if diagonal:
                    s0, s1 = jnp.where(causal, s0, NEG_BIG), jnp.where(causal, s1, NEG_BIG)
                n0 = jnp.maximum(m0, jnp.max(s0, axis=1, keepdims=True))
                n1 = jnp.maximum(m1, jnp.max(s1, axis=1, keepdims=True))
                a0, a1 = jnp.exp(m0 - n0), jnp.exp(m1 - n1)
                p0, p1 = jnp.exp(s0 - n0), jnp.exp(s1 - n1)
                l0 = a0 * l0 + jnp.sum(p0, axis=1, keepdims=True)
                l1 = a1 * l1 + jnp.sum(p1, axis=1, keepdims=True)
                acc = jnp.where(lo, a0, a1) * acc + _dot(p0, jnp.where(lo, vv, 0.0)) + _dot(p1, jnp.where(lo, 0.0, vv))
                return n0, l0, n1, l1, acc

            def col(val):
                return jnp.full((t, 1), val, F32)

            init = (col(NEG_BIG), col(0.0), col(NEG_BIG), col(0.0), jnp.zeros((t, 128), F32))
            carry = lax.fori_loop(0, qi, lambda kj, c: step(kj, c, False), init)
            m0, l0, m1, l1, acc = step(qi, carry, True)
            o_ref[pl.ds(q0, t), :] = (acc / jnp.where(lo, l0, l1)).astype(BF16)
            lse_rows = jnp.where(lo, m0 + jnp.log(l0), m1 + jnp.log(l1)).T
            ri = _row_iota((N_HEADS, t))
            old = lse_ref[:, pl.ds(q0, t)]
            lse_ref[:, pl.ds(q0, t)] = jnp.where(
                ri == 2 * p, lse_rows[0:1, :], jnp.where(ri == 2 * p + 1, lse_rows[HEAD_DIM:HEAD_DIM + 1, :], old))
            return 0

        lax.fori_loop(0, nq, q_loop, 0)

    pair = pl.BlockSpec((s, 128), lambda p: (0, p))
    return pl.pallas_call(
        body, name="fox_fwd", grid=(N_PAIRS,), in_specs=[pair] * 5,
        out_specs=[pair, pl.BlockSpec((N_HEADS, s), lambda p: (0, 0))],
        out_shape=[jax.ShapeDtypeStruct((s, 1024), BF16), jax.ShapeDtypeStruct((N_HEADS, s), F32)],
        compiler_params=_params(("arbitrary",)),
    )(qn, kn, aq, ak, vb)


def fox_bwd(qn, kn, aq, ak, knt, akt, vb, lse, dmixed):
    s = qn.shape[0]
    t = FOX_T
    nq = s // t
    once = pl.Buffered(1)

    def body(q_ref, k_ref, aq_ref, ak_ref, kt_ref, akt_ref, v_ref, lse_ref, do_ref, dq_ref, dk_ref, dv_ref, dc0_ref, dc1_ref,
             p_scr, dp_scr):
        p = pl.program_id(0)
        dk_ref[...] = jnp.zeros_like(dk_ref)
        dv_ref[...] = jnp.zeros_like(dv_ref)
        dc0_ref[...] = jnp.zeros_like(dc0_ref)
        dc1_ref[...] = jnp.zeros_like(dc1_ref)
        lo = _lane_iota((t, 128)) < HEAD_DIM
        lo_rows = _row_iota((128, t)) < HEAD_DIM
        causal_t = _lane_iota((t, t)) >= _row_iota((t, t))

        def q_loop(qi, _):
            q0 = pl.multiple_of(qi * t, t)
            qv, aqv = q_ref[pl.ds(q0, t), :], aq_ref[pl.ds(q0, t), :]
            qa, qb = jnp.where(lo, qv, aqv), jnp.where(lo, aqv, qv)
            do = do_ref[pl.ds(q0, t), :]
            doa, dob = jnp.where(lo, do, 0.0).astype(BF16), jnp.where(lo, 0.0, do).astype(BF16)
            lse_blk = lse_ref[:, pl.ds(q0, t)]
            ri = _row_iota((N_HEADS, t))
            lse0 = jnp.sum(jnp.where(ri == 2 * p, lse_blk, 0.0), axis=0, keepdims=True)
            lse1 = jnp.sum(jnp.where(ri == 2 * p + 1, lse_blk, 0.0), axis=0, keepdims=True)

            def pass1(kj, carry, diagonal):
                d0, d1 = carry
                k0 = pl.multiple_of(kj * t, t)
                kv, akv, vv = k_ref[pl.ds(k0, t), :], ak_ref[pl.ds(k0, t), :], v_ref[pl.ds(k0, t), :]
                s0 = _dot(jnp.where(lo, kv, akv), qa, "nt")
                s1 = _dot(jnp.where(lo, akv, kv), qb, "nt")
                if diagonal:
                    s0, s1 = jnp.where(causal_t, s0, NEG_BIG), jnp.where(causal_t, s1, NEG_BIG)
                p0, p1 = jnp.exp(s0 - lse0), jnp.exp(s1 - lse1)
                dp0, dp1 = _dot(vv, doa, "nt"), _dot(vv, dob, "nt")
                p_scr[0, kj], p_scr[1, kj] = p0, p1
                dp_scr[0, kj], dp_scr[1, kj] = dp0, dp1
                return d0 + jnp.sum(p0 * dp0, axis=0, keepdims=True), d1 + jnp.sum(p1 * dp1, axis=0, keepdims=True)

            zero = jnp.zeros((1, t), F32)
            carry = lax.fori_loop(0, qi, lambda kj, c: pass1(kj, c, False), (zero, zero))
            d0, d1 = pass1(qi, carry, True)

            def pass2(kj, carry):
                dq0, dq1 = carry
                k0 = pl.multiple_of(kj * t, t)
                p0, p1 = p_scr[0, kj], p_scr[1, kj]
                ds0, ds1 = p0 * (dp_scr[0, kj] - d0), p1 * (dp_scr[1, kj] - d1)
                dk_ref[pl.ds(k0, t), :] += jnp.where(lo, _dot(ds0, qa), _dot(ds1, qb))
                dv_ref[pl.ds(k0, t), :] += _dot(p0, doa) + _dot(p1, dob)
                dc0_ref[pl.ds(k0, t), :] += ds0[:, :128] + ds0[:, 128:]
                dc1_ref[pl.ds(k0, t), :] += ds1[:, :128] + ds1[:, 128:]
                ktv, aktv = kt_ref[:, pl.ds(k0, t)], akt_ref[:, pl.ds(k0, t)]
                return dq0 + _dot(jnp.where(lo_rows, ktv, aktv), ds0), dq1 + _dot(jnp.where(lo_rows, aktv, ktv), ds1)

            zq = jnp.zeros((128, t), F32)
            dq0, dq1 = lax.fori_loop(0, qi + 1, pass2, (zq, zq))
            dq_ref[pl.ds(q0, t), :] = jnp.where(lo_rows, dq0, dq1).T
            return 0

        lax.fori_loop(0, nq, q_loop, 0)

    pair = pl.BlockSpec((s, 128), lambda p: (0, p), pipeline_mode=once)
    pair_t = pl.BlockSpec((128, s), lambda p: (p, 0), pipeline_mode=once)
    out = jax.ShapeDtypeStruct((s, 1024), F32)
    return pl.pallas_call(
        body, name="fox_bwd", grid=(N_PAIRS,),
        in_specs=[pair, pair, pair, pair, pair_t, pair_t, pair, pl.BlockSpec((N_HEADS, s), lambda p: (0, 0)),
                  pl.BlockSpec((s, 128), lambda p: (0, 8 + p), pipeline_mode=once)],
        out_specs=[pair] * 5, out_shape=[out] * 5,
        scratch_shapes=[pltpu.VMEM((2, nq, t, t), F32), pltpu.VMEM((2, nq, t, t), F32)],
        compiler_params=_params(("arbitrary",)),
    )(qn, kn, aq, ak, knt, akt, vb, lse, dmixed)


def fox_post(dqn, dkn, dc0, dc1, proj, small, smallp, qw, kw, sel, selt, fold_even, fold_odd, *, tm=256):
    s = proj.shape[0]
    nrow = s // tm

    def body(dqn_ref, dkn_ref, dc0_ref, dc1_ref, q_ref, k_ref, small_ref, sp_ref, qw_ref, kw_ref, sel_ref, selt_ref,
             fe_ref, fo_ref, dq_ref, dk_ref, dsmall_ref, gqw_ref, gkw_ref, gfb_ref, carry):
        step = pl.program_id(0)

        @pl.when(step == 0)
        def _():
            carry[...] = jnp.zeros_like(carry)

        def norm_bwd(x_ref, w_ref, dn, out_ref):
            x = x_ref[...]
            rf = _head_rstd(x, sel_ref, selt_ref)
            xh = x * rf
            g = dn * w_ref[...]
            mean_gx = _split3_dot(_split3_dot(g * xh, sel_ref[...]) * (1.0 / HEAD_DIM), selt_ref[...])
            out_ref[...] = (rf * (g - xh * mean_gx)).astype(BF16)
            return jnp.sum(dn * xh, axis=0, keepdims=True)

        gqw = norm_bwd(q_ref, qw_ref, dqn_ref[...] * FOX_SCALE, dq_ref)
        gkw = norm_bwd(k_ref, kw_ref, dkn_ref[...], dk_ref)
        li = _lane_iota((tm, 128))
        f_lane = jnp.logical_and(li >= F_LANE, li < F_LANE + N_HEADS)
        dcum = -(_split3_dot(dc0_ref[...], fe_ref[...]) + _split3_dot(dc1_ref[...], fo_ref[...]))
        dlogf = _rev_cumsum_rows(dcum) + carry[...]
        carry[...] = dlogf[0:1, :]
        dfr = jnp.where(f_lane, dlogf * _sigmoid(-(small_ref[...] + sp_ref[3:4, :])), 0.0)
        dsmall_ref[...] = dfr
        gfb = jnp.sum(dfr, axis=0, keepdims=True)

        @pl.when(step == 0)
        def _():
            gqw_ref[...] = gqw
            gkw_ref[...] = gkw
            gfb_ref[...] = gfb

        @pl.when(step > 0)
        def _():
            gqw_ref[...] += gqw
            gkw_ref[...] += gkw
            gfb_ref[...] += gfb

    def rb(i):
        return nrow - 1 - i

    row = pl.BlockSpec((tm, 1024), lambda i: (rb(i), 0))
    vec = pl.BlockSpec((1, 1024), lambda i: (0, 0))
    fold = pl.BlockSpec((1024, 128), lambda i: (0, 0))
    return pl.pallas_call(
        body, name="fox_post", grid=(nrow,),
        in_specs=[row, row, row, row, pl.BlockSpec((tm, 1024), lambda i: (rb(i), Q_COL)),
                  pl.BlockSpec((tm, 1024), lambda i: (rb(i), K_COL)),
                  pl.BlockSpec((tm, 128), lambda i: (rb(i), 0)), pl.BlockSpec((8, 128), lambda i: (0, 0)), vec, vec,
                  fold, pl.BlockSpec((128, 1024), lambda i: (0, 0)), fold, fold],
        out_specs=[row, row, pl.BlockSpec((tm, 128), lambda i: (rb(i), 0)), vec, vec, pl.BlockSpec((1, 128), lambda i: (0, 0))],
        out_shape=[jax.ShapeDtypeStruct((s, 1024), BF16), jax.ShapeDtypeStruct((s, 1024), BF16),
                   jax.ShapeDtypeStruct((s, 128), F32), jax.ShapeDtypeStruct((1, 1024), F32),
                   jax.ShapeDtypeStruct((1, 1024), F32), jax.ShapeDtypeStruct((1, 128), F32)],
        scratch_shapes=[pltpu.VMEM((1, 128), F32)], compiler_params=_params(("arbitrary",)),
    )(dqn, dkn, dc0, dc1, proj, proj, small, smallp, qw, kw, sel, selt, fold_even, fold_odd)


def local_step(x, target, wm, ws, w_out, w_up, w_down, ssd_cw8, ssd_cb, smallp, ssd_nw, qw_t, kw_t, sel, selt,
               norm_mix_w, norm_ffn_w, ffn_cw8, ffn_cb):
    h = rms_fwd(x, norm_mix_w, name="rms_mix_fwd")
    proj = matmul(h, wm, mode="nn", tm=1024, tn=512, tk=1024, out_dtype=F32, name="mm_in_proj")
    small = matmul(h, ws, mode="nn", tm=1024, tn=128, tk=1024, out_dtype=F32, name="mm_in_proj_small")
    y_ssd, ypre, states = ssd_fwd(proj, small, ssd_cw8, ssd_cb, smallp, ssd_nw)
    place_q, place_k, ones_q, ones_k, fold_even, fold_odd = fox_tables()
    qn, kn, aq, ak, vb, knt, akt = fox_prep(proj, small, smallp, qw_t, kw_t, sel, selt, place_q, place_k, ones_q, ones_k)
    y_fox, lse = fox_fwd(qn, kn, aq, ak, vb)
    x1 = matmul(y_ssd, w_out, mode="nn", tm=512, tn=1024, tk=1024, out_dtype=F32, name="mm_out_ssd", add=x)
    x1 = matmul(y_fox, w_out, mode="nn", tm=512, tn=1024, tk=1024, out_dtype=F32, name="mm_out_fox", add=x1, b_koff=1)
    hf = rms_fwd(x1, norm_ffn_w, name="rms_ffn_fwd")
    hu = matmul(hf, w_up, mode="nn", tm=1024, tn=512, tk=1024, out_dtype=F32, name="mm_up")
    act = ffn_mid_fwd(hu, ffn_cw8, ffn_cb)
    y = matmul(act, w_down, mode="nn", tm=512, tn=1024, tk=1408, out_dtype=F32, name="mm_down", add=x1)
    dy, sq = loss_head(y, target)

    dact = matmul(dy, w_down, mode="nt", tm=512, tn=1408, tk=1024, out_dtype=F32, name="mm_dact")
    g_down = matmul(act, dy, mode="tn", tm=1408, tn=1024, tk=512, out_dtype=BF16, name="mm_dw_down")
    dhu_g, dhu_v, gcw_g, gcw_v = ffn_mid_bwd(hu, dact, ffn_cw8, ffn_cb)
    dhf = matmul(dhu_g, w_up, mode="nt", tm=512, tn=1024, tk=1408, out_dtype=F32, name="mm_dhf_gate")
    dhf = matmul(dhu_v, w_up, mode="nt", tm=512, tn=1024, tk=1408, out_dtype=F32, name="mm_dhf_val", add=dhf, b_koff=2)
    g_up_g = matmul(hf, dhu_g, mode="tn", tm=1024, tn=1408, tk=512, out_dtype=BF16, name="mm_dw_up_gate")
    g_up_v = matmul(hf, dhu_v, mode="tn", tm=1024, tn=1408, tk=512, out_dtype=BF16, name="mm_dw_up_val")
    dx1, g_norm_ffn = rms_bwd(dhf, x1, norm_ffn_w, dy, name="rms_ffn_bwd")
    dmixed = matmul(dx1, w_out, mode="nt", tm=512, tn=1024, tk=1024, out_dtype=F32, name="mm_dmixed")
    g_out_a = matmul(y_ssd, dx1, mode="tn", tm=1024, tn=1024, tk=512, out_dtype=BF16, name="mm_dw_out_ssd")
    g_out_b = matmul(y_fox, dx1, mode="tn", tm=1024, tn=1024, tk=512, out_dtype=BF16, name="mm_dw_out_fox")
    dz, dxs, db, dc, dsmall_ssd, gcw_x, gcw_b, gcw_c, g_sp, g_ssd_nw = ssd_bwd(
        proj, small, ssd_cw8, ssd_cb, smallp, ssd_nw, ypre, states, dmixed)
    dqn, dkn, dv, dc0, dc1 = fox_bwd(qn, kn, aq, ak, knt, akt, vb, lse, dmixed)
    dq, dk, dsmall_fox, g_qw, g_kw, g_fb = fox_post(dqn, dkn, dc0, dc1, proj, small, smallp, qw_t, kw_t, sel, selt,
                                                    fold_even, fold_odd)
    dproj = jnp.concatenate([dz, dxs, dq, dk, dv.astype(BF16), db, dc], axis=1)
    dsmall = (dsmall_ssd + dsmall_fox).astype(BF16)
    dh = matmul(dproj, wm, mode="nt", tm=512, tn=1024, tk=512, out_dtype=F32, name="mm_dh")
    dh = matmul(dsmall, ws, mode="nt", tm=512, tn=1024, tk=128, out_dtype=F32, name="mm_dh_small", add=dh)
    g_wm = matmul(h, dproj, mode="tn", tm=1024, tn=1408, tk=512, out_dtype=BF16, name="mm_dw_in")
    g_ws = matmul(h, dsmall, mode="tn", tm=1024, tn=128, tk=512, out_dtype=BF16, name="mm_dw_in_small")
    grad_x, g_norm_mix = rms_bwd(dh, x, norm_mix_w, dx1, name="rms_mix_bwd")
    return dict(
        sq=sq, grad_x=grad_x, g_wm=g_wm, g_ws=g_ws, g_out=jnp.concatenate([g_out_a, g_out_b], axis=0),
        g_up=jnp.concatenate([g_up_g, g_up_v], axis=1), g_down=g_down,
        g_norm_mix=g_norm_mix, g_norm_ffn=g_norm_ffn, g_ssd_nw=g_ssd_nw,
        g_ssd_cw=jnp.concatenate([gcw_x, gcw_b, gcw_c], axis=1), g_sp=g_sp, g_fb=g_fb, g_qw=g_qw, g_kw=g_kw,
        g_ffn_cw=jnp.concatenate([gcw_g, gcw_v], axis=1))


def adamw(w, g, m, v, *, name, tr):
    rows, cols = w.shape

    def body(w_ref, g_ref, m_ref, v_ref, d_ref, mo_ref, vo_ref):
        gv = g_ref[...]
        mn = ADAM_B1 * m_ref[...] + (1.0 - ADAM_B1) * gv
        vn = ADAM_B2 * v_ref[...] + (1.0 - ADAM_B2) * (gv * gv)
        m_hat = mn / (1.0 - ADAM_B1 ** ADAM_STEP)
        v_hat = vn / (1.0 - ADAM_B2 ** ADAM_STEP)
        d_ref[...] = -ADAM_LR * (m_hat / (jnp.sqrt(v_hat) + ADAM_EPS) + ADAM_WD * w_ref[...])
        mo_ref[...] = mn
        vo_ref[...] = vn

    blk = pl.BlockSpec((tr, cols), lambda i: (i, 0))
    shp = jax.ShapeDtypeStruct((rows, cols), F32)
    return pl.pallas_call(
        body, name=name, grid=(rows // tr,), in_specs=[blk] * 4, out_specs=[blk] * 3, out_shape=[shp] * 3,
        compiler_params=_params(("parallel",)),
    )(w, g, m, v)


def add_pair(a, b, *, name, tr):
    _, rows, cols = a.shape

    def body(a_ref, b_ref, o_ref):
        o_ref[...] = (a_ref[...].astype(F32) + b_ref[...].astype(F32)).astype(BF16)

    blk = pl.BlockSpec((1, tr, cols), lambda j, i: (j, i, 0))
    return pl.pallas_call(
        body, name=name, grid=(4, rows // tr), in_specs=[blk, blk], out_specs=blk,
        out_shape=jax.ShapeDtypeStruct(a.shape, BF16), compiler_params=_params(("parallel", "parallel")),
    )(a, b)


def sum_chips(parts, *, name, tr):
    _, rows, cols = parts.shape

    def body(p_ref, o_ref):
        acc = p_ref[0].astype(F32)
        for k in range(1, 4):
            acc = acc + p_ref[k].astype(F32)
        o_ref[...] = acc

    return pl.pallas_call(
        body, name=name, grid=(rows // tr,), in_specs=[pl.BlockSpec((4, tr, cols), lambda i: (0, i, 0))],
        out_specs=pl.BlockSpec((tr, cols), lambda i: (i, 0)), out_shape=jax.ShapeDtypeStruct((rows, cols), F32),
        compiler_params=_params(("parallel",)),
    )(parts)


ANY = pl.BlockSpec(memory_space=pl.ANY)


def _place():
    x, y, c = lax.axis_index("x"), lax.axis_index("y"), lax.axis_index("c")
    chips = [(1 - x, y), (x, 1 - y), (1 - x, 1 - y)]
    return x, y, c, chips


def _chunks(rows):
    size = next((c for c in (128, 176, 64, 32, 16, 8) if rows % c == 0), rows)
    return [(r, size) for r in range(0, rows, size)]


def gather_weights(shards):
    n = len(shards)

    def body(*refs):
        ins, outs = refs[:n], refs[n:2 * n]
        send_sems, recv_sems = refs[2 * n:]
        x, y, c, chips = _place()
        me = 2 * x + y
        sibling = (x, y, 1 - c)

        def half(a, blk, r=0, nr=None):
            rows = ins[a].shape[0] // 2
            return outs[a].at[blk, pl.ds(c * rows + r, rows if nr is None else nr), :]

        def to_chip(a, t, r=0, nr=None):
            rows = ins[a].shape[0] // 2
            return pltpu.make_async_remote_copy(
                src_ref=ins[a].at[pl.ds(c * rows + r, rows if nr is None else nr), :], dst_ref=half(a, me, r, nr),
                send_sem=send_sems.at[a, t], recv_sem=recv_sems.at[a, t],
                device_id=(*chips[t], c), device_id_type=MESH)

        def from_chip(a, t):
            blk = 2 * chips[t][0] + chips[t][1]
            return pltpu.make_async_remote_copy(
                src_ref=half(a, blk), dst_ref=half(a, blk), send_sem=send_sems.at[a, t], recv_sem=recv_sems.at[a, t],
                device_id=(*chips[t], c), device_id_type=MESH)

        def to_sibling(a, t, r=0, nr=None):
            blk = 2 * chips[t][0] + chips[t][1]
            return pltpu.make_async_remote_copy(
                src_ref=half(a, blk, r, nr), dst_ref=half(a, blk, r, nr), send_sem=send_sems.at[a, 3 + t],
                recv_sem=recv_sems.at[a, 3 + t], device_id=sibling, device_id_type=MESH)

        def from_sibling(a, t):
            blk = 2 * chips[t][0] + chips[t][1]
            rows = ins[a].shape[0] // 2
            dst = outs[a].at[blk, pl.ds((1 - c) * rows, rows), :]
            return pltpu.make_async_remote_copy(
                src_ref=dst, dst_ref=dst, send_sem=send_sems.at[a, 3 + t], recv_sem=recv_sems.at[a, 3 + t],
                device_id=sibling, device_id_type=MESH)

        for a in range(n):
            for t in range(3):
                for r, nr in _chunks(ins[a].shape[0] // 2):
                    to_chip(a, t, r, nr).start()
        for a in range(n):
            for t in range(3):
                from_chip(a, t).wait_recv()
                for r, nr in _chunks(ins[a].shape[0] // 2):
                    to_sibling(a, t, r, nr).start()
        for a in range(n):
            for t in range(3):
                from_sibling(a, t).wait_recv()
        for a in range(n):
            for t in range(3):
                to_chip(a, t).wait_send()
                to_sibling(a, t).wait_send()

    chip = 2 * lax.axis_index("x") + lax.axis_index("y")
    gathered = pl.pallas_call(
        body, name="gather_weights", in_specs=[ANY] * n, out_specs=[ANY] * n,
        out_shape=[jax.ShapeDtypeStruct((4,) + s.shape, s.dtype) for s in shards],
        scratch_shapes=[pltpu.SemaphoreType.DMA((n, 6)), pltpu.SemaphoreType.DMA((n, 6))],
    )(*shards)
    return [lax.dynamic_update_slice(g, s[None], (chip, 0, 0)) for g, s in zip(gathered, shards)]


def pair_swap_halves(grads):
    n = len(grads)

    def body(*refs):
        ins, theirs = refs[:n], refs[n:2 * n]
        send_sems, recv_sems = refs[2 * n:]
        x, y, c, _ = _place()
        sibling = (x, y, 1 - c)
        for a in range(n):
            rows = ins[a].shape[1] // 2
            for j in range(4):
                for r, nr in _chunks(rows):
                    pltpu.make_async_remote_copy(
                        src_ref=ins[a].at[j, pl.ds((1 - c) * rows + r, nr), :], dst_ref=theirs[a].at[j, pl.ds(r, nr), :],
                        send_sem=send_sems.at[a], recv_sem=recv_sems.at[a], device_id=sibling, device_id_type=MESH).start()
        for a in range(n):
            pltpu.make_async_remote_copy(src_ref=theirs[a], dst_ref=theirs[a], send_sem=send_sems.at[a],
                                         recv_sem=recv_sems.at[a], device_id=sibling, device_id_type=MESH).wait()

    halves = [jax.ShapeDtypeStruct((4, g.shape[1] // 2, g.shape[2]), g.dtype) for g in grads]
    theirs = pl.pallas_call(
        body, name="pair_swap_halves", in_specs=[ANY] * n, out_specs=[ANY] * n, out_shape=halves,
        scratch_shapes=[pltpu.SemaphoreType.DMA((n,)), pltpu.SemaphoreType.DMA((n,))],
    )(*grads)
    c = lax.axis_index("c")
    mine = [lax.dynamic_slice_in_dim(g, c * (g.shape[1] // 2), g.shape[1] // 2, axis=1) for g in grads]
    return mine, theirs


def scatter_to_chips(parts):
    n = len(parts)

    def body(*refs):
        ins, outs = refs[:n], refs[n:2 * n]
        send_sems, recv_sems = refs[2 * n:]
        x, y, c, chips = _place()
        me = 2 * x + y
        blks = [2 * cx + cy for cx, cy in chips]
        for a in range(n):
            for r, nr in _chunks(ins[a].shape[1]):
                for t in range(3):
                    pltpu.make_async_remote_copy(
                        src_ref=ins[a].at[blks[t], pl.ds(r, nr), :], dst_ref=outs[a].at[me, pl.ds(r, nr), :],
                        send_sem=send_sems.at[a, t], recv_sem=recv_sems.at[a, t],
                        device_id=(*chips[t], c), device_id_type=MESH).start()
        for a in range(n):
            for t in range(3):
                pltpu.make_async_remote_copy(
                    src_ref=outs[a].at[blks[t]], dst_ref=outs[a].at[blks[t]], send_sem=send_sems.at[a, t],
                    recv_sem=recv_sems.at[a, t], device_id=(*chips[t], c), device_id_type=MESH).wait()

    landed = pl.pallas_call(
        body, name="scatter_to_chips", in_specs=[ANY] * n, out_specs=[ANY] * n,
        out_shape=[jax.ShapeDtypeStruct(p.shape, p.dtype) for p in parts],
        scratch_shapes=[pltpu.SemaphoreType.DMA((n, 3)), pltpu.SemaphoreType.DMA((n, 3))],
    )(*parts)
    chip = 2 * lax.axis_index("x") + lax.axis_index("y")
    return [lax.dynamic_update_slice(l, lax.dynamic_slice_in_dim(p, chip, 1, axis=0), (chip, 0, 0))
            for l, p in zip(landed, parts)]


def pair_join_halves(halves):
    n = len(halves)

    def body(*refs):
        ins, outs = refs[:n], refs[n:2 * n]
        send_sems, recv_sems = refs[2 * n:]
        x, y, c, _ = _place()
        sibling = (x, y, 1 - c)
        for a in range(n):
            rows = ins[a].shape[0]
            for r, nr in _chunks(rows):
                pltpu.make_async_remote_copy(
                    src_ref=ins[a].at[pl.ds(r, nr), :], dst_ref=outs[a].at[pl.ds(c * rows + r, nr), :],
                    send_sem=send_sems.at[a], recv_sem=recv_sems.at[a], device_id=sibling, device_id_type=MESH).start()
        for a in range(n):
            rows = ins[a].shape[0]
            got = outs[a].at[pl.ds((1 - c) * rows, rows), :]
            pltpu.make_async_remote_copy(src_ref=ins[a], dst_ref=got, send_sem=send_sems.at[a], recv_sem=recv_sems.at[a],
                                         device_id=sibling, device_id_type=MESH).wait()

    joined = pl.pallas_call(
        body, name="pair_join_halves", in_specs=[ANY] * n, out_specs=[ANY] * n,
        out_shape=[jax.ShapeDtypeStruct((2 * h.shape[0], h.shape[1]), h.dtype) for h in halves],
        scratch_shapes=[pltpu.SemaphoreType.DMA((n,)), pltpu.SemaphoreType.DMA((n,))],
    )(*halves)
    c = lax.axis_index("c")
    return [lax.dynamic_update_slice(j, h, (c * h.shape[0], 0)) for j, h in zip(joined, halves)]


def allreduce_small(packed):
    rows = packed.shape[0]

    def body(in_ref, out_ref, gathered, send_sems, recv_sems):
        x, y, c, _ = _place()
        me = 4 * x + 2 * y + c
        gathered[me] = in_ref[...]
        flips = [(fx, fy, fc) for fx in (0, 1) for fy in (0, 1) for fc in (0, 1)][1:]
        peers = [((1 - x) if fx else x, (1 - y) if fy else y, (1 - c) if fc else c) for fx, fy, fc in flips]
        copies = []
        for t, peer in enumerate(peers):
            cp = pltpu.make_async_remote_copy(
                src_ref=in_ref, dst_ref=gathered.at[me], send_sem=send_sems.at[t], recv_sem=recv_sems.at[t],
                device_id=peer, device_id_type=MESH)
            cp.start()
            copies.append(cp)
        for t, (px, py, pc) in enumerate(peers):
            slot = gathered.at[4 * px + 2 * py + pc]
            pltpu.make_async_remote_copy(
                src_ref=slot, dst_ref=slot, send_sem=send_sems.at[t], recv_sem=recv_sems.at[t],
                device_id=(px, py, pc), device_id_type=MESH).wait_recv()
        for cp in copies:
            cp.wait_send()
        acc = gathered[0]
        for k in range(1, 8):
            acc = acc + gathered[k]
        out_ref[...] = acc

    vm = pl.BlockSpec(memory_space=pltpu.VMEM)
    return pl.pallas_call(
        body, name="allreduce_small", in_specs=[vm], out_specs=vm, out_shape=jax.ShapeDtypeStruct(packed.shape, F32),
        scratch_shapes=[pltpu.VMEM((8, rows, 128), F32), pltpu.SemaphoreType.DMA((7,)), pltpu.SemaphoreType.DMA((7,))],
    )(packed)


SMALL_NAMES = ("norm_mix_w", "ssd_conv_w", "ssd_conv_b", "ssd_dt_bias", "ssd_a_log", "ssd_d", "ssd_norm_w", "fox_f_bias",
               "fox_q_norm_w", "fox_k_norm_w", "norm_ffn_w", "ffn_conv_w", "ffn_conv_b")
BIG_NAMES = ("w_in", "w_out", "w_up", "w_down")
WEIGHT_ORDER = ("norm_mix_w", "w_in", "ssd_conv_w", "ssd_conv_b", "ssd_dt_bias", "ssd_a_log", "ssd_d", "ssd_norm_w",
                "fox_f_bias", "fox_q_norm_w", "fox_k_norm_w", "w_out", "norm_ffn_w", "w_up", "ffn_conv_w", "ffn_conv_b", "w_down")
ADAM_ROWS = {"w_in": 256, "w_out": 256, "w_up": 256, "w_down": 176}


def _pack(arrays):
    rows = []
    for a in arrays:
        flat = a.reshape(-1).astype(F32)
        rows.append(jnp.pad(flat, (0, (-flat.shape[0]) % 1024)).reshape(-1, 128))
    return jnp.concatenate(rows, axis=0)


def _unpack(packed, shapes):
    out, r = [], 0
    for shp in shapes:
        size = 1
        for d in shp:
            size *= d
        nrow = 8 * (-(-size // 1024))
        out.append(packed[r:r + nrow].reshape(-1)[:size].reshape(shp))
        r += nrow
    return out


def _pad_rows(a, rows):
    return jnp.pad(a, ((0, rows - a.shape[0]), (0, 0)))


def kernel(x, norm_mix_w, w_in, ssd_conv_w, ssd_conv_b, ssd_dt_bias, ssd_a_log, ssd_d, ssd_norm_w, fox_f_bias, fox_q_norm_w, fox_k_norm_w, w_out, norm_ffn_w, w_up, ffn_conv_w, ffn_conv_b, w_down, loss_target, m_norm_mix_w, m_w_in, m_ssd_conv_w, m_ssd_conv_b, m_ssd_dt_bias, m_ssd_a_log, m_ssd_d, m_ssd_norm_w, m_fox_f_bias, m_fox_q_norm_w, m_fox_k_norm_w, m_w_out, m_norm_ffn_w, m_w_up, m_ffn_conv_w, m_ffn_conv_b, m_w_down, v_norm_mix_w, v_w_in, v_ssd_conv_w, v_ssd_conv_b, v_ssd_dt_bias, v_ssd_a_log, v_ssd_d, v_ssd_norm_w, v_fox_f_bias, v_fox_q_norm_w, v_fox_k_norm_w, v_w_out, v_norm_ffn_w, v_w_up, v_ffn_conv_w, v_ffn_conv_b, v_w_down):
    w = dict(norm_mix_w=norm_mix_w, w_in=w_in, ssd_conv_w=ssd_conv_w, ssd_conv_b=ssd_conv_b, ssd_dt_bias=ssd_dt_bias,
             ssd_a_log=ssd_a_log, ssd_d=ssd_d, ssd_norm_w=ssd_norm_w, fox_f_bias=fox_f_bias, fox_q_norm_w=fox_q_norm_w,
             fox_k_norm_w=fox_k_norm_w, w_out=w_out, norm_ffn_w=norm_ffn_w, w_up=w_up, ffn_conv_w=ffn_conv_w,
             ffn_conv_b=ffn_conv_b, w_down=w_down)
    m = dict(norm_mix_w=m_norm_mix_w, w_in=m_w_in, ssd_conv_w=m_ssd_conv_w, ssd_conv_b=m_ssd_conv_b, ssd_dt_bias=m_ssd_dt_bias,
             ssd_a_log=m_ssd_a_log, ssd_d=m_ssd_d, ssd_norm_w=m_ssd_norm_w, fox_f_bias=m_fox_f_bias, fox_q_norm_w=m_fox_q_norm_w,
             fox_k_norm_w=m_fox_k_norm_w, w_out=m_w_out, norm_ffn_w=m_norm_ffn_w, w_up=m_w_up, ffn_conv_w=m_ffn_conv_w,
             ffn_conv_b=m_ffn_conv_b, w_down=m_w_down)
    v = dict(norm_mix_w=v_norm_mix_w, w_in=v_w_in, ssd_conv_w=v_ssd_conv_w, ssd_conv_b=v_ssd_conv_b, ssd_dt_bias=v_ssd_dt_bias,
             ssd_a_log=v_ssd_a_log, ssd_d=v_ssd_d, ssd_norm_w=v_ssd_norm_w, fox_f_bias=v_fox_f_bias, fox_q_norm_w=v_fox_q_norm_w,
             fox_k_norm_w=v_fox_k_norm_w, w_out=v_w_out, norm_ffn_w=v_norm_ffn_w, w_up=v_w_up, ffn_conv_w=v_ffn_conv_w,
             ffn_conv_b=v_ffn_conv_b, w_down=v_w_down)
    chip = 2 * lax.axis_index("x") + lax.axis_index("y")

    shards = [w_in[0].astype(BF16), w_out[0].astype(BF16), w_up[0].astype(BF16), w_down[0].astype(BF16),
              _pad_rows(ssd_conv_w[0], 16), _pad_rows(ffn_conv_w[0], 16)]
    a_in, a_out, a_up, a_down, a_scw, a_fcw = gather_weights(shards)
    w_full = a_in.transpose(1, 0, 2).reshape(D_MODEL, IN_COLS)
    wm = jnp.concatenate([w_full[:, :2048], w_full[:, 2576:5648], w_full[:, 2048:2560]], axis=1)
    ws = jnp.concatenate([w_full[:, 2560:2576], w_full[:, 5648:5664], jnp.zeros((D_MODEL, SMALL_COLS - 32), BF16)], axis=1)
    wo = a_out.reshape(2048, D_MODEL)
    wu = a_up.transpose(1, 0, 2).reshape(D_MODEL, 2 * D_FF)
    wd = a_down.reshape(D_FF, D_MODEL)
    ssd_cw8 = a_scw.transpose(1, 0, 2).reshape(16, 1536)[:8]
    ffn_cw8 = a_fcw.transpose(1, 0, 2).reshape(16, 2 * D_FF)[:8]
    smallp = jnp.zeros((8, 128), F32)
    smallp = smallp.at[0, :16].set(ssd_dt_bias[0]).at[1, :16].set(ssd_a_log[0]).at[2, :16].set(ssd_d[0])
    smallp = smallp.at[3, F_LANE:F_LANE + 16].set(fox_f_bias[0])
    qw_t = jnp.tile(fox_q_norm_w[0], N_HEADS)[None]
    kw_t = jnp.tile(fox_k_norm_w[0], N_HEADS)[None]
    sel = (jnp.arange(1024)[:, None] // HEAD_DIM == jnp.arange(128)[None, :]).astype(BF16)

    res = local_step(x[0], loss_target[0], wm, ws, wo, wu, wd, ssd_cw8, ssd_conv_b, smallp, ssd_norm_w, qw_t, kw_t,
                     sel, sel.T, norm_mix_w, norm_ffn_w, ffn_cw8, ffn_conv_b)

    full_shapes = [(1, 1024), (1, 4, 1536), (1, 1536), (1, 16), (1, 16), (1, 16), (1, 1024), (1, 16), (1, 64), (1, 64),
                   (1, 1024), (1, 3, 2 * D_FF), (1, 2 * D_FF), (1,)]
    local_small = [res["g_norm_mix"], res["g_ssd_cw"][:4], res["g_ssd_cw"][4], res["g_sp"][0, :16], res["g_sp"][1, :16],
                   res["g_sp"][2, :16], res["g_ssd_nw"], res["g_fb"][0, F_LANE:F_LANE + 16],
                   res["g_qw"].reshape(N_HEADS, HEAD_DIM).sum(0), res["g_kw"].reshape(N_HEADS, HEAD_DIM).sum(0),
                   res["g_norm_ffn"], res["g_ffn_cw"][:3], res["g_ffn_cw"][3], jnp.sum(res["sq"])]
    summed = _unpack(allreduce_small(_pack(local_small)), full_shapes)
    loss = (0.5 / D_MODEL) * summed[-1][0]
    g_small = dict(zip(SMALL_NAMES, summed[:-1]))
    g_small["ssd_conv_w"] = lax.dynamic_slice(g_small["ssd_conv_w"], (0, 0, 384 * chip), (1, 4, 384))
    g_small["ffn_conv_w"] = lax.dynamic_slice(g_small["ffn_conv_w"], (0, 0, 1408 * chip), (1, 3, 1408))

    g_wm, g_ws = res["g_wm"], res["g_ws"]
    g_in_full = jnp.concatenate([g_wm[:, :2048], g_wm[:, 5120:5632], g_ws[:, :16], g_wm[:, 2048:5120], g_ws[:, 16:32]], axis=1)
    big = [g_in_full.reshape(D_MODEL, 4, 1416).transpose(1, 0, 2), res["g_out"].reshape(4, 512, D_MODEL),
           res["g_up"].reshape(D_MODEL, 4, 1408).transpose(1, 0, 2), res["g_down"].reshape(4, 704, D_MODEL)]
    mine, theirs = pair_swap_halves(big)
    parts = [add_pair(a, b, name="add_pair_" + n, tr=ADAM_ROWS[n]) for a, b, n in zip(mine, theirs, BIG_NAMES)]
    landed = scatter_to_chips(parts)
    halves = [sum_chips(p, name="sum_chips_" + n, tr=ADAM_ROWS[n]) for p, n in zip(landed, BIG_NAMES)]
    g_big = dict(zip(BIG_NAMES, pair_join_halves(halves)))

    grads, deltas, new_m, new_v = {}, {}, {}, {}
    for n in BIG_NAMES:
        d, mn, vn = adamw(w[n][0], g_big[n], m[n][0], v[n][0], name="adamw_" + n, tr=ADAM_ROWS[n])
        grads[n], deltas[n], new_m[n], new_v[n] = g_big[n][None], d[None], mn[None], vn[None]
    shapes = [w[n].shape for n in SMALL_NAMES]
    d, mn, vn = adamw(_pack([w[n] for n in SMALL_NAMES]), _pack([g_small[n] for n in SMALL_NAMES]),
                      _pack([m[n] for n in SMALL_NAMES]), _pack([v[n] for n in SMALL_NAMES]), name="adamw_small", tr=8)
    for n, dd, mm, vv in zip(SMALL_NAMES, _unpack(d, shapes), _unpack(mn, shapes), _unpack(vn, shapes)):
        grads[n], deltas[n], new_m[n], new_v[n] = g_small[n].reshape(w[n].shape), dd, mm, vv
    return (loss, res["grad_x"][None], *[grads[n] for n in WEIGHT_ORDER], *[deltas[n] for n in WEIGHT_ORDER],
            *[new_m[n] for n in WEIGHT_ORDER], *[new_v[n] for n in WEIGHT_ORDER])
```

```python
import functools

import jax
import jax.numpy as jnp
from jax import lax
from jax.experimental import pallas as pl
from jax.experimental.pallas import tpu as pltpu

F32 = jnp.float32
BF16 = jnp.bfloat16
MESH = pl.DeviceIdType.MESH

D_MODEL = 1024
HEAD_DIM = 64
N_HEADS = 16
N_PAIRS = N_HEADS // 2
SSD_CHUNK = 128
SSD_STATE = 128
SSD_CONV = 4
D_FF = 2816
FFN_CONV = 3
NORM_EPS = 1e-6
MAIN_COLS = 5632
SMALL_COLS = 128
F_LANE = 16
IN_COLS = 5664

ADAM_LR = 0.001
ADAM_B1 = 0.9
ADAM_B2 = 0.999
ADAM_EPS = 1e-08
ADAM_WD = 0.01
ADAM_STEP = 10

VMEM_LIMIT_V7X = 56 * 1024 * 1024
NEG_BIG = -1e30


def _params(sem=None):
    return pltpu.CompilerParams(dimension_semantics=sem, vmem_limit_bytes=VMEM_LIMIT_V7X)


def _sigmoid(x):
    return 1.0 / (1.0 + jnp.exp(-x))


def _silu_and_grad(x):
    s = _sigmoid(x)
    return x * s, s * (1.0 + x * (1.0 - s))


def _shift_down(v, j):
    return v if j == 0 else pltpu.roll(v, j, 0)


def _shift_up(v, j):
    return v if j == 0 else pltpu.roll(v, v.shape[0] - j, 0)


def _row_iota(shape):
    return lax.broadcasted_iota(jnp.int32, shape, 0)


def _lane_iota(shape):
    return lax.broadcasted_iota(jnp.int32, shape, 1)


def _dot(a, b, mode="nn"):
    dims = {"nn": (((1,), (0,)), ((), ())), "nt": (((1,), (1,)), ((), ())), "tn": (((0,), (0,)), ((), ()))}[mode]
    return lax.dot_general(a.astype(BF16), b.astype(BF16), dims, preferred_element_type=F32)


def _dot_f32(a, b):
    return jnp.dot(a, b, precision=lax.Precision.HIGHEST, preferred_element_type=F32)


def matmul(a, b, *, mode, tm, tn, tk, out_dtype, name, add=None, b_koff=0):
    (m, k), n = a.shape, (b.shape[1] if mode == "nn" else b.shape[0])
    assert m % tm == 0 and n % tn == 0 and k % tk == 0, (name, m, n, k, tm, tn, tk)
    nk = k // tk
    a_spec = pl.BlockSpec((tm, tk), lambda i, j, kk: (i, kk))
    b_spec = (pl.BlockSpec((tn, tk), lambda i, j, kk: (j, kk + b_koff)) if mode == "nt"
              else pl.BlockSpec((tk, tn), lambda i, j, kk: (kk + b_koff, j)))
    o_spec = pl.BlockSpec((tm, tn), lambda i, j, kk: (i, j))
    has_add = add is not None

    def body(*refs):
        if has_add:
            a_ref, b_ref, add_ref, o_ref, acc_ref = refs
        else:
            a_ref, b_ref, o_ref, acc_ref = refs
        kk = pl.program_id(2)
        part = _dot(a_ref[...], b_ref[...], mode)

        def finish(total):
            if has_add:
                total = total + add_ref[...]
            o_ref[...] = total.astype(out_dtype)

        if nk == 1:
            finish(part)
        else:
            @pl.when(kk == 0)
            def _():
                acc_ref[...] = part

            @pl.when(jnp.logical_and(kk > 0, kk < nk - 1))
            def _():
                acc_ref[...] += part

            @pl.when(kk == nk - 1)
            def _():
                finish(acc_ref[...] + part)

    in_specs = [a_spec, b_spec] + ([o_spec] if has_add else [])
    args = (a, b) + ((add,) if has_add else ())
    return pl.pallas_call(
        body, name=name, grid=(m // tm, n // tn, nk), in_specs=in_specs, out_specs=o_spec,
        out_shape=jax.ShapeDtypeStruct((m, n), out_dtype),
        scratch_shapes=[pltpu.VMEM((tm, tn) if nk > 1 else (8, 128), F32)],
        compiler_params=_params(("parallel", "parallel", "arbitrary")),
    )(*args)


def rms_fwd(x, w, *, name, tm=512):
    s, d = x.shape

    def body(x_ref, w_ref, h_ref, ht_ref):
        xv = x_ref[...]
        r = lax.rsqrt(jnp.mean(xv * xv, axis=-1, keepdims=True) + NORM_EPS)
        h = (xv * r) * w_ref[...]
        h_ref[...] = h.astype(BF16)
        ht_ref[...] = h.T.astype(BF16)

    return pl.pallas_call(
        body, name=name, grid=(s // tm,),
        in_specs=[pl.BlockSpec((tm, d), lambda i: (i, 0)), pl.BlockSpec((1, d), lambda i: (0, 0))],
        out_specs=[pl.BlockSpec((tm, d), lambda i: (i, 0)), pl.BlockSpec((d, tm), lambda i: (0, i))],
        out_shape=[jax.ShapeDtypeStruct((s, d), BF16), jax.ShapeDtypeStruct((d, s), BF16)],
        compiler_params=_params(("parallel",)),
    )(x, w)


def rms_bwd(dh, x, w, resid, *, name, tm=512):
    s, d = x.shape

    def body(dh_ref, x_ref, w_ref, res_ref, dx_ref, dw_ref):
        xv = x_ref[...]
        dhv = dh_ref[...]
        r = lax.rsqrt(jnp.mean(xv * xv, axis=-1, keepdims=True) + NORM_EPS)
        xh = xv * r
        g = dhv * w_ref[...]
        dx_ref[...] = res_ref[...] + r * (g - xh * jnp.mean(g * xh, axis=-1, keepdims=True))
        part = jnp.sum(dhv * xh, axis=0, keepdims=True)

        @pl.when(pl.program_id(0) == 0)
        def _():
            dw_ref[...] = part

        @pl.when(pl.program_id(0) > 0)
        def _():
            dw_ref[...] += part

    row = pl.BlockSpec((tm, d), lambda i: (i, 0))
    vec = pl.BlockSpec((1, d), lambda i: (0, 0))
    return pl.pallas_call(
        body, name=name, grid=(s // tm,), in_specs=[row, row, vec, row], out_specs=[row, vec],
        out_shape=[jax.ShapeDtypeStruct((s, d), F32), jax.ShapeDtypeStruct((1, d), F32)],
        compiler_params=_params(("arbitrary",)),
    )(dh, x, w, resid)


def loss_head(y, target, *, tm=512):
    s, d = y.shape

    def body(y_ref, t_ref, dy_ref, sq_ref):
        e = y_ref[...] - t_ref[...]
        dy_ref[...] = e / float(d)
        part = jnp.sum(e * e, axis=0, keepdims=True)

        @pl.when(pl.program_id(0) == 0)
        def _():
            sq_ref[...] = part

        @pl.when(pl.program_id(0) > 0)
        def _():
            sq_ref[...] += part

    row = pl.BlockSpec((tm, d), lambda i: (i, 0))
    vec = pl.BlockSpec((1, d), lambda i: (0, 0))
    return pl.pallas_call(
        body, name="loss_head", grid=(s // tm,), in_specs=[row, row], out_specs=[row, vec],
        out_shape=[jax.ShapeDtypeStruct((s, d), F32), jax.ShapeDtypeStruct((1, d), F32)],
        compiler_params=_params(("arbitrary",)),
    )(y, target)


def _conv_rows(ext, w, k_taps):
    acc = w[k_taps - 1:k_taps, :] * ext
    for k in range(k_taps - 1):
        acc = acc + w[k:k + 1, :] * _shift_down(ext, k_taps - 1 - k)
    return acc


def _conv_rows_transposed(dext, w, k_taps):
    acc = w[k_taps - 1:k_taps, :] * dext
    for k in range(k_taps - 1):
        acc = acc + w[k:k + 1, :] * _shift_up(dext, k_taps - 1 - k)
    return acc


def _stack_rows(rows, width):
    ri = _row_iota((8, width))
    out = jnp.zeros((8, width), F32)
    for k, r in enumerate(rows):
        out = out + jnp.where(ri == k, r, 0.0)
    return out


def ffn_mid_fwd(hu, conv_w8, conv_b, *, tm=512, tc=256):
    s = hu.shape[0]
    ncol = D_FF // tc
    r8 = tm // 8

    def body(g_ref, v_ref, gp_ref, vp_ref, wg_ref, wv_ref, bg_ref, bv_ref, o_ref, ot_ref):
        first = pl.program_id(1) == 0

        def conv(cur_ref, prev_ref, w_ref, b_ref):
            prev = jnp.where(first, 0.0, prev_ref[...])
            ext = jnp.concatenate([prev, cur_ref[...]], axis=0)
            return _conv_rows(ext, w_ref[...], FFN_CONV)[8:] + b_ref[...]

        gc = conv(g_ref, gp_ref, wg_ref, bg_ref)
        vc = conv(v_ref, vp_ref, wv_ref, bv_ref)
        act = gc * _sigmoid(gc) * vc
        o_ref[...] = act.astype(BF16)
        ot_ref[...] = act.T.astype(BF16)

    def prev_idx(i):
        return jnp.maximum(i * r8 - 1, 0)

    in_specs = [
        pl.BlockSpec((tm, tc), lambda j, i: (i, j)),
        pl.BlockSpec((tm, tc), lambda j, i: (i, j + ncol)),
        pl.BlockSpec((8, tc), lambda j, i: (prev_idx(i), j)),
        pl.BlockSpec((8, tc), lambda j, i: (prev_idx(i), j + ncol)),
        pl.BlockSpec((8, tc), lambda j, i: (0, j)),
        pl.BlockSpec((8, tc), lambda j, i: (0, j + ncol)),
        pl.BlockSpec((1, tc), lambda j, i: (0, j)),
        pl.BlockSpec((1, tc), lambda j, i: (0, j + ncol)),
    ]
    return pl.pallas_call(
        body, name="ffn_mid_fwd", grid=(ncol, s // tm), in_specs=in_specs,
        out_specs=[pl.BlockSpec((tm, tc), lambda j, i: (i, j)), pl.BlockSpec((tc, tm), lambda j, i: (j, i))],
        out_shape=[jax.ShapeDtypeStruct((s, D_FF), BF16), jax.ShapeDtypeStruct((D_FF, s), BF16)],
        compiler_params=_params(("parallel", "parallel")),
    )(hu, hu, hu, hu, conv_w8, conv_w8, conv_b, conv_b)


def ffn_mid_bwd(hu, dact, conv_w8, conv_b, *, tm=512, tc=256):
    s = hu.shape[0]
    ncol = D_FF // tc
    nrow = s // tm
    r8 = tm // 8

    def body(g_ref, v_ref, gp_ref, vp_ref, gn_ref, vn_ref, da_ref, dan_ref, wg_ref, wv_ref, bg_ref, bv_ref,
             dg_ref, dv_ref, wgo_ref, wvo_ref):
        i = pl.program_id(1)
        first = i == 0
        last = i == nrow - 1

        def ext_of(cur_ref, prev_ref, next_ref):
            prev = jnp.where(first, 0.0, prev_ref[...])
            return jnp.concatenate([prev, cur_ref[...], next_ref[...]], axis=0)

        g_ext = ext_of(g_ref, gp_ref, gn_ref)
        v_ext = ext_of(v_ref, vp_ref, vn_ref)
        gc = _conv_rows(g_ext, wg_ref[...], FFN_CONV) + bg_ref[...]
        vc = _conv_rows(v_ext, wv_ref[...], FFN_CONV) + bv_ref[...]
        da_ext = jnp.concatenate([jnp.zeros((8, tc), F32), da_ref[...], jnp.where(last, 0.0, dan_ref[...])], axis=0)
        silu, dsilu = _silu_and_grad(gc)
        dgc = da_ext * vc * dsilu
        dvc = da_ext * silu
        dg_ref[...] = _conv_rows_transposed(dgc, wg_ref[...], FFN_CONV)[8:8 + tm].astype(BF16)
        dv_ref[...] = _conv_rows_transposed(dvc, wv_ref[...], FFN_CONV)[8:8 + tm].astype(BF16)

        def wgrad(dcur, x_ext):
            rows = [jnp.sum(dcur * _shift_down(x_ext, FFN_CONV - 1 - k)[8:8 + tm], axis=0, keepdims=True)
                    for k in range(FFN_CONV)]
            rows.append(jnp.sum(dcur, axis=0, keepdims=True))
            return _stack_rows(rows, tc)

        pg = wgrad(dgc[8:8 + tm], g_ext)
        pv = wgrad(dvc[8:8 + tm], v_ext)

        @pl.when(first)
        def _():
            wgo_ref[...] = pg
            wvo_ref[...] = pv

        @pl.when(i > 0)
        def _():
            wgo_ref[...] += pg
            wvo_ref[...] += pv

    def prev_idx(i):
        return jnp.maximum(i * r8 - 1, 0)

    def next_idx(i):
        return jnp.minimum((i + 1) * r8, s // 8 - 1)

    cur_g = pl.BlockSpec((tm, tc), lambda j, i: (i, j))
    cur_v = pl.BlockSpec((tm, tc), lambda j, i: (i, j + ncol))
    in_specs = [
        cur_g, cur_v,
        pl.BlockSpec((8, tc), lambda j, i: (prev_idx(i), j)),
        pl.BlockSpec((8, tc), lambda j, i: (prev_idx(i), j + ncol)),
        pl.BlockSpec((8, tc), lambda j, i: (next_idx(i), j)),
        pl.BlockSpec((8, tc), lambda j, i: (next_idx(i), j + ncol)),
        cur_g,
        pl.BlockSpec((8, tc), lambda j, i: (next_idx(i), j)),
        pl.BlockSpec((8, tc), lambda j, i: (0, j)),
        pl.BlockSpec((8, tc), lambda j, i: (0, j + ncol)),
        pl.BlockSpec((1, tc), lambda j, i: (0, j)),
        pl.BlockSpec((1, tc), lambda j, i: (0, j + ncol)),
    ]
    out_specs = [cur_g, cur_g, pl.BlockSpec((8, tc), lambda j, i: (0, j)), pl.BlockSpec((8, tc), lambda j, i: (0, j))]
    out_shape = [jax.ShapeDtypeStruct((s, D_FF), BF16), jax.ShapeDtypeStruct((s, D_FF), BF16),
                 jax.ShapeDtypeStruct((8, D_FF), F32), jax.ShapeDtypeStruct((8, D_FF), F32)]
    return pl.pallas_call(
        body, name="ffn_mid_bwd", grid=(ncol, nrow), in_specs=in_specs, out_specs=out_specs, out_shape=out_shape,
        compiler_params=_params(("parallel", "arbitrary")),
    )(hu, hu, hu, hu, hu, hu, dact, dact, conv_w8, conv_w8, conv_b, conv_b)


def _softplus(x):
    return jnp.maximum(x, 0.0) + jnp.log(1.0 + jnp.exp(-jnp.abs(x)))


def _cumsum_rows(v):
    n = v.shape[0]
    ri = _row_iota(v.shape)
    sh = 1
    while sh < n:
        v = v + jnp.where(ri >= sh, _shift_down(v, sh), 0.0)
        sh *= 2
    return v


def _rev_cumsum_rows(v):
    n = v.shape[0]
    ri = _row_iota(v.shape)
    sh = 1
    while sh < n:
        v = v + jnp.where(ri < n - sh, _shift_up(v, sh), 0.0)
        sh *= 2
    return v


def _half_row_sums(v, lo):
    s0 = jnp.sum(jnp.where(lo, v, 0.0), axis=1, keepdims=True)
    return s0, jnp.sum(v, axis=1, keepdims=True) - s0


def _total(v):
    return jnp.sum(jnp.sum(v, axis=1, keepdims=True), axis=0, keepdims=True)


def _ssd_in_specs(rev_nc=None):
    def ch(c):
        return c if rev_nc is None else rev_nc - 1 - c

    def prev(c):
        return jnp.maximum(ch(c) * (SSD_CHUNK // 8) - 1, 0)

    L = SSD_CHUNK
    return [
        pl.BlockSpec((L, 1024), lambda c: (ch(c), 0)),
        pl.BlockSpec((L, 1024), lambda c: (ch(c), 1)),
        pl.BlockSpec((L, 256), lambda c: (ch(c), 20)),
        pl.BlockSpec((L, 256), lambda c: (ch(c), 21)),
        pl.BlockSpec((8, 1024), lambda c: (prev(c), 1)),
        pl.BlockSpec((8, 256), lambda c: (prev(c), 20)),
        pl.BlockSpec((8, 256), lambda c: (prev(c), 21)),
        pl.BlockSpec((8, 1024), lambda c: (0, 0)),
        pl.BlockSpec((8, 256), lambda c: (0, 4)),
        pl.BlockSpec((8, 256), lambda c: (0, 5)),
        pl.BlockSpec((1, 1024), lambda c: (0, 0)),
        pl.BlockSpec((1, 256), lambda c: (0, 4)),
        pl.BlockSpec((1, 256), lambda c: (0, 5)),
        pl.BlockSpec((L, SMALL_COLS), lambda c: (ch(c), 0)),
        pl.BlockSpec((8, 128), lambda c: (0, 0)),
        pl.BlockSpec((1, 1024), lambda c: (0, 0)),
    ]


def _ssd_conv_pre(cur_ref, prev_ref, w_ref, b_ref, first):
    prev = jnp.where(first, 0.0, prev_ref[...])
    ext = jnp.concatenate([prev, cur_ref[...]], axis=0)
    return ext, _conv_rows(ext, w_ref[...], SSD_CONV)[8:] + b_ref[...]


def _ssd_time_consts(small_ref, sp_ref):
    dt_pre = small_ref[...] + sp_ref[0:1, :]
    dt = _softplus(dt_pre)
    a = -jnp.exp(sp_ref[1:2, :])
    acs = _cumsum_rows(dt * a)
    return dt_pre, dt, a, acs


def ssd_fwd(proj, small, conv_w8, conv_b, smallp, norm_w):
    s = proj.shape[0]
    nc = s // SSD_CHUNK
    L = SSD_CHUNK

    def body(z_ref, xs_ref, b_ref, c_ref, xsp_ref, bp_ref, cp_ref, wx_ref, wb_ref, wc_ref, bx_ref, bb_ref, bc_ref,
             small_ref, sp_ref, nw_ref, y_ref, yt_ref, ypre_ref, st_ref, state):
        first = pl.program_id(0) == 0

        @pl.when(first)
        def _():
            state[...] = jnp.zeros_like(state)

        xs = _ssd_conv_pre(xs_ref, xsp_ref, wx_ref, bx_ref, first)[1]
        xs = xs * _sigmoid(xs)
        bm = _ssd_conv_pre(b_ref, bp_ref, wb_ref, bb_ref, first)[1]
        bm = bm * _sigmoid(bm)
        cm = _ssd_conv_pre(c_ref, cp_ref, wc_ref, bc_ref, first)[1]
        cm = cm * _sigmoid(cm)
        _, dt, _, acs = _ssd_time_consts(small_ref, sp_ref)
        acs_t = acs.T
        li = _lane_iota((L, L))
        ri = _row_iota((L, L))
        tri = ri >= li
        lo = li < HEAD_DIM
        st_ref[0] = state[...]
        for g in range(2):
            bg = bm[:, 128 * g:128 * g + 128]
            cg = cm[:, 128 * g:128 * g + 128]
            gmat = _dot(cg, bg, "nt")
            for pp in range(4):
                p = 4 * g + pp
                h0, h1 = 2 * p, 2 * p + 1
                x = xs[:, 128 * p:128 * p + 128]
                a0, a1 = acs[:, h0:h0 + 1], acs[:, h1:h1 + 1]
                xdt = x * jnp.where(lo, dt[:, h0:h0 + 1], dt[:, h1:h1 + 1])
                m0 = gmat * jnp.exp(jnp.where(tri, a0 - acs_t[h0:h0 + 1, :], NEG_BIG))
                m1 = gmat * jnp.exp(jnp.where(tri, a1 - acs_t[h1:h1 + 1, :], NEG_BIG))
                yd = _dot(m0, jnp.where(lo, xdt, 0.0)) + _dot(m1, jnp.where(lo, 0.0, xdt))
                hin = state[p]
                yo = _dot(cg, hin, "nt") * jnp.exp(jnp.where(lo, a0, a1))
                dskip = jnp.where(lo[0:1], sp_ref[2:3, h0:h0 + 1], sp_ref[2:3, h1:h1 + 1])
                ypre_ref[:, 128 * p:128 * p + 128] = yd + yo + dskip * x
                al0, al1 = acs[L - 1:L, h0:h0 + 1], acs[L - 1:L, h1:h1 + 1]
                w = jnp.exp(jnp.where(lo, al0 - a0, al1 - a1))
                dec = jnp.exp(jnp.where(ri < HEAD_DIM, al0, al1))
                state[p] = dec * hin + _dot(xdt * w, bg, "tn")
        z = z_ref[...]
        yg = ypre_ref[...] * (z * _sigmoid(z))
        for g in range(2):
            seg = yg[:, 512 * g:512 * g + 512]
            r = lax.rsqrt(jnp.mean(seg * seg, axis=-1, keepdims=True) + NORM_EPS)
            out = (seg * r) * nw_ref[:, 512 * g:512 * g + 512]
            y_ref[:, 512 * g:512 * g + 512] = out.astype(BF16)
            yt_ref[512 * g:512 * g + 512, :] = out.T.astype(BF16)

    row = pl.BlockSpec((L, 1024), lambda c: (c, 0))
    return pl.pallas_call(
        body, name="ssd_fwd", grid=(nc,), in_specs=_ssd_in_specs(),
        out_specs=[row, pl.BlockSpec((1024, L), lambda c: (0, c)), row,
                   pl.BlockSpec((1, N_PAIRS, 128, 128), lambda c: (c, 0, 0, 0))],
        out_shape=[jax.ShapeDtypeStruct((s, 1024), BF16), jax.ShapeDtypeStruct((1024, s), BF16),
                   jax.ShapeDtypeStruct((s, 1024), F32), jax.ShapeDtypeStruct((nc, N_PAIRS, 128, 128), F32)],
        scratch_shapes=[pltpu.VMEM((N_PAIRS, 128, 128), F32)],
        compiler_params=_params(("arbitrary",)),
    )(proj, proj, proj, proj, proj, proj, proj, conv_w8, conv_w8, conv_w8, conv_b, conv_b, conv_b, small, smallp, norm_w)


def ssd_bwd(proj, small, conv_w8, conv_b, smallp, norm_w, ypre, states, dy):
    s = proj.shape[0]
    nc = s // SSD_CHUNK
    L = SSD_CHUNK

    def body(z_ref, xs_ref, b_ref, c_ref, xsp_ref, bp_ref, cp_ref, wx_ref, wb_ref, wc_ref, bx_ref, bb_ref, bc_ref,
             small_ref, sp_ref, nw_ref, ypre_ref, st_ref, dy_ref,
             dz_ref, dxs_ref, db_ref, dc_ref, dsmall_ref, gwx_ref, gwb_ref, gwc_ref, gsp_ref, gnw_ref,
             dstate, carry_x, carry_b, carry_c, dxs_buf, dbm_buf, dcm_buf):
        step = pl.program_id(0)
        first_chunk = step == nc - 1
        start = step == 0

        @pl.when(start)
        def _():
            dstate[...] = jnp.zeros_like(dstate)
            carry_x[...] = jnp.zeros_like(carry_x)
            carry_b[...] = jnp.zeros_like(carry_b)
            carry_c[...] = jnp.zeros_like(carry_c)

        xs_ext, xs_pre = _ssd_conv_pre(xs_ref, xsp_ref, wx_ref, bx_ref, first_chunk)
        b_ext, b_pre = _ssd_conv_pre(b_ref, bp_ref, wb_ref, bb_ref, first_chunk)
        c_ext, c_pre = _ssd_conv_pre(c_ref, cp_ref, wc_ref, bc_ref, first_chunk)
        xs, xs_ds = _silu_and_grad(xs_pre)
        bm, b_ds = _silu_and_grad(b_pre)
        cm, c_ds = _silu_and_grad(c_pre)
        dt_pre, dt, a, acs = _ssd_time_consts(small_ref, sp_ref)
        acs_t = acs.T
        li = _lane_iota((L, L))
        ri = _row_iota((L, L))
        tri = ri >= li
        lo = li < HEAD_DIM
        lo_rows = ri < HEAD_DIM
        li1 = _lane_iota((1, L))

        z = z_ref[...]
        sz, dsz = _silu_and_grad(z)
        y = ypre_ref[...]
        yg = y * sz
        dout = dy_ref[...]
        dyg_parts = []
        gnw_parts = []
        for g in range(2):
            sl = slice(512 * g, 512 * g + 512)
            seg = yg[:, sl]
            r = lax.rsqrt(jnp.mean(seg * seg, axis=-1, keepdims=True) + NORM_EPS)
            n = seg * r
            gnw_parts.append(jnp.sum(dout[:, sl] * n, axis=0, keepdims=True))
            gg = dout[:, sl] * nw_ref[:, sl]
            dyg_parts.append(r * (gg - n * jnp.mean(gg * n, axis=-1, keepdims=True)))
        dyg = jnp.concatenate(dyg_parts, axis=1)
        gnw = jnp.concatenate(gnw_parts, axis=1)
        dz_ref[...] = (dyg * y * dsz).astype(BF16)
        dypre = dyg * sz

        ddt = jnp.zeros((L, L), F32)
        dacs = jnp.zeros((L, L), F32)
        dacs_t = jnp.zeros((L, L), F32)
        dalast = jnp.zeros((1, L), F32)
        dskip_g = jnp.zeros((1, L), F32)
        for g in range(2):
            bg = bm[:, 128 * g:128 * g + 128]
            cg = cm[:, 128 * g:128 * g + 128]
            gmat = _dot(cg, bg, "nt")
            dgmat = jnp.zeros((L, L), F32)
            dbg = jnp.zeros((L, L), F32)
            dcg = jnp.zeros((L, L), F32)
            for pp in range(4):
                p = 4 * g + pp
                h0, h1 = 2 * p, 2 * p + 1
                x = xs[:, 128 * p:128 * p + 128]
                dyp = dypre[:, 128 * p:128 * p + 128]
                a0, a1 = acs[:, h0:h0 + 1], acs[:, h1:h1 + 1]
                dtl = jnp.where(lo, dt[:, h0:h0 + 1], dt[:, h1:h1 + 1])
                xdt = x * dtl
                l0 = jnp.exp(jnp.where(tri, a0 - acs_t[h0:h0 + 1, :], NEG_BIG))
                l1 = jnp.exp(jnp.where(tri, a1 - acs_t[h1:h1 + 1, :], NEG_BIG))
                m0, m1 = gmat * l0, gmat * l1
                dskip = jnp.where(lo[0:1], sp_ref[2:3, h0:h0 + 1], sp_ref[2:3, h1:h1 + 1])
                s0, s1 = _half_row_sums(dyp * x, lo)
                dskip_g = dskip_g + jnp.where(li1 == h0, _total(s0), 0.0) + jnp.where(li1 == h1, _total(s1), 0.0)
                dx = dyp * dskip
                dy0, dy1 = jnp.where(lo, dyp, 0.0), jnp.where(lo, 0.0, dyp)
                x0, x1 = jnp.where(lo, xdt, 0.0), jnp.where(lo, 0.0, xdt)
                dm0, dm1 = _dot(dy0, x0, "nt"), _dot(dy1, x1, "nt")
                dxdt = _dot(m0, dy0, "tn") + _dot(m1, dy1, "tn")
                q0, q1 = dm0 * m0, dm1 * m1
                dacs = dacs + jnp.where(li == h0, jnp.sum(q0, axis=1, keepdims=True), 0.0) \
                            + jnp.where(li == h1, jnp.sum(q1, axis=1, keepdims=True), 0.0)
                dacs_t = dacs_t - jnp.where(ri == h0, jnp.sum(q0, axis=0, keepdims=True), 0.0) \
                                - jnp.where(ri == h1, jnp.sum(q1, axis=0, keepdims=True), 0.0)
                dgmat = dgmat + dm0 * l0 + dm1 * l1
                hin = st_ref[0, p]
                e = jnp.exp(jnp.where(lo, a0, a1))
                ch = _dot(cg, hin, "nt")
                dch = dyp * e
                dcg = dcg + _dot(dch, hin)
                dhin = _dot(dch, cg, "tn")
                s0, s1 = _half_row_sums(dch * ch, lo)
                dacs = dacs + jnp.where(li == h0, s0, 0.0) + jnp.where(li == h1, s1, 0.0)
                dhout = dstate[p]
                al0, al1 = acs[L - 1:L, h0:h0 + 1], acs[L - 1:L, h1:h1 + 1]
                dec = jnp.exp(jnp.where(lo_rows, al0, al1))
                dhin = dhin + dec * dhout
                dal = dhout * hin * dec
                dal0 = _total(jnp.where(lo_rows, dal, 0.0))
                dal1 = _total(dal) - dal0
                w = jnp.exp(jnp.where(lo, al0 - a0, al1 - a1))
                xw = xdt * w
                dxw = _dot(bg, dhout, "nt")
                dbg = dbg + _dot(xw, dhout)
                dxdt = dxdt + dxw * w
                s0, s1 = _half_row_sums(dxw * xw, lo)
                dacs = dacs - jnp.where(li == h0, s0, 0.0) - jnp.where(li == h1, s1, 0.0)
                dal0, dal1 = dal0 + _total(s0), dal1 + _total(s1)
                dalast = dalast + jnp.where(li1 == h0, dal0, 0.0) + jnp.where(li1 == h1, dal1, 0.0)
                dx = dx + dxdt * dtl
                s0, s1 = _half_row_sums(dxdt * x, lo)
                ddt = ddt + jnp.where(li == h0, s0, 0.0) + jnp.where(li == h1, s1, 0.0)
                dxs_buf[:, 128 * p:128 * p + 128] = dx
                dstate[p] = dhin
            dcg = dcg + _dot(dgmat, bg)
            dbg = dbg + _dot(dgmat, cg, "tn")
            dbm_buf[:, 128 * g:128 * g + 128] = dbg
            dcm_buf[:, 128 * g:128 * g + 128] = dcg

        dacs_tot = dacs + dacs_t.T + jnp.where(ri == L - 1, dalast, 0.0)
        dstep = _rev_cumsum_rows(dacs_tot)
        ddt = ddt + dstep * a
        head_lane = li < N_HEADS
        ddt_pre = jnp.where(head_lane, ddt * _sigmoid(dt_pre), 0.0)
        dsmall_ref[...] = ddt_pre
        da = jnp.sum(jnp.where(head_lane, dstep * dt, 0.0), axis=0, keepdims=True)
        gsp = _stack_rows([jnp.sum(ddt_pre, axis=0, keepdims=True), da * a, dskip_g], L)

        def conv_back(dpost, ds, ext, w_ref, carry, out_ref, width):
            dpre = dpost * ds
            dext = jnp.concatenate([dpre, carry[...]], axis=0)
            out_ref[...] = _conv_rows_transposed(dext, w_ref[...], SSD_CONV)[:L].astype(BF16)
            carry[...] = dpre[0:8]
            rows = [jnp.sum(dpre * _shift_down(ext, SSD_CONV - 1 - k)[8:], axis=0, keepdims=True) for k in range(SSD_CONV)]
            rows.append(jnp.sum(dpre, axis=0, keepdims=True))
            return _stack_rows(rows, width)

        gwx = conv_back(dxs_buf[...], xs_ds, xs_ext, wx_ref, carry_x, dxs_ref, 1024)
        gwb = conv_back(dbm_buf[...], b_ds, b_ext, wb_ref, carry_b, db_ref, 256)
        gwc = conv_back(dcm_buf[...], c_ds, c_ext, wc_ref, carry_c, dc_ref, 256)

        @pl.when(start)
        def _():
            gwx_ref[...] = gwx
            gwb_ref[...] = gwb
            gwc_ref[...] = gwc
            gsp_ref[...] = gsp
            gnw_ref[...] = gnw

        @pl.when(step > 0)
        def _():
            gwx_ref[...] += gwx
            gwb_ref[...] += gwb
            gwc_ref[...] += gwc
            gsp_ref[...] += gsp
            gnw_ref[...] += gnw

    def ch(c):
        return nc - 1 - c

    row = pl.BlockSpec((L, 1024), lambda c: (ch(c), 0))
    row256 = pl.BlockSpec((L, 256), lambda c: (ch(c), 0))
    in_specs = _ssd_in_specs(rev_nc=nc) + [row, pl.BlockSpec((1, N_PAIRS, 128, 128), lambda c: (ch(c), 0, 0, 0)), row]
    out_specs = [row, row, row256, row256, pl.BlockSpec((L, 128), lambda c: (ch(c), 0)),
                 pl.BlockSpec((8, 1024), lambda c: (0, 0)), pl.BlockSpec((8, 256), lambda c: (0, 0)),
                 pl.BlockSpec((8, 256), lambda c: (0, 0)), pl.BlockSpec((8, 128), lambda c: (0, 0)),
                 pl.BlockSpec((1, 1024), lambda c: (0, 0))]
    out_shape = [jax.ShapeDtypeStruct((s, 1024), BF16), jax.ShapeDtypeStruct((s, 1024), BF16),
                 jax.ShapeDtypeStruct((s, 256), BF16), jax.ShapeDtypeStruct((s, 256), BF16),
                 jax.ShapeDtypeStruct((s, 128), F32),
                 jax.ShapeDtypeStruct((8, 1024), F32), jax.ShapeDtypeStruct((8, 256), F32),
                 jax.ShapeDtypeStruct((8, 256), F32), jax.ShapeDtypeStruct((8, 128), F32),
                 jax.ShapeDtypeStruct((1, 1024), F32)]
    scratch = [pltpu.VMEM((N_PAIRS, 128, 128), F32), pltpu.VMEM((8, 1024), F32), pltpu.VMEM((8, 256), F32),
               pltpu.VMEM((8, 256), F32), pltpu.VMEM((L, 1024), F32), pltpu.VMEM((L, 256), F32), pltpu.VMEM((L, 256), F32)]
    return pl.pallas_call(
        body, name="ssd_bwd", grid=(nc,), in_specs=in_specs, out_specs=out_specs, out_shape=out_shape,
        scratch_shapes=scratch, compiler_params=_params(("arbitrary",)),
    )(proj, proj, proj, proj, proj, proj, proj, conv_w8, conv_w8, conv_w8, conv_b, conv_b, conv_b, small, smallp, norm_w,
      ypre, states, dy)


FOX_SCALE = HEAD_DIM ** -0.5
FOX_T = 256
Q_COL, K_COL, V_COL = 2, 3, 4


def _split3_dot(v, m):
    hi = v.astype(BF16)
    r1 = v - hi.astype(F32)
    mid = r1.astype(BF16)
    lo = (r1 - mid.astype(F32)).astype(BF16)
    return _dot(hi, m) + _dot(mid, m) + _dot(lo, m)


def _head_rstd(x, sel_ref, selt_ref):
    ms = _split3_dot(x * x, sel_ref[...]) * (1.0 / HEAD_DIM)
    return _split3_dot(lax.rsqrt(ms + NORM_EPS), selt_ref[...])


def fox_tables():
    r = jnp.arange(3 * 128)
    piece, lane = r // 128, r % 128
    head = lane - F_LANE
    is_head = jnp.logical_and(head >= 0, head < N_HEADS)
    col = 128 * (head // 2) + HEAD_DIM * (1 - head % 2) + piece
    cols = jnp.arange(1024)
    place_q = jnp.logical_and(is_head[:, None], cols[None, :] == col[:, None]).astype(BF16)
    place_k = jnp.logical_and(is_head[:, None], cols[None, :] == (col + 3)[:, None]).astype(BF16)
    ones_q = jnp.logical_and(cols % HEAD_DIM >= 3, cols % HEAD_DIM < 6).astype(F32)[None]
    ones_k = (cols % HEAD_DIM < 3).astype(F32)[None]
    h = jnp.arange(128) - F_LANE
    ok = jnp.logical_and(h >= 0, h < N_HEADS)
    same_pair = cols[:, None] // 128 == (h // 2)[None, :]
    fold_even = jnp.logical_and(jnp.logical_and(ok, h % 2 == 0)[None, :], same_pair).astype(BF16)
    fold_odd = jnp.logical_and(jnp.logical_and(ok, h % 2 == 1)[None, :], same_pair).astype(BF16)
    return place_q, place_k, ones_q, ones_k, fold_even, fold_odd


def fox_prep(proj, small, smallp, qw, kw, sel, selt, place_q, place_k, ones_q, ones_k, *, tm=256):
    s = proj.shape[0]

    def body(q_ref, k_ref, v_ref, small_ref, sp_ref, qw_ref, kw_ref, sel_ref, selt_ref, pq_ref, pk_ref, oq_ref, ok_ref,
             qn_ref, kn_ref, aq_ref, ak_ref, vb_ref, knt_ref, akt_ref, carry):
        @pl.when(pl.program_id(0) == 0)
        def _():
            carry[...] = jnp.zeros_like(carry)

        q = q_ref[...]
        qn_ref[...] = (((q * _head_rstd(q, sel_ref, selt_ref)) * qw_ref[...]) * FOX_SCALE).astype(BF16)
        k = k_ref[...]
        kn = ((k * _head_rstd(k, sel_ref, selt_ref)) * kw_ref[...]).astype(BF16)
        kn_ref[...] = kn
        knt_ref[...] = kn.astype(F32).T.astype(BF16)
        vb_ref[...] = v_ref[...].astype(BF16)
        li = _lane_iota((tm, 128))
        f_lane = jnp.logical_and(li >= F_LANE, li < F_LANE + N_HEADS)
        logf = jnp.where(f_lane, -_softplus(-(small_ref[...] + sp_ref[3:4, :])), 0.0)
        cum = _cumsum_rows(logf) + carry[...]
        carry[...] = cum[tm - 1:tm, :]
        hi = cum.astype(BF16)
        r1 = cum - hi.astype(F32)
        mid = r1.astype(BF16)
        lo = (r1 - mid.astype(F32)).astype(BF16)
        pieces = jnp.concatenate([hi, mid, lo], axis=1)
        aq_ref[...] = (_dot(pieces, pq_ref[...]) + oq_ref[...]).astype(BF16)
        ak = ok_ref[...] - _dot(pieces, pk_ref[...])
        ak_ref[...] = ak.astype(BF16)
        akt_ref[...] = ak.T.astype(BF16)

    row = pl.BlockSpec((tm, 1024), lambda i: (i, 0))
    col = pl.BlockSpec((1024, tm), lambda i: (0, i))
    vec = pl.BlockSpec((1, 1024), lambda i: (0, 0))
    table = pl.BlockSpec((384, 1024), lambda i: (0, 0))
    wide = jax.ShapeDtypeStruct((s, 1024), BF16)
    tall = jax.ShapeDtypeStruct((1024, s), BF16)
    return pl.pallas_call(
        body, name="fox_prep", grid=(s // tm,),
        in_specs=[pl.BlockSpec((tm, 1024), lambda i: (i, Q_COL)), pl.BlockSpec((tm, 1024), lambda i: (i, K_COL)),
                  pl.BlockSpec((tm, 1024), lambda i: (i, V_COL)),
                  pl.BlockSpec((tm, 128), lambda i: (i, 0)), pl.BlockSpec((8, 128), lambda i: (0, 0)), vec, vec,
                  pl.BlockSpec((1024, 128), lambda i: (0, 0)), pl.BlockSpec((128, 1024), lambda i: (0, 0)),
                  table, table, vec, vec],
        out_specs=[row, row, row, row, row, col, col],
        out_shape=[wide, wide, wide, wide, wide, tall, tall],
        scratch_shapes=[pltpu.VMEM((1, 128), F32)], compiler_params=_params(("arbitrary",)),
    )(proj, proj, proj, small, smallp, qw, kw, sel, selt, place_q, place_k, ones_q, ones_k)


def fox_fwd(qn, kn, aq, ak, vb):
    s = qn.shape[0]
    t = FOX_T
    nq = s // t

    def body(q_ref, k_ref, aq_ref, ak_ref, v_ref, o_ref, ot_ref, lse_ref):
        p = pl.program_id(0)

        @pl.when(p == 0)
        def _():
            lse_ref[...] = jnp.zeros_like(lse_ref)

        lo = _lane_iota((t, 128)) < HEAD_DIM
        causal = _row_iota((t, t)) >= _lane_iota((t, t))

        def q_loop(qi, _):
            q0 = pl.multiple_of(qi * t, t)
            qv, aqv = q_ref[pl.ds(q0, t), :], aq_ref[pl.ds(q0, t), :]
            qa, qb = jnp.where(lo, qv, aqv), jnp.where(lo, aqv, qv)

            def step(kj, carry, diagonal):
                m0, l0, m1, l1, acc = carry
                k0 = pl.multiple_of(kj * t, t)
                kv, akv, vv = k_ref[pl.ds(k0, t), :], ak_ref[pl.ds(k0, t), :], v_ref[pl.ds(k0, t), :]
                s0 = _dot(qa, jnp.where(lo, kv, akv), "nt")
                s1 = _dot(qb, jnp.where(lo, akv, kv), "nt")
                if diagonal:
                    s0, s1 = jnp.where(causal, s0, NEG_BIG), jnp.where(causal, s1, NEG_BIG)
                n0 = jnp.maximum(m0, jnp.max(s0, axis=1, keepdims=True))
                n1 = jnp.maximum(m1, jnp.max(s1, axis=1, keepdims=True))
                a0, a1 = jnp.exp(m0 - n0), jnp.exp(m1 - n1)
                p0, p1 = jnp.exp(s0 - n0), jnp.exp(s1 - n1)
                l0 = a0 * l0 + jnp.sum(p0, axis=1, keepdims=True)
                l1 = a1 * l1 + jnp.sum(p1, axis=1, keepdims=True)
                acc = jnp.where(lo, a0, a1) * acc + _dot(p0, jnp.where(lo, vv, 0.0)) + _dot(p1, jnp.where(lo, 0.0, vv))
                return n0, l0, n1, l1, acc

            def col(val):
                return jnp.full((t, 1), val, F32)

            init = (col(NEG_BIG), col(0.0), col(NEG_BIG), col(0.0), jnp.zeros((t, 128), F32))
            carry = lax.fori_loop(0, qi, lambda kj, c: step(kj, c, False), init)
            m0, l0, m1, l1, acc = step(qi, carry, True)
            out = acc / jnp.where(lo, l0, l1)
            o_ref[pl.ds(q0, t), :] = out.astype(BF16)
            ot_ref[:, pl.ds(q0, t)] = out.T.astype(BF16)
            lse_rows = jnp.where(lo, m0 + jnp.log(l0), m1 + jnp.log(l1)).T
            ri = _row_iota((N_HEADS, t))
            old = lse_ref[:, pl.ds(q0, t)]
            lse_ref[:, pl.ds(q0, t)] = jnp.where(
                ri == 2 * p, lse_rows[0:1, :], jnp.where(ri == 2 * p + 1, lse_rows[HEAD_DIM:HEAD_DIM + 1, :], old))
            return 0

        lax.fori_loop(0, nq, q_loop, 0)

    pair = pl.BlockSpec((s, 128), lambda p: (0, p))
    return pl.pallas_call(
        body, name="fox_fwd", grid=(N_PAIRS,), in_specs=[pair] * 5,
        out_specs=[pair, pl.BlockSpec((128, s), lambda p: (p, 0)), pl.BlockSpec((N_HEADS, s), lambda p: (0, 0))],
        out_shape=[jax.ShapeDtypeStruct((s, 1024), BF16), jax.ShapeDtypeStruct((1024, s), BF16),
                   jax.ShapeDtypeStruct((N_HEADS, s), F32)],
        compiler_params=_params(("arbitrary",)),
    )(qn, kn, aq, ak, vb)


def fox_bwd(qn, kn, aq, ak, knt, akt, vb, lse, dmixed):
    s = qn.shape[0]
    t = FOX_T
    nq = s // t
    once = pl.Buffered(1)

    def body(q_ref, k_ref, aq_ref, ak_ref, kt_ref, akt_ref, v_ref, lse_ref, do_ref, dq_ref, dk_ref, dv_ref, dc0_ref, dc1_ref,
             p_scr, dp_scr):
        p = pl.program_id(0)
        dk_ref[...] = jnp.zeros_like(dk_ref)
        dv_ref[...] = jnp.zeros_like(dv_ref)
        dc0_ref[...] = jnp.zeros_like(dc0_ref)
        dc1_ref[...] = jnp.zeros_like(dc1_ref)
        lo = _lane_iota((t, 128)) < HEAD_DIM
        lo_rows = _row_iota((128, t)) < HEAD_DIM
        causal_t = _lane_iota((t, t)) >= _row_iota((t, t))

        def q_loop(qi, _):
            q0 = pl.multiple_of(qi * t, t)
            qv, aqv = q_ref[pl.ds(q0, t), :], aq_ref[pl.ds(q0, t), :]
            qa, qb = jnp.where(lo, qv, aqv), jnp.where(lo, aqv, qv)
            do = do_ref[pl.ds(q0, t), :]
            doa, dob = jnp.where(lo, do, 0.0).astype(BF16), jnp.where(lo, 0.0, do).astype(BF16)
            lse_blk = lse_ref[:, pl.ds(q0, t)]
            ri = _row_iota((N_HEADS, t))
            lse0 = jnp.sum(jnp.where(ri == 2 * p, lse_blk, 0.0), axis=0, keepdims=True)
            lse1 = jnp.sum(jnp.where(ri == 2 * p + 1, lse_blk, 0.0), axis=0, keepdims=True)

            def pass1(kj, carry, diagonal):
                d0, d1 = carry
                k0 = pl.multiple_of(kj * t, t)
                kv, akv, vv = k_ref[pl.ds(k0, t), :], ak_ref[pl.ds(k0, t), :], v_ref[pl.ds(k0, t), :]
                s0 = _dot(jnp.where(lo, kv, akv), qa, "nt")
                s1 = _dot(jnp.where(lo, akv, kv), qb, "nt")
                if diagonal:
                    s0, s1 = jnp.where(causal_t, s0, NEG_BIG), jnp.where(causal_t, s1, NEG_BIG)
                p0, p1 = jnp.exp(s0 - lse0), jnp.exp(s1 - lse1)
                dp0, dp1 = _dot(vv, doa, "nt"), _dot(vv, dob, "nt")
                p_scr[0, kj], p_scr[1, kj] = p0, p1
                dp_scr[0, kj], dp_scr[1, kj] = dp0, dp1
                return d0 + jnp.sum(p0 * dp0, axis=0, keepdims=True), d1 + jnp.sum(p1 * dp1, axis=0, keepdims=True)

            zero = jnp.zeros((1, t), F32)
            carry = lax.fori_loop(0, qi, lambda kj, c: pass1(kj, c, False), (zero, zero))
            d0, d1 = pass1(qi, carry, True)

            def pass2(kj, carry):
                dq0, dq1 = carry
                k0 = pl.multiple_of(kj * t, t)
                p0, p1 = p_scr[0, kj], p_scr[1, kj]
                ds0, ds1 = p0 * (dp_scr[0, kj] - d0), p1 * (dp_scr[1, kj] - d1)
                dk_ref[pl.ds(k0, t), :] += jnp.where(lo, _dot(ds0, qa), _dot(ds1, qb))
                dv_ref[pl.ds(k0, t), :] += _dot(p0, doa) + _dot(p1, dob)
                dc0_ref[pl.ds(k0, t), :] += ds0[:, :128] + ds0[:, 128:]
                dc1_ref[pl.ds(k0, t), :] += ds1[:, :128] + ds1[:, 128:]
                ktv, aktv = kt_ref[:, pl.ds(k0, t)], akt_ref[:, pl.ds(k0, t)]
                return dq0 + _dot(jnp.where(lo_rows, ktv, aktv), ds0), dq1 + _dot(jnp.where(lo_rows, aktv, ktv), ds1)

            zq = jnp.zeros((128, t), F32)
            dq0, dq1 = lax.fori_loop(0, qi + 1, pass2, (zq, zq))
            dq_ref[pl.ds(q0, t), :] = jnp.where(lo_rows, dq0, dq1).T
            return 0

        lax.fori_loop(0, nq, q_loop, 0)

    pair = pl.BlockSpec((s, 128), lambda p: (0, p), pipeline_mode=once)
    pair_t = pl.BlockSpec((128, s), lambda p: (p, 0), pipeline_mode=once)
    out = jax.ShapeDtypeStruct((s, 1024), F32)
    return pl.pallas_call(
        body, name="fox_bwd", grid=(N_PAIRS,),
        in_specs=[pair, pair, pair, pair, pair_t, pair_t, pair, pl.BlockSpec((N_HEADS, s), lambda p: (0, 0)),
                  pl.BlockSpec((s, 128), lambda p: (0, 8 + p), pipeline_mode=once)],
        out_specs=[pair] * 5, out_shape=[out] * 5,
        scratch_shapes=[pltpu.VMEM((2, nq, t, t), F32), pltpu.VMEM((2, nq, t, t), F32)],
        compiler_params=_params(("arbitrary",)),
    )(qn, kn, aq, ak, knt, akt, vb, lse, dmixed)


def fox_post(dqn, dkn, dc0, dc1, proj, small, smallp, qw, kw, sel, selt, fold_even, fold_odd, *, tm=256):
    s = proj.shape[0]
    nrow = s // tm

    def body(dqn_ref, dkn_ref, dc0_ref, dc1_ref, q_ref, k_ref, small_ref, sp_ref, qw_ref, kw_ref, sel_ref, selt_ref,
             fe_ref, fo_ref, dq_ref, dk_ref, dsmall_ref, gqw_ref, gkw_ref, gfb_ref, carry):
        step = pl.program_id(0)

        @pl.when(step == 0)
        def _():
            carry[...] = jnp.zeros_like(carry)

        def norm_bwd(x_ref, w_ref, dn, out_ref):
            x = x_ref[...]
            rf = _head_rstd(x, sel_ref, selt_ref)
            xh = x * rf
            g = dn * w_ref[...]
            mean_gx = _split3_dot(_split3_dot(g * xh, sel_ref[...]) * (1.0 / HEAD_DIM), selt_ref[...])
            out_ref[...] = (rf * (g - xh * mean_gx)).astype(BF16)
            return jnp.sum(dn * xh, axis=0, keepdims=True)

        gqw = norm_bwd(q_ref, qw_ref, dqn_ref[...] * FOX_SCALE, dq_ref)
        gkw = norm_bwd(k_ref, kw_ref, dkn_ref[...], dk_ref)
        li = _lane_iota((tm, 128))
        f_lane = jnp.logical_and(li >= F_LANE, li < F_LANE + N_HEADS)
        dcum = -(_split3_dot(dc0_ref[...], fe_ref[...]) + _split3_dot(dc1_ref[...], fo_ref[...]))
        dlogf = _rev_cumsum_rows(dcum) + carry[...]
        carry[...] = dlogf[0:1, :]
        dfr = jnp.where(f_lane, dlogf * _sigmoid(-(small_ref[...] + sp_ref[3:4, :])), 0.0)
        dsmall_ref[...] = dfr
        gfb = jnp.sum(dfr, axis=0, keepdims=True)

        @pl.when(step == 0)
        def _():
            gqw_ref[...] = gqw
            gkw_ref[...] = gkw
            gfb_ref[...] = gfb

        @pl.when(step > 0)
        def _():
            gqw_ref[...] += gqw
            gkw_ref[...] += gkw
            gfb_ref[...] += gfb

    def rb(i):
        return nrow - 1 - i

    row = pl.BlockSpec((tm, 1024), lambda i: (rb(i), 0))
    vec = pl.BlockSpec((1, 1024), lambda i: (0, 0))
    fold = pl.BlockSpec((1024, 128), lambda i: (0, 0))
    return pl.pallas_call(
        body, name="fox_post", grid=(nrow,),
        in_specs=[row, row, row, row, pl.BlockSpec((tm, 1024), lambda i: (rb(i), Q_COL)),
                  pl.BlockSpec((tm, 1024), lambda i: (rb(i), K_COL)),
                  pl.BlockSpec((tm, 128), lambda i: (rb(i), 0)), pl.BlockSpec((8, 128), lambda i: (0, 0)), vec, vec,
                  fold, pl.BlockSpec((128, 1024), lambda i: (0, 0)), fold, fold],
        out_specs=[row, row, pl.BlockSpec((tm, 128), lambda i: (rb(i), 0)), vec, vec, pl.BlockSpec((1, 128), lambda i: (0, 0))],
        out_shape=[jax.ShapeDtypeStruct((s, 1024), BF16), jax.ShapeDtypeStruct((s, 1024), BF16),
                   jax.ShapeDtypeStruct((s, 128), F32), jax.ShapeDtypeStruct((1, 1024), F32),
                   jax.ShapeDtypeStruct((1, 1024), F32), jax.ShapeDtypeStruct((1, 128), F32)],
        scratch_shapes=[pltpu.VMEM((1, 128), F32)], compiler_params=_params(("arbitrary",)),
    )(dqn, dkn, dc0, dc1, proj, proj, small, smallp, qw, kw, sel, selt, fold_even, fold_odd)


def local_step(x, target, wm, ws, w_out, w_up, w_down, ssd_cw8, ssd_cb, smallp, ssd_nw, qw_t, kw_t, sel, selt,
               norm_mix_w, norm_ffn_w, ffn_cw8, ffn_cb):
    h, h_t = rms_fwd(x, norm_mix_w, name="rms_mix_fwd")
    proj = matmul(h, wm, mode="nn", tm=1024, tn=1408, tk=1024, out_dtype=F32, name="mm_in_proj")
    small = matmul(h, ws, mode="nn", tm=1024, tn=128, tk=1024, out_dtype=F32, name="mm_in_proj_small")
    y_ssd, y_ssd_t, ypre, states = ssd_fwd(proj, small, ssd_cw8, ssd_cb, smallp, ssd_nw)
    place_q, place_k, ones_q, ones_k, fold_even, fold_odd = fox_tables()
    qn, kn, aq, ak, vb, knt, akt = fox_prep(proj, small, smallp, qw_t, kw_t, sel, selt, place_q, place_k, ones_q, ones_k)
    y_fox, y_fox_t, lse = fox_fwd(qn, kn, aq, ak, vb)
    x1 = matmul(y_ssd, w_out, mode="nn", tm=1024, tn=1024, tk=1024, out_dtype=F32, name="mm_out_ssd", add=x)
    x1 = matmul(y_fox, w_out, mode="nn", tm=1024, tn=1024, tk=1024, out_dtype=F32, name="mm_out_fox", add=x1, b_koff=1)
    hf, hf_t = rms_fwd(x1, norm_ffn_w, name="rms_ffn_fwd")
    hu = matmul(hf, w_up, mode="nn", tm=1024, tn=1408, tk=1024, out_dtype=F32, name="mm_up")
    act, act_t = ffn_mid_fwd(hu, ffn_cw8, ffn_cb)
    y = matmul(act, w_down, mode="nn", tm=1024, tn=1024, tk=1408, out_dtype=F32, name="mm_down", add=x1)
    dy, sq = loss_head(y, target)

    dact = matmul(dy, w_down, mode="nt", tm=1024, tn=1408, tk=1024, out_dtype=F32, name="mm_dact")
    g_down = matmul(act_t, dy, mode="nn", tm=1408, tn=1024, tk=1024, out_dtype=BF16, name="mm_dw_down")
    dhu_g, dhu_v, gcw_g, gcw_v = ffn_mid_bwd(hu, dact, ffn_cw8, ffn_cb)
    dhf = matmul(dhu_g, w_up, mode="nt", tm=1024, tn=1024, tk=1408, out_dtype=F32, name="mm_dhf_gate")
    dhf = matmul(dhu_v, w_up, mode="nt", tm=1024, tn=1024, tk=1408, out_dtype=F32, name="mm_dhf_val", add=dhf, b_koff=2)
    g_up_g = matmul(hf_t, dhu_g, mode="nn", tm=1024, tn=1408, tk=1024, out_dtype=BF16, name="mm_dw_up_gate")
    g_up_v = matmul(hf_t, dhu_v, mode="nn", tm=1024, tn=1408, tk=1024, out_dtype=BF16, name="mm_dw_up_val")
    dx1, g_norm_ffn = rms_bwd(dhf, x1, norm_ffn_w, dy, name="rms_ffn_bwd")
    dmixed = matmul(dx1, w_out, mode="nt", tm=1024, tn=1024, tk=1024, out_dtype=F32, name="mm_dmixed")
    g_out_a = matmul(y_ssd_t, dx1, mode="nn", tm=1024, tn=1024, tk=1024, out_dtype=BF16, name="mm_dw_out_ssd")
    g_out_b = matmul(y_fox_t, dx1, mode="nn", tm=1024, tn=1024, tk=1024, out_dtype=BF16, name="mm_dw_out_fox")
    dz, dxs, db, dc, dsmall_ssd, gcw_x, gcw_b, gcw_c, g_sp, g_ssd_nw = ssd_bwd(
        proj, small, ssd_cw8, ssd_cb, smallp, ssd_nw, ypre, states, dmixed)
    dqn, dkn, dv, dc0, dc1 = fox_bwd(qn, kn, aq, ak, knt, akt, vb, lse, dmixed)
    dq, dk, dsmall_fox, g_qw, g_kw, g_fb = fox_post(dqn, dkn, dc0, dc1, proj, small, smallp, qw_t, kw_t, sel, selt,
                                                    fold_even, fold_odd)
    dproj = jnp.concatenate([dz, dxs, dq, dk, dv.astype(BF16), db, dc], axis=1)
    dsmall = (dsmall_ssd + dsmall_fox).astype(BF16)
    dh = matmul(dproj, wm, mode="nt", tm=1024, tn=1024, tk=1408, out_dtype=F32, name="mm_dh")
    dh = matmul(dsmall, ws, mode="nt", tm=1024, tn=1024, tk=128, out_dtype=F32, name="mm_dh_small", add=dh)
    g_wm = matmul(h_t, dproj, mode="nn", tm=1024, tn=1408, tk=1024, out_dtype=BF16, name="mm_dw_in")
    g_ws = matmul(h_t, dsmall, mode="nn", tm=1024, tn=128, tk=1024, out_dtype=BF16, name="mm_dw_in_small")
    grad_x, g_norm_mix = rms_bwd(dh, x, norm_mix_w, dx1, name="rms_mix_bwd")
    return dict(
        sq=sq, grad_x=grad_x, g_wm=g_wm, g_ws=g_ws, g_out=jnp.concatenate([g_out_a, g_out_b], axis=0),
        g_up=jnp.concatenate([g_up_g, g_up_v], axis=1), g_down=g_down,
        g_norm_mix=g_norm_mix, g_norm_ffn=g_norm_ffn, g_ssd_nw=g_ssd_nw,
        g_ssd_cw=jnp.concatenate([gcw_x, gcw_b, gcw_c], axis=1), g_sp=g_sp, g_fb=g_fb, g_qw=g_qw, g_kw=g_kw,
        g_ffn_cw=jnp.concatenate([gcw_g, gcw_v], axis=1))


def adamw(w, g, m, v, *, name, tr):
    rows, cols = w.shape

    def body(w_ref, g_ref, m_ref, v_ref, d_ref, mo_ref, vo_ref):
        gv = g_ref[...]
        mn = ADAM_B1 * m_ref[...] + (1.0 - ADAM_B1) * gv
        vn = ADAM_B2 * v_ref[...] + (1.0 - ADAM_B2) * (gv * gv)
        m_hat = mn / (1.0 - ADAM_B1 ** ADAM_STEP)
        v_hat = vn / (1.0 - ADAM_B2 ** ADAM_STEP)
        d_ref[...] = -ADAM_LR * (m_hat / (jnp.sqrt(v_hat) + ADAM_EPS) + ADAM_WD * w_ref[...])
        mo_ref[...] = mn
        vo_ref[...] = vn

    blk = pl.BlockSpec((tr, cols), lambda i: (i, 0))
    shp = jax.ShapeDtypeStruct((rows, cols), F32)
    return pl.pallas_call(
        body, name=name, grid=(rows // tr,), in_specs=[blk] * 4, out_specs=[blk] * 3, out_shape=[shp] * 3,
        compiler_params=_params(("parallel",)),
    )(w, g, m, v)


def add_pair(a, b, *, name, tr):
    _, rows, cols = a.shape

    def body(a_ref, b_ref, o_ref):
        o_ref[...] = (a_ref[...].astype(F32) + b_ref[...].astype(F32)).astype(BF16)

    blk = pl.BlockSpec((1, tr, cols), lambda j, i: (j, i, 0))
    return pl.pallas_call(
        body, name=name, grid=(4, rows // tr), in_specs=[blk, blk], out_specs=blk,
        out_shape=jax.ShapeDtypeStruct(a.shape, BF16), compiler_params=_params(("parallel", "parallel")),
    )(a, b)


def sum_chips(parts, *, name, tr):
    _, rows, cols = parts.shape

    def body(p_ref, o_ref):
        acc = p_ref[0].astype(F32)
        for k in range(1, 4):
            acc = acc + p_ref[k].astype(F32)
        o_ref[...] = acc

    return pl.pallas_call(
        body, name=name, grid=(rows // tr,), in_specs=[pl.BlockSpec((4, tr, cols), lambda i: (0, i, 0))],
        out_specs=pl.BlockSpec((tr, cols), lambda i: (i, 0)), out_shape=jax.ShapeDtypeStruct((rows, cols), F32),
        compiler_params=_params(("parallel",)),
    )(parts)


ANY = pl.BlockSpec(memory_space=pl.ANY)


def _place():
    x, y, c = lax.axis_index("x"), lax.axis_index("y"), lax.axis_index("c")
    chips = [(1 - x, y), (x, 1 - y), (1 - x, 1 - y)]
    return x, y, c, chips


def _chunks(rows):
    size = next((c for c in (128, 176, 64, 32, 16, 8) if rows % c == 0), rows)
    return [(r, size) for r in range(0, rows, size)]


def gather_weights(shards):
    n = len(shards)

    def body(*refs):
        ins, outs = refs[:n], refs[n:2 * n]
        send_sems, recv_sems = refs[2 * n:]
        x, y, c, chips = _place()
        me = 2 * x + y
        sibling = (x, y, 1 - c)

        def half(a, blk, r=0, nr=None):
            rows = ins[a].shape[0] // 2
            return outs[a].at[blk, pl.ds(c * rows + r, rows if nr is None else nr), :]

        def to_chip(a, t, r=0, nr=None):
            rows = ins[a].shape[0] // 2
            return pltpu.make_async_remote_copy(
                src_ref=ins[a].at[pl.ds(c * rows + r, rows if nr is None else nr), :], dst_ref=half(a, me, r, nr),
                send_sem=send_sems.at[a, t], recv_sem=recv_sems.at[a, t],
                device_id=(*chips[t], c), device_id_type=MESH)

        def from_chip(a, t):
            blk = 2 * chips[t][0] + chips[t][1]
            return pltpu.make_async_remote_copy(
                src_ref=half(a, blk), dst_ref=half(a, blk), send_sem=send_sems.at[a, t], recv_sem=recv_sems.at[a, t],
                device_id=(*chips[t], c), device_id_type=MESH)

        def to_sibling(a, t, r=0, nr=None):
            blk = 2 * chips[t][0] + chips[t][1]
            return pltpu.make_async_remote_copy(
                src_ref=half(a, blk, r, nr), dst_ref=half(a, blk, r, nr), send_sem=send_sems.at[a, 3 + t],
                recv_sem=recv_sems.at[a, 3 + t], device_id=sibling, device_id_type=MESH)

        def from_sibling(a, t):
            blk = 2 * chips[t][0] + chips[t][1]
            rows = ins[a].shape[0] // 2
            dst = outs[a].at[blk, pl.ds((1 - c) * rows, rows), :]
            return pltpu.make_async_remote_copy(
                src_ref=dst, dst_ref=dst, send_sem=send_sems.at[a, 3 + t], recv_sem=recv_sems.at[a, 3 + t],
                device_id=sibling, device_id_type=MESH)

        for a in range(n):
            for t in range(3):
                for r, nr in _chunks(ins[a].shape[0] // 2):
                    to_chip(a, t, r, nr).start()
        for a in range(n):
            for t in range(3):
                from_chip(a, t).wait_recv()
                for r, nr in _chunks(ins[a].shape[0] // 2):
                    to_sibling(a, t, r, nr).start()
        for a in range(n):
            for t in range(3):
                from_sibling(a, t).wait_recv()
        for a in range(n):
            for t in range(3):
                to_chip(a, t).wait_send()
                to_sibling(a, t).wait_send()

    chip = 2 * lax.axis_index("x") + lax.axis_index("y")
    gathered = pl.pallas_call(
        body, name="gather_weights", in_specs=[ANY] * n, out_specs=[ANY] * n,
        out_shape=[jax.ShapeDtypeStruct((4,) + s.shape, s.dtype) for s in shards],
        scratch_shapes=[pltpu.SemaphoreType.DMA((n, 6)), pltpu.SemaphoreType.DMA((n, 6))],
    )(*shards)
    return [lax.dynamic_update_slice(g, s[None], (chip, 0, 0)) for g, s in zip(gathered, shards)]


def pair_swap_halves(grads):
    n = len(grads)

    def body(*refs):
        ins, theirs = refs[:n], refs[n:2 * n]
        send_sems, recv_sems = refs[2 * n:]
        x, y, c, _ = _place()
        sibling = (x, y, 1 - c)
        for a in range(n):
            rows = ins[a].shape[1] // 2
            for j in range(4):
                for r, nr in _chunks(rows):
                    pltpu.make_async_remote_copy(
                        src_ref=ins[a].at[j, pl.ds((1 - c) * rows + r, nr), :], dst_ref=theirs[a].at[j, pl.ds(r, nr), :],
                        send_sem=send_sems.at[a], recv_sem=recv_sems.at[a], device_id=sibling, device_id_type=MESH).start()
        for a in range(n):
            pltpu.make_async_remote_copy(src_ref=theirs[a], dst_ref=theirs[a], send_sem=send_sems.at[a],
                                         recv_sem=recv_sems.at[a], device_id=sibling, device_id_type=MESH).wait()

    halves = [jax.ShapeDtypeStruct((4, g.shape[1] // 2, g.shape[2]), g.dtype) for g in grads]
    theirs = pl.pallas_call(
        body, name="pair_swap_halves", in_specs=[ANY] * n, out_specs=[ANY] * n, out_shape=halves,
        scratch_shapes=[pltpu.SemaphoreType.DMA((n,)), pltpu.SemaphoreType.DMA((n,))],
    )(*grads)
    c = lax.axis_index("c")
    mine = [lax.dynamic_slice_in_dim(g, c * (g.shape[1] // 2), g.shape[1] // 2, axis=1) for g in grads]
    return mine, theirs


def scatter_to_chips(parts):
    n = len(parts)

    def body(*refs):
        ins, outs = refs[:n], refs[n:2 * n]
        send_sems, recv_sems = refs[2 * n:]
        x, y, c, chips = _place()
        me = 2 * x + y
        blks = [2 * cx + cy for cx, cy in chips]
        for a in range(n):
            for r, nr in _chunks(ins[a].shape[1]):
                for t in range(3):
                    pltpu.make_async_remote_copy(
                        src_ref=ins[a].at[blks[t], pl.ds(r, nr), :], dst_ref=outs[a].at[me, pl.ds(r, nr), :],
                        send_sem=send_sems.at[a, t], recv_sem=recv_sems.at[a, t],
                        device_id=(*chips[t], c), device_id_type=MESH).start()
        for a in range(n):
            for t in range(3):
                pltpu.make_async_remote_copy(
                    src_ref=outs[a].at[blks[t]], dst_ref=outs[a].at[blks[t]], send_sem=send_sems.at[a, t],
                    recv_sem=recv_sems.at[a, t], device_id=(*chips[t], c), device_id_type=MESH).wait()

    landed = pl.pallas_call(
        body, name="scatter_to_chips", in_specs=[ANY] * n, out_specs=[ANY] * n,
        out_shape=[jax.ShapeDtypeStruct(p.shape, p.dtype) for p in parts],
        scratch_shapes=[pltpu.SemaphoreType.DMA((n, 3)), pltpu.SemaphoreType.DMA((n, 3))],
    )(*parts)
    chip = 2 * lax.axis_index("x") + lax.axis_index("y")
    return [lax.dynamic_update_slice(l, lax.dynamic_slice_in_dim(p, chip, 1, axis=0), (chip, 0, 0))
            for l, p in zip(landed, parts)]


def pair_join_halves(halves):
    n = len(halves)

    def body(*refs):
        ins, outs = refs[:n], refs[n:2 * n]
        send_sems, recv_sems = refs[2 * n:]
        x, y, c, _ = _place()
        sibling = (x, y, 1 - c)
        for a in range(n):
            rows = ins[a].shape[0]
            for r, nr in _chunks(rows):
                pltpu.make_async_remote_copy(
                    src_ref=ins[a].at[pl.ds(r, nr), :], dst_ref=outs[a].at[pl.ds(c * rows + r, nr), :],
                    send_sem=send_sems.at[a], recv_sem=recv_sems.at[a], device_id=sibling, device_id_type=MESH).start()
        for a in range(n):
            rows = ins[a].shape[0]
            got = outs[a].at[pl.ds((1 - c) * rows, rows), :]
            pltpu.make_async_remote_copy(src_ref=ins[a], dst_ref=got, send_sem=send_sems.at[a], recv_sem=recv_sems.at[a],
                                         device_id=sibling, device_id_type=MESH).wait()

    joined = pl.pallas_call(
        body, name="pair_join_halves", in_specs=[ANY] * n, out_specs=[ANY] * n,
        out_shape=[jax.ShapeDtypeStruct((2 * h.shape[0], h.shape[1]), h.dtype) for h in halves],
        scratch_shapes=[pltpu.SemaphoreType.DMA((n,)), pltpu.SemaphoreType.DMA((n,))],
    )(*halves)
    c = lax.axis_index("c")
    return [lax.dynamic_update_slice(j, h, (c * h.shape[0], 0)) for j, h in zip(joined, halves)]


def allreduce_small(packed):
    rows = packed.shape[0]

    def body(in_ref, out_ref, gathered, send_sems, recv_sems):
        x, y, c, _ = _place()
        me = 4 * x + 2 * y + c
        gathered[me] = in_ref[...]
        flips = [(fx, fy, fc) for fx in (0, 1) for fy in (0, 1) for fc in (0, 1)][1:]
        peers = [((1 - x) if fx else x, (1 - y) if fy else y, (1 - c) if fc else c) for fx, fy, fc in flips]
        copies = []
        for t, peer in enumerate(peers):
            cp = pltpu.make_async_remote_copy(
                src_ref=in_ref, dst_ref=gathered.at[me], send_sem=send_sems.at[t], recv_sem=recv_sems.at[t],
                device_id=peer, device_id_type=MESH)
            cp.start()
            copies.append(cp)
        for t, (px, py, pc) in enumerate(peers):
            slot = gathered.at[4 * px + 2 * py + pc]
            pltpu.make_async_remote_copy(
                src_ref=slot, dst_ref=slot, send_sem=send_sems.at[t], recv_sem=recv_sems.at[t],
                device_id=(px, py, pc), device_id_type=MESH).wait_recv()
        for cp in copies:
            cp.wait_send()
        acc = gathered[0]
        for k in range(1, 8):
            acc = acc + gathered[k]
        out_ref[...] = acc

    vm = pl.BlockSpec(memory_space=pltpu.VMEM)
    return pl.pallas_call(
        body, name="allreduce_small", in_specs=[vm], out_specs=vm, out_shape=jax.ShapeDtypeStruct(packed.shape, F32),
        scratch_shapes=[pltpu.VMEM((8, rows, 128), F32), pltpu.SemaphoreType.DMA((7,)), pltpu.SemaphoreType.DMA((7,))],
    )(packed)


SMALL_NAMES = ("norm_mix_w", "ssd_conv_w", "ssd_conv_b", "ssd_dt_bias", "ssd_a_log", "ssd_d", "ssd_norm_w", "fox_f_bias",
               "fox_q_norm_w", "fox_k_norm_w", "norm_ffn_w", "ffn_conv_w", "ffn_conv_b")
BIG_NAMES = ("w_in", "w_out", "w_up", "w_down")
WEIGHT_ORDER = ("norm_mix_w", "w_in", "ssd_conv_w", "ssd_conv_b", "ssd_dt_bias", "ssd_a_log", "ssd_d", "ssd_norm_w",
                "fox_f_bias", "fox_q_norm_w", "fox_k_norm_w", "w_out", "norm_ffn_w", "w_up", "ffn_conv_w", "ffn_conv_b", "w_down")
ADAM_ROWS = {"w_in": 256, "w_out": 256, "w_up": 256, "w_down": 176}


def _pack(arrays):
    rows = []
    for a in arrays:
        flat = a.reshape(-1).astype(F32)
        rows.append(jnp.pad(flat, (0, (-flat.shape[0]) % 1024)).reshape(-1, 128))
    return jnp.concatenate(rows, axis=0)


def _unpack(packed, shapes):
    out, r = [], 0
    for shp in shapes:
        size = 1
        for d in shp:
            size *= d
        nrow = 8 * (-(-size // 1024))
        out.append(packed[r:r + nrow].reshape(-1)[:size].reshape(shp))
        r += nrow
    return out


def _pad_rows(a, rows):
    return jnp.pad(a, ((0, rows - a.shape[0]), (0, 0)))


def kernel(x, norm_mix_w, w_in, ssd_conv_w, ssd_conv_b, ssd_dt_bias, ssd_a_log, ssd_d, ssd_norm_w, fox_f_bias, fox_q_norm_w, fox_k_norm_w, w_out, norm_ffn_w, w_up, ffn_conv_w, ffn_conv_b, w_down, loss_target, m_norm_mix_w, m_w_in, m_ssd_conv_w, m_ssd_conv_b, m_ssd_dt_bias, m_ssd_a_log, m_ssd_d, m_ssd_norm_w, m_fox_f_bias, m_fox_q_norm_w, m_fox_k_norm_w, m_w_out, m_norm_ffn_w, m_w_up, m_ffn_conv_w, m_ffn_conv_b, m_w_down, v_norm_mix_w, v_w_in, v_ssd_conv_w, v_ssd_conv_b, v_ssd_dt_bias, v_ssd_a_log, v_ssd_d, v_ssd_norm_w, v_fox_f_bias, v_fox_q_norm_w, v_fox_k_norm_w, v_w_out, v_norm_ffn_w, v_w_up, v_ffn_conv_w, v_ffn_conv_b, v_w_down):
    w = dict(norm_mix_w=norm_mix_w, w_in=w_in, ssd_conv_w=ssd_conv_w, ssd_conv_b=ssd_conv_b, ssd_dt_bias=ssd_dt_bias,
             ssd_a_log=ssd_a_log, ssd_d=ssd_d, ssd_norm_w=ssd_norm_w, fox_f_bias=fox_f_bias, fox_q_norm_w=fox_q_norm_w,
             fox_k_norm_w=fox_k_norm_w, w_out=w_out, norm_ffn_w=norm_ffn_w, w_up=w_up, ffn_conv_w=ffn_conv_w,
             ffn_conv_b=ffn_conv_b, w_down=w_down)
    m = dict(norm_mix_w=m_norm_mix_w, w_in=m_w_in, ssd_conv_w=m_ssd_conv_w, ssd_conv_b=m_ssd_conv_b, ssd_dt_bias=m_ssd_dt_bias,
             ssd_a_log=m_ssd_a_log, ssd_d=m_ssd_d, ssd_norm_w=m_ssd_norm_w, fox_f_bias=m_fox_f_bias, fox_q_norm_w=m_fox_q_norm_w,
             fox_k_norm_w=m_fox_k_norm_w, w_out=m_w_out, norm_ffn_w=m_norm_ffn_w, w_up=m_w_up, ffn_conv_w=m_ffn_conv_w,
             ffn_conv_b=m_ffn_conv_b, w_down=m_w_down)
    v = dict(norm_mix_w=v_norm_mix_w, w_in=v_w_in, ssd_conv_w=v_ssd_conv_w, ssd_conv_b=v_ssd_conv_b, ssd_dt_bias=v_ssd_dt_bias,
             ssd_a_log=v_ssd_a_log, ssd_d=v_ssd_d, ssd_norm_w=v_ssd_norm_w, fox_f_bias=v_fox_f_bias, fox_q_norm_w=v_fox_q_norm_w,
             fox_k_norm_w=v_fox_k_norm_w, w_out=v_w_out, norm_ffn_w=v_norm_ffn_w, w_up=v_w_up, ffn_conv_w=v_ffn_conv_w,
             ffn_conv_b=v_ffn_conv_b, w_down=v_w_down)
    chip = 2 * lax.axis_index("x") + lax.axis_index("y")

    shards = [w_in[0].astype(BF16), w_out[0].astype(BF16), w_up[0].astype(BF16), w_down[0].astype(BF16),
              _pad_rows(ssd_conv_w[0], 16), _pad_rows(ffn_conv_w[0], 16)]
    a_in, a_out, a_up, a_down, a_scw, a_fcw = gather_weights(shards)
    w_full = a_in.transpose(1, 0, 2).reshape(D_MODEL, IN_COLS)
    wm = jnp.concatenate([w_full[:, :2048], w_full[:, 2576:5648], w_full[:, 2048:2560]], axis=1)
    ws = jnp.concatenate([w_full[:, 2560:2576], w_full[:, 5648:5664], jnp.zeros((D_MODEL, SMALL_COLS - 32), BF16)], axis=1)
    wo = a_out.reshape(2048, D_MODEL)
    wu = a_up.transpose(1, 0, 2).reshape(D_MODEL, 2 * D_FF)
    wd = a_down.reshape(D_FF, D_MODEL)
    ssd_cw8 = a_scw.transpose(1, 0, 2).reshape(16, 1536)[:8]
    ffn_cw8 = a_fcw.transpose(1, 0, 2).reshape(16, 2 * D_FF)[:8]
    smallp = jnp.zeros((8, 128), F32)
    smallp = smallp.at[0, :16].set(ssd_dt_bias[0]).at[1, :16].set(ssd_a_log[0]).at[2, :16].set(ssd_d[0])
    smallp = smallp.at[3, F_LANE:F_LANE + 16].set(fox_f_bias[0])
    qw_t = jnp.tile(fox_q_norm_w[0], N_HEADS)[None]
    kw_t = jnp.tile(fox_k_norm_w[0], N_HEADS)[None]
    sel = (jnp.arange(1024)[:, None] // HEAD_DIM == jnp.arange(128)[None, :]).astype(BF16)

    res = local_step(x[0], loss_target[0], wm, ws, wo, wu, wd, ssd_cw8, ssd_conv_b, smallp, ssd_norm_w, qw_t, kw_t,
                     sel, sel.T, norm_mix_w, norm_ffn_w, ffn_cw8, ffn_conv_b)

    full_shapes = [(1, 1024), (1, 4, 1536), (1, 1536), (1, 16), (1, 16), (1, 16), (1, 1024), (1, 16), (1, 64), (1, 64),
                   (1, 1024), (1, 3, 2 * D_FF), (1, 2 * D_FF), (1,)]
    local_small = [res["g_norm_mix"], res["g_ssd_cw"][:4], res["g_ssd_cw"][4], res["g_sp"][0, :16], res["g_sp"][1, :16],
                   res["g_sp"][2, :16], res["g_ssd_nw"], res["g_fb"][0, F_LANE:F_LANE + 16],
                   res["g_qw"].reshape(N_HEADS, HEAD_DIM).sum(0), res["g_kw"].reshape(N_HEADS, HEAD_DIM).sum(0),
                   res["g_norm_ffn"], res["g_ffn_cw"][:3], res["g_ffn_cw"][3], jnp.sum(res["sq"])]
    summed = _unpack(allreduce_small(_pack(local_small)), full_shapes)
    loss = (0.5 / D_MODEL) * summed[-1][0]
    g_small = dict(zip(SMALL_NAMES, summed[:-1]))
    g_small["ssd_conv_w"] = lax.dynamic_slice(g_small["ssd_conv_w"], (0, 0, 384 * chip), (1, 4, 384))
    g_small["ffn_conv_w"] = lax.dynamic_slice(g_small["ffn_conv_w"], (0, 0, 1408 * chip), (1, 3, 1408))

    g_wm, g_ws = res["g_wm"], res["g_ws"]
    g_in_full = jnp.concatenate([g_wm[:, :2048], g_wm[:, 5120:5632], g_ws[:, :16], g_wm[:, 2048:5120], g_ws[:, 16:32]], axis=1)
    big = [g_in_full.reshape(D_MODEL, 4, 1416).transpose(1, 0, 2), res["g_out"].reshape(4, 512, D_MODEL),
           res["g_up"].reshape(D_MODEL, 4, 1408).transpose(1, 0, 2), res["g_down"].reshape(4, 704, D_MODEL)]
    mine, theirs = pair_swap_halves(big)
    parts = [add_pair(a, b, name="add_pair_" + n, tr=ADAM_ROWS[n]) for a, b, n in zip(mine, theirs, BIG_NAMES)]
    landed = scatter_to_chips(parts)
    halves = [sum_chips(p, name="sum_chips_" + n, tr=ADAM_ROWS[n]) for p, n in zip(landed, BIG_NAMES)]
    g_big = dict(zip(BIG_NAMES, pair_join_halves(halves)))

    grads, deltas, new_m, new_v = {}, {}, {}, {}
    for n in BIG_NAMES:
        d, mn, vn = adamw(w[n][0], g_big[n], m[n][0], v[n][0], name="adamw_" + n, tr=ADAM_ROWS[n])
        grads[n], deltas[n], new_m[n], new_v[n] = g_big[n][None], d[None], mn[None], vn[None]
    shapes = [w[n].shape for n in SMALL_NAMES]
    d, mn, vn = adamw(_pack([w[n] for n in SMALL_NAMES]), _pack([g_small[n] for n in SMALL_NAMES]),
                      _pack([m[n] for n in SMALL_NAMES]), _pack([v[n] for n in SMALL_NAMES]), name="adamw_small", tr=8)
    for n, dd, mm, vv in zip(SMALL_NAMES, _unpack(d, shapes), _unpack(mn, shapes), _unpack(vn, shapes)):
        grads[n], deltas[n], new_m[n], new_v[n] = g_small[n].reshape(w[n].shape), dd, mm, vv
    return (loss, res["grad_x"][None], *[grads[n] for n in WEIGHT_ORDER], *[deltas[n] for n in WEIGHT_ORDER],
            *[new_m[n] for n in WEIGHT_ORDER], *[new_v[n] for n in WEIGHT_ORDER])
```

```python
import functools

import jax
import jax.numpy as jnp
from jax import lax
from jax.experimental import pallas as pl
from jax.experimental.pallas import tpu as pltpu

F32 = jnp.float32
BF16 = jnp.bfloat16
MESH = pl.DeviceIdType.MESH

D_MODEL = 1024
HEAD_DIM = 64
N_HEADS = 16
N_PAIRS = N_HEADS // 2
SSD_CHUNK = 128
SSD_STATE = 128
SSD_CONV = 4
D_FF = 2816
FFN_CONV = 3
NORM_EPS = 1e-6
MAIN_COLS = 5632
SMALL_COLS = 128
F_LANE = 16
IN_COLS = 5664

ADAM_LR = 0.001
ADAM_B1 = 0.9
ADAM_B2 = 0.999
ADAM_EPS = 1e-08
ADAM_WD = 0.01
ADAM_STEP = 10

VMEM_LIMIT_V7X = 56 * 1024 * 1024
NEG_BIG = -1e30


def _params(sem=None):
    return pltpu.CompilerParams(dimension_semantics=sem, vmem_limit_bytes=VMEM_LIMIT_V7X)


def _sigmoid(x):
    return 1.0 / (1.0 + jnp.exp(-x))


def _silu_and_grad(x):
    s = _sigmoid(x)
    return x * s, s * (1.0 + x * (1.0 - s))


def _shift_down(v, j):
    return v if j == 0 else pltpu.roll(v, j, 0)


def _shift_up(v, j):
    return v if j == 0 else pltpu.roll(v, v.shape[0] - j, 0)


def _row_iota(shape):
    return lax.broadcasted_iota(jnp.int32, shape, 0)


def _lane_iota(shape):
    return lax.broadcasted_iota(jnp.int32, shape, 1)


def _dot(a, b, mode="nn"):
    dims = {"nn": (((1,), (0,)), ((), ())), "nt": (((1,), (1,)), ((), ())), "tn": (((0,), (0,)), ((), ()))}[mode]
    return lax.dot_general(a.astype(BF16), b.astype(BF16), dims, preferred_element_type=F32)


def _dot_f32(a, b):
    return jnp.dot(a, b, precision=lax.Precision.HIGHEST, preferred_element_type=F32)


def matmul(a, b, *, mode, tm, tn, tk, out_dtype, name, add=None, b_koff=0):
    (m, k), n = a.shape, (b.shape[1] if mode == "nn" else b.shape[0])
    assert m % tm == 0 and n % tn == 0 and k % tk == 0, (name, m, n, k, tm, tn, tk)
    nk = k // tk
    a_spec = pl.BlockSpec((tm, tk), lambda i, j, kk: (i, kk))
    b_spec = (pl.BlockSpec((tn, tk), lambda i, j, kk: (j, kk + b_koff)) if mode == "nt"
              else pl.BlockSpec((tk, tn), lambda i, j, kk: (kk + b_koff, j)))
    o_spec = pl.BlockSpec((tm, tn), lambda i, j, kk: (i, j))
    has_add = add is not None

    def body(*refs):
        if has_add:
            a_ref, b_ref, add_ref, o_ref, acc_ref = refs
        else:
            a_ref, b_ref, o_ref, acc_ref = refs
        kk = pl.program_id(2)
        part = _dot(a_ref[...], b_ref[...], mode)

        def finish(total):
            if has_add:
                total = total + add_ref[...]
            o_ref[...] = total.astype(out_dtype)

        if nk == 1:
            finish(part)
        else:
            @pl.when(kk == 0)
            def _():
                acc_ref[...] = part

            @pl.when(jnp.logical_and(kk > 0, kk < nk - 1))
            def _():
                acc_ref[...] += part

            @pl.when(kk == nk - 1)
            def _():
                finish(acc_ref[...] + part)

    in_specs = [a_spec, b_spec] + ([o_spec] if has_add else [])
    args = (a, b) + ((add,) if has_add else ())
    return pl.pallas_call(
        body, name=name, grid=(m // tm, n // tn, nk), in_specs=in_specs, out_specs=o_spec,
        out_shape=jax.ShapeDtypeStruct((m, n), out_dtype),
        scratch_shapes=[pltpu.VMEM((tm, tn) if nk > 1 else (8, 128), F32)],
        compiler_params=_params(("parallel", "parallel", "arbitrary")),
    )(*args)


def rms_fwd(x, w, *, name, tm=512):
    s, d = x.shape

    def body(x_ref, w_ref, h_ref, ht_ref):
        xv = x_ref[...]
        r = lax.rsqrt(jnp.mean(xv * xv, axis=-1, keepdims=True) + NORM_EPS)
        h = (xv * r) * w_ref[...]
        h_ref[...] = h.astype(BF16)
        ht_ref[...] = h.T.astype(BF16)

    return pl.pallas_call(
        body, name=name, grid=(s // tm,),
        in_specs=[pl.BlockSpec((tm, d), lambda i: (i, 0)), pl.BlockSpec((1, d), lambda i: (0, 0))],
        out_specs=[pl.BlockSpec((tm, d), lambda i: (i, 0)), pl.BlockSpec((d, tm), lambda i: (0, i))],
        out_shape=[jax.ShapeDtypeStruct((s, d), BF16), jax.ShapeDtypeStruct((d, s), BF16)],
        compiler_params=_params(("parallel",)),
    )(x, w)


def rms_bwd(dh, x, w, resid, *, name, tm=512):
    s, d = x.shape

    def body(dh_ref, x_ref, w_ref, res_ref, dx_ref, dw_ref):
        xv = x_ref[...]
        dhv = dh_ref[...]
        r = lax.rsqrt(jnp.mean(xv * xv, axis=-1, keepdims=True) + NORM_EPS)
        xh = xv * r
        g = dhv * w_ref[...]
        dx_ref[...] = res_ref[...] + r * (g - xh * jnp.mean(g * xh, axis=-1, keepdims=True))
        part = jnp.sum(dhv * xh, axis=0, keepdims=True)

        @pl.when(pl.program_id(0) == 0)
        def _():
            dw_ref[...] = part

        @pl.when(pl.program_id(0) > 0)
        def _():
            dw_ref[...] += part

    row = pl.BlockSpec((tm, d), lambda i: (i, 0))
    vec = pl.BlockSpec((1, d), lambda i: (0, 0))
    return pl.pallas_call(
        body, name=name, grid=(s // tm,), in_specs=[row, row, vec, row], out_specs=[row, vec],
        out_shape=[jax.ShapeDtypeStruct((s, d), F32), jax.ShapeDtypeStruct((1, d), F32)],
        compiler_params=_params(("arbitrary",)),
    )(dh, x, w, resid)


def loss_head(y, target, *, tm=512):
    s, d = y.shape

    def body(y_ref, t_ref, dy_ref, sq_ref):
        e = y_ref[...] - t_ref[...]
        dy_ref[...] = e / float(d)
        part = jnp.sum(e * e, axis=0, keepdims=True)

        @pl.when(pl.program_id(0) == 0)
        def _():
            sq_ref[...] = part

        @pl.when(pl.program_id(0) > 0)
        def _():
            sq_ref[...] += part

    row = pl.BlockSpec((tm, d), lambda i: (i, 0))
    vec = pl.BlockSpec((1, d), lambda i: (0, 0))
    return pl.pallas_call(
        body, name="loss_head", grid=(s // tm,), in_specs=[row, row], out_specs=[row, vec],
        out_shape=[jax.ShapeDtypeStruct((s, d), F32), jax.ShapeDtypeStruct((1, d), F32)],
        compiler_params=_params(("arbitrary",)),
    )(y, target)


def _conv_rows(ext, w, k_taps):
    acc = w[k_taps - 1:k_taps, :] * ext
    for k in range(k_taps - 1):
        acc = acc + w[k:k + 1, :] * _shift_down(ext, k_taps - 1 - k)
    return acc


def _conv_rows_transposed(dext, w, k_taps):
    acc = w[k_taps - 1:k_taps, :] * dext
    for k in range(k_taps - 1):
        acc = acc + w[k:k + 1, :] * _shift_up(dext, k_taps - 1 - k)
    return acc


def _stack_rows(rows, width):
    ri = _row_iota((8, width))
    out = jnp.zeros((8, width), F32)
    for k, r in enumerate(rows):
        out = out + jnp.where(ri == k, r, 0.0)
    return out


def ffn_mid_fwd(hu, conv_w8, conv_b, *, tm=512, tc=256):
    s = hu.shape[0]
    ncol = D_FF // tc
    r8 = tm // 8

    def body(g_ref, v_ref, gp_ref, vp_ref, wg_ref, wv_ref, bg_ref, bv_ref, o_ref, ot_ref):
        first = pl.program_id(1) == 0

        def conv(cur_ref, prev_ref, w_ref, b_ref):
            prev = jnp.where(first, 0.0, prev_ref[...])
            ext = jnp.concatenate([prev, cur_ref[...]], axis=0)
            return _conv_rows(ext, w_ref[...], FFN_CONV)[8:] + b_ref[...]

        gc = conv(g_ref, gp_ref, wg_ref, bg_ref)
        vc = conv(v_ref, vp_ref, wv_ref, bv_ref)
        act = gc * _sigmoid(gc) * vc
        o_ref[...] = act.astype(BF16)
        ot_ref[...] = act.T.astype(BF16)

    def prev_idx(i):
        return jnp.maximum(i * r8 - 1, 0)

    in_specs = [
        pl.BlockSpec((tm, tc), lambda j, i: (i, j)),
        pl.BlockSpec((tm, tc), lambda j, i: (i, j + ncol)),
        pl.BlockSpec((8, tc), lambda j, i: (prev_idx(i), j)),
        pl.BlockSpec((8, tc), lambda j, i: (prev_idx(i), j + ncol)),
        pl.BlockSpec((8, tc), lambda j, i: (0, j)),
        pl.BlockSpec((8, tc), lambda j, i: (0, j + ncol)),
        pl.BlockSpec((1, tc), lambda j, i: (0, j)),
        pl.BlockSpec((1, tc), lambda j, i: (0, j + ncol)),
    ]
    return pl.pallas_call(
        body, name="ffn_mid_fwd", grid=(ncol, s // tm), in_specs=in_specs,
        out_specs=[pl.BlockSpec((tm, tc), lambda j, i: (i, j)), pl.BlockSpec((tc, tm), lambda j, i: (j, i))],
        out_shape=[jax.ShapeDtypeStruct((s, D_FF), BF16), jax.ShapeDtypeStruct((D_FF, s), BF16)],
        compiler_params=_params(("parallel", "parallel")),
    )(hu, hu, hu, hu, conv_w8, conv_w8, conv_b, conv_b)


def ffn_mid_bwd(hu, dact, conv_w8, conv_b, *, tm=512, tc=256):
    s = hu.shape[0]
    ncol = D_FF // tc
    nrow = s // tm
    r8 = tm // 8

    def body(g_ref, v_ref, gp_ref, vp_ref, gn_ref, vn_ref, da_ref, dan_ref, wg_ref, wv_ref, bg_ref, bv_ref,
             dg_ref, dv_ref, wgo_ref, wvo_ref):
        i = pl.program_id(1)
        first = i == 0
        last = i == nrow - 1

        def ext_of(cur_ref, prev_ref, next_ref):
            prev = jnp.where(first, 0.0, prev_ref[...])
            return jnp.concatenate([prev, cur_ref[...], next_ref[...]], axis=0)

        g_ext = ext_of(g_ref, gp_ref, gn_ref)
        v_ext = ext_of(v_ref, vp_ref, vn_ref)
        gc = _conv_rows(g_ext, wg_ref[...], FFN_CONV) + bg_ref[...]
        vc = _conv_rows(v_ext, wv_ref[...], FFN_CONV) + bv_ref[...]
        da_ext = jnp.concatenate([jnp.zeros((8, tc), F32), da_ref[...], jnp.where(last, 0.0, dan_ref[...])], axis=0)
        silu, dsilu = _silu_and_grad(gc)
        dgc = da_ext * vc * dsilu
        dvc = da_ext * silu
        dg_ref[...] = _conv_rows_transposed(dgc, wg_ref[...], FFN_CONV)[8:8 + tm].astype(BF16)
        dv_ref[...] = _conv_rows_transposed(dvc, wv_ref[...], FFN_CONV)[8:8 + tm].astype(BF16)

        def wgrad(dcur, x_ext):
            rows = [jnp.sum(dcur * _shift_down(x_ext, FFN_CONV - 1 - k)[8:8 + tm], axis=0, keepdims=True)
                    for k in range(FFN_CONV)]
            rows.append(jnp.sum(dcur, axis=0, keepdims=True))
            return _stack_rows(rows, tc)

        pg = wgrad(dgc[8:8 + tm], g_ext)
        pv = wgrad(dvc[8:8 + tm], v_ext)

        @pl.when(first)
        def _():
            wgo_ref[...] = pg
            wvo_ref[...] = pv

        @pl.when(i > 0)
        def _():
            wgo_ref[...] += pg
            wvo_ref[...] += pv

    def prev_idx(i):
        return jnp.maximum(i * r8 - 1, 0)

    def next_idx(i):
        return jnp.minimum((i + 1) * r8, s // 8 - 1)

    cur_g = pl.BlockSpec((tm, tc), lambda j, i: (i, j))
    cur_v = pl.BlockSpec((tm, tc), lambda j, i: (i, j + ncol))
    in_specs = [
        cur_g, cur_v,
        pl.BlockSpec((8, tc), lambda j, i: (prev_idx(i), j)),
        pl.BlockSpec((8, tc), lambda j, i: (prev_idx(i), j + ncol)),
        pl.BlockSpec((8, tc), lambda j, i: (next_idx(i), j)),
        pl.BlockSpec((8, tc), lambda j, i: (next_idx(i), j + ncol)),
        cur_g,
        pl.BlockSpec((8, tc), lambda j, i: (next_idx(i), j)),
        pl.BlockSpec((8, tc), lambda j, i: (0, j)),
        pl.BlockSpec((8, tc), lambda j, i: (0, j + ncol)),
        pl.BlockSpec((1, tc), lambda j, i: (0, j)),
        pl.BlockSpec((1, tc), lambda j, i: (0, j + ncol)),
    ]
    out_specs = [cur_g, cur_g, pl.BlockSpec((8, tc), lambda j, i: (0, j)), pl.BlockSpec((8, tc), lambda j, i: (0, j))]
    out_shape = [jax.ShapeDtypeStruct((s, D_FF), BF16), jax.ShapeDtypeStruct((s, D_FF), BF16),
                 jax.ShapeDtypeStruct((8, D_FF), F32), jax.ShapeDtypeStruct((8, D_FF), F32)]
    return pl.pallas_call(
        body, name="ffn_mid_bwd", grid=(ncol, nrow), in_specs=in_specs, out_specs=out_specs, out_shape=out_shape,
        compiler_params=_params(("parallel", "arbitrary")),
    )(hu, hu, hu, hu, hu, hu, dact, dact, conv_w8, conv_w8, conv_b, conv_b)


def _softplus(x):
    return jnp.maximum(x, 0.0) + jnp.log(1.0 + jnp.exp(-jnp.abs(x)))


def _cumsum_rows(v):
    n = v.shape[0]
    ri = _row_iota(v.shape)
    sh = 1
    while sh < n:
        v = v + jnp.where(ri >= sh, _shift_down(v, sh), 0.0)
        sh *= 2
    return v


def _rev_cumsum_rows(v):
    n = v.shape[0]
    ri = _row_iota(v.shape)
    sh = 1
    while sh < n:
        v = v + jnp.where(ri < n - sh, _shift_up(v, sh), 0.0)
        sh *= 2
    return v


def _half_row_sums(v, lo):
    s0 = jnp.sum(jnp.where(lo, v, 0.0), axis=1, keepdims=True)
    return s0, jnp.sum(v, axis=1, keepdims=True) - s0


def _total(v):
    return jnp.sum(jnp.sum(v, axis=1, keepdims=True), axis=0, keepdims=True)


def _ssd_in_specs(rev_nc=None):
    def ch(c):
        return c if rev_nc is None else rev_nc - 1 - c

    def prev(c):
        return jnp.maximum(ch(c) * (SSD_CHUNK // 8) - 1, 0)

    L = SSD_CHUNK
    return [
        pl.BlockSpec((L, 1024), lambda c: (ch(c), 0)),
        pl.BlockSpec((L, 1024), lambda c: (ch(c), 1)),
        pl.BlockSpec((L, 256), lambda c: (ch(c), 20)),
        pl.BlockSpec((L, 256), lambda c: (ch(c), 21)),
        pl.BlockSpec((8, 1024), lambda c: (prev(c), 1)),
        pl.BlockSpec((8, 256), lambda c: (prev(c), 20)),
        pl.BlockSpec((8, 256), lambda c: (prev(c), 21)),
        pl.BlockSpec((8, 1024), lambda c: (0, 0)),
        pl.BlockSpec((8, 256), lambda c: (0, 4)),
        pl.BlockSpec((8, 256), lambda c: (0, 5)),
        pl.BlockSpec((1, 1024), lambda c: (0, 0)),
        pl.BlockSpec((1, 256), lambda c: (0, 4)),
        pl.BlockSpec((1, 256), lambda c: (0, 5)),
        pl.BlockSpec((L, SMALL_COLS), lambda c: (ch(c), 0)),
        pl.BlockSpec((8, 128), lambda c: (0, 0)),
        pl.BlockSpec((1, 1024), lambda c: (0, 0)),
    ]


def _ssd_conv_pre(cur_ref, prev_ref, w_ref, b_ref, first):
    prev = jnp.where(first, 0.0, prev_ref[...])
    ext = jnp.concatenate([prev, cur_ref[...]], axis=0)
    return ext, _conv_rows(ext, w_ref[...], SSD_CONV)[8:] + b_ref[...]


def _ssd_time_consts(small_ref, sp_ref):
    dt_pre = small_ref[...] + sp_ref[0:1, :]
    dt = _softplus(dt_pre)
    a = -jnp.exp(sp_ref[1:2, :])
    acs = _cumsum_rows(dt * a)
    return dt_pre, dt, a, acs


def ssd_fwd(proj, small, conv_w8, conv_b, smallp, norm_w):
    s = proj.shape[0]
    nc = s // SSD_CHUNK
    L = SSD_CHUNK

    def body(z_ref, xs_ref, b_ref, c_ref, xsp_ref, bp_ref, cp_ref, wx_ref, wb_ref, wc_ref, bx_ref, bb_ref, bc_ref,
             small_ref, sp_ref, nw_ref, y_ref, yt_ref, ypre_ref, st_ref, state):
        first = pl.program_id(0) == 0

        @pl.when(first)
        def _():
            state[...] = jnp.zeros_like(state)

        xs = _ssd_conv_pre(xs_ref, xsp_ref, wx_ref, bx_ref, first)[1]
        xs = xs * _sigmoid(xs)
        bm = _ssd_conv_pre(b_ref, bp_ref, wb_ref, bb_ref, first)[1]
        bm = bm * _sigmoid(bm)
        cm = _ssd_conv_pre(c_ref, cp_ref, wc_ref, bc_ref, first)[1]
        cm = cm * _sigmoid(cm)
        _, dt, _, acs = _ssd_time_consts(small_ref, sp_ref)
        acs_t = acs.T
        li = _lane_iota((L, L))
        ri = _row_iota((L, L))
        tri = ri >= li
        lo = li < HEAD_DIM
        st_ref[0] = state[...]
        for g in range(2):
            bg = bm[:, 128 * g:128 * g + 128]
            cg = cm[:, 128 * g:128 * g + 128]
            gmat = _dot(cg, bg, "nt")
            for pp in range(4):
                p = 4 * g + pp
                h0, h1 = 2 * p, 2 * p + 1
                x = xs[:, 128 * p:128 * p + 128]
                a0, a1 = acs[:, h0:h0 + 1], acs[:, h1:h1 + 1]
                xdt = x * jnp.where(lo, dt[:, h0:h0 + 1], dt[:, h1:h1 + 1])
                m0 = gmat * jnp.exp(jnp.where(tri, a0 - acs_t[h0:h0 + 1, :], NEG_BIG))
                m1 = gmat * jnp.exp(jnp.where(tri, a1 - acs_t[h1:h1 + 1, :], NEG_BIG))
                yd = _dot(m0, jnp.where(lo, xdt, 0.0)) + _dot(m1, jnp.where(lo, 0.0, xdt))
                hin = state[p]
                yo = _dot(cg, hin, "nt") * jnp.exp(jnp.where(lo, a0, a1))
                dskip = jnp.where(lo[0:1], sp_ref[2:3, h0:h0 + 1], sp_ref[2:3, h1:h1 + 1])
                ypre_ref[:, 128 * p:128 * p + 128] = yd + yo + dskip * x
                al0, al1 = acs[L - 1:L, h0:h0 + 1], acs[L - 1:L, h1:h1 + 1]
                w = jnp.exp(jnp.where(lo, al0 - a0, al1 - a1))
                dec = jnp.exp(jnp.where(ri < HEAD_DIM, al0, al1))
                state[p] = dec * hin + _dot(xdt * w, bg, "tn")
        z = z_ref[...]
        yg = ypre_ref[...] * (z * _sigmoid(z))
        for g in range(2):
            seg = yg[:, 512 * g:512 * g + 512]
            r = lax.rsqrt(jnp.mean(seg * seg, axis=-1, keepdims=True) + NORM_EPS)
            out = (seg * r) * nw_ref[:, 512 * g:512 * g + 512]
            y_ref[:, 512 * g:512 * g + 512] = out.astype(BF16)
            yt_ref[512 * g:512 * g + 512, :] = out.T.astype(BF16)

    row = pl.BlockSpec((L, 1024), lambda c: (c, 0))
    return pl.pallas_call(
        body, name="ssd_fwd", grid=(nc,), in_specs=_ssd_in_specs(),
        out_specs=[row, pl.BlockSpec((1024, L), lambda c: (0, c)), row,
                   pl.BlockSpec((1, N_PAIRS, 128, 128), lambda c: (c, 0, 0, 0))],
        out_shape=[jax.ShapeDtypeStruct((s, 1024), BF16), jax.ShapeDtypeStruct((1024, s), BF16),
                   jax.ShapeDtypeStruct((s, 1024), F32), jax.ShapeDtypeStruct((nc, N_PAIRS, 128, 128), F32)],
        scratch_shapes=[pltpu.VMEM((N_PAIRS, 128, 128), F32)],
        compiler_params=_params(("arbitrary",)),
    )(proj, proj, proj, proj, proj, proj, proj, conv_w8, conv_w8, conv_w8, conv_b, conv_b, conv_b, small, smallp, norm_w)


def ssd_bwd(proj, small, conv_w8, conv_b, smallp, norm_w, ypre, states, dy):
    s = proj.shape[0]
    nc = s // SSD_CHUNK
    L = SSD_CHUNK

    def body(z_ref, xs_ref, b_ref, c_ref, xsp_ref, bp_ref, cp_ref, wx_ref, wb_ref, wc_ref, bx_ref, bb_ref, bc_ref,
             small_ref, sp_ref, nw_ref, ypre_ref, st_ref, dy_ref,
             dz_ref, dxs_ref, db_ref, dc_ref, dsmall_ref, gwx_ref, gwb_ref, gwc_ref, gsp_ref, gnw_ref,
             dstate, carry_x, carry_b, carry_c, dxs_buf, dbm_buf, dcm_buf):
        step = pl.program_id(0)
        first_chunk = step == nc - 1
        start = step == 0

        @pl.when(start)
        def _():
            dstate[...] = jnp.zeros_like(dstate)
            carry_x[...] = jnp.zeros_like(carry_x)
            carry_b[...] = jnp.zeros_like(carry_b)
            carry_c[...] = jnp.zeros_like(carry_c)

        xs_ext, xs_pre = _ssd_conv_pre(xs_ref, xsp_ref, wx_ref, bx_ref, first_chunk)
        b_ext, b_pre = _ssd_conv_pre(b_ref, bp_ref, wb_ref, bb_ref, first_chunk)
        c_ext, c_pre = _ssd_conv_pre(c_ref, cp_ref, wc_ref, bc_ref, first_chunk)
        xs, xs_ds = _silu_and_grad(xs_pre)
        bm, b_ds = _silu_and_grad(b_pre)
        cm, c_ds = _silu_and_grad(c_pre)
        dt_pre, dt, a, acs = _ssd_time_consts(small_ref, sp_ref)
        acs_t = acs.T
        li = _lane_iota((L, L))
        ri = _row_iota((L, L))
        tri = ri >= li
        lo = li < HEAD_DIM
        lo_rows = ri < HEAD_DIM
        li1 = _lane_iota((1, L))

        z = z_ref[...]
        sz, dsz = _silu_and_grad(z)
        y = ypre_ref[...]
        yg = y * sz
        dout = dy_ref[...]
        dyg_parts = []
        gnw_parts = []
        for g in range(2):
            sl = slice(512 * g, 512 * g + 512)
            seg = yg[:, sl]
            r = lax.rsqrt(jnp.mean(seg * seg, axis=-1, keepdims=True) + NORM_EPS)
            n = seg * r
            gnw_parts.append(jnp.sum(dout[:, sl] * n, axis=0, keepdims=True))
            gg = dout[:, sl] * nw_ref[:, sl]
            dyg_parts.append(r * (gg - n * jnp.mean(gg * n, axis=-1, keepdims=True)))
        dyg = jnp.concatenate(dyg_parts, axis=1)
        gnw = jnp.concatenate(gnw_parts, axis=1)
        dz_ref[...] = (dyg * y * dsz).astype(BF16)
        dypre = dyg * sz

        ddt = jnp.zeros((L, L), F32)
        dacs = jnp.zeros((L, L), F32)
        dacs_t = jnp.zeros((L, L), F32)
        dalast = jnp.zeros((1, L), F32)
        dskip_g = jnp.zeros((1, L), F32)
        for g in range(2):
            bg = bm[:, 128 * g:128 * g + 128]
            cg = cm[:, 128 * g:128 * g + 128]
            gmat = _dot(cg, bg, "nt")
            dgmat = jnp.zeros((L, L), F32)
            dbg = jnp.zeros((L, L), F32)
            dcg = jnp.zeros((L, L), F32)
            for pp in range(4):
                p = 4 * g + pp
                h0, h1 = 2 * p, 2 * p + 1
                x = xs[:, 128 * p:128 * p + 128]
                dyp = dypre[:, 128 * p:128 * p + 128]
                a0, a1 = acs[:, h0:h0 + 1], acs[:, h1:h1 + 1]
                dtl = jnp.where(lo, dt[:, h0:h0 + 1], dt[:, h1:h1 + 1])
                xdt = x * dtl
                l0 = jnp.exp(jnp.where(tri, a0 - acs_t[h0:h0 + 1, :], NEG_BIG))
                l1 = jnp.exp(jnp.where(tri, a1 - acs_t[h1:h1 + 1, :], NEG_BIG))
                m0, m1 = gmat * l0, gmat * l1
                dskip = jnp.where(lo[0:1], sp_ref[2:3, h0:h0 + 1], sp_ref[2:3, h1:h1 + 1])
                s0, s1 = _half_row_sums(dyp * x, lo)
                dskip_g = dskip_g + jnp.where(li1 == h0, _total(s0), 0.0) + jnp.where(li1 == h1, _total(s1), 0.0)
                dx = dyp * dskip
                dy0, dy1 = jnp.where(lo, dyp, 0.0), jnp.where(lo, 0.0, dyp)
                x0, x1 = jnp.where(lo, xdt, 0.0), jnp.where(lo, 0.0, xdt)
                dm0, dm1 = _dot(dy0, x0, "nt"), _dot(dy1, x1, "nt")
                dxdt = _dot(m0, dy0, "tn") + _dot(m1, dy1, "tn")
                q0, q1 = dm0 * m0, dm1 * m1
                dacs = dacs + jnp.where(li == h0, jnp.sum(q0, axis=1, keepdims=True), 0.0) \
                            + jnp.where(li == h1, jnp.sum(q1, axis=1, keepdims=True), 0.0)
                dacs_t = dacs_t - jnp.where(ri == h0, jnp.sum(q0, axis=0, keepdims=True), 0.0) \
                                - jnp.where(ri == h1, jnp.sum(q1, axis=0, keepdims=True), 0.0)
                dgmat = dgmat + dm0 * l0 + dm1 * l1
                hin = st_ref[0, p]
                e = jnp.exp(jnp.where(lo, a0, a1))
                ch = _dot(cg, hin, "nt")
                dch = dyp * e
                dcg = dcg + _dot(dch, hin)
                dhin = _dot(dch, cg, "tn")
                s0, s1 = _half_row_sums(dch * ch, lo)
                dacs = dacs + jnp.where(li == h0, s0, 0.0) + jnp.where(li == h1, s1, 0.0)
                dhout = dstate[p]
                al0, al1 = acs[L - 1:L, h0:h0 + 1], acs[L - 1:L, h1:h1 + 1]
                dec = jnp.exp(jnp.where(lo_rows, al0, al1))
                dhin = dhin + dec * dhout
                dal = dhout * hin * dec
                dal0 = _total(jnp.where(lo_rows, dal, 0.0))
                dal1 = _total(dal) - dal0
                w = jnp.exp(jnp.where(lo, al0 - a0, al1 - a1))
                xw = xdt * w
                dxw = _dot(bg, dhout, "nt")
                dbg = dbg + _dot(xw, dhout)
                dxdt = dxdt + dxw * w
                s0, s1 = _half_row_sums(dxw * xw, lo)
                dacs = dacs - jnp.where(li == h0, s0, 0.0) - jnp.where(li == h1, s1, 0.0)
                dal0, dal1 = dal0 + _total(s0), dal1 + _total(s1)
                dalast = dalast + jnp.where(li1 == h0, dal0, 0.0) + jnp.where(li1 == h1, dal1, 0.0)
                dx = dx + dxdt * dtl
                s0, s1 = _half_row_sums(dxdt * x, lo)
                ddt = ddt + jnp.where(li == h0, s0, 0.0) + jnp.where(li == h1, s1, 0.0)
                dxs_buf[:, 128 * p:128 * p + 128] = dx
                dstate[p] = dhin
            dcg = dcg + _dot(dgmat, bg)
            dbg = dbg + _dot(dgmat, cg, "tn")
            dbm_buf[:, 128 * g:128 * g + 128] = dbg
            dcm_buf[:, 128 * g:128 * g + 128] = dcg

        dacs_tot = dacs + dacs_t.T + jnp.where(ri == L - 1, dalast, 0.0)
        dstep = _rev_cumsum_rows(dacs_tot)
        ddt = ddt + dstep * a
        head_lane = li < N_HEADS
        ddt_pre = jnp.where(head_lane, ddt * _sigmoid(dt_pre), 0.0)
        dsmall_ref[...] = ddt_pre
        da = jnp.sum(jnp.where(head_lane, dstep * dt, 0.0), axis=0, keepdims=True)
        gsp = _stack_rows([jnp.sum(ddt_pre, axis=0, keepdims=True), da * a, dskip_g], L)

        def conv_back(dpost, ds, ext, w_ref, carry, out_ref, width):
            dpre = dpost * ds
            dext = jnp.concatenate([dpre, carry[...]], axis=0)
            out_ref[...] = _conv_rows_transposed(dext, w_ref[...], SSD_CONV)[:L].astype(BF16)
            carry[...] = dpre[0:8]
            rows = [jnp.sum(dpre * _shift_down(ext, SSD_CONV - 1 - k)[8:], axis=0, keepdims=True) for k in range(SSD_CONV)]
            rows.append(jnp.sum(dpre, axis=0, keepdims=True))
            return _stack_rows(rows, width)

        gwx = conv_back(dxs_buf[...], xs_ds, xs_ext, wx_ref, carry_x, dxs_ref, 1024)
        gwb = conv_back(dbm_buf[...], b_ds, b_ext, wb_ref, carry_b, db_ref, 256)
        gwc = conv_back(dcm_buf[...], c_ds, c_ext, wc_ref, carry_c, dc_ref, 256)

        @pl.when(start)
        def _():
            gwx_ref[...] = gwx
            gwb_ref[...] = gwb
            gwc_ref[...] = gwc
            gsp_ref[...] = gsp
            gnw_ref[...] = gnw

        @pl.when(step > 0)
        def _():
            gwx_ref[...] += gwx
            gwb_ref[...] += gwb
            gwc_ref[...] += gwc
            gsp_ref[...] += gsp
            gnw_ref[...] += gnw

    def ch(c):
        return nc - 1 - c

    row = pl.BlockSpec((L, 1024), lambda c: (ch(c), 0))
    row256 = pl.BlockSpec((L, 256), lambda c: (ch(c), 0))
    in_specs = _ssd_in_specs(rev_nc=nc) + [row, pl.BlockSpec((1, N_PAIRS, 128, 128), lambda c: (ch(c), 0, 0, 0)), row]
    out_specs = [row, row, row256, row256, pl.BlockSpec((L, 128), lambda c: (ch(c), 0)),
                 pl.BlockSpec((8, 1024), lambda c: (0, 0)), pl.BlockSpec((8, 256), lambda c: (0, 0)),
                 pl.BlockSpec((8, 256), lambda c: (0, 0)), pl.BlockSpec((8, 128), lambda c: (0, 0)),
                 pl.BlockSpec((1, 1024), lambda c: (0, 0))]
    out_shape = [jax.ShapeDtypeStruct((s, 1024), BF16), jax.ShapeDtypeStruct((s, 1024), BF16),
                 jax.ShapeDtypeStruct((s, 256), BF16), jax.ShapeDtypeStruct((s, 256), BF16),
                 jax.ShapeDtypeStruct((s, 128), F32),
                 jax.ShapeDtypeStruct((8, 1024), F32), jax.ShapeDtypeStruct((8, 256), F32),
                 jax.ShapeDtypeStruct((8, 256), F32), jax.ShapeDtypeStruct((8, 128), F32),
                 jax.ShapeDtypeStruct((1, 1024), F32)]
    scratch = [pltpu.VMEM((N_PAIRS, 128, 128), F32), pltpu.VMEM((8, 1024), F32), pltpu.VMEM((8, 256), F32),
               pltpu.VMEM((8, 256), F32), pltpu.VMEM((L, 1024), F32), pltpu.VMEM((L, 256), F32), pltpu.VMEM((L, 256), F32)]
    return pl.pallas_call(
        body, name="ssd_bwd", grid=(nc,), in_specs=in_specs, out_specs=out_specs, out_shape=out_shape,
        scratch_shapes=scratch, compiler_params=_params(("arbitrary",)),
    )(proj, proj, proj, proj, proj, proj, proj, conv_w8, conv_w8, conv_w8, conv_b, conv_b, conv_b, small, smallp, norm_w,
      ypre, states, dy)


FOX_SCALE = HEAD_DIM ** -0.5
FOX_T = 256
Q_COL, K_COL, V_COL = 2, 3, 4


def _split3_dot(v, m):
    hi = v.astype(BF16)
    r1 = v - hi.astype(F32)
    mid = r1.astype(BF16)
    lo = (r1 - mid.astype(F32)).astype(BF16)
    return _dot(hi, m) + _dot(mid, m) + _dot(lo, m)


def _head_rstd(x, sel_ref, selt_ref):
    ms = _split3_dot(x * x, sel_ref[...]) * (1.0 / HEAD_DIM)
    return _split3_dot(lax.rsqrt(ms + NORM_EPS), selt_ref[...])


def fox_tables():
    r = jnp.arange(3 * 128)
    piece, lane = r // 128, r % 128
    head = lane - F_LANE
    is_head = jnp.logical_and(head >= 0, head < N_HEADS)
    col = 128 * (head // 2) + HEAD_DIM * (1 - head % 2) + piece
    cols = jnp.arange(1024)
    place_q = jnp.logical_and(is_head[:, None], cols[None, :] == col[:, None]).astype(BF16)
    place_k = jnp.logical_and(is_head[:, None], cols[None, :] == (col + 3)[:, None]).astype(BF16)
    ones_q = jnp.logical_and(cols % HEAD_DIM >= 3, cols % HEAD_DIM < 6).astype(F32)[None]
    ones_k = (cols % HEAD_DIM < 3).astype(F32)[None]
    h = jnp.arange(128) - F_LANE
    ok = jnp.logical_and(h >= 0, h < N_HEADS)
    same_pair = cols[:, None] // 128 == (h // 2)[None, :]
    fold_even = jnp.logical_and(jnp.logical_and(ok, h % 2 == 0)[None, :], same_pair).astype(BF16)
    fold_odd = jnp.logical_and(jnp.logical_and(ok, h % 2 == 1)[None, :], same_pair).astype(BF16)
    return place_q, place_k, ones_q, ones_k, fold_even, fold_odd


def fox_prep(proj, small, smallp, qw, kw, sel, selt, place_q, place_k, ones_q, ones_k, *, tm=256):
    s = proj.shape[0]

    def body(q_ref, k_ref, v_ref, small_ref, sp_ref, qw_ref, kw_ref, sel_ref, selt_ref, pq_ref, pk_ref, oq_ref, ok_ref,
             qn_ref, kn_ref, aq_ref, ak_ref, vb_ref, knt_ref, akt_ref, carry):
        @pl.when(pl.program_id(0) == 0)
        def _():
            carry[...] = jnp.zeros_like(carry)

        q = q_ref[...]
        qn_ref[...] = (((q * _head_rstd(q, sel_ref, selt_ref)) * qw_ref[...]) * FOX_SCALE).astype(BF16)
        k = k_ref[...]
        kn = ((k * _head_rstd(k, sel_ref, selt_ref)) * kw_ref[...]).astype(BF16)
        kn_ref[...] = kn
        knt_ref[...] = kn.astype(F32).T.astype(BF16)
        vb_ref[...] = v_ref[...].astype(BF16)
        li = _lane_iota((tm, 128))
        f_lane = jnp.logical_and(li >= F_LANE, li < F_LANE + N_HEADS)
        logf = jnp.where(f_lane, -_softplus(-(small_ref[...] + sp_ref[3:4, :])), 0.0)
        cum = _cumsum_rows(logf) + carry[...]
        carry[...] = cum[tm - 1:tm, :]
        hi = cum.astype(BF16)
        r1 = cum - hi.astype(F32)
        mid = r1.astype(BF16)
        lo = (r1 - mid.astype(F32)).astype(BF16)
        pieces = jnp.concatenate([hi, mid, lo], axis=1)
        aq_ref[...] = (_dot(pieces, pq_ref[...]) + oq_ref[...]).astype(BF16)
        ak = ok_ref[...] - _dot(pieces, pk_ref[...])
        ak_ref[...] = ak.astype(BF16)
        akt_ref[...] = ak.T.astype(BF16)

    row = pl.BlockSpec((tm, 1024), lambda i: (i, 0))
    col = pl.BlockSpec((1024, tm), lambda i: (0, i))
    vec = pl.BlockSpec((1, 1024), lambda i: (0, 0))
    table = pl.BlockSpec((384, 1024), lambda i: (0, 0))
    wide = jax.ShapeDtypeStruct((s, 1024), BF16)
    tall = jax.ShapeDtypeStruct((1024, s), BF16)
    return pl.pallas_call(
        body, name="fox_prep", grid=(s // tm,),
        in_specs=[pl.BlockSpec((tm, 1024), lambda i: (i, Q_COL)), pl.BlockSpec((tm, 1024), lambda i: (i, K_COL)),
                  pl.BlockSpec((tm, 1024), lambda i: (i, V_COL)),
                  pl.BlockSpec((tm, 128), lambda i: (i, 0)), pl.BlockSpec((8, 128), lambda i: (0, 0)), vec, vec,
                  pl.BlockSpec((1024, 128), lambda i: (0, 0)), pl.BlockSpec((128, 1024), lambda i: (0, 0)),
                  table, table, vec, vec],
        out_specs=[row, row, row, row, row, col, col],
        out_shape=[wide, wide, wide, wide, wide, tall, tall],
        scratch_shapes=[pltpu.VMEM((1, 128), F32)], compiler_params=_params(("arbitrary",)),
    )(proj, proj, proj, small, smallp, qw, kw, sel, selt, place_q, place_k, ones_q, ones_k)


def fox_fwd(qn, kn, aq, ak, vb, shards=()):
    s = qn.shape[0]
    t = FOX_T
    nq = s // t
    ng = len(shards)

    def body(*refs):
        q_ref, k_ref, aq_ref, ak_ref, v_ref = refs[:5]
        o_ref, ot_ref, lse_ref = refs[5 + ng:8 + ng]
        p = pl.program_id(0)
        if ng:
            start, forward, finish = _gather_phases(refs[5:5 + ng], refs[8 + ng:8 + 2 * ng], *refs[8 + 2 * ng:])
            pl.when(p == 0)(start)
            pl.when(p == N_PAIRS // 2)(forward)

        @pl.when(p == 0)
        def _():
            lse_ref[...] = jnp.zeros_like(lse_ref)

        lo = _lane_iota((t, 128)) < HEAD_DIM
        causal = _row_iota((t, t)) >= _lane_iota((t, t))

        def q_loop(qi, _):
            q0 = pl.multiple_of(qi * t, t)
            qv, aqv = q_ref[pl.ds(q0, t), :], aq_ref[pl.ds(q0, t), :]
            qa, qb = jnp.where(lo, qv, aqv), jnp.where(lo, aqv, qv)

            def scores(kj):
                k0 = pl.multiple_of(kj * t, t)
                kv, akv = k_ref[pl.ds(k0, t), :], ak_ref[pl.ds(k0, t), :]
                return _dot(qa, jnp.where(lo, kv, akv), "nt"), _dot(qb, jnp.where(lo, akv, kv), "nt")

            def update(kj, stats, s0, s1):
                m0, l0, m1, l1, acc = stats
                vv = v_ref[pl.ds(pl.multiple_of(kj * t, t), t), :]
                n0 = jnp.maximum(m0, jnp.max(s0, axis=1, keepdims=True))
                n1 = jnp.maximum(m1, jnp.max(s1, axis=1, keepdims=True))
                a0, a1 = jnp.exp(m0 - n0), jnp.exp(m1 - n1)
                p0, p1 = jnp.exp(s0 - n0), jnp.exp(s1 - n1)
                l0 = a0 * l0 + jnp.sum(p0, axis=1, keepdims=True)
                l1 = a1 * l1 + jnp.sum(p1, axis=1, keepdims=True)
                acc = jnp.where(lo, a0, a1) * acc + _dot(p0, jnp.where(lo, vv, 0.0)) + _dot(p1, jnp.where(lo, 0.0, vv))
                return n0, l0, n1, l1, acc

            def step(kj, carry):
                stats, (s0, s1) = carry[:5], carry[5:]
                nxt = scores(kj + 1)
                return (*update(kj, stats, s0, s1), *nxt)

            def col(val):
                return jnp.full((t, 1), val, F32)

            init = (col(NEG_BIG), col(0.0), col(NEG_BIG), col(0.0), jnp.zeros((t, 128), F32), *scores(0))
            carry = lax.fori_loop(0, qi, step, init)
            s0, s1 = jnp.where(causal, carry[5], NEG_BIG), jnp.where(causal, carry[6], NEG_BIG)
            m0, l0, m1, l1, acc = update(qi, carry[:5], s0, s1)
            out = acc / jnp.where(lo, l0, l1)
            o_ref[pl.ds(q0, t), :] = out.astype(BF16)
            ot_ref[:, pl.ds(q0, t)] = out.T.astype(BF16)
            lse_rows = jnp.where(lo, m0 + jnp.log(l0), m1 + jnp.log(l1)).T
            ri = _row_iota((N_HEADS, t))
            old = lse_ref[:, pl.ds(q0, t)]
            lse_ref[:, pl.ds(q0, t)] = jnp.where(
                ri == 2 * p, lse_rows[0:1, :], jnp.where(ri == 2 * p + 1, lse_rows[HEAD_DIM:HEAD_DIM + 1, :], old))
            return 0

        lax.fori_loop(0, nq, q_loop, 0)
        if ng:
            pl.when(p == N_PAIRS - 1)(finish)

    pair = pl.BlockSpec((s, 128), lambda p: (0, p))
    outs = pl.pallas_call(
        body, name="fox_fwd", grid=(N_PAIRS,), in_specs=[pair] * 5 + [ANY] * ng,
        out_specs=[pair, pl.BlockSpec((128, s), lambda p: (p, 0)), pl.BlockSpec((N_HEADS, s), lambda p: (0, 0))] + [ANY] * ng,
        out_shape=[jax.ShapeDtypeStruct((s, 1024), BF16), jax.ShapeDtypeStruct((1024, s), BF16),
                   jax.ShapeDtypeStruct((N_HEADS, s), F32)] + _gather_out_shapes(shards),
        scratch_shapes=_gather_scratch(ng) if ng else [],
        compiler_params=_params(("arbitrary",)),
    )(qn, kn, aq, ak, vb, *shards)
    return outs[0], outs[1], outs[2], _place_own_blocks(outs[3:], shards)


def fox_bwd(qn, kn, aq, ak, knt, akt, vb, lse, dmixed, parts=()):
    s = qn.shape[0]
    t = FOX_T
    nq = s // t
    once = pl.Buffered(1)
    ns = len(parts)

    def body(*refs):
        q_ref, k_ref, aq_ref, ak_ref, kt_ref, akt_ref, v_ref, lse_ref, do_ref = refs[:9]
        dq_ref, dk_ref, dv_ref, dc0_ref, dc1_ref = refs[9 + ns:14 + ns]
        p_scr, dp_scr = refs[14 + 2 * ns:16 + 2 * ns]
        p = pl.program_id(0)
        if ns:
            start, finish = _scatter_phases(refs[9:9 + ns], refs[14 + ns:14 + 2 * ns], *refs[16 + 2 * ns:])
            pl.when(p == 0)(start)
        dk_ref[...] = jnp.zeros_like(dk_ref)
        dv_ref[...] = jnp.zeros_like(dv_ref)
        dc0_ref[...] = jnp.zeros_like(dc0_ref)
        dc1_ref[...] = jnp.zeros_like(dc1_ref)
        lo = _lane_iota((t, 128)) < HEAD_DIM
        lo_rows = _row_iota((128, t)) < HEAD_DIM
        causal_t = _lane_iota((t, t)) >= _row_iota((t, t))

        def q_loop(qi, _):
            q0 = pl.multiple_of(qi * t, t)
            qv, aqv = q_ref[pl.ds(q0, t), :], aq_ref[pl.ds(q0, t), :]
            qa, qb = jnp.where(lo, qv, aqv), jnp.where(lo, aqv, qv)
            do = do_ref[pl.ds(q0, t), :]
            doa, dob = jnp.where(lo, do, 0.0).astype(BF16), jnp.where(lo, 0.0, do).astype(BF16)
            lse_blk = lse_ref[:, pl.ds(q0, t)]
            ri = _row_iota((N_HEADS, t))
            lse0 = jnp.sum(jnp.where(ri == 2 * p, lse_blk, 0.0), axis=0, keepdims=True)
            lse1 = jnp.sum(jnp.where(ri == 2 * p + 1, lse_blk, 0.0), axis=0, keepdims=True)

            def pass1(kj, carry, diagonal):
                d0, d1 = carry
                k0 = pl.multiple_of(kj * t, t)
                kv, akv, vv = k_ref[pl.ds(k0, t), :], ak_ref[pl.ds(k0, t), :], v_ref[pl.ds(k0, t), :]
                s0 = _dot(jnp.where(lo, kv, akv), qa, "nt")
                s1 = _dot(jnp.where(lo, akv, kv), qb, "nt")
                if diagonal:
                    s0, s1 = jnp.where(causal_t, s0, NEG_BIG), jnp.where(causal_t, s1, NEG_BIG)
                p0, p1 = jnp.exp(s0 - lse0), jnp.exp(s1 - lse1)
                dp0, dp1 = _dot(vv, doa, "nt"), _dot(vv, dob, "nt")
                p_scr[0, kj], p_scr[1, kj] = p0, p1
                dp_scr[0, kj], dp_scr[1, kj] = dp0, dp1
                return d0 + jnp.sum(p0 * dp0, axis=0, keepdims=True), d1 + jnp.sum(p1 * dp1, axis=0, keepdims=True)

            zero = jnp.zeros((1, t), F32)
            carry = lax.fori_loop(0, qi, lambda kj, c: pass1(kj, c, False), (zero, zero))
            d0, d1 = pass1(qi, carry, True)

            def pass2(kj, carry):
                dq0, dq1 = carry
                k0 = pl.multiple_of(kj * t, t)
                p0, p1 = p_scr[0, kj], p_scr[1, kj]
                ds0, ds1 = p0 * (dp_scr[0, kj] - d0), p1 * (dp_scr[1, kj] - d1)
                dk_ref[pl.ds(k0, t), :] += jnp.where(lo, _dot(ds0, qa), _dot(ds1, qb))
                dv_ref[pl.ds(k0, t), :] += _dot(p0, doa) + _dot(p1, dob)
                dc0_ref[pl.ds(k0, t), :] += ds0[:, :128] + ds0[:, 128:]
                dc1_ref[pl.ds(k0, t), :] += ds1[:, :128] + ds1[:, 128:]
                ktv, aktv = kt_ref[:, pl.ds(k0, t)], akt_ref[:, pl.ds(k0, t)]
                return dq0 + _dot(jnp.where(lo_rows, ktv, aktv), ds0), dq1 + _dot(jnp.where(lo_rows, aktv, ktv), ds1)

            zq = jnp.zeros((128, t), F32)
            dq0, dq1 = lax.fori_loop(0, qi + 1, pass2, (zq, zq))
            dq_ref[pl.ds(q0, t), :] = jnp.where(lo_rows, dq0, dq1).T
            return 0

        lax.fori_loop(0, nq, q_loop, 0)
        if ns:
            pl.when(p == N_PAIRS - 1)(finish)

    pair = pl.BlockSpec((s, 128), lambda p: (0, p), pipeline_mode=once)
    pair_t = pl.BlockSpec((128, s), lambda p: (p, 0), pipeline_mode=once)
    out = jax.ShapeDtypeStruct((s, 1024), F32)
    outs = pl.pallas_call(
        body, name="fox_bwd", grid=(N_PAIRS,),
        in_specs=[pair, pair, pair, pair, pair_t, pair_t, pair, pl.BlockSpec((N_HEADS, s), lambda p: (0, 0)),
                  pl.BlockSpec((s, 128), lambda p: (0, 8 + p), pipeline_mode=once)] + [ANY] * ns,
        out_specs=[pair] * 5 + [ANY] * ns,
        out_shape=[out] * 5 + [jax.ShapeDtypeStruct(p.shape, p.dtype) for p in parts],
        scratch_shapes=[pltpu.VMEM((2, nq, t, t), F32), pltpu.VMEM((2, nq, t, t), F32)] + (_scatter_scratch(ns) if ns else []),
        compiler_params=_params(("arbitrary",)),
    )(qn, kn, aq, ak, knt, akt, vb, lse, dmixed, *parts)
    return (*outs[:5], _keep_own_blocks(outs[5:], parts))


def fox_post(dqn, dkn, dc0, dc1, proj, small, smallp, qw, kw, sel, selt, fold_even, fold_odd, *, tm=256):
    s = proj.shape[0]
    nrow = s // tm

    def body(dqn_ref, dkn_ref, dc0_ref, dc1_ref, q_ref, k_ref, small_ref, sp_ref, qw_ref, kw_ref, sel_ref, selt_ref,
             fe_ref, fo_ref, dq_ref, dk_ref, dsmall_ref, gqw_ref, gkw_ref, gfb_ref, carry):
        step = pl.program_id(0)

        @pl.when(step == 0)
        def _():
            carry[...] = jnp.zeros_like(carry)

        def norm_bwd(x_ref, w_ref, dn, out_ref):
            x = x_ref[...]
            rf = _head_rstd(x, sel_ref, selt_ref)
            xh = x * rf
            g = dn * w_ref[...]
            mean_gx = _split3_dot(_split3_dot(g * xh, sel_ref[...]) * (1.0 / HEAD_DIM), selt_ref[...])
            out_ref[...] = (rf * (g - xh * mean_gx)).astype(BF16)
            return jnp.sum(dn * xh, axis=0, keepdims=True)

        gqw = norm_bwd(q_ref, qw_ref, dqn_ref[...] * FOX_SCALE, dq_ref)
        gkw = norm_bwd(k_ref, kw_ref, dkn_ref[...], dk_ref)
        li = _lane_iota((tm, 128))
        f_lane = jnp.logical_and(li >= F_LANE, li < F_LANE + N_HEADS)
        dcum = -(_split3_dot(dc0_ref[...], fe_ref[...]) + _split3_dot(dc1_ref[...], fo_ref[...]))
        dlogf = _rev_cumsum_rows(dcum) + carry[...]
        carry[...] = dlogf[0:1, :]
        dfr = jnp.where(f_lane, dlogf * _sigmoid(-(small_ref[...] + sp_ref[3:4, :])), 0.0)
        dsmall_ref[...] = dfr
        gfb = jnp.sum(dfr, axis=0, keepdims=True)

        @pl.when(step == 0)
        def _():
            gqw_ref[...] = gqw
            gkw_ref[...] = gkw
            gfb_ref[...] = gfb

        @pl.when(step > 0)
        def _():
            gqw_ref[...] += gqw
            gkw_ref[...] += gkw
            gfb_ref[...] += gfb

    def rb(i):
        return nrow - 1 - i

    row = pl.BlockSpec((tm, 1024), lambda i: (rb(i), 0))
    vec = pl.BlockSpec((1, 1024), lambda i: (0, 0))
    fold = pl.BlockSpec((1024, 128), lambda i: (0, 0))
    return pl.pallas_call(
        body, name="fox_post", grid=(nrow,),
        in_specs=[row, row, row, row, pl.BlockSpec((tm, 1024), lambda i: (rb(i), Q_COL)),
                  pl.BlockSpec((tm, 1024), lambda i: (rb(i), K_COL)),
                  pl.BlockSpec((tm, 128), lambda i: (rb(i), 0)), pl.BlockSpec((8, 128), lambda i: (0, 0)), vec, vec,
                  fold, pl.BlockSpec((128, 1024), lambda i: (0, 0)), fold, fold],
        out_specs=[row, row, pl.BlockSpec((tm, 128), lambda i: (rb(i), 0)), vec, vec, pl.BlockSpec((1, 128), lambda i: (0, 0))],
        out_shape=[jax.ShapeDtypeStruct((s, 1024), BF16), jax.ShapeDtypeStruct((s, 1024), BF16),
                   jax.ShapeDtypeStruct((s, 128), F32), jax.ShapeDtypeStruct((1, 1024), F32),
                   jax.ShapeDtypeStruct((1, 1024), F32), jax.ShapeDtypeStruct((1, 128), F32)],
        scratch_shapes=[pltpu.VMEM((1, 128), F32)], compiler_params=_params(("arbitrary",)),
    )(dqn, dkn, dc0, dc1, proj, proj, small, smallp, qw, kw, sel, selt, fold_even, fold_odd)


def local_step(x, target, wm, ws, later_shards, ssd_cw8, ssd_cb, smallp, ssd_nw, qw_t, kw_t, sel, selt,
               norm_mix_w, norm_ffn_w, ffn_cw8, ffn_cb):
    h, h_t = rms_fwd(x, norm_mix_w, name="rms_mix_fwd")
    proj = matmul(h, wm, mode="nn", tm=1024, tn=1408, tk=1024, out_dtype=F32, name="mm_in_proj")
    small = matmul(h, ws, mode="nn", tm=1024, tn=128, tk=1024, out_dtype=F32, name="mm_in_proj_small")
    y_ssd, y_ssd_t, ypre, states = ssd_fwd(proj, small, ssd_cw8, ssd_cb, smallp, ssd_nw)
    place_q, place_k, ones_q, ones_k, fold_even, fold_odd = fox_tables()
    qn, kn, aq, ak, vb, knt, akt = fox_prep(proj, small, smallp, qw_t, kw_t, sel, selt, place_q, place_k, ones_q, ones_k)
    y_fox, y_fox_t, lse, (a_out, a_up, a_down) = fox_fwd(qn, kn, aq, ak, vb, shards=later_shards)
    w_out = a_out.reshape(2048, D_MODEL)
    w_up = a_up.transpose(1, 0, 2).reshape(D_MODEL, 2 * D_FF)
    w_down = a_down.reshape(D_FF, D_MODEL)
    x1 = matmul(y_ssd, w_out, mode="nn", tm=1024, tn=1024, tk=1024, out_dtype=F32, name="mm_out_ssd", add=x)
    x1 = matmul(y_fox, w_out, mode="nn", tm=1024, tn=1024, tk=1024, out_dtype=F32, name="mm_out_fox", add=x1, b_koff=1)
    hf, hf_t = rms_fwd(x1, norm_ffn_w, name="rms_ffn_fwd")
    hu = matmul(hf, w_up, mode="nn", tm=1024, tn=1408, tk=1024, out_dtype=F32, name="mm_up")
    act, act_t = ffn_mid_fwd(hu, ffn_cw8, ffn_cb)
    y = matmul(act, w_down, mode="nn", tm=1024, tn=1024, tk=1408, out_dtype=F32, name="mm_down", add=x1)
    dy, sq = loss_head(y, target)

    dact = matmul(dy, w_down, mode="nt", tm=1024, tn=1408, tk=1024, out_dtype=F32, name="mm_dact")
    g_down = matmul(act_t, dy, mode="nn", tm=1408, tn=1024, tk=1024, out_dtype=BF16, name="mm_dw_down")
    dhu_g, dhu_v, gcw_g, gcw_v = ffn_mid_bwd(hu, dact, ffn_cw8, ffn_cb)
    dhf = matmul(dhu_g, w_up, mode="nt", tm=1024, tn=1024, tk=1408, out_dtype=F32, name="mm_dhf_gate")
    dhf = matmul(dhu_v, w_up, mode="nt", tm=1024, tn=1024, tk=1408, out_dtype=F32, name="mm_dhf_val", add=dhf, b_koff=2)
    g_up_g = matmul(hf_t, dhu_g, mode="nn", tm=1024, tn=1408, tk=1024, out_dtype=BF16, name="mm_dw_up_gate")
    g_up_v = matmul(hf_t, dhu_v, mode="nn", tm=1024, tn=1408, tk=1024, out_dtype=BF16, name="mm_dw_up_val")
    dx1, g_norm_ffn = rms_bwd(dhf, x1, norm_ffn_w, dy, name="rms_ffn_bwd")
    dmixed = matmul(dx1, w_out, mode="nt", tm=1024, tn=1024, tk=1024, out_dtype=F32, name="mm_dmixed")
    g_out_a = matmul(y_ssd_t, dx1, mode="nn", tm=1024, tn=1024, tk=1024, out_dtype=BF16, name="mm_dw_out_ssd")
    g_out_b = matmul(y_fox_t, dx1, mode="nn", tm=1024, tn=1024, tk=1024, out_dtype=BF16, name="mm_dw_out_fox")
    early = [jnp.concatenate([g_out_a, g_out_b], axis=0).reshape(4, 512, D_MODEL),
             jnp.concatenate([g_up_g, g_up_v], axis=1).reshape(D_MODEL, 4, 1408).transpose(1, 0, 2),
             g_down.reshape(4, 704, D_MODEL)]
    mine, theirs = pair_swap_halves(early, name="pair_swap_early")
    parts = [add_pair(a, b, name="add_pair_" + n, tr=ADAM_ROWS[n]) for a, b, n in zip(mine, theirs, BIG_NAMES[1:])]
    dz, dxs, db, dc, dsmall_ssd, gcw_x, gcw_b, gcw_c, g_sp, g_ssd_nw = ssd_bwd(
        proj, small, ssd_cw8, ssd_cb, smallp, ssd_nw, ypre, states, dmixed)
    dqn, dkn, dv, dc0, dc1, landed_early = fox_bwd(qn, kn, aq, ak, knt, akt, vb, lse, dmixed, parts=parts)
    dq, dk, dsmall_fox, g_qw, g_kw, g_fb = fox_post(dqn, dkn, dc0, dc1, proj, small, smallp, qw_t, kw_t, sel, selt,
                                                    fold_even, fold_odd)
    dproj = jnp.concatenate([dz, dxs, dq, dk, dv.astype(BF16), db, dc], axis=1)
    dsmall = (dsmall_ssd + dsmall_fox).astype(BF16)
    dh = matmul(dproj, wm, mode="nt", tm=1024, tn=1024, tk=1408, out_dtype=F32, name="mm_dh")
    dh = matmul(dsmall, ws, mode="nt", tm=1024, tn=1024, tk=128, out_dtype=F32, name="mm_dh_small", add=dh)
    g_wm = matmul(h_t, dproj, mode="nn", tm=1024, tn=1408, tk=1024, out_dtype=BF16, name="mm_dw_in")
    g_ws = matmul(h_t, dsmall, mode="nn", tm=1024, tn=128, tk=1024, out_dtype=BF16, name="mm_dw_in_small")
    grad_x, g_norm_mix = rms_bwd(dh, x, norm_mix_w, dx1, name="rms_mix_bwd")
    return dict(
        sq=sq, grad_x=grad_x, g_wm=g_wm, g_ws=g_ws, landed_early=landed_early,
        g_norm_mix=g_norm_mix, g_norm_ffn=g_norm_ffn, g_ssd_nw=g_ssd_nw,
        g_ssd_cw=jnp.concatenate([gcw_x, gcw_b, gcw_c], axis=1), g_sp=g_sp, g_fb=g_fb, g_qw=g_qw, g_kw=g_kw,
        g_ffn_cw=jnp.concatenate([gcw_g, gcw_v], axis=1))


def adamw(w, g, m, v, *, name, tr):
    rows, cols = w.shape

    def body(w_ref, g_ref, m_ref, v_ref, d_ref, mo_ref, vo_ref):
        gv = g_ref[...]
        mn = ADAM_B1 * m_ref[...] + (1.0 - ADAM_B1) * gv
        vn = ADAM_B2 * v_ref[...] + (1.0 - ADAM_B2) * (gv * gv)
        m_hat = mn / (1.0 - ADAM_B1 ** ADAM_STEP)
        v_hat = vn / (1.0 - ADAM_B2 ** ADAM_STEP)
        d_ref[...] = -ADAM_LR * (m_hat / (jnp.sqrt(v_hat) + ADAM_EPS) + ADAM_WD * w_ref[...])
        mo_ref[...] = mn
        vo_ref[...] = vn

    blk = pl.BlockSpec((tr, cols), lambda i: (i, 0))
    shp = jax.ShapeDtypeStruct((rows, cols), F32)
    return pl.pallas_call(
        body, name=name, grid=(rows // tr,), in_specs=[blk] * 4, out_specs=[blk] * 3, out_shape=[shp] * 3,
        compiler_params=_params(("parallel",)),
    )(w, g, m, v)


def add_pair(a, b, *, name, tr):
    _, rows, cols = a.shape

    def body(a_ref, b_ref, o_ref):
        o_ref[...] = (a_ref[...].astype(F32) + b_ref[...].astype(F32)).astype(BF16)

    blk = pl.BlockSpec((1, tr, cols), lambda j, i: (j, i, 0))
    return pl.pallas_call(
        body, name=name, grid=(4, rows // tr), in_specs=[blk, blk], out_specs=blk,
        out_shape=jax.ShapeDtypeStruct(a.shape, BF16), compiler_params=_params(("parallel", "parallel")),
    )(a, b)


def sum_chips(parts, *, name, tr):
    _, rows, cols = parts.shape

    def body(p_ref, o_ref):
        acc = p_ref[0].astype(F32)
        for k in range(1, 4):
            acc = acc + p_ref[k].astype(F32)
        o_ref[...] = acc

    return pl.pallas_call(
        body, name=name, grid=(rows // tr,), in_specs=[pl.BlockSpec((4, tr, cols), lambda i: (0, i, 0))],
        out_specs=pl.BlockSpec((tr, cols), lambda i: (i, 0)), out_shape=jax.ShapeDtypeStruct((rows, cols), F32),
        compiler_params=_params(("parallel",)),
    )(parts)


ANY = pl.BlockSpec(memory_space=pl.ANY)


def _place():
    x, y, c = lax.axis_index("x"), lax.axis_index("y"), lax.axis_index("c")
    chips = [(1 - x, y), (x, 1 - y), (1 - x, 1 - y)]
    return x, y, c, chips


def _chunks(rows):
    size = next((c for c in (128, 176, 64, 32, 16, 8) if rows % c == 0), rows)
    return [(r, size) for r in range(0, rows, size)]


def gather_weights(shards):
    n = len(shards)

    def body(*refs):
        start, forward, finish = _gather_phases(refs[:n], refs[n:2 * n], *refs[2 * n:])
        start()
        forward()
        finish()

    gathered = pl.pallas_call(
        body, name="gather_weights", in_specs=[ANY] * n, out_specs=[ANY] * n,
        out_shape=_gather_out_shapes(shards), scratch_shapes=_gather_scratch(n),
    )(*shards)
    return _place_own_blocks(gathered, shards)


def _gather_out_shapes(shards):
    return [jax.ShapeDtypeStruct((4,) + s.shape, s.dtype) for s in shards]


def _gather_scratch(n):
    return [pltpu.SemaphoreType.DMA((n, 6)), pltpu.SemaphoreType.DMA((n, 6))]


def _place_own_blocks(gathered, shards):
    chip = 2 * lax.axis_index("x") + lax.axis_index("y")
    return [lax.dynamic_update_slice(g, s[None], (chip, 0, 0)) for g, s in zip(gathered, shards)]


def _gather_phases(ins, outs, send_sems, recv_sems):
    n = len(ins)
    x, y, c, chips = _place()
    me = 2 * x + y
    sibling = (x, y, 1 - c)
    blks = [2 * cx + cy for cx, cy in chips]

    def half(a, blk, r=0, nr=None):
        rows = ins[a].shape[0] // 2
        return outs[a].at[blk, pl.ds(c * rows + r, rows if nr is None else nr), :]

    def to_chip(a, t, r=0, nr=None):
        rows = ins[a].shape[0] // 2
        return pltpu.make_async_remote_copy(
            src_ref=ins[a].at[pl.ds(c * rows + r, rows if nr is None else nr), :], dst_ref=half(a, me, r, nr),
            send_sem=send_sems.at[a, t], recv_sem=recv_sems.at[a, t], device_id=(*chips[t], c), device_id_type=MESH)

    def from_chip(a, t):
        return pltpu.make_async_remote_copy(
            src_ref=half(a, blks[t]), dst_ref=half(a, blks[t]), send_sem=send_sems.at[a, t], recv_sem=recv_sems.at[a, t],
            device_id=(*chips[t], c), device_id_type=MESH)

    def to_sibling(a, t, r=0, nr=None):
        return pltpu.make_async_remote_copy(
            src_ref=half(a, blks[t], r, nr), dst_ref=half(a, blks[t], r, nr), send_sem=send_sems.at[a, 3 + t],
            recv_sem=recv_sems.at[a, 3 + t], device_id=sibling, device_id_type=MESH)

    def from_sibling(a, t):
        rows = ins[a].shape[0] // 2
        dst = outs[a].at[blks[t], pl.ds((1 - c) * rows, rows), :]
        return pltpu.make_async_remote_copy(
            src_ref=dst, dst_ref=dst, send_sem=send_sems.at[a, 3 + t], recv_sem=recv_sems.at[a, 3 + t],
            device_id=sibling, device_id_type=MESH)

    def start():
        for a in range(n):
            for t in range(3):
                for r, nr in _chunks(ins[a].shape[0] // 2):
                    to_chip(a, t, r, nr).start()

    def forward():
        for a in range(n):
            for t in range(3):
                from_chip(a, t).wait_recv()
                for r, nr in _chunks(ins[a].shape[0] // 2):
                    to_sibling(a, t, r, nr).start()

    def finish():
        for a in range(n):
            for t in range(3):
                from_sibling(a, t).wait_recv()
        for a in range(n):
            for t in range(3):
                to_chip(a, t).wait_send()
                to_sibling(a, t).wait_send()

    return start, forward, finish


def pair_swap_halves(grads, *, name):
    n = len(grads)

    def body(*refs):
        ins, theirs = refs[:n], refs[n:2 * n]
        send_sems, recv_sems = refs[2 * n:]
        x, y, c, _ = _place()
        sibling = (x, y, 1 - c)
        for a in range(n):
            rows = ins[a].shape[1] // 2
            for j in range(4):
                for r, nr in _chunks(rows):
                    pltpu.make_async_remote_copy(
                        src_ref=ins[a].at[j, pl.ds((1 - c) * rows + r, nr), :], dst_ref=theirs[a].at[j, pl.ds(r, nr), :],
                        send_sem=send_sems.at[a], recv_sem=recv_sems.at[a], device_id=sibling, device_id_type=MESH).start()
        for a in range(n):
            pltpu.make_async_remote_copy(src_ref=theirs[a], dst_ref=theirs[a], send_sem=send_sems.at[a],
                                         recv_sem=recv_sems.at[a], device_id=sibling, device_id_type=MESH).wait()

    halves = [jax.ShapeDtypeStruct((4, g.shape[1] // 2, g.shape[2]), g.dtype) for g in grads]
    theirs = pl.pallas_call(
        body, name=name, in_specs=[ANY] * n, out_specs=[ANY] * n, out_shape=halves,
        scratch_shapes=[pltpu.SemaphoreType.DMA((n,)), pltpu.SemaphoreType.DMA((n,))],
    )(*grads)
    c = lax.axis_index("c")
    mine = [lax.dynamic_slice_in_dim(g, c * (g.shape[1] // 2), g.shape[1] // 2, axis=1) for g in grads]
    return mine, theirs


def scatter_to_chips(parts):
    n = len(parts)

    def body(*refs):
        start, finish = _scatter_phases(refs[:n], refs[n:2 * n], *refs[2 * n:])
        start()
        finish()

    landed = pl.pallas_call(
        body, name="scatter_to_chips", in_specs=[ANY] * n, out_specs=[ANY] * n,
        out_shape=[jax.ShapeDtypeStruct(p.shape, p.dtype) for p in parts], scratch_shapes=_scatter_scratch(n),
    )(*parts)
    return _keep_own_blocks(landed, parts)


def _scatter_scratch(n):
    return [pltpu.SemaphoreType.DMA((n, 3)), pltpu.SemaphoreType.DMA((n, 3))]


def _keep_own_blocks(landed, parts):
    chip = 2 * lax.axis_index("x") + lax.axis_index("y")
    return [lax.dynamic_update_slice(l, lax.dynamic_slice_in_dim(p, chip, 1, axis=0), (chip, 0, 0))
            for l, p in zip(landed, parts)]


def _scatter_phases(ins, outs, send_sems, recv_sems):
    n = len(ins)
    x, y, c, chips = _place()
    me = 2 * x + y
    blks = [2 * cx + cy for cx, cy in chips]

    def start():
        for a in range(n):
            for r, nr in _chunks(ins[a].shape[1]):
                for t in range(3):
                    pltpu.make_async_remote_copy(
                        src_ref=ins[a].at[blks[t], pl.ds(r, nr), :], dst_ref=outs[a].at[me, pl.ds(r, nr), :],
                        send_sem=send_sems.at[a, t], recv_sem=recv_sems.at[a, t],
                        device_id=(*chips[t], c), device_id_type=MESH).start()

    def finish():
        for a in range(n):
            for t in range(3):
                pltpu.make_async_remote_copy(
                    src_ref=outs[a].at[blks[t]], dst_ref=outs[a].at[blks[t]], send_sem=send_sems.at[a, t],
                    recv_sem=recv_sems.at[a, t], device_id=(*chips[t], c), device_id_type=MESH).wait()

    return start, finish


def pair_join_halves(halves):
    n = len(halves)

    def body(*refs):
        ins, outs = refs[:n], refs[n:2 * n]
        send_sems, recv_sems = refs[2 * n:]
        x, y, c, _ = _place()
        sibling = (x, y, 1 - c)
        for a in range(n):
            rows = ins[a].shape[0]
            for r, nr in _chunks(rows):
                pltpu.make_async_remote_copy(
                    src_ref=ins[a].at[pl.ds(r, nr), :], dst_ref=outs[a].at[pl.ds(c * rows + r, nr), :],
                    send_sem=send_sems.at[a], recv_sem=recv_sems.at[a], device_id=sibling, device_id_type=MESH).start()
        for a in range(n):
            rows = ins[a].shape[0]
            got = outs[a].at[pl.ds((1 - c) * rows, rows), :]
            pltpu.make_async_remote_copy(src_ref=ins[a], dst_ref=got, send_sem=send_sems.at[a], recv_sem=recv_sems.at[a],
                                         device_id=sibling, device_id_type=MESH).wait()

    joined = pl.pallas_call(
        body, name="pair_join_halves", in_specs=[ANY] * n, out_specs=[ANY] * n,
        out_shape=[jax.ShapeDtypeStruct((2 * h.shape[0], h.shape[1]), h.dtype) for h in halves],
        scratch_shapes=[pltpu.SemaphoreType.DMA((n,)), pltpu.SemaphoreType.DMA((n,))],
    )(*halves)
    c = lax.axis_index("c")
    return [lax.dynamic_update_slice(j, h, (c * h.shape[0], 0)) for j, h in zip(joined, halves)]


def allreduce_small(packed):
    rows = packed.shape[0]

    def body(in_ref, out_ref, gathered, send_sems, recv_sems):
        x, y, c, _ = _place()
        me = 4 * x + 2 * y + c
        gathered[me] = in_ref[...]
        flips = [(fx, fy, fc) for fx in (0, 1) for fy in (0, 1) for fc in (0, 1)][1:]
        peers = [((1 - x) if fx else x, (1 - y) if fy else y, (1 - c) if fc else c) for fx, fy, fc in flips]
        copies = []
        for t, peer in enumerate(peers):
            cp = pltpu.make_async_remote_copy(
                src_ref=in_ref, dst_ref=gathered.at[me], send_sem=send_sems.at[t], recv_sem=recv_sems.at[t],
                device_id=peer, device_id_type=MESH)
            cp.start()
            copies.append(cp)
        for t, (px, py, pc) in enumerate(peers):
            slot = gathered.at[4 * px + 2 * py + pc]
            pltpu.make_async_remote_copy(
                src_ref=slot, dst_ref=slot, send_sem=send_sems.at[t], recv_sem=recv_sems.at[t],
                device_id=(px, py, pc), device_id_type=MESH).wait_recv()
        for cp in copies:
            cp.wait_send()
        acc = gathered[0]
        for k in range(1, 8):
            acc = acc + gathered[k]
        out_ref[...] = acc

    vm = pl.BlockSpec(memory_space=pltpu.VMEM)
    return pl.pallas_call(
        body, name="allreduce_small", in_specs=[vm], out_specs=vm, out_shape=jax.ShapeDtypeStruct(packed.shape, F32),
        scratch_shapes=[pltpu.VMEM((8, rows, 128), F32), pltpu.SemaphoreType.DMA((7,)), pltpu.SemaphoreType.DMA((7,))],
    )(packed)


SMALL_NAMES = ("norm_mix_w", "ssd_conv_w", "ssd_conv_b", "ssd_dt_bias", "ssd_a_log", "ssd_d", "ssd_norm_w", "fox_f_bias",
               "fox_q_norm_w", "fox_k_norm_w", "norm_ffn_w", "ffn_conv_w", "ffn_conv_b")
BIG_NAMES = ("w_in", "w_out", "w_up", "w_down")
WEIGHT_ORDER = ("norm_mix_w", "w_in", "ssd_conv_w", "ssd_conv_b", "ssd_dt_bias", "ssd_a_log", "ssd_d", "ssd_norm_w",
                "fox_f_bias", "fox_q_norm_w", "fox_k_norm_w", "w_out", "norm_ffn_w", "w_up", "ffn_conv_w", "ffn_conv_b", "w_down")
ADAM_ROWS = {"w_in": 256, "w_out": 256, "w_up": 256, "w_down": 176}


def _pack(arrays):
    rows = []
    for a in arrays:
        flat = a.reshape(-1).astype(F32)
        rows.append(jnp.pad(flat, (0, (-flat.shape[0]) % 1024)).reshape(-1, 128))
    return jnp.concatenate(rows, axis=0)


def _unpack(packed, shapes):
    out, r = [], 0
    for shp in shapes:
        size = 1
        for d in shp:
            size *= d
        nrow = 8 * (-(-size // 1024))
        out.append(packed[r:r + nrow].reshape(-1)[:size].reshape(shp))
        r += nrow
    return out


def _pad_rows(a, rows):
    return jnp.pad(a, ((0, rows - a.shape[0]), (0, 0)))


def kernel(x, norm_mix_w, w_in, ssd_conv_w, ssd_conv_b, ssd_dt_bias, ssd_a_log, ssd_d, ssd_norm_w, fox_f_bias, fox_q_norm_w, fox_k_norm_w, w_out, norm_ffn_w, w_up, ffn_conv_w, ffn_conv_b, w_down, loss_target, m_norm_mix_w, m_w_in, m_ssd_conv_w, m_ssd_conv_b, m_ssd_dt_bias, m_ssd_a_log, m_ssd_d, m_ssd_norm_w, m_fox_f_bias, m_fox_q_norm_w, m_fox_k_norm_w, m_w_out, m_norm_ffn_w, m_w_up, m_ffn_conv_w, m_ffn_conv_b, m_w_down, v_norm_mix_w, v_w_in, v_ssd_conv_w, v_ssd_conv_b, v_ssd_dt_bias, v_ssd_a_log, v_ssd_d, v_ssd_norm_w, v_fox_f_bias, v_fox_q_norm_w, v_fox_k_norm_w, v_w_out, v_norm_ffn_w, v_w_up, v_ffn_conv_w, v_ffn_conv_b, v_w_down):
    w = dict(norm_mix_w=norm_mix_w, w_in=w_in, ssd_conv_w=ssd_conv_w, ssd_conv_b=ssd_conv_b, ssd_dt_bias=ssd_dt_bias,
             ssd_a_log=ssd_a_log, ssd_d=ssd_d, ssd_norm_w=ssd_norm_w, fox_f_bias=fox_f_bias, fox_q_norm_w=fox_q_norm_w,
             fox_k_norm_w=fox_k_norm_w, w_out=w_out, norm_ffn_w=norm_ffn_w, w_up=w_up, ffn_conv_w=ffn_conv_w,
             ffn_conv_b=ffn_conv_b, w_down=w_down)
    m = dict(norm_mix_w=m_norm_mix_w, w_in=m_w_in, ssd_conv_w=m_ssd_conv_w, ssd_conv_b=m_ssd_conv_b, ssd_dt_bias=m_ssd_dt_bias,
             ssd_a_log=m_ssd_a_log, ssd_d=m_ssd_d, ssd_norm_w=m_ssd_norm_w, fox_f_bias=m_fox_f_bias, fox_q_norm_w=m_fox_q_norm_w,
             fox_k_norm_w=m_fox_k_norm_w, w_out=m_w_out, norm_ffn_w=m_norm_ffn_w, w_up=m_w_up, ffn_conv_w=m_ffn_conv_w,
             ffn_conv_b=m_ffn_conv_b, w_down=m_w_down)
    v = dict(norm_mix_w=v_norm_mix_w, w_in=v_w_in, ssd_conv_w=v_ssd_conv_w, ssd_conv_b=v_ssd_conv_b, ssd_dt_bias=v_ssd_dt_bias,
             ssd_a_log=v_ssd_a_log, ssd_d=v_ssd_d, ssd_norm_w=v_ssd_norm_w, fox_f_bias=v_fox_f_bias, fox_q_norm_w=v_fox_q_norm_w,
             fox_k_norm_w=v_fox_k_norm_w, w_out=v_w_out, norm_ffn_w=v_norm_ffn_w, w_up=v_w_up, ffn_conv_w=v_ffn_conv_w,
             ffn_conv_b=v_ffn_conv_b, w_down=v_w_down)
    chip = 2 * lax.axis_index("x") + lax.axis_index("y")

    a_in, a_scw, a_fcw = gather_weights([w_in[0].astype(BF16), _pad_rows(ssd_conv_w[0], 16), _pad_rows(ffn_conv_w[0], 16)])
    later_shards = [w_out[0].astype(BF16), w_up[0].astype(BF16), w_down[0].astype(BF16)]
    w_full = a_in.transpose(1, 0, 2).reshape(D_MODEL, IN_COLS)
    wm = jnp.concatenate([w_full[:, :2048], w_full[:, 2576:5648], w_full[:, 2048:2560]], axis=1)
    ws = jnp.concatenate([w_full[:, 2560:2576], w_full[:, 5648:5664], jnp.zeros((D_MODEL, SMALL_COLS - 32), BF16)], axis=1)
    ssd_cw8 = a_scw.transpose(1, 0, 2).reshape(16, 1536)[:8]
    ffn_cw8 = a_fcw.transpose(1, 0, 2).reshape(16, 2 * D_FF)[:8]
    smallp = jnp.zeros((8, 128), F32)
    smallp = smallp.at[0, :16].set(ssd_dt_bias[0]).at[1, :16].set(ssd_a_log[0]).at[2, :16].set(ssd_d[0])
    smallp = smallp.at[3, F_LANE:F_LANE + 16].set(fox_f_bias[0])
    qw_t = jnp.tile(fox_q_norm_w[0], N_HEADS)[None]
    kw_t = jnp.tile(fox_k_norm_w[0], N_HEADS)[None]
    sel = (jnp.arange(1024)[:, None] // HEAD_DIM == jnp.arange(128)[None, :]).astype(BF16)

    res = local_step(x[0], loss_target[0], wm, ws, later_shards, ssd_cw8, ssd_conv_b, smallp, ssd_norm_w, qw_t, kw_t,
                     sel, sel.T, norm_mix_w, norm_ffn_w, ffn_cw8, ffn_conv_b)

    full_shapes = [(1, 1024), (1, 4, 1536), (1, 1536), (1, 16), (1, 16), (1, 16), (1, 1024), (1, 16), (1, 64), (1, 64),
                   (1, 1024), (1, 3, 2 * D_FF), (1, 2 * D_FF), (1,)]
    local_small = [res["g_norm_mix"], res["g_ssd_cw"][:4], res["g_ssd_cw"][4], res["g_sp"][0, :16], res["g_sp"][1, :16],
                   res["g_sp"][2, :16], res["g_ssd_nw"], res["g_fb"][0, F_LANE:F_LANE + 16],
                   res["g_qw"].reshape(N_HEADS, HEAD_DIM).sum(0), res["g_kw"].reshape(N_HEADS, HEAD_DIM).sum(0),
                   res["g_norm_ffn"], res["g_ffn_cw"][:3], res["g_ffn_cw"][3], jnp.sum(res["sq"])]
    summed = _unpack(allreduce_small(_pack(local_small)), full_shapes)
    loss = (0.5 / D_MODEL) * summed[-1][0]
    g_small = dict(zip(SMALL_NAMES, summed[:-1]))
    g_small["ssd_conv_w"] = lax.dynamic_slice(g_small["ssd_conv_w"], (0, 0, 384 * chip), (1, 4, 384))
    g_small["ffn_conv_w"] = lax.dynamic_slice(g_small["ffn_conv_w"], (0, 0, 1408 * chip), (1, 3, 1408))

    g_wm, g_ws = res["g_wm"], res["g_ws"]
    g_in_full = jnp.concatenate([g_wm[:, :2048], g_wm[:, 5120:5632], g_ws[:, :16], g_wm[:, 2048:5120], g_ws[:, 16:32]], axis=1)
    mine, theirs = pair_swap_halves([g_in_full.reshape(D_MODEL, 4, 1416).transpose(1, 0, 2)], name="pair_swap_w_in")
    landed_in = scatter_to_chips([add_pair(mine[0], theirs[0], name="add_pair_w_in", tr=ADAM_ROWS["w_in"])])
    landed = landed_in + res["landed_early"]
    halves = [sum_chips(p, name="sum_chips_" + n, tr=ADAM_ROWS[n]) for p, n in zip(landed, BIG_NAMES)]
    g_big = dict(zip(BIG_NAMES, pair_join_halves(halves)))

    grads, deltas, new_m, new_v = {}, {}, {}, {}
    for n in BIG_NAMES:
        d, mn, vn = adamw(w[n][0], g_big[n], m[n][0], v[n][0], name="adamw_" + n, tr=ADAM_ROWS[n])
        grads[n], deltas[n], new_m[n], new_v[n] = g_big[n][None], d[None], mn[None], vn[None]
    shapes = [w[n].shape for n in SMALL_NAMES]
    d, mn, vn = adamw(_pack([w[n] for n in SMALL_NAMES]), _pack([g_small[n] for n in SMALL_NAMES]),
                      _pack([m[n] for n in SMALL_NAMES]), _pack([v[n] for n in SMALL_NAMES]), name="adamw_small", tr=8)
    for n, dd, mm, vv in zip(SMALL_NAMES, _unpack(d, shapes), _unpack(mn, shapes), _unpack(vn, shapes)):
        grads[n], deltas[n], new_m[n], new_v[n] = g_small[n].reshape(w[n].shape), dd, mm, vv
    return (loss, res["grad_x"][None], *[grads[n] for n in WEIGHT_ORDER], *[deltas[n] for n in WEIGHT_ORDER],
            *[new_m[n] for n in WEIGHT_ORDER], *[new_v[n] for n in WEIGHT_ORDER])
```

```python
import functools

import jax
import jax.numpy as jnp
from jax import lax
from jax.experimental import pallas as pl
from jax.experimental.pallas import tpu as pltpu

F32 = jnp.float32
BF16 = jnp.bfloat16
MESH = pl.DeviceIdType.MESH

D_MODEL = 1024
HEAD_DIM = 64
N_HEADS = 16
N_PAIRS = N_HEADS // 2
SSD_CHUNK = 128
SSD_STATE = 128
SSD_CONV = 4
D_FF = 2816
FFN_CONV = 3
NORM_EPS = 1e-6
MAIN_COLS = 5632
SMALL_COLS = 128
F_LANE = 16
IN_COLS = 5664

ADAM_LR = 0.001
ADAM_B1 = 0.9
ADAM_B2 = 0.999
ADAM_EPS = 1e-08
ADAM_WD = 0.01
ADAM_STEP = 10

VMEM_LIMIT_V7X = 56 * 1024 * 1024
NEG_BIG = -1e30


def _params(sem=None):
    return pltpu.CompilerParams(dimension_semantics=sem, vmem_limit_bytes=VMEM_LIMIT_V7X)


def _sigmoid(x):
    return 1.0 / (1.0 + jnp.exp(-x))


def _silu_and_grad(x):
    s = _sigmoid(x)
    return x * s, s * (1.0 + x * (1.0 - s))


def _shift_down(v, j):
    return v if j == 0 else pltpu.roll(v, j, 0)


def _shift_up(v, j):
    return v if j == 0 else pltpu.roll(v, v.shape[0] - j, 0)


def _row_iota(shape):
    return lax.broadcasted_iota(jnp.int32, shape, 0)


def _lane_iota(shape):
    return lax.broadcasted_iota(jnp.int32, shape, 1)


def _dot(a, b, mode="nn"):
    dims = {"nn": (((1,), (0,)), ((), ())), "nt": (((1,), (1,)), ((), ())), "tn": (((0,), (0,)), ((), ()))}[mode]
    return lax.dot_general(a.astype(BF16), b.astype(BF16), dims, preferred_element_type=F32)


def _dot_f32(a, b):
    return jnp.dot(a, b, precision=lax.Precision.HIGHEST, preferred_element_type=F32)


def matmul(a, b, *, mode, tm, tn, tk, out_dtype, name, add=None, b_koff=0):
    (m, k), n = a.shape, (b.shape[1] if mode == "nn" else b.shape[0])
    assert m % tm == 0 and n % tn == 0 and k % tk == 0, (name, m, n, k, tm, tn, tk)
    nk = k // tk
    a_spec = pl.BlockSpec((tm, tk), lambda i, j, kk: (i, kk))
    b_spec = (pl.BlockSpec((tn, tk), lambda i, j, kk: (j, kk + b_koff)) if mode == "nt"
              else pl.BlockSpec((tk, tn), lambda i, j, kk: (kk + b_koff, j)))
    o_spec = pl.BlockSpec((tm, tn), lambda i, j, kk: (i, j))
    has_add = add is not None

    def body(*refs):
        if has_add:
            a_ref, b_ref, add_ref, o_ref, acc_ref = refs
        else:
            a_ref, b_ref, o_ref, acc_ref = refs
        kk = pl.program_id(2)
        part = _dot(a_ref[...], b_ref[...], mode)

        def finish(total):
            if has_add:
                total = total + add_ref[...]
            o_ref[...] = total.astype(out_dtype)

        if nk == 1:
            finish(part)
        else:
            @pl.when(kk == 0)
            def _():
                acc_ref[...] = part

            @pl.when(jnp.logical_and(kk > 0, kk < nk - 1))
            def _():
                acc_ref[...] += part

            @pl.when(kk == nk - 1)
            def _():
                finish(acc_ref[...] + part)

    in_specs = [a_spec, b_spec] + ([o_spec] if has_add else [])
    args = (a, b) + ((add,) if has_add else ())
    return pl.pallas_call(
        body, name=name, grid=(m // tm, n // tn, nk), in_specs=in_specs, out_specs=o_spec,
        out_shape=jax.ShapeDtypeStruct((m, n), out_dtype),
        scratch_shapes=[pltpu.VMEM((tm, tn) if nk > 1 else (8, 128), F32)],
        compiler_params=_params(("parallel", "parallel", "arbitrary")),
    )(*args)


def rms_fwd(x, w, *, name, tm=512):
    s, d = x.shape

    def body(x_ref, w_ref, h_ref, ht_ref):
        xv = x_ref[...]
        r = lax.rsqrt(jnp.mean(xv * xv, axis=-1, keepdims=True) + NORM_EPS)
        h = (xv * r) * w_ref[...]
        h_ref[...] = h.astype(BF16)
        ht_ref[...] = h.T.astype(BF16)

    return pl.pallas_call(
        body, name=name, grid=(s // tm,),
        in_specs=[pl.BlockSpec((tm, d), lambda i: (i, 0)), pl.BlockSpec((1, d), lambda i: (0, 0))],
        out_specs=[pl.BlockSpec((tm, d), lambda i: (i, 0)), pl.BlockSpec((d, tm), lambda i: (0, i))],
        out_shape=[jax.ShapeDtypeStruct((s, d), BF16), jax.ShapeDtypeStruct((d, s), BF16)],
        compiler_params=_params(("parallel",)),
    )(x, w)


def rms_bwd(dh, x, w, resid, *, name, tm=512):
    s, d = x.shape

    def body(dh_ref, x_ref, w_ref, res_ref, dx_ref, dw_ref):
        xv = x_ref[...]
        dhv = dh_ref[...]
        r = lax.rsqrt(jnp.mean(xv * xv, axis=-1, keepdims=True) + NORM_EPS)
        xh = xv * r
        g = dhv * w_ref[...]
        dx_ref[...] = res_ref[...] + r * (g - xh * jnp.mean(g * xh, axis=-1, keepdims=True))
        part = jnp.sum(dhv * xh, axis=0, keepdims=True)

        @pl.when(pl.program_id(0) == 0)
        def _():
            dw_ref[...] = part

        @pl.when(pl.program_id(0) > 0)
        def _():
            dw_ref[...] += part

    row = pl.BlockSpec((tm, d), lambda i: (i, 0))
    vec = pl.BlockSpec((1, d), lambda i: (0, 0))
    return pl.pallas_call(
        body, name=name, grid=(s // tm,), in_specs=[row, row, vec, row], out_specs=[row, vec],
        out_shape=[jax.ShapeDtypeStruct((s, d), F32), jax.ShapeDtypeStruct((1, d), F32)],
        compiler_params=_params(("arbitrary",)),
    )(dh, x, w, resid)


def loss_head(y, target, *, tm=512):
    s, d = y.shape

    def body(y_ref, t_ref, dy_ref, sq_ref):
        e = y_ref[...] - t_ref[...]
        dy_ref[...] = e / float(d)
        part = jnp.sum(e * e, axis=0, keepdims=True)

        @pl.when(pl.program_id(0) == 0)
        def _():
            sq_ref[...] = part

        @pl.when(pl.program_id(0) > 0)
        def _():
            sq_ref[...] += part

    row = pl.BlockSpec((tm, d), lambda i: (i, 0))
    vec = pl.BlockSpec((1, d), lambda i: (0, 0))
    return pl.pallas_call(
        body, name="loss_head", grid=(s // tm,), in_specs=[row, row], out_specs=[row, vec],
        out_shape=[jax.ShapeDtypeStruct((s, d), F32), jax.ShapeDtypeStruct((1, d), F32)],
        compiler_params=_params(("arbitrary",)),
    )(y, target)


def _conv_rows(ext, w, k_taps):
    acc = w[k_taps - 1:k_taps, :] * ext
    for k in range(k_taps - 1):
        acc = acc + w[k:k + 1, :] * _shift_down(ext, k_taps - 1 - k)
    return acc


def _conv_rows_transposed(dext, w, k_taps):
    acc = w[k_taps - 1:k_taps, :] * dext
    for k in range(k_taps - 1):
        acc = acc + w[k:k + 1, :] * _shift_up(dext, k_taps - 1 - k)
    return acc


def _stack_rows(rows, width):
    ri = _row_iota((8, width))
    out = jnp.zeros((8, width), F32)
    for k, r in enumerate(rows):
        out = out + jnp.where(ri == k, r, 0.0)
    return out


def ffn_mid_fwd(hu, conv_w8, conv_b, *, tm=512, tc=256):
    s = hu.shape[0]
    ncol = D_FF // tc
    r8 = tm // 8

    def body(g_ref, v_ref, gp_ref, vp_ref, wg_ref, wv_ref, bg_ref, bv_ref, o_ref, ot_ref):
        first = pl.program_id(1) == 0

        def conv(cur_ref, prev_ref, w_ref, b_ref):
            prev = jnp.where(first, 0.0, prev_ref[...])
            ext = jnp.concatenate([prev, cur_ref[...]], axis=0)
            return _conv_rows(ext, w_ref[...], FFN_CONV)[8:] + b_ref[...]

        gc = conv(g_ref, gp_ref, wg_ref, bg_ref)
        vc = conv(v_ref, vp_ref, wv_ref, bv_ref)
        act = gc * _sigmoid(gc) * vc
        o_ref[...] = act.astype(BF16)
        ot_ref[...] = act.T.astype(BF16)

    def prev_idx(i):
        return jnp.maximum(i * r8 - 1, 0)

    in_specs = [
        pl.BlockSpec((tm, tc), lambda j, i: (i, j)),
        pl.BlockSpec((tm, tc), lambda j, i: (i, j + ncol)),
        pl.BlockSpec((8, tc), lambda j, i: (prev_idx(i), j)),
        pl.BlockSpec((8, tc), lambda j, i: (prev_idx(i), j + ncol)),
        pl.BlockSpec((8, tc), lambda j, i: (0, j)),
        pl.BlockSpec((8, tc), lambda j, i: (0, j + ncol)),
        pl.BlockSpec((1, tc), lambda j, i: (0, j)),
        pl.BlockSpec((1, tc), lambda j, i: (0, j + ncol)),
    ]
    return pl.pallas_call(
        body, name="ffn_mid_fwd", grid=(ncol, s // tm), in_specs=in_specs,
        out_specs=[pl.BlockSpec((tm, tc), lambda j, i: (i, j)), pl.BlockSpec((tc, tm), lambda j, i: (j, i))],
        out_shape=[jax.ShapeDtypeStruct((s, D_FF), BF16), jax.ShapeDtypeStruct((D_FF, s), BF16)],
        compiler_params=_params(("parallel", "parallel")),
    )(hu, hu, hu, hu, conv_w8, conv_w8, conv_b, conv_b)


def ffn_mid_bwd(hu, dact, conv_w8, conv_b, *, tm=512, tc=256):
    s = hu.shape[0]
    ncol = D_FF // tc
    nrow = s // tm
    r8 = tm // 8

    def body(g_ref, v_ref, gp_ref, vp_ref, gn_ref, vn_ref, da_ref, dan_ref, wg_ref, wv_ref, bg_ref, bv_ref,
             dg_ref, dv_ref, wgo_ref, wvo_ref):
        i = pl.program_id(1)
        first = i == 0
        last = i == nrow - 1

        def ext_of(cur_ref, prev_ref, next_ref):
            prev = jnp.where(first, 0.0, prev_ref[...])
            return jnp.concatenate([prev, cur_ref[...], next_ref[...]], axis=0)

        g_ext = ext_of(g_ref, gp_ref, gn_ref)
        v_ext = ext_of(v_ref, vp_ref, vn_ref)
        gc = _conv_rows(g_ext, wg_ref[...], FFN_CONV) + bg_ref[...]
        vc = _conv_rows(v_ext, wv_ref[...], FFN_CONV) + bv_ref[...]
        da_ext = jnp.concatenate([jnp.zeros((8, tc), F32), da_ref[...], jnp.where(last, 0.0, dan_ref[...])], axis=0)
        silu, dsilu = _silu_and_grad(gc)
        dgc = da_ext * vc * dsilu
        dvc = da_ext * silu
        dg_ref[...] = _conv_rows_transposed(dgc, wg_ref[...], FFN_CONV)[8:8 + tm].astype(BF16)
        dv_ref[...] = _conv_rows_transposed(dvc, wv_ref[...], FFN_CONV)[8:8 + tm].astype(BF16)

        def wgrad(dcur, x_ext):
            rows = [jnp.sum(dcur * _shift_down(x_ext, FFN_CONV - 1 - k)[8:8 + tm], axis=0, keepdims=True)
                    for k in range(FFN_CONV)]
            rows.append(jnp.sum(dcur, axis=0, keepdims=True))
            return _stack_rows(rows, tc)

        pg = wgrad(dgc[8:8 + tm], g_ext)
        pv = wgrad(dvc[8:8 + tm], v_ext)

        @pl.when(first)
        def _():
            wgo_ref[...] = pg
            wvo_ref[...] = pv

        @pl.when(i > 0)
        def _():
            wgo_ref[...] += pg
            wvo_ref[...] += pv

    def prev_idx(i):
        return jnp.maximum(i * r8 - 1, 0)

    def next_idx(i):
        return jnp.minimum((i + 1) * r8, s // 8 - 1)

    cur_g = pl.BlockSpec((tm, tc), lambda j, i: (i, j))
    cur_v = pl.BlockSpec((tm, tc), lambda j, i: (i, j + ncol))
    in_specs = [
        cur_g, cur_v,
        pl.BlockSpec((8, tc), lambda j, i: (prev_idx(i), j)),
        pl.BlockSpec((8, tc), lambda j, i: (prev_idx(i), j + ncol)),
        pl.BlockSpec((8, tc), lambda j, i: (next_idx(i), j)),
        pl.BlockSpec((8, tc), lambda j, i: (next_idx(i), j + ncol)),
        cur_g,
        pl.BlockSpec((8, tc), lambda j, i: (next_idx(i), j)),
        pl.BlockSpec((8, tc), lambda j, i: (0, j)),
        pl.BlockSpec((8, tc), lambda j, i: (0, j + ncol)),
        pl.BlockSpec((1, tc), lambda j, i: (0, j)),
        pl.BlockSpec((1, tc), lambda j, i: (0, j + ncol)),
    ]
    out_specs = [cur_g, cur_g, pl.BlockSpec((8, tc), lambda j, i: (0, j)), pl.BlockSpec((8, tc), lambda j, i: (0, j))]
    out_shape = [jax.ShapeDtypeStruct((s, D_FF), BF16), jax.ShapeDtypeStruct((s, D_FF), BF16),
                 jax.ShapeDtypeStruct((8, D_FF), F32), jax.ShapeDtypeStruct((8, D_FF), F32)]
    return pl.pallas_call(
        body, name="ffn_mid_bwd", grid=(ncol, nrow), in_specs=in_specs, out_specs=out_specs, out_shape=out_shape,
        compiler_params=_params(("parallel", "arbitrary")),
    )(hu, hu, hu, hu, hu, hu, dact, dact, conv_w8, conv_w8, conv_b, conv_b)


def _softplus(x):
    return jnp.maximum(x, 0.0) + jnp.log(1.0 + jnp.exp(-jnp.abs(x)))


def _cumsum_rows(v):
    n = v.shape[0]
    ri = _row_iota(v.shape)
    sh = 1
    while sh < n:
        v = v + jnp.where(ri >= sh, _shift_down(v, sh), 0.0)
        sh *= 2
    return v


def _rev_cumsum_rows(v):
    n = v.shape[0]
    ri = _row_iota(v.shape)
    sh = 1
    while sh < n:
        v = v + jnp.where(ri < n - sh, _shift_up(v, sh), 0.0)
        sh *= 2
    return v


def _half_row_sums(v, lo):
    s0 = jnp.sum(jnp.where(lo, v, 0.0), axis=1, keepdims=True)
    return s0, jnp.sum(v, axis=1, keepdims=True) - s0


def _total(v):
    return jnp.sum(jnp.sum(v, axis=1, keepdims=True), axis=0, keepdims=True)


def _ssd_in_specs(rev_nc=None):
    def ch(c):
        return c if rev_nc is None else rev_nc - 1 - c

    def prev(c):
        return jnp.maximum(ch(c) * (SSD_CHUNK // 8) - 1, 0)

    L = SSD_CHUNK
    return [
        pl.BlockSpec((L, 1024), lambda c: (ch(c), 0)),
        pl.BlockSpec((L, 1024), lambda c: (ch(c), 1)),
        pl.BlockSpec((L, 256), lambda c: (ch(c), 20)),
        pl.BlockSpec((L, 256), lambda c: (ch(c), 21)),
        pl.BlockSpec((8, 1024), lambda c: (prev(c), 1)),
        pl.BlockSpec((8, 256), lambda c: (prev(c), 20)),
        pl.BlockSpec((8, 256), lambda c: (prev(c), 21)),
        pl.BlockSpec((8, 1024), lambda c: (0, 0)),
        pl.BlockSpec((8, 256), lambda c: (0, 4)),
        pl.BlockSpec((8, 256), lambda c: (0, 5)),
        pl.BlockSpec((1, 1024), lambda c: (0, 0)),
        pl.BlockSpec((1, 256), lambda c: (0, 4)),
        pl.BlockSpec((1, 256), lambda c: (0, 5)),
        pl.BlockSpec((L, SMALL_COLS), lambda c: (ch(c), 0)),
        pl.BlockSpec((8, 128), lambda c: (0, 0)),
        pl.BlockSpec((1, 1024), lambda c: (0, 0)),
    ]


def _ssd_conv_pre(cur_ref, prev_ref, w_ref, b_ref, first):
    prev = jnp.where(first, 0.0, prev_ref[...])
    ext = jnp.concatenate([prev, cur_ref[...]], axis=0)
    return ext, _conv_rows(ext, w_ref[...], SSD_CONV)[8:] + b_ref[...]


def _ssd_time_consts(small_ref, sp_ref):
    dt_pre = small_ref[...] + sp_ref[0:1, :]
    dt = _softplus(dt_pre)
    a = -jnp.exp(sp_ref[1:2, :])
    acs = _cumsum_rows(dt * a)
    return dt_pre, dt, a, acs


def ssd_fwd(proj, small, conv_w8, conv_b, smallp, norm_w):
    s = proj.shape[0]
    nc = s // SSD_CHUNK
    L = SSD_CHUNK

    def body(z_ref, xs_ref, b_ref, c_ref, xsp_ref, bp_ref, cp_ref, wx_ref, wb_ref, wc_ref, bx_ref, bb_ref, bc_ref,
             small_ref, sp_ref, nw_ref, y_ref, yt_ref, ypre_ref, st_ref, state):
        first = pl.program_id(0) == 0

        @pl.when(first)
        def _():
            state[...] = jnp.zeros_like(state)

        xs = _ssd_conv_pre(xs_ref, xsp_ref, wx_ref, bx_ref, first)[1]
        xs = xs * _sigmoid(xs)
        bm = _ssd_conv_pre(b_ref, bp_ref, wb_ref, bb_ref, first)[1]
        bm = bm * _sigmoid(bm)
        cm = _ssd_conv_pre(c_ref, cp_ref, wc_ref, bc_ref, first)[1]
        cm = cm * _sigmoid(cm)
        _, dt, _, acs = _ssd_time_consts(small_ref, sp_ref)
        acs_t = acs.T
        li = _lane_iota((L, L))
        ri = _row_iota((L, L))
        tri = ri >= li
        lo = li < HEAD_DIM
        st_ref[0] = state[...]
        for g in range(2):
            bg = bm[:, 128 * g:128 * g + 128]
            cg = cm[:, 128 * g:128 * g + 128]
            gmat = _dot(cg, bg, "nt")
            for pp in range(4):
                p = 4 * g + pp
                h0, h1 = 2 * p, 2 * p + 1
                x = xs[:, 128 * p:128 * p + 128]
                a0, a1 = acs[:, h0:h0 + 1], acs[:, h1:h1 + 1]
                xdt = x * jnp.where(lo, dt[:, h0:h0 + 1], dt[:, h1:h1 + 1])
                m0 = gmat * jnp.exp(jnp.where(tri, a0 - acs_t[h0:h0 + 1, :], NEG_BIG))
                m1 = gmat * jnp.exp(jnp.where(tri, a1 - acs_t[h1:h1 + 1, :], NEG_BIG))
                yd = _dot(m0, jnp.where(lo, xdt, 0.0)) + _dot(m1, jnp.where(lo, 0.0, xdt))
                hin = state[p]
                yo = _dot(cg, hin, "nt") * jnp.exp(jnp.where(lo, a0, a1))
                dskip = jnp.where(lo[0:1], sp_ref[2:3, h0:h0 + 1], sp_ref[2:3, h1:h1 + 1])
                ypre_ref[:, 128 * p:128 * p + 128] = yd + yo + dskip * x
                al0, al1 = acs[L - 1:L, h0:h0 + 1], acs[L - 1:L, h1:h1 + 1]
                w = jnp.exp(jnp.where(lo, al0 - a0, al1 - a1))
                dec = jnp.exp(jnp.where(ri < HEAD_DIM, al0, al1))
                state[p] = dec * hin + _dot(xdt * w, bg, "tn")
        z = z_ref[...]
        yg = ypre_ref[...] * (z * _sigmoid(z))
        for g in range(2):
            seg = yg[:, 512 * g:512 * g + 512]
            r = lax.rsqrt(jnp.mean(seg * seg, axis=-1, keepdims=True) + NORM_EPS)
            out = (seg * r) * nw_ref[:, 512 * g:512 * g + 512]
            y_ref[:, 512 * g:512 * g + 512] = out.astype(BF16)
            yt_ref[512 * g:512 * g + 512, :] = out.T.astype(BF16)

    row = pl.BlockSpec((L, 1024), lambda c: (c, 0))
    return pl.pallas_call(
        body, name="ssd_fwd", grid=(nc,), in_specs=_ssd_in_specs(),
        out_specs=[row, pl.BlockSpec((1024, L), lambda c: (0, c)), row,
                   pl.BlockSpec((1, N_PAIRS, 128, 128), lambda c: (c, 0, 0, 0))],
        out_shape=[jax.ShapeDtypeStruct((s, 1024), BF16), jax.ShapeDtypeStruct((1024, s), BF16),
                   jax.ShapeDtypeStruct((s, 1024), F32), jax.ShapeDtypeStruct((nc, N_PAIRS, 128, 128), F32)],
        scratch_shapes=[pltpu.VMEM((N_PAIRS, 128, 128), F32)],
        compiler_params=_params(("arbitrary",)),
    )(proj, proj, proj, proj, proj, proj, proj, conv_w8, conv_w8, conv_w8, conv_b, conv_b, conv_b, small, smallp, norm_w)


def ssd_bwd(proj, small, conv_w8, conv_b, smallp, norm_w, ypre, states, dy):
    s = proj.shape[0]
    nc = s // SSD_CHUNK
    L = SSD_CHUNK

    def body(z_ref, xs_ref, b_ref, c_ref, xsp_ref, bp_ref, cp_ref, wx_ref, wb_ref, wc_ref, bx_ref, bb_ref, bc_ref,
             small_ref, sp_ref, nw_ref, ypre_ref, st_ref, dy_ref,
             dz_ref, dxs_ref, db_ref, dc_ref, dsmall_ref, gwx_ref, gwb_ref, gwc_ref, gsp_ref, gnw_ref,
             dstate, carry_x, carry_b, carry_c, dxs_buf, dbm_buf, dcm_buf):
        step = pl.program_id(0)
        first_chunk = step == nc - 1
        start = step == 0

        @pl.when(start)
        def _():
            dstate[...] = jnp.zeros_like(dstate)
            carry_x[...] = jnp.zeros_like(carry_x)
            carry_b[...] = jnp.zeros_like(carry_b)
            carry_c[...] = jnp.zeros_like(carry_c)

        xs_ext, xs_pre = _ssd_conv_pre(xs_ref, xsp_ref, wx_ref, bx_ref, first_chunk)
        b_ext, b_pre = _ssd_conv_pre(b_ref, bp_ref, wb_ref, bb_ref, first_chunk)
        c_ext, c_pre = _ssd_conv_pre(c_ref, cp_ref, wc_ref, bc_ref, first_chunk)
        xs, xs_ds = _silu_and_grad(xs_pre)
        bm, b_ds = _silu_and_grad(b_pre)
        cm, c_ds = _silu_and_grad(c_pre)
        dt_pre, dt, a, acs = _ssd_time_consts(small_ref, sp_ref)
        acs_t = acs.T
        li = _lane_iota((L, L))
        ri = _row_iota((L, L))
        tri = ri >= li
        lo = li < HEAD_DIM
        lo_rows = ri < HEAD_DIM
        li1 = _lane_iota((1, L))

        z = z_ref[...]
        sz, dsz = _silu_and_grad(z)
        y = ypre_ref[...]
        yg = y * sz
        dout = dy_ref[...]
        dyg_parts = []
        gnw_parts = []
        for g in range(2):
            sl = slice(512 * g, 512 * g + 512)
            seg = yg[:, sl]
            r = lax.rsqrt(jnp.mean(seg * seg, axis=-1, keepdims=True) + NORM_EPS)
            n = seg * r
            gnw_parts.append(jnp.sum(dout[:, sl] * n, axis=0, keepdims=True))
            gg = dout[:, sl] * nw_ref[:, sl]
            dyg_parts.append(r * (gg - n * jnp.mean(gg * n, axis=-1, keepdims=True)))
        dyg = jnp.concatenate(dyg_parts, axis=1)
        gnw = jnp.concatenate(gnw_parts, axis=1)
        dz_ref[...] = (dyg * y * dsz).astype(BF16)
        dypre = dyg * sz

        ddt = jnp.zeros((L, L), F32)
        dacs = jnp.zeros((L, L), F32)
        dacs_t = jnp.zeros((L, L), F32)
        dalast = jnp.zeros((1, L), F32)
        dskip_g = jnp.zeros((1, L), F32)
        for g in range(2):
            bg = bm[:, 128 * g:128 * g + 128]
            cg = cm[:, 128 * g:128 * g + 128]
            gmat = _dot(cg, bg, "nt")
            dgmat = jnp.zeros((L, L), F32)
            dbg = jnp.zeros((L, L), F32)
            dcg = jnp.zeros((L, L), F32)
            for pp in range(4):
                p = 4 * g + pp
                h0, h1 = 2 * p, 2 * p + 1
                x = xs[:, 128 * p:128 * p + 128]
                dyp = dypre[:, 128 * p:128 * p + 128]
                a0, a1 = acs[:, h0:h0 + 1], acs[:, h1:h1 + 1]
                dtl = jnp.where(lo, dt[:, h0:h0 + 1], dt[:, h1:h1 + 1])
                xdt = x * dtl
                l0 = jnp.exp(jnp.where(tri, a0 - acs_t[h0:h0 + 1, :], NEG_BIG))
                l1 = jnp.exp(jnp.where(tri, a1 - acs_t[h1:h1 + 1, :], NEG_BIG))
                m0, m1 = gmat * l0, gmat * l1
                dskip = jnp.where(lo[0:1], sp_ref[2:3, h0:h0 + 1], sp_ref[2:3, h1:h1 + 1])
                s0, s1 = _half_row_sums(dyp * x, lo)
                dskip_g = dskip_g + jnp.where(li1 == h0, _total(s0), 0.0) + jnp.where(li1 == h1, _total(s1), 0.0)
                dx = dyp * dskip
                dy0, dy1 = jnp.where(lo, dyp, 0.0), jnp.where(lo, 0.0, dyp)
                x0, x1 = jnp.where(lo, xdt, 0.0), jnp.where(lo, 0.0, xdt)
                dm0, dm1 = _dot(dy0, x0, "nt"), _dot(dy1, x1, "nt")
                dxdt = _dot(m0, dy0, "tn") + _dot(m1, dy1, "tn")
                q0, q1 = dm0 * m0, dm1 * m1
                dacs = dacs + jnp.where(li == h0, jnp.sum(q0, axis=1, keepdims=True), 0.0) \
                            + jnp.where(li == h1, jnp.sum(q1, axis=1, keepdims=True), 0.0)
                dacs_t = dacs_t - jnp.where(ri == h0, jnp.sum(q0, axis=0, keepdims=True), 0.0) \
                                - jnp.where(ri == h1, jnp.sum(q1, axis=0, keepdims=True), 0.0)
                dgmat = dgmat + dm0 * l0 + dm1 * l1
                hin = st_ref[0, p]
                e = jnp.exp(jnp.where(lo, a0, a1))
                ch = _dot(cg, hin, "nt")
                dch = dyp * e
                dcg = dcg + _dot(dch, hin)
                dhin = _dot(dch, cg, "tn")
                s0, s1 = _half_row_sums(dch * ch, lo)
                dacs = dacs + jnp.where(li == h0, s0, 0.0) + jnp.where(li == h1, s1, 0.0)
                dhout = dstate[p]
                al0, al1 = acs[L - 1:L, h0:h0 + 1], acs[L - 1:L, h1:h1 + 1]
                dec = jnp.exp(jnp.where(lo_rows, al0, al1))
                dhin = dhin + dec * dhout
                dal = dhout * hin * dec
                dal0 = _total(jnp.where(lo_rows, dal, 0.0))
                dal1 = _total(dal) - dal0
                w = jnp.exp(jnp.where(lo, al0 - a0, al1 - a1))
                xw = xdt * w
                dxw = _dot(bg, dhout, "nt")
                dbg = dbg + _dot(xw, dhout)
                dxdt = dxdt + dxw * w
                s0, s1 = _half_row_sums(dxw * xw, lo)
                dacs = dacs - jnp.where(li == h0, s0, 0.0) - jnp.where(li == h1, s1, 0.0)
                dal0, dal1 = dal0 + _total(s0), dal1 + _total(s1)
                dalast = dalast + jnp.where(li1 == h0, dal0, 0.0) + jnp.where(li1 == h1, dal1, 0.0)
                dx = dx + dxdt * dtl
                s0, s1 = _half_row_sums(dxdt * x, lo)
                ddt = ddt + jnp.where(li == h0, s0, 0.0) + jnp.where(li == h1, s1, 0.0)
                dxs_buf[:, 128 * p:128 * p + 128] = dx
                dstate[p] = dhin
            dcg = dcg + _dot(dgmat, bg)
            dbg = dbg + _dot(dgmat, cg, "tn")
            dbm_buf[:, 128 * g:128 * g + 128] = dbg
            dcm_buf[:, 128 * g:128 * g + 128] = dcg

        dacs_tot = dacs + dacs_t.T + jnp.where(ri == L - 1, dalast, 0.0)
        dstep = _rev_cumsum_rows(dacs_tot)
        ddt = ddt + dstep * a
        head_lane = li < N_HEADS
        ddt_pre = jnp.where(head_lane, ddt * _sigmoid(dt_pre), 0.0)
        dsmall_ref[...] = ddt_pre
        da = jnp.sum(jnp.where(head_lane, dstep * dt, 0.0), axis=0, keepdims=True)
        gsp = _stack_rows([jnp.sum(ddt_pre, axis=0, keepdims=True), da * a, dskip_g], L)

        def conv_back(dpost, ds, ext, w_ref, carry, out_ref, width):
            dpre = dpost * ds
            dext = jnp.concatenate([dpre, carry[...]], axis=0)
            out_ref[...] = _conv_rows_transposed(dext, w_ref[...], SSD_CONV)[:L].astype(BF16)
            carry[...] = dpre[0:8]
            rows = [jnp.sum(dpre * _shift_down(ext, SSD_CONV - 1 - k)[8:], axis=0, keepdims=True) for k in range(SSD_CONV)]
            rows.append(jnp.sum(dpre, axis=0, keepdims=True))
            return _stack_rows(rows, width)

        gwx = conv_back(dxs_buf[...], xs_ds, xs_ext, wx_ref, carry_x, dxs_ref, 1024)
        gwb = conv_back(dbm_buf[...], b_ds, b_ext, wb_ref, carry_b, db_ref, 256)
        gwc = conv_back(dcm_buf[...], c_ds, c_ext, wc_ref, carry_c, dc_ref, 256)

        @pl.when(start)
        def _():
            gwx_ref[...] = gwx
            gwb_ref[...] = gwb
            gwc_ref[...] = gwc
            gsp_ref[...] = gsp
            gnw_ref[...] = gnw

        @pl.when(step > 0)
        def _():
            gwx_ref[...] += gwx
            gwb_ref[...] += gwb
            gwc_ref[...] += gwc
            gsp_ref[...] += gsp
            gnw_ref[...] += gnw

    def ch(c):
        return nc - 1 - c

    row = pl.BlockSpec((L, 1024), lambda c: (ch(c), 0))
    row256 = pl.BlockSpec((L, 256), lambda c: (ch(c), 0))
    in_specs = _ssd_in_specs(rev_nc=nc) + [row, pl.BlockSpec((1, N_PAIRS, 128, 128), lambda c: (ch(c), 0, 0, 0)), row]
    out_specs = [row, row, row256, row256, pl.BlockSpec((L, 128), lambda c: (ch(c), 0)),
                 pl.BlockSpec((8, 1024), lambda c: (0, 0)), pl.BlockSpec((8, 256), lambda c: (0, 0)),
                 pl.BlockSpec((8, 256), lambda c: (0, 0)), pl.BlockSpec((8, 128), lambda c: (0, 0)),
                 pl.BlockSpec((1, 1024), lambda c: (0, 0))]
    out_shape = [jax.ShapeDtypeStruct((s, 1024), BF16), jax.ShapeDtypeStruct((s, 1024), BF16),
                 jax.ShapeDtypeStruct((s, 256), BF16), jax.ShapeDtypeStruct((s, 256), BF16),
                 jax.ShapeDtypeStruct((s, 128), F32),
                 jax.ShapeDtypeStruct((8, 1024), F32), jax.ShapeDtypeStruct((8, 256), F32),
                 jax.ShapeDtypeStruct((8, 256), F32), jax.ShapeDtypeStruct((8, 128), F32),
                 jax.ShapeDtypeStruct((1, 1024), F32)]
    scratch = [pltpu.VMEM((N_PAIRS, 128, 128), F32), pltpu.VMEM((8, 1024), F32), pltpu.VMEM((8, 256), F32),
               pltpu.VMEM((8, 256), F32), pltpu.VMEM((L, 1024), F32), pltpu.VMEM((L, 256), F32), pltpu.VMEM((L, 256), F32)]
    return pl.pallas_call(
        body, name="ssd_bwd", grid=(nc,), in_specs=in_specs, out_specs=out_specs, out_shape=out_shape,
        scratch_shapes=scratch, compiler_params=_params(("arbitrary",)),
    )(proj, proj, proj, proj, proj, proj, proj, conv_w8, conv_w8, conv_w8, conv_b, conv_b, conv_b, small, smallp, norm_w,
      ypre, states, dy)


FOX_SCALE = HEAD_DIM ** -0.5
FOX_T = 256
Q_COL, K_COL, V_COL = 2, 3, 4


def _split3_dot(v, m):
    hi = v.astype(BF16)
    r1 = v - hi.astype(F32)
    mid = r1.astype(BF16)
    lo = (r1 - mid.astype(F32)).astype(BF16)
    return _dot(hi, m) + _dot(mid, m) + _dot(lo, m)


def _head_rstd(x, sel_ref, selt_ref):
    ms = _split3_dot(x * x, sel_ref[...]) * (1.0 / HEAD_DIM)
    return _split3_dot(lax.rsqrt(ms + NORM_EPS), selt_ref[...])


def fox_tables():
    r = jnp.arange(3 * 128)
    piece, lane = r // 128, r % 128
    head = lane - F_LANE
    is_head = jnp.logical_and(head >= 0, head < N_HEADS)
    col = 128 * (head // 2) + HEAD_DIM * (1 - head % 2) + piece
    cols = jnp.arange(1024)
    place_q = jnp.logical_and(is_head[:, None], cols[None, :] == col[:, None]).astype(BF16)
    place_k = jnp.logical_and(is_head[:, None], cols[None, :] == (col + 3)[:, None]).astype(BF16)
    ones_q = jnp.logical_and(cols % HEAD_DIM >= 3, cols % HEAD_DIM < 6).astype(F32)[None]
    ones_k = (cols % HEAD_DIM < 3).astype(F32)[None]
    h = jnp.arange(128) - F_LANE
    ok = jnp.logical_and(h >= 0, h < N_HEADS)
    same_pair = cols[:, None] // 128 == (h // 2)[None, :]
    fold_even = jnp.logical_and(jnp.logical_and(ok, h % 2 == 0)[None, :], same_pair).astype(BF16)
    fold_odd = jnp.logical_and(jnp.logical_and(ok, h % 2 == 1)[None, :], same_pair).astype(BF16)
    return place_q, place_k, ones_q, ones_k, fold_even, fold_odd


def fox_prep(proj, small, smallp, qw, kw, sel, selt, place_q, place_k, ones_q, ones_k, *, tm=256):
    s = proj.shape[0]

    def body(q_ref, k_ref, v_ref, small_ref, sp_ref, qw_ref, kw_ref, sel_ref, selt_ref, pq_ref, pk_ref, oq_ref, ok_ref,
             qn_ref, kn_ref, aq_ref, ak_ref, vb_ref, knt_ref, akt_ref, vt_ref, carry):
        @pl.when(pl.program_id(0) == 0)
        def _():
            carry[...] = jnp.zeros_like(carry)

        q = q_ref[...]
        qn_ref[...] = (((q * _head_rstd(q, sel_ref, selt_ref)) * qw_ref[...]) * FOX_SCALE).astype(BF16)
        k = k_ref[...]
        kn = ((k * _head_rstd(k, sel_ref, selt_ref)) * kw_ref[...]).astype(BF16)
        kn_ref[...] = kn
        knt_ref[...] = kn.astype(F32).T.astype(BF16)
        vb_ref[...] = v_ref[...].astype(BF16)
        vt_ref[...] = v_ref[...].T.astype(BF16)
        li = _lane_iota((tm, 128))
        f_lane = jnp.logical_and(li >= F_LANE, li < F_LANE + N_HEADS)
        logf = jnp.where(f_lane, -_softplus(-(small_ref[...] + sp_ref[3:4, :])), 0.0)
        cum = _cumsum_rows(logf) + carry[...]
        carry[...] = cum[tm - 1:tm, :]
        hi = cum.astype(BF16)
        r1 = cum - hi.astype(F32)
        mid = r1.astype(BF16)
        lo = (r1 - mid.astype(F32)).astype(BF16)
        pieces = jnp.concatenate([hi, mid, lo], axis=1)
        aq_ref[...] = (_dot(pieces, pq_ref[...]) + oq_ref[...]).astype(BF16)
        ak = ok_ref[...] - _dot(pieces, pk_ref[...])
        ak_ref[...] = ak.astype(BF16)
        akt_ref[...] = ak.T.astype(BF16)

    row = pl.BlockSpec((tm, 1024), lambda i: (i, 0))
    col = pl.BlockSpec((1024, tm), lambda i: (0, i))
    vec = pl.BlockSpec((1, 1024), lambda i: (0, 0))
    table = pl.BlockSpec((384, 1024), lambda i: (0, 0))
    wide = jax.ShapeDtypeStruct((s, 1024), BF16)
    tall = jax.ShapeDtypeStruct((1024, s), BF16)
    return pl.pallas_call(
        body, name="fox_prep", grid=(s // tm,),
        in_specs=[pl.BlockSpec((tm, 1024), lambda i: (i, Q_COL)), pl.BlockSpec((tm, 1024), lambda i: (i, K_COL)),
                  pl.BlockSpec((tm, 1024), lambda i: (i, V_COL)),
                  pl.BlockSpec((tm, 128), lambda i: (i, 0)), pl.BlockSpec((8, 128), lambda i: (0, 0)), vec, vec,
                  pl.BlockSpec((1024, 128), lambda i: (0, 0)), pl.BlockSpec((128, 1024), lambda i: (0, 0)),
                  table, table, vec, vec],
        out_specs=[row, row, row, row, row, col, col, col],
        out_shape=[wide, wide, wide, wide, wide, tall, tall, tall],
        scratch_shapes=[pltpu.VMEM((1, 128), F32)], compiler_params=_params(("arbitrary",)),
    )(proj, proj, proj, small, smallp, qw, kw, sel, selt, place_q, place_k, ones_q, ones_k)


def fox_fwd(qn, kn, aq, ak, vt, shards=()):
    s = qn.shape[0]
    t = FOX_T
    nq = s // t
    ng = len(shards)

    def body(*refs):
        q_ref, k_ref, aq_ref, ak_ref, vt_ref = refs[:5]
        o_ref, ot_ref, lse_ref = refs[5 + ng:8 + ng]
        p = pl.program_id(0)
        if ng:
            start, forward, finish = _gather_phases(refs[5:5 + ng], refs[8 + ng:8 + 2 * ng], *refs[8 + 2 * ng:])
            pl.when(p == 0)(start)
            pl.when(p == N_PAIRS // 2)(forward)

        @pl.when(p == 0)
        def _():
            lse_ref[...] = jnp.zeros_like(lse_ref)

        lo = _lane_iota((t, 128)) < HEAD_DIM
        lo_rows = _row_iota((128, t)) < HEAD_DIM
        causal_t = _lane_iota((t, t)) >= _row_iota((t, t))

        def q_loop(qi, _):
            q0 = pl.multiple_of(qi * t, t)
            qv, aqv = q_ref[pl.ds(q0, t), :], aq_ref[pl.ds(q0, t), :]
            qa, qb = jnp.where(lo, qv, aqv), jnp.where(lo, aqv, qv)

            def scores(kj):
                k0 = pl.multiple_of(kj * t, t)
                kv, akv = k_ref[pl.ds(k0, t), :], ak_ref[pl.ds(k0, t), :]
                return _dot(jnp.where(lo, kv, akv), qa, "nt"), _dot(jnp.where(lo, akv, kv), qb, "nt")

            def update(kj, stats, s0, s1):
                m0, l0, m1, l1, acc = stats
                vtv = vt_ref[:, pl.ds(pl.multiple_of(kj * t, t), t)]
                n0 = jnp.maximum(m0, jnp.max(s0, axis=0, keepdims=True))
                n1 = jnp.maximum(m1, jnp.max(s1, axis=0, keepdims=True))
                a0, a1 = jnp.exp(m0 - n0), jnp.exp(m1 - n1)
                p0, p1 = jnp.exp(s0 - n0), jnp.exp(s1 - n1)
                l0 = a0 * l0 + jnp.sum(p0, axis=0, keepdims=True)
                l1 = a1 * l1 + jnp.sum(p1, axis=0, keepdims=True)
                acc = (jnp.where(lo_rows, a0, a1) * acc + _dot(jnp.where(lo_rows, vtv, 0.0), p0)
                       + _dot(jnp.where(lo_rows, 0.0, vtv), p1))
                return n0, l0, n1, l1, acc

            def step(kj, carry):
                stats, (s0, s1) = carry[:5], carry[5:]
                nxt = scores(kj + 1)
                return (*update(kj, stats, s0, s1), *nxt)

            def row(val):
                return jnp.full((1, t), val, F32)

            init = (row(NEG_BIG), row(0.0), row(NEG_BIG), row(0.0), jnp.zeros((128, t), F32), *scores(0))
            carry = lax.fori_loop(0, qi, step, init)
            s0, s1 = jnp.where(causal_t, carry[5], NEG_BIG), jnp.where(causal_t, carry[6], NEG_BIG)
            m0, l0, m1, l1, acc = update(qi, carry[:5], s0, s1)
            out_t = acc / jnp.where(lo_rows, l0, l1)
            ot_ref[:, pl.ds(q0, t)] = out_t.astype(BF16)
            o_ref[pl.ds(q0, t), :] = out_t.T.astype(BF16)
            ri = _row_iota((N_HEADS, t))
            old = lse_ref[:, pl.ds(q0, t)]
            lse_ref[:, pl.ds(q0, t)] = jnp.where(
                ri == 2 * p, m0 + jnp.log(l0), jnp.where(ri == 2 * p + 1, m1 + jnp.log(l1), old))
            return 0

        lax.fori_loop(0, nq, q_loop, 0)
        if ng:
            pl.when(p == N_PAIRS - 1)(finish)

    pair = pl.BlockSpec((s, 128), lambda p: (0, p))
    outs = pl.pallas_call(
        body, name="fox_fwd", grid=(N_PAIRS,),
        in_specs=[pair] * 4 + [pl.BlockSpec((128, s), lambda p: (p, 0))] + [ANY] * ng,
        out_specs=[pair, pl.BlockSpec((128, s), lambda p: (p, 0)), pl.BlockSpec((N_HEADS, s), lambda p: (0, 0))] + [ANY] * ng,
        out_shape=[jax.ShapeDtypeStruct((s, 1024), BF16), jax.ShapeDtypeStruct((1024, s), BF16),
                   jax.ShapeDtypeStruct((N_HEADS, s), F32)] + _gather_out_shapes(shards),
        scratch_shapes=_gather_scratch(ng) if ng else [],
        compiler_params=_params(("arbitrary",)),
    )(qn, kn, aq, ak, vt, *shards)
    return outs[0], outs[1], outs[2], _place_own_blocks(outs[3:], shards)


def fox_bwd(qn, kn, aq, ak, knt, akt, vb, lse, dmixed, parts=()):
    s = qn.shape[0]
    t = FOX_T
    nq = s // t
    once = pl.Buffered(1)
    ns = len(parts)

    def body(*refs):
        q_ref, k_ref, aq_ref, ak_ref, kt_ref, akt_ref, v_ref, lse_ref, do_ref = refs[:9]
        dq_ref, dk_ref, dv_ref, dc0_ref, dc1_ref = refs[9 + ns:14 + ns]
        p_scr, dp_scr = refs[14 + 2 * ns:16 + 2 * ns]
        p = pl.program_id(0)
        if ns:
            start, finish = _scatter_phases(refs[9:9 + ns], refs[14 + ns:14 + 2 * ns], *refs[16 + 2 * ns:])
            pl.when(p == 0)(start)
        dk_ref[...] = jnp.zeros_like(dk_ref)
        dv_ref[...] = jnp.zeros_like(dv_ref)
        dc0_ref[...] = jnp.zeros_like(dc0_ref)
        dc1_ref[...] = jnp.zeros_like(dc1_ref)
        lo = _lane_iota((t, 128)) < HEAD_DIM
        lo_rows = _row_iota((128, t)) < HEAD_DIM
        causal_t = _lane_iota((t, t)) >= _row_iota((t, t))

        def q_loop(qi, _):
            q0 = pl.multiple_of(qi * t, t)
            qv, aqv = q_ref[pl.ds(q0, t), :], aq_ref[pl.ds(q0, t), :]
            qa, qb = jnp.where(lo, qv, aqv), jnp.where(lo, aqv, qv)
            do = do_ref[pl.ds(q0, t), :]
            doa, dob = jnp.where(lo, do, 0.0).astype(BF16), jnp.where(lo, 0.0, do).astype(BF16)
            lse_blk = lse_ref[:, pl.ds(q0, t)]
            ri = _row_iota((N_HEADS, t))
            lse0 = jnp.sum(jnp.where(ri == 2 * p, lse_blk, 0.0), axis=0, keepdims=True)
            lse1 = jnp.sum(jnp.where(ri == 2 * p + 1, lse_blk, 0.0), axis=0, keepdims=True)

            def scores(kj):
                k0 = pl.multiple_of(kj * t, t)
                kv, akv = k_ref[pl.ds(k0, t), :], ak_ref[pl.ds(k0, t), :]
                return _dot(jnp.where(lo, kv, akv), qa, "nt"), _dot(jnp.where(lo, akv, kv), qb, "nt")

            def pass1(kj, d0, d1, diagonal):
                k0 = pl.multiple_of(kj * t, t)
                vv = v_ref[pl.ds(k0, t), :]
                s0, s1 = scores(kj)
                if diagonal:
                    s0, s1 = jnp.where(causal_t, s0, NEG_BIG), jnp.where(causal_t, s1, NEG_BIG)
                p0, p1 = jnp.exp(s0 - lse0), jnp.exp(s1 - lse1)
                dp0, dp1 = _dot(vv, doa, "nt"), _dot(vv, dob, "nt")
                p_scr[0, kj], p_scr[1, kj] = p0, p1
                dp_scr[0, kj], dp_scr[1, kj] = dp0, dp1
                dv_ref[pl.ds(k0, t), :] += _dot(p0, doa) + _dot(p1, dob)
                return d0 + jnp.sum(p0 * dp0, axis=0, keepdims=True), d1 + jnp.sum(p1 * dp1, axis=0, keepdims=True)

            zero = jnp.zeros((1, t), F32)
            d0, d1 = lax.fori_loop(0, qi, lambda kj, c: pass1(kj, *c, False), (zero, zero))
            d0, d1 = pass1(qi, d0, d1, True)

            def pass2(kj, carry):
                dq0, dq1 = carry
                k0 = pl.multiple_of(kj * t, t)
                p0, p1 = p_scr[0, kj], p_scr[1, kj]
                ds0, ds1 = p0 * (dp_scr[0, kj] - d0), p1 * (dp_scr[1, kj] - d1)
                dk_ref[pl.ds(k0, t), :] += jnp.where(lo, _dot(ds0, qa), _dot(ds1, qb))
                dc0_ref[pl.ds(k0, t), :] += ds0[:, :128] + ds0[:, 128:]
                dc1_ref[pl.ds(k0, t), :] += ds1[:, :128] + ds1[:, 128:]
                ktv, aktv = kt_ref[:, pl.ds(k0, t)], akt_ref[:, pl.ds(k0, t)]
                return dq0 + _dot(jnp.where(lo_rows, ktv, aktv), ds0), dq1 + _dot(jnp.where(lo_rows, aktv, ktv), ds1)

            zq = jnp.zeros((128, t), F32)
            dq0, dq1 = lax.fori_loop(0, qi + 1, pass2, (zq, zq))
            dq_ref[pl.ds(q0, t), :] = jnp.where(lo_rows, dq0, dq1).T
            return 0

        lax.fori_loop(0, nq, q_loop, 0)
        if ns:
            pl.when(p == N_PAIRS - 1)(finish)

    pair = pl.BlockSpec((s, 128), lambda p: (0, p), pipeline_mode=once)
    pair_t = pl.BlockSpec((128, s), lambda p: (p, 0), pipeline_mode=once)
    out = jax.ShapeDtypeStruct((s, 1024), F32)
    outs = pl.pallas_call(
        body, name="fox_bwd", grid=(N_PAIRS,),
        in_specs=[pair, pair, pair, pair, pair_t, pair_t, pair, pl.BlockSpec((N_HEADS, s), lambda p: (0, 0)),
                  pl.BlockSpec((s, 128), lambda p: (0, 8 + p), pipeline_mode=once)] + [ANY] * ns,
        out_specs=[pair] * 5 + [ANY] * ns,
        out_shape=[out] * 5 + [jax.ShapeDtypeStruct(p.shape, p.dtype) for p in parts],
        scratch_shapes=[pltpu.VMEM((2, nq, t, t), F32), pltpu.VMEM((2, nq, t, t), F32)] + (_scatter_scratch(ns) if ns else []),
        compiler_params=_params(("arbitrary",)),
    )(qn, kn, aq, ak, knt, akt, vb, lse, dmixed, *parts)
    return (*outs[:5], _keep_own_blocks(outs[5:], parts))


def fox_post(dqn, dkn, dc0, dc1, proj, small, smallp, qw, kw, sel, selt, fold_even, fold_odd, *, tm=256):
    s = proj.shape[0]
    nrow = s // tm

    def body(dqn_ref, dkn_ref, dc0_ref, dc1_ref, q_ref, k_ref, small_ref, sp_ref, qw_ref, kw_ref, sel_ref, selt_ref,
             fe_ref, fo_ref, dq_ref, dk_ref, dsmall_ref, gqw_ref, gkw_ref, gfb_ref, carry):
        step = pl.program_id(0)

        @pl.when(step == 0)
        def _():
            carry[...] = jnp.zeros_like(carry)

        def norm_bwd(x_ref, w_ref, dn, out_ref):
            x = x_ref[...]
            rf = _head_rstd(x, sel_ref, selt_ref)
            xh = x * rf
            g = dn * w_ref[...]
            mean_gx = _split3_dot(_split3_dot(g * xh, sel_ref[...]) * (1.0 / HEAD_DIM), selt_ref[...])
            out_ref[...] = (rf * (g - xh * mean_gx)).astype(BF16)
            return jnp.sum(dn * xh, axis=0, keepdims=True)

        gqw = norm_bwd(q_ref, qw_ref, dqn_ref[...] * FOX_SCALE, dq_ref)
        gkw = norm_bwd(k_ref, kw_ref, dkn_ref[...], dk_ref)
        li = _lane_iota((tm, 128))
        f_lane = jnp.logical_and(li >= F_LANE, li < F_LANE + N_HEADS)
        dcum = -(_split3_dot(dc0_ref[...], fe_ref[...]) + _split3_dot(dc1_ref[...], fo_ref[...]))
        dlogf = _rev_cumsum_rows(dcum) + carry[...]
        carry[...] = dlogf[0:1, :]
        dfr = jnp.where(f_lane, dlogf * _sigmoid(-(small_ref[...] + sp_ref[3:4, :])), 0.0)
        dsmall_ref[...] = dfr
        gfb = jnp.sum(dfr, axis=0, keepdims=True)

        @pl.when(step == 0)
        def _():
            gqw_ref[...] = gqw
            gkw_ref[...] = gkw
            gfb_ref[...] = gfb

        @pl.when(step > 0)
        def _():
            gqw_ref[...] += gqw
            gkw_ref[...] += gkw
            gfb_ref[...] += gfb

    def rb(i):
        return nrow - 1 - i

    row = pl.BlockSpec((tm, 1024), lambda i: (rb(i), 0))
    vec = pl.BlockSpec((1, 1024), lambda i: (0, 0))
    fold = pl.BlockSpec((1024, 128), lambda i: (0, 0))
    return pl.pallas_call(
        body, name="fox_post", grid=(nrow,),
        in_specs=[row, row, row, row, pl.BlockSpec((tm, 1024), lambda i: (rb(i), Q_COL)),
                  pl.BlockSpec((tm, 1024), lambda i: (rb(i), K_COL)),
                  pl.BlockSpec((tm, 128), lambda i: (rb(i), 0)), pl.BlockSpec((8, 128), lambda i: (0, 0)), vec, vec,
                  fold, pl.BlockSpec((128, 1024), lambda i: (0, 0)), fold, fold],
        out_specs=[row, row, pl.BlockSpec((tm, 128), lambda i: (rb(i), 0)), vec, vec, pl.BlockSpec((1, 128), lambda i: (0, 0))],
        out_shape=[jax.ShapeDtypeStruct((s, 1024), BF16), jax.ShapeDtypeStruct((s, 1024), BF16),
                   jax.ShapeDtypeStruct((s, 128), F32), jax.ShapeDtypeStruct((1, 1024), F32),
                   jax.ShapeDtypeStruct((1, 1024), F32), jax.ShapeDtypeStruct((1, 128), F32)],
        scratch_shapes=[pltpu.VMEM((1, 128), F32)], compiler_params=_params(("arbitrary",)),
    )(dqn, dkn, dc0, dc1, proj, proj, small, smallp, qw, kw, sel, selt, fold_even, fold_odd)


def local_step(x, target, wm, ws, later_shards, ssd_cw8, ssd_cb, smallp, ssd_nw, qw_t, kw_t, sel, selt,
               norm_mix_w, norm_ffn_w, ffn_cw8, ffn_cb):
    h, h_t = rms_fwd(x, norm_mix_w, name="rms_mix_fwd")
    proj = matmul(h, wm, mode="nn", tm=1024, tn=1408, tk=1024, out_dtype=F32, name="mm_in_proj")
    small = matmul(h, ws, mode="nn", tm=1024, tn=128, tk=1024, out_dtype=F32, name="mm_in_proj_small")
    y_ssd, y_ssd_t, ypre, states = ssd_fwd(proj, small, ssd_cw8, ssd_cb, smallp, ssd_nw)
    place_q, place_k, ones_q, ones_k, fold_even, fold_odd = fox_tables()
    qn, kn, aq, ak, vb, knt, akt, vt = fox_prep(proj, small, smallp, qw_t, kw_t, sel, selt, place_q, place_k, ones_q, ones_k)
    y_fox, y_fox_t, lse, (a_out, a_up, a_down) = fox_fwd(qn, kn, aq, ak, vt, shards=later_shards)
    w_out = a_out.reshape(2048, D_MODEL)
    w_up = a_up.transpose(1, 0, 2).reshape(D_MODEL, 2 * D_FF)
    w_down = a_down.reshape(D_FF, D_MODEL)
    x1 = matmul(y_ssd, w_out, mode="nn", tm=1024, tn=1024, tk=1024, out_dtype=F32, name="mm_out_ssd", add=x)
    x1 = matmul(y_fox, w_out, mode="nn", tm=1024, tn=1024, tk=1024, out_dtype=F32, name="mm_out_fox", add=x1, b_koff=1)
    hf, hf_t = rms_fwd(x1, norm_ffn_w, name="rms_ffn_fwd")
    hu = matmul(hf, w_up, mode="nn", tm=1024, tn=1408, tk=1024, out_dtype=F32, name="mm_up")
    act, act_t = ffn_mid_fwd(hu, ffn_cw8, ffn_cb)
    y = matmul(act, w_down, mode="nn", tm=1024, tn=1024, tk=1408, out_dtype=F32, name="mm_down", add=x1)
    dy, sq = loss_head(y, target)

    dact = matmul(dy, w_down, mode="nt", tm=1024, tn=1408, tk=1024, out_dtype=F32, name="mm_dact")
    g_down = matmul(act_t, dy, mode="nn", tm=1408, tn=1024, tk=1024, out_dtype=BF16, name="mm_dw_down")
    dhu_g, dhu_v, gcw_g, gcw_v = ffn_mid_bwd(hu, dact, ffn_cw8, ffn_cb)
    dhf = matmul(dhu_g, w_up, mode="nt", tm=1024, tn=1024, tk=1408, out_dtype=F32, name="mm_dhf_gate")
    dhf = matmul(dhu_v, w_up, mode="nt", tm=1024, tn=1024, tk=1408, out_dtype=F32, name="mm_dhf_val", add=dhf, b_koff=2)
    g_up_g = matmul(hf_t, dhu_g, mode="nn", tm=1024, tn=1408, tk=1024, out_dtype=BF16, name="mm_dw_up_gate")
    g_up_v = matmul(hf_t, dhu_v, mode="nn", tm=1024, tn=1408, tk=1024, out_dtype=BF16, name="mm_dw_up_val")
    dx1, g_norm_ffn = rms_bwd(dhf, x1, norm_ffn_w, dy, name="rms_ffn_bwd")
    dmixed = matmul(dx1, w_out, mode="nt", tm=1024, tn=1024, tk=1024, out_dtype=F32, name="mm_dmixed")
    g_out_a = matmul(y_ssd_t, dx1, mode="nn", tm=1024, tn=1024, tk=1024, out_dtype=BF16, name="mm_dw_out_ssd")
    g_out_b = matmul(y_fox_t, dx1, mode="nn", tm=1024, tn=1024, tk=1024, out_dtype=BF16, name="mm_dw_out_fox")
    early = [jnp.concatenate([g_out_a, g_out_b], axis=0).reshape(4, 512, D_MODEL),
             jnp.concatenate([g_up_g, g_up_v], axis=1).reshape(D_MODEL, 4, 1408).transpose(1, 0, 2),
             g_down.reshape(4, 704, D_MODEL)]
    mine, theirs = pair_swap_halves(early, name="pair_swap_early")
    parts = [add_pair(a, b, name="add_pair_" + n, tr=ADAM_ROWS[n]) for a, b, n in zip(mine, theirs, BIG_NAMES[1:])]
    dz, dxs, db, dc, dsmall_ssd, gcw_x, gcw_b, gcw_c, g_sp, g_ssd_nw = ssd_bwd(
        proj, small, ssd_cw8, ssd_cb, smallp, ssd_nw, ypre, states, dmixed)
    dqn, dkn, dv, dc0, dc1, landed_early = fox_bwd(qn, kn, aq, ak, knt, akt, vb, lse, dmixed, parts=parts)
    dq, dk, dsmall_fox, g_qw, g_kw, g_fb = fox_post(dqn, dkn, dc0, dc1, proj, small, smallp, qw_t, kw_t, sel, selt,
                                                    fold_even, fold_odd)
    dproj = jnp.concatenate([dz, dxs, dq, dk, dv.astype(BF16), db, dc], axis=1)
    dsmall = (dsmall_ssd + dsmall_fox).astype(BF16)
    dh = matmul(dproj, wm, mode="nt", tm=1024, tn=1024, tk=1408, out_dtype=F32, name="mm_dh")
    dh = matmul(dsmall, ws, mode="nt", tm=1024, tn=1024, tk=128, out_dtype=F32, name="mm_dh_small", add=dh)
    g_wm = matmul(h_t, dproj, mode="nn", tm=1024, tn=1408, tk=1024, out_dtype=BF16, name="mm_dw_in")
    g_ws = matmul(h_t, dsmall, mode="nn", tm=1024, tn=128, tk=1024, out_dtype=BF16, name="mm_dw_in_small")
    grad_x, g_norm_mix = rms_bwd(dh, x, norm_mix_w, dx1, name="rms_mix_bwd")
    return dict(
        sq=sq, grad_x=grad_x, g_wm=g_wm, g_ws=g_ws, landed_early=landed_early,
        g_norm_mix=g_norm_mix, g_norm_ffn=g_norm_ffn, g_ssd_nw=g_ssd_nw,
        g_ssd_cw=jnp.concatenate([gcw_x, gcw_b, gcw_c], axis=1), g_sp=g_sp, g_fb=g_fb, g_qw=g_qw, g_kw=g_kw,
        g_ffn_cw=jnp.concatenate([gcw_g, gcw_v], axis=1))


def adamw(w, g, m, v, *, name, tr):
    rows, cols = w.shape

    def body(w_ref, g_ref, m_ref, v_ref, d_ref, mo_ref, vo_ref):
        gv = g_ref[...]
        mn = ADAM_B1 * m_ref[...] + (1.0 - ADAM_B1) * gv
        vn = ADAM_B2 * v_ref[...] + (1.0 - ADAM_B2) * (gv * gv)
        m_hat = mn / (1.0 - ADAM_B1 ** ADAM_STEP)
        v_hat = vn / (1.0 - ADAM_B2 ** ADAM_STEP)
        d_ref[...] = -ADAM_LR * (m_hat / (jnp.sqrt(v_hat) + ADAM_EPS) + ADAM_WD * w_ref[...])
        mo_ref[...] = mn
        vo_ref[...] = vn

    blk = pl.BlockSpec((tr, cols), lambda i: (i, 0))
    shp = jax.ShapeDtypeStruct((rows, cols), F32)
    return pl.pallas_call(
        body, name=name, grid=(rows // tr,), in_specs=[blk] * 4, out_specs=[blk] * 3, out_shape=[shp] * 3,
        compiler_params=_params(("parallel",)),
    )(w, g, m, v)


def add_pair(a, b, *, name, tr):
    _, rows, cols = a.shape

    def body(a_ref, b_ref, o_ref):
        o_ref[...] = (a_ref[...].astype(F32) + b_ref[...].astype(F32)).astype(BF16)

    blk = pl.BlockSpec((1, tr, cols), lambda j, i: (j, i, 0))
    return pl.pallas_call(
        body, name=name, grid=(4, rows // tr), in_specs=[blk, blk], out_specs=blk,
        out_shape=jax.ShapeDtypeStruct(a.shape, BF16), compiler_params=_params(("parallel", "parallel")),
    )(a, b)


def sum_chips(parts, *, name, tr):
    _, rows, cols = parts.shape

    def body(p_ref, o_ref):
        acc = p_ref[0].astype(F32)
        for k in range(1, 4):
            acc = acc + p_ref[k].astype(F32)
        o_ref[...] = acc

    return pl.pallas_call(
        body, name=name, grid=(rows // tr,), in_specs=[pl.BlockSpec((4, tr, cols), lambda i: (0, i, 0))],
        out_specs=pl.BlockSpec((tr, cols), lambda i: (i, 0)), out_shape=jax.ShapeDtypeStruct((rows, cols), F32),
        compiler_params=_params(("parallel",)),
    )(parts)


ANY = pl.BlockSpec(memory_space=pl.ANY)


def _place():
    x, y, c = lax.axis_index("x"), lax.axis_index("y"), lax.axis_index("c")
    chips = [(1 - x, y), (x, 1 - y), (1 - x, 1 - y)]
    return x, y, c, chips


def _chunks(rows):
    size = next((c for c in (128, 176, 64, 32, 16, 8) if rows % c == 0), rows)
    return [(r, size) for r in range(0, rows, size)]


def gather_weights(shards):
    n = len(shards)

    def body(*refs):
        start, forward, finish = _gather_phases(refs[:n], refs[n:2 * n], *refs[2 * n:])
        start()
        forward()
        finish()

    gathered = pl.pallas_call(
        body, name="gather_weights", in_specs=[ANY] * n, out_specs=[ANY] * n,
        out_shape=_gather_out_shapes(shards), scratch_shapes=_gather_scratch(n),
    )(*shards)
    return _place_own_blocks(gathered, shards)


def _gather_out_shapes(shards):
    return [jax.ShapeDtypeStruct((4,) + s.shape, s.dtype) for s in shards]


def _gather_scratch(n):
    return [pltpu.SemaphoreType.DMA((n, 6)), pltpu.SemaphoreType.DMA((n, 6))]


def _place_own_blocks(gathered, shards):
    if not shards:
        return []
    chip = 2 * lax.axis_index("x") + lax.axis_index("y")
    return [lax.dynamic_update_slice(g, s[None], (chip, 0, 0)) for g, s in zip(gathered, shards)]


def _gather_phases(ins, outs, send_sems, recv_sems):
    n = len(ins)
    x, y, c, chips = _place()
    me = 2 * x + y
    sibling = (x, y, 1 - c)
    blks = [2 * cx + cy for cx, cy in chips]

    def half(a, blk, r=0, nr=None):
        rows = ins[a].shape[0] // 2
        return outs[a].at[blk, pl.ds(c * rows + r, rows if nr is None else nr), :]

    def to_chip(a, t, r=0, nr=None):
        rows = ins[a].shape[0] // 2
        return pltpu.make_async_remote_copy(
            src_ref=ins[a].at[pl.ds(c * rows + r, rows if nr is None else nr), :], dst_ref=half(a, me, r, nr),
            send_sem=send_sems.at[a, t], recv_sem=recv_sems.at[a, t], device_id=(*chips[t], c), device_id_type=MESH)

    def from_chip(a, t):
        return pltpu.make_async_remote_copy(
            src_ref=half(a, blks[t]), dst_ref=half(a, blks[t]), send_sem=send_sems.at[a, t], recv_sem=recv_sems.at[a, t],
            device_id=(*chips[t], c), device_id_type=MESH)

    def to_sibling(a, t, r=0, nr=None):
        return pltpu.make_async_remote_copy(
            src_ref=half(a, blks[t], r, nr), dst_ref=half(a, blks[t], r, nr), send_sem=send_sems.at[a, 3 + t],
            recv_sem=recv_sems.at[a, 3 + t], device_id=sibling, device_id_type=MESH)

    def from_sibling(a, t):
        rows = ins[a].shape[0] // 2
        dst = outs[a].at[blks[t], pl.ds((1 - c) * rows, rows), :]
        return pltpu.make_async_remote_copy(
            src_ref=dst, dst_ref=dst, send_sem=send_sems.at[a, 3 + t], recv_sem=recv_sems.at[a, 3 + t],
            device_id=sibling, device_id_type=MESH)

    def start():
        for a in range(n):
            for t in range(3):
                for r, nr in _chunks(ins[a].shape[0] // 2):
                    to_chip(a, t, r, nr).start()

    def forward():
        for a in range(n):
            for t in range(3):
                from_chip(a, t).wait_recv()
                for r, nr in _chunks(ins[a].shape[0] // 2):
                    to_sibling(a, t, r, nr).start()

    def finish():
        for a in range(n):
            for t in range(3):
                from_sibling(a, t).wait_recv()
        for a in range(n):
            for t in range(3):
                to_chip(a, t).wait_send()
                to_sibling(a, t).wait_send()

    return start, forward, finish


def pair_swap_halves(grads, *, name):
    n = len(grads)

    def body(*refs):
        ins, theirs = refs[:n], refs[n:2 * n]
        send_sems, recv_sems = refs[2 * n:]
        x, y, c, _ = _place()
        sibling = (x, y, 1 - c)
        for a in range(n):
            rows = ins[a].shape[1] // 2
            for j in range(4):
                for r, nr in _chunks(rows):
                    pltpu.make_async_remote_copy(
                        src_ref=ins[a].at[j, pl.ds((1 - c) * rows + r, nr), :], dst_ref=theirs[a].at[j, pl.ds(r, nr), :],
                        send_sem=send_sems.at[a], recv_sem=recv_sems.at[a], device_id=sibling, device_id_type=MESH).start()
        for a in range(n):
            pltpu.make_async_remote_copy(src_ref=theirs[a], dst_ref=theirs[a], send_sem=send_sems.at[a],
                                         recv_sem=recv_sems.at[a], device_id=sibling, device_id_type=MESH).wait()

    halves = [jax.ShapeDtypeStruct((4, g.shape[1] // 2, g.shape[2]), g.dtype) for g in grads]
    theirs = pl.pallas_call(
        body, name=name, in_specs=[ANY] * n, out_specs=[ANY] * n, out_shape=halves,
        scratch_shapes=[pltpu.SemaphoreType.DMA((n,)), pltpu.SemaphoreType.DMA((n,))],
    )(*grads)
    c = lax.axis_index("c")
    mine = [lax.dynamic_slice_in_dim(g, c * (g.shape[1] // 2), g.shape[1] // 2, axis=1) for g in grads]
    return mine, theirs


def scatter_to_chips(parts):
    n = len(parts)

    def body(*refs):
        start, finish = _scatter_phases(refs[:n], refs[n:2 * n], *refs[2 * n:])
        start()
        finish()

    landed = pl.pallas_call(
        body, name="scatter_to_chips", in_specs=[ANY] * n, out_specs=[ANY] * n,
        out_shape=[jax.ShapeDtypeStruct(p.shape, p.dtype) for p in parts], scratch_shapes=_scatter_scratch(n),
    )(*parts)
    return _keep_own_blocks(landed, parts)


def _scatter_scratch(n):
    return [pltpu.SemaphoreType.DMA((n, 3)), pltpu.SemaphoreType.DMA((n, 3))]


def _keep_own_blocks(landed, parts):
    if not parts:
        return []
    chip = 2 * lax.axis_index("x") + lax.axis_index("y")
    return [lax.dynamic_update_slice(l, lax.dynamic_slice_in_dim(p, chip, 1, axis=0), (chip, 0, 0))
            for l, p in zip(landed, parts)]


def _scatter_phases(ins, outs, send_sems, recv_sems):
    n = len(ins)
    x, y, c, chips = _place()
    me = 2 * x + y
    blks = [2 * cx + cy for cx, cy in chips]

    def start():
        for a in range(n):
            for r, nr in _chunks(ins[a].shape[1]):
                for t in range(3):
                    pltpu.make_async_remote_copy(
                        src_ref=ins[a].at[blks[t], pl.ds(r, nr), :], dst_ref=outs[a].at[me, pl.ds(r, nr), :],
                        send_sem=send_sems.at[a, t], recv_sem=recv_sems.at[a, t],
                        device_id=(*chips[t], c), device_id_type=MESH).start()

    def finish():
        for a in range(n):
            for t in range(3):
                pltpu.make_async_remote_copy(
                    src_ref=outs[a].at[blks[t]], dst_ref=outs[a].at[blks[t]], send_sem=send_sems.at[a, t],
                    recv_sem=recv_sems.at[a, t], device_id=(*chips[t], c), device_id_type=MESH).wait()

    return start, finish


def pair_join_halves(halves):
    n = len(halves)

    def body(*refs):
        ins, outs = refs[:n], refs[n:2 * n]
        send_sems, recv_sems = refs[2 * n:]
        x, y, c, _ = _place()
        sibling = (x, y, 1 - c)
        for a in range(n):
            rows = ins[a].shape[0]
            for r, nr in _chunks(rows):
                pltpu.make_async_remote_copy(
                    src_ref=ins[a].at[pl.ds(r, nr), :], dst_ref=outs[a].at[pl.ds(c * rows + r, nr), :],
                    send_sem=send_sems.at[a], recv_sem=recv_sems.at[a], device_id=sibling, device_id_type=MESH).start()
        for a in range(n):
            rows = ins[a].shape[0]
            got = outs[a].at[pl.ds((1 - c) * rows, rows), :]
            pltpu.make_async_remote_copy(src_ref=ins[a], dst_ref=got, send_sem=send_sems.at[a], recv_sem=recv_sems.at[a],
                                         device_id=sibling, device_id_type=MESH).wait()

    joined = pl.pallas_call(
        body, name="pair_join_halves", in_specs=[ANY] * n, out_specs=[ANY] * n,
        out_shape=[jax.ShapeDtypeStruct((2 * h.shape[0], h.shape[1]), h.dtype) for h in halves],
        scratch_shapes=[pltpu.SemaphoreType.DMA((n,)), pltpu.SemaphoreType.DMA((n,))],
    )(*halves)
    c = lax.axis_index("c")
    return [lax.dynamic_update_slice(j, h, (c * h.shape[0], 0)) for j, h in zip(joined, halves)]


def allreduce_small(packed):
    rows = packed.shape[0]

    def body(in_ref, out_ref, gathered, send_sems, recv_sems):
        x, y, c, _ = _place()
        me = 4 * x + 2 * y + c
        gathered[me] = in_ref[...]
        flips = [(fx, fy, fc) for fx in (0, 1) for fy in (0, 1) for fc in (0, 1)][1:]
        peers = [((1 - x) if fx else x, (1 - y) if fy else y, (1 - c) if fc else c) for fx, fy, fc in flips]
        copies = []
        for t, peer in enumerate(peers):
            cp = pltpu.make_async_remote_copy(
                src_ref=in_ref, dst_ref=gathered.at[me], send_sem=send_sems.at[t], recv_sem=recv_sems.at[t],
                device_id=peer, device_id_type=MESH)
            cp.start()
            copies.append(cp)
        for t, (px, py, pc) in enumerate(peers):
            slot = gathered.at[4 * px + 2 * py + pc]
            pltpu.make_async_remote_copy(
                src_ref=slot, dst_ref=slot, send_sem=send_sems.at[t], recv_sem=recv_sems.at[t],
                device_id=(px, py, pc), device_id_type=MESH).wait_recv()
        for cp in copies:
            cp.wait_send()
        acc = gathered[0]
        for k in range(1, 8):
            acc = acc + gathered[k]
        out_ref[...] = acc

    vm = pl.BlockSpec(memory_space=pltpu.VMEM)
    return pl.pallas_call(
        body, name="allreduce_small", in_specs=[vm], out_specs=vm, out_shape=jax.ShapeDtypeStruct(packed.shape, F32),
        scratch_shapes=[pltpu.VMEM((8, rows, 128), F32), pltpu.SemaphoreType.DMA((7,)), pltpu.SemaphoreType.DMA((7,))],
    )(packed)


SMALL_NAMES = ("norm_mix_w", "ssd_conv_w", "ssd_conv_b", "ssd_dt_bias", "ssd_a_log", "ssd_d", "ssd_norm_w", "fox_f_bias",
               "fox_q_norm_w", "fox_k_norm_w", "norm_ffn_w", "ffn_conv_w", "ffn_conv_b")
BIG_NAMES = ("w_in", "w_out", "w_up", "w_down")
WEIGHT_ORDER = ("norm_mix_w", "w_in", "ssd_conv_w", "ssd_conv_b", "ssd_dt_bias", "ssd_a_log", "ssd_d", "ssd_norm_w",
                "fox_f_bias", "fox_q_norm_w", "fox_k_norm_w", "w_out", "norm_ffn_w", "w_up", "ffn_conv_w", "ffn_conv_b", "w_down")
ADAM_ROWS = {"w_in": 256, "w_out": 256, "w_up": 256, "w_down": 176}


def _pack(arrays):
    rows = []
    for a in arrays:
        flat = a.reshape(-1).astype(F32)
        rows.append(jnp.pad(flat, (0, (-flat.shape[0]) % 1024)).reshape(-1, 128))
    return jnp.concatenate(rows, axis=0)


def _unpack(packed, shapes):
    out, r = [], 0
    for shp in shapes:
        size = 1
        for d in shp:
            size *= d
        nrow = 8 * (-(-size // 1024))
        out.append(packed[r:r + nrow].reshape(-1)[:size].reshape(shp))
        r += nrow
    return out


def _pad_rows(a, rows):
    return jnp.pad(a, ((0, rows - a.shape[0]), (0, 0)))


def kernel(x, norm_mix_w, w_in, ssd_conv_w, ssd_conv_b, ssd_dt_bias, ssd_a_log, ssd_d, ssd_norm_w, fox_f_bias, fox_q_norm_w, fox_k_norm_w, w_out, norm_ffn_w, w_up, ffn_conv_w, ffn_conv_b, w_down, loss_target, m_norm_mix_w, m_w_in, m_ssd_conv_w, m_ssd_conv_b, m_ssd_dt_bias, m_ssd_a_log, m_ssd_d, m_ssd_norm_w, m_fox_f_bias, m_fox_q_norm_w, m_fox_k_norm_w, m_w_out, m_norm_ffn_w, m_w_up, m_ffn_conv_w, m_ffn_conv_b, m_w_down, v_norm_mix_w, v_w_in, v_ssd_conv_w, v_ssd_conv_b, v_ssd_dt_bias, v_ssd_a_log, v_ssd_d, v_ssd_norm_w, v_fox_f_bias, v_fox_q_norm_w, v_fox_k_norm_w, v_w_out, v_norm_ffn_w, v_w_up, v_ffn_conv_w, v_ffn_conv_b, v_w_down):
    w = dict(norm_mix_w=norm_mix_w, w_in=w_in, ssd_conv_w=ssd_conv_w, ssd_conv_b=ssd_conv_b, ssd_dt_bias=ssd_dt_bias,
             ssd_a_log=ssd_a_log, ssd_d=ssd_d, ssd_norm_w=ssd_norm_w, fox_f_bias=fox_f_bias, fox_q_norm_w=fox_q_norm_w,
             fox_k_norm_w=fox_k_norm_w, w_out=w_out, norm_ffn_w=norm_ffn_w, w_up=w_up, ffn_conv_w=ffn_conv_w,
             ffn_conv_b=ffn_conv_b, w_down=w_down)
    m = dict(norm_mix_w=m_norm_mix_w, w_in=m_w_in, ssd_conv_w=m_ssd_conv_w, ssd_conv_b=m_ssd_conv_b, ssd_dt_bias=m_ssd_dt_bias,
             ssd_a_log=m_ssd_a_log, ssd_d=m_ssd_d, ssd_norm_w=m_ssd_norm_w, fox_f_bias=m_fox_f_bias, fox_q_norm_w=m_fox_q_norm_w,
             fox_k_norm_w=m_fox_k_norm_w, w_out=m_w_out, norm_ffn_w=m_norm_ffn_w, w_up=m_w_up, ffn_conv_w=m_ffn_conv_w,
             ffn_conv_b=m_ffn_conv_b, w_down=m_w_down)
    v = dict(norm_mix_w=v_norm_mix_w, w_in=v_w_in, ssd_conv_w=v_ssd_conv_w, ssd_conv_b=v_ssd_conv_b, ssd_dt_bias=v_ssd_dt_bias,
             ssd_a_log=v_ssd_a_log, ssd_d=v_ssd_d, ssd_norm_w=v_ssd_norm_w, fox_f_bias=v_fox_f_bias, fox_q_norm_w=v_fox_q_norm_w,
             fox_k_norm_w=v_fox_k_norm_w, w_out=v_w_out, norm_ffn_w=v_norm_ffn_w, w_up=v_w_up, ffn_conv_w=v_ffn_conv_w,
             ffn_conv_b=v_ffn_conv_b, w_down=v_w_down)
    chip = 2 * lax.axis_index("x") + lax.axis_index("y")

    a_in, a_scw, a_fcw = gather_weights([w_in[0].astype(BF16), _pad_rows(ssd_conv_w[0], 16), _pad_rows(ffn_conv_w[0], 16)])
    later_shards = [w_out[0].astype(BF16), w_up[0].astype(BF16), w_down[0].astype(BF16)]
    w_full = a_in.transpose(1, 0, 2).reshape(D_MODEL, IN_COLS)
    wm = jnp.concatenate([w_full[:, :2048], w_full[:, 2576:5648], w_full[:, 2048:2560]], axis=1)
    ws = jnp.concatenate([w_full[:, 2560:2576], w_full[:, 5648:5664], jnp.zeros((D_MODEL, SMALL_COLS - 32), BF16)], axis=1)
    ssd_cw8 = a_scw.transpose(1, 0, 2).reshape(16, 1536)[:8]
    ffn_cw8 = a_fcw.transpose(1, 0, 2).reshape(16, 2 * D_FF)[:8]
    smallp = jnp.zeros((8, 128), F32)
    smallp = smallp.at[0, :16].set(ssd_dt_bias[0]).at[1, :16].set(ssd_a_log[0]).at[2, :16].set(ssd_d[0])
    smallp = smallp.at[3, F_LANE:F_LANE + 16].set(fox_f_bias[0])
    qw_t = jnp.tile(fox_q_norm_w[0], N_HEADS)[None]
    kw_t = jnp.tile(fox_k_norm_w[0], N_HEADS)[None]
    sel = (jnp.arange(1024)[:, None] // HEAD_DIM == jnp.arange(128)[None, :]).astype(BF16)

    res = local_step(x[0], loss_target[0], wm, ws, later_shards, ssd_cw8, ssd_conv_b, smallp, ssd_norm_w, qw_t, kw_t,
                     sel, sel.T, norm_mix_w, norm_ffn_w, ffn_cw8, ffn_conv_b)

    full_shapes = [(1, 1024), (1, 4, 1536), (1, 1536), (1, 16), (1, 16), (1, 16), (1, 1024), (1, 16), (1, 64), (1, 64),
                   (1, 1024), (1, 3, 2 * D_FF), (1, 2 * D_FF), (1,)]
    local_small = [res["g_norm_mix"], res["g_ssd_cw"][:4], res["g_ssd_cw"][4], res["g_sp"][0, :16], res["g_sp"][1, :16],
                   res["g_sp"][2, :16], res["g_ssd_nw"], res["g_fb"][0, F_LANE:F_LANE + 16],
                   res["g_qw"].reshape(N_HEADS, HEAD_DIM).sum(0), res["g_kw"].reshape(N_HEADS, HEAD_DIM).sum(0),
                   res["g_norm_ffn"], res["g_ffn_cw"][:3], res["g_ffn_cw"][3], jnp.sum(res["sq"])]
    summed = _unpack(allreduce_small(_pack(local_small)), full_shapes)
    loss = (0.5 / D_MODEL) * summed[-1][0]
    g_small = dict(zip(SMALL_NAMES, summed[:-1]))
    g_small["ssd_conv_w"] = lax.dynamic_slice(g_small["ssd_conv_w"], (0, 0, 384 * chip), (1, 4, 384))
    g_small["ffn_conv_w"] = lax.dynamic_slice(g_small["ffn_conv_w"], (0, 0, 1408 * chip), (1, 3, 1408))

    g_wm, g_ws = res["g_wm"], res["g_ws"]
    g_in_full = jnp.concatenate([g_wm[:, :2048], g_wm[:, 5120:5632], g_ws[:, :16], g_wm[:, 2048:5120], g_ws[:, 16:32]], axis=1)
    mine, theirs = pair_swap_halves([g_in_full.reshape(D_MODEL, 4, 1416).transpose(1, 0, 2)], name="pair_swap_w_in")
    landed_in = scatter_to_chips([add_pair(mine[0], theirs[0], name="add_pair_w_in", tr=ADAM_ROWS["w_in"])])
    landed = landed_in + res["landed_early"]
    halves = [sum_chips(p, name="sum_chips_" + n, tr=ADAM_ROWS[n]) for p, n in zip(landed, BIG_NAMES)]
    g_big = dict(zip(BIG_NAMES, pair_join_halves(halves)))

    grads, deltas, new_m, new_v = {}, {}, {}, {}
    for n in BIG_NAMES:
        d, mn, vn = adamw(w[n][0], g_big[n], m[n][0], v[n][0], name="adamw_" + n, tr=ADAM_ROWS[n])
        grads[n], deltas[n], new_m[n], new_v[n] = g_big[n][None], d[None], mn[None], vn[None]
    shapes = [w[n].shape for n in SMALL_NAMES]
    d, mn, vn = adamw(_pack([w[n] for n in SMALL_NAMES]), _pack([g_small[n] for n in SMALL_NAMES]),
                      _pack([m[n] for n in SMALL_NAMES]), _pack([v[n] for n in SMALL_NAMES]), name="adamw_small", tr=8)
    for n, dd, mm, vv in zip(SMALL_NAMES, _unpack(d, shapes), _unpack(mn, shapes), _unpack(vn, shapes)):
        grads[n], deltas[n], new_m[n], new_v[n] = g_small[n].reshape(w[n].shape), dd, mm, vv
    return (loss, res["grad_x"][None], *[grads[n] for n in WEIGHT_ORDER], *[deltas[n] for n in WEIGHT_ORDER],
            *[new_m[n] for n in WEIGHT_ORDER], *[new_v[n] for n in WEIGHT_ORDER])
```

```python
import functools

import jax
import jax.numpy as jnp
from jax import lax
from jax.experimental import pallas as pl
from jax.experimental.pallas import tpu as pltpu

F32 = jnp.float32
BF16 = jnp.bfloat16
MESH = pl.DeviceIdType.MESH

D_MODEL = 1024
HEAD_DIM = 64
N_HEADS = 16
N_PAIRS = N_HEADS // 2
SSD_CHUNK = 128
SSD_STATE = 128
SSD_CONV = 4
D_FF = 2816
FFN_CONV = 3
NORM_EPS = 1e-6
MAIN_COLS = 5632
SMALL_COLS = 128
F_LANE = 16
IN_COLS = 5664

ADAM_LR = 0.001
ADAM_B1 = 0.9
ADAM_B2 = 0.999
ADAM_EPS = 1e-08
ADAM_WD = 0.01
ADAM_STEP = 10

VMEM_LIMIT_V7X = 56 * 1024 * 1024
NEG_BIG = -1e30


def _params(sem=None):
    return pltpu.CompilerParams(dimension_semantics=sem, vmem_limit_bytes=VMEM_LIMIT_V7X)


def _sigmoid(x):
    return 1.0 / (1.0 + jnp.exp(-x))


def _silu_and_grad(x):
    s = _sigmoid(x)
    return x * s, s * (1.0 + x * (1.0 - s))


def _shift_down(v, j):
    return v if j == 0 else pltpu.roll(v, j, 0)


def _shift_up(v, j):
    return v if j == 0 else pltpu.roll(v, v.shape[0] - j, 0)


def _row_iota(shape):
    return lax.broadcasted_iota(jnp.int32, shape, 0)


def _lane_iota(shape):
    return lax.broadcasted_iota(jnp.int32, shape, 1)


def _dot(a, b, mode="nn"):
    dims = {"nn": (((1,), (0,)), ((), ())), "nt": (((1,), (1,)), ((), ())), "tn": (((0,), (0,)), ((), ()))}[mode]
    return lax.dot_general(a.astype(BF16), b.astype(BF16), dims, preferred_element_type=F32)


def _dot_f32(a, b):
    return jnp.dot(a, b, precision=lax.Precision.HIGHEST, preferred_element_type=F32)


def matmul(a, b, *, mode, tm, tn, tk, out_dtype, name, add=None, b_koff=0, scatter=()):
    (m, k), n = a.shape, (b.shape[1] if mode == "nn" else b.shape[0])
    assert m % tm == 0 and n % tn == 0 and k % tk == 0, (name, m, n, k, tm, tn, tk)
    nk = k // tk
    grid = (m // tm, n // tn, nk)
    a_spec = pl.BlockSpec((tm, tk), lambda i, j, kk: (i, kk))
    b_spec = (pl.BlockSpec((tn, tk), lambda i, j, kk: (j, kk + b_koff)) if mode == "nt"
              else pl.BlockSpec((tk, tn), lambda i, j, kk: (kk + b_koff, j)))
    o_spec = pl.BlockSpec((tm, tn), lambda i, j, kk: (i, j))
    has_add = add is not None
    n_in = 3 if has_add else 2
    ns = len(scatter)

    def body(*refs):
        a_ref, b_ref = refs[:2]
        add_ref = refs[2] if has_add else None
        o_ref, acc_ref = refs[n_in + ns], refs[n_in + 2 * ns + 1]
        kk = pl.program_id(2)
        if ns:
            step = (pl.program_id(0) * grid[1] + pl.program_id(1)) * grid[2] + kk
            start, finish_copies = _scatter_phases(refs[n_in:n_in + ns], refs[n_in + ns + 1:n_in + 2 * ns + 1],
                                                   *refs[n_in + 2 * ns + 2:])
            pl.when(step == 0)(start)
        part = _dot(a_ref[...], b_ref[...], mode)

        def finish(total):
            if has_add:
                total = total + add_ref[...]
            o_ref[...] = total.astype(out_dtype)

        if nk == 1:
            finish(part)
        else:
            @pl.when(kk == 0)
            def _():
                acc_ref[...] = part

            @pl.when(jnp.logical_and(kk > 0, kk < nk - 1))
            def _():
                acc_ref[...] += part

            @pl.when(kk == nk - 1)
            def _():
                finish(acc_ref[...] + part)

        if ns:
            pl.when(step == grid[0] * grid[1] * grid[2] - 1)(finish_copies)

    in_specs = [a_spec, b_spec] + ([o_spec] if has_add else [])
    args = (a, b) + ((add,) if has_add else ())
    acc = pltpu.VMEM((tm, tn) if nk > 1 else (8, 128), F32)
    if not ns:
        return pl.pallas_call(
            body, name=name, grid=grid, in_specs=in_specs, out_specs=o_spec, out_shape=jax.ShapeDtypeStruct((m, n), out_dtype),
            scratch_shapes=[acc], compiler_params=_params(("parallel", "parallel", "arbitrary")),
        )(*args)
    outs = pl.pallas_call(
        body, name=name, grid=grid, in_specs=in_specs + [ANY] * ns, out_specs=[o_spec] + [ANY] * ns,
        out_shape=[jax.ShapeDtypeStruct((m, n), out_dtype)] + [jax.ShapeDtypeStruct(p.shape, p.dtype) for p in scatter],
        scratch_shapes=[acc] + _scatter_scratch(ns), compiler_params=_params(("arbitrary", "arbitrary", "arbitrary")),
    )(*args, *scatter)
    return outs[0], _keep_own_blocks(outs[1:], scatter)


def rms_fwd(x, w, *, name, tm=512):
    s, d = x.shape

    def body(x_ref, w_ref, h_ref, ht_ref):
        xv = x_ref[...]
        r = lax.rsqrt(jnp.mean(xv * xv, axis=-1, keepdims=True) + NORM_EPS)
        h = (xv * r) * w_ref[...]
        h_ref[...] = h.astype(BF16)
        ht_ref[...] = h.T.astype(BF16)

    return pl.pallas_call(
        body, name=name, grid=(s // tm,),
        in_specs=[pl.BlockSpec((tm, d), lambda i: (i, 0)), pl.BlockSpec((1, d), lambda i: (0, 0))],
        out_specs=[pl.BlockSpec((tm, d), lambda i: (i, 0)), pl.BlockSpec((d, tm), lambda i: (0, i))],
        out_shape=[jax.ShapeDtypeStruct((s, d), BF16), jax.ShapeDtypeStruct((d, s), BF16)],
        compiler_params=_params(("parallel",)),
    )(x, w)


def rms_bwd(dh, x, w, resid, *, name, tm=512):
    s, d = x.shape

    def body(dh_ref, x_ref, w_ref, res_ref, dx_ref, dw_ref):
        xv = x_ref[...]
        dhv = dh_ref[...]
        r = lax.rsqrt(jnp.mean(xv * xv, axis=-1, keepdims=True) + NORM_EPS)
        xh = xv * r
        g = dhv * w_ref[...]
        dx_ref[...] = res_ref[...] + r * (g - xh * jnp.mean(g * xh, axis=-1, keepdims=True))
        part = jnp.sum(dhv * xh, axis=0, keepdims=True)

        @pl.when(pl.program_id(0) == 0)
        def _():
            dw_ref[...] = part

        @pl.when(pl.program_id(0) > 0)
        def _():
            dw_ref[...] += part

    row = pl.BlockSpec((tm, d), lambda i: (i, 0))
    vec = pl.BlockSpec((1, d), lambda i: (0, 0))
    return pl.pallas_call(
        body, name=name, grid=(s // tm,), in_specs=[row, row, vec, row], out_specs=[row, vec],
        out_shape=[jax.ShapeDtypeStruct((s, d), F32), jax.ShapeDtypeStruct((1, d), F32)],
        compiler_params=_params(("arbitrary",)),
    )(dh, x, w, resid)


def loss_head(y, target, *, tm=512):
    s, d = y.shape

    def body(y_ref, t_ref, dy_ref, sq_ref):
        e = y_ref[...] - t_ref[...]
        dy_ref[...] = e / float(d)
        part = jnp.sum(e * e, axis=0, keepdims=True)

        @pl.when(pl.program_id(0) == 0)
        def _():
            sq_ref[...] = part

        @pl.when(pl.program_id(0) > 0)
        def _():
            sq_ref[...] += part

    row = pl.BlockSpec((tm, d), lambda i: (i, 0))
    vec = pl.BlockSpec((1, d), lambda i: (0, 0))
    return pl.pallas_call(
        body, name="loss_head", grid=(s // tm,), in_specs=[row, row], out_specs=[row, vec],
        out_shape=[jax.ShapeDtypeStruct((s, d), F32), jax.ShapeDtypeStruct((1, d), F32)],
        compiler_params=_params(("arbitrary",)),
    )(y, target)


def _row_shifts(ext, k_taps):
    return [_shift_down(ext, j) for j in range(k_taps)]


def _conv_rows(shifts, w):
    k_taps = len(shifts)
    acc = w[k_taps - 1:k_taps, :] * shifts[0]
    for k in range(k_taps - 1):
        acc = acc + w[k:k + 1, :] * shifts[k_taps - 1 - k]
    return acc


def _conv_weight_grad(dcur, shifts, rows, width):
    k_taps = len(shifts)
    out = [jnp.sum(dcur * shifts[k_taps - 1 - k][rows], axis=0, keepdims=True) for k in range(k_taps)]
    out.append(jnp.sum(dcur, axis=0, keepdims=True))
    return _stack_rows(out, width)


def _conv_rows_transposed(dext, w, k_taps):
    acc = w[k_taps - 1:k_taps, :] * dext
    for k in range(k_taps - 1):
        acc = acc + w[k:k + 1, :] * _shift_up(dext, k_taps - 1 - k)
    return acc


def _stack_rows(rows, width):
    ri = _row_iota((8, width))
    out = jnp.zeros((8, width), F32)
    for k, r in enumerate(rows):
        out = out + jnp.where(ri == k, r, 0.0)
    return out


def ffn_mid_fwd(hu, conv_w8, conv_b, *, tm=512, tc=256):
    s = hu.shape[0]
    ncol = D_FF // tc
    r8 = tm // 8

    def body(g_ref, v_ref, gp_ref, vp_ref, wg_ref, wv_ref, bg_ref, bv_ref, o_ref, ot_ref):
        first = pl.program_id(1) == 0

        def conv(cur_ref, prev_ref, w_ref, b_ref):
            prev = jnp.where(first, 0.0, prev_ref[...])
            ext = jnp.concatenate([prev, cur_ref[...]], axis=0)
            return _conv_rows(_row_shifts(ext, FFN_CONV), w_ref[...])[8:] + b_ref[...]

        gc = conv(g_ref, gp_ref, wg_ref, bg_ref)
        vc = conv(v_ref, vp_ref, wv_ref, bv_ref)
        act = gc * _sigmoid(gc) * vc
        o_ref[...] = act.astype(BF16)
        ot_ref[...] = act.T.astype(BF16)

    def prev_idx(i):
        return jnp.maximum(i * r8 - 1, 0)

    in_specs = [
        pl.BlockSpec((tm, tc), lambda j, i: (i, j)),
        pl.BlockSpec((tm, tc), lambda j, i: (i, j + ncol)),
        pl.BlockSpec((8, tc), lambda j, i: (prev_idx(i), j)),
        pl.BlockSpec((8, tc), lambda j, i: (prev_idx(i), j + ncol)),
        pl.BlockSpec((8, tc), lambda j, i: (0, j)),
        pl.BlockSpec((8, tc), lambda j, i: (0, j + ncol)),
        pl.BlockSpec((1, tc), lambda j, i: (0, j)),
        pl.BlockSpec((1, tc), lambda j, i: (0, j + ncol)),
    ]
    return pl.pallas_call(
        body, name="ffn_mid_fwd", grid=(ncol, s // tm), in_specs=in_specs,
        out_specs=[pl.BlockSpec((tm, tc), lambda j, i: (i, j)), pl.BlockSpec((tc, tm), lambda j, i: (j, i))],
        out_shape=[jax.ShapeDtypeStruct((s, D_FF), BF16), jax.ShapeDtypeStruct((D_FF, s), BF16)],
        compiler_params=_params(("parallel", "parallel")),
    )(hu, hu, hu, hu, conv_w8, conv_w8, conv_b, conv_b)


def ffn_mid_bwd(hu, dact, conv_w8, conv_b, *, tm=512, tc=256):
    s = hu.shape[0]
    ncol = D_FF // tc
    nrow = s // tm
    r8 = tm // 8

    def body(g_ref, v_ref, gp_ref, vp_ref, gn_ref, vn_ref, da_ref, dan_ref, wg_ref, wv_ref, bg_ref, bv_ref,
             dg_ref, dv_ref, wgo_ref, wvo_ref):
        i = pl.program_id(1)
        first = i == 0
        last = i == nrow - 1

        def ext_of(cur_ref, prev_ref, next_ref):
            prev = jnp.where(first, 0.0, prev_ref[...])
            return jnp.concatenate([prev, cur_ref[...], next_ref[...]], axis=0)

        g_sh = _row_shifts(ext_of(g_ref, gp_ref, gn_ref), FFN_CONV)
        v_sh = _row_shifts(ext_of(v_ref, vp_ref, vn_ref), FFN_CONV)
        gc = _conv_rows(g_sh, wg_ref[...]) + bg_ref[...]
        vc = _conv_rows(v_sh, wv_ref[...]) + bv_ref[...]
        da_ext = jnp.concatenate([jnp.zeros((8, tc), F32), da_ref[...], jnp.where(last, 0.0, dan_ref[...])], axis=0)
        silu, dsilu = _silu_and_grad(gc)
        dgc = da_ext * vc * dsilu
        dvc = da_ext * silu
        dg_ref[...] = _conv_rows_transposed(dgc, wg_ref[...], FFN_CONV)[8:8 + tm].astype(BF16)
        dv_ref[...] = _conv_rows_transposed(dvc, wv_ref[...], FFN_CONV)[8:8 + tm].astype(BF16)

        cur = slice(8, 8 + tm)
        pg = _conv_weight_grad(dgc[cur], g_sh, cur, tc)
        pv = _conv_weight_grad(dvc[cur], v_sh, cur, tc)

        @pl.when(first)
        def _():
            wgo_ref[...] = pg
            wvo_ref[...] = pv

        @pl.when(i > 0)
        def _():
            wgo_ref[...] += pg
            wvo_ref[...] += pv

    def prev_idx(i):
        return jnp.maximum(i * r8 - 1, 0)

    def next_idx(i):
        return jnp.minimum((i + 1) * r8, s // 8 - 1)

    cur_g = pl.BlockSpec((tm, tc), lambda j, i: (i, j))
    cur_v = pl.BlockSpec((tm, tc), lambda j, i: (i, j + ncol))
    in_specs = [
        cur_g, cur_v,
        pl.BlockSpec((8, tc), lambda j, i: (prev_idx(i), j)),
        pl.BlockSpec((8, tc), lambda j, i: (prev_idx(i), j + ncol)),
        pl.BlockSpec((8, tc), lambda j, i: (next_idx(i), j)),
        pl.BlockSpec((8, tc), lambda j, i: (next_idx(i), j + ncol)),
        cur_g,
        pl.BlockSpec((8, tc), lambda j, i: (next_idx(i), j)),
        pl.BlockSpec((8, tc), lambda j, i: (0, j)),
        pl.BlockSpec((8, tc), lambda j, i: (0, j + ncol)),
        pl.BlockSpec((1, tc), lambda j, i: (0, j)),
        pl.BlockSpec((1, tc), lambda j, i: (0, j + ncol)),
    ]
    out_specs = [cur_g, cur_g, pl.BlockSpec((8, tc), lambda j, i: (0, j)), pl.BlockSpec((8, tc), lambda j, i: (0, j))]
    out_shape = [jax.ShapeDtypeStruct((s, D_FF), BF16), jax.ShapeDtypeStruct((s, D_FF), BF16),
                 jax.ShapeDtypeStruct((8, D_FF), F32), jax.ShapeDtypeStruct((8, D_FF), F32)]
    return pl.pallas_call(
        body, name="ffn_mid_bwd", grid=(ncol, nrow), in_specs=in_specs, out_specs=out_specs, out_shape=out_shape,
        compiler_params=_params(("parallel", "arbitrary")),
    )(hu, hu, hu, hu, hu, hu, dact, dact, conv_w8, conv_w8, conv_b, conv_b)


def _softplus(x):
    return jnp.maximum(x, 0.0) + jnp.log(1.0 + jnp.exp(-jnp.abs(x)))


def _cumsum_rows(v):
    n = v.shape[0]
    ri = _row_iota(v.shape)
    sh = 1
    while sh < n:
        v = v + jnp.where(ri >= sh, _shift_down(v, sh), 0.0)
        sh *= 2
    return v


def _rev_cumsum_rows(v):
    n = v.shape[0]
    ri = _row_iota(v.shape)
    sh = 1
    while sh < n:
        v = v + jnp.where(ri < n - sh, _shift_up(v, sh), 0.0)
        sh *= 2
    return v


def _half_row_sums(v, lo):
    s0 = jnp.sum(jnp.where(lo, v, 0.0), axis=1, keepdims=True)
    return s0, jnp.sum(v, axis=1, keepdims=True) - s0


def _total(v):
    return jnp.sum(jnp.sum(v, axis=1, keepdims=True), axis=0, keepdims=True)


def _ssd_in_specs(rev_nc=None):
    def ch(c):
        return c if rev_nc is None else rev_nc - 1 - c

    def prev(c):
        return jnp.maximum(ch(c) * (SSD_CHUNK // 8) - 1, 0)

    L = SSD_CHUNK
    return [
        pl.BlockSpec((L, 1024), lambda c: (ch(c), 0)),
        pl.BlockSpec((L, 1024), lambda c: (ch(c), 1)),
        pl.BlockSpec((L, 256), lambda c: (ch(c), 20)),
        pl.BlockSpec((L, 256), lambda c: (ch(c), 21)),
        pl.BlockSpec((8, 1024), lambda c: (prev(c), 1)),
        pl.BlockSpec((8, 256), lambda c: (prev(c), 20)),
        pl.BlockSpec((8, 256), lambda c: (prev(c), 21)),
        pl.BlockSpec((8, 1024), lambda c: (0, 0)),
        pl.BlockSpec((8, 256), lambda c: (0, 4)),
        pl.BlockSpec((8, 256), lambda c: (0, 5)),
        pl.BlockSpec((1, 1024), lambda c: (0, 0)),
        pl.BlockSpec((1, 256), lambda c: (0, 4)),
        pl.BlockSpec((1, 256), lambda c: (0, 5)),
        pl.BlockSpec((L, SMALL_COLS), lambda c: (ch(c), 0)),
        pl.BlockSpec((8, 128), lambda c: (0, 0)),
        pl.BlockSpec((1, 1024), lambda c: (0, 0)),
    ]


def _ssd_conv_pre(cur_ref, prev_ref, w_ref, b_ref, first):
    prev = jnp.where(first, 0.0, prev_ref[...])
    shifts = _row_shifts(jnp.concatenate([prev, cur_ref[...]], axis=0), SSD_CONV)
    return shifts, _conv_rows(shifts, w_ref[...])[8:] + b_ref[...]


def _ssd_time_consts(small_ref, sp_ref):
    dt_pre = small_ref[...] + sp_ref[0:1, :]
    dt = _softplus(dt_pre)
    a = -jnp.exp(sp_ref[1:2, :])
    acs = _cumsum_rows(dt * a)
    return dt_pre, dt, a, acs


def ssd_fwd(proj, small, conv_w8, conv_b, smallp, norm_w):
    s = proj.shape[0]
    nc = s // SSD_CHUNK
    L = SSD_CHUNK

    def body(z_ref, xs_ref, b_ref, c_ref, xsp_ref, bp_ref, cp_ref, wx_ref, wb_ref, wc_ref, bx_ref, bb_ref, bc_ref,
             small_ref, sp_ref, nw_ref, y_ref, yt_ref, ypre_ref, st_ref, state):
        first = pl.program_id(0) == 0

        @pl.when(first)
        def _():
            state[...] = jnp.zeros_like(state)

        xs = _ssd_conv_pre(xs_ref, xsp_ref, wx_ref, bx_ref, first)[1]
        xs = xs * _sigmoid(xs)
        bm = _ssd_conv_pre(b_ref, bp_ref, wb_ref, bb_ref, first)[1]
        bm = bm * _sigmoid(bm)
        cm = _ssd_conv_pre(c_ref, cp_ref, wc_ref, bc_ref, first)[1]
        cm = cm * _sigmoid(cm)
        _, dt, _, acs = _ssd_time_consts(small_ref, sp_ref)
        acs_t = acs.T
        li = _lane_iota((L, L))
        ri = _row_iota((L, L))
        tri = ri >= li
        lo = li < HEAD_DIM
        st_ref[0] = state[...]
        for g in range(2):
            bg = bm[:, 128 * g:128 * g + 128]
            cg = cm[:, 128 * g:128 * g + 128]
            gmat = _dot(cg, bg, "nt")
            for pp in range(4):
                p = 4 * g + pp
                h0, h1 = 2 * p, 2 * p + 1
                x = xs[:, 128 * p:128 * p + 128]
                a0, a1 = acs[:, h0:h0 + 1], acs[:, h1:h1 + 1]
                xdt = x * jnp.where(lo, dt[:, h0:h0 + 1], dt[:, h1:h1 + 1])
                m0 = gmat * jnp.exp(jnp.where(tri, a0 - acs_t[h0:h0 + 1, :], NEG_BIG))
                m1 = gmat * jnp.exp(jnp.where(tri, a1 - acs_t[h1:h1 + 1, :], NEG_BIG))
                yd = _dot(m0, jnp.where(lo, xdt, 0.0)) + _dot(m1, jnp.where(lo, 0.0, xdt))
                hin = state[p]
                yo = _dot(cg, hin, "nt") * jnp.exp(jnp.where(lo, a0, a1))
                dskip = jnp.where(lo[0:1], sp_ref[2:3, h0:h0 + 1], sp_ref[2:3, h1:h1 + 1])
                ypre_ref[:, 128 * p:128 * p + 128] = yd + yo + dskip * x
                al0, al1 = acs[L - 1:L, h0:h0 + 1], acs[L - 1:L, h1:h1 + 1]
                w = jnp.exp(jnp.where(lo, al0 - a0, al1 - a1))
                dec = jnp.exp(jnp.where(ri < HEAD_DIM, al0, al1))
                state[p] = dec * hin + _dot(xdt * w, bg, "tn")
        z = z_ref[...]
        yg = ypre_ref[...] * (z * _sigmoid(z))
        for g in range(2):
            seg = yg[:, 512 * g:512 * g + 512]
            r = lax.rsqrt(jnp.mean(seg * seg, axis=-1, keepdims=True) + NORM_EPS)
            out = (seg * r) * nw_ref[:, 512 * g:512 * g + 512]
            y_ref[:, 512 * g:512 * g + 512] = out.astype(BF16)
            yt_ref[512 * g:512 * g + 512, :] = out.T.astype(BF16)

    row = pl.BlockSpec((L, 1024), lambda c: (c, 0))
    return pl.pallas_call(
        body, name="ssd_fwd", grid=(nc,), in_specs=_ssd_in_specs(),
        out_specs=[row, pl.BlockSpec((1024, L), lambda c: (0, c)), row,
                   pl.BlockSpec((1, N_PAIRS, 128, 128), lambda c: (c, 0, 0, 0))],
        out_shape=[jax.ShapeDtypeStruct((s, 1024), BF16), jax.ShapeDtypeStruct((1024, s), BF16),
                   jax.ShapeDtypeStruct((s, 1024), F32), jax.ShapeDtypeStruct((nc, N_PAIRS, 128, 128), F32)],
        scratch_shapes=[pltpu.VMEM((N_PAIRS, 128, 128), F32)],
        compiler_params=_params(("arbitrary",)),
    )(proj, proj, proj, proj, proj, proj, proj, conv_w8, conv_w8, conv_w8, conv_b, conv_b, conv_b, small, smallp, norm_w)


def ssd_bwd(proj, small, conv_w8, conv_b, smallp, norm_w, ypre, states, dy):
    s = proj.shape[0]
    nc = s // SSD_CHUNK
    L = SSD_CHUNK

    def body(z_ref, xs_ref, b_ref, c_ref, xsp_ref, bp_ref, cp_ref, wx_ref, wb_ref, wc_ref, bx_ref, bb_ref, bc_ref,
             small_ref, sp_ref, nw_ref, ypre_ref, st_ref, dy_ref,
             dz_ref, dxs_ref, db_ref, dc_ref, dsmall_ref, gwx_ref, gwb_ref, gwc_ref, gsp_ref, gnw_ref,
             dstate, carry_x, carry_b, carry_c, dxs_buf, dbm_buf, dcm_buf):
        step = pl.program_id(0)
        first_chunk = step == nc - 1
        start = step == 0

        @pl.when(start)
        def _():
            dstate[...] = jnp.zeros_like(dstate)
            carry_x[...] = jnp.zeros_like(carry_x)
            carry_b[...] = jnp.zeros_like(carry_b)
            carry_c[...] = jnp.zeros_like(carry_c)

        xs_ext, xs_pre = _ssd_conv_pre(xs_ref, xsp_ref, wx_ref, bx_ref, first_chunk)
        b_ext, b_pre = _ssd_conv_pre(b_ref, bp_ref, wb_ref, bb_ref, first_chunk)
        c_ext, c_pre = _ssd_conv_pre(c_ref, cp_ref, wc_ref, bc_ref, first_chunk)
        xs, xs_ds = _silu_and_grad(xs_pre)
        bm, b_ds = _silu_and_grad(b_pre)
        cm, c_ds = _silu_and_grad(c_pre)
        dt_pre, dt, a, acs = _ssd_time_consts(small_ref, sp_ref)
        acs_t = acs.T
        li = _lane_iota((L, L))
        ri = _row_iota((L, L))
        tri = ri >= li
        lo = li < HEAD_DIM
        lo_rows = ri < HEAD_DIM
        li1 = _lane_iota((1, L))

        z = z_ref[...]
        sz, dsz = _silu_and_grad(z)
        y = ypre_ref[...]
        yg = y * sz
        dout = dy_ref[...]
        dyg_parts = []
        gnw_parts = []
        for g in range(2):
            sl = slice(512 * g, 512 * g + 512)
            seg = yg[:, sl]
            r = lax.rsqrt(jnp.mean(seg * seg, axis=-1, keepdims=True) + NORM_EPS)
            n = seg * r
            gnw_parts.append(jnp.sum(dout[:, sl] * n, axis=0, keepdims=True))
            gg = dout[:, sl] * nw_ref[:, sl]
            dyg_parts.append(r * (gg - n * jnp.mean(gg * n, axis=-1, keepdims=True)))
        dyg = jnp.concatenate(dyg_parts, axis=1)
        gnw = jnp.concatenate(gnw_parts, axis=1)
        dz_ref[...] = (dyg * y * dsz).astype(BF16)
        dypre = dyg * sz

        ddt = jnp.zeros((L, L), F32)
        dacs = jnp.zeros((L, L), F32)
        dacs_t = jnp.zeros((L, L), F32)
        dalast = jnp.zeros((1, L), F32)
        dskip_g = jnp.zeros((1, L), F32)
        for g in range(2):
            bg = bm[:, 128 * g:128 * g + 128]
            cg = cm[:, 128 * g:128 * g + 128]
            gmat = _dot(cg, bg, "nt")
            dgmat = jnp.zeros((L, L), F32)
            dbg = jnp.zeros((L, L), F32)
            dcg = jnp.zeros((L, L), F32)
            for pp in range(4):
                p = 4 * g + pp
                h0, h1 = 2 * p, 2 * p + 1
                x = xs[:, 128 * p:128 * p + 128]
                dyp = dypre[:, 128 * p:128 * p + 128]
                a0, a1 = acs[:, h0:h0 + 1], acs[:, h1:h1 + 1]
                dtl = jnp.where(lo, dt[:, h0:h0 + 1], dt[:, h1:h1 + 1])
                xdt = x * dtl
                l0 = jnp.exp(jnp.where(tri, a0 - acs_t[h0:h0 + 1, :], NEG_BIG))
                l1 = jnp.exp(jnp.where(tri, a1 - acs_t[h1:h1 + 1, :], NEG_BIG))
                m0, m1 = gmat * l0, gmat * l1
                dskip = jnp.where(lo[0:1], sp_ref[2:3, h0:h0 + 1], sp_ref[2:3, h1:h1 + 1])
                s0, s1 = _half_row_sums(dyp * x, lo)
                dskip_g = dskip_g + jnp.where(li1 == h0, _total(s0), 0.0) + jnp.where(li1 == h1, _total(s1), 0.0)
                dx = dyp * dskip
                dy0, dy1 = jnp.where(lo, dyp, 0.0), jnp.where(lo, 0.0, dyp)
                x0, x1 = jnp.where(lo, xdt, 0.0), jnp.where(lo, 0.0, xdt)
                dm0, dm1 = _dot(dy0, x0, "nt"), _dot(dy1, x1, "nt")
                dxdt = _dot(m0, dy0, "tn") + _dot(m1, dy1, "tn")
                q0, q1 = dm0 * m0, dm1 * m1
                dacs = dacs + jnp.where(li == h0, jnp.sum(q0, axis=1, keepdims=True), 0.0) \
                            + jnp.where(li == h1, jnp.sum(q1, axis=1, keepdims=True), 0.0)
                dacs_t = dacs_t - jnp.where(ri == h0, jnp.sum(q0, axis=0, keepdims=True), 0.0) \
                                - jnp.where(ri == h1, jnp.sum(q1, axis=0, keepdims=True), 0.0)
                dgmat = dgmat + dm0 * l0 + dm1 * l1
                hin = st_ref[0, p]
                e = jnp.exp(jnp.where(lo, a0, a1))
                ch = _dot(cg, hin, "nt")
                dch = dyp * e
                dcg = dcg + _dot(dch, hin)
                dhin = _dot(dch, cg, "tn")
                s0, s1 = _half_row_sums(dch * ch, lo)
                dacs = dacs + jnp.where(li == h0, s0, 0.0) + jnp.where(li == h1, s1, 0.0)
                dhout = dstate[p]
                al0, al1 = acs[L - 1:L, h0:h0 + 1], acs[L - 1:L, h1:h1 + 1]
                dec = jnp.exp(jnp.where(lo_rows, al0, al1))
                dhin = dhin + dec * dhout
                dal = dhout * hin * dec
                dal0 = _total(jnp.where(lo_rows, dal, 0.0))
                dal1 = _total(dal) - dal0
                w = jnp.exp(jnp.where(lo, al0 - a0, al1 - a1))
                xw = xdt * w
                dxw = _dot(bg, dhout, "nt")
                dbg = dbg + _dot(xw, dhout)
                dxdt = dxdt + dxw * w
                s0, s1 = _half_row_sums(dxw * xw, lo)
                dacs = dacs - jnp.where(li == h0, s0, 0.0) - jnp.where(li == h1, s1, 0.0)
                dal0, dal1 = dal0 + _total(s0), dal1 + _total(s1)
                dalast = dalast + jnp.where(li1 == h0, dal0, 0.0) + jnp.where(li1 == h1, dal1, 0.0)
                dx = dx + dxdt * dtl
                s0, s1 = _half_row_sums(dxdt * x, lo)
                ddt = ddt + jnp.where(li == h0, s0, 0.0) + jnp.where(li == h1, s1, 0.0)
                dxs_buf[:, 128 * p:128 * p + 128] = dx
                dstate[p] = dhin
            dcg = dcg + _dot(dgmat, bg)
            dbg = dbg + _dot(dgmat, cg, "tn")
            dbm_buf[:, 128 * g:128 * g + 128] = dbg
            dcm_buf[:, 128 * g:128 * g + 128] = dcg

        dacs_tot = dacs + dacs_t.T + jnp.where(ri == L - 1, dalast, 0.0)
        dstep = _rev_cumsum_rows(dacs_tot)
        ddt = ddt + dstep * a
        head_lane = li < N_HEADS
        ddt_pre = jnp.where(head_lane, ddt * _sigmoid(dt_pre), 0.0)
        dsmall_ref[...] = ddt_pre
        da = jnp.sum(jnp.where(head_lane, dstep * dt, 0.0), axis=0, keepdims=True)
        gsp = _stack_rows([jnp.sum(ddt_pre, axis=0, keepdims=True), da * a, dskip_g], L)

        def conv_back(dpost, ds, shifts, w_ref, carry, out_ref, width):
            dpre = dpost * ds
            dext = jnp.concatenate([dpre, carry[...]], axis=0)
            out_ref[...] = _conv_rows_transposed(dext, w_ref[...], SSD_CONV)[:L].astype(BF16)
            carry[...] = dpre[0:8]
            return _conv_weight_grad(dpre, shifts, slice(8, 8 + L), width)

        gwx = conv_back(dxs_buf[...], xs_ds, xs_ext, wx_ref, carry_x, dxs_ref, 1024)
        gwb = conv_back(dbm_buf[...], b_ds, b_ext, wb_ref, carry_b, db_ref, 256)
        gwc = conv_back(dcm_buf[...], c_ds, c_ext, wc_ref, carry_c, dc_ref, 256)

        @pl.when(start)
        def _():
            gwx_ref[...] = gwx
            gwb_ref[...] = gwb
            gwc_ref[...] = gwc
            gsp_ref[...] = gsp
            gnw_ref[...] = gnw

        @pl.when(step > 0)
        def _():
            gwx_ref[...] += gwx
            gwb_ref[...] += gwb
            gwc_ref[...] += gwc
            gsp_ref[...] += gsp
            gnw_ref[...] += gnw

    def ch(c):
        return nc - 1 - c

    row = pl.BlockSpec((L, 1024), lambda c: (ch(c), 0))
    row256 = pl.BlockSpec((L, 256), lambda c: (ch(c), 0))
    in_specs = _ssd_in_specs(rev_nc=nc) + [row, pl.BlockSpec((1, N_PAIRS, 128, 128), lambda c: (ch(c), 0, 0, 0)), row]
    out_specs = [row, row, row256, row256, pl.BlockSpec((L, 128), lambda c: (ch(c), 0)),
                 pl.BlockSpec((8, 1024), lambda c: (0, 0)), pl.BlockSpec((8, 256), lambda c: (0, 0)),
                 pl.BlockSpec((8, 256), lambda c: (0, 0)), pl.BlockSpec((8, 128), lambda c: (0, 0)),
                 pl.BlockSpec((1, 1024), lambda c: (0, 0))]
    out_shape = [jax.ShapeDtypeStruct((s, 1024), BF16), jax.ShapeDtypeStruct((s, 1024), BF16),
                 jax.ShapeDtypeStruct((s, 256), BF16), jax.ShapeDtypeStruct((s, 256), BF16),
                 jax.ShapeDtypeStruct((s, 128), F32),
                 jax.ShapeDtypeStruct((8, 1024), F32), jax.ShapeDtypeStruct((8, 256), F32),
                 jax.ShapeDtypeStruct((8, 256), F32), jax.ShapeDtypeStruct((8, 128), F32),
                 jax.ShapeDtypeStruct((1, 1024), F32)]
    scratch = [pltpu.VMEM((N_PAIRS, 128, 128), F32), pltpu.VMEM((8, 1024), F32), pltpu.VMEM((8, 256), F32),
               pltpu.VMEM((8, 256), F32), pltpu.VMEM((L, 1024), F32), pltpu.VMEM((L, 256), F32), pltpu.VMEM((L, 256), F32)]
    return pl.pallas_call(
        body, name="ssd_bwd", grid=(nc,), in_specs=in_specs, out_specs=out_specs, out_shape=out_shape,
        scratch_shapes=scratch, compiler_params=_params(("arbitrary",)),
    )(proj, proj, proj, proj, proj, proj, proj, conv_w8, conv_w8, conv_w8, conv_b, conv_b, conv_b, small, smallp, norm_w,
      ypre, states, dy)


FOX_SCALE = HEAD_DIM ** -0.5
FOX_T = 256
Q_COL, K_COL, V_COL = 2, 3, 4


def _split3_dot(v, m):
    hi = v.astype(BF16)
    r1 = v - hi.astype(F32)
    mid = r1.astype(BF16)
    lo = (r1 - mid.astype(F32)).astype(BF16)
    return _dot(hi, m) + _dot(mid, m) + _dot(lo, m)


def _head_rstd(x, sel_ref, selt_ref):
    ms = _split3_dot(x * x, sel_ref[...]) * (1.0 / HEAD_DIM)
    return _split3_dot(lax.rsqrt(ms + NORM_EPS), selt_ref[...])


def fox_tables():
    r = jnp.arange(3 * 128)
    piece, lane = r // 128, r % 128
    head = lane - F_LANE
    is_head = jnp.logical_and(head >= 0, head < N_HEADS)
    col = 128 * (head // 2) + HEAD_DIM * (1 - head % 2) + piece
    cols = jnp.arange(1024)
    place_q = jnp.logical_and(is_head[:, None], cols[None, :] == col[:, None]).astype(BF16)
    place_k = jnp.logical_and(is_head[:, None], cols[None, :] == (col + 3)[:, None]).astype(BF16)
    ones_q = jnp.logical_and(cols % HEAD_DIM >= 3, cols % HEAD_DIM < 6).astype(F32)[None]
    ones_k = (cols % HEAD_DIM < 3).astype(F32)[None]
    h = jnp.arange(128) - F_LANE
    ok = jnp.logical_and(h >= 0, h < N_HEADS)
    same_pair = cols[:, None] // 128 == (h // 2)[None, :]
    fold_even = jnp.logical_and(jnp.logical_and(ok, h % 2 == 0)[None, :], same_pair).astype(BF16)
    fold_odd = jnp.logical_and(jnp.logical_and(ok, h % 2 == 1)[None, :], same_pair).astype(BF16)
    return place_q, place_k, ones_q, ones_k, fold_even, fold_odd


def fox_prep(proj, small, smallp, qw, kw, sel, selt, place_q, place_k, ones_q, ones_k, *, tm=256):
    s = proj.shape[0]

    def body(q_ref, k_ref, v_ref, small_ref, sp_ref, qw_ref, kw_ref, sel_ref, selt_ref, pq_ref, pk_ref, oq_ref, ok_ref,
             qn_ref, kn_ref, aq_ref, ak_ref, vb_ref, knt_ref, akt_ref, vt_ref, carry):
        @pl.when(pl.program_id(0) == 0)
        def _():
            carry[...] = jnp.zeros_like(carry)

        q = q_ref[...]
        qn_ref[...] = (((q * _head_rstd(q, sel_ref, selt_ref)) * qw_ref[...]) * FOX_SCALE).astype(BF16)
        k = k_ref[...]
        kn = ((k * _head_rstd(k, sel_ref, selt_ref)) * kw_ref[...]).astype(BF16)
        kn_ref[...] = kn
        knt_ref[...] = kn.astype(F32).T.astype(BF16)
        vb_ref[...] = v_ref[...].astype(BF16)
        vt_ref[...] = v_ref[...].T.astype(BF16)
        li = _lane_iota((tm, 128))
        f_lane = jnp.logical_and(li >= F_LANE, li < F_LANE + N_HEADS)
        logf = jnp.where(f_lane, -_softplus(-(small_ref[...] + sp_ref[3:4, :])), 0.0)
        cum = _cumsum_rows(logf) + carry[...]
        carry[...] = cum[tm - 1:tm, :]
        hi = cum.astype(BF16)
        r1 = cum - hi.astype(F32)
        mid = r1.astype(BF16)
        lo = (r1 - mid.astype(F32)).astype(BF16)
        pieces = jnp.concatenate([hi, mid, lo], axis=1)
        aq_ref[...] = (_dot(pieces, pq_ref[...]) + oq_ref[...]).astype(BF16)
        ak = ok_ref[...] - _dot(pieces, pk_ref[...])
        ak_ref[...] = ak.astype(BF16)
        akt_ref[...] = ak.T.astype(BF16)

    row = pl.BlockSpec((tm, 1024), lambda i: (i, 0))
    col = pl.BlockSpec((1024, tm), lambda i: (0, i))
    vec = pl.BlockSpec((1, 1024), lambda i: (0, 0))
    table = pl.BlockSpec((384, 1024), lambda i: (0, 0))
    wide = jax.ShapeDtypeStruct((s, 1024), BF16)
    tall = jax.ShapeDtypeStruct((1024, s), BF16)
    return pl.pallas_call(
        body, name="fox_prep", grid=(s // tm,),
        in_specs=[pl.BlockSpec((tm, 1024), lambda i: (i, Q_COL)), pl.BlockSpec((tm, 1024), lambda i: (i, K_COL)),
                  pl.BlockSpec((tm, 1024), lambda i: (i, V_COL)),
                  pl.BlockSpec((tm, 128), lambda i: (i, 0)), pl.BlockSpec((8, 128), lambda i: (0, 0)), vec, vec,
                  pl.BlockSpec((1024, 128), lambda i: (0, 0)), pl.BlockSpec((128, 1024), lambda i: (0, 0)),
                  table, table, vec, vec],
        out_specs=[row, row, row, row, row, col, col, col],
        out_shape=[wide, wide, wide, wide, wide, tall, tall, tall],
        scratch_shapes=[pltpu.VMEM((1, 128), F32)], compiler_params=_params(("arbitrary",)),
    )(proj, proj, proj, small, smallp, qw, kw, sel, selt, place_q, place_k, ones_q, ones_k)


def fox_fwd(qn, kn, aq, ak, vt, shards=()):
    s = qn.shape[0]
    t = FOX_T
    nq = s // t
    ng = len(shards)

    def body(*refs):
        q_ref, k_ref, aq_ref, ak_ref, vt_ref = refs[:5]
        o_ref, ot_ref, lse_ref = refs[5 + ng:8 + ng]
        p = pl.program_id(0)
        if ng:
            start, forward, finish = _gather_phases(refs[5:5 + ng], refs[8 + ng:8 + 2 * ng], *refs[8 + 2 * ng:])
            pl.when(p == 0)(start)
            pl.when(p == N_PAIRS // 2)(forward)

        @pl.when(p == 0)
        def _():
            lse_ref[...] = jnp.zeros_like(lse_ref)

        lo = _lane_iota((t, 128)) < HEAD_DIM
        lo_rows = _row_iota((128, t)) < HEAD_DIM
        causal_t = _lane_iota((t, t)) >= _row_iota((t, t))

        def q_loop(qi, _):
            q0 = pl.multiple_of(qi * t, t)
            qv, aqv = q_ref[pl.ds(q0, t), :], aq_ref[pl.ds(q0, t), :]
            qa, qb = jnp.where(lo, qv, aqv), jnp.where(lo, aqv, qv)

            def scores(kj):
                k0 = pl.multiple_of(kj * t, t)
                kv, akv = k_ref[pl.ds(k0, t), :], ak_ref[pl.ds(k0, t), :]
                return _dot(jnp.where(lo, kv, akv), qa, "nt"), _dot(jnp.where(lo, akv, kv), qb, "nt")

            def update(kj, stats, s0, s1):
                m0, l0, m1, l1, acc = stats
                vtv = vt_ref[:, pl.ds(pl.multiple_of(kj * t, t), t)]
                n0 = jnp.maximum(m0, jnp.max(s0, axis=0, keepdims=True))
                n1 = jnp.maximum(m1, jnp.max(s1, axis=0, keepdims=True))
                a0, a1 = jnp.exp(m0 - n0), jnp.exp(m1 - n1)
                p0, p1 = jnp.exp(s0 - n0), jnp.exp(s1 - n1)
                l0 = a0 * l0 + jnp.sum(p0, axis=0, keepdims=True)
                l1 = a1 * l1 + jnp.sum(p1, axis=0, keepdims=True)
                acc = (jnp.where(lo_rows, a0, a1) * acc + _dot(jnp.where(lo_rows, vtv, 0.0), p0)
                       + _dot(jnp.where(lo_rows, 0.0, vtv), p1))
                return n0, l0, n1, l1, acc

            def step(kj, carry):
                stats, (s0, s1) = carry[:5], carry[5:]
                nxt = scores(kj + 1)
                return (*update(kj, stats, s0, s1), *nxt)

            def row(val):
                return jnp.full((1, t), val, F32)

            init = (row(NEG_BIG), row(0.0), row(NEG_BIG), row(0.0), jnp.zeros((128, t), F32), *scores(0))
            carry = lax.fori_loop(0, qi, step, init)
            s0, s1 = jnp.where(causal_t, carry[5], NEG_BIG), jnp.where(causal_t, carry[6], NEG_BIG)
            m0, l0, m1, l1, acc = update(qi, carry[:5], s0, s1)
            out_t = acc / jnp.where(lo_rows, l0, l1)
            ot_ref[:, pl.ds(q0, t)] = out_t.astype(BF16)
            o_ref[pl.ds(q0, t), :] = out_t.T.astype(BF16)
            ri = _row_iota((N_HEADS, t))
            old = lse_ref[:, pl.ds(q0, t)]
            lse_ref[:, pl.ds(q0, t)] = jnp.where(
                ri == 2 * p, m0 + jnp.log(l0), jnp.where(ri == 2 * p + 1, m1 + jnp.log(l1), old))
            return 0

        lax.fori_loop(0, nq, q_loop, 0)
        if ng:
            pl.when(p == N_PAIRS - 1)(finish)

    pair = pl.BlockSpec((s, 128), lambda p: (0, p))
    outs = pl.pallas_call(
        body, name="fox_fwd", grid=(N_PAIRS,),
        in_specs=[pair] * 4 + [pl.BlockSpec((128, s), lambda p: (p, 0))] + [ANY] * ng,
        out_specs=[pair, pl.BlockSpec((128, s), lambda p: (p, 0)), pl.BlockSpec((N_HEADS, s), lambda p: (0, 0))] + [ANY] * ng,
        out_shape=[jax.ShapeDtypeStruct((s, 1024), BF16), jax.ShapeDtypeStruct((1024, s), BF16),
                   jax.ShapeDtypeStruct((N_HEADS, s), F32)] + _gather_out_shapes(shards),
        scratch_shapes=_gather_scratch(ng) if ng else [],
        compiler_params=_params(("arbitrary",)),
    )(qn, kn, aq, ak, vt, *shards)
    return outs[0], outs[1], outs[2], _place_own_blocks(outs[3:], shards)


def fox_bwd(qn, kn, aq, ak, knt, akt, vb, lse, dmixed, parts=()):
    s = qn.shape[0]
    t = FOX_T
    nq = s // t
    once = pl.Buffered(1)
    ns = len(parts)

    def body(*refs):
        q_ref, k_ref, aq_ref, ak_ref, kt_ref, akt_ref, v_ref, lse_ref, do_ref = refs[:9]
        dq_ref, dk_ref, dv_ref, dc0_ref, dc1_ref = refs[9 + ns:14 + ns]
        p_scr, dp_scr = refs[14 + 2 * ns:16 + 2 * ns]
        p = pl.program_id(0)
        if ns:
            start, finish = _scatter_phases(refs[9:9 + ns], refs[14 + ns:14 + 2 * ns], *refs[16 + 2 * ns:])
            pl.when(p == 0)(start)
        dk_ref[...] = jnp.zeros_like(dk_ref)
        dv_ref[...] = jnp.zeros_like(dv_ref)
        dc0_ref[...] = jnp.zeros_like(dc0_ref)
        dc1_ref[...] = jnp.zeros_like(dc1_ref)
        lo = _lane_iota((t, 128)) < HEAD_DIM
        lo_rows = _row_iota((128, t)) < HEAD_DIM
        causal_t = _lane_iota((t, t)) >= _row_iota((t, t))

        def q_loop(qi, _):
            q0 = pl.multiple_of(qi * t, t)
            qv, aqv = q_ref[pl.ds(q0, t), :], aq_ref[pl.ds(q0, t), :]
            qa, qb = jnp.where(lo, qv, aqv), jnp.where(lo, aqv, qv)
            do = do_ref[pl.ds(q0, t), :]
            doa, dob = jnp.where(lo, do, 0.0).astype(BF16), jnp.where(lo, 0.0, do).astype(BF16)
            lse_blk = lse_ref[:, pl.ds(q0, t)]
            ri = _row_iota((N_HEADS, t))
            lse0 = jnp.sum(jnp.where(ri == 2 * p, lse_blk, 0.0), axis=0, keepdims=True)
            lse1 = jnp.sum(jnp.where(ri == 2 * p + 1, lse_blk, 0.0), axis=0, keepdims=True)

            def scores(kj):
                k0 = pl.multiple_of(kj * t, t)
                kv, akv = k_ref[pl.ds(k0, t), :], ak_ref[pl.ds(k0, t), :]
                return _dot(jnp.where(lo, kv, akv), qa, "nt"), _dot(jnp.where(lo, akv, kv), qb, "nt")

            def pass1(kj, d0, d1, diagonal):
                k0 = pl.multiple_of(kj * t, t)
                vv = v_ref[pl.ds(k0, t), :]
                s0, s1 = scores(kj)
                if diagonal:
                    s0, s1 = jnp.where(causal_t, s0, NEG_BIG), jnp.where(causal_t, s1, NEG_BIG)
                p0, p1 = jnp.exp(s0 - lse0), jnp.exp(s1 - lse1)
                dp0, dp1 = _dot(vv, doa, "nt"), _dot(vv, dob, "nt")
                p_scr[0, kj], p_scr[1, kj] = p0, p1
                dp_scr[0, kj], dp_scr[1, kj] = dp0, dp1
                dv_ref[pl.ds(k0, t), :] += _dot(p0, doa) + _dot(p1, dob)
                return d0 + jnp.sum(p0 * dp0, axis=0, keepdims=True), d1 + jnp.sum(p1 * dp1, axis=0, keepdims=True)

            zero = jnp.zeros((1, t), F32)
            d0, d1 = lax.fori_loop(0, qi, lambda kj, c: pass1(kj, *c, False), (zero, zero))
            d0, d1 = pass1(qi, d0, d1, True)

            def pass2(kj, carry):
                dq0, dq1 = carry
                k0 = pl.multiple_of(kj * t, t)
                p0, p1 = p_scr[0, kj], p_scr[1, kj]
                ds0, ds1 = p0 * (dp_scr[0, kj] - d0), p1 * (dp_scr[1, kj] - d1)
                dk_ref[pl.ds(k0, t), :] += jnp.where(lo, _dot(ds0, qa), _dot(ds1, qb))
                dc0_ref[pl.ds(k0, t), :] += ds0[:, :128] + ds0[:, 128:]
                dc1_ref[pl.ds(k0, t), :] += ds1[:, :128] + ds1[:, 128:]
                ktv, aktv = kt_ref[:, pl.ds(k0, t)], akt_ref[:, pl.ds(k0, t)]
                return dq0 + _dot(jnp.where(lo_rows, ktv, aktv), ds0), dq1 + _dot(jnp.where(lo_rows, aktv, ktv), ds1)

            zq = jnp.zeros((128, t), F32)
            dq0, dq1 = lax.fori_loop(0, qi + 1, pass2, (zq, zq))
            dq_ref[pl.ds(q0, t), :] = jnp.where(lo_rows, dq0, dq1).T
            return 0

        lax.fori_loop(0, nq, q_loop, 0)
        if ns:
            pl.when(p == N_PAIRS - 1)(finish)

    pair = pl.BlockSpec((s, 128), lambda p: (0, p), pipeline_mode=once)
    pair_t = pl.BlockSpec((128, s), lambda p: (p, 0), pipeline_mode=once)
    out = jax.ShapeDtypeStruct((s, 1024), F32)
    outs = pl.pallas_call(
        body, name="fox_bwd", grid=(N_PAIRS,),
        in_specs=[pair, pair, pair, pair, pair_t, pair_t, pair, pl.BlockSpec((N_HEADS, s), lambda p: (0, 0)),
                  pl.BlockSpec((s, 128), lambda p: (0, 8 + p), pipeline_mode=once)] + [ANY] * ns,
        out_specs=[pair] * 5 + [ANY] * ns,
        out_shape=[out] * 5 + [jax.ShapeDtypeStruct(p.shape, p.dtype) for p in parts],
        scratch_shapes=[pltpu.VMEM((2, nq, t, t), F32), pltpu.VMEM((2, nq, t, t), F32)] + (_scatter_scratch(ns) if ns else []),
        compiler_params=_params(("arbitrary",)),
    )(qn, kn, aq, ak, knt, akt, vb, lse, dmixed, *parts)
    return (*outs[:5], _keep_own_blocks(outs[5:], parts))


def fox_post(dqn, dkn, dc0, dc1, proj, small, smallp, qw, kw, sel, selt, fold_even, fold_odd, *, tm=256):
    s = proj.shape[0]
    nrow = s // tm

    def body(dqn_ref, dkn_ref, dc0_ref, dc1_ref, q_ref, k_ref, small_ref, sp_ref, qw_ref, kw_ref, sel_ref, selt_ref,
             fe_ref, fo_ref, dq_ref, dk_ref, dsmall_ref, gqw_ref, gkw_ref, gfb_ref, carry):
        step = pl.program_id(0)

        @pl.when(step == 0)
        def _():
            carry[...] = jnp.zeros_like(carry)

        def norm_bwd(x_ref, w_ref, dn, out_ref):
            x = x_ref[...]
            rf = _head_rstd(x, sel_ref, selt_ref)
            xh = x * rf
            g = dn * w_ref[...]
            mean_gx = _split3_dot(_split3_dot(g * xh, sel_ref[...]) * (1.0 / HEAD_DIM), selt_ref[...])
            out_ref[...] = (rf * (g - xh * mean_gx)).astype(BF16)
            return jnp.sum(dn * xh, axis=0, keepdims=True)

        gqw = norm_bwd(q_ref, qw_ref, dqn_ref[...] * FOX_SCALE, dq_ref)
        gkw = norm_bwd(k_ref, kw_ref, dkn_ref[...], dk_ref)
        li = _lane_iota((tm, 128))
        f_lane = jnp.logical_and(li >= F_LANE, li < F_LANE + N_HEADS)
        dcum = -(_split3_dot(dc0_ref[...], fe_ref[...]) + _split3_dot(dc1_ref[...], fo_ref[...]))
        dlogf = _rev_cumsum_rows(dcum) + carry[...]
        carry[...] = dlogf[0:1, :]
        dfr = jnp.where(f_lane, dlogf * _sigmoid(-(small_ref[...] + sp_ref[3:4, :])), 0.0)
        dsmall_ref[...] = dfr
        gfb = jnp.sum(dfr, axis=0, keepdims=True)

        @pl.when(step == 0)
        def _():
            gqw_ref[...] = gqw
            gkw_ref[...] = gkw
            gfb_ref[...] = gfb

        @pl.when(step > 0)
        def _():
            gqw_ref[...] += gqw
            gkw_ref[...] += gkw
            gfb_ref[...] += gfb

    def rb(i):
        return nrow - 1 - i

    row = pl.BlockSpec((tm, 1024), lambda i: (rb(i), 0))
    vec = pl.BlockSpec((1, 1024), lambda i: (0, 0))
    fold = pl.BlockSpec((1024, 128), lambda i: (0, 0))
    return pl.pallas_call(
        body, name="fox_post", grid=(nrow,),
        in_specs=[row, row, row, row, pl.BlockSpec((tm, 1024), lambda i: (rb(i), Q_COL)),
                  pl.BlockSpec((tm, 1024), lambda i: (rb(i), K_COL)),
                  pl.BlockSpec((tm, 128), lambda i: (rb(i), 0)), pl.BlockSpec((8, 128), lambda i: (0, 0)), vec, vec,
                  fold, pl.BlockSpec((128, 1024), lambda i: (0, 0)), fold, fold],
        out_specs=[row, row, pl.BlockSpec((tm, 128), lambda i: (rb(i), 0)), vec, vec, pl.BlockSpec((1, 128), lambda i: (0, 0))],
        out_shape=[jax.ShapeDtypeStruct((s, 1024), BF16), jax.ShapeDtypeStruct((s, 1024), BF16),
                   jax.ShapeDtypeStruct((s, 128), F32), jax.ShapeDtypeStruct((1, 1024), F32),
                   jax.ShapeDtypeStruct((1, 1024), F32), jax.ShapeDtypeStruct((1, 128), F32)],
        scratch_shapes=[pltpu.VMEM((1, 128), F32)], compiler_params=_params(("arbitrary",)),
    )(dqn, dkn, dc0, dc1, proj, proj, small, smallp, qw, kw, sel, selt, fold_even, fold_odd)


def local_step(x, target, wm, ws, later_shards, ssd_cw8, ssd_cb, smallp, ssd_nw, qw_t, kw_t, sel, selt,
               norm_mix_w, norm_ffn_w, ffn_cw8, ffn_cb):
    h, h_t = rms_fwd(x, norm_mix_w, name="rms_mix_fwd")
    proj = matmul(h, wm, mode="nn", tm=1024, tn=1408, tk=1024, out_dtype=F32, name="mm_in_proj")
    small = matmul(h, ws, mode="nn", tm=1024, tn=128, tk=1024, out_dtype=F32, name="mm_in_proj_small")
    y_ssd, y_ssd_t, ypre, states = ssd_fwd(proj, small, ssd_cw8, ssd_cb, smallp, ssd_nw)
    place_q, place_k, ones_q, ones_k, fold_even, fold_odd = fox_tables()
    qn, kn, aq, ak, vb, knt, akt, vt = fox_prep(proj, small, smallp, qw_t, kw_t, sel, selt, place_q, place_k, ones_q, ones_k)
    y_fox, y_fox_t, lse, (a_out, a_up, a_down) = fox_fwd(qn, kn, aq, ak, vt, shards=later_shards)
    w_out = a_out.reshape(2048, D_MODEL)
    w_up = a_up.transpose(1, 0, 2).reshape(D_MODEL, 2 * D_FF)
    w_down = a_down.reshape(D_FF, D_MODEL)
    x1 = matmul(y_ssd, w_out, mode="nn", tm=1024, tn=1024, tk=1024, out_dtype=F32, name="mm_out_ssd", add=x)
    x1 = matmul(y_fox, w_out, mode="nn", tm=1024, tn=1024, tk=1024, out_dtype=F32, name="mm_out_fox", add=x1, b_koff=1)
    hf, hf_t = rms_fwd(x1, norm_ffn_w, name="rms_ffn_fwd")
    hu = matmul(hf, w_up, mode="nn", tm=1024, tn=1408, tk=1024, out_dtype=F32, name="mm_up")
    act, act_t = ffn_mid_fwd(hu, ffn_cw8, ffn_cb)
    y = matmul(act, w_down, mode="nn", tm=1024, tn=1024, tk=1408, out_dtype=F32, name="mm_down", add=x1)
    dy, sq = loss_head(y, target)

    dact = matmul(dy, w_down, mode="nt", tm=1024, tn=1408, tk=1024, out_dtype=F32, name="mm_dact")
    g_down = matmul(act_t, dy, mode="nn", tm=1408, tn=1024, tk=1024, out_dtype=BF16, name="mm_dw_down")
    dhu_g, dhu_v, gcw_g, gcw_v = ffn_mid_bwd(hu, dact, ffn_cw8, ffn_cb)
    dhf = matmul(dhu_g, w_up, mode="nt", tm=1024, tn=1024, tk=1408, out_dtype=F32, name="mm_dhf_gate")
    dhf = matmul(dhu_v, w_up, mode="nt", tm=1024, tn=1024, tk=1408, out_dtype=F32, name="mm_dhf_val", add=dhf, b_koff=2)
    g_up_g = matmul(hf_t, dhu_g, mode="nn", tm=1024, tn=1408, tk=1024, out_dtype=BF16, name="mm_dw_up_gate")
    g_up_v = matmul(hf_t, dhu_v, mode="nn", tm=1024, tn=1408, tk=1024, out_dtype=BF16, name="mm_dw_up_val")
    dx1, g_norm_ffn = rms_bwd(dhf, x1, norm_ffn_w, dy, name="rms_ffn_bwd")
    dmixed = matmul(dx1, w_out, mode="nt", tm=1024, tn=1024, tk=1024, out_dtype=F32, name="mm_dmixed")
    g_out_a = matmul(y_ssd_t, dx1, mode="nn", tm=1024, tn=1024, tk=1024, out_dtype=BF16, name="mm_dw_out_ssd")
    g_out_b = matmul(y_fox_t, dx1, mode="nn", tm=1024, tn=1024, tk=1024, out_dtype=BF16, name="mm_dw_out_fox")
    early = [jnp.concatenate([g_out_a, g_out_b], axis=0).reshape(4, 512, D_MODEL),
             jnp.concatenate([g_up_g, g_up_v], axis=1).reshape(D_MODEL, 4, 1408).transpose(1, 0, 2),
             g_down.reshape(4, 704, D_MODEL)]
    mine, theirs = pair_swap_halves(early, name="pair_swap_early")
    parts = [add_pair(a, b, name="add_pair_" + n, tr=ADAM_ROWS[n]) for a, b, n in zip(mine, theirs, BIG_NAMES[1:])]
    dz, dxs, db, dc, dsmall_ssd, gcw_x, gcw_b, gcw_c, g_sp, g_ssd_nw = ssd_bwd(
        proj, small, ssd_cw8, ssd_cb, smallp, ssd_nw, ypre, states, dmixed)
    dqn, dkn, dv, dc0, dc1, landed_early = fox_bwd(qn, kn, aq, ak, knt, akt, vb, lse, dmixed, parts=parts)
    dq, dk, dsmall_fox, g_qw, g_kw, g_fb = fox_post(dqn, dkn, dc0, dc1, proj, small, smallp, qw_t, kw_t, sel, selt,
                                                    fold_even, fold_odd)
    dproj = jnp.concatenate([dz, dxs, dq, dk, dv.astype(BF16), db, dc], axis=1)
    dsmall = (dsmall_ssd + dsmall_fox).astype(BF16)
    g_wm = matmul(h_t, dproj, mode="nn", tm=1024, tn=1408, tk=1024, out_dtype=BF16, name="mm_dw_in")
    g_ws = matmul(h_t, dsmall, mode="nn", tm=1024, tn=128, tk=1024, out_dtype=BF16, name="mm_dw_in_small")
    g_in = jnp.concatenate([g_wm[:, :2048], g_wm[:, 5120:5632], g_ws[:, :16], g_wm[:, 2048:5120], g_ws[:, 16:32]], axis=1)
    mine, theirs = pair_swap_halves([g_in.reshape(D_MODEL, 4, 1416).transpose(1, 0, 2)], name="pair_swap_w_in")
    part_in = add_pair(mine[0], theirs[0], name="add_pair_w_in", tr=ADAM_ROWS["w_in"])
    dh, landed_in = matmul(dproj, wm, mode="nt", tm=1024, tn=1024, tk=1408, out_dtype=F32, name="mm_dh", scatter=[part_in])
    dh = matmul(dsmall, ws, mode="nt", tm=1024, tn=1024, tk=128, out_dtype=F32, name="mm_dh_small", add=dh)
    grad_x, g_norm_mix = rms_bwd(dh, x, norm_mix_w, dx1, name="rms_mix_bwd")
    return dict(
        sq=sq, grad_x=grad_x, landed=landed_in + landed_early,
        g_norm_mix=g_norm_mix, g_norm_ffn=g_norm_ffn, g_ssd_nw=g_ssd_nw,
        g_ssd_cw=jnp.concatenate([gcw_x, gcw_b, gcw_c], axis=1), g_sp=g_sp, g_fb=g_fb, g_qw=g_qw, g_kw=g_kw,
        g_ffn_cw=jnp.concatenate([gcw_g, gcw_v], axis=1))


def adamw(w, g, m, v, *, name, tr):
    rows, cols = w.shape

    def body(w_ref, g_ref, m_ref, v_ref, d_ref, mo_ref, vo_ref):
        gv = g_ref[...]
        mn = ADAM_B1 * m_ref[...] + (1.0 - ADAM_B1) * gv
        vn = ADAM_B2 * v_ref[...] + (1.0 - ADAM_B2) * (gv * gv)
        m_hat = mn / (1.0 - ADAM_B1 ** ADAM_STEP)
        v_hat = vn / (1.0 - ADAM_B2 ** ADAM_STEP)
        d_ref[...] = -ADAM_LR * (m_hat / (jnp.sqrt(v_hat) + ADAM_EPS) + ADAM_WD * w_ref[...])
        mo_ref[...] = mn
        vo_ref[...] = vn

    blk = pl.BlockSpec((tr, cols), lambda i: (i, 0))
    shp = jax.ShapeDtypeStruct((rows, cols), F32)
    return pl.pallas_call(
        body, name=name, grid=(rows // tr,), in_specs=[blk] * 4, out_specs=[blk] * 3, out_shape=[shp] * 3,
        compiler_params=_params(("parallel",)),
    )(w, g, m, v)


def add_pair(a, b, *, name, tr):
    _, rows, cols = a.shape

    def body(a_ref, b_ref, o_ref):
        o_ref[...] = (a_ref[...].astype(F32) + b_ref[...].astype(F32)).astype(BF16)

    blk = pl.BlockSpec((1, tr, cols), lambda j, i: (j, i, 0))
    return pl.pallas_call(
        body, name=name, grid=(4, rows // tr), in_specs=[blk, blk], out_specs=blk,
        out_shape=jax.ShapeDtypeStruct(a.shape, BF16), compiler_params=_params(("parallel", "parallel")),
    )(a, b)


def sum_chips(parts, *, name, tr):
    _, rows, cols = parts.shape

    def body(p_ref, o_ref):
        acc = p_ref[0].astype(F32)
        for k in range(1, 4):
            acc = acc + p_ref[k].astype(F32)
        o_ref[...] = acc

    return pl.pallas_call(
        body, name=name, grid=(rows // tr,), in_specs=[pl.BlockSpec((4, tr, cols), lambda i: (0, i, 0))],
        out_specs=pl.BlockSpec((tr, cols), lambda i: (i, 0)), out_shape=jax.ShapeDtypeStruct((rows, cols), F32),
        compiler_params=_params(("parallel",)),
    )(parts)


ANY = pl.BlockSpec(memory_space=pl.ANY)


def _place():
    x, y, c = lax.axis_index("x"), lax.axis_index("y"), lax.axis_index("c")
    chips = [(1 - x, y), (x, 1 - y), (1 - x, 1 - y)]
    return x, y, c, chips


def _chunks(rows):
    size = next((c for c in (128, 176, 64, 32, 16, 8) if rows % c == 0), rows)
    return [(r, size) for r in range(0, rows, size)]


def gather_weights(shards):
    n = len(shards)

    def body(*refs):
        start, forward, finish = _gather_phases(refs[:n], refs[n:2 * n], *refs[2 * n:])
        start()
        forward()
        finish()

    gathered = pl.pallas_call(
        body, name="gather_weights", in_specs=[ANY] * n, out_specs=[ANY] * n,
        out_shape=_gather_out_shapes(shards), scratch_shapes=_gather_scratch(n),
    )(*shards)
    return _place_own_blocks(gathered, shards)


def _gather_out_shapes(shards):
    return [jax.ShapeDtypeStruct((4,) + s.shape, s.dtype) for s in shards]


def _gather_scratch(n):
    return [pltpu.SemaphoreType.DMA((n, 6)), pltpu.SemaphoreType.DMA((n, 6))]


def _place_own_blocks(gathered, shards):
    if not shards:
        return []
    chip = 2 * lax.axis_index("x") + lax.axis_index("y")
    return [lax.dynamic_update_slice(g, s[None], (chip, 0, 0)) for g, s in zip(gathered, shards)]


def _gather_phases(ins, outs, send_sems, recv_sems):
    n = len(ins)
    x, y, c, chips = _place()
    me = 2 * x + y
    sibling = (x, y, 1 - c)
    blks = [2 * cx + cy for cx, cy in chips]

    def half(a, blk, r=0, nr=None):
        rows = ins[a].shape[0] // 2
        return outs[a].at[blk, pl.ds(c * rows + r, rows if nr is None else nr), :]

    def to_chip(a, t, r=0, nr=None):
        rows = ins[a].shape[0] // 2
        return pltpu.make_async_remote_copy(
            src_ref=ins[a].at[pl.ds(c * rows + r, rows if nr is None else nr), :], dst_ref=half(a, me, r, nr),
            send_sem=send_sems.at[a, t], recv_sem=recv_sems.at[a, t], device_id=(*chips[t], c), device_id_type=MESH)

    def from_chip(a, t):
        return pltpu.make_async_remote_copy(
            src_ref=half(a, blks[t]), dst_ref=half(a, blks[t]), send_sem=send_sems.at[a, t], recv_sem=recv_sems.at[a, t],
            device_id=(*chips[t], c), device_id_type=MESH)

    def to_sibling(a, t, r=0, nr=None):
        return pltpu.make_async_remote_copy(
            src_ref=half(a, blks[t], r, nr), dst_ref=half(a, blks[t], r, nr), send_sem=send_sems.at[a, 3 + t],
            recv_sem=recv_sems.at[a, 3 + t], device_id=sibling, device_id_type=MESH)

    def from_sibling(a, t):
        rows = ins[a].shape[0] // 2
        dst = outs[a].at[blks[t], pl.ds((1 - c) * rows, rows), :]
        return pltpu.make_async_remote_copy(
            src_ref=dst, dst_ref=dst, send_sem=send_sems.at[a, 3 + t], recv_sem=recv_sems.at[a, 3 + t],
            device_id=sibling, device_id_type=MESH)

    def start():
        for a in range(n):
            for t in range(3):
                for r, nr in _chunks(ins[a].shape[0] // 2):
                    to_chip(a, t, r, nr).start()

    def forward():
        for a in range(n):
            for t in range(3):
                from_chip(a, t).wait_recv()
                for r, nr in _chunks(ins[a].shape[0] // 2):
                    to_sibling(a, t, r, nr).start()

    def finish():
        for a in range(n):
            for t in range(3):
                from_sibling(a, t).wait_recv()
        for a in range(n):
            for t in range(3):
                to_chip(a, t).wait_send()
                to_sibling(a, t).wait_send()

    return start, forward, finish


def pair_swap_halves(grads, *, name):
    n = len(grads)

    def body(*refs):
        ins, theirs = refs[:n], refs[n:2 * n]
        send_sems, recv_sems = refs[2 * n:]
        x, y, c, _ = _place()
        sibling = (x, y, 1 - c)
        for a in range(n):
            rows = ins[a].shape[1] // 2
            for j in range(4):
                for r, nr in _chunks(rows):
                    pltpu.make_async_remote_copy(
                        src_ref=ins[a].at[j, pl.ds((1 - c) * rows + r, nr), :], dst_ref=theirs[a].at[j, pl.ds(r, nr), :],
                        send_sem=send_sems.at[a], recv_sem=recv_sems.at[a], device_id=sibling, device_id_type=MESH).start()
        for a in range(n):
            pltpu.make_async_remote_copy(src_ref=theirs[a], dst_ref=theirs[a], send_sem=send_sems.at[a],
                                         recv_sem=recv_sems.at[a], device_id=sibling, device_id_type=MESH).wait()

    halves = [jax.ShapeDtypeStruct((4, g.shape[1] // 2, g.shape[2]), g.dtype) for g in grads]
    theirs = pl.pallas_call(
        body, name=name, in_specs=[ANY] * n, out_specs=[ANY] * n, out_shape=halves,
        scratch_shapes=[pltpu.SemaphoreType.DMA((n,)), pltpu.SemaphoreType.DMA((n,))],
    )(*grads)
    c = lax.axis_index("c")
    mine = [lax.dynamic_slice_in_dim(g, c * (g.shape[1] // 2), g.shape[1] // 2, axis=1) for g in grads]
    return mine, theirs


def scatter_to_chips(parts):
    n = len(parts)

    def body(*refs):
        start, finish = _scatter_phases(refs[:n], refs[n:2 * n], *refs[2 * n:])
        start()
        finish()

    landed = pl.pallas_call(
        body, name="scatter_to_chips", in_specs=[ANY] * n, out_specs=[ANY] * n,
        out_shape=[jax.ShapeDtypeStruct(p.shape, p.dtype) for p in parts], scratch_shapes=_scatter_scratch(n),
    )(*parts)
    return _keep_own_blocks(landed, parts)


def _scatter_scratch(n):
    return [pltpu.SemaphoreType.DMA((n, 3)), pltpu.SemaphoreType.DMA((n, 3))]


def _keep_own_blocks(landed, parts):
    if not parts:
        return []
    chip = 2 * lax.axis_index("x") + lax.axis_index("y")
    return [lax.dynamic_update_slice(l, lax.dynamic_slice_in_dim(p, chip, 1, axis=0), (chip, 0, 0))
            for l, p in zip(landed, parts)]


def _scatter_phases(ins, outs, send_sems, recv_sems):
    n = len(ins)
    x, y, c, chips = _place()
    me = 2 * x + y
    blks = [2 * cx + cy for cx, cy in chips]

    def start():
        for a in range(n):
            for r, nr in _chunks(ins[a].shape[1]):
                for t in range(3):
                    pltpu.make_async_remote_copy(
                        src_ref=ins[a].at[blks[t], pl.ds(r, nr), :], dst_ref=outs[a].at[me, pl.ds(r, nr), :],
                        send_sem=send_sems.at[a, t], recv_sem=recv_sems.at[a, t],
                        device_id=(*chips[t], c), device_id_type=MESH).start()

    def finish():
        for a in range(n):
            for t in range(3):
                pltpu.make_async_remote_copy(
                    src_ref=outs[a].at[blks[t]], dst_ref=outs[a].at[blks[t]], send_sem=send_sems.at[a, t],
                    recv_sem=recv_sems.at[a, t], device_id=(*chips[t], c), device_id_type=MESH).wait()

    return start, finish


def pair_join_halves(halves):
    n = len(halves)

    def body(*refs):
        ins, outs = refs[:n], refs[n:2 * n]
        send_sems, recv_sems = refs[2 * n:]
        x, y, c, _ = _place()
        sibling = (x, y, 1 - c)
        for a in range(n):
            rows = ins[a].shape[0]
            for r, nr in _chunks(rows):
                pltpu.make_async_remote_copy(
                    src_ref=ins[a].at[pl.ds(r, nr), :], dst_ref=outs[a].at[pl.ds(c * rows + r, nr), :],
                    send_sem=send_sems.at[a], recv_sem=recv_sems.at[a], device_id=sibling, device_id_type=MESH).start()
        for a in range(n):
            rows = ins[a].shape[0]
            got = outs[a].at[pl.ds((1 - c) * rows, rows), :]
            pltpu.make_async_remote_copy(src_ref=ins[a], dst_ref=got, send_sem=send_sems.at[a], recv_sem=recv_sems.at[a],
                                         device_id=sibling, device_id_type=MESH).wait()

    joined = pl.pallas_call(
        body, name="pair_join_halves", in_specs=[ANY] * n, out_specs=[ANY] * n,
        out_shape=[jax.ShapeDtypeStruct((2 * h.shape[0], h.shape[1]), h.dtype) for h in halves],
        scratch_shapes=[pltpu.SemaphoreType.DMA((n,)), pltpu.SemaphoreType.DMA((n,))],
    )(*halves)
    c = lax.axis_index("c")
    return [lax.dynamic_update_slice(j, h, (c * h.shape[0], 0)) for j, h in zip(joined, halves)]


def allreduce_small(packed):
    rows = packed.shape[0]

    def body(in_ref, out_ref, gathered, send_sems, recv_sems):
        x, y, c, _ = _place()
        me = 4 * x + 2 * y + c
        gathered[me] = in_ref[...]
        flips = [(fx, fy, fc) for fx in (0, 1) for fy in (0, 1) for fc in (0, 1)][1:]
        peers = [((1 - x) if fx else x, (1 - y) if fy else y, (1 - c) if fc else c) for fx, fy, fc in flips]
        copies = []
        for t, peer in enumerate(peers):
            cp = pltpu.make_async_remote_copy(
                src_ref=in_ref, dst_ref=gathered.at[me], send_sem=send_sems.at[t], recv_sem=recv_sems.at[t],
                device_id=peer, device_id_type=MESH)
            cp.start()
            copies.append(cp)
        for t, (px, py, pc) in enumerate(peers):
            slot = gathered.at[4 * px + 2 * py + pc]
            pltpu.make_async_remote_copy(
                src_ref=slot, dst_ref=slot, send_sem=send_sems.at[t], recv_sem=recv_sems.at[t],
                device_id=(px, py, pc), device_id_type=MESH).wait_recv()
        for cp in copies:
            cp.wait_send()
        acc = gathered[0]
        for k in range(1, 8):
            acc = acc + gathered[k]
        out_ref[...] = acc

    vm = pl.BlockSpec(memory_space=pltpu.VMEM)
    return pl.pallas_call(
        body, name="allreduce_small", in_specs=[vm], out_specs=vm, out_shape=jax.ShapeDtypeStruct(packed.shape, F32),
        scratch_shapes=[pltpu.VMEM((8, rows, 128), F32), pltpu.SemaphoreType.DMA((7,)), pltpu.SemaphoreType.DMA((7,))],
    )(packed)


SMALL_NAMES = ("norm_mix_w", "ssd_conv_w", "ssd_conv_b", "ssd_dt_bias", "ssd_a_log", "ssd_d", "ssd_norm_w", "fox_f_bias",
               "fox_q_norm_w", "fox_k_norm_w", "norm_ffn_w", "ffn_conv_w", "ffn_conv_b")
BIG_NAMES = ("w_in", "w_out", "w_up", "w_down")
WEIGHT_ORDER = ("norm_mix_w", "w_in", "ssd_conv_w", "ssd_conv_b", "ssd_dt_bias", "ssd_a_log", "ssd_d", "ssd_norm_w",
                "fox_f_bias", "fox_q_norm_w", "fox_k_norm_w", "w_out", "norm_ffn_w", "w_up", "ffn_conv_w", "ffn_conv_b", "w_down")
ADAM_ROWS = {"w_in": 256, "w_out": 256, "w_up": 256, "w_down": 176}


def _pack(arrays):
    rows = []
    for a in arrays:
        flat = a.reshape(-1).astype(F32)
        rows.append(jnp.pad(flat, (0, (-flat.shape[0]) % 1024)).reshape(-1, 128))
    return jnp.concatenate(rows, axis=0)


def _unpack(packed, shapes):
    out, r = [], 0
    for shp in shapes:
        size = 1
        for d in shp:
            size *= d
        nrow = 8 * (-(-size // 1024))
        out.append(packed[r:r + nrow].reshape(-1)[:size].reshape(shp))
        r += nrow
    return out


def _pad_rows(a, rows):
    return jnp.pad(a, ((0, rows - a.shape[0]), (0, 0)))


def kernel(x, norm_mix_w, w_in, ssd_conv_w, ssd_conv_b, ssd_dt_bias, ssd_a_log, ssd_d, ssd_norm_w, fox_f_bias, fox_q_norm_w, fox_k_norm_w, w_out, norm_ffn_w, w_up, ffn_conv_w, ffn_conv_b, w_down, loss_target, m_norm_mix_w, m_w_in, m_ssd_conv_w, m_ssd_conv_b, m_ssd_dt_bias, m_ssd_a_log, m_ssd_d, m_ssd_norm_w, m_fox_f_bias, m_fox_q_norm_w, m_fox_k_norm_w, m_w_out, m_norm_ffn_w, m_w_up, m_ffn_conv_w, m_ffn_conv_b, m_w_down, v_norm_mix_w, v_w_in, v_ssd_conv_w, v_ssd_conv_b, v_ssd_dt_bias, v_ssd_a_log, v_ssd_d, v_ssd_norm_w, v_fox_f_bias, v_fox_q_norm_w, v_fox_k_norm_w, v_w_out, v_norm_ffn_w, v_w_up, v_ffn_conv_w, v_ffn_conv_b, v_w_down):
    w = dict(norm_mix_w=norm_mix_w, w_in=w_in, ssd_conv_w=ssd_conv_w, ssd_conv_b=ssd_conv_b, ssd_dt_bias=ssd_dt_bias,
             ssd_a_log=ssd_a_log, ssd_d=ssd_d, ssd_norm_w=ssd_norm_w, fox_f_bias=fox_f_bias, fox_q_norm_w=fox_q_norm_w,
             fox_k_norm_w=fox_k_norm_w, w_out=w_out, norm_ffn_w=norm_ffn_w, w_up=w_up, ffn_conv_w=ffn_conv_w,
             ffn_conv_b=ffn_conv_b, w_down=w_down)
    m = dict(norm_mix_w=m_norm_mix_w, w_in=m_w_in, ssd_conv_w=m_ssd_conv_w, ssd_conv_b=m_ssd_conv_b, ssd_dt_bias=m_ssd_dt_bias,
             ssd_a_log=m_ssd_a_log, ssd_d=m_ssd_d, ssd_norm_w=m_ssd_norm_w, fox_f_bias=m_fox_f_bias, fox_q_norm_w=m_fox_q_norm_w,
             fox_k_norm_w=m_fox_k_norm_w, w_out=m_w_out, norm_ffn_w=m_norm_ffn_w, w_up=m_w_up, ffn_conv_w=m_ffn_conv_w,
             ffn_conv_b=m_ffn_conv_b, w_down=m_w_down)
    v = dict(norm_mix_w=v_norm_mix_w, w_in=v_w_in, ssd_conv_w=v_ssd_conv_w, ssd_conv_b=v_ssd_conv_b, ssd_dt_bias=v_ssd_dt_bias,
             ssd_a_log=v_ssd_a_log, ssd_d=v_ssd_d, ssd_norm_w=v_ssd_norm_w, fox_f_bias=v_fox_f_bias, fox_q_norm_w=v_fox_q_norm_w,
             fox_k_norm_w=v_fox_k_norm_w, w_out=v_w_out, norm_ffn_w=v_norm_ffn_w, w_up=v_w_up, ffn_conv_w=v_ffn_conv_w,
             ffn_conv_b=v_ffn_conv_b, w_down=v_w_down)
    chip = 2 * lax.axis_index("x") + lax.axis_index("y")

    a_in, a_scw, a_fcw = gather_weights([w_in[0].astype(BF16), _pad_rows(ssd_conv_w[0], 16), _pad_rows(ffn_conv_w[0], 16)])
    later_shards = [w_out[0].astype(BF16), w_up[0].astype(BF16), w_down[0].astype(BF16)]
    w_full = a_in.transpose(1, 0, 2).reshape(D_MODEL, IN_COLS)
    wm = jnp.concatenate([w_full[:, :2048], w_full[:, 2576:5648], w_full[:, 2048:2560]], axis=1)
    ws = jnp.concatenate([w_full[:, 2560:2576], w_full[:, 5648:5664], jnp.zeros((D_MODEL, SMALL_COLS - 32), BF16)], axis=1)
    ssd_cw8 = a_scw.transpose(1, 0, 2).reshape(16, 1536)[:8]
    ffn_cw8 = a_fcw.transpose(1, 0, 2).reshape(16, 2 * D_FF)[:8]
    smallp = jnp.zeros((8, 128), F32)
    smallp = smallp.at[0, :16].set(ssd_dt_bias[0]).at[1, :16].set(ssd_a_log[0]).at[2, :16].set(ssd_d[0])
    smallp = smallp.at[3, F_LANE:F_LANE + 16].set(fox_f_bias[0])
    qw_t = jnp.tile(fox_q_norm_w[0], N_HEADS)[None]
    kw_t = jnp.tile(fox_k_norm_w[0], N_HEADS)[None]
    sel = (jnp.arange(1024)[:, None] // HEAD_DIM == jnp.arange(128)[None, :]).astype(BF16)

    res = local_step(x[0], loss_target[0], wm, ws, later_shards, ssd_cw8, ssd_conv_b, smallp, ssd_norm_w, qw_t, kw_t,
                     sel, sel.T, norm_mix_w, norm_ffn_w, ffn_cw8, ffn_conv_b)

    full_shapes = [(1, 1024), (1, 4, 1536), (1, 1536), (1, 16), (1, 16), (1, 16), (1, 1024), (1, 16), (1, 64), (1, 64),
                   (1, 1024), (1, 3, 2 * D_FF), (1, 2 * D_FF), (1,)]
    local_small = [res["g_norm_mix"], res["g_ssd_cw"][:4], res["g_ssd_cw"][4], res["g_sp"][0, :16], res["g_sp"][1, :16],
                   res["g_sp"][2, :16], res["g_ssd_nw"], res["g_fb"][0, F_LANE:F_LANE + 16],
                   res["g_qw"].reshape(N_HEADS, HEAD_DIM).sum(0), res["g_kw"].reshape(N_HEADS, HEAD_DIM).sum(0),
                   res["g_norm_ffn"], res["g_ffn_cw"][:3], res["g_ffn_cw"][3], jnp.sum(res["sq"])]
    summed = _unpack(allreduce_small(_pack(local_small)), full_shapes)
    loss = (0.5 / D_MODEL) * summed[-1][0]
    g_small = dict(zip(SMALL_NAMES, summed[:-1]))
    g_small["ssd_conv_w"] = lax.dynamic_slice(g_small["ssd_conv_w"], (0, 0, 384 * chip), (1, 4, 384))
    g_small["ffn_conv_w"] = lax.dynamic_slice(g_small["ffn_conv_w"], (0, 0, 1408 * chip), (1, 3, 1408))

    landed = res["landed"]
    halves = [sum_chips(p, name="sum_chips_" + n, tr=ADAM_ROWS[n]) for p, n in zip(landed, BIG_NAMES)]
    g_big = dict(zip(BIG_NAMES, pair_join_halves(halves)))

    grads, deltas, new_m, new_v = {}, {}, {}, {}
    for n in BIG_NAMES:
        d, mn, vn = adamw(w[n][0], g_big[n], m[n][0], v[n][0], name="adamw_" + n, tr=ADAM_ROWS[n])
        grads[n], deltas[n], new_m[n], new_v[n] = g_big[n][None], d[None], mn[None], vn[None]
    shapes = [w[n].shape for n in SMALL_NAMES]
    d, mn, vn = adamw(_pack([w[n] for n in SMALL_NAMES]), _pack([g_small[n] for n in SMALL_NAMES]),
                      _pack([m[n] for n in SMALL_NAMES]), _pack([v[n] for n in SMALL_NAMES]), name="adamw_small", tr=8)
    for n, dd, mm, vv in zip(SMALL_NAMES, _unpack(d, shapes), _unpack(mn, shapes), _unpack(vn, shapes)):
        grads[n], deltas[n], new_m[n], new_v[n] = g_small[n].reshape(w[n].shape), dd, mm, vv
    return (loss, res["grad_x"][None], *[grads[n] for n in WEIGHT_ORDER], *[deltas[n] for n in WEIGHT_ORDER],
            *[new_m[n] for n in WEIGHT_ORDER], *[new_v[n] for n in WEIGHT_ORDER])
```

```python
import functools

import jax
import jax.numpy as jnp
import numpy as np
from jax import lax
from jax.experimental import pallas as pl
from jax.experimental.pallas import tpu as pltpu

F32 = jnp.float32
BF16 = jnp.bfloat16
MESH = pl.DeviceIdType.MESH

D_MODEL = 1024
HEAD_DIM = 64
N_HEADS = 16
N_PAIRS = N_HEADS // 2
SSD_CHUNK = 128
SSD_STATE = 128
SSD_CONV = 4
D_FF = 2816
FFN_CONV = 3
NORM_EPS = 1e-6
MAIN_COLS = 5632
SMALL_COLS = 128
F_LANE = 16
IN_COLS = 5664

ADAM_LR = 0.001
ADAM_B1 = 0.9
ADAM_B2 = 0.999
ADAM_EPS = 1e-08
ADAM_WD = 0.01
ADAM_STEP = 10

VMEM_LIMIT_V7X = 56 * 1024 * 1024
NEG_BIG = -1e30


def _params(sem=None):
    return pltpu.CompilerParams(dimension_semantics=sem, vmem_limit_bytes=VMEM_LIMIT_V7X)


def _sigmoid(x):
    return 1.0 / (1.0 + jnp.exp(-x))


def _silu_and_grad(x):
    s = _sigmoid(x)
    return x * s, s * (1.0 + x * (1.0 - s))


def _shift_down(v, j):
    return v if j == 0 else pltpu.roll(v, j, 0)


def _shift_up(v, j):
    return v if j == 0 else pltpu.roll(v, v.shape[0] - j, 0)


def _row_iota(shape):
    return lax.broadcasted_iota(jnp.int32, shape, 0)


def _lane_iota(shape):
    return lax.broadcasted_iota(jnp.int32, shape, 1)


def _dot(a, b, mode="nn"):
    dims = {"nn": (((1,), (0,)), ((), ())), "nt": (((1,), (1,)), ((), ())), "tn": (((0,), (0,)), ((), ()))}[mode]
    return lax.dot_general(a.astype(BF16), b.astype(BF16), dims, preferred_element_type=F32)


def _dot_f32(a, b):
    return jnp.dot(a, b, precision=lax.Precision.HIGHEST, preferred_element_type=F32)


def matmul(a, b, *, mode, tm, tn, tk, out_dtype, name, add=None, b_koff=0, scatter=()):
    (m, k), n = a.shape, (b.shape[1] if mode == "nn" else b.shape[0])
    assert m % tm == 0 and n % tn == 0 and k % tk == 0, (name, m, n, k, tm, tn, tk)
    nk = k // tk
    grid = (m // tm, n // tn, nk)
    a_spec = pl.BlockSpec((tm, tk), lambda i, j, kk: (i, kk))
    b_spec = (pl.BlockSpec((tn, tk), lambda i, j, kk: (j, kk + b_koff)) if mode == "nt"
              else pl.BlockSpec((tk, tn), lambda i, j, kk: (kk + b_koff, j)))
    o_spec = pl.BlockSpec((tm, tn), lambda i, j, kk: (i, j))
    has_add = add is not None
    n_in = 3 if has_add else 2
    ns = len(scatter)

    def body(*refs):
        a_ref, b_ref = refs[:2]
        add_ref = refs[2] if has_add else None
        o_ref, acc_ref = refs[n_in + ns], refs[n_in + 2 * ns + 1]
        kk = pl.program_id(2)
        if ns:
            step = (pl.program_id(0) * grid[1] + pl.program_id(1)) * grid[2] + kk
            start, finish_copies = _scatter_phases(refs[n_in:n_in + ns], refs[n_in + ns + 1:n_in + 2 * ns + 1],
                                                   *refs[n_in + 2 * ns + 2:])
            pl.when(step == 0)(start)
        part = _dot(a_ref[...], b_ref[...], mode)

        def finish(total):
            if has_add:
                total = total + add_ref[...]
            o_ref[...] = total.astype(out_dtype)

        if nk == 1:
            finish(part)
        else:
            @pl.when(kk == 0)
            def _():
                acc_ref[...] = part

            @pl.when(jnp.logical_and(kk > 0, kk < nk - 1))
            def _():
                acc_ref[...] += part

            @pl.when(kk == nk - 1)
            def _():
                finish(acc_ref[...] + part)

        if ns:
            pl.when(step == grid[0] * grid[1] * grid[2] - 1)(finish_copies)

    in_specs = [a_spec, b_spec] + ([o_spec] if has_add else [])
    args = (a, b) + ((add,) if has_add else ())
    acc = pltpu.VMEM((tm, tn) if nk > 1 else (8, 128), F32)
    if not ns:
        return pl.pallas_call(
            body, name=name, grid=grid, in_specs=in_specs, out_specs=o_spec, out_shape=jax.ShapeDtypeStruct((m, n), out_dtype),
            scratch_shapes=[acc], compiler_params=_params(("parallel", "parallel", "arbitrary")),
        )(*args)
    outs = pl.pallas_call(
        body, name=name, grid=grid, in_specs=in_specs + [ANY] * ns, out_specs=[o_spec] + [ANY] * ns,
        out_shape=[jax.ShapeDtypeStruct((m, n), out_dtype)] + [jax.ShapeDtypeStruct(p.shape, p.dtype) for p in scatter],
        scratch_shapes=[acc] + _scatter_scratch(ns), compiler_params=_params(("arbitrary", "arbitrary", "arbitrary")),
    )(*args, *scatter)
    return outs[0], _keep_own_blocks(outs[1:], scatter)


def rms_fwd(x, w, *, name, tm=512):
    s, d = x.shape

    def body(x_ref, w_ref, h_ref, ht_ref):
        xv = x_ref[...]
        r = lax.rsqrt(jnp.mean(xv * xv, axis=-1, keepdims=True) + NORM_EPS)
        h = (xv * r) * w_ref[...]
        h_ref[...] = h.astype(BF16)
        ht_ref[...] = h.T.astype(BF16)

    return pl.pallas_call(
        body, name=name, grid=(s // tm,),
        in_specs=[pl.BlockSpec((tm, d), lambda i: (i, 0)), pl.BlockSpec((1, d), lambda i: (0, 0))],
        out_specs=[pl.BlockSpec((tm, d), lambda i: (i, 0)), pl.BlockSpec((d, tm), lambda i: (0, i))],
        out_shape=[jax.ShapeDtypeStruct((s, d), BF16), jax.ShapeDtypeStruct((d, s), BF16)],
        compiler_params=_params(("parallel",)),
    )(x, w)


def rms_bwd(dh, x, w, resid, *, name, tm=512):
    s, d = x.shape

    def body(dh_ref, x_ref, w_ref, res_ref, dx_ref, dw_ref):
        xv = x_ref[...]
        dhv = dh_ref[...]
        r = lax.rsqrt(jnp.mean(xv * xv, axis=-1, keepdims=True) + NORM_EPS)
        xh = xv * r
        g = dhv * w_ref[...]
        dx_ref[...] = res_ref[...] + r * (g - xh * jnp.mean(g * xh, axis=-1, keepdims=True))
        part = jnp.sum(dhv * xh, axis=0, keepdims=True)

        @pl.when(pl.program_id(0) == 0)
        def _():
            dw_ref[...] = part

        @pl.when(pl.program_id(0) > 0)
        def _():
            dw_ref[...] += part

    row = pl.BlockSpec((tm, d), lambda i: (i, 0))
    vec = pl.BlockSpec((1, d), lambda i: (0, 0))
    return pl.pallas_call(
        body, name=name, grid=(s // tm,), in_specs=[row, row, vec, row], out_specs=[row, vec],
        out_shape=[jax.ShapeDtypeStruct((s, d), F32), jax.ShapeDtypeStruct((1, d), F32)],
        compiler_params=_params(("arbitrary",)),
    )(dh, x, w, resid)


def loss_head(y, target, *, tm=512):
    s, d = y.shape

    def body(y_ref, t_ref, dy_ref, sq_ref):
        e = y_ref[...] - t_ref[...]
        dy_ref[...] = e / float(d)
        part = jnp.sum(e * e, axis=0, keepdims=True)

        @pl.when(pl.program_id(0) == 0)
        def _():
            sq_ref[...] = part

        @pl.when(pl.program_id(0) > 0)
        def _():
            sq_ref[...] += part

    row = pl.BlockSpec((tm, d), lambda i: (i, 0))
    vec = pl.BlockSpec((1, d), lambda i: (0, 0))
    return pl.pallas_call(
        body, name="loss_head", grid=(s // tm,), in_specs=[row, row], out_specs=[row, vec],
        out_shape=[jax.ShapeDtypeStruct((s, d), F32), jax.ShapeDtypeStruct((1, d), F32)],
        compiler_params=_params(("arbitrary",)),
    )(y, target)


def _row_shifts(ext, k_taps):
    return [_shift_down(ext, j) for j in range(k_taps)]


def _conv_rows(shifts, w):
    k_taps = len(shifts)
    acc = w[k_taps - 1:k_taps, :] * shifts[0]
    for k in range(k_taps - 1):
        acc = acc + w[k:k + 1, :] * shifts[k_taps - 1 - k]
    return acc


def _conv_weight_grad(dcur, shifts, rows, width):
    k_taps = len(shifts)
    out = [jnp.sum(dcur * shifts[k_taps - 1 - k][rows], axis=0, keepdims=True) for k in range(k_taps)]
    out.append(jnp.sum(dcur, axis=0, keepdims=True))
    return _stack_rows(out, width)


def _conv_rows_transposed(dext, w, k_taps):
    acc = w[k_taps - 1:k_taps, :] * dext
    for k in range(k_taps - 1):
        acc = acc + w[k:k + 1, :] * _shift_up(dext, k_taps - 1 - k)
    return acc


def _stack_rows(rows, width):
    ri = _row_iota((8, width))
    out = jnp.zeros((8, width), F32)
    for k, r in enumerate(rows):
        out = out + jnp.where(ri == k, r, 0.0)
    return out


def ffn_mid_fwd(hu, conv_w8, conv_b, *, tm=512, tc=256):
    s = hu.shape[0]
    ncol = D_FF // tc
    r8 = tm // 8

    def body(g_ref, v_ref, gp_ref, vp_ref, wg_ref, wv_ref, bg_ref, bv_ref, o_ref, ot_ref):
        first = pl.program_id(1) == 0

        def conv(cur_ref, prev_ref, w_ref, b_ref):
            prev = jnp.where(first, 0.0, prev_ref[...])
            ext = jnp.concatenate([prev, cur_ref[...]], axis=0)
            return _conv_rows(_row_shifts(ext, FFN_CONV), w_ref[...])[8:] + b_ref[...]

        gc = conv(g_ref, gp_ref, wg_ref, bg_ref)
        vc = conv(v_ref, vp_ref, wv_ref, bv_ref)
        act = gc * _sigmoid(gc) * vc
        o_ref[...] = act.astype(BF16)
        ot_ref[...] = act.T.astype(BF16)

    def prev_idx(i):
        return jnp.maximum(i * r8 - 1, 0)

    in_specs = [
        pl.BlockSpec((tm, tc), lambda j, i: (i, j)),
        pl.BlockSpec((tm, tc), lambda j, i: (i, j + ncol)),
        pl.BlockSpec((8, tc), lambda j, i: (prev_idx(i), j)),
        pl.BlockSpec((8, tc), lambda j, i: (prev_idx(i), j + ncol)),
        pl.BlockSpec((8, tc), lambda j, i: (0, j)),
        pl.BlockSpec((8, tc), lambda j, i: (0, j + ncol)),
        pl.BlockSpec((1, tc), lambda j, i: (0, j)),
        pl.BlockSpec((1, tc), lambda j, i: (0, j + ncol)),
    ]
    return pl.pallas_call(
        body, name="ffn_mid_fwd", grid=(ncol, s // tm), in_specs=in_specs,
        out_specs=[pl.BlockSpec((tm, tc), lambda j, i: (i, j)), pl.BlockSpec((tc, tm), lambda j, i: (j, i))],
        out_shape=[jax.ShapeDtypeStruct((s, D_FF), BF16), jax.ShapeDtypeStruct((D_FF, s), BF16)],
        compiler_params=_params(("parallel", "parallel")),
    )(hu, hu, hu, hu, conv_w8, conv_w8, conv_b, conv_b)


def ffn_mid_bwd(hu, dact, conv_w8, conv_b, *, tm=512, tc=256):
    s = hu.shape[0]
    ncol = D_FF // tc
    nrow = s // tm
    r8 = tm // 8

    def body(g_ref, v_ref, gp_ref, vp_ref, gn_ref, vn_ref, da_ref, dan_ref, wg_ref, wv_ref, bg_ref, bv_ref,
             dg_ref, dv_ref, wgo_ref, wvo_ref):
        i = pl.program_id(1)
        first = i == 0
        last = i == nrow - 1

        def ext_of(cur_ref, prev_ref, next_ref):
            prev = jnp.where(first, 0.0, prev_ref[...])
            return jnp.concatenate([prev, cur_ref[...], next_ref[...]], axis=0)

        g_sh = _row_shifts(ext_of(g_ref, gp_ref, gn_ref), FFN_CONV)
        v_sh = _row_shifts(ext_of(v_ref, vp_ref, vn_ref), FFN_CONV)
        gc = _conv_rows(g_sh, wg_ref[...]) + bg_ref[...]
        vc = _conv_rows(v_sh, wv_ref[...]) + bv_ref[...]
        da_ext = jnp.concatenate([jnp.zeros((8, tc), F32), da_ref[...], jnp.where(last, 0.0, dan_ref[...])], axis=0)
        silu, dsilu = _silu_and_grad(gc)
        dgc = da_ext * vc * dsilu
        dvc = da_ext * silu
        dg_ref[...] = _conv_rows_transposed(dgc, wg_ref[...], FFN_CONV)[8:8 + tm].astype(BF16)
        dv_ref[...] = _conv_rows_transposed(dvc, wv_ref[...], FFN_CONV)[8:8 + tm].astype(BF16)

        cur = slice(8, 8 + tm)
        pg = _conv_weight_grad(dgc[cur], g_sh, cur, tc)
        pv = _conv_weight_grad(dvc[cur], v_sh, cur, tc)

        @pl.when(first)
        def _():
            wgo_ref[...] = pg
            wvo_ref[...] = pv

        @pl.when(i > 0)
        def _():
            wgo_ref[...] += pg
            wvo_ref[...] += pv

    def prev_idx(i):
        return jnp.maximum(i * r8 - 1, 0)

    def next_idx(i):
        return jnp.minimum((i + 1) * r8, s // 8 - 1)

    cur_g = pl.BlockSpec((tm, tc), lambda j, i: (i, j))
    cur_v = pl.BlockSpec((tm, tc), lambda j, i: (i, j + ncol))
    in_specs = [
        cur_g, cur_v,
        pl.BlockSpec((8, tc), lambda j, i: (prev_idx(i), j)),
        pl.BlockSpec((8, tc), lambda j, i: (prev_idx(i), j + ncol)),
        pl.BlockSpec((8, tc), lambda j, i: (next_idx(i), j)),
        pl.BlockSpec((8, tc), lambda j, i: (next_idx(i), j + ncol)),
        cur_g,
        pl.BlockSpec((8, tc), lambda j, i: (next_idx(i), j)),
        pl.BlockSpec((8, tc), lambda j, i: (0, j)),
        pl.BlockSpec((8, tc), lambda j, i: (0, j + ncol)),
        pl.BlockSpec((1, tc), lambda j, i: (0, j)),
        pl.BlockSpec((1, tc), lambda j, i: (0, j + ncol)),
    ]
    out_specs = [cur_g, cur_g, pl.BlockSpec((8, tc), lambda j, i: (0, j)), pl.BlockSpec((8, tc), lambda j, i: (0, j))]
    out_shape = [jax.ShapeDtypeStruct((s, D_FF), BF16), jax.ShapeDtypeStruct((s, D_FF), BF16),
                 jax.ShapeDtypeStruct((8, D_FF), F32), jax.ShapeDtypeStruct((8, D_FF), F32)]
    return pl.pallas_call(
        body, name="ffn_mid_bwd", grid=(ncol, nrow), in_specs=in_specs, out_specs=out_specs, out_shape=out_shape,
        compiler_params=_params(("parallel", "arbitrary")),
    )(hu, hu, hu, hu, hu, hu, dact, dact, conv_w8, conv_w8, conv_b, conv_b)


def _softplus(x):
    return jnp.maximum(x, 0.0) + jnp.log(1.0 + jnp.exp(-jnp.abs(x)))


def _cumsum_rows(v):
    n = v.shape[0]
    ri = _row_iota(v.shape)
    sh = 1
    while sh < n:
        v = v + jnp.where(ri >= sh, _shift_down(v, sh), 0.0)
        sh *= 2
    return v


def _rev_cumsum_rows(v):
    n = v.shape[0]
    ri = _row_iota(v.shape)
    sh = 1
    while sh < n:
        v = v + jnp.where(ri < n - sh, _shift_up(v, sh), 0.0)
        sh *= 2
    return v


def _total(v):
    return jnp.sum(jnp.sum(v, axis=1, keepdims=True), axis=0, keepdims=True)


def _ssd_in_specs(rev_nc=None):
    def ch(c):
        return c if rev_nc is None else rev_nc - 1 - c

    def prev(c):
        return jnp.maximum(ch(c) * (SSD_CHUNK // 8) - 1, 0)

    L = SSD_CHUNK
    return [
        pl.BlockSpec((L, 1024), lambda c: (ch(c), 0)),
        pl.BlockSpec((L, 1024), lambda c: (ch(c), 1)),
        pl.BlockSpec((L, 256), lambda c: (ch(c), 20)),
        pl.BlockSpec((L, 256), lambda c: (ch(c), 21)),
        pl.BlockSpec((8, 1024), lambda c: (prev(c), 1)),
        pl.BlockSpec((8, 256), lambda c: (prev(c), 20)),
        pl.BlockSpec((8, 256), lambda c: (prev(c), 21)),
        pl.BlockSpec((8, 1024), lambda c: (0, 0)),
        pl.BlockSpec((8, 256), lambda c: (0, 4)),
        pl.BlockSpec((8, 256), lambda c: (0, 5)),
        pl.BlockSpec((1, 1024), lambda c: (0, 0)),
        pl.BlockSpec((1, 256), lambda c: (0, 4)),
        pl.BlockSpec((1, 256), lambda c: (0, 5)),
        pl.BlockSpec((L, SMALL_COLS), lambda c: (ch(c), 0)),
        pl.BlockSpec((8, 128), lambda c: (0, 0)),
        pl.BlockSpec((1, 1024), lambda c: (0, 0)),
    ]


def _ssd_conv_pre(cur_ref, prev_ref, w_ref, b_ref, first):
    prev = jnp.where(first, 0.0, prev_ref[...])
    shifts = _row_shifts(jnp.concatenate([prev, cur_ref[...]], axis=0), SSD_CONV)
    return shifts, _conv_rows(shifts, w_ref[...])[8:] + b_ref[...]


def _ssd_time_consts(small_ref, sp_ref):
    dt_pre = small_ref[...] + sp_ref[0:1, :]
    dt = _softplus(dt_pre)
    a = -jnp.exp(sp_ref[1:2, :])
    acs = _cumsum_rows(dt * a)
    return dt_pre, dt, a, acs


def ssd_fwd(proj, small, conv_w8, conv_b, smallp, norm_w):
    s = proj.shape[0]
    nc = s // SSD_CHUNK
    L = SSD_CHUNK

    def body(z_ref, xs_ref, b_ref, c_ref, xsp_ref, bp_ref, cp_ref, wx_ref, wb_ref, wc_ref, bx_ref, bb_ref, bc_ref,
             small_ref, sp_ref, nw_ref, y_ref, yt_ref, ypre_ref, st_ref, state):
        first = pl.program_id(0) == 0

        @pl.when(first)
        def _():
            state[...] = jnp.zeros_like(state)

        xs = _ssd_conv_pre(xs_ref, xsp_ref, wx_ref, bx_ref, first)[1]
        xs = xs * _sigmoid(xs)
        bm = _ssd_conv_pre(b_ref, bp_ref, wb_ref, bb_ref, first)[1]
        bm = bm * _sigmoid(bm)
        cm = _ssd_conv_pre(c_ref, cp_ref, wc_ref, bc_ref, first)[1]
        cm = cm * _sigmoid(cm)
        _, dt, _, acs = _ssd_time_consts(small_ref, sp_ref)
        acs_t = acs.T
        li = _lane_iota((L, L))
        ri = _row_iota((L, L))
        tri = ri >= li
        lo = li < HEAD_DIM
        st_ref[0] = state[...]
        for g in range(2):
            bg = bm[:, 128 * g:128 * g + 128]
            cg = cm[:, 128 * g:128 * g + 128]
            gmat = _dot(cg, bg, "nt")
            for pp in range(4):
                p = 4 * g + pp
                h0, h1 = 2 * p, 2 * p + 1
                x = xs[:, 128 * p:128 * p + 128]
                a0, a1 = acs[:, h0:h0 + 1], acs[:, h1:h1 + 1]
                xdt = x * jnp.where(lo, dt[:, h0:h0 + 1], dt[:, h1:h1 + 1])
                m0 = gmat * jnp.exp(jnp.where(tri, a0 - acs_t[h0:h0 + 1, :], NEG_BIG))
                m1 = gmat * jnp.exp(jnp.where(tri, a1 - acs_t[h1:h1 + 1, :], NEG_BIG))
                yd = _dot(m0, jnp.where(lo, xdt, 0.0)) + _dot(m1, jnp.where(lo, 0.0, xdt))
                hin = state[p]
                yo = _dot(cg, hin, "nt") * jnp.exp(jnp.where(lo, a0, a1))
                dskip = jnp.where(lo[0:1], sp_ref[2:3, h0:h0 + 1], sp_ref[2:3, h1:h1 + 1])
                ypre_ref[:, 128 * p:128 * p + 128] = yd + yo + dskip * x
                al0, al1 = acs[L - 1:L, h0:h0 + 1], acs[L - 1:L, h1:h1 + 1]
                w = jnp.exp(jnp.where(lo, al0 - a0, al1 - a1))
                dec = jnp.exp(jnp.where(ri < HEAD_DIM, al0, al1))
                state[p] = dec * hin + _dot(xdt * w, bg, "tn")
        z = z_ref[...]
        yg = ypre_ref[...] * (z * _sigmoid(z))
        for g in range(2):
            seg = yg[:, 512 * g:512 * g + 512]
            r = lax.rsqrt(jnp.mean(seg * seg, axis=-1, keepdims=True) + NORM_EPS)
            out = (seg * r) * nw_ref[:, 512 * g:512 * g + 512]
            y_ref[:, 512 * g:512 * g + 512] = out.astype(BF16)
            yt_ref[512 * g:512 * g + 512, :] = out.T.astype(BF16)

    row = pl.BlockSpec((L, 1024), lambda c: (c, 0))
    return pl.pallas_call(
        body, name="ssd_fwd", grid=(nc,), in_specs=_ssd_in_specs(),
        out_specs=[row, pl.BlockSpec((1024, L), lambda c: (0, c)), row,
                   pl.BlockSpec((1, N_PAIRS, 128, 128), lambda c: (c, 0, 0, 0))],
        out_shape=[jax.ShapeDtypeStruct((s, 1024), BF16), jax.ShapeDtypeStruct((1024, s), BF16),
                   jax.ShapeDtypeStruct((s, 1024), F32), jax.ShapeDtypeStruct((nc, N_PAIRS, 128, 128), F32)],
        scratch_shapes=[pltpu.VMEM((N_PAIRS, 128, 128), F32)],
        compiler_params=_params(("arbitrary",)),
    )(proj, proj, proj, proj, proj, proj, proj, conv_w8, conv_w8, conv_w8, conv_b, conv_b, conv_b, small, smallp, norm_w)


def ssd_bwd(proj, small, conv_w8, conv_b, smallp, norm_w, ypre, states, dy, sel):
    s = proj.shape[0]
    nc = s // SSD_CHUNK
    L = SSD_CHUNK

    def body(z_ref, xs_ref, b_ref, c_ref, xsp_ref, bp_ref, cp_ref, wx_ref, wb_ref, wc_ref, bx_ref, bb_ref, bc_ref,
             small_ref, sp_ref, nw_ref, ypre_ref, st_ref, dy_ref, sel_ref,
             dz_ref, dxs_ref, db_ref, dc_ref, dsmall_ref, gwx_ref, gwb_ref, gwc_ref, gsp_ref, gnw_ref,
             dstate, carry_x, carry_b, carry_c, dxs_buf, dbm_buf, dcm_buf, qcs, col_sums, acs_terms, dt_terms):
        step = pl.program_id(0)
        col_sums[...] = jnp.zeros_like(col_sums)
        first_chunk = step == nc - 1
        start = step == 0

        @pl.when(start)
        def _():
            dstate[...] = jnp.zeros_like(dstate)
            carry_x[...] = jnp.zeros_like(carry_x)
            carry_b[...] = jnp.zeros_like(carry_b)
            carry_c[...] = jnp.zeros_like(carry_c)

        xs_ext, xs_pre = _ssd_conv_pre(xs_ref, xsp_ref, wx_ref, bx_ref, first_chunk)
        b_ext, b_pre = _ssd_conv_pre(b_ref, bp_ref, wb_ref, bb_ref, first_chunk)
        c_ext, c_pre = _ssd_conv_pre(c_ref, cp_ref, wc_ref, bc_ref, first_chunk)
        xs, xs_ds = _silu_and_grad(xs_pre)
        bm, b_ds = _silu_and_grad(b_pre)
        cm, c_ds = _silu_and_grad(c_pre)
        dt_pre, dt, a, acs = _ssd_time_consts(small_ref, sp_ref)
        acs_t = acs.T
        li = _lane_iota((L, L))
        ri = _row_iota((L, L))
        tri = ri >= li
        lo = li < HEAD_DIM
        lo_rows = ri < HEAD_DIM
        li1 = _lane_iota((1, L))

        z = z_ref[...]
        sz, dsz = _silu_and_grad(z)
        y = ypre_ref[...]
        yg = y * sz
        dout = dy_ref[...]
        dyg_parts = []
        gnw_parts = []
        for g in range(2):
            sl = slice(512 * g, 512 * g + 512)
            seg = yg[:, sl]
            r = lax.rsqrt(jnp.mean(seg * seg, axis=-1, keepdims=True) + NORM_EPS)
            n = seg * r
            gnw_parts.append(jnp.sum(dout[:, sl] * n, axis=0, keepdims=True))
            gg = dout[:, sl] * nw_ref[:, sl]
            dyg_parts.append(r * (gg - n * jnp.mean(gg * n, axis=-1, keepdims=True)))
        dyg = jnp.concatenate(dyg_parts, axis=1)
        gnw = jnp.concatenate(gnw_parts, axis=1)
        dz_ref[...] = (dyg * y * dsz).astype(BF16)
        dypre = dyg * sz

        qcs[...] = jnp.zeros_like(qcs)
        dalast = jnp.zeros((1, L), F32)
        for g in range(2):
            bg = bm[:, 128 * g:128 * g + 128]
            cg = cm[:, 128 * g:128 * g + 128]
            gmat = _dot(cg, bg, "nt")
            dgmat = jnp.zeros((L, L), F32)
            dbg = jnp.zeros((L, L), F32)
            dcg = jnp.zeros((L, L), F32)
            for pp in range(4):
                p = 4 * g + pp
                h0, h1 = 2 * p, 2 * p + 1
                lanes = slice(128 * p, 128 * p + 128)
                x = xs[:, lanes]
                dyp = dypre[:, lanes]
                a0, a1 = acs[:, h0:h0 + 1], acs[:, h1:h1 + 1]
                dtl = jnp.where(lo, dt[:, h0:h0 + 1], dt[:, h1:h1 + 1])
                xdt = x * dtl
                l0 = jnp.exp(jnp.where(tri, a0 - acs_t[h0:h0 + 1, :], NEG_BIG))
                l1 = jnp.exp(jnp.where(tri, a1 - acs_t[h1:h1 + 1, :], NEG_BIG))
                m0, m1 = gmat * l0, gmat * l1
                dskip = jnp.where(lo[0:1], sp_ref[2:3, h0:h0 + 1], sp_ref[2:3, h1:h1 + 1])
                col_sums[0:1, lanes] = jnp.sum(dyp * x, axis=0, keepdims=True)
                dx = dyp * dskip
                dy0, dy1 = jnp.where(lo, dyp, 0.0), jnp.where(lo, 0.0, dyp)
                x0, x1 = jnp.where(lo, xdt, 0.0), jnp.where(lo, 0.0, xdt)
                dm0, dm1 = _dot(dy0, x0, "nt"), _dot(dy1, x1, "nt")
                dxdt = _dot(m0, dy0, "tn") + _dot(m1, dy1, "tn")
                q0, q1 = dm0 * m0, dm1 * m1
                qcs[h0:h0 + 1, :] = jnp.sum(q0, axis=0, keepdims=True)
                qcs[h1:h1 + 1, :] = jnp.sum(q1, axis=0, keepdims=True)
                row_terms = jnp.where(lo, q0 + pltpu.roll(q0, HEAD_DIM, 1), q1 + pltpu.roll(q1, HEAD_DIM, 1))
                dgmat = dgmat + dm0 * l0 + dm1 * l1
                hin = st_ref[0, p]
                e = jnp.exp(jnp.where(lo, a0, a1))
                ch = _dot(cg, hin, "nt")
                dch = dyp * e
                dcg = dcg + _dot(dch, hin)
                dhin = _dot(dch, cg, "tn")
                dhout = dstate[p]
                al0, al1 = acs[L - 1:L, h0:h0 + 1], acs[L - 1:L, h1:h1 + 1]
                dec = jnp.exp(jnp.where(lo_rows, al0, al1))
                dhin = dhin + dec * dhout
                dal = dhout * hin * dec
                dal0 = _total(jnp.where(lo_rows, dal, 0.0))
                dal1 = _total(dal) - dal0
                dalast = dalast + jnp.where(li1 == h0, dal0, 0.0) + jnp.where(li1 == h1, dal1, 0.0)
                w = jnp.exp(jnp.where(lo, al0 - a0, al1 - a1))
                xw = xdt * w
                dxw = _dot(bg, dhout, "nt")
                dbg = dbg + _dot(xw, dhout)
                dxdt = dxdt + dxw * w
                dww = dxw * xw
                col_sums[1:2, lanes] = jnp.sum(dww, axis=0, keepdims=True)
                acs_terms[:, lanes] = row_terms + dch * ch - dww
                dx = dx + dxdt * dtl
                dt_terms[:, lanes] = dxdt * x
                dxs_buf[:, lanes] = dx
                dstate[p] = dhin
            dcg = dcg + _dot(dgmat, bg)
            dbg = dbg + _dot(dgmat, cg, "tn")
            dbm_buf[:, 128 * g:128 * g + 128] = dbg
            dcm_buf[:, 128 * g:128 * g + 128] = dcg

        head_sums = _split3_dot(col_sums[...], sel_ref[...])
        dskip_g = head_sums[0:1, :]
        dalast = dalast + head_sums[1:2, :]
        ddt = _split3_dot(dt_terms[...], sel_ref[...])
        dacs_tot = _split3_dot(acs_terms[...], sel_ref[...]) - qcs[...].T + jnp.where(ri == L - 1, dalast, 0.0)
        dstep = _rev_cumsum_rows(dacs_tot)
        ddt = ddt + dstep * a
        head_lane = li < N_HEADS
        ddt_pre = jnp.where(head_lane, ddt * _sigmoid(dt_pre), 0.0)
        dsmall_ref[...] = ddt_pre
        da = jnp.sum(jnp.where(head_lane, dstep * dt, 0.0), axis=0, keepdims=True)
        gsp = _stack_rows([jnp.sum(ddt_pre, axis=0, keepdims=True), da * a, dskip_g], L)

        def conv_back(dpost, ds, shifts, w_ref, carry, out_ref, width):
            dpre = dpost * ds
            dext = jnp.concatenate([dpre, carry[...]], axis=0)
            out_ref[...] = _conv_rows_transposed(dext, w_ref[...], SSD_CONV)[:L].astype(BF16)
            carry[...] = dpre[0:8]
            return _conv_weight_grad(dpre, shifts, slice(8, 8 + L), width)

        gwx = conv_back(dxs_buf[...], xs_ds, xs_ext, wx_ref, carry_x, dxs_ref, 1024)
        gwb = conv_back(dbm_buf[...], b_ds, b_ext, wb_ref, carry_b, db_ref, 256)
        gwc = conv_back(dcm_buf[...], c_ds, c_ext, wc_ref, carry_c, dc_ref, 256)

        @pl.when(start)
        def _():
            gwx_ref[...] = gwx
            gwb_ref[...] = gwb
            gwc_ref[...] = gwc
            gsp_ref[...] = gsp
            gnw_ref[...] = gnw

        @pl.when(step > 0)
        def _():
            gwx_ref[...] += gwx
            gwb_ref[...] += gwb
            gwc_ref[...] += gwc
            gsp_ref[...] += gsp
            gnw_ref[...] += gnw

    def ch(c):
        return nc - 1 - c

    row = pl.BlockSpec((L, 1024), lambda c: (ch(c), 0))
    row256 = pl.BlockSpec((L, 256), lambda c: (ch(c), 0))
    in_specs = _ssd_in_specs(rev_nc=nc) + [row, pl.BlockSpec((1, N_PAIRS, 128, 128), lambda c: (ch(c), 0, 0, 0)), row,
                                           pl.BlockSpec((1024, 128), lambda c: (0, 0))]
    out_specs = [row, row, row256, row256, pl.BlockSpec((L, 128), lambda c: (ch(c), 0)),
                 pl.BlockSpec((8, 1024), lambda c: (0, 0)), pl.BlockSpec((8, 256), lambda c: (0, 0)),
                 pl.BlockSpec((8, 256), lambda c: (0, 0)), pl.BlockSpec((8, 128), lambda c: (0, 0)),
                 pl.BlockSpec((1, 1024), lambda c: (0, 0))]
    out_shape = [jax.ShapeDtypeStruct((s, 1024), BF16), jax.ShapeDtypeStruct((s, 1024), BF16),
                 jax.ShapeDtypeStruct((s, 256), BF16), jax.ShapeDtypeStruct((s, 256), BF16),
                 jax.ShapeDtypeStruct((s, 128), F32),
                 jax.ShapeDtypeStruct((8, 1024), F32), jax.ShapeDtypeStruct((8, 256), F32),
                 jax.ShapeDtypeStruct((8, 256), F32), jax.ShapeDtypeStruct((8, 128), F32),
                 jax.ShapeDtypeStruct((1, 1024), F32)]
    scratch = [pltpu.VMEM((N_PAIRS, 128, 128), F32), pltpu.VMEM((8, 1024), F32), pltpu.VMEM((8, 256), F32),
               pltpu.VMEM((8, 256), F32), pltpu.VMEM((L, 1024), F32), pltpu.VMEM((L, 256), F32), pltpu.VMEM((L, 256), F32),
               pltpu.VMEM((L, L), F32), pltpu.VMEM((8, 1024), F32), pltpu.VMEM((L, 1024), F32), pltpu.VMEM((L, 1024), F32)]
    return pl.pallas_call(
        body, name="ssd_bwd", grid=(nc,), in_specs=in_specs, out_specs=out_specs, out_shape=out_shape,
        scratch_shapes=scratch, compiler_params=_params(("arbitrary",)),
    )(proj, proj, proj, proj, proj, proj, proj, conv_w8, conv_w8, conv_w8, conv_b, conv_b, conv_b, small, smallp, norm_w,
      ypre, states, dy, sel)


FOX_SCALE = HEAD_DIM ** -0.5
FOX_T = 256
Q_COL, K_COL, V_COL = 2, 3, 4


def _split3_dot(v, m):
    hi = v.astype(BF16)
    r1 = v - hi.astype(F32)
    mid = r1.astype(BF16)
    lo = (r1 - mid.astype(F32)).astype(BF16)
    return _dot(hi, m) + _dot(mid, m) + _dot(lo, m)


def _head_rstd(x, sel_ref, selt_ref):
    ms = _split3_dot(x * x, sel_ref[...]) * (1.0 / HEAD_DIM)
    return _split3_dot(lax.rsqrt(ms + NORM_EPS), selt_ref[...])


def fox_tables():
    r = np.arange(3 * 128)
    piece, lane = r // 128, r % 128
    head = lane - F_LANE
    is_head = np.logical_and(head >= 0, head < N_HEADS)
    col = 128 * (head // 2) + HEAD_DIM * (1 - head % 2) + piece
    cols = np.arange(1024)
    place_q = np.logical_and(is_head[:, None], cols[None, :] == col[:, None])
    place_k = np.logical_and(is_head[:, None], cols[None, :] == (col + 3)[:, None])
    ones_q = np.logical_and(cols % HEAD_DIM >= 3, cols % HEAD_DIM < 6)[None]
    ones_k = (cols % HEAD_DIM < 3)[None]
    h = np.arange(128) - F_LANE
    ok = np.logical_and(h >= 0, h < N_HEADS)
    same_pair = cols[:, None] // 128 == (h // 2)[None, :]
    fold_even = np.logical_and(np.logical_and(ok, h % 2 == 0)[None, :], same_pair)
    fold_odd = np.logical_and(np.logical_and(ok, h % 2 == 1)[None, :], same_pair)
    as_bf16 = lambda t: jnp.asarray(t.astype(np.float32), BF16)
    return (as_bf16(place_q), as_bf16(place_k), jnp.asarray(ones_q, F32), jnp.asarray(ones_k, F32),
            as_bf16(fold_even), as_bf16(fold_odd))


def fox_prep(proj, small, smallp, qw, kw, sel, selt, place_q, place_k, ones_q, ones_k, *, tm=256):
    s = proj.shape[0]

    def body(q_ref, k_ref, v_ref, small_ref, sp_ref, qw_ref, kw_ref, sel_ref, selt_ref, pq_ref, pk_ref, oq_ref, ok_ref,
             qn_ref, kn_ref, aq_ref, ak_ref, vb_ref, knt_ref, akt_ref, vt_ref, carry):
        @pl.when(pl.program_id(0) == 0)
        def _():
            carry[...] = jnp.zeros_like(carry)

        q = q_ref[...]
        qn_ref[...] = (((q * _head_rstd(q, sel_ref, selt_ref)) * qw_ref[...]) * FOX_SCALE).astype(BF16)
        k = k_ref[...]
        kn = ((k * _head_rstd(k, sel_ref, selt_ref)) * kw_ref[...]).astype(BF16)
        kn_ref[...] = kn
        knt_ref[...] = kn.astype(F32).T.astype(BF16)
        vb_ref[...] = v_ref[...].astype(BF16)
        vt_ref[...] = v_ref[...].T.astype(BF16)
        li = _lane_iota((tm, 128))
        f_lane = jnp.logical_and(li >= F_LANE, li < F_LANE + N_HEADS)
        logf = jnp.where(f_lane, -_softplus(-(small_ref[...] + sp_ref[3:4, :])), 0.0)
        cum = _cumsum_rows(logf) + carry[...]
        carry[...] = cum[tm - 1:tm, :]
        hi = cum.astype(BF16)
        r1 = cum - hi.astype(F32)
        mid = r1.astype(BF16)
        lo = (r1 - mid.astype(F32)).astype(BF16)
        pieces = jnp.concatenate([hi, mid, lo], axis=1)
        aq_ref[...] = (_dot(pieces, pq_ref[...]) + oq_ref[...]).astype(BF16)
        ak = ok_ref[...] - _dot(pieces, pk_ref[...])
        ak_ref[...] = ak.astype(BF16)
        akt_ref[...] = ak.T.astype(BF16)

    row = pl.BlockSpec((tm, 1024), lambda i: (i, 0))
    col = pl.BlockSpec((1024, tm), lambda i: (0, i))
    vec = pl.BlockSpec((1, 1024), lambda i: (0, 0))
    table = pl.BlockSpec((384, 1024), lambda i: (0, 0))
    wide = jax.ShapeDtypeStruct((s, 1024), BF16)
    tall = jax.ShapeDtypeStruct((1024, s), BF16)
    return pl.pallas_call(
        body, name="fox_prep", grid=(s // tm,),
        in_specs=[pl.BlockSpec((tm, 1024), lambda i: (i, Q_COL)), pl.BlockSpec((tm, 1024), lambda i: (i, K_COL)),
                  pl.BlockSpec((tm, 1024), lambda i: (i, V_COL)),
                  pl.BlockSpec((tm, 128), lambda i: (i, 0)), pl.BlockSpec((8, 128), lambda i: (0, 0)), vec, vec,
                  pl.BlockSpec((1024, 128), lambda i: (0, 0)), pl.BlockSpec((128, 1024), lambda i: (0, 0)),
                  table, table, vec, vec],
        out_specs=[row, row, row, row, row, col, col, col],
        out_shape=[wide, wide, wide, wide, wide, tall, tall, tall],
        scratch_shapes=[pltpu.VMEM((1, 128), F32)], compiler_params=_params(("arbitrary",)),
    )(proj, proj, proj, small, smallp, qw, kw, sel, selt, place_q, place_k, ones_q, ones_k)


def fox_fwd(qn, kn, aq, ak, vt, shards=()):
    s = qn.shape[0]
    t = FOX_T
    nq = s // t
    ng = len(shards)

    def body(*refs):
        q_ref, k_ref, aq_ref, ak_ref, vt_ref = refs[:5]
        o_ref, ot_ref, lse_ref = refs[5 + ng:8 + ng]
        p = pl.program_id(0)
        if ng:
            start, forward, finish = _gather_phases(refs[5:5 + ng], refs[8 + ng:8 + 2 * ng], *refs[8 + 2 * ng:])
            pl.when(p == 0)(start)
            pl.when(p == N_PAIRS // 2)(forward)

        @pl.when(p == 0)
        def _():
            lse_ref[...] = jnp.zeros_like(lse_ref)

        lo = _lane_iota((t, 128)) < HEAD_DIM
        lo_rows = _row_iota((128, t)) < HEAD_DIM
        causal_t = _lane_iota((t, t)) >= _row_iota((t, t))

        def q_loop(qi, _):
            q0 = pl.multiple_of(qi * t, t)
            qv, aqv = q_ref[pl.ds(q0, t), :], aq_ref[pl.ds(q0, t), :]
            qa, qb = jnp.where(lo, qv, aqv), jnp.where(lo, aqv, qv)

            def scores(kj):
                k0 = pl.multiple_of(kj * t, t)
                kv, akv = k_ref[pl.ds(k0, t), :], ak_ref[pl.ds(k0, t), :]
                return _dot(jnp.where(lo, kv, akv), qa, "nt"), _dot(jnp.where(lo, akv, kv), qb, "nt")

            def update(kj, stats, s0, s1):
                m0, l0, m1, l1, acc = stats
                vtv = vt_ref[:, pl.ds(pl.multiple_of(kj * t, t), t)]
                n0 = jnp.maximum(m0, jnp.max(s0, axis=0, keepdims=True))
                n1 = jnp.maximum(m1, jnp.max(s1, axis=0, keepdims=True))
                a0, a1 = jnp.exp(m0 - n0), jnp.exp(m1 - n1)
                p0, p1 = jnp.exp(s0 - n0), jnp.exp(s1 - n1)
                l0 = a0 * l0 + jnp.sum(p0, axis=0, keepdims=True)
                l1 = a1 * l1 + jnp.sum(p1, axis=0, keepdims=True)
                acc = (jnp.where(lo_rows, a0, a1) * acc + _dot(jnp.where(lo_rows, vtv, 0.0), p0)
                       + _dot(jnp.where(lo_rows, 0.0, vtv), p1))
                return n0, l0, n1, l1, acc

            def step(kj, carry):
                stats, (s0, s1) = carry[:5], carry[5:]
                nxt = scores(kj + 1)
                return (*update(kj, stats, s0, s1), *nxt)

            def row(val):
                return jnp.full((1, t), val, F32)

            init = (row(NEG_BIG), row(0.0), row(NEG_BIG), row(0.0), jnp.zeros((128, t), F32), *scores(0))
            carry = lax.fori_loop(0, qi, step, init)
            s0, s1 = jnp.where(causal_t, carry[5], NEG_BIG), jnp.where(causal_t, carry[6], NEG_BIG)
            m0, l0, m1, l1, acc = update(qi, carry[:5], s0, s1)
            out_t = acc / jnp.where(lo_rows, l0, l1)
            ot_ref[:, pl.ds(q0, t)] = out_t.astype(BF16)
            o_ref[pl.ds(q0, t), :] = out_t.T.astype(BF16)
            ri = _row_iota((N_HEADS, t))
            old = lse_ref[:, pl.ds(q0, t)]
            lse_ref[:, pl.ds(q0, t)] = jnp.where(
                ri == 2 * p, m0 + jnp.log(l0), jnp.where(ri == 2 * p + 1, m1 + jnp.log(l1), old))
            return 0

        lax.fori_loop(0, nq, q_loop, 0)
        if ng:
            pl.when(p == N_PAIRS - 1)(finish)

    pair = pl.BlockSpec((s, 128), lambda p: (0, p))
    outs = pl.pallas_call(
        body, name="fox_fwd", grid=(N_PAIRS,),
        in_specs=[pair] * 4 + [pl.BlockSpec((128, s), lambda p: (p, 0))] + [ANY] * ng,
        out_specs=[pair, pl.BlockSpec((128, s), lambda p: (p, 0)), pl.BlockSpec((N_HEADS, s), lambda p: (0, 0))] + [ANY] * ng,
        out_shape=[jax.ShapeDtypeStruct((s, 1024), BF16), jax.ShapeDtypeStruct((1024, s), BF16),
                   jax.ShapeDtypeStruct((N_HEADS, s), F32)] + _gather_out_shapes(shards),
        scratch_shapes=_gather_scratch(ng) if ng else [],
        compiler_params=_params(("arbitrary",)),
    )(qn, kn, aq, ak, vt, *shards)
    return outs[0], outs[1], outs[2], list(outs[3:])


def fox_bwd(qn, kn, aq, ak, knt, akt, vb, lse, dmixed, parts=()):
    s = qn.shape[0]
    t = FOX_T
    nq = s // t
    once = pl.Buffered(1)
    ns = len(parts)

    def body(*refs):
        q_ref, k_ref, aq_ref, ak_ref, kt_ref, akt_ref, v_ref, lse_ref, do_ref = refs[:9]
        dq_ref, dk_ref, dv_ref, dc0_ref, dc1_ref = refs[9 + ns:14 + ns]
        p_scr, dp_scr = refs[14 + 2 * ns:16 + 2 * ns]
        p = pl.program_id(0)
        if ns:
            start, finish = _scatter_phases(refs[9:9 + ns], refs[14 + ns:14 + 2 * ns], *refs[16 + 2 * ns:])
            pl.when(p == 0)(start)
        dk_ref[...] = jnp.zeros_like(dk_ref)
        dv_ref[...] = jnp.zeros_like(dv_ref)
        dc0_ref[...] = jnp.zeros_like(dc0_ref)
        dc1_ref[...] = jnp.zeros_like(dc1_ref)
        lo = _lane_iota((t, 128)) < HEAD_DIM
        lo_rows = _row_iota((128, t)) < HEAD_DIM
        causal_t = _lane_iota((t, t)) >= _row_iota((t, t))

        def q_loop(qi, _):
            q0 = pl.multiple_of(qi * t, t)
            qv, aqv = q_ref[pl.ds(q0, t), :], aq_ref[pl.ds(q0, t), :]
            qa, qb = jnp.where(lo, qv, aqv), jnp.where(lo, aqv, qv)
            do = do_ref[pl.ds(q0, t), :]
            doa, dob = jnp.where(lo, do, 0.0).astype(BF16), jnp.where(lo, 0.0, do).astype(BF16)
            lse_blk = lse_ref[:, pl.ds(q0, t)]
            ri = _row_iota((N_HEADS, t))
            lse0 = jnp.sum(jnp.where(ri == 2 * p, lse_blk, 0.0), axis=0, keepdims=True)
            lse1 = jnp.sum(jnp.where(ri == 2 * p + 1, lse_blk, 0.0), axis=0, keepdims=True)

            def scores(kj):
                k0 = pl.multiple_of(kj * t, t)
                kv, akv = k_ref[pl.ds(k0, t), :], ak_ref[pl.ds(k0, t), :]
                return _dot(jnp.where(lo, kv, akv), qa, "nt"), _dot(jnp.where(lo, akv, kv), qb, "nt")

            def pass1(kj, d0, d1, diagonal):
                k0 = pl.multiple_of(kj * t, t)
                vv = v_ref[pl.ds(k0, t), :]
                s0, s1 = scores(kj)
                if diagonal:
                    s0, s1 = jnp.where(causal_t, s0, NEG_BIG), jnp.where(causal_t, s1, NEG_BIG)
                p0, p1 = jnp.exp(s0 - lse0), jnp.exp(s1 - lse1)
                dp0, dp1 = _dot(vv, doa, "nt"), _dot(vv, dob, "nt")
                p_scr[0, kj], p_scr[1, kj] = p0, p1
                dp_scr[0, kj], dp_scr[1, kj] = dp0, dp1
                dv_ref[pl.ds(k0, t), :] += _dot(p0, doa) + _dot(p1, dob)
                return d0 + jnp.sum(p0 * dp0, axis=0, keepdims=True), d1 + jnp.sum(p1 * dp1, axis=0, keepdims=True)

            zero = jnp.zeros((1, t), F32)
            d0, d1 = lax.fori_loop(0, qi, lambda kj, c: pass1(kj, *c, False), (zero, zero))
            d0, d1 = pass1(qi, d0, d1, True)

            def pass2(kj, carry):
                dq0, dq1 = carry
                k0 = pl.multiple_of(kj * t, t)
                p0, p1 = p_scr[0, kj], p_scr[1, kj]
                ds0, ds1 = p0 * (dp_scr[0, kj] - d0), p1 * (dp_scr[1, kj] - d1)
                dk_ref[pl.ds(k0, t), :] += jnp.where(lo, _dot(ds0, qa), _dot(ds1, qb))
                dc0_ref[pl.ds(k0, t), :] += ds0[:, :128] + ds0[:, 128:]
                dc1_ref[pl.ds(k0, t), :] += ds1[:, :128] + ds1[:, 128:]
                ktv, aktv = kt_ref[:, pl.ds(k0, t)], akt_ref[:, pl.ds(k0, t)]
                return dq0 + _dot(jnp.where(lo_rows, ktv, aktv), ds0), dq1 + _dot(jnp.where(lo_rows, aktv, ktv), ds1)

            zq = jnp.zeros((128, t), F32)
            dq0, dq1 = lax.fori_loop(0, qi + 1, pass2, (zq, zq))
            dq_ref[pl.ds(q0, t), :] = jnp.where(lo_rows, dq0, dq1).T
            return 0

        lax.fori_loop(0, nq, q_loop, 0)
        if ns:
            pl.when(p == N_PAIRS - 1)(finish)

    pair = pl.BlockSpec((s, 128), lambda p: (0, p), pipeline_mode=once)
    pair_t = pl.BlockSpec((128, s), lambda p: (p, 0), pipeline_mode=once)
    out = jax.ShapeDtypeStruct((s, 1024), F32)
    outs = pl.pallas_call(
        body, name="fox_bwd", grid=(N_PAIRS,),
        in_specs=[pair, pair, pair, pair, pair_t, pair_t, pair, pl.BlockSpec((N_HEADS, s), lambda p: (0, 0)),
                  pl.BlockSpec((s, 128), lambda p: (0, 8 + p), pipeline_mode=once)] + [ANY] * ns,
        out_specs=[pair] * 5 + [ANY] * ns,
        out_shape=[out] * 5 + [jax.ShapeDtypeStruct(p.shape, p.dtype) for p in parts],
        scratch_shapes=[pltpu.VMEM((2, nq, t, t), F32), pltpu.VMEM((2, nq, t, t), F32)] + (_scatter_scratch(ns) if ns else []),
        compiler_params=_params(("arbitrary",)),
    )(qn, kn, aq, ak, knt, akt, vb, lse, dmixed, *parts)
    return (*outs[:5], _keep_own_blocks(outs[5:], parts))


def fox_post(dqn, dkn, dc0, dc1, proj, small, smallp, qw, kw, sel, selt, fold_even, fold_odd, *, tm=256):
    s = proj.shape[0]
    nrow = s // tm

    def body(dqn_ref, dkn_ref, dc0_ref, dc1_ref, q_ref, k_ref, small_ref, sp_ref, qw_ref, kw_ref, sel_ref, selt_ref,
             fe_ref, fo_ref, dq_ref, dk_ref, dsmall_ref, gqw_ref, gkw_ref, gfb_ref, carry):
        step = pl.program_id(0)

        @pl.when(step == 0)
        def _():
            carry[...] = jnp.zeros_like(carry)

        def norm_bwd(x_ref, w_ref, dn, out_ref):
            x = x_ref[...]
            rf = _head_rstd(x, sel_ref, selt_ref)
            xh = x * rf
            g = dn * w_ref[...]
            mean_gx = _split3_dot(_split3_dot(g * xh, sel_ref[...]) * (1.0 / HEAD_DIM), selt_ref[...])
            out_ref[...] = (rf * (g - xh * mean_gx)).astype(BF16)
            return jnp.sum(dn * xh, axis=0, keepdims=True)

        gqw = norm_bwd(q_ref, qw_ref, dqn_ref[...] * FOX_SCALE, dq_ref)
        gkw = norm_bwd(k_ref, kw_ref, dkn_ref[...], dk_ref)
        li = _lane_iota((tm, 128))
        f_lane = jnp.logical_and(li >= F_LANE, li < F_LANE + N_HEADS)
        dcum = -(_split3_dot(dc0_ref[...], fe_ref[...]) + _split3_dot(dc1_ref[...], fo_ref[...]))
        dlogf = _rev_cumsum_rows(dcum) + carry[...]
        carry[...] = dlogf[0:1, :]
        dfr = jnp.where(f_lane, dlogf * _sigmoid(-(small_ref[...] + sp_ref[3:4, :])), 0.0)
        dsmall_ref[...] = dfr
        gfb = jnp.sum(dfr, axis=0, keepdims=True)

        @pl.when(step == 0)
        def _():
            gqw_ref[...] = gqw
            gkw_ref[...] = gkw
            gfb_ref[...] = gfb

        @pl.when(step > 0)
        def _():
            gqw_ref[...] += gqw
            gkw_ref[...] += gkw
            gfb_ref[...] += gfb

    def rb(i):
        return nrow - 1 - i

    row = pl.BlockSpec((tm, 1024), lambda i: (rb(i), 0))
    vec = pl.BlockSpec((1, 1024), lambda i: (0, 0))
    fold = pl.BlockSpec((1024, 128), lambda i: (0, 0))
    return pl.pallas_call(
        body, name="fox_post", grid=(nrow,),
        in_specs=[row, row, row, row, pl.BlockSpec((tm, 1024), lambda i: (rb(i), Q_COL)),
                  pl.BlockSpec((tm, 1024), lambda i: (rb(i), K_COL)),
                  pl.BlockSpec((tm, 128), lambda i: (rb(i), 0)), pl.BlockSpec((8, 128), lambda i: (0, 0)), vec, vec,
                  fold, pl.BlockSpec((128, 1024), lambda i: (0, 0)), fold, fold],
        out_specs=[row, row, pl.BlockSpec((tm, 128), lambda i: (rb(i), 0)), vec, vec, pl.BlockSpec((1, 128), lambda i: (0, 0))],
        out_shape=[jax.ShapeDtypeStruct((s, 1024), BF16), jax.ShapeDtypeStruct((s, 1024), BF16),
                   jax.ShapeDtypeStruct((s, 128), F32), jax.ShapeDtypeStruct((1, 1024), F32),
                   jax.ShapeDtypeStruct((1, 1024), F32), jax.ShapeDtypeStruct((1, 128), F32)],
        scratch_shapes=[pltpu.VMEM((1, 128), F32)], compiler_params=_params(("arbitrary",)),
    )(dqn, dkn, dc0, dc1, proj, proj, small, smallp, qw, kw, sel, selt, fold_even, fold_odd)


def local_step(x, target, wm, ws, later_shards, ssd_cw8, ssd_cb, smallp, ssd_nw, qw_t, kw_t, sel, selt,
               norm_mix_w, norm_ffn_w, ffn_cw8, ffn_cb):
    h, h_t = rms_fwd(x, norm_mix_w, name="rms_mix_fwd")
    proj = matmul(h, wm, mode="nn", tm=1024, tn=1408, tk=1024, out_dtype=F32, name="mm_in_proj")
    small = matmul(h, ws, mode="nn", tm=1024, tn=128, tk=1024, out_dtype=F32, name="mm_in_proj_small")
    y_ssd, y_ssd_t, ypre, states = ssd_fwd(proj, small, ssd_cw8, ssd_cb, smallp, ssd_nw)
    place_q, place_k, ones_q, ones_k, fold_even, fold_odd = fox_tables()
    qn, kn, aq, ak, vb, knt, akt, vt = fox_prep(proj, small, smallp, qw_t, kw_t, sel, selt, place_q, place_k, ones_q, ones_k)
    y_fox, y_fox_t, lse, (a_out, a_up, a_down) = fox_fwd(qn, kn, aq, ak, vt, shards=later_shards)
    w_out = a_out.reshape(2048, D_MODEL)
    w_up = a_up.transpose(1, 0, 2).reshape(D_MODEL, 2 * D_FF)
    w_down = a_down.reshape(D_FF, D_MODEL)
    x1 = matmul(y_ssd, w_out, mode="nn", tm=1024, tn=1024, tk=1024, out_dtype=F32, name="mm_out_ssd", add=x)
    x1 = matmul(y_fox, w_out, mode="nn", tm=1024, tn=1024, tk=1024, out_dtype=F32, name="mm_out_fox", add=x1, b_koff=1)
    hf, hf_t = rms_fwd(x1, norm_ffn_w, name="rms_ffn_fwd")
    hu = matmul(hf, w_up, mode="nn", tm=1024, tn=1408, tk=1024, out_dtype=F32, name="mm_up")
    act, act_t = ffn_mid_fwd(hu, ffn_cw8, ffn_cb)
    y = matmul(act, w_down, mode="nn", tm=1024, tn=1024, tk=1408, out_dtype=F32, name="mm_down", add=x1)
    dy, sq = loss_head(y, target)

    dact = matmul(dy, w_down, mode="nt", tm=1024, tn=1408, tk=1024, out_dtype=F32, name="mm_dact")
    g_down = matmul(act_t, dy, mode="nn", tm=1408, tn=1024, tk=1024, out_dtype=BF16, name="mm_dw_down")
    dhu_g, dhu_v, gcw_g, gcw_v = ffn_mid_bwd(hu, dact, ffn_cw8, ffn_cb)
    dhf = matmul(dhu_g, w_up, mode="nt", tm=1024, tn=1024, tk=1408, out_dtype=F32, name="mm_dhf_gate")
    dhf = matmul(dhu_v, w_up, mode="nt", tm=1024, tn=1024, tk=1408, out_dtype=F32, name="mm_dhf_val", add=dhf, b_koff=2)
    g_up_g = matmul(hf_t, dhu_g, mode="nn", tm=1024, tn=1408, tk=1024, out_dtype=BF16, name="mm_dw_up_gate")
    g_up_v = matmul(hf_t, dhu_v, mode="nn", tm=1024, tn=1408, tk=1024, out_dtype=BF16, name="mm_dw_up_val")
    dx1, g_norm_ffn = rms_bwd(dhf, x1, norm_ffn_w, dy, name="rms_ffn_bwd")
    dmixed = matmul(dx1, w_out, mode="nt", tm=1024, tn=1024, tk=1024, out_dtype=F32, name="mm_dmixed")
    g_out_a = matmul(y_ssd_t, dx1, mode="nn", tm=1024, tn=1024, tk=1024, out_dtype=BF16, name="mm_dw_out_ssd")
    g_out_b = matmul(y_fox_t, dx1, mode="nn", tm=1024, tn=1024, tk=1024, out_dtype=BF16, name="mm_dw_out_fox")
    early = [jnp.concatenate([g_out_a, g_out_b], axis=0).reshape(4, 512, D_MODEL),
             jnp.concatenate([g_up_g, g_up_v], axis=1).reshape(D_MODEL, 4, 1408).transpose(1, 0, 2),
             g_down.reshape(4, 704, D_MODEL)]
    mine, theirs = pair_swap_halves(early, name="pair_swap_early")
    parts = [add_pair(a, b, name="add_pair_" + n, tr=ADAM_ROWS[n]) for a, b, n in zip(mine, theirs, BIG_NAMES[1:])]
    dz, dxs, db, dc, dsmall_ssd, gcw_x, gcw_b, gcw_c, g_sp, g_ssd_nw = ssd_bwd(
        proj, small, ssd_cw8, ssd_cb, smallp, ssd_nw, ypre, states, dmixed, sel)
    dqn, dkn, dv, dc0, dc1, landed_early = fox_bwd(qn, kn, aq, ak, knt, akt, vb, lse, dmixed, parts=parts)
    dq, dk, dsmall_fox, g_qw, g_kw, g_fb = fox_post(dqn, dkn, dc0, dc1, proj, small, smallp, qw_t, kw_t, sel, selt,
                                                    fold_even, fold_odd)
    dproj = jnp.concatenate([dz, dxs, dq, dk, dv.astype(BF16), db, dc], axis=1)
    dsmall = (dsmall_ssd + dsmall_fox).astype(BF16)
    g_wm = matmul(h_t, dproj, mode="nn", tm=1024, tn=1408, tk=1024, out_dtype=BF16, name="mm_dw_in")
    g_ws = matmul(h_t, dsmall, mode="nn", tm=1024, tn=128, tk=1024, out_dtype=BF16, name="mm_dw_in_small")
    g_in = jnp.concatenate([g_wm[:, :2048], g_wm[:, 5120:5632], g_ws[:, :16], g_wm[:, 2048:5120], g_ws[:, 16:32]], axis=1)
    mine, theirs = pair_swap_halves([g_in.reshape(D_MODEL, 4, 1416).transpose(1, 0, 2)], name="pair_swap_w_in")
    part_in = add_pair(mine[0], theirs[0], name="add_pair_w_in", tr=ADAM_ROWS["w_in"])
    dh, landed_in = matmul(dproj, wm, mode="nt", tm=1024, tn=1024, tk=1408, out_dtype=F32, name="mm_dh", scatter=[part_in])
    dh = matmul(dsmall, ws, mode="nt", tm=1024, tn=1024, tk=128, out_dtype=F32, name="mm_dh_small", add=dh)
    grad_x, g_norm_mix = rms_bwd(dh, x, norm_mix_w, dx1, name="rms_mix_bwd")
    return dict(
        sq=sq, grad_x=grad_x, landed=landed_in + landed_early,
        g_norm_mix=g_norm_mix, g_norm_ffn=g_norm_ffn, g_ssd_nw=g_ssd_nw,
        g_ssd_cw=jnp.concatenate([gcw_x, gcw_b, gcw_c], axis=1), g_sp=g_sp, g_fb=g_fb, g_qw=g_qw, g_kw=g_kw,
        g_ffn_cw=jnp.concatenate([gcw_g, gcw_v], axis=1))


def adamw(w, g, m, v, *, name, tr):
    rows, cols = w.shape

    def body(w_ref, g_ref, m_ref, v_ref, d_ref, mo_ref, vo_ref):
        gv = g_ref[...]
        mn = ADAM_B1 * m_ref[...] + (1.0 - ADAM_B1) * gv
        vn = ADAM_B2 * v_ref[...] + (1.0 - ADAM_B2) * (gv * gv)
        m_hat = mn / (1.0 - ADAM_B1 ** ADAM_STEP)
        v_hat = vn / (1.0 - ADAM_B2 ** ADAM_STEP)
        d_ref[...] = -ADAM_LR * (m_hat / (jnp.sqrt(v_hat) + ADAM_EPS) + ADAM_WD * w_ref[...])
        mo_ref[...] = mn
        vo_ref[...] = vn

    blk = pl.BlockSpec((tr, cols), lambda i: (i, 0))
    shp = jax.ShapeDtypeStruct((rows, cols), F32)
    return pl.pallas_call(
        body, name=name, grid=(rows // tr,), in_specs=[blk] * 4, out_specs=[blk] * 3, out_shape=[shp] * 3,
        compiler_params=_params(("parallel",)),
    )(w, g, m, v)


def add_pair(a, b, *, name, tr):
    _, rows, cols = a.shape

    def body(a_ref, b_ref, o_ref):
        o_ref[...] = (a_ref[...].astype(F32) + b_ref[...].astype(F32)).astype(BF16)

    blk = pl.BlockSpec((1, tr, cols), lambda j, i: (j, i, 0))
    return pl.pallas_call(
        body, name=name, grid=(4, rows // tr), in_specs=[blk, blk], out_specs=blk,
        out_shape=jax.ShapeDtypeStruct(a.shape, BF16), compiler_params=_params(("parallel", "parallel")),
    )(a, b)


def sum_chips(parts, *, name, tr):
    _, rows, cols = parts.shape

    def body(p_ref, o_ref):
        acc = p_ref[0].astype(F32)
        for k in range(1, 4):
            acc = acc + p_ref[k].astype(F32)
        o_ref[...] = acc

    return pl.pallas_call(
        body, name=name, grid=(rows // tr,), in_specs=[pl.BlockSpec((4, tr, cols), lambda i: (0, i, 0))],
        out_specs=pl.BlockSpec((tr, cols), lambda i: (i, 0)), out_shape=jax.ShapeDtypeStruct((rows, cols), F32),
        compiler_params=_params(("parallel",)),
    )(parts)


ANY = pl.BlockSpec(memory_space=pl.ANY)


def _place():
    x, y, c = lax.axis_index("x"), lax.axis_index("y"), lax.axis_index("c")
    chips = [(1 - x, y), (x, 1 - y), (1 - x, 1 - y)]
    return x, y, c, chips


def _chunks(rows):
    size = next((c for c in (128, 176, 64, 32, 16, 8) if rows % c == 0), rows)
    return [(r, size) for r in range(0, rows, size)]


def gather_weights(shards):
    n = len(shards)

    def body(*refs):
        start, forward, finish = _gather_phases(refs[:n], refs[n:2 * n], *refs[2 * n:])
        start()
        forward()
        finish()

    gathered = pl.pallas_call(
        body, name="gather_weights", in_specs=[ANY] * n, out_specs=[ANY] * n,
        out_shape=_gather_out_shapes(shards), scratch_shapes=_gather_scratch(n),
    )(*shards)
    return gathered


def _gather_out_shapes(shards):
    return [jax.ShapeDtypeStruct((4,) + s.shape, s.dtype) for s in shards]


def _gather_scratch(n):
    return [pltpu.SemaphoreType.DMA((n, 7)), pltpu.SemaphoreType.DMA((n, 7))]


def _gather_phases(ins, outs, send_sems, recv_sems):
    n = len(ins)
    x, y, c, chips = _place()
    me = 2 * x + y
    sibling = (x, y, 1 - c)
    blks = [2 * cx + cy for cx, cy in chips]

    def half(a, blk, r=0, nr=None):
        rows = ins[a].shape[0] // 2
        return outs[a].at[blk, pl.ds(c * rows + r, rows if nr is None else nr), :]

    def to_chip(a, t, r=0, nr=None):
        rows = ins[a].shape[0] // 2
        return pltpu.make_async_remote_copy(
            src_ref=ins[a].at[pl.ds(c * rows + r, rows if nr is None else nr), :], dst_ref=half(a, me, r, nr),
            send_sem=send_sems.at[a, t], recv_sem=recv_sems.at[a, t], device_id=(*chips[t], c), device_id_type=MESH)

    def from_chip(a, t):
        return pltpu.make_async_remote_copy(
            src_ref=half(a, blks[t]), dst_ref=half(a, blks[t]), send_sem=send_sems.at[a, t], recv_sem=recv_sems.at[a, t],
            device_id=(*chips[t], c), device_id_type=MESH)

    def to_sibling(a, t, r=0, nr=None):
        return pltpu.make_async_remote_copy(
            src_ref=half(a, blks[t], r, nr), dst_ref=half(a, blks[t], r, nr), send_sem=send_sems.at[a, 3 + t],
            recv_sem=recv_sems.at[a, 3 + t], device_id=sibling, device_id_type=MESH)

    def from_sibling(a, t):
        rows = ins[a].shape[0] // 2
        dst = outs[a].at[blks[t], pl.ds((1 - c) * rows, rows), :]
        return pltpu.make_async_remote_copy(
            src_ref=dst, dst_ref=dst, send_sem=send_sems.at[a, 3 + t], recv_sem=recv_sems.at[a, 3 + t],
            device_id=sibling, device_id_type=MESH)

    def own(a, r=0, nr=None):
        return pltpu.make_async_remote_copy(
            src_ref=ins[a].at[pl.ds(r, ins[a].shape[0] if nr is None else nr), :],
            dst_ref=outs[a].at[me, pl.ds(r, ins[a].shape[0] if nr is None else nr), :],
            send_sem=send_sems.at[a, 6], recv_sem=recv_sems.at[a, 6], device_id=sibling, device_id_type=MESH)

    def start():
        for a in range(n):
            for t in range(3):
                for r, nr in _chunks(ins[a].shape[0] // 2):
                    to_chip(a, t, r, nr).start()
            for r, nr in _chunks(ins[a].shape[0]):
                own(a, r, nr).start()

    def forward():
        for a in range(n):
            for t in range(3):
                from_chip(a, t).wait_recv()
                for r, nr in _chunks(ins[a].shape[0] // 2):
                    to_sibling(a, t, r, nr).start()

    def finish():
        for a in range(n):
            for t in range(3):
                from_sibling(a, t).wait_recv()
        for a in range(n):
            for t in range(3):
                to_chip(a, t).wait_send()
                to_sibling(a, t).wait_send()
            own(a).wait()

    return start, forward, finish


def pair_swap_halves(grads, *, name):
    n = len(grads)

    def body(*refs):
        ins, theirs = refs[:n], refs[n:2 * n]
        send_sems, recv_sems = refs[2 * n:]
        x, y, c, _ = _place()
        sibling = (x, y, 1 - c)
        for a in range(n):
            rows = ins[a].shape[1] // 2
            for j in range(4):
                for r, nr in _chunks(rows):
                    pltpu.make_async_remote_copy(
                        src_ref=ins[a].at[j, pl.ds((1 - c) * rows + r, nr), :], dst_ref=theirs[a].at[j, pl.ds(r, nr), :],
                        send_sem=send_sems.at[a], recv_sem=recv_sems.at[a], device_id=sibling, device_id_type=MESH).start()
        for a in range(n):
            pltpu.make_async_remote_copy(src_ref=theirs[a], dst_ref=theirs[a], send_sem=send_sems.at[a],
                                         recv_sem=recv_sems.at[a], device_id=sibling, device_id_type=MESH).wait()

    halves = [jax.ShapeDtypeStruct((4, g.shape[1] // 2, g.shape[2]), g.dtype) for g in grads]
    theirs = pl.pallas_call(
        body, name=name, in_specs=[ANY] * n, out_specs=[ANY] * n, out_shape=halves,
        scratch_shapes=[pltpu.SemaphoreType.DMA((n,)), pltpu.SemaphoreType.DMA((n,))],
    )(*grads)
    c = lax.axis_index("c")
    mine = [lax.dynamic_slice_in_dim(g, c * (g.shape[1] // 2), g.shape[1] // 2, axis=1) for g in grads]
    return mine, theirs


def _scatter_scratch(n):
    return [pltpu.SemaphoreType.DMA((n, 3)), pltpu.SemaphoreType.DMA((n, 3))]


def _keep_own_blocks(landed, parts):
    if not parts:
        return []
    chip = 2 * lax.axis_index("x") + lax.axis_index("y")
    return [lax.dynamic_update_slice(l, lax.dynamic_slice_in_dim(p, chip, 1, axis=0), (chip, 0, 0))
            for l, p in zip(landed, parts)]


def _scatter_phases(ins, outs, send_sems, recv_sems):
    n = len(ins)
    x, y, c, chips = _place()
    me = 2 * x + y
    blks = [2 * cx + cy for cx, cy in chips]

    def start():
        for a in range(n):
            for r, nr in _chunks(ins[a].shape[1]):
                for t in range(3):
                    pltpu.make_async_remote_copy(
                        src_ref=ins[a].at[blks[t], pl.ds(r, nr), :], dst_ref=outs[a].at[me, pl.ds(r, nr), :],
                        send_sem=send_sems.at[a, t], recv_sem=recv_sems.at[a, t],
                        device_id=(*chips[t], c), device_id_type=MESH).start()

    def finish():
        for a in range(n):
            for t in range(3):
                pltpu.make_async_remote_copy(
                    src_ref=outs[a].at[blks[t]], dst_ref=outs[a].at[blks[t]], send_sem=send_sems.at[a, t],
                    recv_sem=recv_sems.at[a, t], device_id=(*chips[t], c), device_id_type=MESH).wait()

    return start, finish


def pair_join_halves(halves):
    n = len(halves)

    def body(*refs):
        ins, outs = refs[:n], refs[n:2 * n]
        send_sems, recv_sems = refs[2 * n:]
        x, y, c, _ = _place()
        sibling = (x, y, 1 - c)
        for a in range(n):
            rows = ins[a].shape[0]
            for r, nr in _chunks(rows):
                pltpu.make_async_remote_copy(
                    src_ref=ins[a].at[pl.ds(r, nr), :], dst_ref=outs[a].at[pl.ds(c * rows + r, nr), :],
                    send_sem=send_sems.at[a], recv_sem=recv_sems.at[a], device_id=sibling, device_id_type=MESH).start()
        for a in range(n):
            rows = ins[a].shape[0]
            got = outs[a].at[pl.ds((1 - c) * rows, rows), :]
            pltpu.make_async_remote_copy(src_ref=ins[a], dst_ref=got, send_sem=send_sems.at[a], recv_sem=recv_sems.at[a],
                                         device_id=sibling, device_id_type=MESH).wait()

    joined = pl.pallas_call(
        body, name="pair_join_halves", in_specs=[ANY] * n, out_specs=[ANY] * n,
        out_shape=[jax.ShapeDtypeStruct((2 * h.shape[0], h.shape[1]), h.dtype) for h in halves],
        scratch_shapes=[pltpu.SemaphoreType.DMA((n,)), pltpu.SemaphoreType.DMA((n,))],
    )(*halves)
    c = lax.axis_index("c")
    return [lax.dynamic_update_slice(j, h, (c * h.shape[0], 0)) for j, h in zip(joined, halves)]


def allreduce_small(packed):
    rows = packed.shape[0]

    def body(in_ref, out_ref, gathered, send_sems, recv_sems):
        x, y, c, _ = _place()
        me = 4 * x + 2 * y + c
        gathered[me] = in_ref[...]
        flips = [(fx, fy, fc) for fx in (0, 1) for fy in (0, 1) for fc in (0, 1)][1:]
        peers = [((1 - x) if fx else x, (1 - y) if fy else y, (1 - c) if fc else c) for fx, fy, fc in flips]
        copies = []
        for t, peer in enumerate(peers):
            cp = pltpu.make_async_remote_copy(
                src_ref=in_ref, dst_ref=gathered.at[me], send_sem=send_sems.at[t], recv_sem=recv_sems.at[t],
                device_id=peer, device_id_type=MESH)
            cp.start()
            copies.append(cp)
        for t, (px, py, pc) in enumerate(peers):
            slot = gathered.at[4 * px + 2 * py + pc]
            pltpu.make_async_remote_copy(
                src_ref=slot, dst_ref=slot, send_sem=send_sems.at[t], recv_sem=recv_sems.at[t],
                device_id=(px, py, pc), device_id_type=MESH).wait_recv()
        for cp in copies:
            cp.wait_send()
        acc = gathered[0]
        for k in range(1, 8):
            acc = acc + gathered[k]
        out_ref[...] = acc

    vm = pl.BlockSpec(memory_space=pltpu.VMEM)
    return pl.pallas_call(
        body, name="allreduce_small", in_specs=[vm], out_specs=vm, out_shape=jax.ShapeDtypeStruct(packed.shape, F32),
        scratch_shapes=[pltpu.VMEM((8, rows, 128), F32), pltpu.SemaphoreType.DMA((7,)), pltpu.SemaphoreType.DMA((7,))],
    )(packed)


SMALL_NAMES = ("norm_mix_w", "ssd_conv_w", "ssd_conv_b", "ssd_dt_bias", "ssd_a_log", "ssd_d", "ssd_norm_w", "fox_f_bias",
               "fox_q_norm_w", "fox_k_norm_w", "norm_ffn_w", "ffn_conv_w", "ffn_conv_b")
BIG_NAMES = ("w_in", "w_out", "w_up", "w_down")
WEIGHT_ORDER = ("norm_mix_w", "w_in", "ssd_conv_w", "ssd_conv_b", "ssd_dt_bias", "ssd_a_log", "ssd_d", "ssd_norm_w",
                "fox_f_bias", "fox_q_norm_w", "fox_k_norm_w", "w_out", "norm_ffn_w", "w_up", "ffn_conv_w", "ffn_conv_b", "w_down")
ADAM_ROWS = {"w_in": 256, "w_out": 256, "w_up": 256, "w_down": 176}


def _pack(arrays):
    pieces = []
    for a in arrays:
        flat = a.reshape(-1).astype(F32)
        pieces += [flat, jnp.zeros(((-flat.shape[0]) % 1024,), F32)]
    return jnp.concatenate(pieces).reshape(-1, 128)


def _unpack(packed, shapes):
    out, r = [], 0
    for shp in shapes:
        size = 1
        for d in shp:
            size *= d
        nrow = 8 * (-(-size // 1024))
        out.append(packed[r:r + nrow].reshape(-1)[:size].reshape(shp))
        r += nrow
    return out


def _pad_rows(a, rows):
    return jnp.pad(a, ((0, rows - a.shape[0]), (0, 0)))


def kernel(x, norm_mix_w, w_in, ssd_conv_w, ssd_conv_b, ssd_dt_bias, ssd_a_log, ssd_d, ssd_norm_w, fox_f_bias, fox_q_norm_w, fox_k_norm_w, w_out, norm_ffn_w, w_up, ffn_conv_w, ffn_conv_b, w_down, loss_target, m_norm_mix_w, m_w_in, m_ssd_conv_w, m_ssd_conv_b, m_ssd_dt_bias, m_ssd_a_log, m_ssd_d, m_ssd_norm_w, m_fox_f_bias, m_fox_q_norm_w, m_fox_k_norm_w, m_w_out, m_norm_ffn_w, m_w_up, m_ffn_conv_w, m_ffn_conv_b, m_w_down, v_norm_mix_w, v_w_in, v_ssd_conv_w, v_ssd_conv_b, v_ssd_dt_bias, v_ssd_a_log, v_ssd_d, v_ssd_norm_w, v_fox_f_bias, v_fox_q_norm_w, v_fox_k_norm_w, v_w_out, v_norm_ffn_w, v_w_up, v_ffn_conv_w, v_ffn_conv_b, v_w_down):
    w = dict(norm_mix_w=norm_mix_w, w_in=w_in, ssd_conv_w=ssd_conv_w, ssd_conv_b=ssd_conv_b, ssd_dt_bias=ssd_dt_bias,
             ssd_a_log=ssd_a_log, ssd_d=ssd_d, ssd_norm_w=ssd_norm_w, fox_f_bias=fox_f_bias, fox_q_norm_w=fox_q_norm_w,
             fox_k_norm_w=fox_k_norm_w, w_out=w_out, norm_ffn_w=norm_ffn_w, w_up=w_up, ffn_conv_w=ffn_conv_w,
             ffn_conv_b=ffn_conv_b, w_down=w_down)
    m = dict(norm_mix_w=m_norm_mix_w, w_in=m_w_in, ssd_conv_w=m_ssd_conv_w, ssd_conv_b=m_ssd_conv_b, ssd_dt_bias=m_ssd_dt_bias,
             ssd_a_log=m_ssd_a_log, ssd_d=m_ssd_d, ssd_norm_w=m_ssd_norm_w, fox_f_bias=m_fox_f_bias, fox_q_norm_w=m_fox_q_norm_w,
             fox_k_norm_w=m_fox_k_norm_w, w_out=m_w_out, norm_ffn_w=m_norm_ffn_w, w_up=m_w_up, ffn_conv_w=m_ffn_conv_w,
             ffn_conv_b=m_ffn_conv_b, w_down=m_w_down)
    v = dict(norm_mix_w=v_norm_mix_w, w_in=v_w_in, ssd_conv_w=v_ssd_conv_w, ssd_conv_b=v_ssd_conv_b, ssd_dt_bias=v_ssd_dt_bias,
             ssd_a_log=v_ssd_a_log, ssd_d=v_ssd_d, ssd_norm_w=v_ssd_norm_w, fox_f_bias=v_fox_f_bias, fox_q_norm_w=v_fox_q_norm_w,
             fox_k_norm_w=v_fox_k_norm_w, w_out=v_w_out, norm_ffn_w=v_norm_ffn_w, w_up=v_w_up, ffn_conv_w=v_ffn_conv_w,
             ffn_conv_b=v_ffn_conv_b, w_down=v_w_down)
    chip = 2 * lax.axis_index("x") + lax.axis_index("y")

    a_in, a_scw, a_fcw = gather_weights([w_in[0].astype(BF16), _pad_rows(ssd_conv_w[0], 16), _pad_rows(ffn_conv_w[0], 16)])
    later_shards = [w_out[0].astype(BF16), w_up[0].astype(BF16), w_down[0].astype(BF16)]
    w_full = a_in.transpose(1, 0, 2).reshape(D_MODEL, IN_COLS)
    wm = jnp.concatenate([w_full[:, :2048], w_full[:, 2576:5648], w_full[:, 2048:2560]], axis=1)
    ws = jnp.concatenate([w_full[:, 2560:2576], w_full[:, 5648:5664], jnp.zeros((D_MODEL, SMALL_COLS - 32), BF16)], axis=1)
    ssd_cw8 = a_scw.transpose(1, 0, 2).reshape(16, 1536)[:8]
    ffn_cw8 = a_fcw.transpose(1, 0, 2).reshape(16, 2 * D_FF)[:8]
    smallp = jnp.zeros((8, 128), F32)
    smallp = smallp.at[0, :16].set(ssd_dt_bias[0]).at[1, :16].set(ssd_a_log[0]).at[2, :16].set(ssd_d[0])
    smallp = smallp.at[3, F_LANE:F_LANE + 16].set(fox_f_bias[0])
    qw_t = jnp.tile(fox_q_norm_w[0], N_HEADS)[None]
    kw_t = jnp.tile(fox_k_norm_w[0], N_HEADS)[None]
    sel = jnp.asarray((np.arange(1024)[:, None] // HEAD_DIM == np.arange(128)[None, :]).astype(np.float32), BF16)

    res = local_step(x[0], loss_target[0], wm, ws, later_shards, ssd_cw8, ssd_conv_b, smallp, ssd_norm_w, qw_t, kw_t,
                     sel, sel.T, norm_mix_w, norm_ffn_w, ffn_cw8, ffn_conv_b)

    full_shapes = [(1, 1024), (1, 4, 1536), (1, 1536), (1, 16), (1, 16), (1, 16), (1, 1024), (1, 16), (1, 64), (1, 64),
                   (1, 1024), (1, 3, 2 * D_FF), (1, 2 * D_FF), (1,)]
    local_small = [res["g_norm_mix"], res["g_ssd_cw"][:4], res["g_ssd_cw"][4], res["g_sp"][0, :16], res["g_sp"][1, :16],
                   res["g_sp"][2, :16], res["g_ssd_nw"], res["g_fb"][0, F_LANE:F_LANE + 16],
                   res["g_qw"].reshape(N_HEADS, HEAD_DIM).sum(0), res["g_kw"].reshape(N_HEADS, HEAD_DIM).sum(0),
                   res["g_norm_ffn"], res["g_ffn_cw"][:3], res["g_ffn_cw"][3], jnp.sum(res["sq"])]
    summed = _unpack(allreduce_small(_pack(local_small)), full_shapes)
    loss = (0.5 / D_MODEL) * summed[-1][0]
    g_small = dict(zip(SMALL_NAMES, summed[:-1]))
    g_small["ssd_conv_w"] = lax.dynamic_slice(g_small["ssd_conv_w"], (0, 0, 384 * chip), (1, 4, 384))
    g_small["ffn_conv_w"] = lax.dynamic_slice(g_small["ffn_conv_w"], (0, 0, 1408 * chip), (1, 3, 1408))

    landed = res["landed"]
    halves = [sum_chips(p, name="sum_chips_" + n, tr=ADAM_ROWS[n]) for p, n in zip(landed, BIG_NAMES)]
    g_big = dict(zip(BIG_NAMES, pair_join_halves(halves)))

    grads, deltas, new_m, new_v = {}, {}, {}, {}
    for n in BIG_NAMES:
        d, mn, vn = adamw(w[n][0], g_big[n], m[n][0], v[n][0], name="adamw_" + n, tr=ADAM_ROWS[n])
        grads[n], deltas[n], new_m[n], new_v[n] = g_big[n][None], d[None], mn[None], vn[None]
    shapes = [w[n].shape for n in SMALL_NAMES]
    d, mn, vn = adamw(_pack([w[n] for n in SMALL_NAMES]), _pack([g_small[n] for n in SMALL_NAMES]),
                      _pack([m[n] for n in SMALL_NAMES]), _pack([v[n] for n in SMALL_NAMES]), name="adamw_small", tr=8)
    for n, dd, mm, vv in zip(SMALL_NAMES, _unpack(d, shapes), _unpack(mn, shapes), _unpack(vn, shapes)):
        grads[n], deltas[n], new_m[n], new_v[n] = g_small[n].reshape(w[n].shape), dd, mm, vv
    return (loss, res["grad_x"][None], *[grads[n] for n in WEIGHT_ORDER], *[deltas[n] for n in WEIGHT_ORDER],
            *[new_m[n] for n in WEIGHT_ORDER], *[new_v[n] for n in WEIGHT_ORDER])
```

```python
import functools

import jax
import jax.numpy as jnp
import numpy as np
from jax import lax
from jax.experimental import pallas as pl
from jax.experimental.pallas import tpu as pltpu

F32 = jnp.float32
BF16 = jnp.bfloat16
MESH = pl.DeviceIdType.MESH

D_MODEL = 1024
HEAD_DIM = 64
N_HEADS = 16
N_PAIRS = N_HEADS // 2
SSD_CHUNK = 128
SSD_STATE = 128
SSD_CONV = 4
D_FF = 2816
FFN_CONV = 3
NORM_EPS = 1e-6
MAIN_COLS = 5632
SMALL_COLS = 128
F_LANE = 16
IN_COLS = 5664

ADAM_LR = 0.001
ADAM_B1 = 0.9
ADAM_B2 = 0.999
ADAM_EPS = 1e-08
ADAM_WD = 0.01
ADAM_STEP = 10

VMEM_LIMIT_V7X = 56 * 1024 * 1024
NEG_BIG = -1e30


def _params(sem=None):
    return pltpu.CompilerParams(dimension_semantics=sem, vmem_limit_bytes=VMEM_LIMIT_V7X)


def _sigmoid(x):
    return 1.0 / (1.0 + jnp.exp(-x))


def _silu_and_grad(x):
    s = _sigmoid(x)
    return x * s, s * (1.0 + x * (1.0 - s))


def _shift_down(v, j):
    return v if j == 0 else pltpu.roll(v, j, 0)


def _shift_up(v, j):
    return v if j == 0 else pltpu.roll(v, v.shape[0] - j, 0)


def _row_iota(shape):
    return lax.broadcasted_iota(jnp.int32, shape, 0)


def _lane_iota(shape):
    return lax.broadcasted_iota(jnp.int32, shape, 1)


def _dot(a, b, mode="nn"):
    dims = {"nn": (((1,), (0,)), ((), ())), "nt": (((1,), (1,)), ((), ())), "tn": (((0,), (0,)), ((), ()))}[mode]
    return lax.dot_general(a.astype(BF16), b.astype(BF16), dims, preferred_element_type=F32)


def _dot_f32(a, b):
    return jnp.dot(a, b, precision=lax.Precision.HIGHEST, preferred_element_type=F32)


def matmul(a, b, *, mode, tm, tn, tk, out_dtype, name, add=None, b_koff=0, scatter=(), layout=None):
    layout = layout or {}
    if layout:
        m, n, k = layout["m"], layout["n"], layout["k"]
    else:
        (m, k), n = a.shape, (b.shape[1] if mode == "nn" else b.shape[0])
    assert m % tm == 0 and n % tn == 0 and k % tk == 0, (name, m, n, k, tm, tn, tk)
    nk = k // tk
    grid = (m // tm, n // tn, nk)
    a_spec = layout.get("a_spec") or pl.BlockSpec((tm, tk), lambda i, j, kk: (i, kk))
    b_spec = layout.get("b_spec") or (pl.BlockSpec((tn, tk), lambda i, j, kk: (j, kk + b_koff)) if mode == "nt"
                                      else pl.BlockSpec((tk, tn), lambda i, j, kk: (kk + b_koff, j)))
    o_spec = layout.get("o_spec") or pl.BlockSpec((tm, tn), lambda i, j, kk: (i, j))
    out_struct = jax.ShapeDtypeStruct(layout.get("out_shape", (m, n)), out_dtype)
    has_add = add is not None
    n_in = 3 if has_add else 2
    ns = len(scatter)

    def body(*refs):
        a_ref, b_ref = refs[:2]
        add_ref = refs[2] if has_add else None
        o_ref, acc_ref = refs[n_in + ns], refs[n_in + 2 * ns + 1]
        kk = pl.program_id(2)
        if ns:
            step = (pl.program_id(0) * grid[1] + pl.program_id(1)) * grid[2] + kk
            start, finish_copies = _scatter_phases(refs[n_in:n_in + ns], refs[n_in + ns + 1:n_in + 2 * ns + 1],
                                                   *refs[n_in + 2 * ns + 2:])
            pl.when(step == 0)(start)
        part = _dot(a_ref[...], b_ref[...], mode)

        def finish(total):
            if has_add:
                total = total + add_ref[...]
            o_ref[...] = total.astype(out_dtype)

        if nk == 1:
            finish(part)
        else:
            @pl.when(kk == 0)
            def _():
                acc_ref[...] = part

            @pl.when(jnp.logical_and(kk > 0, kk < nk - 1))
            def _():
                acc_ref[...] += part

            @pl.when(kk == nk - 1)
            def _():
                finish(acc_ref[...] + part)

        if ns:
            pl.when(step == grid[0] * grid[1] * grid[2] - 1)(finish_copies)

    in_specs = [a_spec, b_spec] + ([o_spec] if has_add else [])
    args = (a, b) + ((add,) if has_add else ())
    acc = pltpu.VMEM((tm, tn) if nk > 1 else (8, 128), F32)
    if not ns:
        return pl.pallas_call(
            body, name=name, grid=grid, in_specs=in_specs, out_specs=o_spec, out_shape=out_struct,
            scratch_shapes=[acc], compiler_params=_params(("parallel", "parallel", "arbitrary")),
        )(*args)
    outs = pl.pallas_call(
        body, name=name, grid=grid, in_specs=in_specs + [ANY] * ns, out_specs=[o_spec] + [ANY] * ns,
        out_shape=[out_struct] + [jax.ShapeDtypeStruct(p.shape, p.dtype) for p in scatter],
        scratch_shapes=[acc] + _scatter_scratch(ns), compiler_params=_params(("arbitrary", "arbitrary", "arbitrary")),
    )(*args, *scatter)
    return outs[0], _keep_own_blocks(outs[1:], scatter)


def rms_fwd(x, w, *, name, tm=512):
    s, d = x.shape

    def body(x_ref, w_ref, h_ref, ht_ref):
        xv = x_ref[...]
        r = lax.rsqrt(jnp.mean(xv * xv, axis=-1, keepdims=True) + NORM_EPS)
        h = (xv * r) * w_ref[...]
        h_ref[...] = h.astype(BF16)
        ht_ref[...] = h.T.astype(BF16)

    return pl.pallas_call(
        body, name=name, grid=(s // tm,),
        in_specs=[pl.BlockSpec((tm, d), lambda i: (i, 0)), pl.BlockSpec((1, d), lambda i: (0, 0))],
        out_specs=[pl.BlockSpec((tm, d), lambda i: (i, 0)), pl.BlockSpec((d, tm), lambda i: (0, i))],
        out_shape=[jax.ShapeDtypeStruct((s, d), BF16), jax.ShapeDtypeStruct((d, s), BF16)],
        compiler_params=_params(("parallel",)),
    )(x, w)


def rms_bwd(dh, x, w, resid, *, name, tm=512):
    s, d = x.shape

    def body(dh_ref, x_ref, w_ref, res_ref, dx_ref, dw_ref):
        xv = x_ref[...]
        dhv = dh_ref[...]
        r = lax.rsqrt(jnp.mean(xv * xv, axis=-1, keepdims=True) + NORM_EPS)
        xh = xv * r
        g = dhv * w_ref[...]
        dx_ref[...] = res_ref[...] + r * (g - xh * jnp.mean(g * xh, axis=-1, keepdims=True))
        part = jnp.sum(dhv * xh, axis=0, keepdims=True)

        @pl.when(pl.program_id(0) == 0)
        def _():
            dw_ref[...] = part

        @pl.when(pl.program_id(0) > 0)
        def _():
            dw_ref[...] += part

    row = pl.BlockSpec((tm, d), lambda i: (i, 0))
    vec = pl.BlockSpec((1, d), lambda i: (0, 0))
    return pl.pallas_call(
        body, name=name, grid=(s // tm,), in_specs=[row, row, vec, row], out_specs=[row, vec],
        out_shape=[jax.ShapeDtypeStruct((s, d), F32), jax.ShapeDtypeStruct((1, d), F32)],
        compiler_params=_params(("arbitrary",)),
    )(dh, x, w, resid)


def loss_head(y, target, *, tm=512):
    s, d = y.shape

    def body(y_ref, t_ref, dy_ref, sq_ref):
        e = y_ref[...] - t_ref[...]
        dy_ref[...] = e / float(d)
        part = jnp.sum(e * e, axis=0, keepdims=True)

        @pl.when(pl.program_id(0) == 0)
        def _():
            sq_ref[...] = part

        @pl.when(pl.program_id(0) > 0)
        def _():
            sq_ref[...] += part

    row = pl.BlockSpec((tm, d), lambda i: (i, 0))
    vec = pl.BlockSpec((1, d), lambda i: (0, 0))
    return pl.pallas_call(
        body, name="loss_head", grid=(s // tm,), in_specs=[row, row], out_specs=[row, vec],
        out_shape=[jax.ShapeDtypeStruct((s, d), F32), jax.ShapeDtypeStruct((1, d), F32)],
        compiler_params=_params(("arbitrary",)),
    )(y, target)


def _row_shifts(ext, k_taps):
    return [_shift_down(ext, j) for j in range(k_taps)]


def _conv_rows(shifts, w):
    k_taps = len(shifts)
    acc = w[k_taps - 1:k_taps, :] * shifts[0]
    for k in range(k_taps - 1):
        acc = acc + w[k:k + 1, :] * shifts[k_taps - 1 - k]
    return acc


def _conv_weight_grad(dcur, shifts, rows, width):
    k_taps = len(shifts)
    out = [jnp.sum(dcur * shifts[k_taps - 1 - k][rows], axis=0, keepdims=True) for k in range(k_taps)]
    out.append(jnp.sum(dcur, axis=0, keepdims=True))
    return _stack_rows(out, width)


def _conv_rows_transposed(dext, w, k_taps):
    acc = w[k_taps - 1:k_taps, :] * dext
    for k in range(k_taps - 1):
        acc = acc + w[k:k + 1, :] * _shift_up(dext, k_taps - 1 - k)
    return acc


def _stack_rows(rows, width):
    ri = _row_iota((8, width))
    out = jnp.zeros((8, width), F32)
    for k, r in enumerate(rows):
        out = out + jnp.where(ri == k, r, 0.0)
    return out


def ffn_mid_fwd(hu, conv_w8, conv_b, *, tm=1024, tc=256):
    s = hu.shape[0]
    ncol = D_FF // tc
    r8 = tm // 8

    def body(g_ref, v_ref, gp_ref, vp_ref, wg_ref, wv_ref, bg_ref, bv_ref, o_ref, ot_ref):
        first = pl.program_id(1) == 0

        def conv(cur_ref, prev_ref, w_ref, b_ref):
            prev = jnp.where(first, 0.0, prev_ref[...])
            ext = jnp.concatenate([prev, cur_ref[...]], axis=0)
            return _conv_rows(_row_shifts(ext, FFN_CONV), w_ref[...])[8:] + b_ref[...]

        gc = conv(g_ref, gp_ref, wg_ref, bg_ref)
        vc = conv(v_ref, vp_ref, wv_ref, bv_ref)
        act = gc * _sigmoid(gc) * vc
        o_ref[...] = act.astype(BF16)
        ot_ref[...] = act.T.astype(BF16)

    def prev_idx(i):
        return jnp.maximum(i * r8 - 1, 0)

    in_specs = [
        pl.BlockSpec((tm, tc), lambda j, i: (i, j)),
        pl.BlockSpec((tm, tc), lambda j, i: (i, j + ncol)),
        pl.BlockSpec((8, tc), lambda j, i: (prev_idx(i), j)),
        pl.BlockSpec((8, tc), lambda j, i: (prev_idx(i), j + ncol)),
        pl.BlockSpec((8, tc), lambda j, i: (0, j)),
        pl.BlockSpec((8, tc), lambda j, i: (0, j + ncol)),
        pl.BlockSpec((1, tc), lambda j, i: (0, j)),
        pl.BlockSpec((1, tc), lambda j, i: (0, j + ncol)),
    ]
    return pl.pallas_call(
        body, name="ffn_mid_fwd", grid=(ncol, s // tm), in_specs=in_specs,
        out_specs=[pl.BlockSpec((tm, tc), lambda j, i: (i, j)), pl.BlockSpec((tc, tm), lambda j, i: (j, i))],
        out_shape=[jax.ShapeDtypeStruct((s, D_FF), BF16), jax.ShapeDtypeStruct((D_FF, s), BF16)],
        compiler_params=_params(("parallel", "parallel")),
    )(hu, hu, hu, hu, conv_w8, conv_w8, conv_b, conv_b)


def ffn_mid_bwd(hu, dact, conv_w8, conv_b, *, tm=1024, tc=256):
    s = hu.shape[0]
    ncol = D_FF // tc
    nrow = s // tm
    r8 = tm // 8

    def body(g_ref, v_ref, gp_ref, vp_ref, gn_ref, vn_ref, da_ref, dan_ref, wg_ref, wv_ref, bg_ref, bv_ref,
             dhu_ref, wgo_ref, wvo_ref):
        i = pl.program_id(1)
        first = i == 0
        last = i == nrow - 1

        def ext_of(cur_ref, prev_ref, next_ref):
            prev = jnp.where(first, 0.0, prev_ref[...])
            return jnp.concatenate([prev, cur_ref[...], next_ref[...]], axis=0)

        g_sh = _row_shifts(ext_of(g_ref, gp_ref, gn_ref), FFN_CONV)
        v_sh = _row_shifts(ext_of(v_ref, vp_ref, vn_ref), FFN_CONV)
        gc = _conv_rows(g_sh, wg_ref[...]) + bg_ref[...]
        vc = _conv_rows(v_sh, wv_ref[...]) + bv_ref[...]
        da_ext = jnp.concatenate([jnp.zeros((8, tc), F32), da_ref[...], jnp.where(last, 0.0, dan_ref[...])], axis=0)
        silu, dsilu = _silu_and_grad(gc)
        dgc = da_ext * vc * dsilu
        dvc = da_ext * silu
        dhu_ref[0] = _conv_rows_transposed(dgc, wg_ref[...], FFN_CONV)[8:8 + tm].astype(BF16)
        dhu_ref[1] = _conv_rows_transposed(dvc, wv_ref[...], FFN_CONV)[8:8 + tm].astype(BF16)

        cur = slice(8, 8 + tm)
        pg = _conv_weight_grad(dgc[cur], g_sh, cur, tc)
        pv = _conv_weight_grad(dvc[cur], v_sh, cur, tc)

        @pl.when(first)
        def _():
            wgo_ref[...] = pg
            wvo_ref[...] = pv

        @pl.when(i > 0)
        def _():
            wgo_ref[...] += pg
            wvo_ref[...] += pv

    def prev_idx(i):
        return jnp.maximum(i * r8 - 1, 0)

    def next_idx(i):
        return jnp.minimum((i + 1) * r8, s // 8 - 1)

    cur_g = pl.BlockSpec((tm, tc), lambda j, i: (i, j))
    cur_v = pl.BlockSpec((tm, tc), lambda j, i: (i, j + ncol))
    in_specs = [
        cur_g, cur_v,
        pl.BlockSpec((8, tc), lambda j, i: (prev_idx(i), j)),
        pl.BlockSpec((8, tc), lambda j, i: (prev_idx(i), j + ncol)),
        pl.BlockSpec((8, tc), lambda j, i: (next_idx(i), j)),
        pl.BlockSpec((8, tc), lambda j, i: (next_idx(i), j + ncol)),
        cur_g,
        pl.BlockSpec((8, tc), lambda j, i: (next_idx(i), j)),
        pl.BlockSpec((8, tc), lambda j, i: (0, j)),
        pl.BlockSpec((8, tc), lambda j, i: (0, j + ncol)),
        pl.BlockSpec((1, tc), lambda j, i: (0, j)),
        pl.BlockSpec((1, tc), lambda j, i: (0, j + ncol)),
    ]
    out_specs = [pl.BlockSpec((2, tm, tc), lambda j, i: (0, i, j)), pl.BlockSpec((8, tc), lambda j, i: (0, j)),
                 pl.BlockSpec((8, tc), lambda j, i: (0, j))]
    out_shape = [jax.ShapeDtypeStruct((2, s, D_FF), BF16),
                 jax.ShapeDtypeStruct((8, D_FF), F32), jax.ShapeDtypeStruct((8, D_FF), F32)]
    return pl.pallas_call(
        body, name="ffn_mid_bwd", grid=(ncol, nrow), in_specs=in_specs, out_specs=out_specs, out_shape=out_shape,
        compiler_params=_params(("parallel", "arbitrary")),
    )(hu, hu, hu, hu, hu, hu, dact, dact, conv_w8, conv_w8, conv_b, conv_b)


def _softplus(x):
    return jnp.maximum(x, 0.0) + jnp.log(1.0 + jnp.exp(-jnp.abs(x)))


def _cumsum_rows(v):
    n = v.shape[0]
    ri = _row_iota(v.shape)
    sh = 1
    while sh < n:
        v = v + jnp.where(ri >= sh, _shift_down(v, sh), 0.0)
        sh *= 2
    return v


def _rev_cumsum_rows(v):
    n = v.shape[0]
    ri = _row_iota(v.shape)
    sh = 1
    while sh < n:
        v = v + jnp.where(ri < n - sh, _shift_up(v, sh), 0.0)
        sh *= 2
    return v


def _total(v):
    return jnp.sum(jnp.sum(v, axis=1, keepdims=True), axis=0, keepdims=True)


def _ssd_in_specs(rev_nc=None):
    def ch(c):
        return c if rev_nc is None else rev_nc - 1 - c

    def prev(c):
        return jnp.maximum(ch(c) * (SSD_CHUNK // 8) - 1, 0)

    L = SSD_CHUNK
    return [
        pl.BlockSpec((L, 1024), lambda c: (ch(c), 0)),
        pl.BlockSpec((L, 1024), lambda c: (ch(c), 1)),
        pl.BlockSpec((L, 256), lambda c: (ch(c), 20)),
        pl.BlockSpec((L, 256), lambda c: (ch(c), 21)),
        pl.BlockSpec((8, 1024), lambda c: (prev(c), 1)),
        pl.BlockSpec((8, 256), lambda c: (prev(c), 20)),
        pl.BlockSpec((8, 256), lambda c: (prev(c), 21)),
        pl.BlockSpec((8, 1024), lambda c: (0, 0)),
        pl.BlockSpec((8, 256), lambda c: (0, 4)),
        pl.BlockSpec((8, 256), lambda c: (0, 5)),
        pl.BlockSpec((1, 1024), lambda c: (0, 0)),
        pl.BlockSpec((1, 256), lambda c: (0, 4)),
        pl.BlockSpec((1, 256), lambda c: (0, 5)),
        pl.BlockSpec((L, SMALL_COLS), lambda c: (ch(c), 0)),
        pl.BlockSpec((8, 128), lambda c: (0, 0)),
        pl.BlockSpec((1, 1024), lambda c: (0, 0)),
    ]


def _ssd_conv_pre(cur_ref, prev_ref, w_ref, b_ref, first):
    prev = jnp.where(first, 0.0, prev_ref[...])
    shifts = _row_shifts(jnp.concatenate([prev, cur_ref[...]], axis=0), SSD_CONV)
    return shifts, _conv_rows(shifts, w_ref[...])[8:] + b_ref[...]


def _ssd_time_consts(small_ref, sp_ref):
    dt_pre = small_ref[...] + sp_ref[0:1, :]
    dt = _softplus(dt_pre)
    a = -jnp.exp(sp_ref[1:2, :])
    acs = _cumsum_rows(dt * a)
    return dt_pre, dt, a, acs


def ssd_fwd(proj, small, conv_w8, conv_b, smallp, norm_w):
    s = proj.shape[0]
    nc = s // SSD_CHUNK
    L = SSD_CHUNK

    def body(z_ref, xs_ref, b_ref, c_ref, xsp_ref, bp_ref, cp_ref, wx_ref, wb_ref, wc_ref, bx_ref, bb_ref, bc_ref,
             small_ref, sp_ref, nw_ref, y_ref, yt_ref, ypre_ref, st_ref, state):
        first = pl.program_id(0) == 0

        @pl.when(first)
        def _():
            state[...] = jnp.zeros_like(state)

        xs = _ssd_conv_pre(xs_ref, xsp_ref, wx_ref, bx_ref, first)[1]
        xs = xs * _sigmoid(xs)
        bm = _ssd_conv_pre(b_ref, bp_ref, wb_ref, bb_ref, first)[1]
        bm = bm * _sigmoid(bm)
        cm = _ssd_conv_pre(c_ref, cp_ref, wc_ref, bc_ref, first)[1]
        cm = cm * _sigmoid(cm)
        _, dt, _, acs = _ssd_time_consts(small_ref, sp_ref)
        acs_t = acs.T
        li = _lane_iota((L, L))
        ri = _row_iota((L, L))
        tri = ri >= li
        lo = li < HEAD_DIM
        st_ref[0] = state[...]
        for g in range(2):
            bg = bm[:, 128 * g:128 * g + 128]
            cg = cm[:, 128 * g:128 * g + 128]
            gmat = _dot(cg, bg, "nt")
            for pp in range(4):
                p = 4 * g + pp
                h0, h1 = 2 * p, 2 * p + 1
                x = xs[:, 128 * p:128 * p + 128]
                a0, a1 = acs[:, h0:h0 + 1], acs[:, h1:h1 + 1]
                xdt = x * jnp.where(lo, dt[:, h0:h0 + 1], dt[:, h1:h1 + 1])
                m0 = gmat * jnp.exp(jnp.where(tri, a0 - acs_t[h0:h0 + 1, :], NEG_BIG))
                m1 = gmat * jnp.exp(jnp.where(tri, a1 - acs_t[h1:h1 + 1, :], NEG_BIG))
                yd = _dot(m0, jnp.where(lo, xdt, 0.0)) + _dot(m1, jnp.where(lo, 0.0, xdt))
                hin = state[p]
                yo = _dot(cg, hin, "nt") * jnp.exp(jnp.where(lo, a0, a1))
                dskip = jnp.where(lo[0:1], sp_ref[2:3, h0:h0 + 1], sp_ref[2:3, h1:h1 + 1])
                ypre_ref[:, 128 * p:128 * p + 128] = yd + yo + dskip * x
                al0, al1 = acs[L - 1:L, h0:h0 + 1], acs[L - 1:L, h1:h1 + 1]
                w = jnp.exp(jnp.where(lo, al0 - a0, al1 - a1))
                dec = jnp.exp(jnp.where(ri < HEAD_DIM, al0, al1))
                state[p] = dec * hin + _dot(xdt * w, bg, "tn")
        z = z_ref[...]
        yg = ypre_ref[...] * (z * _sigmoid(z))
        for g in range(2):
            seg = yg[:, 512 * g:512 * g + 512]
            r = lax.rsqrt(jnp.mean(seg * seg, axis=-1, keepdims=True) + NORM_EPS)
            out = (seg * r) * nw_ref[:, 512 * g:512 * g + 512]
            y_ref[:, 512 * g:512 * g + 512] = out.astype(BF16)
            yt_ref[512 * g:512 * g + 512, :] = out.T.astype(BF16)

    row = pl.BlockSpec((L, 1024), lambda c: (c, 0))
    return pl.pallas_call(
        body, name="ssd_fwd", grid=(nc,), in_specs=_ssd_in_specs(),
        out_specs=[row, pl.BlockSpec((1024, L), lambda c: (0, c)), row,
                   pl.BlockSpec((1, N_PAIRS, 128, 128), lambda c: (c, 0, 0, 0))],
        out_shape=[jax.ShapeDtypeStruct((s, 1024), BF16), jax.ShapeDtypeStruct((1024, s), BF16),
                   jax.ShapeDtypeStruct((s, 1024), F32), jax.ShapeDtypeStruct((nc, N_PAIRS, 128, 128), F32)],
        scratch_shapes=[pltpu.VMEM((N_PAIRS, 128, 128), F32)],
        compiler_params=_params(("arbitrary",)),
    )(proj, proj, proj, proj, proj, proj, proj, conv_w8, conv_w8, conv_w8, conv_b, conv_b, conv_b, small, smallp, norm_w)


def ssd_bwd(proj, small, conv_w8, conv_b, smallp, norm_w, ypre, states, dy, sel):
    s = proj.shape[0]
    nc = s // SSD_CHUNK
    L = SSD_CHUNK

    def body(z_ref, xs_ref, b_ref, c_ref, xsp_ref, bp_ref, cp_ref, wx_ref, wb_ref, wc_ref, bx_ref, bb_ref, bc_ref,
             small_ref, sp_ref, nw_ref, ypre_ref, st_ref, dy_ref, sel_ref,
             dz_ref, dxs_ref, db_ref, dc_ref, dsmall_ref, gwx_ref, gwb_ref, gwc_ref, gsp_ref, gnw_ref,
             dstate, carry_x, carry_b, carry_c, dxs_buf, dbm_buf, dcm_buf, qcs, col_sums, acs_terms, dt_terms):
        step = pl.program_id(0)
        col_sums[...] = jnp.zeros_like(col_sums)
        first_chunk = step == nc - 1
        start = step == 0

        @pl.when(start)
        def _():
            dstate[...] = jnp.zeros_like(dstate)
            carry_x[...] = jnp.zeros_like(carry_x)
            carry_b[...] = jnp.zeros_like(carry_b)
            carry_c[...] = jnp.zeros_like(carry_c)

        xs_ext, xs_pre = _ssd_conv_pre(xs_ref, xsp_ref, wx_ref, bx_ref, first_chunk)
        b_ext, b_pre = _ssd_conv_pre(b_ref, bp_ref, wb_ref, bb_ref, first_chunk)
        c_ext, c_pre = _ssd_conv_pre(c_ref, cp_ref, wc_ref, bc_ref, first_chunk)
        xs, xs_ds = _silu_and_grad(xs_pre)
        bm, b_ds = _silu_and_grad(b_pre)
        cm, c_ds = _silu_and_grad(c_pre)
        dt_pre, dt, a, acs = _ssd_time_consts(small_ref, sp_ref)
        acs_t = acs.T
        li = _lane_iota((L, L))
        ri = _row_iota((L, L))
        tri = ri >= li
        lo = li < HEAD_DIM
        lo_rows = ri < HEAD_DIM
        li1 = _lane_iota((1, L))

        z = z_ref[...]
        sz, dsz = _silu_and_grad(z)
        y = ypre_ref[...]
        yg = y * sz
        dout = dy_ref[...]
        dyg_parts = []
        gnw_parts = []
        for g in range(2):
            sl = slice(512 * g, 512 * g + 512)
            seg = yg[:, sl]
            r = lax.rsqrt(jnp.mean(seg * seg, axis=-1, keepdims=True) + NORM_EPS)
            n = seg * r
            gnw_parts.append(jnp.sum(dout[:, sl] * n, axis=0, keepdims=True))
            gg = dout[:, sl] * nw_ref[:, sl]
            dyg_parts.append(r * (gg - n * jnp.mean(gg * n, axis=-1, keepdims=True)))
        dyg = jnp.concatenate(dyg_parts, axis=1)
        gnw = jnp.concatenate(gnw_parts, axis=1)
        dz_ref[...] = (dyg * y * dsz).astype(BF16)
        dypre = dyg * sz

        qcs[...] = jnp.zeros_like(qcs)
        dalast = jnp.zeros((1, L), F32)
        for g in range(2):
            bg = bm[:, 128 * g:128 * g + 128]
            cg = cm[:, 128 * g:128 * g + 128]
            gmat = _dot(cg, bg, "nt")
            dgmat = jnp.zeros((L, L), F32)
            dbg = jnp.zeros((L, L), F32)
            dcg = jnp.zeros((L, L), F32)
            for pp in range(4):
                p = 4 * g + pp
                h0, h1 = 2 * p, 2 * p + 1
                lanes = slice(128 * p, 128 * p + 128)
                x = xs[:, lanes]
                dyp = dypre[:, lanes]
                a0, a1 = acs[:, h0:h0 + 1], acs[:, h1:h1 + 1]
                dtl = jnp.where(lo, dt[:, h0:h0 + 1], dt[:, h1:h1 + 1])
                xdt = x * dtl
                l0 = jnp.exp(jnp.where(tri, a0 - acs_t[h0:h0 + 1, :], NEG_BIG))
                l1 = jnp.exp(jnp.where(tri, a1 - acs_t[h1:h1 + 1, :], NEG_BIG))
                m0, m1 = gmat * l0, gmat * l1
                dskip = jnp.where(lo[0:1], sp_ref[2:3, h0:h0 + 1], sp_ref[2:3, h1:h1 + 1])
                col_sums[0:1, lanes] = jnp.sum(dyp * x, axis=0, keepdims=True)
                dx = dyp * dskip
                dy0, dy1 = jnp.where(lo, dyp, 0.0), jnp.where(lo, 0.0, dyp)
                x0, x1 = jnp.where(lo, xdt, 0.0), jnp.where(lo, 0.0, xdt)
                dm0, dm1 = _dot(dy0, x0, "nt"), _dot(dy1, x1, "nt")
                dxdt = _dot(m0, dy0, "tn") + _dot(m1, dy1, "tn")
                q0, q1 = dm0 * m0, dm1 * m1
                qcs[h0:h0 + 1, :] = jnp.sum(q0, axis=0, keepdims=True)
                qcs[h1:h1 + 1, :] = jnp.sum(q1, axis=0, keepdims=True)
                row_terms = jnp.where(lo, q0 + pltpu.roll(q0, HEAD_DIM, 1), q1 + pltpu.roll(q1, HEAD_DIM, 1))
                dgmat = dgmat + dm0 * l0 + dm1 * l1
                hin = st_ref[0, p]
                e = jnp.exp(jnp.where(lo, a0, a1))
                ch = _dot(cg, hin, "nt")
                dch = dyp * e
                dcg = dcg + _dot(dch, hin)
                dhin = _dot(dch, cg, "tn")
                dhout = dstate[p]
                al0, al1 = acs[L - 1:L, h0:h0 + 1], acs[L - 1:L, h1:h1 + 1]
                dec = jnp.exp(jnp.where(lo_rows, al0, al1))
                dhin = dhin + dec * dhout
                dal = dhout * hin * dec
                dal0 = _total(jnp.where(lo_rows, dal, 0.0))
                dal1 = _total(dal) - dal0
                dalast = dalast + jnp.where(li1 == h0, dal0, 0.0) + jnp.where(li1 == h1, dal1, 0.0)
                w = jnp.exp(jnp.where(lo, al0 - a0, al1 - a1))
                xw = xdt * w
                dxw = _dot(bg, dhout, "nt")
                dbg = dbg + _dot(xw, dhout)
                dxdt = dxdt + dxw * w
                dww = dxw * xw
                col_sums[1:2, lanes] = jnp.sum(dww, axis=0, keepdims=True)
                acs_terms[:, lanes] = row_terms + dch * ch - dww
                dx = dx + dxdt * dtl
                dt_terms[:, lanes] = dxdt * x
                dxs_buf[:, lanes] = dx
                dstate[p] = dhin
            dcg = dcg + _dot(dgmat, bg)
            dbg = dbg + _dot(dgmat, cg, "tn")
            dbm_buf[:, 128 * g:128 * g + 128] = dbg
            dcm_buf[:, 128 * g:128 * g + 128] = dcg

        head_sums = _split3_dot(col_sums[...], sel_ref[...])
        dskip_g = head_sums[0:1, :]
        dalast = dalast + head_sums[1:2, :]
        ddt = _split3_dot(dt_terms[...], sel_ref[...])
        dacs_tot = _split3_dot(acs_terms[...], sel_ref[...]) - qcs[...].T + jnp.where(ri == L - 1, dalast, 0.0)
        dstep = _rev_cumsum_rows(dacs_tot)
        ddt = ddt + dstep * a
        head_lane = li < N_HEADS
        ddt_pre = jnp.where(head_lane, ddt * _sigmoid(dt_pre), 0.0)
        dsmall_ref[...] = ddt_pre
        da = jnp.sum(jnp.where(head_lane, dstep * dt, 0.0), axis=0, keepdims=True)
        gsp = _stack_rows([jnp.sum(ddt_pre, axis=0, keepdims=True), da * a, dskip_g], L)

        def conv_back(dpost, ds, shifts, w_ref, carry, out_ref, width):
            dpre = dpost * ds
            dext = jnp.concatenate([dpre, carry[...]], axis=0)
            out_ref[...] = _conv_rows_transposed(dext, w_ref[...], SSD_CONV)[:L].astype(BF16)
            carry[...] = dpre[0:8]
            return _conv_weight_grad(dpre, shifts, slice(8, 8 + L), width)

        gwx = conv_back(dxs_buf[...], xs_ds, xs_ext, wx_ref, carry_x, dxs_ref, 1024)
        gwb = conv_back(dbm_buf[...], b_ds, b_ext, wb_ref, carry_b, db_ref, 256)
        gwc = conv_back(dcm_buf[...], c_ds, c_ext, wc_ref, carry_c, dc_ref, 256)

        @pl.when(start)
        def _():
            gwx_ref[...] = gwx
            gwb_ref[...] = gwb
            gwc_ref[...] = gwc
            gsp_ref[...] = gsp
            gnw_ref[...] = gnw

        @pl.when(step > 0)
        def _():
            gwx_ref[...] += gwx
            gwb_ref[...] += gwb
            gwc_ref[...] += gwc
            gsp_ref[...] += gsp
            gnw_ref[...] += gnw

    def ch(c):
        return nc - 1 - c

    row = pl.BlockSpec((L, 1024), lambda c: (ch(c), 0))
    row256 = pl.BlockSpec((L, 256), lambda c: (ch(c), 0))
    in_specs = _ssd_in_specs(rev_nc=nc) + [row, pl.BlockSpec((1, N_PAIRS, 128, 128), lambda c: (ch(c), 0, 0, 0)), row,
                                           pl.BlockSpec((1024, 128), lambda c: (0, 0))]
    out_specs = [row, row, row256, row256, pl.BlockSpec((L, 128), lambda c: (ch(c), 0)),
                 pl.BlockSpec((8, 1024), lambda c: (0, 0)), pl.BlockSpec((8, 256), lambda c: (0, 0)),
                 pl.BlockSpec((8, 256), lambda c: (0, 0)), pl.BlockSpec((8, 128), lambda c: (0, 0)),
                 pl.BlockSpec((1, 1024), lambda c: (0, 0))]
    out_shape = [jax.ShapeDtypeStruct((s, 1024), BF16), jax.ShapeDtypeStruct((s, 1024), BF16),
                 jax.ShapeDtypeStruct((s, 256), BF16), jax.ShapeDtypeStruct((s, 256), BF16),
                 jax.ShapeDtypeStruct((s, 128), F32),
                 jax.ShapeDtypeStruct((8, 1024), F32), jax.ShapeDtypeStruct((8, 256), F32),
                 jax.ShapeDtypeStruct((8, 256), F32), jax.ShapeDtypeStruct((8, 128), F32),
                 jax.ShapeDtypeStruct((1, 1024), F32)]
    scratch = [pltpu.VMEM((N_PAIRS, 128, 128), F32), pltpu.VMEM((8, 1024), F32), pltpu.VMEM((8, 256), F32),
               pltpu.VMEM((8, 256), F32), pltpu.VMEM((L, 1024), F32), pltpu.VMEM((L, 256), F32), pltpu.VMEM((L, 256), F32),
               pltpu.VMEM((L, L), F32), pltpu.VMEM((8, 1024), F32), pltpu.VMEM((L, 1024), F32), pltpu.VMEM((L, 1024), F32)]
    return pl.pallas_call(
        body, name="ssd_bwd", grid=(nc,), in_specs=in_specs, out_specs=out_specs, out_shape=out_shape,
        scratch_shapes=scratch, compiler_params=_params(("arbitrary",)),
    )(proj, proj, proj, proj, proj, proj, proj, conv_w8, conv_w8, conv_w8, conv_b, conv_b, conv_b, small, smallp, norm_w,
      ypre, states, dy, sel)


FOX_SCALE = HEAD_DIM ** -0.5
FOX_T = 256
Q_COL, K_COL, V_COL = 2, 3, 4


def _split3_dot(v, m):
    hi = v.astype(BF16)
    r1 = v - hi.astype(F32)
    mid = r1.astype(BF16)
    lo = (r1 - mid.astype(F32)).astype(BF16)
    return _dot(hi, m) + _dot(mid, m) + _dot(lo, m)


def _head_rstd(x, sel_ref, selt_ref):
    ms = _split3_dot(x * x, sel_ref[...]) * (1.0 / HEAD_DIM)
    return _split3_dot(lax.rsqrt(ms + NORM_EPS), selt_ref[...])


def fox_tables():
    r = np.arange(3 * 128)
    piece, lane = r // 128, r % 128
    head = lane - F_LANE
    is_head = np.logical_and(head >= 0, head < N_HEADS)
    col = 128 * (head // 2) + HEAD_DIM * (1 - head % 2) + piece
    cols = np.arange(1024)
    place_q = np.logical_and(is_head[:, None], cols[None, :] == col[:, None])
    place_k = np.logical_and(is_head[:, None], cols[None, :] == (col + 3)[:, None])
    ones_q = np.logical_and(cols % HEAD_DIM >= 3, cols % HEAD_DIM < 6)[None]
    ones_k = (cols % HEAD_DIM < 3)[None]
    h = np.arange(128) - F_LANE
    ok = np.logical_and(h >= 0, h < N_HEADS)
    same_pair = cols[:, None] // 128 == (h // 2)[None, :]
    fold_even = np.logical_and(np.logical_and(ok, h % 2 == 0)[None, :], same_pair)
    fold_odd = np.logical_and(np.logical_and(ok, h % 2 == 1)[None, :], same_pair)
    as_bf16 = lambda t: jnp.asarray(t.astype(np.float32), BF16)
    return (as_bf16(place_q), as_bf16(place_k), jnp.asarray(ones_q, F32), jnp.asarray(ones_k, F32),
            as_bf16(fold_even), as_bf16(fold_odd))


def fox_prep(proj, small, smallp, qw, kw, sel, selt, place_q, place_k, ones_q, ones_k, *, tm=256):
    s = proj.shape[0]

    def body(q_ref, k_ref, v_ref, small_ref, sp_ref, qw_ref, kw_ref, sel_ref, selt_ref, pq_ref, pk_ref, oq_ref, ok_ref,
             qn_ref, kn_ref, aq_ref, ak_ref, vb_ref, knt_ref, akt_ref, vt_ref, carry):
        @pl.when(pl.program_id(0) == 0)
        def _():
            carry[...] = jnp.zeros_like(carry)

        q = q_ref[...]
        qn_ref[...] = (((q * _head_rstd(q, sel_ref, selt_ref)) * qw_ref[...]) * FOX_SCALE).astype(BF16)
        k = k_ref[...]
        kn = ((k * _head_rstd(k, sel_ref, selt_ref)) * kw_ref[...]).astype(BF16)
        kn_ref[...] = kn
        knt_ref[...] = kn.astype(F32).T.astype(BF16)
        vb_ref[...] = v_ref[...].astype(BF16)
        vt_ref[...] = v_ref[...].T.astype(BF16)
        li = _lane_iota((tm, 128))
        f_lane = jnp.logical_and(li >= F_LANE, li < F_LANE + N_HEADS)
        logf = jnp.where(f_lane, -_softplus(-(small_ref[...] + sp_ref[3:4, :])), 0.0)
        cum = _cumsum_rows(logf) + carry[...]
        carry[...] = cum[tm - 1:tm, :]
        hi = cum.astype(BF16)
        r1 = cum - hi.astype(F32)
        mid = r1.astype(BF16)
        lo = (r1 - mid.astype(F32)).astype(BF16)
        pieces = jnp.concatenate([hi, mid, lo], axis=1)
        aq_ref[...] = (_dot(pieces, pq_ref[...]) + oq_ref[...]).astype(BF16)
        ak = ok_ref[...] - _dot(pieces, pk_ref[...])
        ak_ref[...] = ak.astype(BF16)
        akt_ref[...] = ak.T.astype(BF16)

    row = pl.BlockSpec((tm, 1024), lambda i: (i, 0))
    col = pl.BlockSpec((1024, tm), lambda i: (0, i))
    vec = pl.BlockSpec((1, 1024), lambda i: (0, 0))
    table = pl.BlockSpec((384, 1024), lambda i: (0, 0))
    wide = jax.ShapeDtypeStruct((s, 1024), BF16)
    tall = jax.ShapeDtypeStruct((1024, s), BF16)
    return pl.pallas_call(
        body, name="fox_prep", grid=(s // tm,),
        in_specs=[pl.BlockSpec((tm, 1024), lambda i: (i, Q_COL)), pl.BlockSpec((tm, 1024), lambda i: (i, K_COL)),
                  pl.BlockSpec((tm, 1024), lambda i: (i, V_COL)),
                  pl.BlockSpec((tm, 128), lambda i: (i, 0)), pl.BlockSpec((8, 128), lambda i: (0, 0)), vec, vec,
                  pl.BlockSpec((1024, 128), lambda i: (0, 0)), pl.BlockSpec((128, 1024), lambda i: (0, 0)),
                  table, table, vec, vec],
        out_specs=[row, row, row, row, row, col, col, col],
        out_shape=[wide, wide, wide, wide, wide, tall, tall, tall],
        scratch_shapes=[pltpu.VMEM((1, 128), F32)], compiler_params=_params(("arbitrary",)),
    )(proj, proj, proj, small, smallp, qw, kw, sel, selt, place_q, place_k, ones_q, ones_k)


def fox_fwd(qn, kn, aq, ak, vt, shards=()):
    s = qn.shape[0]
    t = FOX_T
    nq = s // t
    ng = len(shards)

    def body(*refs):
        q_ref, k_ref, aq_ref, ak_ref, vt_ref = refs[:5]
        o_ref, ot_ref, lse_ref = refs[5 + ng:8 + ng]
        p = pl.program_id(0)
        if ng:
            start, forward, finish = _gather_phases(refs[5:5 + ng], refs[8 + ng:8 + 2 * ng], *refs[8 + 2 * ng:])
            pl.when(p == 0)(start)
            pl.when(p == N_PAIRS // 2)(forward)

        @pl.when(p == 0)
        def _():
            lse_ref[...] = jnp.zeros_like(lse_ref)

        lo = _lane_iota((t, 128)) < HEAD_DIM
        lo_rows = _row_iota((128, t)) < HEAD_DIM
        causal_t = _lane_iota((t, t)) >= _row_iota((t, t))

        def q_loop(qi, _):
            q0 = pl.multiple_of(qi * t, t)
            qv, aqv = q_ref[pl.ds(q0, t), :], aq_ref[pl.ds(q0, t), :]
            qa, qb = jnp.where(lo, qv, aqv), jnp.where(lo, aqv, qv)

            def scores(kj):
                k0 = pl.multiple_of(kj * t, t)
                kv, akv = k_ref[pl.ds(k0, t), :], ak_ref[pl.ds(k0, t), :]
                return _dot(jnp.where(lo, kv, akv), qa, "nt"), _dot(jnp.where(lo, akv, kv), qb, "nt")

            def update(kj, stats, s0, s1):
                m0, l0, m1, l1, acc = stats
                vtv = vt_ref[:, pl.ds(pl.multiple_of(kj * t, t), t)]
                n0 = jnp.maximum(m0, jnp.max(s0, axis=0, keepdims=True))
                n1 = jnp.maximum(m1, jnp.max(s1, axis=0, keepdims=True))
                a0, a1 = jnp.exp(m0 - n0), jnp.exp(m1 - n1)
                p0, p1 = jnp.exp(s0 - n0), jnp.exp(s1 - n1)
                l0 = a0 * l0 + jnp.sum(p0, axis=0, keepdims=True)
                l1 = a1 * l1 + jnp.sum(p1, axis=0, keepdims=True)
                acc = (jnp.where(lo_rows, a0, a1) * acc + _dot(jnp.where(lo_rows, vtv, 0.0), p0)
                       + _dot(jnp.where(lo_rows, 0.0, vtv), p1))
                return n0, l0, n1, l1, acc

            def step(kj, carry):
                stats, (s0, s1) = carry[:5], carry[5:]
                nxt = scores(kj + 1)
                return (*update(kj, stats, s0, s1), *nxt)

            def row(val):
                return jnp.full((1, t), val, F32)

            init = (row(NEG_BIG), row(0.0), row(NEG_BIG), row(0.0), jnp.zeros((128, t), F32), *scores(0))
            carry = lax.fori_loop(0, qi, step, init)
            s0, s1 = jnp.where(causal_t, carry[5], NEG_BIG), jnp.where(causal_t, carry[6], NEG_BIG)
            m0, l0, m1, l1, acc = update(qi, carry[:5], s0, s1)
            out_t = acc / jnp.where(lo_rows, l0, l1)
            ot_ref[:, pl.ds(q0, t)] = out_t.astype(BF16)
            o_ref[pl.ds(q0, t), :] = out_t.T.astype(BF16)
            ri = _row_iota((N_HEADS, t))
            old = lse_ref[:, pl.ds(q0, t)]
            lse_ref[:, pl.ds(q0, t)] = jnp.where(
                ri == 2 * p, m0 + jnp.log(l0), jnp.where(ri == 2 * p + 1, m1 + jnp.log(l1), old))
            return 0

        lax.fori_loop(0, nq, q_loop, 0)
        if ng:
            pl.when(p == N_PAIRS - 1)(finish)

    pair = pl.BlockSpec((s, 128), lambda p: (0, p))
    outs = pl.pallas_call(
        body, name="fox_fwd", grid=(N_PAIRS,),
        in_specs=[pair] * 4 + [pl.BlockSpec((128, s), lambda p: (p, 0))] + [ANY] * ng,
        out_specs=[pair, pl.BlockSpec((128, s), lambda p: (p, 0)), pl.BlockSpec((N_HEADS, s), lambda p: (0, 0))] + [ANY] * ng,
        out_shape=[jax.ShapeDtypeStruct((s, 1024), BF16), jax.ShapeDtypeStruct((1024, s), BF16),
                   jax.ShapeDtypeStruct((N_HEADS, s), F32)] + _gather_out_shapes(shards),
        scratch_shapes=_gather_scratch(ng) if ng else [],
        compiler_params=_params(("arbitrary",)),
    )(qn, kn, aq, ak, vt, *shards)
    return outs[0], outs[1], outs[2], list(outs[3:])


def fox_bwd(qn, kn, aq, ak, knt, akt, vb, lse, dmixed, parts=()):
    s = qn.shape[0]
    t = FOX_T
    nq = s // t
    once = pl.Buffered(1)
    ns = len(parts)

    def body(*refs):
        q_ref, k_ref, aq_ref, ak_ref, kt_ref, akt_ref, v_ref, lse_ref, do_ref = refs[:9]
        dq_ref, dk_ref, dv_ref, dc0_ref, dc1_ref = refs[9 + ns:14 + ns]
        p_scr, dp_scr = refs[14 + 2 * ns:16 + 2 * ns]
        p = pl.program_id(0)
        if ns:
            start, finish = _scatter_phases(refs[9:9 + ns], refs[14 + ns:14 + 2 * ns], *refs[16 + 2 * ns:])
            pl.when(p == 0)(start)
        dk_ref[...] = jnp.zeros_like(dk_ref)
        dv_ref[...] = jnp.zeros_like(dv_ref)
        dc0_ref[...] = jnp.zeros_like(dc0_ref)
        dc1_ref[...] = jnp.zeros_like(dc1_ref)
        lo = _lane_iota((t, 128)) < HEAD_DIM
        lo_rows = _row_iota((128, t)) < HEAD_DIM
        causal_t = _lane_iota((t, t)) >= _row_iota((t, t))

        def q_loop(qi, _):
            q0 = pl.multiple_of(qi * t, t)
            qv, aqv = q_ref[pl.ds(q0, t), :], aq_ref[pl.ds(q0, t), :]
            qa, qb = jnp.where(lo, qv, aqv), jnp.where(lo, aqv, qv)
            do = do_ref[pl.ds(q0, t), :]
            doa, dob = jnp.where(lo, do, 0.0).astype(BF16), jnp.where(lo, 0.0, do).astype(BF16)
            lse_blk = lse_ref[:, pl.ds(q0, t)]
            ri = _row_iota((N_HEADS, t))
            lse0 = jnp.sum(jnp.where(ri == 2 * p, lse_blk, 0.0), axis=0, keepdims=True)
            lse1 = jnp.sum(jnp.where(ri == 2 * p + 1, lse_blk, 0.0), axis=0, keepdims=True)

            def scores(kj):
                k0 = pl.multiple_of(kj * t, t)
                kv, akv = k_ref[pl.ds(k0, t), :], ak_ref[pl.ds(k0, t), :]
                return _dot(jnp.where(lo, kv, akv), qa, "nt"), _dot(jnp.where(lo, akv, kv), qb, "nt")

            def pass1(kj, d0, d1, diagonal):
                k0 = pl.multiple_of(kj * t, t)
                vv = v_ref[pl.ds(k0, t), :]
                s0, s1 = scores(kj)
                if diagonal:
                    s0, s1 = jnp.where(causal_t, s0, NEG_BIG), jnp.where(causal_t, s1, NEG_BIG)
                p0, p1 = jnp.exp(s0 - lse0), jnp.exp(s1 - lse1)
                dp0, dp1 = _dot(vv, doa, "nt"), _dot(vv, dob, "nt")
                p_scr[0, kj], p_scr[1, kj] = p0, p1
                dp_scr[0, kj], dp_scr[1, kj] = dp0, dp1
                dv_ref[pl.ds(k0, t), :] += _dot(p0, doa) + _dot(p1, dob)
                return d0 + jnp.sum(p0 * dp0, axis=0, keepdims=True), d1 + jnp.sum(p1 * dp1, axis=0, keepdims=True)

            zero = jnp.zeros((1, t), F32)
            d0, d1 = lax.fori_loop(0, qi, lambda kj, c: pass1(kj, *c, False), (zero, zero))
            d0, d1 = pass1(qi, d0, d1, True)

            def pass2(kj, carry):
                dq0, dq1 = carry
                k0 = pl.multiple_of(kj * t, t)
                p0, p1 = p_scr[0, kj], p_scr[1, kj]
                ds0, ds1 = p0 * (dp_scr[0, kj] - d0), p1 * (dp_scr[1, kj] - d1)
                dk_ref[pl.ds(k0, t), :] += jnp.where(lo, _dot(ds0, qa), _dot(ds1, qb))
                dc0_ref[pl.ds(k0, t), :] += ds0[:, :128] + ds0[:, 128:]
                dc1_ref[pl.ds(k0, t), :] += ds1[:, :128] + ds1[:, 128:]
                ktv, aktv = kt_ref[:, pl.ds(k0, t)], akt_ref[:, pl.ds(k0, t)]
                return dq0 + _dot(jnp.where(lo_rows, ktv, aktv), ds0), dq1 + _dot(jnp.where(lo_rows, aktv, ktv), ds1)

            zq = jnp.zeros((128, t), F32)
            dq0, dq1 = lax.fori_loop(0, qi + 1, pass2, (zq, zq))
            dq_ref[pl.ds(q0, t), :] = jnp.where(lo_rows, dq0, dq1).T
            return 0

        lax.fori_loop(0, nq, q_loop, 0)
        if ns:
            pl.when(p == N_PAIRS - 1)(finish)

    pair = pl.BlockSpec((s, 128), lambda p: (0, p), pipeline_mode=once)
    pair_t = pl.BlockSpec((128, s), lambda p: (p, 0), pipeline_mode=once)
    out = jax.ShapeDtypeStruct((s, 1024), F32)
    outs = pl.pallas_call(
        body, name="fox_bwd", grid=(N_PAIRS,),
        in_specs=[pair, pair, pair, pair, pair_t, pair_t, pair, pl.BlockSpec((N_HEADS, s), lambda p: (0, 0)),
                  pl.BlockSpec((s, 128), lambda p: (0, 8 + p), pipeline_mode=once)] + [ANY] * ns,
        out_specs=[pair] * 5 + [ANY] * ns,
        out_shape=[out] * 5 + [jax.ShapeDtypeStruct(p.shape, p.dtype) for p in parts],
        scratch_shapes=[pltpu.VMEM((2, nq, t, t), F32), pltpu.VMEM((2, nq, t, t), F32)] + (_scatter_scratch(ns) if ns else []),
        compiler_params=_params(("arbitrary",)),
    )(qn, kn, aq, ak, knt, akt, vb, lse, dmixed, *parts)
    return (*outs[:5], _keep_own_blocks(outs[5:], parts))


def fox_post(dqn, dkn, dc0, dc1, proj, small, smallp, qw, kw, sel, selt, fold_even, fold_odd, *, tm=256):
    s = proj.shape[0]
    nrow = s // tm

    def body(dqn_ref, dkn_ref, dc0_ref, dc1_ref, q_ref, k_ref, small_ref, sp_ref, qw_ref, kw_ref, sel_ref, selt_ref,
             fe_ref, fo_ref, dq_ref, dk_ref, dsmall_ref, gqw_ref, gkw_ref, gfb_ref, carry):
        step = pl.program_id(0)

        @pl.when(step == 0)
        def _():
            carry[...] = jnp.zeros_like(carry)

        def norm_bwd(x_ref, w_ref, dn, out_ref):
            x = x_ref[...]
            rf = _head_rstd(x, sel_ref, selt_ref)
            xh = x * rf
            g = dn * w_ref[...]
            mean_gx = _split3_dot(_split3_dot(g * xh, sel_ref[...]) * (1.0 / HEAD_DIM), selt_ref[...])
            out_ref[...] = (rf * (g - xh * mean_gx)).astype(BF16)
            return jnp.sum(dn * xh, axis=0, keepdims=True)

        gqw = norm_bwd(q_ref, qw_ref, dqn_ref[...] * FOX_SCALE, dq_ref)
        gkw = norm_bwd(k_ref, kw_ref, dkn_ref[...], dk_ref)
        li = _lane_iota((tm, 128))
        f_lane = jnp.logical_and(li >= F_LANE, li < F_LANE + N_HEADS)
        dcum = -(_split3_dot(dc0_ref[...], fe_ref[...]) + _split3_dot(dc1_ref[...], fo_ref[...]))
        dlogf = _rev_cumsum_rows(dcum) + carry[...]
        carry[...] = dlogf[0:1, :]
        dfr = jnp.where(f_lane, dlogf * _sigmoid(-(small_ref[...] + sp_ref[3:4, :])), 0.0)
        dsmall_ref[...] = dfr
        gfb = jnp.sum(dfr, axis=0, keepdims=True)

        @pl.when(step == 0)
        def _():
            gqw_ref[...] = gqw
            gkw_ref[...] = gkw
            gfb_ref[...] = gfb

        @pl.when(step > 0)
        def _():
            gqw_ref[...] += gqw
            gkw_ref[...] += gkw
            gfb_ref[...] += gfb

    def rb(i):
        return nrow - 1 - i

    row = pl.BlockSpec((tm, 1024), lambda i: (rb(i), 0))
    vec = pl.BlockSpec((1, 1024), lambda i: (0, 0))
    fold = pl.BlockSpec((1024, 128), lambda i: (0, 0))
    return pl.pallas_call(
        body, name="fox_post", grid=(nrow,),
        in_specs=[row, row, row, row, pl.BlockSpec((tm, 1024), lambda i: (rb(i), Q_COL)),
                  pl.BlockSpec((tm, 1024), lambda i: (rb(i), K_COL)),
                  pl.BlockSpec((tm, 128), lambda i: (rb(i), 0)), pl.BlockSpec((8, 128), lambda i: (0, 0)), vec, vec,
                  fold, pl.BlockSpec((128, 1024), lambda i: (0, 0)), fold, fold],
        out_specs=[row, row, pl.BlockSpec((tm, 128), lambda i: (rb(i), 0)), vec, vec, pl.BlockSpec((1, 128), lambda i: (0, 0))],
        out_shape=[jax.ShapeDtypeStruct((s, 1024), BF16), jax.ShapeDtypeStruct((s, 1024), BF16),
                   jax.ShapeDtypeStruct((s, 128), F32), jax.ShapeDtypeStruct((1, 1024), F32),
                   jax.ShapeDtypeStruct((1, 1024), F32), jax.ShapeDtypeStruct((1, 128), F32)],
        scratch_shapes=[pltpu.VMEM((1, 128), F32)], compiler_params=_params(("arbitrary",)),
    )(dqn, dkn, dc0, dc1, proj, proj, small, smallp, qw, kw, sel, selt, fold_even, fold_odd)


def local_step(x, target, wm, ws, later_shards, ssd_cw8, ssd_cb, smallp, ssd_nw, qw_t, kw_t, sel, selt,
               norm_mix_w, norm_ffn_w, ffn_cw8, ffn_cb):
    h, h_t = rms_fwd(x, norm_mix_w, name="rms_mix_fwd")
    proj = matmul(h, wm, mode="nn", tm=1024, tn=1408, tk=1024, out_dtype=F32, name="mm_in_proj")
    small = matmul(h, ws, mode="nn", tm=1024, tn=128, tk=1024, out_dtype=F32, name="mm_in_proj_small")
    y_ssd, y_ssd_t, ypre, states = ssd_fwd(proj, small, ssd_cw8, ssd_cb, smallp, ssd_nw)
    place_q, place_k, ones_q, ones_k, fold_even, fold_odd = fox_tables()
    qn, kn, aq, ak, vb, knt, akt, vt = fox_prep(proj, small, smallp, qw_t, kw_t, sel, selt, place_q, place_k, ones_q, ones_k)
    y_fox, y_fox_t, lse, (a_out, a_up, a_down) = fox_fwd(qn, kn, aq, ak, vt, shards=later_shards)
    w_out = a_out.reshape(2048, D_MODEL)
    w_down = a_down.reshape(D_FF, D_MODEL)
    s = x.shape[0]
    shard = lambda index: pl.BlockSpec((None, 1024, 1408), index)
    x1 = matmul(y_ssd, w_out, mode="nn", tm=1024, tn=1024, tk=1024, out_dtype=F32, name="mm_out_ssd", add=x)
    x1 = matmul(y_fox, w_out, mode="nn", tm=1024, tn=1024, tk=1024, out_dtype=F32, name="mm_out_fox", add=x1, b_koff=1)
    hf, hf_t = rms_fwd(x1, norm_ffn_w, name="rms_ffn_fwd")
    hu = matmul(hf, a_up, mode="nn", tm=1024, tn=1408, tk=1024, out_dtype=F32, name="mm_up",
                layout=dict(m=s, n=2 * D_FF, k=D_MODEL, b_spec=shard(lambda i, j, kk: (j, kk, 0))))
    act, act_t = ffn_mid_fwd(hu, ffn_cw8, ffn_cb)
    y = matmul(act, w_down, mode="nn", tm=1024, tn=1024, tk=1408, out_dtype=F32, name="mm_down", add=x1)
    dy, sq = loss_head(y, target)

    dact = matmul(dy, w_down, mode="nt", tm=1024, tn=1408, tk=1024, out_dtype=F32, name="mm_dact")
    g_down = matmul(act_t, dy, mode="nn", tm=1408, tn=1024, tk=1024, out_dtype=BF16, name="mm_dw_down")
    dhu, gcw_g, gcw_v = ffn_mid_bwd(hu, dact, ffn_cw8, ffn_cb)
    dhf = matmul(dhu, a_up, mode="nt", tm=1024, tn=1024, tk=1408, out_dtype=F32, name="mm_dhf",
                 layout=dict(m=s, n=D_MODEL, k=2 * D_FF, a_spec=shard(lambda i, j, kk: (kk // 2, i, kk % 2)),
                             b_spec=shard(lambda i, j, kk: (kk, 0, 0))))
    g_up = matmul(hf_t, dhu, mode="nn", tm=1024, tn=1408, tk=1024, out_dtype=BF16, name="mm_dw_up",
                  layout=dict(m=D_MODEL, n=2 * D_FF, k=s, b_spec=shard(lambda i, j, kk: (j // 2, kk, j % 2)),
                              o_spec=shard(lambda i, j, kk: (j, i, 0)), out_shape=(4, D_MODEL, 1408)))
    dx1, g_norm_ffn = rms_bwd(dhf, x1, norm_ffn_w, dy, name="rms_ffn_bwd")
    dmixed = matmul(dx1, w_out, mode="nt", tm=1024, tn=1024, tk=1024, out_dtype=F32, name="mm_dmixed")
    g_out_a = matmul(y_ssd_t, dx1, mode="nn", tm=1024, tn=1024, tk=1024, out_dtype=BF16, name="mm_dw_out_ssd")
    g_out_b = matmul(y_fox_t, dx1, mode="nn", tm=1024, tn=1024, tk=1024, out_dtype=BF16, name="mm_dw_out_fox")
    early = [jnp.concatenate([g_out_a, g_out_b], axis=0).reshape(4, 512, D_MODEL), g_up, g_down.reshape(4, 704, D_MODEL)]
    mine, theirs = pair_swap_halves(early, name="pair_swap_early")
    parts = [add_pair(a, b, name="add_pair_" + n, tr=ADAM_ROWS[n]) for a, b, n in zip(mine, theirs, BIG_NAMES[1:])]
    dz, dxs, db, dc, dsmall_ssd, gcw_x, gcw_b, gcw_c, g_sp, g_ssd_nw = ssd_bwd(
        proj, small, ssd_cw8, ssd_cb, smallp, ssd_nw, ypre, states, dmixed, sel)
    dqn, dkn, dv, dc0, dc1, landed_early = fox_bwd(qn, kn, aq, ak, knt, akt, vb, lse, dmixed, parts=parts)
    dq, dk, dsmall_fox, g_qw, g_kw, g_fb = fox_post(dqn, dkn, dc0, dc1, proj, small, smallp, qw_t, kw_t, sel, selt,
                                                    fold_even, fold_odd)
    dproj = jnp.concatenate([dz, dxs, dq, dk, dv.astype(BF16), db, dc], axis=1)
    dsmall = (dsmall_ssd + dsmall_fox).astype(BF16)
    g_wm = matmul(h_t, dproj, mode="nn", tm=1024, tn=1408, tk=1024, out_dtype=BF16, name="mm_dw_in")
    g_ws = matmul(h_t, dsmall, mode="nn", tm=1024, tn=128, tk=1024, out_dtype=BF16, name="mm_dw_in_small")
    g_in = jnp.concatenate([g_wm[:, :2048], g_wm[:, 5120:5632], g_ws[:, :16], g_wm[:, 2048:5120], g_ws[:, 16:32]], axis=1)
    mine, theirs = pair_swap_halves([g_in.reshape(D_MODEL, 4, 1416).transpose(1, 0, 2)], name="pair_swap_w_in")
    part_in = add_pair(mine[0], theirs[0], name="add_pair_w_in", tr=ADAM_ROWS["w_in"])
    dh, landed_in = matmul(dproj, wm, mode="nt", tm=1024, tn=1024, tk=1408, out_dtype=F32, name="mm_dh", scatter=[part_in])
    dh = matmul(dsmall, ws, mode="nt", tm=1024, tn=1024, tk=128, out_dtype=F32, name="mm_dh_small", add=dh)
    grad_x, g_norm_mix = rms_bwd(dh, x, norm_mix_w, dx1, name="rms_mix_bwd")
    return dict(
        sq=sq, grad_x=grad_x, landed=landed_in + landed_early,
        g_norm_mix=g_norm_mix, g_norm_ffn=g_norm_ffn, g_ssd_nw=g_ssd_nw,
        g_ssd_cw=jnp.concatenate([gcw_x, gcw_b, gcw_c], axis=1), g_sp=g_sp, g_fb=g_fb, g_qw=g_qw, g_kw=g_kw,
        g_ffn_cw=jnp.concatenate([gcw_g, gcw_v], axis=1))


def adamw(w, g, m, v, *, name, tr):
    rows, cols = w.shape

    def body(w_ref, g_ref, m_ref, v_ref, d_ref, mo_ref, vo_ref):
        gv = g_ref[...]
        mn = ADAM_B1 * m_ref[...] + (1.0 - ADAM_B1) * gv
        vn = ADAM_B2 * v_ref[...] + (1.0 - ADAM_B2) * (gv * gv)
        m_hat = mn / (1.0 - ADAM_B1 ** ADAM_STEP)
        v_hat = vn / (1.0 - ADAM_B2 ** ADAM_STEP)
        d_ref[...] = -ADAM_LR * (m_hat / (jnp.sqrt(v_hat) + ADAM_EPS) + ADAM_WD * w_ref[...])
        mo_ref[...] = mn
        vo_ref[...] = vn

    blk = pl.BlockSpec((tr, cols), lambda i: (i, 0))
    shp = jax.ShapeDtypeStruct((rows, cols), F32)
    return pl.pallas_call(
        body, name=name, grid=(rows // tr,), in_specs=[blk] * 4, out_specs=[blk] * 3, out_shape=[shp] * 3,
        compiler_params=_params(("parallel",)),
    )(w, g, m, v)


def add_pair(a, b, *, name, tr):
    _, rows, cols = a.shape

    def body(a_ref, b_ref, o_ref):
        o_ref[...] = (a_ref[...].astype(F32) + b_ref[...].astype(F32)).astype(BF16)

    blk = pl.BlockSpec((1, tr, cols), lambda j, i: (j, i, 0))
    return pl.pallas_call(
        body, name=name, grid=(4, rows // tr), in_specs=[blk, blk], out_specs=blk,
        out_shape=jax.ShapeDtypeStruct(a.shape, BF16), compiler_params=_params(("parallel", "parallel")),
    )(a, b)


def sum_chips(parts, *, name, tr):
    _, rows, cols = parts.shape

    def body(p_ref, o_ref):
        acc = p_ref[0].astype(F32)
        for k in range(1, 4):
            acc = acc + p_ref[k].astype(F32)
        o_ref[...] = acc

    return pl.pallas_call(
        body, name=name, grid=(rows // tr,), in_specs=[pl.BlockSpec((4, tr, cols), lambda i: (0, i, 0))],
        out_specs=pl.BlockSpec((tr, cols), lambda i: (i, 0)), out_shape=jax.ShapeDtypeStruct((rows, cols), F32),
        compiler_params=_params(("parallel",)),
    )(parts)


ANY = pl.BlockSpec(memory_space=pl.ANY)


def _place():
    x, y, c = lax.axis_index("x"), lax.axis_index("y"), lax.axis_index("c")
    chips = [(1 - x, y), (x, 1 - y), (1 - x, 1 - y)]
    return x, y, c, chips


def _chunks(rows):
    size = next((c for c in (128, 176, 64, 32, 16, 8) if rows % c == 0), rows)
    return [(r, size) for r in range(0, rows, size)]


def gather_weights(shards):
    n = len(shards)

    def body(*refs):
        start, forward, finish = _gather_phases(refs[:n], refs[n:2 * n], *refs[2 * n:])
        start()
        forward()
        finish()

    gathered = pl.pallas_call(
        body, name="gather_weights", in_specs=[ANY] * n, out_specs=[ANY] * n,
        out_shape=_gather_out_shapes(shards), scratch_shapes=_gather_scratch(n),
    )(*shards)
    return gathered


def _gather_out_shapes(shards):
    return [jax.ShapeDtypeStruct((4,) + s.shape, s.dtype) for s in shards]


def _gather_scratch(n):
    return [pltpu.SemaphoreType.DMA((n, 7)), pltpu.SemaphoreType.DMA((n, 7))]


def _gather_phases(ins, outs, send_sems, recv_sems):
    n = len(ins)
    x, y, c, chips = _place()
    me = 2 * x + y
    sibling = (x, y, 1 - c)
    blks = [2 * cx + cy for cx, cy in chips]

    def half(a, blk, r=0, nr=None):
        rows = ins[a].shape[0] // 2
        return outs[a].at[blk, pl.ds(c * rows + r, rows if nr is None else nr), :]

    def to_chip(a, t, r=0, nr=None):
        rows = ins[a].shape[0] // 2
        return pltpu.make_async_remote_copy(
            src_ref=ins[a].at[pl.ds(c * rows + r, rows if nr is None else nr), :], dst_ref=half(a, me, r, nr),
            send_sem=send_sems.at[a, t], recv_sem=recv_sems.at[a, t], device_id=(*chips[t], c), device_id_type=MESH)

    def from_chip(a, t):
        return pltpu.make_async_remote_copy(
            src_ref=half(a, blks[t]), dst_ref=half(a, blks[t]), send_sem=send_sems.at[a, t], recv_sem=recv_sems.at[a, t],
            device_id=(*chips[t], c), device_id_type=MESH)

    def to_sibling(a, t, r=0, nr=None):
        return pltpu.make_async_remote_copy(
            src_ref=half(a, blks[t], r, nr), dst_ref=half(a, blks[t], r, nr), send_sem=send_sems.at[a, 3 + t],
            recv_sem=recv_sems.at[a, 3 + t], device_id=sibling, device_id_type=MESH)

    def from_sibling(a, t):
        rows = ins[a].shape[0] // 2
        dst = outs[a].at[blks[t], pl.ds((1 - c) * rows, rows), :]
        return pltpu.make_async_remote_copy(
            src_ref=dst, dst_ref=dst, send_sem=send_sems.at[a, 3 + t], recv_sem=recv_sems.at[a, 3 + t],
            device_id=sibling, device_id_type=MESH)

    def own(a, r=0, nr=None):
        return pltpu.make_async_remote_copy(
            src_ref=ins[a].at[pl.ds(r, ins[a].shape[0] if nr is None else nr), :],
            dst_ref=outs[a].at[me, pl.ds(r, ins[a].shape[0] if nr is None else nr), :],
            send_sem=send_sems.at[a, 6], recv_sem=recv_sems.at[a, 6], device_id=sibling, device_id_type=MESH)

    def start():
        for a in range(n):
            for t in range(3):
                for r, nr in _chunks(ins[a].shape[0] // 2):
                    to_chip(a, t, r, nr).start()
            for r, nr in _chunks(ins[a].shape[0]):
                own(a, r, nr).start()

    def forward():
        for a in range(n):
            for t in range(3):
                from_chip(a, t).wait_recv()
                for r, nr in _chunks(ins[a].shape[0] // 2):
                    to_sibling(a, t, r, nr).start()

    def finish():
        for a in range(n):
            for t in range(3):
                from_sibling(a, t).wait_recv()
        for a in range(n):
            for t in range(3):
                to_chip(a, t).wait_send()
                to_sibling(a, t).wait_send()
            own(a).wait()

    return start, forward, finish


def pair_swap_halves(grads, *, name):
    n = len(grads)

    def body(*refs):
        ins, theirs = refs[:n], refs[n:2 * n]
        send_sems, recv_sems = refs[2 * n:]
        x, y, c, _ = _place()
        sibling = (x, y, 1 - c)
        for a in range(n):
            rows = ins[a].shape[1] // 2
            for j in range(4):
                for r, nr in _chunks(rows):
                    pltpu.make_async_remote_copy(
                        src_ref=ins[a].at[j, pl.ds((1 - c) * rows + r, nr), :], dst_ref=theirs[a].at[j, pl.ds(r, nr), :],
                        send_sem=send_sems.at[a], recv_sem=recv_sems.at[a], device_id=sibling, device_id_type=MESH).start()
        for a in range(n):
            pltpu.make_async_remote_copy(src_ref=theirs[a], dst_ref=theirs[a], send_sem=send_sems.at[a],
                                         recv_sem=recv_sems.at[a], device_id=sibling, device_id_type=MESH).wait()

    halves = [jax.ShapeDtypeStruct((4, g.shape[1] // 2, g.shape[2]), g.dtype) for g in grads]
    theirs = pl.pallas_call(
        body, name=name, in_specs=[ANY] * n, out_specs=[ANY] * n, out_shape=halves,
        scratch_shapes=[pltpu.SemaphoreType.DMA((n,)), pltpu.SemaphoreType.DMA((n,))],
    )(*grads)
    c = lax.axis_index("c")
    mine = [lax.dynamic_slice_in_dim(g, c * (g.shape[1] // 2), g.shape[1] // 2, axis=1) for g in grads]
    return mine, theirs


def _scatter_scratch(n):
    return [pltpu.SemaphoreType.DMA((n, 3)), pltpu.SemaphoreType.DMA((n, 3))]


def _keep_own_blocks(landed, parts):
    if not parts:
        return []
    chip = 2 * lax.axis_index("x") + lax.axis_index("y")
    return [lax.dynamic_update_slice(l, lax.dynamic_slice_in_dim(p, chip, 1, axis=0), (chip, 0, 0))
            for l, p in zip(landed, parts)]


def _scatter_phases(ins, outs, send_sems, recv_sems):
    n = len(ins)
    x, y, c, chips = _place()
    me = 2 * x + y
    blks = [2 * cx + cy for cx, cy in chips]

    def start():
        for a in range(n):
            for r, nr in _chunks(ins[a].shape[1]):
                for t in range(3):
                    pltpu.make_async_remote_copy(
                        src_ref=ins[a].at[blks[t], pl.ds(r, nr), :], dst_ref=outs[a].at[me, pl.ds(r, nr), :],
                        send_sem=send_sems.at[a, t], recv_sem=recv_sems.at[a, t],
                        device_id=(*chips[t], c), device_id_type=MESH).start()

    def finish():
        for a in range(n):
            for t in range(3):
                pltpu.make_async_remote_copy(
                    src_ref=outs[a].at[blks[t]], dst_ref=outs[a].at[blks[t]], send_sem=send_sems.at[a, t],
                    recv_sem=recv_sems.at[a, t], device_id=(*chips[t], c), device_id_type=MESH).wait()

    return start, finish


def pair_join_halves(halves):
    n = len(halves)

    def body(*refs):
        ins, outs = refs[:n], refs[n:2 * n]
        send_sems, recv_sems = refs[2 * n:]
        x, y, c, _ = _place()
        sibling = (x, y, 1 - c)
        for a in range(n):
            rows = ins[a].shape[0]
            for r, nr in _chunks(rows):
                pltpu.make_async_remote_copy(
                    src_ref=ins[a].at[pl.ds(r, nr), :], dst_ref=outs[a].at[pl.ds(c * rows + r, nr), :],
                    send_sem=send_sems.at[a], recv_sem=recv_sems.at[a], device_id=sibling, device_id_type=MESH).start()
        for a in range(n):
            rows = ins[a].shape[0]
            got = outs[a].at[pl.ds((1 - c) * rows, rows), :]
            pltpu.make_async_remote_copy(src_ref=ins[a], dst_ref=got, send_sem=send_sems.at[a], recv_sem=recv_sems.at[a],
                                         device_id=sibling, device_id_type=MESH).wait()

    joined = pl.pallas_call(
        body, name="pair_join_halves", in_specs=[ANY] * n, out_specs=[ANY] * n,
        out_shape=[jax.ShapeDtypeStruct((2 * h.shape[0], h.shape[1]), h.dtype) for h in halves],
        scratch_shapes=[pltpu.SemaphoreType.DMA((n,)), pltpu.SemaphoreType.DMA((n,))],
    )(*halves)
    c = lax.axis_index("c")
    return [lax.dynamic_update_slice(j, h, (c * h.shape[0], 0)) for j, h in zip(joined, halves)]


def allreduce_small(packed):
    rows = packed.shape[0]

    def body(in_ref, out_ref, gathered, send_sems, recv_sems):
        x, y, c, _ = _place()
        me = 4 * x + 2 * y + c
        gathered[me] = in_ref[...]
        flips = [(fx, fy, fc) for fx in (0, 1) for fy in (0, 1) for fc in (0, 1)][1:]
        peers = [((1 - x) if fx else x, (1 - y) if fy else y, (1 - c) if fc else c) for fx, fy, fc in flips]
        copies = []
        for t, peer in enumerate(peers):
            cp = pltpu.make_async_remote_copy(
                src_ref=in_ref, dst_ref=gathered.at[me], send_sem=send_sems.at[t], recv_sem=recv_sems.at[t],
                device_id=peer, device_id_type=MESH)
            cp.start()
            copies.append(cp)
        for t, (px, py, pc) in enumerate(peers):
            slot = gathered.at[4 * px + 2 * py + pc]
            pltpu.make_async_remote_copy(
                src_ref=slot, dst_ref=slot, send_sem=send_sems.at[t], recv_sem=recv_sems.at[t],
                device_id=(px, py, pc), device_id_type=MESH).wait_recv()
        for cp in copies:
            cp.wait_send()
        acc = gathered[0]
        for k in range(1, 8):
            acc = acc + gathered[k]
        out_ref[...] = acc

    vm = pl.BlockSpec(memory_space=pltpu.VMEM)
    return pl.pallas_call(
        body, name="allreduce_small", in_specs=[vm], out_specs=vm, out_shape=jax.ShapeDtypeStruct(packed.shape, F32),
        scratch_shapes=[pltpu.VMEM((8, rows, 128), F32), pltpu.SemaphoreType.DMA((7,)), pltpu.SemaphoreType.DMA((7,))],
    )(packed)


SMALL_NAMES = ("norm_mix_w", "ssd_conv_w", "ssd_conv_b", "ssd_dt_bias", "ssd_a_log", "ssd_d", "ssd_norm_w", "fox_f_bias",
               "fox_q_norm_w", "fox_k_norm_w", "norm_ffn_w", "ffn_conv_w", "ffn_conv_b")
BIG_NAMES = ("w_in", "w_out", "w_up", "w_down")
WEIGHT_ORDER = ("norm_mix_w", "w_in", "ssd_conv_w", "ssd_conv_b", "ssd_dt_bias", "ssd_a_log", "ssd_d", "ssd_norm_w",
                "fox_f_bias", "fox_q_norm_w", "fox_k_norm_w", "w_out", "norm_ffn_w", "w_up", "ffn_conv_w", "ffn_conv_b", "w_down")
ADAM_ROWS = {"w_in": 256, "w_out": 256, "w_up": 256, "w_down": 176}


def _pack(arrays):
    pieces = []
    for a in arrays:
        flat = a.reshape(-1).astype(F32)
        pieces += [flat, jnp.zeros(((-flat.shape[0]) % 1024,), F32)]
    return jnp.concatenate(pieces).reshape(-1, 128)


def _unpack(packed, shapes):
    out, r = [], 0
    for shp in shapes:
        size = 1
        for d in shp:
            size *= d
        nrow = 8 * (-(-size // 1024))
        out.append(packed[r:r + nrow].reshape(-1)[:size].reshape(shp))
        r += nrow
    return out


def _pad_rows(a, rows):
    return jnp.pad(a, ((0, rows - a.shape[0]), (0, 0)))


def kernel(x, norm_mix_w, w_in, ssd_conv_w, ssd_conv_b, ssd_dt_bias, ssd_a_log, ssd_d, ssd_norm_w, fox_f_bias, fox_q_norm_w, fox_k_norm_w, w_out, norm_ffn_w, w_up, ffn_conv_w, ffn_conv_b, w_down, loss_target, m_norm_mix_w, m_w_in, m_ssd_conv_w, m_ssd_conv_b, m_ssd_dt_bias, m_ssd_a_log, m_ssd_d, m_ssd_norm_w, m_fox_f_bias, m_fox_q_norm_w, m_fox_k_norm_w, m_w_out, m_norm_ffn_w, m_w_up, m_ffn_conv_w, m_ffn_conv_b, m_w_down, v_norm_mix_w, v_w_in, v_ssd_conv_w, v_ssd_conv_b, v_ssd_dt_bias, v_ssd_a_log, v_ssd_d, v_ssd_norm_w, v_fox_f_bias, v_fox_q_norm_w, v_fox_k_norm_w, v_w_out, v_norm_ffn_w, v_w_up, v_ffn_conv_w, v_ffn_conv_b, v_w_down):
    w = dict(norm_mix_w=norm_mix_w, w_in=w_in, ssd_conv_w=ssd_conv_w, ssd_conv_b=ssd_conv_b, ssd_dt_bias=ssd_dt_bias,
             ssd_a_log=ssd_a_log, ssd_d=ssd_d, ssd_norm_w=ssd_norm_w, fox_f_bias=fox_f_bias, fox_q_norm_w=fox_q_norm_w,
             fox_k_norm_w=fox_k_norm_w, w_out=w_out, norm_ffn_w=norm_ffn_w, w_up=w_up, ffn_conv_w=ffn_conv_w,
             ffn_conv_b=ffn_conv_b, w_down=w_down)
    m = dict(norm_mix_w=m_norm_mix_w, w_in=m_w_in, ssd_conv_w=m_ssd_conv_w, ssd_conv_b=m_ssd_conv_b, ssd_dt_bias=m_ssd_dt_bias,
             ssd_a_log=m_ssd_a_log, ssd_d=m_ssd_d, ssd_norm_w=m_ssd_norm_w, fox_f_bias=m_fox_f_bias, fox_q_norm_w=m_fox_q_norm_w,
             fox_k_norm_w=m_fox_k_norm_w, w_out=m_w_out, norm_ffn_w=m_norm_ffn_w, w_up=m_w_up, ffn_conv_w=m_ffn_conv_w,
             ffn_conv_b=m_ffn_conv_b, w_down=m_w_down)
    v = dict(norm_mix_w=v_norm_mix_w, w_in=v_w_in, ssd_conv_w=v_ssd_conv_w, ssd_conv_b=v_ssd_conv_b, ssd_dt_bias=v_ssd_dt_bias,
             ssd_a_log=v_ssd_a_log, ssd_d=v_ssd_d, ssd_norm_w=v_ssd_norm_w, fox_f_bias=v_fox_f_bias, fox_q_norm_w=v_fox_q_norm_w,
             fox_k_norm_w=v_fox_k_norm_w, w_out=v_w_out, norm_ffn_w=v_norm_ffn_w, w_up=v_w_up, ffn_conv_w=v_ffn_conv_w,
             ffn_conv_b=v_ffn_conv_b, w_down=v_w_down)
    chip = 2 * lax.axis_index("x") + lax.axis_index("y")

    a_in, a_scw, a_fcw = gather_weights([w_in[0].astype(BF16), _pad_rows(ssd_conv_w[0], 16), _pad_rows(ffn_conv_w[0], 16)])
    later_shards = [w_out[0].astype(BF16), w_up[0].astype(BF16), w_down[0].astype(BF16)]
    w_full = a_in.transpose(1, 0, 2).reshape(D_MODEL, IN_COLS)
    wm = jnp.concatenate([w_full[:, :2048], w_full[:, 2576:5648], w_full[:, 2048:2560]], axis=1)
    ws = jnp.concatenate([w_full[:, 2560:2576], w_full[:, 5648:5664], jnp.zeros((D_MODEL, SMALL_COLS - 32), BF16)], axis=1)
    ssd_cw8 = a_scw.transpose(1, 0, 2).reshape(16, 1536)[:8]
    ffn_cw8 = a_fcw.transpose(1, 0, 2).reshape(16, 2 * D_FF)[:8]
    smallp = jnp.zeros((8, 128), F32)
    smallp = smallp.at[0, :16].set(ssd_dt_bias[0]).at[1, :16].set(ssd_a_log[0]).at[2, :16].set(ssd_d[0])
    smallp = smallp.at[3, F_LANE:F_LANE + 16].set(fox_f_bias[0])
    qw_t = jnp.tile(fox_q_norm_w[0], N_HEADS)[None]
    kw_t = jnp.tile(fox_k_norm_w[0], N_HEADS)[None]
    sel = jnp.asarray((np.arange(1024)[:, None] // HEAD_DIM == np.arange(128)[None, :]).astype(np.float32), BF16)

    res = local_step(x[0], loss_target[0], wm, ws, later_shards, ssd_cw8, ssd_conv_b, smallp, ssd_norm_w, qw_t, kw_t,
                     sel, sel.T, norm_mix_w, norm_ffn_w, ffn_cw8, ffn_conv_b)

    full_shapes = [(1, 1024), (1, 4, 1536), (1, 1536), (1, 16), (1, 16), (1, 16), (1, 1024), (1, 16), (1, 64), (1, 64),
                   (1, 1024), (1, 3, 2 * D_FF), (1, 2 * D_FF), (1,)]
    local_small = [res["g_norm_mix"], res["g_ssd_cw"][:4], res["g_ssd_cw"][4], res["g_sp"][0, :16], res["g_sp"][1, :16],
                   res["g_sp"][2, :16], res["g_ssd_nw"], res["g_fb"][0, F_LANE:F_LANE + 16],
                   res["g_qw"].reshape(N_HEADS, HEAD_DIM).sum(0), res["g_kw"].reshape(N_HEADS, HEAD_DIM).sum(0),
                   res["g_norm_ffn"], res["g_ffn_cw"][:3], res["g_ffn_cw"][3], jnp.sum(res["sq"])]
    summed = _unpack(allreduce_small(_pack(local_small)), full_shapes)
    loss = (0.5 / D_MODEL) * summed[-1][0]
    g_small = dict(zip(SMALL_NAMES, summed[:-1]))
    g_small["ssd_conv_w"] = lax.dynamic_slice(g_small["ssd_conv_w"], (0, 0, 384 * chip), (1, 4, 384))
    g_small["ffn_conv_w"] = lax.dynamic_slice(g_small["ffn_conv_w"], (0, 0, 1408 * chip), (1, 3, 1408))

    landed = res["landed"]
    halves = [sum_chips(p, name="sum_chips_" + n, tr=ADAM_ROWS[n]) for p, n in zip(landed, BIG_NAMES)]
    g_big = dict(zip(BIG_NAMES, pair_join_halves(halves)))

    grads, deltas, new_m, new_v = {}, {}, {}, {}
    for n in BIG_NAMES:
        d, mn, vn = adamw(w[n][0], g_big[n], m[n][0], v[n][0], name="adamw_" + n, tr=ADAM_ROWS[n])
        grads[n], deltas[n], new_m[n], new_v[n] = g_big[n][None], d[None], mn[None], vn[None]
    shapes = [w[n].shape for n in SMALL_NAMES]
    packed_w = _pack([w[n] for n in SMALL_NAMES])
    d, mn, vn = adamw(packed_w, _pack([g_small[n] for n in SMALL_NAMES]), _pack([m[n] for n in SMALL_NAMES]),
                      _pack([v[n] for n in SMALL_NAMES]), name="adamw_small", tr=packed_w.shape[0])
    for n, dd, mm, vv in zip(SMALL_NAMES, _unpack(d, shapes), _unpack(mn, shapes), _unpack(vn, shapes)):
        grads[n], deltas[n], new_m[n], new_v[n] = g_small[n].reshape(w[n].shape), dd, mm, vv
    return (loss, res["grad_x"][None], *[grads[n] for n in WEIGHT_ORDER], *[deltas[n] for n in WEIGHT_ORDER],
            *[new_m[n] for n in WEIGHT_ORDER], *[new_v[n] for n in WEIGHT_ORDER])
```

```python
import functools

import jax
import jax.numpy as jnp
import numpy as np
from jax import lax
from jax.experimental import pallas as pl
from jax.experimental.pallas import tpu as pltpu

F32 = jnp.float32
BF16 = jnp.bfloat16
MESH = pl.DeviceIdType.MESH

D_MODEL = 1024
HEAD_DIM = 64
N_HEADS = 16
N_PAIRS = N_HEADS // 2
SSD_CHUNK = 128
SSD_STATE = 128
SSD_CONV = 4
D_FF = 2816
FFN_CONV = 3
NORM_EPS = 1e-6
MAIN_COLS = 5632
SMALL_COLS = 128
F_LANE = 16
IN_COLS = 5664

ADAM_LR = 0.001
ADAM_B1 = 0.9
ADAM_B2 = 0.999
ADAM_EPS = 1e-08
ADAM_WD = 0.01
ADAM_STEP = 10

VMEM_LIMIT_V7X = 56 * 1024 * 1024
NEG_BIG = -1e30


def _params(sem=None):
    return pltpu.CompilerParams(dimension_semantics=sem, vmem_limit_bytes=VMEM_LIMIT_V7X)


def _sigmoid(x):
    return 1.0 / (1.0 + jnp.exp(-x))


def _silu_and_grad(x):
    s = _sigmoid(x)
    return x * s, s * (1.0 + x * (1.0 - s))


def _shift_down(v, j):
    return v if j == 0 else pltpu.roll(v, j, 0)


def _shift_up(v, j):
    return v if j == 0 else pltpu.roll(v, v.shape[0] - j, 0)


def _row_iota(shape):
    return lax.broadcasted_iota(jnp.int32, shape, 0)


def _lane_iota(shape):
    return lax.broadcasted_iota(jnp.int32, shape, 1)


def _dot(a, b, mode="nn"):
    dims = {"nn": (((1,), (0,)), ((), ())), "nt": (((1,), (1,)), ((), ())), "tn": (((0,), (0,)), ((), ()))}[mode]
    return lax.dot_general(a.astype(BF16), b.astype(BF16), dims, preferred_element_type=F32)


def _dot_f32(a, b):
    return jnp.dot(a, b, precision=lax.Precision.HIGHEST, preferred_element_type=F32)


def matmul(a, b, *, mode, tm, tn, tk, out_dtype, name, add=None, b_koff=0, scatter=(), layout=None):
    layout = layout or {}
    if layout:
        m, n, k = layout["m"], layout["n"], layout["k"]
    else:
        (m, k), n = a.shape, (b.shape[1] if mode == "nn" else b.shape[0])
    assert m % tm == 0 and n % tn == 0 and k % tk == 0, (name, m, n, k, tm, tn, tk)
    nk = k // tk
    grid = (m // tm, n // tn, nk)
    a_spec = layout.get("a_spec") or pl.BlockSpec((tm, tk), lambda i, j, kk: (i, kk))
    b_spec = layout.get("b_spec") or (pl.BlockSpec((tn, tk), lambda i, j, kk: (j, kk + b_koff)) if mode == "nt"
                                      else pl.BlockSpec((tk, tn), lambda i, j, kk: (kk + b_koff, j)))
    o_spec = layout.get("o_spec") or pl.BlockSpec((tm, tn), lambda i, j, kk: (i, j))
    out_struct = jax.ShapeDtypeStruct(layout.get("out_shape", (m, n)), out_dtype)
    has_add = add is not None
    n_in = 3 if has_add else 2
    ns = len(scatter)

    def body(*refs):
        a_ref, b_ref = refs[:2]
        add_ref = refs[2] if has_add else None
        o_ref, acc_ref = refs[n_in + ns], refs[n_in + 2 * ns + 1]
        kk = pl.program_id(2)
        if ns:
            step = (pl.program_id(0) * grid[1] + pl.program_id(1)) * grid[2] + kk
            start, finish_copies = _scatter_phases(refs[n_in:n_in + ns], refs[n_in + ns + 1:n_in + 2 * ns + 1],
                                                   *refs[n_in + 2 * ns + 2:])
            pl.when(step == 0)(start)
        part = _dot(a_ref[...], b_ref[...], mode)

        def finish(total):
            if has_add:
                total = total + add_ref[...]
            o_ref[...] = total.astype(out_dtype)

        if nk == 1:
            finish(part)
        else:
            @pl.when(kk == 0)
            def _():
                acc_ref[...] = part

            @pl.when(jnp.logical_and(kk > 0, kk < nk - 1))
            def _():
                acc_ref[...] += part

            @pl.when(kk == nk - 1)
            def _():
                finish(acc_ref[...] + part)

        if ns:
            pl.when(step == grid[0] * grid[1] * grid[2] - 1)(finish_copies)

    in_specs = [a_spec, b_spec] + ([o_spec] if has_add else [])
    args = (a, b) + ((add,) if has_add else ())
    acc = pltpu.VMEM((tm, tn) if nk > 1 else (8, 128), F32)
    if not ns:
        return pl.pallas_call(
            body, name=name, grid=grid, in_specs=in_specs, out_specs=o_spec, out_shape=out_struct,
            scratch_shapes=[acc], compiler_params=_params(("parallel", "parallel", "arbitrary")),
        )(*args)
    outs = pl.pallas_call(
        body, name=name, grid=grid, in_specs=in_specs + [ANY] * ns, out_specs=[o_spec] + [ANY] * ns,
        out_shape=[out_struct] + [jax.ShapeDtypeStruct(p.shape, p.dtype) for p in scatter],
        scratch_shapes=[acc] + _scatter_scratch(ns), compiler_params=_params(("arbitrary", "arbitrary", "arbitrary")),
    )(*args, *scatter)
    return outs[0], _keep_own_blocks(outs[1:], scatter)


def rms_fwd(x, w, *, name, tm=512):
    s, d = x.shape

    def body(x_ref, w_ref, h_ref, ht_ref):
        xv = x_ref[...]
        r = lax.rsqrt(jnp.mean(xv * xv, axis=-1, keepdims=True) + NORM_EPS)
        h = (xv * r) * w_ref[...]
        h_ref[...] = h.astype(BF16)
        ht_ref[...] = h.T.astype(BF16)

    return pl.pallas_call(
        body, name=name, grid=(s // tm,),
        in_specs=[pl.BlockSpec((tm, d), lambda i: (i, 0)), pl.BlockSpec((1, d), lambda i: (0, 0))],
        out_specs=[pl.BlockSpec((tm, d), lambda i: (i, 0)), pl.BlockSpec((d, tm), lambda i: (0, i))],
        out_shape=[jax.ShapeDtypeStruct((s, d), BF16), jax.ShapeDtypeStruct((d, s), BF16)],
        compiler_params=_params(("parallel",)),
    )(x, w)


def rms_bwd(dh, x, w, resid, *, name, tm=512):
    s, d = x.shape

    def body(dh_ref, x_ref, w_ref, res_ref, dx_ref, dw_ref):
        xv = x_ref[...]
        dhv = dh_ref[...]
        r = lax.rsqrt(jnp.mean(xv * xv, axis=-1, keepdims=True) + NORM_EPS)
        xh = xv * r
        g = dhv * w_ref[...]
        dx_ref[...] = res_ref[...] + r * (g - xh * jnp.mean(g * xh, axis=-1, keepdims=True))
        part = jnp.sum(dhv * xh, axis=0, keepdims=True)

        @pl.when(pl.program_id(0) == 0)
        def _():
            dw_ref[...] = part

        @pl.when(pl.program_id(0) > 0)
        def _():
            dw_ref[...] += part

    row = pl.BlockSpec((tm, d), lambda i: (i, 0))
    vec = pl.BlockSpec((1, d), lambda i: (0, 0))
    return pl.pallas_call(
        body, name=name, grid=(s // tm,), in_specs=[row, row, vec, row], out_specs=[row, vec],
        out_shape=[jax.ShapeDtypeStruct((s, d), F32), jax.ShapeDtypeStruct((1, d), F32)],
        compiler_params=_params(("arbitrary",)),
    )(dh, x, w, resid)


def loss_head(y, target, *, tm=512):
    s, d = y.shape

    def body(y_ref, t_ref, dy_ref, sq_ref):
        e = y_ref[...] - t_ref[...]
        dy_ref[...] = e / float(d)
        part = jnp.sum(e * e, axis=0, keepdims=True)

        @pl.when(pl.program_id(0) == 0)
        def _():
            sq_ref[...] = part

        @pl.when(pl.program_id(0) > 0)
        def _():
            sq_ref[...] += part

    row = pl.BlockSpec((tm, d), lambda i: (i, 0))
    vec = pl.BlockSpec((1, d), lambda i: (0, 0))
    return pl.pallas_call(
        body, name="loss_head", grid=(s // tm,), in_specs=[row, row], out_specs=[row, vec],
        out_shape=[jax.ShapeDtypeStruct((s, d), F32), jax.ShapeDtypeStruct((1, d), F32)],
        compiler_params=_params(("arbitrary",)),
    )(y, target)


def _row_shifts(ext, k_taps):
    return [_shift_down(ext, j) for j in range(k_taps)]


def _conv_rows(shifts, w):
    k_taps = len(shifts)
    acc = w[k_taps - 1:k_taps, :] * shifts[0]
    for k in range(k_taps - 1):
        acc = acc + w[k:k + 1, :] * shifts[k_taps - 1 - k]
    return acc


def _conv_weight_grad(dcur, shifts, rows, width):
    k_taps = len(shifts)
    out = [jnp.sum(dcur * shifts[k_taps - 1 - k][rows], axis=0, keepdims=True) for k in range(k_taps)]
    out.append(jnp.sum(dcur, axis=0, keepdims=True))
    return _stack_rows(out, width)


def _conv_rows_transposed(dext, w, k_taps):
    acc = w[k_taps - 1:k_taps, :] * dext
    for k in range(k_taps - 1):
        acc = acc + w[k:k + 1, :] * _shift_up(dext, k_taps - 1 - k)
    return acc


def _stack_rows(rows, width):
    ri = _row_iota((8, width))
    out = jnp.zeros((8, width), F32)
    for k, r in enumerate(rows):
        out = out + jnp.where(ri == k, r, 0.0)
    return out


def ffn_mid_fwd(hu, conv_w8, conv_b, *, tm=1024, tc=256):
    s = hu.shape[0]
    ncol = D_FF // tc
    r8 = tm // 8

    def body(g_ref, v_ref, gp_ref, vp_ref, wg_ref, wv_ref, bg_ref, bv_ref, o_ref, ot_ref):
        first = pl.program_id(1) == 0

        def conv(cur_ref, prev_ref, w_ref, b_ref):
            prev = jnp.where(first, 0.0, prev_ref[...])
            ext = jnp.concatenate([prev, cur_ref[...]], axis=0)
            return _conv_rows(_row_shifts(ext, FFN_CONV), w_ref[...])[8:] + b_ref[...]

        gc = conv(g_ref, gp_ref, wg_ref, bg_ref)
        vc = conv(v_ref, vp_ref, wv_ref, bv_ref)
        act = gc * _sigmoid(gc) * vc
        o_ref[...] = act.astype(BF16)
        ot_ref[...] = act.T.astype(BF16)

    def prev_idx(i):
        return jnp.maximum(i * r8 - 1, 0)

    in_specs = [
        pl.BlockSpec((tm, tc), lambda j, i: (i, j)),
        pl.BlockSpec((tm, tc), lambda j, i: (i, j + ncol)),
        pl.BlockSpec((8, tc), lambda j, i: (prev_idx(i), j)),
        pl.BlockSpec((8, tc), lambda j, i: (prev_idx(i), j + ncol)),
        pl.BlockSpec((8, tc), lambda j, i: (0, j)),
        pl.BlockSpec((8, tc), lambda j, i: (0, j + ncol)),
        pl.BlockSpec((1, tc), lambda j, i: (0, j)),
        pl.BlockSpec((1, tc), lambda j, i: (0, j + ncol)),
    ]
    return pl.pallas_call(
        body, name="ffn_mid_fwd", grid=(ncol, s // tm), in_specs=in_specs,
        out_specs=[pl.BlockSpec((tm, tc), lambda j, i: (i, j)), pl.BlockSpec((tc, tm), lambda j, i: (j, i))],
        out_shape=[jax.ShapeDtypeStruct((s, D_FF), BF16), jax.ShapeDtypeStruct((D_FF, s), BF16)],
        compiler_params=_params(("parallel", "parallel")),
    )(hu, hu, hu, hu, conv_w8, conv_w8, conv_b, conv_b)


def ffn_mid_bwd(hu, dact, conv_w8, conv_b, *, tm=1024, tc=256):
    s = hu.shape[0]
    ncol = D_FF // tc
    nrow = s // tm
    r8 = tm // 8

    def body(g_ref, v_ref, gp_ref, vp_ref, gn_ref, vn_ref, da_ref, dan_ref, wg_ref, wv_ref, bg_ref, bv_ref,
             dhu_ref, wgo_ref, wvo_ref):
        i = pl.program_id(1)
        first = i == 0
        last = i == nrow - 1

        def ext_of(cur_ref, prev_ref, next_ref):
            prev = jnp.where(first, 0.0, prev_ref[...])
            return jnp.concatenate([prev, cur_ref[...], next_ref[...]], axis=0)

        g_sh = _row_shifts(ext_of(g_ref, gp_ref, gn_ref), FFN_CONV)
        v_sh = _row_shifts(ext_of(v_ref, vp_ref, vn_ref), FFN_CONV)
        gc = _conv_rows(g_sh, wg_ref[...]) + bg_ref[...]
        vc = _conv_rows(v_sh, wv_ref[...]) + bv_ref[...]
        da_ext = jnp.concatenate([jnp.zeros((8, tc), F32), da_ref[...], jnp.where(last, 0.0, dan_ref[...])], axis=0)
        silu, dsilu = _silu_and_grad(gc)
        dgc = da_ext * vc * dsilu
        dvc = da_ext * silu
        dhu_ref[0] = _conv_rows_transposed(dgc, wg_ref[...], FFN_CONV)[8:8 + tm].astype(BF16)
        dhu_ref[1] = _conv_rows_transposed(dvc, wv_ref[...], FFN_CONV)[8:8 + tm].astype(BF16)

        cur = slice(8, 8 + tm)
        pg = _conv_weight_grad(dgc[cur], g_sh, cur, tc)
        pv = _conv_weight_grad(dvc[cur], v_sh, cur, tc)

        @pl.when(first)
        def _():
            wgo_ref[...] = pg
            wvo_ref[...] = pv

        @pl.when(i > 0)
        def _():
            wgo_ref[...] += pg
            wvo_ref[...] += pv

    def prev_idx(i):
        return jnp.maximum(i * r8 - 1, 0)

    def next_idx(i):
        return jnp.minimum((i + 1) * r8, s // 8 - 1)

    cur_g = pl.BlockSpec((tm, tc), lambda j, i: (i, j))
    cur_v = pl.BlockSpec((tm, tc), lambda j, i: (i, j + ncol))
    in_specs = [
        cur_g, cur_v,
        pl.BlockSpec((8, tc), lambda j, i: (prev_idx(i), j)),
        pl.BlockSpec((8, tc), lambda j, i: (prev_idx(i), j + ncol)),
        pl.BlockSpec((8, tc), lambda j, i: (next_idx(i), j)),
        pl.BlockSpec((8, tc), lambda j, i: (next_idx(i), j + ncol)),
        cur_g,
        pl.BlockSpec((8, tc), lambda j, i: (next_idx(i), j)),
        pl.BlockSpec((8, tc), lambda j, i: (0, j)),
        pl.BlockSpec((8, tc), lambda j, i: (0, j + ncol)),
        pl.BlockSpec((1, tc), lambda j, i: (0, j)),
        pl.BlockSpec((1, tc), lambda j, i: (0, j + ncol)),
    ]
    out_specs = [pl.BlockSpec((2, tm, tc), lambda j, i: (0, i, j)), pl.BlockSpec((8, tc), lambda j, i: (0, j)),
                 pl.BlockSpec((8, tc), lambda j, i: (0, j))]
    out_shape = [jax.ShapeDtypeStruct((2, s, D_FF), BF16),
                 jax.ShapeDtypeStruct((8, D_FF), F32), jax.ShapeDtypeStruct((8, D_FF), F32)]
    return pl.pallas_call(
        body, name="ffn_mid_bwd", grid=(ncol, nrow), in_specs=in_specs, out_specs=out_specs, out_shape=out_shape,
        compiler_params=_params(("parallel", "arbitrary")),
    )(hu, hu, hu, hu, hu, hu, dact, dact, conv_w8, conv_w8, conv_b, conv_b)


def _softplus(x):
    return jnp.maximum(x, 0.0) + jnp.log(1.0 + jnp.exp(-jnp.abs(x)))


def _cumsum_rows(v):
    n = v.shape[0]
    ri = _row_iota(v.shape)
    sh = 1
    while sh < n:
        v = v + jnp.where(ri >= sh, _shift_down(v, sh), 0.0)
        sh *= 2
    return v


def _rev_cumsum_rows(v):
    n = v.shape[0]
    ri = _row_iota(v.shape)
    sh = 1
    while sh < n:
        v = v + jnp.where(ri < n - sh, _shift_up(v, sh), 0.0)
        sh *= 2
    return v


def _total(v):
    return jnp.sum(jnp.sum(v, axis=1, keepdims=True), axis=0, keepdims=True)


def _ssd_in_specs(rev_nc=None):
    def ch(c):
        return c if rev_nc is None else rev_nc - 1 - c

    def prev(c):
        return jnp.maximum(ch(c) * (SSD_CHUNK // 8) - 1, 0)

    L = SSD_CHUNK
    return [
        pl.BlockSpec((L, 1024), lambda c: (ch(c), 0)),
        pl.BlockSpec((L, 1024), lambda c: (ch(c), 1)),
        pl.BlockSpec((L, 256), lambda c: (ch(c), 20)),
        pl.BlockSpec((L, 256), lambda c: (ch(c), 21)),
        pl.BlockSpec((8, 1024), lambda c: (prev(c), 1)),
        pl.BlockSpec((8, 256), lambda c: (prev(c), 20)),
        pl.BlockSpec((8, 256), lambda c: (prev(c), 21)),
        pl.BlockSpec((8, 1024), lambda c: (0, 0)),
        pl.BlockSpec((8, 256), lambda c: (0, 4)),
        pl.BlockSpec((8, 256), lambda c: (0, 5)),
        pl.BlockSpec((1, 1024), lambda c: (0, 0)),
        pl.BlockSpec((1, 256), lambda c: (0, 4)),
        pl.BlockSpec((1, 256), lambda c: (0, 5)),
        pl.BlockSpec((L, SMALL_COLS), lambda c: (ch(c), 0)),
        pl.BlockSpec((8, 128), lambda c: (0, 0)),
        pl.BlockSpec((1, 1024), lambda c: (0, 0)),
    ]


def _ssd_conv_pre(cur_ref, prev_ref, w_ref, b_ref, first):
    prev = jnp.where(first, 0.0, prev_ref[...])
    shifts = _row_shifts(jnp.concatenate([prev, cur_ref[...]], axis=0), SSD_CONV)
    return shifts, _conv_rows(shifts, w_ref[...])[8:] + b_ref[...]


def _ssd_time_consts(small_ref, sp_ref):
    dt_pre = small_ref[...] + sp_ref[0:1, :]
    dt = _softplus(dt_pre)
    a = -jnp.exp(sp_ref[1:2, :])
    acs = _cumsum_rows(dt * a)
    return dt_pre, dt, a, acs


def ssd_fwd(proj, small, conv_w8, conv_b, smallp, norm_w):
    s = proj.shape[0]
    nc = s // SSD_CHUNK
    L = SSD_CHUNK

    def body(z_ref, xs_ref, b_ref, c_ref, xsp_ref, bp_ref, cp_ref, wx_ref, wb_ref, wc_ref, bx_ref, bb_ref, bc_ref,
             small_ref, sp_ref, nw_ref, y_ref, yt_ref, ypre_ref, st_ref, state):
        first = pl.program_id(0) == 0

        @pl.when(first)
        def _():
            state[...] = jnp.zeros_like(state)

        xs = _ssd_conv_pre(xs_ref, xsp_ref, wx_ref, bx_ref, first)[1]
        xs = xs * _sigmoid(xs)
        bm = _ssd_conv_pre(b_ref, bp_ref, wb_ref, bb_ref, first)[1]
        bm = bm * _sigmoid(bm)
        cm = _ssd_conv_pre(c_ref, cp_ref, wc_ref, bc_ref, first)[1]
        cm = cm * _sigmoid(cm)
        _, dt, _, acs = _ssd_time_consts(small_ref, sp_ref)
        acs_t = acs.T
        li = _lane_iota((L, L))
        ri = _row_iota((L, L))
        tri = ri >= li
        lo = li < HEAD_DIM
        st_ref[0] = state[...]
        for g in range(2):
            bg = bm[:, 128 * g:128 * g + 128]
            cg = cm[:, 128 * g:128 * g + 128]
            gmat = _dot(cg, bg, "nt")
            for pp in range(4):
                p = 4 * g + pp
                h0, h1 = 2 * p, 2 * p + 1
                x = xs[:, 128 * p:128 * p + 128]
                a0, a1 = acs[:, h0:h0 + 1], acs[:, h1:h1 + 1]
                xdt = x * jnp.where(lo, dt[:, h0:h0 + 1], dt[:, h1:h1 + 1])
                m0 = gmat * jnp.exp(jnp.where(tri, a0 - acs_t[h0:h0 + 1, :], NEG_BIG))
                m1 = gmat * jnp.exp(jnp.where(tri, a1 - acs_t[h1:h1 + 1, :], NEG_BIG))
                yd = _dot(m0, jnp.where(lo, xdt, 0.0)) + _dot(m1, jnp.where(lo, 0.0, xdt))
                hin = state[p]
                yo = _dot(cg, hin, "nt") * jnp.exp(jnp.where(lo, a0, a1))
                dskip = jnp.where(lo[0:1], sp_ref[2:3, h0:h0 + 1], sp_ref[2:3, h1:h1 + 1])
                ypre_ref[:, 128 * p:128 * p + 128] = yd + yo + dskip * x
                al0, al1 = acs[L - 1:L, h0:h0 + 1], acs[L - 1:L, h1:h1 + 1]
                w = jnp.exp(jnp.where(lo, al0 - a0, al1 - a1))
                dec = jnp.exp(jnp.where(ri < HEAD_DIM, al0, al1))
                state[p] = dec * hin + _dot(xdt * w, bg, "tn")
        z = z_ref[...]
        yg = ypre_ref[...] * (z * _sigmoid(z))
        for g in range(2):
            seg = yg[:, 512 * g:512 * g + 512]
            r = lax.rsqrt(jnp.mean(seg * seg, axis=-1, keepdims=True) + NORM_EPS)
            out = (seg * r) * nw_ref[:, 512 * g:512 * g + 512]
            y_ref[:, 512 * g:512 * g + 512] = out.astype(BF16)
            yt_ref[512 * g:512 * g + 512, :] = out.T.astype(BF16)

    row = pl.BlockSpec((L, 1024), lambda c: (c, 0))
    return pl.pallas_call(
        body, name="ssd_fwd", grid=(nc,), in_specs=_ssd_in_specs(),
        out_specs=[row, pl.BlockSpec((1024, L), lambda c: (0, c)), row,
                   pl.BlockSpec((1, N_PAIRS, 128, 128), lambda c: (c, 0, 0, 0))],
        out_shape=[jax.ShapeDtypeStruct((s, 1024), BF16), jax.ShapeDtypeStruct((1024, s), BF16),
                   jax.ShapeDtypeStruct((s, 1024), F32), jax.ShapeDtypeStruct((nc, N_PAIRS, 128, 128), F32)],
        scratch_shapes=[pltpu.VMEM((N_PAIRS, 128, 128), F32)],
        compiler_params=_params(("arbitrary",)),
    )(proj, proj, proj, proj, proj, proj, proj, conv_w8, conv_w8, conv_w8, conv_b, conv_b, conv_b, small, smallp, norm_w)


def ssd_bwd(proj, small, conv_w8, conv_b, smallp, norm_w, ypre, states, dy, sel):
    s = proj.shape[0]
    nc = s // SSD_CHUNK
    L = SSD_CHUNK

    def body(z_ref, xs_ref, b_ref, c_ref, xsp_ref, bp_ref, cp_ref, wx_ref, wb_ref, wc_ref, bx_ref, bb_ref, bc_ref,
             small_ref, sp_ref, nw_ref, ypre_ref, st_ref, dy_ref, sel_ref,
             dz_ref, dxs_ref, db_ref, dc_ref, dsmall_ref, gwx_ref, gwb_ref, gwc_ref, gsp_ref, gnw_ref,
             dstate, carry_x, carry_b, carry_c, dxs_buf, dbm_buf, dcm_buf, qcs, col_sums, acs_terms, dt_terms):
        step = pl.program_id(0)
        col_sums[...] = jnp.zeros_like(col_sums)
        first_chunk = step == nc - 1
        start = step == 0

        @pl.when(start)
        def _():
            dstate[...] = jnp.zeros_like(dstate)
            carry_x[...] = jnp.zeros_like(carry_x)
            carry_b[...] = jnp.zeros_like(carry_b)
            carry_c[...] = jnp.zeros_like(carry_c)

        xs_ext, xs_pre = _ssd_conv_pre(xs_ref, xsp_ref, wx_ref, bx_ref, first_chunk)
        b_ext, b_pre = _ssd_conv_pre(b_ref, bp_ref, wb_ref, bb_ref, first_chunk)
        c_ext, c_pre = _ssd_conv_pre(c_ref, cp_ref, wc_ref, bc_ref, first_chunk)
        xs, xs_ds = _silu_and_grad(xs_pre)
        bm, b_ds = _silu_and_grad(b_pre)
        cm, c_ds = _silu_and_grad(c_pre)
        dt_pre, dt, a, acs = _ssd_time_consts(small_ref, sp_ref)
        acs_t = acs.T
        li = _lane_iota((L, L))
        ri = _row_iota((L, L))
        tri = ri >= li
        lo = li < HEAD_DIM
        lo_rows = ri < HEAD_DIM
        li1 = _lane_iota((1, L))

        z = z_ref[...]
        sz, dsz = _silu_and_grad(z)
        y = ypre_ref[...]
        yg = y * sz
        dout = dy_ref[...]
        dyg_parts = []
        gnw_parts = []
        for g in range(2):
            sl = slice(512 * g, 512 * g + 512)
            seg = yg[:, sl]
            r = lax.rsqrt(jnp.mean(seg * seg, axis=-1, keepdims=True) + NORM_EPS)
            n = seg * r
            gnw_parts.append(jnp.sum(dout[:, sl] * n, axis=0, keepdims=True))
            gg = dout[:, sl] * nw_ref[:, sl]
            dyg_parts.append(r * (gg - n * jnp.mean(gg * n, axis=-1, keepdims=True)))
        dyg = jnp.concatenate(dyg_parts, axis=1)
        gnw = jnp.concatenate(gnw_parts, axis=1)
        dz_ref[...] = (dyg * y * dsz).astype(BF16)
        dypre = dyg * sz

        qcs[...] = jnp.zeros_like(qcs)
        dalast = jnp.zeros((1, L), F32)
        for g in range(2):
            bg = bm[:, 128 * g:128 * g + 128]
            cg = cm[:, 128 * g:128 * g + 128]
            gmat = _dot(cg, bg, "nt")
            dgmat = jnp.zeros((L, L), F32)
            dbg = jnp.zeros((L, L), F32)
            dcg = jnp.zeros((L, L), F32)
            for pp in range(4):
                p = 4 * g + pp
                h0, h1 = 2 * p, 2 * p + 1
                lanes = slice(128 * p, 128 * p + 128)
                x = xs[:, lanes]
                dyp = dypre[:, lanes]
                a0, a1 = acs[:, h0:h0 + 1], acs[:, h1:h1 + 1]
                dtl = jnp.where(lo, dt[:, h0:h0 + 1], dt[:, h1:h1 + 1])
                xdt = x * dtl
                l0 = jnp.exp(jnp.where(tri, a0 - acs_t[h0:h0 + 1, :], NEG_BIG))
                l1 = jnp.exp(jnp.where(tri, a1 - acs_t[h1:h1 + 1, :], NEG_BIG))
                m0, m1 = gmat * l0, gmat * l1
                dskip = jnp.where(lo[0:1], sp_ref[2:3, h0:h0 + 1], sp_ref[2:3, h1:h1 + 1])
                col_sums[0:1, lanes] = jnp.sum(dyp * x, axis=0, keepdims=True)
                dx = dyp * dskip
                dy0, dy1 = jnp.where(lo, dyp, 0.0), jnp.where(lo, 0.0, dyp)
                x0, x1 = jnp.where(lo, xdt, 0.0), jnp.where(lo, 0.0, xdt)
                dm0, dm1 = _dot(dy0, x0, "nt"), _dot(dy1, x1, "nt")
                dxdt = _dot(m0, dy0, "tn") + _dot(m1, dy1, "tn")
                q0, q1 = dm0 * m0, dm1 * m1
                qcs[h0:h0 + 1, :] = jnp.sum(q0, axis=0, keepdims=True)
                qcs[h1:h1 + 1, :] = jnp.sum(q1, axis=0, keepdims=True)
                row_terms = jnp.where(lo, q0 + pltpu.roll(q0, HEAD_DIM, 1), q1 + pltpu.roll(q1, HEAD_DIM, 1))
                dgmat = dgmat + dm0 * l0 + dm1 * l1
                hin = st_ref[0, p]
                e = jnp.exp(jnp.where(lo, a0, a1))
                ch = _dot(cg, hin, "nt")
                dch = dyp * e
                dcg = dcg + _dot(dch, hin)
                dhin = _dot(dch, cg, "tn")
                dhout = dstate[p]
                al0, al1 = acs[L - 1:L, h0:h0 + 1], acs[L - 1:L, h1:h1 + 1]
                dec = jnp.exp(jnp.where(lo_rows, al0, al1))
                dhin = dhin + dec * dhout
                dal = dhout * hin * dec
                dal0 = _total(jnp.where(lo_rows, dal, 0.0))
                dal1 = _total(dal) - dal0
                dalast = dalast + jnp.where(li1 == h0, dal0, 0.0) + jnp.where(li1 == h1, dal1, 0.0)
                w = jnp.exp(jnp.where(lo, al0 - a0, al1 - a1))
                xw = xdt * w
                dxw = _dot(bg, dhout, "nt")
                dbg = dbg + _dot(xw, dhout)
                dxdt = dxdt + dxw * w
                dww = dxw * xw
                col_sums[1:2, lanes] = jnp.sum(dww, axis=0, keepdims=True)
                acs_terms[:, lanes] = row_terms + dch * ch - dww
                dx = dx + dxdt * dtl
                dt_terms[:, lanes] = dxdt * x
                dxs_buf[:, lanes] = dx
                dstate[p] = dhin
            dcg = dcg + _dot(dgmat, bg)
            dbg = dbg + _dot(dgmat, cg, "tn")
            dbm_buf[:, 128 * g:128 * g + 128] = dbg
            dcm_buf[:, 128 * g:128 * g + 128] = dcg

        head_sums = _split3_dot(col_sums[...], sel_ref[...])
        dskip_g = head_sums[0:1, :]
        dalast = dalast + head_sums[1:2, :]
        ddt = _split3_dot(dt_terms[...], sel_ref[...])
        dacs_tot = _split3_dot(acs_terms[...], sel_ref[...]) - qcs[...].T + jnp.where(ri == L - 1, dalast, 0.0)
        dstep = _rev_cumsum_rows(dacs_tot)
        ddt = ddt + dstep * a
        head_lane = li < N_HEADS
        ddt_pre = jnp.where(head_lane, ddt * _sigmoid(dt_pre), 0.0)
        dsmall_ref[...] = ddt_pre
        da = jnp.sum(jnp.where(head_lane, dstep * dt, 0.0), axis=0, keepdims=True)
        gsp = _stack_rows([jnp.sum(ddt_pre, axis=0, keepdims=True), da * a, dskip_g], L)

        def conv_back(dpost, ds, shifts, w_ref, carry, out_ref, width):
            dpre = dpost * ds
            dext = jnp.concatenate([dpre, carry[...]], axis=0)
            out_ref[...] = _conv_rows_transposed(dext, w_ref[...], SSD_CONV)[:L].astype(BF16)
            carry[...] = dpre[0:8]
            return _conv_weight_grad(dpre, shifts, slice(8, 8 + L), width)

        gwx = conv_back(dxs_buf[...], xs_ds, xs_ext, wx_ref, carry_x, dxs_ref, 1024)
        gwb = conv_back(dbm_buf[...], b_ds, b_ext, wb_ref, carry_b, db_ref, 256)
        gwc = conv_back(dcm_buf[...], c_ds, c_ext, wc_ref, carry_c, dc_ref, 256)

        @pl.when(start)
        def _():
            gwx_ref[...] = gwx
            gwb_ref[...] = gwb
            gwc_ref[...] = gwc
            gsp_ref[...] = gsp
            gnw_ref[...] = gnw

        @pl.when(step > 0)
        def _():
            gwx_ref[...] += gwx
            gwb_ref[...] += gwb
            gwc_ref[...] += gwc
            gsp_ref[...] += gsp
            gnw_ref[...] += gnw

    def ch(c):
        return nc - 1 - c

    row = pl.BlockSpec((L, 1024), lambda c: (ch(c), 0))
    row256 = pl.BlockSpec((L, 256), lambda c: (ch(c), 0))
    in_specs = _ssd_in_specs(rev_nc=nc) + [row, pl.BlockSpec((1, N_PAIRS, 128, 128), lambda c: (ch(c), 0, 0, 0)), row,
                                           pl.BlockSpec((1024, 128), lambda c: (0, 0))]
    out_specs = [row, row, row256, row256, pl.BlockSpec((L, 128), lambda c: (ch(c), 0)),
                 pl.BlockSpec((8, 1024), lambda c: (0, 0)), pl.BlockSpec((8, 256), lambda c: (0, 0)),
                 pl.BlockSpec((8, 256), lambda c: (0, 0)), pl.BlockSpec((8, 128), lambda c: (0, 0)),
                 pl.BlockSpec((1, 1024), lambda c: (0, 0))]
    out_shape = [jax.ShapeDtypeStruct((s, 1024), BF16), jax.ShapeDtypeStruct((s, 1024), BF16),
                 jax.ShapeDtypeStruct((s, 256), BF16), jax.ShapeDtypeStruct((s, 256), BF16),
                 jax.ShapeDtypeStruct((s, 128), F32),
                 jax.ShapeDtypeStruct((8, 1024), F32), jax.ShapeDtypeStruct((8, 256), F32),
                 jax.ShapeDtypeStruct((8, 256), F32), jax.ShapeDtypeStruct((8, 128), F32),
                 jax.ShapeDtypeStruct((1, 1024), F32)]
    scratch = [pltpu.VMEM((N_PAIRS, 128, 128), F32), pltpu.VMEM((8, 1024), F32), pltpu.VMEM((8, 256), F32),
               pltpu.VMEM((8, 256), F32), pltpu.VMEM((L, 1024), F32), pltpu.VMEM((L, 256), F32), pltpu.VMEM((L, 256), F32),
               pltpu.VMEM((L, L), F32), pltpu.VMEM((8, 1024), F32), pltpu.VMEM((L, 1024), F32), pltpu.VMEM((L, 1024), F32)]
    return pl.pallas_call(
        body, name="ssd_bwd", grid=(nc,), in_specs=in_specs, out_specs=out_specs, out_shape=out_shape,
        scratch_shapes=scratch, compiler_params=_params(("arbitrary",)),
    )(proj, proj, proj, proj, proj, proj, proj, conv_w8, conv_w8, conv_w8, conv_b, conv_b, conv_b, small, smallp, norm_w,
      ypre, states, dy, sel)


FOX_SCALE = HEAD_DIM ** -0.5
FOX_T = 256
Q_COL, K_COL, V_COL = 2, 3, 4


def _split_dot(v, m, terms):
    out, rest = None, v
    for i in range(terms):
        piece = rest.astype(BF16)
        out = _dot(piece, m) if out is None else out + _dot(piece, m)
        if i + 1 < terms:
            rest = rest - piece.astype(F32)
    return out


def _split3_dot(v, m):
    return _split_dot(v, m, 3)


def _head_mean(x, sel_ref, selt_ref):
    return _dot(x, sel_ref[...]) * (1.0 / HEAD_DIM)


def _head_spread(v, selt_ref):
    return _split_dot(v, selt_ref[...], 2)


def _head_rstd(x, sel_ref, selt_ref):
    return _head_spread(lax.rsqrt(_head_mean(x * x, sel_ref, selt_ref) + NORM_EPS), selt_ref)


def fox_tables():
    r = np.arange(3 * 128)
    piece, lane = r // 128, r % 128
    head = lane - F_LANE
    is_head = np.logical_and(head >= 0, head < N_HEADS)
    col = 128 * (head // 2) + HEAD_DIM * (1 - head % 2) + piece
    cols = np.arange(1024)
    place_q = np.logical_and(is_head[:, None], cols[None, :] == col[:, None])
    place_k = np.logical_and(is_head[:, None], cols[None, :] == (col + 3)[:, None])
    ones_q = np.logical_and(cols % HEAD_DIM >= 3, cols % HEAD_DIM < 6)[None]
    ones_k = (cols % HEAD_DIM < 3)[None]
    h = np.arange(128) - F_LANE
    ok = np.logical_and(h >= 0, h < N_HEADS)
    same_pair = cols[:, None] // 128 == (h // 2)[None, :]
    fold_even = np.logical_and(np.logical_and(ok, h % 2 == 0)[None, :], same_pair)
    fold_odd = np.logical_and(np.logical_and(ok, h % 2 == 1)[None, :], same_pair)
    as_bf16 = lambda t: jnp.asarray(t.astype(np.float32), BF16)
    return (as_bf16(place_q), as_bf16(place_k), jnp.asarray(ones_q, F32), jnp.asarray(ones_k, F32),
            as_bf16(fold_even), as_bf16(fold_odd))


def fox_prep(proj, small, smallp, qw, kw, sel, selt, place_q, place_k, ones_q, ones_k, *, tm=256):
    s = proj.shape[0]

    def body(q_ref, k_ref, v_ref, small_ref, sp_ref, qw_ref, kw_ref, sel_ref, selt_ref, pq_ref, pk_ref, oq_ref, ok_ref,
             qn_ref, kn_ref, aq_ref, ak_ref, vb_ref, knt_ref, akt_ref, vt_ref, carry):
        @pl.when(pl.program_id(0) == 0)
        def _():
            carry[...] = jnp.zeros_like(carry)

        q = q_ref[...]
        qn_ref[...] = (((q * _head_rstd(q, sel_ref, selt_ref)) * qw_ref[...]) * FOX_SCALE).astype(BF16)
        k = k_ref[...]
        kn = ((k * _head_rstd(k, sel_ref, selt_ref)) * kw_ref[...]).astype(BF16)
        kn_ref[...] = kn
        knt_ref[...] = kn.astype(F32).T.astype(BF16)
        vb_ref[...] = v_ref[...].astype(BF16)
        vt_ref[...] = v_ref[...].T.astype(BF16)
        li = _lane_iota((tm, 128))
        f_lane = jnp.logical_and(li >= F_LANE, li < F_LANE + N_HEADS)
        logf = jnp.where(f_lane, -_softplus(-(small_ref[...] + sp_ref[3:4, :])), 0.0)
        cum = _cumsum_rows(logf) + carry[...]
        carry[...] = cum[tm - 1:tm, :]
        hi = cum.astype(BF16)
        r1 = cum - hi.astype(F32)
        mid = r1.astype(BF16)
        lo = (r1 - mid.astype(F32)).astype(BF16)
        pieces = jnp.concatenate([hi, mid, lo], axis=1)
        aq_ref[...] = (_dot(pieces, pq_ref[...]) + oq_ref[...]).astype(BF16)
        ak = ok_ref[...] - _dot(pieces, pk_ref[...])
        ak_ref[...] = ak.astype(BF16)
        akt_ref[...] = ak.T.astype(BF16)

    row = pl.BlockSpec((tm, 1024), lambda i: (i, 0))
    col = pl.BlockSpec((1024, tm), lambda i: (0, i))
    vec = pl.BlockSpec((1, 1024), lambda i: (0, 0))
    table = pl.BlockSpec((384, 1024), lambda i: (0, 0))
    wide = jax.ShapeDtypeStruct((s, 1024), BF16)
    tall = jax.ShapeDtypeStruct((1024, s), BF16)
    return pl.pallas_call(
        body, name="fox_prep", grid=(s // tm,),
        in_specs=[pl.BlockSpec((tm, 1024), lambda i: (i, Q_COL)), pl.BlockSpec((tm, 1024), lambda i: (i, K_COL)),
                  pl.BlockSpec((tm, 1024), lambda i: (i, V_COL)),
                  pl.BlockSpec((tm, 128), lambda i: (i, 0)), pl.BlockSpec((8, 128), lambda i: (0, 0)), vec, vec,
                  pl.BlockSpec((1024, 128), lambda i: (0, 0)), pl.BlockSpec((128, 1024), lambda i: (0, 0)),
                  table, table, vec, vec],
        out_specs=[row, row, row, row, row, col, col, col],
        out_shape=[wide, wide, wide, wide, wide, tall, tall, tall],
        scratch_shapes=[pltpu.VMEM((1, 128), F32)], compiler_params=_params(("arbitrary",)),
    )(proj, proj, proj, small, smallp, qw, kw, sel, selt, place_q, place_k, ones_q, ones_k)


def fox_fwd(qn, kn, aq, ak, vt, shards=()):
    s = qn.shape[0]
    t = FOX_T
    nq = s // t
    ng = len(shards)

    def body(*refs):
        q_ref, k_ref, aq_ref, ak_ref, vt_ref = refs[:5]
        o_ref, ot_ref, lse_ref = refs[5 + ng:8 + ng]
        p = pl.program_id(0)
        if ng:
            start, forward, finish = _gather_phases(refs[5:5 + ng], refs[8 + ng:8 + 2 * ng], *refs[8 + 2 * ng:])
            pl.when(p == 0)(start)
            pl.when(p == N_PAIRS // 2)(forward)

        @pl.when(p == 0)
        def _():
            lse_ref[...] = jnp.zeros_like(lse_ref)

        lo = _lane_iota((t, 128)) < HEAD_DIM
        lo_rows = _row_iota((128, t)) < HEAD_DIM
        causal_t = _lane_iota((t, t)) >= _row_iota((t, t))

        def q_loop(qi, _):
            q0 = pl.multiple_of(qi * t, t)
            qv, aqv = q_ref[pl.ds(q0, t), :], aq_ref[pl.ds(q0, t), :]
            qa, qb = jnp.where(lo, qv, aqv), jnp.where(lo, aqv, qv)

            def scores(kj):
                k0 = pl.multiple_of(kj * t, t)
                kv, akv = k_ref[pl.ds(k0, t), :], ak_ref[pl.ds(k0, t), :]
                return _dot(jnp.where(lo, kv, akv), qa, "nt"), _dot(jnp.where(lo, akv, kv), qb, "nt")

            def update(kj, stats, s0, s1):
                m0, l0, m1, l1, acc = stats
                vtv = vt_ref[:, pl.ds(pl.multiple_of(kj * t, t), t)]
                n0 = jnp.maximum(m0, jnp.max(s0, axis=0, keepdims=True))
                n1 = jnp.maximum(m1, jnp.max(s1, axis=0, keepdims=True))
                a0, a1 = jnp.exp(m0 - n0), jnp.exp(m1 - n1)
                p0, p1 = jnp.exp(s0 - n0), jnp.exp(s1 - n1)
                l0 = a0 * l0 + jnp.sum(p0, axis=0, keepdims=True)
                l1 = a1 * l1 + jnp.sum(p1, axis=0, keepdims=True)
                acc = (jnp.where(lo_rows, a0, a1) * acc + _dot(jnp.where(lo_rows, vtv, 0.0), p0)
                       + _dot(jnp.where(lo_rows, 0.0, vtv), p1))
                return n0, l0, n1, l1, acc

            def step(kj, carry):
                stats, (s0, s1) = carry[:5], carry[5:]
                nxt = scores(kj + 1)
                return (*update(kj, stats, s0, s1), *nxt)

            def row(val):
                return jnp.full((1, t), val, F32)

            init = (row(NEG_BIG), row(0.0), row(NEG_BIG), row(0.0), jnp.zeros((128, t), F32), *scores(0))
            carry = lax.fori_loop(0, qi, step, init)
            s0, s1 = jnp.where(causal_t, carry[5], NEG_BIG), jnp.where(causal_t, carry[6], NEG_BIG)
            m0, l0, m1, l1, acc = update(qi, carry[:5], s0, s1)
            out_t = acc / jnp.where(lo_rows, l0, l1)
            ot_ref[:, pl.ds(q0, t)] = out_t.astype(BF16)
            o_ref[pl.ds(q0, t), :] = out_t.T.astype(BF16)
            ri = _row_iota((N_HEADS, t))
            old = lse_ref[:, pl.ds(q0, t)]
            lse_ref[:, pl.ds(q0, t)] = jnp.where(
                ri == 2 * p, m0 + jnp.log(l0), jnp.where(ri == 2 * p + 1, m1 + jnp.log(l1), old))
            return 0

        lax.fori_loop(0, nq, q_loop, 0)
        if ng:
            pl.when(p == N_PAIRS - 1)(finish)

    pair = pl.BlockSpec((s, 128), lambda p: (0, p))
    outs = pl.pallas_call(
        body, name="fox_fwd", grid=(N_PAIRS,),
        in_specs=[pair] * 4 + [pl.BlockSpec((128, s), lambda p: (p, 0))] + [ANY] * ng,
        out_specs=[pair, pl.BlockSpec((128, s), lambda p: (p, 0)), pl.BlockSpec((N_HEADS, s), lambda p: (0, 0))] + [ANY] * ng,
        out_shape=[jax.ShapeDtypeStruct((s, 1024), BF16), jax.ShapeDtypeStruct((1024, s), BF16),
                   jax.ShapeDtypeStruct((N_HEADS, s), F32)] + _gather_out_shapes(shards),
        scratch_shapes=_gather_scratch(ng) if ng else [],
        compiler_params=_params(("arbitrary",)),
    )(qn, kn, aq, ak, vt, *shards)
    return outs[0], outs[1], outs[2], list(outs[3:])


def fox_bwd(qn, kn, aq, ak, knt, akt, vb, lse, dmixed, parts=()):
    s = qn.shape[0]
    t = FOX_T
    nq = s // t
    once = pl.Buffered(1)
    ns = len(parts)

    def body(*refs):
        q_ref, k_ref, aq_ref, ak_ref, kt_ref, akt_ref, v_ref, lse_ref, do_ref = refs[:9]
        dq_ref, dk_ref, dv_ref, dc0_ref, dc1_ref = refs[9 + ns:14 + ns]
        p_scr, dp_scr = refs[14 + 2 * ns:16 + 2 * ns]
        p = pl.program_id(0)
        if ns:
            start, finish = _scatter_phases(refs[9:9 + ns], refs[14 + ns:14 + 2 * ns], *refs[16 + 2 * ns:])
            pl.when(p == 0)(start)
        dk_ref[...] = jnp.zeros_like(dk_ref)
        dv_ref[...] = jnp.zeros_like(dv_ref)
        dc0_ref[...] = jnp.zeros_like(dc0_ref)
        dc1_ref[...] = jnp.zeros_like(dc1_ref)
        lo = _lane_iota((t, 128)) < HEAD_DIM
        lo_rows = _row_iota((128, t)) < HEAD_DIM
        causal_t = _lane_iota((t, t)) >= _row_iota((t, t))

        def q_loop(qi, _):
            q0 = pl.multiple_of(qi * t, t)
            qv, aqv = q_ref[pl.ds(q0, t), :], aq_ref[pl.ds(q0, t), :]
            qa, qb = jnp.where(lo, qv, aqv), jnp.where(lo, aqv, qv)
            do = do_ref[pl.ds(q0, t), :]
            doa, dob = jnp.where(lo, do, 0.0).astype(BF16), jnp.where(lo, 0.0, do).astype(BF16)
            lse_blk = lse_ref[:, pl.ds(q0, t)]
            ri = _row_iota((N_HEADS, t))
            lse0 = jnp.sum(jnp.where(ri == 2 * p, lse_blk, 0.0), axis=0, keepdims=True)
            lse1 = jnp.sum(jnp.where(ri == 2 * p + 1, lse_blk, 0.0), axis=0, keepdims=True)

            def scores(kj):
                k0 = pl.multiple_of(kj * t, t)
                kv, akv = k_ref[pl.ds(k0, t), :], ak_ref[pl.ds(k0, t), :]
                return _dot(jnp.where(lo, kv, akv), qa, "nt"), _dot(jnp.where(lo, akv, kv), qb, "nt")

            def pass1(kj, d0, d1, diagonal):
                k0 = pl.multiple_of(kj * t, t)
                vv = v_ref[pl.ds(k0, t), :]
                s0, s1 = scores(kj)
                if diagonal:
                    s0, s1 = jnp.where(causal_t, s0, NEG_BIG), jnp.where(causal_t, s1, NEG_BIG)
                p0, p1 = jnp.exp(s0 - lse0), jnp.exp(s1 - lse1)
                dp0, dp1 = _dot(vv, doa, "nt"), _dot(vv, dob, "nt")
                p_scr[0, kj], p_scr[1, kj] = p0, p1
                dp_scr[0, kj], dp_scr[1, kj] = dp0, dp1
                dv_ref[pl.ds(k0, t), :] += _dot(p0, doa) + _dot(p1, dob)
                return d0 + jnp.sum(p0 * dp0, axis=0, keepdims=True), d1 + jnp.sum(p1 * dp1, axis=0, keepdims=True)

            zero = jnp.zeros((1, t), F32)
            d0, d1 = lax.fori_loop(0, qi, lambda kj, c: pass1(kj, *c, False), (zero, zero))
            d0, d1 = pass1(qi, d0, d1, True)

            def pass2(kj, carry):
                dq0, dq1 = carry
                k0 = pl.multiple_of(kj * t, t)
                p0, p1 = p_scr[0, kj], p_scr[1, kj]
                ds0, ds1 = p0 * (dp_scr[0, kj] - d0), p1 * (dp_scr[1, kj] - d1)
                dk_ref[pl.ds(k0, t), :] += jnp.where(lo, _dot(ds0, qa), _dot(ds1, qb))
                dc0_ref[pl.ds(k0, t), :] += ds0[:, :128] + ds0[:, 128:]
                dc1_ref[pl.ds(k0, t), :] += ds1[:, :128] + ds1[:, 128:]
                ktv, aktv = kt_ref[:, pl.ds(k0, t)], akt_ref[:, pl.ds(k0, t)]
                return dq0 + _dot(jnp.where(lo_rows, ktv, aktv), ds0), dq1 + _dot(jnp.where(lo_rows, aktv, ktv), ds1)

            zq = jnp.zeros((128, t), F32)
            dq0, dq1 = lax.fori_loop(0, qi + 1, pass2, (zq, zq))
            dq_ref[pl.ds(q0, t), :] = jnp.where(lo_rows, dq0, dq1).T
            return 0

        lax.fori_loop(0, nq, q_loop, 0)
        if ns:
            pl.when(p == N_PAIRS - 1)(finish)

    pair = pl.BlockSpec((s, 128), lambda p: (0, p), pipeline_mode=once)
    pair_t = pl.BlockSpec((128, s), lambda p: (p, 0), pipeline_mode=once)
    out = jax.ShapeDtypeStruct((s, 1024), F32)
    outs = pl.pallas_call(
        body, name="fox_bwd", grid=(N_PAIRS,),
        in_specs=[pair, pair, pair, pair, pair_t, pair_t, pair, pl.BlockSpec((N_HEADS, s), lambda p: (0, 0)),
                  pl.BlockSpec((s, 128), lambda p: (0, 8 + p), pipeline_mode=once)] + [ANY] * ns,
        out_specs=[pair] * 5 + [ANY] * ns,
        out_shape=[out] * 5 + [jax.ShapeDtypeStruct(p.shape, p.dtype) for p in parts],
        scratch_shapes=[pltpu.VMEM((2, nq, t, t), F32), pltpu.VMEM((2, nq, t, t), F32)] + (_scatter_scratch(ns) if ns else []),
        compiler_params=_params(("arbitrary",)),
    )(qn, kn, aq, ak, knt, akt, vb, lse, dmixed, *parts)
    return (*outs[:5], _keep_own_blocks(outs[5:], parts))


def fox_post(dqn, dkn, dc0, dc1, proj, small, smallp, qw, kw, sel, selt, fold_even, fold_odd, *, tm=256):
    s = proj.shape[0]
    nrow = s // tm

    def body(dqn_ref, dkn_ref, dc0_ref, dc1_ref, q_ref, k_ref, small_ref, sp_ref, qw_ref, kw_ref, sel_ref, selt_ref,
             fe_ref, fo_ref, dq_ref, dk_ref, dsmall_ref, gqw_ref, gkw_ref, gfb_ref, carry):
        step = pl.program_id(0)

        @pl.when(step == 0)
        def _():
            carry[...] = jnp.zeros_like(carry)

        def norm_bwd(x_ref, w_ref, dn, out_ref):
            x = x_ref[...]
            rf = _head_rstd(x, sel_ref, selt_ref)
            xh = x * rf
            g = dn * w_ref[...]
            mean_gx = _head_spread(_head_mean(g * xh, sel_ref, selt_ref), selt_ref)
            out_ref[...] = (rf * (g - xh * mean_gx)).astype(BF16)
            return jnp.sum(dn * xh, axis=0, keepdims=True)

        gqw = norm_bwd(q_ref, qw_ref, dqn_ref[...] * FOX_SCALE, dq_ref)
        gkw = norm_bwd(k_ref, kw_ref, dkn_ref[...], dk_ref)
        li = _lane_iota((tm, 128))
        f_lane = jnp.logical_and(li >= F_LANE, li < F_LANE + N_HEADS)
        dcum = -(_split3_dot(dc0_ref[...], fe_ref[...]) + _split3_dot(dc1_ref[...], fo_ref[...]))
        dlogf = _rev_cumsum_rows(dcum) + carry[...]
        carry[...] = dlogf[0:1, :]
        dfr = jnp.where(f_lane, dlogf * _sigmoid(-(small_ref[...] + sp_ref[3:4, :])), 0.0)
        dsmall_ref[...] = dfr
        gfb = jnp.sum(dfr, axis=0, keepdims=True)

        @pl.when(step == 0)
        def _():
            gqw_ref[...] = gqw
            gkw_ref[...] = gkw
            gfb_ref[...] = gfb

        @pl.when(step > 0)
        def _():
            gqw_ref[...] += gqw
            gkw_ref[...] += gkw
            gfb_ref[...] += gfb

    def rb(i):
        return nrow - 1 - i

    row = pl.BlockSpec((tm, 1024), lambda i: (rb(i), 0))
    vec = pl.BlockSpec((1, 1024), lambda i: (0, 0))
    fold = pl.BlockSpec((1024, 128), lambda i: (0, 0))
    return pl.pallas_call(
        body, name="fox_post", grid=(nrow,),
        in_specs=[row, row, row, row, pl.BlockSpec((tm, 1024), lambda i: (rb(i), Q_COL)),
                  pl.BlockSpec((tm, 1024), lambda i: (rb(i), K_COL)),
                  pl.BlockSpec((tm, 128), lambda i: (rb(i), 0)), pl.BlockSpec((8, 128), lambda i: (0, 0)), vec, vec,
                  fold, pl.BlockSpec((128, 1024), lambda i: (0, 0)), fold, fold],
        out_specs=[row, row, pl.BlockSpec((tm, 128), lambda i: (rb(i), 0)), vec, vec, pl.BlockSpec((1, 128), lambda i: (0, 0))],
        out_shape=[jax.ShapeDtypeStruct((s, 1024), BF16), jax.ShapeDtypeStruct((s, 1024), BF16),
                   jax.ShapeDtypeStruct((s, 128), F32), jax.ShapeDtypeStruct((1, 1024), F32),
                   jax.ShapeDtypeStruct((1, 1024), F32), jax.ShapeDtypeStruct((1, 128), F32)],
        scratch_shapes=[pltpu.VMEM((1, 128), F32)], compiler_params=_params(("arbitrary",)),
    )(dqn, dkn, dc0, dc1, proj, proj, small, smallp, qw, kw, sel, selt, fold_even, fold_odd)


def local_step(x, target, wm, ws, later_shards, ssd_cw8, ssd_cb, smallp, ssd_nw, qw_t, kw_t, sel, selt,
               norm_mix_w, norm_ffn_w, ffn_cw8, ffn_cb):
    h, h_t = rms_fwd(x, norm_mix_w, name="rms_mix_fwd")
    proj = matmul(h, wm, mode="nn", tm=1024, tn=1408, tk=1024, out_dtype=F32, name="mm_in_proj")
    small = matmul(h, ws, mode="nn", tm=1024, tn=128, tk=1024, out_dtype=F32, name="mm_in_proj_small")
    y_ssd, y_ssd_t, ypre, states = ssd_fwd(proj, small, ssd_cw8, ssd_cb, smallp, ssd_nw)
    place_q, place_k, ones_q, ones_k, fold_even, fold_odd = fox_tables()
    qn, kn, aq, ak, vb, knt, akt, vt = fox_prep(proj, small, smallp, qw_t, kw_t, sel, selt, place_q, place_k, ones_q, ones_k)
    y_fox, y_fox_t, lse, (a_out, a_up, a_down) = fox_fwd(qn, kn, aq, ak, vt, shards=later_shards)
    w_out = a_out.reshape(2048, D_MODEL)
    w_down = a_down.reshape(D_FF, D_MODEL)
    s = x.shape[0]
    shard = lambda index: pl.BlockSpec((None, 1024, 1408), index)
    x1 = matmul(y_ssd, w_out, mode="nn", tm=1024, tn=1024, tk=1024, out_dtype=F32, name="mm_out_ssd", add=x)
    x1 = matmul(y_fox, w_out, mode="nn", tm=1024, tn=1024, tk=1024, out_dtype=F32, name="mm_out_fox", add=x1, b_koff=1)
    hf, hf_t = rms_fwd(x1, norm_ffn_w, name="rms_ffn_fwd")
    hu = matmul(hf, a_up, mode="nn", tm=1024, tn=1408, tk=1024, out_dtype=F32, name="mm_up",
                layout=dict(m=s, n=2 * D_FF, k=D_MODEL, b_spec=shard(lambda i, j, kk: (j, kk, 0))))
    act, act_t = ffn_mid_fwd(hu, ffn_cw8, ffn_cb)
    y = matmul(act, w_down, mode="nn", tm=1024, tn=1024, tk=1408, out_dtype=F32, name="mm_down", add=x1)
    dy, sq = loss_head(y, target)

    dact = matmul(dy, w_down, mode="nt", tm=1024, tn=1408, tk=1024, out_dtype=F32, name="mm_dact")
    g_down = matmul(act_t, dy, mode="nn", tm=1408, tn=1024, tk=1024, out_dtype=BF16, name="mm_dw_down")
    dhu, gcw_g, gcw_v = ffn_mid_bwd(hu, dact, ffn_cw8, ffn_cb)
    dhf = matmul(dhu, a_up, mode="nt", tm=1024, tn=1024, tk=1408, out_dtype=F32, name="mm_dhf",
                 layout=dict(m=s, n=D_MODEL, k=2 * D_FF, a_spec=shard(lambda i, j, kk: (kk // 2, i, kk % 2)),
                             b_spec=shard(lambda i, j, kk: (kk, 0, 0))))
    g_up = matmul(hf_t, dhu, mode="nn", tm=1024, tn=1408, tk=1024, out_dtype=BF16, name="mm_dw_up",
                  layout=dict(m=D_MODEL, n=2 * D_FF, k=s, b_spec=shard(lambda i, j, kk: (j // 2, kk, j % 2)),
                              o_spec=shard(lambda i, j, kk: (j, i, 0)), out_shape=(4, D_MODEL, 1408)))
    dx1, g_norm_ffn = rms_bwd(dhf, x1, norm_ffn_w, dy, name="rms_ffn_bwd")
    dmixed = matmul(dx1, w_out, mode="nt", tm=1024, tn=1024, tk=1024, out_dtype=F32, name="mm_dmixed")
    g_out_a = matmul(y_ssd_t, dx1, mode="nn", tm=1024, tn=1024, tk=1024, out_dtype=BF16, name="mm_dw_out_ssd")
    g_out_b = matmul(y_fox_t, dx1, mode="nn", tm=1024, tn=1024, tk=1024, out_dtype=BF16, name="mm_dw_out_fox")
    early = [jnp.concatenate([g_out_a, g_out_b], axis=0).reshape(4, 512, D_MODEL), g_up, g_down.reshape(4, 704, D_MODEL)]
    mine, theirs = pair_swap_halves(early, name="pair_swap_early")
    parts = [add_pair(a, b, name="add_pair_" + n, tr=ADAM_ROWS[n]) for a, b, n in zip(mine, theirs, BIG_NAMES[1:])]
    dz, dxs, db, dc, dsmall_ssd, gcw_x, gcw_b, gcw_c, g_sp, g_ssd_nw = ssd_bwd(
        proj, small, ssd_cw8, ssd_cb, smallp, ssd_nw, ypre, states, dmixed, sel)
    dqn, dkn, dv, dc0, dc1, landed_early = fox_bwd(qn, kn, aq, ak, knt, akt, vb, lse, dmixed, parts=parts)
    dq, dk, dsmall_fox, g_qw, g_kw, g_fb = fox_post(dqn, dkn, dc0, dc1, proj, small, smallp, qw_t, kw_t, sel, selt,
                                                    fold_even, fold_odd)
    dproj = jnp.concatenate([dz, dxs, dq, dk, dv.astype(BF16), db, dc], axis=1)
    dsmall = (dsmall_ssd + dsmall_fox).astype(BF16)
    g_wm = matmul(h_t, dproj, mode="nn", tm=1024, tn=1408, tk=1024, out_dtype=BF16, name="mm_dw_in")
    g_ws = matmul(h_t, dsmall, mode="nn", tm=1024, tn=128, tk=1024, out_dtype=BF16, name="mm_dw_in_small")
    g_in = jnp.concatenate([g_wm[:, :2048], g_wm[:, 5120:5632], g_ws[:, :16], g_wm[:, 2048:5120], g_ws[:, 16:32]], axis=1)
    mine, theirs = pair_swap_halves([g_in.reshape(D_MODEL, 4, 1416).transpose(1, 0, 2)], name="pair_swap_w_in")
    part_in = add_pair(mine[0], theirs[0], name="add_pair_w_in", tr=ADAM_ROWS["w_in"])
    dh, landed_in = matmul(dproj, wm, mode="nt", tm=1024, tn=1024, tk=1408, out_dtype=F32, name="mm_dh", scatter=[part_in])
    dh = matmul(dsmall, ws, mode="nt", tm=1024, tn=1024, tk=128, out_dtype=F32, name="mm_dh_small", add=dh)
    grad_x, g_norm_mix = rms_bwd(dh, x, norm_mix_w, dx1, name="rms_mix_bwd")
    return dict(
        sq=sq, grad_x=grad_x, landed=landed_in + landed_early,
        g_norm_mix=g_norm_mix, g_norm_ffn=g_norm_ffn, g_ssd_nw=g_ssd_nw,
        g_ssd_cw=jnp.concatenate([gcw_x, gcw_b, gcw_c], axis=1), g_sp=g_sp, g_fb=g_fb, g_qw=g_qw, g_kw=g_kw,
        g_ffn_cw=jnp.concatenate([gcw_g, gcw_v], axis=1))


def adamw(w, g, m, v, *, name, tr):
    rows, cols = w.shape

    def body(w_ref, g_ref, m_ref, v_ref, d_ref, mo_ref, vo_ref):
        gv = g_ref[...]
        mn = ADAM_B1 * m_ref[...] + (1.0 - ADAM_B1) * gv
        vn = ADAM_B2 * v_ref[...] + (1.0 - ADAM_B2) * (gv * gv)
        m_hat = mn / (1.0 - ADAM_B1 ** ADAM_STEP)
        v_hat = vn / (1.0 - ADAM_B2 ** ADAM_STEP)
        d_ref[...] = -ADAM_LR * (m_hat / (jnp.sqrt(v_hat) + ADAM_EPS) + ADAM_WD * w_ref[...])
        mo_ref[...] = mn
        vo_ref[...] = vn

    blk = pl.BlockSpec((tr, cols), lambda i: (i, 0))
    shp = jax.ShapeDtypeStruct((rows, cols), F32)
    return pl.pallas_call(
        body, name=name, grid=(rows // tr,), in_specs=[blk] * 4, out_specs=[blk] * 3, out_shape=[shp] * 3,
        compiler_params=_params(("parallel",)),
    )(w, g, m, v)


def add_pair(a, b, *, name, tr):
    _, rows, cols = a.shape

    def body(a_ref, b_ref, o_ref):
        o_ref[...] = (a_ref[...].astype(F32) + b_ref[...].astype(F32)).astype(BF16)

    blk = pl.BlockSpec((1, tr, cols), lambda j, i: (j, i, 0))
    return pl.pallas_call(
        body, name=name, grid=(4, rows // tr), in_specs=[blk, blk], out_specs=blk,
        out_shape=jax.ShapeDtypeStruct(a.shape, BF16), compiler_params=_params(("parallel", "parallel")),
    )(a, b)


def sum_chips(parts, core, *, name, tr):
    _, rows, cols = parts.shape
    nblk = rows // tr

    def body(c_ref, p_ref, o_ref):
        acc = p_ref[0].astype(F32)
        for k in range(1, 4):
            acc = acc + p_ref[k].astype(F32)
        o_ref[...] = acc

    grid_spec = pltpu.PrefetchScalarGridSpec(
        num_scalar_prefetch=1, grid=(nblk,), in_specs=[pl.BlockSpec((4, tr, cols), lambda i, c: (0, i, 0))],
        out_specs=pl.BlockSpec((tr, cols), lambda i, c: (c[0] * nblk + i, 0)))
    return pl.pallas_call(
        body, name=name, grid_spec=grid_spec, out_shape=jax.ShapeDtypeStruct((2 * rows, cols), F32),
        compiler_params=_params(("parallel",)),
    )(core, parts)


ANY = pl.BlockSpec(memory_space=pl.ANY)


def _place():
    x, y, c = lax.axis_index("x"), lax.axis_index("y"), lax.axis_index("c")
    chips = [(1 - x, y), (x, 1 - y), (1 - x, 1 - y)]
    return x, y, c, chips


def _chunks(rows):
    size = next((c for c in (128, 176, 64, 32, 16, 8) if rows % c == 0), rows)
    return [(r, size) for r in range(0, rows, size)]


def gather_weights(shards):
    n = len(shards)

    def body(*refs):
        start, forward, finish = _gather_phases(refs[:n], refs[n:2 * n], *refs[2 * n:])
        start()
        forward()
        finish()

    gathered = pl.pallas_call(
        body, name="gather_weights", in_specs=[ANY] * n, out_specs=[ANY] * n,
        out_shape=_gather_out_shapes(shards), scratch_shapes=_gather_scratch(n),
    )(*shards)
    return gathered


def _gather_out_shapes(shards):
    return [jax.ShapeDtypeStruct((4,) + s.shape, s.dtype) for s in shards]


def _gather_scratch(n):
    return [pltpu.SemaphoreType.DMA((n, 7)), pltpu.SemaphoreType.DMA((n, 7))]


def _gather_phases(ins, outs, send_sems, recv_sems):
    n = len(ins)
    x, y, c, chips = _place()
    me = 2 * x + y
    sibling = (x, y, 1 - c)
    blks = [2 * cx + cy for cx, cy in chips]

    def half(a, blk, r=0, nr=None):
        rows = ins[a].shape[0] // 2
        return outs[a].at[blk, pl.ds(c * rows + r, rows if nr is None else nr), :]

    def to_chip(a, t, r=0, nr=None):
        rows = ins[a].shape[0] // 2
        return pltpu.make_async_remote_copy(
            src_ref=ins[a].at[pl.ds(c * rows + r, rows if nr is None else nr), :], dst_ref=half(a, me, r, nr),
            send_sem=send_sems.at[a, t], recv_sem=recv_sems.at[a, t], device_id=(*chips[t], c), device_id_type=MESH)

    def from_chip(a, t):
        return pltpu.make_async_remote_copy(
            src_ref=half(a, blks[t]), dst_ref=half(a, blks[t]), send_sem=send_sems.at[a, t], recv_sem=recv_sems.at[a, t],
            device_id=(*chips[t], c), device_id_type=MESH)

    def to_sibling(a, t, r=0, nr=None):
        return pltpu.make_async_remote_copy(
            src_ref=half(a, blks[t], r, nr), dst_ref=half(a, blks[t], r, nr), send_sem=send_sems.at[a, 3 + t],
            recv_sem=recv_sems.at[a, 3 + t], device_id=sibling, device_id_type=MESH)

    def from_sibling(a, t):
        rows = ins[a].shape[0] // 2
        dst = outs[a].at[blks[t], pl.ds((1 - c) * rows, rows), :]
        return pltpu.make_async_remote_copy(
            src_ref=dst, dst_ref=dst, send_sem=send_sems.at[a, 3 + t], recv_sem=recv_sems.at[a, 3 + t],
            device_id=sibling, device_id_type=MESH)

    def own(a, r=0, nr=None):
        return pltpu.make_async_remote_copy(
            src_ref=ins[a].at[pl.ds(r, ins[a].shape[0] if nr is None else nr), :],
            dst_ref=outs[a].at[me, pl.ds(r, ins[a].shape[0] if nr is None else nr), :],
            send_sem=send_sems.at[a, 6], recv_sem=recv_sems.at[a, 6], device_id=sibling, device_id_type=MESH)

    def start():
        for a in range(n):
            for t in range(3):
                for r, nr in _chunks(ins[a].shape[0] // 2):
                    to_chip(a, t, r, nr).start()
            for r, nr in _chunks(ins[a].shape[0]):
                own(a, r, nr).start()

    def forward():
        for a in range(n):
            for t in range(3):
                from_chip(a, t).wait_recv()
                for r, nr in _chunks(ins[a].shape[0] // 2):
                    to_sibling(a, t, r, nr).start()

    def finish():
        for a in range(n):
            for t in range(3):
                from_sibling(a, t).wait_recv()
        for a in range(n):
            for t in range(3):
                to_chip(a, t).wait_send()
                to_sibling(a, t).wait_send()
            own(a).wait()

    return start, forward, finish


def pair_swap_halves(grads, *, name):
    n = len(grads)

    def body(*refs):
        ins, theirs = refs[:n], refs[n:2 * n]
        send_sems, recv_sems = refs[2 * n:]
        x, y, c, _ = _place()
        sibling = (x, y, 1 - c)
        for a in range(n):
            rows = ins[a].shape[1] // 2
            for j in range(4):
                for r, nr in _chunks(rows):
                    pltpu.make_async_remote_copy(
                        src_ref=ins[a].at[j, pl.ds((1 - c) * rows + r, nr), :], dst_ref=theirs[a].at[j, pl.ds(r, nr), :],
                        send_sem=send_sems.at[a], recv_sem=recv_sems.at[a], device_id=sibling, device_id_type=MESH).start()
        for a in range(n):
            pltpu.make_async_remote_copy(src_ref=theirs[a], dst_ref=theirs[a], send_sem=send_sems.at[a],
                                         recv_sem=recv_sems.at[a], device_id=sibling, device_id_type=MESH).wait()

    halves = [jax.ShapeDtypeStruct((4, g.shape[1] // 2, g.shape[2]), g.dtype) for g in grads]
    theirs = pl.pallas_call(
        body, name=name, in_specs=[ANY] * n, out_specs=[ANY] * n, out_shape=halves,
        scratch_shapes=[pltpu.SemaphoreType.DMA((n,)), pltpu.SemaphoreType.DMA((n,))],
    )(*grads)
    c = lax.axis_index("c")
    mine = [lax.dynamic_slice_in_dim(g, c * (g.shape[1] // 2), g.shape[1] // 2, axis=1) for g in grads]
    return mine, theirs


def _scatter_scratch(n):
    return [pltpu.SemaphoreType.DMA((n, 3)), pltpu.SemaphoreType.DMA((n, 3))]


def _keep_own_blocks(landed, parts):
    if not parts:
        return []
    chip = 2 * lax.axis_index("x") + lax.axis_index("y")
    return [lax.dynamic_update_slice(l, lax.dynamic_slice_in_dim(p, chip, 1, axis=0), (chip, 0, 0))
            for l, p in zip(landed, parts)]


def _scatter_phases(ins, outs, send_sems, recv_sems):
    n = len(ins)
    x, y, c, chips = _place()
    me = 2 * x + y
    blks = [2 * cx + cy for cx, cy in chips]

    def start():
        for a in range(n):
            for r, nr in _chunks(ins[a].shape[1]):
                for t in range(3):
                    pltpu.make_async_remote_copy(
                        src_ref=ins[a].at[blks[t], pl.ds(r, nr), :], dst_ref=outs[a].at[me, pl.ds(r, nr), :],
                        send_sem=send_sems.at[a, t], recv_sem=recv_sems.at[a, t],
                        device_id=(*chips[t], c), device_id_type=MESH).start()

    def finish():
        for a in range(n):
            for t in range(3):
                pltpu.make_async_remote_copy(
                    src_ref=outs[a].at[blks[t]], dst_ref=outs[a].at[blks[t]], send_sem=send_sems.at[a, t],
                    recv_sem=recv_sems.at[a, t], device_id=(*chips[t], c), device_id_type=MESH).wait()

    return start, finish


def pair_join_halves(bufs):
    n = len(bufs)

    def body(*refs):
        outs = refs[n:2 * n]
        send_sems, recv_sems = refs[2 * n:]
        x, y, c, _ = _place()
        sibling = (x, y, 1 - c)
        for a in range(n):
            rows = outs[a].shape[0] // 2
            for r, nr in _chunks(rows):
                mine = outs[a].at[pl.ds(c * rows + r, nr), :]
                pltpu.make_async_remote_copy(src_ref=mine, dst_ref=mine, send_sem=send_sems.at[a], recv_sem=recv_sems.at[a],
                                             device_id=sibling, device_id_type=MESH).start()
        for a in range(n):
            rows = outs[a].shape[0] // 2
            pltpu.make_async_remote_copy(
                src_ref=outs[a].at[pl.ds(c * rows, rows), :], dst_ref=outs[a].at[pl.ds((1 - c) * rows, rows), :],
                send_sem=send_sems.at[a], recv_sem=recv_sems.at[a], device_id=sibling, device_id_type=MESH).wait()

    return pl.pallas_call(
        body, name="pair_join_halves", in_specs=[ANY] * n, out_specs=[ANY] * n,
        out_shape=[jax.ShapeDtypeStruct(b.shape, b.dtype) for b in bufs], input_output_aliases={a: a for a in range(n)},
        scratch_shapes=[pltpu.SemaphoreType.DMA((n,)), pltpu.SemaphoreType.DMA((n,))],
    )(*bufs)


def allreduce_small(packed):
    rows = packed.shape[0]

    def body(in_ref, out_ref, gathered, send_sems, recv_sems):
        x, y, c, _ = _place()
        me = 4 * x + 2 * y + c
        gathered[me] = in_ref[...]
        flips = [(fx, fy, fc) for fx in (0, 1) for fy in (0, 1) for fc in (0, 1)][1:]
        peers = [((1 - x) if fx else x, (1 - y) if fy else y, (1 - c) if fc else c) for fx, fy, fc in flips]
        copies = []
        for t, peer in enumerate(peers):
            cp = pltpu.make_async_remote_copy(
                src_ref=in_ref, dst_ref=gathered.at[me], send_sem=send_sems.at[t], recv_sem=recv_sems.at[t],
                device_id=peer, device_id_type=MESH)
            cp.start()
            copies.append(cp)
        for t, (px, py, pc) in enumerate(peers):
            slot = gathered.at[4 * px + 2 * py + pc]
            pltpu.make_async_remote_copy(
                src_ref=slot, dst_ref=slot, send_sem=send_sems.at[t], recv_sem=recv_sems.at[t],
                device_id=(px, py, pc), device_id_type=MESH).wait_recv()
        for cp in copies:
            cp.wait_send()
        acc = gathered[0]
        for k in range(1, 8):
            acc = acc + gathered[k]
        out_ref[...] = acc

    vm = pl.BlockSpec(memory_space=pltpu.VMEM)
    return pl.pallas_call(
        body, name="allreduce_small", in_specs=[vm], out_specs=vm, out_shape=jax.ShapeDtypeStruct(packed.shape, F32),
        scratch_shapes=[pltpu.VMEM((8, rows, 128), F32), pltpu.SemaphoreType.DMA((7,)), pltpu.SemaphoreType.DMA((7,))],
    )(packed)


SMALL_NAMES = ("norm_mix_w", "ssd_conv_w", "ssd_conv_b", "ssd_dt_bias", "ssd_a_log", "ssd_d", "ssd_norm_w", "fox_f_bias",
               "fox_q_norm_w", "fox_k_norm_w", "norm_ffn_w", "ffn_conv_w", "ffn_conv_b")
BIG_NAMES = ("w_in", "w_out", "w_up", "w_down")
WEIGHT_ORDER = ("norm_mix_w", "w_in", "ssd_conv_w", "ssd_conv_b", "ssd_dt_bias", "ssd_a_log", "ssd_d", "ssd_norm_w",
                "fox_f_bias", "fox_q_norm_w", "fox_k_norm_w", "w_out", "norm_ffn_w", "w_up", "ffn_conv_w", "ffn_conv_b", "w_down")
ADAM_ROWS = {"w_in": 256, "w_out": 256, "w_up": 256, "w_down": 176}


def _pack(arrays):
    pieces = []
    for a in arrays:
        flat = a.reshape(-1).astype(F32)
        pieces += [flat, jnp.zeros(((-flat.shape[0]) % 1024,), F32)]
    return jnp.concatenate(pieces).reshape(-1, 128)


def _unpack(packed, shapes):
    out, r = [], 0
    for shp in shapes:
        size = 1
        for d in shp:
            size *= d
        nrow = 8 * (-(-size // 1024))
        out.append(packed[r:r + nrow].reshape(-1)[:size].reshape(shp))
        r += nrow
    return out


def _pad_rows(a, rows):
    return jnp.pad(a, ((0, rows - a.shape[0]), (0, 0)))


def kernel(x, norm_mix_w, w_in, ssd_conv_w, ssd_conv_b, ssd_dt_bias, ssd_a_log, ssd_d, ssd_norm_w, fox_f_bias, fox_q_norm_w, fox_k_norm_w, w_out, norm_ffn_w, w_up, ffn_conv_w, ffn_conv_b, w_down, loss_target, m_norm_mix_w, m_w_in, m_ssd_conv_w, m_ssd_conv_b, m_ssd_dt_bias, m_ssd_a_log, m_ssd_d, m_ssd_norm_w, m_fox_f_bias, m_fox_q_norm_w, m_fox_k_norm_w, m_w_out, m_norm_ffn_w, m_w_up, m_ffn_conv_w, m_ffn_conv_b, m_w_down, v_norm_mix_w, v_w_in, v_ssd_conv_w, v_ssd_conv_b, v_ssd_dt_bias, v_ssd_a_log, v_ssd_d, v_ssd_norm_w, v_fox_f_bias, v_fox_q_norm_w, v_fox_k_norm_w, v_w_out, v_norm_ffn_w, v_w_up, v_ffn_conv_w, v_ffn_conv_b, v_w_down):
    w = dict(norm_mix_w=norm_mix_w, w_in=w_in, ssd_conv_w=ssd_conv_w, ssd_conv_b=ssd_conv_b, ssd_dt_bias=ssd_dt_bias,
             ssd_a_log=ssd_a_log, ssd_d=ssd_d, ssd_norm_w=ssd_norm_w, fox_f_bias=fox_f_bias, fox_q_norm_w=fox_q_norm_w,
             fox_k_norm_w=fox_k_norm_w, w_out=w_out, norm_ffn_w=norm_ffn_w, w_up=w_up, ffn_conv_w=ffn_conv_w,
             ffn_conv_b=ffn_conv_b, w_down=w_down)
    m = dict(norm_mix_w=m_norm_mix_w, w_in=m_w_in, ssd_conv_w=m_ssd_conv_w, ssd_conv_b=m_ssd_conv_b, ssd_dt_bias=m_ssd_dt_bias,
             ssd_a_log=m_ssd_a_log, ssd_d=m_ssd_d, ssd_norm_w=m_ssd_norm_w, fox_f_bias=m_fox_f_bias, fox_q_norm_w=m_fox_q_norm_w,
             fox_k_norm_w=m_fox_k_norm_w, w_out=m_w_out, norm_ffn_w=m_norm_ffn_w, w_up=m_w_up, ffn_conv_w=m_ffn_conv_w,
             ffn_conv_b=m_ffn_conv_b, w_down=m_w_down)
    v = dict(norm_mix_w=v_norm_mix_w, w_in=v_w_in, ssd_conv_w=v_ssd_conv_w, ssd_conv_b=v_ssd_conv_b, ssd_dt_bias=v_ssd_dt_bias,
             ssd_a_log=v_ssd_a_log, ssd_d=v_ssd_d, ssd_norm_w=v_ssd_norm_w, fox_f_bias=v_fox_f_bias, fox_q_norm_w=v_fox_q_norm_w,
             fox_k_norm_w=v_fox_k_norm_w, w_out=v_w_out, norm_ffn_w=v_norm_ffn_w, w_up=v_w_up, ffn_conv_w=v_ffn_conv_w,
             ffn_conv_b=v_ffn_conv_b, w_down=v_w_down)
    chip = 2 * lax.axis_index("x") + lax.axis_index("y")

    a_in, a_scw, a_fcw = gather_weights([w_in[0].astype(BF16), _pad_rows(ssd_conv_w[0], 16), _pad_rows(ffn_conv_w[0], 16)])
    later_shards = [w_out[0].astype(BF16), w_up[0].astype(BF16), w_down[0].astype(BF16)]
    w_full = a_in.transpose(1, 0, 2).reshape(D_MODEL, IN_COLS)
    wm = jnp.concatenate([w_full[:, :2048], w_full[:, 2576:5648], w_full[:, 2048:2560]], axis=1)
    ws = jnp.concatenate([w_full[:, 2560:2576], w_full[:, 5648:5664], jnp.zeros((D_MODEL, SMALL_COLS - 32), BF16)], axis=1)
    ssd_cw8 = a_scw.transpose(1, 0, 2).reshape(16, 1536)[:8]
    ffn_cw8 = a_fcw.transpose(1, 0, 2).reshape(16, 2 * D_FF)[:8]
    smallp = jnp.zeros((8, 128), F32)
    smallp = smallp.at[0, :16].set(ssd_dt_bias[0]).at[1, :16].set(ssd_a_log[0]).at[2, :16].set(ssd_d[0])
    smallp = smallp.at[3, F_LANE:F_LANE + 16].set(fox_f_bias[0])
    qw_t = jnp.tile(fox_q_norm_w[0], N_HEADS)[None]
    kw_t = jnp.tile(fox_k_norm_w[0], N_HEADS)[None]
    sel = jnp.asarray((np.arange(1024)[:, None] // HEAD_DIM == np.arange(128)[None, :]).astype(np.float32), BF16)

    res = local_step(x[0], loss_target[0], wm, ws, later_shards, ssd_cw8, ssd_conv_b, smallp, ssd_norm_w, qw_t, kw_t,
                     sel, sel.T, norm_mix_w, norm_ffn_w, ffn_cw8, ffn_conv_b)

    full_shapes = [(1, 1024), (1, 4, 1536), (1, 1536), (1, 16), (1, 16), (1, 16), (1, 1024), (1, 16), (1, 64), (1, 64),
                   (1, 1024), (1, 3, 2 * D_FF), (1, 2 * D_FF), (1,)]
    local_small = [res["g_norm_mix"], res["g_ssd_cw"][:4], res["g_ssd_cw"][4], res["g_sp"][0, :16], res["g_sp"][1, :16],
                   res["g_sp"][2, :16], res["g_ssd_nw"], res["g_fb"][0, F_LANE:F_LANE + 16],
                   res["g_qw"].reshape(N_HEADS, HEAD_DIM).sum(0), res["g_kw"].reshape(N_HEADS, HEAD_DIM).sum(0),
                   res["g_norm_ffn"], res["g_ffn_cw"][:3], res["g_ffn_cw"][3], jnp.sum(res["sq"])]
    summed = _unpack(allreduce_small(_pack(local_small)), full_shapes)
    loss = (0.5 / D_MODEL) * summed[-1][0]
    g_small = dict(zip(SMALL_NAMES, summed[:-1]))
    g_small["ssd_conv_w"] = lax.dynamic_slice(g_small["ssd_conv_w"], (0, 0, 384 * chip), (1, 4, 384))
    g_small["ffn_conv_w"] = lax.dynamic_slice(g_small["ffn_conv_w"], (0, 0, 1408 * chip), (1, 3, 1408))

    landed = res["landed"]
    core = lax.axis_index("c").astype(jnp.int32).reshape(1)
    halves = [sum_chips(p, core, name="sum_chips_" + n, tr=ADAM_ROWS[n]) for p, n in zip(landed, BIG_NAMES)]
    g_big = dict(zip(BIG_NAMES, pair_join_halves(halves)))

    grads, deltas, new_m, new_v = {}, {}, {}, {}
    for n in BIG_NAMES:
        d, mn, vn = adamw(w[n][0], g_big[n], m[n][0], v[n][0], name="adamw_" + n, tr=ADAM_ROWS[n])
        grads[n], deltas[n], new_m[n], new_v[n] = g_big[n][None], d[None], mn[None], vn[None]
    shapes = [w[n].shape for n in SMALL_NAMES]
    packed_w = _pack([w[n] for n in SMALL_NAMES])
    d, mn, vn = adamw(packed_w, _pack([g_small[n] for n in SMALL_NAMES]), _pack([m[n] for n in SMALL_NAMES]),
                      _pack([v[n] for n in SMALL_NAMES]), name="adamw_small", tr=packed_w.shape[0])
    for n, dd, mm, vv in zip(SMALL_NAMES, _unpack(d, shapes), _unpack(mn, shapes), _unpack(vn, shapes)):
        grads[n], deltas[n], new_m[n], new_v[n] = g_small[n].reshape(w[n].shape), dd, mm, vv
    return (loss, res["grad_x"][None], *[grads[n] for n in WEIGHT_ORDER], *[deltas[n] for n in WEIGHT_ORDER],
            *[new_m[n] for n in WEIGHT_ORDER], *[new_v[n] for n in WEIGHT_ORDER])
```

```python
import functools

import jax
import jax.numpy as jnp
import numpy as np
from jax import lax
from jax.experimental import pallas as pl
from jax.experimental.pallas import tpu as pltpu

F32 = jnp.float32
BF16 = jnp.bfloat16
MESH = pl.DeviceIdType.MESH

D_MODEL = 1024
HEAD_DIM = 64
N_HEADS = 16
N_PAIRS = N_HEADS // 2
SSD_CHUNK = 128
SSD_STATE = 128
SSD_CONV = 4
D_FF = 2816
FFN_CONV = 3
NORM_EPS = 1e-6
MAIN_COLS = 5632
SMALL_COLS = 128
F_LANE = 16
IN_COLS = 5664

ADAM_LR = 0.001
ADAM_B1 = 0.9
ADAM_B2 = 0.999
ADAM_EPS = 1e-08
ADAM_WD = 0.01
ADAM_STEP = 10

VMEM_LIMIT_V7X = 56 * 1024 * 1024
NEG_BIG = -1e30


def _params(sem=None):
    return pltpu.CompilerParams(dimension_semantics=sem, vmem_limit_bytes=VMEM_LIMIT_V7X)


def _sigmoid(x):
    return 1.0 / (1.0 + jnp.exp(-x))


def _silu_and_grad(x):
    s = _sigmoid(x)
    return x * s, s * (1.0 + x * (1.0 - s))


def _shift_down(v, j):
    return v if j == 0 else pltpu.roll(v, j, 0)


def _shift_up(v, j):
    return v if j == 0 else pltpu.roll(v, v.shape[0] - j, 0)


def _row_iota(shape):
    return lax.broadcasted_iota(jnp.int32, shape, 0)


def _lane_iota(shape):
    return lax.broadcasted_iota(jnp.int32, shape, 1)


def _dot(a, b, mode="nn"):
    dims = {"nn": (((1,), (0,)), ((), ())), "nt": (((1,), (1,)), ((), ())), "tn": (((0,), (0,)), ((), ()))}[mode]
    return lax.dot_general(a.astype(BF16), b.astype(BF16), dims, preferred_element_type=F32)


def _dot_f32(a, b):
    return jnp.dot(a, b, precision=lax.Precision.HIGHEST, preferred_element_type=F32)


def matmul(a, b, *, mode, tm, tn, tk, out_dtype, name, add=None, b_koff=0, scatter=(), layout=None):
    layout = layout or {}
    if layout:
        m, n, k = layout["m"], layout["n"], layout["k"]
    else:
        (m, k), n = a.shape, (b.shape[1] if mode == "nn" else b.shape[0])
    assert m % tm == 0 and n % tn == 0 and k % tk == 0, (name, m, n, k, tm, tn, tk)
    nk = k // tk
    grid = (m // tm, n // tn, nk)
    a_spec = layout.get("a_spec") or pl.BlockSpec((tm, tk), lambda i, j, kk: (i, kk))
    b_spec = layout.get("b_spec") or (pl.BlockSpec((tn, tk), lambda i, j, kk: (j, kk + b_koff)) if mode == "nt"
                                      else pl.BlockSpec((tk, tn), lambda i, j, kk: (kk + b_koff, j)))
    o_spec = layout.get("o_spec") or pl.BlockSpec((tm, tn), lambda i, j, kk: (i, j))
    out_struct = jax.ShapeDtypeStruct(layout.get("out_shape", (m, n)), out_dtype)
    has_add = add is not None
    n_in = 3 if has_add else 2
    ns = len(scatter)

    def body(*refs):
        a_ref, b_ref = refs[:2]
        add_ref = refs[2] if has_add else None
        o_ref, acc_ref = refs[n_in + ns], refs[n_in + 2 * ns + 1]
        kk = pl.program_id(2)
        if ns:
            step = (pl.program_id(0) * grid[1] + pl.program_id(1)) * grid[2] + kk
            start, finish_copies = _scatter_phases(refs[n_in:n_in + ns], refs[n_in + ns + 1:n_in + 2 * ns + 1],
                                                   *refs[n_in + 2 * ns + 2:])
            pl.when(step == 0)(start)
        part = _dot(a_ref[...], b_ref[...], mode)

        def finish(total):
            if has_add:
                total = total + add_ref[...]
            o_ref[...] = total.astype(out_dtype)

        if nk == 1:
            finish(part)
        else:
            @pl.when(kk == 0)
            def _():
                acc_ref[...] = part

            @pl.when(jnp.logical_and(kk > 0, kk < nk - 1))
            def _():
                acc_ref[...] += part

            @pl.when(kk == nk - 1)
            def _():
                finish(acc_ref[...] + part)

        if ns:
            pl.when(step == grid[0] * grid[1] * grid[2] - 1)(finish_copies)

    in_specs = [a_spec, b_spec] + ([o_spec] if has_add else [])
    args = (a, b) + ((add,) if has_add else ())
    acc = pltpu.VMEM((tm, tn) if nk > 1 else (8, 128), F32)
    if not ns:
        return pl.pallas_call(
            body, name=name, grid=grid, in_specs=in_specs, out_specs=o_spec, out_shape=out_struct,
            scratch_shapes=[acc], compiler_params=_params(("parallel", "parallel", "arbitrary")),
        )(*args)
    outs = pl.pallas_call(
        body, name=name, grid=grid, in_specs=in_specs + [ANY] * ns, out_specs=[o_spec] + [ANY] * ns,
        out_shape=[out_struct] + [jax.ShapeDtypeStruct(p.shape, p.dtype) for p in scatter],
        scratch_shapes=[acc] + _scatter_scratch(ns), compiler_params=_params(("arbitrary", "arbitrary", "arbitrary")),
    )(*args, *scatter)
    return outs[0], _keep_own_blocks(outs[1:], scatter)


def rms_fwd(x, w, *, name, tm=512):
    s, d = x.shape

    def body(x_ref, w_ref, h_ref, ht_ref):
        xv = x_ref[...]
        r = lax.rsqrt(jnp.mean(xv * xv, axis=-1, keepdims=True) + NORM_EPS)
        h = (xv * r) * w_ref[...]
        h_ref[...] = h.astype(BF16)
        ht_ref[...] = h.T.astype(BF16)

    return pl.pallas_call(
        body, name=name, grid=(s // tm,),
        in_specs=[pl.BlockSpec((tm, d), lambda i: (i, 0)), pl.BlockSpec((1, d), lambda i: (0, 0))],
        out_specs=[pl.BlockSpec((tm, d), lambda i: (i, 0)), pl.BlockSpec((d, tm), lambda i: (0, i))],
        out_shape=[jax.ShapeDtypeStruct((s, d), BF16), jax.ShapeDtypeStruct((d, s), BF16)],
        compiler_params=_params(("parallel",)),
    )(x, w)


def rms_bwd(dh, x, w, resid, *, name, tm=512):
    s, d = x.shape

    def body(dh_ref, x_ref, w_ref, res_ref, dx_ref, dw_ref):
        xv = x_ref[...]
        dhv = dh_ref[...]
        r = lax.rsqrt(jnp.mean(xv * xv, axis=-1, keepdims=True) + NORM_EPS)
        xh = xv * r
        g = dhv * w_ref[...]
        dx_ref[...] = res_ref[...] + r * (g - xh * jnp.mean(g * xh, axis=-1, keepdims=True))
        part = jnp.sum(dhv * xh, axis=0, keepdims=True)

        @pl.when(pl.program_id(0) == 0)
        def _():
            dw_ref[...] = part

        @pl.when(pl.program_id(0) > 0)
        def _():
            dw_ref[...] += part

    row = pl.BlockSpec((tm, d), lambda i: (i, 0))
    vec = pl.BlockSpec((1, d), lambda i: (0, 0))
    return pl.pallas_call(
        body, name=name, grid=(s // tm,), in_specs=[row, row, vec, row], out_specs=[row, vec],
        out_shape=[jax.ShapeDtypeStruct((s, d), F32), jax.ShapeDtypeStruct((1, d), F32)],
        compiler_params=_params(("arbitrary",)),
    )(dh, x, w, resid)


def loss_head(y, target, *, tm=512):
    s, d = y.shape

    def body(y_ref, t_ref, dy_ref, sq_ref):
        e = y_ref[...] - t_ref[...]
        dy_ref[...] = e / float(d)
        part = jnp.sum(e * e, axis=0, keepdims=True)

        @pl.when(pl.program_id(0) == 0)
        def _():
            sq_ref[...] = part

        @pl.when(pl.program_id(0) > 0)
        def _():
            sq_ref[...] += part

    row = pl.BlockSpec((tm, d), lambda i: (i, 0))
    vec = pl.BlockSpec((1, d), lambda i: (0, 0))
    return pl.pallas_call(
        body, name="loss_head", grid=(s // tm,), in_specs=[row, row], out_specs=[row, vec],
        out_shape=[jax.ShapeDtypeStruct((s, d), F32), jax.ShapeDtypeStruct((1, d), F32)],
        compiler_params=_params(("arbitrary",)),
    )(y, target)


def _row_shifts(ext, k_taps):
    return [_shift_down(ext, j) for j in range(k_taps)]


def _conv_rows(shifts, w):
    k_taps = len(shifts)
    acc = w[k_taps - 1:k_taps, :] * shifts[0]
    for k in range(k_taps - 1):
        acc = acc + w[k:k + 1, :] * shifts[k_taps - 1 - k]
    return acc


def _conv_weight_grad(dcur, shifts, rows, width):
    k_taps = len(shifts)
    out = [jnp.sum(dcur * shifts[k_taps - 1 - k][rows], axis=0, keepdims=True) for k in range(k_taps)]
    out.append(jnp.sum(dcur, axis=0, keepdims=True))
    return _stack_rows(out, width)


def _conv_rows_transposed(dext, w, k_taps):
    acc = w[k_taps - 1:k_taps, :] * dext
    for k in range(k_taps - 1):
        acc = acc + w[k:k + 1, :] * _shift_up(dext, k_taps - 1 - k)
    return acc


def _stack_rows(rows, width):
    ri = _row_iota((8, width))
    out = jnp.zeros((8, width), F32)
    for k, r in enumerate(rows):
        out = out + jnp.where(ri == k, r, 0.0)
    return out


def ffn_mid_fwd(hu, conv_w8, conv_b, *, tm=1024, tc=256):
    s = hu.shape[0]
    ncol = D_FF // tc
    r8 = tm // 8

    def body(g_ref, v_ref, gp_ref, vp_ref, wg_ref, wv_ref, bg_ref, bv_ref, o_ref, ot_ref):
        first = pl.program_id(1) == 0

        def conv(cur_ref, prev_ref, w_ref, b_ref):
            prev = jnp.where(first, 0.0, prev_ref[...])
            ext = jnp.concatenate([prev, cur_ref[...]], axis=0)
            return _conv_rows(_row_shifts(ext, FFN_CONV), w_ref[...])[8:] + b_ref[...]

        gc = conv(g_ref, gp_ref, wg_ref, bg_ref)
        vc = conv(v_ref, vp_ref, wv_ref, bv_ref)
        act = gc * _sigmoid(gc) * vc
        o_ref[...] = act.astype(BF16)
        ot_ref[...] = act.T.astype(BF16)

    def prev_idx(i):
        return jnp.maximum(i * r8 - 1, 0)

    in_specs = [
        pl.BlockSpec((tm, tc), lambda j, i: (i, j)),
        pl.BlockSpec((tm, tc), lambda j, i: (i, j + ncol)),
        pl.BlockSpec((8, tc), lambda j, i: (prev_idx(i), j)),
        pl.BlockSpec((8, tc), lambda j, i: (prev_idx(i), j + ncol)),
        pl.BlockSpec((8, tc), lambda j, i: (0, j)),
        pl.BlockSpec((8, tc), lambda j, i: (0, j + ncol)),
        pl.BlockSpec((1, tc), lambda j, i: (0, j)),
        pl.BlockSpec((1, tc), lambda j, i: (0, j + ncol)),
    ]
    return pl.pallas_call(
        body, name="ffn_mid_fwd", grid=(ncol, s // tm), in_specs=in_specs,
        out_specs=[pl.BlockSpec((tm, tc), lambda j, i: (i, j)), pl.BlockSpec((tc, tm), lambda j, i: (j, i))],
        out_shape=[jax.ShapeDtypeStruct((s, D_FF), BF16), jax.ShapeDtypeStruct((D_FF, s), BF16)],
        compiler_params=_params(("parallel", "parallel")),
    )(hu, hu, hu, hu, conv_w8, conv_w8, conv_b, conv_b)


def ffn_mid_bwd(hu, dact, conv_w8, conv_b, *, tm=1024, tc=256):
    s = hu.shape[0]
    ncol = D_FF // tc
    nrow = s // tm
    r8 = tm // 8

    def body(g_ref, v_ref, gp_ref, vp_ref, gn_ref, vn_ref, da_ref, dan_ref, wg_ref, wv_ref, bg_ref, bv_ref,
             dhu_ref, wgo_ref, wvo_ref):
        i = pl.program_id(1)
        first = i == 0
        last = i == nrow - 1

        def ext_of(cur_ref, prev_ref, next_ref):
            prev = jnp.where(first, 0.0, prev_ref[...])
            return jnp.concatenate([prev, cur_ref[...], next_ref[...]], axis=0)

        g_sh = _row_shifts(ext_of(g_ref, gp_ref, gn_ref), FFN_CONV)
        v_sh = _row_shifts(ext_of(v_ref, vp_ref, vn_ref), FFN_CONV)
        gc = _conv_rows(g_sh, wg_ref[...]) + bg_ref[...]
        vc = _conv_rows(v_sh, wv_ref[...]) + bv_ref[...]
        da_ext = jnp.concatenate([jnp.zeros((8, tc), F32), da_ref[...], jnp.where(last, 0.0, dan_ref[...])], axis=0)
        silu, dsilu = _silu_and_grad(gc)
        dgc = da_ext * vc * dsilu
        dvc = da_ext * silu
        dhu_ref[0] = _conv_rows_transposed(dgc, wg_ref[...], FFN_CONV)[8:8 + tm].astype(BF16)
        dhu_ref[1] = _conv_rows_transposed(dvc, wv_ref[...], FFN_CONV)[8:8 + tm].astype(BF16)

        cur = slice(8, 8 + tm)
        pg = _conv_weight_grad(dgc[cur], g_sh, cur, tc)
        pv = _conv_weight_grad(dvc[cur], v_sh, cur, tc)

        @pl.when(first)
        def _():
            wgo_ref[...] = pg
            wvo_ref[...] = pv

        @pl.when(i > 0)
        def _():
            wgo_ref[...] += pg
            wvo_ref[...] += pv

    def prev_idx(i):
        return jnp.maximum(i * r8 - 1, 0)

    def next_idx(i):
        return jnp.minimum((i + 1) * r8, s // 8 - 1)

    cur_g = pl.BlockSpec((tm, tc), lambda j, i: (i, j))
    cur_v = pl.BlockSpec((tm, tc), lambda j, i: (i, j + ncol))
    in_specs = [
        cur_g, cur_v,
        pl.BlockSpec((8, tc), lambda j, i: (prev_idx(i), j)),
        pl.BlockSpec((8, tc), lambda j, i: (prev_idx(i), j + ncol)),
        pl.BlockSpec((8, tc), lambda j, i: (next_idx(i), j)),
        pl.BlockSpec((8, tc), lambda j, i: (next_idx(i), j + ncol)),
        cur_g,
        pl.BlockSpec((8, tc), lambda j, i: (next_idx(i), j)),
        pl.BlockSpec((8, tc), lambda j, i: (0, j)),
        pl.BlockSpec((8, tc), lambda j, i: (0, j + ncol)),
        pl.BlockSpec((1, tc), lambda j, i: (0, j)),
        pl.BlockSpec((1, tc), lambda j, i: (0, j + ncol)),
    ]
    out_specs = [pl.BlockSpec((2, tm, tc), lambda j, i: (0, i, j)), pl.BlockSpec((8, tc), lambda j, i: (0, j)),
                 pl.BlockSpec((8, tc), lambda j, i: (0, j))]
    out_shape = [jax.ShapeDtypeStruct((2, s, D_FF), BF16),
                 jax.ShapeDtypeStruct((8, D_FF), F32), jax.ShapeDtypeStruct((8, D_FF), F32)]
    return pl.pallas_call(
        body, name="ffn_mid_bwd", grid=(ncol, nrow), in_specs=in_specs, out_specs=out_specs, out_shape=out_shape,
        compiler_params=_params(("parallel", "arbitrary")),
    )(hu, hu, hu, hu, hu, hu, dact, dact, conv_w8, conv_w8, conv_b, conv_b)


def _softplus(x):
    return jnp.maximum(x, 0.0) + jnp.log(1.0 + jnp.exp(-jnp.abs(x)))


def _cumsum_rows(v):
    n = v.shape[0]
    ri = _row_iota(v.shape)
    sh = 1
    while sh < n:
        v = v + jnp.where(ri >= sh, _shift_down(v, sh), 0.0)
        sh *= 2
    return v


def _rev_cumsum_rows(v):
    n = v.shape[0]
    ri = _row_iota(v.shape)
    sh = 1
    while sh < n:
        v = v + jnp.where(ri < n - sh, _shift_up(v, sh), 0.0)
        sh *= 2
    return v


def _total(v):
    return jnp.sum(jnp.sum(v, axis=1, keepdims=True), axis=0, keepdims=True)


def _ssd_in_specs(rev_nc=None):
    def ch(c):
        return c if rev_nc is None else rev_nc - 1 - c

    def prev(c):
        return jnp.maximum(ch(c) * (SSD_CHUNK // 8) - 1, 0)

    L = SSD_CHUNK
    return [
        pl.BlockSpec((L, 1024), lambda c: (ch(c), 0)),
        pl.BlockSpec((L, 1024), lambda c: (ch(c), 1)),
        pl.BlockSpec((L, 256), lambda c: (ch(c), 20)),
        pl.BlockSpec((L, 256), lambda c: (ch(c), 21)),
        pl.BlockSpec((8, 1024), lambda c: (prev(c), 1)),
        pl.BlockSpec((8, 256), lambda c: (prev(c), 20)),
        pl.BlockSpec((8, 256), lambda c: (prev(c), 21)),
        pl.BlockSpec((8, 1024), lambda c: (0, 0)),
        pl.BlockSpec((8, 256), lambda c: (0, 4)),
        pl.BlockSpec((8, 256), lambda c: (0, 5)),
        pl.BlockSpec((1, 1024), lambda c: (0, 0)),
        pl.BlockSpec((1, 256), lambda c: (0, 4)),
        pl.BlockSpec((1, 256), lambda c: (0, 5)),
        pl.BlockSpec((L, SMALL_COLS), lambda c: (ch(c), 0)),
        pl.BlockSpec((8, 128), lambda c: (0, 0)),
        pl.BlockSpec((1, 1024), lambda c: (0, 0)),
    ]


def _ssd_conv_pre(cur_ref, prev_ref, w_ref, b_ref, first):
    prev = jnp.where(first, 0.0, prev_ref[...])
    shifts = _row_shifts(jnp.concatenate([prev, cur_ref[...]], axis=0), SSD_CONV)
    return shifts, _conv_rows(shifts, w_ref[...])[8:] + b_ref[...]


def _ssd_time_consts(small_ref, sp_ref):
    dt_pre = small_ref[...] + sp_ref[0:1, :]
    dt = _softplus(dt_pre)
    a = -jnp.exp(sp_ref[1:2, :])
    acs = _cumsum_rows(dt * a)
    return dt_pre, dt, a, acs


def ssd_fwd(proj, small, conv_w8, conv_b, smallp, norm_w):
    s = proj.shape[0]
    nc = s // SSD_CHUNK
    L = SSD_CHUNK

    def body(z_ref, xs_ref, b_ref, c_ref, xsp_ref, bp_ref, cp_ref, wx_ref, wb_ref, wc_ref, bx_ref, bb_ref, bc_ref,
             small_ref, sp_ref, nw_ref, y_ref, yt_ref, ypre_ref, st_ref, state):
        first = pl.program_id(0) == 0

        @pl.when(first)
        def _():
            state[...] = jnp.zeros_like(state)

        xs = _ssd_conv_pre(xs_ref, xsp_ref, wx_ref, bx_ref, first)[1]
        xs = xs * _sigmoid(xs)
        bm = _ssd_conv_pre(b_ref, bp_ref, wb_ref, bb_ref, first)[1]
        bm = bm * _sigmoid(bm)
        cm = _ssd_conv_pre(c_ref, cp_ref, wc_ref, bc_ref, first)[1]
        cm = cm * _sigmoid(cm)
        _, dt, _, acs = _ssd_time_consts(small_ref, sp_ref)
        acs_t = acs.T
        li = _lane_iota((L, L))
        ri = _row_iota((L, L))
        tri = ri >= li
        lo = li < HEAD_DIM
        st_ref[0] = state[...]
        for g in range(2):
            bg = bm[:, 128 * g:128 * g + 128]
            cg = cm[:, 128 * g:128 * g + 128]
            gmat = _dot(cg, bg, "nt")
            for pp in range(4):
                p = 4 * g + pp
                h0, h1 = 2 * p, 2 * p + 1
                x = xs[:, 128 * p:128 * p + 128]
                a0, a1 = acs[:, h0:h0 + 1], acs[:, h1:h1 + 1]
                xdt = x * jnp.where(lo, dt[:, h0:h0 + 1], dt[:, h1:h1 + 1])
                m0 = gmat * jnp.exp(jnp.where(tri, a0 - acs_t[h0:h0 + 1, :], NEG_BIG))
                m1 = gmat * jnp.exp(jnp.where(tri, a1 - acs_t[h1:h1 + 1, :], NEG_BIG))
                yd = _dot(m0, jnp.where(lo, xdt, 0.0)) + _dot(m1, jnp.where(lo, 0.0, xdt))
                hin = state[p]
                yo = _dot(cg, hin, "nt") * jnp.exp(jnp.where(lo, a0, a1))
                dskip = jnp.where(lo[0:1], sp_ref[2:3, h0:h0 + 1], sp_ref[2:3, h1:h1 + 1])
                ypre_ref[:, 128 * p:128 * p + 128] = yd + yo + dskip * x
                al0, al1 = acs[L - 1:L, h0:h0 + 1], acs[L - 1:L, h1:h1 + 1]
                w = jnp.exp(jnp.where(lo, al0 - a0, al1 - a1))
                dec = jnp.exp(jnp.where(ri < HEAD_DIM, al0, al1))
                state[p] = dec * hin + _dot(xdt * w, bg, "tn")
        z = z_ref[...]
        yg = ypre_ref[...] * (z * _sigmoid(z))
        for g in range(2):
            seg = yg[:, 512 * g:512 * g + 512]
            r = lax.rsqrt(jnp.mean(seg * seg, axis=-1, keepdims=True) + NORM_EPS)
            out = (seg * r) * nw_ref[:, 512 * g:512 * g + 512]
            y_ref[:, 512 * g:512 * g + 512] = out.astype(BF16)
            yt_ref[512 * g:512 * g + 512, :] = out.T.astype(BF16)

    row = pl.BlockSpec((L, 1024), lambda c: (c, 0))
    return pl.pallas_call(
        body, name="ssd_fwd", grid=(nc,), in_specs=_ssd_in_specs(),
        out_specs=[row, pl.BlockSpec((1024, L), lambda c: (0, c)), row,
                   pl.BlockSpec((1, N_PAIRS, 128, 128), lambda c: (c, 0, 0, 0))],
        out_shape=[jax.ShapeDtypeStruct((s, 1024), BF16), jax.ShapeDtypeStruct((1024, s), BF16),
                   jax.ShapeDtypeStruct((s, 1024), F32), jax.ShapeDtypeStruct((nc, N_PAIRS, 128, 128), F32)],
        scratch_shapes=[pltpu.VMEM((N_PAIRS, 128, 128), F32)],
        compiler_params=_params(("arbitrary",)),
    )(proj, proj, proj, proj, proj, proj, proj, conv_w8, conv_w8, conv_w8, conv_b, conv_b, conv_b, small, smallp, norm_w)


def ssd_bwd(proj, small, conv_w8, conv_b, smallp, norm_w, ypre, states, dy, sel):
    s = proj.shape[0]
    nc = s // SSD_CHUNK
    L = SSD_CHUNK

    def body(z_ref, xs_ref, b_ref, c_ref, xsp_ref, bp_ref, cp_ref, wx_ref, wb_ref, wc_ref, bx_ref, bb_ref, bc_ref,
             small_ref, sp_ref, nw_ref, ypre_ref, st_ref, dy_ref, sel_ref,
             dz_ref, dxs_ref, db_ref, dc_ref, dsmall_ref, gwx_ref, gwb_ref, gwc_ref, gsp_ref, gnw_ref,
             dstate, carry_x, carry_b, carry_c, dxs_buf, dbm_buf, dcm_buf, qcs, col_sums, acs_terms, dt_terms):
        step = pl.program_id(0)
        col_sums[...] = jnp.zeros_like(col_sums)
        first_chunk = step == nc - 1
        start = step == 0

        @pl.when(start)
        def _():
            dstate[...] = jnp.zeros_like(dstate)
            carry_x[...] = jnp.zeros_like(carry_x)
            carry_b[...] = jnp.zeros_like(carry_b)
            carry_c[...] = jnp.zeros_like(carry_c)

        xs_ext, xs_pre = _ssd_conv_pre(xs_ref, xsp_ref, wx_ref, bx_ref, first_chunk)
        b_ext, b_pre = _ssd_conv_pre(b_ref, bp_ref, wb_ref, bb_ref, first_chunk)
        c_ext, c_pre = _ssd_conv_pre(c_ref, cp_ref, wc_ref, bc_ref, first_chunk)
        xs, xs_ds = _silu_and_grad(xs_pre)
        bm, b_ds = _silu_and_grad(b_pre)
        cm, c_ds = _silu_and_grad(c_pre)
        dt_pre, dt, a, acs = _ssd_time_consts(small_ref, sp_ref)
        acs_t = acs.T
        li = _lane_iota((L, L))
        ri = _row_iota((L, L))
        tri = ri >= li
        lo = li < HEAD_DIM
        lo_rows = ri < HEAD_DIM
        li1 = _lane_iota((1, L))

        z = z_ref[...]
        sz, dsz = _silu_and_grad(z)
        y = ypre_ref[...]
        yg = y * sz
        dout = dy_ref[...]
        dyg_parts = []
        gnw_parts = []
        for g in range(2):
            sl = slice(512 * g, 512 * g + 512)
            seg = yg[:, sl]
            r = lax.rsqrt(jnp.mean(seg * seg, axis=-1, keepdims=True) + NORM_EPS)
            n = seg * r
            gnw_parts.append(jnp.sum(dout[:, sl] * n, axis=0, keepdims=True))
            gg = dout[:, sl] * nw_ref[:, sl]
            dyg_parts.append(r * (gg - n * jnp.mean(gg * n, axis=-1, keepdims=True)))
        dyg = jnp.concatenate(dyg_parts, axis=1)
        gnw = jnp.concatenate(gnw_parts, axis=1)
        dz_ref[...] = (dyg * y * dsz).astype(BF16)
        dypre = dyg * sz

        qcs[...] = jnp.zeros_like(qcs)
        dalast = jnp.zeros((1, L), F32)
        for g in range(2):
            bg = bm[:, 128 * g:128 * g + 128]
            cg = cm[:, 128 * g:128 * g + 128]
            gmat = _dot(cg, bg, "nt")
            dgmat = jnp.zeros((L, L), F32)
            dbg = jnp.zeros((L, L), F32)
            dcg = jnp.zeros((L, L), F32)
            for pp in range(4):
                p = 4 * g + pp
                h0, h1 = 2 * p, 2 * p + 1
                lanes = slice(128 * p, 128 * p + 128)
                x = xs[:, lanes]
                dyp = dypre[:, lanes]
                a0, a1 = acs[:, h0:h0 + 1], acs[:, h1:h1 + 1]
                dtl = jnp.where(lo, dt[:, h0:h0 + 1], dt[:, h1:h1 + 1])
                xdt = x * dtl
                l0 = jnp.exp(jnp.where(tri, a0 - acs_t[h0:h0 + 1, :], NEG_BIG))
                l1 = jnp.exp(jnp.where(tri, a1 - acs_t[h1:h1 + 1, :], NEG_BIG))
                m0, m1 = gmat * l0, gmat * l1
                dskip = jnp.where(lo[0:1], sp_ref[2:3, h0:h0 + 1], sp_ref[2:3, h1:h1 + 1])
                col_sums[0:1, lanes] = jnp.sum(dyp * x, axis=0, keepdims=True)
                dx = dyp * dskip
                dy0, dy1 = jnp.where(lo, dyp, 0.0), jnp.where(lo, 0.0, dyp)
                x0, x1 = jnp.where(lo, xdt, 0.0), jnp.where(lo, 0.0, xdt)
                dm0, dm1 = _dot(dy0, x0, "nt"), _dot(dy1, x1, "nt")
                dxdt = _dot(m0, dy0, "tn") + _dot(m1, dy1, "tn")
                q0, q1 = dm0 * m0, dm1 * m1
                qcs[h0:h0 + 1, :] = jnp.sum(q0, axis=0, keepdims=True)
                qcs[h1:h1 + 1, :] = jnp.sum(q1, axis=0, keepdims=True)
                row_terms = jnp.where(lo, q0 + pltpu.roll(q0, HEAD_DIM, 1), q1 + pltpu.roll(q1, HEAD_DIM, 1))
                dgmat = dgmat + dm0 * l0 + dm1 * l1
                hin = st_ref[0, p]
                e = jnp.exp(jnp.where(lo, a0, a1))
                ch = _dot(cg, hin, "nt")
                dch = dyp * e
                dcg = dcg + _dot(dch, hin)
                dhin = _dot(dch, cg, "tn")
                dhout = dstate[p]
                al0, al1 = acs[L - 1:L, h0:h0 + 1], acs[L - 1:L, h1:h1 + 1]
                dec = jnp.exp(jnp.where(lo_rows, al0, al1))
                dhin = dhin + dec * dhout
                dal = dhout * hin * dec
                dal0 = _total(jnp.where(lo_rows, dal, 0.0))
                dal1 = _total(dal) - dal0
                dalast = dalast + jnp.where(li1 == h0, dal0, 0.0) + jnp.where(li1 == h1, dal1, 0.0)
                w = jnp.exp(jnp.where(lo, al0 - a0, al1 - a1))
                xw = xdt * w
                dxw = _dot(bg, dhout, "nt")
                dbg = dbg + _dot(xw, dhout)
                dxdt = dxdt + dxw * w
                dww = dxw * xw
                col_sums[1:2, lanes] = jnp.sum(dww, axis=0, keepdims=True)
                acs_terms[:, lanes] = row_terms + dch * ch - dww
                dx = dx + dxdt * dtl
                dt_terms[:, lanes] = dxdt * x
                dxs_buf[:, lanes] = dx
                dstate[p] = dhin
            dcg = dcg + _dot(dgmat, bg)
            dbg = dbg + _dot(dgmat, cg, "tn")
            dbm_buf[:, 128 * g:128 * g + 128] = dbg
            dcm_buf[:, 128 * g:128 * g + 128] = dcg

        head_sums = _split3_dot(col_sums[...], sel_ref[...])
        dskip_g = head_sums[0:1, :]
        dalast = dalast + head_sums[1:2, :]
        ddt = _split3_dot(dt_terms[...], sel_ref[...])
        dacs_tot = _split3_dot(acs_terms[...], sel_ref[...]) - qcs[...].T + jnp.where(ri == L - 1, dalast, 0.0)
        dstep = _rev_cumsum_rows(dacs_tot)
        ddt = ddt + dstep * a
        head_lane = li < N_HEADS
        ddt_pre = jnp.where(head_lane, ddt * _sigmoid(dt_pre), 0.0)
        dsmall_ref[...] = ddt_pre
        da = jnp.sum(jnp.where(head_lane, dstep * dt, 0.0), axis=0, keepdims=True)
        gsp = _stack_rows([jnp.sum(ddt_pre, axis=0, keepdims=True), da * a, dskip_g], L)

        def conv_back(dpost, ds, shifts, w_ref, carry, out_ref, width):
            dpre = dpost * ds
            dext = jnp.concatenate([dpre, carry[...]], axis=0)
            out_ref[...] = _conv_rows_transposed(dext, w_ref[...], SSD_CONV)[:L].astype(BF16)
            carry[...] = dpre[0:8]
            return _conv_weight_grad(dpre, shifts, slice(8, 8 + L), width)

        gwx = conv_back(dxs_buf[...], xs_ds, xs_ext, wx_ref, carry_x, dxs_ref, 1024)
        gwb = conv_back(dbm_buf[...], b_ds, b_ext, wb_ref, carry_b, db_ref, 256)
        gwc = conv_back(dcm_buf[...], c_ds, c_ext, wc_ref, carry_c, dc_ref, 256)

        @pl.when(start)
        def _():
            gwx_ref[...] = gwx
            gwb_ref[...] = gwb
            gwc_ref[...] = gwc
            gsp_ref[...] = gsp
            gnw_ref[...] = gnw

        @pl.when(step > 0)
        def _():
            gwx_ref[...] += gwx
            gwb_ref[...] += gwb
            gwc_ref[...] += gwc
            gsp_ref[...] += gsp
            gnw_ref[...] += gnw

    def ch(c):
        return nc - 1 - c

    row = pl.BlockSpec((L, 1024), lambda c: (ch(c), 0))
    row256 = pl.BlockSpec((L, 256), lambda c: (ch(c), 0))
    in_specs = _ssd_in_specs(rev_nc=nc) + [row, pl.BlockSpec((1, N_PAIRS, 128, 128), lambda c: (ch(c), 0, 0, 0)), row,
                                           pl.BlockSpec((1024, 128), lambda c: (0, 0))]
    out_specs = [row, row, row256, row256, pl.BlockSpec((L, 128), lambda c: (ch(c), 0)),
                 pl.BlockSpec((8, 1024), lambda c: (0, 0)), pl.BlockSpec((8, 256), lambda c: (0, 0)),
                 pl.BlockSpec((8, 256), lambda c: (0, 0)), pl.BlockSpec((8, 128), lambda c: (0, 0)),
                 pl.BlockSpec((1, 1024), lambda c: (0, 0))]
    out_shape = [jax.ShapeDtypeStruct((s, 1024), BF16), jax.ShapeDtypeStruct((s, 1024), BF16),
                 jax.ShapeDtypeStruct((s, 256), BF16), jax.ShapeDtypeStruct((s, 256), BF16),
                 jax.ShapeDtypeStruct((s, 128), F32),
                 jax.ShapeDtypeStruct((8, 1024), F32), jax.ShapeDtypeStruct((8, 256), F32),
                 jax.ShapeDtypeStruct((8, 256), F32), jax.ShapeDtypeStruct((8, 128), F32),
                 jax.ShapeDtypeStruct((1, 1024), F32)]
    scratch = [pltpu.VMEM((N_PAIRS, 128, 128), F32), pltpu.VMEM((8, 1024), F32), pltpu.VMEM((8, 256), F32),
               pltpu.VMEM((8, 256), F32), pltpu.VMEM((L, 1024), F32), pltpu.VMEM((L, 256), F32), pltpu.VMEM((L, 256), F32),
               pltpu.VMEM((L, L), F32), pltpu.VMEM((8, 1024), F32), pltpu.VMEM((L, 1024), F32), pltpu.VMEM((L, 1024), F32)]
    return pl.pallas_call(
        body, name="ssd_bwd", grid=(nc,), in_specs=in_specs, out_specs=out_specs, out_shape=out_shape,
        scratch_shapes=scratch, compiler_params=_params(("arbitrary",)),
    )(proj, proj, proj, proj, proj, proj, proj, conv_w8, conv_w8, conv_w8, conv_b, conv_b, conv_b, small, smallp, norm_w,
      ypre, states, dy, sel)


FOX_SCALE = HEAD_DIM ** -0.5
FOX_T = 256
Q_COL, K_COL, V_COL = 2, 3, 4


def _split_dot(v, m, terms):
    out, rest = None, v
    for i in range(terms):
        piece = rest.astype(BF16)
        out = _dot(piece, m) if out is None else out + _dot(piece, m)
        if i + 1 < terms:
            rest = rest - piece.astype(F32)
    return out


def _split3_dot(v, m):
    return _split_dot(v, m, 3)


def _head_mean(x, sel_ref, selt_ref):
    return _dot(x, sel_ref[...]) * (1.0 / HEAD_DIM)


def _head_spread(v, selt_ref):
    return _split_dot(v, selt_ref[...], 2)


def _head_rstd(x, sel_ref, selt_ref):
    return _head_spread(lax.rsqrt(_head_mean(x * x, sel_ref, selt_ref) + NORM_EPS), selt_ref)


def fox_tables():
    r = np.arange(3 * 128)
    piece, lane = r // 128, r % 128
    head = lane - F_LANE
    is_head = np.logical_and(head >= 0, head < N_HEADS)
    col = 128 * (head // 2) + HEAD_DIM * (1 - head % 2) + piece
    cols = np.arange(1024)
    place_q = np.logical_and(is_head[:, None], cols[None, :] == col[:, None])
    place_k = np.logical_and(is_head[:, None], cols[None, :] == (col + 3)[:, None])
    ones_q = np.logical_and(cols % HEAD_DIM >= 3, cols % HEAD_DIM < 6)[None]
    ones_k = (cols % HEAD_DIM < 3)[None]
    h = np.arange(128) - F_LANE
    ok = np.logical_and(h >= 0, h < N_HEADS)
    same_pair = cols[:, None] // 128 == (h // 2)[None, :]
    fold_even = np.logical_and(np.logical_and(ok, h % 2 == 0)[None, :], same_pair)
    fold_odd = np.logical_and(np.logical_and(ok, h % 2 == 1)[None, :], same_pair)
    as_bf16 = lambda t: jnp.asarray(t.astype(np.float32), BF16)
    return (as_bf16(place_q), as_bf16(place_k), jnp.asarray(ones_q, F32), jnp.asarray(ones_k, F32),
            as_bf16(fold_even), as_bf16(fold_odd))


def fox_prep(proj, small, smallp, qw, kw, sel, selt, place_q, place_k, ones_q, ones_k, *, tm=256):
    s = proj.shape[0]

    def body(q_ref, k_ref, v_ref, small_ref, sp_ref, qw_ref, kw_ref, sel_ref, selt_ref, pq_ref, pk_ref, oq_ref, ok_ref,
             qn_ref, kn_ref, aq_ref, ak_ref, vb_ref, knt_ref, akt_ref, vt_ref, carry):
        @pl.when(pl.program_id(0) == 0)
        def _():
            carry[...] = jnp.zeros_like(carry)

        q = q_ref[...]
        qn_ref[...] = (((q * _head_rstd(q, sel_ref, selt_ref)) * qw_ref[...]) * FOX_SCALE).astype(BF16)
        k = k_ref[...]
        kn = ((k * _head_rstd(k, sel_ref, selt_ref)) * kw_ref[...]).astype(BF16)
        kn_ref[...] = kn
        knt_ref[...] = kn.astype(F32).T.astype(BF16)
        vb_ref[...] = v_ref[...].astype(BF16)
        vt_ref[...] = v_ref[...].T.astype(BF16)
        li = _lane_iota((tm, 128))
        f_lane = jnp.logical_and(li >= F_LANE, li < F_LANE + N_HEADS)
        logf = jnp.where(f_lane, -_softplus(-(small_ref[...] + sp_ref[3:4, :])), 0.0)
        cum = _cumsum_rows(logf) + carry[...]
        carry[...] = cum[tm - 1:tm, :]
        hi = cum.astype(BF16)
        r1 = cum - hi.astype(F32)
        mid = r1.astype(BF16)
        lo = (r1 - mid.astype(F32)).astype(BF16)
        pieces = jnp.concatenate([hi, mid, lo], axis=1)
        aq_ref[...] = (_dot(pieces, pq_ref[...]) + oq_ref[...]).astype(BF16)
        ak = ok_ref[...] - _dot(pieces, pk_ref[...])
        ak_ref[...] = ak.astype(BF16)
        akt_ref[...] = ak.T.astype(BF16)

    row = pl.BlockSpec((tm, 1024), lambda i: (i, 0))
    col = pl.BlockSpec((1024, tm), lambda i: (0, i))
    vec = pl.BlockSpec((1, 1024), lambda i: (0, 0))
    table = pl.BlockSpec((384, 1024), lambda i: (0, 0))
    wide = jax.ShapeDtypeStruct((s, 1024), BF16)
    tall = jax.ShapeDtypeStruct((1024, s), BF16)
    return pl.pallas_call(
        body, name="fox_prep", grid=(s // tm,),
        in_specs=[pl.BlockSpec((tm, 1024), lambda i: (i, Q_COL)), pl.BlockSpec((tm, 1024), lambda i: (i, K_COL)),
                  pl.BlockSpec((tm, 1024), lambda i: (i, V_COL)),
                  pl.BlockSpec((tm, 128), lambda i: (i, 0)), pl.BlockSpec((8, 128), lambda i: (0, 0)), vec, vec,
                  pl.BlockSpec((1024, 128), lambda i: (0, 0)), pl.BlockSpec((128, 1024), lambda i: (0, 0)),
                  table, table, vec, vec],
        out_specs=[row, row, row, row, row, col, col, col],
        out_shape=[wide, wide, wide, wide, wide, tall, tall, tall],
        scratch_shapes=[pltpu.VMEM((1, 128), F32)], compiler_params=_params(("arbitrary",)),
    )(proj, proj, proj, small, smallp, qw, kw, sel, selt, place_q, place_k, ones_q, ones_k)


def fox_fwd(qn, kn, aq, ak, vt, shards=()):
    s = qn.shape[0]
    t = FOX_T
    nq = s // t
    ng = len(shards)

    def body(*refs):
        q_ref, k_ref, aq_ref, ak_ref, vt_ref = refs[:5]
        o_ref, ot_ref, lse_ref = refs[5 + ng:8 + ng]
        p = pl.program_id(0)
        if ng:
            start, forward, finish = _gather_phases(refs[5:5 + ng], refs[8 + ng:8 + 2 * ng], *refs[8 + 2 * ng:])
            pl.when(p == 0)(start)
            pl.when(p == N_PAIRS // 2)(forward)

        @pl.when(p == 0)
        def _():
            lse_ref[...] = jnp.zeros_like(lse_ref)

        lo = _lane_iota((t, 128)) < HEAD_DIM
        lo_rows = _row_iota((128, t)) < HEAD_DIM
        causal_t = _lane_iota((t, t)) >= _row_iota((t, t))

        def q_loop(qi, _):
            q0 = pl.multiple_of(qi * t, t)
            qv, aqv = q_ref[pl.ds(q0, t), :], aq_ref[pl.ds(q0, t), :]
            qa, qb = jnp.where(lo, qv, aqv), jnp.where(lo, aqv, qv)

            def scores(kj):
                k0 = pl.multiple_of(kj * t, t)
                kv, akv = k_ref[pl.ds(k0, t), :], ak_ref[pl.ds(k0, t), :]
                return _dot(jnp.where(lo, kv, akv), qa, "nt"), _dot(jnp.where(lo, akv, kv), qb, "nt")

            def update(kj, stats, s0, s1):
                m0, l0, m1, l1, acc = stats
                vtv = vt_ref[:, pl.ds(pl.multiple_of(kj * t, t), t)]
                n0 = jnp.maximum(m0, jnp.max(s0, axis=0, keepdims=True))
                n1 = jnp.maximum(m1, jnp.max(s1, axis=0, keepdims=True))
                a0, a1 = jnp.exp(m0 - n0), jnp.exp(m1 - n1)
                p0, p1 = jnp.exp(s0 - n0), jnp.exp(s1 - n1)
                l0 = a0 * l0 + jnp.sum(p0, axis=0, keepdims=True)
                l1 = a1 * l1 + jnp.sum(p1, axis=0, keepdims=True)
                acc = (jnp.where(lo_rows, a0, a1) * acc + _dot(jnp.where(lo_rows, vtv, 0.0), p0)
                       + _dot(jnp.where(lo_rows, 0.0, vtv), p1))
                return n0, l0, n1, l1, acc

            def step(kj, carry):
                stats, (s0, s1) = carry[:5], carry[5:]
                nxt = scores(kj + 1)
                return (*update(kj, stats, s0, s1), *nxt)

            def row(val):
                return jnp.full((1, t), val, F32)

            init = (row(NEG_BIG), row(0.0), row(NEG_BIG), row(0.0), jnp.zeros((128, t), F32), *scores(0))
            carry = lax.fori_loop(0, qi, step, init)
            s0, s1 = jnp.where(causal_t, carry[5], NEG_BIG), jnp.where(causal_t, carry[6], NEG_BIG)
            m0, l0, m1, l1, acc = update(qi, carry[:5], s0, s1)
            out_t = acc / jnp.where(lo_rows, l0, l1)
            ot_ref[:, pl.ds(q0, t)] = out_t.astype(BF16)
            o_ref[pl.ds(q0, t), :] = out_t.T.astype(BF16)
            ri = _row_iota((N_HEADS, t))
            old = lse_ref[:, pl.ds(q0, t)]
            lse_ref[:, pl.ds(q0, t)] = jnp.where(
                ri == 2 * p, m0 + jnp.log(l0), jnp.where(ri == 2 * p + 1, m1 + jnp.log(l1), old))
            return 0

        lax.fori_loop(0, nq, q_loop, 0)
        if ng:
            pl.when(p == N_PAIRS - 1)(finish)

    pair = pl.BlockSpec((s, 128), lambda p: (0, p))
    outs = pl.pallas_call(
        body, name="fox_fwd", grid=(N_PAIRS,),
        in_specs=[pair] * 4 + [pl.BlockSpec((128, s), lambda p: (p, 0))] + [ANY] * ng,
        out_specs=[pair, pl.BlockSpec((128, s), lambda p: (p, 0)), pl.BlockSpec((N_HEADS, s), lambda p: (0, 0))] + [ANY] * ng,
        out_shape=[jax.ShapeDtypeStruct((s, 1024), BF16), jax.ShapeDtypeStruct((1024, s), BF16),
                   jax.ShapeDtypeStruct((N_HEADS, s), F32)] + _gather_out_shapes(shards),
        scratch_shapes=_gather_scratch(ng) if ng else [],
        compiler_params=_params(("arbitrary",)),
    )(qn, kn, aq, ak, vt, *shards)
    return outs[0], outs[1], outs[2], list(outs[3:])


def fox_bwd(qn, kn, aq, ak, knt, akt, vb, lse, dmixed, parts=()):
    s = qn.shape[0]
    t = FOX_T
    nq = s // t
    once = pl.Buffered(1)
    ns = len(parts)

    def body(*refs):
        q_ref, k_ref, aq_ref, ak_ref, kt_ref, akt_ref, v_ref, lse_ref, do_ref = refs[:9]
        dq_ref, dk_ref, dv_ref, dc0_ref, dc1_ref = refs[9 + ns:14 + ns]
        p_scr, dp_scr = refs[14 + 2 * ns:16 + 2 * ns]
        p = pl.program_id(0)
        if ns:
            start, finish = _scatter_phases(refs[9:9 + ns], refs[14 + ns:14 + 2 * ns], *refs[16 + 2 * ns:])
            pl.when(p == 0)(start)
        dk_ref[...] = jnp.zeros_like(dk_ref)
        dv_ref[...] = jnp.zeros_like(dv_ref)
        dc0_ref[...] = jnp.zeros_like(dc0_ref)
        dc1_ref[...] = jnp.zeros_like(dc1_ref)
        lo = _lane_iota((t, 128)) < HEAD_DIM
        lo_rows = _row_iota((128, t)) < HEAD_DIM
        causal_t = _lane_iota((t, t)) >= _row_iota((t, t))

        def q_loop(qi, _):
            q0 = pl.multiple_of(qi * t, t)
            qv, aqv = q_ref[pl.ds(q0, t), :], aq_ref[pl.ds(q0, t), :]
            qa, qb = jnp.where(lo, qv, aqv), jnp.where(lo, aqv, qv)
            do = do_ref[pl.ds(q0, t), :]
            doa, dob = jnp.where(lo, do, 0.0).astype(BF16), jnp.where(lo, 0.0, do).astype(BF16)
            lse_blk = lse_ref[:, pl.ds(q0, t)]
            ri = _row_iota((N_HEADS, t))
            lse0 = jnp.sum(jnp.where(ri == 2 * p, lse_blk, 0.0), axis=0, keepdims=True)
            lse1 = jnp.sum(jnp.where(ri == 2 * p + 1, lse_blk, 0.0), axis=0, keepdims=True)

            def scores(kj):
                k0 = pl.multiple_of(kj * t, t)
                kv, akv = k_ref[pl.ds(k0, t), :], ak_ref[pl.ds(k0, t), :]
                return _dot(jnp.where(lo, kv, akv), qa, "nt"), _dot(jnp.where(lo, akv, kv), qb, "nt")

            def pass1(kj, d0, d1, diagonal):
                k0 = pl.multiple_of(kj * t, t)
                vv = v_ref[pl.ds(k0, t), :]
                s0, s1 = scores(kj)
                if diagonal:
                    s0, s1 = jnp.where(causal_t, s0, NEG_BIG), jnp.where(causal_t, s1, NEG_BIG)
                p0, p1 = jnp.exp(s0 - lse0), jnp.exp(s1 - lse1)
                dp0, dp1 = _dot(vv, doa, "nt"), _dot(vv, dob, "nt")
                p_scr[0, kj], p_scr[1, kj] = p0, p1
                dp_scr[0, kj], dp_scr[1, kj] = dp0, dp1
                dv_ref[pl.ds(k0, t), :] += _dot(p0, doa) + _dot(p1, dob)
                return d0 + jnp.sum(p0 * dp0, axis=0, keepdims=True), d1 + jnp.sum(p1 * dp1, axis=0, keepdims=True)

            zero = jnp.zeros((1, t), F32)
            d0, d1 = lax.fori_loop(0, qi, lambda kj, c: pass1(kj, *c, False), (zero, zero))
            d0, d1 = pass1(qi, d0, d1, True)

            def pass2(kj, carry):
                dq0, dq1 = carry
                k0 = pl.multiple_of(kj * t, t)
                p0, p1 = p_scr[0, kj], p_scr[1, kj]
                ds0, ds1 = p0 * (dp_scr[0, kj] - d0), p1 * (dp_scr[1, kj] - d1)
                dk_ref[pl.ds(k0, t), :] += jnp.where(lo, _dot(ds0, qa), _dot(ds1, qb))
                dc0_ref[pl.ds(k0, t), :] += ds0[:, :128] + ds0[:, 128:]
                dc1_ref[pl.ds(k0, t), :] += ds1[:, :128] + ds1[:, 128:]
                ktv, aktv = kt_ref[:, pl.ds(k0, t)], akt_ref[:, pl.ds(k0, t)]
                return dq0 + _dot(jnp.where(lo_rows, ktv, aktv), ds0), dq1 + _dot(jnp.where(lo_rows, aktv, ktv), ds1)

            zq = jnp.zeros((128, t), F32)
            dq0, dq1 = lax.fori_loop(0, qi + 1, pass2, (zq, zq))
            dq_ref[pl.ds(q0, t), :] = jnp.where(lo_rows, dq0, dq1).T
            return 0

        lax.fori_loop(0, nq, q_loop, 0)
        if ns:
            pl.when(p == N_PAIRS - 1)(finish)

    pair = pl.BlockSpec((s, 128), lambda p: (0, p), pipeline_mode=once)
    pair_t = pl.BlockSpec((128, s), lambda p: (p, 0), pipeline_mode=once)
    out = jax.ShapeDtypeStruct((s, 1024), F32)
    outs = pl.pallas_call(
        body, name="fox_bwd", grid=(N_PAIRS,),
        in_specs=[pair, pair, pair, pair, pair_t, pair_t, pair, pl.BlockSpec((N_HEADS, s), lambda p: (0, 0)),
                  pl.BlockSpec((s, 128), lambda p: (0, 8 + p), pipeline_mode=once)] + [ANY] * ns,
        out_specs=[pair] * 5 + [ANY] * ns,
        out_shape=[out] * 5 + [jax.ShapeDtypeStruct(p.shape, p.dtype) for p in parts],
        scratch_shapes=[pltpu.VMEM((2, nq, t, t), F32), pltpu.VMEM((2, nq, t, t), F32)] + (_scatter_scratch(ns) if ns else []),
        compiler_params=_params(("arbitrary",)),
    )(qn, kn, aq, ak, knt, akt, vb, lse, dmixed, *parts)
    return (*outs[:5], _keep_own_blocks(outs[5:], parts))


def fox_post(dqn, dkn, dc0, dc1, proj, small, smallp, qw, kw, sel, selt, fold_even, fold_odd, *, tm=256):
    s = proj.shape[0]
    nrow = s // tm

    def body(dqn_ref, dkn_ref, dc0_ref, dc1_ref, q_ref, k_ref, small_ref, sp_ref, qw_ref, kw_ref, sel_ref, selt_ref,
             fe_ref, fo_ref, dq_ref, dk_ref, dsmall_ref, gqw_ref, gkw_ref, gfb_ref, carry):
        step = pl.program_id(0)

        @pl.when(step == 0)
        def _():
            carry[...] = jnp.zeros_like(carry)

        def norm_bwd(x_ref, w_ref, dn, out_ref):
            x = x_ref[...]
            rf = _head_rstd(x, sel_ref, selt_ref)
            xh = x * rf
            g = dn * w_ref[...]
            mean_gx = _head_spread(_head_mean(g * xh, sel_ref, selt_ref), selt_ref)
            out_ref[...] = (rf * (g - xh * mean_gx)).astype(BF16)
            return jnp.sum(dn * xh, axis=0, keepdims=True)

        gqw = norm_bwd(q_ref, qw_ref, dqn_ref[...] * FOX_SCALE, dq_ref)
        gkw = norm_bwd(k_ref, kw_ref, dkn_ref[...], dk_ref)
        li = _lane_iota((tm, 128))
        f_lane = jnp.logical_and(li >= F_LANE, li < F_LANE + N_HEADS)
        dcum = -(_split3_dot(dc0_ref[...], fe_ref[...]) + _split3_dot(dc1_ref[...], fo_ref[...]))
        dlogf = _rev_cumsum_rows(dcum) + carry[...]
        carry[...] = dlogf[0:1, :]
        dfr = jnp.where(f_lane, dlogf * _sigmoid(-(small_ref[...] + sp_ref[3:4, :])), 0.0)
        dsmall_ref[...] = dfr
        gfb = jnp.sum(dfr, axis=0, keepdims=True)

        @pl.when(step == 0)
        def _():
            gqw_ref[...] = gqw
            gkw_ref[...] = gkw
            gfb_ref[...] = gfb

        @pl.when(step > 0)
        def _():
            gqw_ref[...] += gqw
            gkw_ref[...] += gkw
            gfb_ref[...] += gfb

    def rb(i):
        return nrow - 1 - i

    row = pl.BlockSpec((tm, 1024), lambda i: (rb(i), 0))
    vec = pl.BlockSpec((1, 1024), lambda i: (0, 0))
    fold = pl.BlockSpec((1024, 128), lambda i: (0, 0))
    return pl.pallas_call(
        body, name="fox_post", grid=(nrow,),
        in_specs=[row, row, row, row, pl.BlockSpec((tm, 1024), lambda i: (rb(i), Q_COL)),
                  pl.BlockSpec((tm, 1024), lambda i: (rb(i), K_COL)),
                  pl.BlockSpec((tm, 128), lambda i: (rb(i), 0)), pl.BlockSpec((8, 128), lambda i: (0, 0)), vec, vec,
                  fold, pl.BlockSpec((128, 1024), lambda i: (0, 0)), fold, fold],
        out_specs=[row, row, pl.BlockSpec((tm, 128), lambda i: (rb(i), 0)), vec, vec, pl.BlockSpec((1, 128), lambda i: (0, 0))],
        out_shape=[jax.ShapeDtypeStruct((s, 1024), BF16), jax.ShapeDtypeStruct((s, 1024), BF16),
                   jax.ShapeDtypeStruct((s, 128), F32), jax.ShapeDtypeStruct((1, 1024), F32),
                   jax.ShapeDtypeStruct((1, 1024), F32), jax.ShapeDtypeStruct((1, 128), F32)],
        scratch_shapes=[pltpu.VMEM((1, 128), F32)], compiler_params=_params(("arbitrary",)),
    )(dqn, dkn, dc0, dc1, proj, proj, small, smallp, qw, kw, sel, selt, fold_even, fold_odd)


def local_step(x, target, wm, ws, later_shards, ssd_cw8, ssd_cb, smallp, ssd_nw, qw_t, kw_t, sel, selt,
               norm_mix_w, norm_ffn_w, ffn_cw8, ffn_cb):
    h, h_t = rms_fwd(x, norm_mix_w, name="rms_mix_fwd")
    proj = matmul(h, wm, mode="nn", tm=1024, tn=1408, tk=1024, out_dtype=F32, name="mm_in_proj")
    small = matmul(h, ws, mode="nn", tm=1024, tn=128, tk=1024, out_dtype=F32, name="mm_in_proj_small")
    y_ssd, y_ssd_t, ypre, states = ssd_fwd(proj, small, ssd_cw8, ssd_cb, smallp, ssd_nw)
    place_q, place_k, ones_q, ones_k, fold_even, fold_odd = fox_tables()
    qn, kn, aq, ak, vb, knt, akt, vt = fox_prep(proj, small, smallp, qw_t, kw_t, sel, selt, place_q, place_k, ones_q, ones_k)
    y_fox, y_fox_t, lse, (a_out, a_up, a_down) = fox_fwd(qn, kn, aq, ak, vt, shards=later_shards)
    w_out = a_out.reshape(2048, D_MODEL)
    w_down = a_down.reshape(D_FF, D_MODEL)
    s = x.shape[0]
    shard = lambda index: pl.BlockSpec((None, 1024, 1408), index)
    x1 = matmul(y_ssd, w_out, mode="nn", tm=1024, tn=1024, tk=1024, out_dtype=F32, name="mm_out_ssd", add=x)
    x1 = matmul(y_fox, w_out, mode="nn", tm=1024, tn=1024, tk=1024, out_dtype=F32, name="mm_out_fox", add=x1, b_koff=1)
    hf, hf_t = rms_fwd(x1, norm_ffn_w, name="rms_ffn_fwd")
    hu = matmul(hf, a_up, mode="nn", tm=1024, tn=1408, tk=1024, out_dtype=F32, name="mm_up",
                layout=dict(m=s, n=2 * D_FF, k=D_MODEL, b_spec=shard(lambda i, j, kk: (j, kk, 0))))
    act, act_t = ffn_mid_fwd(hu, ffn_cw8, ffn_cb)
    y = matmul(act, w_down, mode="nn", tm=1024, tn=1024, tk=1408, out_dtype=F32, name="mm_down", add=x1)
    dy, sq = loss_head(y, target)

    dact = matmul(dy, w_down, mode="nt", tm=1024, tn=1408, tk=1024, out_dtype=F32, name="mm_dact")
    g_down = matmul(act_t, dy, mode="nn", tm=1408, tn=1024, tk=1024, out_dtype=BF16, name="mm_dw_down")
    dhu, gcw_g, gcw_v = ffn_mid_bwd(hu, dact, ffn_cw8, ffn_cb)
    dhf = matmul(dhu, a_up, mode="nt", tm=1024, tn=1024, tk=1408, out_dtype=F32, name="mm_dhf",
                 layout=dict(m=s, n=D_MODEL, k=2 * D_FF, a_spec=shard(lambda i, j, kk: (kk // 2, i, kk % 2)),
                             b_spec=shard(lambda i, j, kk: (kk, 0, 0))))
    g_up = matmul(hf_t, dhu, mode="nn", tm=1024, tn=1408, tk=1024, out_dtype=BF16, name="mm_dw_up",
                  layout=dict(m=D_MODEL, n=2 * D_FF, k=s, b_spec=shard(lambda i, j, kk: (j // 2, kk, j % 2)),
                              o_spec=shard(lambda i, j, kk: (j, i, 0)), out_shape=(4, D_MODEL, 1408)))
    dx1, g_norm_ffn = rms_bwd(dhf, x1, norm_ffn_w, dy, name="rms_ffn_bwd")
    dmixed = matmul(dx1, w_out, mode="nt", tm=1024, tn=1024, tk=1024, out_dtype=F32, name="mm_dmixed")
    g_out_a = matmul(y_ssd_t, dx1, mode="nn", tm=1024, tn=1024, tk=1024, out_dtype=BF16, name="mm_dw_out_ssd")
    g_out_b = matmul(y_fox_t, dx1, mode="nn", tm=1024, tn=1024, tk=1024, out_dtype=BF16, name="mm_dw_out_fox")
    early = [jnp.concatenate([g_out_a, g_out_b], axis=0).reshape(4, 512, D_MODEL), g_up, g_down.reshape(4, 704, D_MODEL)]
    mine, theirs = pair_swap_halves(early, name="pair_swap_early")
    parts = [add_pair(a, b, name="add_pair_" + n, tr=ADAM_ROWS[n]) for a, b, n in zip(mine, theirs, BIG_NAMES[1:])]
    dz, dxs, db, dc, dsmall_ssd, gcw_x, gcw_b, gcw_c, g_sp, g_ssd_nw = ssd_bwd(
        proj, small, ssd_cw8, ssd_cb, smallp, ssd_nw, ypre, states, dmixed, sel)
    dqn, dkn, dv, dc0, dc1, landed_early = fox_bwd(qn, kn, aq, ak, knt, akt, vb, lse, dmixed, parts=parts)
    dq, dk, dsmall_fox, g_qw, g_kw, g_fb = fox_post(dqn, dkn, dc0, dc1, proj, small, smallp, qw_t, kw_t, sel, selt,
                                                    fold_even, fold_odd)
    dproj = jnp.concatenate([dz, dxs, dq, dk, dv.astype(BF16), db, dc], axis=1)
    dsmall = (dsmall_ssd + dsmall_fox).astype(BF16)
    g_wm = matmul(h_t, dproj, mode="nn", tm=1024, tn=1408, tk=1024, out_dtype=BF16, name="mm_dw_in")
    g_ws = matmul(h_t, dsmall, mode="nn", tm=1024, tn=128, tk=1024, out_dtype=BF16, name="mm_dw_in_small")
    mine, theirs = pair_swap_halves([_in_grad_shards(g_wm, g_ws)], name="pair_swap_w_in")
    part_in = add_pair(mine[0], theirs[0], name="add_pair_w_in", tr=ADAM_ROWS["w_in"])
    dh, landed_in = matmul(dproj, wm, mode="nt", tm=1024, tn=1024, tk=1408, out_dtype=F32, name="mm_dh", scatter=[part_in])
    dh = matmul(dsmall, ws, mode="nt", tm=1024, tn=1024, tk=128, out_dtype=F32, name="mm_dh_small", add=dh)
    grad_x, g_norm_mix = rms_bwd(dh, x, norm_mix_w, dx1, name="rms_mix_bwd")
    return dict(
        sq=sq, grad_x=grad_x, landed=landed_in + landed_early,
        g_norm_mix=g_norm_mix, g_norm_ffn=g_norm_ffn, g_ssd_nw=g_ssd_nw,
        g_ssd_cw=jnp.concatenate([gcw_x, gcw_b, gcw_c], axis=1), g_sp=g_sp, g_fb=g_fb, g_qw=g_qw, g_kw=g_kw,
        g_ffn_cw=jnp.concatenate([gcw_g, gcw_v], axis=1))


def adamw(w, g, m, v, *, name, tr):
    rows, cols = w.shape

    def body(w_ref, g_ref, m_ref, v_ref, d_ref, mo_ref, vo_ref):
        gv = g_ref[...]
        mn = ADAM_B1 * m_ref[...] + (1.0 - ADAM_B1) * gv
        vn = ADAM_B2 * v_ref[...] + (1.0 - ADAM_B2) * (gv * gv)
        m_hat = mn / (1.0 - ADAM_B1 ** ADAM_STEP)
        v_hat = vn / (1.0 - ADAM_B2 ** ADAM_STEP)
        d_ref[...] = -ADAM_LR * (m_hat / (jnp.sqrt(v_hat) + ADAM_EPS) + ADAM_WD * w_ref[...])
        mo_ref[...] = mn
        vo_ref[...] = vn

    blk = pl.BlockSpec((tr, cols), lambda i: (i, 0))
    shp = jax.ShapeDtypeStruct((rows, cols), F32)
    return pl.pallas_call(
        body, name=name, grid=(rows // tr,), in_specs=[blk] * 4, out_specs=[blk] * 3, out_shape=[shp] * 3,
        compiler_params=_params(("parallel",)),
    )(w, g, m, v)


def add_pair(a, b, *, name, tr):
    _, rows, cols = a.shape

    def body(a_ref, b_ref, o_ref):
        o_ref[...] = (a_ref[...].astype(F32) + b_ref[...].astype(F32)).astype(BF16)

    blk = pl.BlockSpec((1, tr, cols), lambda j, i: (j, i, 0))
    return pl.pallas_call(
        body, name=name, grid=(4, rows // tr), in_specs=[blk, blk], out_specs=blk,
        out_shape=jax.ShapeDtypeStruct(a.shape, BF16), compiler_params=_params(("parallel", "parallel")),
    )(a, b)


def sum_chips(parts, core, *, name, tr):
    _, rows, cols = parts.shape
    nblk = rows // tr

    def body(c_ref, p_ref, o_ref):
        acc = p_ref[0].astype(F32)
        for k in range(1, 4):
            acc = acc + p_ref[k].astype(F32)
        o_ref[...] = acc

    grid_spec = pltpu.PrefetchScalarGridSpec(
        num_scalar_prefetch=1, grid=(nblk,), in_specs=[pl.BlockSpec((4, tr, cols), lambda i, c: (0, i, 0))],
        out_specs=pl.BlockSpec((tr, cols), lambda i, c: (c[0] * nblk + i, 0)))
    return pl.pallas_call(
        body, name=name, grid_spec=grid_spec, out_shape=jax.ShapeDtypeStruct((2 * rows, cols), F32),
        compiler_params=_params(("parallel",)),
    )(core, parts)


ANY = pl.BlockSpec(memory_space=pl.ANY)


def _place():
    x, y, c = lax.axis_index("x"), lax.axis_index("y"), lax.axis_index("c")
    chips = [(1 - x, y), (x, 1 - y), (1 - x, 1 - y)]
    return x, y, c, chips


def _chunks(rows):
    size = next((c for c in (128, 176, 64, 32, 16, 8) if rows % c == 0), rows)
    return [(r, size) for r in range(0, rows, size)]


def gather_weights(shards):
    n = len(shards)

    def body(*refs):
        start, forward, finish = _gather_phases(refs[:n], refs[n:2 * n], *refs[2 * n:])
        start()
        forward()
        finish()

    gathered = pl.pallas_call(
        body, name="gather_weights", in_specs=[ANY] * n, out_specs=[ANY] * n,
        out_shape=_gather_out_shapes(shards), scratch_shapes=_gather_scratch(n),
    )(*shards)
    return gathered


def _gather_out_shapes(shards):
    return [jax.ShapeDtypeStruct((4,) + s.shape, s.dtype) for s in shards]


def _gather_scratch(n):
    return [pltpu.SemaphoreType.DMA((n, 7)), pltpu.SemaphoreType.DMA((n, 7))]


def _gather_phases(ins, outs, send_sems, recv_sems):
    n = len(ins)
    x, y, c, chips = _place()
    me = 2 * x + y
    sibling = (x, y, 1 - c)
    blks = [2 * cx + cy for cx, cy in chips]

    def half(a, blk, r=0, nr=None):
        rows = ins[a].shape[0] // 2
        return outs[a].at[blk, pl.ds(c * rows + r, rows if nr is None else nr), :]

    def to_chip(a, t, r=0, nr=None):
        rows = ins[a].shape[0] // 2
        return pltpu.make_async_remote_copy(
            src_ref=ins[a].at[pl.ds(c * rows + r, rows if nr is None else nr), :], dst_ref=half(a, me, r, nr),
            send_sem=send_sems.at[a, t], recv_sem=recv_sems.at[a, t], device_id=(*chips[t], c), device_id_type=MESH)

    def from_chip(a, t):
        return pltpu.make_async_remote_copy(
            src_ref=half(a, blks[t]), dst_ref=half(a, blks[t]), send_sem=send_sems.at[a, t], recv_sem=recv_sems.at[a, t],
            device_id=(*chips[t], c), device_id_type=MESH)

    def to_sibling(a, t, r=0, nr=None):
        return pltpu.make_async_remote_copy(
            src_ref=half(a, blks[t], r, nr), dst_ref=half(a, blks[t], r, nr), send_sem=send_sems.at[a, 3 + t],
            recv_sem=recv_sems.at[a, 3 + t], device_id=sibling, device_id_type=MESH)

    def from_sibling(a, t):
        rows = ins[a].shape[0] // 2
        dst = outs[a].at[blks[t], pl.ds((1 - c) * rows, rows), :]
        return pltpu.make_async_remote_copy(
            src_ref=dst, dst_ref=dst, send_sem=send_sems.at[a, 3 + t], recv_sem=recv_sems.at[a, 3 + t],
            device_id=sibling, device_id_type=MESH)

    def own(a, r=0, nr=None):
        return pltpu.make_async_remote_copy(
            src_ref=ins[a].at[pl.ds(r, ins[a].shape[0] if nr is None else nr), :],
            dst_ref=outs[a].at[me, pl.ds(r, ins[a].shape[0] if nr is None else nr), :],
            send_sem=send_sems.at[a, 6], recv_sem=recv_sems.at[a, 6], device_id=sibling, device_id_type=MESH)

    def start():
        for a in range(n):
            for t in range(3):
                for r, nr in _chunks(ins[a].shape[0] // 2):
                    to_chip(a, t, r, nr).start()
            for r, nr in _chunks(ins[a].shape[0]):
                own(a, r, nr).start()

    def forward():
        for a in range(n):
            for t in range(3):
                from_chip(a, t).wait_recv()
                for r, nr in _chunks(ins[a].shape[0] // 2):
                    to_sibling(a, t, r, nr).start()

    def finish():
        for a in range(n):
            for t in range(3):
                from_sibling(a, t).wait_recv()
        for a in range(n):
            for t in range(3):
                to_chip(a, t).wait_send()
                to_sibling(a, t).wait_send()
            own(a).wait()

    return start, forward, finish


def pair_swap_halves(grads, *, name):
    n = len(grads)

    def body(*refs):
        ins, theirs = refs[:n], refs[n:2 * n]
        send_sems, recv_sems = refs[2 * n:]
        x, y, c, _ = _place()
        sibling = (x, y, 1 - c)
        for a in range(n):
            rows = ins[a].shape[1] // 2
            for j in range(4):
                for r, nr in _chunks(rows):
                    pltpu.make_async_remote_copy(
                        src_ref=ins[a].at[j, pl.ds((1 - c) * rows + r, nr), :], dst_ref=theirs[a].at[j, pl.ds(r, nr), :],
                        send_sem=send_sems.at[a], recv_sem=recv_sems.at[a], device_id=sibling, device_id_type=MESH).start()
        for a in range(n):
            pltpu.make_async_remote_copy(src_ref=theirs[a], dst_ref=theirs[a], send_sem=send_sems.at[a],
                                         recv_sem=recv_sems.at[a], device_id=sibling, device_id_type=MESH).wait()

    halves = [jax.ShapeDtypeStruct((4, g.shape[1] // 2, g.shape[2]), g.dtype) for g in grads]
    theirs = pl.pallas_call(
        body, name=name, in_specs=[ANY] * n, out_specs=[ANY] * n, out_shape=halves,
        scratch_shapes=[pltpu.SemaphoreType.DMA((n,)), pltpu.SemaphoreType.DMA((n,))],
    )(*grads)
    c = lax.axis_index("c")
    mine = [lax.dynamic_slice_in_dim(g, c * (g.shape[1] // 2), g.shape[1] // 2, axis=1) for g in grads]
    return mine, theirs


def _scatter_scratch(n):
    return [pltpu.SemaphoreType.DMA((n, 3)), pltpu.SemaphoreType.DMA((n, 3))]


def _keep_own_blocks(landed, parts):
    if not parts:
        return []
    chip = 2 * lax.axis_index("x") + lax.axis_index("y")
    return [lax.dynamic_update_slice(l, lax.dynamic_slice_in_dim(p, chip, 1, axis=0), (chip, 0, 0))
            for l, p in zip(landed, parts)]


def _scatter_phases(ins, outs, send_sems, recv_sems):
    n = len(ins)
    x, y, c, chips = _place()
    me = 2 * x + y
    blks = [2 * cx + cy for cx, cy in chips]

    def start():
        for a in range(n):
            for r, nr in _chunks(ins[a].shape[1]):
                for t in range(3):
                    pltpu.make_async_remote_copy(
                        src_ref=ins[a].at[blks[t], pl.ds(r, nr), :], dst_ref=outs[a].at[me, pl.ds(r, nr), :],
                        send_sem=send_sems.at[a, t], recv_sem=recv_sems.at[a, t],
                        device_id=(*chips[t], c), device_id_type=MESH).start()

    def finish():
        for a in range(n):
            for t in range(3):
                pltpu.make_async_remote_copy(
                    src_ref=outs[a].at[blks[t]], dst_ref=outs[a].at[blks[t]], send_sem=send_sems.at[a, t],
                    recv_sem=recv_sems.at[a, t], device_id=(*chips[t], c), device_id_type=MESH).wait()

    return start, finish


def pair_join_halves(bufs):
    n = len(bufs)

    def body(*refs):
        outs = refs[n:2 * n]
        send_sems, recv_sems = refs[2 * n:]
        x, y, c, _ = _place()
        sibling = (x, y, 1 - c)
        for a in range(n):
            rows = outs[a].shape[0] // 2
            for r, nr in _chunks(rows):
                mine = outs[a].at[pl.ds(c * rows + r, nr), :]
                pltpu.make_async_remote_copy(src_ref=mine, dst_ref=mine, send_sem=send_sems.at[a], recv_sem=recv_sems.at[a],
                                             device_id=sibling, device_id_type=MESH).start()
        for a in range(n):
            rows = outs[a].shape[0] // 2
            pltpu.make_async_remote_copy(
                src_ref=outs[a].at[pl.ds(c * rows, rows), :], dst_ref=outs[a].at[pl.ds((1 - c) * rows, rows), :],
                send_sem=send_sems.at[a], recv_sem=recv_sems.at[a], device_id=sibling, device_id_type=MESH).wait()

    return pl.pallas_call(
        body, name="pair_join_halves", in_specs=[ANY] * n, out_specs=[ANY] * n,
        out_shape=[jax.ShapeDtypeStruct(b.shape, b.dtype) for b in bufs], input_output_aliases={a: a for a in range(n)},
        scratch_shapes=[pltpu.SemaphoreType.DMA((n,)), pltpu.SemaphoreType.DMA((n,))],
    )(*bufs)


def allreduce_small(packed):
    rows = packed.shape[0]

    def body(in_ref, out_ref, gathered, send_sems, recv_sems):
        x, y, c, _ = _place()
        me = 4 * x + 2 * y + c
        gathered[me] = in_ref[...]
        flips = [(fx, fy, fc) for fx in (0, 1) for fy in (0, 1) for fc in (0, 1)][1:]
        peers = [((1 - x) if fx else x, (1 - y) if fy else y, (1 - c) if fc else c) for fx, fy, fc in flips]
        copies = []
        for t, peer in enumerate(peers):
            cp = pltpu.make_async_remote_copy(
                src_ref=in_ref, dst_ref=gathered.at[me], send_sem=send_sems.at[t], recv_sem=recv_sems.at[t],
                device_id=peer, device_id_type=MESH)
            cp.start()
            copies.append(cp)
        for t, (px, py, pc) in enumerate(peers):
            slot = gathered.at[4 * px + 2 * py + pc]
            pltpu.make_async_remote_copy(
                src_ref=slot, dst_ref=slot, send_sem=send_sems.at[t], recv_sem=recv_sems.at[t],
                device_id=(px, py, pc), device_id_type=MESH).wait_recv()
        for cp in copies:
            cp.wait_send()
        acc = gathered[0]
        for k in range(1, 8):
            acc = acc + gathered[k]
        out_ref[...] = acc

    vm = pl.BlockSpec(memory_space=pltpu.VMEM)
    return pl.pallas_call(
        body, name="allreduce_small", in_specs=[vm], out_specs=vm, out_shape=jax.ShapeDtypeStruct(packed.shape, F32),
        scratch_shapes=[pltpu.VMEM((8, rows, 128), F32), pltpu.SemaphoreType.DMA((7,)), pltpu.SemaphoreType.DMA((7,))],
    )(packed)


SMALL_NAMES = ("norm_mix_w", "ssd_conv_w", "ssd_conv_b", "ssd_dt_bias", "ssd_a_log", "ssd_d", "ssd_norm_w", "fox_f_bias",
               "fox_q_norm_w", "fox_k_norm_w", "norm_ffn_w", "ffn_conv_w", "ffn_conv_b")
BIG_NAMES = ("w_in", "w_out", "w_up", "w_down")
WEIGHT_ORDER = ("norm_mix_w", "w_in", "ssd_conv_w", "ssd_conv_b", "ssd_dt_bias", "ssd_a_log", "ssd_d", "ssd_norm_w",
                "fox_f_bias", "fox_q_norm_w", "fox_k_norm_w", "w_out", "norm_ffn_w", "w_up", "ffn_conv_w", "ffn_conv_b", "w_down")
ADAM_ROWS = {"w_in": 256, "w_out": 256, "w_up": 256, "w_down": 176}


def _pack(arrays):
    pieces = []
    for a in arrays:
        flat = a.reshape(-1).astype(F32)
        pieces += [flat, jnp.zeros(((-flat.shape[0]) % 1024,), F32)]
    return jnp.concatenate(pieces).reshape(-1, 128)


def _unpack(packed, shapes):
    out, r = [], 0
    for shp in shapes:
        size = 1
        for d in shp:
            size *= d
        nrow = 8 * (-(-size // 1024))
        out.append(packed[r:r + nrow].reshape(-1)[:size].reshape(shp))
        r += nrow
    return out


IN_SHARD = IN_COLS // 4
IN_SEGMENTS = ((0, 2048, "main", 0), (2048, 2560, "main", 5120), (2560, 2576, "small", 0), (2576, 5648, "main", 2048),
               (5648, 5664, "small", 16))


def _in_cols(shards, lo, hi):
    out = []
    for j in range(4):
        a, b = max(lo, IN_SHARD * j), min(hi, IN_SHARD * (j + 1))
        if a < b:
            out.append(shards[j][:, a - IN_SHARD * j:b - IN_SHARD * j])
    return out


def _in_grad_shards(g_main, g_small):
    shards = []
    for j in range(4):
        pieces = []
        for lo, hi, src, at in IN_SEGMENTS:
            a, b = max(lo, IN_SHARD * j), min(hi, IN_SHARD * (j + 1))
            if a < b:
                pieces.append((g_main if src == "main" else g_small)[:, at + a - lo:at + b - lo])
        shards.append(jnp.concatenate(pieces, axis=1))
    return jnp.stack(shards)


def _pad_rows(a, rows):
    return jnp.pad(a, ((0, rows - a.shape[0]), (0, 0)))


def kernel(x, norm_mix_w, w_in, ssd_conv_w, ssd_conv_b, ssd_dt_bias, ssd_a_log, ssd_d, ssd_norm_w, fox_f_bias, fox_q_norm_w, fox_k_norm_w, w_out, norm_ffn_w, w_up, ffn_conv_w, ffn_conv_b, w_down, loss_target, m_norm_mix_w, m_w_in, m_ssd_conv_w, m_ssd_conv_b, m_ssd_dt_bias, m_ssd_a_log, m_ssd_d, m_ssd_norm_w, m_fox_f_bias, m_fox_q_norm_w, m_fox_k_norm_w, m_w_out, m_norm_ffn_w, m_w_up, m_ffn_conv_w, m_ffn_conv_b, m_w_down, v_norm_mix_w, v_w_in, v_ssd_conv_w, v_ssd_conv_b, v_ssd_dt_bias, v_ssd_a_log, v_ssd_d, v_ssd_norm_w, v_fox_f_bias, v_fox_q_norm_w, v_fox_k_norm_w, v_w_out, v_norm_ffn_w, v_w_up, v_ffn_conv_w, v_ffn_conv_b, v_w_down):
    w = dict(norm_mix_w=norm_mix_w, w_in=w_in, ssd_conv_w=ssd_conv_w, ssd_conv_b=ssd_conv_b, ssd_dt_bias=ssd_dt_bias,
             ssd_a_log=ssd_a_log, ssd_d=ssd_d, ssd_norm_w=ssd_norm_w, fox_f_bias=fox_f_bias, fox_q_norm_w=fox_q_norm_w,
             fox_k_norm_w=fox_k_norm_w, w_out=w_out, norm_ffn_w=norm_ffn_w, w_up=w_up, ffn_conv_w=ffn_conv_w,
             ffn_conv_b=ffn_conv_b, w_down=w_down)
    m = dict(norm_mix_w=m_norm_mix_w, w_in=m_w_in, ssd_conv_w=m_ssd_conv_w, ssd_conv_b=m_ssd_conv_b, ssd_dt_bias=m_ssd_dt_bias,
             ssd_a_log=m_ssd_a_log, ssd_d=m_ssd_d, ssd_norm_w=m_ssd_norm_w, fox_f_bias=m_fox_f_bias, fox_q_norm_w=m_fox_q_norm_w,
             fox_k_norm_w=m_fox_k_norm_w, w_out=m_w_out, norm_ffn_w=m_norm_ffn_w, w_up=m_w_up, ffn_conv_w=m_ffn_conv_w,
             ffn_conv_b=m_ffn_conv_b, w_down=m_w_down)
    v = dict(norm_mix_w=v_norm_mix_w, w_in=v_w_in, ssd_conv_w=v_ssd_conv_w, ssd_conv_b=v_ssd_conv_b, ssd_dt_bias=v_ssd_dt_bias,
             ssd_a_log=v_ssd_a_log, ssd_d=v_ssd_d, ssd_norm_w=v_ssd_norm_w, fox_f_bias=v_fox_f_bias, fox_q_norm_w=v_fox_q_norm_w,
             fox_k_norm_w=v_fox_k_norm_w, w_out=v_w_out, norm_ffn_w=v_norm_ffn_w, w_up=v_w_up, ffn_conv_w=v_ffn_conv_w,
             ffn_conv_b=v_ffn_conv_b, w_down=v_w_down)
    chip = 2 * lax.axis_index("x") + lax.axis_index("y")

    a_in, a_scw, a_fcw = gather_weights([w_in[0].astype(BF16), _pad_rows(ssd_conv_w[0], 16), _pad_rows(ffn_conv_w[0], 16)])
    later_shards = [w_out[0].astype(BF16), w_up[0].astype(BF16), w_down[0].astype(BF16)]
    wm = jnp.concatenate([p for lo, hi, src, _ in sorted(IN_SEGMENTS, key=lambda seg: seg[3]) if src == "main"
                          for p in _in_cols(a_in, lo, hi)], axis=1)
    ws = jnp.concatenate([p for lo, hi, src, _ in IN_SEGMENTS if src == "small" for p in _in_cols(a_in, lo, hi)]
                         + [jnp.zeros((D_MODEL, SMALL_COLS - 32), BF16)], axis=1)
    ssd_cw8 = a_scw.transpose(1, 0, 2).reshape(16, 1536)[:8]
    ffn_cw8 = a_fcw.transpose(1, 0, 2).reshape(16, 2 * D_FF)[:8]
    smallp = jnp.zeros((8, 128), F32)
    smallp = smallp.at[0, :16].set(ssd_dt_bias[0]).at[1, :16].set(ssd_a_log[0]).at[2, :16].set(ssd_d[0])
    smallp = smallp.at[3, F_LANE:F_LANE + 16].set(fox_f_bias[0])
    qw_t = jnp.tile(fox_q_norm_w[0], N_HEADS)[None]
    kw_t = jnp.tile(fox_k_norm_w[0], N_HEADS)[None]
    sel = jnp.asarray((np.arange(1024)[:, None] // HEAD_DIM == np.arange(128)[None, :]).astype(np.float32), BF16)

    res = local_step(x[0], loss_target[0], wm, ws, later_shards, ssd_cw8, ssd_conv_b, smallp, ssd_norm_w, qw_t, kw_t,
                     sel, sel.T, norm_mix_w, norm_ffn_w, ffn_cw8, ffn_conv_b)

    full_shapes = [(1, 1024), (1, 4, 1536), (1, 1536), (1, 16), (1, 16), (1, 16), (1, 1024), (1, 16), (1, 64), (1, 64),
                   (1, 1024), (1, 3, 2 * D_FF), (1, 2 * D_FF), (1,)]
    local_small = [res["g_norm_mix"], res["g_ssd_cw"][:4], res["g_ssd_cw"][4], res["g_sp"][0, :16], res["g_sp"][1, :16],
                   res["g_sp"][2, :16], res["g_ssd_nw"], res["g_fb"][0, F_LANE:F_LANE + 16],
                   res["g_qw"].reshape(N_HEADS, HEAD_DIM).sum(0), res["g_kw"].reshape(N_HEADS, HEAD_DIM).sum(0),
                   res["g_norm_ffn"], res["g_ffn_cw"][:3], res["g_ffn_cw"][3], jnp.sum(res["sq"])]
    summed = _unpack(allreduce_small(_pack(local_small)), full_shapes)
    loss = (0.5 / D_MODEL) * summed[-1][0]
    g_small = dict(zip(SMALL_NAMES, summed[:-1]))
    g_small["ssd_conv_w"] = lax.dynamic_slice(g_small["ssd_conv_w"], (0, 0, 384 * chip), (1, 4, 384))
    g_small["ffn_conv_w"] = lax.dynamic_slice(g_small["ffn_conv_w"], (0, 0, 1408 * chip), (1, 3, 1408))

    landed = res["landed"]
    core = lax.axis_index("c").astype(jnp.int32).reshape(1)
    halves = [sum_chips(p, core, name="sum_chips_" + n, tr=ADAM_ROWS[n]) for p, n in zip(landed, BIG_NAMES)]
    g_big = dict(zip(BIG_NAMES, pair_join_halves(halves)))

    grads, deltas, new_m, new_v = {}, {}, {}, {}
    for n in BIG_NAMES:
        d, mn, vn = adamw(w[n][0], g_big[n], m[n][0], v[n][0], name="adamw_" + n, tr=ADAM_ROWS[n])
        grads[n], deltas[n], new_m[n], new_v[n] = g_big[n][None], d[None], mn[None], vn[None]
    shapes = [w[n].shape for n in SMALL_NAMES]
    packed_w = _pack([w[n] for n in SMALL_NAMES])
    d, mn, vn = adamw(packed_w, _pack([g_small[n] for n in SMALL_NAMES]), _pack([m[n] for n in SMALL_NAMES]),
                      _pack([v[n] for n in SMALL_NAMES]), name="adamw_small", tr=packed_w.shape[0])
    for n, dd, mm, vv in zip(SMALL_NAMES, _unpack(d, shapes), _unpack(mn, shapes), _unpack(vn, shapes)):
        grads[n], deltas[n], new_m[n], new_v[n] = g_small[n].reshape(w[n].shape), dd, mm, vv
    return (loss, res["grad_x"][None], *[grads[n] for n in WEIGHT_ORDER], *[deltas[n] for n in WEIGHT_ORDER],
            *[new_m[n] for n in WEIGHT_ORDER], *[new_v[n] for n in WEIGHT_ORDER])
```

```python
import functools

import jax
import jax.numpy as jnp
import numpy as np
from jax import lax
from jax.experimental import pallas as pl
from jax.experimental.pallas import tpu as pltpu

F32 = jnp.float32
BF16 = jnp.bfloat16
MESH = pl.DeviceIdType.MESH

D_MODEL = 1024
HEAD_DIM = 64
N_HEADS = 16
N_PAIRS = N_HEADS // 2
SSD_CHUNK = 128
SSD_STATE = 128
SSD_CONV = 4
D_FF = 2816
FFN_CONV = 3
NORM_EPS = 1e-6
MAIN_COLS = 5632
SMALL_COLS = 128
F_LANE = 16
IN_COLS = 5664

ADAM_LR = 0.001
ADAM_B1 = 0.9
ADAM_B2 = 0.999
ADAM_EPS = 1e-08
ADAM_WD = 0.01
ADAM_STEP = 10

VMEM_LIMIT_V7X = 56 * 1024 * 1024
NEG_BIG = -1e30


def _params(sem=None):
    return pltpu.CompilerParams(dimension_semantics=sem, vmem_limit_bytes=VMEM_LIMIT_V7X)


def _sigmoid(x):
    return 1.0 / (1.0 + jnp.exp(-x))


def _silu_and_grad(x):
    s = _sigmoid(x)
    return x * s, s * (1.0 + x * (1.0 - s))


def _shift_down(v, j):
    return v if j == 0 else pltpu.roll(v, j, 0)


def _shift_up(v, j):
    return v if j == 0 else pltpu.roll(v, v.shape[0] - j, 0)


def _row_iota(shape):
    return lax.broadcasted_iota(jnp.int32, shape, 0)


def _lane_iota(shape):
    return lax.broadcasted_iota(jnp.int32, shape, 1)


def _dot(a, b, mode="nn"):
    dims = {"nn": (((1,), (0,)), ((), ())), "nt": (((1,), (1,)), ((), ())), "tn": (((0,), (0,)), ((), ()))}[mode]
    return lax.dot_general(a.astype(BF16), b.astype(BF16), dims, preferred_element_type=F32)


def _dot_f32(a, b):
    return jnp.dot(a, b, precision=lax.Precision.HIGHEST, preferred_element_type=F32)


def matmul(a, b, *, mode, tm, tn, tk, out_dtype, name, add=None, b_koff=0, scatter=(), layout=None):
    layout = layout or {}
    if layout:
        m, n, k = layout["m"], layout["n"], layout["k"]
    else:
        (m, k), n = a.shape, (b.shape[1] if mode == "nn" else b.shape[0])
    assert m % tm == 0 and n % tn == 0 and k % tk == 0, (name, m, n, k, tm, tn, tk)
    nk = k // tk
    grid = (m // tm, n // tn, nk)
    a_spec = layout.get("a_spec") or pl.BlockSpec((tm, tk), lambda i, j, kk: (i, kk))
    b_spec = layout.get("b_spec") or (pl.BlockSpec((tn, tk), lambda i, j, kk: (j, kk + b_koff)) if mode == "nt"
                                      else pl.BlockSpec((tk, tn), lambda i, j, kk: (kk + b_koff, j)))
    o_spec = layout.get("o_spec") or pl.BlockSpec((tm, tn), lambda i, j, kk: (i, j))
    out_struct = jax.ShapeDtypeStruct(layout.get("out_shape", (m, n)), out_dtype)
    has_add = add is not None
    n_in = 3 if has_add else 2
    ns = len(scatter)

    def body(*refs):
        a_ref, b_ref = refs[:2]
        add_ref = refs[2] if has_add else None
        o_ref, acc_ref = refs[n_in + ns], refs[n_in + 2 * ns + 1]
        kk = pl.program_id(2)
        if ns:
            step = (pl.program_id(0) * grid[1] + pl.program_id(1)) * grid[2] + kk
            start, finish_copies = _scatter_phases(refs[n_in:n_in + ns], refs[n_in + ns + 1:n_in + 2 * ns + 1],
                                                   *refs[n_in + 2 * ns + 2:])
            pl.when(step == 0)(start)
        part = _dot(a_ref[...], b_ref[...], mode)

        def finish(total):
            if has_add:
                total = total + add_ref[...]
            o_ref[...] = total.astype(out_dtype)

        if nk == 1:
            finish(part)
        else:
            @pl.when(kk == 0)
            def _():
                acc_ref[...] = part

            @pl.when(jnp.logical_and(kk > 0, kk < nk - 1))
            def _():
                acc_ref[...] += part

            @pl.when(kk == nk - 1)
            def _():
                finish(acc_ref[...] + part)

        if ns:
            pl.when(step == grid[0] * grid[1] * grid[2] - 1)(finish_copies)

    in_specs = [a_spec, b_spec] + ([o_spec] if has_add else [])
    args = (a, b) + ((add,) if has_add else ())
    acc = pltpu.VMEM((tm, tn) if nk > 1 else (8, 128), F32)
    if not ns:
        return pl.pallas_call(
            body, name=name, grid=grid, in_specs=in_specs, out_specs=o_spec, out_shape=out_struct,
            scratch_shapes=[acc], compiler_params=_params(("parallel", "parallel", "arbitrary")),
        )(*args)
    outs = pl.pallas_call(
        body, name=name, grid=grid, in_specs=in_specs + [ANY] * ns, out_specs=[o_spec] + [ANY] * ns,
        out_shape=[out_struct] + [jax.ShapeDtypeStruct(p.shape, p.dtype) for p in scatter],
        scratch_shapes=[acc] + _scatter_scratch(ns), compiler_params=_params(("arbitrary", "arbitrary", "arbitrary")),
    )(*args, *scatter)
    return outs[0], _keep_own_blocks(outs[1:], scatter)


def rms_fwd(x, w, *, name, tm=512):
    s, d = x.shape

    def body(x_ref, w_ref, h_ref, ht_ref):
        xv = x_ref[...]
        r = lax.rsqrt(jnp.mean(xv * xv, axis=-1, keepdims=True) + NORM_EPS)
        h = (xv * r) * w_ref[...]
        h_ref[...] = h.astype(BF16)
        ht_ref[...] = h.T.astype(BF16)

    return pl.pallas_call(
        body, name=name, grid=(s // tm,),
        in_specs=[pl.BlockSpec((tm, d), lambda i: (i, 0)), pl.BlockSpec((1, d), lambda i: (0, 0))],
        out_specs=[pl.BlockSpec((tm, d), lambda i: (i, 0)), pl.BlockSpec((d, tm), lambda i: (0, i))],
        out_shape=[jax.ShapeDtypeStruct((s, d), BF16), jax.ShapeDtypeStruct((d, s), BF16)],
        compiler_params=_params(("parallel",)),
    )(x, w)


def rms_bwd(dh, x, w, resid, *, name, tm=512):
    s, d = x.shape

    def body(dh_ref, x_ref, w_ref, res_ref, dx_ref, dw_ref):
        xv = x_ref[...]
        dhv = dh_ref[...]
        r = lax.rsqrt(jnp.mean(xv * xv, axis=-1, keepdims=True) + NORM_EPS)
        xh = xv * r
        g = dhv * w_ref[...]
        dx_ref[...] = res_ref[...] + r * (g - xh * jnp.mean(g * xh, axis=-1, keepdims=True))
        part = jnp.sum(dhv * xh, axis=0, keepdims=True)

        @pl.when(pl.program_id(0) == 0)
        def _():
            dw_ref[...] = part

        @pl.when(pl.program_id(0) > 0)
        def _():
            dw_ref[...] += part

    row = pl.BlockSpec((tm, d), lambda i: (i, 0))
    vec = pl.BlockSpec((1, d), lambda i: (0, 0))
    return pl.pallas_call(
        body, name=name, grid=(s // tm,), in_specs=[row, row, vec, row], out_specs=[row, vec],
        out_shape=[jax.ShapeDtypeStruct((s, d), F32), jax.ShapeDtypeStruct((1, d), F32)],
        compiler_params=_params(("arbitrary",)),
    )(dh, x, w, resid)


def loss_head(y, target, *, tm=512):
    s, d = y.shape

    def body(y_ref, t_ref, dy_ref, sq_ref):
        e = y_ref[...] - t_ref[...]
        dy_ref[...] = e / float(d)
        part = jnp.sum(e * e, axis=0, keepdims=True)

        @pl.when(pl.program_id(0) == 0)
        def _():
            sq_ref[...] = part

        @pl.when(pl.program_id(0) > 0)
        def _():
            sq_ref[...] += part

    row = pl.BlockSpec((tm, d), lambda i: (i, 0))
    vec = pl.BlockSpec((1, d), lambda i: (0, 0))
    return pl.pallas_call(
        body, name="loss_head", grid=(s // tm,), in_specs=[row, row], out_specs=[row, vec],
        out_shape=[jax.ShapeDtypeStruct((s, d), F32), jax.ShapeDtypeStruct((1, d), F32)],
        compiler_params=_params(("arbitrary",)),
    )(y, target)


def _row_shifts(ext, k_taps):
    return [_shift_down(ext, j) for j in range(k_taps)]


def _conv_rows(shifts, w):
    k_taps = len(shifts)
    acc = w[k_taps - 1:k_taps, :] * shifts[0]
    for k in range(k_taps - 1):
        acc = acc + w[k:k + 1, :] * shifts[k_taps - 1 - k]
    return acc


def _conv_weight_grad(dcur, shifts, rows, width):
    k_taps = len(shifts)
    out = [jnp.sum(dcur * shifts[k_taps - 1 - k][rows], axis=0, keepdims=True) for k in range(k_taps)]
    out.append(jnp.sum(dcur, axis=0, keepdims=True))
    return _stack_rows(out, width)


def _conv_rows_transposed(dext, w, k_taps):
    acc = w[k_taps - 1:k_taps, :] * dext
    for k in range(k_taps - 1):
        acc = acc + w[k:k + 1, :] * _shift_up(dext, k_taps - 1 - k)
    return acc


def _stack_rows(rows, width):
    ri = _row_iota((8, width))
    out = jnp.zeros((8, width), F32)
    for k, r in enumerate(rows):
        out = out + jnp.where(ri == k, r, 0.0)
    return out


def ffn_mid_fwd(hu, conv_w8, conv_b, *, tm=1024, tc=256):
    s = hu.shape[0]
    ncol = D_FF // tc
    r8 = tm // 8

    def body(g_ref, v_ref, gp_ref, vp_ref, wg_ref, wv_ref, bg_ref, bv_ref, o_ref, ot_ref):
        first = pl.program_id(1) == 0

        def conv(cur_ref, prev_ref, w_ref, b_ref):
            prev = jnp.where(first, 0.0, prev_ref[...])
            ext = jnp.concatenate([prev, cur_ref[...]], axis=0)
            return _conv_rows(_row_shifts(ext, FFN_CONV), w_ref[...])[8:] + b_ref[...]

        gc = conv(g_ref, gp_ref, wg_ref, bg_ref)
        vc = conv(v_ref, vp_ref, wv_ref, bv_ref)
        act = gc * _sigmoid(gc) * vc
        o_ref[...] = act.astype(BF16)
        ot_ref[...] = act.T.astype(BF16)

    def prev_idx(i):
        return jnp.maximum(i * r8 - 1, 0)

    in_specs = [
        pl.BlockSpec((tm, tc), lambda j, i: (i, j)),
        pl.BlockSpec((tm, tc), lambda j, i: (i, j + ncol)),
        pl.BlockSpec((8, tc), lambda j, i: (prev_idx(i), j)),
        pl.BlockSpec((8, tc), lambda j, i: (prev_idx(i), j + ncol)),
        pl.BlockSpec((8, tc), lambda j, i: (0, j)),
        pl.BlockSpec((8, tc), lambda j, i: (0, j + ncol)),
        pl.BlockSpec((1, tc), lambda j, i: (0, j)),
        pl.BlockSpec((1, tc), lambda j, i: (0, j + ncol)),
    ]
    return pl.pallas_call(
        body, name="ffn_mid_fwd", grid=(ncol, s // tm), in_specs=in_specs,
        out_specs=[pl.BlockSpec((tm, tc), lambda j, i: (i, j)), pl.BlockSpec((tc, tm), lambda j, i: (j, i))],
        out_shape=[jax.ShapeDtypeStruct((s, D_FF), BF16), jax.ShapeDtypeStruct((D_FF, s), BF16)],
        compiler_params=_params(("parallel", "parallel")),
    )(hu, hu, hu, hu, conv_w8, conv_w8, conv_b, conv_b)


def ffn_mid_bwd(hu, dact, conv_w8, conv_b, *, tm=1024, tc=256):
    s = hu.shape[0]
    ncol = D_FF // tc
    nrow = s // tm
    r8 = tm // 8

    def body(g_ref, v_ref, gp_ref, vp_ref, gn_ref, vn_ref, da_ref, dan_ref, wg_ref, wv_ref, bg_ref, bv_ref,
             dhu_ref, wgo_ref, wvo_ref):
        i = pl.program_id(1)
        first = i == 0
        last = i == nrow - 1

        def ext_of(cur_ref, prev_ref, next_ref):
            prev = jnp.where(first, 0.0, prev_ref[...])
            return jnp.concatenate([prev, cur_ref[...], next_ref[...]], axis=0)

        g_sh = _row_shifts(ext_of(g_ref, gp_ref, gn_ref), FFN_CONV)
        v_sh = _row_shifts(ext_of(v_ref, vp_ref, vn_ref), FFN_CONV)
        gc = _conv_rows(g_sh, wg_ref[...]) + bg_ref[...]
        vc = _conv_rows(v_sh, wv_ref[...]) + bv_ref[...]
        da_ext = jnp.concatenate([jnp.zeros((8, tc), F32), da_ref[...], jnp.where(last, 0.0, dan_ref[...])], axis=0)
        silu, dsilu = _silu_and_grad(gc)
        dgc = da_ext * vc * dsilu
        dvc = da_ext * silu
        dhu_ref[0] = _conv_rows_transposed(dgc, wg_ref[...], FFN_CONV)[8:8 + tm].astype(BF16)
        dhu_ref[1] = _conv_rows_transposed(dvc, wv_ref[...], FFN_CONV)[8:8 + tm].astype(BF16)

        cur = slice(8, 8 + tm)
        pg = _conv_weight_grad(dgc[cur], g_sh, cur, tc)
        pv = _conv_weight_grad(dvc[cur], v_sh, cur, tc)

        @pl.when(first)
        def _():
            wgo_ref[...] = pg
            wvo_ref[...] = pv

        @pl.when(i > 0)
        def _():
            wgo_ref[...] += pg
            wvo_ref[...] += pv

    def prev_idx(i):
        return jnp.maximum(i * r8 - 1, 0)

    def next_idx(i):
        return jnp.minimum((i + 1) * r8, s // 8 - 1)

    cur_g = pl.BlockSpec((tm, tc), lambda j, i: (i, j))
    cur_v = pl.BlockSpec((tm, tc), lambda j, i: (i, j + ncol))
    in_specs = [
        cur_g, cur_v,
        pl.BlockSpec((8, tc), lambda j, i: (prev_idx(i), j)),
        pl.BlockSpec((8, tc), lambda j, i: (prev_idx(i), j + ncol)),
        pl.BlockSpec((8, tc), lambda j, i: (next_idx(i), j)),
        pl.BlockSpec((8, tc), lambda j, i: (next_idx(i), j + ncol)),
        cur_g,
        pl.BlockSpec((8, tc), lambda j, i: (next_idx(i), j)),
        pl.BlockSpec((8, tc), lambda j, i: (0, j)),
        pl.BlockSpec((8, tc), lambda j, i: (0, j + ncol)),
        pl.BlockSpec((1, tc), lambda j, i: (0, j)),
        pl.BlockSpec((1, tc), lambda j, i: (0, j + ncol)),
    ]
    out_specs = [pl.BlockSpec((2, tm, tc), lambda j, i: (0, i, j)), pl.BlockSpec((8, tc), lambda j, i: (0, j)),
                 pl.BlockSpec((8, tc), lambda j, i: (0, j))]
    out_shape = [jax.ShapeDtypeStruct((2, s, D_FF), BF16),
                 jax.ShapeDtypeStruct((8, D_FF), F32), jax.ShapeDtypeStruct((8, D_FF), F32)]
    return pl.pallas_call(
        body, name="ffn_mid_bwd", grid=(ncol, nrow), in_specs=in_specs, out_specs=out_specs, out_shape=out_shape,
        compiler_params=_params(("parallel", "arbitrary")),
    )(hu, hu, hu, hu, hu, hu, dact, dact, conv_w8, conv_w8, conv_b, conv_b)


def _softplus(x):
    return jnp.maximum(x, 0.0) + jnp.log(1.0 + jnp.exp(-jnp.abs(x)))


def _cumsum_rows(v):
    n = v.shape[0]
    ri = _row_iota(v.shape)
    sh = 1
    while sh < n:
        v = v + jnp.where(ri >= sh, _shift_down(v, sh), 0.0)
        sh *= 2
    return v


def _rev_cumsum_rows(v):
    n = v.shape[0]
    ri = _row_iota(v.shape)
    sh = 1
    while sh < n:
        v = v + jnp.where(ri < n - sh, _shift_up(v, sh), 0.0)
        sh *= 2
    return v


def _total(v):
    return jnp.sum(jnp.sum(v, axis=1, keepdims=True), axis=0, keepdims=True)


def _ssd_in_specs(rev_nc=None):
    def ch(c):
        return c if rev_nc is None else rev_nc - 1 - c

    def prev(c):
        return jnp.maximum(ch(c) * (SSD_CHUNK // 8) - 1, 0)

    L = SSD_CHUNK
    return [
        pl.BlockSpec((L, 1024), lambda c: (ch(c), 0)),
        pl.BlockSpec((L, 1024), lambda c: (ch(c), 1)),
        pl.BlockSpec((L, 256), lambda c: (ch(c), 20)),
        pl.BlockSpec((L, 256), lambda c: (ch(c), 21)),
        pl.BlockSpec((8, 1024), lambda c: (prev(c), 1)),
        pl.BlockSpec((8, 256), lambda c: (prev(c), 20)),
        pl.BlockSpec((8, 256), lambda c: (prev(c), 21)),
        pl.BlockSpec((8, 1024), lambda c: (0, 0)),
        pl.BlockSpec((8, 256), lambda c: (0, 4)),
        pl.BlockSpec((8, 256), lambda c: (0, 5)),
        pl.BlockSpec((1, 1024), lambda c: (0, 0)),
        pl.BlockSpec((1, 256), lambda c: (0, 4)),
        pl.BlockSpec((1, 256), lambda c: (0, 5)),
        pl.BlockSpec((L, SMALL_COLS), lambda c: (ch(c), 0)),
        pl.BlockSpec((8, 128), lambda c: (0, 0)),
        pl.BlockSpec((1, 1024), lambda c: (0, 0)),
    ]


def _ssd_conv_pre(cur_ref, prev_ref, w_ref, b_ref, first):
    prev = jnp.where(first, 0.0, prev_ref[...])
    shifts = _row_shifts(jnp.concatenate([prev, cur_ref[...]], axis=0), SSD_CONV)
    return shifts, _conv_rows(shifts, w_ref[...])[8:] + b_ref[...]


def _ssd_time_consts(small_ref, sp_ref):
    dt_pre = small_ref[...] + sp_ref[0:1, :]
    dt = _softplus(dt_pre)
    a = -jnp.exp(sp_ref[1:2, :])
    acs = _cumsum_rows(dt * a)
    return dt_pre, dt, a, acs


def ssd_fwd(proj, small, conv_w8, conv_b, smallp, norm_w):
    s = proj.shape[0]
    nc = s // SSD_CHUNK
    L = SSD_CHUNK

    def body(z_ref, xs_ref, b_ref, c_ref, xsp_ref, bp_ref, cp_ref, wx_ref, wb_ref, wc_ref, bx_ref, bb_ref, bc_ref,
             small_ref, sp_ref, nw_ref, y_ref, yt_ref, ypre_ref, st_ref, state):
        first = pl.program_id(0) == 0

        @pl.when(first)
        def _():
            state[...] = jnp.zeros_like(state)

        xs = _ssd_conv_pre(xs_ref, xsp_ref, wx_ref, bx_ref, first)[1]
        xs = xs * _sigmoid(xs)
        bm = _ssd_conv_pre(b_ref, bp_ref, wb_ref, bb_ref, first)[1]
        bm = bm * _sigmoid(bm)
        cm = _ssd_conv_pre(c_ref, cp_ref, wc_ref, bc_ref, first)[1]
        cm = cm * _sigmoid(cm)
        _, dt, _, acs = _ssd_time_consts(small_ref, sp_ref)
        acs_t = acs.T
        li = _lane_iota((L, L))
        ri = _row_iota((L, L))
        tri = ri >= li
        lo = li < HEAD_DIM
        st_ref[0] = state[...]
        for g in range(2):
            bg = bm[:, 128 * g:128 * g + 128]
            cg = cm[:, 128 * g:128 * g + 128]
            gmat = _dot(cg, bg, "nt")
            for pp in range(4):
                p = 4 * g + pp
                h0, h1 = 2 * p, 2 * p + 1
                x = xs[:, 128 * p:128 * p + 128]
                a0, a1 = acs[:, h0:h0 + 1], acs[:, h1:h1 + 1]
                xdt = x * jnp.where(lo, dt[:, h0:h0 + 1], dt[:, h1:h1 + 1])
                m0 = gmat * jnp.exp(jnp.where(tri, a0 - acs_t[h0:h0 + 1, :], NEG_BIG))
                m1 = gmat * jnp.exp(jnp.where(tri, a1 - acs_t[h1:h1 + 1, :], NEG_BIG))
                yd = _dot(m0, jnp.where(lo, xdt, 0.0)) + _dot(m1, jnp.where(lo, 0.0, xdt))
                hin = state[p]
                yo = _dot(cg, hin, "nt") * jnp.exp(jnp.where(lo, a0, a1))
                dskip = jnp.where(lo[0:1], sp_ref[2:3, h0:h0 + 1], sp_ref[2:3, h1:h1 + 1])
                ypre_ref[:, 128 * p:128 * p + 128] = yd + yo + dskip * x
                al0, al1 = acs[L - 1:L, h0:h0 + 1], acs[L - 1:L, h1:h1 + 1]
                w = jnp.exp(jnp.where(lo, al0 - a0, al1 - a1))
                dec = jnp.exp(jnp.where(ri < HEAD_DIM, al0, al1))
                state[p] = dec * hin + _dot(xdt * w, bg, "tn")
        z = z_ref[...]
        yg = ypre_ref[...] * (z * _sigmoid(z))
        for g in range(2):
            seg = yg[:, 512 * g:512 * g + 512]
            r = lax.rsqrt(jnp.mean(seg * seg, axis=-1, keepdims=True) + NORM_EPS)
            out = (seg * r) * nw_ref[:, 512 * g:512 * g + 512]
            y_ref[:, 512 * g:512 * g + 512] = out.astype(BF16)
            yt_ref[512 * g:512 * g + 512, :] = out.T.astype(BF16)

    row = pl.BlockSpec((L, 1024), lambda c: (c, 0))
    return pl.pallas_call(
        body, name="ssd_fwd", grid=(nc,), in_specs=_ssd_in_specs(),
        out_specs=[row, pl.BlockSpec((1024, L), lambda c: (0, c)), row,
                   pl.BlockSpec((1, N_PAIRS, 128, 128), lambda c: (c, 0, 0, 0))],
        out_shape=[jax.ShapeDtypeStruct((s, 1024), BF16), jax.ShapeDtypeStruct((1024, s), BF16),
                   jax.ShapeDtypeStruct((s, 1024), F32), jax.ShapeDtypeStruct((nc, N_PAIRS, 128, 128), F32)],
        scratch_shapes=[pltpu.VMEM((N_PAIRS, 128, 128), F32)],
        compiler_params=_params(("arbitrary",)),
    )(proj, proj, proj, proj, proj, proj, proj, conv_w8, conv_w8, conv_w8, conv_b, conv_b, conv_b, small, smallp, norm_w)


def ssd_bwd(proj, small, conv_w8, conv_b, smallp, norm_w, ypre, states, dy, sel, swap=()):
    s = proj.shape[0]
    nc = s // SSD_CHUNK
    L = SSD_CHUNK

    ns = len(swap)
    n_in, n_out, n_scratch = 20, 10, 11

    def body(*refs):
        own = refs[:n_in] + refs[n_in + ns:n_in + ns + n_out] + refs[n_in + 2 * ns + n_out:n_in + 2 * ns + n_out + n_scratch]
        if ns:
            start, finish = _pair_swap_phases(refs[n_in:n_in + ns], refs[n_in + ns + n_out:n_in + 2 * ns + n_out],
                                              *refs[n_in + 2 * ns + n_out + n_scratch:])
            pl.when(pl.program_id(0) == 0)(start)
        compute(*own)
        if ns:
            pl.when(pl.program_id(0) == nc - 1)(finish)

    def compute(z_ref, xs_ref, b_ref, c_ref, xsp_ref, bp_ref, cp_ref, wx_ref, wb_ref, wc_ref, bx_ref, bb_ref, bc_ref,
                small_ref, sp_ref, nw_ref, ypre_ref, st_ref, dy_ref, sel_ref,
                dz_ref, dxs_ref, db_ref, dc_ref, dsmall_ref, gwx_ref, gwb_ref, gwc_ref, gsp_ref, gnw_ref,
                dstate, carry_x, carry_b, carry_c, dxs_buf, dbm_buf, dcm_buf, qcs, col_sums, acs_terms, dt_terms):
        step = pl.program_id(0)
        col_sums[...] = jnp.zeros_like(col_sums)
        first_chunk = step == nc - 1
        start = step == 0

        @pl.when(start)
        def _():
            dstate[...] = jnp.zeros_like(dstate)
            carry_x[...] = jnp.zeros_like(carry_x)
            carry_b[...] = jnp.zeros_like(carry_b)
            carry_c[...] = jnp.zeros_like(carry_c)

        xs_sh, xs_pre = _ssd_conv_pre(xs_ref, xsp_ref, wx_ref, bx_ref, first_chunk)
        b_sh, b_pre = _ssd_conv_pre(b_ref, bp_ref, wb_ref, bb_ref, first_chunk)
        c_sh, c_pre = _ssd_conv_pre(c_ref, cp_ref, wc_ref, bc_ref, first_chunk)
        xs, xs_ds = _silu_and_grad(xs_pre)
        bm, b_ds = _silu_and_grad(b_pre)
        cm, c_ds = _silu_and_grad(c_pre)
        dt_pre, dt, a, acs = _ssd_time_consts(small_ref, sp_ref)
        acs_t = acs.T
        li = _lane_iota((L, L))
        ri = _row_iota((L, L))
        tri = ri >= li
        lo = li < HEAD_DIM
        lo_rows = ri < HEAD_DIM
        li1 = _lane_iota((1, L))

        z = z_ref[...]
        sz, dsz = _silu_and_grad(z)
        y = ypre_ref[...]
        yg = y * sz
        dout = dy_ref[...]
        dyg_parts = []
        gnw_parts = []
        for g in range(2):
            sl = slice(512 * g, 512 * g + 512)
            seg = yg[:, sl]
            r = lax.rsqrt(jnp.mean(seg * seg, axis=-1, keepdims=True) + NORM_EPS)
            n = seg * r
            gnw_parts.append(jnp.sum(dout[:, sl] * n, axis=0, keepdims=True))
            gg = dout[:, sl] * nw_ref[:, sl]
            dyg_parts.append(r * (gg - n * jnp.mean(gg * n, axis=-1, keepdims=True)))
        dyg = jnp.concatenate(dyg_parts, axis=1)
        gnw = jnp.concatenate(gnw_parts, axis=1)
        dz_ref[...] = (dyg * y * dsz).astype(BF16)
        dypre = dyg * sz

        qcs[...] = jnp.zeros_like(qcs)
        dalast = jnp.zeros((1, L), F32)
        for g in range(2):
            bg = bm[:, 128 * g:128 * g + 128]
            cg = cm[:, 128 * g:128 * g + 128]
            gmat = _dot(cg, bg, "nt")
            dgmat = jnp.zeros((L, L), F32)
            dbg = jnp.zeros((L, L), F32)
            dcg = jnp.zeros((L, L), F32)
            for pp in range(4):
                p = 4 * g + pp
                h0, h1 = 2 * p, 2 * p + 1
                lanes = slice(128 * p, 128 * p + 128)
                x = xs[:, lanes]
                dyp = dypre[:, lanes]
                a0, a1 = acs[:, h0:h0 + 1], acs[:, h1:h1 + 1]
                dtl = jnp.where(lo, dt[:, h0:h0 + 1], dt[:, h1:h1 + 1])
                xdt = x * dtl
                l0 = jnp.exp(jnp.where(tri, a0 - acs_t[h0:h0 + 1, :], NEG_BIG))
                l1 = jnp.exp(jnp.where(tri, a1 - acs_t[h1:h1 + 1, :], NEG_BIG))
                m0, m1 = gmat * l0, gmat * l1
                dskip = jnp.where(lo[0:1], sp_ref[2:3, h0:h0 + 1], sp_ref[2:3, h1:h1 + 1])
                col_sums[0:1, lanes] = jnp.sum(dyp * x, axis=0, keepdims=True)
                dx = dyp * dskip
                dy0, dy1 = jnp.where(lo, dyp, 0.0), jnp.where(lo, 0.0, dyp)
                x0, x1 = jnp.where(lo, xdt, 0.0), jnp.where(lo, 0.0, xdt)
                dm0, dm1 = _dot(dy0, x0, "nt"), _dot(dy1, x1, "nt")
                dxdt = _dot(m0, dy0, "tn") + _dot(m1, dy1, "tn")
                q0, q1 = dm0 * m0, dm1 * m1
                qcs[h0:h0 + 1, :] = jnp.sum(q0, axis=0, keepdims=True)
                qcs[h1:h1 + 1, :] = jnp.sum(q1, axis=0, keepdims=True)
                row_terms = jnp.where(lo, q0 + pltpu.roll(q0, HEAD_DIM, 1), q1 + pltpu.roll(q1, HEAD_DIM, 1))
                dgmat = dgmat + dm0 * l0 + dm1 * l1
                hin = st_ref[0, p]
                e = jnp.exp(jnp.where(lo, a0, a1))
                ch = _dot(cg, hin, "nt")
                dch = dyp * e
                dcg = dcg + _dot(dch, hin)
                dhin = _dot(dch, cg, "tn")
                dhout = dstate[p]
                al0, al1 = acs[L - 1:L, h0:h0 + 1], acs[L - 1:L, h1:h1 + 1]
                dec = jnp.exp(jnp.where(lo_rows, al0, al1))
                dhin = dhin + dec * dhout
                dal = dhout * hin * dec
                dal0 = _total(jnp.where(lo_rows, dal, 0.0))
                dal1 = _total(dal) - dal0
                dalast = dalast + jnp.where(li1 == h0, dal0, 0.0) + jnp.where(li1 == h1, dal1, 0.0)
                w = jnp.exp(jnp.where(lo, al0 - a0, al1 - a1))
                xw = xdt * w
                dxw = _dot(bg, dhout, "nt")
                dbg = dbg + _dot(xw, dhout)
                dxdt = dxdt + dxw * w
                dww = dxw * xw
                col_sums[1:2, lanes] = jnp.sum(dww, axis=0, keepdims=True)
                acs_terms[:, lanes] = row_terms + dch * ch - dww
                dx = dx + dxdt * dtl
                dt_terms[:, lanes] = dxdt * x
                dxs_buf[:, lanes] = dx
                dstate[p] = dhin
            dcg = dcg + _dot(dgmat, bg)
            dbg = dbg + _dot(dgmat, cg, "tn")
            dbm_buf[:, 128 * g:128 * g + 128] = dbg
            dcm_buf[:, 128 * g:128 * g + 128] = dcg

        head_sums = _split3_dot(col_sums[...], sel_ref[...])
        dskip_g = head_sums[0:1, :]
        dalast = dalast + head_sums[1:2, :]
        ddt = _split3_dot(dt_terms[...], sel_ref[...])
        dacs_tot = _split3_dot(acs_terms[...], sel_ref[...]) - qcs[...].T + jnp.where(ri == L - 1, dalast, 0.0)
        dstep = _rev_cumsum_rows(dacs_tot)
        ddt = ddt + dstep * a
        head_lane = li < N_HEADS
        ddt_pre = jnp.where(head_lane, ddt * _sigmoid(dt_pre), 0.0)
        dsmall_ref[...] = ddt_pre
        da = jnp.sum(jnp.where(head_lane, dstep * dt, 0.0), axis=0, keepdims=True)
        gsp = _stack_rows([jnp.sum(ddt_pre, axis=0, keepdims=True), da * a, dskip_g], L)

        def conv_back(dpost, ds, shifts, w_ref, carry, out_ref, width):
            dpre = dpost * ds
            dext = jnp.concatenate([dpre, carry[...]], axis=0)
            out_ref[...] = _conv_rows_transposed(dext, w_ref[...], SSD_CONV)[:L].astype(BF16)
            carry[...] = dpre[0:8]
            return _conv_weight_grad(dpre, shifts, slice(8, 8 + L), width)

        gwx = conv_back(dxs_buf[...], xs_ds, xs_sh, wx_ref, carry_x, dxs_ref, 1024)
        gwb = conv_back(dbm_buf[...], b_ds, b_sh, wb_ref, carry_b, db_ref, 256)
        gwc = conv_back(dcm_buf[...], c_ds, c_sh, wc_ref, carry_c, dc_ref, 256)

        @pl.when(start)
        def _():
            gwx_ref[...] = gwx
            gwb_ref[...] = gwb
            gwc_ref[...] = gwc
            gsp_ref[...] = gsp
            gnw_ref[...] = gnw

        @pl.when(step > 0)
        def _():
            gwx_ref[...] += gwx
            gwb_ref[...] += gwb
            gwc_ref[...] += gwc
            gsp_ref[...] += gsp
            gnw_ref[...] += gnw

    def ch(c):
        return nc - 1 - c

    row = pl.BlockSpec((L, 1024), lambda c: (ch(c), 0))
    row256 = pl.BlockSpec((L, 256), lambda c: (ch(c), 0))
    in_specs = _ssd_in_specs(rev_nc=nc) + [row, pl.BlockSpec((1, N_PAIRS, 128, 128), lambda c: (ch(c), 0, 0, 0)), row,
                                           pl.BlockSpec((1024, 128), lambda c: (0, 0))]
    out_specs = [row, row, row256, row256, pl.BlockSpec((L, 128), lambda c: (ch(c), 0)),
                 pl.BlockSpec((8, 1024), lambda c: (0, 0)), pl.BlockSpec((8, 256), lambda c: (0, 0)),
                 pl.BlockSpec((8, 256), lambda c: (0, 0)), pl.BlockSpec((8, 128), lambda c: (0, 0)),
                 pl.BlockSpec((1, 1024), lambda c: (0, 0))]
    out_shape = [jax.ShapeDtypeStruct((s, 1024), BF16), jax.ShapeDtypeStruct((s, 1024), BF16),
                 jax.ShapeDtypeStruct((s, 256), BF16), jax.ShapeDtypeStruct((s, 256), BF16),
                 jax.ShapeDtypeStruct((s, 128), F32),
                 jax.ShapeDtypeStruct((8, 1024), F32), jax.ShapeDtypeStruct((8, 256), F32),
                 jax.ShapeDtypeStruct((8, 256), F32), jax.ShapeDtypeStruct((8, 128), F32),
                 jax.ShapeDtypeStruct((1, 1024), F32)]
    scratch = [pltpu.VMEM((N_PAIRS, 128, 128), F32), pltpu.VMEM((8, 1024), F32), pltpu.VMEM((8, 256), F32),
               pltpu.VMEM((8, 256), F32), pltpu.VMEM((L, 1024), F32), pltpu.VMEM((L, 256), F32), pltpu.VMEM((L, 256), F32),
               pltpu.VMEM((L, L), F32), pltpu.VMEM((8, 1024), F32), pltpu.VMEM((L, 1024), F32), pltpu.VMEM((L, 1024), F32)]
    assert (len(in_specs), len(out_specs), len(scratch)) == (n_in, n_out, n_scratch)
    outs = pl.pallas_call(
        body, name="ssd_bwd", grid=(nc,), in_specs=in_specs + [ANY] * ns, out_specs=out_specs + [ANY] * ns,
        out_shape=out_shape + _pair_swap_out_shapes(swap), scratch_shapes=scratch + (_pair_swap_scratch(ns) if ns else []),
        compiler_params=_params(("arbitrary",)),
    )(proj, proj, proj, proj, proj, proj, proj, conv_w8, conv_w8, conv_w8, conv_b, conv_b, conv_b, small, smallp, norm_w,
      ypre, states, dy, sel, *swap)
    return (*outs[:n_out], list(outs[n_out:]))


FOX_SCALE = HEAD_DIM ** -0.5
FOX_T = 256
Q_COL, K_COL, V_COL = 2, 3, 4


def _split_dot(v, m, terms):
    out, rest = None, v
    for i in range(terms):
        piece = rest.astype(BF16)
        out = _dot(piece, m) if out is None else out + _dot(piece, m)
        if i + 1 < terms:
            rest = rest - piece.astype(F32)
    return out


def _split3_dot(v, m):
    return _split_dot(v, m, 3)


def _head_mean(x, sel_ref, selt_ref):
    return _dot(x, sel_ref[...]) * (1.0 / HEAD_DIM)


def _head_spread(v, selt_ref):
    return _split_dot(v, selt_ref[...], 2)


def _head_rstd(x, sel_ref, selt_ref):
    return _head_spread(lax.rsqrt(_head_mean(x * x, sel_ref, selt_ref) + NORM_EPS), selt_ref)


def fox_tables():
    r = np.arange(3 * 128)
    piece, lane = r // 128, r % 128
    head = lane - F_LANE
    is_head = np.logical_and(head >= 0, head < N_HEADS)
    col = 128 * (head // 2) + HEAD_DIM * (1 - head % 2) + piece
    cols = np.arange(1024)
    place_q = np.logical_and(is_head[:, None], cols[None, :] == col[:, None])
    place_k = np.logical_and(is_head[:, None], cols[None, :] == (col + 3)[:, None])
    ones_q = np.logical_and(cols % HEAD_DIM >= 3, cols % HEAD_DIM < 6)[None]
    ones_k = (cols % HEAD_DIM < 3)[None]
    h = np.arange(128) - F_LANE
    ok = np.logical_and(h >= 0, h < N_HEADS)
    same_pair = cols[:, None] // 128 == (h // 2)[None, :]
    fold_even = np.logical_and(np.logical_and(ok, h % 2 == 0)[None, :], same_pair)
    fold_odd = np.logical_and(np.logical_and(ok, h % 2 == 1)[None, :], same_pair)
    as_bf16 = lambda t: jnp.asarray(t.astype(np.float32), BF16)
    return (as_bf16(place_q), as_bf16(place_k), jnp.asarray(ones_q, F32), jnp.asarray(ones_k, F32),
            as_bf16(fold_even), as_bf16(fold_odd))


def fox_prep(proj, small, smallp, qw, kw, sel, selt, place_q, place_k, ones_q, ones_k, *, tm=256):
    s = proj.shape[0]

    def body(q_ref, k_ref, v_ref, small_ref, sp_ref, qw_ref, kw_ref, sel_ref, selt_ref, pq_ref, pk_ref, oq_ref, ok_ref,
             qn_ref, kn_ref, aq_ref, ak_ref, vb_ref, knt_ref, akt_ref, vt_ref, carry):
        @pl.when(pl.program_id(0) == 0)
        def _():
            carry[...] = jnp.zeros_like(carry)

        q = q_ref[...]
        qn_ref[...] = (((q * _head_rstd(q, sel_ref, selt_ref)) * qw_ref[...]) * FOX_SCALE).astype(BF16)
        k = k_ref[...]
        kn = ((k * _head_rstd(k, sel_ref, selt_ref)) * kw_ref[...]).astype(BF16)
        kn_ref[...] = kn
        knt_ref[...] = kn.astype(F32).T.astype(BF16)
        vb_ref[...] = v_ref[...].astype(BF16)
        vt_ref[...] = v_ref[...].T.astype(BF16)
        li = _lane_iota((tm, 128))
        f_lane = jnp.logical_and(li >= F_LANE, li < F_LANE + N_HEADS)
        logf = jnp.where(f_lane, -_softplus(-(small_ref[...] + sp_ref[3:4, :])), 0.0)
        cum = _cumsum_rows(logf) + carry[...]
        carry[...] = cum[tm - 1:tm, :]
        hi = cum.astype(BF16)
        r1 = cum - hi.astype(F32)
        mid = r1.astype(BF16)
        lo = (r1 - mid.astype(F32)).astype(BF16)
        pieces = jnp.concatenate([hi, mid, lo], axis=1)
        aq_ref[...] = (_dot(pieces, pq_ref[...]) + oq_ref[...]).astype(BF16)
        ak = ok_ref[...] - _dot(pieces, pk_ref[...])
        ak_ref[...] = ak.astype(BF16)
        akt_ref[...] = ak.T.astype(BF16)

    row = pl.BlockSpec((tm, 1024), lambda i: (i, 0))
    col = pl.BlockSpec((1024, tm), lambda i: (0, i))
    vec = pl.BlockSpec((1, 1024), lambda i: (0, 0))
    table = pl.BlockSpec((384, 1024), lambda i: (0, 0))
    wide = jax.ShapeDtypeStruct((s, 1024), BF16)
    tall = jax.ShapeDtypeStruct((1024, s), BF16)
    return pl.pallas_call(
        body, name="fox_prep", grid=(s // tm,),
        in_specs=[pl.BlockSpec((tm, 1024), lambda i: (i, Q_COL)), pl.BlockSpec((tm, 1024), lambda i: (i, K_COL)),
                  pl.BlockSpec((tm, 1024), lambda i: (i, V_COL)),
                  pl.BlockSpec((tm, 128), lambda i: (i, 0)), pl.BlockSpec((8, 128), lambda i: (0, 0)), vec, vec,
                  pl.BlockSpec((1024, 128), lambda i: (0, 0)), pl.BlockSpec((128, 1024), lambda i: (0, 0)),
                  table, table, vec, vec],
        out_specs=[row, row, row, row, row, col, col, col],
        out_shape=[wide, wide, wide, wide, wide, tall, tall, tall],
        scratch_shapes=[pltpu.VMEM((1, 128), F32)], compiler_params=_params(("arbitrary",)),
    )(proj, proj, proj, small, smallp, qw, kw, sel, selt, place_q, place_k, ones_q, ones_k)


def fox_fwd(qn, kn, aq, ak, vt, shards=()):
    s = qn.shape[0]
    t = FOX_T
    nq = s // t
    ng = len(shards)

    def body(*refs):
        q_ref, k_ref, aq_ref, ak_ref, vt_ref = refs[:5]
        o_ref, ot_ref, lse_ref = refs[5 + ng:8 + ng]
        p = pl.program_id(0)
        if ng:
            start, forward, finish = _gather_phases(refs[5:5 + ng], refs[8 + ng:8 + 2 * ng], *refs[8 + 2 * ng:])
            pl.when(p == 0)(start)
            pl.when(p == N_PAIRS // 2)(forward)

        @pl.when(p == 0)
        def _():
            lse_ref[...] = jnp.zeros_like(lse_ref)

        lo = _lane_iota((t, 128)) < HEAD_DIM
        lo_rows = _row_iota((128, t)) < HEAD_DIM
        causal_t = _lane_iota((t, t)) >= _row_iota((t, t))

        def q_loop(qi, _):
            q0 = pl.multiple_of(qi * t, t)
            qv, aqv = q_ref[pl.ds(q0, t), :], aq_ref[pl.ds(q0, t), :]
            qa, qb = jnp.where(lo, qv, aqv), jnp.where(lo, aqv, qv)

            def scores(kj):
                k0 = pl.multiple_of(kj * t, t)
                kv, akv = k_ref[pl.ds(k0, t), :], ak_ref[pl.ds(k0, t), :]
                return _dot(jnp.where(lo, kv, akv), qa, "nt"), _dot(jnp.where(lo, akv, kv), qb, "nt")

            def update(kj, stats, s0, s1):
                m0, l0, m1, l1, acc = stats
                vtv = vt_ref[:, pl.ds(pl.multiple_of(kj * t, t), t)]
                n0 = jnp.maximum(m0, jnp.max(s0, axis=0, keepdims=True))
                n1 = jnp.maximum(m1, jnp.max(s1, axis=0, keepdims=True))
                a0, a1 = jnp.exp(m0 - n0), jnp.exp(m1 - n1)
                p0, p1 = jnp.exp(s0 - n0), jnp.exp(s1 - n1)
                l0 = a0 * l0 + jnp.sum(p0, axis=0, keepdims=True)
                l1 = a1 * l1 + jnp.sum(p1, axis=0, keepdims=True)
                acc = (jnp.where(lo_rows, a0, a1) * acc + _dot(jnp.where(lo_rows, vtv, 0.0), p0)
                       + _dot(jnp.where(lo_rows, 0.0, vtv), p1))
                return n0, l0, n1, l1, acc

            def step(kj, carry):
                stats, (s0, s1) = carry[:5], carry[5:]
                nxt = scores(kj + 1)
                return (*update(kj, stats, s0, s1), *nxt)

            def row(val):
                return jnp.full((1, t), val, F32)

            init = (row(NEG_BIG), row(0.0), row(NEG_BIG), row(0.0), jnp.zeros((128, t), F32), *scores(0))
            carry = lax.fori_loop(0, qi, step, init)
            s0, s1 = jnp.where(causal_t, carry[5], NEG_BIG), jnp.where(causal_t, carry[6], NEG_BIG)
            m0, l0, m1, l1, acc = update(qi, carry[:5], s0, s1)
            out_t = acc / jnp.where(lo_rows, l0, l1)
            ot_ref[:, pl.ds(q0, t)] = out_t.astype(BF16)
            o_ref[pl.ds(q0, t), :] = out_t.T.astype(BF16)
            ri = _row_iota((N_HEADS, t))
            old = lse_ref[:, pl.ds(q0, t)]
            lse_ref[:, pl.ds(q0, t)] = jnp.where(
                ri == 2 * p, m0 + jnp.log(l0), jnp.where(ri == 2 * p + 1, m1 + jnp.log(l1), old))
            return 0

        lax.fori_loop(0, nq, q_loop, 0)
        if ng:
            pl.when(p == N_PAIRS - 1)(finish)

    pair = pl.BlockSpec((s, 128), lambda p: (0, p))
    outs = pl.pallas_call(
        body, name="fox_fwd", grid=(N_PAIRS,),
        in_specs=[pair] * 4 + [pl.BlockSpec((128, s), lambda p: (p, 0))] + [ANY] * ng,
        out_specs=[pair, pl.BlockSpec((128, s), lambda p: (p, 0)), pl.BlockSpec((N_HEADS, s), lambda p: (0, 0))] + [ANY] * ng,
        out_shape=[jax.ShapeDtypeStruct((s, 1024), BF16), jax.ShapeDtypeStruct((1024, s), BF16),
                   jax.ShapeDtypeStruct((N_HEADS, s), F32)] + _gather_out_shapes(shards),
        scratch_shapes=_gather_scratch(ng) if ng else [],
        compiler_params=_params(("arbitrary",)),
    )(qn, kn, aq, ak, vt, *shards)
    return outs[0], outs[1], outs[2], list(outs[3:])


def fox_bwd(qn, kn, aq, ak, knt, akt, vb, lse, dmixed, parts=()):
    s = qn.shape[0]
    t = FOX_T
    nq = s // t
    once = pl.Buffered(1)
    ns = len(parts)

    def body(*refs):
        q_ref, k_ref, aq_ref, ak_ref, kt_ref, akt_ref, v_ref, lse_ref, do_ref = refs[:9]
        dq_ref, dk_ref, dv_ref, dc0_ref, dc1_ref = refs[9 + ns:14 + ns]
        p_scr, dp_scr = refs[14 + 2 * ns:16 + 2 * ns]
        p = pl.program_id(0)
        if ns:
            start, finish = _scatter_phases(refs[9:9 + ns], refs[14 + ns:14 + 2 * ns], *refs[16 + 2 * ns:])
            pl.when(p == 0)(start)
        dk_ref[...] = jnp.zeros_like(dk_ref)
        dv_ref[...] = jnp.zeros_like(dv_ref)
        dc0_ref[...] = jnp.zeros_like(dc0_ref)
        dc1_ref[...] = jnp.zeros_like(dc1_ref)
        lo = _lane_iota((t, 128)) < HEAD_DIM
        lo_rows = _row_iota((128, t)) < HEAD_DIM
        causal_t = _lane_iota((t, t)) >= _row_iota((t, t))

        def q_loop(qi, _):
            q0 = pl.multiple_of(qi * t, t)
            qv, aqv = q_ref[pl.ds(q0, t), :], aq_ref[pl.ds(q0, t), :]
            qa, qb = jnp.where(lo, qv, aqv), jnp.where(lo, aqv, qv)
            do = do_ref[pl.ds(q0, t), :]
            doa, dob = jnp.where(lo, do, 0.0).astype(BF16), jnp.where(lo, 0.0, do).astype(BF16)
            lse_blk = lse_ref[:, pl.ds(q0, t)]
            ri = _row_iota((N_HEADS, t))
            lse0 = jnp.sum(jnp.where(ri == 2 * p, lse_blk, 0.0), axis=0, keepdims=True)
            lse1 = jnp.sum(jnp.where(ri == 2 * p + 1, lse_blk, 0.0), axis=0, keepdims=True)

            def scores(kj):
                k0 = pl.multiple_of(kj * t, t)
                kv, akv = k_ref[pl.ds(k0, t), :], ak_ref[pl.ds(k0, t), :]
                return _dot(jnp.where(lo, kv, akv), qa, "nt"), _dot(jnp.where(lo, akv, kv), qb, "nt")

            def pass1(kj, d0, d1, diagonal):
                k0 = pl.multiple_of(kj * t, t)
                vv = v_ref[pl.ds(k0, t), :]
                s0, s1 = scores(kj)
                if diagonal:
                    s0, s1 = jnp.where(causal_t, s0, NEG_BIG), jnp.where(causal_t, s1, NEG_BIG)
                p0, p1 = jnp.exp(s0 - lse0), jnp.exp(s1 - lse1)
                dp0, dp1 = _dot(vv, doa, "nt"), _dot(vv, dob, "nt")
                p_scr[0, kj], p_scr[1, kj] = p0, p1
                dp_scr[0, kj], dp_scr[1, kj] = dp0, dp1
                dv_ref[pl.ds(k0, t), :] += _dot(p0, doa) + _dot(p1, dob)
                return d0 + jnp.sum(p0 * dp0, axis=0, keepdims=True), d1 + jnp.sum(p1 * dp1, axis=0, keepdims=True)

            zero = jnp.zeros((1, t), F32)
            d0, d1 = lax.fori_loop(0, qi, lambda kj, c: pass1(kj, *c, False), (zero, zero))
            d0, d1 = pass1(qi, d0, d1, True)

            def pass2(kj, carry):
                dq0, dq1 = carry
                k0 = pl.multiple_of(kj * t, t)
                p0, p1 = p_scr[0, kj], p_scr[1, kj]
                ds0, ds1 = p0 * (dp_scr[0, kj] - d0), p1 * (dp_scr[1, kj] - d1)
                dk_ref[pl.ds(k0, t), :] += jnp.where(lo, _dot(ds0, qa), _dot(ds1, qb))
                dc0_ref[pl.ds(k0, t), :] += ds0[:, :128] + ds0[:, 128:]
                dc1_ref[pl.ds(k0, t), :] += ds1[:, :128] + ds1[:, 128:]
                ktv, aktv = kt_ref[:, pl.ds(k0, t)], akt_ref[:, pl.ds(k0, t)]
                return dq0 + _dot(jnp.where(lo_rows, ktv, aktv), ds0), dq1 + _dot(jnp.where(lo_rows, aktv, ktv), ds1)

            zq = jnp.zeros((128, t), F32)
            dq0, dq1 = lax.fori_loop(0, qi + 1, pass2, (zq, zq))
            dq_ref[pl.ds(q0, t), :] = jnp.where(lo_rows, dq0, dq1).T
            return 0

        lax.fori_loop(0, nq, q_loop, 0)
        if ns:
            pl.when(p == N_PAIRS - 1)(finish)

    pair = pl.BlockSpec((s, 128), lambda p: (0, p), pipeline_mode=once)
    pair_t = pl.BlockSpec((128, s), lambda p: (p, 0), pipeline_mode=once)
    out = jax.ShapeDtypeStruct((s, 1024), F32)
    outs = pl.pallas_call(
        body, name="fox_bwd", grid=(N_PAIRS,),
        in_specs=[pair, pair, pair, pair, pair_t, pair_t, pair, pl.BlockSpec((N_HEADS, s), lambda p: (0, 0)),
                  pl.BlockSpec((s, 128), lambda p: (0, 8 + p), pipeline_mode=once)] + [ANY] * ns,
        out_specs=[pair] * 5 + [ANY] * ns,
        out_shape=[out] * 5 + [jax.ShapeDtypeStruct(p.shape, p.dtype) for p in parts],
        scratch_shapes=[pltpu.VMEM((2, nq, t, t), F32), pltpu.VMEM((2, nq, t, t), F32)] + (_scatter_scratch(ns) if ns else []),
        compiler_params=_params(("arbitrary",)),
    )(qn, kn, aq, ak, knt, akt, vb, lse, dmixed, *parts)
    return (*outs[:5], _keep_own_blocks(outs[5:], parts))


def fox_post(dqn, dkn, dc0, dc1, proj, small, smallp, qw, kw, sel, selt, fold_even, fold_odd, *, tm=256):
    s = proj.shape[0]
    nrow = s // tm

    def body(dqn_ref, dkn_ref, dc0_ref, dc1_ref, q_ref, k_ref, small_ref, sp_ref, qw_ref, kw_ref, sel_ref, selt_ref,
             fe_ref, fo_ref, dq_ref, dk_ref, dsmall_ref, gqw_ref, gkw_ref, gfb_ref, carry):
        step = pl.program_id(0)

        @pl.when(step == 0)
        def _():
            carry[...] = jnp.zeros_like(carry)

        def norm_bwd(x_ref, w_ref, dn, out_ref):
            x = x_ref[...]
            rf = _head_rstd(x, sel_ref, selt_ref)
            xh = x * rf
            g = dn * w_ref[...]
            mean_gx = _head_spread(_head_mean(g * xh, sel_ref, selt_ref), selt_ref)
            out_ref[...] = (rf * (g - xh * mean_gx)).astype(BF16)
            return jnp.sum(dn * xh, axis=0, keepdims=True)

        gqw = norm_bwd(q_ref, qw_ref, dqn_ref[...] * FOX_SCALE, dq_ref)
        gkw = norm_bwd(k_ref, kw_ref, dkn_ref[...], dk_ref)
        li = _lane_iota((tm, 128))
        f_lane = jnp.logical_and(li >= F_LANE, li < F_LANE + N_HEADS)
        dcum = -(_split3_dot(dc0_ref[...], fe_ref[...]) + _split3_dot(dc1_ref[...], fo_ref[...]))
        dlogf = _rev_cumsum_rows(dcum) + carry[...]
        carry[...] = dlogf[0:1, :]
        dfr = jnp.where(f_lane, dlogf * _sigmoid(-(small_ref[...] + sp_ref[3:4, :])), 0.0)
        dsmall_ref[...] = dfr
        gfb = jnp.sum(dfr, axis=0, keepdims=True)

        @pl.when(step == 0)
        def _():
            gqw_ref[...] = gqw
            gkw_ref[...] = gkw
            gfb_ref[...] = gfb

        @pl.when(step > 0)
        def _():
            gqw_ref[...] += gqw
            gkw_ref[...] += gkw
            gfb_ref[...] += gfb

    def rb(i):
        return nrow - 1 - i

    row = pl.BlockSpec((tm, 1024), lambda i: (rb(i), 0))
    vec = pl.BlockSpec((1, 1024), lambda i: (0, 0))
    fold = pl.BlockSpec((1024, 128), lambda i: (0, 0))
    return pl.pallas_call(
        body, name="fox_post", grid=(nrow,),
        in_specs=[row, row, row, row, pl.BlockSpec((tm, 1024), lambda i: (rb(i), Q_COL)),
                  pl.BlockSpec((tm, 1024), lambda i: (rb(i), K_COL)),
                  pl.BlockSpec((tm, 128), lambda i: (rb(i), 0)), pl.BlockSpec((8, 128), lambda i: (0, 0)), vec, vec,
                  fold, pl.BlockSpec((128, 1024), lambda i: (0, 0)), fold, fold],
        out_specs=[row, row, pl.BlockSpec((tm, 128), lambda i: (rb(i), 0)), vec, vec, pl.BlockSpec((1, 128), lambda i: (0, 0))],
        out_shape=[jax.ShapeDtypeStruct((s, 1024), BF16), jax.ShapeDtypeStruct((s, 1024), BF16),
                   jax.ShapeDtypeStruct((s, 128), F32), jax.ShapeDtypeStruct((1, 1024), F32),
                   jax.ShapeDtypeStruct((1, 1024), F32), jax.ShapeDtypeStruct((1, 128), F32)],
        scratch_shapes=[pltpu.VMEM((1, 128), F32)], compiler_params=_params(("arbitrary",)),
    )(dqn, dkn, dc0, dc1, proj, proj, small, smallp, qw, kw, sel, selt, fold_even, fold_odd)


def local_step(x, target, wm, ws, later_shards, ssd_cw8, ssd_cb, smallp, ssd_nw, qw_t, kw_t, sel, selt,
               norm_mix_w, norm_ffn_w, ffn_cw8, ffn_cb):
    h, h_t = rms_fwd(x, norm_mix_w, name="rms_mix_fwd")
    proj = matmul(h, wm, mode="nn", tm=1024, tn=1408, tk=1024, out_dtype=F32, name="mm_in_proj")
    small = matmul(h, ws, mode="nn", tm=1024, tn=128, tk=1024, out_dtype=F32, name="mm_in_proj_small")
    y_ssd, y_ssd_t, ypre, states = ssd_fwd(proj, small, ssd_cw8, ssd_cb, smallp, ssd_nw)
    place_q, place_k, ones_q, ones_k, fold_even, fold_odd = fox_tables()
    qn, kn, aq, ak, vb, knt, akt, vt = fox_prep(proj, small, smallp, qw_t, kw_t, sel, selt, place_q, place_k, ones_q, ones_k)
    y_fox, y_fox_t, lse, (a_out, a_up, a_down) = fox_fwd(qn, kn, aq, ak, vt, shards=later_shards)
    w_out = a_out.reshape(2048, D_MODEL)
    w_down = a_down.reshape(D_FF, D_MODEL)
    s = x.shape[0]
    shard = lambda index: pl.BlockSpec((None, 1024, 1408), index)
    x1 = matmul(y_ssd, w_out, mode="nn", tm=1024, tn=1024, tk=1024, out_dtype=F32, name="mm_out_ssd", add=x)
    x1 = matmul(y_fox, w_out, mode="nn", tm=1024, tn=1024, tk=1024, out_dtype=F32, name="mm_out_fox", add=x1, b_koff=1)
    hf, hf_t = rms_fwd(x1, norm_ffn_w, name="rms_ffn_fwd")
    hu = matmul(hf, a_up, mode="nn", tm=1024, tn=1408, tk=1024, out_dtype=F32, name="mm_up",
                layout=dict(m=s, n=2 * D_FF, k=D_MODEL, b_spec=shard(lambda i, j, kk: (j, kk, 0))))
    act, act_t = ffn_mid_fwd(hu, ffn_cw8, ffn_cb)
    y = matmul(act, w_down, mode="nn", tm=1024, tn=1024, tk=1408, out_dtype=F32, name="mm_down", add=x1)
    dy, sq = loss_head(y, target)

    dact = matmul(dy, w_down, mode="nt", tm=1024, tn=1408, tk=1024, out_dtype=F32, name="mm_dact")
    g_down = matmul(act_t, dy, mode="nn", tm=1408, tn=1024, tk=1024, out_dtype=BF16, name="mm_dw_down")
    dhu, gcw_g, gcw_v = ffn_mid_bwd(hu, dact, ffn_cw8, ffn_cb)
    dhf = matmul(dhu, a_up, mode="nt", tm=1024, tn=1024, tk=1408, out_dtype=F32, name="mm_dhf",
                 layout=dict(m=s, n=D_MODEL, k=2 * D_FF, a_spec=shard(lambda i, j, kk: (kk // 2, i, kk % 2)),
                             b_spec=shard(lambda i, j, kk: (kk, 0, 0))))
    g_up = matmul(hf_t, dhu, mode="nn", tm=1024, tn=1408, tk=1024, out_dtype=BF16, name="mm_dw_up",
                  layout=dict(m=D_MODEL, n=2 * D_FF, k=s, b_spec=shard(lambda i, j, kk: (j // 2, kk, j % 2)),
                              o_spec=shard(lambda i, j, kk: (j, i, 0)), out_shape=(4, D_MODEL, 1408)))
    dx1, g_norm_ffn = rms_bwd(dhf, x1, norm_ffn_w, dy, name="rms_ffn_bwd")
    dmixed = matmul(dx1, w_out, mode="nt", tm=1024, tn=1024, tk=1024, out_dtype=F32, name="mm_dmixed")
    g_out_a = matmul(y_ssd_t, dx1, mode="nn", tm=1024, tn=1024, tk=1024, out_dtype=BF16, name="mm_dw_out_ssd")
    g_out_b = matmul(y_fox_t, dx1, mode="nn", tm=1024, tn=1024, tk=1024, out_dtype=BF16, name="mm_dw_out_fox")
    early = [jnp.concatenate([g_out_a, g_out_b], axis=0).reshape(4, 512, D_MODEL), g_up, g_down.reshape(4, 704, D_MODEL)]
    dz, dxs, db, dc, dsmall_ssd, gcw_x, gcw_b, gcw_c, g_sp, g_ssd_nw, theirs = ssd_bwd(
        proj, small, ssd_cw8, ssd_cb, smallp, ssd_nw, ypre, states, dmixed, sel, swap=early)
    parts = [add_pair(a, b, name="add_pair_" + n, tr=ADAM_ROWS[n])
             for a, b, n in zip(_own_halves(early), theirs, BIG_NAMES[1:])]
    dqn, dkn, dv, dc0, dc1, landed_early = fox_bwd(qn, kn, aq, ak, knt, akt, vb, lse, dmixed, parts=parts)
    dq, dk, dsmall_fox, g_qw, g_kw, g_fb = fox_post(dqn, dkn, dc0, dc1, proj, small, smallp, qw_t, kw_t, sel, selt,
                                                    fold_even, fold_odd)
    dproj = jnp.concatenate([dz, dxs, dq, dk, dv.astype(BF16), db, dc], axis=1)
    dsmall = (dsmall_ssd + dsmall_fox).astype(BF16)
    g_wm = matmul(h_t, dproj, mode="nn", tm=1024, tn=1408, tk=1024, out_dtype=BF16, name="mm_dw_in")
    g_ws = matmul(h_t, dsmall, mode="nn", tm=1024, tn=128, tk=1024, out_dtype=BF16, name="mm_dw_in_small")
    mine, theirs = pair_swap_halves([_in_grad_shards(g_wm, g_ws)], name="pair_swap_w_in")
    part_in = add_pair(mine[0], theirs[0], name="add_pair_w_in", tr=ADAM_ROWS["w_in"])
    dh, landed_in = matmul(dproj, wm, mode="nt", tm=1024, tn=1024, tk=1408, out_dtype=F32, name="mm_dh", scatter=[part_in])
    dh = matmul(dsmall, ws, mode="nt", tm=1024, tn=1024, tk=128, out_dtype=F32, name="mm_dh_small", add=dh)
    grad_x, g_norm_mix = rms_bwd(dh, x, norm_mix_w, dx1, name="rms_mix_bwd")
    return dict(
        sq=sq, grad_x=grad_x, landed=landed_in + landed_early,
        g_norm_mix=g_norm_mix, g_norm_ffn=g_norm_ffn, g_ssd_nw=g_ssd_nw,
        g_ssd_cw=jnp.concatenate([gcw_x, gcw_b, gcw_c], axis=1), g_sp=g_sp, g_fb=g_fb, g_qw=g_qw, g_kw=g_kw,
        g_ffn_cw=jnp.concatenate([gcw_g, gcw_v], axis=1))


def adamw(w, g, m, v, *, name, tr):
    rows, cols = w.shape

    def body(w_ref, g_ref, m_ref, v_ref, d_ref, mo_ref, vo_ref):
        gv = g_ref[...]
        mn = ADAM_B1 * m_ref[...] + (1.0 - ADAM_B1) * gv
        vn = ADAM_B2 * v_ref[...] + (1.0 - ADAM_B2) * (gv * gv)
        m_hat = mn / (1.0 - ADAM_B1 ** ADAM_STEP)
        v_hat = vn / (1.0 - ADAM_B2 ** ADAM_STEP)
        d_ref[...] = -ADAM_LR * (m_hat / (jnp.sqrt(v_hat) + ADAM_EPS) + ADAM_WD * w_ref[...])
        mo_ref[...] = mn
        vo_ref[...] = vn

    blk = pl.BlockSpec((tr, cols), lambda i: (i, 0))
    shp = jax.ShapeDtypeStruct((rows, cols), F32)
    return pl.pallas_call(
        body, name=name, grid=(rows // tr,), in_specs=[blk] * 4, out_specs=[blk] * 3, out_shape=[shp] * 3,
        compiler_params=_params(("parallel",)),
    )(w, g, m, v)


def add_pair(a, b, *, name, tr):
    _, rows, cols = a.shape

    def body(a_ref, b_ref, o_ref):
        o_ref[...] = (a_ref[...].astype(F32) + b_ref[...].astype(F32)).astype(BF16)

    blk = pl.BlockSpec((1, tr, cols), lambda j, i: (j, i, 0))
    return pl.pallas_call(
        body, name=name, grid=(4, rows // tr), in_specs=[blk, blk], out_specs=blk,
        out_shape=jax.ShapeDtypeStruct(a.shape, BF16), compiler_params=_params(("parallel", "parallel")),
    )(a, b)


def sum_chips(parts, core, *, name, tr):
    _, rows, cols = parts.shape
    nblk = rows // tr

    def body(c_ref, p_ref, o_ref):
        acc = p_ref[0].astype(F32)
        for k in range(1, 4):
            acc = acc + p_ref[k].astype(F32)
        o_ref[...] = acc

    grid_spec = pltpu.PrefetchScalarGridSpec(
        num_scalar_prefetch=1, grid=(nblk,), in_specs=[pl.BlockSpec((4, tr, cols), lambda i, c: (0, i, 0))],
        out_specs=pl.BlockSpec((tr, cols), lambda i, c: (c[0] * nblk + i, 0)))
    return pl.pallas_call(
        body, name=name, grid_spec=grid_spec, out_shape=jax.ShapeDtypeStruct((2 * rows, cols), F32),
        compiler_params=_params(("parallel",)),
    )(core, parts)


ANY = pl.BlockSpec(memory_space=pl.ANY)


def _place():
    x, y, c = lax.axis_index("x"), lax.axis_index("y"), lax.axis_index("c")
    chips = [(1 - x, y), (x, 1 - y), (1 - x, 1 - y)]
    return x, y, c, chips


def _chunks(rows):
    size = next((c for c in (128, 176, 64, 32, 16, 8) if rows % c == 0), rows)
    return [(r, size) for r in range(0, rows, size)]


def gather_weights(shards):
    n = len(shards)

    def body(*refs):
        start, forward, finish = _gather_phases(refs[:n], refs[n:2 * n], *refs[2 * n:])
        start()
        forward()
        finish()

    gathered = pl.pallas_call(
        body, name="gather_weights", in_specs=[ANY] * n, out_specs=[ANY] * n,
        out_shape=_gather_out_shapes(shards), scratch_shapes=_gather_scratch(n),
    )(*shards)
    return gathered


def _gather_out_shapes(shards):
    return [jax.ShapeDtypeStruct((4,) + s.shape, s.dtype) for s in shards]


def _gather_scratch(n):
    return [pltpu.SemaphoreType.DMA((n, 7)), pltpu.SemaphoreType.DMA((n, 7))]


def _gather_phases(ins, outs, send_sems, recv_sems):
    n = len(ins)
    x, y, c, chips = _place()
    me = 2 * x + y
    sibling = (x, y, 1 - c)
    blks = [2 * cx + cy for cx, cy in chips]

    def half(a, blk, r=0, nr=None):
        rows = ins[a].shape[0] // 2
        return outs[a].at[blk, pl.ds(c * rows + r, rows if nr is None else nr), :]

    def to_chip(a, t, r=0, nr=None):
        rows = ins[a].shape[0] // 2
        return pltpu.make_async_remote_copy(
            src_ref=ins[a].at[pl.ds(c * rows + r, rows if nr is None else nr), :], dst_ref=half(a, me, r, nr),
            send_sem=send_sems.at[a, t], recv_sem=recv_sems.at[a, t], device_id=(*chips[t], c), device_id_type=MESH)

    def from_chip(a, t):
        return pltpu.make_async_remote_copy(
            src_ref=half(a, blks[t]), dst_ref=half(a, blks[t]), send_sem=send_sems.at[a, t], recv_sem=recv_sems.at[a, t],
            device_id=(*chips[t], c), device_id_type=MESH)

    def to_sibling(a, t, r=0, nr=None):
        return pltpu.make_async_remote_copy(
            src_ref=half(a, blks[t], r, nr), dst_ref=half(a, blks[t], r, nr), send_sem=send_sems.at[a, 3 + t],
            recv_sem=recv_sems.at[a, 3 + t], device_id=sibling, device_id_type=MESH)

    def from_sibling(a, t):
        rows = ins[a].shape[0] // 2
        dst = outs[a].at[blks[t], pl.ds((1 - c) * rows, rows), :]
        return pltpu.make_async_remote_copy(
            src_ref=dst, dst_ref=dst, send_sem=send_sems.at[a, 3 + t], recv_sem=recv_sems.at[a, 3 + t],
            device_id=sibling, device_id_type=MESH)

    def own(a, r=0, nr=None):
        return pltpu.make_async_remote_copy(
            src_ref=ins[a].at[pl.ds(r, ins[a].shape[0] if nr is None else nr), :],
            dst_ref=outs[a].at[me, pl.ds(r, ins[a].shape[0] if nr is None else nr), :],
            send_sem=send_sems.at[a, 6], recv_sem=recv_sems.at[a, 6], device_id=sibling, device_id_type=MESH)

    def start():
        for a in range(n):
            for t in range(3):
                for r, nr in _chunks(ins[a].shape[0] // 2):
                    to_chip(a, t, r, nr).start()
            for r, nr in _chunks(ins[a].shape[0]):
                own(a, r, nr).start()

    def forward():
        for a in range(n):
            for t in range(3):
                from_chip(a, t).wait_recv()
                for r, nr in _chunks(ins[a].shape[0] // 2):
                    to_sibling(a, t, r, nr).start()

    def finish():
        for a in range(n):
            for t in range(3):
                from_sibling(a, t).wait_recv()
        for a in range(n):
            for t in range(3):
                to_chip(a, t).wait_send()
                to_sibling(a, t).wait_send()
            own(a).wait()

    return start, forward, finish


def pair_swap_halves(grads, *, name):
    n = len(grads)

    def body(*refs):
        start, finish = _pair_swap_phases(refs[:n], refs[n:2 * n], *refs[2 * n:])
        start()
        finish()

    theirs = pl.pallas_call(
        body, name=name, in_specs=[ANY] * n, out_specs=[ANY] * n, out_shape=_pair_swap_out_shapes(grads),
        scratch_shapes=_pair_swap_scratch(n),
    )(*grads)
    return _own_halves(grads), theirs


def _pair_swap_out_shapes(grads):
    return [jax.ShapeDtypeStruct((4, g.shape[1] // 2, g.shape[2]), g.dtype) for g in grads]


def _pair_swap_scratch(n):
    return [pltpu.SemaphoreType.DMA((n,)), pltpu.SemaphoreType.DMA((n,))]


def _own_halves(grads):
    c = lax.axis_index("c")
    return [lax.dynamic_slice_in_dim(g, c * (g.shape[1] // 2), g.shape[1] // 2, axis=1) for g in grads]


def _pair_swap_phases(ins, theirs, send_sems, recv_sems):
    n = len(ins)
    x, y, c, _ = _place()
    sibling = (x, y, 1 - c)

    def start():
        for a in range(n):
            rows = ins[a].shape[1] // 2
            for j in range(4):
                for r, nr in _chunks(rows):
                    pltpu.make_async_remote_copy(
                        src_ref=ins[a].at[j, pl.ds((1 - c) * rows + r, nr), :], dst_ref=theirs[a].at[j, pl.ds(r, nr), :],
                        send_sem=send_sems.at[a], recv_sem=recv_sems.at[a], device_id=sibling, device_id_type=MESH).start()

    def finish():
        for a in range(n):
            pltpu.make_async_remote_copy(src_ref=theirs[a], dst_ref=theirs[a], send_sem=send_sems.at[a],
                                         recv_sem=recv_sems.at[a], device_id=sibling, device_id_type=MESH).wait()

    return start, finish


def _scatter_scratch(n):
    return [pltpu.SemaphoreType.DMA((n, 3)), pltpu.SemaphoreType.DMA((n, 3))]


def _keep_own_blocks(landed, parts):
    if not parts:
        return []
    chip = 2 * lax.axis_index("x") + lax.axis_index("y")
    return [lax.dynamic_update_slice(l, lax.dynamic_slice_in_dim(p, chip, 1, axis=0), (chip, 0, 0))
            for l, p in zip(landed, parts)]


def _scatter_phases(ins, outs, send_sems, recv_sems):
    n = len(ins)
    x, y, c, chips = _place()
    me = 2 * x + y
    blks = [2 * cx + cy for cx, cy in chips]

    def start():
        for a in range(n):
            for r, nr in _chunks(ins[a].shape[1]):
                for t in range(3):
                    pltpu.make_async_remote_copy(
                        src_ref=ins[a].at[blks[t], pl.ds(r, nr), :], dst_ref=outs[a].at[me, pl.ds(r, nr), :],
                        send_sem=send_sems.at[a, t], recv_sem=recv_sems.at[a, t],
                        device_id=(*chips[t], c), device_id_type=MESH).start()

    def finish():
        for a in range(n):
            for t in range(3):
                pltpu.make_async_remote_copy(
                    src_ref=outs[a].at[blks[t]], dst_ref=outs[a].at[blks[t]], send_sem=send_sems.at[a, t],
                    recv_sem=recv_sems.at[a, t], device_id=(*chips[t], c), device_id_type=MESH).wait()

    return start, finish


def pair_join_halves(bufs):
    n = len(bufs)

    def body(*refs):
        outs = refs[n:2 * n]
        send_sems, recv_sems = refs[2 * n:]
        x, y, c, _ = _place()
        sibling = (x, y, 1 - c)
        for a in range(n):
            rows = outs[a].shape[0] // 2
            for r, nr in _chunks(rows):
                mine = outs[a].at[pl.ds(c * rows + r, nr), :]
                pltpu.make_async_remote_copy(src_ref=mine, dst_ref=mine, send_sem=send_sems.at[a], recv_sem=recv_sems.at[a],
                                             device_id=sibling, device_id_type=MESH).start()
        for a in range(n):
            rows = outs[a].shape[0] // 2
            pltpu.make_async_remote_copy(
                src_ref=outs[a].at[pl.ds(c * rows, rows), :], dst_ref=outs[a].at[pl.ds((1 - c) * rows, rows), :],
                send_sem=send_sems.at[a], recv_sem=recv_sems.at[a], device_id=sibling, device_id_type=MESH).wait()

    return pl.pallas_call(
        body, name="pair_join_halves", in_specs=[ANY] * n, out_specs=[ANY] * n,
        out_shape=[jax.ShapeDtypeStruct(b.shape, b.dtype) for b in bufs], input_output_aliases={a: a for a in range(n)},
        scratch_shapes=[pltpu.SemaphoreType.DMA((n,)), pltpu.SemaphoreType.DMA((n,))],
    )(*bufs)


def allreduce_small(packed):
    rows = packed.shape[0]

    def body(in_ref, out_ref, gathered, send_sems, recv_sems):
        x, y, c, _ = _place()
        me = 4 * x + 2 * y + c
        gathered[me] = in_ref[...]
        flips = [(fx, fy, fc) for fx in (0, 1) for fy in (0, 1) for fc in (0, 1)][1:]
        peers = [((1 - x) if fx else x, (1 - y) if fy else y, (1 - c) if fc else c) for fx, fy, fc in flips]
        copies = []
        for t, peer in enumerate(peers):
            cp = pltpu.make_async_remote_copy(
                src_ref=in_ref, dst_ref=gathered.at[me], send_sem=send_sems.at[t], recv_sem=recv_sems.at[t],
                device_id=peer, device_id_type=MESH)
            cp.start()
            copies.append(cp)
        for t, (px, py, pc) in enumerate(peers):
            slot = gathered.at[4 * px + 2 * py + pc]
            pltpu.make_async_remote_copy(
                src_ref=slot, dst_ref=slot, send_sem=send_sems.at[t], recv_sem=recv_sems.at[t],
                device_id=(px, py, pc), device_id_type=MESH).wait_recv()
        for cp in copies:
            cp.wait_send()
        acc = gathered[0]
        for k in range(1, 8):
            acc = acc + gathered[k]
        out_ref[...] = acc

    vm = pl.BlockSpec(memory_space=pltpu.VMEM)
    return pl.pallas_call(
        body, name="allreduce_small", in_specs=[vm], out_specs=vm, out_shape=jax.ShapeDtypeStruct(packed.shape, F32),
        scratch_shapes=[pltpu.VMEM((8, rows, 128), F32), pltpu.SemaphoreType.DMA((7,)), pltpu.SemaphoreType.DMA((7,))],
    )(packed)


SMALL_NAMES = ("norm_mix_w", "ssd_conv_w", "ssd_conv_b", "ssd_dt_bias", "ssd_a_log", "ssd_d", "ssd_norm_w", "fox_f_bias",
               "fox_q_norm_w", "fox_k_norm_w", "norm_ffn_w", "ffn_conv_w", "ffn_conv_b")
BIG_NAMES = ("w_in", "w_out", "w_up", "w_down")
WEIGHT_ORDER = ("norm_mix_w", "w_in", "ssd_conv_w", "ssd_conv_b", "ssd_dt_bias", "ssd_a_log", "ssd_d", "ssd_norm_w",
                "fox_f_bias", "fox_q_norm_w", "fox_k_norm_w", "w_out", "norm_ffn_w", "w_up", "ffn_conv_w", "ffn_conv_b", "w_down")
ADAM_ROWS = {"w_in": 256, "w_out": 256, "w_up": 256, "w_down": 176}


def _pack(arrays):
    pieces = []
    for a in arrays:
        flat = a.reshape(-1).astype(F32)
        pieces += [flat, jnp.zeros(((-flat.shape[0]) % 1024,), F32)]
    return jnp.concatenate(pieces).reshape(-1, 128)


def _unpack(packed, shapes):
    out, r = [], 0
    for shp in shapes:
        size = 1
        for d in shp:
            size *= d
        nrow = 8 * (-(-size // 1024))
        out.append(packed[r:r + nrow].reshape(-1)[:size].reshape(shp))
        r += nrow
    return out


IN_SHARD = IN_COLS // 4
IN_SEGMENTS = ((0, 2048, "main", 0), (2048, 2560, "main", 5120), (2560, 2576, "small", 0), (2576, 5648, "main", 2048),
               (5648, 5664, "small", 16))


def _in_cols(shards, lo, hi):
    out = []
    for j in range(4):
        a, b = max(lo, IN_SHARD * j), min(hi, IN_SHARD * (j + 1))
        if a < b:
            out.append(shards[j][:, a - IN_SHARD * j:b - IN_SHARD * j])
    return out


def _in_grad_shards(g_main, g_small):
    shards = []
    for j in range(4):
        pieces = []
        for lo, hi, src, at in IN_SEGMENTS:
            a, b = max(lo, IN_SHARD * j), min(hi, IN_SHARD * (j + 1))
            if a < b:
                pieces.append((g_main if src == "main" else g_small)[:, at + a - lo:at + b - lo])
        shards.append(jnp.concatenate(pieces, axis=1))
    return jnp.stack(shards)


def _pad_rows(a, rows):
    return jnp.pad(a, ((0, rows - a.shape[0]), (0, 0)))


def kernel(x, norm_mix_w, w_in, ssd_conv_w, ssd_conv_b, ssd_dt_bias, ssd_a_log, ssd_d, ssd_norm_w, fox_f_bias, fox_q_norm_w, fox_k_norm_w, w_out, norm_ffn_w, w_up, ffn_conv_w, ffn_conv_b, w_down, loss_target, m_norm_mix_w, m_w_in, m_ssd_conv_w, m_ssd_conv_b, m_ssd_dt_bias, m_ssd_a_log, m_ssd_d, m_ssd_norm_w, m_fox_f_bias, m_fox_q_norm_w, m_fox_k_norm_w, m_w_out, m_norm_ffn_w, m_w_up, m_ffn_conv_w, m_ffn_conv_b, m_w_down, v_norm_mix_w, v_w_in, v_ssd_conv_w, v_ssd_conv_b, v_ssd_dt_bias, v_ssd_a_log, v_ssd_d, v_ssd_norm_w, v_fox_f_bias, v_fox_q_norm_w, v_fox_k_norm_w, v_w_out, v_norm_ffn_w, v_w_up, v_ffn_conv_w, v_ffn_conv_b, v_w_down):
    w = dict(norm_mix_w=norm_mix_w, w_in=w_in, ssd_conv_w=ssd_conv_w, ssd_conv_b=ssd_conv_b, ssd_dt_bias=ssd_dt_bias,
             ssd_a_log=ssd_a_log, ssd_d=ssd_d, ssd_norm_w=ssd_norm_w, fox_f_bias=fox_f_bias, fox_q_norm_w=fox_q_norm_w,
             fox_k_norm_w=fox_k_norm_w, w_out=w_out, norm_ffn_w=norm_ffn_w, w_up=w_up, ffn_conv_w=ffn_conv_w,
             ffn_conv_b=ffn_conv_b, w_down=w_down)
    m = dict(norm_mix_w=m_norm_mix_w, w_in=m_w_in, ssd_conv_w=m_ssd_conv_w, ssd_conv_b=m_ssd_conv_b, ssd_dt_bias=m_ssd_dt_bias,
             ssd_a_log=m_ssd_a_log, ssd_d=m_ssd_d, ssd_norm_w=m_ssd_norm_w, fox_f_bias=m_fox_f_bias, fox_q_norm_w=m_fox_q_norm_w,
             fox_k_norm_w=m_fox_k_norm_w, w_out=m_w_out, norm_ffn_w=m_norm_ffn_w, w_up=m_w_up, ffn_conv_w=m_ffn_conv_w,
             ffn_conv_b=m_ffn_conv_b, w_down=m_w_down)
    v = dict(norm_mix_w=v_norm_mix_w, w_in=v_w_in, ssd_conv_w=v_ssd_conv_w, ssd_conv_b=v_ssd_conv_b, ssd_dt_bias=v_ssd_dt_bias,
             ssd_a_log=v_ssd_a_log, ssd_d=v_ssd_d, ssd_norm_w=v_ssd_norm_w, fox_f_bias=v_fox_f_bias, fox_q_norm_w=v_fox_q_norm_w,
             fox_k_norm_w=v_fox_k_norm_w, w_out=v_w_out, norm_ffn_w=v_norm_ffn_w, w_up=v_w_up, ffn_conv_w=v_ffn_conv_w,
             ffn_conv_b=v_ffn_conv_b, w_down=v_w_down)
    chip = 2 * lax.axis_index("x") + lax.axis_index("y")

    a_in, a_scw, a_fcw = gather_weights([w_in[0].astype(BF16), _pad_rows(ssd_conv_w[0], 16), _pad_rows(ffn_conv_w[0], 16)])
    later_shards = [w_out[0].astype(BF16), w_up[0].astype(BF16), w_down[0].astype(BF16)]
    wm = jnp.concatenate([p for lo, hi, src, _ in sorted(IN_SEGMENTS, key=lambda seg: seg[3]) if src == "main"
                          for p in _in_cols(a_in, lo, hi)], axis=1)
    ws = jnp.concatenate([p for lo, hi, src, _ in IN_SEGMENTS if src == "small" for p in _in_cols(a_in, lo, hi)]
                         + [jnp.zeros((D_MODEL, SMALL_COLS - 32), BF16)], axis=1)
    ssd_cw8 = a_scw.transpose(1, 0, 2).reshape(16, 1536)[:8]
    ffn_cw8 = a_fcw.transpose(1, 0, 2).reshape(16, 2 * D_FF)[:8]
    smallp = jnp.zeros((8, 128), F32)
    smallp = smallp.at[0, :16].set(ssd_dt_bias[0]).at[1, :16].set(ssd_a_log[0]).at[2, :16].set(ssd_d[0])
    smallp = smallp.at[3, F_LANE:F_LANE + 16].set(fox_f_bias[0])
    qw_t = jnp.tile(fox_q_norm_w[0], N_HEADS)[None]
    kw_t = jnp.tile(fox_k_norm_w[0], N_HEADS)[None]
    sel = jnp.asarray((np.arange(1024)[:, None] // HEAD_DIM == np.arange(128)[None, :]).astype(np.float32), BF16)

    res = local_step(x[0], loss_target[0], wm, ws, later_shards, ssd_cw8, ssd_conv_b, smallp, ssd_norm_w, qw_t, kw_t,
                     sel, sel.T, norm_mix_w, norm_ffn_w, ffn_cw8, ffn_conv_b)

    full_shapes = [(1, 1024), (1, 4, 1536), (1, 1536), (1, 16), (1, 16), (1, 16), (1, 1024), (1, 16), (1, 64), (1, 64),
                   (1, 1024), (1, 3, 2 * D_FF), (1, 2 * D_FF), (1,)]
    local_small = [res["g_norm_mix"], res["g_ssd_cw"][:4], res["g_ssd_cw"][4], res["g_sp"][0, :16], res["g_sp"][1, :16],
                   res["g_sp"][2, :16], res["g_ssd_nw"], res["g_fb"][0, F_LANE:F_LANE + 16],
                   res["g_qw"].reshape(N_HEADS, HEAD_DIM).sum(0), res["g_kw"].reshape(N_HEADS, HEAD_DIM).sum(0),
                   res["g_norm_ffn"], res["g_ffn_cw"][:3], res["g_ffn_cw"][3], jnp.sum(res["sq"])]
    summed = _unpack(allreduce_small(_pack(local_small)), full_shapes)
    loss = (0.5 / D_MODEL) * summed[-1][0]
    g_small = dict(zip(SMALL_NAMES, summed[:-1]))
    g_small["ssd_conv_w"] = lax.dynamic_slice(g_small["ssd_conv_w"], (0, 0, 384 * chip), (1, 4, 384))
    g_small["ffn_conv_w"] = lax.dynamic_slice(g_small["ffn_conv_w"], (0, 0, 1408 * chip), (1, 3, 1408))

    landed = res["landed"]
    core = lax.axis_index("c").astype(jnp.int32).reshape(1)
    halves = [sum_chips(p, core, name="sum_chips_" + n, tr=ADAM_ROWS[n]) for p, n in zip(landed, BIG_NAMES)]
    g_big = dict(zip(BIG_NAMES, pair_join_halves(halves)))

    grads, deltas, new_m, new_v = {}, {}, {}, {}
    for n in BIG_NAMES:
        d, mn, vn = adamw(w[n][0], g_big[n], m[n][0], v[n][0], name="adamw_" + n, tr=ADAM_ROWS[n])
        grads[n], deltas[n], new_m[n], new_v[n] = g_big[n][None], d[None], mn[None], vn[None]
    shapes = [w[n].shape for n in SMALL_NAMES]
    packed_w = _pack([w[n] for n in SMALL_NAMES])
    d, mn, vn = adamw(packed_w, _pack([g_small[n] for n in SMALL_NAMES]), _pack([m[n] for n in SMALL_NAMES]),
                      _pack([v[n] for n in SMALL_NAMES]), name="adamw_small", tr=packed_w.shape[0])
    for n, dd, mm, vv in zip(SMALL_NAMES, _unpack(d, shapes), _unpack(mn, shapes), _unpack(vn, shapes)):
        grads[n], deltas[n], new_m[n], new_v[n] = g_small[n].reshape(w[n].shape), dd, mm, vv
    return (loss, res["grad_x"][None], *[grads[n] for n in WEIGHT_ORDER], *[deltas[n] for n in WEIGHT_ORDER],
            *[new_m[n] for n in WEIGHT_ORDER], *[new_v[n] for n in WEIGHT_ORDER])
```

```python
import functools

import jax
import jax.numpy as jnp
import numpy as np
from jax import lax
from jax.experimental import pallas as pl
from jax.experimental.pallas import tpu as pltpu

F32 = jnp.float32
BF16 = jnp.bfloat16
MESH = pl.DeviceIdType.MESH

D_MODEL = 1024
HEAD_DIM = 64
N_HEADS = 16
N_PAIRS = N_HEADS // 2
SSD_CHUNK = 128
SSD_STATE = 128
SSD_CONV = 4
D_FF = 2816
FFN_CONV = 3
NORM_EPS = 1e-6
MAIN_COLS = 5632
SMALL_COLS = 128
F_LANE = 16
IN_COLS = 5664

ADAM_LR = 0.001
ADAM_B1 = 0.9
ADAM_B2 = 0.999
ADAM_EPS = 1e-08
ADAM_WD = 0.01
ADAM_STEP = 10

VMEM_LIMIT_V7X = 56 * 1024 * 1024
NEG_BIG = -1e30


def _params(sem=None):
    return pltpu.CompilerParams(dimension_semantics=sem, vmem_limit_bytes=VMEM_LIMIT_V7X)


def _sigmoid(x):
    return 1.0 / (1.0 + jnp.exp(-x))


def _silu_and_grad(x):
    s = _sigmoid(x)
    return x * s, s * (1.0 + x * (1.0 - s))


def _shift_down(v, j):
    return v if j == 0 else pltpu.roll(v, j, 0)


def _shift_up(v, j):
    return v if j == 0 else pltpu.roll(v, v.shape[0] - j, 0)


def _row_iota(shape):
    return lax.broadcasted_iota(jnp.int32, shape, 0)


def _lane_iota(shape):
    return lax.broadcasted_iota(jnp.int32, shape, 1)


def _dot(a, b, mode="nn"):
    dims = {"nn": (((1,), (0,)), ((), ())), "nt": (((1,), (1,)), ((), ())), "tn": (((0,), (0,)), ((), ()))}[mode]
    return lax.dot_general(a.astype(BF16), b.astype(BF16), dims, preferred_element_type=F32)


def _dot_f32(a, b):
    return jnp.dot(a, b, precision=lax.Precision.HIGHEST, preferred_element_type=F32)


def matmul(a, b, *, mode, tm, tn, tk, out_dtype, name, add=None, b_koff=0, scatter=(), layout=None):
    layout = layout or {}
    if layout:
        m, n, k = layout["m"], layout["n"], layout["k"]
    else:
        (m, k), n = a.shape, (b.shape[1] if mode == "nn" else b.shape[0])
    assert m % tm == 0 and n % tn == 0 and k % tk == 0, (name, m, n, k, tm, tn, tk)
    nk = k // tk
    grid = (m // tm, n // tn, nk)
    a_spec = layout.get("a_spec") or pl.BlockSpec((tm, tk), lambda i, j, kk: (i, kk))
    b_spec = layout.get("b_spec") or (pl.BlockSpec((tn, tk), lambda i, j, kk: (j, kk + b_koff)) if mode == "nt"
                                      else pl.BlockSpec((tk, tn), lambda i, j, kk: (kk + b_koff, j)))
    o_spec = layout.get("o_spec") or pl.BlockSpec((tm, tn), lambda i, j, kk: (i, j))
    out_struct = jax.ShapeDtypeStruct(layout.get("out_shape", (m, n)), out_dtype)
    has_add = add is not None
    n_in = 3 if has_add else 2
    ns = len(scatter)

    def body(*refs):
        a_ref, b_ref = refs[:2]
        add_ref = refs[2] if has_add else None
        o_ref, acc_ref = refs[n_in + ns], refs[n_in + 2 * ns + 1]
        kk = pl.program_id(2)
        if ns:
            step = (pl.program_id(0) * grid[1] + pl.program_id(1)) * grid[2] + kk
            start, finish_copies = _scatter_phases(refs[n_in:n_in + ns], refs[n_in + ns + 1:n_in + 2 * ns + 1],
                                                   *refs[n_in + 2 * ns + 2:])
            pl.when(step == 0)(start)
        part = _dot(a_ref[...], b_ref[...], mode)

        def finish(total):
            if has_add:
                total = total + add_ref[...]
            o_ref[...] = total.astype(out_dtype)

        if nk == 1:
            finish(part)
        else:
            @pl.when(kk == 0)
            def _():
                acc_ref[...] = part

            @pl.when(jnp.logical_and(kk > 0, kk < nk - 1))
            def _():
                acc_ref[...] += part

            @pl.when(kk == nk - 1)
            def _():
                finish(acc_ref[...] + part)

        if ns:
            pl.when(step == grid[0] * grid[1] * grid[2] - 1)(finish_copies)

    in_specs = [a_spec, b_spec] + ([o_spec] if has_add else [])
    args = (a, b) + ((add,) if has_add else ())
    acc = pltpu.VMEM((tm, tn) if nk > 1 else (8, 128), F32)
    if not ns:
        return pl.pallas_call(
            body, name=name, grid=grid, in_specs=in_specs, out_specs=o_spec, out_shape=out_struct,
            scratch_shapes=[acc], compiler_params=_params(("parallel", "parallel", "arbitrary")),
        )(*args)
    outs = pl.pallas_call(
        body, name=name, grid=grid, in_specs=in_specs + [ANY] * ns, out_specs=[o_spec] + [ANY] * ns,
        out_shape=[out_struct] + [jax.ShapeDtypeStruct(p.shape, p.dtype) for p in scatter],
        scratch_shapes=[acc] + _scatter_scratch(ns), compiler_params=_params(("arbitrary", "arbitrary", "arbitrary")),
    )(*args, *scatter)
    return outs[0], _keep_own_blocks(outs[1:], scatter)


def rms_fwd(x, w, *, name, tm=1024):
    s, d = x.shape

    def body(x_ref, w_ref, h_ref, ht_ref):
        xv = x_ref[...]
        r = lax.rsqrt(jnp.mean(xv * xv, axis=-1, keepdims=True) + NORM_EPS)
        h = (xv * r) * w_ref[...]
        h_ref[...] = h.astype(BF16)
        ht_ref[...] = h.T.astype(BF16)

    return pl.pallas_call(
        body, name=name, grid=(s // tm,),
        in_specs=[pl.BlockSpec((tm, d), lambda i: (i, 0)), pl.BlockSpec((1, d), lambda i: (0, 0))],
        out_specs=[pl.BlockSpec((tm, d), lambda i: (i, 0)), pl.BlockSpec((d, tm), lambda i: (0, i))],
        out_shape=[jax.ShapeDtypeStruct((s, d), BF16), jax.ShapeDtypeStruct((d, s), BF16)],
        compiler_params=_params(("parallel",)),
    )(x, w)


def rms_bwd(dh, x, w, resid, *, name, tm=1024):
    s, d = x.shape

    def body(dh_ref, x_ref, w_ref, res_ref, dx_ref, dw_ref):
        xv = x_ref[...]
        dhv = dh_ref[...]
        r = lax.rsqrt(jnp.mean(xv * xv, axis=-1, keepdims=True) + NORM_EPS)
        xh = xv * r
        g = dhv * w_ref[...]
        dx_ref[...] = res_ref[...] + r * (g - xh * jnp.mean(g * xh, axis=-1, keepdims=True))
        part = jnp.sum(dhv * xh, axis=0, keepdims=True)

        @pl.when(pl.program_id(0) == 0)
        def _():
            dw_ref[...] = part

        @pl.when(pl.program_id(0) > 0)
        def _():
            dw_ref[...] += part

    row = pl.BlockSpec((tm, d), lambda i: (i, 0))
    vec = pl.BlockSpec((1, d), lambda i: (0, 0))
    return pl.pallas_call(
        body, name=name, grid=(s // tm,), in_specs=[row, row, vec, row], out_specs=[row, vec],
        out_shape=[jax.ShapeDtypeStruct((s, d), F32), jax.ShapeDtypeStruct((1, d), F32)],
        compiler_params=_params(("arbitrary",)),
    )(dh, x, w, resid)


def loss_head(y, target, *, tm=1024):
    s, d = y.shape

    def body(y_ref, t_ref, dy_ref, sq_ref):
        e = y_ref[...] - t_ref[...]
        dy_ref[...] = e / float(d)
        part = jnp.sum(e * e, axis=0, keepdims=True)

        @pl.when(pl.program_id(0) == 0)
        def _():
            sq_ref[...] = part

        @pl.when(pl.program_id(0) > 0)
        def _():
            sq_ref[...] += part

    row = pl.BlockSpec((tm, d), lambda i: (i, 0))
    vec = pl.BlockSpec((1, d), lambda i: (0, 0))
    return pl.pallas_call(
        body, name="loss_head", grid=(s // tm,), in_specs=[row, row], out_specs=[row, vec],
        out_shape=[jax.ShapeDtypeStruct((s, d), F32), jax.ShapeDtypeStruct((1, d), F32)],
        compiler_params=_params(("arbitrary",)),
    )(y, target)


def _row_shifts(ext, k_taps):
    return [_shift_down(ext, j) for j in range(k_taps)]


def _conv_rows(shifts, w):
    k_taps = len(shifts)
    acc = w[k_taps - 1:k_taps, :] * shifts[0]
    for k in range(k_taps - 1):
        acc = acc + w[k:k + 1, :] * shifts[k_taps - 1 - k]
    return acc


def _conv_weight_grad(dcur, shifts, rows, width):
    k_taps = len(shifts)
    out = [jnp.sum(dcur * shifts[k_taps - 1 - k][rows], axis=0, keepdims=True) for k in range(k_taps)]
    out.append(jnp.sum(dcur, axis=0, keepdims=True))
    return _stack_rows(out, width)


def _conv_rows_transposed(dext, w, k_taps):
    acc = w[k_taps - 1:k_taps, :] * dext
    for k in range(k_taps - 1):
        acc = acc + w[k:k + 1, :] * _shift_up(dext, k_taps - 1 - k)
    return acc


def _stack_rows(rows, width):
    ri = _row_iota((8, width))
    out = jnp.zeros((8, width), F32)
    for k, r in enumerate(rows):
        out = out + jnp.where(ri == k, r, 0.0)
    return out


def ffn_mid_fwd(hu, conv_w8, conv_b, *, tm=1024, tc=256):
    s = hu.shape[0]
    ncol = D_FF // tc
    r8 = tm // 8

    def body(g_ref, v_ref, gp_ref, vp_ref, wg_ref, wv_ref, bg_ref, bv_ref, o_ref, ot_ref):
        first = pl.program_id(1) == 0

        def conv(cur_ref, prev_ref, w_ref, b_ref):
            prev = jnp.where(first, 0.0, prev_ref[...])
            ext = jnp.concatenate([prev, cur_ref[...]], axis=0)
            return _conv_rows(_row_shifts(ext, FFN_CONV), w_ref[...])[8:] + b_ref[...]

        gc = conv(g_ref, gp_ref, wg_ref, bg_ref)
        vc = conv(v_ref, vp_ref, wv_ref, bv_ref)
        act = gc * _sigmoid(gc) * vc
        o_ref[...] = act.astype(BF16)
        ot_ref[...] = act.T.astype(BF16)

    def prev_idx(i):
        return jnp.maximum(i * r8 - 1, 0)

    in_specs = [
        pl.BlockSpec((tm, tc), lambda j, i: (i, j)),
        pl.BlockSpec((tm, tc), lambda j, i: (i, j + ncol)),
        pl.BlockSpec((8, tc), lambda j, i: (prev_idx(i), j)),
        pl.BlockSpec((8, tc), lambda j, i: (prev_idx(i), j + ncol)),
        pl.BlockSpec((8, tc), lambda j, i: (0, j)),
        pl.BlockSpec((8, tc), lambda j, i: (0, j + ncol)),
        pl.BlockSpec((1, tc), lambda j, i: (0, j)),
        pl.BlockSpec((1, tc), lambda j, i: (0, j + ncol)),
    ]
    return pl.pallas_call(
        body, name="ffn_mid_fwd", grid=(ncol, s // tm), in_specs=in_specs,
        out_specs=[pl.BlockSpec((tm, tc), lambda j, i: (i, j)), pl.BlockSpec((tc, tm), lambda j, i: (j, i))],
        out_shape=[jax.ShapeDtypeStruct((s, D_FF), BF16), jax.ShapeDtypeStruct((D_FF, s), BF16)],
        compiler_params=_params(("parallel", "parallel")),
    )(hu, hu, hu, hu, conv_w8, conv_w8, conv_b, conv_b)


def ffn_mid_bwd(hu, dact, conv_w8, conv_b, *, tm=1024, tc=256):
    s = hu.shape[0]
    ncol = D_FF // tc
    nrow = s // tm
    r8 = tm // 8

    def body(g_ref, v_ref, gp_ref, vp_ref, gn_ref, vn_ref, da_ref, dan_ref, wg_ref, wv_ref, bg_ref, bv_ref,
             dhu_ref, wgo_ref, wvo_ref):
        i = pl.program_id(1)
        first = i == 0
        last = i == nrow - 1

        def ext_of(cur_ref, prev_ref, next_ref):
            prev = jnp.where(first, 0.0, prev_ref[...])
            return jnp.concatenate([prev, cur_ref[...], next_ref[...]], axis=0)

        g_sh = _row_shifts(ext_of(g_ref, gp_ref, gn_ref), FFN_CONV)
        v_sh = _row_shifts(ext_of(v_ref, vp_ref, vn_ref), FFN_CONV)
        gc = _conv_rows(g_sh, wg_ref[...]) + bg_ref[...]
        vc = _conv_rows(v_sh, wv_ref[...]) + bv_ref[...]
        da_ext = jnp.concatenate([jnp.zeros((8, tc), F32), da_ref[...], jnp.where(last, 0.0, dan_ref[...])], axis=0)
        silu, dsilu = _silu_and_grad(gc)
        dgc = da_ext * vc * dsilu
        dvc = da_ext * silu
        dhu_ref[0] = _conv_rows_transposed(dgc, wg_ref[...], FFN_CONV)[8:8 + tm].astype(BF16)
        dhu_ref[1] = _conv_rows_transposed(dvc, wv_ref[...], FFN_CONV)[8:8 + tm].astype(BF16)

        cur = slice(8, 8 + tm)
        pg = _conv_weight_grad(dgc[cur], g_sh, cur, tc)
        pv = _conv_weight_grad(dvc[cur], v_sh, cur, tc)

        @pl.when(first)
        def _():
            wgo_ref[...] = pg
            wvo_ref[...] = pv

        @pl.when(i > 0)
        def _():
            wgo_ref[...] += pg
            wvo_ref[...] += pv

    def prev_idx(i):
        return jnp.maximum(i * r8 - 1, 0)

    def next_idx(i):
        return jnp.minimum((i + 1) * r8, s // 8 - 1)

    cur_g = pl.BlockSpec((tm, tc), lambda j, i: (i, j))
    cur_v = pl.BlockSpec((tm, tc), lambda j, i: (i, j + ncol))
    in_specs = [
        cur_g, cur_v,
        pl.BlockSpec((8, tc), lambda j, i: (prev_idx(i), j)),
        pl.BlockSpec((8, tc), lambda j, i: (prev_idx(i), j + ncol)),
        pl.BlockSpec((8, tc), lambda j, i: (next_idx(i), j)),
        pl.BlockSpec((8, tc), lambda j, i: (next_idx(i), j + ncol)),
        cur_g,
        pl.BlockSpec((8, tc), lambda j, i: (next_idx(i), j)),
        pl.BlockSpec((8, tc), lambda j, i: (0, j)),
        pl.BlockSpec((8, tc), lambda j, i: (0, j + ncol)),
        pl.BlockSpec((1, tc), lambda j, i: (0, j)),
        pl.BlockSpec((1, tc), lambda j, i: (0, j + ncol)),
    ]
    out_specs = [pl.BlockSpec((2, tm, tc), lambda j, i: (0, i, j)), pl.BlockSpec((8, tc), lambda j, i: (0, j)),
                 pl.BlockSpec((8, tc), lambda j, i: (0, j))]
    out_shape = [jax.ShapeDtypeStruct((2, s, D_FF), BF16),
                 jax.ShapeDtypeStruct((8, D_FF), F32), jax.ShapeDtypeStruct((8, D_FF), F32)]
    return pl.pallas_call(
        body, name="ffn_mid_bwd", grid=(ncol, nrow), in_specs=in_specs, out_specs=out_specs, out_shape=out_shape,
        compiler_params=_params(("parallel", "arbitrary")),
    )(hu, hu, hu, hu, hu, hu, dact, dact, conv_w8, conv_w8, conv_b, conv_b)


def _softplus(x):
    return jnp.maximum(x, 0.0) + jnp.log(1.0 + jnp.exp(-jnp.abs(x)))


def _cumsum_rows(v):
    n = v.shape[0]
    ri = _row_iota(v.shape)
    sh = 1
    while sh < n:
        v = v + jnp.where(ri >= sh, _shift_down(v, sh), 0.0)
        sh *= 2
    return v


def _rev_cumsum_rows(v):
    n = v.shape[0]
    ri = _row_iota(v.shape)
    sh = 1
    while sh < n:
        v = v + jnp.where(ri < n - sh, _shift_up(v, sh), 0.0)
        sh *= 2
    return v


def _total(v):
    return jnp.sum(jnp.sum(v, axis=1, keepdims=True), axis=0, keepdims=True)


def _ssd_in_specs(rev_nc=None):
    def ch(c):
        return c if rev_nc is None else rev_nc - 1 - c

    def prev(c):
        return jnp.maximum(ch(c) * (SSD_CHUNK // 8) - 1, 0)

    L = SSD_CHUNK
    return [
        pl.BlockSpec((L, 1024), lambda c: (ch(c), 0)),
        pl.BlockSpec((L, 1024), lambda c: (ch(c), 1)),
        pl.BlockSpec((L, 256), lambda c: (ch(c), 20)),
        pl.BlockSpec((L, 256), lambda c: (ch(c), 21)),
        pl.BlockSpec((8, 1024), lambda c: (prev(c), 1)),
        pl.BlockSpec((8, 256), lambda c: (prev(c), 20)),
        pl.BlockSpec((8, 256), lambda c: (prev(c), 21)),
        pl.BlockSpec((8, 1024), lambda c: (0, 0)),
        pl.BlockSpec((8, 256), lambda c: (0, 4)),
        pl.BlockSpec((8, 256), lambda c: (0, 5)),
        pl.BlockSpec((1, 1024), lambda c: (0, 0)),
        pl.BlockSpec((1, 256), lambda c: (0, 4)),
        pl.BlockSpec((1, 256), lambda c: (0, 5)),
        pl.BlockSpec((L, SMALL_COLS), lambda c: (ch(c), 0)),
        pl.BlockSpec((8, 128), lambda c: (0, 0)),
        pl.BlockSpec((1, 1024), lambda c: (0, 0)),
    ]


def _ssd_conv_pre(cur_ref, prev_ref, w_ref, b_ref, first):
    prev = jnp.where(first, 0.0, prev_ref[...])
    shifts = _row_shifts(jnp.concatenate([prev, cur_ref[...]], axis=0), SSD_CONV)
    return shifts, _conv_rows(shifts, w_ref[...])[8:] + b_ref[...]


def _ssd_time_consts(small_ref, sp_ref):
    dt_pre = small_ref[...] + sp_ref[0:1, :]
    dt = _softplus(dt_pre)
    a = -jnp.exp(sp_ref[1:2, :])
    acs = _cumsum_rows(dt * a)
    return dt_pre, dt, a, acs


def ssd_fwd(proj, small, conv_w8, conv_b, smallp, norm_w):
    s = proj.shape[0]
    nc = s // SSD_CHUNK
    L = SSD_CHUNK

    def body(z_ref, xs_ref, b_ref, c_ref, xsp_ref, bp_ref, cp_ref, wx_ref, wb_ref, wc_ref, bx_ref, bb_ref, bc_ref,
             small_ref, sp_ref, nw_ref, y_ref, yt_ref, ypre_ref, st_ref, state):
        first = pl.program_id(0) == 0

        @pl.when(first)
        def _():
            state[...] = jnp.zeros_like(state)

        xs = _ssd_conv_pre(xs_ref, xsp_ref, wx_ref, bx_ref, first)[1]
        xs = xs * _sigmoid(xs)
        bm = _ssd_conv_pre(b_ref, bp_ref, wb_ref, bb_ref, first)[1]
        bm = bm * _sigmoid(bm)
        cm = _ssd_conv_pre(c_ref, cp_ref, wc_ref, bc_ref, first)[1]
        cm = cm * _sigmoid(cm)
        _, dt, _, acs = _ssd_time_consts(small_ref, sp_ref)
        acs_t = acs.T
        li = _lane_iota((L, L))
        ri = _row_iota((L, L))
        tri = ri >= li
        lo = li < HEAD_DIM
        st_ref[0] = state[...]
        for g in range(2):
            bg = bm[:, 128 * g:128 * g + 128]
            cg = cm[:, 128 * g:128 * g + 128]
            gmat = _dot(cg, bg, "nt")
            for pp in range(4):
                p = 4 * g + pp
                h0, h1 = 2 * p, 2 * p + 1
                x = xs[:, 128 * p:128 * p + 128]
                a0, a1 = acs[:, h0:h0 + 1], acs[:, h1:h1 + 1]
                xdt = x * jnp.where(lo, dt[:, h0:h0 + 1], dt[:, h1:h1 + 1])
                m0 = gmat * jnp.exp(jnp.where(tri, a0 - acs_t[h0:h0 + 1, :], NEG_BIG))
                m1 = gmat * jnp.exp(jnp.where(tri, a1 - acs_t[h1:h1 + 1, :], NEG_BIG))
                yd = _dot(m0, jnp.where(lo, xdt, 0.0)) + _dot(m1, jnp.where(lo, 0.0, xdt))
                hin = state[p]
                yo = _dot(cg, hin, "nt") * jnp.exp(jnp.where(lo, a0, a1))
                dskip = jnp.where(lo[0:1], sp_ref[2:3, h0:h0 + 1], sp_ref[2:3, h1:h1 + 1])
                ypre_ref[:, 128 * p:128 * p + 128] = yd + yo + dskip * x
                al0, al1 = acs[L - 1:L, h0:h0 + 1], acs[L - 1:L, h1:h1 + 1]
                w = jnp.exp(jnp.where(lo, al0 - a0, al1 - a1))
                dec = jnp.exp(jnp.where(ri < HEAD_DIM, al0, al1))
                state[p] = dec * hin + _dot(xdt * w, bg, "tn")
        z = z_ref[...]
        yg = ypre_ref[...] * (z * _sigmoid(z))
        for g in range(2):
            seg = yg[:, 512 * g:512 * g + 512]
            r = lax.rsqrt(jnp.mean(seg * seg, axis=-1, keepdims=True) + NORM_EPS)
            out = (seg * r) * nw_ref[:, 512 * g:512 * g + 512]
            y_ref[:, 512 * g:512 * g + 512] = out.astype(BF16)
            yt_ref[512 * g:512 * g + 512, :] = out.T.astype(BF16)

    row = pl.BlockSpec((L, 1024), lambda c: (c, 0))
    return pl.pallas_call(
        body, name="ssd_fwd", grid=(nc,), in_specs=_ssd_in_specs(),
        out_specs=[row, pl.BlockSpec((1024, L), lambda c: (0, c)), row,
                   pl.BlockSpec((1, N_PAIRS, 128, 128), lambda c: (c, 0, 0, 0))],
        out_shape=[jax.ShapeDtypeStruct((s, 1024), BF16), jax.ShapeDtypeStruct((1024, s), BF16),
                   jax.ShapeDtypeStruct((s, 1024), F32), jax.ShapeDtypeStruct((nc, N_PAIRS, 128, 128), F32)],
        scratch_shapes=[pltpu.VMEM((N_PAIRS, 128, 128), F32)],
        compiler_params=_params(("arbitrary",)),
    )(proj, proj, proj, proj, proj, proj, proj, conv_w8, conv_w8, conv_w8, conv_b, conv_b, conv_b, small, smallp, norm_w)


def ssd_bwd(proj, small, conv_w8, conv_b, smallp, norm_w, ypre, states, dy, sel, swap=()):
    s = proj.shape[0]
    nc = s // SSD_CHUNK
    L = SSD_CHUNK

    ns = len(swap)
    n_in, n_out, n_scratch = 20, 10, 11

    def body(*refs):
        own = refs[:n_in] + refs[n_in + ns:n_in + ns + n_out] + refs[n_in + 2 * ns + n_out:n_in + 2 * ns + n_out + n_scratch]
        if ns:
            start, finish = _pair_swap_phases(refs[n_in:n_in + ns], refs[n_in + ns + n_out:n_in + 2 * ns + n_out],
                                              *refs[n_in + 2 * ns + n_out + n_scratch:])
            pl.when(pl.program_id(0) == 0)(start)
        compute(*own)
        if ns:
            pl.when(pl.program_id(0) == nc - 1)(finish)

    def compute(z_ref, xs_ref, b_ref, c_ref, xsp_ref, bp_ref, cp_ref, wx_ref, wb_ref, wc_ref, bx_ref, bb_ref, bc_ref,
                small_ref, sp_ref, nw_ref, ypre_ref, st_ref, dy_ref, sel_ref,
                dz_ref, dxs_ref, db_ref, dc_ref, dsmall_ref, gwx_ref, gwb_ref, gwc_ref, gsp_ref, gnw_ref,
                dstate, carry_x, carry_b, carry_c, dxs_buf, dbm_buf, dcm_buf, qcs, col_sums, acs_terms, dt_terms):
        step = pl.program_id(0)
        col_sums[...] = jnp.zeros_like(col_sums)
        first_chunk = step == nc - 1
        start = step == 0

        @pl.when(start)
        def _():
            dstate[...] = jnp.zeros_like(dstate)
            carry_x[...] = jnp.zeros_like(carry_x)
            carry_b[...] = jnp.zeros_like(carry_b)
            carry_c[...] = jnp.zeros_like(carry_c)

        xs_sh, xs_pre = _ssd_conv_pre(xs_ref, xsp_ref, wx_ref, bx_ref, first_chunk)
        b_sh, b_pre = _ssd_conv_pre(b_ref, bp_ref, wb_ref, bb_ref, first_chunk)
        c_sh, c_pre = _ssd_conv_pre(c_ref, cp_ref, wc_ref, bc_ref, first_chunk)
        xs, xs_ds = _silu_and_grad(xs_pre)
        bm, b_ds = _silu_and_grad(b_pre)
        cm, c_ds = _silu_and_grad(c_pre)
        dt_pre, dt, a, acs = _ssd_time_consts(small_ref, sp_ref)
        acs_t = acs.T
        li = _lane_iota((L, L))
        ri = _row_iota((L, L))
        tri = ri >= li
        lo = li < HEAD_DIM
        lo_rows = ri < HEAD_DIM
        li1 = _lane_iota((1, L))

        z = z_ref[...]
        sz, dsz = _silu_and_grad(z)
        y = ypre_ref[...]
        yg = y * sz
        dout = dy_ref[...]
        dyg_parts = []
        gnw_parts = []
        for g in range(2):
            sl = slice(512 * g, 512 * g + 512)
            seg = yg[:, sl]
            r = lax.rsqrt(jnp.mean(seg * seg, axis=-1, keepdims=True) + NORM_EPS)
            n = seg * r
            gnw_parts.append(jnp.sum(dout[:, sl] * n, axis=0, keepdims=True))
            gg = dout[:, sl] * nw_ref[:, sl]
            dyg_parts.append(r * (gg - n * jnp.mean(gg * n, axis=-1, keepdims=True)))
        dyg = jnp.concatenate(dyg_parts, axis=1)
        gnw = jnp.concatenate(gnw_parts, axis=1)
        dz_ref[...] = (dyg * y * dsz).astype(BF16)
        dypre = dyg * sz

        qcs[...] = jnp.zeros_like(qcs)
        dalast = jnp.zeros((1, L), F32)
        for g in range(2):
            bg = bm[:, 128 * g:128 * g + 128]
            cg = cm[:, 128 * g:128 * g + 128]
            gmat = _dot(cg, bg, "nt")
            dgmat = jnp.zeros((L, L), F32)
            dbg = jnp.zeros((L, L), F32)
            dcg = jnp.zeros((L, L), F32)
            for pp in range(4):
                p = 4 * g + pp
                h0, h1 = 2 * p, 2 * p + 1
                lanes = slice(128 * p, 128 * p + 128)
                x = xs[:, lanes]
                dyp = dypre[:, lanes]
                a0, a1 = acs[:, h0:h0 + 1], acs[:, h1:h1 + 1]
                dtl = jnp.where(lo, dt[:, h0:h0 + 1], dt[:, h1:h1 + 1])
                xdt = x * dtl
                l0 = jnp.exp(jnp.where(tri, a0 - acs_t[h0:h0 + 1, :], NEG_BIG))
                l1 = jnp.exp(jnp.where(tri, a1 - acs_t[h1:h1 + 1, :], NEG_BIG))
                m0, m1 = gmat * l0, gmat * l1
                dskip = jnp.where(lo[0:1], sp_ref[2:3, h0:h0 + 1], sp_ref[2:3, h1:h1 + 1])
                col_sums[0:1, lanes] = jnp.sum(dyp * x, axis=0, keepdims=True)
                dx = dyp * dskip
                dy0, dy1 = jnp.where(lo, dyp, 0.0), jnp.where(lo, 0.0, dyp)
                x0, x1 = jnp.where(lo, xdt, 0.0), jnp.where(lo, 0.0, xdt)
                dm0, dm1 = _dot(dy0, x0, "nt"), _dot(dy1, x1, "nt")
                dxdt = _dot(m0, dy0, "tn") + _dot(m1, dy1, "tn")
                q0, q1 = dm0 * m0, dm1 * m1
                qcs[h0:h0 + 1, :] = jnp.sum(q0, axis=0, keepdims=True)
                qcs[h1:h1 + 1, :] = jnp.sum(q1, axis=0, keepdims=True)
                row_terms = jnp.where(lo, q0 + pltpu.roll(q0, HEAD_DIM, 1), q1 + pltpu.roll(q1, HEAD_DIM, 1))
                dgmat = dgmat + dm0 * l0 + dm1 * l1
                hin = st_ref[0, p]
                e = jnp.exp(jnp.where(lo, a0, a1))
                ch = _dot(cg, hin, "nt")
                dch = dyp * e
                dcg = dcg + _dot(dch, hin)
                dhin = _dot(dch, cg, "tn")
                dhout = dstate[p]
                al0, al1 = acs[L - 1:L, h0:h0 + 1], acs[L - 1:L, h1:h1 + 1]
                dec = jnp.exp(jnp.where(lo_rows, al0, al1))
                dhin = dhin + dec * dhout
                dal = dhout * hin * dec
                dal0 = _total(jnp.where(lo_rows, dal, 0.0))
                dal1 = _total(dal) - dal0
                dalast = dalast + jnp.where(li1 == h0, dal0, 0.0) + jnp.where(li1 == h1, dal1, 0.0)
                w = jnp.exp(jnp.where(lo, al0 - a0, al1 - a1))
                xw = xdt * w
                dxw = _dot(bg, dhout, "nt")
                dbg = dbg + _dot(xw, dhout)
                dxdt = dxdt + dxw * w
                dww = dxw * xw
                col_sums[1:2, lanes] = jnp.sum(dww, axis=0, keepdims=True)
                acs_terms[:, lanes] = row_terms + dch * ch - dww
                dx = dx + dxdt * dtl
                dt_terms[:, lanes] = dxdt * x
                dxs_buf[:, lanes] = dx
                dstate[p] = dhin
            dcg = dcg + _dot(dgmat, bg)
            dbg = dbg + _dot(dgmat, cg, "tn")
            dbm_buf[:, 128 * g:128 * g + 128] = dbg
            dcm_buf[:, 128 * g:128 * g + 128] = dcg

        head_sums = _split3_dot(col_sums[...], sel_ref[...])
        dskip_g = head_sums[0:1, :]
        dalast = dalast + head_sums[1:2, :]
        ddt = _split3_dot(dt_terms[...], sel_ref[...])
        dacs_tot = _split3_dot(acs_terms[...], sel_ref[...]) - qcs[...].T + jnp.where(ri == L - 1, dalast, 0.0)
        dstep = _rev_cumsum_rows(dacs_tot)
        ddt = ddt + dstep * a
        head_lane = li < N_HEADS
        ddt_pre = jnp.where(head_lane, ddt * _sigmoid(dt_pre), 0.0)
        dsmall_ref[...] = ddt_pre
        da = jnp.sum(jnp.where(head_lane, dstep * dt, 0.0), axis=0, keepdims=True)
        gsp = _stack_rows([jnp.sum(ddt_pre, axis=0, keepdims=True), da * a, dskip_g], L)

        def conv_back(dpost, ds, shifts, w_ref, carry, out_ref, width):
            dpre = dpost * ds
            dext = jnp.concatenate([dpre, carry[...]], axis=0)
            out_ref[...] = _conv_rows_transposed(dext, w_ref[...], SSD_CONV)[:L].astype(BF16)
            carry[...] = dpre[0:8]
            return _conv_weight_grad(dpre, shifts, slice(8, 8 + L), width)

        gwx = conv_back(dxs_buf[...], xs_ds, xs_sh, wx_ref, carry_x, dxs_ref, 1024)
        gwb = conv_back(dbm_buf[...], b_ds, b_sh, wb_ref, carry_b, db_ref, 256)
        gwc = conv_back(dcm_buf[...], c_ds, c_sh, wc_ref, carry_c, dc_ref, 256)

        @pl.when(start)
        def _():
            gwx_ref[...] = gwx
            gwb_ref[...] = gwb
            gwc_ref[...] = gwc
            gsp_ref[...] = gsp
            gnw_ref[...] = gnw

        @pl.when(step > 0)
        def _():
            gwx_ref[...] += gwx
            gwb_ref[...] += gwb
            gwc_ref[...] += gwc
            gsp_ref[...] += gsp
            gnw_ref[...] += gnw

    def ch(c):
        return nc - 1 - c

    row = pl.BlockSpec((L, 1024), lambda c: (ch(c), 0))
    row256 = pl.BlockSpec((L, 256), lambda c: (ch(c), 0))
    in_specs = _ssd_in_specs(rev_nc=nc) + [row, pl.BlockSpec((1, N_PAIRS, 128, 128), lambda c: (ch(c), 0, 0, 0)), row,
                                           pl.BlockSpec((1024, 128), lambda c: (0, 0))]
    out_specs = [row, row, row256, row256, pl.BlockSpec((L, 128), lambda c: (ch(c), 0)),
                 pl.BlockSpec((8, 1024), lambda c: (0, 0)), pl.BlockSpec((8, 256), lambda c: (0, 0)),
                 pl.BlockSpec((8, 256), lambda c: (0, 0)), pl.BlockSpec((8, 128), lambda c: (0, 0)),
                 pl.BlockSpec((1, 1024), lambda c: (0, 0))]
    out_shape = [jax.ShapeDtypeStruct((s, 1024), BF16), jax.ShapeDtypeStruct((s, 1024), BF16),
                 jax.ShapeDtypeStruct((s, 256), BF16), jax.ShapeDtypeStruct((s, 256), BF16),
                 jax.ShapeDtypeStruct((s, 128), F32),
                 jax.ShapeDtypeStruct((8, 1024), F32), jax.ShapeDtypeStruct((8, 256), F32),
                 jax.ShapeDtypeStruct((8, 256), F32), jax.ShapeDtypeStruct((8, 128), F32),
                 jax.ShapeDtypeStruct((1, 1024), F32)]
    scratch = [pltpu.VMEM((N_PAIRS, 128, 128), F32), pltpu.VMEM((8, 1024), F32), pltpu.VMEM((8, 256), F32),
               pltpu.VMEM((8, 256), F32), pltpu.VMEM((L, 1024), F32), pltpu.VMEM((L, 256), F32), pltpu.VMEM((L, 256), F32),
               pltpu.VMEM((L, L), F32), pltpu.VMEM((8, 1024), F32), pltpu.VMEM((L, 1024), F32), pltpu.VMEM((L, 1024), F32)]
    assert (len(in_specs), len(out_specs), len(scratch)) == (n_in, n_out, n_scratch)
    outs = pl.pallas_call(
        body, name="ssd_bwd", grid=(nc,), in_specs=in_specs + [ANY] * ns, out_specs=out_specs + [ANY] * ns,
        out_shape=out_shape + _pair_swap_out_shapes(swap), scratch_shapes=scratch + (_pair_swap_scratch(ns) if ns else []),
        compiler_params=_params(("arbitrary",)),
    )(proj, proj, proj, proj, proj, proj, proj, conv_w8, conv_w8, conv_w8, conv_b, conv_b, conv_b, small, smallp, norm_w,
      ypre, states, dy, sel, *swap)
    return (*outs[:n_out], list(outs[n_out:]))


FOX_SCALE = HEAD_DIM ** -0.5
FOX_T = 256
Q_COL, K_COL, V_COL = 2, 3, 4


def _split_dot(v, m, terms):
    out, rest = None, v
    for i in range(terms):
        piece = rest.astype(BF16)
        out = _dot(piece, m) if out is None else out + _dot(piece, m)
        if i + 1 < terms:
            rest = rest - piece.astype(F32)
    return out


def _split3_dot(v, m):
    return _split_dot(v, m, 3)


def _head_mean(x, sel_ref, selt_ref):
    return _dot(x, sel_ref[...]) * (1.0 / HEAD_DIM)


def _head_spread(v, selt_ref):
    return _split_dot(v, selt_ref[...], 2)


def _head_rstd(x, sel_ref, selt_ref):
    return _head_spread(lax.rsqrt(_head_mean(x * x, sel_ref, selt_ref) + NORM_EPS), selt_ref)


def fox_tables():
    r = np.arange(3 * 128)
    piece, lane = r // 128, r % 128
    head = lane - F_LANE
    is_head = np.logical_and(head >= 0, head < N_HEADS)
    col = 128 * (head // 2) + HEAD_DIM * (1 - head % 2) + piece
    cols = np.arange(1024)
    place_q = np.logical_and(is_head[:, None], cols[None, :] == col[:, None])
    place_k = np.logical_and(is_head[:, None], cols[None, :] == (col + 3)[:, None])
    ones_q = np.logical_and(cols % HEAD_DIM >= 3, cols % HEAD_DIM < 6)[None]
    ones_k = (cols % HEAD_DIM < 3)[None]
    h = np.arange(128) - F_LANE
    ok = np.logical_and(h >= 0, h < N_HEADS)
    same_pair = cols[:, None] // 128 == (h // 2)[None, :]
    fold_even = np.logical_and(np.logical_and(ok, h % 2 == 0)[None, :], same_pair)
    fold_odd = np.logical_and(np.logical_and(ok, h % 2 == 1)[None, :], same_pair)
    as_bf16 = lambda t: jnp.asarray(t.astype(np.float32), BF16)
    return (as_bf16(place_q), as_bf16(place_k), jnp.asarray(ones_q, F32), jnp.asarray(ones_k, F32),
            as_bf16(fold_even), as_bf16(fold_odd))


def fox_prep(proj, small, smallp, qw, kw, sel, selt, place_q, place_k, ones_q, ones_k, *, tm=256):
    s = proj.shape[0]

    def body(q_ref, k_ref, v_ref, small_ref, sp_ref, qw_ref, kw_ref, sel_ref, selt_ref, pq_ref, pk_ref, oq_ref, ok_ref,
             qn_ref, kn_ref, aq_ref, ak_ref, vb_ref, knt_ref, akt_ref, vt_ref, carry):
        @pl.when(pl.program_id(0) == 0)
        def _():
            carry[...] = jnp.zeros_like(carry)

        q = q_ref[...]
        qn_ref[...] = (((q * _head_rstd(q, sel_ref, selt_ref)) * qw_ref[...]) * FOX_SCALE).astype(BF16)
        k = k_ref[...]
        kn = ((k * _head_rstd(k, sel_ref, selt_ref)) * kw_ref[...]).astype(BF16)
        kn_ref[...] = kn
        knt_ref[...] = kn.astype(F32).T.astype(BF16)
        vb_ref[...] = v_ref[...].astype(BF16)
        vt_ref[...] = v_ref[...].T.astype(BF16)
        li = _lane_iota((tm, 128))
        f_lane = jnp.logical_and(li >= F_LANE, li < F_LANE + N_HEADS)
        logf = jnp.where(f_lane, -_softplus(-(small_ref[...] + sp_ref[3:4, :])), 0.0)
        cum = _cumsum_rows(logf) + carry[...]
        carry[...] = cum[tm - 1:tm, :]
        hi = cum.astype(BF16)
        r1 = cum - hi.astype(F32)
        mid = r1.astype(BF16)
        lo = (r1 - mid.astype(F32)).astype(BF16)
        pieces = jnp.concatenate([hi, mid, lo], axis=1)
        aq_ref[...] = (_dot(pieces, pq_ref[...]) + oq_ref[...]).astype(BF16)
        ak = ok_ref[...] - _dot(pieces, pk_ref[...])
        ak_ref[...] = ak.astype(BF16)
        akt_ref[...] = ak.T.astype(BF16)

    row = pl.BlockSpec((tm, 1024), lambda i: (i, 0))
    col = pl.BlockSpec((1024, tm), lambda i: (0, i))
    vec = pl.BlockSpec((1, 1024), lambda i: (0, 0))
    table = pl.BlockSpec((384, 1024), lambda i: (0, 0))
    wide = jax.ShapeDtypeStruct((s, 1024), BF16)
    tall = jax.ShapeDtypeStruct((1024, s), BF16)
    return pl.pallas_call(
        body, name="fox_prep", grid=(s // tm,),
        in_specs=[pl.BlockSpec((tm, 1024), lambda i: (i, Q_COL)), pl.BlockSpec((tm, 1024), lambda i: (i, K_COL)),
                  pl.BlockSpec((tm, 1024), lambda i: (i, V_COL)),
                  pl.BlockSpec((tm, 128), lambda i: (i, 0)), pl.BlockSpec((8, 128), lambda i: (0, 0)), vec, vec,
                  pl.BlockSpec((1024, 128), lambda i: (0, 0)), pl.BlockSpec((128, 1024), lambda i: (0, 0)),
                  table, table, vec, vec],
        out_specs=[row, row, row, row, row, col, col, col],
        out_shape=[wide, wide, wide, wide, wide, tall, tall, tall],
        scratch_shapes=[pltpu.VMEM((1, 128), F32)], compiler_params=_params(("arbitrary",)),
    )(proj, proj, proj, small, smallp, qw, kw, sel, selt, place_q, place_k, ones_q, ones_k)


def fox_fwd(qn, kn, aq, ak, vt, shards=()):
    s = qn.shape[0]
    t = FOX_T
    nq = s // t
    ng = len(shards)

    def body(*refs):
        q_ref, k_ref, aq_ref, ak_ref, vt_ref = refs[:5]
        o_ref, ot_ref, lse_ref = refs[5 + ng:8 + ng]
        p = pl.program_id(0)
        if ng:
            start, forward, finish = _gather_phases(refs[5:5 + ng], refs[8 + ng:8 + 2 * ng], *refs[8 + 2 * ng:])
            pl.when(p == 0)(start)
            pl.when(p == N_PAIRS // 2)(forward)

        @pl.when(p == 0)
        def _():
            lse_ref[...] = jnp.zeros_like(lse_ref)

        lo = _lane_iota((t, 128)) < HEAD_DIM
        lo_rows = _row_iota((128, t)) < HEAD_DIM
        causal_t = _lane_iota((t, t)) >= _row_iota((t, t))

        def q_loop(qi, _):
            q0 = pl.multiple_of(qi * t, t)
            qv, aqv = q_ref[pl.ds(q0, t), :], aq_ref[pl.ds(q0, t), :]
            qa, qb = jnp.where(lo, qv, aqv), jnp.where(lo, aqv, qv)

            def scores(kj):
                k0 = pl.multiple_of(kj * t, t)
                kv, akv = k_ref[pl.ds(k0, t), :], ak_ref[pl.ds(k0, t), :]
                return _dot(jnp.where(lo, kv, akv), qa, "nt"), _dot(jnp.where(lo, akv, kv), qb, "nt")

            def update(kj, stats, s0, s1):
                m0, l0, m1, l1, acc = stats
                vtv = vt_ref[:, pl.ds(pl.multiple_of(kj * t, t), t)]
                n0 = jnp.maximum(m0, jnp.max(s0, axis=0, keepdims=True))
                n1 = jnp.maximum(m1, jnp.max(s1, axis=0, keepdims=True))
                a0, a1 = jnp.exp(m0 - n0), jnp.exp(m1 - n1)
                p0, p1 = jnp.exp(s0 - n0), jnp.exp(s1 - n1)
                l0 = a0 * l0 + jnp.sum(p0, axis=0, keepdims=True)
                l1 = a1 * l1 + jnp.sum(p1, axis=0, keepdims=True)
                acc = (jnp.where(lo_rows, a0, a1) * acc + _dot(jnp.where(lo_rows, vtv, 0.0), p0)
                       + _dot(jnp.where(lo_rows, 0.0, vtv), p1))
                return n0, l0, n1, l1, acc

            def step(kj, carry):
                stats, (s0, s1) = carry[:5], carry[5:]
                nxt = scores(kj + 1)
                return (*update(kj, stats, s0, s1), *nxt)

            def row(val):
                return jnp.full((1, t), val, F32)

            init = (row(NEG_BIG), row(0.0), row(NEG_BIG), row(0.0), jnp.zeros((128, t), F32), *scores(0))
            carry = lax.fori_loop(0, qi, step, init)
            s0, s1 = jnp.where(causal_t, carry[5], NEG_BIG), jnp.where(causal_t, carry[6], NEG_BIG)
            m0, l0, m1, l1, acc = update(qi, carry[:5], s0, s1)
            out_t = acc / jnp.where(lo_rows, l0, l1)
            ot_ref[:, pl.ds(q0, t)] = out_t.astype(BF16)
            o_ref[pl.ds(q0, t), :] = out_t.T.astype(BF16)
            ri = _row_iota((N_HEADS, t))
            old = lse_ref[:, pl.ds(q0, t)]
            lse_ref[:, pl.ds(q0, t)] = jnp.where(
                ri == 2 * p, m0 + jnp.log(l0), jnp.where(ri == 2 * p + 1, m1 + jnp.log(l1), old))
            return 0

        lax.fori_loop(0, nq, q_loop, 0)
        if ng:
            pl.when(p == N_PAIRS - 1)(finish)

    pair = pl.BlockSpec((s, 128), lambda p: (0, p))
    outs = pl.pallas_call(
        body, name="fox_fwd", grid=(N_PAIRS,),
        in_specs=[pair] * 4 + [pl.BlockSpec((128, s), lambda p: (p, 0))] + [ANY] * ng,
        out_specs=[pair, pl.BlockSpec((128, s), lambda p: (p, 0)), pl.BlockSpec((N_HEADS, s), lambda p: (0, 0))] + [ANY] * ng,
        out_shape=[jax.ShapeDtypeStruct((s, 1024), BF16), jax.ShapeDtypeStruct((1024, s), BF16),
                   jax.ShapeDtypeStruct((N_HEADS, s), F32)] + _gather_out_shapes(shards),
        scratch_shapes=_gather_scratch(ng) if ng else [],
        compiler_params=_params(("arbitrary",)),
    )(qn, kn, aq, ak, vt, *shards)
    return outs[0], outs[1], outs[2], list(outs[3:])


def fox_bwd(qn, kn, aq, ak, knt, akt, vb, lse, dmixed, parts=()):
    s = qn.shape[0]
    t = FOX_T
    nq = s // t
    once = pl.Buffered(1)
    ns = len(parts)

    def body(*refs):
        q_ref, k_ref, aq_ref, ak_ref, kt_ref, akt_ref, v_ref, lse_ref, do_ref = refs[:9]
        dq_ref, dk_ref, dv_ref, dc0_ref, dc1_ref = refs[9 + ns:14 + ns]
        p_scr, dp_scr = refs[14 + 2 * ns:16 + 2 * ns]
        p = pl.program_id(0)
        if ns:
            start, finish = _scatter_phases(refs[9:9 + ns], refs[14 + ns:14 + 2 * ns], *refs[16 + 2 * ns:])
            pl.when(p == 0)(start)
        dk_ref[...] = jnp.zeros_like(dk_ref)
        dv_ref[...] = jnp.zeros_like(dv_ref)
        dc0_ref[...] = jnp.zeros_like(dc0_ref)
        dc1_ref[...] = jnp.zeros_like(dc1_ref)
        lo = _lane_iota((t, 128)) < HEAD_DIM
        lo_rows = _row_iota((128, t)) < HEAD_DIM
        causal_t = _lane_iota((t, t)) >= _row_iota((t, t))

        def q_loop(qi, _):
            q0 = pl.multiple_of(qi * t, t)
            qv, aqv = q_ref[pl.ds(q0, t), :], aq_ref[pl.ds(q0, t), :]
            qa, qb = jnp.where(lo, qv, aqv), jnp.where(lo, aqv, qv)
            do = do_ref[pl.ds(q0, t), :]
            doa, dob = jnp.where(lo, do, 0.0).astype(BF16), jnp.where(lo, 0.0, do).astype(BF16)
            lse_blk = lse_ref[:, pl.ds(q0, t)]
            ri = _row_iota((N_HEADS, t))
            lse0 = jnp.sum(jnp.where(ri == 2 * p, lse_blk, 0.0), axis=0, keepdims=True)
            lse1 = jnp.sum(jnp.where(ri == 2 * p + 1, lse_blk, 0.0), axis=0, keepdims=True)

            def scores(kj):
                k0 = pl.multiple_of(kj * t, t)
                kv, akv = k_ref[pl.ds(k0, t), :], ak_ref[pl.ds(k0, t), :]
                return _dot(jnp.where(lo, kv, akv), qa, "nt"), _dot(jnp.where(lo, akv, kv), qb, "nt")

            def pass1(kj, d0, d1, diagonal):
                k0 = pl.multiple_of(kj * t, t)
                vv = v_ref[pl.ds(k0, t), :]
                s0, s1 = scores(kj)
                if diagonal:
                    s0, s1 = jnp.where(causal_t, s0, NEG_BIG), jnp.where(causal_t, s1, NEG_BIG)
                p0, p1 = jnp.exp(s0 - lse0), jnp.exp(s1 - lse1)
                dp0, dp1 = _dot(vv, doa, "nt"), _dot(vv, dob, "nt")
                p_scr[0, kj], p_scr[1, kj] = p0, p1
                dp_scr[0, kj], dp_scr[1, kj] = dp0, dp1
                dv_ref[pl.ds(k0, t), :] += _dot(p0, doa) + _dot(p1, dob)
                return d0 + jnp.sum(p0 * dp0, axis=0, keepdims=True), d1 + jnp.sum(p1 * dp1, axis=0, keepdims=True)

            zero = jnp.zeros((1, t), F32)
            d0, d1 = lax.fori_loop(0, qi, lambda kj, c: pass1(kj, *c, False), (zero, zero))
            d0, d1 = pass1(qi, d0, d1, True)

            def pass2(kj, carry):
                dq0, dq1 = carry
                k0 = pl.multiple_of(kj * t, t)
                p0, p1 = p_scr[0, kj], p_scr[1, kj]
                ds0, ds1 = p0 * (dp_scr[0, kj] - d0), p1 * (dp_scr[1, kj] - d1)
                dk_ref[pl.ds(k0, t), :] += jnp.where(lo, _dot(ds0, qa), _dot(ds1, qb))
                dc0_ref[pl.ds(k0, t), :] += ds0[:, :128] + ds0[:, 128:]
                dc1_ref[pl.ds(k0, t), :] += ds1[:, :128] + ds1[:, 128:]
                ktv, aktv = kt_ref[:, pl.ds(k0, t)], akt_ref[:, pl.ds(k0, t)]
                return dq0 + _dot(jnp.where(lo_rows, ktv, aktv), ds0), dq1 + _dot(jnp.where(lo_rows, aktv, ktv), ds1)

            zq = jnp.zeros((128, t), F32)
            dq0, dq1 = lax.fori_loop(0, qi + 1, pass2, (zq, zq))
            dq_ref[pl.ds(q0, t), :] = jnp.where(lo_rows, dq0, dq1).T
            return 0

        lax.fori_loop(0, nq, q_loop, 0)
        if ns:
            pl.when(p == N_PAIRS - 1)(finish)

    pair = pl.BlockSpec((s, 128), lambda p: (0, p))
    pair_t = pl.BlockSpec((128, s), lambda p: (p, 0))
    out = jax.ShapeDtypeStruct((s, 1024), F32)
    outs = pl.pallas_call(
        body, name="fox_bwd", grid=(N_PAIRS,),
        in_specs=[pair, pair, pair, pair, pair_t, pair_t, pair, pl.BlockSpec((N_HEADS, s), lambda p: (0, 0)),
                  pl.BlockSpec((s, 128), lambda p: (0, 8 + p))] + [ANY] * ns,
        out_specs=[pl.BlockSpec((s, 128), lambda p: (0, p), pipeline_mode=once)] * 5 + [ANY] * ns,
        out_shape=[out] * 5 + [jax.ShapeDtypeStruct(p.shape, p.dtype) for p in parts],
        scratch_shapes=[pltpu.VMEM((2, nq, t, t), F32), pltpu.VMEM((2, nq, t, t), F32)] + (_scatter_scratch(ns) if ns else []),
        compiler_params=_params(("arbitrary",)),
    )(qn, kn, aq, ak, knt, akt, vb, lse, dmixed, *parts)
    return (*outs[:5], _keep_own_blocks(outs[5:], parts))


def fox_post(dqn, dkn, dc0, dc1, proj, small, smallp, qw, kw, sel, selt, fold_even, fold_odd, *, tm=256):
    s = proj.shape[0]
    nrow = s // tm

    def body(dqn_ref, dkn_ref, dc0_ref, dc1_ref, q_ref, k_ref, small_ref, sp_ref, qw_ref, kw_ref, sel_ref, selt_ref,
             fe_ref, fo_ref, dq_ref, dk_ref, dsmall_ref, gqw_ref, gkw_ref, gfb_ref, carry):
        step = pl.program_id(0)

        @pl.when(step == 0)
        def _():
            carry[...] = jnp.zeros_like(carry)

        def norm_bwd(x_ref, w_ref, dn, out_ref):
            x = x_ref[...]
            rf = _head_rstd(x, sel_ref, selt_ref)
            xh = x * rf
            g = dn * w_ref[...]
            mean_gx = _head_spread(_head_mean(g * xh, sel_ref, selt_ref), selt_ref)
            out_ref[...] = (rf * (g - xh * mean_gx)).astype(BF16)
            return jnp.sum(dn * xh, axis=0, keepdims=True)

        gqw = norm_bwd(q_ref, qw_ref, dqn_ref[...] * FOX_SCALE, dq_ref)
        gkw = norm_bwd(k_ref, kw_ref, dkn_ref[...], dk_ref)
        li = _lane_iota((tm, 128))
        f_lane = jnp.logical_and(li >= F_LANE, li < F_LANE + N_HEADS)
        dcum = -(_split3_dot(dc0_ref[...], fe_ref[...]) + _split3_dot(dc1_ref[...], fo_ref[...]))
        dlogf = _rev_cumsum_rows(dcum) + carry[...]
        carry[...] = dlogf[0:1, :]
        dfr = jnp.where(f_lane, dlogf * _sigmoid(-(small_ref[...] + sp_ref[3:4, :])), 0.0)
        dsmall_ref[...] = dfr
        gfb = jnp.sum(dfr, axis=0, keepdims=True)

        @pl.when(step == 0)
        def _():
            gqw_ref[...] = gqw
            gkw_ref[...] = gkw
            gfb_ref[...] = gfb

        @pl.when(step > 0)
        def _():
            gqw_ref[...] += gqw
            gkw_ref[...] += gkw
            gfb_ref[...] += gfb

    def rb(i):
        return nrow - 1 - i

    row = pl.BlockSpec((tm, 1024), lambda i: (rb(i), 0))
    vec = pl.BlockSpec((1, 1024), lambda i: (0, 0))
    fold = pl.BlockSpec((1024, 128), lambda i: (0, 0))
    return pl.pallas_call(
        body, name="fox_post", grid=(nrow,),
        in_specs=[row, row, row, row, pl.BlockSpec((tm, 1024), lambda i: (rb(i), Q_COL)),
                  pl.BlockSpec((tm, 1024), lambda i: (rb(i), K_COL)),
                  pl.BlockSpec((tm, 128), lambda i: (rb(i), 0)), pl.BlockSpec((8, 128), lambda i: (0, 0)), vec, vec,
                  fold, pl.BlockSpec((128, 1024), lambda i: (0, 0)), fold, fold],
        out_specs=[row, row, pl.BlockSpec((tm, 128), lambda i: (rb(i), 0)), vec, vec, pl.BlockSpec((1, 128), lambda i: (0, 0))],
        out_shape=[jax.ShapeDtypeStruct((s, 1024), BF16), jax.ShapeDtypeStruct((s, 1024), BF16),
                   jax.ShapeDtypeStruct((s, 128), F32), jax.ShapeDtypeStruct((1, 1024), F32),
                   jax.ShapeDtypeStruct((1, 1024), F32), jax.ShapeDtypeStruct((1, 128), F32)],
        scratch_shapes=[pltpu.VMEM((1, 128), F32)], compiler_params=_params(("arbitrary",)),
    )(dqn, dkn, dc0, dc1, proj, proj, small, smallp, qw, kw, sel, selt, fold_even, fold_odd)


def local_step(x, target, wm, ws, later_shards, ssd_cw8, ssd_cb, smallp, ssd_nw, qw_t, kw_t, sel, selt,
               norm_mix_w, norm_ffn_w, ffn_cw8, ffn_cb):
    h, h_t = rms_fwd(x, norm_mix_w, name="rms_mix_fwd")
    proj = matmul(h, wm, mode="nn", tm=1024, tn=1408, tk=1024, out_dtype=F32, name="mm_in_proj")
    small = matmul(h, ws, mode="nn", tm=1024, tn=128, tk=1024, out_dtype=F32, name="mm_in_proj_small")
    y_ssd, y_ssd_t, ypre, states = ssd_fwd(proj, small, ssd_cw8, ssd_cb, smallp, ssd_nw)
    place_q, place_k, ones_q, ones_k, fold_even, fold_odd = fox_tables()
    qn, kn, aq, ak, vb, knt, akt, vt = fox_prep(proj, small, smallp, qw_t, kw_t, sel, selt, place_q, place_k, ones_q, ones_k)
    y_fox, y_fox_t, lse, (a_out, a_up, a_down) = fox_fwd(qn, kn, aq, ak, vt, shards=later_shards)
    w_out = a_out.reshape(2048, D_MODEL)
    w_down = a_down.reshape(D_FF, D_MODEL)
    s = x.shape[0]
    shard = lambda index: pl.BlockSpec((None, 1024, 1408), index)
    x1 = matmul(y_ssd, w_out, mode="nn", tm=1024, tn=1024, tk=1024, out_dtype=F32, name="mm_out_ssd", add=x)
    x1 = matmul(y_fox, w_out, mode="nn", tm=1024, tn=1024, tk=1024, out_dtype=F32, name="mm_out_fox", add=x1, b_koff=1)
    hf, hf_t = rms_fwd(x1, norm_ffn_w, name="rms_ffn_fwd")
    hu = matmul(hf, a_up, mode="nn", tm=1024, tn=1408, tk=1024, out_dtype=F32, name="mm_up",
                layout=dict(m=s, n=2 * D_FF, k=D_MODEL, b_spec=shard(lambda i, j, kk: (j, kk, 0))))
    act, act_t = ffn_mid_fwd(hu, ffn_cw8, ffn_cb)
    y = matmul(act, w_down, mode="nn", tm=1024, tn=1024, tk=1408, out_dtype=F32, name="mm_down", add=x1)
    dy, sq = loss_head(y, target)

    dact = matmul(dy, w_down, mode="nt", tm=1024, tn=1408, tk=1024, out_dtype=F32, name="mm_dact")
    g_down = matmul(act_t, dy, mode="nn", tm=1408, tn=1024, tk=1024, out_dtype=BF16, name="mm_dw_down")
    dhu, gcw_g, gcw_v = ffn_mid_bwd(hu, dact, ffn_cw8, ffn_cb)
    dhf = matmul(dhu, a_up, mode="nt", tm=1024, tn=1024, tk=1408, out_dtype=F32, name="mm_dhf",
                 layout=dict(m=s, n=D_MODEL, k=2 * D_FF, a_spec=shard(lambda i, j, kk: (kk // 2, i, kk % 2)),
                             b_spec=shard(lambda i, j, kk: (kk, 0, 0))))
    g_up = matmul(hf_t, dhu, mode="nn", tm=1024, tn=1408, tk=1024, out_dtype=BF16, name="mm_dw_up",
                  layout=dict(m=D_MODEL, n=2 * D_FF, k=s, b_spec=shard(lambda i, j, kk: (j // 2, kk, j % 2)),
                              o_spec=shard(lambda i, j, kk: (j, i, 0)), out_shape=(4, D_MODEL, 1408)))
    dx1, g_norm_ffn = rms_bwd(dhf, x1, norm_ffn_w, dy, name="rms_ffn_bwd")
    dmixed = matmul(dx1, w_out, mode="nt", tm=1024, tn=1024, tk=1024, out_dtype=F32, name="mm_dmixed")
    g_out_a = matmul(y_ssd_t, dx1, mode="nn", tm=1024, tn=1024, tk=1024, out_dtype=BF16, name="mm_dw_out_ssd")
    g_out_b = matmul(y_fox_t, dx1, mode="nn", tm=1024, tn=1024, tk=1024, out_dtype=BF16, name="mm_dw_out_fox")
    early = [jnp.concatenate([g_out_a, g_out_b], axis=0).reshape(4, 512, D_MODEL), g_up, g_down.reshape(4, 704, D_MODEL)]
    dz, dxs, db, dc, dsmall_ssd, gcw_x, gcw_b, gcw_c, g_sp, g_ssd_nw, theirs = ssd_bwd(
        proj, small, ssd_cw8, ssd_cb, smallp, ssd_nw, ypre, states, dmixed, sel, swap=early)
    parts = [add_pair(a, b, name="add_pair_" + n, tr=ADAM_ROWS[n])
             for a, b, n in zip(_own_halves(early), theirs, BIG_NAMES[1:])]
    dqn, dkn, dv, dc0, dc1, landed_early = fox_bwd(qn, kn, aq, ak, knt, akt, vb, lse, dmixed, parts=parts)
    dq, dk, dsmall_fox, g_qw, g_kw, g_fb = fox_post(dqn, dkn, dc0, dc1, proj, small, smallp, qw_t, kw_t, sel, selt,
                                                    fold_even, fold_odd)
    dproj = jnp.concatenate([dz, dxs, dq, dk, dv.astype(BF16), db, dc], axis=1)
    dsmall = (dsmall_ssd + dsmall_fox).astype(BF16)
    g_wm = matmul(h_t, dproj, mode="nn", tm=1024, tn=1408, tk=1024, out_dtype=BF16, name="mm_dw_in")
    g_ws = matmul(h_t, dsmall, mode="nn", tm=1024, tn=128, tk=1024, out_dtype=BF16, name="mm_dw_in_small")
    mine, theirs = pair_swap_halves([_in_grad_shards(g_wm, g_ws)], name="pair_swap_w_in")
    part_in = add_pair(mine[0], theirs[0], name="add_pair_w_in", tr=ADAM_ROWS["w_in"])
    dh, landed_in = matmul(dproj, wm, mode="nt", tm=1024, tn=1024, tk=1408, out_dtype=F32, name="mm_dh", scatter=[part_in])
    dh = matmul(dsmall, ws, mode="nt", tm=1024, tn=1024, tk=128, out_dtype=F32, name="mm_dh_small", add=dh)
    grad_x, g_norm_mix = rms_bwd(dh, x, norm_mix_w, dx1, name="rms_mix_bwd")
    return dict(
        sq=sq, grad_x=grad_x, landed=landed_in + landed_early,
        g_norm_mix=g_norm_mix, g_norm_ffn=g_norm_ffn, g_ssd_nw=g_ssd_nw,
        g_ssd_cw=jnp.concatenate([gcw_x, gcw_b, gcw_c], axis=1), g_sp=g_sp, g_fb=g_fb, g_qw=g_qw, g_kw=g_kw,
        g_ffn_cw=jnp.concatenate([gcw_g, gcw_v], axis=1))


def adamw(w, g, m, v, *, name, tr, allreduce=None):
    rows, cols = w.shape
    nsteps = rows // tr

    def body(*refs):
        if allreduce is None:
            w_ref, g_ref, m_ref, v_ref, d_ref, mo_ref, vo_ref = refs
        else:
            w_ref, g_ref, m_ref, v_ref, packed_ref, d_ref, mo_ref, vo_ref, summed_ref = refs[:9]
            start, finish = _allreduce_phases(packed_ref, summed_ref, *refs[9:])
            pl.when(pl.program_id(0) == 0)(start)
        gv = g_ref[...]
        mn = ADAM_B1 * m_ref[...] + (1.0 - ADAM_B1) * gv
        vn = ADAM_B2 * v_ref[...] + (1.0 - ADAM_B2) * (gv * gv)
        m_hat = mn / (1.0 - ADAM_B1 ** ADAM_STEP)
        v_hat = vn / (1.0 - ADAM_B2 ** ADAM_STEP)
        d_ref[...] = -ADAM_LR * (m_hat / (jnp.sqrt(v_hat) + ADAM_EPS) + ADAM_WD * w_ref[...])
        mo_ref[...] = mn
        vo_ref[...] = vn
        if allreduce is not None:
            pl.when(pl.program_id(0) == nsteps - 1)(finish)

    blk = pl.BlockSpec((tr, cols), lambda i: (i, 0))
    shp = jax.ShapeDtypeStruct((rows, cols), F32)
    if allreduce is None:
        return pl.pallas_call(
            body, name=name, grid=(nsteps,), in_specs=[blk] * 4, out_specs=[blk] * 3, out_shape=[shp] * 3,
            compiler_params=_params(("parallel",)),
        )(w, g, m, v)
    whole = pl.BlockSpec(memory_space=pltpu.VMEM)
    return pl.pallas_call(
        body, name=name, grid=(nsteps,), in_specs=[blk] * 4 + [whole], out_specs=[blk] * 3 + [whole],
        out_shape=[shp] * 3 + [jax.ShapeDtypeStruct(allreduce.shape, F32)],
        scratch_shapes=_allreduce_scratch(allreduce.shape[0]), compiler_params=_params(("arbitrary",)),
    )(w, g, m, v, allreduce)


def add_pair(a, b, *, name, tr):
    _, rows, cols = a.shape

    def body(a_ref, b_ref, o_ref):
        o_ref[...] = (a_ref[...].astype(F32) + b_ref[...].astype(F32)).astype(BF16)

    blk = pl.BlockSpec((1, tr, cols), lambda j, i: (j, i, 0))
    return pl.pallas_call(
        body, name=name, grid=(4, rows // tr), in_specs=[blk, blk], out_specs=blk,
        out_shape=jax.ShapeDtypeStruct(a.shape, BF16), compiler_params=_params(("parallel", "parallel")),
    )(a, b)


def sum_chips(parts, core, *, name, tr):
    _, rows, cols = parts.shape
    nblk = rows // tr

    def body(c_ref, p_ref, o_ref):
        acc = p_ref[0].astype(F32)
        for k in range(1, 4):
            acc = acc + p_ref[k].astype(F32)
        o_ref[...] = acc

    grid_spec = pltpu.PrefetchScalarGridSpec(
        num_scalar_prefetch=1, grid=(nblk,), in_specs=[pl.BlockSpec((4, tr, cols), lambda i, c: (0, i, 0))],
        out_specs=pl.BlockSpec((tr, cols), lambda i, c: (c[0] * nblk + i, 0)))
    return pl.pallas_call(
        body, name=name, grid_spec=grid_spec, out_shape=jax.ShapeDtypeStruct((2 * rows, cols), F32),
        compiler_params=_params(("parallel",)),
    )(core, parts)


ANY = pl.BlockSpec(memory_space=pl.ANY)


def _place():
    x, y, c = lax.axis_index("x"), lax.axis_index("y"), lax.axis_index("c")
    chips = [(1 - x, y), (x, 1 - y), (1 - x, 1 - y)]
    return x, y, c, chips


def _chunks(rows):
    size = next((c for c in (128, 176, 64, 32, 16, 8) if rows % c == 0), rows)
    return [(r, size) for r in range(0, rows, size)]


def gather_weights(shards):
    n = len(shards)

    def body(*refs):
        start, forward, finish = _gather_phases(refs[:n], refs[n:2 * n], *refs[2 * n:])
        start()
        forward()
        finish()

    gathered = pl.pallas_call(
        body, name="gather_weights", in_specs=[ANY] * n, out_specs=[ANY] * n,
        out_shape=_gather_out_shapes(shards), scratch_shapes=_gather_scratch(n),
    )(*shards)
    return gathered


def _gather_out_shapes(shards):
    return [jax.ShapeDtypeStruct((4,) + s.shape, s.dtype) for s in shards]


def _gather_scratch(n):
    return [pltpu.SemaphoreType.DMA((n, 7)), pltpu.SemaphoreType.DMA((n, 7))]


def _gather_phases(ins, outs, send_sems, recv_sems):
    n = len(ins)
    x, y, c, chips = _place()
    me = 2 * x + y
    sibling = (x, y, 1 - c)
    blks = [2 * cx + cy for cx, cy in chips]

    def half(a, blk, r=0, nr=None):
        rows = ins[a].shape[0] // 2
        return outs[a].at[blk, pl.ds(c * rows + r, rows if nr is None else nr), :]

    def to_chip(a, t, r=0, nr=None):
        rows = ins[a].shape[0] // 2
        return pltpu.make_async_remote_copy(
            src_ref=ins[a].at[pl.ds(c * rows + r, rows if nr is None else nr), :], dst_ref=half(a, me, r, nr),
            send_sem=send_sems.at[a, t], recv_sem=recv_sems.at[a, t], device_id=(*chips[t], c), device_id_type=MESH)

    def from_chip(a, t):
        return pltpu.make_async_remote_copy(
            src_ref=half(a, blks[t]), dst_ref=half(a, blks[t]), send_sem=send_sems.at[a, t], recv_sem=recv_sems.at[a, t],
            device_id=(*chips[t], c), device_id_type=MESH)

    def to_sibling(a, t, r=0, nr=None):
        return pltpu.make_async_remote_copy(
            src_ref=half(a, blks[t], r, nr), dst_ref=half(a, blks[t], r, nr), send_sem=send_sems.at[a, 3 + t],
            recv_sem=recv_sems.at[a, 3 + t], device_id=sibling, device_id_type=MESH)

    def from_sibling(a, t):
        rows = ins[a].shape[0] // 2
        dst = outs[a].at[blks[t], pl.ds((1 - c) * rows, rows), :]
        return pltpu.make_async_remote_copy(
            src_ref=dst, dst_ref=dst, send_sem=send_sems.at[a, 3 + t], recv_sem=recv_sems.at[a, 3 + t],
            device_id=sibling, device_id_type=MESH)

    def own(a, r=0, nr=None):
        return pltpu.make_async_remote_copy(
            src_ref=ins[a].at[pl.ds(r, ins[a].shape[0] if nr is None else nr), :],
            dst_ref=outs[a].at[me, pl.ds(r, ins[a].shape[0] if nr is None else nr), :],
            send_sem=send_sems.at[a, 6], recv_sem=recv_sems.at[a, 6], device_id=sibling, device_id_type=MESH)

    def start():
        for a in range(n):
            for t in range(3):
                for r, nr in _chunks(ins[a].shape[0] // 2):
                    to_chip(a, t, r, nr).start()
            for r, nr in _chunks(ins[a].shape[0]):
                own(a, r, nr).start()

    def forward():
        for a in range(n):
            for t in range(3):
                from_chip(a, t).wait_recv()
                for r, nr in _chunks(ins[a].shape[0] // 2):
                    to_sibling(a, t, r, nr).start()

    def finish():
        for a in range(n):
            for t in range(3):
                from_sibling(a, t).wait_recv()
        for a in range(n):
            for t in range(3):
                to_chip(a, t).wait_send()
                to_sibling(a, t).wait_send()
            own(a).wait()

    return start, forward, finish


def pair_swap_halves(grads, *, name):
    n = len(grads)

    def body(*refs):
        start, finish = _pair_swap_phases(refs[:n], refs[n:2 * n], *refs[2 * n:])
        start()
        finish()

    theirs = pl.pallas_call(
        body, name=name, in_specs=[ANY] * n, out_specs=[ANY] * n, out_shape=_pair_swap_out_shapes(grads),
        scratch_shapes=_pair_swap_scratch(n),
    )(*grads)
    return _own_halves(grads), theirs


def _pair_swap_out_shapes(grads):
    return [jax.ShapeDtypeStruct((4, g.shape[1] // 2, g.shape[2]), g.dtype) for g in grads]


def _pair_swap_scratch(n):
    return [pltpu.SemaphoreType.DMA((n,)), pltpu.SemaphoreType.DMA((n,))]


def _own_halves(grads):
    c = lax.axis_index("c")
    return [lax.dynamic_slice_in_dim(g, c * (g.shape[1] // 2), g.shape[1] // 2, axis=1) for g in grads]


def _pair_swap_phases(ins, theirs, send_sems, recv_sems):
    n = len(ins)
    x, y, c, _ = _place()
    sibling = (x, y, 1 - c)

    def start():
        for a in range(n):
            rows = ins[a].shape[1] // 2
            for j in range(4):
                for r, nr in _chunks(rows):
                    pltpu.make_async_remote_copy(
                        src_ref=ins[a].at[j, pl.ds((1 - c) * rows + r, nr), :], dst_ref=theirs[a].at[j, pl.ds(r, nr), :],
                        send_sem=send_sems.at[a], recv_sem=recv_sems.at[a], device_id=sibling, device_id_type=MESH).start()

    def finish():
        for a in range(n):
            pltpu.make_async_remote_copy(src_ref=theirs[a], dst_ref=theirs[a], send_sem=send_sems.at[a],
                                         recv_sem=recv_sems.at[a], device_id=sibling, device_id_type=MESH).wait()

    return start, finish


def _scatter_scratch(n):
    return [pltpu.SemaphoreType.DMA((n, 3)), pltpu.SemaphoreType.DMA((n, 3))]


def _keep_own_blocks(landed, parts):
    if not parts:
        return []
    chip = 2 * lax.axis_index("x") + lax.axis_index("y")
    return [lax.dynamic_update_slice(l, lax.dynamic_slice_in_dim(p, chip, 1, axis=0), (chip, 0, 0))
            for l, p in zip(landed, parts)]


def _scatter_phases(ins, outs, send_sems, recv_sems):
    n = len(ins)
    x, y, c, chips = _place()
    me = 2 * x + y
    blks = [2 * cx + cy for cx, cy in chips]

    def start():
        for a in range(n):
            for r, nr in _chunks(ins[a].shape[1]):
                for t in range(3):
                    pltpu.make_async_remote_copy(
                        src_ref=ins[a].at[blks[t], pl.ds(r, nr), :], dst_ref=outs[a].at[me, pl.ds(r, nr), :],
                        send_sem=send_sems.at[a, t], recv_sem=recv_sems.at[a, t],
                        device_id=(*chips[t], c), device_id_type=MESH).start()

    def finish():
        for a in range(n):
            for t in range(3):
                pltpu.make_async_remote_copy(
                    src_ref=outs[a].at[blks[t]], dst_ref=outs[a].at[blks[t]], send_sem=send_sems.at[a, t],
                    recv_sem=recv_sems.at[a, t], device_id=(*chips[t], c), device_id_type=MESH).wait()

    return start, finish


def pair_join_halves(bufs):
    n = len(bufs)

    def body(*refs):
        outs = refs[n:2 * n]
        send_sems, recv_sems = refs[2 * n:]
        x, y, c, _ = _place()
        sibling = (x, y, 1 - c)
        for a in range(n):
            rows = outs[a].shape[0] // 2
            for r, nr in _chunks(rows):
                mine = outs[a].at[pl.ds(c * rows + r, nr), :]
                pltpu.make_async_remote_copy(src_ref=mine, dst_ref=mine, send_sem=send_sems.at[a], recv_sem=recv_sems.at[a],
                                             device_id=sibling, device_id_type=MESH).start()
        for a in range(n):
            rows = outs[a].shape[0] // 2
            pltpu.make_async_remote_copy(
                src_ref=outs[a].at[pl.ds(c * rows, rows), :], dst_ref=outs[a].at[pl.ds((1 - c) * rows, rows), :],
                send_sem=send_sems.at[a], recv_sem=recv_sems.at[a], device_id=sibling, device_id_type=MESH).wait()

    return pl.pallas_call(
        body, name="pair_join_halves", in_specs=[ANY] * n, out_specs=[ANY] * n,
        out_shape=[jax.ShapeDtypeStruct(b.shape, b.dtype) for b in bufs], input_output_aliases={a: a for a in range(n)},
        scratch_shapes=[pltpu.SemaphoreType.DMA((n,)), pltpu.SemaphoreType.DMA((n,))],
    )(*bufs)


def _allreduce_scratch(rows):
    return [pltpu.VMEM((8, rows, 128), F32), pltpu.SemaphoreType.DMA((7,)), pltpu.SemaphoreType.DMA((7,))]


def _allreduce_phases(in_ref, out_ref, gathered, send_sems, recv_sems):
    x, y, c, _ = _place()
    me = 4 * x + 2 * y + c
    flips = [(fx, fy, fc) for fx in (0, 1) for fy in (0, 1) for fc in (0, 1)][1:]
    peers = [((1 - x) if fx else x, (1 - y) if fy else y, (1 - c) if fc else c) for fx, fy, fc in flips]

    def send(t):
        return pltpu.make_async_remote_copy(
            src_ref=in_ref, dst_ref=gathered.at[me], send_sem=send_sems.at[t], recv_sem=recv_sems.at[t],
            device_id=peers[t], device_id_type=MESH)

    def start():
        gathered[me] = in_ref[...]
        for t in range(7):
            send(t).start()

    def finish():
        for t, (px, py, pc) in enumerate(peers):
            slot = gathered.at[4 * px + 2 * py + pc]
            pltpu.make_async_remote_copy(
                src_ref=slot, dst_ref=slot, send_sem=send_sems.at[t], recv_sem=recv_sems.at[t],
                device_id=(px, py, pc), device_id_type=MESH).wait_recv()
        for t in range(7):
            send(t).wait_send()
        acc = gathered[0]
        for k in range(1, 8):
            acc = acc + gathered[k]
        out_ref[...] = acc

    return start, finish


SMALL_NAMES = ("norm_mix_w", "ssd_conv_w", "ssd_conv_b", "ssd_dt_bias", "ssd_a_log", "ssd_d", "ssd_norm_w", "fox_f_bias",
               "fox_q_norm_w", "fox_k_norm_w", "norm_ffn_w", "ffn_conv_w", "ffn_conv_b")
BIG_NAMES = ("w_in", "w_out", "w_up", "w_down")
WEIGHT_ORDER = ("norm_mix_w", "w_in", "ssd_conv_w", "ssd_conv_b", "ssd_dt_bias", "ssd_a_log", "ssd_d", "ssd_norm_w",
                "fox_f_bias", "fox_q_norm_w", "fox_k_norm_w", "w_out", "norm_ffn_w", "w_up", "ffn_conv_w", "ffn_conv_b", "w_down")
ADAM_ROWS = {"w_in": 256, "w_out": 256, "w_up": 256, "w_down": 176}


def _pack(arrays):
    pieces = []
    for a in arrays:
        flat = a.reshape(-1).astype(F32)
        pieces += [flat, jnp.zeros(((-flat.shape[0]) % 1024,), F32)]
    return jnp.concatenate(pieces).reshape(-1, 128)


def _unpack(packed, shapes):
    out, r = [], 0
    for shp in shapes:
        size = 1
        for d in shp:
            size *= d
        nrow = 8 * (-(-size // 1024))
        out.append(packed[r:r + nrow].reshape(-1)[:size].reshape(shp))
        r += nrow
    return out


IN_SHARD = IN_COLS // 4
IN_SEGMENTS = ((0, 2048, "main", 0), (2048, 2560, "main", 5120), (2560, 2576, "small", 0), (2576, 5648, "main", 2048),
               (5648, 5664, "small", 16))


def _in_cols(shards, lo, hi):
    out = []
    for j in range(4):
        a, b = max(lo, IN_SHARD * j), min(hi, IN_SHARD * (j + 1))
        if a < b:
            out.append(shards[j][:, a - IN_SHARD * j:b - IN_SHARD * j])
    return out


def _in_grad_shards(g_main, g_small):
    shards = []
    for j in range(4):
        pieces = []
        for lo, hi, src, at in IN_SEGMENTS:
            a, b = max(lo, IN_SHARD * j), min(hi, IN_SHARD * (j + 1))
            if a < b:
                pieces.append((g_main if src == "main" else g_small)[:, at + a - lo:at + b - lo])
        shards.append(jnp.concatenate(pieces, axis=1))
    return jnp.stack(shards)


def _pad_rows(a, rows):
    return jnp.pad(a, ((0, rows - a.shape[0]), (0, 0)))


def kernel(x, norm_mix_w, w_in, ssd_conv_w, ssd_conv_b, ssd_dt_bias, ssd_a_log, ssd_d, ssd_norm_w, fox_f_bias, fox_q_norm_w, fox_k_norm_w, w_out, norm_ffn_w, w_up, ffn_conv_w, ffn_conv_b, w_down, loss_target, m_norm_mix_w, m_w_in, m_ssd_conv_w, m_ssd_conv_b, m_ssd_dt_bias, m_ssd_a_log, m_ssd_d, m_ssd_norm_w, m_fox_f_bias, m_fox_q_norm_w, m_fox_k_norm_w, m_w_out, m_norm_ffn_w, m_w_up, m_ffn_conv_w, m_ffn_conv_b, m_w_down, v_norm_mix_w, v_w_in, v_ssd_conv_w, v_ssd_conv_b, v_ssd_dt_bias, v_ssd_a_log, v_ssd_d, v_ssd_norm_w, v_fox_f_bias, v_fox_q_norm_w, v_fox_k_norm_w, v_w_out, v_norm_ffn_w, v_w_up, v_ffn_conv_w, v_ffn_conv_b, v_w_down):
    w = dict(norm_mix_w=norm_mix_w, w_in=w_in, ssd_conv_w=ssd_conv_w, ssd_conv_b=ssd_conv_b, ssd_dt_bias=ssd_dt_bias,
             ssd_a_log=ssd_a_log, ssd_d=ssd_d, ssd_norm_w=ssd_norm_w, fox_f_bias=fox_f_bias, fox_q_norm_w=fox_q_norm_w,
             fox_k_norm_w=fox_k_norm_w, w_out=w_out, norm_ffn_w=norm_ffn_w, w_up=w_up, ffn_conv_w=ffn_conv_w,
             ffn_conv_b=ffn_conv_b, w_down=w_down)
    m = dict(norm_mix_w=m_norm_mix_w, w_in=m_w_in, ssd_conv_w=m_ssd_conv_w, ssd_conv_b=m_ssd_conv_b, ssd_dt_bias=m_ssd_dt_bias,
             ssd_a_log=m_ssd_a_log, ssd_d=m_ssd_d, ssd_norm_w=m_ssd_norm_w, fox_f_bias=m_fox_f_bias, fox_q_norm_w=m_fox_q_norm_w,
             fox_k_norm_w=m_fox_k_norm_w, w_out=m_w_out, norm_ffn_w=m_norm_ffn_w, w_up=m_w_up, ffn_conv_w=m_ffn_conv_w,
             ffn_conv_b=m_ffn_conv_b, w_down=m_w_down)
    v = dict(norm_mix_w=v_norm_mix_w, w_in=v_w_in, ssd_conv_w=v_ssd_conv_w, ssd_conv_b=v_ssd_conv_b, ssd_dt_bias=v_ssd_dt_bias,
             ssd_a_log=v_ssd_a_log, ssd_d=v_ssd_d, ssd_norm_w=v_ssd_norm_w, fox_f_bias=v_fox_f_bias, fox_q_norm_w=v_fox_q_norm_w,
             fox_k_norm_w=v_fox_k_norm_w, w_out=v_w_out, norm_ffn_w=v_norm_ffn_w, w_up=v_w_up, ffn_conv_w=v_ffn_conv_w,
             ffn_conv_b=v_ffn_conv_b, w_down=v_w_down)
    chip = 2 * lax.axis_index("x") + lax.axis_index("y")

    a_in, a_scw, a_fcw = gather_weights([w_in[0].astype(BF16), _pad_rows(ssd_conv_w[0], 16), _pad_rows(ffn_conv_w[0], 16)])
    later_shards = [w_out[0].astype(BF16), w_up[0].astype(BF16), w_down[0].astype(BF16)]
    wm = jnp.concatenate([p for lo, hi, src, _ in sorted(IN_SEGMENTS, key=lambda seg: seg[3]) if src == "main"
                          for p in _in_cols(a_in, lo, hi)], axis=1)
    ws = jnp.concatenate([p for lo, hi, src, _ in IN_SEGMENTS if src == "small" for p in _in_cols(a_in, lo, hi)]
                         + [jnp.zeros((D_MODEL, SMALL_COLS - 32), BF16)], axis=1)
    ssd_cw8 = a_scw.transpose(1, 0, 2).reshape(16, 1536)[:8]
    ffn_cw8 = a_fcw.transpose(1, 0, 2).reshape(16, 2 * D_FF)[:8]
    smallp = jnp.zeros((8, 128), F32)
    smallp = smallp.at[0, :16].set(ssd_dt_bias[0]).at[1, :16].set(ssd_a_log[0]).at[2, :16].set(ssd_d[0])
    smallp = smallp.at[3, F_LANE:F_LANE + 16].set(fox_f_bias[0])
    qw_t = jnp.tile(fox_q_norm_w[0], N_HEADS)[None]
    kw_t = jnp.tile(fox_k_norm_w[0], N_HEADS)[None]
    sel = jnp.asarray((np.arange(1024)[:, None] // HEAD_DIM == np.arange(128)[None, :]).astype(np.float32), BF16)

    res = local_step(x[0], loss_target[0], wm, ws, later_shards, ssd_cw8, ssd_conv_b, smallp, ssd_norm_w, qw_t, kw_t,
                     sel, sel.T, norm_mix_w, norm_ffn_w, ffn_cw8, ffn_conv_b)

    full_shapes = [(1, 1024), (1, 4, 1536), (1, 1536), (1, 16), (1, 16), (1, 16), (1, 1024), (1, 16), (1, 64), (1, 64),
                   (1, 1024), (1, 3, 2 * D_FF), (1, 2 * D_FF), (1,)]
    local_small = [res["g_norm_mix"], res["g_ssd_cw"][:4], res["g_ssd_cw"][4], res["g_sp"][0, :16], res["g_sp"][1, :16],
                   res["g_sp"][2, :16], res["g_ssd_nw"], res["g_fb"][0, F_LANE:F_LANE + 16],
                   res["g_qw"].reshape(N_HEADS, HEAD_DIM).sum(0), res["g_kw"].reshape(N_HEADS, HEAD_DIM).sum(0),
                   res["g_norm_ffn"], res["g_ffn_cw"][:3], res["g_ffn_cw"][3], jnp.sum(res["sq"])]
    landed = res["landed"]
    core = lax.axis_index("c").astype(jnp.int32).reshape(1)
    halves = [sum_chips(p, core, name="sum_chips_" + n, tr=ADAM_ROWS[n]) for p, n in zip(landed, BIG_NAMES)]
    g_big = dict(zip(BIG_NAMES, pair_join_halves(halves)))

    grads, deltas, new_m, new_v = {}, {}, {}, {}
    for n in BIG_NAMES:
        out = adamw(w[n][0], g_big[n], m[n][0], v[n][0], name="adamw_" + n, tr=ADAM_ROWS[n],
                    allreduce=_pack(local_small) if n == BIG_NAMES[0] else None)
        if n == BIG_NAMES[0]:
            summed = _unpack(out[3], full_shapes)
        d, mn, vn = out[:3]
        grads[n], deltas[n], new_m[n], new_v[n] = g_big[n][None], d[None], mn[None], vn[None]
    loss = (0.5 / D_MODEL) * summed[-1][0]
    g_small = dict(zip(SMALL_NAMES, summed[:-1]))
    g_small["ssd_conv_w"] = lax.dynamic_slice(g_small["ssd_conv_w"], (0, 0, 384 * chip), (1, 4, 384))
    g_small["ffn_conv_w"] = lax.dynamic_slice(g_small["ffn_conv_w"], (0, 0, 1408 * chip), (1, 3, 1408))
    shapes = [w[n].shape for n in SMALL_NAMES]
    packed_w = _pack([w[n] for n in SMALL_NAMES])
    d, mn, vn = adamw(packed_w, _pack([g_small[n] for n in SMALL_NAMES]), _pack([m[n] for n in SMALL_NAMES]),
                      _pack([v[n] for n in SMALL_NAMES]), name="adamw_small", tr=packed_w.shape[0])
    for n, dd, mm, vv in zip(SMALL_NAMES, _unpack(d, shapes), _unpack(mn, shapes), _unpack(vn, shapes)):
        grads[n], deltas[n], new_m[n], new_v[n] = g_small[n].reshape(w[n].shape), dd, mm, vv
    return (loss, res["grad_x"][None], *[grads[n] for n in WEIGHT_ORDER], *[deltas[n] for n in WEIGHT_ORDER],
            *[new_m[n] for n in WEIGHT_ORDER], *[new_v[n] for n in WEIGHT_ORDER])
```

```python
import functools

import jax
import jax.numpy as jnp
import numpy as np
from jax import lax
from jax.experimental import pallas as pl
from jax.experimental.pallas import tpu as pltpu

F32 = jnp.float32
BF16 = jnp.bfloat16
MESH = pl.DeviceIdType.MESH

D_MODEL = 1024
HEAD_DIM = 64
N_HEADS = 16
N_PAIRS = N_HEADS // 2
SSD_CHUNK = 128
SSD_STATE = 128
SSD_CONV = 4
D_FF = 2816
FFN_CONV = 3
NORM_EPS = 1e-6
MAIN_COLS = 5632
SMALL_COLS = 128
PROJ_COLS = MAIN_COLS + SMALL_COLS
SMALL_BLOCK = MAIN_COLS // SMALL_COLS
PROJ_TILE = 1152
F_LANE = 16
IN_COLS = 5664

ADAM_LR = 0.001
ADAM_B1 = 0.9
ADAM_B2 = 0.999
ADAM_EPS = 1e-08
ADAM_WD = 0.01
ADAM_STEP = 10

VMEM_LIMIT_V7X = 56 * 1024 * 1024
NEG_BIG = -1e30


def _params(sem=None):
    return pltpu.CompilerParams(dimension_semantics=sem, vmem_limit_bytes=VMEM_LIMIT_V7X)


def _sigmoid(x):
    return 1.0 / (1.0 + jnp.exp(-x))


def _silu_and_grad(x):
    s = _sigmoid(x)
    return x * s, s * (1.0 + x * (1.0 - s))


def _shift_down(v, j):
    return v if j == 0 else pltpu.roll(v, j, 0)


def _shift_up(v, j):
    return v if j == 0 else pltpu.roll(v, v.shape[0] - j, 0)


def _row_iota(shape):
    return lax.broadcasted_iota(jnp.int32, shape, 0)


def _lane_iota(shape):
    return lax.broadcasted_iota(jnp.int32, shape, 1)


def _dot(a, b, mode="nn"):
    dims = {"nn": (((1,), (0,)), ((), ())), "nt": (((1,), (1,)), ((), ())), "tn": (((0,), (0,)), ((), ()))}[mode]
    return lax.dot_general(a.astype(BF16), b.astype(BF16), dims, preferred_element_type=F32)


def _dot_f32(a, b):
    return jnp.dot(a, b, precision=lax.Precision.HIGHEST, preferred_element_type=F32)


def matmul(a, b, *, mode, tm, tn, tk, out_dtype, name, add=None, b_koff=0, scatter=(), layout=None):
    layout = layout or {}
    if layout:
        m, n, k = layout["m"], layout["n"], layout["k"]
    else:
        (m, k), n = a.shape, (b.shape[1] if mode == "nn" else b.shape[0])
    assert m % tm == 0 and n % tn == 0 and k % tk == 0, (name, m, n, k, tm, tn, tk)
    nk = k // tk
    grid = (m // tm, n // tn, nk)
    a_spec = layout.get("a_spec") or pl.BlockSpec((tm, tk), lambda i, j, kk: (i, kk))
    b_spec = layout.get("b_spec") or (pl.BlockSpec((tn, tk), lambda i, j, kk: (j, kk + b_koff)) if mode == "nt"
                                      else pl.BlockSpec((tk, tn), lambda i, j, kk: (kk + b_koff, j)))
    o_spec = layout.get("o_spec") or pl.BlockSpec((tm, tn), lambda i, j, kk: (i, j))
    out_struct = jax.ShapeDtypeStruct(layout.get("out_shape", (m, n)), out_dtype)
    has_add = add is not None
    n_in = 3 if has_add else 2
    ns = len(scatter)

    def body(*refs):
        a_ref, b_ref = refs[:2]
        add_ref = refs[2] if has_add else None
        o_ref, acc_ref = refs[n_in + ns], refs[n_in + 2 * ns + 1]
        kk = pl.program_id(2)
        if ns:
            step = (pl.program_id(0) * grid[1] + pl.program_id(1)) * grid[2] + kk
            start, finish_copies = _scatter_phases(refs[n_in:n_in + ns], refs[n_in + ns + 1:n_in + 2 * ns + 1],
                                                   *refs[n_in + 2 * ns + 2:])
            pl.when(step == 0)(start)
        part = _dot(a_ref[...], b_ref[...], mode)

        def finish(total):
            if has_add:
                total = total + add_ref[...]
            o_ref[...] = total.astype(out_dtype)

        if nk == 1:
            finish(part)
        else:
            @pl.when(kk == 0)
            def _():
                acc_ref[...] = part

            @pl.when(jnp.logical_and(kk > 0, kk < nk - 1))
            def _():
                acc_ref[...] += part

            @pl.when(kk == nk - 1)
            def _():
                finish(acc_ref[...] + part)

        if ns:
            pl.when(step == grid[0] * grid[1] * grid[2] - 1)(finish_copies)

    in_specs = [a_spec, b_spec] + ([o_spec] if has_add else [])
    args = (a, b) + ((add,) if has_add else ())
    acc = pltpu.VMEM((tm, tn) if nk > 1 else (8, 128), F32)
    if not ns:
        return pl.pallas_call(
            body, name=name, grid=grid, in_specs=in_specs, out_specs=o_spec, out_shape=out_struct,
            scratch_shapes=[acc], compiler_params=_params(("parallel", "parallel", "arbitrary")),
        )(*args)
    outs = pl.pallas_call(
        body, name=name, grid=grid, in_specs=in_specs + [ANY] * ns, out_specs=[o_spec] + [ANY] * ns,
        out_shape=[out_struct] + [jax.ShapeDtypeStruct(p.shape, p.dtype) for p in scatter],
        scratch_shapes=[acc] + _scatter_scratch(ns), compiler_params=_params(("arbitrary", "arbitrary", "arbitrary")),
    )(*args, *scatter)
    return outs[0], _keep_own_blocks(outs[1:], scatter)


def rms_fwd(x, w, *, name, tm=1024):
    s, d = x.shape

    def body(x_ref, w_ref, h_ref, ht_ref):
        xv = x_ref[...]
        r = lax.rsqrt(jnp.mean(xv * xv, axis=-1, keepdims=True) + NORM_EPS)
        h = (xv * r) * w_ref[...]
        h_ref[...] = h.astype(BF16)
        ht_ref[...] = h.T.astype(BF16)

    return pl.pallas_call(
        body, name=name, grid=(s // tm,),
        in_specs=[pl.BlockSpec((tm, d), lambda i: (i, 0)), pl.BlockSpec((1, d), lambda i: (0, 0))],
        out_specs=[pl.BlockSpec((tm, d), lambda i: (i, 0)), pl.BlockSpec((d, tm), lambda i: (0, i))],
        out_shape=[jax.ShapeDtypeStruct((s, d), BF16), jax.ShapeDtypeStruct((d, s), BF16)],
        compiler_params=_params(("parallel",)),
    )(x, w)


def rms_bwd(dh, x, w, resid, *, name, tm=1024):
    s, d = x.shape

    def body(dh_ref, x_ref, w_ref, res_ref, dx_ref, dw_ref):
        xv = x_ref[...]
        dhv = dh_ref[...]
        r = lax.rsqrt(jnp.mean(xv * xv, axis=-1, keepdims=True) + NORM_EPS)
        xh = xv * r
        g = dhv * w_ref[...]
        dx_ref[...] = res_ref[...] + r * (g - xh * jnp.mean(g * xh, axis=-1, keepdims=True))
        part = jnp.sum(dhv * xh, axis=0, keepdims=True)

        @pl.when(pl.program_id(0) == 0)
        def _():
            dw_ref[...] = part

        @pl.when(pl.program_id(0) > 0)
        def _():
            dw_ref[...] += part

    row = pl.BlockSpec((tm, d), lambda i: (i, 0))
    vec = pl.BlockSpec((1, d), lambda i: (0, 0))
    return pl.pallas_call(
        body, name=name, grid=(s // tm,), in_specs=[row, row, vec, row], out_specs=[row, vec],
        out_shape=[jax.ShapeDtypeStruct((s, d), F32), jax.ShapeDtypeStruct((1, d), F32)],
        compiler_params=_params(("arbitrary",)),
    )(dh, x, w, resid)


def loss_head(y, target, *, tm=1024):
    s, d = y.shape

    def body(y_ref, t_ref, dy_ref, sq_ref):
        e = y_ref[...] - t_ref[...]
        dy_ref[...] = e / float(d)
        part = jnp.sum(e * e, axis=0, keepdims=True)

        @pl.when(pl.program_id(0) == 0)
        def _():
            sq_ref[...] = part

        @pl.when(pl.program_id(0) > 0)
        def _():
            sq_ref[...] += part

    row = pl.BlockSpec((tm, d), lambda i: (i, 0))
    vec = pl.BlockSpec((1, d), lambda i: (0, 0))
    return pl.pallas_call(
        body, name="loss_head", grid=(s // tm,), in_specs=[row, row], out_specs=[row, vec],
        out_shape=[jax.ShapeDtypeStruct((s, d), F32), jax.ShapeDtypeStruct((1, d), F32)],
        compiler_params=_params(("arbitrary",)),
    )(y, target)


def _row_shifts(ext, k_taps):
    return [_shift_down(ext, j) for j in range(k_taps)]


def _conv_rows(shifts, w):
    k_taps = len(shifts)
    acc = w[k_taps - 1:k_taps, :] * shifts[0]
    for k in range(k_taps - 1):
        acc = acc + w[k:k + 1, :] * shifts[k_taps - 1 - k]
    return acc


def _conv_weight_grad(dcur, shifts, rows, width):
    k_taps = len(shifts)
    out = [jnp.sum(dcur * shifts[k_taps - 1 - k][rows], axis=0, keepdims=True) for k in range(k_taps)]
    out.append(jnp.sum(dcur, axis=0, keepdims=True))
    return _stack_rows(out, width)


def _conv_rows_transposed(dext, w, k_taps):
    acc = w[k_taps - 1:k_taps, :] * dext
    for k in range(k_taps - 1):
        acc = acc + w[k:k + 1, :] * _shift_up(dext, k_taps - 1 - k)
    return acc


def _stack_rows(rows, width):
    ri = _row_iota((8, width))
    out = jnp.zeros((8, width), F32)
    for k, r in enumerate(rows):
        out = out + jnp.where(ri == k, r, 0.0)
    return out


def ffn_mid_fwd(hu, conv_w8, conv_b, *, tm=1024, tc=256):
    s = hu.shape[0]
    ncol = D_FF // tc
    r8 = tm // 8

    def body(g_ref, v_ref, gp_ref, vp_ref, wg_ref, wv_ref, bg_ref, bv_ref, o_ref, ot_ref):
        first = pl.program_id(1) == 0

        def conv(cur_ref, prev_ref, w_ref, b_ref):
            prev = jnp.where(first, 0.0, prev_ref[...])
            ext = jnp.concatenate([prev, cur_ref[...]], axis=0)
            return _conv_rows(_row_shifts(ext, FFN_CONV), w_ref[...])[8:] + b_ref[...]

        gc = conv(g_ref, gp_ref, wg_ref, bg_ref)
        vc = conv(v_ref, vp_ref, wv_ref, bv_ref)
        act = gc * _sigmoid(gc) * vc
        o_ref[...] = act.astype(BF16)
        ot_ref[...] = act.T.astype(BF16)

    def prev_idx(i):
        return jnp.maximum(i * r8 - 1, 0)

    in_specs = [
        pl.BlockSpec((tm, tc), lambda j, i: (i, j)),
        pl.BlockSpec((tm, tc), lambda j, i: (i, j + ncol)),
        pl.BlockSpec((8, tc), lambda j, i: (prev_idx(i), j)),
        pl.BlockSpec((8, tc), lambda j, i: (prev_idx(i), j + ncol)),
        pl.BlockSpec((8, tc), lambda j, i: (0, j)),
        pl.BlockSpec((8, tc), lambda j, i: (0, j + ncol)),
        pl.BlockSpec((1, tc), lambda j, i: (0, j)),
        pl.BlockSpec((1, tc), lambda j, i: (0, j + ncol)),
    ]
    return pl.pallas_call(
        body, name="ffn_mid_fwd", grid=(ncol, s // tm), in_specs=in_specs,
        out_specs=[pl.BlockSpec((tm, tc), lambda j, i: (i, j)), pl.BlockSpec((tc, tm), lambda j, i: (j, i))],
        out_shape=[jax.ShapeDtypeStruct((s, D_FF), BF16), jax.ShapeDtypeStruct((D_FF, s), BF16)],
        compiler_params=_params(("parallel", "parallel")),
    )(hu, hu, hu, hu, conv_w8, conv_w8, conv_b, conv_b)


def ffn_mid_bwd(hu, dact, conv_w8, conv_b, *, tm=1024, tc=256):
    s = hu.shape[0]
    ncol = D_FF // tc
    nrow = s // tm
    r8 = tm // 8

    def body(g_ref, v_ref, gp_ref, vp_ref, gn_ref, vn_ref, da_ref, dan_ref, wg_ref, wv_ref, bg_ref, bv_ref,
             dhu_ref, wgo_ref, wvo_ref):
        i = pl.program_id(1)
        first = i == 0
        last = i == nrow - 1

        def ext_of(cur_ref, prev_ref, next_ref):
            prev = jnp.where(first, 0.0, prev_ref[...])
            return jnp.concatenate([prev, cur_ref[...], next_ref[...]], axis=0)

        g_sh = _row_shifts(ext_of(g_ref, gp_ref, gn_ref), FFN_CONV)
        v_sh = _row_shifts(ext_of(v_ref, vp_ref, vn_ref), FFN_CONV)
        gc = _conv_rows(g_sh, wg_ref[...]) + bg_ref[...]
        vc = _conv_rows(v_sh, wv_ref[...]) + bv_ref[...]
        da_ext = jnp.concatenate([jnp.zeros((8, tc), F32), da_ref[...], jnp.where(last, 0.0, dan_ref[...])], axis=0)
        silu, dsilu = _silu_and_grad(gc)
        dgc = da_ext * vc * dsilu
        dvc = da_ext * silu
        dhu_ref[0] = _conv_rows_transposed(dgc, wg_ref[...], FFN_CONV)[8:8 + tm].astype(BF16)
        dhu_ref[1] = _conv_rows_transposed(dvc, wv_ref[...], FFN_CONV)[8:8 + tm].astype(BF16)

        cur = slice(8, 8 + tm)
        pg = _conv_weight_grad(dgc[cur], g_sh, cur, tc)
        pv = _conv_weight_grad(dvc[cur], v_sh, cur, tc)

        @pl.when(first)
        def _():
            wgo_ref[...] = pg
            wvo_ref[...] = pv

        @pl.when(i > 0)
        def _():
            wgo_ref[...] += pg
            wvo_ref[...] += pv

    def prev_idx(i):
        return jnp.maximum(i * r8 - 1, 0)

    def next_idx(i):
        return jnp.minimum((i + 1) * r8, s // 8 - 1)

    cur_g = pl.BlockSpec((tm, tc), lambda j, i: (i, j))
    cur_v = pl.BlockSpec((tm, tc), lambda j, i: (i, j + ncol))
    in_specs = [
        cur_g, cur_v,
        pl.BlockSpec((8, tc), lambda j, i: (prev_idx(i), j)),
        pl.BlockSpec((8, tc), lambda j, i: (prev_idx(i), j + ncol)),
        pl.BlockSpec((8, tc), lambda j, i: (next_idx(i), j)),
        pl.BlockSpec((8, tc), lambda j, i: (next_idx(i), j + ncol)),
        cur_g,
        pl.BlockSpec((8, tc), lambda j, i: (next_idx(i), j)),
        pl.BlockSpec((8, tc), lambda j, i: (0, j)),
        pl.BlockSpec((8, tc), lambda j, i: (0, j + ncol)),
        pl.BlockSpec((1, tc), lambda j, i: (0, j)),
        pl.BlockSpec((1, tc), lambda j, i: (0, j + ncol)),
    ]
    out_specs = [pl.BlockSpec((2, tm, tc), lambda j, i: (0, i, j)), pl.BlockSpec((8, tc), lambda j, i: (0, j)),
                 pl.BlockSpec((8, tc), lambda j, i: (0, j))]
    out_shape = [jax.ShapeDtypeStruct((2, s, D_FF), BF16),
                 jax.ShapeDtypeStruct((8, D_FF), F32), jax.ShapeDtypeStruct((8, D_FF), F32)]
    return pl.pallas_call(
        body, name="ffn_mid_bwd", grid=(ncol, nrow), in_specs=in_specs, out_specs=out_specs, out_shape=out_shape,
        compiler_params=_params(("parallel", "arbitrary")),
    )(hu, hu, hu, hu, hu, hu, dact, dact, conv_w8, conv_w8, conv_b, conv_b)


def _softplus(x):
    return jnp.maximum(x, 0.0) + jnp.log(1.0 + jnp.exp(-jnp.abs(x)))


def _cumsum_rows(v):
    n = v.shape[0]
    ri = _row_iota(v.shape)
    sh = 1
    while sh < n:
        v = v + jnp.where(ri >= sh, _shift_down(v, sh), 0.0)
        sh *= 2
    return v


def _rev_cumsum_rows(v):
    n = v.shape[0]
    ri = _row_iota(v.shape)
    sh = 1
    while sh < n:
        v = v + jnp.where(ri < n - sh, _shift_up(v, sh), 0.0)
        sh *= 2
    return v


def _total(v):
    return jnp.sum(jnp.sum(v, axis=1, keepdims=True), axis=0, keepdims=True)


def _ssd_in_specs(rev_nc=None):
    def ch(c):
        return c if rev_nc is None else rev_nc - 1 - c

    def prev(c):
        return jnp.maximum(ch(c) * (SSD_CHUNK // 8) - 1, 0)

    L = SSD_CHUNK
    return [
        pl.BlockSpec((L, 1024), lambda c: (ch(c), 0)),
        pl.BlockSpec((L, 1024), lambda c: (ch(c), 1)),
        pl.BlockSpec((L, 256), lambda c: (ch(c), 20)),
        pl.BlockSpec((L, 256), lambda c: (ch(c), 21)),
        pl.BlockSpec((8, 1024), lambda c: (prev(c), 1)),
        pl.BlockSpec((8, 256), lambda c: (prev(c), 20)),
        pl.BlockSpec((8, 256), lambda c: (prev(c), 21)),
        pl.BlockSpec((8, 1024), lambda c: (0, 0)),
        pl.BlockSpec((8, 256), lambda c: (0, 4)),
        pl.BlockSpec((8, 256), lambda c: (0, 5)),
        pl.BlockSpec((1, 1024), lambda c: (0, 0)),
        pl.BlockSpec((1, 256), lambda c: (0, 4)),
        pl.BlockSpec((1, 256), lambda c: (0, 5)),
        pl.BlockSpec((L, SMALL_COLS), lambda c: (ch(c), SMALL_BLOCK)),
        pl.BlockSpec((8, 128), lambda c: (0, 0)),
        pl.BlockSpec((1, 1024), lambda c: (0, 0)),
    ]


def _ssd_conv_pre(cur_ref, prev_ref, w_ref, b_ref, first):
    prev = jnp.where(first, 0.0, prev_ref[...])
    shifts = _row_shifts(jnp.concatenate([prev, cur_ref[...]], axis=0), SSD_CONV)
    return shifts, _conv_rows(shifts, w_ref[...])[8:] + b_ref[...]


def _ssd_time_consts(small_ref, sp_ref):
    dt_pre = small_ref[...] + sp_ref[0:1, :]
    dt = _softplus(dt_pre)
    a = -jnp.exp(sp_ref[1:2, :])
    acs = _cumsum_rows(dt * a)
    return dt_pre, dt, a, acs


def ssd_fwd(proj, conv_w8, conv_b, smallp, norm_w):
    s = proj.shape[0]
    nc = s // SSD_CHUNK
    L = SSD_CHUNK

    def body(z_ref, xs_ref, b_ref, c_ref, xsp_ref, bp_ref, cp_ref, wx_ref, wb_ref, wc_ref, bx_ref, bb_ref, bc_ref,
             small_ref, sp_ref, nw_ref, y_ref, yt_ref, ypre_ref, st_ref, state):
        first = pl.program_id(0) == 0

        @pl.when(first)
        def _():
            state[...] = jnp.zeros_like(state)

        xs = _ssd_conv_pre(xs_ref, xsp_ref, wx_ref, bx_ref, first)[1]
        xs = xs * _sigmoid(xs)
        bm = _ssd_conv_pre(b_ref, bp_ref, wb_ref, bb_ref, first)[1]
        bm = bm * _sigmoid(bm)
        cm = _ssd_conv_pre(c_ref, cp_ref, wc_ref, bc_ref, first)[1]
        cm = cm * _sigmoid(cm)
        _, dt, _, acs = _ssd_time_consts(small_ref, sp_ref)
        acs_t = acs.T
        li = _lane_iota((L, L))
        ri = _row_iota((L, L))
        tri = ri >= li
        lo = li < HEAD_DIM
        st_ref[0] = state[...]
        for g in range(2):
            bg = bm[:, 128 * g:128 * g + 128]
            cg = cm[:, 128 * g:128 * g + 128]
            gmat = _dot(cg, bg, "nt")
            for pp in range(4):
                p = 4 * g + pp
                h0, h1 = 2 * p, 2 * p + 1
                x = xs[:, 128 * p:128 * p + 128]
                a0, a1 = acs[:, h0:h0 + 1], acs[:, h1:h1 + 1]
                xdt = x * jnp.where(lo, dt[:, h0:h0 + 1], dt[:, h1:h1 + 1])
                m0 = gmat * jnp.exp(jnp.where(tri, a0 - acs_t[h0:h0 + 1, :], NEG_BIG))
                m1 = gmat * jnp.exp(jnp.where(tri, a1 - acs_t[h1:h1 + 1, :], NEG_BIG))
                yd = _dot(m0, jnp.where(lo, xdt, 0.0)) + _dot(m1, jnp.where(lo, 0.0, xdt))
                hin = state[p]
                yo = _dot(cg, hin, "nt") * jnp.exp(jnp.where(lo, a0, a1))
                dskip = jnp.where(lo[0:1], sp_ref[2:3, h0:h0 + 1], sp_ref[2:3, h1:h1 + 1])
                ypre_ref[:, 128 * p:128 * p + 128] = yd + yo + dskip * x
                al0, al1 = acs[L - 1:L, h0:h0 + 1], acs[L - 1:L, h1:h1 + 1]
                w = jnp.exp(jnp.where(lo, al0 - a0, al1 - a1))
                dec = jnp.exp(jnp.where(ri < HEAD_DIM, al0, al1))
                state[p] = dec * hin + _dot(xdt * w, bg, "tn")
        z = z_ref[...]
        yg = ypre_ref[...] * (z * _sigmoid(z))
        for g in range(2):
            seg = yg[:, 512 * g:512 * g + 512]
            r = lax.rsqrt(jnp.mean(seg * seg, axis=-1, keepdims=True) + NORM_EPS)
            out = (seg * r) * nw_ref[:, 512 * g:512 * g + 512]
            y_ref[:, 512 * g:512 * g + 512] = out.astype(BF16)
            yt_ref[512 * g:512 * g + 512, :] = out.T.astype(BF16)

    row = pl.BlockSpec((L, 1024), lambda c: (c, 0))
    return pl.pallas_call(
        body, name="ssd_fwd", grid=(nc,), in_specs=_ssd_in_specs(),
        out_specs=[row, pl.BlockSpec((1024, L), lambda c: (0, c)), row,
                   pl.BlockSpec((1, N_PAIRS, 128, 128), lambda c: (c, 0, 0, 0))],
        out_shape=[jax.ShapeDtypeStruct((s, 1024), BF16), jax.ShapeDtypeStruct((1024, s), BF16),
                   jax.ShapeDtypeStruct((s, 1024), F32), jax.ShapeDtypeStruct((nc, N_PAIRS, 128, 128), F32)],
        scratch_shapes=[pltpu.VMEM((N_PAIRS, 128, 128), F32)],
        compiler_params=_params(("arbitrary",)),
    )(proj, proj, proj, proj, proj, proj, proj, conv_w8, conv_w8, conv_w8, conv_b, conv_b, conv_b, proj, smallp, norm_w)


def ssd_bwd(proj, conv_w8, conv_b, smallp, norm_w, ypre, states, dy, sel, swap=()):
    s = proj.shape[0]
    nc = s // SSD_CHUNK
    L = SSD_CHUNK

    ns = len(swap)
    n_in, n_out, n_scratch = 20, 10, 11

    def body(*refs):
        own = refs[:n_in] + refs[n_in + ns:n_in + ns + n_out] + refs[n_in + 2 * ns + n_out:n_in + 2 * ns + n_out + n_scratch]
        if ns:
            start, finish = _pair_swap_phases(refs[n_in:n_in + ns], refs[n_in + ns + n_out:n_in + 2 * ns + n_out],
                                              *refs[n_in + 2 * ns + n_out + n_scratch:])
            pl.when(pl.program_id(0) == 0)(start)
        compute(*own)
        if ns:
            pl.when(pl.program_id(0) == nc - 1)(finish)

    def compute(z_ref, xs_ref, b_ref, c_ref, xsp_ref, bp_ref, cp_ref, wx_ref, wb_ref, wc_ref, bx_ref, bb_ref, bc_ref,
                small_ref, sp_ref, nw_ref, ypre_ref, st_ref, dy_ref, sel_ref,
                dz_ref, dxs_ref, db_ref, dc_ref, dsmall_ref, gwx_ref, gwb_ref, gwc_ref, gsp_ref, gnw_ref,
                dstate, carry_x, carry_b, carry_c, dxs_buf, dbm_buf, dcm_buf, qcs, col_sums, acs_terms, dt_terms):
        step = pl.program_id(0)
        col_sums[...] = jnp.zeros_like(col_sums)
        first_chunk = step == nc - 1
        start = step == 0

        @pl.when(start)
        def _():
            dstate[...] = jnp.zeros_like(dstate)
            carry_x[...] = jnp.zeros_like(carry_x)
            carry_b[...] = jnp.zeros_like(carry_b)
            carry_c[...] = jnp.zeros_like(carry_c)

        xs_sh, xs_pre = _ssd_conv_pre(xs_ref, xsp_ref, wx_ref, bx_ref, first_chunk)
        b_sh, b_pre = _ssd_conv_pre(b_ref, bp_ref, wb_ref, bb_ref, first_chunk)
        c_sh, c_pre = _ssd_conv_pre(c_ref, cp_ref, wc_ref, bc_ref, first_chunk)
        xs, xs_ds = _silu_and_grad(xs_pre)
        bm, b_ds = _silu_and_grad(b_pre)
        cm, c_ds = _silu_and_grad(c_pre)
        dt_pre, dt, a, acs = _ssd_time_consts(small_ref, sp_ref)
        acs_t = acs.T
        li = _lane_iota((L, L))
        ri = _row_iota((L, L))
        tri = ri >= li
        lo = li < HEAD_DIM
        lo_rows = ri < HEAD_DIM
        li1 = _lane_iota((1, L))

        z = z_ref[...]
        sz, dsz = _silu_and_grad(z)
        y = ypre_ref[...]
        yg = y * sz
        dout = dy_ref[...]
        dyg_parts = []
        gnw_parts = []
        for g in range(2):
            sl = slice(512 * g, 512 * g + 512)
            seg = yg[:, sl]
            r = lax.rsqrt(jnp.mean(seg * seg, axis=-1, keepdims=True) + NORM_EPS)
            n = seg * r
            gnw_parts.append(jnp.sum(dout[:, sl] * n, axis=0, keepdims=True))
            gg = dout[:, sl] * nw_ref[:, sl]
            dyg_parts.append(r * (gg - n * jnp.mean(gg * n, axis=-1, keepdims=True)))
        dyg = jnp.concatenate(dyg_parts, axis=1)
        gnw = jnp.concatenate(gnw_parts, axis=1)
        dz_ref[...] = (dyg * y * dsz).astype(BF16)
        dypre = dyg * sz

        qcs[...] = jnp.zeros_like(qcs)
        dalast = jnp.zeros((1, L), F32)
        for g in range(2):
            bg = bm[:, 128 * g:128 * g + 128]
            cg = cm[:, 128 * g:128 * g + 128]
            gmat = _dot(cg, bg, "nt")
            dgmat = jnp.zeros((L, L), F32)
            dbg = jnp.zeros((L, L), F32)
            dcg = jnp.zeros((L, L), F32)
            for pp in range(4):
                p = 4 * g + pp
                h0, h1 = 2 * p, 2 * p + 1
                lanes = slice(128 * p, 128 * p + 128)
                x = xs[:, lanes]
                dyp = dypre[:, lanes]
                a0, a1 = acs[:, h0:h0 + 1], acs[:, h1:h1 + 1]
                dtl = jnp.where(lo, dt[:, h0:h0 + 1], dt[:, h1:h1 + 1])
                xdt = x * dtl
                l0 = jnp.exp(jnp.where(tri, a0 - acs_t[h0:h0 + 1, :], NEG_BIG))
                l1 = jnp.exp(jnp.where(tri, a1 - acs_t[h1:h1 + 1, :], NEG_BIG))
                m0, m1 = gmat * l0, gmat * l1
                dskip = jnp.where(lo[0:1], sp_ref[2:3, h0:h0 + 1], sp_ref[2:3, h1:h1 + 1])
                col_sums[0:1, lanes] = jnp.sum(dyp * x, axis=0, keepdims=True)
                dx = dyp * dskip
                dy0, dy1 = jnp.where(lo, dyp, 0.0), jnp.where(lo, 0.0, dyp)
                x0, x1 = jnp.where(lo, xdt, 0.0), jnp.where(lo, 0.0, xdt)
                dm0, dm1 = _dot(dy0, x0, "nt"), _dot(dy1, x1, "nt")
                dxdt = _dot(m0, dy0, "tn") + _dot(m1, dy1, "tn")
                q0, q1 = dm0 * m0, dm1 * m1
                qcs[h0:h0 + 1, :] = jnp.sum(q0, axis=0, keepdims=True)
                qcs[h1:h1 + 1, :] = jnp.sum(q1, axis=0, keepdims=True)
                row_terms = jnp.where(lo, q0 + pltpu.roll(q0, HEAD_DIM, 1), q1 + pltpu.roll(q1, HEAD_DIM, 1))
                dgmat = dgmat + dm0 * l0 + dm1 * l1
                hin = st_ref[0, p]
                e = jnp.exp(jnp.where(lo, a0, a1))
                ch = _dot(cg, hin, "nt")
                dch = dyp * e
                dcg = dcg + _dot(dch, hin)
                dhin = _dot(dch, cg, "tn")
                dhout = dstate[p]
                al0, al1 = acs[L - 1:L, h0:h0 + 1], acs[L - 1:L, h1:h1 + 1]
                dec = jnp.exp(jnp.where(lo_rows, al0, al1))
                dhin = dhin + dec * dhout
                dal = dhout * hin * dec
                dal0 = _total(jnp.where(lo_rows, dal, 0.0))
                dal1 = _total(dal) - dal0
                dalast = dalast + jnp.where(li1 == h0, dal0, 0.0) + jnp.where(li1 == h1, dal1, 0.0)
                w = jnp.exp(jnp.where(lo, al0 - a0, al1 - a1))
                xw = xdt * w
                dxw = _dot(bg, dhout, "nt")
                dbg = dbg + _dot(xw, dhout)
                dxdt = dxdt + dxw * w
                dww = dxw * xw
                col_sums[1:2, lanes] = jnp.sum(dww, axis=0, keepdims=True)
                acs_terms[:, lanes] = row_terms + dch * ch - dww
                dx = dx + dxdt * dtl
                dt_terms[:, lanes] = dxdt * x
                dxs_buf[:, lanes] = dx
                dstate[p] = dhin
            dcg = dcg + _dot(dgmat, bg)
            dbg = dbg + _dot(dgmat, cg, "tn")
            dbm_buf[:, 128 * g:128 * g + 128] = dbg
            dcm_buf[:, 128 * g:128 * g + 128] = dcg

        head_sums = _split3_dot(col_sums[...], sel_ref[...])
        dskip_g = head_sums[0:1, :]
        dalast = dalast + head_sums[1:2, :]
        ddt = _split3_dot(dt_terms[...], sel_ref[...])
        dacs_tot = _split3_dot(acs_terms[...], sel_ref[...]) - qcs[...].T + jnp.where(ri == L - 1, dalast, 0.0)
        dstep = _rev_cumsum_rows(dacs_tot)
        ddt = ddt + dstep * a
        head_lane = li < N_HEADS
        ddt_pre = jnp.where(head_lane, ddt * _sigmoid(dt_pre), 0.0)
        dsmall_ref[...] = ddt_pre
        da = jnp.sum(jnp.where(head_lane, dstep * dt, 0.0), axis=0, keepdims=True)
        gsp = _stack_rows([jnp.sum(ddt_pre, axis=0, keepdims=True), da * a, dskip_g], L)

        def conv_back(dpost, ds, shifts, w_ref, carry, out_ref, width):
            dpre = dpost * ds
            dext = jnp.concatenate([dpre, carry[...]], axis=0)
            out_ref[...] = _conv_rows_transposed(dext, w_ref[...], SSD_CONV)[:L].astype(BF16)
            carry[...] = dpre[0:8]
            return _conv_weight_grad(dpre, shifts, slice(8, 8 + L), width)

        gwx = conv_back(dxs_buf[...], xs_ds, xs_sh, wx_ref, carry_x, dxs_ref, 1024)
        gwb = conv_back(dbm_buf[...], b_ds, b_sh, wb_ref, carry_b, db_ref, 256)
        gwc = conv_back(dcm_buf[...], c_ds, c_sh, wc_ref, carry_c, dc_ref, 256)

        @pl.when(start)
        def _():
            gwx_ref[...] = gwx
            gwb_ref[...] = gwb
            gwc_ref[...] = gwc
            gsp_ref[...] = gsp
            gnw_ref[...] = gnw

        @pl.when(step > 0)
        def _():
            gwx_ref[...] += gwx
            gwb_ref[...] += gwb
            gwc_ref[...] += gwc
            gsp_ref[...] += gsp
            gnw_ref[...] += gnw

    def ch(c):
        return nc - 1 - c

    row = pl.BlockSpec((L, 1024), lambda c: (ch(c), 0))
    row256 = pl.BlockSpec((L, 256), lambda c: (ch(c), 0))
    in_specs = _ssd_in_specs(rev_nc=nc) + [row, pl.BlockSpec((1, N_PAIRS, 128, 128), lambda c: (ch(c), 0, 0, 0)), row,
                                           pl.BlockSpec((1024, 128), lambda c: (0, 0))]
    out_specs = [row, row, row256, row256, pl.BlockSpec((L, 128), lambda c: (ch(c), 0)),
                 pl.BlockSpec((8, 1024), lambda c: (0, 0)), pl.BlockSpec((8, 256), lambda c: (0, 0)),
                 pl.BlockSpec((8, 256), lambda c: (0, 0)), pl.BlockSpec((8, 128), lambda c: (0, 0)),
                 pl.BlockSpec((1, 1024), lambda c: (0, 0))]
    out_shape = [jax.ShapeDtypeStruct((s, 1024), BF16), jax.ShapeDtypeStruct((s, 1024), BF16),
                 jax.ShapeDtypeStruct((s, 256), BF16), jax.ShapeDtypeStruct((s, 256), BF16),
                 jax.ShapeDtypeStruct((s, 128), F32),
                 jax.ShapeDtypeStruct((8, 1024), F32), jax.ShapeDtypeStruct((8, 256), F32),
                 jax.ShapeDtypeStruct((8, 256), F32), jax.ShapeDtypeStruct((8, 128), F32),
                 jax.ShapeDtypeStruct((1, 1024), F32)]
    scratch = [pltpu.VMEM((N_PAIRS, 128, 128), F32), pltpu.VMEM((8, 1024), F32), pltpu.VMEM((8, 256), F32),
               pltpu.VMEM((8, 256), F32), pltpu.VMEM((L, 1024), F32), pltpu.VMEM((L, 256), F32), pltpu.VMEM((L, 256), F32),
               pltpu.VMEM((L, L), F32), pltpu.VMEM((8, 1024), F32), pltpu.VMEM((L, 1024), F32), pltpu.VMEM((L, 1024), F32)]
    assert (len(in_specs), len(out_specs), len(scratch)) == (n_in, n_out, n_scratch)
    outs = pl.pallas_call(
        body, name="ssd_bwd", grid=(nc,), in_specs=in_specs + [ANY] * ns, out_specs=out_specs + [ANY] * ns,
        out_shape=out_shape + _pair_swap_out_shapes(swap), scratch_shapes=scratch + (_pair_swap_scratch(ns) if ns else []),
        compiler_params=_params(("arbitrary",)),
    )(proj, proj, proj, proj, proj, proj, proj, conv_w8, conv_w8, conv_w8, conv_b, conv_b, conv_b, proj, smallp, norm_w,
      ypre, states, dy, sel, *swap)
    return (*outs[:n_out], list(outs[n_out:]))


FOX_SCALE = HEAD_DIM ** -0.5
FOX_T = 256
Q_COL, K_COL, V_COL = 2, 3, 4


def _split_dot(v, m, terms):
    out, rest = None, v
    for i in range(terms):
        piece = rest.astype(BF16)
        out = _dot(piece, m) if out is None else out + _dot(piece, m)
        if i + 1 < terms:
            rest = rest - piece.astype(F32)
    return out


def _split3_dot(v, m):
    return _split_dot(v, m, 3)


def _head_mean(x, sel_ref, selt_ref):
    return _dot(x, sel_ref[...]) * (1.0 / HEAD_DIM)


def _head_spread(v, selt_ref):
    return _split_dot(v, selt_ref[...], 2)


def _head_rstd(x, sel_ref, selt_ref):
    return _head_spread(lax.rsqrt(_head_mean(x * x, sel_ref, selt_ref) + NORM_EPS), selt_ref)


def fox_tables():
    r = np.arange(3 * 128)
    piece, lane = r // 128, r % 128
    head = lane - F_LANE
    is_head = np.logical_and(head >= 0, head < N_HEADS)
    col = 128 * (head // 2) + HEAD_DIM * (1 - head % 2) + piece
    cols = np.arange(1024)
    place_q = np.logical_and(is_head[:, None], cols[None, :] == col[:, None])
    place_k = np.logical_and(is_head[:, None], cols[None, :] == (col + 3)[:, None])
    ones_q = np.logical_and(cols % HEAD_DIM >= 3, cols % HEAD_DIM < 6)[None]
    ones_k = (cols % HEAD_DIM < 3)[None]
    h = np.arange(128) - F_LANE
    ok = np.logical_and(h >= 0, h < N_HEADS)
    same_pair = cols[:, None] // 128 == (h // 2)[None, :]
    fold_even = np.logical_and(np.logical_and(ok, h % 2 == 0)[None, :], same_pair)
    fold_odd = np.logical_and(np.logical_and(ok, h % 2 == 1)[None, :], same_pair)
    as_bf16 = lambda t: jnp.asarray(t.astype(np.float32), BF16)
    return (as_bf16(place_q), as_bf16(place_k), jnp.asarray(ones_q, F32), jnp.asarray(ones_k, F32),
            as_bf16(fold_even), as_bf16(fold_odd))


def fox_prep(proj, smallp, qw, kw, sel, selt, place_q, place_k, ones_q, ones_k, *, tm=256):
    s = proj.shape[0]

    def body(q_ref, k_ref, v_ref, small_ref, sp_ref, qw_ref, kw_ref, sel_ref, selt_ref, pq_ref, pk_ref, oq_ref, ok_ref,
             qn_ref, kn_ref, aq_ref, ak_ref, vb_ref, knt_ref, akt_ref, vt_ref, carry):
        @pl.when(pl.program_id(0) == 0)
        def _():
            carry[...] = jnp.zeros_like(carry)

        q = q_ref[...]
        qn_ref[...] = (((q * _head_rstd(q, sel_ref, selt_ref)) * qw_ref[...]) * FOX_SCALE).astype(BF16)
        k = k_ref[...]
        kn = ((k * _head_rstd(k, sel_ref, selt_ref)) * kw_ref[...]).astype(BF16)
        kn_ref[...] = kn
        knt_ref[...] = kn.astype(F32).T.astype(BF16)
        vb_ref[...] = v_ref[...].astype(BF16)
        vt_ref[...] = v_ref[...].T.astype(BF16)
        li = _lane_iota((tm, 128))
        f_lane = jnp.logical_and(li >= F_LANE, li < F_LANE + N_HEADS)
        logf = jnp.where(f_lane, -_softplus(-(small_ref[...] + sp_ref[3:4, :])), 0.0)
        cum = _cumsum_rows(logf) + carry[...]
        carry[...] = cum[tm - 1:tm, :]
        hi = cum.astype(BF16)
        r1 = cum - hi.astype(F32)
        mid = r1.astype(BF16)
        lo = (r1 - mid.astype(F32)).astype(BF16)
        pieces = jnp.concatenate([hi, mid, lo], axis=1)
        aq_ref[...] = (_dot(pieces, pq_ref[...]) + oq_ref[...]).astype(BF16)
        ak = ok_ref[...] - _dot(pieces, pk_ref[...])
        ak_ref[...] = ak.astype(BF16)
        akt_ref[...] = ak.T.astype(BF16)

    row = pl.BlockSpec((tm, 1024), lambda i: (i, 0))
    col = pl.BlockSpec((1024, tm), lambda i: (0, i))
    vec = pl.BlockSpec((1, 1024), lambda i: (0, 0))
    table = pl.BlockSpec((384, 1024), lambda i: (0, 0))
    wide = jax.ShapeDtypeStruct((s, 1024), BF16)
    tall = jax.ShapeDtypeStruct((1024, s), BF16)
    return pl.pallas_call(
        body, name="fox_prep", grid=(s // tm,),
        in_specs=[pl.BlockSpec((tm, 1024), lambda i: (i, Q_COL)), pl.BlockSpec((tm, 1024), lambda i: (i, K_COL)),
                  pl.BlockSpec((tm, 1024), lambda i: (i, V_COL)),
                  pl.BlockSpec((tm, 128), lambda i: (i, SMALL_BLOCK)), pl.BlockSpec((8, 128), lambda i: (0, 0)), vec, vec,
                  pl.BlockSpec((1024, 128), lambda i: (0, 0)), pl.BlockSpec((128, 1024), lambda i: (0, 0)),
                  table, table, vec, vec],
        out_specs=[row, row, row, row, row, col, col, col],
        out_shape=[wide, wide, wide, wide, wide, tall, tall, tall],
        scratch_shapes=[pltpu.VMEM((1, 128), F32)], compiler_params=_params(("arbitrary",)),
    )(proj, proj, proj, proj, smallp, qw, kw, sel, selt, place_q, place_k, ones_q, ones_k)


def fox_fwd(qn, kn, aq, ak, vt, shards=()):
    s = qn.shape[0]
    t = FOX_T
    nq = s // t
    ng = len(shards)

    def body(*refs):
        q_ref, k_ref, aq_ref, ak_ref, vt_ref = refs[:5]
        o_ref, ot_ref, lse_ref = refs[5 + ng:8 + ng]
        p = pl.program_id(0)
        if ng:
            start, forward, finish = _gather_phases(refs[5:5 + ng], refs[8 + ng:8 + 2 * ng], *refs[8 + 2 * ng:])
            pl.when(p == 0)(start)
            pl.when(p == N_PAIRS // 2)(forward)

        @pl.when(p == 0)
        def _():
            lse_ref[...] = jnp.zeros_like(lse_ref)

        lo = _lane_iota((t, 128)) < HEAD_DIM
        lo_rows = _row_iota((128, t)) < HEAD_DIM
        causal_t = _lane_iota((t, t)) >= _row_iota((t, t))

        def q_loop(qi, _):
            q0 = pl.multiple_of(qi * t, t)
            qv, aqv = q_ref[pl.ds(q0, t), :], aq_ref[pl.ds(q0, t), :]
            qa, qb = jnp.where(lo, qv, aqv), jnp.where(lo, aqv, qv)

            def scores(kj):
                k0 = pl.multiple_of(kj * t, t)
                kv, akv = k_ref[pl.ds(k0, t), :], ak_ref[pl.ds(k0, t), :]
                return _dot(jnp.where(lo, kv, akv), qa, "nt"), _dot(jnp.where(lo, akv, kv), qb, "nt")

            def update(kj, stats, s0, s1):
                m0, l0, m1, l1, acc = stats
                vtv = vt_ref[:, pl.ds(pl.multiple_of(kj * t, t), t)]
                n0 = jnp.maximum(m0, jnp.max(s0, axis=0, keepdims=True))
                n1 = jnp.maximum(m1, jnp.max(s1, axis=0, keepdims=True))
                a0, a1 = jnp.exp(m0 - n0), jnp.exp(m1 - n1)
                p0, p1 = jnp.exp(s0 - n0), jnp.exp(s1 - n1)
                l0 = a0 * l0 + jnp.sum(p0, axis=0, keepdims=True)
                l1 = a1 * l1 + jnp.sum(p1, axis=0, keepdims=True)
                acc = (jnp.where(lo_rows, a0, a1) * acc + _dot(jnp.where(lo_rows, vtv, 0.0), p0)
                       + _dot(jnp.where(lo_rows, 0.0, vtv), p1))
                return n0, l0, n1, l1, acc

            def step(kj, carry):
                stats, (s0, s1) = carry[:5], carry[5:]
                nxt = scores(kj + 1)
                return (*update(kj, stats, s0, s1), *nxt)

            def row(val):
                return jnp.full((1, t), val, F32)

            init = (row(NEG_BIG), row(0.0), row(NEG_BIG), row(0.0), jnp.zeros((128, t), F32), *scores(0))
            carry = lax.fori_loop(0, qi, step, init)
            s0, s1 = jnp.where(causal_t, carry[5], NEG_BIG), jnp.where(causal_t, carry[6], NEG_BIG)
            m0, l0, m1, l1, acc = update(qi, carry[:5], s0, s1)
            out_t = acc / jnp.where(lo_rows, l0, l1)
            ot_ref[:, pl.ds(q0, t)] = out_t.astype(BF16)
            o_ref[pl.ds(q0, t), :] = out_t.T.astype(BF16)
            ri = _row_iota((N_HEADS, t))
            old = lse_ref[:, pl.ds(q0, t)]
            lse_ref[:, pl.ds(q0, t)] = jnp.where(
                ri == 2 * p, m0 + jnp.log(l0), jnp.where(ri == 2 * p + 1, m1 + jnp.log(l1), old))
            return 0

        lax.fori_loop(0, nq, q_loop, 0)
        if ng:
            pl.when(p == N_PAIRS - 1)(finish)

    pair = pl.BlockSpec((s, 128), lambda p: (0, p))
    outs = pl.pallas_call(
        body, name="fox_fwd", grid=(N_PAIRS,),
        in_specs=[pair] * 4 + [pl.BlockSpec((128, s), lambda p: (p, 0))] + [ANY] * ng,
        out_specs=[pair, pl.BlockSpec((128, s), lambda p: (p, 0)), pl.BlockSpec((N_HEADS, s), lambda p: (0, 0))] + [ANY] * ng,
        out_shape=[jax.ShapeDtypeStruct((s, 1024), BF16), jax.ShapeDtypeStruct((1024, s), BF16),
                   jax.ShapeDtypeStruct((N_HEADS, s), F32)] + _gather_out_shapes(shards),
        scratch_shapes=_gather_scratch(ng) if ng else [],
        compiler_params=_params(("arbitrary",)),
    )(qn, kn, aq, ak, vt, *shards)
    return outs[0], outs[1], outs[2], list(outs[3:])


def fox_bwd(qn, kn, aq, ak, knt, akt, vb, lse, dmixed, parts=()):
    s = qn.shape[0]
    t = FOX_T
    nq = s // t
    once = pl.Buffered(1)
    ns = len(parts)

    def body(*refs):
        q_ref, k_ref, aq_ref, ak_ref, kt_ref, akt_ref, v_ref, lse_ref, do_ref = refs[:9]
        dq_ref, dk_ref, dv_ref, dc0_ref, dc1_ref = refs[9 + ns:14 + ns]
        p_scr, dp_scr = refs[14 + 2 * ns:16 + 2 * ns]
        p = pl.program_id(0)
        if ns:
            start, finish = _scatter_phases(refs[9:9 + ns], refs[14 + ns:14 + 2 * ns], *refs[16 + 2 * ns:])
            pl.when(p == 0)(start)
        dk_ref[...] = jnp.zeros_like(dk_ref)
        dv_ref[...] = jnp.zeros_like(dv_ref)
        dc0_ref[...] = jnp.zeros_like(dc0_ref)
        dc1_ref[...] = jnp.zeros_like(dc1_ref)
        lo = _lane_iota((t, 128)) < HEAD_DIM
        lo_rows = _row_iota((128, t)) < HEAD_DIM
        causal_t = _lane_iota((t, t)) >= _row_iota((t, t))

        def q_loop(qi, _):
            q0 = pl.multiple_of(qi * t, t)
            qv, aqv = q_ref[pl.ds(q0, t), :], aq_ref[pl.ds(q0, t), :]
            qa, qb = jnp.where(lo, qv, aqv), jnp.where(lo, aqv, qv)
            do = do_ref[pl.ds(q0, t), :]
            doa, dob = jnp.where(lo, do, 0.0).astype(BF16), jnp.where(lo, 0.0, do).astype(BF16)
            lse_blk = lse_ref[:, pl.ds(q0, t)]
            ri = _row_iota((N_HEADS, t))
            lse0 = jnp.sum(jnp.where(ri == 2 * p, lse_blk, 0.0), axis=0, keepdims=True)
            lse1 = jnp.sum(jnp.where(ri == 2 * p + 1, lse_blk, 0.0), axis=0, keepdims=True)

            def scores(kj):
                k0 = pl.multiple_of(kj * t, t)
                kv, akv = k_ref[pl.ds(k0, t), :], ak_ref[pl.ds(k0, t), :]
                return _dot(jnp.where(lo, kv, akv), qa, "nt"), _dot(jnp.where(lo, akv, kv), qb, "nt")

            def pass1(kj, d0, d1, diagonal):
                k0 = pl.multiple_of(kj * t, t)
                vv = v_ref[pl.ds(k0, t), :]
                s0, s1 = scores(kj)
                if diagonal:
                    s0, s1 = jnp.where(causal_t, s0, NEG_BIG), jnp.where(causal_t, s1, NEG_BIG)
                p0, p1 = jnp.exp(s0 - lse0), jnp.exp(s1 - lse1)
                dp0, dp1 = _dot(vv, doa, "nt"), _dot(vv, dob, "nt")
                p_scr[0, kj], p_scr[1, kj] = p0, p1
                dp_scr[0, kj], dp_scr[1, kj] = dp0, dp1
                dv_ref[pl.ds(k0, t), :] += _dot(p0, doa) + _dot(p1, dob)
                return d0 + jnp.sum(p0 * dp0, axis=0, keepdims=True), d1 + jnp.sum(p1 * dp1, axis=0, keepdims=True)

            zero = jnp.zeros((1, t), F32)
            d0, d1 = lax.fori_loop(0, qi, lambda kj, c: pass1(kj, *c, False), (zero, zero))
            d0, d1 = pass1(qi, d0, d1, True)

            def pass2(kj, carry):
                dq0, dq1 = carry
                k0 = pl.multiple_of(kj * t, t)
                p0, p1 = p_scr[0, kj], p_scr[1, kj]
                ds0, ds1 = p0 * (dp_scr[0, kj] - d0), p1 * (dp_scr[1, kj] - d1)
                dk_ref[pl.ds(k0, t), :] += jnp.where(lo, _dot(ds0, qa), _dot(ds1, qb))
                dc0_ref[pl.ds(k0, t), :] += ds0[:, :128] + ds0[:, 128:]
                dc1_ref[pl.ds(k0, t), :] += ds1[:, :128] + ds1[:, 128:]
                ktv, aktv = kt_ref[:, pl.ds(k0, t)], akt_ref[:, pl.ds(k0, t)]
                return dq0 + _dot(jnp.where(lo_rows, ktv, aktv), ds0), dq1 + _dot(jnp.where(lo_rows, aktv, ktv), ds1)

            zq = jnp.zeros((128, t), F32)
            dq0, dq1 = lax.fori_loop(0, qi + 1, pass2, (zq, zq))
            dq_ref[pl.ds(q0, t), :] = jnp.where(lo_rows, dq0, dq1).T
            return 0

        lax.fori_loop(0, nq, q_loop, 0)
        if ns:
            pl.when(p == N_PAIRS - 1)(finish)

    pair = pl.BlockSpec((s, 128), lambda p: (0, p))
    pair_t = pl.BlockSpec((128, s), lambda p: (p, 0))
    out = jax.ShapeDtypeStruct((s, 1024), F32)
    outs = pl.pallas_call(
        body, name="fox_bwd", grid=(N_PAIRS,),
        in_specs=[pair, pair, pair, pair, pair_t, pair_t, pair, pl.BlockSpec((N_HEADS, s), lambda p: (0, 0)),
                  pl.BlockSpec((s, 128), lambda p: (0, 8 + p))] + [ANY] * ns,
        out_specs=[pl.BlockSpec((s, 128), lambda p: (0, p), pipeline_mode=once)] * 5 + [ANY] * ns,
        out_shape=[out] * 5 + [jax.ShapeDtypeStruct(p.shape, p.dtype) for p in parts],
        scratch_shapes=[pltpu.VMEM((2, nq, t, t), F32), pltpu.VMEM((2, nq, t, t), F32)] + (_scatter_scratch(ns) if ns else []),
        compiler_params=_params(("arbitrary",)),
    )(qn, kn, aq, ak, knt, akt, vb, lse, dmixed, *parts)
    return (*outs[:5], _keep_own_blocks(outs[5:], parts))


def fox_post(dqn, dkn, dc0, dc1, proj, smallp, qw, kw, sel, selt, fold_even, fold_odd, *, tm=256):
    s = proj.shape[0]
    nrow = s // tm

    def body(dqn_ref, dkn_ref, dc0_ref, dc1_ref, q_ref, k_ref, small_ref, sp_ref, qw_ref, kw_ref, sel_ref, selt_ref,
             fe_ref, fo_ref, dq_ref, dk_ref, dsmall_ref, gqw_ref, gkw_ref, gfb_ref, carry):
        step = pl.program_id(0)

        @pl.when(step == 0)
        def _():
            carry[...] = jnp.zeros_like(carry)

        def norm_bwd(x_ref, w_ref, dn, out_ref):
            x = x_ref[...]
            rf = _head_rstd(x, sel_ref, selt_ref)
            xh = x * rf
            g = dn * w_ref[...]
            mean_gx = _head_spread(_head_mean(g * xh, sel_ref, selt_ref), selt_ref)
            out_ref[...] = (rf * (g - xh * mean_gx)).astype(BF16)
            return jnp.sum(dn * xh, axis=0, keepdims=True)

        gqw = norm_bwd(q_ref, qw_ref, dqn_ref[...] * FOX_SCALE, dq_ref)
        gkw = norm_bwd(k_ref, kw_ref, dkn_ref[...], dk_ref)
        li = _lane_iota((tm, 128))
        f_lane = jnp.logical_and(li >= F_LANE, li < F_LANE + N_HEADS)
        dcum = -(_split3_dot(dc0_ref[...], fe_ref[...]) + _split3_dot(dc1_ref[...], fo_ref[...]))
        dlogf = _rev_cumsum_rows(dcum) + carry[...]
        carry[...] = dlogf[0:1, :]
        dfr = jnp.where(f_lane, dlogf * _sigmoid(-(small_ref[...] + sp_ref[3:4, :])), 0.0)
        dsmall_ref[...] = dfr
        gfb = jnp.sum(dfr, axis=0, keepdims=True)

        @pl.when(step == 0)
        def _():
            gqw_ref[...] = gqw
            gkw_ref[...] = gkw
            gfb_ref[...] = gfb

        @pl.when(step > 0)
        def _():
            gqw_ref[...] += gqw
            gkw_ref[...] += gkw
            gfb_ref[...] += gfb

    def rb(i):
        return nrow - 1 - i

    row = pl.BlockSpec((tm, 1024), lambda i: (rb(i), 0))
    vec = pl.BlockSpec((1, 1024), lambda i: (0, 0))
    fold = pl.BlockSpec((1024, 128), lambda i: (0, 0))
    return pl.pallas_call(
        body, name="fox_post", grid=(nrow,),
        in_specs=[row, row, row, row, pl.BlockSpec((tm, 1024), lambda i: (rb(i), Q_COL)),
                  pl.BlockSpec((tm, 1024), lambda i: (rb(i), K_COL)),
                  pl.BlockSpec((tm, 128), lambda i: (rb(i), SMALL_BLOCK)), pl.BlockSpec((8, 128), lambda i: (0, 0)), vec, vec,
                  fold, pl.BlockSpec((128, 1024), lambda i: (0, 0)), fold, fold],
        out_specs=[row, row, pl.BlockSpec((tm, 128), lambda i: (rb(i), 0)), vec, vec, pl.BlockSpec((1, 128), lambda i: (0, 0))],
        out_shape=[jax.ShapeDtypeStruct((s, 1024), BF16), jax.ShapeDtypeStruct((s, 1024), BF16),
                   jax.ShapeDtypeStruct((s, 128), F32), jax.ShapeDtypeStruct((1, 1024), F32),
                   jax.ShapeDtypeStruct((1, 1024), F32), jax.ShapeDtypeStruct((1, 128), F32)],
        scratch_shapes=[pltpu.VMEM((1, 128), F32)], compiler_params=_params(("arbitrary",)),
    )(dqn, dkn, dc0, dc1, proj, proj, proj, smallp, qw, kw, sel, selt, fold_even, fold_odd)


def local_step(x, target, wx, later_shards, ssd_cw8, ssd_cb, smallp, ssd_nw, qw_t, kw_t, sel, selt,
               norm_mix_w, norm_ffn_w, ffn_cw8, ffn_cb):
    h, h_t = rms_fwd(x, norm_mix_w, name="rms_mix_fwd")
    proj = matmul(h, wx, mode="nn", tm=1024, tn=PROJ_TILE, tk=1024, out_dtype=F32, name="mm_in_proj")
    y_ssd, y_ssd_t, ypre, states = ssd_fwd(proj, ssd_cw8, ssd_cb, smallp, ssd_nw)
    place_q, place_k, ones_q, ones_k, fold_even, fold_odd = fox_tables()
    qn, kn, aq, ak, vb, knt, akt, vt = fox_prep(proj, smallp, qw_t, kw_t, sel, selt, place_q, place_k, ones_q, ones_k)
    y_fox, y_fox_t, lse, (a_out, a_up, a_down) = fox_fwd(qn, kn, aq, ak, vt, shards=later_shards)
    w_out = a_out.reshape(2048, D_MODEL)
    w_down = a_down.reshape(D_FF, D_MODEL)
    s = x.shape[0]
    shard = lambda index: pl.BlockSpec((None, 1024, 1408), index)
    x1 = matmul(y_ssd, w_out, mode="nn", tm=1024, tn=1024, tk=1024, out_dtype=F32, name="mm_out_ssd", add=x)
    x1 = matmul(y_fox, w_out, mode="nn", tm=1024, tn=1024, tk=1024, out_dtype=F32, name="mm_out_fox", add=x1, b_koff=1)
    hf, hf_t = rms_fwd(x1, norm_ffn_w, name="rms_ffn_fwd")
    hu = matmul(hf, a_up, mode="nn", tm=1024, tn=1408, tk=1024, out_dtype=F32, name="mm_up",
                layout=dict(m=s, n=2 * D_FF, k=D_MODEL, b_spec=shard(lambda i, j, kk: (j, kk, 0))))
    act, act_t = ffn_mid_fwd(hu, ffn_cw8, ffn_cb)
    y = matmul(act, w_down, mode="nn", tm=1024, tn=1024, tk=1408, out_dtype=F32, name="mm_down", add=x1)
    dy, sq = loss_head(y, target)

    dact = matmul(dy, w_down, mode="nt", tm=1024, tn=1408, tk=1024, out_dtype=F32, name="mm_dact")
    g_down = matmul(act_t, dy, mode="nn", tm=1408, tn=1024, tk=1024, out_dtype=BF16, name="mm_dw_down")
    dhu, gcw_g, gcw_v = ffn_mid_bwd(hu, dact, ffn_cw8, ffn_cb)
    dhf = matmul(dhu, a_up, mode="nt", tm=1024, tn=1024, tk=1408, out_dtype=F32, name="mm_dhf",
                 layout=dict(m=s, n=D_MODEL, k=2 * D_FF, a_spec=shard(lambda i, j, kk: (kk // 2, i, kk % 2)),
                             b_spec=shard(lambda i, j, kk: (kk, 0, 0))))
    g_up = matmul(hf_t, dhu, mode="nn", tm=1024, tn=1408, tk=1024, out_dtype=BF16, name="mm_dw_up",
                  layout=dict(m=D_MODEL, n=2 * D_FF, k=s, b_spec=shard(lambda i, j, kk: (j // 2, kk, j % 2)),
                              o_spec=shard(lambda i, j, kk: (j, i, 0)), out_shape=(4, D_MODEL, 1408)))
    dx1, g_norm_ffn = rms_bwd(dhf, x1, norm_ffn_w, dy, name="rms_ffn_bwd")
    dmixed = matmul(dx1, w_out, mode="nt", tm=1024, tn=1024, tk=1024, out_dtype=F32, name="mm_dmixed")
    g_out_a = matmul(y_ssd_t, dx1, mode="nn", tm=1024, tn=1024, tk=1024, out_dtype=BF16, name="mm_dw_out_ssd")
    g_out_b = matmul(y_fox_t, dx1, mode="nn", tm=1024, tn=1024, tk=1024, out_dtype=BF16, name="mm_dw_out_fox")
    early = [jnp.concatenate([g_out_a, g_out_b], axis=0).reshape(4, 512, D_MODEL), g_up, g_down.reshape(4, 704, D_MODEL)]
    dz, dxs, db, dc, dsmall_ssd, gcw_x, gcw_b, gcw_c, g_sp, g_ssd_nw, theirs = ssd_bwd(
        proj, ssd_cw8, ssd_cb, smallp, ssd_nw, ypre, states, dmixed, sel, swap=early)
    parts = [add_pair(a, b, name="add_pair_" + n, tr=ADAM_ROWS[n])
             for a, b, n in zip(_own_halves(early), theirs, BIG_NAMES[1:])]
    dqn, dkn, dv, dc0, dc1, landed_early = fox_bwd(qn, kn, aq, ak, knt, akt, vb, lse, dmixed, parts=parts)
    dq, dk, dsmall_fox, g_qw, g_kw, g_fb = fox_post(dqn, dkn, dc0, dc1, proj, smallp, qw_t, kw_t, sel, selt,
                                                    fold_even, fold_odd)
    dproj = jnp.concatenate([dz, dxs, dq, dk, dv.astype(BF16), db, dc, (dsmall_ssd + dsmall_fox).astype(BF16)], axis=1)
    g_wx = matmul(h_t, dproj, mode="nn", tm=1024, tn=PROJ_TILE, tk=1024, out_dtype=BF16, name="mm_dw_in")
    mine, theirs = pair_swap_halves([_in_grad_shards(g_wx)], name="pair_swap_w_in")
    part_in = add_pair(mine[0], theirs[0], name="add_pair_w_in", tr=ADAM_ROWS["w_in"])
    dh, landed_in = matmul(dproj, wx, mode="nt", tm=1024, tn=1024, tk=PROJ_TILE, out_dtype=F32, name="mm_dh",
                           scatter=[part_in])
    grad_x, g_norm_mix = rms_bwd(dh, x, norm_mix_w, dx1, name="rms_mix_bwd")
    return dict(
        sq=sq, grad_x=grad_x, landed=landed_in + landed_early,
        g_norm_mix=g_norm_mix, g_norm_ffn=g_norm_ffn, g_ssd_nw=g_ssd_nw,
        g_ssd_cw=jnp.concatenate([gcw_x, gcw_b, gcw_c], axis=1), g_sp=g_sp, g_fb=g_fb, g_qw=g_qw, g_kw=g_kw,
        g_ffn_cw=jnp.concatenate([gcw_g, gcw_v], axis=1))


def adamw(w, g, m, v, *, name, tr, allreduce=None):
    rows, cols = w.shape
    nsteps = rows // tr

    def body(*refs):
        if allreduce is None:
            w_ref, g_ref, m_ref, v_ref, d_ref, mo_ref, vo_ref = refs
        else:
            w_ref, g_ref, m_ref, v_ref, packed_ref, d_ref, mo_ref, vo_ref, summed_ref = refs[:9]
            start, finish = _allreduce_phases(packed_ref, summed_ref, *refs[9:])
            pl.when(pl.program_id(0) == 0)(start)
        gv = g_ref[...]
        mn = ADAM_B1 * m_ref[...] + (1.0 - ADAM_B1) * gv
        vn = ADAM_B2 * v_ref[...] + (1.0 - ADAM_B2) * (gv * gv)
        m_hat = mn / (1.0 - ADAM_B1 ** ADAM_STEP)
        v_hat = vn / (1.0 - ADAM_B2 ** ADAM_STEP)
        d_ref[...] = -ADAM_LR * (m_hat / (jnp.sqrt(v_hat) + ADAM_EPS) + ADAM_WD * w_ref[...])
        mo_ref[...] = mn
        vo_ref[...] = vn
        if allreduce is not None:
            pl.when(pl.program_id(0) == nsteps - 1)(finish)

    blk = pl.BlockSpec((tr, cols), lambda i: (i, 0))
    shp = jax.ShapeDtypeStruct((rows, cols), F32)
    if allreduce is None:
        return pl.pallas_call(
            body, name=name, grid=(nsteps,), in_specs=[blk] * 4, out_specs=[blk] * 3, out_shape=[shp] * 3,
            compiler_params=_params(("parallel",)),
        )(w, g, m, v)
    whole = pl.BlockSpec(memory_space=pltpu.VMEM)
    return pl.pallas_call(
        body, name=name, grid=(nsteps,), in_specs=[blk] * 4 + [whole], out_specs=[blk] * 3 + [whole],
        out_shape=[shp] * 3 + [jax.ShapeDtypeStruct(allreduce.shape, F32)],
        scratch_shapes=_allreduce_scratch(allreduce.shape[0]), compiler_params=_params(("arbitrary",)),
    )(w, g, m, v, allreduce)


def add_pair(a, b, *, name, tr):
    _, rows, cols = a.shape

    def body(a_ref, b_ref, o_ref):
        o_ref[...] = (a_ref[...].astype(F32) + b_ref[...].astype(F32)).astype(BF16)

    blk = pl.BlockSpec((1, tr, cols), lambda j, i: (j, i, 0))
    return pl.pallas_call(
        body, name=name, grid=(4, rows // tr), in_specs=[blk, blk], out_specs=blk,
        out_shape=jax.ShapeDtypeStruct(a.shape, BF16), compiler_params=_params(("parallel", "parallel")),
    )(a, b)


def sum_chips(parts, core, *, name, tr):
    _, rows, cols = parts.shape
    nblk = rows // tr

    def body(c_ref, p_ref, o_ref):
        acc = p_ref[0].astype(F32)
        for k in range(1, 4):
            acc = acc + p_ref[k].astype(F32)
        o_ref[...] = acc

    grid_spec = pltpu.PrefetchScalarGridSpec(
        num_scalar_prefetch=1, grid=(nblk,), in_specs=[pl.BlockSpec((4, tr, cols), lambda i, c: (0, i, 0))],
        out_specs=pl.BlockSpec((tr, cols), lambda i, c: (c[0] * nblk + i, 0)))
    return pl.pallas_call(
        body, name=name, grid_spec=grid_spec, out_shape=jax.ShapeDtypeStruct((2 * rows, cols), F32),
        compiler_params=_params(("parallel",)),
    )(core, parts)


ANY = pl.BlockSpec(memory_space=pl.ANY)


def _place():
    x, y, c = lax.axis_index("x"), lax.axis_index("y"), lax.axis_index("c")
    chips = [(1 - x, y), (x, 1 - y), (1 - x, 1 - y)]
    return x, y, c, chips


def _chunks(rows):
    size = next((c for c in (128, 176, 64, 32, 16, 8) if rows % c == 0), rows)
    return [(r, size) for r in range(0, rows, size)]


def gather_weights(shards):
    n = len(shards)

    def body(*refs):
        start, forward, finish = _gather_phases(refs[:n], refs[n:2 * n], *refs[2 * n:])
        start()
        forward()
        finish()

    gathered = pl.pallas_call(
        body, name="gather_weights", in_specs=[ANY] * n, out_specs=[ANY] * n,
        out_shape=_gather_out_shapes(shards), scratch_shapes=_gather_scratch(n),
    )(*shards)
    return gathered


def _gather_out_shapes(shards):
    return [jax.ShapeDtypeStruct((4,) + s.shape, s.dtype) for s in shards]


def _gather_scratch(n):
    return [pltpu.SemaphoreType.DMA((n, 7)), pltpu.SemaphoreType.DMA((n, 7))]


def _gather_phases(ins, outs, send_sems, recv_sems):
    n = len(ins)
    x, y, c, chips = _place()
    me = 2 * x + y
    sibling = (x, y, 1 - c)
    blks = [2 * cx + cy for cx, cy in chips]

    def half(a, blk, r=0, nr=None):
        rows = ins[a].shape[0] // 2
        return outs[a].at[blk, pl.ds(c * rows + r, rows if nr is None else nr), :]

    def to_chip(a, t, r=0, nr=None):
        rows = ins[a].shape[0] // 2
        return pltpu.make_async_remote_copy(
            src_ref=ins[a].at[pl.ds(c * rows + r, rows if nr is None else nr), :], dst_ref=half(a, me, r, nr),
            send_sem=send_sems.at[a, t], recv_sem=recv_sems.at[a, t], device_id=(*chips[t], c), device_id_type=MESH)

    def from_chip(a, t):
        return pltpu.make_async_remote_copy(
            src_ref=half(a, blks[t]), dst_ref=half(a, blks[t]), send_sem=send_sems.at[a, t], recv_sem=recv_sems.at[a, t],
            device_id=(*chips[t], c), device_id_type=MESH)

    def to_sibling(a, t, r=0, nr=None):
        return pltpu.make_async_remote_copy(
            src_ref=half(a, blks[t], r, nr), dst_ref=half(a, blks[t], r, nr), send_sem=send_sems.at[a, 3 + t],
            recv_sem=recv_sems.at[a, 3 + t], device_id=sibling, device_id_type=MESH)

    def from_sibling(a, t):
        rows = ins[a].shape[0] // 2
        dst = outs[a].at[blks[t], pl.ds((1 - c) * rows, rows), :]
        return pltpu.make_async_remote_copy(
            src_ref=dst, dst_ref=dst, send_sem=send_sems.at[a, 3 + t], recv_sem=recv_sems.at[a, 3 + t],
            device_id=sibling, device_id_type=MESH)

    def own(a, r=0, nr=None):
        return pltpu.make_async_remote_copy(
            src_ref=ins[a].at[pl.ds(r, ins[a].shape[0] if nr is None else nr), :],
            dst_ref=outs[a].at[me, pl.ds(r, ins[a].shape[0] if nr is None else nr), :],
            send_sem=send_sems.at[a, 6], recv_sem=recv_sems.at[a, 6], device_id=sibling, device_id_type=MESH)

    def start():
        for a in range(n):
            for t in range(3):
                for r, nr in _chunks(ins[a].shape[0] // 2):
                    to_chip(a, t, r, nr).start()
            for r, nr in _chunks(ins[a].shape[0]):
                own(a, r, nr).start()

    def forward():
        for a in range(n):
            for t in range(3):
                from_chip(a, t).wait_recv()
                for r, nr in _chunks(ins[a].shape[0] // 2):
                    to_sibling(a, t, r, nr).start()

    def finish():
        for a in range(n):
            for t in range(3):
                from_sibling(a, t).wait_recv()
        for a in range(n):
            for t in range(3):
                to_chip(a, t).wait_send()
                to_sibling(a, t).wait_send()
            own(a).wait()

    return start, forward, finish


def pair_swap_halves(grads, *, name):
    n = len(grads)

    def body(*refs):
        start, finish = _pair_swap_phases(refs[:n], refs[n:2 * n], *refs[2 * n:])
        start()
        finish()

    theirs = pl.pallas_call(
        body, name=name, in_specs=[ANY] * n, out_specs=[ANY] * n, out_shape=_pair_swap_out_shapes(grads),
        scratch_shapes=_pair_swap_scratch(n),
    )(*grads)
    return _own_halves(grads), theirs


def _pair_swap_out_shapes(grads):
    return [jax.ShapeDtypeStruct((4, g.shape[1] // 2, g.shape[2]), g.dtype) for g in grads]


def _pair_swap_scratch(n):
    return [pltpu.SemaphoreType.DMA((n,)), pltpu.SemaphoreType.DMA((n,))]


def _own_halves(grads):
    c = lax.axis_index("c")
    return [lax.dynamic_slice_in_dim(g, c * (g.shape[1] // 2), g.shape[1] // 2, axis=1) for g in grads]


def _pair_swap_phases(ins, theirs, send_sems, recv_sems):
    n = len(ins)
    x, y, c, _ = _place()
    sibling = (x, y, 1 - c)

    def start():
        for a in range(n):
            rows = ins[a].shape[1] // 2
            for j in range(4):
                for r, nr in _chunks(rows):
                    pltpu.make_async_remote_copy(
                        src_ref=ins[a].at[j, pl.ds((1 - c) * rows + r, nr), :], dst_ref=theirs[a].at[j, pl.ds(r, nr), :],
                        send_sem=send_sems.at[a], recv_sem=recv_sems.at[a], device_id=sibling, device_id_type=MESH).start()

    def finish():
        for a in range(n):
            pltpu.make_async_remote_copy(src_ref=theirs[a], dst_ref=theirs[a], send_sem=send_sems.at[a],
                                         recv_sem=recv_sems.at[a], device_id=sibling, device_id_type=MESH).wait()

    return start, finish


def _scatter_scratch(n):
    return [pltpu.SemaphoreType.DMA((n, 3)), pltpu.SemaphoreType.DMA((n, 3))]


def _keep_own_blocks(landed, parts):
    if not parts:
        return []
    chip = 2 * lax.axis_index("x") + lax.axis_index("y")
    return [lax.dynamic_update_slice(l, lax.dynamic_slice_in_dim(p, chip, 1, axis=0), (chip, 0, 0))
            for l, p in zip(landed, parts)]


def _scatter_phases(ins, outs, send_sems, recv_sems):
    n = len(ins)
    x, y, c, chips = _place()
    me = 2 * x + y
    blks = [2 * cx + cy for cx, cy in chips]

    def start():
        for a in range(n):
            for r, nr in _chunks(ins[a].shape[1]):
                for t in range(3):
                    pltpu.make_async_remote_copy(
                        src_ref=ins[a].at[blks[t], pl.ds(r, nr), :], dst_ref=outs[a].at[me, pl.ds(r, nr), :],
                        send_sem=send_sems.at[a, t], recv_sem=recv_sems.at[a, t],
                        device_id=(*chips[t], c), device_id_type=MESH).start()

    def finish():
        for a in range(n):
            for t in range(3):
                pltpu.make_async_remote_copy(
                    src_ref=outs[a].at[blks[t]], dst_ref=outs[a].at[blks[t]], send_sem=send_sems.at[a, t],
                    recv_sem=recv_sems.at[a, t], device_id=(*chips[t], c), device_id_type=MESH).wait()

    return start, finish


def pair_join_halves(bufs):
    n = len(bufs)

    def body(*refs):
        outs = refs[n:2 * n]
        send_sems, recv_sems = refs[2 * n:]
        x, y, c, _ = _place()
        sibling = (x, y, 1 - c)
        for a in range(n):
            rows = outs[a].shape[0] // 2
            for r, nr in _chunks(rows):
                mine = outs[a].at[pl.ds(c * rows + r, nr), :]
                pltpu.make_async_remote_copy(src_ref=mine, dst_ref=mine, send_sem=send_sems.at[a], recv_sem=recv_sems.at[a],
                                             device_id=sibling, device_id_type=MESH).start()
        for a in range(n):
            rows = outs[a].shape[0] // 2
            pltpu.make_async_remote_copy(
                src_ref=outs[a].at[pl.ds(c * rows, rows), :], dst_ref=outs[a].at[pl.ds((1 - c) * rows, rows), :],
                send_sem=send_sems.at[a], recv_sem=recv_sems.at[a], device_id=sibling, device_id_type=MESH).wait()

    return pl.pallas_call(
        body, name="pair_join_halves", in_specs=[ANY] * n, out_specs=[ANY] * n,
        out_shape=[jax.ShapeDtypeStruct(b.shape, b.dtype) for b in bufs], input_output_aliases={a: a for a in range(n)},
        scratch_shapes=[pltpu.SemaphoreType.DMA((n,)), pltpu.SemaphoreType.DMA((n,))],
    )(*bufs)


def _allreduce_scratch(rows):
    return [pltpu.VMEM((8, rows, 128), F32), pltpu.SemaphoreType.DMA((7,)), pltpu.SemaphoreType.DMA((7,))]


def _allreduce_phases(in_ref, out_ref, gathered, send_sems, recv_sems):
    x, y, c, _ = _place()
    me = 4 * x + 2 * y + c
    flips = [(fx, fy, fc) for fx in (0, 1) for fy in (0, 1) for fc in (0, 1)][1:]
    peers = [((1 - x) if fx else x, (1 - y) if fy else y, (1 - c) if fc else c) for fx, fy, fc in flips]

    def send(t):
        return pltpu.make_async_remote_copy(
            src_ref=in_ref, dst_ref=gathered.at[me], send_sem=send_sems.at[t], recv_sem=recv_sems.at[t],
            device_id=peers[t], device_id_type=MESH)

    def start():
        gathered[me] = in_ref[...]
        for t in range(7):
            send(t).start()

    def finish():
        for t, (px, py, pc) in enumerate(peers):
            slot = gathered.at[4 * px + 2 * py + pc]
            pltpu.make_async_remote_copy(
                src_ref=slot, dst_ref=slot, send_sem=send_sems.at[t], recv_sem=recv_sems.at[t],
                device_id=(px, py, pc), device_id_type=MESH).wait_recv()
        for t in range(7):
            send(t).wait_send()
        acc = gathered[0]
        for k in range(1, 8):
            acc = acc + gathered[k]
        out_ref[...] = acc

    return start, finish


SMALL_NAMES = ("norm_mix_w", "ssd_conv_w", "ssd_conv_b", "ssd_dt_bias", "ssd_a_log", "ssd_d", "ssd_norm_w", "fox_f_bias",
               "fox_q_norm_w", "fox_k_norm_w", "norm_ffn_w", "ffn_conv_w", "ffn_conv_b")
BIG_NAMES = ("w_in", "w_out", "w_up", "w_down")
WEIGHT_ORDER = ("norm_mix_w", "w_in", "ssd_conv_w", "ssd_conv_b", "ssd_dt_bias", "ssd_a_log", "ssd_d", "ssd_norm_w",
                "fox_f_bias", "fox_q_norm_w", "fox_k_norm_w", "w_out", "norm_ffn_w", "w_up", "ffn_conv_w", "ffn_conv_b", "w_down")
ADAM_ROWS = {"w_in": 256, "w_out": 256, "w_up": 256, "w_down": 176}


def _pack(arrays):
    pieces = []
    for a in arrays:
        flat = a.reshape(-1).astype(F32)
        pieces += [flat, jnp.zeros(((-flat.shape[0]) % 1024,), F32)]
    return jnp.concatenate(pieces).reshape(-1, 128)


def _unpack(packed, shapes):
    out, r = [], 0
    for shp in shapes:
        size = 1
        for d in shp:
            size *= d
        nrow = 8 * (-(-size // 1024))
        out.append(packed[r:r + nrow].reshape(-1)[:size].reshape(shp))
        r += nrow
    return out


IN_SHARD = IN_COLS // 4
IN_SEGMENTS = ((0, 2048, 0), (2048, 2560, 5120), (2560, 2576, MAIN_COLS), (2576, 5648, 2048), (5648, 5664, MAIN_COLS + F_LANE))


def _in_cols(shards, lo, hi):
    out = []
    for j in range(4):
        a, b = max(lo, IN_SHARD * j), min(hi, IN_SHARD * (j + 1))
        if a < b:
            out.append(shards[j][:, a - IN_SHARD * j:b - IN_SHARD * j])
    return out


def _in_grad_shards(g):
    shards = []
    for j in range(4):
        pieces = []
        for lo, hi, at in IN_SEGMENTS:
            a, b = max(lo, IN_SHARD * j), min(hi, IN_SHARD * (j + 1))
            if a < b:
                pieces.append(g[:, at + a - lo:at + b - lo])
        shards.append(jnp.concatenate(pieces, axis=1))
    return jnp.stack(shards)


def _pad_rows(a, rows):
    return jnp.pad(a, ((0, rows - a.shape[0]), (0, 0)))


def kernel(x, norm_mix_w, w_in, ssd_conv_w, ssd_conv_b, ssd_dt_bias, ssd_a_log, ssd_d, ssd_norm_w, fox_f_bias, fox_q_norm_w, fox_k_norm_w, w_out, norm_ffn_w, w_up, ffn_conv_w, ffn_conv_b, w_down, loss_target, m_norm_mix_w, m_w_in, m_ssd_conv_w, m_ssd_conv_b, m_ssd_dt_bias, m_ssd_a_log, m_ssd_d, m_ssd_norm_w, m_fox_f_bias, m_fox_q_norm_w, m_fox_k_norm_w, m_w_out, m_norm_ffn_w, m_w_up, m_ffn_conv_w, m_ffn_conv_b, m_w_down, v_norm_mix_w, v_w_in, v_ssd_conv_w, v_ssd_conv_b, v_ssd_dt_bias, v_ssd_a_log, v_ssd_d, v_ssd_norm_w, v_fox_f_bias, v_fox_q_norm_w, v_fox_k_norm_w, v_w_out, v_norm_ffn_w, v_w_up, v_ffn_conv_w, v_ffn_conv_b, v_w_down):
    w = dict(norm_mix_w=norm_mix_w, w_in=w_in, ssd_conv_w=ssd_conv_w, ssd_conv_b=ssd_conv_b, ssd_dt_bias=ssd_dt_bias,
             ssd_a_log=ssd_a_log, ssd_d=ssd_d, ssd_norm_w=ssd_norm_w, fox_f_bias=fox_f_bias, fox_q_norm_w=fox_q_norm_w,
             fox_k_norm_w=fox_k_norm_w, w_out=w_out, norm_ffn_w=norm_ffn_w, w_up=w_up, ffn_conv_w=ffn_conv_w,
             ffn_conv_b=ffn_conv_b, w_down=w_down)
    m = dict(norm_mix_w=m_norm_mix_w, w_in=m_w_in, ssd_conv_w=m_ssd_conv_w, ssd_conv_b=m_ssd_conv_b, ssd_dt_bias=m_ssd_dt_bias,
             ssd_a_log=m_ssd_a_log, ssd_d=m_ssd_d, ssd_norm_w=m_ssd_norm_w, fox_f_bias=m_fox_f_bias, fox_q_norm_w=m_fox_q_norm_w,
             fox_k_norm_w=m_fox_k_norm_w, w_out=m_w_out, norm_ffn_w=m_norm_ffn_w, w_up=m_w_up, ffn_conv_w=m_ffn_conv_w,
             ffn_conv_b=m_ffn_conv_b, w_down=m_w_down)
    v = dict(norm_mix_w=v_norm_mix_w, w_in=v_w_in, ssd_conv_w=v_ssd_conv_w, ssd_conv_b=v_ssd_conv_b, ssd_dt_bias=v_ssd_dt_bias,
             ssd_a_log=v_ssd_a_log, ssd_d=v_ssd_d, ssd_norm_w=v_ssd_norm_w, fox_f_bias=v_fox_f_bias, fox_q_norm_w=v_fox_q_norm_w,
             fox_k_norm_w=v_fox_k_norm_w, w_out=v_w_out, norm_ffn_w=v_norm_ffn_w, w_up=v_w_up, ffn_conv_w=v_ffn_conv_w,
             ffn_conv_b=v_ffn_conv_b, w_down=v_w_down)
    chip = 2 * lax.axis_index("x") + lax.axis_index("y")

    a_in, a_scw, a_fcw = gather_weights([w_in[0].astype(BF16), _pad_rows(ssd_conv_w[0], 16), _pad_rows(ffn_conv_w[0], 16)])
    later_shards = [w_out[0].astype(BF16), w_up[0].astype(BF16), w_down[0].astype(BF16)]
    wx = jnp.concatenate([p for lo, hi, _ in sorted(IN_SEGMENTS, key=lambda seg: seg[2]) for p in _in_cols(a_in, lo, hi)]
                         + [jnp.zeros((D_MODEL, PROJ_COLS - IN_COLS), BF16)], axis=1)
    ssd_cw8 = a_scw.transpose(1, 0, 2).reshape(16, 1536)[:8]
    ffn_cw8 = a_fcw.transpose(1, 0, 2).reshape(16, 2 * D_FF)[:8]
    smallp = jnp.zeros((8, 128), F32)
    smallp = smallp.at[0, :16].set(ssd_dt_bias[0]).at[1, :16].set(ssd_a_log[0]).at[2, :16].set(ssd_d[0])
    smallp = smallp.at[3, F_LANE:F_LANE + 16].set(fox_f_bias[0])
    qw_t = jnp.tile(fox_q_norm_w[0], N_HEADS)[None]
    kw_t = jnp.tile(fox_k_norm_w[0], N_HEADS)[None]
    sel = jnp.asarray((np.arange(1024)[:, None] // HEAD_DIM == np.arange(128)[None, :]).astype(np.float32), BF16)

    res = local_step(x[0], loss_target[0], wx, later_shards, ssd_cw8, ssd_conv_b, smallp, ssd_norm_w, qw_t, kw_t,
                     sel, sel.T, norm_mix_w, norm_ffn_w, ffn_cw8, ffn_conv_b)

    full_shapes = [(1, 1024), (1, 4, 1536), (1, 1536), (1, 16), (1, 16), (1, 16), (1, 1024), (1, 16), (1, 64), (1, 64),
                   (1, 1024), (1, 3, 2 * D_FF), (1, 2 * D_FF), (1,)]
    local_small = [res["g_norm_mix"], res["g_ssd_cw"][:4], res["g_ssd_cw"][4], res["g_sp"][0, :16], res["g_sp"][1, :16],
                   res["g_sp"][2, :16], res["g_ssd_nw"], res["g_fb"][0, F_LANE:F_LANE + 16],
                   res["g_qw"].reshape(N_HEADS, HEAD_DIM).sum(0), res["g_kw"].reshape(N_HEADS, HEAD_DIM).sum(0),
                   res["g_norm_ffn"], res["g_ffn_cw"][:3], res["g_ffn_cw"][3], jnp.sum(res["sq"])]
    landed = res["landed"]
    core = lax.axis_index("c").astype(jnp.int32).reshape(1)
    halves = [sum_chips(p, core, name="sum_chips_" + n, tr=ADAM_ROWS[n]) for p, n in zip(landed, BIG_NAMES)]
    g_big = dict(zip(BIG_NAMES, pair_join_halves(halves)))

    grads, deltas, new_m, new_v = {}, {}, {}, {}
    for n in BIG_NAMES:
        out = adamw(w[n][0], g_big[n], m[n][0], v[n][0], name="adamw_" + n, tr=ADAM_ROWS[n],
                    allreduce=_pack(local_small) if n == BIG_NAMES[0] else None)
        if n == BIG_NAMES[0]:
            summed = _unpack(out[3], full_shapes)
        d, mn, vn = out[:3]
        grads[n], deltas[n], new_m[n], new_v[n] = g_big[n][None], d[None], mn[None], vn[None]
    loss = (0.5 / D_MODEL) * summed[-1][0]
    g_small = dict(zip(SMALL_NAMES, summed[:-1]))
    g_small["ssd_conv_w"] = lax.dynamic_slice(g_small["ssd_conv_w"], (0, 0, 384 * chip), (1, 4, 384))
    g_small["ffn_conv_w"] = lax.dynamic_slice(g_small["ffn_conv_w"], (0, 0, 1408 * chip), (1, 3, 1408))
    shapes = [w[n].shape for n in SMALL_NAMES]
    packed_w = _pack([w[n] for n in SMALL_NAMES])
    d, mn, vn = adamw(packed_w, _pack([g_small[n] for n in SMALL_NAMES]), _pack([m[n] for n in SMALL_NAMES]),
                      _pack([v[n] for n in SMALL_NAMES]), name="adamw_small", tr=packed_w.shape[0])
    for n, dd, mm, vv in zip(SMALL_NAMES, _unpack(d, shapes), _unpack(mn, shapes), _unpack(vn, shapes)):
        grads[n], deltas[n], new_m[n], new_v[n] = g_small[n].reshape(w[n].shape), dd, mm, vv
    return (loss, res["grad_x"][None], *[grads[n] for n in WEIGHT_ORDER], *[deltas[n] for n in WEIGHT_ORDER],
            *[new_m[n] for n in WEIGHT_ORDER], *[new_v[n] for n in WEIGHT_ORDER])
```

```python
import functools

import jax
import jax.numpy as jnp
import numpy as np
from jax import lax
from jax.experimental import pallas as pl
from jax.experimental.pallas import tpu as pltpu

F32 = jnp.float32
BF16 = jnp.bfloat16
MESH = pl.DeviceIdType.MESH

D_MODEL = 1024
HEAD_DIM = 64
N_HEADS = 16
N_PAIRS = N_HEADS // 2
SSD_CHUNK = 128
SSD_STATE = 128
SSD_CONV = 4
D_FF = 2816
FFN_CONV = 3
NORM_EPS = 1e-6
MAIN_COLS = 5632
SMALL_COLS = 128
PROJ_COLS = MAIN_COLS + SMALL_COLS
SMALL_BLOCK = MAIN_COLS // SMALL_COLS
PROJ_TILE = 1152
F_LANE = 16
IN_COLS = 5664

ADAM_LR = 0.001
ADAM_B1 = 0.9
ADAM_B2 = 0.999
ADAM_EPS = 1e-08
ADAM_WD = 0.01
ADAM_STEP = 10

VMEM_LIMIT_V7X = 56 * 1024 * 1024
NEG_BIG = -1e30


def _params(sem=None):
    return pltpu.CompilerParams(dimension_semantics=sem, vmem_limit_bytes=VMEM_LIMIT_V7X)


def _sigmoid(x):
    return 1.0 / (1.0 + jnp.exp(-x))


def _silu_and_grad(x):
    s = _sigmoid(x)
    return x * s, s * (1.0 + x * (1.0 - s))


def _shift_down(v, j):
    return v if j == 0 else pltpu.roll(v, j, 0)


def _shift_up(v, j):
    return v if j == 0 else pltpu.roll(v, v.shape[0] - j, 0)


def _row_iota(shape):
    return lax.broadcasted_iota(jnp.int32, shape, 0)


def _lane_iota(shape):
    return lax.broadcasted_iota(jnp.int32, shape, 1)


def _dot(a, b, mode="nn"):
    dims = {"nn": (((1,), (0,)), ((), ())), "nt": (((1,), (1,)), ((), ())), "tn": (((0,), (0,)), ((), ()))}[mode]
    return lax.dot_general(a.astype(BF16), b.astype(BF16), dims, preferred_element_type=F32)


def _dot_f32(a, b):
    return jnp.dot(a, b, precision=lax.Precision.HIGHEST, preferred_element_type=F32)


def matmul(a, b, *, mode, tm, tn, tk, out_dtype, name, add=None, b_koff=0, scatter=(), layout=None):
    layout = layout or {}
    if layout:
        m, n, k = layout["m"], layout["n"], layout["k"]
    else:
        (m, k), n = a.shape, (b.shape[1] if mode == "nn" else b.shape[0])
    assert m % tm == 0 and n % tn == 0 and k % tk == 0, (name, m, n, k, tm, tn, tk)
    nk = k // tk
    grid = (m // tm, n // tn, nk)
    a_spec = layout.get("a_spec") or pl.BlockSpec((tm, tk), lambda i, j, kk: (i, kk))
    b_spec = layout.get("b_spec") or (pl.BlockSpec((tn, tk), lambda i, j, kk: (j, kk + b_koff)) if mode == "nt"
                                      else pl.BlockSpec((tk, tn), lambda i, j, kk: (kk + b_koff, j)))
    o_spec = layout.get("o_spec") or pl.BlockSpec((tm, tn), lambda i, j, kk: (i, j))
    out_struct = jax.ShapeDtypeStruct(layout.get("out_shape", (m, n)), out_dtype)
    has_add = add is not None
    n_in = 3 if has_add else 2
    ns = len(scatter)

    def body(*refs):
        a_ref, b_ref = refs[:2]
        add_ref = refs[2] if has_add else None
        o_ref, acc_ref = refs[n_in + ns], refs[n_in + 2 * ns + 1]
        kk = pl.program_id(2)
        if ns:
            step = (pl.program_id(0) * grid[1] + pl.program_id(1)) * grid[2] + kk
            start, finish_copies = _scatter_phases(refs[n_in:n_in + ns], refs[n_in + ns + 1:n_in + 2 * ns + 1],
                                                   *refs[n_in + 2 * ns + 2:])
            pl.when(step == 0)(start)
        part = _dot(a_ref[...], b_ref[...], mode)

        def finish(total):
            if has_add:
                total = total + add_ref[...]
            o_ref[...] = total.astype(out_dtype)

        if nk == 1:
            finish(part)
        else:
            @pl.when(kk == 0)
            def _():
                acc_ref[...] = part

            @pl.when(jnp.logical_and(kk > 0, kk < nk - 1))
            def _():
                acc_ref[...] += part

            @pl.when(kk == nk - 1)
            def _():
                finish(acc_ref[...] + part)

        if ns:
            pl.when(step == grid[0] * grid[1] * grid[2] - 1)(finish_copies)

    in_specs = [a_spec, b_spec] + ([o_spec] if has_add else [])
    args = (a, b) + ((add,) if has_add else ())
    acc = pltpu.VMEM((tm, tn) if nk > 1 else (8, 128), F32)
    if not ns:
        return pl.pallas_call(
            body, name=name, grid=grid, in_specs=in_specs, out_specs=o_spec, out_shape=out_struct,
            scratch_shapes=[acc], compiler_params=_params(("parallel", "parallel", "arbitrary")),
        )(*args)
    outs = pl.pallas_call(
        body, name=name, grid=grid, in_specs=in_specs + [ANY] * ns, out_specs=[o_spec] + [ANY] * ns,
        out_shape=[out_struct] + [jax.ShapeDtypeStruct(p.shape, p.dtype) for p in scatter],
        scratch_shapes=[acc] + _scatter_scratch(ns), compiler_params=_params(("arbitrary", "arbitrary", "arbitrary")),
    )(*args, *scatter)
    return outs[0], _keep_own_blocks(outs[1:], scatter)


def rms_fwd(x, w, *, name, tm=1024):
    s, d = x.shape

    def body(x_ref, w_ref, h_ref, ht_ref):
        xv = x_ref[...]
        r = lax.rsqrt(jnp.mean(xv * xv, axis=-1, keepdims=True) + NORM_EPS)
        h = (xv * r) * w_ref[...]
        h_ref[...] = h.astype(BF16)
        ht_ref[...] = h.T.astype(BF16)

    return pl.pallas_call(
        body, name=name, grid=(s // tm,),
        in_specs=[pl.BlockSpec((tm, d), lambda i: (i, 0)), pl.BlockSpec((1, d), lambda i: (0, 0))],
        out_specs=[pl.BlockSpec((tm, d), lambda i: (i, 0)), pl.BlockSpec((d, tm), lambda i: (0, i))],
        out_shape=[jax.ShapeDtypeStruct((s, d), BF16), jax.ShapeDtypeStruct((d, s), BF16)],
        compiler_params=_params(("parallel",)),
    )(x, w)


def rms_bwd(dh, x, w, resid, *, name, tm=1024):
    s, d = x.shape

    def body(dh_ref, x_ref, w_ref, res_ref, dx_ref, dw_ref):
        xv = x_ref[...]
        dhv = dh_ref[...]
        r = lax.rsqrt(jnp.mean(xv * xv, axis=-1, keepdims=True) + NORM_EPS)
        xh = xv * r
        g = dhv * w_ref[...]
        dx_ref[...] = res_ref[...] + r * (g - xh * jnp.mean(g * xh, axis=-1, keepdims=True))
        part = jnp.sum(dhv * xh, axis=0, keepdims=True)

        @pl.when(pl.program_id(0) == 0)
        def _():
            dw_ref[...] = part

        @pl.when(pl.program_id(0) > 0)
        def _():
            dw_ref[...] += part

    row = pl.BlockSpec((tm, d), lambda i: (i, 0))
    vec = pl.BlockSpec((1, d), lambda i: (0, 0))
    return pl.pallas_call(
        body, name=name, grid=(s // tm,), in_specs=[row, row, vec, row], out_specs=[row, vec],
        out_shape=[jax.ShapeDtypeStruct((s, d), F32), jax.ShapeDtypeStruct((1, d), F32)],
        compiler_params=_params(("arbitrary",)),
    )(dh, x, w, resid)


def loss_head(y, target, *, tm=1024):
    s, d = y.shape

    def body(y_ref, t_ref, dy_ref, sq_ref):
        e = y_ref[...] - t_ref[...]
        dy_ref[...] = e / float(d)
        part = jnp.sum(e * e, axis=0, keepdims=True)

        @pl.when(pl.program_id(0) == 0)
        def _():
            sq_ref[...] = part

        @pl.when(pl.program_id(0) > 0)
        def _():
            sq_ref[...] += part

    row = pl.BlockSpec((tm, d), lambda i: (i, 0))
    vec = pl.BlockSpec((1, d), lambda i: (0, 0))
    return pl.pallas_call(
        body, name="loss_head", grid=(s // tm,), in_specs=[row, row], out_specs=[row, vec],
        out_shape=[jax.ShapeDtypeStruct((s, d), F32), jax.ShapeDtypeStruct((1, d), F32)],
        compiler_params=_params(("arbitrary",)),
    )(y, target)


def _row_shifts(ext, k_taps):
    return [_shift_down(ext, j) for j in range(k_taps)]


def _conv_rows(shifts, w):
    k_taps = len(shifts)
    acc = w[k_taps - 1:k_taps, :] * shifts[0]
    for k in range(k_taps - 1):
        acc = acc + w[k:k + 1, :] * shifts[k_taps - 1 - k]
    return acc


def _conv_weight_grad(dcur, shifts, rows, width):
    k_taps = len(shifts)
    out = [jnp.sum(dcur * shifts[k_taps - 1 - k][rows], axis=0, keepdims=True) for k in range(k_taps)]
    out.append(jnp.sum(dcur, axis=0, keepdims=True))
    return _stack_rows(out, width)


def _conv_rows_transposed(dext, w, k_taps):
    acc = w[k_taps - 1:k_taps, :] * dext
    for k in range(k_taps - 1):
        acc = acc + w[k:k + 1, :] * _shift_up(dext, k_taps - 1 - k)
    return acc


def _stack_rows(rows, width):
    ri = _row_iota((8, width))
    out = jnp.zeros((8, width), F32)
    for k, r in enumerate(rows):
        out = out + jnp.where(ri == k, r, 0.0)
    return out


def ffn_mid_fwd(hu, conv_w8, conv_b, *, tm=1024, tc=256):
    s = hu.shape[0]
    ncol = D_FF // tc
    r8 = tm // 8

    def body(g_ref, v_ref, gp_ref, vp_ref, wg_ref, wv_ref, bg_ref, bv_ref, o_ref, ot_ref):
        first = pl.program_id(1) == 0

        def conv(cur_ref, prev_ref, w_ref, b_ref):
            prev = jnp.where(first, 0.0, prev_ref[...])
            ext = jnp.concatenate([prev, cur_ref[...]], axis=0)
            return _conv_rows(_row_shifts(ext, FFN_CONV), w_ref[...])[8:] + b_ref[...]

        gc = conv(g_ref, gp_ref, wg_ref, bg_ref)
        vc = conv(v_ref, vp_ref, wv_ref, bv_ref)
        act = gc * _sigmoid(gc) * vc
        o_ref[...] = act.astype(BF16)
        ot_ref[...] = act.T.astype(BF16)

    def prev_idx(i):
        return jnp.maximum(i * r8 - 1, 0)

    in_specs = [
        pl.BlockSpec((tm, tc), lambda j, i: (i, j)),
        pl.BlockSpec((tm, tc), lambda j, i: (i, j + ncol)),
        pl.BlockSpec((8, tc), lambda j, i: (prev_idx(i), j)),
        pl.BlockSpec((8, tc), lambda j, i: (prev_idx(i), j + ncol)),
        pl.BlockSpec((8, tc), lambda j, i: (0, j)),
        pl.BlockSpec((8, tc), lambda j, i: (0, j + ncol)),
        pl.BlockSpec((1, tc), lambda j, i: (0, j)),
        pl.BlockSpec((1, tc), lambda j, i: (0, j + ncol)),
    ]
    return pl.pallas_call(
        body, name="ffn_mid_fwd", grid=(ncol, s // tm), in_specs=in_specs,
        out_specs=[pl.BlockSpec((tm, tc), lambda j, i: (i, j)), pl.BlockSpec((tc, tm), lambda j, i: (j, i))],
        out_shape=[jax.ShapeDtypeStruct((s, D_FF), BF16), jax.ShapeDtypeStruct((D_FF, s), BF16)],
        compiler_params=_params(("parallel", "parallel")),
    )(hu, hu, hu, hu, conv_w8, conv_w8, conv_b, conv_b)


def ffn_mid_bwd(hu, dact, conv_w8, conv_b, *, tm=1024, tc=256):
    s = hu.shape[0]
    ncol = D_FF // tc
    nrow = s // tm
    r8 = tm // 8

    def body(g_ref, v_ref, gp_ref, vp_ref, gn_ref, vn_ref, da_ref, dan_ref, wg_ref, wv_ref, bg_ref, bv_ref,
             dhu_ref, wgo_ref, wvo_ref):
        i = pl.program_id(1)
        first = i == 0
        last = i == nrow - 1

        def ext_of(cur_ref, prev_ref, next_ref):
            prev = jnp.where(first, 0.0, prev_ref[...])
            return jnp.concatenate([prev, cur_ref[...], next_ref[...]], axis=0)

        g_sh = _row_shifts(ext_of(g_ref, gp_ref, gn_ref), FFN_CONV)
        v_sh = _row_shifts(ext_of(v_ref, vp_ref, vn_ref), FFN_CONV)
        gc = _conv_rows(g_sh, wg_ref[...]) + bg_ref[...]
        vc = _conv_rows(v_sh, wv_ref[...]) + bv_ref[...]
        da_ext = jnp.concatenate([jnp.zeros((8, tc), F32), da_ref[...], jnp.where(last, 0.0, dan_ref[...])], axis=0)
        silu, dsilu = _silu_and_grad(gc)
        dgc = da_ext * vc * dsilu
        dvc = da_ext * silu
        dhu_ref[0] = _conv_rows_transposed(dgc, wg_ref[...], FFN_CONV)[8:8 + tm].astype(BF16)
        dhu_ref[1] = _conv_rows_transposed(dvc, wv_ref[...], FFN_CONV)[8:8 + tm].astype(BF16)

        cur = slice(8, 8 + tm)
        pg = _conv_weight_grad(dgc[cur], g_sh, cur, tc)
        pv = _conv_weight_grad(dvc[cur], v_sh, cur, tc)

        @pl.when(first)
        def _():
            wgo_ref[...] = pg
            wvo_ref[...] = pv

        @pl.when(i > 0)
        def _():
            wgo_ref[...] += pg
            wvo_ref[...] += pv

    def prev_idx(i):
        return jnp.maximum(i * r8 - 1, 0)

    def next_idx(i):
        return jnp.minimum((i + 1) * r8, s // 8 - 1)

    cur_g = pl.BlockSpec((tm, tc), lambda j, i: (i, j))
    cur_v = pl.BlockSpec((tm, tc), lambda j, i: (i, j + ncol))
    in_specs = [
        cur_g, cur_v,
        pl.BlockSpec((8, tc), lambda j, i: (prev_idx(i), j)),
        pl.BlockSpec((8, tc), lambda j, i: (prev_idx(i), j + ncol)),
        pl.BlockSpec((8, tc), lambda j, i: (next_idx(i), j)),
        pl.BlockSpec((8, tc), lambda j, i: (next_idx(i), j + ncol)),
        cur_g,
        pl.BlockSpec((8, tc), lambda j, i: (next_idx(i), j)),
        pl.BlockSpec((8, tc), lambda j, i: (0, j)),
        pl.BlockSpec((8, tc), lambda j, i: (0, j + ncol)),
        pl.BlockSpec((1, tc), lambda j, i: (0, j)),
        pl.BlockSpec((1, tc), lambda j, i: (0, j + ncol)),
    ]
    out_specs = [pl.BlockSpec((2, tm, tc), lambda j, i: (0, i, j)), pl.BlockSpec((8, tc), lambda j, i: (0, j)),
                 pl.BlockSpec((8, tc), lambda j, i: (0, j))]
    out_shape = [jax.ShapeDtypeStruct((2, s, D_FF), BF16),
                 jax.ShapeDtypeStruct((8, D_FF), F32), jax.ShapeDtypeStruct((8, D_FF), F32)]
    return pl.pallas_call(
        body, name="ffn_mid_bwd", grid=(ncol, nrow), in_specs=in_specs, out_specs=out_specs, out_shape=out_shape,
        compiler_params=_params(("parallel", "arbitrary")),
    )(hu, hu, hu, hu, hu, hu, dact, dact, conv_w8, conv_w8, conv_b, conv_b)


def _softplus(x):
    return jnp.maximum(x, 0.0) + jnp.log(1.0 + jnp.exp(-jnp.abs(x)))


def _cumsum_rows(v):
    n = v.shape[0]
    ri = _row_iota(v.shape)
    sh = 1
    while sh < n:
        v = v + jnp.where(ri >= sh, _shift_down(v, sh), 0.0)
        sh *= 2
    return v


def _rev_cumsum_rows(v):
    n = v.shape[0]
    ri = _row_iota(v.shape)
    sh = 1
    while sh < n:
        v = v + jnp.where(ri < n - sh, _shift_up(v, sh), 0.0)
        sh *= 2
    return v


def _total(v):
    return jnp.sum(jnp.sum(v, axis=1, keepdims=True), axis=0, keepdims=True)


def _ssd_in_specs(rev_nc=None):
    def ch(c):
        return c if rev_nc is None else rev_nc - 1 - c

    def prev(c):
        return jnp.maximum(ch(c) * (SSD_CHUNK // 8) - 1, 0)

    L = SSD_CHUNK
    return [
        pl.BlockSpec((L, 1024), lambda c: (ch(c), 0)),
        pl.BlockSpec((L, 1024), lambda c: (ch(c), 1)),
        pl.BlockSpec((L, 256), lambda c: (ch(c), 20)),
        pl.BlockSpec((L, 256), lambda c: (ch(c), 21)),
        pl.BlockSpec((8, 1024), lambda c: (prev(c), 1)),
        pl.BlockSpec((8, 256), lambda c: (prev(c), 20)),
        pl.BlockSpec((8, 256), lambda c: (prev(c), 21)),
        pl.BlockSpec((8, 1024), lambda c: (0, 0)),
        pl.BlockSpec((8, 256), lambda c: (0, 4)),
        pl.BlockSpec((8, 256), lambda c: (0, 5)),
        pl.BlockSpec((1, 1024), lambda c: (0, 0)),
        pl.BlockSpec((1, 256), lambda c: (0, 4)),
        pl.BlockSpec((1, 256), lambda c: (0, 5)),
        pl.BlockSpec((L, SMALL_COLS), lambda c: (ch(c), SMALL_BLOCK)),
        pl.BlockSpec((8, 128), lambda c: (0, 0)),
        pl.BlockSpec((1, 1024), lambda c: (0, 0)),
    ]


def _ssd_conv_pre(cur_ref, prev_ref, w_ref, b_ref, first):
    prev = jnp.where(first, 0.0, prev_ref[...])
    shifts = _row_shifts(jnp.concatenate([prev, cur_ref[...]], axis=0), SSD_CONV)
    return shifts, _conv_rows(shifts, w_ref[...])[8:] + b_ref[...]


def _ssd_time_consts(small_ref, sp_ref):
    dt_pre = small_ref[...] + sp_ref[0:1, :]
    dt = _softplus(dt_pre)
    a = -jnp.exp(sp_ref[1:2, :])
    acs = _cumsum_rows(dt * a)
    return dt_pre, dt, a, acs


def ssd_fwd(proj, conv_w8, conv_b, smallp, norm_w):
    s = proj.shape[0]
    nc = s // SSD_CHUNK
    L = SSD_CHUNK

    def body(z_ref, xs_ref, b_ref, c_ref, xsp_ref, bp_ref, cp_ref, wx_ref, wb_ref, wc_ref, bx_ref, bb_ref, bc_ref,
             small_ref, sp_ref, nw_ref, y_ref, yt_ref, ypre_ref, st_ref, state):
        first = pl.program_id(0) == 0

        @pl.when(first)
        def _():
            state[...] = jnp.zeros_like(state)

        xs = _ssd_conv_pre(xs_ref, xsp_ref, wx_ref, bx_ref, first)[1]
        xs = xs * _sigmoid(xs)
        bm = _ssd_conv_pre(b_ref, bp_ref, wb_ref, bb_ref, first)[1]
        bm = bm * _sigmoid(bm)
        cm = _ssd_conv_pre(c_ref, cp_ref, wc_ref, bc_ref, first)[1]
        cm = cm * _sigmoid(cm)
        _, dt, _, acs = _ssd_time_consts(small_ref, sp_ref)
        acs_t = acs.T
        li = _lane_iota((L, L))
        ri = _row_iota((L, L))
        tri = ri >= li
        lo = li < HEAD_DIM
        st_ref[0] = state[...]
        for g in range(2):
            bg = bm[:, 128 * g:128 * g + 128]
            cg = cm[:, 128 * g:128 * g + 128]
            gmat = _dot(cg, bg, "nt")
            for pp in range(4):
                p = 4 * g + pp
                h0, h1 = 2 * p, 2 * p + 1
                x = xs[:, 128 * p:128 * p + 128]
                a0, a1 = acs[:, h0:h0 + 1], acs[:, h1:h1 + 1]
                xdt = x * jnp.where(lo, dt[:, h0:h0 + 1], dt[:, h1:h1 + 1])
                m0 = gmat * jnp.exp(jnp.where(tri, a0 - acs_t[h0:h0 + 1, :], NEG_BIG))
                m1 = gmat * jnp.exp(jnp.where(tri, a1 - acs_t[h1:h1 + 1, :], NEG_BIG))
                yd = _dot(m0, jnp.where(lo, xdt, 0.0)) + _dot(m1, jnp.where(lo, 0.0, xdt))
                hin = state[p]
                yo = _dot(cg, hin, "nt") * jnp.exp(jnp.where(lo, a0, a1))
                dskip = jnp.where(lo[0:1], sp_ref[2:3, h0:h0 + 1], sp_ref[2:3, h1:h1 + 1])
                ypre_ref[:, 128 * p:128 * p + 128] = yd + yo + dskip * x
                al0, al1 = acs[L - 1:L, h0:h0 + 1], acs[L - 1:L, h1:h1 + 1]
                w = jnp.exp(jnp.where(lo, al0 - a0, al1 - a1))
                dec = jnp.exp(jnp.where(ri < HEAD_DIM, al0, al1))
                state[p] = dec * hin + _dot(xdt * w, bg, "tn")
        z = z_ref[...]
        yg = ypre_ref[...] * (z * _sigmoid(z))
        for g in range(2):
            seg = yg[:, 512 * g:512 * g + 512]
            r = lax.rsqrt(jnp.mean(seg * seg, axis=-1, keepdims=True) + NORM_EPS)
            out = (seg * r) * nw_ref[:, 512 * g:512 * g + 512]
            y_ref[:, 512 * g:512 * g + 512] = out.astype(BF16)
            yt_ref[512 * g:512 * g + 512, :] = out.T.astype(BF16)

    row = pl.BlockSpec((L, 1024), lambda c: (c, 0))
    return pl.pallas_call(
        body, name="ssd_fwd", grid=(nc,), in_specs=_ssd_in_specs(),
        out_specs=[row, pl.BlockSpec((1024, L), lambda c: (0, c)), row,
                   pl.BlockSpec((1, N_PAIRS, 128, 128), lambda c: (c, 0, 0, 0))],
        out_shape=[jax.ShapeDtypeStruct((s, 1024), BF16), jax.ShapeDtypeStruct((1024, s), BF16),
                   jax.ShapeDtypeStruct((s, 1024), F32), jax.ShapeDtypeStruct((nc, N_PAIRS, 128, 128), F32)],
        scratch_shapes=[pltpu.VMEM((N_PAIRS, 128, 128), F32)],
        compiler_params=_params(("arbitrary",)),
    )(proj, proj, proj, proj, proj, proj, proj, conv_w8, conv_w8, conv_w8, conv_b, conv_b, conv_b, proj, smallp, norm_w)


def ssd_bwd(proj, conv_w8, conv_b, smallp, norm_w, ypre, states, dy, sel, swap=()):
    s = proj.shape[0]
    nc = s // SSD_CHUNK
    L = SSD_CHUNK

    ns = len(swap)
    n_in, n_out, n_scratch = 20, 10, 11

    def body(*refs):
        own = refs[:n_in] + refs[n_in + ns:n_in + ns + n_out] + refs[n_in + 2 * ns + n_out:n_in + 2 * ns + n_out + n_scratch]
        if ns:
            start, finish = _pair_swap_phases(refs[n_in:n_in + ns], refs[n_in + ns + n_out:n_in + 2 * ns + n_out],
                                              *refs[n_in + 2 * ns + n_out + n_scratch:])
            pl.when(pl.program_id(0) == 0)(start)
        compute(*own)
        if ns:
            pl.when(pl.program_id(0) == nc - 1)(finish)

    def compute(z_ref, xs_ref, b_ref, c_ref, xsp_ref, bp_ref, cp_ref, wx_ref, wb_ref, wc_ref, bx_ref, bb_ref, bc_ref,
                small_ref, sp_ref, nw_ref, ypre_ref, st_ref, dy_ref, sel_ref,
                dz_ref, dxs_ref, db_ref, dc_ref, dsmall_ref, gwx_ref, gwb_ref, gwc_ref, gsp_ref, gnw_ref,
                dstate, carry_x, carry_b, carry_c, dxs_buf, dbm_buf, dcm_buf, qcs, col_sums, acs_terms, dt_terms):
        step = pl.program_id(0)
        col_sums[...] = jnp.zeros_like(col_sums)
        first_chunk = step == nc - 1
        start = step == 0

        @pl.when(start)
        def _():
            dstate[...] = jnp.zeros_like(dstate)
            carry_x[...] = jnp.zeros_like(carry_x)
            carry_b[...] = jnp.zeros_like(carry_b)
            carry_c[...] = jnp.zeros_like(carry_c)

        xs_sh, xs_pre = _ssd_conv_pre(xs_ref, xsp_ref, wx_ref, bx_ref, first_chunk)
        b_sh, b_pre = _ssd_conv_pre(b_ref, bp_ref, wb_ref, bb_ref, first_chunk)
        c_sh, c_pre = _ssd_conv_pre(c_ref, cp_ref, wc_ref, bc_ref, first_chunk)
        xs, xs_ds = _silu_and_grad(xs_pre)
        bm, b_ds = _silu_and_grad(b_pre)
        cm, c_ds = _silu_and_grad(c_pre)
        dt_pre, dt, a, acs = _ssd_time_consts(small_ref, sp_ref)
        acs_t = acs.T
        li = _lane_iota((L, L))
        ri = _row_iota((L, L))
        tri = ri >= li
        lo = li < HEAD_DIM
        lo_rows = ri < HEAD_DIM
        li1 = _lane_iota((1, L))

        z = z_ref[...]
        sz, dsz = _silu_and_grad(z)
        y = ypre_ref[...]
        yg = y * sz
        dout = dy_ref[...]
        dyg_parts = []
        gnw_parts = []
        for g in range(2):
            sl = slice(512 * g, 512 * g + 512)
            seg = yg[:, sl]
            r = lax.rsqrt(jnp.mean(seg * seg, axis=-1, keepdims=True) + NORM_EPS)
            n = seg * r
            gnw_parts.append(jnp.sum(dout[:, sl] * n, axis=0, keepdims=True))
            gg = dout[:, sl] * nw_ref[:, sl]
            dyg_parts.append(r * (gg - n * jnp.mean(gg * n, axis=-1, keepdims=True)))
        dyg = jnp.concatenate(dyg_parts, axis=1)
        gnw = jnp.concatenate(gnw_parts, axis=1)
        dz_ref[...] = (dyg * y * dsz).astype(BF16)
        dypre = dyg * sz

        qcs[...] = jnp.zeros_like(qcs)
        dalast = jnp.zeros((1, L), F32)
        for g in range(2):
            bg = bm[:, 128 * g:128 * g + 128]
            cg = cm[:, 128 * g:128 * g + 128]
            gmat = _dot(cg, bg, "nt")
            dgmat = jnp.zeros((L, L), F32)
            dbg = jnp.zeros((L, L), F32)
            dcg = jnp.zeros((L, L), F32)
            for pp in range(4):
                p = 4 * g + pp
                h0, h1 = 2 * p, 2 * p + 1
                lanes = slice(128 * p, 128 * p + 128)
                x = xs[:, lanes]
                dyp = dypre[:, lanes]
                a0, a1 = acs[:, h0:h0 + 1], acs[:, h1:h1 + 1]
                dtl = jnp.where(lo, dt[:, h0:h0 + 1], dt[:, h1:h1 + 1])
                xdt = x * dtl
                l0 = jnp.exp(jnp.where(tri, a0 - acs_t[h0:h0 + 1, :], NEG_BIG))
                l1 = jnp.exp(jnp.where(tri, a1 - acs_t[h1:h1 + 1, :], NEG_BIG))
                m0, m1 = gmat * l0, gmat * l1
                dskip = jnp.where(lo[0:1], sp_ref[2:3, h0:h0 + 1], sp_ref[2:3, h1:h1 + 1])
                col_sums[0:1, lanes] = jnp.sum(dyp * x, axis=0, keepdims=True)
                dx = dyp * dskip
                dy0, dy1 = jnp.where(lo, dyp, 0.0), jnp.where(lo, 0.0, dyp)
                x0, x1 = jnp.where(lo, xdt, 0.0), jnp.where(lo, 0.0, xdt)
                dm0, dm1 = _dot(dy0, x0, "nt"), _dot(dy1, x1, "nt")
                dxdt = _dot(m0, dy0, "tn") + _dot(m1, dy1, "tn")
                q0, q1 = dm0 * m0, dm1 * m1
                qcs[h0:h0 + 1, :] = jnp.sum(q0, axis=0, keepdims=True)
                qcs[h1:h1 + 1, :] = jnp.sum(q1, axis=0, keepdims=True)
                row_terms = jnp.where(lo, q0 + pltpu.roll(q0, HEAD_DIM, 1), q1 + pltpu.roll(q1, HEAD_DIM, 1))
                dgmat = dgmat + dm0 * l0 + dm1 * l1
                hin = st_ref[0, p]
                e = jnp.exp(jnp.where(lo, a0, a1))
                ch = _dot(cg, hin, "nt")
                dch = dyp * e
                dcg = dcg + _dot(dch, hin)
                dhin = _dot(dch, cg, "tn")
                dhout = dstate[p]
                al0, al1 = acs[L - 1:L, h0:h0 + 1], acs[L - 1:L, h1:h1 + 1]
                dec = jnp.exp(jnp.where(lo_rows, al0, al1))
                dhin = dhin + dec * dhout
                dal = dhout * hin * dec
                dal0 = _total(jnp.where(lo_rows, dal, 0.0))
                dal1 = _total(dal) - dal0
                dalast = dalast + jnp.where(li1 == h0, dal0, 0.0) + jnp.where(li1 == h1, dal1, 0.0)
                w = jnp.exp(jnp.where(lo, al0 - a0, al1 - a1))
                xw = xdt * w
                dxw = _dot(bg, dhout, "nt")
                dbg = dbg + _dot(xw, dhout)
                dxdt = dxdt + dxw * w
                dww = dxw * xw
                col_sums[1:2, lanes] = jnp.sum(dww, axis=0, keepdims=True)
                acs_terms[:, lanes] = row_terms + dch * ch - dww
                dx = dx + dxdt * dtl
                dt_terms[:, lanes] = dxdt * x
                dxs_buf[:, lanes] = dx
                dstate[p] = dhin
            dcg = dcg + _dot(dgmat, bg)
            dbg = dbg + _dot(dgmat, cg, "tn")
            dbm_buf[:, 128 * g:128 * g + 128] = dbg
            dcm_buf[:, 128 * g:128 * g + 128] = dcg

        head_sums = _split3_dot(col_sums[...], sel_ref[...])
        dskip_g = head_sums[0:1, :]
        dalast = dalast + head_sums[1:2, :]
        ddt = _split3_dot(dt_terms[...], sel_ref[...])
        dacs_tot = _split3_dot(acs_terms[...], sel_ref[...]) - qcs[...].T + jnp.where(ri == L - 1, dalast, 0.0)
        dstep = _rev_cumsum_rows(dacs_tot)
        ddt = ddt + dstep * a
        head_lane = li < N_HEADS
        ddt_pre = jnp.where(head_lane, ddt * _sigmoid(dt_pre), 0.0)
        dsmall_ref[...] = ddt_pre
        da = jnp.sum(jnp.where(head_lane, dstep * dt, 0.0), axis=0, keepdims=True)
        gsp = _stack_rows([jnp.sum(ddt_pre, axis=0, keepdims=True), da * a, dskip_g], L)

        def conv_back(dpost, ds, shifts, w_ref, carry, out_ref, width):
            dpre = dpost * ds
            dext = jnp.concatenate([dpre, carry[...]], axis=0)
            out_ref[...] = _conv_rows_transposed(dext, w_ref[...], SSD_CONV)[:L].astype(BF16)
            carry[...] = dpre[0:8]
            return _conv_weight_grad(dpre, shifts, slice(8, 8 + L), width)

        gwx = conv_back(dxs_buf[...], xs_ds, xs_sh, wx_ref, carry_x, dxs_ref, 1024)
        gwb = conv_back(dbm_buf[...], b_ds, b_sh, wb_ref, carry_b, db_ref, 256)
        gwc = conv_back(dcm_buf[...], c_ds, c_sh, wc_ref, carry_c, dc_ref, 256)

        @pl.when(start)
        def _():
            gwx_ref[...] = gwx
            gwb_ref[...] = gwb
            gwc_ref[...] = gwc
            gsp_ref[...] = gsp
            gnw_ref[...] = gnw

        @pl.when(step > 0)
        def _():
            gwx_ref[...] += gwx
            gwb_ref[...] += gwb
            gwc_ref[...] += gwc
            gsp_ref[...] += gsp
            gnw_ref[...] += gnw

    def ch(c):
        return nc - 1 - c

    row = pl.BlockSpec((L, 1024), lambda c: (ch(c), 0))
    row256 = pl.BlockSpec((L, 256), lambda c: (ch(c), 0))
    in_specs = _ssd_in_specs(rev_nc=nc) + [row, pl.BlockSpec((1, N_PAIRS, 128, 128), lambda c: (ch(c), 0, 0, 0)), row,
                                           pl.BlockSpec((1024, 128), lambda c: (0, 0))]
    out_specs = [row, row, row256, row256, pl.BlockSpec((L, 128), lambda c: (ch(c), 0)),
                 pl.BlockSpec((8, 1024), lambda c: (0, 0)), pl.BlockSpec((8, 256), lambda c: (0, 0)),
                 pl.BlockSpec((8, 256), lambda c: (0, 0)), pl.BlockSpec((8, 128), lambda c: (0, 0)),
                 pl.BlockSpec((1, 1024), lambda c: (0, 0))]
    out_shape = [jax.ShapeDtypeStruct((s, 1024), BF16), jax.ShapeDtypeStruct((s, 1024), BF16),
                 jax.ShapeDtypeStruct((s, 256), BF16), jax.ShapeDtypeStruct((s, 256), BF16),
                 jax.ShapeDtypeStruct((s, 128), F32),
                 jax.ShapeDtypeStruct((8, 1024), F32), jax.ShapeDtypeStruct((8, 256), F32),
                 jax.ShapeDtypeStruct((8, 256), F32), jax.ShapeDtypeStruct((8, 128), F32),
                 jax.ShapeDtypeStruct((1, 1024), F32)]
    scratch = [pltpu.VMEM((N_PAIRS, 128, 128), F32), pltpu.VMEM((8, 1024), F32), pltpu.VMEM((8, 256), F32),
               pltpu.VMEM((8, 256), F32), pltpu.VMEM((L, 1024), F32), pltpu.VMEM((L, 256), F32), pltpu.VMEM((L, 256), F32),
               pltpu.VMEM((L, L), F32), pltpu.VMEM((8, 1024), F32), pltpu.VMEM((L, 1024), F32), pltpu.VMEM((L, 1024), F32)]
    assert (len(in_specs), len(out_specs), len(scratch)) == (n_in, n_out, n_scratch)
    outs = pl.pallas_call(
        body, name="ssd_bwd", grid=(nc,), in_specs=in_specs + [ANY] * ns, out_specs=out_specs + [ANY] * ns,
        out_shape=out_shape + _pair_swap_out_shapes(swap), scratch_shapes=scratch + (_pair_swap_scratch(ns) if ns else []),
        compiler_params=_params(("arbitrary",)),
    )(proj, proj, proj, proj, proj, proj, proj, conv_w8, conv_w8, conv_w8, conv_b, conv_b, conv_b, proj, smallp, norm_w,
      ypre, states, dy, sel, *swap)
    return (*outs[:n_out], list(outs[n_out:]))


FOX_SCALE = HEAD_DIM ** -0.5
FOX_T = 256
Q_COL, K_COL, V_COL = 2, 3, 4


def _split_dot(v, m, terms):
    out, rest = None, v
    for i in range(terms):
        piece = rest.astype(BF16)
        out = _dot(piece, m) if out is None else out + _dot(piece, m)
        if i + 1 < terms:
            rest = rest - piece.astype(F32)
    return out


def _split3_dot(v, m):
    return _split_dot(v, m, 3)


def _head_mean(x, sel_ref, selt_ref):
    return _dot(x, sel_ref[...]) * (1.0 / HEAD_DIM)


def _head_spread(v, selt_ref):
    return _split_dot(v, selt_ref[...], 2)


def _head_rstd(x, sel_ref, selt_ref):
    return _head_spread(lax.rsqrt(_head_mean(x * x, sel_ref, selt_ref) + NORM_EPS), selt_ref)


def fox_tables():
    r = np.arange(3 * 128)
    piece, lane = r // 128, r % 128
    head = lane - F_LANE
    is_head = np.logical_and(head >= 0, head < N_HEADS)
    col = 128 * (head // 2) + HEAD_DIM * (1 - head % 2) + piece
    cols = np.arange(1024)
    place_q = np.logical_and(is_head[:, None], cols[None, :] == col[:, None])
    place_k = np.logical_and(is_head[:, None], cols[None, :] == (col + 3)[:, None])
    ones_q = np.logical_and(cols % HEAD_DIM >= 3, cols % HEAD_DIM < 6)[None]
    ones_k = (cols % HEAD_DIM < 3)[None]
    h = np.arange(128) - F_LANE
    ok = np.logical_and(h >= 0, h < N_HEADS)
    same_pair = cols[:, None] // 128 == (h // 2)[None, :]
    fold_even = np.logical_and(np.logical_and(ok, h % 2 == 0)[None, :], same_pair)
    fold_odd = np.logical_and(np.logical_and(ok, h % 2 == 1)[None, :], same_pair)
    as_bf16 = lambda t: jnp.asarray(t.astype(np.float32), BF16)
    return (as_bf16(place_q), as_bf16(place_k), jnp.asarray(ones_q, F32), jnp.asarray(ones_k, F32),
            as_bf16(fold_even), as_bf16(fold_odd))


def fox_prep(proj, smallp, qw, kw, sel, selt, place_q, place_k, ones_q, ones_k, *, tm=256):
    s = proj.shape[0]

    def body(q_ref, k_ref, v_ref, small_ref, sp_ref, qw_ref, kw_ref, sel_ref, selt_ref, pq_ref, pk_ref, oq_ref, ok_ref,
             qn_ref, kn_ref, aq_ref, ak_ref, vb_ref, knt_ref, akt_ref, vt_ref, carry):
        @pl.when(pl.program_id(0) == 0)
        def _():
            carry[...] = jnp.zeros_like(carry)

        q = q_ref[...]
        qn_ref[...] = (((q * _head_rstd(q, sel_ref, selt_ref)) * qw_ref[...]) * FOX_SCALE).astype(BF16)
        k = k_ref[...]
        kn = ((k * _head_rstd(k, sel_ref, selt_ref)) * kw_ref[...]).astype(BF16)
        kn_ref[...] = kn
        knt_ref[...] = kn.astype(F32).T.astype(BF16)
        vb_ref[...] = v_ref[...].astype(BF16)
        vt_ref[...] = v_ref[...].T.astype(BF16)
        li = _lane_iota((tm, 128))
        f_lane = jnp.logical_and(li >= F_LANE, li < F_LANE + N_HEADS)
        logf = jnp.where(f_lane, -_softplus(-(small_ref[...] + sp_ref[3:4, :])), 0.0)
        cum = _cumsum_rows(logf) + carry[...]
        carry[...] = cum[tm - 1:tm, :]
        hi = cum.astype(BF16)
        r1 = cum - hi.astype(F32)
        mid = r1.astype(BF16)
        lo = (r1 - mid.astype(F32)).astype(BF16)
        pieces = jnp.concatenate([hi, mid, lo], axis=1)
        aq_ref[...] = (_dot(pieces, pq_ref[...]) + oq_ref[...]).astype(BF16)
        ak = ok_ref[...] - _dot(pieces, pk_ref[...])
        ak_ref[...] = ak.astype(BF16)
        akt_ref[...] = ak.T.astype(BF16)

    row = pl.BlockSpec((tm, 1024), lambda i: (i, 0))
    col = pl.BlockSpec((1024, tm), lambda i: (0, i))
    vec = pl.BlockSpec((1, 1024), lambda i: (0, 0))
    table = pl.BlockSpec((384, 1024), lambda i: (0, 0))
    wide = jax.ShapeDtypeStruct((s, 1024), BF16)
    tall = jax.ShapeDtypeStruct((1024, s), BF16)
    return pl.pallas_call(
        body, name="fox_prep", grid=(s // tm,),
        in_specs=[pl.BlockSpec((tm, 1024), lambda i: (i, Q_COL)), pl.BlockSpec((tm, 1024), lambda i: (i, K_COL)),
                  pl.BlockSpec((tm, 1024), lambda i: (i, V_COL)),
                  pl.BlockSpec((tm, 128), lambda i: (i, SMALL_BLOCK)), pl.BlockSpec((8, 128), lambda i: (0, 0)), vec, vec,
                  pl.BlockSpec((1024, 128), lambda i: (0, 0)), pl.BlockSpec((128, 1024), lambda i: (0, 0)),
                  table, table, vec, vec],
        out_specs=[row, row, row, row, row, col, col, col],
        out_shape=[wide, wide, wide, wide, wide, tall, tall, tall],
        scratch_shapes=[pltpu.VMEM((1, 128), F32)], compiler_params=_params(("arbitrary",)),
    )(proj, proj, proj, proj, smallp, qw, kw, sel, selt, place_q, place_k, ones_q, ones_k)


def fox_fwd(qn, kn, aq, ak, vt, shards=()):
    s = qn.shape[0]
    t = FOX_T
    nq = s // t
    ng = len(shards)

    def body(*refs):
        q_ref, k_ref, aq_ref, ak_ref, vt_ref = refs[:5]
        o_ref, ot_ref, lse_ref = refs[5 + ng:8 + ng]
        p = pl.program_id(0)
        if ng:
            start, forward, finish = _gather_phases(refs[5:5 + ng], refs[8 + ng:8 + 2 * ng], *refs[8 + 2 * ng:])
            pl.when(p == 0)(start)
            pl.when(p == N_PAIRS // 2)(forward)

        @pl.when(p == 0)
        def _():
            lse_ref[...] = jnp.zeros_like(lse_ref)

        lo = _lane_iota((t, 128)) < HEAD_DIM
        lo_rows = _row_iota((128, t)) < HEAD_DIM
        causal_t = _lane_iota((t, t)) >= _row_iota((t, t))

        def q_loop(qi, _):
            q0 = pl.multiple_of(qi * t, t)
            qv, aqv = q_ref[pl.ds(q0, t), :], aq_ref[pl.ds(q0, t), :]
            qa, qb = jnp.where(lo, qv, aqv), jnp.where(lo, aqv, qv)

            def scores(kj):
                k0 = pl.multiple_of(kj * t, t)
                kv, akv = k_ref[pl.ds(k0, t), :], ak_ref[pl.ds(k0, t), :]
                return _dot(jnp.where(lo, kv, akv), qa, "nt"), _dot(jnp.where(lo, akv, kv), qb, "nt")

            def update(kj, stats, s0, s1):
                m0, l0, m1, l1, acc = stats
                vtv = vt_ref[:, pl.ds(pl.multiple_of(kj * t, t), t)]
                n0 = jnp.maximum(m0, jnp.max(s0, axis=0, keepdims=True))
                n1 = jnp.maximum(m1, jnp.max(s1, axis=0, keepdims=True))
                a0, a1 = jnp.exp(m0 - n0), jnp.exp(m1 - n1)
                p0, p1 = jnp.exp(s0 - n0), jnp.exp(s1 - n1)
                l0 = a0 * l0 + jnp.sum(p0, axis=0, keepdims=True)
                l1 = a1 * l1 + jnp.sum(p1, axis=0, keepdims=True)
                acc = (jnp.where(lo_rows, a0, a1) * acc + _dot(jnp.where(lo_rows, vtv, 0.0), p0)
                       + _dot(jnp.where(lo_rows, 0.0, vtv), p1))
                return n0, l0, n1, l1, acc

            def step(kj, carry):
                stats, (s0, s1) = carry[:5], carry[5:]
                nxt = scores(kj + 1)
                return (*update(kj, stats, s0, s1), *nxt)

            def row(val):
                return jnp.full((1, t), val, F32)

            init = (row(NEG_BIG), row(0.0), row(NEG_BIG), row(0.0), jnp.zeros((128, t), F32), *scores(0))
            carry = lax.fori_loop(0, qi, step, init)
            s0, s1 = jnp.where(causal_t, carry[5], NEG_BIG), jnp.where(causal_t, carry[6], NEG_BIG)
            m0, l0, m1, l1, acc = update(qi, carry[:5], s0, s1)
            out_t = acc / jnp.where(lo_rows, l0, l1)
            ot_ref[:, pl.ds(q0, t)] = out_t.astype(BF16)
            o_ref[pl.ds(q0, t), :] = out_t.T.astype(BF16)
            ri = _row_iota((N_HEADS, t))
            old = lse_ref[:, pl.ds(q0, t)]
            lse_ref[:, pl.ds(q0, t)] = jnp.where(
                ri == 2 * p, m0 + jnp.log(l0), jnp.where(ri == 2 * p + 1, m1 + jnp.log(l1), old))
            return 0

        lax.fori_loop(0, nq, q_loop, 0)
        if ng:
            pl.when(p == N_PAIRS - 1)(finish)

    pair = pl.BlockSpec((s, 128), lambda p: (0, p))
    outs = pl.pallas_call(
        body, name="fox_fwd", grid=(N_PAIRS,),
        in_specs=[pair] * 4 + [pl.BlockSpec((128, s), lambda p: (p, 0))] + [ANY] * ng,
        out_specs=[pair, pl.BlockSpec((128, s), lambda p: (p, 0)), pl.BlockSpec((N_HEADS, s), lambda p: (0, 0))] + [ANY] * ng,
        out_shape=[jax.ShapeDtypeStruct((s, 1024), BF16), jax.ShapeDtypeStruct((1024, s), BF16),
                   jax.ShapeDtypeStruct((N_HEADS, s), F32)] + _gather_out_shapes(shards),
        scratch_shapes=_gather_scratch(ng) if ng else [],
        compiler_params=_params(("arbitrary",)),
    )(qn, kn, aq, ak, vt, *shards)
    return outs[0], outs[1], outs[2], list(outs[3:])


def fox_bwd(qn, kn, aq, ak, knt, akt, vb, lse, dmixed, parts=()):
    s = qn.shape[0]
    t = FOX_T
    nq = s // t
    once = pl.Buffered(1)
    ns = len(parts)

    def body(*refs):
        q_ref, k_ref, aq_ref, ak_ref, kt_ref, akt_ref, v_ref, lse_ref, do_ref = refs[:9]
        dq_ref, dk_ref, dv_ref, dc0_ref, dc1_ref = refs[9 + ns:14 + ns]
        p_scr, dp_scr = refs[14 + 2 * ns:16 + 2 * ns]
        p = pl.program_id(0)
        if ns:
            start, finish = _scatter_phases(refs[9:9 + ns], refs[14 + ns:14 + 2 * ns], *refs[16 + 2 * ns:])
            pl.when(p == 0)(start)
        dk_ref[...] = jnp.zeros_like(dk_ref)
        dv_ref[...] = jnp.zeros_like(dv_ref)
        dc0_ref[...] = jnp.zeros_like(dc0_ref)
        dc1_ref[...] = jnp.zeros_like(dc1_ref)
        lo = _lane_iota((t, 128)) < HEAD_DIM
        lo_rows = _row_iota((128, t)) < HEAD_DIM
        causal_t = _lane_iota((t, t)) >= _row_iota((t, t))

        def q_loop(qi, _):
            q0 = pl.multiple_of(qi * t, t)
            qv, aqv = q_ref[pl.ds(q0, t), :], aq_ref[pl.ds(q0, t), :]
            qa, qb = jnp.where(lo, qv, aqv), jnp.where(lo, aqv, qv)
            do = do_ref[pl.ds(q0, t), :]
            doa, dob = jnp.where(lo, do, 0.0).astype(BF16), jnp.where(lo, 0.0, do).astype(BF16)
            lse_blk = lse_ref[:, pl.ds(q0, t)]
            ri = _row_iota((N_HEADS, t))
            lse0 = jnp.sum(jnp.where(ri == 2 * p, lse_blk, 0.0), axis=0, keepdims=True)
            lse1 = jnp.sum(jnp.where(ri == 2 * p + 1, lse_blk, 0.0), axis=0, keepdims=True)

            def scores(kj):
                k0 = pl.multiple_of(kj * t, t)
                kv, akv = k_ref[pl.ds(k0, t), :], ak_ref[pl.ds(k0, t), :]
                return _dot(jnp.where(lo, kv, akv), qa, "nt"), _dot(jnp.where(lo, akv, kv), qb, "nt")

            def pass1(kj, d0, d1, diagonal):
                k0 = pl.multiple_of(kj * t, t)
                vv = v_ref[pl.ds(k0, t), :]
                s0, s1 = scores(kj)
                if diagonal:
                    s0, s1 = jnp.where(causal_t, s0, NEG_BIG), jnp.where(causal_t, s1, NEG_BIG)
                p0, p1 = jnp.exp(s0 - lse0), jnp.exp(s1 - lse1)
                dp0, dp1 = _dot(vv, doa, "nt"), _dot(vv, dob, "nt")
                p_scr[0, kj], p_scr[1, kj] = p0, p1
                dp_scr[0, kj], dp_scr[1, kj] = dp0, dp1
                dv_ref[pl.ds(k0, t), :] += _dot(p0, doa) + _dot(p1, dob)
                return d0 + jnp.sum(p0 * dp0, axis=0, keepdims=True), d1 + jnp.sum(p1 * dp1, axis=0, keepdims=True)

            zero = jnp.zeros((1, t), F32)
            d0, d1 = lax.fori_loop(0, qi, lambda kj, c: pass1(kj, *c, False), (zero, zero))
            d0, d1 = pass1(qi, d0, d1, True)

            def pass2(kj, carry):
                dq0, dq1 = carry
                k0 = pl.multiple_of(kj * t, t)
                p0, p1 = p_scr[0, kj], p_scr[1, kj]
                ds0, ds1 = p0 * (dp_scr[0, kj] - d0), p1 * (dp_scr[1, kj] - d1)
                dk_ref[pl.ds(k0, t), :] += jnp.where(lo, _dot(ds0, qa), _dot(ds1, qb))
                dc0_ref[pl.ds(k0, t), :] += ds0[:, :128] + ds0[:, 128:]
                dc1_ref[pl.ds(k0, t), :] += ds1[:, :128] + ds1[:, 128:]
                ktv, aktv = kt_ref[:, pl.ds(k0, t)], akt_ref[:, pl.ds(k0, t)]
                return dq0 + _dot(jnp.where(lo_rows, ktv, aktv), ds0), dq1 + _dot(jnp.where(lo_rows, aktv, ktv), ds1)

            zq = jnp.zeros((128, t), F32)
            dq0, dq1 = lax.fori_loop(0, qi + 1, pass2, (zq, zq))
            dq_ref[pl.ds(q0, t), :] = jnp.where(lo_rows, dq0, dq1).T
            return 0

        lax.fori_loop(0, nq, q_loop, 0)
        if ns:
            pl.when(p == N_PAIRS - 1)(finish)

    pair = pl.BlockSpec((s, 128), lambda p: (0, p))
    pair_t = pl.BlockSpec((128, s), lambda p: (p, 0))
    out = jax.ShapeDtypeStruct((s, 1024), F32)
    outs = pl.pallas_call(
        body, name="fox_bwd", grid=(N_PAIRS,),
        in_specs=[pair, pair, pair, pair, pair_t, pair_t, pair, pl.BlockSpec((N_HEADS, s), lambda p: (0, 0)),
                  pl.BlockSpec((s, 128), lambda p: (0, 8 + p))] + [ANY] * ns,
        out_specs=[pl.BlockSpec((s, 128), lambda p: (0, p), pipeline_mode=once)] * 5 + [ANY] * ns,
        out_shape=[out] * 5 + [jax.ShapeDtypeStruct(p.shape, p.dtype) for p in parts],
        scratch_shapes=[pltpu.VMEM((2, nq, t, t), F32), pltpu.VMEM((2, nq, t, t), F32)] + (_scatter_scratch(ns) if ns else []),
        compiler_params=_params(("arbitrary",)),
    )(qn, kn, aq, ak, knt, akt, vb, lse, dmixed, *parts)
    return (*outs[:5], _keep_own_blocks(outs[5:], parts))


def fox_post(dqn, dkn, dc0, dc1, proj, smallp, qw, kw, sel, selt, fold_even, fold_odd, *, tm=256):
    s = proj.shape[0]
    nrow = s // tm

    def body(dqn_ref, dkn_ref, dc0_ref, dc1_ref, q_ref, k_ref, small_ref, sp_ref, qw_ref, kw_ref, sel_ref, selt_ref,
             fe_ref, fo_ref, dq_ref, dk_ref, dsmall_ref, gqw_ref, gkw_ref, gfb_ref, carry):
        step = pl.program_id(0)

        @pl.when(step == 0)
        def _():
            carry[...] = jnp.zeros_like(carry)

        def norm_bwd(x_ref, w_ref, dn, out_ref):
            x = x_ref[...]
            rf = _head_rstd(x, sel_ref, selt_ref)
            xh = x * rf
            g = dn * w_ref[...]
            mean_gx = _head_spread(_head_mean(g * xh, sel_ref, selt_ref), selt_ref)
            out_ref[...] = (rf * (g - xh * mean_gx)).astype(BF16)
            return jnp.sum(dn * xh, axis=0, keepdims=True)

        gqw = norm_bwd(q_ref, qw_ref, dqn_ref[...] * FOX_SCALE, dq_ref)
        gkw = norm_bwd(k_ref, kw_ref, dkn_ref[...], dk_ref)
        li = _lane_iota((tm, 128))
        f_lane = jnp.logical_and(li >= F_LANE, li < F_LANE + N_HEADS)
        dcum = -(_split3_dot(dc0_ref[...], fe_ref[...]) + _split3_dot(dc1_ref[...], fo_ref[...]))
        dlogf = _rev_cumsum_rows(dcum) + carry[...]
        carry[...] = dlogf[0:1, :]
        dfr = jnp.where(f_lane, dlogf * _sigmoid(-(small_ref[...] + sp_ref[3:4, :])), 0.0)
        dsmall_ref[...] = dfr
        gfb = jnp.sum(dfr, axis=0, keepdims=True)

        @pl.when(step == 0)
        def _():
            gqw_ref[...] = gqw
            gkw_ref[...] = gkw
            gfb_ref[...] = gfb

        @pl.when(step > 0)
        def _():
            gqw_ref[...] += gqw
            gkw_ref[...] += gkw
            gfb_ref[...] += gfb

    def rb(i):
        return nrow - 1 - i

    row = pl.BlockSpec((tm, 1024), lambda i: (rb(i), 0))
    vec = pl.BlockSpec((1, 1024), lambda i: (0, 0))
    fold = pl.BlockSpec((1024, 128), lambda i: (0, 0))
    return pl.pallas_call(
        body, name="fox_post", grid=(nrow,),
        in_specs=[row, row, row, row, pl.BlockSpec((tm, 1024), lambda i: (rb(i), Q_COL)),
                  pl.BlockSpec((tm, 1024), lambda i: (rb(i), K_COL)),
                  pl.BlockSpec((tm, 128), lambda i: (rb(i), SMALL_BLOCK)), pl.BlockSpec((8, 128), lambda i: (0, 0)), vec, vec,
                  fold, pl.BlockSpec((128, 1024), lambda i: (0, 0)), fold, fold],
        out_specs=[row, row, pl.BlockSpec((tm, 128), lambda i: (rb(i), 0)), vec, vec, pl.BlockSpec((1, 128), lambda i: (0, 0))],
        out_shape=[jax.ShapeDtypeStruct((s, 1024), BF16), jax.ShapeDtypeStruct((s, 1024), BF16),
                   jax.ShapeDtypeStruct((s, 128), F32), jax.ShapeDtypeStruct((1, 1024), F32),
                   jax.ShapeDtypeStruct((1, 1024), F32), jax.ShapeDtypeStruct((1, 128), F32)],
        scratch_shapes=[pltpu.VMEM((1, 128), F32)], compiler_params=_params(("arbitrary",)),
    )(dqn, dkn, dc0, dc1, proj, proj, proj, smallp, qw, kw, sel, selt, fold_even, fold_odd)


def local_step(x, target, wx, later_shards, ssd_cw8, ssd_cb, smallp, ssd_nw, qw_t, kw_t, sel, selt,
               norm_mix_w, norm_ffn_w, ffn_cw8, ffn_cb):
    h, h_t = rms_fwd(x, norm_mix_w, name="rms_mix_fwd")
    proj = matmul(h, wx, mode="nn", tm=1024, tn=PROJ_TILE, tk=1024, out_dtype=F32, name="mm_in_proj")
    y_ssd, y_ssd_t, ypre, states = ssd_fwd(proj, ssd_cw8, ssd_cb, smallp, ssd_nw)
    place_q, place_k, ones_q, ones_k, fold_even, fold_odd = fox_tables()
    qn, kn, aq, ak, vb, knt, akt, vt = fox_prep(proj, smallp, qw_t, kw_t, sel, selt, place_q, place_k, ones_q, ones_k)
    y_fox, y_fox_t, lse, (a_out, a_up, a_down) = fox_fwd(qn, kn, aq, ak, vt, shards=later_shards)
    w_out = a_out.reshape(2048, D_MODEL)
    w_down = a_down.reshape(D_FF, D_MODEL)
    s = x.shape[0]
    shard = lambda index: pl.BlockSpec((None, 1024, 1408), index)
    x1 = matmul(y_ssd, w_out, mode="nn", tm=1024, tn=1024, tk=1024, out_dtype=F32, name="mm_out_ssd", add=x)
    x1 = matmul(y_fox, w_out, mode="nn", tm=1024, tn=1024, tk=1024, out_dtype=F32, name="mm_out_fox", add=x1, b_koff=1)
    hf, hf_t = rms_fwd(x1, norm_ffn_w, name="rms_ffn_fwd")
    hu = matmul(hf, a_up, mode="nn", tm=1024, tn=1408, tk=1024, out_dtype=F32, name="mm_up",
                layout=dict(m=s, n=2 * D_FF, k=D_MODEL, b_spec=shard(lambda i, j, kk: (j, kk, 0))))
    act, act_t = ffn_mid_fwd(hu, ffn_cw8, ffn_cb)
    y = matmul(act, w_down, mode="nn", tm=1024, tn=1024, tk=1408, out_dtype=F32, name="mm_down", add=x1)
    dy, sq = loss_head(y, target)

    dact = matmul(dy, w_down, mode="nt", tm=1024, tn=1408, tk=1024, out_dtype=F32, name="mm_dact")
    g_down = matmul(act_t, dy, mode="nn", tm=1408, tn=1024, tk=1024, out_dtype=BF16, name="mm_dw_down")
    dhu, gcw_g, gcw_v = ffn_mid_bwd(hu, dact, ffn_cw8, ffn_cb)
    dhf = matmul(dhu, a_up, mode="nt", tm=1024, tn=1024, tk=1408, out_dtype=F32, name="mm_dhf",
                 layout=dict(m=s, n=D_MODEL, k=2 * D_FF, a_spec=shard(lambda i, j, kk: (kk // 2, i, kk % 2)),
                             b_spec=shard(lambda i, j, kk: (kk, 0, 0))))
    g_up = matmul(hf_t, dhu, mode="nn", tm=1024, tn=1408, tk=1024, out_dtype=BF16, name="mm_dw_up",
                  layout=dict(m=D_MODEL, n=2 * D_FF, k=s, b_spec=shard(lambda i, j, kk: (j // 2, kk, j % 2)),
                              o_spec=shard(lambda i, j, kk: (j, i, 0)), out_shape=(4, D_MODEL, 1408)))
    dx1, g_norm_ffn = rms_bwd(dhf, x1, norm_ffn_w, dy, name="rms_ffn_bwd")
    dmixed = matmul(dx1, w_out, mode="nt", tm=1024, tn=1024, tk=1024, out_dtype=F32, name="mm_dmixed")
    g_out_a = matmul(y_ssd_t, dx1, mode="nn", tm=1024, tn=1024, tk=1024, out_dtype=BF16, name="mm_dw_out_ssd")
    g_out_b = matmul(y_fox_t, dx1, mode="nn", tm=1024, tn=1024, tk=1024, out_dtype=BF16, name="mm_dw_out_fox")
    early = [jnp.concatenate([g_out_a, g_out_b], axis=0).reshape(4, 512, D_MODEL), g_up, g_down.reshape(4, 704, D_MODEL)]
    dz, dxs, db, dc, dsmall_ssd, gcw_x, gcw_b, gcw_c, g_sp, g_ssd_nw, theirs = ssd_bwd(
        proj, ssd_cw8, ssd_cb, smallp, ssd_nw, ypre, states, dmixed, sel, swap=early)
    core = lax.axis_index("c").astype(jnp.int32).reshape(1)
    parts = [add_pair(a, b, core, name="add_pair_" + n, tr=ADAM_ROWS[n]) for a, b, n in zip(early, theirs, BIG_NAMES[1:])]
    dqn, dkn, dv, dc0, dc1, landed_early = fox_bwd(qn, kn, aq, ak, knt, akt, vb, lse, dmixed, parts=parts)
    dq, dk, dsmall_fox, g_qw, g_kw, g_fb = fox_post(dqn, dkn, dc0, dc1, proj, smallp, qw_t, kw_t, sel, selt,
                                                    fold_even, fold_odd)
    dproj = jnp.concatenate([dz, dxs, dq, dk, dv.astype(BF16), db, dc, (dsmall_ssd + dsmall_fox).astype(BF16)], axis=1)
    g_wx = matmul(h_t, dproj, mode="nn", tm=1024, tn=PROJ_TILE, tk=1024, out_dtype=BF16, name="mm_dw_in")
    g_in = _in_grad_shards(g_wx)
    part_in = add_pair(g_in, pair_swap_halves([g_in], name="pair_swap_w_in")[0], core, name="add_pair_w_in",
                       tr=ADAM_ROWS["w_in"])
    dh, landed_in = matmul(dproj, wx, mode="nt", tm=1024, tn=1024, tk=PROJ_TILE, out_dtype=F32, name="mm_dh",
                           scatter=[part_in])
    grad_x, g_norm_mix = rms_bwd(dh, x, norm_mix_w, dx1, name="rms_mix_bwd")
    return dict(
        sq=sq, grad_x=grad_x, landed=landed_in + landed_early,
        g_norm_mix=g_norm_mix, g_norm_ffn=g_norm_ffn, g_ssd_nw=g_ssd_nw,
        g_ssd_cw=jnp.concatenate([gcw_x, gcw_b, gcw_c], axis=1), g_sp=g_sp, g_fb=g_fb, g_qw=g_qw, g_kw=g_kw,
        g_ffn_cw=jnp.concatenate([gcw_g, gcw_v], axis=1))


def adamw(w, g, m, v, *, name, tr, allreduce=None):
    rows, cols = w.shape
    nsteps = rows // tr

    def body(*refs):
        if allreduce is None:
            w_ref, g_ref, m_ref, v_ref, d_ref, mo_ref, vo_ref = refs
        else:
            w_ref, g_ref, m_ref, v_ref, packed_ref, d_ref, mo_ref, vo_ref, summed_ref = refs[:9]
            start, finish = _allreduce_phases(packed_ref, summed_ref, *refs[9:])
            pl.when(pl.program_id(0) == 0)(start)
        gv = g_ref[...]
        mn = ADAM_B1 * m_ref[...] + (1.0 - ADAM_B1) * gv
        vn = ADAM_B2 * v_ref[...] + (1.0 - ADAM_B2) * (gv * gv)
        m_hat = mn / (1.0 - ADAM_B1 ** ADAM_STEP)
        v_hat = vn / (1.0 - ADAM_B2 ** ADAM_STEP)
        d_ref[...] = -ADAM_LR * (m_hat / (jnp.sqrt(v_hat) + ADAM_EPS) + ADAM_WD * w_ref[...])
        mo_ref[...] = mn
        vo_ref[...] = vn
        if allreduce is not None:
            pl.when(pl.program_id(0) == nsteps - 1)(finish)

    blk = pl.BlockSpec((tr, cols), lambda i: (i, 0))
    shp = jax.ShapeDtypeStruct((rows, cols), F32)
    if allreduce is None:
        return pl.pallas_call(
            body, name=name, grid=(nsteps,), in_specs=[blk] * 4, out_specs=[blk] * 3, out_shape=[shp] * 3,
            compiler_params=_params(("parallel",)),
        )(w, g, m, v)
    whole = pl.BlockSpec(memory_space=pltpu.VMEM)
    return pl.pallas_call(
        body, name=name, grid=(nsteps,), in_specs=[blk] * 4 + [whole], out_specs=[blk] * 3 + [whole],
        out_shape=[shp] * 3 + [jax.ShapeDtypeStruct(allreduce.shape, F32)],
        scratch_shapes=_allreduce_scratch(allreduce.shape[0]), compiler_params=_params(("arbitrary",)),
    )(w, g, m, v, allreduce)


def add_pair(full, theirs, core, *, name, tr):
    _, rows, cols = theirs.shape
    nblk = rows // tr

    def body(c_ref, a_ref, b_ref, o_ref):
        o_ref[...] = (a_ref[...].astype(F32) + b_ref[...].astype(F32)).astype(BF16)

    blk = pl.BlockSpec((1, tr, cols), lambda j, i, c: (j, i, 0))
    grid_spec = pltpu.PrefetchScalarGridSpec(
        num_scalar_prefetch=1, grid=(4, nblk),
        in_specs=[pl.BlockSpec((1, tr, cols), lambda j, i, c: (j, c[0] * nblk + i, 0)), blk], out_specs=blk)
    return pl.pallas_call(
        body, name=name, grid_spec=grid_spec, out_shape=jax.ShapeDtypeStruct(theirs.shape, BF16),
        compiler_params=_params(("parallel", "parallel")),
    )(core, full, theirs)


def sum_chips(parts, core, *, name, tr):
    _, rows, cols = parts.shape
    nblk = rows // tr

    def body(c_ref, p_ref, o_ref):
        acc = p_ref[0].astype(F32)
        for k in range(1, 4):
            acc = acc + p_ref[k].astype(F32)
        o_ref[...] = acc

    grid_spec = pltpu.PrefetchScalarGridSpec(
        num_scalar_prefetch=1, grid=(nblk,), in_specs=[pl.BlockSpec((4, tr, cols), lambda i, c: (0, i, 0))],
        out_specs=pl.BlockSpec((tr, cols), lambda i, c: (c[0] * nblk + i, 0)))
    return pl.pallas_call(
        body, name=name, grid_spec=grid_spec, out_shape=jax.ShapeDtypeStruct((2 * rows, cols), F32),
        compiler_params=_params(("parallel",)),
    )(core, parts)


ANY = pl.BlockSpec(memory_space=pl.ANY)


def _place():
    x, y, c = lax.axis_index("x"), lax.axis_index("y"), lax.axis_index("c")
    chips = [(1 - x, y), (x, 1 - y), (1 - x, 1 - y)]
    return x, y, c, chips


def _chunks(rows):
    size = next((c for c in (128, 176, 64, 32, 16, 8) if rows % c == 0), rows)
    return [(r, size) for r in range(0, rows, size)]


def gather_weights(shards):
    n = len(shards)

    def body(*refs):
        start, forward, finish = _gather_phases(refs[:n], refs[n:2 * n], *refs[2 * n:])
        start()
        forward()
        finish()

    gathered = pl.pallas_call(
        body, name="gather_weights", in_specs=[ANY] * n, out_specs=[ANY] * n,
        out_shape=_gather_out_shapes(shards), scratch_shapes=_gather_scratch(n),
    )(*shards)
    return gathered


def _gather_out_shapes(shards):
    return [jax.ShapeDtypeStruct((4,) + s.shape, s.dtype) for s in shards]


def _gather_scratch(n):
    return [pltpu.SemaphoreType.DMA((n, 7)), pltpu.SemaphoreType.DMA((n, 7))]


def _gather_phases(ins, outs, send_sems, recv_sems):
    n = len(ins)
    x, y, c, chips = _place()
    me = 2 * x + y
    sibling = (x, y, 1 - c)
    blks = [2 * cx + cy for cx, cy in chips]

    def half(a, blk, r=0, nr=None):
        rows = ins[a].shape[0] // 2
        return outs[a].at[blk, pl.ds(c * rows + r, rows if nr is None else nr), :]

    def to_chip(a, t, r=0, nr=None):
        rows = ins[a].shape[0] // 2
        return pltpu.make_async_remote_copy(
            src_ref=ins[a].at[pl.ds(c * rows + r, rows if nr is None else nr), :], dst_ref=half(a, me, r, nr),
            send_sem=send_sems.at[a, t], recv_sem=recv_sems.at[a, t], device_id=(*chips[t], c), device_id_type=MESH)

    def from_chip(a, t):
        return pltpu.make_async_remote_copy(
            src_ref=half(a, blks[t]), dst_ref=half(a, blks[t]), send_sem=send_sems.at[a, t], recv_sem=recv_sems.at[a, t],
            device_id=(*chips[t], c), device_id_type=MESH)

    def to_sibling(a, t, r=0, nr=None):
        return pltpu.make_async_remote_copy(
            src_ref=half(a, blks[t], r, nr), dst_ref=half(a, blks[t], r, nr), send_sem=send_sems.at[a, 3 + t],
            recv_sem=recv_sems.at[a, 3 + t], device_id=sibling, device_id_type=MESH)

    def from_sibling(a, t):
        rows = ins[a].shape[0] // 2
        dst = outs[a].at[blks[t], pl.ds((1 - c) * rows, rows), :]
        return pltpu.make_async_remote_copy(
            src_ref=dst, dst_ref=dst, send_sem=send_sems.at[a, 3 + t], recv_sem=recv_sems.at[a, 3 + t],
            device_id=sibling, device_id_type=MESH)

    def own(a, r=0, nr=None):
        return pltpu.make_async_remote_copy(
            src_ref=ins[a].at[pl.ds(r, ins[a].shape[0] if nr is None else nr), :],
            dst_ref=outs[a].at[me, pl.ds(r, ins[a].shape[0] if nr is None else nr), :],
            send_sem=send_sems.at[a, 6], recv_sem=recv_sems.at[a, 6], device_id=sibling, device_id_type=MESH)

    def start():
        for a in range(n):
            for t in range(3):
                for r, nr in _chunks(ins[a].shape[0] // 2):
                    to_chip(a, t, r, nr).start()
            for r, nr in _chunks(ins[a].shape[0]):
                own(a, r, nr).start()

    def forward():
        for a in range(n):
            for t in range(3):
                from_chip(a, t).wait_recv()
                for r, nr in _chunks(ins[a].shape[0] // 2):
                    to_sibling(a, t, r, nr).start()

    def finish():
        for a in range(n):
            for t in range(3):
                from_sibling(a, t).wait_recv()
        for a in range(n):
            for t in range(3):
                to_chip(a, t).wait_send()
                to_sibling(a, t).wait_send()
            own(a).wait()

    return start, forward, finish


def pair_swap_halves(grads, *, name):
    n = len(grads)

    def body(*refs):
        start, finish = _pair_swap_phases(refs[:n], refs[n:2 * n], *refs[2 * n:])
        start()
        finish()

    return pl.pallas_call(
        body, name=name, in_specs=[ANY] * n, out_specs=[ANY] * n, out_shape=_pair_swap_out_shapes(grads),
        scratch_shapes=_pair_swap_scratch(n),
    )(*grads)


def _pair_swap_out_shapes(grads):
    return [jax.ShapeDtypeStruct((4, g.shape[1] // 2, g.shape[2]), g.dtype) for g in grads]


def _pair_swap_scratch(n):
    return [pltpu.SemaphoreType.DMA((n,)), pltpu.SemaphoreType.DMA((n,))]


def _pair_swap_phases(ins, theirs, send_sems, recv_sems):
    n = len(ins)
    x, y, c, _ = _place()
    sibling = (x, y, 1 - c)

    def start():
        for a in range(n):
            rows = ins[a].shape[1] // 2
            for j in range(4):
                for r, nr in _chunks(rows):
                    pltpu.make_async_remote_copy(
                        src_ref=ins[a].at[j, pl.ds((1 - c) * rows + r, nr), :], dst_ref=theirs[a].at[j, pl.ds(r, nr), :],
                        send_sem=send_sems.at[a], recv_sem=recv_sems.at[a], device_id=sibling, device_id_type=MESH).start()

    def finish():
        for a in range(n):
            pltpu.make_async_remote_copy(src_ref=theirs[a], dst_ref=theirs[a], send_sem=send_sems.at[a],
                                         recv_sem=recv_sems.at[a], device_id=sibling, device_id_type=MESH).wait()

    return start, finish


def _scatter_scratch(n):
    return [pltpu.SemaphoreType.DMA((n, 3)), pltpu.SemaphoreType.DMA((n, 3))]


def _keep_own_blocks(landed, parts):
    if not parts:
        return []
    chip = 2 * lax.axis_index("x") + lax.axis_index("y")
    return [lax.dynamic_update_slice(l, lax.dynamic_slice_in_dim(p, chip, 1, axis=0), (chip, 0, 0))
            for l, p in zip(landed, parts)]


def _scatter_phases(ins, outs, send_sems, recv_sems):
    n = len(ins)
    x, y, c, chips = _place()
    me = 2 * x + y
    blks = [2 * cx + cy for cx, cy in chips]

    def start():
        for a in range(n):
            for r, nr in _chunks(ins[a].shape[1]):
                for t in range(3):
                    pltpu.make_async_remote_copy(
                        src_ref=ins[a].at[blks[t], pl.ds(r, nr), :], dst_ref=outs[a].at[me, pl.ds(r, nr), :],
                        send_sem=send_sems.at[a, t], recv_sem=recv_sems.at[a, t],
                        device_id=(*chips[t], c), device_id_type=MESH).start()

    def finish():
        for a in range(n):
            for t in range(3):
                pltpu.make_async_remote_copy(
                    src_ref=outs[a].at[blks[t]], dst_ref=outs[a].at[blks[t]], send_sem=send_sems.at[a, t],
                    recv_sem=recv_sems.at[a, t], device_id=(*chips[t], c), device_id_type=MESH).wait()

    return start, finish


def pair_join_halves(bufs):
    n = len(bufs)

    def body(*refs):
        outs = refs[n:2 * n]
        send_sems, recv_sems = refs[2 * n:]
        x, y, c, _ = _place()
        sibling = (x, y, 1 - c)
        for a in range(n):
            rows = outs[a].shape[0] // 2
            for r, nr in _chunks(rows):
                mine = outs[a].at[pl.ds(c * rows + r, nr), :]
                pltpu.make_async_remote_copy(src_ref=mine, dst_ref=mine, send_sem=send_sems.at[a], recv_sem=recv_sems.at[a],
                                             device_id=sibling, device_id_type=MESH).start()
        for a in range(n):
            rows = outs[a].shape[0] // 2
            pltpu.make_async_remote_copy(
                src_ref=outs[a].at[pl.ds(c * rows, rows), :], dst_ref=outs[a].at[pl.ds((1 - c) * rows, rows), :],
                send_sem=send_sems.at[a], recv_sem=recv_sems.at[a], device_id=sibling, device_id_type=MESH).wait()

    return pl.pallas_call(
        body, name="pair_join_halves", in_specs=[ANY] * n, out_specs=[ANY] * n,
        out_shape=[jax.ShapeDtypeStruct(b.shape, b.dtype) for b in bufs], input_output_aliases={a: a for a in range(n)},
        scratch_shapes=[pltpu.SemaphoreType.DMA((n,)), pltpu.SemaphoreType.DMA((n,))],
    )(*bufs)


def _allreduce_scratch(rows):
    return [pltpu.VMEM((8, rows, 128), F32), pltpu.SemaphoreType.DMA((7,)), pltpu.SemaphoreType.DMA((7,))]


def _allreduce_phases(in_ref, out_ref, gathered, send_sems, recv_sems):
    x, y, c, _ = _place()
    me = 4 * x + 2 * y + c
    flips = [(fx, fy, fc) for fx in (0, 1) for fy in (0, 1) for fc in (0, 1)][1:]
    peers = [((1 - x) if fx else x, (1 - y) if fy else y, (1 - c) if fc else c) for fx, fy, fc in flips]

    def send(t):
        return pltpu.make_async_remote_copy(
            src_ref=in_ref, dst_ref=gathered.at[me], send_sem=send_sems.at[t], recv_sem=recv_sems.at[t],
            device_id=peers[t], device_id_type=MESH)

    def start():
        gathered[me] = in_ref[...]
        for t in range(7):
            send(t).start()

    def finish():
        for t, (px, py, pc) in enumerate(peers):
            slot = gathered.at[4 * px + 2 * py + pc]
            pltpu.make_async_remote_copy(
                src_ref=slot, dst_ref=slot, send_sem=send_sems.at[t], recv_sem=recv_sems.at[t],
                device_id=(px, py, pc), device_id_type=MESH).wait_recv()
        for t in range(7):
            send(t).wait_send()
        acc = gathered[0]
        for k in range(1, 8):
            acc = acc + gathered[k]
        out_ref[...] = acc

    return start, finish


SMALL_NAMES = ("norm_mix_w", "ssd_conv_w", "ssd_conv_b", "ssd_dt_bias", "ssd_a_log", "ssd_d", "ssd_norm_w", "fox_f_bias",
               "fox_q_norm_w", "fox_k_norm_w", "norm_ffn_w", "ffn_conv_w", "ffn_conv_b")
BIG_NAMES = ("w_in", "w_out", "w_up", "w_down")
WEIGHT_ORDER = ("norm_mix_w", "w_in", "ssd_conv_w", "ssd_conv_b", "ssd_dt_bias", "ssd_a_log", "ssd_d", "ssd_norm_w",
                "fox_f_bias", "fox_q_norm_w", "fox_k_norm_w", "w_out", "norm_ffn_w", "w_up", "ffn_conv_w", "ffn_conv_b", "w_down")
ADAM_ROWS = {"w_in": 256, "w_out": 256, "w_up": 256, "w_down": 176}


def _pack(arrays):
    pieces = []
    for a in arrays:
        flat = a.reshape(-1).astype(F32)
        pieces += [flat, jnp.zeros(((-flat.shape[0]) % 1024,), F32)]
    return jnp.concatenate(pieces).reshape(-1, 128)


def _unpack(packed, shapes):
    out, r = [], 0
    for shp in shapes:
        size = 1
        for d in shp:
            size *= d
        nrow = 8 * (-(-size // 1024))
        out.append(packed[r:r + nrow].reshape(-1)[:size].reshape(shp))
        r += nrow
    return out


IN_SHARD = IN_COLS // 4
IN_SEGMENTS = ((0, 2048, 0), (2048, 2560, 5120), (2560, 2576, MAIN_COLS), (2576, 5648, 2048), (5648, 5664, MAIN_COLS + F_LANE))


def _in_cols(shards, lo, hi):
    out = []
    for j in range(4):
        a, b = max(lo, IN_SHARD * j), min(hi, IN_SHARD * (j + 1))
        if a < b:
            out.append(shards[j][:, a - IN_SHARD * j:b - IN_SHARD * j])
    return out


def _in_grad_shards(g):
    shards = []
    for j in range(4):
        pieces = []
        for lo, hi, at in IN_SEGMENTS:
            a, b = max(lo, IN_SHARD * j), min(hi, IN_SHARD * (j + 1))
            if a < b:
                pieces.append(g[:, at + a - lo:at + b - lo])
        shards.append(jnp.concatenate(pieces, axis=1))
    return jnp.stack(shards)


def _pad_rows(a, rows):
    return jnp.pad(a, ((0, rows - a.shape[0]), (0, 0)))


def kernel(x, norm_mix_w, w_in, ssd_conv_w, ssd_conv_b, ssd_dt_bias, ssd_a_log, ssd_d, ssd_norm_w, fox_f_bias, fox_q_norm_w, fox_k_norm_w, w_out, norm_ffn_w, w_up, ffn_conv_w, ffn_conv_b, w_down, loss_target, m_norm_mix_w, m_w_in, m_ssd_conv_w, m_ssd_conv_b, m_ssd_dt_bias, m_ssd_a_log, m_ssd_d, m_ssd_norm_w, m_fox_f_bias, m_fox_q_norm_w, m_fox_k_norm_w, m_w_out, m_norm_ffn_w, m_w_up, m_ffn_conv_w, m_ffn_conv_b, m_w_down, v_norm_mix_w, v_w_in, v_ssd_conv_w, v_ssd_conv_b, v_ssd_dt_bias, v_ssd_a_log, v_ssd_d, v_ssd_norm_w, v_fox_f_bias, v_fox_q_norm_w, v_fox_k_norm_w, v_w_out, v_norm_ffn_w, v_w_up, v_ffn_conv_w, v_ffn_conv_b, v_w_down):
    w = dict(norm_mix_w=norm_mix_w, w_in=w_in, ssd_conv_w=ssd_conv_w, ssd_conv_b=ssd_conv_b, ssd_dt_bias=ssd_dt_bias,
             ssd_a_log=ssd_a_log, ssd_d=ssd_d, ssd_norm_w=ssd_norm_w, fox_f_bias=fox_f_bias, fox_q_norm_w=fox_q_norm_w,
             fox_k_norm_w=fox_k_norm_w, w_out=w_out, norm_ffn_w=norm_ffn_w, w_up=w_up, ffn_conv_w=ffn_conv_w,
             ffn_conv_b=ffn_conv_b, w_down=w_down)
    m = dict(norm_mix_w=m_norm_mix_w, w_in=m_w_in, ssd_conv_w=m_ssd_conv_w, ssd_conv_b=m_ssd_conv_b, ssd_dt_bias=m_ssd_dt_bias,
             ssd_a_log=m_ssd_a_log, ssd_d=m_ssd_d, ssd_norm_w=m_ssd_norm_w, fox_f_bias=m_fox_f_bias, fox_q_norm_w=m_fox_q_norm_w,
             fox_k_norm_w=m_fox_k_norm_w, w_out=m_w_out, norm_ffn_w=m_norm_ffn_w, w_up=m_w_up, ffn_conv_w=m_ffn_conv_w,
             ffn_conv_b=m_ffn_conv_b, w_down=m_w_down)
    v = dict(norm_mix_w=v_norm_mix_w, w_in=v_w_in, ssd_conv_w=v_ssd_conv_w, ssd_conv_b=v_ssd_conv_b, ssd_dt_bias=v_ssd_dt_bias,
             ssd_a_log=v_ssd_a_log, ssd_d=v_ssd_d, ssd_norm_w=v_ssd_norm_w, fox_f_bias=v_fox_f_bias, fox_q_norm_w=v_fox_q_norm_w,
             fox_k_norm_w=v_fox_k_norm_w, w_out=v_w_out, norm_ffn_w=v_norm_ffn_w, w_up=v_w_up, ffn_conv_w=v_ffn_conv_w,
             ffn_conv_b=v_ffn_conv_b, w_down=v_w_down)
    chip = 2 * lax.axis_index("x") + lax.axis_index("y")

    a_in, a_scw, a_fcw = gather_weights([w_in[0].astype(BF16), _pad_rows(ssd_conv_w[0], 16), _pad_rows(ffn_conv_w[0], 16)])
    later_shards = [w_out[0].astype(BF16), w_up[0].astype(BF16), w_down[0].astype(BF16)]
    wx = jnp.concatenate([p for lo, hi, _ in sorted(IN_SEGMENTS, key=lambda seg: seg[2]) for p in _in_cols(a_in, lo, hi)]
                         + [jnp.zeros((D_MODEL, PROJ_COLS - IN_COLS), BF16)], axis=1)
    ssd_cw8 = a_scw.transpose(1, 0, 2).reshape(16, 1536)[:8]
    ffn_cw8 = a_fcw.transpose(1, 0, 2).reshape(16, 2 * D_FF)[:8]
    gap = lambda n: jnp.zeros((n,), F32)
    smallp = jnp.concatenate([ssd_dt_bias[0], gap(112), ssd_a_log[0], gap(112), ssd_d[0], gap(112),
                              gap(F_LANE), fox_f_bias[0], gap(128 - F_LANE - N_HEADS), gap(4 * 128)]).reshape(8, 128)
    qw_t = jnp.tile(fox_q_norm_w[0], N_HEADS)[None]
    kw_t = jnp.tile(fox_k_norm_w[0], N_HEADS)[None]
    sel = jnp.asarray((np.arange(1024)[:, None] // HEAD_DIM == np.arange(128)[None, :]).astype(np.float32), BF16)

    res = local_step(x[0], loss_target[0], wx, later_shards, ssd_cw8, ssd_conv_b, smallp, ssd_norm_w, qw_t, kw_t,
                     sel, sel.T, norm_mix_w, norm_ffn_w, ffn_cw8, ffn_conv_b)

    full_shapes = [(1, 1024), (1, 4, 1536), (1, 1536), (1, 16), (1, 16), (1, 16), (1, 1024), (1, 16), (1, 64), (1, 64),
                   (1, 1024), (1, 3, 2 * D_FF), (1, 2 * D_FF), (1,)]
    local_small = [res["g_norm_mix"], res["g_ssd_cw"][:4], res["g_ssd_cw"][4], res["g_sp"][0, :16], res["g_sp"][1, :16],
                   res["g_sp"][2, :16], res["g_ssd_nw"], res["g_fb"][0, F_LANE:F_LANE + 16],
                   res["g_qw"].reshape(N_HEADS, HEAD_DIM).sum(0), res["g_kw"].reshape(N_HEADS, HEAD_DIM).sum(0),
                   res["g_norm_ffn"], res["g_ffn_cw"][:3], res["g_ffn_cw"][3], jnp.sum(res["sq"])]
    landed = res["landed"]
    core = lax.axis_index("c").astype(jnp.int32).reshape(1)
    halves = [sum_chips(p, core, name="sum_chips_" + n, tr=ADAM_ROWS[n]) for p, n in zip(landed, BIG_NAMES)]
    g_big = dict(zip(BIG_NAMES, pair_join_halves(halves)))

    grads, deltas, new_m, new_v = {}, {}, {}, {}
    for n in BIG_NAMES:
        out = adamw(w[n][0], g_big[n], m[n][0], v[n][0], name="adamw_" + n, tr=ADAM_ROWS[n],
                    allreduce=_pack(local_small) if n == BIG_NAMES[0] else None)
        if n == BIG_NAMES[0]:
            summed = _unpack(out[3], full_shapes)
        d, mn, vn = out[:3]
        grads[n], deltas[n], new_m[n], new_v[n] = g_big[n][None], d[None], mn[None], vn[None]
    loss = (0.5 / D_MODEL) * summed[-1][0]
    g_small = dict(zip(SMALL_NAMES, summed[:-1]))
    g_small["ssd_conv_w"] = lax.dynamic_slice(g_small["ssd_conv_w"], (0, 0, 384 * chip), (1, 4, 384))
    g_small["ffn_conv_w"] = lax.dynamic_slice(g_small["ffn_conv_w"], (0, 0, 1408 * chip), (1, 3, 1408))
    shapes = [w[n].shape for n in SMALL_NAMES]
    packed_w = _pack([w[n] for n in SMALL_NAMES])
    d, mn, vn = adamw(packed_w, _pack([g_small[n] for n in SMALL_NAMES]), _pack([m[n] for n in SMALL_NAMES]),
                      _pack([v[n] for n in SMALL_NAMES]), name="adamw_small", tr=packed_w.shape[0])
    for n, dd, mm, vv in zip(SMALL_NAMES, _unpack(d, shapes), _unpack(mn, shapes), _unpack(vn, shapes)):
        grads[n], deltas[n], new_m[n], new_v[n] = g_small[n].reshape(w[n].shape), dd, mm, vv
    return (loss, res["grad_x"][None], *[grads[n] for n in WEIGHT_ORDER], *[deltas[n] for n in WEIGHT_ORDER],
            *[new_m[n] for n in WEIGHT_ORDER], *[new_v[n] for n in WEIGHT_ORDER])
```

```python
import functools

import jax
import jax.numpy as jnp
import numpy as np
from jax import lax
from jax.experimental import pallas as pl
from jax.experimental.pallas import tpu as pltpu

F32 = jnp.float32
BF16 = jnp.bfloat16
MESH = pl.DeviceIdType.MESH

D_MODEL = 1024
HEAD_DIM = 64
N_HEADS = 16
N_PAIRS = N_HEADS // 2
SSD_CHUNK = 128
SSD_STATE = 128
SSD_CONV = 4
D_FF = 2816
FFN_CONV = 3
NORM_EPS = 1e-6
MAIN_COLS = 5632
SMALL_COLS = 128
PROJ_COLS = MAIN_COLS + SMALL_COLS
SMALL_BLOCK = MAIN_COLS // SMALL_COLS
PROJ_TILE = 1152
F_LANE = 16
IN_COLS = 5664

ADAM_LR = 0.001
ADAM_B1 = 0.9
ADAM_B2 = 0.999
ADAM_EPS = 1e-08
ADAM_WD = 0.01
ADAM_STEP = 10

VMEM_LIMIT_V7X = 56 * 1024 * 1024
NEG_BIG = -1e30


def _params(sem=None):
    return pltpu.CompilerParams(dimension_semantics=sem, vmem_limit_bytes=VMEM_LIMIT_V7X)


def _sigmoid(x):
    return 1.0 / (1.0 + jnp.exp(-x))


def _silu_and_grad(x):
    s = _sigmoid(x)
    return x * s, s * (1.0 + x * (1.0 - s))


def _shift_down(v, j):
    return v if j == 0 else pltpu.roll(v, j, 0)


def _shift_up(v, j):
    return v if j == 0 else pltpu.roll(v, v.shape[0] - j, 0)


def _row_iota(shape):
    return lax.broadcasted_iota(jnp.int32, shape, 0)


def _lane_iota(shape):
    return lax.broadcasted_iota(jnp.int32, shape, 1)


def _dot(a, b, mode="nn"):
    dims = {"nn": (((1,), (0,)), ((), ())), "nt": (((1,), (1,)), ((), ())), "tn": (((0,), (0,)), ((), ()))}[mode]
    return lax.dot_general(a.astype(BF16), b.astype(BF16), dims, preferred_element_type=F32)


def _dot_f32(a, b):
    return jnp.dot(a, b, precision=lax.Precision.HIGHEST, preferred_element_type=F32)


def matmul(a, b, *, mode, tm, tn, tk, out_dtype, name, add=None, b_koff=0, scatter=(), layout=None):
    layout = layout or {}
    if layout:
        m, n, k = layout["m"], layout["n"], layout["k"]
    else:
        (m, k), n = a.shape, (b.shape[1] if mode == "nn" else b.shape[0])
    assert m % tm == 0 and n % tn == 0 and k % tk == 0, (name, m, n, k, tm, tn, tk)
    nk = k // tk
    grid = (m // tm, n // tn, nk)
    a_spec = layout.get("a_spec") or pl.BlockSpec((tm, tk), lambda i, j, kk: (i, kk))
    b_spec = layout.get("b_spec") or (pl.BlockSpec((tn, tk), lambda i, j, kk: (j, kk + b_koff)) if mode == "nt"
                                      else pl.BlockSpec((tk, tn), lambda i, j, kk: (kk + b_koff, j)))
    o_spec = layout.get("o_spec") or pl.BlockSpec((tm, tn), lambda i, j, kk: (i, j))
    out_struct = jax.ShapeDtypeStruct(layout.get("out_shape", (m, n)), out_dtype)
    has_add = add is not None
    n_in = 3 if has_add else 2
    ns = len(scatter)

    def body(*refs):
        a_ref, b_ref = refs[:2]
        add_ref = refs[2] if has_add else None
        o_ref, acc_ref = refs[n_in + ns], refs[n_in + 2 * ns + 1]
        kk = pl.program_id(2)
        if ns:
            step = (pl.program_id(0) * grid[1] + pl.program_id(1)) * grid[2] + kk
            start, finish_copies = _scatter_phases(refs[n_in:n_in + ns], refs[n_in + ns + 1:n_in + 2 * ns + 1],
                                                   *refs[n_in + 2 * ns + 2:])
            pl.when(step == 0)(start)
        part = _dot(a_ref[...], b_ref[...], mode)

        def finish(total):
            if has_add:
                total = total + add_ref[...]
            o_ref[...] = total.astype(out_dtype)

        if nk == 1:
            finish(part)
        else:
            @pl.when(kk == 0)
            def _():
                acc_ref[...] = part

            @pl.when(jnp.logical_and(kk > 0, kk < nk - 1))
            def _():
                acc_ref[...] += part

            @pl.when(kk == nk - 1)
            def _():
                finish(acc_ref[...] + part)

        if ns:
            pl.when(step == grid[0] * grid[1] * grid[2] - 1)(finish_copies)

    in_specs = [a_spec, b_spec] + ([o_spec] if has_add else [])
    args = (a, b) + ((add,) if has_add else ())
    acc = pltpu.VMEM((tm, tn) if nk > 1 else (8, 128), F32)
    if not ns:
        return pl.pallas_call(
            body, name=name, grid=grid, in_specs=in_specs, out_specs=o_spec, out_shape=out_struct,
            scratch_shapes=[acc], compiler_params=_params(("parallel", "parallel", "arbitrary")),
        )(*args)
    outs = pl.pallas_call(
        body, name=name, grid=grid, in_specs=in_specs + [ANY] * ns, out_specs=[o_spec] + [ANY] * ns,
        out_shape=[out_struct] + [jax.ShapeDtypeStruct(p.shape, p.dtype) for p in scatter],
        scratch_shapes=[acc] + _scatter_scratch(ns), compiler_params=_params(("arbitrary", "arbitrary", "arbitrary")),
    )(*args, *scatter)
    return outs[0], _keep_own_blocks(outs[1:], scatter)


def rms_fwd(x, w, *, name, tm=1024):
    s, d = x.shape

    def body(x_ref, w_ref, h_ref, ht_ref):
        xv = x_ref[...]
        r = lax.rsqrt(jnp.mean(xv * xv, axis=-1, keepdims=True) + NORM_EPS)
        h = (xv * r) * w_ref[...]
        h_ref[...] = h.astype(BF16)
        ht_ref[...] = h.T.astype(BF16)

    return pl.pallas_call(
        body, name=name, grid=(s // tm,),
        in_specs=[pl.BlockSpec((tm, d), lambda i: (i, 0)), pl.BlockSpec((1, d), lambda i: (0, 0))],
        out_specs=[pl.BlockSpec((tm, d), lambda i: (i, 0)), pl.BlockSpec((d, tm), lambda i: (0, i))],
        out_shape=[jax.ShapeDtypeStruct((s, d), BF16), jax.ShapeDtypeStruct((d, s), BF16)],
        compiler_params=_params(("parallel",)),
    )(x, w)


def rms_bwd(dh, x, w, resid, *, name, tm=1024):
    s, d = x.shape

    def body(dh_ref, x_ref, w_ref, res_ref, dx_ref, dw_ref):
        xv = x_ref[...]
        dhv = dh_ref[...]
        r = lax.rsqrt(jnp.mean(xv * xv, axis=-1, keepdims=True) + NORM_EPS)
        xh = xv * r
        g = dhv * w_ref[...]
        dx_ref[...] = res_ref[...] + r * (g - xh * jnp.mean(g * xh, axis=-1, keepdims=True))
        part = jnp.sum(dhv * xh, axis=0, keepdims=True)

        @pl.when(pl.program_id(0) == 0)
        def _():
            dw_ref[...] = part

        @pl.when(pl.program_id(0) > 0)
        def _():
            dw_ref[...] += part

    row = pl.BlockSpec((tm, d), lambda i: (i, 0))
    vec = pl.BlockSpec((1, d), lambda i: (0, 0))
    return pl.pallas_call(
        body, name=name, grid=(s // tm,), in_specs=[row, row, vec, row], out_specs=[row, vec],
        out_shape=[jax.ShapeDtypeStruct((s, d), F32), jax.ShapeDtypeStruct((1, d), F32)],
        compiler_params=_params(("arbitrary",)),
    )(dh, x, w, resid)


def loss_head(y, target, *, tm=1024):
    s, d = y.shape

    def body(y_ref, t_ref, dy_ref, sq_ref):
        e = y_ref[...] - t_ref[...]
        dy_ref[...] = e / float(d)
        part = jnp.sum(e * e, axis=0, keepdims=True)

        @pl.when(pl.program_id(0) == 0)
        def _():
            sq_ref[...] = part

        @pl.when(pl.program_id(0) > 0)
        def _():
            sq_ref[...] += part

    row = pl.BlockSpec((tm, d), lambda i: (i, 0))
    vec = pl.BlockSpec((1, d), lambda i: (0, 0))
    return pl.pallas_call(
        body, name="loss_head", grid=(s // tm,), in_specs=[row, row], out_specs=[row, vec],
        out_shape=[jax.ShapeDtypeStruct((s, d), F32), jax.ShapeDtypeStruct((1, d), F32)],
        compiler_params=_params(("arbitrary",)),
    )(y, target)


def _row_shifts(ext, k_taps):
    return [_shift_down(ext, j) for j in range(k_taps)]


def _conv_rows(shifts, w):
    k_taps = len(shifts)
    acc = w[k_taps - 1:k_taps, :] * shifts[0]
    for k in range(k_taps - 1):
        acc = acc + w[k:k + 1, :] * shifts[k_taps - 1 - k]
    return acc


def _conv_weight_grad(dcur, shifts, rows, width):
    k_taps = len(shifts)
    out = [jnp.sum(dcur * shifts[k_taps - 1 - k][rows], axis=0, keepdims=True) for k in range(k_taps)]
    out.append(jnp.sum(dcur, axis=0, keepdims=True))
    return _stack_rows(out, width)


def _conv_rows_transposed(dext, w, k_taps):
    acc = w[k_taps - 1:k_taps, :] * dext
    for k in range(k_taps - 1):
        acc = acc + w[k:k + 1, :] * _shift_up(dext, k_taps - 1 - k)
    return acc


def _stack_rows(rows, width):
    ri = _row_iota((8, width))
    out = jnp.zeros((8, width), F32)
    for k, r in enumerate(rows):
        out = out + jnp.where(ri == k, r, 0.0)
    return out


UP_SHARD = 1408
UP_ROWS = 256


def up_ffn_fwd(hf, a_up, conv_w8, conv_b, *, tm=512):
    s = hf.shape[0]

    def body(a_ref, bg_ref, bv_ref, wg_ref, wv_ref, cbg_ref, cbv_ref, hu_ref, act_ref, actt_ref, carry):
        i, j = pl.program_id(0), pl.program_id(1)
        prev_g = jnp.where(i == 0, 0.0, carry[0, j])
        prev_v = jnp.where(i == 0, 0.0, carry[1, j])
        for r in range(0, tm, UP_ROWS):
            rows = slice(r, r + UP_ROWS)
            a = a_ref[rows, :]
            hg, hv = _dot(a, bg_ref[...]), _dot(a, bv_ref[...])
            hu_ref[0, rows, :] = hg
            hu_ref[1, rows, :] = hv
            gc = _conv_rows(_row_shifts(jnp.concatenate([prev_g, hg], axis=0), FFN_CONV), wg_ref[...])[8:] + cbg_ref[...]
            vc = _conv_rows(_row_shifts(jnp.concatenate([prev_v, hv], axis=0), FFN_CONV), wv_ref[...])[8:] + cbv_ref[...]
            act = gc * _sigmoid(gc) * vc
            act_ref[rows, :] = act.astype(BF16)
            actt_ref[:, rows] = act.T.astype(BF16)
            prev_g, prev_v = hg[UP_ROWS - 8:], hv[UP_ROWS - 8:]
        carry[0, j] = prev_g
        carry[1, j] = prev_v

    shard = lambda off: pl.BlockSpec((None, D_MODEL, UP_SHARD), lambda i, j: (j + off, 0, 0))
    taps = lambda off: pl.BlockSpec((8, UP_SHARD), lambda i, j: (0, j + off))
    bias = lambda off: pl.BlockSpec((1, UP_SHARD), lambda i, j: (0, j + off))
    return pl.pallas_call(
        body, name="up_ffn_fwd", grid=(s // tm, 2),
        in_specs=[pl.BlockSpec((tm, D_MODEL), lambda i, j: (i, 0)), shard(0), shard(2), taps(0), taps(2), bias(0), bias(2)],
        out_specs=[pl.BlockSpec((2, tm, UP_SHARD), lambda i, j: (0, i, j)), pl.BlockSpec((tm, UP_SHARD), lambda i, j: (i, j)),
                   pl.BlockSpec((UP_SHARD, tm), lambda i, j: (j, i))],
        out_shape=[jax.ShapeDtypeStruct((2, s, D_FF), F32), jax.ShapeDtypeStruct((s, D_FF), BF16),
                   jax.ShapeDtypeStruct((D_FF, s), BF16)],
        scratch_shapes=[pltpu.VMEM((2, 2, 8, UP_SHARD), F32)], compiler_params=_params(("arbitrary", "arbitrary")),
    )(hf, a_up, a_up, conv_w8, conv_w8, conv_b, conv_b)


def ffn_mid_bwd(hu, dact, conv_w8, conv_b, *, tm=1024, tc=256):
    s = hu.shape[1]
    ncol = D_FF // tc
    nrow = s // tm
    r8 = tm // 8

    def body(g_ref, v_ref, gp_ref, vp_ref, gn_ref, vn_ref, da_ref, dan_ref, wg_ref, wv_ref, bg_ref, bv_ref,
             dhu_ref, wgo_ref, wvo_ref):
        i = pl.program_id(1)
        first = i == 0
        last = i == nrow - 1

        def ext_of(cur_ref, prev_ref, next_ref):
            prev = jnp.where(first, 0.0, prev_ref[...])
            return jnp.concatenate([prev, cur_ref[...], next_ref[...]], axis=0)

        g_sh = _row_shifts(ext_of(g_ref, gp_ref, gn_ref), FFN_CONV)
        v_sh = _row_shifts(ext_of(v_ref, vp_ref, vn_ref), FFN_CONV)
        gc = _conv_rows(g_sh, wg_ref[...]) + bg_ref[...]
        vc = _conv_rows(v_sh, wv_ref[...]) + bv_ref[...]
        da_ext = jnp.concatenate([jnp.zeros((8, tc), F32), da_ref[...], jnp.where(last, 0.0, dan_ref[...])], axis=0)
        silu, dsilu = _silu_and_grad(gc)
        dgc = da_ext * vc * dsilu
        dvc = da_ext * silu
        dhu_ref[0] = _conv_rows_transposed(dgc, wg_ref[...], FFN_CONV)[8:8 + tm].astype(BF16)
        dhu_ref[1] = _conv_rows_transposed(dvc, wv_ref[...], FFN_CONV)[8:8 + tm].astype(BF16)

        cur = slice(8, 8 + tm)
        pg = _conv_weight_grad(dgc[cur], g_sh, cur, tc)
        pv = _conv_weight_grad(dvc[cur], v_sh, cur, tc)

        @pl.when(first)
        def _():
            wgo_ref[...] = pg
            wvo_ref[...] = pv

        @pl.when(i > 0)
        def _():
            wgo_ref[...] += pg
            wvo_ref[...] += pv

    def prev_idx(i):
        return jnp.maximum(i * r8 - 1, 0)

    def next_idx(i):
        return jnp.minimum((i + 1) * r8, s // 8 - 1)

    half = lambda k, rows, row_index: pl.BlockSpec((None, rows, tc), lambda j, i: (k, row_index(i), j))
    in_specs = [
        half(0, tm, lambda i: i), half(1, tm, lambda i: i),
        half(0, 8, prev_idx), half(1, 8, prev_idx),
        half(0, 8, next_idx), half(1, 8, next_idx),
        pl.BlockSpec((tm, tc), lambda j, i: (i, j)),
        pl.BlockSpec((8, tc), lambda j, i: (next_idx(i), j)),
        pl.BlockSpec((8, tc), lambda j, i: (0, j)),
        pl.BlockSpec((8, tc), lambda j, i: (0, j + ncol)),
        pl.BlockSpec((1, tc), lambda j, i: (0, j)),
        pl.BlockSpec((1, tc), lambda j, i: (0, j + ncol)),
    ]
    out_specs = [pl.BlockSpec((2, tm, tc), lambda j, i: (0, i, j)), pl.BlockSpec((8, tc), lambda j, i: (0, j)),
                 pl.BlockSpec((8, tc), lambda j, i: (0, j))]
    out_shape = [jax.ShapeDtypeStruct((2, s, D_FF), BF16),
                 jax.ShapeDtypeStruct((8, D_FF), F32), jax.ShapeDtypeStruct((8, D_FF), F32)]
    return pl.pallas_call(
        body, name="ffn_mid_bwd", grid=(ncol, nrow), in_specs=in_specs, out_specs=out_specs, out_shape=out_shape,
        compiler_params=_params(("parallel", "arbitrary")),
    )(hu, hu, hu, hu, hu, hu, dact, dact, conv_w8, conv_w8, conv_b, conv_b)


def _softplus(x):
    return jnp.maximum(x, 0.0) + jnp.log(1.0 + jnp.exp(-jnp.abs(x)))


def _cumsum_rows(v):
    n = v.shape[0]
    ri = _row_iota(v.shape)
    sh = 1
    while sh < n:
        v = v + jnp.where(ri >= sh, _shift_down(v, sh), 0.0)
        sh *= 2
    return v


def _rev_cumsum_rows(v):
    n = v.shape[0]
    ri = _row_iota(v.shape)
    sh = 1
    while sh < n:
        v = v + jnp.where(ri < n - sh, _shift_up(v, sh), 0.0)
        sh *= 2
    return v


def _total(v):
    return jnp.sum(jnp.sum(v, axis=1, keepdims=True), axis=0, keepdims=True)


def _ssd_in_specs(rev_nc=None):
    def ch(c):
        return c if rev_nc is None else rev_nc - 1 - c

    def prev(c):
        return jnp.maximum(ch(c) * (SSD_CHUNK // 8) - 1, 0)

    L = SSD_CHUNK
    return [
        pl.BlockSpec((L, 1024), lambda c: (ch(c), 0)),
        pl.BlockSpec((L, 1024), lambda c: (ch(c), 1)),
        pl.BlockSpec((L, 256), lambda c: (ch(c), 20)),
        pl.BlockSpec((L, 256), lambda c: (ch(c), 21)),
        pl.BlockSpec((8, 1024), lambda c: (prev(c), 1)),
        pl.BlockSpec((8, 256), lambda c: (prev(c), 20)),
        pl.BlockSpec((8, 256), lambda c: (prev(c), 21)),
        pl.BlockSpec((8, 1024), lambda c: (0, 0)),
        pl.BlockSpec((8, 256), lambda c: (0, 4)),
        pl.BlockSpec((8, 256), lambda c: (0, 5)),
        pl.BlockSpec((1, 1024), lambda c: (0, 0)),
        pl.BlockSpec((1, 256), lambda c: (0, 4)),
        pl.BlockSpec((1, 256), lambda c: (0, 5)),
        pl.BlockSpec((L, SMALL_COLS), lambda c: (ch(c), SMALL_BLOCK)),
        pl.BlockSpec((8, 128), lambda c: (0, 0)),
        pl.BlockSpec((1, 1024), lambda c: (0, 0)),
    ]


def _ssd_conv_pre(cur_ref, prev_ref, w_ref, b_ref, first):
    prev = jnp.where(first, 0.0, prev_ref[...])
    shifts = _row_shifts(jnp.concatenate([prev, cur_ref[...]], axis=0), SSD_CONV)
    return shifts, _conv_rows(shifts, w_ref[...])[8:] + b_ref[...]


def _ssd_time_consts(small_ref, sp_ref):
    dt_pre = small_ref[...] + sp_ref[0:1, :]
    dt = _softplus(dt_pre)
    a = -jnp.exp(sp_ref[1:2, :])
    acs = _cumsum_rows(dt * a)
    return dt_pre, dt, a, acs


def ssd_fwd(proj, conv_w8, conv_b, smallp, norm_w):
    s = proj.shape[0]
    nc = s // SSD_CHUNK
    L = SSD_CHUNK

    def body(z_ref, xs_ref, b_ref, c_ref, xsp_ref, bp_ref, cp_ref, wx_ref, wb_ref, wc_ref, bx_ref, bb_ref, bc_ref,
             small_ref, sp_ref, nw_ref, y_ref, yt_ref, ypre_ref, st_ref, state):
        first = pl.program_id(0) == 0

        @pl.when(first)
        def _():
            state[...] = jnp.zeros_like(state)

        xs = _ssd_conv_pre(xs_ref, xsp_ref, wx_ref, bx_ref, first)[1]
        xs = xs * _sigmoid(xs)
        bm = _ssd_conv_pre(b_ref, bp_ref, wb_ref, bb_ref, first)[1]
        bm = bm * _sigmoid(bm)
        cm = _ssd_conv_pre(c_ref, cp_ref, wc_ref, bc_ref, first)[1]
        cm = cm * _sigmoid(cm)
        _, dt, _, acs = _ssd_time_consts(small_ref, sp_ref)
        acs_t = acs.T
        li = _lane_iota((L, L))
        ri = _row_iota((L, L))
        tri = ri >= li
        lo = li < HEAD_DIM
        st_ref[0] = state[...]
        for g in range(2):
            bg = bm[:, 128 * g:128 * g + 128]
            cg = cm[:, 128 * g:128 * g + 128]
            gmat = _dot(cg, bg, "nt")
            for pp in range(4):
                p = 4 * g + pp
                h0, h1 = 2 * p, 2 * p + 1
                x = xs[:, 128 * p:128 * p + 128]
                a0, a1 = acs[:, h0:h0 + 1], acs[:, h1:h1 + 1]
                xdt = x * jnp.where(lo, dt[:, h0:h0 + 1], dt[:, h1:h1 + 1])
                m0 = gmat * jnp.exp(jnp.where(tri, a0 - acs_t[h0:h0 + 1, :], NEG_BIG))
                m1 = gmat * jnp.exp(jnp.where(tri, a1 - acs_t[h1:h1 + 1, :], NEG_BIG))
                yd = _dot(m0, jnp.where(lo, xdt, 0.0)) + _dot(m1, jnp.where(lo, 0.0, xdt))
                hin = state[p]
                yo = _dot(cg, hin, "nt") * jnp.exp(jnp.where(lo, a0, a1))
                dskip = jnp.where(lo[0:1], sp_ref[2:3, h0:h0 + 1], sp_ref[2:3, h1:h1 + 1])
                ypre_ref[:, 128 * p:128 * p + 128] = yd + yo + dskip * x
                al0, al1 = acs[L - 1:L, h0:h0 + 1], acs[L - 1:L, h1:h1 + 1]
                w = jnp.exp(jnp.where(lo, al0 - a0, al1 - a1))
                dec = jnp.exp(jnp.where(ri < HEAD_DIM, al0, al1))
                state[p] = dec * hin + _dot(xdt * w, bg, "tn")
        z = z_ref[...]
        yg = ypre_ref[...] * (z * _sigmoid(z))
        for g in range(2):
            seg = yg[:, 512 * g:512 * g + 512]
            r = lax.rsqrt(jnp.mean(seg * seg, axis=-1, keepdims=True) + NORM_EPS)
            out = (seg * r) * nw_ref[:, 512 * g:512 * g + 512]
            y_ref[:, 512 * g:512 * g + 512] = out.astype(BF16)
            yt_ref[512 * g:512 * g + 512, :] = out.T.astype(BF16)

    row = pl.BlockSpec((L, 1024), lambda c: (c, 0))
    return pl.pallas_call(
        body, name="ssd_fwd", grid=(nc,), in_specs=_ssd_in_specs(),
        out_specs=[row, pl.BlockSpec((1024, L), lambda c: (0, c)), row,
                   pl.BlockSpec((1, N_PAIRS, 128, 128), lambda c: (c, 0, 0, 0))],
        out_shape=[jax.ShapeDtypeStruct((s, 1024), BF16), jax.ShapeDtypeStruct((1024, s), BF16),
                   jax.ShapeDtypeStruct((s, 1024), F32), jax.ShapeDtypeStruct((nc, N_PAIRS, 128, 128), F32)],
        scratch_shapes=[pltpu.VMEM((N_PAIRS, 128, 128), F32)],
        compiler_params=_params(("arbitrary",)),
    )(proj, proj, proj, proj, proj, proj, proj, conv_w8, conv_w8, conv_w8, conv_b, conv_b, conv_b, proj, smallp, norm_w)


def ssd_bwd(proj, conv_w8, conv_b, smallp, norm_w, ypre, states, dy, sel, swap=()):
    s = proj.shape[0]
    nc = s // SSD_CHUNK
    L = SSD_CHUNK

    ns = len(swap)
    n_in, n_out, n_scratch = 20, 10, 11

    def body(*refs):
        own = refs[:n_in] + refs[n_in + ns:n_in + ns + n_out] + refs[n_in + 2 * ns + n_out:n_in + 2 * ns + n_out + n_scratch]
        if ns:
            start, finish = _pair_swap_phases(refs[n_in:n_in + ns], refs[n_in + ns + n_out:n_in + 2 * ns + n_out],
                                              *refs[n_in + 2 * ns + n_out + n_scratch:])
            pl.when(pl.program_id(0) == 0)(start)
        compute(*own)
        if ns:
            pl.when(pl.program_id(0) == nc - 1)(finish)

    def compute(z_ref, xs_ref, b_ref, c_ref, xsp_ref, bp_ref, cp_ref, wx_ref, wb_ref, wc_ref, bx_ref, bb_ref, bc_ref,
                small_ref, sp_ref, nw_ref, ypre_ref, st_ref, dy_ref, sel_ref,
                dz_ref, dxs_ref, db_ref, dc_ref, dsmall_ref, gwx_ref, gwb_ref, gwc_ref, gsp_ref, gnw_ref,
                dstate, carry_x, carry_b, carry_c, dxs_buf, dbm_buf, dcm_buf, qcs, col_sums, acs_terms, dt_terms):
        step = pl.program_id(0)
        col_sums[...] = jnp.zeros_like(col_sums)
        first_chunk = step == nc - 1
        start = step == 0

        @pl.when(start)
        def _():
            dstate[...] = jnp.zeros_like(dstate)
            carry_x[...] = jnp.zeros_like(carry_x)
            carry_b[...] = jnp.zeros_like(carry_b)
            carry_c[...] = jnp.zeros_like(carry_c)

        xs_sh, xs_pre = _ssd_conv_pre(xs_ref, xsp_ref, wx_ref, bx_ref, first_chunk)
        b_sh, b_pre = _ssd_conv_pre(b_ref, bp_ref, wb_ref, bb_ref, first_chunk)
        c_sh, c_pre = _ssd_conv_pre(c_ref, cp_ref, wc_ref, bc_ref, first_chunk)
        xs, xs_ds = _silu_and_grad(xs_pre)
        bm, b_ds = _silu_and_grad(b_pre)
        cm, c_ds = _silu_and_grad(c_pre)
        dt_pre, dt, a, acs = _ssd_time_consts(small_ref, sp_ref)
        acs_t = acs.T
        li = _lane_iota((L, L))
        ri = _row_iota((L, L))
        tri = ri >= li
        lo = li < HEAD_DIM
        lo_rows = ri < HEAD_DIM
        li1 = _lane_iota((1, L))

        z = z_ref[...]
        sz, dsz = _silu_and_grad(z)
        y = ypre_ref[...]
        yg = y * sz
        dout = dy_ref[...]
        dyg_parts = []
        gnw_parts = []
        for g in range(2):
            sl = slice(512 * g, 512 * g + 512)
            seg = yg[:, sl]
            r = lax.rsqrt(jnp.mean(seg * seg, axis=-1, keepdims=True) + NORM_EPS)
            n = seg * r
            gnw_parts.append(jnp.sum(dout[:, sl] * n, axis=0, keepdims=True))
            gg = dout[:, sl] * nw_ref[:, sl]
            dyg_parts.append(r * (gg - n * jnp.mean(gg * n, axis=-1, keepdims=True)))
        dyg = jnp.concatenate(dyg_parts, axis=1)
        gnw = jnp.concatenate(gnw_parts, axis=1)
        dz_ref[...] = (dyg * y * dsz).astype(BF16)
        dypre = dyg * sz

        qcs[...] = jnp.zeros_like(qcs)
        dalast = jnp.zeros((1, L), F32)
        for g in range(2):
            bg = bm[:, 128 * g:128 * g + 128]
            cg = cm[:, 128 * g:128 * g + 128]
            gmat = _dot(cg, bg, "nt")
            dgmat = jnp.zeros((L, L), F32)
            dbg = jnp.zeros((L, L), F32)
            dcg = jnp.zeros((L, L), F32)
            for pp in range(4):
                p = 4 * g + pp
                h0, h1 = 2 * p, 2 * p + 1
                lanes = slice(128 * p, 128 * p + 128)
                x = xs[:, lanes]
                dyp = dypre[:, lanes]
                a0, a1 = acs[:, h0:h0 + 1], acs[:, h1:h1 + 1]
                dtl = jnp.where(lo, dt[:, h0:h0 + 1], dt[:, h1:h1 + 1])
                xdt = x * dtl
                l0 = jnp.exp(jnp.where(tri, a0 - acs_t[h0:h0 + 1, :], NEG_BIG))
                l1 = jnp.exp(jnp.where(tri, a1 - acs_t[h1:h1 + 1, :], NEG_BIG))
                m0, m1 = gmat * l0, gmat * l1
                dskip = jnp.where(lo[0:1], sp_ref[2:3, h0:h0 + 1], sp_ref[2:3, h1:h1 + 1])
                col_sums[0:1, lanes] = jnp.sum(dyp * x, axis=0, keepdims=True)
                dx = dyp * dskip
                dy0, dy1 = jnp.where(lo, dyp, 0.0), jnp.where(lo, 0.0, dyp)
                x0, x1 = jnp.where(lo, xdt, 0.0), jnp.where(lo, 0.0, xdt)
                dm0, dm1 = _dot(dy0, x0, "nt"), _dot(dy1, x1, "nt")
                dxdt = _dot(m0, dy0, "tn") + _dot(m1, dy1, "tn")
                q0, q1 = dm0 * m0, dm1 * m1
                qcs[h0:h0 + 1, :] = jnp.sum(q0, axis=0, keepdims=True)
                qcs[h1:h1 + 1, :] = jnp.sum(q1, axis=0, keepdims=True)
                row_terms = jnp.where(lo, q0 + pltpu.roll(q0, HEAD_DIM, 1), q1 + pltpu.roll(q1, HEAD_DIM, 1))
                dgmat = dgmat + dm0 * l0 + dm1 * l1
                hin = st_ref[0, p]
                e = jnp.exp(jnp.where(lo, a0, a1))
                ch = _dot(cg, hin, "nt")
                dch = dyp * e
                dcg = dcg + _dot(dch, hin)
                dhin = _dot(dch, cg, "tn")
                dhout = dstate[p]
                al0, al1 = acs[L - 1:L, h0:h0 + 1], acs[L - 1:L, h1:h1 + 1]
                dec = jnp.exp(jnp.where(lo_rows, al0, al1))
                dhin = dhin + dec * dhout
                dal = dhout * hin * dec
                dal0 = _total(jnp.where(lo_rows, dal, 0.0))
                dal1 = _total(dal) - dal0
                dalast = dalast + jnp.where(li1 == h0, dal0, 0.0) + jnp.where(li1 == h1, dal1, 0.0)
                w = jnp.exp(jnp.where(lo, al0 - a0, al1 - a1))
                xw = xdt * w
                dxw = _dot(bg, dhout, "nt")
                dbg = dbg + _dot(xw, dhout)
                dxdt = dxdt + dxw * w
                dww = dxw * xw
                col_sums[1:2, lanes] = jnp.sum(dww, axis=0, keepdims=True)
                acs_terms[:, lanes] = row_terms + dch * ch - dww
                dx = dx + dxdt * dtl
                dt_terms[:, lanes] = dxdt * x
                dxs_buf[:, lanes] = dx
                dstate[p] = dhin
            dcg = dcg + _dot(dgmat, bg)
            dbg = dbg + _dot(dgmat, cg, "tn")
            dbm_buf[:, 128 * g:128 * g + 128] = dbg
            dcm_buf[:, 128 * g:128 * g + 128] = dcg

        head_sums = _split3_dot(col_sums[...], sel_ref[...])
        dskip_g = head_sums[0:1, :]
        dalast = dalast + head_sums[1:2, :]
        ddt = _split3_dot(dt_terms[...], sel_ref[...])
        dacs_tot = _split3_dot(acs_terms[...], sel_ref[...]) - qcs[...].T + jnp.where(ri == L - 1, dalast, 0.0)
        dstep = _rev_cumsum_rows(dacs_tot)
        ddt = ddt + dstep * a
        head_lane = li < N_HEADS
        ddt_pre = jnp.where(head_lane, ddt * _sigmoid(dt_pre), 0.0)
        dsmall_ref[...] = ddt_pre
        da = jnp.sum(jnp.where(head_lane, dstep * dt, 0.0), axis=0, keepdims=True)
        gsp = _stack_rows([jnp.sum(ddt_pre, axis=0, keepdims=True), da * a, dskip_g], L)

        def conv_back(dpost, ds, shifts, w_ref, carry, out_ref, width):
            dpre = dpost * ds
            dext = jnp.concatenate([dpre, carry[...]], axis=0)
            out_ref[...] = _conv_rows_transposed(dext, w_ref[...], SSD_CONV)[:L].astype(BF16)
            carry[...] = dpre[0:8]
            return _conv_weight_grad(dpre, shifts, slice(8, 8 + L), width)

        gwx = conv_back(dxs_buf[...], xs_ds, xs_sh, wx_ref, carry_x, dxs_ref, 1024)
        gwb = conv_back(dbm_buf[...], b_ds, b_sh, wb_ref, carry_b, db_ref, 256)
        gwc = conv_back(dcm_buf[...], c_ds, c_sh, wc_ref, carry_c, dc_ref, 256)

        @pl.when(start)
        def _():
            gwx_ref[...] = gwx
            gwb_ref[...] = gwb
            gwc_ref[...] = gwc
            gsp_ref[...] = gsp
            gnw_ref[...] = gnw

        @pl.when(step > 0)
        def _():
            gwx_ref[...] += gwx
            gwb_ref[...] += gwb
            gwc_ref[...] += gwc
            gsp_ref[...] += gsp
            gnw_ref[...] += gnw

    def ch(c):
        return nc - 1 - c

    row = pl.BlockSpec((L, 1024), lambda c: (ch(c), 0))
    row256 = pl.BlockSpec((L, 256), lambda c: (ch(c), 0))
    in_specs = _ssd_in_specs(rev_nc=nc) + [row, pl.BlockSpec((1, N_PAIRS, 128, 128), lambda c: (ch(c), 0, 0, 0)), row,
                                           pl.BlockSpec((1024, 128), lambda c: (0, 0))]
    out_specs = [row, row, row256, row256, pl.BlockSpec((L, 128), lambda c: (ch(c), 0)),
                 pl.BlockSpec((8, 1024), lambda c: (0, 0)), pl.BlockSpec((8, 256), lambda c: (0, 0)),
                 pl.BlockSpec((8, 256), lambda c: (0, 0)), pl.BlockSpec((8, 128), lambda c: (0, 0)),
                 pl.BlockSpec((1, 1024), lambda c: (0, 0))]
    out_shape = [jax.ShapeDtypeStruct((s, 1024), BF16), jax.ShapeDtypeStruct((s, 1024), BF16),
                 jax.ShapeDtypeStruct((s, 256), BF16), jax.ShapeDtypeStruct((s, 256), BF16),
                 jax.ShapeDtypeStruct((s, 128), F32),
                 jax.ShapeDtypeStruct((8, 1024), F32), jax.ShapeDtypeStruct((8, 256), F32),
                 jax.ShapeDtypeStruct((8, 256), F32), jax.ShapeDtypeStruct((8, 128), F32),
                 jax.ShapeDtypeStruct((1, 1024), F32)]
    scratch = [pltpu.VMEM((N_PAIRS, 128, 128), F32), pltpu.VMEM((8, 1024), F32), pltpu.VMEM((8, 256), F32),
               pltpu.VMEM((8, 256), F32), pltpu.VMEM((L, 1024), F32), pltpu.VMEM((L, 256), F32), pltpu.VMEM((L, 256), F32),
               pltpu.VMEM((L, L), F32), pltpu.VMEM((8, 1024), F32), pltpu.VMEM((L, 1024), F32), pltpu.VMEM((L, 1024), F32)]
    assert (len(in_specs), len(out_specs), len(scratch)) == (n_in, n_out, n_scratch)
    outs = pl.pallas_call(
        body, name="ssd_bwd", grid=(nc,), in_specs=in_specs + [ANY] * ns, out_specs=out_specs + [ANY] * ns,
        out_shape=out_shape + _pair_swap_out_shapes(swap), scratch_shapes=scratch + (_pair_swap_scratch(ns) if ns else []),
        compiler_params=_params(("arbitrary",)),
    )(proj, proj, proj, proj, proj, proj, proj, conv_w8, conv_w8, conv_w8, conv_b, conv_b, conv_b, proj, smallp, norm_w,
      ypre, states, dy, sel, *swap)
    return (*outs[:n_out], list(outs[n_out:]))


FOX_SCALE = HEAD_DIM ** -0.5
FOX_T = 256
Q_COL, K_COL, V_COL = 2, 3, 4


def _split_dot(v, m, terms):
    out, rest = None, v
    for i in range(terms):
        piece = rest.astype(BF16)
        out = _dot(piece, m) if out is None else out + _dot(piece, m)
        if i + 1 < terms:
            rest = rest - piece.astype(F32)
    return out


def _split3_dot(v, m):
    return _split_dot(v, m, 3)


def _head_mean(x, sel_ref, selt_ref):
    return _dot(x, sel_ref[...]) * (1.0 / HEAD_DIM)


def _head_spread(v, selt_ref):
    return _split_dot(v, selt_ref[...], 2)


def _head_rstd(x, sel_ref, selt_ref):
    return _head_spread(lax.rsqrt(_head_mean(x * x, sel_ref, selt_ref) + NORM_EPS), selt_ref)


def fox_tables():
    r = np.arange(3 * 128)
    piece, lane = r // 128, r % 128
    head = lane - F_LANE
    is_head = np.logical_and(head >= 0, head < N_HEADS)
    col = 128 * (head // 2) + HEAD_DIM * (1 - head % 2) + piece
    cols = np.arange(1024)
    place_q = np.logical_and(is_head[:, None], cols[None, :] == col[:, None])
    place_k = np.logical_and(is_head[:, None], cols[None, :] == (col + 3)[:, None])
    ones_q = np.logical_and(cols % HEAD_DIM >= 3, cols % HEAD_DIM < 6)[None]
    ones_k = (cols % HEAD_DIM < 3)[None]
    h = np.arange(128) - F_LANE
    ok = np.logical_and(h >= 0, h < N_HEADS)
    same_pair = cols[:, None] // 128 == (h // 2)[None, :]
    fold_even = np.logical_and(np.logical_and(ok, h % 2 == 0)[None, :], same_pair)
    fold_odd = np.logical_and(np.logical_and(ok, h % 2 == 1)[None, :], same_pair)
    as_bf16 = lambda t: jnp.asarray(t.astype(np.float32), BF16)
    return (as_bf16(place_q), as_bf16(place_k), jnp.asarray(ones_q, F32), jnp.asarray(ones_k, F32),
            as_bf16(fold_even), as_bf16(fold_odd))


def fox_prep(proj, smallp, qw, kw, sel, selt, place_q, place_k, ones_q, ones_k, *, tm=256):
    s = proj.shape[0]

    def body(q_ref, k_ref, v_ref, small_ref, sp_ref, qw_ref, kw_ref, sel_ref, selt_ref, pq_ref, pk_ref, oq_ref, ok_ref,
             qn_ref, kn_ref, aq_ref, ak_ref, vb_ref, knt_ref, akt_ref, vt_ref, carry):
        @pl.when(pl.program_id(0) == 0)
        def _():
            carry[...] = jnp.zeros_like(carry)

        q = q_ref[...]
        qn_ref[...] = (((q * _head_rstd(q, sel_ref, selt_ref)) * qw_ref[...]) * FOX_SCALE).astype(BF16)
        k = k_ref[...]
        kn = ((k * _head_rstd(k, sel_ref, selt_ref)) * kw_ref[...]).astype(BF16)
        kn_ref[...] = kn
        knt_ref[...] = kn.astype(F32).T.astype(BF16)
        vb_ref[...] = v_ref[...].astype(BF16)
        vt_ref[...] = v_ref[...].T.astype(BF16)
        li = _lane_iota((tm, 128))
        f_lane = jnp.logical_and(li >= F_LANE, li < F_LANE + N_HEADS)
        logf = jnp.where(f_lane, -_softplus(-(small_ref[...] + sp_ref[3:4, :])), 0.0)
        cum = _cumsum_rows(logf) + carry[...]
        carry[...] = cum[tm - 1:tm, :]
        hi = cum.astype(BF16)
        r1 = cum - hi.astype(F32)
        mid = r1.astype(BF16)
        lo = (r1 - mid.astype(F32)).astype(BF16)
        pieces = jnp.concatenate([hi, mid, lo], axis=1)
        aq_ref[...] = (_dot(pieces, pq_ref[...]) + oq_ref[...]).astype(BF16)
        ak = ok_ref[...] - _dot(pieces, pk_ref[...])
        ak_ref[...] = ak.astype(BF16)
        akt_ref[...] = ak.T.astype(BF16)

    row = pl.BlockSpec((tm, 1024), lambda i: (i, 0))
    col = pl.BlockSpec((1024, tm), lambda i: (0, i))
    vec = pl.BlockSpec((1, 1024), lambda i: (0, 0))
    table = pl.BlockSpec((384, 1024), lambda i: (0, 0))
    wide = jax.ShapeDtypeStruct((s, 1024), BF16)
    tall = jax.ShapeDtypeStruct((1024, s), BF16)
    return pl.pallas_call(
        body, name="fox_prep", grid=(s // tm,),
        in_specs=[pl.BlockSpec((tm, 1024), lambda i: (i, Q_COL)), pl.BlockSpec((tm, 1024), lambda i: (i, K_COL)),
                  pl.BlockSpec((tm, 1024), lambda i: (i, V_COL)),
                  pl.BlockSpec((tm, 128), lambda i: (i, SMALL_BLOCK)), pl.BlockSpec((8, 128), lambda i: (0, 0)), vec, vec,
                  pl.BlockSpec((1024, 128), lambda i: (0, 0)), pl.BlockSpec((128, 1024), lambda i: (0, 0)),
                  table, table, vec, vec],
        out_specs=[row, row, row, row, row, col, col, col],
        out_shape=[wide, wide, wide, wide, wide, tall, tall, tall],
        scratch_shapes=[pltpu.VMEM((1, 128), F32)], compiler_params=_params(("arbitrary",)),
    )(proj, proj, proj, proj, smallp, qw, kw, sel, selt, place_q, place_k, ones_q, ones_k)


def fox_fwd(qn, kn, aq, ak, vt, shards=()):
    s = qn.shape[0]
    t = FOX_T
    nq = s // t
    ng = len(shards)

    def body(*refs):
        q_ref, k_ref, aq_ref, ak_ref, vt_ref = refs[:5]
        o_ref, ot_ref, lse_ref = refs[5 + ng:8 + ng]
        p = pl.program_id(0)
        if ng:
            start, forward, finish = _gather_phases(refs[5:5 + ng], refs[8 + ng:8 + 2 * ng], *refs[8 + 2 * ng:])
            pl.when(p == 0)(start)
            pl.when(p == N_PAIRS // 2)(forward)

        @pl.when(p == 0)
        def _():
            lse_ref[...] = jnp.zeros_like(lse_ref)

        lo = _lane_iota((t, 128)) < HEAD_DIM
        lo_rows = _row_iota((128, t)) < HEAD_DIM
        causal_t = _lane_iota((t, t)) >= _row_iota((t, t))

        def q_loop(qi, _):
            q0 = pl.multiple_of(qi * t, t)
            qv, aqv = q_ref[pl.ds(q0, t), :], aq_ref[pl.ds(q0, t), :]
            qa, qb = jnp.where(lo, qv, aqv), jnp.where(lo, aqv, qv)

            def scores(kj):
                k0 = pl.multiple_of(kj * t, t)
                kv, akv = k_ref[pl.ds(k0, t), :], ak_ref[pl.ds(k0, t), :]
                return _dot(jnp.where(lo, kv, akv), qa, "nt"), _dot(jnp.where(lo, akv, kv), qb, "nt")

            def update(kj, stats, s0, s1):
                m0, l0, m1, l1, acc = stats
                vtv = vt_ref[:, pl.ds(pl.multiple_of(kj * t, t), t)]
                n0 = jnp.maximum(m0, jnp.max(s0, axis=0, keepdims=True))
                n1 = jnp.maximum(m1, jnp.max(s1, axis=0, keepdims=True))
                a0, a1 = jnp.exp(m0 - n0), jnp.exp(m1 - n1)
                p0, p1 = jnp.exp(s0 - n0), jnp.exp(s1 - n1)
                l0 = a0 * l0 + jnp.sum(p0, axis=0, keepdims=True)
                l1 = a1 * l1 + jnp.sum(p1, axis=0, keepdims=True)
                acc = (jnp.where(lo_rows, a0, a1) * acc + _dot(jnp.where(lo_rows, vtv, 0.0), p0)
                       + _dot(jnp.where(lo_rows, 0.0, vtv), p1))
                return n0, l0, n1, l1, acc

            def step(kj, carry):
                stats, (s0, s1) = carry[:5], carry[5:]
                nxt = scores(kj + 1)
                return (*update(kj, stats, s0, s1), *nxt)

            def row(val):
                return jnp.full((1, t), val, F32)

            init = (row(NEG_BIG), row(0.0), row(NEG_BIG), row(0.0), jnp.zeros((128, t), F32), *scores(0))
            carry = lax.fori_loop(0, qi, step, init)
            s0, s1 = jnp.where(causal_t, carry[5], NEG_BIG), jnp.where(causal_t, carry[6], NEG_BIG)
            m0, l0, m1, l1, acc = update(qi, carry[:5], s0, s1)
            out_t = acc / jnp.where(lo_rows, l0, l1)
            ot_ref[:, pl.ds(q0, t)] = out_t.astype(BF16)
            o_ref[pl.ds(q0, t), :] = out_t.T.astype(BF16)
            ri = _row_iota((N_HEADS, t))
            old = lse_ref[:, pl.ds(q0, t)]
            lse_ref[:, pl.ds(q0, t)] = jnp.where(
                ri == 2 * p, m0 + jnp.log(l0), jnp.where(ri == 2 * p + 1, m1 + jnp.log(l1), old))
            return 0

        lax.fori_loop(0, nq, q_loop, 0)
        if ng:
            pl.when(p == N_PAIRS - 1)(finish)

    pair = pl.BlockSpec((s, 128), lambda p: (0, p))
    outs = pl.pallas_call(
        body, name="fox_fwd", grid=(N_PAIRS,),
        in_specs=[pair] * 4 + [pl.BlockSpec((128, s), lambda p: (p, 0))] + [ANY] * ng,
        out_specs=[pair, pl.BlockSpec((128, s), lambda p: (p, 0)), pl.BlockSpec((N_HEADS, s), lambda p: (0, 0))] + [ANY] * ng,
        out_shape=[jax.ShapeDtypeStruct((s, 1024), BF16), jax.ShapeDtypeStruct((1024, s), BF16),
                   jax.ShapeDtypeStruct((N_HEADS, s), F32)] + _gather_out_shapes(shards),
        scratch_shapes=_gather_scratch(ng) if ng else [],
        compiler_params=_params(("arbitrary",)),
    )(qn, kn, aq, ak, vt, *shards)
    return outs[0], outs[1], outs[2], list(outs[3:])


def fox_bwd(qn, kn, aq, ak, knt, akt, vb, lse, dmixed, parts=()):
    s = qn.shape[0]
    t = FOX_T
    nq = s // t
    once = pl.Buffered(1)
    ns = len(parts)

    def body(*refs):
        q_ref, k_ref, aq_ref, ak_ref, kt_ref, akt_ref, v_ref, lse_ref, do_ref = refs[:9]
        dq_ref, dk_ref, dv_ref, dc0_ref, dc1_ref = refs[9 + ns:14 + ns]
        p_scr, dp_scr = refs[14 + 2 * ns:16 + 2 * ns]
        p = pl.program_id(0)
        if ns:
            start, finish = _scatter_phases(refs[9:9 + ns], refs[14 + ns:14 + 2 * ns], *refs[16 + 2 * ns:])
            pl.when(p == 0)(start)
        dk_ref[...] = jnp.zeros_like(dk_ref)
        dv_ref[...] = jnp.zeros_like(dv_ref)
        dc0_ref[...] = jnp.zeros_like(dc0_ref)
        dc1_ref[...] = jnp.zeros_like(dc1_ref)
        lo = _lane_iota((t, 128)) < HEAD_DIM
        lo_rows = _row_iota((128, t)) < HEAD_DIM
        causal_t = _lane_iota((t, t)) >= _row_iota((t, t))

        def q_loop(qi, _):
            q0 = pl.multiple_of(qi * t, t)
            qv, aqv = q_ref[pl.ds(q0, t), :], aq_ref[pl.ds(q0, t), :]
            qa, qb = jnp.where(lo, qv, aqv), jnp.where(lo, aqv, qv)
            do = do_ref[pl.ds(q0, t), :]
            doa, dob = jnp.where(lo, do, 0.0).astype(BF16), jnp.where(lo, 0.0, do).astype(BF16)
            lse_blk = lse_ref[:, pl.ds(q0, t)]
            ri = _row_iota((N_HEADS, t))
            lse0 = jnp.sum(jnp.where(ri == 2 * p, lse_blk, 0.0), axis=0, keepdims=True)
            lse1 = jnp.sum(jnp.where(ri == 2 * p + 1, lse_blk, 0.0), axis=0, keepdims=True)

            def scores(kj):
                k0 = pl.multiple_of(kj * t, t)
                kv, akv = k_ref[pl.ds(k0, t), :], ak_ref[pl.ds(k0, t), :]
                return _dot(jnp.where(lo, kv, akv), qa, "nt"), _dot(jnp.where(lo, akv, kv), qb, "nt")

            def pass1(kj, d0, d1, diagonal):
                k0 = pl.multiple_of(kj * t, t)
                vv = v_ref[pl.ds(k0, t), :]
                s0, s1 = scores(kj)
                if diagonal:
                    s0, s1 = jnp.where(causal_t, s0, NEG_BIG), jnp.where(causal_t, s1, NEG_BIG)
                p0, p1 = jnp.exp(s0 - lse0), jnp.exp(s1 - lse1)
                dp0, dp1 = _dot(vv, doa, "nt"), _dot(vv, dob, "nt")
                p_scr[0, kj], p_scr[1, kj] = p0, p1
                dp_scr[0, kj], dp_scr[1, kj] = dp0, dp1
                dv_ref[pl.ds(k0, t), :] += _dot(p0, doa) + _dot(p1, dob)
                return d0 + jnp.sum(p0 * dp0, axis=0, keepdims=True), d1 + jnp.sum(p1 * dp1, axis=0, keepdims=True)

            zero = jnp.zeros((1, t), F32)
            d0, d1 = lax.fori_loop(0, qi, lambda kj, c: pass1(kj, *c, False), (zero, zero))
            d0, d1 = pass1(qi, d0, d1, True)

            def fold_lanes(v):
                return functools.reduce(lambda a, b: a + b, [v[:, 128 * i:128 * (i + 1)] for i in range(t // 128)])

            def pass2(kj, carry):
                dq0, dq1 = carry
                k0 = pl.multiple_of(kj * t, t)
                p0, p1 = p_scr[0, kj], p_scr[1, kj]
                ds0, ds1 = p0 * (dp_scr[0, kj] - d0), p1 * (dp_scr[1, kj] - d1)
                dk_ref[pl.ds(k0, t), :] += jnp.where(lo, _dot(ds0, qa), _dot(ds1, qb))
                dc0_ref[pl.ds(k0, t), :] += fold_lanes(ds0)
                dc1_ref[pl.ds(k0, t), :] += fold_lanes(ds1)
                ktv, aktv = kt_ref[:, pl.ds(k0, t)], akt_ref[:, pl.ds(k0, t)]
                return dq0 + _dot(jnp.where(lo_rows, ktv, aktv), ds0), dq1 + _dot(jnp.where(lo_rows, aktv, ktv), ds1)

            zq = jnp.zeros((128, t), F32)
            dq0, dq1 = lax.fori_loop(0, qi + 1, pass2, (zq, zq))
            dq_ref[pl.ds(q0, t), :] = jnp.where(lo_rows, dq0, dq1).T
            return 0

        lax.fori_loop(0, nq, q_loop, 0)
        if ns:
            pl.when(p == N_PAIRS - 1)(finish)

    pair = pl.BlockSpec((s, 128), lambda p: (0, p))
    pair_t = pl.BlockSpec((128, s), lambda p: (p, 0))
    out = jax.ShapeDtypeStruct((s, 1024), F32)
    outs = pl.pallas_call(
        body, name="fox_bwd", grid=(N_PAIRS,),
        in_specs=[pair, pair, pair, pair, pair_t, pair_t, pair, pl.BlockSpec((N_HEADS, s), lambda p: (0, 0)),
                  pl.BlockSpec((s, 128), lambda p: (0, 8 + p))] + [ANY] * ns,
        out_specs=[pl.BlockSpec((s, 128), lambda p: (0, p), pipeline_mode=once)] * 5 + [ANY] * ns,
        out_shape=[out] * 5 + [jax.ShapeDtypeStruct(p.shape, p.dtype) for p in parts],
        scratch_shapes=[pltpu.VMEM((2, nq, t, t), F32), pltpu.VMEM((2, nq, t, t), F32)] + (_scatter_scratch(ns) if ns else []),
        compiler_params=_params(("arbitrary",)),
    )(qn, kn, aq, ak, knt, akt, vb, lse, dmixed, *parts)
    return (*outs[:5], _keep_own_blocks(outs[5:], parts))


def fox_post(dqn, dkn, dc0, dc1, proj, smallp, qw, kw, sel, selt, fold_even, fold_odd, *, tm=256):
    s = proj.shape[0]
    nrow = s // tm

    def body(dqn_ref, dkn_ref, dc0_ref, dc1_ref, q_ref, k_ref, small_ref, sp_ref, qw_ref, kw_ref, sel_ref, selt_ref,
             fe_ref, fo_ref, dq_ref, dk_ref, dsmall_ref, gqw_ref, gkw_ref, gfb_ref, carry):
        step = pl.program_id(0)

        @pl.when(step == 0)
        def _():
            carry[...] = jnp.zeros_like(carry)

        def norm_bwd(x_ref, w_ref, dn, out_ref):
            x = x_ref[...]
            rf = _head_rstd(x, sel_ref, selt_ref)
            xh = x * rf
            g = dn * w_ref[...]
            mean_gx = _head_spread(_head_mean(g * xh, sel_ref, selt_ref), selt_ref)
            out_ref[...] = (rf * (g - xh * mean_gx)).astype(BF16)
            return jnp.sum(dn * xh, axis=0, keepdims=True)

        gqw = norm_bwd(q_ref, qw_ref, dqn_ref[...] * FOX_SCALE, dq_ref)
        gkw = norm_bwd(k_ref, kw_ref, dkn_ref[...], dk_ref)
        li = _lane_iota((tm, 128))
        f_lane = jnp.logical_and(li >= F_LANE, li < F_LANE + N_HEADS)
        dcum = -(_split3_dot(dc0_ref[...], fe_ref[...]) + _split3_dot(dc1_ref[...], fo_ref[...]))
        dlogf = _rev_cumsum_rows(dcum) + carry[...]
        carry[...] = dlogf[0:1, :]
        dfr = jnp.where(f_lane, dlogf * _sigmoid(-(small_ref[...] + sp_ref[3:4, :])), 0.0)
        dsmall_ref[...] = dfr
        gfb = jnp.sum(dfr, axis=0, keepdims=True)

        @pl.when(step == 0)
        def _():
            gqw_ref[...] = gqw
            gkw_ref[...] = gkw
            gfb_ref[...] = gfb

        @pl.when(step > 0)
        def _():
            gqw_ref[...] += gqw
            gkw_ref[...] += gkw
            gfb_ref[...] += gfb

    def rb(i):
        return nrow - 1 - i

    row = pl.BlockSpec((tm, 1024), lambda i: (rb(i), 0))
    vec = pl.BlockSpec((1, 1024), lambda i: (0, 0))
    fold = pl.BlockSpec((1024, 128), lambda i: (0, 0))
    return pl.pallas_call(
        body, name="fox_post", grid=(nrow,),
        in_specs=[row, row, row, row, pl.BlockSpec((tm, 1024), lambda i: (rb(i), Q_COL)),
                  pl.BlockSpec((tm, 1024), lambda i: (rb(i), K_COL)),
                  pl.BlockSpec((tm, 128), lambda i: (rb(i), SMALL_BLOCK)), pl.BlockSpec((8, 128), lambda i: (0, 0)), vec, vec,
                  fold, pl.BlockSpec((128, 1024), lambda i: (0, 0)), fold, fold],
        out_specs=[row, row, pl.BlockSpec((tm, 128), lambda i: (rb(i), 0)), vec, vec, pl.BlockSpec((1, 128), lambda i: (0, 0))],
        out_shape=[jax.ShapeDtypeStruct((s, 1024), BF16), jax.ShapeDtypeStruct((s, 1024), BF16),
                   jax.ShapeDtypeStruct((s, 128), F32), jax.ShapeDtypeStruct((1, 1024), F32),
                   jax.ShapeDtypeStruct((1, 1024), F32), jax.ShapeDtypeStruct((1, 128), F32)],
        scratch_shapes=[pltpu.VMEM((1, 128), F32)], compiler_params=_params(("arbitrary",)),
    )(dqn, dkn, dc0, dc1, proj, proj, proj, smallp, qw, kw, sel, selt, fold_even, fold_odd)


def local_step(x, target, wx, later_shards, ssd_cw8, ssd_cb, smallp, ssd_nw, qw_t, kw_t, sel, selt,
               norm_mix_w, norm_ffn_w, ffn_cw8, ffn_cb):
    h, h_t = rms_fwd(x, norm_mix_w, name="rms_mix_fwd")
    proj = matmul(h, wx, mode="nn", tm=1024, tn=PROJ_TILE, tk=1024, out_dtype=F32, name="mm_in_proj")
    y_ssd, y_ssd_t, ypre, states = ssd_fwd(proj, ssd_cw8, ssd_cb, smallp, ssd_nw)
    place_q, place_k, ones_q, ones_k, fold_even, fold_odd = fox_tables()
    qn, kn, aq, ak, vb, knt, akt, vt = fox_prep(proj, smallp, qw_t, kw_t, sel, selt, place_q, place_k, ones_q, ones_k)
    y_fox, y_fox_t, lse, (a_out, a_up, a_down) = fox_fwd(qn, kn, aq, ak, vt, shards=later_shards)
    w_out = a_out.reshape(2048, D_MODEL)
    w_down = a_down.reshape(D_FF, D_MODEL)
    s = x.shape[0]
    shard = lambda index: pl.BlockSpec((None, 1024, 1408), index)
    x1 = matmul(y_ssd, w_out, mode="nn", tm=1024, tn=1024, tk=1024, out_dtype=F32, name="mm_out_ssd", add=x)
    x1 = matmul(y_fox, w_out, mode="nn", tm=1024, tn=1024, tk=1024, out_dtype=F32, name="mm_out_fox", add=x1, b_koff=1)
    hf, hf_t = rms_fwd(x1, norm_ffn_w, name="rms_ffn_fwd")
    hu, act, act_t = up_ffn_fwd(hf, a_up, ffn_cw8, ffn_cb)
    y = matmul(act, w_down, mode="nn", tm=1024, tn=1024, tk=1408, out_dtype=F32, name="mm_down", add=x1)
    dy, sq = loss_head(y, target)

    dact = matmul(dy, w_down, mode="nt", tm=1024, tn=1408, tk=1024, out_dtype=F32, name="mm_dact")
    g_down = matmul(act_t, dy, mode="nn", tm=1408, tn=1024, tk=1024, out_dtype=BF16, name="mm_dw_down")
    dhu, gcw_g, gcw_v = ffn_mid_bwd(hu, dact, ffn_cw8, ffn_cb)
    dhf = matmul(dhu, a_up, mode="nt", tm=1024, tn=1024, tk=1408, out_dtype=F32, name="mm_dhf",
                 layout=dict(m=s, n=D_MODEL, k=2 * D_FF, a_spec=shard(lambda i, j, kk: (kk // 2, i, kk % 2)),
                             b_spec=shard(lambda i, j, kk: (kk, 0, 0))))
    g_up = matmul(hf_t, dhu, mode="nn", tm=1024, tn=1408, tk=1024, out_dtype=BF16, name="mm_dw_up",
                  layout=dict(m=D_MODEL, n=2 * D_FF, k=s, b_spec=shard(lambda i, j, kk: (j // 2, kk, j % 2)),
                              o_spec=shard(lambda i, j, kk: (j, i, 0)), out_shape=(4, D_MODEL, 1408)))
    dx1, g_norm_ffn = rms_bwd(dhf, x1, norm_ffn_w, dy, name="rms_ffn_bwd")
    dmixed = matmul(dx1, w_out, mode="nt", tm=1024, tn=1024, tk=1024, out_dtype=F32, name="mm_dmixed")
    g_out_a = matmul(y_ssd_t, dx1, mode="nn", tm=1024, tn=1024, tk=1024, out_dtype=BF16, name="mm_dw_out_ssd")
    g_out_b = matmul(y_fox_t, dx1, mode="nn", tm=1024, tn=1024, tk=1024, out_dtype=BF16, name="mm_dw_out_fox")
    early = [jnp.concatenate([g_out_a, g_out_b], axis=0).reshape(4, 512, D_MODEL), g_up, g_down.reshape(4, 704, D_MODEL)]
    dz, dxs, db, dc, dsmall_ssd, gcw_x, gcw_b, gcw_c, g_sp, g_ssd_nw, theirs = ssd_bwd(
        proj, ssd_cw8, ssd_cb, smallp, ssd_nw, ypre, states, dmixed, sel, swap=early)
    core = lax.axis_index("c").astype(jnp.int32).reshape(1)
    parts = [add_pair(a, b, core, name="add_pair_" + n, tr=ADAM_ROWS[n]) for a, b, n in zip(early, theirs, BIG_NAMES[1:])]
    dqn, dkn, dv, dc0, dc1, landed_early = fox_bwd(qn, kn, aq, ak, knt, akt, vb, lse, dmixed, parts=parts)
    dq, dk, dsmall_fox, g_qw, g_kw, g_fb = fox_post(dqn, dkn, dc0, dc1, proj, smallp, qw_t, kw_t, sel, selt,
                                                    fold_even, fold_odd)
    dproj = jnp.concatenate([dz, dxs, dq, dk, dv.astype(BF16), db, dc, (dsmall_ssd + dsmall_fox).astype(BF16)], axis=1)
    g_wx = matmul(h_t, dproj, mode="nn", tm=1024, tn=PROJ_TILE, tk=1024, out_dtype=BF16, name="mm_dw_in")
    g_in = _in_grad_shards(g_wx)
    part_in = add_pair(g_in, pair_swap_halves([g_in], name="pair_swap_w_in")[0], core, name="add_pair_w_in",
                       tr=ADAM_ROWS["w_in"])
    dh, landed_in = matmul(dproj, wx, mode="nt", tm=1024, tn=1024, tk=PROJ_TILE, out_dtype=F32, name="mm_dh",
                           scatter=[part_in])
    grad_x, g_norm_mix = rms_bwd(dh, x, norm_mix_w, dx1, name="rms_mix_bwd")
    return dict(
        sq=sq, grad_x=grad_x, landed=landed_in + landed_early,
        g_norm_mix=g_norm_mix, g_norm_ffn=g_norm_ffn, g_ssd_nw=g_ssd_nw,
        g_ssd_cw=jnp.concatenate([gcw_x, gcw_b, gcw_c], axis=1), g_sp=g_sp, g_fb=g_fb, g_qw=g_qw, g_kw=g_kw,
        g_ffn_cw=jnp.concatenate([gcw_g, gcw_v], axis=1))


def adamw(w, g, m, v, *, name, tr, allreduce=None):
    rows, cols = w.shape
    nsteps = rows // tr

    def body(*refs):
        if allreduce is None:
            w_ref, g_ref, m_ref, v_ref, d_ref, mo_ref, vo_ref = refs
        else:
            w_ref, g_ref, m_ref, v_ref, packed_ref, d_ref, mo_ref, vo_ref, summed_ref = refs[:9]
            start, finish = _allreduce_phases(packed_ref, summed_ref, *refs[9:])
            pl.when(pl.program_id(0) == 0)(start)
        gv = g_ref[...]
        mn = ADAM_B1 * m_ref[...] + (1.0 - ADAM_B1) * gv
        vn = ADAM_B2 * v_ref[...] + (1.0 - ADAM_B2) * (gv * gv)
        m_hat = mn / (1.0 - ADAM_B1 ** ADAM_STEP)
        v_hat = vn / (1.0 - ADAM_B2 ** ADAM_STEP)
        d_ref[...] = -ADAM_LR * (m_hat / (jnp.sqrt(v_hat) + ADAM_EPS) + ADAM_WD * w_ref[...])
        mo_ref[...] = mn
        vo_ref[...] = vn
        if allreduce is not None:
            pl.when(pl.program_id(0) == nsteps - 1)(finish)

    blk = pl.BlockSpec((tr, cols), lambda i: (i, 0))
    shp = jax.ShapeDtypeStruct((rows, cols), F32)
    if allreduce is None:
        return pl.pallas_call(
            body, name=name, grid=(nsteps,), in_specs=[blk] * 4, out_specs=[blk] * 3, out_shape=[shp] * 3,
            compiler_params=_params(("parallel",)),
        )(w, g, m, v)
    whole = pl.BlockSpec(memory_space=pltpu.VMEM)
    return pl.pallas_call(
        body, name=name, grid=(nsteps,), in_specs=[blk] * 4 + [whole], out_specs=[blk] * 3 + [whole],
        out_shape=[shp] * 3 + [jax.ShapeDtypeStruct(allreduce.shape, F32)],
        scratch_shapes=_allreduce_scratch(allreduce.shape[0]), compiler_params=_params(("arbitrary",)),
    )(w, g, m, v, allreduce)


def add_pair(full, theirs, core, *, name, tr):
    _, rows, cols = theirs.shape
    nblk = rows // tr

    def body(c_ref, a_ref, b_ref, o_ref):
        o_ref[...] = (a_ref[...].astype(F32) + b_ref[...].astype(F32)).astype(BF16)

    blk = pl.BlockSpec((1, tr, cols), lambda j, i, c: (j, i, 0))
    grid_spec = pltpu.PrefetchScalarGridSpec(
        num_scalar_prefetch=1, grid=(4, nblk),
        in_specs=[pl.BlockSpec((1, tr, cols), lambda j, i, c: (j, c[0] * nblk + i, 0)), blk], out_specs=blk)
    return pl.pallas_call(
        body, name=name, grid_spec=grid_spec, out_shape=jax.ShapeDtypeStruct(theirs.shape, BF16),
        compiler_params=_params(("parallel", "parallel")),
    )(core, full, theirs)


def sum_chips(parts, core, *, name, tr):
    _, rows, cols = parts.shape
    nblk = rows // tr

    def body(c_ref, p_ref, o_ref):
        acc = p_ref[0].astype(F32)
        for k in range(1, 4):
            acc = acc + p_ref[k].astype(F32)
        o_ref[...] = acc

    grid_spec = pltpu.PrefetchScalarGridSpec(
        num_scalar_prefetch=1, grid=(nblk,), in_specs=[pl.BlockSpec((4, tr, cols), lambda i, c: (0, i, 0))],
        out_specs=pl.BlockSpec((tr, cols), lambda i, c: (c[0] * nblk + i, 0)))
    return pl.pallas_call(
        body, name=name, grid_spec=grid_spec, out_shape=jax.ShapeDtypeStruct((2 * rows, cols), F32),
        compiler_params=_params(("parallel",)),
    )(core, parts)


ANY = pl.BlockSpec(memory_space=pl.ANY)


def _place():
    x, y, c = lax.axis_index("x"), lax.axis_index("y"), lax.axis_index("c")
    chips = [(1 - x, y), (x, 1 - y), (1 - x, 1 - y)]
    return x, y, c, chips


def _chunks(rows):
    size = next((c for c in (128, 176, 64, 32, 16, 8) if rows % c == 0), rows)
    return [(r, size) for r in range(0, rows, size)]


def gather_weights(shards):
    n = len(shards)

    def body(*refs):
        start, forward, finish = _gather_phases(refs[:n], refs[n:2 * n], *refs[2 * n:])
        start()
        forward()
        finish()

    gathered = pl.pallas_call(
        body, name="gather_weights", in_specs=[ANY] * n, out_specs=[ANY] * n,
        out_shape=_gather_out_shapes(shards), scratch_shapes=_gather_scratch(n),
    )(*shards)
    return gathered


def _gather_out_shapes(shards):
    return [jax.ShapeDtypeStruct((4,) + s.shape, s.dtype) for s in shards]


def _gather_scratch(n):
    return [pltpu.SemaphoreType.DMA((n, 7)), pltpu.SemaphoreType.DMA((n, 7))]


def _gather_phases(ins, outs, send_sems, recv_sems):
    n = len(ins)
    x, y, c, chips = _place()
    me = 2 * x + y
    sibling = (x, y, 1 - c)
    blks = [2 * cx + cy for cx, cy in chips]

    def half(a, blk, r=0, nr=None):
        rows = ins[a].shape[0] // 2
        return outs[a].at[blk, pl.ds(c * rows + r, rows if nr is None else nr), :]

    def to_chip(a, t, r=0, nr=None):
        rows = ins[a].shape[0] // 2
        return pltpu.make_async_remote_copy(
            src_ref=ins[a].at[pl.ds(c * rows + r, rows if nr is None else nr), :], dst_ref=half(a, me, r, nr),
            send_sem=send_sems.at[a, t], recv_sem=recv_sems.at[a, t], device_id=(*chips[t], c), device_id_type=MESH)

    def from_chip(a, t):
        return pltpu.make_async_remote_copy(
            src_ref=half(a, blks[t]), dst_ref=half(a, blks[t]), send_sem=send_sems.at[a, t], recv_sem=recv_sems.at[a, t],
            device_id=(*chips[t], c), device_id_type=MESH)

    def to_sibling(a, t, r=0, nr=None):
        return pltpu.make_async_remote_copy(
            src_ref=half(a, blks[t], r, nr), dst_ref=half(a, blks[t], r, nr), send_sem=send_sems.at[a, 3 + t],
            recv_sem=recv_sems.at[a, 3 + t], device_id=sibling, device_id_type=MESH)

    def from_sibling(a, t):
        rows = ins[a].shape[0] // 2
        dst = outs[a].at[blks[t], pl.ds((1 - c) * rows, rows), :]
        return pltpu.make_async_remote_copy(
            src_ref=dst, dst_ref=dst, send_sem=send_sems.at[a, 3 + t], recv_sem=recv_sems.at[a, 3 + t],
            device_id=sibling, device_id_type=MESH)

    def own(a, r=0, nr=None):
        return pltpu.make_async_remote_copy(
            src_ref=ins[a].at[pl.ds(r, ins[a].shape[0] if nr is None else nr), :],
            dst_ref=outs[a].at[me, pl.ds(r, ins[a].shape[0] if nr is None else nr), :],
            send_sem=send_sems.at[a, 6], recv_sem=recv_sems.at[a, 6], device_id=sibling, device_id_type=MESH)

    def start():
        for a in range(n):
            for t in range(3):
                for r, nr in _chunks(ins[a].shape[0] // 2):
                    to_chip(a, t, r, nr).start()
            for r, nr in _chunks(ins[a].shape[0]):
                own(a, r, nr).start()

    def forward():
        for a in range(n):
            for t in range(3):
                from_chip(a, t).wait_recv()
                for r, nr in _chunks(ins[a].shape[0] // 2):
                    to_sibling(a, t, r, nr).start()

    def finish():
        for a in range(n):
            for t in range(3):
                from_sibling(a, t).wait_recv()
        for a in range(n):
            for t in range(3):
                to_chip(a, t).wait_send()
                to_sibling(a, t).wait_send()
            own(a).wait()

    return start, forward, finish


def pair_swap_halves(grads, *, name):
    n = len(grads)

    def body(*refs):
        start, finish = _pair_swap_phases(refs[:n], refs[n:2 * n], *refs[2 * n:])
        start()
        finish()

    return pl.pallas_call(
        body, name=name, in_specs=[ANY] * n, out_specs=[ANY] * n, out_shape=_pair_swap_out_shapes(grads),
        scratch_shapes=_pair_swap_scratch(n),
    )(*grads)


def _pair_swap_out_shapes(grads):
    return [jax.ShapeDtypeStruct((4, g.shape[1] // 2, g.shape[2]), g.dtype) for g in grads]


def _pair_swap_scratch(n):
    return [pltpu.SemaphoreType.DMA((n,)), pltpu.SemaphoreType.DMA((n,))]


def _pair_swap_phases(ins, theirs, send_sems, recv_sems):
    n = len(ins)
    x, y, c, _ = _place()
    sibling = (x, y, 1 - c)

    def start():
        for a in range(n):
            rows = ins[a].shape[1] // 2
            for j in range(4):
                for r, nr in _chunks(rows):
                    pltpu.make_async_remote_copy(
                        src_ref=ins[a].at[j, pl.ds((1 - c) * rows + r, nr), :], dst_ref=theirs[a].at[j, pl.ds(r, nr), :],
                        send_sem=send_sems.at[a], recv_sem=recv_sems.at[a], device_id=sibling, device_id_type=MESH).start()

    def finish():
        for a in range(n):
            pltpu.make_async_remote_copy(src_ref=theirs[a], dst_ref=theirs[a], send_sem=send_sems.at[a],
                                         recv_sem=recv_sems.at[a], device_id=sibling, device_id_type=MESH).wait()

    return start, finish


def _scatter_scratch(n):
    return [pltpu.SemaphoreType.DMA((n, 3)), pltpu.SemaphoreType.DMA((n, 3))]


def _keep_own_blocks(landed, parts):
    if not parts:
        return []
    chip = 2 * lax.axis_index("x") + lax.axis_index("y")
    return [lax.dynamic_update_slice(l, lax.dynamic_slice_in_dim(p, chip, 1, axis=0), (chip, 0, 0))
            for l, p in zip(landed, parts)]


def _scatter_phases(ins, outs, send_sems, recv_sems):
    n = len(ins)
    x, y, c, chips = _place()
    me = 2 * x + y
    blks = [2 * cx + cy for cx, cy in chips]

    def start():
        for a in range(n):
            for r, nr in _chunks(ins[a].shape[1]):
                for t in range(3):
                    pltpu.make_async_remote_copy(
                        src_ref=ins[a].at[blks[t], pl.ds(r, nr), :], dst_ref=outs[a].at[me, pl.ds(r, nr), :],
                        send_sem=send_sems.at[a, t], recv_sem=recv_sems.at[a, t],
                        device_id=(*chips[t], c), device_id_type=MESH).start()

    def finish():
        for a in range(n):
            for t in range(3):
                pltpu.make_async_remote_copy(
                    src_ref=outs[a].at[blks[t]], dst_ref=outs[a].at[blks[t]], send_sem=send_sems.at[a, t],
                    recv_sem=recv_sems.at[a, t], device_id=(*chips[t], c), device_id_type=MESH).wait()

    return start, finish


def pair_join_halves(bufs):
    n = len(bufs)

    def body(*refs):
        outs = refs[n:2 * n]
        send_sems, recv_sems = refs[2 * n:]
        x, y, c, _ = _place()
        sibling = (x, y, 1 - c)
        for a in range(n):
            rows = outs[a].shape[0] // 2
            for r, nr in _chunks(rows):
                mine = outs[a].at[pl.ds(c * rows + r, nr), :]
                pltpu.make_async_remote_copy(src_ref=mine, dst_ref=mine, send_sem=send_sems.at[a], recv_sem=recv_sems.at[a],
                                             device_id=sibling, device_id_type=MESH).start()
        for a in range(n):
            rows = outs[a].shape[0] // 2
            pltpu.make_async_remote_copy(
                src_ref=outs[a].at[pl.ds(c * rows, rows), :], dst_ref=outs[a].at[pl.ds((1 - c) * rows, rows), :],
                send_sem=send_sems.at[a], recv_sem=recv_sems.at[a], device_id=sibling, device_id_type=MESH).wait()

    return pl.pallas_call(
        body, name="pair_join_halves", in_specs=[ANY] * n, out_specs=[ANY] * n,
        out_shape=[jax.ShapeDtypeStruct(b.shape, b.dtype) for b in bufs], input_output_aliases={a: a for a in range(n)},
        scratch_shapes=[pltpu.SemaphoreType.DMA((n,)), pltpu.SemaphoreType.DMA((n,))],
    )(*bufs)


def _allreduce_scratch(rows):
    return [pltpu.VMEM((8, rows, 128), F32), pltpu.SemaphoreType.DMA((7,)), pltpu.SemaphoreType.DMA((7,))]


def _allreduce_phases(in_ref, out_ref, gathered, send_sems, recv_sems):
    x, y, c, _ = _place()
    me = 4 * x + 2 * y + c
    flips = [(fx, fy, fc) for fx in (0, 1) for fy in (0, 1) for fc in (0, 1)][1:]
    peers = [((1 - x) if fx else x, (1 - y) if fy else y, (1 - c) if fc else c) for fx, fy, fc in flips]

    def send(t):
        return pltpu.make_async_remote_copy(
            src_ref=in_ref, dst_ref=gathered.at[me], send_sem=send_sems.at[t], recv_sem=recv_sems.at[t],
            device_id=peers[t], device_id_type=MESH)

    def start():
        gathered[me] = in_ref[...]
        for t in range(7):
            send(t).start()

    def finish():
        for t, (px, py, pc) in enumerate(peers):
            slot = gathered.at[4 * px + 2 * py + pc]
            pltpu.make_async_remote_copy(
                src_ref=slot, dst_ref=slot, send_sem=send_sems.at[t], recv_sem=recv_sems.at[t],
                device_id=(px, py, pc), device_id_type=MESH).wait_recv()
        for t in range(7):
            send(t).wait_send()
        acc = gathered[0]
        for k in range(1, 8):
            acc = acc + gathered[k]
        out_ref[...] = acc

    return start, finish


SMALL_NAMES = ("norm_mix_w", "ssd_conv_w", "ssd_conv_b", "ssd_dt_bias", "ssd_a_log", "ssd_d", "ssd_norm_w", "fox_f_bias",
               "fox_q_norm_w", "fox_k_norm_w", "norm_ffn_w", "ffn_conv_w", "ffn_conv_b")
BIG_NAMES = ("w_in", "w_out", "w_up", "w_down")
WEIGHT_ORDER = ("norm_mix_w", "w_in", "ssd_conv_w", "ssd_conv_b", "ssd_dt_bias", "ssd_a_log", "ssd_d", "ssd_norm_w",
                "fox_f_bias", "fox_q_norm_w", "fox_k_norm_w", "w_out", "norm_ffn_w", "w_up", "ffn_conv_w", "ffn_conv_b", "w_down")
ADAM_ROWS = {"w_in": 256, "w_out": 256, "w_up": 256, "w_down": 176}


def _pack(arrays):
    pieces = []
    for a in arrays:
        flat = a.reshape(-1).astype(F32)
        pieces += [flat, jnp.zeros(((-flat.shape[0]) % 1024,), F32)]
    return jnp.concatenate(pieces).reshape(-1, 128)


def _unpack(packed, shapes):
    out, r = [], 0
    for shp in shapes:
        size = 1
        for d in shp:
            size *= d
        nrow = 8 * (-(-size // 1024))
        out.append(packed[r:r + nrow].reshape(-1)[:size].reshape(shp))
        r += nrow
    return out


IN_SHARD = IN_COLS // 4
IN_SEGMENTS = ((0, 2048, 0), (2048, 2560, 5120), (2560, 2576, MAIN_COLS), (2576, 5648, 2048), (5648, 5664, MAIN_COLS + F_LANE))


def _in_cols(shards, lo, hi):
    out = []
    for j in range(4):
        a, b = max(lo, IN_SHARD * j), min(hi, IN_SHARD * (j + 1))
        if a < b:
            out.append(shards[j][:, a - IN_SHARD * j:b - IN_SHARD * j])
    return out


def _in_grad_shards(g):
    shards = []
    for j in range(4):
        pieces = []
        for lo, hi, at in IN_SEGMENTS:
            a, b = max(lo, IN_SHARD * j), min(hi, IN_SHARD * (j + 1))
            if a < b:
                pieces.append(g[:, at + a - lo:at + b - lo])
        shards.append(jnp.concatenate(pieces, axis=1))
    return jnp.stack(shards)


def _pad_rows(a, rows):
    return jnp.pad(a, ((0, rows - a.shape[0]), (0, 0)))


def kernel(x, norm_mix_w, w_in, ssd_conv_w, ssd_conv_b, ssd_dt_bias, ssd_a_log, ssd_d, ssd_norm_w, fox_f_bias, fox_q_norm_w, fox_k_norm_w, w_out, norm_ffn_w, w_up, ffn_conv_w, ffn_conv_b, w_down, loss_target, m_norm_mix_w, m_w_in, m_ssd_conv_w, m_ssd_conv_b, m_ssd_dt_bias, m_ssd_a_log, m_ssd_d, m_ssd_norm_w, m_fox_f_bias, m_fox_q_norm_w, m_fox_k_norm_w, m_w_out, m_norm_ffn_w, m_w_up, m_ffn_conv_w, m_ffn_conv_b, m_w_down, v_norm_mix_w, v_w_in, v_ssd_conv_w, v_ssd_conv_b, v_ssd_dt_bias, v_ssd_a_log, v_ssd_d, v_ssd_norm_w, v_fox_f_bias, v_fox_q_norm_w, v_fox_k_norm_w, v_w_out, v_norm_ffn_w, v_w_up, v_ffn_conv_w, v_ffn_conv_b, v_w_down):
    w = dict(norm_mix_w=norm_mix_w, w_in=w_in, ssd_conv_w=ssd_conv_w, ssd_conv_b=ssd_conv_b, ssd_dt_bias=ssd_dt_bias,
             ssd_a_log=ssd_a_log, ssd_d=ssd_d, ssd_norm_w=ssd_norm_w, fox_f_bias=fox_f_bias, fox_q_norm_w=fox_q_norm_w,
             fox_k_norm_w=fox_k_norm_w, w_out=w_out, norm_ffn_w=norm_ffn_w, w_up=w_up, ffn_conv_w=ffn_conv_w,
             ffn_conv_b=ffn_conv_b, w_down=w_down)
    m = dict(norm_mix_w=m_norm_mix_w, w_in=m_w_in, ssd_conv_w=m_ssd_conv_w, ssd_conv_b=m_ssd_conv_b, ssd_dt_bias=m_ssd_dt_bias,
             ssd_a_log=m_ssd_a_log, ssd_d=m_ssd_d, ssd_norm_w=m_ssd_norm_w, fox_f_bias=m_fox_f_bias, fox_q_norm_w=m_fox_q_norm_w,
             fox_k_norm_w=m_fox_k_norm_w, w_out=m_w_out, norm_ffn_w=m_norm_ffn_w, w_up=m_w_up, ffn_conv_w=m_ffn_conv_w,
             ffn_conv_b=m_ffn_conv_b, w_down=m_w_down)
    v = dict(norm_mix_w=v_norm_mix_w, w_in=v_w_in, ssd_conv_w=v_ssd_conv_w, ssd_conv_b=v_ssd_conv_b, ssd_dt_bias=v_ssd_dt_bias,
             ssd_a_log=v_ssd_a_log, ssd_d=v_ssd_d, ssd_norm_w=v_ssd_norm_w, fox_f_bias=v_fox_f_bias, fox_q_norm_w=v_fox_q_norm_w,
             fox_k_norm_w=v_fox_k_norm_w, w_out=v_w_out, norm_ffn_w=v_norm_ffn_w, w_up=v_w_up, ffn_conv_w=v_ffn_conv_w,
             ffn_conv_b=v_ffn_conv_b, w_down=v_w_down)
    chip = 2 * lax.axis_index("x") + lax.axis_index("y")

    a_in, a_scw, a_fcw = gather_weights([w_in[0].astype(BF16), _pad_rows(ssd_conv_w[0], 16), _pad_rows(ffn_conv_w[0], 16)])
    later_shards = [w_out[0].astype(BF16), w_up[0].astype(BF16), w_down[0].astype(BF16)]
    wx = jnp.concatenate([p for lo, hi, _ in sorted(IN_SEGMENTS, key=lambda seg: seg[2]) for p in _in_cols(a_in, lo, hi)]
                         + [jnp.zeros((D_MODEL, PROJ_COLS - IN_COLS), BF16)], axis=1)
    ssd_cw8 = a_scw.transpose(1, 0, 2).reshape(16, 1536)[:8]
    ffn_cw8 = a_fcw.transpose(1, 0, 2).reshape(16, 2 * D_FF)[:8]
    gap = lambda n: jnp.zeros((n,), F32)
    smallp = jnp.concatenate([ssd_dt_bias[0], gap(112), ssd_a_log[0], gap(112), ssd_d[0], gap(112),
                              gap(F_LANE), fox_f_bias[0], gap(128 - F_LANE - N_HEADS), gap(4 * 128)]).reshape(8, 128)
    qw_t = jnp.tile(fox_q_norm_w[0], N_HEADS)[None]
    kw_t = jnp.tile(fox_k_norm_w[0], N_HEADS)[None]
    sel = jnp.asarray((np.arange(1024)[:, None] // HEAD_DIM == np.arange(128)[None, :]).astype(np.float32), BF16)

    res = local_step(x[0], loss_target[0], wx, later_shards, ssd_cw8, ssd_conv_b, smallp, ssd_norm_w, qw_t, kw_t,
                     sel, sel.T, norm_mix_w, norm_ffn_w, ffn_cw8, ffn_conv_b)

    full_shapes = [(1, 1024), (1, 4, 1536), (1, 1536), (1, 16), (1, 16), (1, 16), (1, 1024), (1, 16), (1, 64), (1, 64),
                   (1, 1024), (1, 3, 2 * D_FF), (1, 2 * D_FF), (1,)]
    local_small = [res["g_norm_mix"], res["g_ssd_cw"][:4], res["g_ssd_cw"][4], res["g_sp"][0, :16], res["g_sp"][1, :16],
                   res["g_sp"][2, :16], res["g_ssd_nw"], res["g_fb"][0, F_LANE:F_LANE + 16],
                   res["g_qw"].reshape(N_HEADS, HEAD_DIM).sum(0), res["g_kw"].reshape(N_HEADS, HEAD_DIM).sum(0),
                   res["g_norm_ffn"], res["g_ffn_cw"][:3], res["g_ffn_cw"][3], jnp.sum(res["sq"])]
    landed = res["landed"]
    core = lax.axis_index("c").astype(jnp.int32).reshape(1)
    halves = [sum_chips(p, core, name="sum_chips_" + n, tr=ADAM_ROWS[n]) for p, n in zip(landed, BIG_NAMES)]
    g_big = dict(zip(BIG_NAMES, pair_join_halves(halves)))

    grads, deltas, new_m, new_v = {}, {}, {}, {}
    for n in BIG_NAMES:
        out = adamw(w[n][0], g_big[n], m[n][0], v[n][0], name="adamw_" + n, tr=ADAM_ROWS[n],
                    allreduce=_pack(local_small) if n == BIG_NAMES[0] else None)
        if n == BIG_NAMES[0]:
            summed = _unpack(out[3], full_shapes)
        d, mn, vn = out[:3]
        grads[n], deltas[n], new_m[n], new_v[n] = g_big[n][None], d[None], mn[None], vn[None]
    loss = (0.5 / D_MODEL) * summed[-1][0]
    g_small = dict(zip(SMALL_NAMES, summed[:-1]))
    g_small["ssd_conv_w"] = lax.dynamic_slice(g_small["ssd_conv_w"], (0, 0, 384 * chip), (1, 4, 384))
    g_small["ffn_conv_w"] = lax.dynamic_slice(g_small["ffn_conv_w"], (0, 0, 1408 * chip), (1, 3, 1408))
    shapes = [w[n].shape for n in SMALL_NAMES]
    packed_w = _pack([w[n] for n in SMALL_NAMES])
    d, mn, vn = adamw(packed_w, _pack([g_small[n] for n in SMALL_NAMES]), _pack([m[n] for n in SMALL_NAMES]),
                      _pack([v[n] for n in SMALL_NAMES]), name="adamw_small", tr=packed_w.shape[0])
    for n, dd, mm, vv in zip(SMALL_NAMES, _unpack(d, shapes), _unpack(mn, shapes), _unpack(vn, shapes)):
        grads[n], deltas[n], new_m[n], new_v[n] = g_small[n].reshape(w[n].shape), dd, mm, vv
    return (loss, res["grad_x"][None], *[grads[n] for n in WEIGHT_ORDER], *[deltas[n] for n in WEIGHT_ORDER],
            *[new_m[n] for n in WEIGHT_ORDER], *[new_v[n] for n in WEIGHT_ORDER])
```

```python
import functools

import jax
import jax.numpy as jnp
import numpy as np
from jax import lax
from jax.experimental import pallas as pl
from jax.experimental.pallas import tpu as pltpu

F32 = jnp.float32
BF16 = jnp.bfloat16
MESH = pl.DeviceIdType.MESH

D_MODEL = 1024
HEAD_DIM = 64
N_HEADS = 16
N_PAIRS = N_HEADS // 2
SSD_CHUNK = 128
SSD_STATE = 128
SSD_CONV = 4
D_FF = 2816
FFN_CONV = 3
NORM_EPS = 1e-6
MAIN_COLS = 5632
SMALL_COLS = 128
PROJ_COLS = MAIN_COLS + SMALL_COLS
SMALL_BLOCK = MAIN_COLS // SMALL_COLS
PROJ_TILE = 1152
F_LANE = 16
IN_COLS = 5664

ADAM_LR = 0.001
ADAM_B1 = 0.9
ADAM_B2 = 0.999
ADAM_EPS = 1e-08
ADAM_WD = 0.01
ADAM_STEP = 10

VMEM_LIMIT_V7X = 56 * 1024 * 1024
NEG_BIG = -1e30


def _params(sem=None):
    return pltpu.CompilerParams(dimension_semantics=sem, vmem_limit_bytes=VMEM_LIMIT_V7X)


def _sigmoid(x):
    return 1.0 / (1.0 + jnp.exp(-x))


def _silu_and_grad(x):
    s = _sigmoid(x)
    return x * s, s * (1.0 + x * (1.0 - s))


def _shift_down(v, j):
    return v if j == 0 else pltpu.roll(v, j, 0)


def _shift_up(v, j):
    return v if j == 0 else pltpu.roll(v, v.shape[0] - j, 0)


def _row_iota(shape):
    return lax.broadcasted_iota(jnp.int32, shape, 0)


def _lane_iota(shape):
    return lax.broadcasted_iota(jnp.int32, shape, 1)


def _dot(a, b, mode="nn"):
    dims = {"nn": (((1,), (0,)), ((), ())), "nt": (((1,), (1,)), ((), ())), "tn": (((0,), (0,)), ((), ()))}[mode]
    return lax.dot_general(a.astype(BF16), b.astype(BF16), dims, preferred_element_type=F32)


def _dot_f32(a, b):
    return jnp.dot(a, b, precision=lax.Precision.HIGHEST, preferred_element_type=F32)


def matmul(a, b, *, mode, tm, tn, tk, out_dtype, name, add=None, b_koff=0, scatter=(), layout=None):
    layout = layout or {}
    if layout:
        m, n, k = layout["m"], layout["n"], layout["k"]
    else:
        (m, k), n = a.shape, (b.shape[1] if mode == "nn" else b.shape[0])
    assert m % tm == 0 and n % tn == 0 and k % tk == 0, (name, m, n, k, tm, tn, tk)
    nk = k // tk
    grid = (m // tm, n // tn, nk)
    a_spec = layout.get("a_spec") or pl.BlockSpec((tm, tk), lambda i, j, kk: (i, kk))
    b_spec = layout.get("b_spec") or (pl.BlockSpec((tn, tk), lambda i, j, kk: (j, kk + b_koff)) if mode == "nt"
                                      else pl.BlockSpec((tk, tn), lambda i, j, kk: (kk + b_koff, j)))
    o_spec = layout.get("o_spec") or pl.BlockSpec((tm, tn), lambda i, j, kk: (i, j))
    out_struct = jax.ShapeDtypeStruct(layout.get("out_shape", (m, n)), out_dtype)
    has_add = add is not None
    n_in = 3 if has_add else 2
    ns = len(scatter)

    def body(*refs):
        a_ref, b_ref = refs[:2]
        add_ref = refs[2] if has_add else None
        o_ref, acc_ref = refs[n_in + ns], refs[n_in + 2 * ns + 1]
        kk = pl.program_id(2)
        if ns:
            step = (pl.program_id(0) * grid[1] + pl.program_id(1)) * grid[2] + kk
            start, finish_copies = _scatter_phases(refs[n_in:n_in + ns], refs[n_in + ns + 1:n_in + 2 * ns + 1],
                                                   *refs[n_in + 2 * ns + 2:])
            pl.when(step == 0)(start)
        part = _dot(a_ref[...], b_ref[...], mode)

        def finish(total):
            if has_add:
                total = total + add_ref[...]
            o_ref[...] = total.astype(out_dtype)

        if nk == 1:
            finish(part)
        else:
            @pl.when(kk == 0)
            def _():
                acc_ref[...] = part

            @pl.when(jnp.logical_and(kk > 0, kk < nk - 1))
            def _():
                acc_ref[...] += part

            @pl.when(kk == nk - 1)
            def _():
                finish(acc_ref[...] + part)

        if ns:
            pl.when(step == grid[0] * grid[1] * grid[2] - 1)(finish_copies)

    in_specs = [a_spec, b_spec] + ([o_spec] if has_add else [])
    args = (a, b) + ((add,) if has_add else ())
    acc = pltpu.VMEM((tm, tn) if nk > 1 else (8, 128), F32)
    if not ns:
        return pl.pallas_call(
            body, name=name, grid=grid, in_specs=in_specs, out_specs=o_spec, out_shape=out_struct,
            scratch_shapes=[acc], compiler_params=_params(("parallel", "parallel", "arbitrary")),
        )(*args)
    outs = pl.pallas_call(
        body, name=name, grid=grid, in_specs=in_specs + [ANY] * ns, out_specs=[o_spec] + [ANY] * ns,
        out_shape=[out_struct] + [jax.ShapeDtypeStruct(p.shape, p.dtype) for p in scatter],
        scratch_shapes=[acc] + _scatter_scratch(ns), compiler_params=_params(("arbitrary", "arbitrary", "arbitrary")),
    )(*args, *scatter)
    return outs[0], _keep_own_blocks(outs[1:], scatter)


def rms_fwd(x, w, *, name, tm=1024):
    s, d = x.shape

    def body(x_ref, w_ref, h_ref, ht_ref):
        xv = x_ref[...]
        r = lax.rsqrt(jnp.mean(xv * xv, axis=-1, keepdims=True) + NORM_EPS)
        h = (xv * r) * w_ref[...]
        h_ref[...] = h.astype(BF16)
        ht_ref[...] = h.T.astype(BF16)

    return pl.pallas_call(
        body, name=name, grid=(s // tm,),
        in_specs=[pl.BlockSpec((tm, d), lambda i: (i, 0)), pl.BlockSpec((1, d), lambda i: (0, 0))],
        out_specs=[pl.BlockSpec((tm, d), lambda i: (i, 0)), pl.BlockSpec((d, tm), lambda i: (0, i))],
        out_shape=[jax.ShapeDtypeStruct((s, d), BF16), jax.ShapeDtypeStruct((d, s), BF16)],
        compiler_params=_params(("parallel",)),
    )(x, w)


def rms_bwd(dh, x, w, resid, *, name, tm=1024):
    s, d = x.shape

    def body(dh_ref, x_ref, w_ref, res_ref, dx_ref, dw_ref):
        xv = x_ref[...]
        dhv = dh_ref[...]
        r = lax.rsqrt(jnp.mean(xv * xv, axis=-1, keepdims=True) + NORM_EPS)
        xh = xv * r
        g = dhv * w_ref[...]
        dx_ref[...] = res_ref[...] + r * (g - xh * jnp.mean(g * xh, axis=-1, keepdims=True))
        part = jnp.sum(dhv * xh, axis=0, keepdims=True)

        @pl.when(pl.program_id(0) == 0)
        def _():
            dw_ref[...] = part

        @pl.when(pl.program_id(0) > 0)
        def _():
            dw_ref[...] += part

    row = pl.BlockSpec((tm, d), lambda i: (i, 0))
    vec = pl.BlockSpec((1, d), lambda i: (0, 0))
    return pl.pallas_call(
        body, name=name, grid=(s // tm,), in_specs=[row, row, vec, row], out_specs=[row, vec],
        out_shape=[jax.ShapeDtypeStruct((s, d), F32), jax.ShapeDtypeStruct((1, d), F32)],
        compiler_params=_params(("arbitrary",)),
    )(dh, x, w, resid)


def loss_head(y, target, *, tm=1024):
    s, d = y.shape

    def body(y_ref, t_ref, dy_ref, sq_ref):
        e = y_ref[...] - t_ref[...]
        dy_ref[...] = e / float(d)
        part = jnp.sum(e * e, axis=0, keepdims=True)

        @pl.when(pl.program_id(0) == 0)
        def _():
            sq_ref[...] = part

        @pl.when(pl.program_id(0) > 0)
        def _():
            sq_ref[...] += part

    row = pl.BlockSpec((tm, d), lambda i: (i, 0))
    vec = pl.BlockSpec((1, d), lambda i: (0, 0))
    return pl.pallas_call(
        body, name="loss_head", grid=(s // tm,), in_specs=[row, row], out_specs=[row, vec],
        out_shape=[jax.ShapeDtypeStruct((s, d), F32), jax.ShapeDtypeStruct((1, d), F32)],
        compiler_params=_params(("arbitrary",)),
    )(y, target)


def _row_shifts(ext, k_taps):
    return [_shift_down(ext, j) for j in range(k_taps)]


def _conv_rows(shifts, w):
    k_taps = len(shifts)
    acc = w[k_taps - 1:k_taps, :] * shifts[0]
    for k in range(k_taps - 1):
        acc = acc + w[k:k + 1, :] * shifts[k_taps - 1 - k]
    return acc


def _conv_weight_grad(dcur, shifts, rows, width):
    k_taps = len(shifts)
    out = [jnp.sum(dcur * shifts[k_taps - 1 - k][rows], axis=0, keepdims=True) for k in range(k_taps)]
    out.append(jnp.sum(dcur, axis=0, keepdims=True))
    return _stack_rows(out, width)


def _conv_rows_transposed(dext, w, k_taps):
    acc = w[k_taps - 1:k_taps, :] * dext
    for k in range(k_taps - 1):
        acc = acc + w[k:k + 1, :] * _shift_up(dext, k_taps - 1 - k)
    return acc


def _stack_rows(rows, width):
    ri = _row_iota((8, width))
    out = jnp.zeros((8, width), F32)
    for k, r in enumerate(rows):
        out = out + jnp.where(ri == k, r, 0.0)
    return out


UP_SHARD = 1408
UP_ROWS = 256


def up_ffn_fwd(hf, a_up, conv_w8, conv_b, *, tm=512):
    s = hf.shape[0]

    def body(a_ref, bg_ref, bv_ref, wg_ref, wv_ref, cbg_ref, cbv_ref, hu_ref, act_ref, actt_ref, carry):
        i, j = pl.program_id(0), pl.program_id(1)
        prev_g = jnp.where(i == 0, 0.0, carry[0, j])
        prev_v = jnp.where(i == 0, 0.0, carry[1, j])
        for r in range(0, tm, UP_ROWS):
            rows = slice(r, r + UP_ROWS)
            a = a_ref[rows, :]
            hg, hv = _dot(a, bg_ref[...]), _dot(a, bv_ref[...])
            hu_ref[0, rows, :] = hg
            hu_ref[1, rows, :] = hv
            gc = _conv_rows(_row_shifts(jnp.concatenate([prev_g, hg], axis=0), FFN_CONV), wg_ref[...])[8:] + cbg_ref[...]
            vc = _conv_rows(_row_shifts(jnp.concatenate([prev_v, hv], axis=0), FFN_CONV), wv_ref[...])[8:] + cbv_ref[...]
            act = gc * _sigmoid(gc) * vc
            act_ref[rows, :] = act.astype(BF16)
            actt_ref[:, rows] = act.T.astype(BF16)
            prev_g, prev_v = hg[UP_ROWS - 8:], hv[UP_ROWS - 8:]
        carry[0, j] = prev_g
        carry[1, j] = prev_v

    shard = lambda off: pl.BlockSpec((None, D_MODEL, UP_SHARD), lambda i, j: (j + off, 0, 0))
    taps = lambda off: pl.BlockSpec((8, UP_SHARD), lambda i, j: (0, j + off))
    bias = lambda off: pl.BlockSpec((1, UP_SHARD), lambda i, j: (0, j + off))
    return pl.pallas_call(
        body, name="up_ffn_fwd", grid=(s // tm, 2),
        in_specs=[pl.BlockSpec((tm, D_MODEL), lambda i, j: (i, 0)), shard(0), shard(2), taps(0), taps(2), bias(0), bias(2)],
        out_specs=[pl.BlockSpec((2, tm, UP_SHARD), lambda i, j: (0, i, j)), pl.BlockSpec((tm, UP_SHARD), lambda i, j: (i, j)),
                   pl.BlockSpec((UP_SHARD, tm), lambda i, j: (j, i))],
        out_shape=[jax.ShapeDtypeStruct((2, s, D_FF), F32), jax.ShapeDtypeStruct((s, D_FF), BF16),
                   jax.ShapeDtypeStruct((D_FF, s), BF16)],
        scratch_shapes=[pltpu.VMEM((2, 2, 8, UP_SHARD), F32)], compiler_params=_params(("arbitrary", "arbitrary")),
    )(hf, a_up, a_up, conv_w8, conv_w8, conv_b, conv_b)


def down_ffn_bwd(dy, w_down, hu, conv_w8, conv_b, *, tm=512):
    s = dy.shape[0]
    nrow = s // tm
    r8 = tm // 8
    sub = UP_ROWS

    def body(dy_ref, w_ref, g_ref, v_ref, gp_ref, vp_ref, wg_ref, wv_ref, bg_ref, bv_ref, dhu_ref, wgo_ref, wvo_ref, carry):
        i = pl.program_id(1)
        start = i == 0
        top = i == nrow - 1
        g_ext = jnp.concatenate([jnp.where(top, 0.0, gp_ref[...]), g_ref[...]], axis=0)
        v_ext = jnp.concatenate([jnp.where(top, 0.0, vp_ref[...]), v_ref[...]], axis=0)
        next_g = jnp.where(start, 0.0, carry[0])
        next_v = jnp.where(start, 0.0, carry[1])
        pg = jnp.zeros((8, UP_SHARD), F32)
        pv = jnp.zeros((8, UP_SHARD), F32)
        own = slice(8, 8 + sub)
        for r in reversed(range(0, tm, sub)):
            da = _dot(dy_ref[r:r + sub, :], w_ref[...], "nt")
            g_sh = _row_shifts(g_ext[r:r + sub + 8], FFN_CONV)
            v_sh = _row_shifts(v_ext[r:r + sub + 8], FFN_CONV)
            gc = _conv_rows(g_sh, wg_ref[...])[own] + bg_ref[...]
            vc = _conv_rows(v_sh, wv_ref[...])[own] + bv_ref[...]
            silu, dsilu = _silu_and_grad(gc)
            dgc = da * vc * dsilu
            dvc = da * silu
            dhu_ref[0, r:r + sub, :] = _conv_rows_transposed(
                jnp.concatenate([dgc, next_g], axis=0), wg_ref[...], FFN_CONV)[:sub].astype(BF16)
            dhu_ref[1, r:r + sub, :] = _conv_rows_transposed(
                jnp.concatenate([dvc, next_v], axis=0), wv_ref[...], FFN_CONV)[:sub].astype(BF16)
            pg = pg + _conv_weight_grad(dgc, g_sh, own, UP_SHARD)
            pv = pv + _conv_weight_grad(dvc, v_sh, own, UP_SHARD)
            next_g, next_v = dgc[:8], dvc[:8]
        carry[0] = next_g
        carry[1] = next_v

        @pl.when(start)
        def _():
            wgo_ref[...] = pg
            wvo_ref[...] = pv

        @pl.when(i > 0)
        def _():
            wgo_ref[...] += pg
            wvo_ref[...] += pv

    def tile(i):
        return nrow - 1 - i

    def above(i):
        return jnp.maximum(tile(i) * r8 - 1, 0)

    half = lambda k, rows, row_index: pl.BlockSpec((None, rows, UP_SHARD), lambda j, i: (k, row_index(i), j))
    in_specs = [
        pl.BlockSpec((tm, D_MODEL), lambda j, i: (tile(i), 0)),
        pl.BlockSpec((UP_SHARD, D_MODEL), lambda j, i: (j, 0)),
        half(0, tm, tile), half(1, tm, tile), half(0, 8, above), half(1, 8, above),
        pl.BlockSpec((8, UP_SHARD), lambda j, i: (0, j)),
        pl.BlockSpec((8, UP_SHARD), lambda j, i: (0, j + 2)),
        pl.BlockSpec((1, UP_SHARD), lambda j, i: (0, j)),
        pl.BlockSpec((1, UP_SHARD), lambda j, i: (0, j + 2)),
    ]
    grads = pl.BlockSpec((8, UP_SHARD), lambda j, i: (0, j))
    out_specs = [pl.BlockSpec((2, tm, UP_SHARD), lambda j, i: (0, tile(i), j)), grads, grads]
    out_shape = [jax.ShapeDtypeStruct((2, s, D_FF), BF16),
                 jax.ShapeDtypeStruct((8, D_FF), F32), jax.ShapeDtypeStruct((8, D_FF), F32)]
    return pl.pallas_call(
        body, name="down_ffn_bwd", grid=(2, nrow), in_specs=in_specs, out_specs=out_specs, out_shape=out_shape,
        scratch_shapes=[pltpu.VMEM((2, 8, UP_SHARD), F32)], compiler_params=_params(("arbitrary", "arbitrary")),
    )(dy, w_down, hu, hu, hu, hu, conv_w8, conv_w8, conv_b, conv_b)


def _softplus(x):
    return jnp.maximum(x, 0.0) + jnp.log(1.0 + jnp.exp(-jnp.abs(x)))


def _cumsum_rows(v):
    n = v.shape[0]
    ri = _row_iota(v.shape)
    sh = 1
    while sh < n:
        v = v + jnp.where(ri >= sh, _shift_down(v, sh), 0.0)
        sh *= 2
    return v


def _rev_cumsum_rows(v):
    n = v.shape[0]
    ri = _row_iota(v.shape)
    sh = 1
    while sh < n:
        v = v + jnp.where(ri < n - sh, _shift_up(v, sh), 0.0)
        sh *= 2
    return v


def _total(v):
    return jnp.sum(jnp.sum(v, axis=1, keepdims=True), axis=0, keepdims=True)


def _ssd_in_specs(rev_nc=None):
    def ch(c):
        return c if rev_nc is None else rev_nc - 1 - c

    def prev(c):
        return jnp.maximum(ch(c) * (SSD_CHUNK // 8) - 1, 0)

    L = SSD_CHUNK
    return [
        pl.BlockSpec((L, 1024), lambda c: (ch(c), 0)),
        pl.BlockSpec((L, 1024), lambda c: (ch(c), 1)),
        pl.BlockSpec((L, 256), lambda c: (ch(c), 20)),
        pl.BlockSpec((L, 256), lambda c: (ch(c), 21)),
        pl.BlockSpec((8, 1024), lambda c: (prev(c), 1)),
        pl.BlockSpec((8, 256), lambda c: (prev(c), 20)),
        pl.BlockSpec((8, 256), lambda c: (prev(c), 21)),
        pl.BlockSpec((8, 1024), lambda c: (0, 0)),
        pl.BlockSpec((8, 256), lambda c: (0, 4)),
        pl.BlockSpec((8, 256), lambda c: (0, 5)),
        pl.BlockSpec((1, 1024), lambda c: (0, 0)),
        pl.BlockSpec((1, 256), lambda c: (0, 4)),
        pl.BlockSpec((1, 256), lambda c: (0, 5)),
        pl.BlockSpec((L, SMALL_COLS), lambda c: (ch(c), SMALL_BLOCK)),
        pl.BlockSpec((8, 128), lambda c: (0, 0)),
        pl.BlockSpec((1, 1024), lambda c: (0, 0)),
    ]


def _ssd_conv_pre(cur_ref, prev_ref, w_ref, b_ref, first):
    prev = jnp.where(first, 0.0, prev_ref[...])
    shifts = _row_shifts(jnp.concatenate([prev, cur_ref[...]], axis=0), SSD_CONV)
    return shifts, _conv_rows(shifts, w_ref[...])[8:] + b_ref[...]


def _ssd_time_consts(small_ref, sp_ref):
    dt_pre = small_ref[...] + sp_ref[0:1, :]
    dt = _softplus(dt_pre)
    a = -jnp.exp(sp_ref[1:2, :])
    acs = _cumsum_rows(dt * a)
    return dt_pre, dt, a, acs


def ssd_fwd(proj, conv_w8, conv_b, smallp, norm_w):
    s = proj.shape[0]
    nc = s // SSD_CHUNK
    L = SSD_CHUNK

    def body(z_ref, xs_ref, b_ref, c_ref, xsp_ref, bp_ref, cp_ref, wx_ref, wb_ref, wc_ref, bx_ref, bb_ref, bc_ref,
             small_ref, sp_ref, nw_ref, y_ref, yt_ref, ypre_ref, st_ref, state):
        first = pl.program_id(0) == 0

        @pl.when(first)
        def _():
            state[...] = jnp.zeros_like(state)

        xs = _ssd_conv_pre(xs_ref, xsp_ref, wx_ref, bx_ref, first)[1]
        xs = xs * _sigmoid(xs)
        bm = _ssd_conv_pre(b_ref, bp_ref, wb_ref, bb_ref, first)[1]
        bm = bm * _sigmoid(bm)
        cm = _ssd_conv_pre(c_ref, cp_ref, wc_ref, bc_ref, first)[1]
        cm = cm * _sigmoid(cm)
        _, dt, _, acs = _ssd_time_consts(small_ref, sp_ref)
        acs_t = acs.T
        li = _lane_iota((L, L))
        ri = _row_iota((L, L))
        tri = ri >= li
        lo = li < HEAD_DIM
        st_ref[0] = state[...]
        for g in range(2):
            bg = bm[:, 128 * g:128 * g + 128]
            cg = cm[:, 128 * g:128 * g + 128]
            gmat = _dot(cg, bg, "nt")
            for pp in range(4):
                p = 4 * g + pp
                h0, h1 = 2 * p, 2 * p + 1
                x = xs[:, 128 * p:128 * p + 128]
                a0, a1 = acs[:, h0:h0 + 1], acs[:, h1:h1 + 1]
                xdt = x * jnp.where(lo, dt[:, h0:h0 + 1], dt[:, h1:h1 + 1])
                m0 = gmat * jnp.exp(jnp.where(tri, a0 - acs_t[h0:h0 + 1, :], NEG_BIG))
                m1 = gmat * jnp.exp(jnp.where(tri, a1 - acs_t[h1:h1 + 1, :], NEG_BIG))
                yd = _dot(m0, jnp.where(lo, xdt, 0.0)) + _dot(m1, jnp.where(lo, 0.0, xdt))
                hin = state[p]
                yo = _dot(cg, hin, "nt") * jnp.exp(jnp.where(lo, a0, a1))
                dskip = jnp.where(lo[0:1], sp_ref[2:3, h0:h0 + 1], sp_ref[2:3, h1:h1 + 1])
                ypre_ref[:, 128 * p:128 * p + 128] = yd + yo + dskip * x
                al0, al1 = acs[L - 1:L, h0:h0 + 1], acs[L - 1:L, h1:h1 + 1]
                w = jnp.exp(jnp.where(lo, al0 - a0, al1 - a1))
                dec = jnp.exp(jnp.where(ri < HEAD_DIM, al0, al1))
                state[p] = dec * hin + _dot(xdt * w, bg, "tn")
        z = z_ref[...]
        yg = ypre_ref[...] * (z * _sigmoid(z))
        for g in range(2):
            seg = yg[:, 512 * g:512 * g + 512]
            r = lax.rsqrt(jnp.mean(seg * seg, axis=-1, keepdims=True) + NORM_EPS)
            out = (seg * r) * nw_ref[:, 512 * g:512 * g + 512]
            y_ref[:, 512 * g:512 * g + 512] = out.astype(BF16)
            yt_ref[512 * g:512 * g + 512, :] = out.T.astype(BF16)

    row = pl.BlockSpec((L, 1024), lambda c: (c, 0))
    return pl.pallas_call(
        body, name="ssd_fwd", grid=(nc,), in_specs=_ssd_in_specs(),
        out_specs=[row, pl.BlockSpec((1024, L), lambda c: (0, c)), row,
                   pl.BlockSpec((1, N_PAIRS, 128, 128), lambda c: (c, 0, 0, 0))],
        out_shape=[jax.ShapeDtypeStruct((s, 1024), BF16), jax.ShapeDtypeStruct((1024, s), BF16),
                   jax.ShapeDtypeStruct((s, 1024), F32), jax.ShapeDtypeStruct((nc, N_PAIRS, 128, 128), F32)],
        scratch_shapes=[pltpu.VMEM((N_PAIRS, 128, 128), F32)],
        compiler_params=_params(("arbitrary",)),
    )(proj, proj, proj, proj, proj, proj, proj, conv_w8, conv_w8, conv_w8, conv_b, conv_b, conv_b, proj, smallp, norm_w)


def ssd_bwd(proj, conv_w8, conv_b, smallp, norm_w, ypre, states, dy, sel, swap=()):
    s = proj.shape[0]
    nc = s // SSD_CHUNK
    L = SSD_CHUNK

    ns = len(swap)
    n_in, n_out, n_scratch = 20, 10, 11

    def body(*refs):
        own = refs[:n_in] + refs[n_in + ns:n_in + ns + n_out] + refs[n_in + 2 * ns + n_out:n_in + 2 * ns + n_out + n_scratch]
        if ns:
            start, finish = _pair_swap_phases(refs[n_in:n_in + ns], refs[n_in + ns + n_out:n_in + 2 * ns + n_out],
                                              *refs[n_in + 2 * ns + n_out + n_scratch:])
            pl.when(pl.program_id(0) == 0)(start)
        compute(*own)
        if ns:
            pl.when(pl.program_id(0) == nc - 1)(finish)

    def compute(z_ref, xs_ref, b_ref, c_ref, xsp_ref, bp_ref, cp_ref, wx_ref, wb_ref, wc_ref, bx_ref, bb_ref, bc_ref,
                small_ref, sp_ref, nw_ref, ypre_ref, st_ref, dy_ref, sel_ref,
                dz_ref, dxs_ref, db_ref, dc_ref, dsmall_ref, gwx_ref, gwb_ref, gwc_ref, gsp_ref, gnw_ref,
                dstate, carry_x, carry_b, carry_c, dxs_buf, dbm_buf, dcm_buf, qcs, col_sums, acs_terms, dt_terms):
        step = pl.program_id(0)
        col_sums[...] = jnp.zeros_like(col_sums)
        first_chunk = step == nc - 1
        start = step == 0

        @pl.when(start)
        def _():
            dstate[...] = jnp.zeros_like(dstate)
            carry_x[...] = jnp.zeros_like(carry_x)
            carry_b[...] = jnp.zeros_like(carry_b)
            carry_c[...] = jnp.zeros_like(carry_c)

        xs_sh, xs_pre = _ssd_conv_pre(xs_ref, xsp_ref, wx_ref, bx_ref, first_chunk)
        b_sh, b_pre = _ssd_conv_pre(b_ref, bp_ref, wb_ref, bb_ref, first_chunk)
        c_sh, c_pre = _ssd_conv_pre(c_ref, cp_ref, wc_ref, bc_ref, first_chunk)
        xs, xs_ds = _silu_and_grad(xs_pre)
        bm, b_ds = _silu_and_grad(b_pre)
        cm, c_ds = _silu_and_grad(c_pre)
        dt_pre, dt, a, acs = _ssd_time_consts(small_ref, sp_ref)
        acs_t = acs.T
        li = _lane_iota((L, L))
        ri = _row_iota((L, L))
        tri = ri >= li
        lo = li < HEAD_DIM
        lo_rows = ri < HEAD_DIM
        li1 = _lane_iota((1, L))

        z = z_ref[...]
        sz, dsz = _silu_and_grad(z)
        y = ypre_ref[...]
        yg = y * sz
        dout = dy_ref[...]
        dyg_parts = []
        gnw_parts = []
        for g in range(2):
            sl = slice(512 * g, 512 * g + 512)
            seg = yg[:, sl]
            r = lax.rsqrt(jnp.mean(seg * seg, axis=-1, keepdims=True) + NORM_EPS)
            n = seg * r
            gnw_parts.append(jnp.sum(dout[:, sl] * n, axis=0, keepdims=True))
            gg = dout[:, sl] * nw_ref[:, sl]
            dyg_parts.append(r * (gg - n * jnp.mean(gg * n, axis=-1, keepdims=True)))
        dyg = jnp.concatenate(dyg_parts, axis=1)
        gnw = jnp.concatenate(gnw_parts, axis=1)
        dz_ref[...] = (dyg * y * dsz).astype(BF16)
        dypre = dyg * sz

        qcs[...] = jnp.zeros_like(qcs)
        dalast = jnp.zeros((1, L), F32)
        for g in range(2):
            bg = bm[:, 128 * g:128 * g + 128]
            cg = cm[:, 128 * g:128 * g + 128]
            gmat = _dot(cg, bg, "nt")
            dgmat = jnp.zeros((L, L), F32)
            dbg = jnp.zeros((L, L), F32)
            dcg = jnp.zeros((L, L), F32)
            for pp in range(4):
                p = 4 * g + pp
                h0, h1 = 2 * p, 2 * p + 1
                lanes = slice(128 * p, 128 * p + 128)
                x = xs[:, lanes]
                dyp = dypre[:, lanes]
                a0, a1 = acs[:, h0:h0 + 1], acs[:, h1:h1 + 1]
                dtl = jnp.where(lo, dt[:, h0:h0 + 1], dt[:, h1:h1 + 1])
                xdt = x * dtl
                l0 = jnp.exp(jnp.where(tri, a0 - acs_t[h0:h0 + 1, :], NEG_BIG))
                l1 = jnp.exp(jnp.where(tri, a1 - acs_t[h1:h1 + 1, :], NEG_BIG))
                m0, m1 = gmat * l0, gmat * l1
                dskip = jnp.where(lo[0:1], sp_ref[2:3, h0:h0 + 1], sp_ref[2:3, h1:h1 + 1])
                col_sums[0:1, lanes] = jnp.sum(dyp * x, axis=0, keepdims=True)
                dx = dyp * dskip
                dy0, dy1 = jnp.where(lo, dyp, 0.0), jnp.where(lo, 0.0, dyp)
                x0, x1 = jnp.where(lo, xdt, 0.0), jnp.where(lo, 0.0, xdt)
                dm0, dm1 = _dot(dy0, x0, "nt"), _dot(dy1, x1, "nt")
                dxdt = _dot(m0, dy0, "tn") + _dot(m1, dy1, "tn")
                q0, q1 = dm0 * m0, dm1 * m1
                qcs[h0:h0 + 1, :] = jnp.sum(q0, axis=0, keepdims=True)
                qcs[h1:h1 + 1, :] = jnp.sum(q1, axis=0, keepdims=True)
                row_terms = jnp.where(lo, q0 + pltpu.roll(q0, HEAD_DIM, 1), q1 + pltpu.roll(q1, HEAD_DIM, 1))
                dgmat = dgmat + dm0 * l0 + dm1 * l1
                hin = st_ref[0, p]
                e = jnp.exp(jnp.where(lo, a0, a1))
                ch = _dot(cg, hin, "nt")
                dch = dyp * e
                dcg = dcg + _dot(dch, hin)
                dhin = _dot(dch, cg, "tn")
                dhout = dstate[p]
                al0, al1 = acs[L - 1:L, h0:h0 + 1], acs[L - 1:L, h1:h1 + 1]
                dec = jnp.exp(jnp.where(lo_rows, al0, al1))
                dhin = dhin + dec * dhout
                dal = dhout * hin * dec
                dal0 = _total(jnp.where(lo_rows, dal, 0.0))
                dal1 = _total(dal) - dal0
                dalast = dalast + jnp.where(li1 == h0, dal0, 0.0) + jnp.where(li1 == h1, dal1, 0.0)
                w = jnp.exp(jnp.where(lo, al0 - a0, al1 - a1))
                xw = xdt * w
                dxw = _dot(bg, dhout, "nt")
                dbg = dbg + _dot(xw, dhout)
                dxdt = dxdt + dxw * w
                dww = dxw * xw
                col_sums[1:2, lanes] = jnp.sum(dww, axis=0, keepdims=True)
                acs_terms[:, lanes] = row_terms + dch * ch - dww
                dx = dx + dxdt * dtl
                dt_terms[:, lanes] = dxdt * x
                dxs_buf[:, lanes] = dx
                dstate[p] = dhin
            dcg = dcg + _dot(dgmat, bg)
            dbg = dbg + _dot(dgmat, cg, "tn")
            dbm_buf[:, 128 * g:128 * g + 128] = dbg
            dcm_buf[:, 128 * g:128 * g + 128] = dcg

        head_sums = _split3_dot(col_sums[...], sel_ref[...])
        dskip_g = head_sums[0:1, :]
        dalast = dalast + head_sums[1:2, :]
        ddt = _split3_dot(dt_terms[...], sel_ref[...])
        dacs_tot = _split3_dot(acs_terms[...], sel_ref[...]) - qcs[...].T + jnp.where(ri == L - 1, dalast, 0.0)
        dstep = _rev_cumsum_rows(dacs_tot)
        ddt = ddt + dstep * a
        head_lane = li < N_HEADS
        ddt_pre = jnp.where(head_lane, ddt * _sigmoid(dt_pre), 0.0)
        dsmall_ref[...] = ddt_pre
        da = jnp.sum(jnp.where(head_lane, dstep * dt, 0.0), axis=0, keepdims=True)
        gsp = _stack_rows([jnp.sum(ddt_pre, axis=0, keepdims=True), da * a, dskip_g], L)

        def conv_back(dpost, ds, shifts, w_ref, carry, out_ref, width):
            dpre = dpost * ds
            dext = jnp.concatenate([dpre, carry[...]], axis=0)
            out_ref[...] = _conv_rows_transposed(dext, w_ref[...], SSD_CONV)[:L].astype(BF16)
            carry[...] = dpre[0:8]
            return _conv_weight_grad(dpre, shifts, slice(8, 8 + L), width)

        gwx = conv_back(dxs_buf[...], xs_ds, xs_sh, wx_ref, carry_x, dxs_ref, 1024)
        gwb = conv_back(dbm_buf[...], b_ds, b_sh, wb_ref, carry_b, db_ref, 256)
        gwc = conv_back(dcm_buf[...], c_ds, c_sh, wc_ref, carry_c, dc_ref, 256)

        @pl.when(start)
        def _():
            gwx_ref[...] = gwx
            gwb_ref[...] = gwb
            gwc_ref[...] = gwc
            gsp_ref[...] = gsp
            gnw_ref[...] = gnw

        @pl.when(step > 0)
        def _():
            gwx_ref[...] += gwx
            gwb_ref[...] += gwb
            gwc_ref[...] += gwc
            gsp_ref[...] += gsp
            gnw_ref[...] += gnw

    def ch(c):
        return nc - 1 - c

    row = pl.BlockSpec((L, 1024), lambda c: (ch(c), 0))
    row256 = pl.BlockSpec((L, 256), lambda c: (ch(c), 0))
    in_specs = _ssd_in_specs(rev_nc=nc) + [row, pl.BlockSpec((1, N_PAIRS, 128, 128), lambda c: (ch(c), 0, 0, 0)), row,
                                           pl.BlockSpec((1024, 128), lambda c: (0, 0))]
    out_specs = [row, row, row256, row256, pl.BlockSpec((L, 128), lambda c: (ch(c), 0)),
                 pl.BlockSpec((8, 1024), lambda c: (0, 0)), pl.BlockSpec((8, 256), lambda c: (0, 0)),
                 pl.BlockSpec((8, 256), lambda c: (0, 0)), pl.BlockSpec((8, 128), lambda c: (0, 0)),
                 pl.BlockSpec((1, 1024), lambda c: (0, 0))]
    out_shape = [jax.ShapeDtypeStruct((s, 1024), BF16), jax.ShapeDtypeStruct((s, 1024), BF16),
                 jax.ShapeDtypeStruct((s, 256), BF16), jax.ShapeDtypeStruct((s, 256), BF16),
                 jax.ShapeDtypeStruct((s, 128), F32),
                 jax.ShapeDtypeStruct((8, 1024), F32), jax.ShapeDtypeStruct((8, 256), F32),
                 jax.ShapeDtypeStruct((8, 256), F32), jax.ShapeDtypeStruct((8, 128), F32),
                 jax.ShapeDtypeStruct((1, 1024), F32)]
    scratch = [pltpu.VMEM((N_PAIRS, 128, 128), F32), pltpu.VMEM((8, 1024), F32), pltpu.VMEM((8, 256), F32),
               pltpu.VMEM((8, 256), F32), pltpu.VMEM((L, 1024), F32), pltpu.VMEM((L, 256), F32), pltpu.VMEM((L, 256), F32),
               pltpu.VMEM((L, L), F32), pltpu.VMEM((8, 1024), F32), pltpu.VMEM((L, 1024), F32), pltpu.VMEM((L, 1024), F32)]
    assert (len(in_specs), len(out_specs), len(scratch)) == (n_in, n_out, n_scratch)
    outs = pl.pallas_call(
        body, name="ssd_bwd", grid=(nc,), in_specs=in_specs + [ANY] * ns, out_specs=out_specs + [ANY] * ns,
        out_shape=out_shape + _pair_swap_out_shapes(swap), scratch_shapes=scratch + (_pair_swap_scratch(ns) if ns else []),
        compiler_params=_params(("arbitrary",)),
    )(proj, proj, proj, proj, proj, proj, proj, conv_w8, conv_w8, conv_w8, conv_b, conv_b, conv_b, proj, smallp, norm_w,
      ypre, states, dy, sel, *swap)
    return (*outs[:n_out], list(outs[n_out:]))


FOX_SCALE = HEAD_DIM ** -0.5
FOX_T = 256
Q_COL, K_COL, V_COL = 2, 3, 4


def _split_dot(v, m, terms):
    out, rest = None, v
    for i in range(terms):
        piece = rest.astype(BF16)
        out = _dot(piece, m) if out is None else out + _dot(piece, m)
        if i + 1 < terms:
            rest = rest - piece.astype(F32)
    return out


def _split3_dot(v, m):
    return _split_dot(v, m, 3)


def _head_mean(x, sel_ref, selt_ref):
    return _dot(x, sel_ref[...]) * (1.0 / HEAD_DIM)


def _head_spread(v, selt_ref):
    return _split_dot(v, selt_ref[...], 2)


def _head_rstd(x, sel_ref, selt_ref):
    return _head_spread(lax.rsqrt(_head_mean(x * x, sel_ref, selt_ref) + NORM_EPS), selt_ref)


def fox_tables():
    r = np.arange(3 * 128)
    piece, lane = r // 128, r % 128
    head = lane - F_LANE
    is_head = np.logical_and(head >= 0, head < N_HEADS)
    col = 128 * (head // 2) + HEAD_DIM * (1 - head % 2) + piece
    cols = np.arange(1024)
    place_q = np.logical_and(is_head[:, None], cols[None, :] == col[:, None])
    place_k = np.logical_and(is_head[:, None], cols[None, :] == (col + 3)[:, None])
    ones_q = np.logical_and(cols % HEAD_DIM >= 3, cols % HEAD_DIM < 6)[None]
    ones_k = (cols % HEAD_DIM < 3)[None]
    h = np.arange(128) - F_LANE
    ok = np.logical_and(h >= 0, h < N_HEADS)
    same_pair = cols[:, None] // 128 == (h // 2)[None, :]
    fold_even = np.logical_and(np.logical_and(ok, h % 2 == 0)[None, :], same_pair)
    fold_odd = np.logical_and(np.logical_and(ok, h % 2 == 1)[None, :], same_pair)
    as_bf16 = lambda t: jnp.asarray(t.astype(np.float32), BF16)
    return (as_bf16(place_q), as_bf16(place_k), jnp.asarray(ones_q, F32), jnp.asarray(ones_k, F32),
            as_bf16(fold_even), as_bf16(fold_odd))


def fox_prep(proj, smallp, qw, kw, sel, selt, place_q, place_k, ones_q, ones_k, *, tm=256):
    s = proj.shape[0]

    def body(q_ref, k_ref, v_ref, small_ref, sp_ref, qw_ref, kw_ref, sel_ref, selt_ref, pq_ref, pk_ref, oq_ref, ok_ref,
             qn_ref, kn_ref, aq_ref, ak_ref, vb_ref, knt_ref, akt_ref, vt_ref, carry):
        @pl.when(pl.program_id(0) == 0)
        def _():
            carry[...] = jnp.zeros_like(carry)

        q = q_ref[...]
        qn_ref[...] = (((q * _head_rstd(q, sel_ref, selt_ref)) * qw_ref[...]) * FOX_SCALE).astype(BF16)
        k = k_ref[...]
        kn = ((k * _head_rstd(k, sel_ref, selt_ref)) * kw_ref[...]).astype(BF16)
        kn_ref[...] = kn
        knt_ref[...] = kn.astype(F32).T.astype(BF16)
        vb_ref[...] = v_ref[...].astype(BF16)
        vt_ref[...] = v_ref[...].T.astype(BF16)
        li = _lane_iota((tm, 128))
        f_lane = jnp.logical_and(li >= F_LANE, li < F_LANE + N_HEADS)
        logf = jnp.where(f_lane, -_softplus(-(small_ref[...] + sp_ref[3:4, :])), 0.0)
        cum = _cumsum_rows(logf) + carry[...]
        carry[...] = cum[tm - 1:tm, :]
        hi = cum.astype(BF16)
        r1 = cum - hi.astype(F32)
        mid = r1.astype(BF16)
        lo = (r1 - mid.astype(F32)).astype(BF16)
        pieces = jnp.concatenate([hi, mid, lo], axis=1)
        aq_ref[...] = (_dot(pieces, pq_ref[...]) + oq_ref[...]).astype(BF16)
        ak = ok_ref[...] - _dot(pieces, pk_ref[...])
        ak_ref[...] = ak.astype(BF16)
        akt_ref[...] = ak.T.astype(BF16)

    row = pl.BlockSpec((tm, 1024), lambda i: (i, 0))
    col = pl.BlockSpec((1024, tm), lambda i: (0, i))
    vec = pl.BlockSpec((1, 1024), lambda i: (0, 0))
    table = pl.BlockSpec((384, 1024), lambda i: (0, 0))
    wide = jax.ShapeDtypeStruct((s, 1024), BF16)
    tall = jax.ShapeDtypeStruct((1024, s), BF16)
    return pl.pallas_call(
        body, name="fox_prep", grid=(s // tm,),
        in_specs=[pl.BlockSpec((tm, 1024), lambda i: (i, Q_COL)), pl.BlockSpec((tm, 1024), lambda i: (i, K_COL)),
                  pl.BlockSpec((tm, 1024), lambda i: (i, V_COL)),
                  pl.BlockSpec((tm, 128), lambda i: (i, SMALL_BLOCK)), pl.BlockSpec((8, 128), lambda i: (0, 0)), vec, vec,
                  pl.BlockSpec((1024, 128), lambda i: (0, 0)), pl.BlockSpec((128, 1024), lambda i: (0, 0)),
                  table, table, vec, vec],
        out_specs=[row, row, row, row, row, col, col, col],
        out_shape=[wide, wide, wide, wide, wide, tall, tall, tall],
        scratch_shapes=[pltpu.VMEM((1, 128), F32)], compiler_params=_params(("arbitrary",)),
    )(proj, proj, proj, proj, smallp, qw, kw, sel, selt, place_q, place_k, ones_q, ones_k)


def fox_fwd(qn, kn, aq, ak, vt, shards=()):
    s = qn.shape[0]
    t = FOX_T
    nq = s // t
    ng = len(shards)

    def body(*refs):
        q_ref, k_ref, aq_ref, ak_ref, vt_ref = refs[:5]
        o_ref, ot_ref, lse_ref = refs[5 + ng:8 + ng]
        p = pl.program_id(0)
        if ng:
            start, forward, finish = _gather_phases(refs[5:5 + ng], refs[8 + ng:8 + 2 * ng], *refs[8 + 2 * ng:])
            pl.when(p == 0)(start)
            pl.when(p == N_PAIRS // 2)(forward)

        @pl.when(p == 0)
        def _():
            lse_ref[...] = jnp.zeros_like(lse_ref)

        lo = _lane_iota((t, 128)) < HEAD_DIM
        lo_rows = _row_iota((128, t)) < HEAD_DIM
        causal_t = _lane_iota((t, t)) >= _row_iota((t, t))

        def q_loop(qi, _):
            q0 = pl.multiple_of(qi * t, t)
            qv, aqv = q_ref[pl.ds(q0, t), :], aq_ref[pl.ds(q0, t), :]
            qa, qb = jnp.where(lo, qv, aqv), jnp.where(lo, aqv, qv)

            def scores(kj):
                k0 = pl.multiple_of(kj * t, t)
                kv, akv = k_ref[pl.ds(k0, t), :], ak_ref[pl.ds(k0, t), :]
                return _dot(jnp.where(lo, kv, akv), qa, "nt"), _dot(jnp.where(lo, akv, kv), qb, "nt")

            def update(kj, stats, s0, s1):
                m0, l0, m1, l1, acc = stats
                vtv = vt_ref[:, pl.ds(pl.multiple_of(kj * t, t), t)]
                n0 = jnp.maximum(m0, jnp.max(s0, axis=0, keepdims=True))
                n1 = jnp.maximum(m1, jnp.max(s1, axis=0, keepdims=True))
                a0, a1 = jnp.exp(m0 - n0), jnp.exp(m1 - n1)
                p0, p1 = jnp.exp(s0 - n0), jnp.exp(s1 - n1)
                l0 = a0 * l0 + jnp.sum(p0, axis=0, keepdims=True)
                l1 = a1 * l1 + jnp.sum(p1, axis=0, keepdims=True)
                acc = (jnp.where(lo_rows, a0, a1) * acc + _dot(jnp.where(lo_rows, vtv, 0.0), p0)
                       + _dot(jnp.where(lo_rows, 0.0, vtv), p1))
                return n0, l0, n1, l1, acc

            def step(kj, carry):
                stats, (s0, s1) = carry[:5], carry[5:]
                nxt = scores(kj + 1)
                return (*update(kj, stats, s0, s1), *nxt)

            def row(val):
                return jnp.full((1, t), val, F32)

            init = (row(NEG_BIG), row(0.0), row(NEG_BIG), row(0.0), jnp.zeros((128, t), F32), *scores(0))
            carry = lax.fori_loop(0, qi, step, init)
            s0, s1 = jnp.where(causal_t, carry[5], NEG_BIG), jnp.where(causal_t, carry[6], NEG_BIG)
            m0, l0, m1, l1, acc = update(qi, carry[:5], s0, s1)
            out_t = acc / jnp.where(lo_rows, l0, l1)
            ot_ref[:, pl.ds(q0, t)] = out_t.astype(BF16)
            o_ref[pl.ds(q0, t), :] = out_t.T.astype(BF16)
            ri = _row_iota((N_HEADS, t))
            old = lse_ref[:, pl.ds(q0, t)]
            lse_ref[:, pl.ds(q0, t)] = jnp.where(
                ri == 2 * p, m0 + jnp.log(l0), jnp.where(ri == 2 * p + 1, m1 + jnp.log(l1), old))
            return 0

        lax.fori_loop(0, nq, q_loop, 0)
        if ng:
            pl.when(p == N_PAIRS - 1)(finish)

    pair = pl.BlockSpec((s, 128), lambda p: (0, p))
    outs = pl.pallas_call(
        body, name="fox_fwd", grid=(N_PAIRS,),
        in_specs=[pair] * 4 + [pl.BlockSpec((128, s), lambda p: (p, 0))] + [ANY] * ng,
        out_specs=[pair, pl.BlockSpec((128, s), lambda p: (p, 0)), pl.BlockSpec((N_HEADS, s), lambda p: (0, 0))] + [ANY] * ng,
        out_shape=[jax.ShapeDtypeStruct((s, 1024), BF16), jax.ShapeDtypeStruct((1024, s), BF16),
                   jax.ShapeDtypeStruct((N_HEADS, s), F32)] + _gather_out_shapes(shards),
        scratch_shapes=_gather_scratch(ng) if ng else [],
        compiler_params=_params(("arbitrary",)),
    )(qn, kn, aq, ak, vt, *shards)
    return outs[0], outs[1], outs[2], list(outs[3:])


def fox_bwd(qn, kn, aq, ak, knt, akt, vb, lse, dmixed, parts=()):
    s = qn.shape[0]
    t = FOX_T
    nq = s // t
    once = pl.Buffered(1)
    ns = len(parts)

    def body(*refs):
        q_ref, k_ref, aq_ref, ak_ref, kt_ref, akt_ref, v_ref, lse_ref, do_ref = refs[:9]
        dq_ref, dk_ref, dv_ref, dc0_ref, dc1_ref = refs[9 + ns:14 + ns]
        p_scr, dp_scr = refs[14 + 2 * ns:16 + 2 * ns]
        p = pl.program_id(0)
        if ns:
            start, finish = _scatter_phases(refs[9:9 + ns], refs[14 + ns:14 + 2 * ns], *refs[16 + 2 * ns:])
            pl.when(p == 0)(start)
        dk_ref[...] = jnp.zeros_like(dk_ref)
        dv_ref[...] = jnp.zeros_like(dv_ref)
        dc0_ref[...] = jnp.zeros_like(dc0_ref)
        dc1_ref[...] = jnp.zeros_like(dc1_ref)
        lo = _lane_iota((t, 128)) < HEAD_DIM
        lo_rows = _row_iota((128, t)) < HEAD_DIM
        causal_t = _lane_iota((t, t)) >= _row_iota((t, t))

        def q_loop(qi, _):
            q0 = pl.multiple_of(qi * t, t)
            qv, aqv = q_ref[pl.ds(q0, t), :], aq_ref[pl.ds(q0, t), :]
            qa, qb = jnp.where(lo, qv, aqv), jnp.where(lo, aqv, qv)
            do = do_ref[pl.ds(q0, t), :]
            doa, dob = jnp.where(lo, do, 0.0).astype(BF16), jnp.where(lo, 0.0, do).astype(BF16)
            lse_blk = lse_ref[:, pl.ds(q0, t)]
            ri = _row_iota((N_HEADS, t))
            lse0 = jnp.sum(jnp.where(ri == 2 * p, lse_blk, 0.0), axis=0, keepdims=True)
            lse1 = jnp.sum(jnp.where(ri == 2 * p + 1, lse_blk, 0.0), axis=0, keepdims=True)

            def scores(kj):
                k0 = pl.multiple_of(kj * t, t)
                kv, akv = k_ref[pl.ds(k0, t), :], ak_ref[pl.ds(k0, t), :]
                return _dot(jnp.where(lo, kv, akv), qa, "nt"), _dot(jnp.where(lo, akv, kv), qb, "nt")

            def pass1(kj, d0, d1, diagonal):
                k0 = pl.multiple_of(kj * t, t)
                vv = v_ref[pl.ds(k0, t), :]
                s0, s1 = scores(kj)
                if diagonal:
                    s0, s1 = jnp.where(causal_t, s0, NEG_BIG), jnp.where(causal_t, s1, NEG_BIG)
                p0, p1 = jnp.exp(s0 - lse0), jnp.exp(s1 - lse1)
                dp0, dp1 = _dot(vv, doa, "nt"), _dot(vv, dob, "nt")
                p_scr[0, kj], p_scr[1, kj] = p0, p1
                dp_scr[0, kj], dp_scr[1, kj] = dp0, dp1
                dv_ref[pl.ds(k0, t), :] += _dot(p0, doa) + _dot(p1, dob)
                return d0 + jnp.sum(p0 * dp0, axis=0, keepdims=True), d1 + jnp.sum(p1 * dp1, axis=0, keepdims=True)

            zero = jnp.zeros((1, t), F32)
            d0, d1 = lax.fori_loop(0, qi, lambda kj, c: pass1(kj, *c, False), (zero, zero))
            d0, d1 = pass1(qi, d0, d1, True)

            def fold_lanes(v):
                return functools.reduce(lambda a, b: a + b, [v[:, 128 * i:128 * (i + 1)] for i in range(t // 128)])

            def pass2(kj, carry):
                dq0, dq1 = carry
                k0 = pl.multiple_of(kj * t, t)
                p0, p1 = p_scr[0, kj], p_scr[1, kj]
                ds0, ds1 = p0 * (dp_scr[0, kj] - d0), p1 * (dp_scr[1, kj] - d1)
                dk_ref[pl.ds(k0, t), :] += jnp.where(lo, _dot(ds0, qa), _dot(ds1, qb))
                dc0_ref[pl.ds(k0, t), :] += fold_lanes(ds0)
                dc1_ref[pl.ds(k0, t), :] += fold_lanes(ds1)
                ktv, aktv = kt_ref[:, pl.ds(k0, t)], akt_ref[:, pl.ds(k0, t)]
                return dq0 + _dot(jnp.where(lo_rows, ktv, aktv), ds0), dq1 + _dot(jnp.where(lo_rows, aktv, ktv), ds1)

            zq = jnp.zeros((128, t), F32)
            dq0, dq1 = lax.fori_loop(0, qi + 1, pass2, (zq, zq))
            dq_ref[pl.ds(q0, t), :] = jnp.where(lo_rows, dq0, dq1).T
            return 0

        lax.fori_loop(0, nq, q_loop, 0)
        if ns:
            pl.when(p == N_PAIRS - 1)(finish)

    pair = pl.BlockSpec((s, 128), lambda p: (0, p))
    pair_t = pl.BlockSpec((128, s), lambda p: (p, 0))
    out = jax.ShapeDtypeStruct((s, 1024), F32)
    outs = pl.pallas_call(
        body, name="fox_bwd", grid=(N_PAIRS,),
        in_specs=[pair, pair, pair, pair, pair_t, pair_t, pair, pl.BlockSpec((N_HEADS, s), lambda p: (0, 0)),
                  pl.BlockSpec((s, 128), lambda p: (0, 8 + p))] + [ANY] * ns,
        out_specs=[pl.BlockSpec((s, 128), lambda p: (0, p), pipeline_mode=once)] * 5 + [ANY] * ns,
        out_shape=[out] * 5 + [jax.ShapeDtypeStruct(p.shape, p.dtype) for p in parts],
        scratch_shapes=[pltpu.VMEM((2, nq, t, t), F32), pltpu.VMEM((2, nq, t, t), F32)] + (_scatter_scratch(ns) if ns else []),
        compiler_params=_params(("arbitrary",)),
    )(qn, kn, aq, ak, knt, akt, vb, lse, dmixed, *parts)
    return (*outs[:5], _keep_own_blocks(outs[5:], parts))


def fox_post(dqn, dkn, dc0, dc1, proj, smallp, qw, kw, sel, selt, fold_even, fold_odd, *, tm=256):
    s = proj.shape[0]
    nrow = s // tm

    def body(dqn_ref, dkn_ref, dc0_ref, dc1_ref, q_ref, k_ref, small_ref, sp_ref, qw_ref, kw_ref, sel_ref, selt_ref,
             fe_ref, fo_ref, dq_ref, dk_ref, dsmall_ref, gqw_ref, gkw_ref, gfb_ref, carry):
        step = pl.program_id(0)

        @pl.when(step == 0)
        def _():
            carry[...] = jnp.zeros_like(carry)

        def norm_bwd(x_ref, w_ref, dn, out_ref):
            x = x_ref[...]
            rf = _head_rstd(x, sel_ref, selt_ref)
            xh = x * rf
            g = dn * w_ref[...]
            mean_gx = _head_spread(_head_mean(g * xh, sel_ref, selt_ref), selt_ref)
            out_ref[...] = (rf * (g - xh * mean_gx)).astype(BF16)
            return jnp.sum(dn * xh, axis=0, keepdims=True)

        gqw = norm_bwd(q_ref, qw_ref, dqn_ref[...] * FOX_SCALE, dq_ref)
        gkw = norm_bwd(k_ref, kw_ref, dkn_ref[...], dk_ref)
        li = _lane_iota((tm, 128))
        f_lane = jnp.logical_and(li >= F_LANE, li < F_LANE + N_HEADS)
        dcum = -(_split3_dot(dc0_ref[...], fe_ref[...]) + _split3_dot(dc1_ref[...], fo_ref[...]))
        dlogf = _rev_cumsum_rows(dcum) + carry[...]
        carry[...] = dlogf[0:1, :]
        dfr = jnp.where(f_lane, dlogf * _sigmoid(-(small_ref[...] + sp_ref[3:4, :])), 0.0)
        dsmall_ref[...] = dfr
        gfb = jnp.sum(dfr, axis=0, keepdims=True)

        @pl.when(step == 0)
        def _():
            gqw_ref[...] = gqw
            gkw_ref[...] = gkw
            gfb_ref[...] = gfb

        @pl.when(step > 0)
        def _():
            gqw_ref[...] += gqw
            gkw_ref[...] += gkw
            gfb_ref[...] += gfb

    def rb(i):
        return nrow - 1 - i

    row = pl.BlockSpec((tm, 1024), lambda i: (rb(i), 0))
    vec = pl.BlockSpec((1, 1024), lambda i: (0, 0))
    fold = pl.BlockSpec((1024, 128), lambda i: (0, 0))
    return pl.pallas_call(
        body, name="fox_post", grid=(nrow,),
        in_specs=[row, row, row, row, pl.BlockSpec((tm, 1024), lambda i: (rb(i), Q_COL)),
                  pl.BlockSpec((tm, 1024), lambda i: (rb(i), K_COL)),
                  pl.BlockSpec((tm, 128), lambda i: (rb(i), SMALL_BLOCK)), pl.BlockSpec((8, 128), lambda i: (0, 0)), vec, vec,
                  fold, pl.BlockSpec((128, 1024), lambda i: (0, 0)), fold, fold],
        out_specs=[row, row, pl.BlockSpec((tm, 128), lambda i: (rb(i), 0)), vec, vec, pl.BlockSpec((1, 128), lambda i: (0, 0))],
        out_shape=[jax.ShapeDtypeStruct((s, 1024), BF16), jax.ShapeDtypeStruct((s, 1024), BF16),
                   jax.ShapeDtypeStruct((s, 128), F32), jax.ShapeDtypeStruct((1, 1024), F32),
                   jax.ShapeDtypeStruct((1, 1024), F32), jax.ShapeDtypeStruct((1, 128), F32)],
        scratch_shapes=[pltpu.VMEM((1, 128), F32)], compiler_params=_params(("arbitrary",)),
    )(dqn, dkn, dc0, dc1, proj, proj, proj, smallp, qw, kw, sel, selt, fold_even, fold_odd)


def local_step(x, target, wx, later_shards, ssd_cw8, ssd_cb, smallp, ssd_nw, qw_t, kw_t, sel, selt,
               norm_mix_w, norm_ffn_w, ffn_cw8, ffn_cb):
    h, h_t = rms_fwd(x, norm_mix_w, name="rms_mix_fwd")
    proj = matmul(h, wx, mode="nn", tm=1024, tn=PROJ_TILE, tk=1024, out_dtype=F32, name="mm_in_proj")
    y_ssd, y_ssd_t, ypre, states = ssd_fwd(proj, ssd_cw8, ssd_cb, smallp, ssd_nw)
    place_q, place_k, ones_q, ones_k, fold_even, fold_odd = fox_tables()
    qn, kn, aq, ak, vb, knt, akt, vt = fox_prep(proj, smallp, qw_t, kw_t, sel, selt, place_q, place_k, ones_q, ones_k)
    y_fox, y_fox_t, lse, (a_out, a_up, a_down) = fox_fwd(qn, kn, aq, ak, vt, shards=later_shards)
    w_out = a_out.reshape(2048, D_MODEL)
    w_down = a_down.reshape(D_FF, D_MODEL)
    s = x.shape[0]
    shard = lambda index: pl.BlockSpec((None, 1024, 1408), index)
    x1 = matmul(y_ssd, w_out, mode="nn", tm=1024, tn=1024, tk=1024, out_dtype=F32, name="mm_out_ssd", add=x)
    x1 = matmul(y_fox, w_out, mode="nn", tm=1024, tn=1024, tk=1024, out_dtype=F32, name="mm_out_fox", add=x1, b_koff=1)
    hf, hf_t = rms_fwd(x1, norm_ffn_w, name="rms_ffn_fwd")
    hu, act, act_t = up_ffn_fwd(hf, a_up, ffn_cw8, ffn_cb)
    y = matmul(act, w_down, mode="nn", tm=1024, tn=1024, tk=1408, out_dtype=F32, name="mm_down", add=x1)
    dy, sq = loss_head(y, target)

    g_down = matmul(act_t, dy, mode="nn", tm=1408, tn=1024, tk=1024, out_dtype=BF16, name="mm_dw_down")
    dhu, gcw_g, gcw_v = down_ffn_bwd(dy, w_down, hu, ffn_cw8, ffn_cb)
    dhf = matmul(dhu, a_up, mode="nt", tm=1024, tn=1024, tk=1408, out_dtype=F32, name="mm_dhf",
                 layout=dict(m=s, n=D_MODEL, k=2 * D_FF, a_spec=shard(lambda i, j, kk: (kk // 2, i, kk % 2)),
                             b_spec=shard(lambda i, j, kk: (kk, 0, 0))))
    g_up = matmul(hf_t, dhu, mode="nn", tm=1024, tn=1408, tk=1024, out_dtype=BF16, name="mm_dw_up",
                  layout=dict(m=D_MODEL, n=2 * D_FF, k=s, b_spec=shard(lambda i, j, kk: (j // 2, kk, j % 2)),
                              o_spec=shard(lambda i, j, kk: (j, i, 0)), out_shape=(4, D_MODEL, 1408)))
    dx1, g_norm_ffn = rms_bwd(dhf, x1, norm_ffn_w, dy, name="rms_ffn_bwd")
    dmixed = matmul(dx1, w_out, mode="nt", tm=1024, tn=1024, tk=1024, out_dtype=F32, name="mm_dmixed")
    g_out_a = matmul(y_ssd_t, dx1, mode="nn", tm=1024, tn=1024, tk=1024, out_dtype=BF16, name="mm_dw_out_ssd")
    g_out_b = matmul(y_fox_t, dx1, mode="nn", tm=1024, tn=1024, tk=1024, out_dtype=BF16, name="mm_dw_out_fox")
    early = [jnp.concatenate([g_out_a, g_out_b], axis=0).reshape(4, 512, D_MODEL), g_up, g_down.reshape(4, 704, D_MODEL)]
    dz, dxs, db, dc, dsmall_ssd, gcw_x, gcw_b, gcw_c, g_sp, g_ssd_nw, theirs = ssd_bwd(
        proj, ssd_cw8, ssd_cb, smallp, ssd_nw, ypre, states, dmixed, sel, swap=early)
    core = lax.axis_index("c").astype(jnp.int32).reshape(1)
    parts = [add_pair(a, b, core, name="add_pair_" + n, tr=ADAM_ROWS[n]) for a, b, n in zip(early, theirs, BIG_NAMES[1:])]
    dqn, dkn, dv, dc0, dc1, landed_early = fox_bwd(qn, kn, aq, ak, knt, akt, vb, lse, dmixed, parts=parts)
    dq, dk, dsmall_fox, g_qw, g_kw, g_fb = fox_post(dqn, dkn, dc0, dc1, proj, smallp, qw_t, kw_t, sel, selt,
                                                    fold_even, fold_odd)
    dproj = jnp.concatenate([dz, dxs, dq, dk, dv.astype(BF16), db, dc, (dsmall_ssd + dsmall_fox).astype(BF16)], axis=1)
    g_wx = matmul(h_t, dproj, mode="nn", tm=1024, tn=PROJ_TILE, tk=1024, out_dtype=BF16, name="mm_dw_in")
    g_in = _in_grad_shards(g_wx)
    part_in = add_pair(g_in, pair_swap_halves([g_in], name="pair_swap_w_in")[0], core, name="add_pair_w_in",
                       tr=ADAM_ROWS["w_in"])
    dh, landed_in = matmul(dproj, wx, mode="nt", tm=1024, tn=1024, tk=PROJ_TILE, out_dtype=F32, name="mm_dh",
                           scatter=[part_in])
    grad_x, g_norm_mix = rms_bwd(dh, x, norm_mix_w, dx1, name="rms_mix_bwd")
    return dict(
        sq=sq, grad_x=grad_x, landed=landed_in + landed_early,
        g_norm_mix=g_norm_mix, g_norm_ffn=g_norm_ffn, g_ssd_nw=g_ssd_nw,
        g_ssd_cw=jnp.concatenate([gcw_x, gcw_b, gcw_c], axis=1), g_sp=g_sp, g_fb=g_fb, g_qw=g_qw, g_kw=g_kw,
        g_ffn_cw=jnp.concatenate([gcw_g, gcw_v], axis=1))


def adamw(w, g, m, v, *, name, tr, allreduce=None):
    rows, cols = w.shape
    nsteps = rows // tr

    def body(*refs):
        if allreduce is None:
            w_ref, g_ref, m_ref, v_ref, d_ref, mo_ref, vo_ref = refs
        else:
            w_ref, g_ref, m_ref, v_ref, packed_ref, d_ref, mo_ref, vo_ref, summed_ref = refs[:9]
            start, finish = _allreduce_phases(packed_ref, summed_ref, *refs[9:])
            pl.when(pl.program_id(0) == 0)(start)
        gv = g_ref[...]
        mn = ADAM_B1 * m_ref[...] + (1.0 - ADAM_B1) * gv
        vn = ADAM_B2 * v_ref[...] + (1.0 - ADAM_B2) * (gv * gv)
        m_hat = mn / (1.0 - ADAM_B1 ** ADAM_STEP)
        v_hat = vn / (1.0 - ADAM_B2 ** ADAM_STEP)
        d_ref[...] = -ADAM_LR * (m_hat / (jnp.sqrt(v_hat) + ADAM_EPS) + ADAM_WD * w_ref[...])
        mo_ref[...] = mn
        vo_ref[...] = vn
        if allreduce is not None:
            pl.when(pl.program_id(0) == nsteps - 1)(finish)

    blk = pl.BlockSpec((tr, cols), lambda i: (i, 0))
    shp = jax.ShapeDtypeStruct((rows, cols), F32)
    if allreduce is None:
        return pl.pallas_call(
            body, name=name, grid=(nsteps,), in_specs=[blk] * 4, out_specs=[blk] * 3, out_shape=[shp] * 3,
            compiler_params=_params(("parallel",)),
        )(w, g, m, v)
    whole = pl.BlockSpec(memory_space=pltpu.VMEM)
    return pl.pallas_call(
        body, name=name, grid=(nsteps,), in_specs=[blk] * 4 + [whole], out_specs=[blk] * 3 + [whole],
        out_shape=[shp] * 3 + [jax.ShapeDtypeStruct(allreduce.shape, F32)],
        scratch_shapes=_allreduce_scratch(allreduce.shape[0]), compiler_params=_params(("arbitrary",)),
    )(w, g, m, v, allreduce)


def add_pair(full, theirs, core, *, name, tr):
    _, rows, cols = theirs.shape
    nblk = rows // tr

    def body(c_ref, a_ref, b_ref, o_ref):
        o_ref[...] = (a_ref[...].astype(F32) + b_ref[...].astype(F32)).astype(BF16)

    blk = pl.BlockSpec((1, tr, cols), lambda j, i, c: (j, i, 0))
    grid_spec = pltpu.PrefetchScalarGridSpec(
        num_scalar_prefetch=1, grid=(4, nblk),
        in_specs=[pl.BlockSpec((1, tr, cols), lambda j, i, c: (j, c[0] * nblk + i, 0)), blk], out_specs=blk)
    return pl.pallas_call(
        body, name=name, grid_spec=grid_spec, out_shape=jax.ShapeDtypeStruct(theirs.shape, BF16),
        compiler_params=_params(("parallel", "parallel")),
    )(core, full, theirs)


def sum_chips(parts, core, *, name, tr):
    _, rows, cols = parts.shape
    nblk = rows // tr

    def body(c_ref, p_ref, o_ref):
        acc = p_ref[0].astype(F32)
        for k in range(1, 4):
            acc = acc + p_ref[k].astype(F32)
        o_ref[...] = acc

    grid_spec = pltpu.PrefetchScalarGridSpec(
        num_scalar_prefetch=1, grid=(nblk,), in_specs=[pl.BlockSpec((4, tr, cols), lambda i, c: (0, i, 0))],
        out_specs=pl.BlockSpec((tr, cols), lambda i, c: (c[0] * nblk + i, 0)))
    return pl.pallas_call(
        body, name=name, grid_spec=grid_spec, out_shape=jax.ShapeDtypeStruct((2 * rows, cols), F32),
        compiler_params=_params(("parallel",)),
    )(core, parts)


ANY = pl.BlockSpec(memory_space=pl.ANY)


def _place():
    x, y, c = lax.axis_index("x"), lax.axis_index("y"), lax.axis_index("c")
    chips = [(1 - x, y), (x, 1 - y), (1 - x, 1 - y)]
    return x, y, c, chips


def _chunks(rows):
    size = next((c for c in (128, 176, 64, 32, 16, 8) if rows % c == 0), rows)
    return [(r, size) for r in range(0, rows, size)]


def gather_weights(shards):
    n = len(shards)

    def body(*refs):
        start, forward, finish = _gather_phases(refs[:n], refs[n:2 * n], *refs[2 * n:])
        start()
        forward()
        finish()

    gathered = pl.pallas_call(
        body, name="gather_weights", in_specs=[ANY] * n, out_specs=[ANY] * n,
        out_shape=_gather_out_shapes(shards), scratch_shapes=_gather_scratch(n),
    )(*shards)
    return gathered


def _gather_out_shapes(shards):
    return [jax.ShapeDtypeStruct((4,) + s.shape, s.dtype) for s in shards]


def _gather_scratch(n):
    return [pltpu.SemaphoreType.DMA((n, 7)), pltpu.SemaphoreType.DMA((n, 7))]


def _gather_phases(ins, outs, send_sems, recv_sems):
    n = len(ins)
    x, y, c, chips = _place()
    me = 2 * x + y
    sibling = (x, y, 1 - c)
    blks = [2 * cx + cy for cx, cy in chips]

    def half(a, blk, r=0, nr=None):
        rows = ins[a].shape[0] // 2
        return outs[a].at[blk, pl.ds(c * rows + r, rows if nr is None else nr), :]

    def to_chip(a, t, r=0, nr=None):
        rows = ins[a].shape[0] // 2
        return pltpu.make_async_remote_copy(
            src_ref=ins[a].at[pl.ds(c * rows + r, rows if nr is None else nr), :], dst_ref=half(a, me, r, nr),
            send_sem=send_sems.at[a, t], recv_sem=recv_sems.at[a, t], device_id=(*chips[t], c), device_id_type=MESH)

    def from_chip(a, t):
        return pltpu.make_async_remote_copy(
            src_ref=half(a, blks[t]), dst_ref=half(a, blks[t]), send_sem=send_sems.at[a, t], recv_sem=recv_sems.at[a, t],
            device_id=(*chips[t], c), device_id_type=MESH)

    def to_sibling(a, t, r=0, nr=None):
        return pltpu.make_async_remote_copy(
            src_ref=half(a, blks[t], r, nr), dst_ref=half(a, blks[t], r, nr), send_sem=send_sems.at[a, 3 + t],
            recv_sem=recv_sems.at[a, 3 + t], device_id=sibling, device_id_type=MESH)

    def from_sibling(a, t):
        rows = ins[a].shape[0] // 2
        dst = outs[a].at[blks[t], pl.ds((1 - c) * rows, rows), :]
        return pltpu.make_async_remote_copy(
            src_ref=dst, dst_ref=dst, send_sem=send_sems.at[a, 3 + t], recv_sem=recv_sems.at[a, 3 + t],
            device_id=sibling, device_id_type=MESH)

    def own(a, r=0, nr=None):
        return pltpu.make_async_remote_copy(
            src_ref=ins[a].at[pl.ds(r, ins[a].shape[0] if nr is None else nr), :],
            dst_ref=outs[a].at[me, pl.ds(r, ins[a].shape[0] if nr is None else nr), :],
            send_sem=send_sems.at[a, 6], recv_sem=recv_sems.at[a, 6], device_id=sibling, device_id_type=MESH)

    def start():
        for a in range(n):
            for t in range(3):
                for r, nr in _chunks(ins[a].shape[0] // 2):
                    to_chip(a, t, r, nr).start()
            for r, nr in _chunks(ins[a].shape[0]):
                own(a, r, nr).start()

    def forward():
        for a in range(n):
            for t in range(3):
                from_chip(a, t).wait_recv()
                for r, nr in _chunks(ins[a].shape[0] // 2):
                    to_sibling(a, t, r, nr).start()

    def finish():
        for a in range(n):
            for t in range(3):
                from_sibling(a, t).wait_recv()
        for a in range(n):
            for t in range(3):
                to_chip(a, t).wait_send()
                to_sibling(a, t).wait_send()
            own(a).wait()

    return start, forward, finish


def pair_swap_halves(grads, *, name):
    n = len(grads)

    def body(*refs):
        start, finish = _pair_swap_phases(refs[:n], refs[n:2 * n], *refs[2 * n:])
        start()
        finish()

    return pl.pallas_call(
        body, name=name, in_specs=[ANY] * n, out_specs=[ANY] * n, out_shape=_pair_swap_out_shapes(grads),
        scratch_shapes=_pair_swap_scratch(n),
    )(*grads)


def _pair_swap_out_shapes(grads):
    return [jax.ShapeDtypeStruct((4, g.shape[1] // 2, g.shape[2]), g.dtype) for g in grads]


def _pair_swap_scratch(n):
    return [pltpu.SemaphoreType.DMA((n,)), pltpu.SemaphoreType.DMA((n,))]


def _pair_swap_phases(ins, theirs, send_sems, recv_sems):
    n = len(ins)
    x, y, c, _ = _place()
    sibling = (x, y, 1 - c)

    def start():
        for a in range(n):
            rows = ins[a].shape[1] // 2
            for j in range(4):
                for r, nr in _chunks(rows):
                    pltpu.make_async_remote_copy(
                        src_ref=ins[a].at[j, pl.ds((1 - c) * rows + r, nr), :], dst_ref=theirs[a].at[j, pl.ds(r, nr), :],
                        send_sem=send_sems.at[a], recv_sem=recv_sems.at[a], device_id=sibling, device_id_type=MESH).start()

    def finish():
        for a in range(n):
            pltpu.make_async_remote_copy(src_ref=theirs[a], dst_ref=theirs[a], send_sem=send_sems.at[a],
                                         recv_sem=recv_sems.at[a], device_id=sibling, device_id_type=MESH).wait()

    return start, finish


def _scatter_scratch(n):
    return [pltpu.SemaphoreType.DMA((n, 3)), pltpu.SemaphoreType.DMA((n, 3))]


def _keep_own_blocks(landed, parts):
    if not parts:
        return []
    chip = 2 * lax.axis_index("x") + lax.axis_index("y")
    return [lax.dynamic_update_slice(l, lax.dynamic_slice_in_dim(p, chip, 1, axis=0), (chip, 0, 0))
            for l, p in zip(landed, parts)]


def _scatter_phases(ins, outs, send_sems, recv_sems):
    n = len(ins)
    x, y, c, chips = _place()
    me = 2 * x + y
    blks = [2 * cx + cy for cx, cy in chips]

    def start():
        for a in range(n):
            for r, nr in _chunks(ins[a].shape[1]):
                for t in range(3):
                    pltpu.make_async_remote_copy(
                        src_ref=ins[a].at[blks[t], pl.ds(r, nr), :], dst_ref=outs[a].at[me, pl.ds(r, nr), :],
                        send_sem=send_sems.at[a, t], recv_sem=recv_sems.at[a, t],
                        device_id=(*chips[t], c), device_id_type=MESH).start()

    def finish():
        for a in range(n):
            for t in range(3):
                pltpu.make_async_remote_copy(
                    src_ref=outs[a].at[blks[t]], dst_ref=outs[a].at[blks[t]], send_sem=send_sems.at[a, t],
                    recv_sem=recv_sems.at[a, t], device_id=(*chips[t], c), device_id_type=MESH).wait()

    return start, finish


def pair_join_halves(bufs):
    n = len(bufs)

    def body(*refs):
        outs = refs[n:2 * n]
        send_sems, recv_sems = refs[2 * n:]
        x, y, c, _ = _place()
        sibling = (x, y, 1 - c)
        for a in range(n):
            rows = outs[a].shape[0] // 2
            for r, nr in _chunks(rows):
                mine = outs[a].at[pl.ds(c * rows + r, nr), :]
                pltpu.make_async_remote_copy(src_ref=mine, dst_ref=mine, send_sem=send_sems.at[a], recv_sem=recv_sems.at[a],
                                             device_id=sibling, device_id_type=MESH).start()
        for a in range(n):
            rows = outs[a].shape[0] // 2
            pltpu.make_async_remote_copy(
                src_ref=outs[a].at[pl.ds(c * rows, rows), :], dst_ref=outs[a].at[pl.ds((1 - c) * rows, rows), :],
                send_sem=send_sems.at[a], recv_sem=recv_sems.at[a], device_id=sibling, device_id_type=MESH).wait()

    return pl.pallas_call(
        body, name="pair_join_halves", in_specs=[ANY] * n, out_specs=[ANY] * n,
        out_shape=[jax.ShapeDtypeStruct(b.shape, b.dtype) for b in bufs], input_output_aliases={a: a for a in range(n)},
        scratch_shapes=[pltpu.SemaphoreType.DMA((n,)), pltpu.SemaphoreType.DMA((n,))],
    )(*bufs)


def _allreduce_scratch(rows):
    return [pltpu.VMEM((8, rows, 128), F32), pltpu.SemaphoreType.DMA((7,)), pltpu.SemaphoreType.DMA((7,))]


def _allreduce_phases(in_ref, out_ref, gathered, send_sems, recv_sems):
    x, y, c, _ = _place()
    me = 4 * x + 2 * y + c
    flips = [(fx, fy, fc) for fx in (0, 1) for fy in (0, 1) for fc in (0, 1)][1:]
    peers = [((1 - x) if fx else x, (1 - y) if fy else y, (1 - c) if fc else c) for fx, fy, fc in flips]

    def send(t):
        return pltpu.make_async_remote_copy(
            src_ref=in_ref, dst_ref=gathered.at[me], send_sem=send_sems.at[t], recv_sem=recv_sems.at[t],
            device_id=peers[t], device_id_type=MESH)

    def start():
        gathered[me] = in_ref[...]
        for t in range(7):
            send(t).start()

    def finish():
        for t, (px, py, pc) in enumerate(peers):
            slot = gathered.at[4 * px + 2 * py + pc]
            pltpu.make_async_remote_copy(
                src_ref=slot, dst_ref=slot, send_sem=send_sems.at[t], recv_sem=recv_sems.at[t],
                device_id=(px, py, pc), device_id_type=MESH).wait_recv()
        for t in range(7):
            send(t).wait_send()
        acc = gathered[0]
        for k in range(1, 8):
            acc = acc + gathered[k]
        out_ref[...] = acc

    return start, finish


SMALL_NAMES = ("norm_mix_w", "ssd_conv_w", "ssd_conv_b", "ssd_dt_bias", "ssd_a_log", "ssd_d", "ssd_norm_w", "fox_f_bias",
               "fox_q_norm_w", "fox_k_norm_w", "norm_ffn_w", "ffn_conv_w", "ffn_conv_b")
BIG_NAMES = ("w_in", "w_out", "w_up", "w_down")
WEIGHT_ORDER = ("norm_mix_w", "w_in", "ssd_conv_w", "ssd_conv_b", "ssd_dt_bias", "ssd_a_log", "ssd_d", "ssd_norm_w",
                "fox_f_bias", "fox_q_norm_w", "fox_k_norm_w", "w_out", "norm_ffn_w", "w_up", "ffn_conv_w", "ffn_conv_b", "w_down")
ADAM_ROWS = {"w_in": 256, "w_out": 256, "w_up": 256, "w_down": 176}


def _pack(arrays):
    pieces = []
    for a in arrays:
        flat = a.reshape(-1).astype(F32)
        pieces += [flat, jnp.zeros(((-flat.shape[0]) % 1024,), F32)]
    return jnp.concatenate(pieces).reshape(-1, 128)


def _unpack(packed, shapes):
    out, r = [], 0
    for shp in shapes:
        size = 1
        for d in shp:
            size *= d
        nrow = 8 * (-(-size // 1024))
        out.append(packed[r:r + nrow].reshape(-1)[:size].reshape(shp))
        r += nrow
    return out


IN_SHARD = IN_COLS // 4
IN_SEGMENTS = ((0, 2048, 0), (2048, 2560, 5120), (2560, 2576, MAIN_COLS), (2576, 5648, 2048), (5648, 5664, MAIN_COLS + F_LANE))


def _in_cols(shards, lo, hi):
    out = []
    for j in range(4):
        a, b = max(lo, IN_SHARD * j), min(hi, IN_SHARD * (j + 1))
        if a < b:
            out.append(shards[j][:, a - IN_SHARD * j:b - IN_SHARD * j])
    return out


def _in_grad_shards(g):
    shards = []
    for j in range(4):
        pieces = []
        for lo, hi, at in IN_SEGMENTS:
            a, b = max(lo, IN_SHARD * j), min(hi, IN_SHARD * (j + 1))
            if a < b:
                pieces.append(g[:, at + a - lo:at + b - lo])
        shards.append(jnp.concatenate(pieces, axis=1))
    return jnp.stack(shards)


def _pad_rows(a, rows):
    return jnp.pad(a, ((0, rows - a.shape[0]), (0, 0)))


def kernel(x, norm_mix_w, w_in, ssd_conv_w, ssd_conv_b, ssd_dt_bias, ssd_a_log, ssd_d, ssd_norm_w, fox_f_bias, fox_q_norm_w, fox_k_norm_w, w_out, norm_ffn_w, w_up, ffn_conv_w, ffn_conv_b, w_down, loss_target, m_norm_mix_w, m_w_in, m_ssd_conv_w, m_ssd_conv_b, m_ssd_dt_bias, m_ssd_a_log, m_ssd_d, m_ssd_norm_w, m_fox_f_bias, m_fox_q_norm_w, m_fox_k_norm_w, m_w_out, m_norm_ffn_w, m_w_up, m_ffn_conv_w, m_ffn_conv_b, m_w_down, v_norm_mix_w, v_w_in, v_ssd_conv_w, v_ssd_conv_b, v_ssd_dt_bias, v_ssd_a_log, v_ssd_d, v_ssd_norm_w, v_fox_f_bias, v_fox_q_norm_w, v_fox_k_norm_w, v_w_out, v_norm_ffn_w, v_w_up, v_ffn_conv_w, v_ffn_conv_b, v_w_down):
    w = dict(norm_mix_w=norm_mix_w, w_in=w_in, ssd_conv_w=ssd_conv_w, ssd_conv_b=ssd_conv_b, ssd_dt_bias=ssd_dt_bias,
             ssd_a_log=ssd_a_log, ssd_d=ssd_d, ssd_norm_w=ssd_norm_w, fox_f_bias=fox_f_bias, fox_q_norm_w=fox_q_norm_w,
             fox_k_norm_w=fox_k_norm_w, w_out=w_out, norm_ffn_w=norm_ffn_w, w_up=w_up, ffn_conv_w=ffn_conv_w,
             ffn_conv_b=ffn_conv_b, w_down=w_down)
    m = dict(norm_mix_w=m_norm_mix_w, w_in=m_w_in, ssd_conv_w=m_ssd_conv_w, ssd_conv_b=m_ssd_conv_b, ssd_dt_bias=m_ssd_dt_bias,
             ssd_a_log=m_ssd_a_log, ssd_d=m_ssd_d, ssd_norm_w=m_ssd_norm_w, fox_f_bias=m_fox_f_bias, fox_q_norm_w=m_fox_q_norm_w,
             fox_k_norm_w=m_fox_k_norm_w, w_out=m_w_out, norm_ffn_w=m_norm_ffn_w, w_up=m_w_up, ffn_conv_w=m_ffn_conv_w,
             ffn_conv_b=m_ffn_conv_b, w_down=m_w_down)
    v = dict(norm_mix_w=v_norm_mix_w, w_in=v_w_in, ssd_conv_w=v_ssd_conv_w, ssd_conv_b=v_ssd_conv_b, ssd_dt_bias=v_ssd_dt_bias,
             ssd_a_log=v_ssd_a_log, ssd_d=v_ssd_d, ssd_norm_w=v_ssd_norm_w, fox_f_bias=v_fox_f_bias, fox_q_norm_w=v_fox_q_norm_w,
             fox_k_norm_w=v_fox_k_norm_w, w_out=v_w_out, norm_ffn_w=v_norm_ffn_w, w_up=v_w_up, ffn_conv_w=v_ffn_conv_w,
             ffn_conv_b=v_ffn_conv_b, w_down=v_w_down)
    chip = 2 * lax.axis_index("x") + lax.axis_index("y")

    a_in, a_scw, a_fcw = gather_weights([w_in[0].astype(BF16), _pad_rows(ssd_conv_w[0], 16), _pad_rows(ffn_conv_w[0], 16)])
    later_shards = [w_out[0].astype(BF16), w_up[0].astype(BF16), w_down[0].astype(BF16)]
    wx = jnp.concatenate([p for lo, hi, _ in sorted(IN_SEGMENTS, key=lambda seg: seg[2]) for p in _in_cols(a_in, lo, hi)]
                         + [jnp.zeros((D_MODEL, PROJ_COLS - IN_COLS), BF16)], axis=1)
    ssd_cw8 = a_scw.transpose(1, 0, 2).reshape(16, 1536)[:8]
    ffn_cw8 = a_fcw.transpose(1, 0, 2).reshape(16, 2 * D_FF)[:8]
    gap = lambda n: jnp.zeros((n,), F32)
    smallp = jnp.concatenate([ssd_dt_bias[0], gap(112), ssd_a_log[0], gap(112), ssd_d[0], gap(112),
                              gap(F_LANE), fox_f_bias[0], gap(128 - F_LANE - N_HEADS), gap(4 * 128)]).reshape(8, 128)
    qw_t = jnp.tile(fox_q_norm_w[0], N_HEADS)[None]
    kw_t = jnp.tile(fox_k_norm_w[0], N_HEADS)[None]
    sel = jnp.asarray((np.arange(1024)[:, None] // HEAD_DIM == np.arange(128)[None, :]).astype(np.float32), BF16)

    res = local_step(x[0], loss_target[0], wx, later_shards, ssd_cw8, ssd_conv_b, smallp, ssd_norm_w, qw_t, kw_t,
                     sel, sel.T, norm_mix_w, norm_ffn_w, ffn_cw8, ffn_conv_b)

    full_shapes = [(1, 1024), (1, 4, 1536), (1, 1536), (1, 16), (1, 16), (1, 16), (1, 1024), (1, 16), (1, 64), (1, 64),
                   (1, 1024), (1, 3, 2 * D_FF), (1, 2 * D_FF), (1,)]
    local_small = [res["g_norm_mix"], res["g_ssd_cw"][:4], res["g_ssd_cw"][4], res["g_sp"][0, :16], res["g_sp"][1, :16],
                   res["g_sp"][2, :16], res["g_ssd_nw"], res["g_fb"][0, F_LANE:F_LANE + 16],
                   res["g_qw"].reshape(N_HEADS, HEAD_DIM).sum(0), res["g_kw"].reshape(N_HEADS, HEAD_DIM).sum(0),
                   res["g_norm_ffn"], res["g_ffn_cw"][:3], res["g_ffn_cw"][3], jnp.sum(res["sq"])]
    landed = res["landed"]
    core = lax.axis_index("c").astype(jnp.int32).reshape(1)
    halves = [sum_chips(p, core, name="sum_chips_" + n, tr=ADAM_ROWS[n]) for p, n in zip(landed, BIG_NAMES)]
    g_big = dict(zip(BIG_NAMES, pair_join_halves(halves)))

    grads, deltas, new_m, new_v = {}, {}, {}, {}
    for n in BIG_NAMES:
        out = adamw(w[n][0], g_big[n], m[n][0], v[n][0], name="adamw_" + n, tr=ADAM_ROWS[n],
                    allreduce=_pack(local_small) if n == BIG_NAMES[0] else None)
        if n == BIG_NAMES[0]:
            summed = _unpack(out[3], full_shapes)
        d, mn, vn = out[:3]
        grads[n], deltas[n], new_m[n], new_v[n] = g_big[n][None], d[None], mn[None], vn[None]
    loss = (0.5 / D_MODEL) * summed[-1][0]
    g_small = dict(zip(SMALL_NAMES, summed[:-1]))
    g_small["ssd_conv_w"] = lax.dynamic_slice(g_small["ssd_conv_w"], (0, 0, 384 * chip), (1, 4, 384))
    g_small["ffn_conv_w"] = lax.dynamic_slice(g_small["ffn_conv_w"], (0, 0, 1408 * chip), (1, 3, 1408))
    shapes = [w[n].shape for n in SMALL_NAMES]
    packed_w = _pack([w[n] for n in SMALL_NAMES])
    d, mn, vn = adamw(packed_w, _pack([g_small[n] for n in SMALL_NAMES]), _pack([m[n] for n in SMALL_NAMES]),
                      _pack([v[n] for n in SMALL_NAMES]), name="adamw_small", tr=packed_w.shape[0])
    for n, dd, mm, vv in zip(SMALL_NAMES, _unpack(d, shapes), _unpack(mn, shapes), _unpack(vn, shapes)):
        grads[n], deltas[n], new_m[n], new_v[n] = g_small[n].reshape(w[n].shape), dd, mm, vv
    return (loss, res["grad_x"][None], *[grads[n] for n in WEIGHT_ORDER], *[deltas[n] for n in WEIGHT_ORDER],
            *[new_m[n] for n in WEIGHT_ORDER], *[new_v[n] for n in WEIGHT_ORDER])
```

```python
import functools

import jax
import jax.numpy as jnp
import numpy as np
from jax import lax
from jax.experimental import pallas as pl
from jax.experimental.pallas import tpu as pltpu

F32 = jnp.float32
BF16 = jnp.bfloat16
MESH = pl.DeviceIdType.MESH

D_MODEL = 1024
HEAD_DIM = 64
N_HEADS = 16
N_PAIRS = N_HEADS // 2
SSD_CHUNK = 128
SSD_STATE = 128
SSD_CONV = 4
D_FF = 2816
FFN_CONV = 3
NORM_EPS = 1e-6
MAIN_COLS = 5632
SMALL_COLS = 128
PROJ_COLS = MAIN_COLS + SMALL_COLS
SMALL_BLOCK = MAIN_COLS // SMALL_COLS
PROJ_TILE = 1152
F_LANE = 16
IN_COLS = 5664

ADAM_LR = 0.001
ADAM_B1 = 0.9
ADAM_B2 = 0.999
ADAM_EPS = 1e-08
ADAM_WD = 0.01
ADAM_STEP = 10

VMEM_LIMIT_V7X = 56 * 1024 * 1024
NEG_BIG = -1e30


def _params(sem=None):
    return pltpu.CompilerParams(dimension_semantics=sem, vmem_limit_bytes=VMEM_LIMIT_V7X)


def _sigmoid(x):
    return 1.0 / (1.0 + jnp.exp(-x))


def _silu_and_grad(x):
    s = _sigmoid(x)
    return x * s, s * (1.0 + x * (1.0 - s))


def _shift_down(v, j):
    return v if j == 0 else pltpu.roll(v, j, 0)


def _shift_up(v, j):
    return v if j == 0 else pltpu.roll(v, v.shape[0] - j, 0)


def _row_iota(shape):
    return lax.broadcasted_iota(jnp.int32, shape, 0)


def _lane_iota(shape):
    return lax.broadcasted_iota(jnp.int32, shape, 1)


def _dot(a, b, mode="nn"):
    dims = {"nn": (((1,), (0,)), ((), ())), "nt": (((1,), (1,)), ((), ())), "tn": (((0,), (0,)), ((), ()))}[mode]
    return lax.dot_general(a.astype(BF16), b.astype(BF16), dims, preferred_element_type=F32)


def _dot_f32(a, b):
    return jnp.dot(a, b, precision=lax.Precision.HIGHEST, preferred_element_type=F32)


def matmul(a, b, *, mode, tm, tn, tk, out_dtype, name, add=None, b_koff=0, scatter=(), layout=None):
    layout = layout or {}
    if layout:
        m, n, k = layout["m"], layout["n"], layout["k"]
    else:
        (m, k), n = a.shape, (b.shape[1] if mode == "nn" else b.shape[0])
    assert m % tm == 0 and n % tn == 0 and k % tk == 0, (name, m, n, k, tm, tn, tk)
    nk = k // tk
    grid = (m // tm, n // tn, nk)
    a_spec = layout.get("a_spec") or pl.BlockSpec((tm, tk), lambda i, j, kk: (i, kk))
    b_spec = layout.get("b_spec") or (pl.BlockSpec((tn, tk), lambda i, j, kk: (j, kk + b_koff)) if mode == "nt"
                                      else pl.BlockSpec((tk, tn), lambda i, j, kk: (kk + b_koff, j)))
    o_spec = layout.get("o_spec") or pl.BlockSpec((tm, tn), lambda i, j, kk: (i, j))
    out_struct = jax.ShapeDtypeStruct(layout.get("out_shape", (m, n)), out_dtype)
    has_add = add is not None
    n_in = 3 if has_add else 2
    ns = len(scatter)

    def body(*refs):
        a_ref, b_ref = refs[:2]
        add_ref = refs[2] if has_add else None
        o_ref, acc_ref = refs[n_in + ns], refs[n_in + 2 * ns + 1]
        kk = pl.program_id(2)
        if ns:
            step = (pl.program_id(0) * grid[1] + pl.program_id(1)) * grid[2] + kk
            start, finish_copies = _scatter_phases(refs[n_in:n_in + ns], refs[n_in + ns + 1:n_in + 2 * ns + 1],
                                                   *refs[n_in + 2 * ns + 2:])
            pl.when(step == 0)(start)
        part = _dot(a_ref[...], b_ref[...], mode)

        def finish(total):
            if has_add:
                total = total + add_ref[...]
            o_ref[...] = total.astype(out_dtype)

        if nk == 1:
            finish(part)
        else:
            @pl.when(kk == 0)
            def _():
                acc_ref[...] = part

            @pl.when(jnp.logical_and(kk > 0, kk < nk - 1))
            def _():
                acc_ref[...] += part

            @pl.when(kk == nk - 1)
            def _():
                finish(acc_ref[...] + part)

        if ns:
            pl.when(step == grid[0] * grid[1] * grid[2] - 1)(finish_copies)

    in_specs = [a_spec, b_spec] + ([o_spec] if has_add else [])
    args = (a, b) + ((add,) if has_add else ())
    acc = pltpu.VMEM((tm, tn) if nk > 1 else (8, 128), F32)
    if not ns:
        return pl.pallas_call(
            body, name=name, grid=grid, in_specs=in_specs, out_specs=o_spec, out_shape=out_struct,
            scratch_shapes=[acc], compiler_params=_params(("parallel", "parallel", "arbitrary")),
        )(*args)
    outs = pl.pallas_call(
        body, name=name, grid=grid, in_specs=in_specs + [ANY] * ns, out_specs=[o_spec] + [ANY] * ns,
        out_shape=[out_struct] + [jax.ShapeDtypeStruct(p.shape, p.dtype) for p in scatter],
        scratch_shapes=[acc] + _scatter_scratch(ns), compiler_params=_params(("arbitrary", "arbitrary", "arbitrary")),
    )(*args, *scatter)
    return outs[0], _keep_own_blocks(outs[1:], scatter)


def rms_fwd(x, w, *, name, tm=1024):
    s, d = x.shape

    def body(x_ref, w_ref, h_ref, ht_ref):
        xv = x_ref[...]
        r = lax.rsqrt(jnp.mean(xv * xv, axis=-1, keepdims=True) + NORM_EPS)
        h = (xv * r) * w_ref[...]
        h_ref[...] = h.astype(BF16)
        ht_ref[...] = h.T.astype(BF16)

    return pl.pallas_call(
        body, name=name, grid=(s // tm,),
        in_specs=[pl.BlockSpec((tm, d), lambda i: (i, 0)), pl.BlockSpec((1, d), lambda i: (0, 0))],
        out_specs=[pl.BlockSpec((tm, d), lambda i: (i, 0)), pl.BlockSpec((d, tm), lambda i: (0, i))],
        out_shape=[jax.ShapeDtypeStruct((s, d), BF16), jax.ShapeDtypeStruct((d, s), BF16)],
        compiler_params=_params(("parallel",)),
    )(x, w)


def rms_bwd(dh, x, w, resid, *, name, tm=1024):
    s, d = x.shape

    def body(dh_ref, x_ref, w_ref, res_ref, dx_ref, dw_ref):
        xv = x_ref[...]
        dhv = dh_ref[...]
        r = lax.rsqrt(jnp.mean(xv * xv, axis=-1, keepdims=True) + NORM_EPS)
        xh = xv * r
        g = dhv * w_ref[...]
        dx_ref[...] = res_ref[...] + r * (g - xh * jnp.mean(g * xh, axis=-1, keepdims=True))
        part = jnp.sum(dhv * xh, axis=0, keepdims=True)

        @pl.when(pl.program_id(0) == 0)
        def _():
            dw_ref[...] = part

        @pl.when(pl.program_id(0) > 0)
        def _():
            dw_ref[...] += part

    row = pl.BlockSpec((tm, d), lambda i: (i, 0))
    vec = pl.BlockSpec((1, d), lambda i: (0, 0))
    return pl.pallas_call(
        body, name=name, grid=(s // tm,), in_specs=[row, row, vec, row], out_specs=[row, vec],
        out_shape=[jax.ShapeDtypeStruct((s, d), F32), jax.ShapeDtypeStruct((1, d), F32)],
        compiler_params=_params(("arbitrary",)),
    )(dh, x, w, resid)


def rms_bwd_matmul(dh, x, w, resid, b, *, name, tm=512):
    s, d = x.shape
    n = b.shape[0]

    def body(dh_ref, x_ref, w_ref, res_ref, b_ref, dx_ref, dw_ref, prod_ref):
        part = jnp.zeros((1, d), F32)
        for r in range(0, tm, UP_ROWS):
            rows = slice(r, r + UP_ROWS)
            xv, dhv = x_ref[rows, :], dh_ref[rows, :]
            rstd = lax.rsqrt(jnp.mean(xv * xv, axis=-1, keepdims=True) + NORM_EPS)
            xh = xv * rstd
            g = dhv * w_ref[...]
            dx = res_ref[rows, :] + rstd * (g - xh * jnp.mean(g * xh, axis=-1, keepdims=True))
            dx_ref[rows, :] = dx
            prod_ref[rows, :] = _dot(dx, b_ref[...], "nt")
            part = part + jnp.sum(dhv * xh, axis=0, keepdims=True)

        @pl.when(pl.program_id(0) == 0)
        def _():
            dw_ref[...] = part

        @pl.when(pl.program_id(0) > 0)
        def _():
            dw_ref[...] += part

    row = pl.BlockSpec((tm, d), lambda i: (i, 0))
    vec = pl.BlockSpec((1, d), lambda i: (0, 0))
    return pl.pallas_call(
        body, name=name, grid=(s // tm,), in_specs=[row, row, vec, row, pl.BlockSpec((n, d), lambda i: (0, 0))],
        out_specs=[row, vec, pl.BlockSpec((tm, n), lambda i: (i, 0))],
        out_shape=[jax.ShapeDtypeStruct((s, d), F32), jax.ShapeDtypeStruct((1, d), F32), jax.ShapeDtypeStruct((s, n), F32)],
        compiler_params=_params(("arbitrary",)),
    )(dh, x, w, resid, b)


def loss_head(y, target, *, tm=1024):
    s, d = y.shape

    def body(y_ref, t_ref, dy_ref, sq_ref):
        e = y_ref[...] - t_ref[...]
        dy_ref[...] = e / float(d)
        part = jnp.sum(e * e, axis=0, keepdims=True)

        @pl.when(pl.program_id(0) == 0)
        def _():
            sq_ref[...] = part

        @pl.when(pl.program_id(0) > 0)
        def _():
            sq_ref[...] += part

    row = pl.BlockSpec((tm, d), lambda i: (i, 0))
    vec = pl.BlockSpec((1, d), lambda i: (0, 0))
    return pl.pallas_call(
        body, name="loss_head", grid=(s // tm,), in_specs=[row, row], out_specs=[row, vec],
        out_shape=[jax.ShapeDtypeStruct((s, d), F32), jax.ShapeDtypeStruct((1, d), F32)],
        compiler_params=_params(("arbitrary",)),
    )(y, target)


def _row_shifts(ext, k_taps):
    return [_shift_down(ext, j) for j in range(k_taps)]


def _conv_rows(shifts, w):
    k_taps = len(shifts)
    acc = w[k_taps - 1:k_taps, :] * shifts[0]
    for k in range(k_taps - 1):
        acc = acc + w[k:k + 1, :] * shifts[k_taps - 1 - k]
    return acc


def _conv_weight_grad(dcur, shifts, rows, width):
    k_taps = len(shifts)
    out = [jnp.sum(dcur * shifts[k_taps - 1 - k][rows], axis=0, keepdims=True) for k in range(k_taps)]
    out.append(jnp.sum(dcur, axis=0, keepdims=True))
    return _stack_rows(out, width)


def _conv_rows_transposed(dext, w, k_taps):
    acc = w[k_taps - 1:k_taps, :] * dext
    for k in range(k_taps - 1):
        acc = acc + w[k:k + 1, :] * _shift_up(dext, k_taps - 1 - k)
    return acc


def _stack_rows(rows, width):
    ri = _row_iota((8, width))
    out = jnp.zeros((8, width), F32)
    for k, r in enumerate(rows):
        out = out + jnp.where(ri == k, r, 0.0)
    return out


UP_SHARD = 1408
UP_ROWS = 256


def up_ffn_fwd(hf, a_up, conv_w8, conv_b, *, tm=512):
    s = hf.shape[0]

    def body(a_ref, bg_ref, bv_ref, wg_ref, wv_ref, cbg_ref, cbv_ref, hu_ref, act_ref, actt_ref, carry):
        i, j = pl.program_id(0), pl.program_id(1)
        prev_g = jnp.where(i == 0, 0.0, carry[0, j])
        prev_v = jnp.where(i == 0, 0.0, carry[1, j])
        for r in range(0, tm, UP_ROWS):
            rows = slice(r, r + UP_ROWS)
            a = a_ref[rows, :]
            hg, hv = _dot(a, bg_ref[...]), _dot(a, bv_ref[...])
            hu_ref[0, rows, :] = hg
            hu_ref[1, rows, :] = hv
            gc = _conv_rows(_row_shifts(jnp.concatenate([prev_g, hg], axis=0), FFN_CONV), wg_ref[...])[8:] + cbg_ref[...]
            vc = _conv_rows(_row_shifts(jnp.concatenate([prev_v, hv], axis=0), FFN_CONV), wv_ref[...])[8:] + cbv_ref[...]
            act = gc * _sigmoid(gc) * vc
            act_ref[rows, :] = act.astype(BF16)
            actt_ref[:, rows] = act.T.astype(BF16)
            prev_g, prev_v = hg[UP_ROWS - 8:], hv[UP_ROWS - 8:]
        carry[0, j] = prev_g
        carry[1, j] = prev_v

    shard = lambda off: pl.BlockSpec((None, D_MODEL, UP_SHARD), lambda i, j: (j + off, 0, 0))
    taps = lambda off: pl.BlockSpec((8, UP_SHARD), lambda i, j: (0, j + off))
    bias = lambda off: pl.BlockSpec((1, UP_SHARD), lambda i, j: (0, j + off))
    return pl.pallas_call(
        body, name="up_ffn_fwd", grid=(s // tm, 2),
        in_specs=[pl.BlockSpec((tm, D_MODEL), lambda i, j: (i, 0)), shard(0), shard(2), taps(0), taps(2), bias(0), bias(2)],
        out_specs=[pl.BlockSpec((2, tm, UP_SHARD), lambda i, j: (0, i, j)), pl.BlockSpec((tm, UP_SHARD), lambda i, j: (i, j)),
                   pl.BlockSpec((UP_SHARD, tm), lambda i, j: (j, i))],
        out_shape=[jax.ShapeDtypeStruct((2, s, D_FF), F32), jax.ShapeDtypeStruct((s, D_FF), BF16),
                   jax.ShapeDtypeStruct((D_FF, s), BF16)],
        scratch_shapes=[pltpu.VMEM((2, 2, 8, UP_SHARD), F32)], compiler_params=_params(("arbitrary", "arbitrary")),
    )(hf, a_up, a_up, conv_w8, conv_w8, conv_b, conv_b)


def ffn_mid_bwd(hu, dact, conv_w8, conv_b, *, tm=1024, tc=256):
    s = hu.shape[1]
    ncol = D_FF // tc
    nrow = s // tm
    r8 = tm // 8

    def body(g_ref, v_ref, gp_ref, vp_ref, gn_ref, vn_ref, da_ref, dan_ref, wg_ref, wv_ref, bg_ref, bv_ref,
             dhu_ref, wgo_ref, wvo_ref):
        i = pl.program_id(1)
        first = i == 0
        last = i == nrow - 1

        def ext_of(cur_ref, prev_ref, next_ref):
            prev = jnp.where(first, 0.0, prev_ref[...])
            return jnp.concatenate([prev, cur_ref[...], next_ref[...]], axis=0)

        g_sh = _row_shifts(ext_of(g_ref, gp_ref, gn_ref), FFN_CONV)
        v_sh = _row_shifts(ext_of(v_ref, vp_ref, vn_ref), FFN_CONV)
        gc = _conv_rows(g_sh, wg_ref[...]) + bg_ref[...]
        vc = _conv_rows(v_sh, wv_ref[...]) + bv_ref[...]
        da_ext = jnp.concatenate([jnp.zeros((8, tc), F32), da_ref[...], jnp.where(last, 0.0, dan_ref[...])], axis=0)
        silu, dsilu = _silu_and_grad(gc)
        dgc = da_ext * vc * dsilu
        dvc = da_ext * silu
        dhu_ref[0] = _conv_rows_transposed(dgc, wg_ref[...], FFN_CONV)[8:8 + tm].astype(BF16)
        dhu_ref[1] = _conv_rows_transposed(dvc, wv_ref[...], FFN_CONV)[8:8 + tm].astype(BF16)

        cur = slice(8, 8 + tm)
        pg = _conv_weight_grad(dgc[cur], g_sh, cur, tc)
        pv = _conv_weight_grad(dvc[cur], v_sh, cur, tc)

        @pl.when(first)
        def _():
            wgo_ref[...] = pg
            wvo_ref[...] = pv

        @pl.when(i > 0)
        def _():
            wgo_ref[...] += pg
            wvo_ref[...] += pv

    def prev_idx(i):
        return jnp.maximum(i * r8 - 1, 0)

    def next_idx(i):
        return jnp.minimum((i + 1) * r8, s // 8 - 1)

    half = lambda k, rows, row_index: pl.BlockSpec((None, rows, tc), lambda j, i: (k, row_index(i), j))
    in_specs = [
        half(0, tm, lambda i: i), half(1, tm, lambda i: i),
        half(0, 8, prev_idx), half(1, 8, prev_idx),
        half(0, 8, next_idx), half(1, 8, next_idx),
        pl.BlockSpec((tm, tc), lambda j, i: (i, j)),
        pl.BlockSpec((8, tc), lambda j, i: (next_idx(i), j)),
        pl.BlockSpec((8, tc), lambda j, i: (0, j)),
        pl.BlockSpec((8, tc), lambda j, i: (0, j + ncol)),
        pl.BlockSpec((1, tc), lambda j, i: (0, j)),
        pl.BlockSpec((1, tc), lambda j, i: (0, j + ncol)),
    ]
    out_specs = [pl.BlockSpec((2, tm, tc), lambda j, i: (0, i, j)), pl.BlockSpec((8, tc), lambda j, i: (0, j)),
                 pl.BlockSpec((8, tc), lambda j, i: (0, j))]
    out_shape = [jax.ShapeDtypeStruct((2, s, D_FF), BF16),
                 jax.ShapeDtypeStruct((8, D_FF), F32), jax.ShapeDtypeStruct((8, D_FF), F32)]
    return pl.pallas_call(
        body, name="ffn_mid_bwd", grid=(ncol, nrow), in_specs=in_specs, out_specs=out_specs, out_shape=out_shape,
        compiler_params=_params(("parallel", "arbitrary")),
    )(hu, hu, hu, hu, hu, hu, dact, dact, conv_w8, conv_w8, conv_b, conv_b)


def _softplus(x):
    return jnp.maximum(x, 0.0) + jnp.log(1.0 + jnp.exp(-jnp.abs(x)))


def _cumsum_rows(v):
    n = v.shape[0]
    ri = _row_iota(v.shape)
    sh = 1
    while sh < n:
        v = v + jnp.where(ri >= sh, _shift_down(v, sh), 0.0)
        sh *= 2
    return v


def _rev_cumsum_rows(v):
    n = v.shape[0]
    ri = _row_iota(v.shape)
    sh = 1
    while sh < n:
        v = v + jnp.where(ri < n - sh, _shift_up(v, sh), 0.0)
        sh *= 2
    return v


def _total(v):
    return jnp.sum(jnp.sum(v, axis=1, keepdims=True), axis=0, keepdims=True)


def _ssd_in_specs(rev_nc=None):
    def ch(c):
        return c if rev_nc is None else rev_nc - 1 - c

    def prev(c):
        return jnp.maximum(ch(c) * (SSD_CHUNK // 8) - 1, 0)

    L = SSD_CHUNK
    return [
        pl.BlockSpec((L, 1024), lambda c: (ch(c), 0)),
        pl.BlockSpec((L, 1024), lambda c: (ch(c), 1)),
        pl.BlockSpec((L, 256), lambda c: (ch(c), 20)),
        pl.BlockSpec((L, 256), lambda c: (ch(c), 21)),
        pl.BlockSpec((8, 1024), lambda c: (prev(c), 1)),
        pl.BlockSpec((8, 256), lambda c: (prev(c), 20)),
        pl.BlockSpec((8, 256), lambda c: (prev(c), 21)),
        pl.BlockSpec((8, 1024), lambda c: (0, 0)),
        pl.BlockSpec((8, 256), lambda c: (0, 4)),
        pl.BlockSpec((8, 256), lambda c: (0, 5)),
        pl.BlockSpec((1, 1024), lambda c: (0, 0)),
        pl.BlockSpec((1, 256), lambda c: (0, 4)),
        pl.BlockSpec((1, 256), lambda c: (0, 5)),
        pl.BlockSpec((L, SMALL_COLS), lambda c: (ch(c), SMALL_BLOCK)),
        pl.BlockSpec((8, 128), lambda c: (0, 0)),
        pl.BlockSpec((1, 1024), lambda c: (0, 0)),
    ]


def _ssd_conv_pre(cur_ref, prev_ref, w_ref, b_ref, first):
    prev = jnp.where(first, 0.0, prev_ref[...])
    shifts = _row_shifts(jnp.concatenate([prev, cur_ref[...]], axis=0), SSD_CONV)
    return shifts, _conv_rows(shifts, w_ref[...])[8:] + b_ref[...]


def _ssd_time_consts(small_ref, sp_ref):
    dt_pre = small_ref[...] + sp_ref[0:1, :]
    dt = _softplus(dt_pre)
    a = -jnp.exp(sp_ref[1:2, :])
    acs = _cumsum_rows(dt * a)
    return dt_pre, dt, a, acs


def ssd_fwd(proj, conv_w8, conv_b, smallp, norm_w):
    s = proj.shape[0]
    nc = s // SSD_CHUNK
    L = SSD_CHUNK

    def body(z_ref, xs_ref, b_ref, c_ref, xsp_ref, bp_ref, cp_ref, wx_ref, wb_ref, wc_ref, bx_ref, bb_ref, bc_ref,
             small_ref, sp_ref, nw_ref, y_ref, yt_ref, ypre_ref, st_ref, state):
        first = pl.program_id(0) == 0

        @pl.when(first)
        def _():
            state[...] = jnp.zeros_like(state)

        xs = _ssd_conv_pre(xs_ref, xsp_ref, wx_ref, bx_ref, first)[1]
        xs = xs * _sigmoid(xs)
        bm = _ssd_conv_pre(b_ref, bp_ref, wb_ref, bb_ref, first)[1]
        bm = bm * _sigmoid(bm)
        cm = _ssd_conv_pre(c_ref, cp_ref, wc_ref, bc_ref, first)[1]
        cm = cm * _sigmoid(cm)
        _, dt, _, acs = _ssd_time_consts(small_ref, sp_ref)
        acs_t = acs.T
        li = _lane_iota((L, L))
        ri = _row_iota((L, L))
        tri = ri >= li
        lo = li < HEAD_DIM
        st_ref[0] = state[...]
        for g in range(2):
            bg = bm[:, 128 * g:128 * g + 128]
            cg = cm[:, 128 * g:128 * g + 128]
            gmat = _dot(cg, bg, "nt")
            for pp in range(4):
                p = 4 * g + pp
                h0, h1 = 2 * p, 2 * p + 1
                x = xs[:, 128 * p:128 * p + 128]
                a0, a1 = acs[:, h0:h0 + 1], acs[:, h1:h1 + 1]
                xdt = x * jnp.where(lo, dt[:, h0:h0 + 1], dt[:, h1:h1 + 1])
                m0 = gmat * jnp.exp(jnp.where(tri, a0 - acs_t[h0:h0 + 1, :], NEG_BIG))
                m1 = gmat * jnp.exp(jnp.where(tri, a1 - acs_t[h1:h1 + 1, :], NEG_BIG))
                yd = _dot(m0, jnp.where(lo, xdt, 0.0)) + _dot(m1, jnp.where(lo, 0.0, xdt))
                hin = state[p]
                yo = _dot(cg, hin, "nt") * jnp.exp(jnp.where(lo, a0, a1))
                dskip = jnp.where(lo[0:1], sp_ref[2:3, h0:h0 + 1], sp_ref[2:3, h1:h1 + 1])
                ypre_ref[:, 128 * p:128 * p + 128] = yd + yo + dskip * x
                al0, al1 = acs[L - 1:L, h0:h0 + 1], acs[L - 1:L, h1:h1 + 1]
                w = jnp.exp(jnp.where(lo, al0 - a0, al1 - a1))
                dec = jnp.exp(jnp.where(ri < HEAD_DIM, al0, al1))
                state[p] = dec * hin + _dot(xdt * w, bg, "tn")
        z = z_ref[...]
        yg = ypre_ref[...] * (z * _sigmoid(z))
        for g in range(2):
            seg = yg[:, 512 * g:512 * g + 512]
            r = lax.rsqrt(jnp.mean(seg * seg, axis=-1, keepdims=True) + NORM_EPS)
            out = (seg * r) * nw_ref[:, 512 * g:512 * g + 512]
            y_ref[:, 512 * g:512 * g + 512] = out.astype(BF16)
            yt_ref[512 * g:512 * g + 512, :] = out.T.astype(BF16)

    row = pl.BlockSpec((L, 1024), lambda c: (c, 0))
    return pl.pallas_call(
        body, name="ssd_fwd", grid=(nc,), in_specs=_ssd_in_specs(),
        out_specs=[row, pl.BlockSpec((1024, L), lambda c: (0, c)), row,
                   pl.BlockSpec((1, N_PAIRS, 128, 128), lambda c: (c, 0, 0, 0))],
        out_shape=[jax.ShapeDtypeStruct((s, 1024), BF16), jax.ShapeDtypeStruct((1024, s), BF16),
                   jax.ShapeDtypeStruct((s, 1024), F32), jax.ShapeDtypeStruct((nc, N_PAIRS, 128, 128), F32)],
        scratch_shapes=[pltpu.VMEM((N_PAIRS, 128, 128), F32)],
        compiler_params=_params(("arbitrary",)),
    )(proj, proj, proj, proj, proj, proj, proj, conv_w8, conv_w8, conv_w8, conv_b, conv_b, conv_b, proj, smallp, norm_w)


def ssd_bwd(proj, conv_w8, conv_b, smallp, norm_w, ypre, states, dy, sel, swap=()):
    s = proj.shape[0]
    nc = s // SSD_CHUNK
    L = SSD_CHUNK

    ns = len(swap)
    n_in, n_out, n_scratch = 20, 10, 11

    def body(*refs):
        own = refs[:n_in] + refs[n_in + ns:n_in + ns + n_out] + refs[n_in + 2 * ns + n_out:n_in + 2 * ns + n_out + n_scratch]
        if ns:
            start, finish = _pair_swap_phases(refs[n_in:n_in + ns], refs[n_in + ns + n_out:n_in + 2 * ns + n_out],
                                              *refs[n_in + 2 * ns + n_out + n_scratch:])
            pl.when(pl.program_id(0) == 0)(start)
        compute(*own)
        if ns:
            pl.when(pl.program_id(0) == nc - 1)(finish)

    def compute(z_ref, xs_ref, b_ref, c_ref, xsp_ref, bp_ref, cp_ref, wx_ref, wb_ref, wc_ref, bx_ref, bb_ref, bc_ref,
                small_ref, sp_ref, nw_ref, ypre_ref, st_ref, dy_ref, sel_ref,
                dz_ref, dxs_ref, db_ref, dc_ref, dsmall_ref, gwx_ref, gwb_ref, gwc_ref, gsp_ref, gnw_ref,
                dstate, carry_x, carry_b, carry_c, dxs_buf, dbm_buf, dcm_buf, qcs, col_sums, acs_terms, dt_terms):
        step = pl.program_id(0)
        col_sums[...] = jnp.zeros_like(col_sums)
        first_chunk = step == nc - 1
        start = step == 0

        @pl.when(start)
        def _():
            dstate[...] = jnp.zeros_like(dstate)
            carry_x[...] = jnp.zeros_like(carry_x)
            carry_b[...] = jnp.zeros_like(carry_b)
            carry_c[...] = jnp.zeros_like(carry_c)

        xs_sh, xs_pre = _ssd_conv_pre(xs_ref, xsp_ref, wx_ref, bx_ref, first_chunk)
        b_sh, b_pre = _ssd_conv_pre(b_ref, bp_ref, wb_ref, bb_ref, first_chunk)
        c_sh, c_pre = _ssd_conv_pre(c_ref, cp_ref, wc_ref, bc_ref, first_chunk)
        xs, xs_ds = _silu_and_grad(xs_pre)
        bm, b_ds = _silu_and_grad(b_pre)
        cm, c_ds = _silu_and_grad(c_pre)
        dt_pre, dt, a, acs = _ssd_time_consts(small_ref, sp_ref)
        acs_t = acs.T
        li = _lane_iota((L, L))
        ri = _row_iota((L, L))
        tri = ri >= li
        lo = li < HEAD_DIM
        lo_rows = ri < HEAD_DIM
        li1 = _lane_iota((1, L))

        z = z_ref[...]
        sz, dsz = _silu_and_grad(z)
        y = ypre_ref[...]
        yg = y * sz
        dout = dy_ref[...]
        dyg_parts = []
        gnw_parts = []
        for g in range(2):
            sl = slice(512 * g, 512 * g + 512)
            seg = yg[:, sl]
            r = lax.rsqrt(jnp.mean(seg * seg, axis=-1, keepdims=True) + NORM_EPS)
            n = seg * r
            gnw_parts.append(jnp.sum(dout[:, sl] * n, axis=0, keepdims=True))
            gg = dout[:, sl] * nw_ref[:, sl]
            dyg_parts.append(r * (gg - n * jnp.mean(gg * n, axis=-1, keepdims=True)))
        dyg = jnp.concatenate(dyg_parts, axis=1)
        gnw = jnp.concatenate(gnw_parts, axis=1)
        dz_ref[...] = (dyg * y * dsz).astype(BF16)
        dypre = dyg * sz

        qcs[...] = jnp.zeros_like(qcs)
        dalast = jnp.zeros((1, L), F32)
        for g in range(2):
            bg = bm[:, 128 * g:128 * g + 128]
            cg = cm[:, 128 * g:128 * g + 128]
            gmat = _dot(cg, bg, "nt")
            dgmat = jnp.zeros((L, L), F32)
            dbg = jnp.zeros((L, L), F32)
            dcg = jnp.zeros((L, L), F32)
            for pp in range(4):
                p = 4 * g + pp
                h0, h1 = 2 * p, 2 * p + 1
                lanes = slice(128 * p, 128 * p + 128)
                x = xs[:, lanes]
                dyp = dypre[:, lanes]
                a0, a1 = acs[:, h0:h0 + 1], acs[:, h1:h1 + 1]
                dtl = jnp.where(lo, dt[:, h0:h0 + 1], dt[:, h1:h1 + 1])
                xdt = x * dtl
                l0 = jnp.exp(jnp.where(tri, a0 - acs_t[h0:h0 + 1, :], NEG_BIG))
                l1 = jnp.exp(jnp.where(tri, a1 - acs_t[h1:h1 + 1, :], NEG_BIG))
                m0, m1 = gmat * l0, gmat * l1
                dskip = jnp.where(lo[0:1], sp_ref[2:3, h0:h0 + 1], sp_ref[2:3, h1:h1 + 1])
                col_sums[0:1, lanes] = jnp.sum(dyp * x, axis=0, keepdims=True)
                dx = dyp * dskip
                dy0, dy1 = jnp.where(lo, dyp, 0.0), jnp.where(lo, 0.0, dyp)
                x0, x1 = jnp.where(lo, xdt, 0.0), jnp.where(lo, 0.0, xdt)
                dm0, dm1 = _dot(dy0, x0, "nt"), _dot(dy1, x1, "nt")
                dxdt = _dot(m0, dy0, "tn") + _dot(m1, dy1, "tn")
                q0, q1 = dm0 * m0, dm1 * m1
                qcs[h0:h0 + 1, :] = jnp.sum(q0, axis=0, keepdims=True)
                qcs[h1:h1 + 1, :] = jnp.sum(q1, axis=0, keepdims=True)
                row_terms = jnp.where(lo, q0 + pltpu.roll(q0, HEAD_DIM, 1), q1 + pltpu.roll(q1, HEAD_DIM, 1))
                dgmat = dgmat + dm0 * l0 + dm1 * l1
                hin = st_ref[0, p]
                e = jnp.exp(jnp.where(lo, a0, a1))
                ch = _dot(cg, hin, "nt")
                dch = dyp * e
                dcg = dcg + _dot(dch, hin)
                dhin = _dot(dch, cg, "tn")
                dhout = dstate[p]
                al0, al1 = acs[L - 1:L, h0:h0 + 1], acs[L - 1:L, h1:h1 + 1]
                dec = jnp.exp(jnp.where(lo_rows, al0, al1))
                dhin = dhin + dec * dhout
                dal = dhout * hin * dec
                dal0 = _total(jnp.where(lo_rows, dal, 0.0))
                dal1 = _total(dal) - dal0
                dalast = dalast + jnp.where(li1 == h0, dal0, 0.0) + jnp.where(li1 == h1, dal1, 0.0)
                w = jnp.exp(jnp.where(lo, al0 - a0, al1 - a1))
                xw = xdt * w
                dxw = _dot(bg, dhout, "nt")
                dbg = dbg + _dot(xw, dhout)
                dxdt = dxdt + dxw * w
                dww = dxw * xw
                col_sums[1:2, lanes] = jnp.sum(dww, axis=0, keepdims=True)
                acs_terms[:, lanes] = row_terms + dch * ch - dww
                dx = dx + dxdt * dtl
                dt_terms[:, lanes] = dxdt * x
                dxs_buf[:, lanes] = dx
                dstate[p] = dhin
            dcg = dcg + _dot(dgmat, bg)
            dbg = dbg + _dot(dgmat, cg, "tn")
            dbm_buf[:, 128 * g:128 * g + 128] = dbg
            dcm_buf[:, 128 * g:128 * g + 128] = dcg

        head_sums = _split3_dot(col_sums[...], sel_ref[...])
        dskip_g = head_sums[0:1, :]
        dalast = dalast + head_sums[1:2, :]
        ddt = _split3_dot(dt_terms[...], sel_ref[...])
        dacs_tot = _split3_dot(acs_terms[...], sel_ref[...]) - qcs[...].T + jnp.where(ri == L - 1, dalast, 0.0)
        dstep = _rev_cumsum_rows(dacs_tot)
        ddt = ddt + dstep * a
        head_lane = li < N_HEADS
        ddt_pre = jnp.where(head_lane, ddt * _sigmoid(dt_pre), 0.0)
        dsmall_ref[...] = ddt_pre
        da = jnp.sum(jnp.where(head_lane, dstep * dt, 0.0), axis=0, keepdims=True)
        gsp = _stack_rows([jnp.sum(ddt_pre, axis=0, keepdims=True), da * a, dskip_g], L)

        def conv_back(dpost, ds, shifts, w_ref, carry, out_ref, width):
            dpre = dpost * ds
            dext = jnp.concatenate([dpre, carry[...]], axis=0)
            out_ref[...] = _conv_rows_transposed(dext, w_ref[...], SSD_CONV)[:L].astype(BF16)
            carry[...] = dpre[0:8]
            return _conv_weight_grad(dpre, shifts, slice(8, 8 + L), width)

        gwx = conv_back(dxs_buf[...], xs_ds, xs_sh, wx_ref, carry_x, dxs_ref, 1024)
        gwb = conv_back(dbm_buf[...], b_ds, b_sh, wb_ref, carry_b, db_ref, 256)
        gwc = conv_back(dcm_buf[...], c_ds, c_sh, wc_ref, carry_c, dc_ref, 256)

        @pl.when(start)
        def _():
            gwx_ref[...] = gwx
            gwb_ref[...] = gwb
            gwc_ref[...] = gwc
            gsp_ref[...] = gsp
            gnw_ref[...] = gnw

        @pl.when(step > 0)
        def _():
            gwx_ref[...] += gwx
            gwb_ref[...] += gwb
            gwc_ref[...] += gwc
            gsp_ref[...] += gsp
            gnw_ref[...] += gnw

    def ch(c):
        return nc - 1 - c

    row = pl.BlockSpec((L, 1024), lambda c: (ch(c), 0))
    row256 = pl.BlockSpec((L, 256), lambda c: (ch(c), 0))
    in_specs = _ssd_in_specs(rev_nc=nc) + [row, pl.BlockSpec((1, N_PAIRS, 128, 128), lambda c: (ch(c), 0, 0, 0)), row,
                                           pl.BlockSpec((1024, 128), lambda c: (0, 0))]
    out_specs = [row, row, row256, row256, pl.BlockSpec((L, 128), lambda c: (ch(c), 0)),
                 pl.BlockSpec((8, 1024), lambda c: (0, 0)), pl.BlockSpec((8, 256), lambda c: (0, 0)),
                 pl.BlockSpec((8, 256), lambda c: (0, 0)), pl.BlockSpec((8, 128), lambda c: (0, 0)),
                 pl.BlockSpec((1, 1024), lambda c: (0, 0))]
    out_shape = [jax.ShapeDtypeStruct((s, 1024), BF16), jax.ShapeDtypeStruct((s, 1024), BF16),
                 jax.ShapeDtypeStruct((s, 256), BF16), jax.ShapeDtypeStruct((s, 256), BF16),
                 jax.ShapeDtypeStruct((s, 128), F32),
                 jax.ShapeDtypeStruct((8, 1024), F32), jax.ShapeDtypeStruct((8, 256), F32),
                 jax.ShapeDtypeStruct((8, 256), F32), jax.ShapeDtypeStruct((8, 128), F32),
                 jax.ShapeDtypeStruct((1, 1024), F32)]
    scratch = [pltpu.VMEM((N_PAIRS, 128, 128), F32), pltpu.VMEM((8, 1024), F32), pltpu.VMEM((8, 256), F32),
               pltpu.VMEM((8, 256), F32), pltpu.VMEM((L, 1024), F32), pltpu.VMEM((L, 256), F32), pltpu.VMEM((L, 256), F32),
               pltpu.VMEM((L, L), F32), pltpu.VMEM((8, 1024), F32), pltpu.VMEM((L, 1024), F32), pltpu.VMEM((L, 1024), F32)]
    assert (len(in_specs), len(out_specs), len(scratch)) == (n_in, n_out, n_scratch)
    outs = pl.pallas_call(
        body, name="ssd_bwd", grid=(nc,), in_specs=in_specs + [ANY] * ns, out_specs=out_specs + [ANY] * ns,
        out_shape=out_shape + _pair_swap_out_shapes(swap), scratch_shapes=scratch + (_pair_swap_scratch(ns) if ns else []),
        compiler_params=_params(("arbitrary",)),
    )(proj, proj, proj, proj, proj, proj, proj, conv_w8, conv_w8, conv_w8, conv_b, conv_b, conv_b, proj, smallp, norm_w,
      ypre, states, dy, sel, *swap)
    return (*outs[:n_out], list(outs[n_out:]))


FOX_SCALE = HEAD_DIM ** -0.5
FOX_T = 256
Q_COL, K_COL, V_COL = 2, 3, 4


def _split_dot(v, m, terms):
    out, rest = None, v
    for i in range(terms):
        piece = rest.astype(BF16)
        out = _dot(piece, m) if out is None else out + _dot(piece, m)
        if i + 1 < terms:
            rest = rest - piece.astype(F32)
    return out


def _split3_dot(v, m):
    return _split_dot(v, m, 3)


def _head_mean(x, sel_ref, selt_ref):
    return _dot(x, sel_ref[...]) * (1.0 / HEAD_DIM)


def _head_spread(v, selt_ref):
    return _split_dot(v, selt_ref[...], 2)


def _head_rstd(x, sel_ref, selt_ref):
    return _head_spread(lax.rsqrt(_head_mean(x * x, sel_ref, selt_ref) + NORM_EPS), selt_ref)


def fox_tables():
    r = np.arange(3 * 128)
    piece, lane = r // 128, r % 128
    head = lane - F_LANE
    is_head = np.logical_and(head >= 0, head < N_HEADS)
    col = 128 * (head // 2) + HEAD_DIM * (1 - head % 2) + piece
    cols = np.arange(1024)
    place_q = np.logical_and(is_head[:, None], cols[None, :] == col[:, None])
    place_k = np.logical_and(is_head[:, None], cols[None, :] == (col + 3)[:, None])
    ones_q = np.logical_and(cols % HEAD_DIM >= 3, cols % HEAD_DIM < 6)[None]
    ones_k = (cols % HEAD_DIM < 3)[None]
    h = np.arange(128) - F_LANE
    ok = np.logical_and(h >= 0, h < N_HEADS)
    same_pair = cols[:, None] // 128 == (h // 2)[None, :]
    fold_even = np.logical_and(np.logical_and(ok, h % 2 == 0)[None, :], same_pair)
    fold_odd = np.logical_and(np.logical_and(ok, h % 2 == 1)[None, :], same_pair)
    as_bf16 = lambda t: jnp.asarray(t.astype(np.float32), BF16)
    return (as_bf16(place_q), as_bf16(place_k), jnp.asarray(ones_q, F32), jnp.asarray(ones_k, F32),
            as_bf16(fold_even), as_bf16(fold_odd))


def fox_prep(proj, smallp, qw, kw, sel, selt, place_q, place_k, ones_q, ones_k, *, tm=256):
    s = proj.shape[0]

    def body(q_ref, k_ref, v_ref, small_ref, sp_ref, qw_ref, kw_ref, sel_ref, selt_ref, pq_ref, pk_ref, oq_ref, ok_ref,
             qn_ref, kn_ref, aq_ref, ak_ref, vb_ref, knt_ref, akt_ref, vt_ref, carry):
        @pl.when(pl.program_id(0) == 0)
        def _():
            carry[...] = jnp.zeros_like(carry)

        q = q_ref[...]
        qn_ref[...] = (((q * _head_rstd(q, sel_ref, selt_ref)) * qw_ref[...]) * FOX_SCALE).astype(BF16)
        k = k_ref[...]
        kn = ((k * _head_rstd(k, sel_ref, selt_ref)) * kw_ref[...]).astype(BF16)
        kn_ref[...] = kn
        knt_ref[...] = kn.astype(F32).T.astype(BF16)
        vb_ref[...] = v_ref[...].astype(BF16)
        vt_ref[...] = v_ref[...].T.astype(BF16)
        li = _lane_iota((tm, 128))
        f_lane = jnp.logical_and(li >= F_LANE, li < F_LANE + N_HEADS)
        logf = jnp.where(f_lane, -_softplus(-(small_ref[...] + sp_ref[3:4, :])), 0.0)
        cum = _cumsum_rows(logf) + carry[...]
        carry[...] = cum[tm - 1:tm, :]
        hi = cum.astype(BF16)
        r1 = cum - hi.astype(F32)
        mid = r1.astype(BF16)
        lo = (r1 - mid.astype(F32)).astype(BF16)
        pieces = jnp.concatenate([hi, mid, lo], axis=1)
        aq_ref[...] = (_dot(pieces, pq_ref[...]) + oq_ref[...]).astype(BF16)
        ak = ok_ref[...] - _dot(pieces, pk_ref[...])
        ak_ref[...] = ak.astype(BF16)
        akt_ref[...] = ak.T.astype(BF16)

    row = pl.BlockSpec((tm, 1024), lambda i: (i, 0))
    col = pl.BlockSpec((1024, tm), lambda i: (0, i))
    vec = pl.BlockSpec((1, 1024), lambda i: (0, 0))
    table = pl.BlockSpec((384, 1024), lambda i: (0, 0))
    wide = jax.ShapeDtypeStruct((s, 1024), BF16)
    tall = jax.ShapeDtypeStruct((1024, s), BF16)
    return pl.pallas_call(
        body, name="fox_prep", grid=(s // tm,),
        in_specs=[pl.BlockSpec((tm, 1024), lambda i: (i, Q_COL)), pl.BlockSpec((tm, 1024), lambda i: (i, K_COL)),
                  pl.BlockSpec((tm, 1024), lambda i: (i, V_COL)),
                  pl.BlockSpec((tm, 128), lambda i: (i, SMALL_BLOCK)), pl.BlockSpec((8, 128), lambda i: (0, 0)), vec, vec,
                  pl.BlockSpec((1024, 128), lambda i: (0, 0)), pl.BlockSpec((128, 1024), lambda i: (0, 0)),
                  table, table, vec, vec],
        out_specs=[row, row, row, row, row, col, col, col],
        out_shape=[wide, wide, wide, wide, wide, tall, tall, tall],
        scratch_shapes=[pltpu.VMEM((1, 128), F32)], compiler_params=_params(("arbitrary",)),
    )(proj, proj, proj, proj, smallp, qw, kw, sel, selt, place_q, place_k, ones_q, ones_k)


def fox_fwd(qn, kn, aq, ak, vt, shards=()):
    s = qn.shape[0]
    t = FOX_T
    nq = s // t
    ng = len(shards)

    def body(*refs):
        q_ref, k_ref, aq_ref, ak_ref, vt_ref = refs[:5]
        o_ref, ot_ref, lse_ref = refs[5 + ng:8 + ng]
        p = pl.program_id(0)
        if ng:
            start, forward, finish = _gather_phases(refs[5:5 + ng], refs[8 + ng:8 + 2 * ng], *refs[8 + 2 * ng:])
            pl.when(p == 0)(start)
            pl.when(p == N_PAIRS // 2)(forward)

        @pl.when(p == 0)
        def _():
            lse_ref[...] = jnp.zeros_like(lse_ref)

        lo = _lane_iota((t, 128)) < HEAD_DIM
        lo_rows = _row_iota((128, t)) < HEAD_DIM
        causal_t = _lane_iota((t, t)) >= _row_iota((t, t))

        def q_loop(qi, _):
            q0 = pl.multiple_of(qi * t, t)
            qv, aqv = q_ref[pl.ds(q0, t), :], aq_ref[pl.ds(q0, t), :]
            qa, qb = jnp.where(lo, qv, aqv), jnp.where(lo, aqv, qv)

            def scores(kj):
                k0 = pl.multiple_of(kj * t, t)
                kv, akv = k_ref[pl.ds(k0, t), :], ak_ref[pl.ds(k0, t), :]
                return _dot(jnp.where(lo, kv, akv), qa, "nt"), _dot(jnp.where(lo, akv, kv), qb, "nt")

            def update(kj, stats, s0, s1):
                m0, l0, m1, l1, acc = stats
                vtv = vt_ref[:, pl.ds(pl.multiple_of(kj * t, t), t)]
                n0 = jnp.maximum(m0, jnp.max(s0, axis=0, keepdims=True))
                n1 = jnp.maximum(m1, jnp.max(s1, axis=0, keepdims=True))
                a0, a1 = jnp.exp(m0 - n0), jnp.exp(m1 - n1)
                p0, p1 = jnp.exp(s0 - n0), jnp.exp(s1 - n1)
                l0 = a0 * l0 + jnp.sum(p0, axis=0, keepdims=True)
                l1 = a1 * l1 + jnp.sum(p1, axis=0, keepdims=True)
                acc = (jnp.where(lo_rows, a0, a1) * acc + _dot(jnp.where(lo_rows, vtv, 0.0), p0)
                       + _dot(jnp.where(lo_rows, 0.0, vtv), p1))
                return n0, l0, n1, l1, acc

            def step(kj, carry):
                stats, (s0, s1) = carry[:5], carry[5:]
                nxt = scores(kj + 1)
                return (*update(kj, stats, s0, s1), *nxt)

            def row(val):
                return jnp.full((1, t), val, F32)

            init = (row(NEG_BIG), row(0.0), row(NEG_BIG), row(0.0), jnp.zeros((128, t), F32), *scores(0))
            carry = lax.fori_loop(0, qi, step, init)
            s0, s1 = jnp.where(causal_t, carry[5], NEG_BIG), jnp.where(causal_t, carry[6], NEG_BIG)
            m0, l0, m1, l1, acc = update(qi, carry[:5], s0, s1)
            out_t = acc / jnp.where(lo_rows, l0, l1)
            ot_ref[:, pl.ds(q0, t)] = out_t.astype(BF16)
            o_ref[pl.ds(q0, t), :] = out_t.T.astype(BF16)
            ri = _row_iota((N_HEADS, t))
            old = lse_ref[:, pl.ds(q0, t)]
            lse_ref[:, pl.ds(q0, t)] = jnp.where(
                ri == 2 * p, m0 + jnp.log(l0), jnp.where(ri == 2 * p + 1, m1 + jnp.log(l1), old))
            return 0

        lax.fori_loop(0, nq, q_loop, 0)
        if ng:
            pl.when(p == N_PAIRS - 1)(finish)

    pair = pl.BlockSpec((s, 128), lambda p: (0, p))
    outs = pl.pallas_call(
        body, name="fox_fwd", grid=(N_PAIRS,),
        in_specs=[pair] * 4 + [pl.BlockSpec((128, s), lambda p: (p, 0))] + [ANY] * ng,
        out_specs=[pair, pl.BlockSpec((128, s), lambda p: (p, 0)), pl.BlockSpec((N_HEADS, s), lambda p: (0, 0))] + [ANY] * ng,
        out_shape=[jax.ShapeDtypeStruct((s, 1024), BF16), jax.ShapeDtypeStruct((1024, s), BF16),
                   jax.ShapeDtypeStruct((N_HEADS, s), F32)] + _gather_out_shapes(shards),
        scratch_shapes=_gather_scratch(ng) if ng else [],
        compiler_params=_params(("arbitrary",)),
    )(qn, kn, aq, ak, vt, *shards)
    return outs[0], outs[1], outs[2], list(outs[3:])


def fox_bwd(qn, kn, aq, ak, knt, akt, vb, lse, dmixed, parts=()):
    s = qn.shape[0]
    t = FOX_T
    nq = s // t
    once = pl.Buffered(1)
    ns = len(parts)

    def body(*refs):
        q_ref, k_ref, aq_ref, ak_ref, kt_ref, akt_ref, v_ref, lse_ref, do_ref = refs[:9]
        dq_ref, dk_ref, dv_ref, dc0_ref, dc1_ref = refs[9 + ns:14 + ns]
        p_scr, dp_scr = refs[14 + 2 * ns:16 + 2 * ns]
        p = pl.program_id(0)
        if ns:
            start, finish = _scatter_phases(refs[9:9 + ns], refs[14 + ns:14 + 2 * ns], *refs[16 + 2 * ns:])
            pl.when(p == 0)(start)
        dk_ref[...] = jnp.zeros_like(dk_ref)
        dv_ref[...] = jnp.zeros_like(dv_ref)
        dc0_ref[...] = jnp.zeros_like(dc0_ref)
        dc1_ref[...] = jnp.zeros_like(dc1_ref)
        lo = _lane_iota((t, 128)) < HEAD_DIM
        lo_rows = _row_iota((128, t)) < HEAD_DIM
        causal_t = _lane_iota((t, t)) >= _row_iota((t, t))

        def q_loop(qi, _):
            q0 = pl.multiple_of(qi * t, t)
            qv, aqv = q_ref[pl.ds(q0, t), :], aq_ref[pl.ds(q0, t), :]
            qa, qb = jnp.where(lo, qv, aqv), jnp.where(lo, aqv, qv)
            do = do_ref[pl.ds(q0, t), :]
            doa, dob = jnp.where(lo, do, 0.0).astype(BF16), jnp.where(lo, 0.0, do).astype(BF16)
            lse_blk = lse_ref[:, pl.ds(q0, t)]
            ri = _row_iota((N_HEADS, t))
            lse0 = jnp.sum(jnp.where(ri == 2 * p, lse_blk, 0.0), axis=0, keepdims=True)
            lse1 = jnp.sum(jnp.where(ri == 2 * p + 1, lse_blk, 0.0), axis=0, keepdims=True)

            def scores(kj):
                k0 = pl.multiple_of(kj * t, t)
                kv, akv = k_ref[pl.ds(k0, t), :], ak_ref[pl.ds(k0, t), :]
                return _dot(jnp.where(lo, kv, akv), qa, "nt"), _dot(jnp.where(lo, akv, kv), qb, "nt")

            def pass1(kj, d0, d1, diagonal):
                k0 = pl.multiple_of(kj * t, t)
                vv = v_ref[pl.ds(k0, t), :]
                s0, s1 = scores(kj)
                if diagonal:
                    s0, s1 = jnp.where(causal_t, s0, NEG_BIG), jnp.where(causal_t, s1, NEG_BIG)
                p0, p1 = jnp.exp(s0 - lse0), jnp.exp(s1 - lse1)
                dp0, dp1 = _dot(vv, doa, "nt"), _dot(vv, dob, "nt")
                p_scr[0, kj], p_scr[1, kj] = p0, p1
                dp_scr[0, kj], dp_scr[1, kj] = dp0, dp1
                dv_ref[pl.ds(k0, t), :] += _dot(p0, doa) + _dot(p1, dob)
                return d0 + jnp.sum(p0 * dp0, axis=0, keepdims=True), d1 + jnp.sum(p1 * dp1, axis=0, keepdims=True)

            zero = jnp.zeros((1, t), F32)
            d0, d1 = lax.fori_loop(0, qi, lambda kj, c: pass1(kj, *c, False), (zero, zero))
            d0, d1 = pass1(qi, d0, d1, True)

            def fold_lanes(v):
                return functools.reduce(lambda a, b: a + b, [v[:, 128 * i:128 * (i + 1)] for i in range(t // 128)])

            def pass2(kj, carry):
                dq0, dq1 = carry
                k0 = pl.multiple_of(kj * t, t)
                p0, p1 = p_scr[0, kj], p_scr[1, kj]
                ds0, ds1 = p0 * (dp_scr[0, kj] - d0), p1 * (dp_scr[1, kj] - d1)
                dk_ref[pl.ds(k0, t), :] += jnp.where(lo, _dot(ds0, qa), _dot(ds1, qb))
                dc0_ref[pl.ds(k0, t), :] += fold_lanes(ds0)
                dc1_ref[pl.ds(k0, t), :] += fold_lanes(ds1)
                ktv, aktv = kt_ref[:, pl.ds(k0, t)], akt_ref[:, pl.ds(k0, t)]
                return dq0 + _dot(jnp.where(lo_rows, ktv, aktv), ds0), dq1 + _dot(jnp.where(lo_rows, aktv, ktv), ds1)

            zq = jnp.zeros((128, t), F32)
            dq0, dq1 = lax.fori_loop(0, qi + 1, pass2, (zq, zq))
            dq_ref[pl.ds(q0, t), :] = jnp.where(lo_rows, dq0, dq1).T
            return 0

        lax.fori_loop(0, nq, q_loop, 0)
        if ns:
            pl.when(p == N_PAIRS - 1)(finish)

    pair = pl.BlockSpec((s, 128), lambda p: (0, p))
    pair_t = pl.BlockSpec((128, s), lambda p: (p, 0))
    out = jax.ShapeDtypeStruct((s, 1024), F32)
    outs = pl.pallas_call(
        body, name="fox_bwd", grid=(N_PAIRS,),
        in_specs=[pair, pair, pair, pair, pair_t, pair_t, pair, pl.BlockSpec((N_HEADS, s), lambda p: (0, 0)),
                  pl.BlockSpec((s, 128), lambda p: (0, 8 + p))] + [ANY] * ns,
        out_specs=[pl.BlockSpec((s, 128), lambda p: (0, p), pipeline_mode=once)] * 5 + [ANY] * ns,
        out_shape=[out] * 5 + [jax.ShapeDtypeStruct(p.shape, p.dtype) for p in parts],
        scratch_shapes=[pltpu.VMEM((2, nq, t, t), F32), pltpu.VMEM((2, nq, t, t), F32)] + (_scatter_scratch(ns) if ns else []),
        compiler_params=_params(("arbitrary",)),
    )(qn, kn, aq, ak, knt, akt, vb, lse, dmixed, *parts)
    return (*outs[:5], _keep_own_blocks(outs[5:], parts))


def fox_post(dqn, dkn, dc0, dc1, proj, smallp, qw, kw, sel, selt, fold_even, fold_odd, *, tm=256):
    s = proj.shape[0]
    nrow = s // tm

    def body(dqn_ref, dkn_ref, dc0_ref, dc1_ref, q_ref, k_ref, small_ref, sp_ref, qw_ref, kw_ref, sel_ref, selt_ref,
             fe_ref, fo_ref, dq_ref, dk_ref, dsmall_ref, gqw_ref, gkw_ref, gfb_ref, carry):
        step = pl.program_id(0)

        @pl.when(step == 0)
        def _():
            carry[...] = jnp.zeros_like(carry)

        def norm_bwd(x_ref, w_ref, dn, out_ref):
            x = x_ref[...]
            rf = _head_rstd(x, sel_ref, selt_ref)
            xh = x * rf
            g = dn * w_ref[...]
            mean_gx = _head_spread(_head_mean(g * xh, sel_ref, selt_ref), selt_ref)
            out_ref[...] = (rf * (g - xh * mean_gx)).astype(BF16)
            return jnp.sum(dn * xh, axis=0, keepdims=True)

        gqw = norm_bwd(q_ref, qw_ref, dqn_ref[...] * FOX_SCALE, dq_ref)
        gkw = norm_bwd(k_ref, kw_ref, dkn_ref[...], dk_ref)
        li = _lane_iota((tm, 128))
        f_lane = jnp.logical_and(li >= F_LANE, li < F_LANE + N_HEADS)
        dcum = -(_split3_dot(dc0_ref[...], fe_ref[...]) + _split3_dot(dc1_ref[...], fo_ref[...]))
        dlogf = _rev_cumsum_rows(dcum) + carry[...]
        carry[...] = dlogf[0:1, :]
        dfr = jnp.where(f_lane, dlogf * _sigmoid(-(small_ref[...] + sp_ref[3:4, :])), 0.0)
        dsmall_ref[...] = dfr
        gfb = jnp.sum(dfr, axis=0, keepdims=True)

        @pl.when(step == 0)
        def _():
            gqw_ref[...] = gqw
            gkw_ref[...] = gkw
            gfb_ref[...] = gfb

        @pl.when(step > 0)
        def _():
            gqw_ref[...] += gqw
            gkw_ref[...] += gkw
            gfb_ref[...] += gfb

    def rb(i):
        return nrow - 1 - i

    row = pl.BlockSpec((tm, 1024), lambda i: (rb(i), 0))
    vec = pl.BlockSpec((1, 1024), lambda i: (0, 0))
    fold = pl.BlockSpec((1024, 128), lambda i: (0, 0))
    return pl.pallas_call(
        body, name="fox_post", grid=(nrow,),
        in_specs=[row, row, row, row, pl.BlockSpec((tm, 1024), lambda i: (rb(i), Q_COL)),
                  pl.BlockSpec((tm, 1024), lambda i: (rb(i), K_COL)),
                  pl.BlockSpec((tm, 128), lambda i: (rb(i), SMALL_BLOCK)), pl.BlockSpec((8, 128), lambda i: (0, 0)), vec, vec,
                  fold, pl.BlockSpec((128, 1024), lambda i: (0, 0)), fold, fold],
        out_specs=[row, row, pl.BlockSpec((tm, 128), lambda i: (rb(i), 0)), vec, vec, pl.BlockSpec((1, 128), lambda i: (0, 0))],
        out_shape=[jax.ShapeDtypeStruct((s, 1024), BF16), jax.ShapeDtypeStruct((s, 1024), BF16),
                   jax.ShapeDtypeStruct((s, 128), F32), jax.ShapeDtypeStruct((1, 1024), F32),
                   jax.ShapeDtypeStruct((1, 1024), F32), jax.ShapeDtypeStruct((1, 128), F32)],
        scratch_shapes=[pltpu.VMEM((1, 128), F32)], compiler_params=_params(("arbitrary",)),
    )(dqn, dkn, dc0, dc1, proj, proj, proj, smallp, qw, kw, sel, selt, fold_even, fold_odd)


def local_step(x, target, wx, later_shards, ssd_cw8, ssd_cb, smallp, ssd_nw, qw_t, kw_t, sel, selt,
               norm_mix_w, norm_ffn_w, ffn_cw8, ffn_cb):
    h, h_t = rms_fwd(x, norm_mix_w, name="rms_mix_fwd")
    proj = matmul(h, wx, mode="nn", tm=1024, tn=PROJ_TILE, tk=1024, out_dtype=F32, name="mm_in_proj")
    y_ssd, y_ssd_t, ypre, states = ssd_fwd(proj, ssd_cw8, ssd_cb, smallp, ssd_nw)
    place_q, place_k, ones_q, ones_k, fold_even, fold_odd = fox_tables()
    qn, kn, aq, ak, vb, knt, akt, vt = fox_prep(proj, smallp, qw_t, kw_t, sel, selt, place_q, place_k, ones_q, ones_k)
    y_fox, y_fox_t, lse, (a_out, a_up, a_down) = fox_fwd(qn, kn, aq, ak, vt, shards=later_shards)
    w_out = a_out.reshape(2048, D_MODEL)
    w_down = a_down.reshape(D_FF, D_MODEL)
    s = x.shape[0]
    shard = lambda index: pl.BlockSpec((None, 1024, 1408), index)
    x1 = matmul(y_ssd, w_out, mode="nn", tm=1024, tn=1024, tk=1024, out_dtype=F32, name="mm_out_ssd", add=x)
    x1 = matmul(y_fox, w_out, mode="nn", tm=1024, tn=1024, tk=1024, out_dtype=F32, name="mm_out_fox", add=x1, b_koff=1)
    hf, hf_t = rms_fwd(x1, norm_ffn_w, name="rms_ffn_fwd")
    hu, act, act_t = up_ffn_fwd(hf, a_up, ffn_cw8, ffn_cb)
    y = matmul(act, w_down, mode="nn", tm=1024, tn=1024, tk=1408, out_dtype=F32, name="mm_down", add=x1)
    dy, sq = loss_head(y, target)

    dact = matmul(dy, w_down, mode="nt", tm=1024, tn=1408, tk=1024, out_dtype=F32, name="mm_dact")
    g_down = matmul(act_t, dy, mode="nn", tm=1408, tn=1024, tk=1024, out_dtype=BF16, name="mm_dw_down")
    dhu, gcw_g, gcw_v = ffn_mid_bwd(hu, dact, ffn_cw8, ffn_cb)
    dhf = matmul(dhu, a_up, mode="nt", tm=1024, tn=1024, tk=1408, out_dtype=F32, name="mm_dhf",
                 layout=dict(m=s, n=D_MODEL, k=2 * D_FF, a_spec=shard(lambda i, j, kk: (kk // 2, i, kk % 2)),
                             b_spec=shard(lambda i, j, kk: (kk, 0, 0))))
    g_up = matmul(hf_t, dhu, mode="nn", tm=1024, tn=1408, tk=1024, out_dtype=BF16, name="mm_dw_up",
                  layout=dict(m=D_MODEL, n=2 * D_FF, k=s, b_spec=shard(lambda i, j, kk: (j // 2, kk, j % 2)),
                              o_spec=shard(lambda i, j, kk: (j, i, 0)), out_shape=(4, D_MODEL, 1408)))
    dx1, g_norm_ffn, dmixed = rms_bwd_matmul(dhf, x1, norm_ffn_w, dy, w_out, name="rms_ffn_bwd_dmixed")
    g_out_a = matmul(y_ssd_t, dx1, mode="nn", tm=1024, tn=1024, tk=1024, out_dtype=BF16, name="mm_dw_out_ssd")
    g_out_b = matmul(y_fox_t, dx1, mode="nn", tm=1024, tn=1024, tk=1024, out_dtype=BF16, name="mm_dw_out_fox")
    early = [jnp.concatenate([g_out_a, g_out_b], axis=0).reshape(4, 512, D_MODEL), g_up, g_down.reshape(4, 704, D_MODEL)]
    dz, dxs, db, dc, dsmall_ssd, gcw_x, gcw_b, gcw_c, g_sp, g_ssd_nw, theirs = ssd_bwd(
        proj, ssd_cw8, ssd_cb, smallp, ssd_nw, ypre, states, dmixed, sel, swap=early)
    core = lax.axis_index("c").astype(jnp.int32).reshape(1)
    parts = [add_pair(a, b, core, name="add_pair_" + n, tr=ADAM_ROWS[n]) for a, b, n in zip(early, theirs, BIG_NAMES[1:])]
    dqn, dkn, dv, dc0, dc1, landed_early = fox_bwd(qn, kn, aq, ak, knt, akt, vb, lse, dmixed, parts=parts)
    dq, dk, dsmall_fox, g_qw, g_kw, g_fb = fox_post(dqn, dkn, dc0, dc1, proj, smallp, qw_t, kw_t, sel, selt,
                                                    fold_even, fold_odd)
    dproj = jnp.concatenate([dz, dxs, dq, dk, dv.astype(BF16), db, dc, (dsmall_ssd + dsmall_fox).astype(BF16)], axis=1)
    g_wx = matmul(h_t, dproj, mode="nn", tm=1024, tn=PROJ_TILE, tk=1024, out_dtype=BF16, name="mm_dw_in")
    g_in = _in_grad_shards(g_wx)
    part_in = add_pair(g_in, pair_swap_halves([g_in], name="pair_swap_w_in")[0], core, name="add_pair_w_in",
                       tr=ADAM_ROWS["w_in"])
    dh, landed_in = matmul(dproj, wx, mode="nt", tm=1024, tn=1024, tk=PROJ_TILE, out_dtype=F32, name="mm_dh",
                           scatter=[part_in])
    grad_x, g_norm_mix = rms_bwd(dh, x, norm_mix_w, dx1, name="rms_mix_bwd")
    return dict(
        sq=sq, grad_x=grad_x, landed=landed_in + landed_early,
        g_norm_mix=g_norm_mix, g_norm_ffn=g_norm_ffn, g_ssd_nw=g_ssd_nw,
        g_ssd_cw=jnp.concatenate([gcw_x, gcw_b, gcw_c], axis=1), g_sp=g_sp, g_fb=g_fb, g_qw=g_qw, g_kw=g_kw,
        g_ffn_cw=jnp.concatenate([gcw_g, gcw_v], axis=1))


def adamw(w, g, m, v, *, name, tr, allreduce=None):
    rows, cols = w.shape
    nsteps = rows // tr

    def body(*refs):
        if allreduce is None:
            w_ref, g_ref, m_ref, v_ref, d_ref, mo_ref, vo_ref = refs
        else:
            w_ref, g_ref, m_ref, v_ref, packed_ref, d_ref, mo_ref, vo_ref, summed_ref = refs[:9]
            start, finish = _allreduce_phases(packed_ref, summed_ref, *refs[9:])
            pl.when(pl.program_id(0) == 0)(start)
        gv = g_ref[...]
        mn = ADAM_B1 * m_ref[...] + (1.0 - ADAM_B1) * gv
        vn = ADAM_B2 * v_ref[...] + (1.0 - ADAM_B2) * (gv * gv)
        m_hat = mn / (1.0 - ADAM_B1 ** ADAM_STEP)
        v_hat = vn / (1.0 - ADAM_B2 ** ADAM_STEP)
        d_ref[...] = -ADAM_LR * (m_hat / (jnp.sqrt(v_hat) + ADAM_EPS) + ADAM_WD * w_ref[...])
        mo_ref[...] = mn
        vo_ref[...] = vn
        if allreduce is not None:
            pl.when(pl.program_id(0) == nsteps - 1)(finish)

    blk = pl.BlockSpec((tr, cols), lambda i: (i, 0))
    shp = jax.ShapeDtypeStruct((rows, cols), F32)
    if allreduce is None:
        return pl.pallas_call(
            body, name=name, grid=(nsteps,), in_specs=[blk] * 4, out_specs=[blk] * 3, out_shape=[shp] * 3,
            compiler_params=_params(("parallel",)),
        )(w, g, m, v)
    whole = pl.BlockSpec(memory_space=pltpu.VMEM)
    return pl.pallas_call(
        body, name=name, grid=(nsteps,), in_specs=[blk] * 4 + [whole], out_specs=[blk] * 3 + [whole],
        out_shape=[shp] * 3 + [jax.ShapeDtypeStruct(allreduce.shape, F32)],
        scratch_shapes=_allreduce_scratch(allreduce.shape[0]), compiler_params=_params(("arbitrary",)),
    )(w, g, m, v, allreduce)


def add_pair(full, theirs, core, *, name, tr):
    _, rows, cols = theirs.shape
    nblk = rows // tr

    def body(c_ref, a_ref, b_ref, o_ref):
        o_ref[...] = (a_ref[...].astype(F32) + b_ref[...].astype(F32)).astype(BF16)

    blk = pl.BlockSpec((1, tr, cols), lambda j, i, c: (j, i, 0))
    grid_spec = pltpu.PrefetchScalarGridSpec(
        num_scalar_prefetch=1, grid=(4, nblk),
        in_specs=[pl.BlockSpec((1, tr, cols), lambda j, i, c: (j, c[0] * nblk + i, 0)), blk], out_specs=blk)
    return pl.pallas_call(
        body, name=name, grid_spec=grid_spec, out_shape=jax.ShapeDtypeStruct(theirs.shape, BF16),
        compiler_params=_params(("parallel", "parallel")),
    )(core, full, theirs)


def sum_chips(parts, core, *, name, tr):
    _, rows, cols = parts.shape
    nblk = rows // tr

    def body(c_ref, p_ref, o_ref):
        acc = p_ref[0].astype(F32)
        for k in range(1, 4):
            acc = acc + p_ref[k].astype(F32)
        o_ref[...] = acc

    grid_spec = pltpu.PrefetchScalarGridSpec(
        num_scalar_prefetch=1, grid=(nblk,), in_specs=[pl.BlockSpec((4, tr, cols), lambda i, c: (0, i, 0))],
        out_specs=pl.BlockSpec((tr, cols), lambda i, c: (c[0] * nblk + i, 0)))
    return pl.pallas_call(
        body, name=name, grid_spec=grid_spec, out_shape=jax.ShapeDtypeStruct((2 * rows, cols), F32),
        compiler_params=_params(("parallel",)),
    )(core, parts)


ANY = pl.BlockSpec(memory_space=pl.ANY)


def _place():
    x, y, c = lax.axis_index("x"), lax.axis_index("y"), lax.axis_index("c")
    chips = [(1 - x, y), (x, 1 - y), (1 - x, 1 - y)]
    return x, y, c, chips


def _chunks(rows):
    size = next((c for c in (128, 176, 64, 32, 16, 8) if rows % c == 0), rows)
    return [(r, size) for r in range(0, rows, size)]


def gather_weights(shards):
    n = len(shards)

    def body(*refs):
        start, forward, finish = _gather_phases(refs[:n], refs[n:2 * n], *refs[2 * n:])
        start()
        forward()
        finish()

    gathered = pl.pallas_call(
        body, name="gather_weights", in_specs=[ANY] * n, out_specs=[ANY] * n,
        out_shape=_gather_out_shapes(shards), scratch_shapes=_gather_scratch(n),
    )(*shards)
    return gathered


def _gather_out_shapes(shards):
    return [jax.ShapeDtypeStruct((4,) + s.shape, s.dtype) for s in shards]


def _gather_scratch(n):
    return [pltpu.SemaphoreType.DMA((n, 7)), pltpu.SemaphoreType.DMA((n, 7))]


def _gather_phases(ins, outs, send_sems, recv_sems):
    n = len(ins)
    x, y, c, chips = _place()
    me = 2 * x + y
    sibling = (x, y, 1 - c)
    blks = [2 * cx + cy for cx, cy in chips]

    def half(a, blk, r=0, nr=None):
        rows = ins[a].shape[0] // 2
        return outs[a].at[blk, pl.ds(c * rows + r, rows if nr is None else nr), :]

    def to_chip(a, t, r=0, nr=None):
        rows = ins[a].shape[0] // 2
        return pltpu.make_async_remote_copy(
            src_ref=ins[a].at[pl.ds(c * rows + r, rows if nr is None else nr), :], dst_ref=half(a, me, r, nr),
            send_sem=send_sems.at[a, t], recv_sem=recv_sems.at[a, t], device_id=(*chips[t], c), device_id_type=MESH)

    def from_chip(a, t):
        return pltpu.make_async_remote_copy(
            src_ref=half(a, blks[t]), dst_ref=half(a, blks[t]), send_sem=send_sems.at[a, t], recv_sem=recv_sems.at[a, t],
            device_id=(*chips[t], c), device_id_type=MESH)

    def to_sibling(a, t, r=0, nr=None):
        return pltpu.make_async_remote_copy(
            src_ref=half(a, blks[t], r, nr), dst_ref=half(a, blks[t], r, nr), send_sem=send_sems.at[a, 3 + t],
            recv_sem=recv_sems.at[a, 3 + t], device_id=sibling, device_id_type=MESH)

    def from_sibling(a, t):
        rows = ins[a].shape[0] // 2
        dst = outs[a].at[blks[t], pl.ds((1 - c) * rows, rows), :]
        return pltpu.make_async_remote_copy(
            src_ref=dst, dst_ref=dst, send_sem=send_sems.at[a, 3 + t], recv_sem=recv_sems.at[a, 3 + t],
            device_id=sibling, device_id_type=MESH)

    def own(a, r=0, nr=None):
        return pltpu.make_async_remote_copy(
            src_ref=ins[a].at[pl.ds(r, ins[a].shape[0] if nr is None else nr), :],
            dst_ref=outs[a].at[me, pl.ds(r, ins[a].shape[0] if nr is None else nr), :],
            send_sem=send_sems.at[a, 6], recv_sem=recv_sems.at[a, 6], device_id=sibling, device_id_type=MESH)

    def start():
        for a in range(n):
            for t in range(3):
                for r, nr in _chunks(ins[a].shape[0] // 2):
                    to_chip(a, t, r, nr).start()
            for r, nr in _chunks(ins[a].shape[0]):
                own(a, r, nr).start()

    def forward():
        for a in range(n):
            for t in range(3):
                from_chip(a, t).wait_recv()
                for r, nr in _chunks(ins[a].shape[0] // 2):
                    to_sibling(a, t, r, nr).start()

    def finish():
        for a in range(n):
            for t in range(3):
                from_sibling(a, t).wait_recv()
        for a in range(n):
            for t in range(3):
                to_chip(a, t).wait_send()
                to_sibling(a, t).wait_send()
            own(a).wait()

    return start, forward, finish


def pair_swap_halves(grads, *, name):
    n = len(grads)

    def body(*refs):
        start, finish = _pair_swap_phases(refs[:n], refs[n:2 * n], *refs[2 * n:])
        start()
        finish()

    return pl.pallas_call(
        body, name=name, in_specs=[ANY] * n, out_specs=[ANY] * n, out_shape=_pair_swap_out_shapes(grads),
        scratch_shapes=_pair_swap_scratch(n),
    )(*grads)


def _pair_swap_out_shapes(grads):
    return [jax.ShapeDtypeStruct((4, g.shape[1] // 2, g.shape[2]), g.dtype) for g in grads]


def _pair_swap_scratch(n):
    return [pltpu.SemaphoreType.DMA((n,)), pltpu.SemaphoreType.DMA((n,))]


def _pair_swap_phases(ins, theirs, send_sems, recv_sems):
    n = len(ins)
    x, y, c, _ = _place()
    sibling = (x, y, 1 - c)

    def start():
        for a in range(n):
            rows = ins[a].shape[1] // 2
            for j in range(4):
                for r, nr in _chunks(rows):
                    pltpu.make_async_remote_copy(
                        src_ref=ins[a].at[j, pl.ds((1 - c) * rows + r, nr), :], dst_ref=theirs[a].at[j, pl.ds(r, nr), :],
                        send_sem=send_sems.at[a], recv_sem=recv_sems.at[a], device_id=sibling, device_id_type=MESH).start()

    def finish():
        for a in range(n):
            pltpu.make_async_remote_copy(src_ref=theirs[a], dst_ref=theirs[a], send_sem=send_sems.at[a],
                                         recv_sem=recv_sems.at[a], device_id=sibling, device_id_type=MESH).wait()

    return start, finish


def _scatter_scratch(n):
    return [pltpu.SemaphoreType.DMA((n, 3)), pltpu.SemaphoreType.DMA((n, 3))]


def _keep_own_blocks(landed, parts):
    if not parts:
        return []
    chip = 2 * lax.axis_index("x") + lax.axis_index("y")
    return [lax.dynamic_update_slice(l, lax.dynamic_slice_in_dim(p, chip, 1, axis=0), (chip, 0, 0))
            for l, p in zip(landed, parts)]


def _scatter_phases(ins, outs, send_sems, recv_sems):
    n = len(ins)
    x, y, c, chips = _place()
    me = 2 * x + y
    blks = [2 * cx + cy for cx, cy in chips]

    def start():
        for a in range(n):
            for r, nr in _chunks(ins[a].shape[1]):
                for t in range(3):
                    pltpu.make_async_remote_copy(
                        src_ref=ins[a].at[blks[t], pl.ds(r, nr), :], dst_ref=outs[a].at[me, pl.ds(r, nr), :],
                        send_sem=send_sems.at[a, t], recv_sem=recv_sems.at[a, t],
                        device_id=(*chips[t], c), device_id_type=MESH).start()

    def finish():
        for a in range(n):
            for t in range(3):
                pltpu.make_async_remote_copy(
                    src_ref=outs[a].at[blks[t]], dst_ref=outs[a].at[blks[t]], send_sem=send_sems.at[a, t],
                    recv_sem=recv_sems.at[a, t], device_id=(*chips[t], c), device_id_type=MESH).wait()

    return start, finish


def pair_join_halves(bufs):
    n = len(bufs)

    def body(*refs):
        outs = refs[n:2 * n]
        send_sems, recv_sems = refs[2 * n:]
        x, y, c, _ = _place()
        sibling = (x, y, 1 - c)
        for a in range(n):
            rows = outs[a].shape[0] // 2
            for r, nr in _chunks(rows):
                mine = outs[a].at[pl.ds(c * rows + r, nr), :]
                pltpu.make_async_remote_copy(src_ref=mine, dst_ref=mine, send_sem=send_sems.at[a], recv_sem=recv_sems.at[a],
                                             device_id=sibling, device_id_type=MESH).start()
        for a in range(n):
            rows = outs[a].shape[0] // 2
            pltpu.make_async_remote_copy(
                src_ref=outs[a].at[pl.ds(c * rows, rows), :], dst_ref=outs[a].at[pl.ds((1 - c) * rows, rows), :],
                send_sem=send_sems.at[a], recv_sem=recv_sems.at[a], device_id=sibling, device_id_type=MESH).wait()

    return pl.pallas_call(
        body, name="pair_join_halves", in_specs=[ANY] * n, out_specs=[ANY] * n,
        out_shape=[jax.ShapeDtypeStruct(b.shape, b.dtype) for b in bufs], input_output_aliases={a: a for a in range(n)},
        scratch_shapes=[pltpu.SemaphoreType.DMA((n,)), pltpu.SemaphoreType.DMA((n,))],
    )(*bufs)


def _allreduce_scratch(rows):
    return [pltpu.VMEM((8, rows, 128), F32), pltpu.SemaphoreType.DMA((7,)), pltpu.SemaphoreType.DMA((7,))]


def _allreduce_phases(in_ref, out_ref, gathered, send_sems, recv_sems):
    x, y, c, _ = _place()
    me = 4 * x + 2 * y + c
    flips = [(fx, fy, fc) for fx in (0, 1) for fy in (0, 1) for fc in (0, 1)][1:]
    peers = [((1 - x) if fx else x, (1 - y) if fy else y, (1 - c) if fc else c) for fx, fy, fc in flips]

    def send(t):
        return pltpu.make_async_remote_copy(
            src_ref=in_ref, dst_ref=gathered.at[me], send_sem=send_sems.at[t], recv_sem=recv_sems.at[t],
            device_id=peers[t], device_id_type=MESH)

    def start():
        gathered[me] = in_ref[...]
        for t in range(7):
            send(t).start()

    def finish():
        for t, (px, py, pc) in enumerate(peers):
            slot = gathered.at[4 * px + 2 * py + pc]
            pltpu.make_async_remote_copy(
                src_ref=slot, dst_ref=slot, send_sem=send_sems.at[t], recv_sem=recv_sems.at[t],
                device_id=(px, py, pc), device_id_type=MESH).wait_recv()
        for t in range(7):
            send(t).wait_send()
        acc = gathered[0]
        for k in range(1, 8):
            acc = acc + gathered[k]
        out_ref[...] = acc

    return start, finish


SMALL_NAMES = ("norm_mix_w", "ssd_conv_w", "ssd_conv_b", "ssd_dt_bias", "ssd_a_log", "ssd_d", "ssd_norm_w", "fox_f_bias",
               "fox_q_norm_w", "fox_k_norm_w", "norm_ffn_w", "ffn_conv_w", "ffn_conv_b")
BIG_NAMES = ("w_in", "w_out", "w_up", "w_down")
WEIGHT_ORDER = ("norm_mix_w", "w_in", "ssd_conv_w", "ssd_conv_b", "ssd_dt_bias", "ssd_a_log", "ssd_d", "ssd_norm_w",
                "fox_f_bias", "fox_q_norm_w", "fox_k_norm_w", "w_out", "norm_ffn_w", "w_up", "ffn_conv_w", "ffn_conv_b", "w_down")
ADAM_ROWS = {"w_in": 256, "w_out": 256, "w_up": 256, "w_down": 176}


def _pack(arrays):
    pieces = []
    for a in arrays:
        flat = a.reshape(-1).astype(F32)
        pieces += [flat, jnp.zeros(((-flat.shape[0]) % 1024,), F32)]
    return jnp.concatenate(pieces).reshape(-1, 128)


def _unpack(packed, shapes):
    out, r = [], 0
    for shp in shapes:
        size = 1
        for d in shp:
            size *= d
        nrow = 8 * (-(-size // 1024))
        out.append(packed[r:r + nrow].reshape(-1)[:size].reshape(shp))
        r += nrow
    return out


IN_SHARD = IN_COLS // 4
IN_SEGMENTS = ((0, 2048, 0), (2048, 2560, 5120), (2560, 2576, MAIN_COLS), (2576, 5648, 2048), (5648, 5664, MAIN_COLS + F_LANE))


def _in_cols(shards, lo, hi):
    out = []
    for j in range(4):
        a, b = max(lo, IN_SHARD * j), min(hi, IN_SHARD * (j + 1))
        if a < b:
            out.append(shards[j][:, a - IN_SHARD * j:b - IN_SHARD * j])
    return out


def _in_grad_shards(g):
    shards = []
    for j in range(4):
        pieces = []
        for lo, hi, at in IN_SEGMENTS:
            a, b = max(lo, IN_SHARD * j), min(hi, IN_SHARD * (j + 1))
            if a < b:
                pieces.append(g[:, at + a - lo:at + b - lo])
        shards.append(jnp.concatenate(pieces, axis=1))
    return jnp.stack(shards)


def _pad_rows(a, rows):
    return jnp.pad(a, ((0, rows - a.shape[0]), (0, 0)))


def kernel(x, norm_mix_w, w_in, ssd_conv_w, ssd_conv_b, ssd_dt_bias, ssd_a_log, ssd_d, ssd_norm_w, fox_f_bias, fox_q_norm_w, fox_k_norm_w, w_out, norm_ffn_w, w_up, ffn_conv_w, ffn_conv_b, w_down, loss_target, m_norm_mix_w, m_w_in, m_ssd_conv_w, m_ssd_conv_b, m_ssd_dt_bias, m_ssd_a_log, m_ssd_d, m_ssd_norm_w, m_fox_f_bias, m_fox_q_norm_w, m_fox_k_norm_w, m_w_out, m_norm_ffn_w, m_w_up, m_ffn_conv_w, m_ffn_conv_b, m_w_down, v_norm_mix_w, v_w_in, v_ssd_conv_w, v_ssd_conv_b, v_ssd_dt_bias, v_ssd_a_log, v_ssd_d, v_ssd_norm_w, v_fox_f_bias, v_fox_q_norm_w, v_fox_k_norm_w, v_w_out, v_norm_ffn_w, v_w_up, v_ffn_conv_w, v_ffn_conv_b, v_w_down):
    w = dict(norm_mix_w=norm_mix_w, w_in=w_in, ssd_conv_w=ssd_conv_w, ssd_conv_b=ssd_conv_b, ssd_dt_bias=ssd_dt_bias,
             ssd_a_log=ssd_a_log, ssd_d=ssd_d, ssd_norm_w=ssd_norm_w, fox_f_bias=fox_f_bias, fox_q_norm_w=fox_q_norm_w,
             fox_k_norm_w=fox_k_norm_w, w_out=w_out, norm_ffn_w=norm_ffn_w, w_up=w_up, ffn_conv_w=ffn_conv_w,
             ffn_conv_b=ffn_conv_b, w_down=w_down)
    m = dict(norm_mix_w=m_norm_mix_w, w_in=m_w_in, ssd_conv_w=m_ssd_conv_w, ssd_conv_b=m_ssd_conv_b, ssd_dt_bias=m_ssd_dt_bias,
             ssd_a_log=m_ssd_a_log, ssd_d=m_ssd_d, ssd_norm_w=m_ssd_norm_w, fox_f_bias=m_fox_f_bias, fox_q_norm_w=m_fox_q_norm_w,
             fox_k_norm_w=m_fox_k_norm_w, w_out=m_w_out, norm_ffn_w=m_norm_ffn_w, w_up=m_w_up, ffn_conv_w=m_ffn_conv_w,
             ffn_conv_b=m_ffn_conv_b, w_down=m_w_down)
    v = dict(norm_mix_w=v_norm_mix_w, w_in=v_w_in, ssd_conv_w=v_ssd_conv_w, ssd_conv_b=v_ssd_conv_b, ssd_dt_bias=v_ssd_dt_bias,
             ssd_a_log=v_ssd_a_log, ssd_d=v_ssd_d, ssd_norm_w=v_ssd_norm_w, fox_f_bias=v_fox_f_bias, fox_q_norm_w=v_fox_q_norm_w,
             fox_k_norm_w=v_fox_k_norm_w, w_out=v_w_out, norm_ffn_w=v_norm_ffn_w, w_up=v_w_up, ffn_conv_w=v_ffn_conv_w,
             ffn_conv_b=v_ffn_conv_b, w_down=v_w_down)
    chip = 2 * lax.axis_index("x") + lax.axis_index("y")

    a_in, a_scw, a_fcw = gather_weights([w_in[0].astype(BF16), _pad_rows(ssd_conv_w[0], 16), _pad_rows(ffn_conv_w[0], 16)])
    later_shards = [w_out[0].astype(BF16), w_up[0].astype(BF16), w_down[0].astype(BF16)]
    wx = jnp.concatenate([p for lo, hi, _ in sorted(IN_SEGMENTS, key=lambda seg: seg[2]) for p in _in_cols(a_in, lo, hi)]
                         + [jnp.zeros((D_MODEL, PROJ_COLS - IN_COLS), BF16)], axis=1)
    ssd_cw8 = a_scw.transpose(1, 0, 2).reshape(16, 1536)[:8]
    ffn_cw8 = a_fcw.transpose(1, 0, 2).reshape(16, 2 * D_FF)[:8]
    gap = lambda n: jnp.zeros((n,), F32)
    smallp = jnp.concatenate([ssd_dt_bias[0], gap(112), ssd_a_log[0], gap(112), ssd_d[0], gap(112),
                              gap(F_LANE), fox_f_bias[0], gap(128 - F_LANE - N_HEADS), gap(4 * 128)]).reshape(8, 128)
    qw_t = jnp.tile(fox_q_norm_w[0], N_HEADS)[None]
    kw_t = jnp.tile(fox_k_norm_w[0], N_HEADS)[None]
    sel = jnp.asarray((np.arange(1024)[:, None] // HEAD_DIM == np.arange(128)[None, :]).astype(np.float32), BF16)

    res = local_step(x[0], loss_target[0], wx, later_shards, ssd_cw8, ssd_conv_b, smallp, ssd_norm_w, qw_t, kw_t,
                     sel, sel.T, norm_mix_w, norm_ffn_w, ffn_cw8, ffn_conv_b)

    full_shapes = [(1, 1024), (1, 4, 1536), (1, 1536), (1, 16), (1, 16), (1, 16), (1, 1024), (1, 16), (1, 64), (1, 64),
                   (1, 1024), (1, 3, 2 * D_FF), (1, 2 * D_FF), (1,)]
    local_small = [res["g_norm_mix"], res["g_ssd_cw"][:4], res["g_ssd_cw"][4], res["g_sp"][0, :16], res["g_sp"][1, :16],
                   res["g_sp"][2, :16], res["g_ssd_nw"], res["g_fb"][0, F_LANE:F_LANE + 16],
                   res["g_qw"].reshape(N_HEADS, HEAD_DIM).sum(0), res["g_kw"].reshape(N_HEADS, HEAD_DIM).sum(0),
                   res["g_norm_ffn"], res["g_ffn_cw"][:3], res["g_ffn_cw"][3], jnp.sum(res["sq"])]
    landed = res["landed"]
    core = lax.axis_index("c").astype(jnp.int32).reshape(1)
    halves = [sum_chips(p, core, name="sum_chips_" + n, tr=ADAM_ROWS[n]) for p, n in zip(landed, BIG_NAMES)]
    g_big = dict(zip(BIG_NAMES, pair_join_halves(halves)))

    grads, deltas, new_m, new_v = {}, {}, {}, {}
    for n in BIG_NAMES:
        out = adamw(w[n][0], g_big[n], m[n][0], v[n][0], name="adamw_" + n, tr=ADAM_ROWS[n],
                    allreduce=_pack(local_small) if n == BIG_NAMES[0] else None)
        if n == BIG_NAMES[0]:
            summed = _unpack(out[3], full_shapes)
        d, mn, vn = out[:3]
        grads[n], deltas[n], new_m[n], new_v[n] = g_big[n][None], d[None], mn[None], vn[None]
    loss = (0.5 / D_MODEL) * summed[-1][0]
    g_small = dict(zip(SMALL_NAMES, summed[:-1]))
    g_small["ssd_conv_w"] = lax.dynamic_slice(g_small["ssd_conv_w"], (0, 0, 384 * chip), (1, 4, 384))
    g_small["ffn_conv_w"] = lax.dynamic_slice(g_small["ffn_conv_w"], (0, 0, 1408 * chip), (1, 3, 1408))
    shapes = [w[n].shape for n in SMALL_NAMES]
    packed_w = _pack([w[n] for n in SMALL_NAMES])
    d, mn, vn = adamw(packed_w, _pack([g_small[n] for n in SMALL_NAMES]), _pack([m[n] for n in SMALL_NAMES]),
                      _pack([v[n] for n in SMALL_NAMES]), name="adamw_small", tr=packed_w.shape[0])
    for n, dd, mm, vv in zip(SMALL_NAMES, _unpack(d, shapes), _unpack(mn, shapes), _unpack(vn, shapes)):
        grads[n], deltas[n], new_m[n], new_v[n] = g_small[n].reshape(w[n].shape), dd, mm, vv
    return (loss, res["grad_x"][None], *[grads[n] for n in WEIGHT_ORDER], *[deltas[n] for n in WEIGHT_ORDER],
            *[new_m[n] for n in WEIGHT_ORDER], *[new_v[n] for n in WEIGHT_ORDER])
```

```python
import functools

import jax
import jax.numpy as jnp
import numpy as np
from jax import lax
from jax.experimental import pallas as pl
from jax.experimental.pallas import tpu as pltpu

F32 = jnp.float32
BF16 = jnp.bfloat16
MESH = pl.DeviceIdType.MESH

D_MODEL = 1024
HEAD_DIM = 64
N_HEADS = 16
N_PAIRS = N_HEADS // 2
SSD_CHUNK = 128
SSD_STATE = 128
SSD_CONV = 4
D_FF = 2816
FFN_CONV = 3
NORM_EPS = 1e-6
MAIN_COLS = 5632
SMALL_COLS = 128
PROJ_COLS = MAIN_COLS + SMALL_COLS
SMALL_BLOCK = MAIN_COLS // SMALL_COLS
PROJ_TILE = 1152
F_LANE = 16
IN_COLS = 5664

ADAM_LR = 0.001
ADAM_B1 = 0.9
ADAM_B2 = 0.999
ADAM_EPS = 1e-08
ADAM_WD = 0.01
ADAM_STEP = 10

VMEM_LIMIT_V7X = 56 * 1024 * 1024
NEG_BIG = -1e30


def _params(sem=None):
    return pltpu.CompilerParams(dimension_semantics=sem, vmem_limit_bytes=VMEM_LIMIT_V7X)


def _sigmoid(x):
    return 1.0 / (1.0 + jnp.exp(-x))


def _silu_and_grad(x):
    s = _sigmoid(x)
    return x * s, s * (1.0 + x * (1.0 - s))


def _shift_down(v, j):
    return v if j == 0 else pltpu.roll(v, j, 0)


def _shift_up(v, j):
    return v if j == 0 else pltpu.roll(v, v.shape[0] - j, 0)


def _row_iota(shape):
    return lax.broadcasted_iota(jnp.int32, shape, 0)


def _lane_iota(shape):
    return lax.broadcasted_iota(jnp.int32, shape, 1)


def _dot(a, b, mode="nn"):
    dims = {"nn": (((1,), (0,)), ((), ())), "nt": (((1,), (1,)), ((), ())), "tn": (((0,), (0,)), ((), ()))}[mode]
    return lax.dot_general(a.astype(BF16), b.astype(BF16), dims, preferred_element_type=F32)


def _dot_f32(a, b):
    return jnp.dot(a, b, precision=lax.Precision.HIGHEST, preferred_element_type=F32)


def matmul(a, b, *, mode, tm, tn, tk, out_dtype, name, add=None, b_koff=0, scatter=(), layout=None):
    layout = layout or {}
    if layout:
        m, n, k = layout["m"], layout["n"], layout["k"]
    else:
        (m, k), n = a.shape, (b.shape[1] if mode == "nn" else b.shape[0])
    assert m % tm == 0 and n % tn == 0 and k % tk == 0, (name, m, n, k, tm, tn, tk)
    nk = k // tk
    grid = (m // tm, n // tn, nk)
    a_spec = layout.get("a_spec") or pl.BlockSpec((tm, tk), lambda i, j, kk: (i, kk))
    b_spec = layout.get("b_spec") or (pl.BlockSpec((tn, tk), lambda i, j, kk: (j, kk + b_koff)) if mode == "nt"
                                      else pl.BlockSpec((tk, tn), lambda i, j, kk: (kk + b_koff, j)))
    o_spec = layout.get("o_spec") or pl.BlockSpec((tm, tn), lambda i, j, kk: (i, j))
    out_struct = jax.ShapeDtypeStruct(layout.get("out_shape", (m, n)), out_dtype)
    has_add = add is not None
    n_in = 3 if has_add else 2
    ns = len(scatter)

    def body(*refs):
        a_ref, b_ref = refs[:2]
        add_ref = refs[2] if has_add else None
        o_ref, acc_ref = refs[n_in + ns], refs[n_in + 2 * ns + 1]
        kk = pl.program_id(2)
        if ns:
            step = (pl.program_id(0) * grid[1] + pl.program_id(1)) * grid[2] + kk
            start, finish_copies = _scatter_phases(refs[n_in:n_in + ns], refs[n_in + ns + 1:n_in + 2 * ns + 1],
                                                   *refs[n_in + 2 * ns + 2:])
            pl.when(step == 0)(start)
        part = _dot(a_ref[...], b_ref[...], mode)

        def finish(total):
            if has_add:
                total = total + add_ref[...]
            o_ref[...] = total.astype(out_dtype)

        if nk == 1:
            finish(part)
        else:
            @pl.when(kk == 0)
            def _():
                acc_ref[...] = part

            @pl.when(jnp.logical_and(kk > 0, kk < nk - 1))
            def _():
                acc_ref[...] += part

            @pl.when(kk == nk - 1)
            def _():
                finish(acc_ref[...] + part)

        if ns:
            pl.when(step == grid[0] * grid[1] * grid[2] - 1)(finish_copies)

    in_specs = [a_spec, b_spec] + ([o_spec] if has_add else [])
    args = (a, b) + ((add,) if has_add else ())
    acc = pltpu.VMEM((tm, tn) if nk > 1 else (8, 128), F32)
    if not ns:
        return pl.pallas_call(
            body, name=name, grid=grid, in_specs=in_specs, out_specs=o_spec, out_shape=out_struct,
            scratch_shapes=[acc], compiler_params=_params(("parallel", "parallel", "arbitrary")),
        )(*args)
    outs = pl.pallas_call(
        body, name=name, grid=grid, in_specs=in_specs + [ANY] * ns, out_specs=[o_spec] + [ANY] * ns,
        out_shape=[out_struct] + [jax.ShapeDtypeStruct(p.shape, p.dtype) for p in scatter],
        scratch_shapes=[acc] + _scatter_scratch(ns), compiler_params=_params(("arbitrary", "arbitrary", "arbitrary")),
    )(*args, *scatter)
    return outs[0], _keep_own_blocks(outs[1:], scatter)


def rms_fwd(x, w, *, name, tm=1024):
    s, d = x.shape

    def body(x_ref, w_ref, h_ref, ht_ref):
        xv = x_ref[...]
        r = lax.rsqrt(jnp.mean(xv * xv, axis=-1, keepdims=True) + NORM_EPS)
        h = (xv * r) * w_ref[...]
        h_ref[...] = h.astype(BF16)
        ht_ref[...] = h.T.astype(BF16)

    return pl.pallas_call(
        body, name=name, grid=(s // tm,),
        in_specs=[pl.BlockSpec((tm, d), lambda i: (i, 0)), pl.BlockSpec((1, d), lambda i: (0, 0))],
        out_specs=[pl.BlockSpec((tm, d), lambda i: (i, 0)), pl.BlockSpec((d, tm), lambda i: (0, i))],
        out_shape=[jax.ShapeDtypeStruct((s, d), BF16), jax.ShapeDtypeStruct((d, s), BF16)],
        compiler_params=_params(("parallel",)),
    )(x, w)


def rms_bwd(dh, x, w, resid, *, name, tm=1024):
    s, d = x.shape

    def body(dh_ref, x_ref, w_ref, res_ref, dx_ref, dw_ref):
        xv = x_ref[...]
        dhv = dh_ref[...]
        r = lax.rsqrt(jnp.mean(xv * xv, axis=-1, keepdims=True) + NORM_EPS)
        xh = xv * r
        g = dhv * w_ref[...]
        dx_ref[...] = res_ref[...] + r * (g - xh * jnp.mean(g * xh, axis=-1, keepdims=True))
        part = jnp.sum(dhv * xh, axis=0, keepdims=True)

        @pl.when(pl.program_id(0) == 0)
        def _():
            dw_ref[...] = part

        @pl.when(pl.program_id(0) > 0)
        def _():
            dw_ref[...] += part

    row = pl.BlockSpec((tm, d), lambda i: (i, 0))
    vec = pl.BlockSpec((1, d), lambda i: (0, 0))
    return pl.pallas_call(
        body, name=name, grid=(s // tm,), in_specs=[row, row, vec, row], out_specs=[row, vec],
        out_shape=[jax.ShapeDtypeStruct((s, d), F32), jax.ShapeDtypeStruct((1, d), F32)],
        compiler_params=_params(("arbitrary",)),
    )(dh, x, w, resid)


def out_proj_rms_fwd(y_ssd, y_fox, w_out, x, norm_w, *, tm=512):
    s, d = x.shape

    def body(ys_ref, yf_ref, w_ref, x_ref, nw_ref, x1_ref, h_ref, ht_ref):
        for r in range(0, tm, UP_ROWS):
            rows = slice(r, r + UP_ROWS)
            x1 = x_ref[rows, :] + _dot(ys_ref[rows, :], w_ref[0:d, :]) + _dot(yf_ref[rows, :], w_ref[d:2 * d, :])
            x1_ref[rows, :] = x1
            rstd = lax.rsqrt(jnp.mean(x1 * x1, axis=-1, keepdims=True) + NORM_EPS)
            h = (x1 * rstd) * nw_ref[...]
            h_ref[rows, :] = h.astype(BF16)
            ht_ref[:, rows] = h.T.astype(BF16)

    row = pl.BlockSpec((tm, d), lambda i: (i, 0))
    return pl.pallas_call(
        body, name="out_proj_rms_fwd", grid=(s // tm,),
        in_specs=[row, row, pl.BlockSpec((2 * d, d), lambda i: (0, 0)), row, pl.BlockSpec((1, d), lambda i: (0, 0))],
        out_specs=[row, row, pl.BlockSpec((d, tm), lambda i: (0, i))],
        out_shape=[jax.ShapeDtypeStruct((s, d), F32), jax.ShapeDtypeStruct((s, d), BF16), jax.ShapeDtypeStruct((d, s), BF16)],
        compiler_params=_params(("parallel",)),
    )(y_ssd, y_fox, w_out, x, norm_w)


def rms_bwd_matmul(dh, x, w, resid, b, *, name, tm=512):
    s, d = x.shape
    n = b.shape[0]

    def body(dh_ref, x_ref, w_ref, res_ref, b_ref, dx_ref, dw_ref, prod_ref):
        part = jnp.zeros((1, d), F32)
        for r in range(0, tm, UP_ROWS):
            rows = slice(r, r + UP_ROWS)
            xv, dhv = x_ref[rows, :], dh_ref[rows, :]
            rstd = lax.rsqrt(jnp.mean(xv * xv, axis=-1, keepdims=True) + NORM_EPS)
            xh = xv * rstd
            g = dhv * w_ref[...]
            dx = res_ref[rows, :] + rstd * (g - xh * jnp.mean(g * xh, axis=-1, keepdims=True))
            dx_ref[rows, :] = dx
            prod_ref[rows, :] = _dot(dx, b_ref[...], "nt")
            part = part + jnp.sum(dhv * xh, axis=0, keepdims=True)

        @pl.when(pl.program_id(0) == 0)
        def _():
            dw_ref[...] = part

        @pl.when(pl.program_id(0) > 0)
        def _():
            dw_ref[...] += part

    row = pl.BlockSpec((tm, d), lambda i: (i, 0))
    vec = pl.BlockSpec((1, d), lambda i: (0, 0))
    return pl.pallas_call(
        body, name=name, grid=(s // tm,), in_specs=[row, row, vec, row, pl.BlockSpec((n, d), lambda i: (0, 0))],
        out_specs=[row, vec, pl.BlockSpec((tm, n), lambda i: (i, 0))],
        out_shape=[jax.ShapeDtypeStruct((s, d), F32), jax.ShapeDtypeStruct((1, d), F32), jax.ShapeDtypeStruct((s, n), F32)],
        compiler_params=_params(("arbitrary",)),
    )(dh, x, w, resid, b)


def loss_head(y, target, *, tm=1024):
    s, d = y.shape

    def body(y_ref, t_ref, dy_ref, sq_ref):
        e = y_ref[...] - t_ref[...]
        dy_ref[...] = e / float(d)
        part = jnp.sum(e * e, axis=0, keepdims=True)

        @pl.when(pl.program_id(0) == 0)
        def _():
            sq_ref[...] = part

        @pl.when(pl.program_id(0) > 0)
        def _():
            sq_ref[...] += part

    row = pl.BlockSpec((tm, d), lambda i: (i, 0))
    vec = pl.BlockSpec((1, d), lambda i: (0, 0))
    return pl.pallas_call(
        body, name="loss_head", grid=(s // tm,), in_specs=[row, row], out_specs=[row, vec],
        out_shape=[jax.ShapeDtypeStruct((s, d), F32), jax.ShapeDtypeStruct((1, d), F32)],
        compiler_params=_params(("arbitrary",)),
    )(y, target)


def _row_shifts(ext, k_taps):
    return [_shift_down(ext, j) for j in range(k_taps)]


def _conv_rows(shifts, w):
    k_taps = len(shifts)
    acc = w[k_taps - 1:k_taps, :] * shifts[0]
    for k in range(k_taps - 1):
        acc = acc + w[k:k + 1, :] * shifts[k_taps - 1 - k]
    return acc


def _conv_weight_grad(dcur, shifts, rows, width):
    k_taps = len(shifts)
    out = [jnp.sum(dcur * shifts[k_taps - 1 - k][rows], axis=0, keepdims=True) for k in range(k_taps)]
    out.append(jnp.sum(dcur, axis=0, keepdims=True))
    return _stack_rows(out, width)


def _conv_rows_transposed(dext, w, k_taps):
    acc = w[k_taps - 1:k_taps, :] * dext
    for k in range(k_taps - 1):
        acc = acc + w[k:k + 1, :] * _shift_up(dext, k_taps - 1 - k)
    return acc


def _stack_rows(rows, width):
    ri = _row_iota((8, width))
    out = jnp.zeros((8, width), F32)
    for k, r in enumerate(rows):
        out = out + jnp.where(ri == k, r, 0.0)
    return out


UP_SHARD = 1408
UP_ROWS = 256


def up_ffn_fwd(hf, a_up, conv_w8, conv_b, *, tm=512):
    s = hf.shape[0]

    def body(a_ref, bg_ref, bv_ref, wg_ref, wv_ref, cbg_ref, cbv_ref, hu_ref, act_ref, actt_ref, carry):
        i, j = pl.program_id(0), pl.program_id(1)
        prev_g = jnp.where(i == 0, 0.0, carry[0, j])
        prev_v = jnp.where(i == 0, 0.0, carry[1, j])
        for r in range(0, tm, UP_ROWS):
            rows = slice(r, r + UP_ROWS)
            a = a_ref[rows, :]
            hg, hv = _dot(a, bg_ref[...]), _dot(a, bv_ref[...])
            hu_ref[0, rows, :] = hg
            hu_ref[1, rows, :] = hv
            gc = _conv_rows(_row_shifts(jnp.concatenate([prev_g, hg], axis=0), FFN_CONV), wg_ref[...])[8:] + cbg_ref[...]
            vc = _conv_rows(_row_shifts(jnp.concatenate([prev_v, hv], axis=0), FFN_CONV), wv_ref[...])[8:] + cbv_ref[...]
            act = gc * _sigmoid(gc) * vc
            act_ref[rows, :] = act.astype(BF16)
            actt_ref[:, rows] = act.T.astype(BF16)
            prev_g, prev_v = hg[UP_ROWS - 8:], hv[UP_ROWS - 8:]
        carry[0, j] = prev_g
        carry[1, j] = prev_v

    shard = lambda off: pl.BlockSpec((None, D_MODEL, UP_SHARD), lambda i, j: (j + off, 0, 0))
    taps = lambda off: pl.BlockSpec((8, UP_SHARD), lambda i, j: (0, j + off))
    bias = lambda off: pl.BlockSpec((1, UP_SHARD), lambda i, j: (0, j + off))
    return pl.pallas_call(
        body, name="up_ffn_fwd", grid=(s // tm, 2),
        in_specs=[pl.BlockSpec((tm, D_MODEL), lambda i, j: (i, 0)), shard(0), shard(2), taps(0), taps(2), bias(0), bias(2)],
        out_specs=[pl.BlockSpec((2, tm, UP_SHARD), lambda i, j: (0, i, j)), pl.BlockSpec((tm, UP_SHARD), lambda i, j: (i, j)),
                   pl.BlockSpec((UP_SHARD, tm), lambda i, j: (j, i))],
        out_shape=[jax.ShapeDtypeStruct((2, s, D_FF), F32), jax.ShapeDtypeStruct((s, D_FF), BF16),
                   jax.ShapeDtypeStruct((D_FF, s), BF16)],
        scratch_shapes=[pltpu.VMEM((2, 2, 8, UP_SHARD), F32)], compiler_params=_params(("arbitrary", "arbitrary")),
    )(hf, a_up, a_up, conv_w8, conv_w8, conv_b, conv_b)


def ffn_mid_bwd(hu, dact, conv_w8, conv_b, *, tm=1024, tc=256):
    s = hu.shape[1]
    ncol = D_FF // tc
    nrow = s // tm
    r8 = tm // 8

    def body(g_ref, v_ref, gp_ref, vp_ref, gn_ref, vn_ref, da_ref, dan_ref, wg_ref, wv_ref, bg_ref, bv_ref,
             dhu_ref, wgo_ref, wvo_ref):
        i = pl.program_id(1)
        first = i == 0
        last = i == nrow - 1

        def ext_of(cur_ref, prev_ref, next_ref):
            prev = jnp.where(first, 0.0, prev_ref[...])
            return jnp.concatenate([prev, cur_ref[...], next_ref[...]], axis=0)

        g_sh = _row_shifts(ext_of(g_ref, gp_ref, gn_ref), FFN_CONV)
        v_sh = _row_shifts(ext_of(v_ref, vp_ref, vn_ref), FFN_CONV)
        gc = _conv_rows(g_sh, wg_ref[...]) + bg_ref[...]
        vc = _conv_rows(v_sh, wv_ref[...]) + bv_ref[...]
        da_ext = jnp.concatenate([jnp.zeros((8, tc), F32), da_ref[...], jnp.where(last, 0.0, dan_ref[...])], axis=0)
        silu, dsilu = _silu_and_grad(gc)
        dgc = da_ext * vc * dsilu
        dvc = da_ext * silu
        dhu_ref[0] = _conv_rows_transposed(dgc, wg_ref[...], FFN_CONV)[8:8 + tm].astype(BF16)
        dhu_ref[1] = _conv_rows_transposed(dvc, wv_ref[...], FFN_CONV)[8:8 + tm].astype(BF16)

        cur = slice(8, 8 + tm)
        pg = _conv_weight_grad(dgc[cur], g_sh, cur, tc)
        pv = _conv_weight_grad(dvc[cur], v_sh, cur, tc)

        @pl.when(first)
        def _():
            wgo_ref[...] = pg
            wvo_ref[...] = pv

        @pl.when(i > 0)
        def _():
            wgo_ref[...] += pg
            wvo_ref[...] += pv

    def prev_idx(i):
        return jnp.maximum(i * r8 - 1, 0)

    def next_idx(i):
        return jnp.minimum((i + 1) * r8, s // 8 - 1)

    half = lambda k, rows, row_index: pl.BlockSpec((None, rows, tc), lambda j, i: (k, row_index(i), j))
    in_specs = [
        half(0, tm, lambda i: i), half(1, tm, lambda i: i),
        half(0, 8, prev_idx), half(1, 8, prev_idx),
        half(0, 8, next_idx), half(1, 8, next_idx),
        pl.BlockSpec((tm, tc), lambda j, i: (i, j)),
        pl.BlockSpec((8, tc), lambda j, i: (next_idx(i), j)),
        pl.BlockSpec((8, tc), lambda j, i: (0, j)),
        pl.BlockSpec((8, tc), lambda j, i: (0, j + ncol)),
        pl.BlockSpec((1, tc), lambda j, i: (0, j)),
        pl.BlockSpec((1, tc), lambda j, i: (0, j + ncol)),
    ]
    out_specs = [pl.BlockSpec((2, tm, tc), lambda j, i: (0, i, j)), pl.BlockSpec((8, tc), lambda j, i: (0, j)),
                 pl.BlockSpec((8, tc), lambda j, i: (0, j))]
    out_shape = [jax.ShapeDtypeStruct((2, s, D_FF), BF16),
                 jax.ShapeDtypeStruct((8, D_FF), F32), jax.ShapeDtypeStruct((8, D_FF), F32)]
    return pl.pallas_call(
        body, name="ffn_mid_bwd", grid=(ncol, nrow), in_specs=in_specs, out_specs=out_specs, out_shape=out_shape,
        compiler_params=_params(("parallel", "arbitrary")),
    )(hu, hu, hu, hu, hu, hu, dact, dact, conv_w8, conv_w8, conv_b, conv_b)


def _softplus(x):
    return jnp.maximum(x, 0.0) + jnp.log(1.0 + jnp.exp(-jnp.abs(x)))


def _cumsum_rows(v):
    n = v.shape[0]
    ri = _row_iota(v.shape)
    sh = 1
    while sh < n:
        v = v + jnp.where(ri >= sh, _shift_down(v, sh), 0.0)
        sh *= 2
    return v


def _rev_cumsum_rows(v):
    n = v.shape[0]
    ri = _row_iota(v.shape)
    sh = 1
    while sh < n:
        v = v + jnp.where(ri < n - sh, _shift_up(v, sh), 0.0)
        sh *= 2
    return v


def _total(v):
    return jnp.sum(jnp.sum(v, axis=1, keepdims=True), axis=0, keepdims=True)


def _ssd_in_specs(rev_nc=None):
    def ch(c):
        return c if rev_nc is None else rev_nc - 1 - c

    def prev(c):
        return jnp.maximum(ch(c) * (SSD_CHUNK // 8) - 1, 0)

    L = SSD_CHUNK
    return [
        pl.BlockSpec((L, 1024), lambda c: (ch(c), 0)),
        pl.BlockSpec((L, 1024), lambda c: (ch(c), 1)),
        pl.BlockSpec((L, 256), lambda c: (ch(c), 20)),
        pl.BlockSpec((L, 256), lambda c: (ch(c), 21)),
        pl.BlockSpec((8, 1024), lambda c: (prev(c), 1)),
        pl.BlockSpec((8, 256), lambda c: (prev(c), 20)),
        pl.BlockSpec((8, 256), lambda c: (prev(c), 21)),
        pl.BlockSpec((8, 1024), lambda c: (0, 0)),
        pl.BlockSpec((8, 256), lambda c: (0, 4)),
        pl.BlockSpec((8, 256), lambda c: (0, 5)),
        pl.BlockSpec((1, 1024), lambda c: (0, 0)),
        pl.BlockSpec((1, 256), lambda c: (0, 4)),
        pl.BlockSpec((1, 256), lambda c: (0, 5)),
        pl.BlockSpec((L, SMALL_COLS), lambda c: (ch(c), SMALL_BLOCK)),
        pl.BlockSpec((8, 128), lambda c: (0, 0)),
        pl.BlockSpec((1, 1024), lambda c: (0, 0)),
    ]


def _ssd_conv_pre(cur_ref, prev_ref, w_ref, b_ref, first):
    prev = jnp.where(first, 0.0, prev_ref[...])
    shifts = _row_shifts(jnp.concatenate([prev, cur_ref[...]], axis=0), SSD_CONV)
    return shifts, _conv_rows(shifts, w_ref[...])[8:] + b_ref[...]


def _ssd_time_consts(small_ref, sp_ref):
    dt_pre = small_ref[...] + sp_ref[0:1, :]
    dt = _softplus(dt_pre)
    a = -jnp.exp(sp_ref[1:2, :])
    acs = _cumsum_rows(dt * a)
    return dt_pre, dt, a, acs


def ssd_fwd(proj, conv_w8, conv_b, smallp, norm_w):
    s = proj.shape[0]
    nc = s // SSD_CHUNK
    L = SSD_CHUNK

    def body(z_ref, xs_ref, b_ref, c_ref, xsp_ref, bp_ref, cp_ref, wx_ref, wb_ref, wc_ref, bx_ref, bb_ref, bc_ref,
             small_ref, sp_ref, nw_ref, y_ref, yt_ref, ypre_ref, st_ref, state):
        first = pl.program_id(0) == 0

        @pl.when(first)
        def _():
            state[...] = jnp.zeros_like(state)

        xs = _ssd_conv_pre(xs_ref, xsp_ref, wx_ref, bx_ref, first)[1]
        xs = xs * _sigmoid(xs)
        bm = _ssd_conv_pre(b_ref, bp_ref, wb_ref, bb_ref, first)[1]
        bm = bm * _sigmoid(bm)
        cm = _ssd_conv_pre(c_ref, cp_ref, wc_ref, bc_ref, first)[1]
        cm = cm * _sigmoid(cm)
        _, dt, _, acs = _ssd_time_consts(small_ref, sp_ref)
        acs_t = acs.T
        li = _lane_iota((L, L))
        ri = _row_iota((L, L))
        tri = ri >= li
        lo = li < HEAD_DIM
        st_ref[0] = state[...]
        for g in range(2):
            bg = bm[:, 128 * g:128 * g + 128]
            cg = cm[:, 128 * g:128 * g + 128]
            gmat = _dot(cg, bg, "nt")
            for pp in range(4):
                p = 4 * g + pp
                h0, h1 = 2 * p, 2 * p + 1
                x = xs[:, 128 * p:128 * p + 128]
                a0, a1 = acs[:, h0:h0 + 1], acs[:, h1:h1 + 1]
                xdt = x * jnp.where(lo, dt[:, h0:h0 + 1], dt[:, h1:h1 + 1])
                m0 = gmat * jnp.exp(jnp.where(tri, a0 - acs_t[h0:h0 + 1, :], NEG_BIG))
                m1 = gmat * jnp.exp(jnp.where(tri, a1 - acs_t[h1:h1 + 1, :], NEG_BIG))
                yd = _dot(m0, jnp.where(lo, xdt, 0.0)) + _dot(m1, jnp.where(lo, 0.0, xdt))
                hin = state[p]
                yo = _dot(cg, hin, "nt") * jnp.exp(jnp.where(lo, a0, a1))
                dskip = jnp.where(lo[0:1], sp_ref[2:3, h0:h0 + 1], sp_ref[2:3, h1:h1 + 1])
                ypre_ref[:, 128 * p:128 * p + 128] = yd + yo + dskip * x
                al0, al1 = acs[L - 1:L, h0:h0 + 1], acs[L - 1:L, h1:h1 + 1]
                w = jnp.exp(jnp.where(lo, al0 - a0, al1 - a1))
                dec = jnp.exp(jnp.where(ri < HEAD_DIM, al0, al1))
                state[p] = dec * hin + _dot(xdt * w, bg, "tn")
        z = z_ref[...]
        yg = ypre_ref[...] * (z * _sigmoid(z))
        for g in range(2):
            seg = yg[:, 512 * g:512 * g + 512]
            r = lax.rsqrt(jnp.mean(seg * seg, axis=-1, keepdims=True) + NORM_EPS)
            out = (seg * r) * nw_ref[:, 512 * g:512 * g + 512]
            y_ref[:, 512 * g:512 * g + 512] = out.astype(BF16)
            yt_ref[512 * g:512 * g + 512, :] = out.T.astype(BF16)

    row = pl.BlockSpec((L, 1024), lambda c: (c, 0))
    return pl.pallas_call(
        body, name="ssd_fwd", grid=(nc,), in_specs=_ssd_in_specs(),
        out_specs=[row, pl.BlockSpec((1024, L), lambda c: (0, c)), row,
                   pl.BlockSpec((1, N_PAIRS, 128, 128), lambda c: (c, 0, 0, 0))],
        out_shape=[jax.ShapeDtypeStruct((s, 1024), BF16), jax.ShapeDtypeStruct((1024, s), BF16),
                   jax.ShapeDtypeStruct((s, 1024), F32), jax.ShapeDtypeStruct((nc, N_PAIRS, 128, 128), F32)],
        scratch_shapes=[pltpu.VMEM((N_PAIRS, 128, 128), F32)],
        compiler_params=_params(("arbitrary",)),
    )(proj, proj, proj, proj, proj, proj, proj, conv_w8, conv_w8, conv_w8, conv_b, conv_b, conv_b, proj, smallp, norm_w)


def ssd_bwd(proj, conv_w8, conv_b, smallp, norm_w, ypre, states, dy, sel, swap=()):
    s = proj.shape[0]
    nc = s // SSD_CHUNK
    L = SSD_CHUNK

    ns = len(swap)
    n_in, n_out, n_scratch = 20, 10, 11

    def body(*refs):
        own = refs[:n_in] + refs[n_in + ns:n_in + ns + n_out] + refs[n_in + 2 * ns + n_out:n_in + 2 * ns + n_out + n_scratch]
        if ns:
            start, finish = _pair_swap_phases(refs[n_in:n_in + ns], refs[n_in + ns + n_out:n_in + 2 * ns + n_out],
                                              *refs[n_in + 2 * ns + n_out + n_scratch:])
            pl.when(pl.program_id(0) == 0)(start)
        compute(*own)
        if ns:
            pl.when(pl.program_id(0) == nc - 1)(finish)

    def compute(z_ref, xs_ref, b_ref, c_ref, xsp_ref, bp_ref, cp_ref, wx_ref, wb_ref, wc_ref, bx_ref, bb_ref, bc_ref,
                small_ref, sp_ref, nw_ref, ypre_ref, st_ref, dy_ref, sel_ref,
                dz_ref, dxs_ref, db_ref, dc_ref, dsmall_ref, gwx_ref, gwb_ref, gwc_ref, gsp_ref, gnw_ref,
                dstate, carry_x, carry_b, carry_c, dxs_buf, dbm_buf, dcm_buf, qcs, col_sums, acs_terms, dt_terms):
        step = pl.program_id(0)
        col_sums[...] = jnp.zeros_like(col_sums)
        first_chunk = step == nc - 1
        start = step == 0

        @pl.when(start)
        def _():
            dstate[...] = jnp.zeros_like(dstate)
            carry_x[...] = jnp.zeros_like(carry_x)
            carry_b[...] = jnp.zeros_like(carry_b)
            carry_c[...] = jnp.zeros_like(carry_c)

        xs_sh, xs_pre = _ssd_conv_pre(xs_ref, xsp_ref, wx_ref, bx_ref, first_chunk)
        b_sh, b_pre = _ssd_conv_pre(b_ref, bp_ref, wb_ref, bb_ref, first_chunk)
        c_sh, c_pre = _ssd_conv_pre(c_ref, cp_ref, wc_ref, bc_ref, first_chunk)
        xs, xs_ds = _silu_and_grad(xs_pre)
        bm, b_ds = _silu_and_grad(b_pre)
        cm, c_ds = _silu_and_grad(c_pre)
        dt_pre, dt, a, acs = _ssd_time_consts(small_ref, sp_ref)
        acs_t = acs.T
        li = _lane_iota((L, L))
        ri = _row_iota((L, L))
        tri = ri >= li
        lo = li < HEAD_DIM
        lo_rows = ri < HEAD_DIM
        li1 = _lane_iota((1, L))

        z = z_ref[...]
        sz, dsz = _silu_and_grad(z)
        y = ypre_ref[...]
        yg = y * sz
        dout = dy_ref[...]
        dyg_parts = []
        gnw_parts = []
        for g in range(2):
            sl = slice(512 * g, 512 * g + 512)
            seg = yg[:, sl]
            r = lax.rsqrt(jnp.mean(seg * seg, axis=-1, keepdims=True) + NORM_EPS)
            n = seg * r
            gnw_parts.append(jnp.sum(dout[:, sl] * n, axis=0, keepdims=True))
            gg = dout[:, sl] * nw_ref[:, sl]
            dyg_parts.append(r * (gg - n * jnp.mean(gg * n, axis=-1, keepdims=True)))
        dyg = jnp.concatenate(dyg_parts, axis=1)
        gnw = jnp.concatenate(gnw_parts, axis=1)
        dz_ref[...] = (dyg * y * dsz).astype(BF16)
        dypre = dyg * sz

        qcs[...] = jnp.zeros_like(qcs)
        dalast = jnp.zeros((1, L), F32)
        for g in range(2):
            bg = bm[:, 128 * g:128 * g + 128]
            cg = cm[:, 128 * g:128 * g + 128]
            gmat = _dot(cg, bg, "nt")
            dgmat = jnp.zeros((L, L), F32)
            dbg = jnp.zeros((L, L), F32)
            dcg = jnp.zeros((L, L), F32)
            for pp in range(4):
                p = 4 * g + pp
                h0, h1 = 2 * p, 2 * p + 1
                lanes = slice(128 * p, 128 * p + 128)
                x = xs[:, lanes]
                dyp = dypre[:, lanes]
                a0, a1 = acs[:, h0:h0 + 1], acs[:, h1:h1 + 1]
                dtl = jnp.where(lo, dt[:, h0:h0 + 1], dt[:, h1:h1 + 1])
                xdt = x * dtl
                l0 = jnp.exp(jnp.where(tri, a0 - acs_t[h0:h0 + 1, :], NEG_BIG))
                l1 = jnp.exp(jnp.where(tri, a1 - acs_t[h1:h1 + 1, :], NEG_BIG))
                m0, m1 = gmat * l0, gmat * l1
                dskip = jnp.where(lo[0:1], sp_ref[2:3, h0:h0 + 1], sp_ref[2:3, h1:h1 + 1])
                col_sums[0:1, lanes] = jnp.sum(dyp * x, axis=0, keepdims=True)
                dx = dyp * dskip
                dy0, dy1 = jnp.where(lo, dyp, 0.0), jnp.where(lo, 0.0, dyp)
                x0, x1 = jnp.where(lo, xdt, 0.0), jnp.where(lo, 0.0, xdt)
                dm0, dm1 = _dot(dy0, x0, "nt"), _dot(dy1, x1, "nt")
                dxdt = _dot(m0, dy0, "tn") + _dot(m1, dy1, "tn")
                q0, q1 = dm0 * m0, dm1 * m1
                qcs[h0:h0 + 1, :] = jnp.sum(q0, axis=0, keepdims=True)
                qcs[h1:h1 + 1, :] = jnp.sum(q1, axis=0, keepdims=True)
                row_terms = jnp.where(lo, q0 + pltpu.roll(q0, HEAD_DIM, 1), q1 + pltpu.roll(q1, HEAD_DIM, 1))
                dgmat = dgmat + dm0 * l0 + dm1 * l1
                hin = st_ref[0, p]
                e = jnp.exp(jnp.where(lo, a0, a1))
                ch = _dot(cg, hin, "nt")
                dch = dyp * e
                dcg = dcg + _dot(dch, hin)
                dhin = _dot(dch, cg, "tn")
                dhout = dstate[p]
                al0, al1 = acs[L - 1:L, h0:h0 + 1], acs[L - 1:L, h1:h1 + 1]
                dec = jnp.exp(jnp.where(lo_rows, al0, al1))
                dhin = dhin + dec * dhout
                dal = dhout * hin * dec
                dal0 = _total(jnp.where(lo_rows, dal, 0.0))
                dal1 = _total(dal) - dal0
                dalast = dalast + jnp.where(li1 == h0, dal0, 0.0) + jnp.where(li1 == h1, dal1, 0.0)
                w = jnp.exp(jnp.where(lo, al0 - a0, al1 - a1))
                xw = xdt * w
                dxw = _dot(bg, dhout, "nt")
                dbg = dbg + _dot(xw, dhout)
                dxdt = dxdt + dxw * w
                dww = dxw * xw
                col_sums[1:2, lanes] = jnp.sum(dww, axis=0, keepdims=True)
                acs_terms[:, lanes] = row_terms + dch * ch - dww
                dx = dx + dxdt * dtl
                dt_terms[:, lanes] = dxdt * x
                dxs_buf[:, lanes] = dx
                dstate[p] = dhin
            dcg = dcg + _dot(dgmat, bg)
            dbg = dbg + _dot(dgmat, cg, "tn")
            dbm_buf[:, 128 * g:128 * g + 128] = dbg
            dcm_buf[:, 128 * g:128 * g + 128] = dcg

        head_sums = _split3_dot(col_sums[...], sel_ref[...])
        dskip_g = head_sums[0:1, :]
        dalast = dalast + head_sums[1:2, :]
        ddt = _split3_dot(dt_terms[...], sel_ref[...])
        dacs_tot = _split3_dot(acs_terms[...], sel_ref[...]) - qcs[...].T + jnp.where(ri == L - 1, dalast, 0.0)
        dstep = _rev_cumsum_rows(dacs_tot)
        ddt = ddt + dstep * a
        head_lane = li < N_HEADS
        ddt_pre = jnp.where(head_lane, ddt * _sigmoid(dt_pre), 0.0)
        dsmall_ref[...] = ddt_pre
        da = jnp.sum(jnp.where(head_lane, dstep * dt, 0.0), axis=0, keepdims=True)
        gsp = _stack_rows([jnp.sum(ddt_pre, axis=0, keepdims=True), da * a, dskip_g], L)

        def conv_back(dpost, ds, shifts, w_ref, carry, out_ref, width):
            dpre = dpost * ds
            dext = jnp.concatenate([dpre, carry[...]], axis=0)
            out_ref[...] = _conv_rows_transposed(dext, w_ref[...], SSD_CONV)[:L].astype(BF16)
            carry[...] = dpre[0:8]
            return _conv_weight_grad(dpre, shifts, slice(8, 8 + L), width)

        gwx = conv_back(dxs_buf[...], xs_ds, xs_sh, wx_ref, carry_x, dxs_ref, 1024)
        gwb = conv_back(dbm_buf[...], b_ds, b_sh, wb_ref, carry_b, db_ref, 256)
        gwc = conv_back(dcm_buf[...], c_ds, c_sh, wc_ref, carry_c, dc_ref, 256)

        @pl.when(start)
        def _():
            gwx_ref[...] = gwx
            gwb_ref[...] = gwb
            gwc_ref[...] = gwc
            gsp_ref[...] = gsp
            gnw_ref[...] = gnw

        @pl.when(step > 0)
        def _():
            gwx_ref[...] += gwx
            gwb_ref[...] += gwb
            gwc_ref[...] += gwc
            gsp_ref[...] += gsp
            gnw_ref[...] += gnw

    def ch(c):
        return nc - 1 - c

    row = pl.BlockSpec((L, 1024), lambda c: (ch(c), 0))
    row256 = pl.BlockSpec((L, 256), lambda c: (ch(c), 0))
    in_specs = _ssd_in_specs(rev_nc=nc) + [row, pl.BlockSpec((1, N_PAIRS, 128, 128), lambda c: (ch(c), 0, 0, 0)), row,
                                           pl.BlockSpec((1024, 128), lambda c: (0, 0))]
    out_specs = [row, row, row256, row256, pl.BlockSpec((L, 128), lambda c: (ch(c), 0)),
                 pl.BlockSpec((8, 1024), lambda c: (0, 0)), pl.BlockSpec((8, 256), lambda c: (0, 0)),
                 pl.BlockSpec((8, 256), lambda c: (0, 0)), pl.BlockSpec((8, 128), lambda c: (0, 0)),
                 pl.BlockSpec((1, 1024), lambda c: (0, 0))]
    out_shape = [jax.ShapeDtypeStruct((s, 1024), BF16), jax.ShapeDtypeStruct((s, 1024), BF16),
                 jax.ShapeDtypeStruct((s, 256), BF16), jax.ShapeDtypeStruct((s, 256), BF16),
                 jax.ShapeDtypeStruct((s, 128), F32),
                 jax.ShapeDtypeStruct((8, 1024), F32), jax.ShapeDtypeStruct((8, 256), F32),
                 jax.ShapeDtypeStruct((8, 256), F32), jax.ShapeDtypeStruct((8, 128), F32),
                 jax.ShapeDtypeStruct((1, 1024), F32)]
    scratch = [pltpu.VMEM((N_PAIRS, 128, 128), F32), pltpu.VMEM((8, 1024), F32), pltpu.VMEM((8, 256), F32),
               pltpu.VMEM((8, 256), F32), pltpu.VMEM((L, 1024), F32), pltpu.VMEM((L, 256), F32), pltpu.VMEM((L, 256), F32),
               pltpu.VMEM((L, L), F32), pltpu.VMEM((8, 1024), F32), pltpu.VMEM((L, 1024), F32), pltpu.VMEM((L, 1024), F32)]
    assert (len(in_specs), len(out_specs), len(scratch)) == (n_in, n_out, n_scratch)
    outs = pl.pallas_call(
        body, name="ssd_bwd", grid=(nc,), in_specs=in_specs + [ANY] * ns, out_specs=out_specs + [ANY] * ns,
        out_shape=out_shape + _pair_swap_out_shapes(swap), scratch_shapes=scratch + (_pair_swap_scratch(ns) if ns else []),
        compiler_params=_params(("arbitrary",)),
    )(proj, proj, proj, proj, proj, proj, proj, conv_w8, conv_w8, conv_w8, conv_b, conv_b, conv_b, proj, smallp, norm_w,
      ypre, states, dy, sel, *swap)
    return (*outs[:n_out], list(outs[n_out:]))


FOX_SCALE = HEAD_DIM ** -0.5
FOX_T = 256
Q_COL, K_COL, V_COL = 2, 3, 4


def _split_dot(v, m, terms):
    out, rest = None, v
    for i in range(terms):
        piece = rest.astype(BF16)
        out = _dot(piece, m) if out is None else out + _dot(piece, m)
        if i + 1 < terms:
            rest = rest - piece.astype(F32)
    return out


def _split3_dot(v, m):
    return _split_dot(v, m, 3)


def _head_mean(x, sel_ref, selt_ref):
    return _dot(x, sel_ref[...]) * (1.0 / HEAD_DIM)


def _head_spread(v, selt_ref):
    return _split_dot(v, selt_ref[...], 2)


def _head_rstd(x, sel_ref, selt_ref):
    return _head_spread(lax.rsqrt(_head_mean(x * x, sel_ref, selt_ref) + NORM_EPS), selt_ref)


def fox_tables():
    r = np.arange(3 * 128)
    piece, lane = r // 128, r % 128
    head = lane - F_LANE
    is_head = np.logical_and(head >= 0, head < N_HEADS)
    col = 128 * (head // 2) + HEAD_DIM * (1 - head % 2) + piece
    cols = np.arange(1024)
    place_q = np.logical_and(is_head[:, None], cols[None, :] == col[:, None])
    place_k = np.logical_and(is_head[:, None], cols[None, :] == (col + 3)[:, None])
    ones_q = np.logical_and(cols % HEAD_DIM >= 3, cols % HEAD_DIM < 6)[None]
    ones_k = (cols % HEAD_DIM < 3)[None]
    h = np.arange(128) - F_LANE
    ok = np.logical_and(h >= 0, h < N_HEADS)
    same_pair = cols[:, None] // 128 == (h // 2)[None, :]
    fold_even = np.logical_and(np.logical_and(ok, h % 2 == 0)[None, :], same_pair)
    fold_odd = np.logical_and(np.logical_and(ok, h % 2 == 1)[None, :], same_pair)
    as_bf16 = lambda t: jnp.asarray(t.astype(np.float32), BF16)
    return (as_bf16(place_q), as_bf16(place_k), jnp.asarray(ones_q, F32), jnp.asarray(ones_k, F32),
            as_bf16(fold_even), as_bf16(fold_odd))


def fox_prep(proj, smallp, qw, kw, sel, selt, place_q, place_k, ones_q, ones_k, *, tm=256):
    s = proj.shape[0]

    def body(q_ref, k_ref, v_ref, small_ref, sp_ref, qw_ref, kw_ref, sel_ref, selt_ref, pq_ref, pk_ref, oq_ref, ok_ref,
             qn_ref, kn_ref, aq_ref, ak_ref, vb_ref, knt_ref, akt_ref, vt_ref, carry):
        @pl.when(pl.program_id(0) == 0)
        def _():
            carry[...] = jnp.zeros_like(carry)

        q = q_ref[...]
        qn_ref[...] = (((q * _head_rstd(q, sel_ref, selt_ref)) * qw_ref[...]) * FOX_SCALE).astype(BF16)
        k = k_ref[...]
        kn = ((k * _head_rstd(k, sel_ref, selt_ref)) * kw_ref[...]).astype(BF16)
        kn_ref[...] = kn
        knt_ref[...] = kn.astype(F32).T.astype(BF16)
        vb_ref[...] = v_ref[...].astype(BF16)
        vt_ref[...] = v_ref[...].T.astype(BF16)
        li = _lane_iota((tm, 128))
        f_lane = jnp.logical_and(li >= F_LANE, li < F_LANE + N_HEADS)
        logf = jnp.where(f_lane, -_softplus(-(small_ref[...] + sp_ref[3:4, :])), 0.0)
        cum = _cumsum_rows(logf) + carry[...]
        carry[...] = cum[tm - 1:tm, :]
        hi = cum.astype(BF16)
        r1 = cum - hi.astype(F32)
        mid = r1.astype(BF16)
        lo = (r1 - mid.astype(F32)).astype(BF16)
        pieces = jnp.concatenate([hi, mid, lo], axis=1)
        aq_ref[...] = (_dot(pieces, pq_ref[...]) + oq_ref[...]).astype(BF16)
        ak = ok_ref[...] - _dot(pieces, pk_ref[...])
        ak_ref[...] = ak.astype(BF16)
        akt_ref[...] = ak.T.astype(BF16)

    row = pl.BlockSpec((tm, 1024), lambda i: (i, 0))
    col = pl.BlockSpec((1024, tm), lambda i: (0, i))
    vec = pl.BlockSpec((1, 1024), lambda i: (0, 0))
    table = pl.BlockSpec((384, 1024), lambda i: (0, 0))
    wide = jax.ShapeDtypeStruct((s, 1024), BF16)
    tall = jax.ShapeDtypeStruct((1024, s), BF16)
    return pl.pallas_call(
        body, name="fox_prep", grid=(s // tm,),
        in_specs=[pl.BlockSpec((tm, 1024), lambda i: (i, Q_COL)), pl.BlockSpec((tm, 1024), lambda i: (i, K_COL)),
                  pl.BlockSpec((tm, 1024), lambda i: (i, V_COL)),
                  pl.BlockSpec((tm, 128), lambda i: (i, SMALL_BLOCK)), pl.BlockSpec((8, 128), lambda i: (0, 0)), vec, vec,
                  pl.BlockSpec((1024, 128), lambda i: (0, 0)), pl.BlockSpec((128, 1024), lambda i: (0, 0)),
                  table, table, vec, vec],
        out_specs=[row, row, row, row, row, col, col, col],
        out_shape=[wide, wide, wide, wide, wide, tall, tall, tall],
        scratch_shapes=[pltpu.VMEM((1, 128), F32)], compiler_params=_params(("arbitrary",)),
    )(proj, proj, proj, proj, smallp, qw, kw, sel, selt, place_q, place_k, ones_q, ones_k)


def fox_fwd(qn, kn, aq, ak, vt, shards=()):
    s = qn.shape[0]
    t = FOX_T
    nq = s // t
    ng = len(shards)

    def body(*refs):
        q_ref, k_ref, aq_ref, ak_ref, vt_ref = refs[:5]
        o_ref, ot_ref, lse_ref = refs[5 + ng:8 + ng]
        p = pl.program_id(0)
        if ng:
            start, forward, finish = _gather_phases(refs[5:5 + ng], refs[8 + ng:8 + 2 * ng], *refs[8 + 2 * ng:])
            pl.when(p == 0)(start)
            pl.when(p == N_PAIRS // 2)(forward)

        @pl.when(p == 0)
        def _():
            lse_ref[...] = jnp.zeros_like(lse_ref)

        lo = _lane_iota((t, 128)) < HEAD_DIM
        lo_rows = _row_iota((128, t)) < HEAD_DIM
        causal_t = _lane_iota((t, t)) >= _row_iota((t, t))

        def q_loop(qi, _):
            q0 = pl.multiple_of(qi * t, t)
            qv, aqv = q_ref[pl.ds(q0, t), :], aq_ref[pl.ds(q0, t), :]
            qa, qb = jnp.where(lo, qv, aqv), jnp.where(lo, aqv, qv)

            def scores(kj):
                k0 = pl.multiple_of(kj * t, t)
                kv, akv = k_ref[pl.ds(k0, t), :], ak_ref[pl.ds(k0, t), :]
                return _dot(jnp.where(lo, kv, akv), qa, "nt"), _dot(jnp.where(lo, akv, kv), qb, "nt")

            def update(kj, stats, s0, s1):
                m0, l0, m1, l1, acc = stats
                vtv = vt_ref[:, pl.ds(pl.multiple_of(kj * t, t), t)]
                n0 = jnp.maximum(m0, jnp.max(s0, axis=0, keepdims=True))
                n1 = jnp.maximum(m1, jnp.max(s1, axis=0, keepdims=True))
                a0, a1 = jnp.exp(m0 - n0), jnp.exp(m1 - n1)
                p0, p1 = jnp.exp(s0 - n0), jnp.exp(s1 - n1)
                l0 = a0 * l0 + jnp.sum(p0, axis=0, keepdims=True)
                l1 = a1 * l1 + jnp.sum(p1, axis=0, keepdims=True)
                acc = (jnp.where(lo_rows, a0, a1) * acc + _dot(jnp.where(lo_rows, vtv, 0.0), p0)
                       + _dot(jnp.where(lo_rows, 0.0, vtv), p1))
                return n0, l0, n1, l1, acc

            def step(kj, carry):
                stats, (s0, s1) = carry[:5], carry[5:]
                nxt = scores(kj + 1)
                return (*update(kj, stats, s0, s1), *nxt)

            def row(val):
                return jnp.full((1, t), val, F32)

            init = (row(NEG_BIG), row(0.0), row(NEG_BIG), row(0.0), jnp.zeros((128, t), F32), *scores(0))
            carry = lax.fori_loop(0, qi, step, init)
            s0, s1 = jnp.where(causal_t, carry[5], NEG_BIG), jnp.where(causal_t, carry[6], NEG_BIG)
            m0, l0, m1, l1, acc = update(qi, carry[:5], s0, s1)
            out_t = acc / jnp.where(lo_rows, l0, l1)
            ot_ref[:, pl.ds(q0, t)] = out_t.astype(BF16)
            o_ref[pl.ds(q0, t), :] = out_t.T.astype(BF16)
            ri = _row_iota((N_HEADS, t))
            old = lse_ref[:, pl.ds(q0, t)]
            lse_ref[:, pl.ds(q0, t)] = jnp.where(
                ri == 2 * p, m0 + jnp.log(l0), jnp.where(ri == 2 * p + 1, m1 + jnp.log(l1), old))
            return 0

        lax.fori_loop(0, nq, q_loop, 0)
        if ng:
            pl.when(p == N_PAIRS - 1)(finish)

    pair = pl.BlockSpec((s, 128), lambda p: (0, p))
    outs = pl.pallas_call(
        body, name="fox_fwd", grid=(N_PAIRS,),
        in_specs=[pair] * 4 + [pl.BlockSpec((128, s), lambda p: (p, 0))] + [ANY] * ng,
        out_specs=[pair, pl.BlockSpec((128, s), lambda p: (p, 0)), pl.BlockSpec((N_HEADS, s), lambda p: (0, 0))] + [ANY] * ng,
        out_shape=[jax.ShapeDtypeStruct((s, 1024), BF16), jax.ShapeDtypeStruct((1024, s), BF16),
                   jax.ShapeDtypeStruct((N_HEADS, s), F32)] + _gather_out_shapes(shards),
        scratch_shapes=_gather_scratch(ng) if ng else [],
        compiler_params=_params(("arbitrary",)),
    )(qn, kn, aq, ak, vt, *shards)
    return outs[0], outs[1], outs[2], list(outs[3:])


def fox_bwd(qn, kn, aq, ak, knt, akt, vb, lse, dmixed, parts=()):
    s = qn.shape[0]
    t = FOX_T
    nq = s // t
    once = pl.Buffered(1)
    ns = len(parts)

    def body(*refs):
        q_ref, k_ref, aq_ref, ak_ref, kt_ref, akt_ref, v_ref, lse_ref, do_ref = refs[:9]
        dq_ref, dk_ref, dv_ref, dc0_ref, dc1_ref = refs[9 + ns:14 + ns]
        p_scr, dp_scr = refs[14 + 2 * ns:16 + 2 * ns]
        p = pl.program_id(0)
        if ns:
            start, finish = _scatter_phases(refs[9:9 + ns], refs[14 + ns:14 + 2 * ns], *refs[16 + 2 * ns:])
            pl.when(p == 0)(start)
        dk_ref[...] = jnp.zeros_like(dk_ref)
        dv_ref[...] = jnp.zeros_like(dv_ref)
        dc0_ref[...] = jnp.zeros_like(dc0_ref)
        dc1_ref[...] = jnp.zeros_like(dc1_ref)
        lo = _lane_iota((t, 128)) < HEAD_DIM
        lo_rows = _row_iota((128, t)) < HEAD_DIM
        causal_t = _lane_iota((t, t)) >= _row_iota((t, t))

        def q_loop(qi, _):
            q0 = pl.multiple_of(qi * t, t)
            qv, aqv = q_ref[pl.ds(q0, t), :], aq_ref[pl.ds(q0, t), :]
            qa, qb = jnp.where(lo, qv, aqv), jnp.where(lo, aqv, qv)
            do = do_ref[pl.ds(q0, t), :]
            doa, dob = jnp.where(lo, do, 0.0).astype(BF16), jnp.where(lo, 0.0, do).astype(BF16)
            lse_blk = lse_ref[:, pl.ds(q0, t)]
            ri = _row_iota((N_HEADS, t))
            lse0 = jnp.sum(jnp.where(ri == 2 * p, lse_blk, 0.0), axis=0, keepdims=True)
            lse1 = jnp.sum(jnp.where(ri == 2 * p + 1, lse_blk, 0.0), axis=0, keepdims=True)

            def scores(kj):
                k0 = pl.multiple_of(kj * t, t)
                kv, akv = k_ref[pl.ds(k0, t), :], ak_ref[pl.ds(k0, t), :]
                return _dot(jnp.where(lo, kv, akv), qa, "nt"), _dot(jnp.where(lo, akv, kv), qb, "nt")

            def pass1(kj, d0, d1, diagonal):
                k0 = pl.multiple_of(kj * t, t)
                vv = v_ref[pl.ds(k0, t), :]
                s0, s1 = scores(kj)
                if diagonal:
                    s0, s1 = jnp.where(causal_t, s0, NEG_BIG), jnp.where(causal_t, s1, NEG_BIG)
                p0, p1 = jnp.exp(s0 - lse0), jnp.exp(s1 - lse1)
                dp0, dp1 = _dot(vv, doa, "nt"), _dot(vv, dob, "nt")
                p_scr[0, kj], p_scr[1, kj] = p0, p1
                dp_scr[0, kj], dp_scr[1, kj] = dp0, dp1
                dv_ref[pl.ds(k0, t), :] += _dot(p0, doa) + _dot(p1, dob)
                return d0 + jnp.sum(p0 * dp0, axis=0, keepdims=True), d1 + jnp.sum(p1 * dp1, axis=0, keepdims=True)

            zero = jnp.zeros((1, t), F32)
            d0, d1 = lax.fori_loop(0, qi, lambda kj, c: pass1(kj, *c, False), (zero, zero))
            d0, d1 = pass1(qi, d0, d1, True)

            def fold_lanes(v):
                return functools.reduce(lambda a, b: a + b, [v[:, 128 * i:128 * (i + 1)] for i in range(t // 128)])

            def pass2(kj, carry):
                dq0, dq1 = carry
                k0 = pl.multiple_of(kj * t, t)
                p0, p1 = p_scr[0, kj], p_scr[1, kj]
                ds0, ds1 = p0 * (dp_scr[0, kj] - d0), p1 * (dp_scr[1, kj] - d1)
                dk_ref[pl.ds(k0, t), :] += jnp.where(lo, _dot(ds0, qa), _dot(ds1, qb))
                dc0_ref[pl.ds(k0, t), :] += fold_lanes(ds0)
                dc1_ref[pl.ds(k0, t), :] += fold_lanes(ds1)
                ktv, aktv = kt_ref[:, pl.ds(k0, t)], akt_ref[:, pl.ds(k0, t)]
                return dq0 + _dot(jnp.where(lo_rows, ktv, aktv), ds0), dq1 + _dot(jnp.where(lo_rows, aktv, ktv), ds1)

            zq = jnp.zeros((128, t), F32)
            dq0, dq1 = lax.fori_loop(0, qi + 1, pass2, (zq, zq))
            dq_ref[pl.ds(q0, t), :] = jnp.where(lo_rows, dq0, dq1).T
            return 0

        lax.fori_loop(0, nq, q_loop, 0)
        if ns:
            pl.when(p == N_PAIRS - 1)(finish)

    pair = pl.BlockSpec((s, 128), lambda p: (0, p))
    pair_t = pl.BlockSpec((128, s), lambda p: (p, 0))
    out = jax.ShapeDtypeStruct((s, 1024), F32)
    outs = pl.pallas_call(
        body, name="fox_bwd", grid=(N_PAIRS,),
        in_specs=[pair, pair, pair, pair, pair_t, pair_t, pair, pl.BlockSpec((N_HEADS, s), lambda p: (0, 0)),
                  pl.BlockSpec((s, 128), lambda p: (0, 8 + p))] + [ANY] * ns,
        out_specs=[pl.BlockSpec((s, 128), lambda p: (0, p), pipeline_mode=once)] * 5 + [ANY] * ns,
        out_shape=[out] * 5 + [jax.ShapeDtypeStruct(p.shape, p.dtype) for p in parts],
        scratch_shapes=[pltpu.VMEM((2, nq, t, t), F32), pltpu.VMEM((2, nq, t, t), F32)] + (_scatter_scratch(ns) if ns else []),
        compiler_params=_params(("arbitrary",)),
    )(qn, kn, aq, ak, knt, akt, vb, lse, dmixed, *parts)
    return (*outs[:5], _keep_own_blocks(outs[5:], parts))


def fox_post(dqn, dkn, dc0, dc1, proj, smallp, qw, kw, sel, selt, fold_even, fold_odd, *, tm=256):
    s = proj.shape[0]
    nrow = s // tm

    def body(dqn_ref, dkn_ref, dc0_ref, dc1_ref, q_ref, k_ref, small_ref, sp_ref, qw_ref, kw_ref, sel_ref, selt_ref,
             fe_ref, fo_ref, dq_ref, dk_ref, dsmall_ref, gqw_ref, gkw_ref, gfb_ref, carry):
        step = pl.program_id(0)

        @pl.when(step == 0)
        def _():
            carry[...] = jnp.zeros_like(carry)

        def norm_bwd(x_ref, w_ref, dn, out_ref):
            x = x_ref[...]
            rf = _head_rstd(x, sel_ref, selt_ref)
            xh = x * rf
            g = dn * w_ref[...]
            mean_gx = _head_spread(_head_mean(g * xh, sel_ref, selt_ref), selt_ref)
            out_ref[...] = (rf * (g - xh * mean_gx)).astype(BF16)
            return jnp.sum(dn * xh, axis=0, keepdims=True)

        gqw = norm_bwd(q_ref, qw_ref, dqn_ref[...] * FOX_SCALE, dq_ref)
        gkw = norm_bwd(k_ref, kw_ref, dkn_ref[...], dk_ref)
        li = _lane_iota((tm, 128))
        f_lane = jnp.logical_and(li >= F_LANE, li < F_LANE + N_HEADS)
        dcum = -(_split3_dot(dc0_ref[...], fe_ref[...]) + _split3_dot(dc1_ref[...], fo_ref[...]))
        dlogf = _rev_cumsum_rows(dcum) + carry[...]
        carry[...] = dlogf[0:1, :]
        dfr = jnp.where(f_lane, dlogf * _sigmoid(-(small_ref[...] + sp_ref[3:4, :])), 0.0)
        dsmall_ref[...] = dfr
        gfb = jnp.sum(dfr, axis=0, keepdims=True)

        @pl.when(step == 0)
        def _():
            gqw_ref[...] = gqw
            gkw_ref[...] = gkw
            gfb_ref[...] = gfb

        @pl.when(step > 0)
        def _():
            gqw_ref[...] += gqw
            gkw_ref[...] += gkw
            gfb_ref[...] += gfb

    def rb(i):
        return nrow - 1 - i

    row = pl.BlockSpec((tm, 1024), lambda i: (rb(i), 0))
    vec = pl.BlockSpec((1, 1024), lambda i: (0, 0))
    fold = pl.BlockSpec((1024, 128), lambda i: (0, 0))
    return pl.pallas_call(
        body, name="fox_post", grid=(nrow,),
        in_specs=[row, row, row, row, pl.BlockSpec((tm, 1024), lambda i: (rb(i), Q_COL)),
                  pl.BlockSpec((tm, 1024), lambda i: (rb(i), K_COL)),
                  pl.BlockSpec((tm, 128), lambda i: (rb(i), SMALL_BLOCK)), pl.BlockSpec((8, 128), lambda i: (0, 0)), vec, vec,
                  fold, pl.BlockSpec((128, 1024), lambda i: (0, 0)), fold, fold],
        out_specs=[row, row, pl.BlockSpec((tm, 128), lambda i: (rb(i), 0)), vec, vec, pl.BlockSpec((1, 128), lambda i: (0, 0))],
        out_shape=[jax.ShapeDtypeStruct((s, 1024), BF16), jax.ShapeDtypeStruct((s, 1024), BF16),
                   jax.ShapeDtypeStruct((s, 128), F32), jax.ShapeDtypeStruct((1, 1024), F32),
                   jax.ShapeDtypeStruct((1, 1024), F32), jax.ShapeDtypeStruct((1, 128), F32)],
        scratch_shapes=[pltpu.VMEM((1, 128), F32)], compiler_params=_params(("arbitrary",)),
    )(dqn, dkn, dc0, dc1, proj, proj, proj, smallp, qw, kw, sel, selt, fold_even, fold_odd)


def local_step(x, target, wx, later_shards, ssd_cw8, ssd_cb, smallp, ssd_nw, qw_t, kw_t, sel, selt,
               norm_mix_w, norm_ffn_w, ffn_cw8, ffn_cb):
    h, h_t = rms_fwd(x, norm_mix_w, name="rms_mix_fwd")
    proj = matmul(h, wx, mode="nn", tm=1024, tn=PROJ_TILE, tk=1024, out_dtype=F32, name="mm_in_proj")
    y_ssd, y_ssd_t, ypre, states = ssd_fwd(proj, ssd_cw8, ssd_cb, smallp, ssd_nw)
    place_q, place_k, ones_q, ones_k, fold_even, fold_odd = fox_tables()
    qn, kn, aq, ak, vb, knt, akt, vt = fox_prep(proj, smallp, qw_t, kw_t, sel, selt, place_q, place_k, ones_q, ones_k)
    y_fox, y_fox_t, lse, (a_out, a_up, a_down) = fox_fwd(qn, kn, aq, ak, vt, shards=later_shards)
    w_out = a_out.reshape(2048, D_MODEL)
    w_down = a_down.reshape(D_FF, D_MODEL)
    s = x.shape[0]
    shard = lambda index: pl.BlockSpec((None, 1024, 1408), index)
    x1, hf, hf_t = out_proj_rms_fwd(y_ssd, y_fox, w_out, x, norm_ffn_w)
    hu, act, act_t = up_ffn_fwd(hf, a_up, ffn_cw8, ffn_cb)
    y = matmul(act, w_down, mode="nn", tm=1024, tn=1024, tk=1408, out_dtype=F32, name="mm_down", add=x1)
    dy, sq = loss_head(y, target)

    dact = matmul(dy, w_down, mode="nt", tm=1024, tn=1408, tk=1024, out_dtype=F32, name="mm_dact")
    g_down = matmul(act_t, dy, mode="nn", tm=1408, tn=1024, tk=1024, out_dtype=BF16, name="mm_dw_down")
    dhu, gcw_g, gcw_v = ffn_mid_bwd(hu, dact, ffn_cw8, ffn_cb)
    dhf = matmul(dhu, a_up, mode="nt", tm=1024, tn=1024, tk=1408, out_dtype=F32, name="mm_dhf",
                 layout=dict(m=s, n=D_MODEL, k=2 * D_FF, a_spec=shard(lambda i, j, kk: (kk // 2, i, kk % 2)),
                             b_spec=shard(lambda i, j, kk: (kk, 0, 0))))
    g_up = matmul(hf_t, dhu, mode="nn", tm=1024, tn=1408, tk=1024, out_dtype=BF16, name="mm_dw_up",
                  layout=dict(m=D_MODEL, n=2 * D_FF, k=s, b_spec=shard(lambda i, j, kk: (j // 2, kk, j % 2)),
                              o_spec=shard(lambda i, j, kk: (j, i, 0)), out_shape=(4, D_MODEL, 1408)))
    dx1, g_norm_ffn, dmixed = rms_bwd_matmul(dhf, x1, norm_ffn_w, dy, w_out, name="rms_ffn_bwd_dmixed")
    g_out_a = matmul(y_ssd_t, dx1, mode="nn", tm=1024, tn=1024, tk=1024, out_dtype=BF16, name="mm_dw_out_ssd")
    g_out_b = matmul(y_fox_t, dx1, mode="nn", tm=1024, tn=1024, tk=1024, out_dtype=BF16, name="mm_dw_out_fox")
    early = [jnp.concatenate([g_out_a, g_out_b], axis=0).reshape(4, 512, D_MODEL), g_up, g_down.reshape(4, 704, D_MODEL)]
    dz, dxs, db, dc, dsmall_ssd, gcw_x, gcw_b, gcw_c, g_sp, g_ssd_nw, theirs = ssd_bwd(
        proj, ssd_cw8, ssd_cb, smallp, ssd_nw, ypre, states, dmixed, sel, swap=early)
    core = lax.axis_index("c").astype(jnp.int32).reshape(1)
    parts = [add_pair(a, b, core, name="add_pair_" + n, tr=ADAM_ROWS[n]) for a, b, n in zip(early, theirs, BIG_NAMES[1:])]
    dqn, dkn, dv, dc0, dc1, landed_early = fox_bwd(qn, kn, aq, ak, knt, akt, vb, lse, dmixed, parts=parts)
    dq, dk, dsmall_fox, g_qw, g_kw, g_fb = fox_post(dqn, dkn, dc0, dc1, proj, smallp, qw_t, kw_t, sel, selt,
                                                    fold_even, fold_odd)
    dproj = jnp.concatenate([dz, dxs, dq, dk, dv.astype(BF16), db, dc, (dsmall_ssd + dsmall_fox).astype(BF16)], axis=1)
    g_wx = matmul(h_t, dproj, mode="nn", tm=1024, tn=PROJ_TILE, tk=1024, out_dtype=BF16, name="mm_dw_in")
    g_in = _in_grad_shards(g_wx)
    part_in = add_pair(g_in, pair_swap_halves([g_in], name="pair_swap_w_in")[0], core, name="add_pair_w_in",
                       tr=ADAM_ROWS["w_in"])
    dh, landed_in = matmul(dproj, wx, mode="nt", tm=1024, tn=1024, tk=PROJ_TILE, out_dtype=F32, name="mm_dh",
                           scatter=[part_in])
    grad_x, g_norm_mix = rms_bwd(dh, x, norm_mix_w, dx1, name="rms_mix_bwd")
    return dict(
        sq=sq, grad_x=grad_x, landed=landed_in + landed_early,
        g_norm_mix=g_norm_mix, g_norm_ffn=g_norm_ffn, g_ssd_nw=g_ssd_nw,
        g_ssd_cw=jnp.concatenate([gcw_x, gcw_b, gcw_c], axis=1), g_sp=g_sp, g_fb=g_fb, g_qw=g_qw, g_kw=g_kw,
        g_ffn_cw=jnp.concatenate([gcw_g, gcw_v], axis=1))


def adamw(w, g, m, v, *, name, tr, allreduce=None):
    rows, cols = w.shape
    nsteps = rows // tr

    def body(*refs):
        if allreduce is None:
            w_ref, g_ref, m_ref, v_ref, d_ref, mo_ref, vo_ref = refs
        else:
            w_ref, g_ref, m_ref, v_ref, packed_ref, d_ref, mo_ref, vo_ref, summed_ref = refs[:9]
            start, finish = _allreduce_phases(packed_ref, summed_ref, *refs[9:])
            pl.when(pl.program_id(0) == 0)(start)
        gv = g_ref[...]
        mn = ADAM_B1 * m_ref[...] + (1.0 - ADAM_B1) * gv
        vn = ADAM_B2 * v_ref[...] + (1.0 - ADAM_B2) * (gv * gv)
        m_hat = mn / (1.0 - ADAM_B1 ** ADAM_STEP)
        v_hat = vn / (1.0 - ADAM_B2 ** ADAM_STEP)
        d_ref[...] = -ADAM_LR * (m_hat / (jnp.sqrt(v_hat) + ADAM_EPS) + ADAM_WD * w_ref[...])
        mo_ref[...] = mn
        vo_ref[...] = vn
        if allreduce is not None:
            pl.when(pl.program_id(0) == nsteps - 1)(finish)

    blk = pl.BlockSpec((tr, cols), lambda i: (i, 0))
    shp = jax.ShapeDtypeStruct((rows, cols), F32)
    if allreduce is None:
        return pl.pallas_call(
            body, name=name, grid=(nsteps,), in_specs=[blk] * 4, out_specs=[blk] * 3, out_shape=[shp] * 3,
            compiler_params=_params(("parallel",)),
        )(w, g, m, v)
    whole = pl.BlockSpec(memory_space=pltpu.VMEM)
    return pl.pallas_call(
        body, name=name, grid=(nsteps,), in_specs=[blk] * 4 + [whole], out_specs=[blk] * 3 + [whole],
        out_shape=[shp] * 3 + [jax.ShapeDtypeStruct(allreduce.shape, F32)],
        scratch_shapes=_allreduce_scratch(allreduce.shape[0]), compiler_params=_params(("arbitrary",)),
    )(w, g, m, v, allreduce)


def add_pair(full, theirs, core, *, name, tr):
    _, rows, cols = theirs.shape
    nblk = rows // tr

    def body(c_ref, a_ref, b_ref, o_ref):
        o_ref[...] = (a_ref[...].astype(F32) + b_ref[...].astype(F32)).astype(BF16)

    blk = pl.BlockSpec((1, tr, cols), lambda j, i, c: (j, i, 0))
    grid_spec = pltpu.PrefetchScalarGridSpec(
        num_scalar_prefetch=1, grid=(4, nblk),
        in_specs=[pl.BlockSpec((1, tr, cols), lambda j, i, c: (j, c[0] * nblk + i, 0)), blk], out_specs=blk)
    return pl.pallas_call(
        body, name=name, grid_spec=grid_spec, out_shape=jax.ShapeDtypeStruct(theirs.shape, BF16),
        compiler_params=_params(("parallel", "parallel")),
    )(core, full, theirs)


def sum_chips(parts, core, *, name, tr):
    _, rows, cols = parts.shape
    nblk = rows // tr

    def body(c_ref, p_ref, o_ref):
        acc = p_ref[0].astype(F32)
        for k in range(1, 4):
            acc = acc + p_ref[k].astype(F32)
        o_ref[...] = acc

    grid_spec = pltpu.PrefetchScalarGridSpec(
        num_scalar_prefetch=1, grid=(nblk,), in_specs=[pl.BlockSpec((4, tr, cols), lambda i, c: (0, i, 0))],
        out_specs=pl.BlockSpec((tr, cols), lambda i, c: (c[0] * nblk + i, 0)))
    return pl.pallas_call(
        body, name=name, grid_spec=grid_spec, out_shape=jax.ShapeDtypeStruct((2 * rows, cols), F32),
        compiler_params=_params(("parallel",)),
    )(core, parts)


ANY = pl.BlockSpec(memory_space=pl.ANY)


def _place():
    x, y, c = lax.axis_index("x"), lax.axis_index("y"), lax.axis_index("c")
    chips = [(1 - x, y), (x, 1 - y), (1 - x, 1 - y)]
    return x, y, c, chips


def _chunks(rows):
    size = next((c for c in (128, 176, 64, 32, 16, 8) if rows % c == 0), rows)
    return [(r, size) for r in range(0, rows, size)]


def gather_weights(shards):
    n = len(shards)

    def body(*refs):
        start, forward, finish = _gather_phases(refs[:n], refs[n:2 * n], *refs[2 * n:])
        start()
        forward()
        finish()

    gathered = pl.pallas_call(
        body, name="gather_weights", in_specs=[ANY] * n, out_specs=[ANY] * n,
        out_shape=_gather_out_shapes(shards), scratch_shapes=_gather_scratch(n),
    )(*shards)
    return gathered


def _gather_out_shapes(shards):
    return [jax.ShapeDtypeStruct((4,) + s.shape, s.dtype) for s in shards]


def _gather_scratch(n):
    return [pltpu.SemaphoreType.DMA((n, 7)), pltpu.SemaphoreType.DMA((n, 7))]


def _gather_phases(ins, outs, send_sems, recv_sems):
    n = len(ins)
    x, y, c, chips = _place()
    me = 2 * x + y
    sibling = (x, y, 1 - c)
    blks = [2 * cx + cy for cx, cy in chips]

    def half(a, blk, r=0, nr=None):
        rows = ins[a].shape[0] // 2
        return outs[a].at[blk, pl.ds(c * rows + r, rows if nr is None else nr), :]

    def to_chip(a, t, r=0, nr=None):
        rows = ins[a].shape[0] // 2
        return pltpu.make_async_remote_copy(
            src_ref=ins[a].at[pl.ds(c * rows + r, rows if nr is None else nr), :], dst_ref=half(a, me, r, nr),
            send_sem=send_sems.at[a, t], recv_sem=recv_sems.at[a, t], device_id=(*chips[t], c), device_id_type=MESH)

    def from_chip(a, t):
        return pltpu.make_async_remote_copy(
            src_ref=half(a, blks[t]), dst_ref=half(a, blks[t]), send_sem=send_sems.at[a, t], recv_sem=recv_sems.at[a, t],
            device_id=(*chips[t], c), device_id_type=MESH)

    def to_sibling(a, t, r=0, nr=None):
        return pltpu.make_async_remote_copy(
            src_ref=half(a, blks[t], r, nr), dst_ref=half(a, blks[t], r, nr), send_sem=send_sems.at[a, 3 + t],
            recv_sem=recv_sems.at[a, 3 + t], device_id=sibling, device_id_type=MESH)

    def from_sibling(a, t):
        rows = ins[a].shape[0] // 2
        dst = outs[a].at[blks[t], pl.ds((1 - c) * rows, rows), :]
        return pltpu.make_async_remote_copy(
            src_ref=dst, dst_ref=dst, send_sem=send_sems.at[a, 3 + t], recv_sem=recv_sems.at[a, 3 + t],
            device_id=sibling, device_id_type=MESH)

    def own(a, r=0, nr=None):
        return pltpu.make_async_remote_copy(
            src_ref=ins[a].at[pl.ds(r, ins[a].shape[0] if nr is None else nr), :],
            dst_ref=outs[a].at[me, pl.ds(r, ins[a].shape[0] if nr is None else nr), :],
            send_sem=send_sems.at[a, 6], recv_sem=recv_sems.at[a, 6], device_id=sibling, device_id_type=MESH)

    def start():
        for a in range(n):
            for t in range(3):
                for r, nr in _chunks(ins[a].shape[0] // 2):
                    to_chip(a, t, r, nr).start()
            for r, nr in _chunks(ins[a].shape[0]):
                own(a, r, nr).start()

    def forward():
        for a in range(n):
            for t in range(3):
                from_chip(a, t).wait_recv()
                for r, nr in _chunks(ins[a].shape[0] // 2):
                    to_sibling(a, t, r, nr).start()

    def finish():
        for a in range(n):
            for t in range(3):
                from_sibling(a, t).wait_recv()
        for a in range(n):
            for t in range(3):
                to_chip(a, t).wait_send()
                to_sibling(a, t).wait_send()
            own(a).wait()

    return start, forward, finish


def pair_swap_halves(grads, *, name):
    n = len(grads)

    def body(*refs):
        start, finish = _pair_swap_phases(refs[:n], refs[n:2 * n], *refs[2 * n:])
        start()
        finish()

    return pl.pallas_call(
        body, name=name, in_specs=[ANY] * n, out_specs=[ANY] * n, out_shape=_pair_swap_out_shapes(grads),
        scratch_shapes=_pair_swap_scratch(n),
    )(*grads)


def _pair_swap_out_shapes(grads):
    return [jax.ShapeDtypeStruct((4, g.shape[1] // 2, g.shape[2]), g.dtype) for g in grads]


def _pair_swap_scratch(n):
    return [pltpu.SemaphoreType.DMA((n,)), pltpu.SemaphoreType.DMA((n,))]


def _pair_swap_phases(ins, theirs, send_sems, recv_sems):
    n = len(ins)
    x, y, c, _ = _place()
    sibling = (x, y, 1 - c)

    def start():
        for a in range(n):
            rows = ins[a].shape[1] // 2
            for j in range(4):
                for r, nr in _chunks(rows):
                    pltpu.make_async_remote_copy(
                        src_ref=ins[a].at[j, pl.ds((1 - c) * rows + r, nr), :], dst_ref=theirs[a].at[j, pl.ds(r, nr), :],
                        send_sem=send_sems.at[a], recv_sem=recv_sems.at[a], device_id=sibling, device_id_type=MESH).start()

    def finish():
        for a in range(n):
            pltpu.make_async_remote_copy(src_ref=theirs[a], dst_ref=theirs[a], send_sem=send_sems.at[a],
                                         recv_sem=recv_sems.at[a], device_id=sibling, device_id_type=MESH).wait()

    return start, finish


def _scatter_scratch(n):
    return [pltpu.SemaphoreType.DMA((n, 3)), pltpu.SemaphoreType.DMA((n, 3))]


def _keep_own_blocks(landed, parts):
    if not parts:
        return []
    chip = 2 * lax.axis_index("x") + lax.axis_index("y")
    return [lax.dynamic_update_slice(l, lax.dynamic_slice_in_dim(p, chip, 1, axis=0), (chip, 0, 0))
            for l, p in zip(landed, parts)]


def _scatter_phases(ins, outs, send_sems, recv_sems):
    n = len(ins)
    x, y, c, chips = _place()
    me = 2 * x + y
    blks = [2 * cx + cy for cx, cy in chips]

    def start():
        for a in range(n):
            for r, nr in _chunks(ins[a].shape[1]):
                for t in range(3):
                    pltpu.make_async_remote_copy(
                        src_ref=ins[a].at[blks[t], pl.ds(r, nr), :], dst_ref=outs[a].at[me, pl.ds(r, nr), :],
                        send_sem=send_sems.at[a, t], recv_sem=recv_sems.at[a, t],
                        device_id=(*chips[t], c), device_id_type=MESH).start()

    def finish():
        for a in range(n):
            for t in range(3):
                pltpu.make_async_remote_copy(
                    src_ref=outs[a].at[blks[t]], dst_ref=outs[a].at[blks[t]], send_sem=send_sems.at[a, t],
                    recv_sem=recv_sems.at[a, t], device_id=(*chips[t], c), device_id_type=MESH).wait()

    return start, finish


def pair_join_halves(bufs):
    n = len(bufs)

    def body(*refs):
        outs = refs[n:2 * n]
        send_sems, recv_sems = refs[2 * n:]
        x, y, c, _ = _place()
        sibling = (x, y, 1 - c)
        for a in range(n):
            rows = outs[a].shape[0] // 2
            for r, nr in _chunks(rows):
                mine = outs[a].at[pl.ds(c * rows + r, nr), :]
                pltpu.make_async_remote_copy(src_ref=mine, dst_ref=mine, send_sem=send_sems.at[a], recv_sem=recv_sems.at[a],
                                             device_id=sibling, device_id_type=MESH).start()
        for a in range(n):
            rows = outs[a].shape[0] // 2
            pltpu.make_async_remote_copy(
                src_ref=outs[a].at[pl.ds(c * rows, rows), :], dst_ref=outs[a].at[pl.ds((1 - c) * rows, rows), :],
                send_sem=send_sems.at[a], recv_sem=recv_sems.at[a], device_id=sibling, device_id_type=MESH).wait()

    return pl.pallas_call(
        body, name="pair_join_halves", in_specs=[ANY] * n, out_specs=[ANY] * n,
        out_shape=[jax.ShapeDtypeStruct(b.shape, b.dtype) for b in bufs], input_output_aliases={a: a for a in range(n)},
        scratch_shapes=[pltpu.SemaphoreType.DMA((n,)), pltpu.SemaphoreType.DMA((n,))],
    )(*bufs)


def _allreduce_scratch(rows):
    return [pltpu.VMEM((8, rows, 128), F32), pltpu.SemaphoreType.DMA((7,)), pltpu.SemaphoreType.DMA((7,))]


def _allreduce_phases(in_ref, out_ref, gathered, send_sems, recv_sems):
    x, y, c, _ = _place()
    me = 4 * x + 2 * y + c
    flips = [(fx, fy, fc) for fx in (0, 1) for fy in (0, 1) for fc in (0, 1)][1:]
    peers = [((1 - x) if fx else x, (1 - y) if fy else y, (1 - c) if fc else c) for fx, fy, fc in flips]

    def send(t):
        return pltpu.make_async_remote_copy(
            src_ref=in_ref, dst_ref=gathered.at[me], send_sem=send_sems.at[t], recv_sem=recv_sems.at[t],
            device_id=peers[t], device_id_type=MESH)

    def start():
        gathered[me] = in_ref[...]
        for t in range(7):
            send(t).start()

    def finish():
        for t, (px, py, pc) in enumerate(peers):
            slot = gathered.at[4 * px + 2 * py + pc]
            pltpu.make_async_remote_copy(
                src_ref=slot, dst_ref=slot, send_sem=send_sems.at[t], recv_sem=recv_sems.at[t],
                device_id=(px, py, pc), device_id_type=MESH).wait_recv()
        for t in range(7):
            send(t).wait_send()
        acc = gathered[0]
        for k in range(1, 8):
            acc = acc + gathered[k]
        out_ref[...] = acc

    return start, finish


SMALL_NAMES = ("norm_mix_w", "ssd_conv_w", "ssd_conv_b", "ssd_dt_bias", "ssd_a_log", "ssd_d", "ssd_norm_w", "fox_f_bias",
               "fox_q_norm_w", "fox_k_norm_w", "norm_ffn_w", "ffn_conv_w", "ffn_conv_b")
BIG_NAMES = ("w_in", "w_out", "w_up", "w_down")
WEIGHT_ORDER = ("norm_mix_w", "w_in", "ssd_conv_w", "ssd_conv_b", "ssd_dt_bias", "ssd_a_log", "ssd_d", "ssd_norm_w",
                "fox_f_bias", "fox_q_norm_w", "fox_k_norm_w", "w_out", "norm_ffn_w", "w_up", "ffn_conv_w", "ffn_conv_b", "w_down")
ADAM_ROWS = {"w_in": 256, "w_out": 256, "w_up": 256, "w_down": 176}


def _pack(arrays):
    pieces = []
    for a in arrays:
        flat = a.reshape(-1).astype(F32)
        pieces += [flat, jnp.zeros(((-flat.shape[0]) % 1024,), F32)]
    return jnp.concatenate(pieces).reshape(-1, 128)


def _unpack(packed, shapes):
    out, r = [], 0
    for shp in shapes:
        size = 1
        for d in shp:
            size *= d
        nrow = 8 * (-(-size // 1024))
        out.append(packed[r:r + nrow].reshape(-1)[:size].reshape(shp))
        r += nrow
    return out


IN_SHARD = IN_COLS // 4
IN_SEGMENTS = ((0, 2048, 0), (2048, 2560, 5120), (2560, 2576, MAIN_COLS), (2576, 5648, 2048), (5648, 5664, MAIN_COLS + F_LANE))


def _in_cols(shards, lo, hi):
    out = []
    for j in range(4):
        a, b = max(lo, IN_SHARD * j), min(hi, IN_SHARD * (j + 1))
        if a < b:
            out.append(shards[j][:, a - IN_SHARD * j:b - IN_SHARD * j])
    return out


def _in_grad_shards(g):
    shards = []
    for j in range(4):
        pieces = []
        for lo, hi, at in IN_SEGMENTS:
            a, b = max(lo, IN_SHARD * j), min(hi, IN_SHARD * (j + 1))
            if a < b:
                pieces.append(g[:, at + a - lo:at + b - lo])
        shards.append(jnp.concatenate(pieces, axis=1))
    return jnp.stack(shards)


def _pad_rows(a, rows):
    return jnp.pad(a, ((0, rows - a.shape[0]), (0, 0)))


def kernel(x, norm_mix_w, w_in, ssd_conv_w, ssd_conv_b, ssd_dt_bias, ssd_a_log, ssd_d, ssd_norm_w, fox_f_bias, fox_q_norm_w, fox_k_norm_w, w_out, norm_ffn_w, w_up, ffn_conv_w, ffn_conv_b, w_down, loss_target, m_norm_mix_w, m_w_in, m_ssd_conv_w, m_ssd_conv_b, m_ssd_dt_bias, m_ssd_a_log, m_ssd_d, m_ssd_norm_w, m_fox_f_bias, m_fox_q_norm_w, m_fox_k_norm_w, m_w_out, m_norm_ffn_w, m_w_up, m_ffn_conv_w, m_ffn_conv_b, m_w_down, v_norm_mix_w, v_w_in, v_ssd_conv_w, v_ssd_conv_b, v_ssd_dt_bias, v_ssd_a_log, v_ssd_d, v_ssd_norm_w, v_fox_f_bias, v_fox_q_norm_w, v_fox_k_norm_w, v_w_out, v_norm_ffn_w, v_w_up, v_ffn_conv_w, v_ffn_conv_b, v_w_down):
    w = dict(norm_mix_w=norm_mix_w, w_in=w_in, ssd_conv_w=ssd_conv_w, ssd_conv_b=ssd_conv_b, ssd_dt_bias=ssd_dt_bias,
             ssd_a_log=ssd_a_log, ssd_d=ssd_d, ssd_norm_w=ssd_norm_w, fox_f_bias=fox_f_bias, fox_q_norm_w=fox_q_norm_w,
             fox_k_norm_w=fox_k_norm_w, w_out=w_out, norm_ffn_w=norm_ffn_w, w_up=w_up, ffn_conv_w=ffn_conv_w,
             ffn_conv_b=ffn_conv_b, w_down=w_down)
    m = dict(norm_mix_w=m_norm_mix_w, w_in=m_w_in, ssd_conv_w=m_ssd_conv_w, ssd_conv_b=m_ssd_conv_b, ssd_dt_bias=m_ssd_dt_bias,
             ssd_a_log=m_ssd_a_log, ssd_d=m_ssd_d, ssd_norm_w=m_ssd_norm_w, fox_f_bias=m_fox_f_bias, fox_q_norm_w=m_fox_q_norm_w,
             fox_k_norm_w=m_fox_k_norm_w, w_out=m_w_out, norm_ffn_w=m_norm_ffn_w, w_up=m_w_up, ffn_conv_w=m_ffn_conv_w,
             ffn_conv_b=m_ffn_conv_b, w_down=m_w_down)
    v = dict(norm_mix_w=v_norm_mix_w, w_in=v_w_in, ssd_conv_w=v_ssd_conv_w, ssd_conv_b=v_ssd_conv_b, ssd_dt_bias=v_ssd_dt_bias,
             ssd_a_log=v_ssd_a_log, ssd_d=v_ssd_d, ssd_norm_w=v_ssd_norm_w, fox_f_bias=v_fox_f_bias, fox_q_norm_w=v_fox_q_norm_w,
             fox_k_norm_w=v_fox_k_norm_w, w_out=v_w_out, norm_ffn_w=v_norm_ffn_w, w_up=v_w_up, ffn_conv_w=v_ffn_conv_w,
             ffn_conv_b=v_ffn_conv_b, w_down=v_w_down)
    chip = 2 * lax.axis_index("x") + lax.axis_index("y")

    a_in, a_scw, a_fcw = gather_weights([w_in[0].astype(BF16), _pad_rows(ssd_conv_w[0], 16), _pad_rows(ffn_conv_w[0], 16)])
    later_shards = [w_out[0].astype(BF16), w_up[0].astype(BF16), w_down[0].astype(BF16)]
    wx = jnp.concatenate([p for lo, hi, _ in sorted(IN_SEGMENTS, key=lambda seg: seg[2]) for p in _in_cols(a_in, lo, hi)]
                         + [jnp.zeros((D_MODEL, PROJ_COLS - IN_COLS), BF16)], axis=1)
    ssd_cw8 = a_scw.transpose(1, 0, 2).reshape(16, 1536)[:8]
    ffn_cw8 = a_fcw.transpose(1, 0, 2).reshape(16, 2 * D_FF)[:8]
    gap = lambda n: jnp.zeros((n,), F32)
    smallp = jnp.concatenate([ssd_dt_bias[0], gap(112), ssd_a_log[0], gap(112), ssd_d[0], gap(112),
                              gap(F_LANE), fox_f_bias[0], gap(128 - F_LANE - N_HEADS), gap(4 * 128)]).reshape(8, 128)
    qw_t = jnp.tile(fox_q_norm_w[0], N_HEADS)[None]
    kw_t = jnp.tile(fox_k_norm_w[0], N_HEADS)[None]
    sel = jnp.asarray((np.arange(1024)[:, None] // HEAD_DIM == np.arange(128)[None, :]).astype(np.float32), BF16)

    res = local_step(x[0], loss_target[0], wx, later_shards, ssd_cw8, ssd_conv_b, smallp, ssd_norm_w, qw_t, kw_t,
                     sel, sel.T, norm_mix_w, norm_ffn_w, ffn_cw8, ffn_conv_b)

    full_shapes = [(1, 1024), (1, 4, 1536), (1, 1536), (1, 16), (1, 16), (1, 16), (1, 1024), (1, 16), (1, 64), (1, 64),
                   (1, 1024), (1, 3, 2 * D_FF), (1, 2 * D_FF), (1,)]
    local_small = [res["g_norm_mix"], res["g_ssd_cw"][:4], res["g_ssd_cw"][4], res["g_sp"][0, :16], res["g_sp"][1, :16],
                   res["g_sp"][2, :16], res["g_ssd_nw"], res["g_fb"][0, F_LANE:F_LANE + 16],
                   res["g_qw"].reshape(N_HEADS, HEAD_DIM).sum(0), res["g_kw"].reshape(N_HEADS, HEAD_DIM).sum(0),
                   res["g_norm_ffn"], res["g_ffn_cw"][:3], res["g_ffn_cw"][3], jnp.sum(res["sq"])]
    landed = res["landed"]
    core = lax.axis_index("c").astype(jnp.int32).reshape(1)
    halves = [sum_chips(p, core, name="sum_chips_" + n, tr=ADAM_ROWS[n]) for p, n in zip(landed, BIG_NAMES)]
    g_big = dict(zip(BIG_NAMES, pair_join_halves(halves)))

    grads, deltas, new_m, new_v = {}, {}, {}, {}
    for n in BIG_NAMES:
        out = adamw(w[n][0], g_big[n], m[n][0], v[n][0], name="adamw_" + n, tr=ADAM_ROWS[n],
                    allreduce=_pack(local_small) if n == BIG_NAMES[0] else None)
        if n == BIG_NAMES[0]:
            summed = _unpack(out[3], full_shapes)
        d, mn, vn = out[:3]
        grads[n], deltas[n], new_m[n], new_v[n] = g_big[n][None], d[None], mn[None], vn[None]
    loss = (0.5 / D_MODEL) * summed[-1][0]
    g_small = dict(zip(SMALL_NAMES, summed[:-1]))
    g_small["ssd_conv_w"] = lax.dynamic_slice(g_small["ssd_conv_w"], (0, 0, 384 * chip), (1, 4, 384))
    g_small["ffn_conv_w"] = lax.dynamic_slice(g_small["ffn_conv_w"], (0, 0, 1408 * chip), (1, 3, 1408))
    shapes = [w[n].shape for n in SMALL_NAMES]
    packed_w = _pack([w[n] for n in SMALL_NAMES])
    d, mn, vn = adamw(packed_w, _pack([g_small[n] for n in SMALL_NAMES]), _pack([m[n] for n in SMALL_NAMES]),
                      _pack([v[n] for n in SMALL_NAMES]), name="adamw_small", tr=packed_w.shape[0])
    for n, dd, mm, vv in zip(SMALL_NAMES, _unpack(d, shapes), _unpack(mn, shapes), _unpack(vn, shapes)):
        grads[n], deltas[n], new_m[n], new_v[n] = g_small[n].reshape(w[n].shape), dd, mm, vv
    return (loss, res["grad_x"][None], *[grads[n] for n in WEIGHT_ORDER], *[deltas[n] for n in WEIGHT_ORDER],
            *[new_m[n] for n in WEIGHT_ORDER], *[new_v[n] for n in WEIGHT_ORDER])
```

```python
import functools

import jax
import jax.numpy as jnp
import numpy as np
from jax import lax
from jax.experimental import pallas as pl
from jax.experimental.pallas import tpu as pltpu

F32 = jnp.float32
BF16 = jnp.bfloat16
MESH = pl.DeviceIdType.MESH

D_MODEL = 1024
HEAD_DIM = 64
N_HEADS = 16
N_PAIRS = N_HEADS // 2
SSD_CHUNK = 128
SSD_STATE = 128
SSD_CONV = 4
D_FF = 2816
FFN_CONV = 3
NORM_EPS = 1e-6
MAIN_COLS = 5632
SMALL_COLS = 128
PROJ_COLS = MAIN_COLS + SMALL_COLS
SMALL_BLOCK = MAIN_COLS // SMALL_COLS
PROJ_TILE = 1152
F_LANE = 16
IN_COLS = 5664

ADAM_LR = 0.001
ADAM_B1 = 0.9
ADAM_B2 = 0.999
ADAM_EPS = 1e-08
ADAM_WD = 0.01
ADAM_STEP = 10

VMEM_LIMIT_V7X = 56 * 1024 * 1024
NEG_BIG = -1e30


def _params(sem=None):
    return pltpu.CompilerParams(dimension_semantics=sem, vmem_limit_bytes=VMEM_LIMIT_V7X)


def _sigmoid(x):
    return 1.0 / (1.0 + jnp.exp(-x))


def _silu_and_grad(x):
    s = _sigmoid(x)
    return x * s, s * (1.0 + x * (1.0 - s))


def _shift_down(v, j):
    return v if j == 0 else pltpu.roll(v, j, 0)


def _shift_up(v, j):
    return v if j == 0 else pltpu.roll(v, v.shape[0] - j, 0)


def _row_iota(shape):
    return lax.broadcasted_iota(jnp.int32, shape, 0)


def _lane_iota(shape):
    return lax.broadcasted_iota(jnp.int32, shape, 1)


def _dot(a, b, mode="nn"):
    dims = {"nn": (((1,), (0,)), ((), ())), "nt": (((1,), (1,)), ((), ())), "tn": (((0,), (0,)), ((), ()))}[mode]
    return lax.dot_general(a.astype(BF16), b.astype(BF16), dims, preferred_element_type=F32)


def _dot_f32(a, b):
    return jnp.dot(a, b, precision=lax.Precision.HIGHEST, preferred_element_type=F32)


def matmul(a, b, *, mode, tm, tn, tk, out_dtype, name, add=None, b_koff=0, scatter=(), layout=None):
    layout = layout or {}
    if layout:
        m, n, k = layout["m"], layout["n"], layout["k"]
    else:
        (m, k), n = a.shape, (b.shape[1] if mode == "nn" else b.shape[0])
    assert m % tm == 0 and n % tn == 0 and k % tk == 0, (name, m, n, k, tm, tn, tk)
    nk = k // tk
    grid = (m // tm, n // tn, nk)
    a_spec = layout.get("a_spec") or pl.BlockSpec((tm, tk), lambda i, j, kk: (i, kk))
    b_spec = layout.get("b_spec") or (pl.BlockSpec((tn, tk), lambda i, j, kk: (j, kk + b_koff)) if mode == "nt"
                                      else pl.BlockSpec((tk, tn), lambda i, j, kk: (kk + b_koff, j)))
    o_spec = layout.get("o_spec") or pl.BlockSpec((tm, tn), lambda i, j, kk: (i, j))
    out_struct = jax.ShapeDtypeStruct(layout.get("out_shape", (m, n)), out_dtype)
    has_add = add is not None
    n_in = 3 if has_add else 2
    ns = len(scatter)

    def body(*refs):
        a_ref, b_ref = refs[:2]
        add_ref = refs[2] if has_add else None
        o_ref, acc_ref = refs[n_in + ns], refs[n_in + 2 * ns + 1]
        kk = pl.program_id(2)
        if ns:
            step = (pl.program_id(0) * grid[1] + pl.program_id(1)) * grid[2] + kk
            start, finish_copies = _scatter_phases(refs[n_in:n_in + ns], refs[n_in + ns + 1:n_in + 2 * ns + 1],
                                                   *refs[n_in + 2 * ns + 2:])
            pl.when(step == 0)(start)
        part = _dot(a_ref[...], b_ref[...], mode)

        def finish(total):
            if has_add:
                total = total + add_ref[...]
            o_ref[...] = total.astype(out_dtype)

        if nk == 1:
            finish(part)
        else:
            @pl.when(kk == 0)
            def _():
                acc_ref[...] = part

            @pl.when(jnp.logical_and(kk > 0, kk < nk - 1))
            def _():
                acc_ref[...] += part

            @pl.when(kk == nk - 1)
            def _():
                finish(acc_ref[...] + part)

        if ns:
            pl.when(step == grid[0] * grid[1] * grid[2] - 1)(finish_copies)

    in_specs = [a_spec, b_spec] + ([o_spec] if has_add else [])
    args = (a, b) + ((add,) if has_add else ())
    acc = pltpu.VMEM((tm, tn) if nk > 1 else (8, 128), F32)
    if not ns:
        return pl.pallas_call(
            body, name=name, grid=grid, in_specs=in_specs, out_specs=o_spec, out_shape=out_struct,
            scratch_shapes=[acc], compiler_params=_params(("parallel", "parallel", "arbitrary")),
        )(*args)
    outs = pl.pallas_call(
        body, name=name, grid=grid, in_specs=in_specs + [ANY] * ns, out_specs=[o_spec] + [ANY] * ns,
        out_shape=[out_struct] + [jax.ShapeDtypeStruct(p.shape, p.dtype) for p in scatter],
        scratch_shapes=[acc] + _scatter_scratch(ns), compiler_params=_params(("arbitrary", "arbitrary", "arbitrary")),
    )(*args, *scatter)
    return outs[0], _keep_own_blocks(outs[1:], scatter)


def rms_fwd(x, w, *, name, tm=1024):
    s, d = x.shape

    def body(x_ref, w_ref, h_ref, ht_ref):
        xv = x_ref[...]
        r = lax.rsqrt(jnp.mean(xv * xv, axis=-1, keepdims=True) + NORM_EPS)
        h = (xv * r) * w_ref[...]
        h_ref[...] = h.astype(BF16)
        ht_ref[...] = h.T.astype(BF16)

    return pl.pallas_call(
        body, name=name, grid=(s // tm,),
        in_specs=[pl.BlockSpec((tm, d), lambda i: (i, 0)), pl.BlockSpec((1, d), lambda i: (0, 0))],
        out_specs=[pl.BlockSpec((tm, d), lambda i: (i, 0)), pl.BlockSpec((d, tm), lambda i: (0, i))],
        out_shape=[jax.ShapeDtypeStruct((s, d), BF16), jax.ShapeDtypeStruct((d, s), BF16)],
        compiler_params=_params(("parallel",)),
    )(x, w)


def rms_bwd(dh, x, w, resid, *, name, tm=1024):
    s, d = x.shape

    def body(dh_ref, x_ref, w_ref, res_ref, dx_ref, dw_ref):
        xv = x_ref[...]
        dhv = dh_ref[...]
        r = lax.rsqrt(jnp.mean(xv * xv, axis=-1, keepdims=True) + NORM_EPS)
        xh = xv * r
        g = dhv * w_ref[...]
        dx_ref[...] = res_ref[...] + r * (g - xh * jnp.mean(g * xh, axis=-1, keepdims=True))
        part = jnp.sum(dhv * xh, axis=0, keepdims=True)

        @pl.when(pl.program_id(0) == 0)
        def _():
            dw_ref[...] = part

        @pl.when(pl.program_id(0) > 0)
        def _():
            dw_ref[...] += part

    row = pl.BlockSpec((tm, d), lambda i: (i, 0))
    vec = pl.BlockSpec((1, d), lambda i: (0, 0))
    return pl.pallas_call(
        body, name=name, grid=(s // tm,), in_specs=[row, row, vec, row], out_specs=[row, vec],
        out_shape=[jax.ShapeDtypeStruct((s, d), F32), jax.ShapeDtypeStruct((1, d), F32)],
        compiler_params=_params(("arbitrary",)),
    )(dh, x, w, resid)


def out_proj_rms_fwd(y_ssd, y_fox, w_out, x, norm_w, *, tm=512):
    s, d = x.shape

    def body(ys_ref, yf_ref, w_ref, x_ref, nw_ref, x1_ref, h_ref, ht_ref):
        for r in range(0, tm, UP_ROWS):
            rows = slice(r, r + UP_ROWS)
            x1 = x_ref[rows, :] + _dot(ys_ref[rows, :], w_ref[0:d, :]) + _dot(yf_ref[rows, :], w_ref[d:2 * d, :])
            x1_ref[rows, :] = x1
            rstd = lax.rsqrt(jnp.mean(x1 * x1, axis=-1, keepdims=True) + NORM_EPS)
            h = (x1 * rstd) * nw_ref[...]
            h_ref[rows, :] = h.astype(BF16)
            ht_ref[:, rows] = h.T.astype(BF16)

    row = pl.BlockSpec((tm, d), lambda i: (i, 0))
    return pl.pallas_call(
        body, name="out_proj_rms_fwd", grid=(s // tm,),
        in_specs=[row, row, pl.BlockSpec((2 * d, d), lambda i: (0, 0)), row, pl.BlockSpec((1, d), lambda i: (0, 0))],
        out_specs=[row, row, pl.BlockSpec((d, tm), lambda i: (0, i))],
        out_shape=[jax.ShapeDtypeStruct((s, d), F32), jax.ShapeDtypeStruct((s, d), BF16), jax.ShapeDtypeStruct((d, s), BF16)],
        compiler_params=_params(("parallel",)),
    )(y_ssd, y_fox, w_out, x, norm_w)


def rms_bwd_matmul(dh, x, w, resid, b, *, name, tm=512):
    s, d = x.shape
    n = b.shape[0]

    def body(dh_ref, x_ref, w_ref, res_ref, b_ref, dx_ref, dw_ref, prod_ref):
        part = jnp.zeros((1, d), F32)
        for r in range(0, tm, UP_ROWS):
            rows = slice(r, r + UP_ROWS)
            xv, dhv = x_ref[rows, :], dh_ref[rows, :]
            rstd = lax.rsqrt(jnp.mean(xv * xv, axis=-1, keepdims=True) + NORM_EPS)
            xh = xv * rstd
            g = dhv * w_ref[...]
            dx = res_ref[rows, :] + rstd * (g - xh * jnp.mean(g * xh, axis=-1, keepdims=True))
            dx_ref[rows, :] = dx
            prod_ref[rows, :] = _dot(dx, b_ref[...], "nt")
            part = part + jnp.sum(dhv * xh, axis=0, keepdims=True)

        @pl.when(pl.program_id(0) == 0)
        def _():
            dw_ref[...] = part

        @pl.when(pl.program_id(0) > 0)
        def _():
            dw_ref[...] += part

    row = pl.BlockSpec((tm, d), lambda i: (i, 0))
    vec = pl.BlockSpec((1, d), lambda i: (0, 0))
    return pl.pallas_call(
        body, name=name, grid=(s // tm,), in_specs=[row, row, vec, row, pl.BlockSpec((n, d), lambda i: (0, 0))],
        out_specs=[row, vec, pl.BlockSpec((tm, n), lambda i: (i, 0))],
        out_shape=[jax.ShapeDtypeStruct((s, d), F32), jax.ShapeDtypeStruct((1, d), F32), jax.ShapeDtypeStruct((s, n), F32)],
        compiler_params=_params(("arbitrary",)),
    )(dh, x, w, resid, b)


def down_proj_loss(act, w_down, x1, target, *, tm=512):
    s, d = x1.shape

    def body(a_ref, w_ref, x1_ref, t_ref, dy_ref, sq_ref):
        part = jnp.zeros((1, d), F32)
        for r in range(0, tm, UP_ROWS):
            rows = slice(r, r + UP_ROWS)
            e = x1_ref[rows, :] + _dot(a_ref[rows, :], w_ref[...]) - t_ref[rows, :]
            dy_ref[rows, :] = e / float(d)
            part = part + jnp.sum(e * e, axis=0, keepdims=True)

        @pl.when(pl.program_id(0) == 0)
        def _():
            sq_ref[...] = part

        @pl.when(pl.program_id(0) > 0)
        def _():
            sq_ref[...] += part

    row = pl.BlockSpec((tm, d), lambda i: (i, 0))
    vec = pl.BlockSpec((1, d), lambda i: (0, 0))
    return pl.pallas_call(
        body, name="down_proj_loss", grid=(s // tm,),
        in_specs=[pl.BlockSpec((tm, D_FF), lambda i: (i, 0)), pl.BlockSpec((D_FF, d), lambda i: (0, 0)), row, row],
        out_specs=[row, vec], out_shape=[jax.ShapeDtypeStruct((s, d), F32), jax.ShapeDtypeStruct((1, d), F32)],
        compiler_params=_params(("arbitrary",)),
    )(act, w_down, x1, target)


def _row_shifts(ext, k_taps):
    return [_shift_down(ext, j) for j in range(k_taps)]


def _conv_rows(shifts, w):
    k_taps = len(shifts)
    acc = w[k_taps - 1:k_taps, :] * shifts[0]
    for k in range(k_taps - 1):
        acc = acc + w[k:k + 1, :] * shifts[k_taps - 1 - k]
    return acc


def _conv_weight_grad(dcur, shifts, rows, width):
    k_taps = len(shifts)
    out = [jnp.sum(dcur * shifts[k_taps - 1 - k][rows], axis=0, keepdims=True) for k in range(k_taps)]
    out.append(jnp.sum(dcur, axis=0, keepdims=True))
    return _stack_rows(out, width)


def _conv_rows_transposed(dext, w, k_taps):
    acc = w[k_taps - 1:k_taps, :] * dext
    for k in range(k_taps - 1):
        acc = acc + w[k:k + 1, :] * _shift_up(dext, k_taps - 1 - k)
    return acc


def _stack_rows(rows, width):
    ri = _row_iota((8, width))
    out = jnp.zeros((8, width), F32)
    for k, r in enumerate(rows):
        out = out + jnp.where(ri == k, r, 0.0)
    return out


UP_SHARD = 1408
UP_ROWS = 256


def up_ffn_fwd(hf, a_up, conv_w8, conv_b, *, tm=512):
    s = hf.shape[0]

    def body(a_ref, bg_ref, bv_ref, wg_ref, wv_ref, cbg_ref, cbv_ref, hu_ref, act_ref, actt_ref, carry):
        i, j = pl.program_id(0), pl.program_id(1)
        prev_g = jnp.where(i == 0, 0.0, carry[0, j])
        prev_v = jnp.where(i == 0, 0.0, carry[1, j])
        for r in range(0, tm, UP_ROWS):
            rows = slice(r, r + UP_ROWS)
            a = a_ref[rows, :]
            hg, hv = _dot(a, bg_ref[...]), _dot(a, bv_ref[...])
            hu_ref[0, rows, :] = hg
            hu_ref[1, rows, :] = hv
            gc = _conv_rows(_row_shifts(jnp.concatenate([prev_g, hg], axis=0), FFN_CONV), wg_ref[...])[8:] + cbg_ref[...]
            vc = _conv_rows(_row_shifts(jnp.concatenate([prev_v, hv], axis=0), FFN_CONV), wv_ref[...])[8:] + cbv_ref[...]
            act = gc * _sigmoid(gc) * vc
            act_ref[rows, :] = act.astype(BF16)
            actt_ref[:, rows] = act.T.astype(BF16)
            prev_g, prev_v = hg[UP_ROWS - 8:], hv[UP_ROWS - 8:]
        carry[0, j] = prev_g
        carry[1, j] = prev_v

    shard = lambda off: pl.BlockSpec((None, D_MODEL, UP_SHARD), lambda i, j: (j + off, 0, 0))
    taps = lambda off: pl.BlockSpec((8, UP_SHARD), lambda i, j: (0, j + off))
    bias = lambda off: pl.BlockSpec((1, UP_SHARD), lambda i, j: (0, j + off))
    return pl.pallas_call(
        body, name="up_ffn_fwd", grid=(s // tm, 2),
        in_specs=[pl.BlockSpec((tm, D_MODEL), lambda i, j: (i, 0)), shard(0), shard(2), taps(0), taps(2), bias(0), bias(2)],
        out_specs=[pl.BlockSpec((2, tm, UP_SHARD), lambda i, j: (0, i, j)), pl.BlockSpec((tm, UP_SHARD), lambda i, j: (i, j)),
                   pl.BlockSpec((UP_SHARD, tm), lambda i, j: (j, i))],
        out_shape=[jax.ShapeDtypeStruct((2, s, D_FF), F32), jax.ShapeDtypeStruct((s, D_FF), BF16),
                   jax.ShapeDtypeStruct((D_FF, s), BF16)],
        scratch_shapes=[pltpu.VMEM((2, 2, 8, UP_SHARD), F32)], compiler_params=_params(("arbitrary", "arbitrary")),
    )(hf, a_up, a_up, conv_w8, conv_w8, conv_b, conv_b)


def ffn_mid_bwd(hu, dact, conv_w8, conv_b, *, tm=1024, tc=256):
    s = hu.shape[1]
    ncol = D_FF // tc
    nrow = s // tm
    r8 = tm // 8

    def body(g_ref, v_ref, gp_ref, vp_ref, gn_ref, vn_ref, da_ref, dan_ref, wg_ref, wv_ref, bg_ref, bv_ref,
             dhu_ref, wgo_ref, wvo_ref):
        i = pl.program_id(1)
        first = i == 0
        last = i == nrow - 1

        def ext_of(cur_ref, prev_ref, next_ref):
            prev = jnp.where(first, 0.0, prev_ref[...])
            return jnp.concatenate([prev, cur_ref[...], next_ref[...]], axis=0)

        g_sh = _row_shifts(ext_of(g_ref, gp_ref, gn_ref), FFN_CONV)
        v_sh = _row_shifts(ext_of(v_ref, vp_ref, vn_ref), FFN_CONV)
        gc = _conv_rows(g_sh, wg_ref[...]) + bg_ref[...]
        vc = _conv_rows(v_sh, wv_ref[...]) + bv_ref[...]
        da_ext = jnp.concatenate([jnp.zeros((8, tc), F32), da_ref[...], jnp.where(last, 0.0, dan_ref[...])], axis=0)
        silu, dsilu = _silu_and_grad(gc)
        dgc = da_ext * vc * dsilu
        dvc = da_ext * silu
        dhu_ref[0] = _conv_rows_transposed(dgc, wg_ref[...], FFN_CONV)[8:8 + tm].astype(BF16)
        dhu_ref[1] = _conv_rows_transposed(dvc, wv_ref[...], FFN_CONV)[8:8 + tm].astype(BF16)

        cur = slice(8, 8 + tm)
        pg = _conv_weight_grad(dgc[cur], g_sh, cur, tc)
        pv = _conv_weight_grad(dvc[cur], v_sh, cur, tc)

        @pl.when(first)
        def _():
            wgo_ref[...] = pg
            wvo_ref[...] = pv

        @pl.when(i > 0)
        def _():
            wgo_ref[...] += pg
            wvo_ref[...] += pv

    def prev_idx(i):
        return jnp.maximum(i * r8 - 1, 0)

    def next_idx(i):
        return jnp.minimum((i + 1) * r8, s // 8 - 1)

    half = lambda k, rows, row_index: pl.BlockSpec((None, rows, tc), lambda j, i: (k, row_index(i), j))
    in_specs = [
        half(0, tm, lambda i: i), half(1, tm, lambda i: i),
        half(0, 8, prev_idx), half(1, 8, prev_idx),
        half(0, 8, next_idx), half(1, 8, next_idx),
        pl.BlockSpec((tm, tc), lambda j, i: (i, j)),
        pl.BlockSpec((8, tc), lambda j, i: (next_idx(i), j)),
        pl.BlockSpec((8, tc), lambda j, i: (0, j)),
        pl.BlockSpec((8, tc), lambda j, i: (0, j + ncol)),
        pl.BlockSpec((1, tc), lambda j, i: (0, j)),
        pl.BlockSpec((1, tc), lambda j, i: (0, j + ncol)),
    ]
    out_specs = [pl.BlockSpec((2, tm, tc), lambda j, i: (0, i, j)), pl.BlockSpec((8, tc), lambda j, i: (0, j)),
                 pl.BlockSpec((8, tc), lambda j, i: (0, j))]
    out_shape = [jax.ShapeDtypeStruct((2, s, D_FF), BF16),
                 jax.ShapeDtypeStruct((8, D_FF), F32), jax.ShapeDtypeStruct((8, D_FF), F32)]
    return pl.pallas_call(
        body, name="ffn_mid_bwd", grid=(ncol, nrow), in_specs=in_specs, out_specs=out_specs, out_shape=out_shape,
        compiler_params=_params(("parallel", "arbitrary")),
    )(hu, hu, hu, hu, hu, hu, dact, dact, conv_w8, conv_w8, conv_b, conv_b)


def _softplus(x):
    return jnp.maximum(x, 0.0) + jnp.log(1.0 + jnp.exp(-jnp.abs(x)))


def _cumsum_rows(v):
    n = v.shape[0]
    ri = _row_iota(v.shape)
    sh = 1
    while sh < n:
        v = v + jnp.where(ri >= sh, _shift_down(v, sh), 0.0)
        sh *= 2
    return v


def _rev_cumsum_rows(v):
    n = v.shape[0]
    ri = _row_iota(v.shape)
    sh = 1
    while sh < n:
        v = v + jnp.where(ri < n - sh, _shift_up(v, sh), 0.0)
        sh *= 2
    return v


def _total(v):
    return jnp.sum(jnp.sum(v, axis=1, keepdims=True), axis=0, keepdims=True)


def _ssd_in_specs(rev_nc=None):
    def ch(c):
        return c if rev_nc is None else rev_nc - 1 - c

    def prev(c):
        return jnp.maximum(ch(c) * (SSD_CHUNK // 8) - 1, 0)

    L = SSD_CHUNK
    return [
        pl.BlockSpec((L, 1024), lambda c: (ch(c), 0)),
        pl.BlockSpec((L, 1024), lambda c: (ch(c), 1)),
        pl.BlockSpec((L, 256), lambda c: (ch(c), 20)),
        pl.BlockSpec((L, 256), lambda c: (ch(c), 21)),
        pl.BlockSpec((8, 1024), lambda c: (prev(c), 1)),
        pl.BlockSpec((8, 256), lambda c: (prev(c), 20)),
        pl.BlockSpec((8, 256), lambda c: (prev(c), 21)),
        pl.BlockSpec((8, 1024), lambda c: (0, 0)),
        pl.BlockSpec((8, 256), lambda c: (0, 4)),
        pl.BlockSpec((8, 256), lambda c: (0, 5)),
        pl.BlockSpec((1, 1024), lambda c: (0, 0)),
        pl.BlockSpec((1, 256), lambda c: (0, 4)),
        pl.BlockSpec((1, 256), lambda c: (0, 5)),
        pl.BlockSpec((L, SMALL_COLS), lambda c: (ch(c), SMALL_BLOCK)),
        pl.BlockSpec((8, 128), lambda c: (0, 0)),
        pl.BlockSpec((1, 1024), lambda c: (0, 0)),
    ]


def _ssd_conv_pre(cur_ref, prev_ref, w_ref, b_ref, first):
    prev = jnp.where(first, 0.0, prev_ref[...])
    shifts = _row_shifts(jnp.concatenate([prev, cur_ref[...]], axis=0), SSD_CONV)
    return shifts, _conv_rows(shifts, w_ref[...])[8:] + b_ref[...]


def _ssd_time_consts(small_ref, sp_ref):
    dt_pre = small_ref[...] + sp_ref[0:1, :]
    dt = _softplus(dt_pre)
    a = -jnp.exp(sp_ref[1:2, :])
    acs = _cumsum_rows(dt * a)
    return dt_pre, dt, a, acs


def ssd_fwd(proj, conv_w8, conv_b, smallp, norm_w):
    s = proj.shape[0]
    nc = s // SSD_CHUNK
    L = SSD_CHUNK

    def body(z_ref, xs_ref, b_ref, c_ref, xsp_ref, bp_ref, cp_ref, wx_ref, wb_ref, wc_ref, bx_ref, bb_ref, bc_ref,
             small_ref, sp_ref, nw_ref, y_ref, yt_ref, ypre_ref, st_ref, state):
        first = pl.program_id(0) == 0

        @pl.when(first)
        def _():
            state[...] = jnp.zeros_like(state)

        xs = _ssd_conv_pre(xs_ref, xsp_ref, wx_ref, bx_ref, first)[1]
        xs = xs * _sigmoid(xs)
        bm = _ssd_conv_pre(b_ref, bp_ref, wb_ref, bb_ref, first)[1]
        bm = bm * _sigmoid(bm)
        cm = _ssd_conv_pre(c_ref, cp_ref, wc_ref, bc_ref, first)[1]
        cm = cm * _sigmoid(cm)
        _, dt, _, acs = _ssd_time_consts(small_ref, sp_ref)
        acs_t = acs.T
        li = _lane_iota((L, L))
        ri = _row_iota((L, L))
        tri = ri >= li
        lo = li < HEAD_DIM
        st_ref[0] = state[...]
        for g in range(2):
            bg = bm[:, 128 * g:128 * g + 128]
            cg = cm[:, 128 * g:128 * g + 128]
            gmat = _dot(cg, bg, "nt")
            for pp in range(4):
                p = 4 * g + pp
                h0, h1 = 2 * p, 2 * p + 1
                x = xs[:, 128 * p:128 * p + 128]
                a0, a1 = acs[:, h0:h0 + 1], acs[:, h1:h1 + 1]
                xdt = x * jnp.where(lo, dt[:, h0:h0 + 1], dt[:, h1:h1 + 1])
                m0 = gmat * jnp.exp(jnp.where(tri, a0 - acs_t[h0:h0 + 1, :], NEG_BIG))
                m1 = gmat * jnp.exp(jnp.where(tri, a1 - acs_t[h1:h1 + 1, :], NEG_BIG))
                yd = _dot(m0, jnp.where(lo, xdt, 0.0)) + _dot(m1, jnp.where(lo, 0.0, xdt))
                hin = state[p]
                yo = _dot(cg, hin, "nt") * jnp.exp(jnp.where(lo, a0, a1))
                dskip = jnp.where(lo[0:1], sp_ref[2:3, h0:h0 + 1], sp_ref[2:3, h1:h1 + 1])
                ypre_ref[:, 128 * p:128 * p + 128] = yd + yo + dskip * x
                al0, al1 = acs[L - 1:L, h0:h0 + 1], acs[L - 1:L, h1:h1 + 1]
                w = jnp.exp(jnp.where(lo, al0 - a0, al1 - a1))
                dec = jnp.exp(jnp.where(ri < HEAD_DIM, al0, al1))
                state[p] = dec * hin + _dot(xdt * w, bg, "tn")
        z = z_ref[...]
        yg = ypre_ref[...] * (z * _sigmoid(z))
        for g in range(2):
            seg = yg[:, 512 * g:512 * g + 512]
            r = lax.rsqrt(jnp.mean(seg * seg, axis=-1, keepdims=True) + NORM_EPS)
            out = (seg * r) * nw_ref[:, 512 * g:512 * g + 512]
            y_ref[:, 512 * g:512 * g + 512] = out.astype(BF16)
            yt_ref[512 * g:512 * g + 512, :] = out.T.astype(BF16)

    row = pl.BlockSpec((L, 1024), lambda c: (c, 0))
    return pl.pallas_call(
        body, name="ssd_fwd", grid=(nc,), in_specs=_ssd_in_specs(),
        out_specs=[row, pl.BlockSpec((1024, L), lambda c: (0, c)), row,
                   pl.BlockSpec((1, N_PAIRS, 128, 128), lambda c: (c, 0, 0, 0))],
        out_shape=[jax.ShapeDtypeStruct((s, 1024), BF16), jax.ShapeDtypeStruct((1024, s), BF16),
                   jax.ShapeDtypeStruct((s, 1024), F32), jax.ShapeDtypeStruct((nc, N_PAIRS, 128, 128), F32)],
        scratch_shapes=[pltpu.VMEM((N_PAIRS, 128, 128), F32)],
        compiler_params=_params(("arbitrary",)),
    )(proj, proj, proj, proj, proj, proj, proj, conv_w8, conv_w8, conv_w8, conv_b, conv_b, conv_b, proj, smallp, norm_w)


def ssd_bwd(proj, conv_w8, conv_b, smallp, norm_w, ypre, states, dy, sel, swap=()):
    s = proj.shape[0]
    nc = s // SSD_CHUNK
    L = SSD_CHUNK

    ns = len(swap)
    n_in, n_out, n_scratch = 20, 10, 11

    def body(*refs):
        own = refs[:n_in] + refs[n_in + ns:n_in + ns + n_out] + refs[n_in + 2 * ns + n_out:n_in + 2 * ns + n_out + n_scratch]
        if ns:
            start, finish = _pair_swap_phases(refs[n_in:n_in + ns], refs[n_in + ns + n_out:n_in + 2 * ns + n_out],
                                              *refs[n_in + 2 * ns + n_out + n_scratch:])
            pl.when(pl.program_id(0) == 0)(start)
        compute(*own)
        if ns:
            pl.when(pl.program_id(0) == nc - 1)(finish)

    def compute(z_ref, xs_ref, b_ref, c_ref, xsp_ref, bp_ref, cp_ref, wx_ref, wb_ref, wc_ref, bx_ref, bb_ref, bc_ref,
                small_ref, sp_ref, nw_ref, ypre_ref, st_ref, dy_ref, sel_ref,
                dz_ref, dxs_ref, db_ref, dc_ref, dsmall_ref, gwx_ref, gwb_ref, gwc_ref, gsp_ref, gnw_ref,
                dstate, carry_x, carry_b, carry_c, dxs_buf, dbm_buf, dcm_buf, qcs, col_sums, acs_terms, dt_terms):
        step = pl.program_id(0)
        col_sums[...] = jnp.zeros_like(col_sums)
        first_chunk = step == nc - 1
        start = step == 0

        @pl.when(start)
        def _():
            dstate[...] = jnp.zeros_like(dstate)
            carry_x[...] = jnp.zeros_like(carry_x)
            carry_b[...] = jnp.zeros_like(carry_b)
            carry_c[...] = jnp.zeros_like(carry_c)

        xs_sh, xs_pre = _ssd_conv_pre(xs_ref, xsp_ref, wx_ref, bx_ref, first_chunk)
        b_sh, b_pre = _ssd_conv_pre(b_ref, bp_ref, wb_ref, bb_ref, first_chunk)
        c_sh, c_pre = _ssd_conv_pre(c_ref, cp_ref, wc_ref, bc_ref, first_chunk)
        xs, xs_ds = _silu_and_grad(xs_pre)
        bm, b_ds = _silu_and_grad(b_pre)
        cm, c_ds = _silu_and_grad(c_pre)
        dt_pre, dt, a, acs = _ssd_time_consts(small_ref, sp_ref)
        acs_t = acs.T
        li = _lane_iota((L, L))
        ri = _row_iota((L, L))
        tri = ri >= li
        lo = li < HEAD_DIM
        lo_rows = ri < HEAD_DIM
        li1 = _lane_iota((1, L))

        z = z_ref[...]
        sz, dsz = _silu_and_grad(z)
        y = ypre_ref[...]
        yg = y * sz
        dout = dy_ref[...]
        dyg_parts = []
        gnw_parts = []
        for g in range(2):
            sl = slice(512 * g, 512 * g + 512)
            seg = yg[:, sl]
            r = lax.rsqrt(jnp.mean(seg * seg, axis=-1, keepdims=True) + NORM_EPS)
            n = seg * r
            gnw_parts.append(jnp.sum(dout[:, sl] * n, axis=0, keepdims=True))
            gg = dout[:, sl] * nw_ref[:, sl]
            dyg_parts.append(r * (gg - n * jnp.mean(gg * n, axis=-1, keepdims=True)))
        dyg = jnp.concatenate(dyg_parts, axis=1)
        gnw = jnp.concatenate(gnw_parts, axis=1)
        dz_ref[...] = (dyg * y * dsz).astype(BF16)
        dypre = dyg * sz

        qcs[...] = jnp.zeros_like(qcs)
        dalast = jnp.zeros((1, L), F32)
        for g in range(2):
            bg = bm[:, 128 * g:128 * g + 128]
            cg = cm[:, 128 * g:128 * g + 128]
            gmat = _dot(cg, bg, "nt")
            dgmat = jnp.zeros((L, L), F32)
            dbg = jnp.zeros((L, L), F32)
            dcg = jnp.zeros((L, L), F32)
            for pp in range(4):
                p = 4 * g + pp
                h0, h1 = 2 * p, 2 * p + 1
                lanes = slice(128 * p, 128 * p + 128)
                x = xs[:, lanes]
                dyp = dypre[:, lanes]
                a0, a1 = acs[:, h0:h0 + 1], acs[:, h1:h1 + 1]
                dtl = jnp.where(lo, dt[:, h0:h0 + 1], dt[:, h1:h1 + 1])
                xdt = x * dtl
                l0 = jnp.exp(jnp.where(tri, a0 - acs_t[h0:h0 + 1, :], NEG_BIG))
                l1 = jnp.exp(jnp.where(tri, a1 - acs_t[h1:h1 + 1, :], NEG_BIG))
                m0, m1 = gmat * l0, gmat * l1
                dskip = jnp.where(lo[0:1], sp_ref[2:3, h0:h0 + 1], sp_ref[2:3, h1:h1 + 1])
                col_sums[0:1, lanes] = jnp.sum(dyp * x, axis=0, keepdims=True)
                dx = dyp * dskip
                dy0, dy1 = jnp.where(lo, dyp, 0.0), jnp.where(lo, 0.0, dyp)
                x0, x1 = jnp.where(lo, xdt, 0.0), jnp.where(lo, 0.0, xdt)
                dm0, dm1 = _dot(dy0, x0, "nt"), _dot(dy1, x1, "nt")
                dxdt = _dot(m0, dy0, "tn") + _dot(m1, dy1, "tn")
                q0, q1 = dm0 * m0, dm1 * m1
                qcs[h0:h0 + 1, :] = jnp.sum(q0, axis=0, keepdims=True)
                qcs[h1:h1 + 1, :] = jnp.sum(q1, axis=0, keepdims=True)
                row_terms = jnp.where(lo, q0 + pltpu.roll(q0, HEAD_DIM, 1), q1 + pltpu.roll(q1, HEAD_DIM, 1))
                dgmat = dgmat + dm0 * l0 + dm1 * l1
                hin = st_ref[0, p]
                e = jnp.exp(jnp.where(lo, a0, a1))
                ch = _dot(cg, hin, "nt")
                dch = dyp * e
                dcg = dcg + _dot(dch, hin)
                dhin = _dot(dch, cg, "tn")
                dhout = dstate[p]
                al0, al1 = acs[L - 1:L, h0:h0 + 1], acs[L - 1:L, h1:h1 + 1]
                dec = jnp.exp(jnp.where(lo_rows, al0, al1))
                dhin = dhin + dec * dhout
                dal = dhout * hin * dec
                dal0 = _total(jnp.where(lo_rows, dal, 0.0))
                dal1 = _total(dal) - dal0
                dalast = dalast + jnp.where(li1 == h0, dal0, 0.0) + jnp.where(li1 == h1, dal1, 0.0)
                w = jnp.exp(jnp.where(lo, al0 - a0, al1 - a1))
                xw = xdt * w
                dxw = _dot(bg, dhout, "nt")
                dbg = dbg + _dot(xw, dhout)
                dxdt = dxdt + dxw * w
                dww = dxw * xw
                col_sums[1:2, lanes] = jnp.sum(dww, axis=0, keepdims=True)
                acs_terms[:, lanes] = row_terms + dch * ch - dww
                dx = dx + dxdt * dtl
                dt_terms[:, lanes] = dxdt * x
                dxs_buf[:, lanes] = dx
                dstate[p] = dhin
            dcg = dcg + _dot(dgmat, bg)
            dbg = dbg + _dot(dgmat, cg, "tn")
            dbm_buf[:, 128 * g:128 * g + 128] = dbg
            dcm_buf[:, 128 * g:128 * g + 128] = dcg

        head_sums = _split3_dot(col_sums[...], sel_ref[...])
        dskip_g = head_sums[0:1, :]
        dalast = dalast + head_sums[1:2, :]
        ddt = _split3_dot(dt_terms[...], sel_ref[...])
        dacs_tot = _split3_dot(acs_terms[...], sel_ref[...]) - qcs[...].T + jnp.where(ri == L - 1, dalast, 0.0)
        dstep = _rev_cumsum_rows(dacs_tot)
        ddt = ddt + dstep * a
        head_lane = li < N_HEADS
        ddt_pre = jnp.where(head_lane, ddt * _sigmoid(dt_pre), 0.0)
        dsmall_ref[...] = ddt_pre
        da = jnp.sum(jnp.where(head_lane, dstep * dt, 0.0), axis=0, keepdims=True)
        gsp = _stack_rows([jnp.sum(ddt_pre, axis=0, keepdims=True), da * a, dskip_g], L)

        def conv_back(dpost, ds, shifts, w_ref, carry, out_ref, width):
            dpre = dpost * ds
            dext = jnp.concatenate([dpre, carry[...]], axis=0)
            out_ref[...] = _conv_rows_transposed(dext, w_ref[...], SSD_CONV)[:L].astype(BF16)
            carry[...] = dpre[0:8]
            return _conv_weight_grad(dpre, shifts, slice(8, 8 + L), width)

        gwx = conv_back(dxs_buf[...], xs_ds, xs_sh, wx_ref, carry_x, dxs_ref, 1024)
        gwb = conv_back(dbm_buf[...], b_ds, b_sh, wb_ref, carry_b, db_ref, 256)
        gwc = conv_back(dcm_buf[...], c_ds, c_sh, wc_ref, carry_c, dc_ref, 256)

        @pl.when(start)
        def _():
            gwx_ref[...] = gwx
            gwb_ref[...] = gwb
            gwc_ref[...] = gwc
            gsp_ref[...] = gsp
            gnw_ref[...] = gnw

        @pl.when(step > 0)
        def _():
            gwx_ref[...] += gwx
            gwb_ref[...] += gwb
            gwc_ref[...] += gwc
            gsp_ref[...] += gsp
            gnw_ref[...] += gnw

    def ch(c):
        return nc - 1 - c

    row = pl.BlockSpec((L, 1024), lambda c: (ch(c), 0))
    row256 = pl.BlockSpec((L, 256), lambda c: (ch(c), 0))
    in_specs = _ssd_in_specs(rev_nc=nc) + [row, pl.BlockSpec((1, N_PAIRS, 128, 128), lambda c: (ch(c), 0, 0, 0)), row,
                                           pl.BlockSpec((1024, 128), lambda c: (0, 0))]
    out_specs = [row, row, row256, row256, pl.BlockSpec((L, 128), lambda c: (ch(c), 0)),
                 pl.BlockSpec((8, 1024), lambda c: (0, 0)), pl.BlockSpec((8, 256), lambda c: (0, 0)),
                 pl.BlockSpec((8, 256), lambda c: (0, 0)), pl.BlockSpec((8, 128), lambda c: (0, 0)),
                 pl.BlockSpec((1, 1024), lambda c: (0, 0))]
    out_shape = [jax.ShapeDtypeStruct((s, 1024), BF16), jax.ShapeDtypeStruct((s, 1024), BF16),
                 jax.ShapeDtypeStruct((s, 256), BF16), jax.ShapeDtypeStruct((s, 256), BF16),
                 jax.ShapeDtypeStruct((s, 128), F32),
                 jax.ShapeDtypeStruct((8, 1024), F32), jax.ShapeDtypeStruct((8, 256), F32),
                 jax.ShapeDtypeStruct((8, 256), F32), jax.ShapeDtypeStruct((8, 128), F32),
                 jax.ShapeDtypeStruct((1, 1024), F32)]
    scratch = [pltpu.VMEM((N_PAIRS, 128, 128), F32), pltpu.VMEM((8, 1024), F32), pltpu.VMEM((8, 256), F32),
               pltpu.VMEM((8, 256), F32), pltpu.VMEM((L, 1024), F32), pltpu.VMEM((L, 256), F32), pltpu.VMEM((L, 256), F32),
               pltpu.VMEM((L, L), F32), pltpu.VMEM((8, 1024), F32), pltpu.VMEM((L, 1024), F32), pltpu.VMEM((L, 1024), F32)]
    assert (len(in_specs), len(out_specs), len(scratch)) == (n_in, n_out, n_scratch)
    outs = pl.pallas_call(
        body, name="ssd_bwd", grid=(nc,), in_specs=in_specs + [ANY] * ns, out_specs=out_specs + [ANY] * ns,
        out_shape=out_shape + _pair_swap_out_shapes(swap), scratch_shapes=scratch + (_pair_swap_scratch(ns) if ns else []),
        compiler_params=_params(("arbitrary",)),
    )(proj, proj, proj, proj, proj, proj, proj, conv_w8, conv_w8, conv_w8, conv_b, conv_b, conv_b, proj, smallp, norm_w,
      ypre, states, dy, sel, *swap)
    return (*outs[:n_out], list(outs[n_out:]))


FOX_SCALE = HEAD_DIM ** -0.5
FOX_T = 256
Q_COL, K_COL, V_COL = 2, 3, 4


def _split_dot(v, m, terms):
    out, rest = None, v
    for i in range(terms):
        piece = rest.astype(BF16)
        out = _dot(piece, m) if out is None else out + _dot(piece, m)
        if i + 1 < terms:
            rest = rest - piece.astype(F32)
    return out


def _split3_dot(v, m):
    return _split_dot(v, m, 3)


def _head_mean(x, sel_ref, selt_ref):
    return _dot(x, sel_ref[...]) * (1.0 / HEAD_DIM)


def _head_spread(v, selt_ref):
    return _split_dot(v, selt_ref[...], 2)


def _head_rstd(x, sel_ref, selt_ref):
    return _head_spread(lax.rsqrt(_head_mean(x * x, sel_ref, selt_ref) + NORM_EPS), selt_ref)


def fox_tables():
    r = np.arange(3 * 128)
    piece, lane = r // 128, r % 128
    head = lane - F_LANE
    is_head = np.logical_and(head >= 0, head < N_HEADS)
    col = 128 * (head // 2) + HEAD_DIM * (1 - head % 2) + piece
    cols = np.arange(1024)
    place_q = np.logical_and(is_head[:, None], cols[None, :] == col[:, None])
    place_k = np.logical_and(is_head[:, None], cols[None, :] == (col + 3)[:, None])
    ones_q = np.logical_and(cols % HEAD_DIM >= 3, cols % HEAD_DIM < 6)[None]
    ones_k = (cols % HEAD_DIM < 3)[None]
    h = np.arange(128) - F_LANE
    ok = np.logical_and(h >= 0, h < N_HEADS)
    same_pair = cols[:, None] // 128 == (h // 2)[None, :]
    fold_even = np.logical_and(np.logical_and(ok, h % 2 == 0)[None, :], same_pair)
    fold_odd = np.logical_and(np.logical_and(ok, h % 2 == 1)[None, :], same_pair)
    as_bf16 = lambda t: jnp.asarray(t.astype(np.float32), BF16)
    return (as_bf16(place_q), as_bf16(place_k), jnp.asarray(ones_q, F32), jnp.asarray(ones_k, F32),
            as_bf16(fold_even), as_bf16(fold_odd))


def fox_prep(proj, smallp, qw, kw, sel, selt, place_q, place_k, ones_q, ones_k, *, tm=256):
    s = proj.shape[0]

    def body(q_ref, k_ref, v_ref, small_ref, sp_ref, qw_ref, kw_ref, sel_ref, selt_ref, pq_ref, pk_ref, oq_ref, ok_ref,
             qn_ref, kn_ref, aq_ref, ak_ref, vb_ref, knt_ref, akt_ref, vt_ref, carry):
        @pl.when(pl.program_id(0) == 0)
        def _():
            carry[...] = jnp.zeros_like(carry)

        q = q_ref[...]
        qn_ref[...] = (((q * _head_rstd(q, sel_ref, selt_ref)) * qw_ref[...]) * FOX_SCALE).astype(BF16)
        k = k_ref[...]
        kn = ((k * _head_rstd(k, sel_ref, selt_ref)) * kw_ref[...]).astype(BF16)
        kn_ref[...] = kn
        knt_ref[...] = kn.astype(F32).T.astype(BF16)
        vb_ref[...] = v_ref[...].astype(BF16)
        vt_ref[...] = v_ref[...].T.astype(BF16)
        li = _lane_iota((tm, 128))
        f_lane = jnp.logical_and(li >= F_LANE, li < F_LANE + N_HEADS)
        logf = jnp.where(f_lane, -_softplus(-(small_ref[...] + sp_ref[3:4, :])), 0.0)
        cum = _cumsum_rows(logf) + carry[...]
        carry[...] = cum[tm - 1:tm, :]
        hi = cum.astype(BF16)
        r1 = cum - hi.astype(F32)
        mid = r1.astype(BF16)
        lo = (r1 - mid.astype(F32)).astype(BF16)
        pieces = jnp.concatenate([hi, mid, lo], axis=1)
        aq_ref[...] = (_dot(pieces, pq_ref[...]) + oq_ref[...]).astype(BF16)
        ak = ok_ref[...] - _dot(pieces, pk_ref[...])
        ak_ref[...] = ak.astype(BF16)
        akt_ref[...] = ak.T.astype(BF16)

    row = pl.BlockSpec((tm, 1024), lambda i: (i, 0))
    col = pl.BlockSpec((1024, tm), lambda i: (0, i))
    vec = pl.BlockSpec((1, 1024), lambda i: (0, 0))
    table = pl.BlockSpec((384, 1024), lambda i: (0, 0))
    wide = jax.ShapeDtypeStruct((s, 1024), BF16)
    tall = jax.ShapeDtypeStruct((1024, s), BF16)
    return pl.pallas_call(
        body, name="fox_prep", grid=(s // tm,),
        in_specs=[pl.BlockSpec((tm, 1024), lambda i: (i, Q_COL)), pl.BlockSpec((tm, 1024), lambda i: (i, K_COL)),
                  pl.BlockSpec((tm, 1024), lambda i: (i, V_COL)),
                  pl.BlockSpec((tm, 128), lambda i: (i, SMALL_BLOCK)), pl.BlockSpec((8, 128), lambda i: (0, 0)), vec, vec,
                  pl.BlockSpec((1024, 128), lambda i: (0, 0)), pl.BlockSpec((128, 1024), lambda i: (0, 0)),
                  table, table, vec, vec],
        out_specs=[row, row, row, row, row, col, col, col],
        out_shape=[wide, wide, wide, wide, wide, tall, tall, tall],
        scratch_shapes=[pltpu.VMEM((1, 128), F32)], compiler_params=_params(("arbitrary",)),
    )(proj, proj, proj, proj, smallp, qw, kw, sel, selt, place_q, place_k, ones_q, ones_k)


def fox_fwd(qn, kn, aq, ak, vt, shards=()):
    s = qn.shape[0]
    t = FOX_T
    nq = s // t
    ng = len(shards)

    def body(*refs):
        q_ref, k_ref, aq_ref, ak_ref, vt_ref = refs[:5]
        o_ref, ot_ref, lse_ref = refs[5 + ng:8 + ng]
        p = pl.program_id(0)
        if ng:
            start, forward, finish = _gather_phases(refs[5:5 + ng], refs[8 + ng:8 + 2 * ng], *refs[8 + 2 * ng:])
            pl.when(p == 0)(start)
            pl.when(p == N_PAIRS // 2)(forward)

        @pl.when(p == 0)
        def _():
            lse_ref[...] = jnp.zeros_like(lse_ref)

        lo = _lane_iota((t, 128)) < HEAD_DIM
        lo_rows = _row_iota((128, t)) < HEAD_DIM
        causal_t = _lane_iota((t, t)) >= _row_iota((t, t))

        def q_loop(qi, _):
            q0 = pl.multiple_of(qi * t, t)
            qv, aqv = q_ref[pl.ds(q0, t), :], aq_ref[pl.ds(q0, t), :]
            qa, qb = jnp.where(lo, qv, aqv), jnp.where(lo, aqv, qv)

            def scores(kj):
                k0 = pl.multiple_of(kj * t, t)
                kv, akv = k_ref[pl.ds(k0, t), :], ak_ref[pl.ds(k0, t), :]
                return _dot(jnp.where(lo, kv, akv), qa, "nt"), _dot(jnp.where(lo, akv, kv), qb, "nt")

            def update(kj, stats, s0, s1):
                m0, l0, m1, l1, acc = stats
                vtv = vt_ref[:, pl.ds(pl.multiple_of(kj * t, t), t)]
                n0 = jnp.maximum(m0, jnp.max(s0, axis=0, keepdims=True))
                n1 = jnp.maximum(m1, jnp.max(s1, axis=0, keepdims=True))
                a0, a1 = jnp.exp(m0 - n0), jnp.exp(m1 - n1)
                p0, p1 = jnp.exp(s0 - n0), jnp.exp(s1 - n1)
                l0 = a0 * l0 + jnp.sum(p0, axis=0, keepdims=True)
                l1 = a1 * l1 + jnp.sum(p1, axis=0, keepdims=True)
                acc = (jnp.where(lo_rows, a0, a1) * acc + _dot(jnp.where(lo_rows, vtv, 0.0), p0)
                       + _dot(jnp.where(lo_rows, 0.0, vtv), p1))
                return n0, l0, n1, l1, acc

            def step(kj, carry):
                stats, (s0, s1) = carry[:5], carry[5:]
                nxt = scores(kj + 1)
                return (*update(kj, stats, s0, s1), *nxt)

            def row(val):
                return jnp.full((1, t), val, F32)

            init = (row(NEG_BIG), row(0.0), row(NEG_BIG), row(0.0), jnp.zeros((128, t), F32), *scores(0))
            carry = lax.fori_loop(0, qi, step, init)
            s0, s1 = jnp.where(causal_t, carry[5], NEG_BIG), jnp.where(causal_t, carry[6], NEG_BIG)
            m0, l0, m1, l1, acc = update(qi, carry[:5], s0, s1)
            out_t = acc / jnp.where(lo_rows, l0, l1)
            ot_ref[:, pl.ds(q0, t)] = out_t.astype(BF16)
            o_ref[pl.ds(q0, t), :] = out_t.T.astype(BF16)
            ri = _row_iota((N_HEADS, t))
            old = lse_ref[:, pl.ds(q0, t)]
            lse_ref[:, pl.ds(q0, t)] = jnp.where(
                ri == 2 * p, m0 + jnp.log(l0), jnp.where(ri == 2 * p + 1, m1 + jnp.log(l1), old))
            return 0

        lax.fori_loop(0, nq, q_loop, 0)
        if ng:
            pl.when(p == N_PAIRS - 1)(finish)

    pair = pl.BlockSpec((s, 128), lambda p: (0, p))
    outs = pl.pallas_call(
        body, name="fox_fwd", grid=(N_PAIRS,),
        in_specs=[pair] * 4 + [pl.BlockSpec((128, s), lambda p: (p, 0))] + [ANY] * ng,
        out_specs=[pair, pl.BlockSpec((128, s), lambda p: (p, 0)), pl.BlockSpec((N_HEADS, s), lambda p: (0, 0))] + [ANY] * ng,
        out_shape=[jax.ShapeDtypeStruct((s, 1024), BF16), jax.ShapeDtypeStruct((1024, s), BF16),
                   jax.ShapeDtypeStruct((N_HEADS, s), F32)] + _gather_out_shapes(shards),
        scratch_shapes=_gather_scratch(ng) if ng else [],
        compiler_params=_params(("arbitrary",)),
    )(qn, kn, aq, ak, vt, *shards)
    return outs[0], outs[1], outs[2], list(outs[3:])


def fox_bwd(qn, kn, aq, ak, knt, akt, vb, lse, dmixed, parts=()):
    s = qn.shape[0]
    t = FOX_T
    nq = s // t
    once = pl.Buffered(1)
    ns = len(parts)

    def body(*refs):
        q_ref, k_ref, aq_ref, ak_ref, kt_ref, akt_ref, v_ref, lse_ref, do_ref = refs[:9]
        dq_ref, dk_ref, dv_ref, dc0_ref, dc1_ref = refs[9 + ns:14 + ns]
        p_scr, dp_scr = refs[14 + 2 * ns:16 + 2 * ns]
        p = pl.program_id(0)
        if ns:
            start, finish = _scatter_phases(refs[9:9 + ns], refs[14 + ns:14 + 2 * ns], *refs[16 + 2 * ns:])
            pl.when(p == 0)(start)
        dk_ref[...] = jnp.zeros_like(dk_ref)
        dv_ref[...] = jnp.zeros_like(dv_ref)
        dc0_ref[...] = jnp.zeros_like(dc0_ref)
        dc1_ref[...] = jnp.zeros_like(dc1_ref)
        lo = _lane_iota((t, 128)) < HEAD_DIM
        lo_rows = _row_iota((128, t)) < HEAD_DIM
        causal_t = _lane_iota((t, t)) >= _row_iota((t, t))

        def q_loop(qi, _):
            q0 = pl.multiple_of(qi * t, t)
            qv, aqv = q_ref[pl.ds(q0, t), :], aq_ref[pl.ds(q0, t), :]
            qa, qb = jnp.where(lo, qv, aqv), jnp.where(lo, aqv, qv)
            do = do_ref[pl.ds(q0, t), :]
            doa, dob = jnp.where(lo, do, 0.0).astype(BF16), jnp.where(lo, 0.0, do).astype(BF16)
            lse_blk = lse_ref[:, pl.ds(q0, t)]
            ri = _row_iota((N_HEADS, t))
            lse0 = jnp.sum(jnp.where(ri == 2 * p, lse_blk, 0.0), axis=0, keepdims=True)
            lse1 = jnp.sum(jnp.where(ri == 2 * p + 1, lse_blk, 0.0), axis=0, keepdims=True)

            def scores(kj):
                k0 = pl.multiple_of(kj * t, t)
                kv, akv = k_ref[pl.ds(k0, t), :], ak_ref[pl.ds(k0, t), :]
                return _dot(jnp.where(lo, kv, akv), qa, "nt"), _dot(jnp.where(lo, akv, kv), qb, "nt")

            def pass1(kj, d0, d1, diagonal):
                k0 = pl.multiple_of(kj * t, t)
                vv = v_ref[pl.ds(k0, t), :]
                s0, s1 = scores(kj)
                if diagonal:
                    s0, s1 = jnp.where(causal_t, s0, NEG_BIG), jnp.where(causal_t, s1, NEG_BIG)
                p0, p1 = jnp.exp(s0 - lse0), jnp.exp(s1 - lse1)
                dp0, dp1 = _dot(vv, doa, "nt"), _dot(vv, dob, "nt")
                p_scr[0, kj], p_scr[1, kj] = p0, p1
                dp_scr[0, kj], dp_scr[1, kj] = dp0, dp1
                dv_ref[pl.ds(k0, t), :] += _dot(p0, doa) + _dot(p1, dob)
                return d0 + jnp.sum(p0 * dp0, axis=0, keepdims=True), d1 + jnp.sum(p1 * dp1, axis=0, keepdims=True)

            zero = jnp.zeros((1, t), F32)
            d0, d1 = lax.fori_loop(0, qi, lambda kj, c: pass1(kj, *c, False), (zero, zero))
            d0, d1 = pass1(qi, d0, d1, True)

            def fold_lanes(v):
                return functools.reduce(lambda a, b: a + b, [v[:, 128 * i:128 * (i + 1)] for i in range(t // 128)])

            def pass2(kj, carry):
                dq0, dq1 = carry
                k0 = pl.multiple_of(kj * t, t)
                p0, p1 = p_scr[0, kj], p_scr[1, kj]
                ds0, ds1 = p0 * (dp_scr[0, kj] - d0), p1 * (dp_scr[1, kj] - d1)
                dk_ref[pl.ds(k0, t), :] += jnp.where(lo, _dot(ds0, qa), _dot(ds1, qb))
                dc0_ref[pl.ds(k0, t), :] += fold_lanes(ds0)
                dc1_ref[pl.ds(k0, t), :] += fold_lanes(ds1)
                ktv, aktv = kt_ref[:, pl.ds(k0, t)], akt_ref[:, pl.ds(k0, t)]
                return dq0 + _dot(jnp.where(lo_rows, ktv, aktv), ds0), dq1 + _dot(jnp.where(lo_rows, aktv, ktv), ds1)

            zq = jnp.zeros((128, t), F32)
            dq0, dq1 = lax.fori_loop(0, qi + 1, pass2, (zq, zq))
            dq_ref[pl.ds(q0, t), :] = jnp.where(lo_rows, dq0, dq1).T
            return 0

        lax.fori_loop(0, nq, q_loop, 0)
        if ns:
            pl.when(p == N_PAIRS - 1)(finish)

    pair = pl.BlockSpec((s, 128), lambda p: (0, p))
    pair_t = pl.BlockSpec((128, s), lambda p: (p, 0))
    out = jax.ShapeDtypeStruct((s, 1024), F32)
    outs = pl.pallas_call(
        body, name="fox_bwd", grid=(N_PAIRS,),
        in_specs=[pair, pair, pair, pair, pair_t, pair_t, pair, pl.BlockSpec((N_HEADS, s), lambda p: (0, 0)),
                  pl.BlockSpec((s, 128), lambda p: (0, 8 + p))] + [ANY] * ns,
        out_specs=[pl.BlockSpec((s, 128), lambda p: (0, p), pipeline_mode=once)] * 5 + [ANY] * ns,
        out_shape=[out] * 5 + [jax.ShapeDtypeStruct(p.shape, p.dtype) for p in parts],
        scratch_shapes=[pltpu.VMEM((2, nq, t, t), F32), pltpu.VMEM((2, nq, t, t), F32)] + (_scatter_scratch(ns) if ns else []),
        compiler_params=_params(("arbitrary",)),
    )(qn, kn, aq, ak, knt, akt, vb, lse, dmixed, *parts)
    return (*outs[:5], _keep_own_blocks(outs[5:], parts))


def fox_post(dqn, dkn, dc0, dc1, proj, smallp, qw, kw, sel, selt, fold_even, fold_odd, *, tm=256):
    s = proj.shape[0]
    nrow = s // tm

    def body(dqn_ref, dkn_ref, dc0_ref, dc1_ref, q_ref, k_ref, small_ref, sp_ref, qw_ref, kw_ref, sel_ref, selt_ref,
             fe_ref, fo_ref, dq_ref, dk_ref, dsmall_ref, gqw_ref, gkw_ref, gfb_ref, carry):
        step = pl.program_id(0)

        @pl.when(step == 0)
        def _():
            carry[...] = jnp.zeros_like(carry)

        def norm_bwd(x_ref, w_ref, dn, out_ref):
            x = x_ref[...]
            rf = _head_rstd(x, sel_ref, selt_ref)
            xh = x * rf
            g = dn * w_ref[...]
            mean_gx = _head_spread(_head_mean(g * xh, sel_ref, selt_ref), selt_ref)
            out_ref[...] = (rf * (g - xh * mean_gx)).astype(BF16)
            return jnp.sum(dn * xh, axis=0, keepdims=True)

        gqw = norm_bwd(q_ref, qw_ref, dqn_ref[...] * FOX_SCALE, dq_ref)
        gkw = norm_bwd(k_ref, kw_ref, dkn_ref[...], dk_ref)
        li = _lane_iota((tm, 128))
        f_lane = jnp.logical_and(li >= F_LANE, li < F_LANE + N_HEADS)
        dcum = -(_split3_dot(dc0_ref[...], fe_ref[...]) + _split3_dot(dc1_ref[...], fo_ref[...]))
        dlogf = _rev_cumsum_rows(dcum) + carry[...]
        carry[...] = dlogf[0:1, :]
        dfr = jnp.where(f_lane, dlogf * _sigmoid(-(small_ref[...] + sp_ref[3:4, :])), 0.0)
        dsmall_ref[...] = dfr
        gfb = jnp.sum(dfr, axis=0, keepdims=True)

        @pl.when(step == 0)
        def _():
            gqw_ref[...] = gqw
            gkw_ref[...] = gkw
            gfb_ref[...] = gfb

        @pl.when(step > 0)
        def _():
            gqw_ref[...] += gqw
            gkw_ref[...] += gkw
            gfb_ref[...] += gfb

    def rb(i):
        return nrow - 1 - i

    row = pl.BlockSpec((tm, 1024), lambda i: (rb(i), 0))
    vec = pl.BlockSpec((1, 1024), lambda i: (0, 0))
    fold = pl.BlockSpec((1024, 128), lambda i: (0, 0))
    return pl.pallas_call(
        body, name="fox_post", grid=(nrow,),
        in_specs=[row, row, row, row, pl.BlockSpec((tm, 1024), lambda i: (rb(i), Q_COL)),
                  pl.BlockSpec((tm, 1024), lambda i: (rb(i), K_COL)),
                  pl.BlockSpec((tm, 128), lambda i: (rb(i), SMALL_BLOCK)), pl.BlockSpec((8, 128), lambda i: (0, 0)), vec, vec,
                  fold, pl.BlockSpec((128, 1024), lambda i: (0, 0)), fold, fold],
        out_specs=[row, row, pl.BlockSpec((tm, 128), lambda i: (rb(i), 0)), vec, vec, pl.BlockSpec((1, 128), lambda i: (0, 0))],
        out_shape=[jax.ShapeDtypeStruct((s, 1024), BF16), jax.ShapeDtypeStruct((s, 1024), BF16),
                   jax.ShapeDtypeStruct((s, 128), F32), jax.ShapeDtypeStruct((1, 1024), F32),
                   jax.ShapeDtypeStruct((1, 1024), F32), jax.ShapeDtypeStruct((1, 128), F32)],
        scratch_shapes=[pltpu.VMEM((1, 128), F32)], compiler_params=_params(("arbitrary",)),
    )(dqn, dkn, dc0, dc1, proj, proj, proj, smallp, qw, kw, sel, selt, fold_even, fold_odd)


def local_step(x, target, wx, later_shards, ssd_cw8, ssd_cb, smallp, ssd_nw, qw_t, kw_t, sel, selt,
               norm_mix_w, norm_ffn_w, ffn_cw8, ffn_cb):
    h, h_t = rms_fwd(x, norm_mix_w, name="rms_mix_fwd")
    proj = matmul(h, wx, mode="nn", tm=1024, tn=PROJ_TILE, tk=1024, out_dtype=F32, name="mm_in_proj")
    y_ssd, y_ssd_t, ypre, states = ssd_fwd(proj, ssd_cw8, ssd_cb, smallp, ssd_nw)
    place_q, place_k, ones_q, ones_k, fold_even, fold_odd = fox_tables()
    qn, kn, aq, ak, vb, knt, akt, vt = fox_prep(proj, smallp, qw_t, kw_t, sel, selt, place_q, place_k, ones_q, ones_k)
    y_fox, y_fox_t, lse, (a_out, a_up, a_down) = fox_fwd(qn, kn, aq, ak, vt, shards=later_shards)
    w_out = a_out.reshape(2048, D_MODEL)
    w_down = a_down.reshape(D_FF, D_MODEL)
    s = x.shape[0]
    shard = lambda index: pl.BlockSpec((None, 1024, 1408), index)
    x1, hf, hf_t = out_proj_rms_fwd(y_ssd, y_fox, w_out, x, norm_ffn_w)
    hu, act, act_t = up_ffn_fwd(hf, a_up, ffn_cw8, ffn_cb)
    dy, sq = down_proj_loss(act, w_down, x1, target)

    dact = matmul(dy, w_down, mode="nt", tm=1024, tn=1408, tk=1024, out_dtype=F32, name="mm_dact")
    g_down = matmul(act_t, dy, mode="nn", tm=1408, tn=1024, tk=1024, out_dtype=BF16, name="mm_dw_down")
    dhu, gcw_g, gcw_v = ffn_mid_bwd(hu, dact, ffn_cw8, ffn_cb)
    dhf = matmul(dhu, a_up, mode="nt", tm=1024, tn=1024, tk=1408, out_dtype=F32, name="mm_dhf",
                 layout=dict(m=s, n=D_MODEL, k=2 * D_FF, a_spec=shard(lambda i, j, kk: (kk // 2, i, kk % 2)),
                             b_spec=shard(lambda i, j, kk: (kk, 0, 0))))
    g_up = matmul(hf_t, dhu, mode="nn", tm=1024, tn=1408, tk=1024, out_dtype=BF16, name="mm_dw_up",
                  layout=dict(m=D_MODEL, n=2 * D_FF, k=s, b_spec=shard(lambda i, j, kk: (j // 2, kk, j % 2)),
                              o_spec=shard(lambda i, j, kk: (j, i, 0)), out_shape=(4, D_MODEL, 1408)))
    dx1, g_norm_ffn, dmixed = rms_bwd_matmul(dhf, x1, norm_ffn_w, dy, w_out, name="rms_ffn_bwd_dmixed")
    g_out_a = matmul(y_ssd_t, dx1, mode="nn", tm=1024, tn=1024, tk=1024, out_dtype=BF16, name="mm_dw_out_ssd")
    g_out_b = matmul(y_fox_t, dx1, mode="nn", tm=1024, tn=1024, tk=1024, out_dtype=BF16, name="mm_dw_out_fox")
    early = [jnp.concatenate([g_out_a, g_out_b], axis=0).reshape(4, 512, D_MODEL), g_up, g_down.reshape(4, 704, D_MODEL)]
    dz, dxs, db, dc, dsmall_ssd, gcw_x, gcw_b, gcw_c, g_sp, g_ssd_nw, theirs = ssd_bwd(
        proj, ssd_cw8, ssd_cb, smallp, ssd_nw, ypre, states, dmixed, sel, swap=early)
    core = lax.axis_index("c").astype(jnp.int32).reshape(1)
    parts = [add_pair(a, b, core, name="add_pair_" + n, tr=ADAM_ROWS[n]) for a, b, n in zip(early, theirs, BIG_NAMES[1:])]
    dqn, dkn, dv, dc0, dc1, landed_early = fox_bwd(qn, kn, aq, ak, knt, akt, vb, lse, dmixed, parts=parts)
    dq, dk, dsmall_fox, g_qw, g_kw, g_fb = fox_post(dqn, dkn, dc0, dc1, proj, smallp, qw_t, kw_t, sel, selt,
                                                    fold_even, fold_odd)
    dproj = jnp.concatenate([dz, dxs, dq, dk, dv.astype(BF16), db, dc, (dsmall_ssd + dsmall_fox).astype(BF16)], axis=1)
    g_wx = matmul(h_t, dproj, mode="nn", tm=1024, tn=PROJ_TILE, tk=1024, out_dtype=BF16, name="mm_dw_in")
    g_in = _in_grad_shards(g_wx)
    part_in = add_pair(g_in, pair_swap_halves([g_in], name="pair_swap_w_in")[0], core, name="add_pair_w_in",
                       tr=ADAM_ROWS["w_in"])
    dh, landed_in = matmul(dproj, wx, mode="nt", tm=1024, tn=1024, tk=PROJ_TILE, out_dtype=F32, name="mm_dh",
                           scatter=[part_in])
    grad_x, g_norm_mix = rms_bwd(dh, x, norm_mix_w, dx1, name="rms_mix_bwd")
    return dict(
        sq=sq, grad_x=grad_x, landed=landed_in + landed_early,
        g_norm_mix=g_norm_mix, g_norm_ffn=g_norm_ffn, g_ssd_nw=g_ssd_nw,
        g_ssd_cw=jnp.concatenate([gcw_x, gcw_b, gcw_c], axis=1), g_sp=g_sp, g_fb=g_fb, g_qw=g_qw, g_kw=g_kw,
        g_ffn_cw=jnp.concatenate([gcw_g, gcw_v], axis=1))


def adamw(w, g, m, v, *, name, tr, allreduce=None):
    rows, cols = w.shape
    nsteps = rows // tr

    def body(*refs):
        if allreduce is None:
            w_ref, g_ref, m_ref, v_ref, d_ref, mo_ref, vo_ref = refs
        else:
            w_ref, g_ref, m_ref, v_ref, packed_ref, d_ref, mo_ref, vo_ref, summed_ref = refs[:9]
            start, finish = _allreduce_phases(packed_ref, summed_ref, *refs[9:])
            pl.when(pl.program_id(0) == 0)(start)
        gv = g_ref[...]
        mn = ADAM_B1 * m_ref[...] + (1.0 - ADAM_B1) * gv
        vn = ADAM_B2 * v_ref[...] + (1.0 - ADAM_B2) * (gv * gv)
        m_hat = mn / (1.0 - ADAM_B1 ** ADAM_STEP)
        v_hat = vn / (1.0 - ADAM_B2 ** ADAM_STEP)
        d_ref[...] = -ADAM_LR * (m_hat / (jnp.sqrt(v_hat) + ADAM_EPS) + ADAM_WD * w_ref[...])
        mo_ref[...] = mn
        vo_ref[...] = vn
        if allreduce is not None:
            pl.when(pl.program_id(0) == nsteps - 1)(finish)

    blk = pl.BlockSpec((tr, cols), lambda i: (i, 0))
    shp = jax.ShapeDtypeStruct((rows, cols), F32)
    if allreduce is None:
        return pl.pallas_call(
            body, name=name, grid=(nsteps,), in_specs=[blk] * 4, out_specs=[blk] * 3, out_shape=[shp] * 3,
            compiler_params=_params(("parallel",)),
        )(w, g, m, v)
    whole = pl.BlockSpec(memory_space=pltpu.VMEM)
    return pl.pallas_call(
        body, name=name, grid=(nsteps,), in_specs=[blk] * 4 + [whole], out_specs=[blk] * 3 + [whole],
        out_shape=[shp] * 3 + [jax.ShapeDtypeStruct(allreduce.shape, F32)],
        scratch_shapes=_allreduce_scratch(allreduce.shape[0]), compiler_params=_params(("arbitrary",)),
    )(w, g, m, v, allreduce)


def add_pair(full, theirs, core, *, name, tr):
    _, rows, cols = theirs.shape
    nblk = rows // tr

    def body(c_ref, a_ref, b_ref, o_ref):
        o_ref[...] = (a_ref[...].astype(F32) + b_ref[...].astype(F32)).astype(BF16)

    blk = pl.BlockSpec((1, tr, cols), lambda j, i, c: (j, i, 0))
    grid_spec = pltpu.PrefetchScalarGridSpec(
        num_scalar_prefetch=1, grid=(4, nblk),
        in_specs=[pl.BlockSpec((1, tr, cols), lambda j, i, c: (j, c[0] * nblk + i, 0)), blk], out_specs=blk)
    return pl.pallas_call(
        body, name=name, grid_spec=grid_spec, out_shape=jax.ShapeDtypeStruct(theirs.shape, BF16),
        compiler_params=_params(("parallel", "parallel")),
    )(core, full, theirs)


def sum_chips(parts, core, *, name, tr):
    _, rows, cols = parts.shape
    nblk = rows // tr

    def body(c_ref, p_ref, o_ref):
        acc = p_ref[0].astype(F32)
        for k in range(1, 4):
            acc = acc + p_ref[k].astype(F32)
        o_ref[...] = acc

    grid_spec = pltpu.PrefetchScalarGridSpec(
        num_scalar_prefetch=1, grid=(nblk,), in_specs=[pl.BlockSpec((4, tr, cols), lambda i, c: (0, i, 0))],
        out_specs=pl.BlockSpec((tr, cols), lambda i, c: (c[0] * nblk + i, 0)))
    return pl.pallas_call(
        body, name=name, grid_spec=grid_spec, out_shape=jax.ShapeDtypeStruct((2 * rows, cols), F32),
        compiler_params=_params(("parallel",)),
    )(core, parts)


ANY = pl.BlockSpec(memory_space=pl.ANY)


def _place():
    x, y, c = lax.axis_index("x"), lax.axis_index("y"), lax.axis_index("c")
    chips = [(1 - x, y), (x, 1 - y), (1 - x, 1 - y)]
    return x, y, c, chips


def _chunks(rows):
    size = next((c for c in (128, 176, 64, 32, 16, 8) if rows % c == 0), rows)
    return [(r, size) for r in range(0, rows, size)]


def gather_weights(shards):
    n = len(shards)

    def body(*refs):
        start, forward, finish = _gather_phases(refs[:n], refs[n:2 * n], *refs[2 * n:])
        start()
        forward()
        finish()

    gathered = pl.pallas_call(
        body, name="gather_weights", in_specs=[ANY] * n, out_specs=[ANY] * n,
        out_shape=_gather_out_shapes(shards), scratch_shapes=_gather_scratch(n),
    )(*shards)
    return gathered


def _gather_out_shapes(shards):
    return [jax.ShapeDtypeStruct((4,) + s.shape, s.dtype) for s in shards]


def _gather_scratch(n):
    return [pltpu.SemaphoreType.DMA((n, 7)), pltpu.SemaphoreType.DMA((n, 7))]


def _gather_phases(ins, outs, send_sems, recv_sems):
    n = len(ins)
    x, y, c, chips = _place()
    me = 2 * x + y
    sibling = (x, y, 1 - c)
    blks = [2 * cx + cy for cx, cy in chips]

    def half(a, blk, r=0, nr=None):
        rows = ins[a].shape[0] // 2
        return outs[a].at[blk, pl.ds(c * rows + r, rows if nr is None else nr), :]

    def to_chip(a, t, r=0, nr=None):
        rows = ins[a].shape[0] // 2
        return pltpu.make_async_remote_copy(
            src_ref=ins[a].at[pl.ds(c * rows + r, rows if nr is None else nr), :], dst_ref=half(a, me, r, nr),
            send_sem=send_sems.at[a, t], recv_sem=recv_sems.at[a, t], device_id=(*chips[t], c), device_id_type=MESH)

    def from_chip(a, t):
        return pltpu.make_async_remote_copy(
            src_ref=half(a, blks[t]), dst_ref=half(a, blks[t]), send_sem=send_sems.at[a, t], recv_sem=recv_sems.at[a, t],
            device_id=(*chips[t], c), device_id_type=MESH)

    def to_sibling(a, t, r=0, nr=None):
        return pltpu.make_async_remote_copy(
            src_ref=half(a, blks[t], r, nr), dst_ref=half(a, blks[t], r, nr), send_sem=send_sems.at[a, 3 + t],
            recv_sem=recv_sems.at[a, 3 + t], device_id=sibling, device_id_type=MESH)

    def from_sibling(a, t):
        rows = ins[a].shape[0] // 2
        dst = outs[a].at[blks[t], pl.ds((1 - c) * rows, rows), :]
        return pltpu.make_async_remote_copy(
            src_ref=dst, dst_ref=dst, send_sem=send_sems.at[a, 3 + t], recv_sem=recv_sems.at[a, 3 + t],
            device_id=sibling, device_id_type=MESH)

    def own(a, r=0, nr=None):
        return pltpu.make_async_remote_copy(
            src_ref=ins[a].at[pl.ds(r, ins[a].shape[0] if nr is None else nr), :],
            dst_ref=outs[a].at[me, pl.ds(r, ins[a].shape[0] if nr is None else nr), :],
            send_sem=send_sems.at[a, 6], recv_sem=recv_sems.at[a, 6], device_id=sibling, device_id_type=MESH)

    def start():
        for a in range(n):
            for t in range(3):
                for r, nr in _chunks(ins[a].shape[0] // 2):
                    to_chip(a, t, r, nr).start()
            for r, nr in _chunks(ins[a].shape[0]):
                own(a, r, nr).start()

    def forward():
        for a in range(n):
            for t in range(3):
                from_chip(a, t).wait_recv()
                for r, nr in _chunks(ins[a].shape[0] // 2):
                    to_sibling(a, t, r, nr).start()

    def finish():
        for a in range(n):
            for t in range(3):
                from_sibling(a, t).wait_recv()
        for a in range(n):
            for t in range(3):
                to_chip(a, t).wait_send()
                to_sibling(a, t).wait_send()
            own(a).wait()

    return start, forward, finish


def pair_swap_halves(grads, *, name):
    n = len(grads)

    def body(*refs):
        start, finish = _pair_swap_phases(refs[:n], refs[n:2 * n], *refs[2 * n:])
        start()
        finish()

    return pl.pallas_call(
        body, name=name, in_specs=[ANY] * n, out_specs=[ANY] * n, out_shape=_pair_swap_out_shapes(grads),
        scratch_shapes=_pair_swap_scratch(n),
    )(*grads)


def _pair_swap_out_shapes(grads):
    return [jax.ShapeDtypeStruct((4, g.shape[1] // 2, g.shape[2]), g.dtype) for g in grads]


def _pair_swap_scratch(n):
    return [pltpu.SemaphoreType.DMA((n,)), pltpu.SemaphoreType.DMA((n,))]


def _pair_swap_phases(ins, theirs, send_sems, recv_sems):
    n = len(ins)
    x, y, c, _ = _place()
    sibling = (x, y, 1 - c)

    def start():
        for a in range(n):
            rows = ins[a].shape[1] // 2
            for j in range(4):
                for r, nr in _chunks(rows):
                    pltpu.make_async_remote_copy(
                        src_ref=ins[a].at[j, pl.ds((1 - c) * rows + r, nr), :], dst_ref=theirs[a].at[j, pl.ds(r, nr), :],
                        send_sem=send_sems.at[a], recv_sem=recv_sems.at[a], device_id=sibling, device_id_type=MESH).start()

    def finish():
        for a in range(n):
            pltpu.make_async_remote_copy(src_ref=theirs[a], dst_ref=theirs[a], send_sem=send_sems.at[a],
                                         recv_sem=recv_sems.at[a], device_id=sibling, device_id_type=MESH).wait()

    return start, finish


def _scatter_scratch(n):
    return [pltpu.SemaphoreType.DMA((n, 3)), pltpu.SemaphoreType.DMA((n, 3))]


def _keep_own_blocks(landed, parts):
    if not parts:
        return []
    chip = 2 * lax.axis_index("x") + lax.axis_index("y")
    return [lax.dynamic_update_slice(l, lax.dynamic_slice_in_dim(p, chip, 1, axis=0), (chip, 0, 0))
            for l, p in zip(landed, parts)]


def _scatter_phases(ins, outs, send_sems, recv_sems):
    n = len(ins)
    x, y, c, chips = _place()
    me = 2 * x + y
    blks = [2 * cx + cy for cx, cy in chips]

    def start():
        for a in range(n):
            for r, nr in _chunks(ins[a].shape[1]):
                for t in range(3):
                    pltpu.make_async_remote_copy(
                        src_ref=ins[a].at[blks[t], pl.ds(r, nr), :], dst_ref=outs[a].at[me, pl.ds(r, nr), :],
                        send_sem=send_sems.at[a, t], recv_sem=recv_sems.at[a, t],
                        device_id=(*chips[t], c), device_id_type=MESH).start()

    def finish():
        for a in range(n):
            for t in range(3):
                pltpu.make_async_remote_copy(
                    src_ref=outs[a].at[blks[t]], dst_ref=outs[a].at[blks[t]], send_sem=send_sems.at[a, t],
                    recv_sem=recv_sems.at[a, t], device_id=(*chips[t], c), device_id_type=MESH).wait()

    return start, finish


def pair_join_halves(bufs):
    n = len(bufs)

    def body(*refs):
        outs = refs[n:2 * n]
        send_sems, recv_sems = refs[2 * n:]
        x, y, c, _ = _place()
        sibling = (x, y, 1 - c)
        for a in range(n):
            rows = outs[a].shape[0] // 2
            for r, nr in _chunks(rows):
                mine = outs[a].at[pl.ds(c * rows + r, nr), :]
                pltpu.make_async_remote_copy(src_ref=mine, dst_ref=mine, send_sem=send_sems.at[a], recv_sem=recv_sems.at[a],
                                             device_id=sibling, device_id_type=MESH).start()
        for a in range(n):
            rows = outs[a].shape[0] // 2
            pltpu.make_async_remote_copy(
                src_ref=outs[a].at[pl.ds(c * rows, rows), :], dst_ref=outs[a].at[pl.ds((1 - c) * rows, rows), :],
                send_sem=send_sems.at[a], recv_sem=recv_sems.at[a], device_id=sibling, device_id_type=MESH).wait()

    return pl.pallas_call(
        body, name="pair_join_halves", in_specs=[ANY] * n, out_specs=[ANY] * n,
        out_shape=[jax.ShapeDtypeStruct(b.shape, b.dtype) for b in bufs], input_output_aliases={a: a for a in range(n)},
        scratch_shapes=[pltpu.SemaphoreType.DMA((n,)), pltpu.SemaphoreType.DMA((n,))],
    )(*bufs)


def _allreduce_scratch(rows):
    return [pltpu.VMEM((8, rows, 128), F32), pltpu.SemaphoreType.DMA((7,)), pltpu.SemaphoreType.DMA((7,))]


def _allreduce_phases(in_ref, out_ref, gathered, send_sems, recv_sems):
    x, y, c, _ = _place()
    me = 4 * x + 2 * y + c
    flips = [(fx, fy, fc) for fx in (0, 1) for fy in (0, 1) for fc in (0, 1)][1:]
    peers = [((1 - x) if fx else x, (1 - y) if fy else y, (1 - c) if fc else c) for fx, fy, fc in flips]

    def send(t):
        return pltpu.make_async_remote_copy(
            src_ref=in_ref, dst_ref=gathered.at[me], send_sem=send_sems.at[t], recv_sem=recv_sems.at[t],
            device_id=peers[t], device_id_type=MESH)

    def start():
        gathered[me] = in_ref[...]
        for t in range(7):
            send(t).start()

    def finish():
        for t, (px, py, pc) in enumerate(peers):
            slot = gathered.at[4 * px + 2 * py + pc]
            pltpu.make_async_remote_copy(
                src_ref=slot, dst_ref=slot, send_sem=send_sems.at[t], recv_sem=recv_sems.at[t],
                device_id=(px, py, pc), device_id_type=MESH).wait_recv()
        for t in range(7):
            send(t).wait_send()
        acc = gathered[0]
        for k in range(1, 8):
            acc = acc + gathered[k]
        out_ref[...] = acc

    return start, finish


SMALL_NAMES = ("norm_mix_w", "ssd_conv_w", "ssd_conv_b", "ssd_dt_bias", "ssd_a_log", "ssd_d", "ssd_norm_w", "fox_f_bias",
               "fox_q_norm_w", "fox_k_norm_w", "norm_ffn_w", "ffn_conv_w", "ffn_conv_b")
BIG_NAMES = ("w_in", "w_out", "w_up", "w_down")
WEIGHT_ORDER = ("norm_mix_w", "w_in", "ssd_conv_w", "ssd_conv_b", "ssd_dt_bias", "ssd_a_log", "ssd_d", "ssd_norm_w",
                "fox_f_bias", "fox_q_norm_w", "fox_k_norm_w", "w_out", "norm_ffn_w", "w_up", "ffn_conv_w", "ffn_conv_b", "w_down")
ADAM_ROWS = {"w_in": 256, "w_out": 256, "w_up": 256, "w_down": 176}


def _pack(arrays):
    pieces = []
    for a in arrays:
        flat = a.reshape(-1).astype(F32)
        pieces += [flat, jnp.zeros(((-flat.shape[0]) % 1024,), F32)]
    return jnp.concatenate(pieces).reshape(-1, 128)


def _unpack(packed, shapes):
    out, r = [], 0
    for shp in shapes:
        size = 1
        for d in shp:
            size *= d
        nrow = 8 * (-(-size // 1024))
        out.append(packed[r:r + nrow].reshape(-1)[:size].reshape(shp))
        r += nrow
    return out


IN_SHARD = IN_COLS // 4
IN_SEGMENTS = ((0, 2048, 0), (2048, 2560, 5120), (2560, 2576, MAIN_COLS), (2576, 5648, 2048), (5648, 5664, MAIN_COLS + F_LANE))


def _in_cols(shards, lo, hi):
    out = []
    for j in range(4):
        a, b = max(lo, IN_SHARD * j), min(hi, IN_SHARD * (j + 1))
        if a < b:
            out.append(shards[j][:, a - IN_SHARD * j:b - IN_SHARD * j])
    return out


def _in_grad_shards(g):
    shards = []
    for j in range(4):
        pieces = []
        for lo, hi, at in IN_SEGMENTS:
            a, b = max(lo, IN_SHARD * j), min(hi, IN_SHARD * (j + 1))
            if a < b:
                pieces.append(g[:, at + a - lo:at + b - lo])
        shards.append(jnp.concatenate(pieces, axis=1))
    return jnp.stack(shards)


def _pad_rows(a, rows):
    return jnp.pad(a, ((0, rows - a.shape[0]), (0, 0)))


def kernel(x, norm_mix_w, w_in, ssd_conv_w, ssd_conv_b, ssd_dt_bias, ssd_a_log, ssd_d, ssd_norm_w, fox_f_bias, fox_q_norm_w, fox_k_norm_w, w_out, norm_ffn_w, w_up, ffn_conv_w, ffn_conv_b, w_down, loss_target, m_norm_mix_w, m_w_in, m_ssd_conv_w, m_ssd_conv_b, m_ssd_dt_bias, m_ssd_a_log, m_ssd_d, m_ssd_norm_w, m_fox_f_bias, m_fox_q_norm_w, m_fox_k_norm_w, m_w_out, m_norm_ffn_w, m_w_up, m_ffn_conv_w, m_ffn_conv_b, m_w_down, v_norm_mix_w, v_w_in, v_ssd_conv_w, v_ssd_conv_b, v_ssd_dt_bias, v_ssd_a_log, v_ssd_d, v_ssd_norm_w, v_fox_f_bias, v_fox_q_norm_w, v_fox_k_norm_w, v_w_out, v_norm_ffn_w, v_w_up, v_ffn_conv_w, v_ffn_conv_b, v_w_down):
    w = dict(norm_mix_w=norm_mix_w, w_in=w_in, ssd_conv_w=ssd_conv_w, ssd_conv_b=ssd_conv_b, ssd_dt_bias=ssd_dt_bias,
             ssd_a_log=ssd_a_log, ssd_d=ssd_d, ssd_norm_w=ssd_norm_w, fox_f_bias=fox_f_bias, fox_q_norm_w=fox_q_norm_w,
             fox_k_norm_w=fox_k_norm_w, w_out=w_out, norm_ffn_w=norm_ffn_w, w_up=w_up, ffn_conv_w=ffn_conv_w,
             ffn_conv_b=ffn_conv_b, w_down=w_down)
    m = dict(norm_mix_w=m_norm_mix_w, w_in=m_w_in, ssd_conv_w=m_ssd_conv_w, ssd_conv_b=m_ssd_conv_b, ssd_dt_bias=m_ssd_dt_bias,
             ssd_a_log=m_ssd_a_log, ssd_d=m_ssd_d, ssd_norm_w=m_ssd_norm_w, fox_f_bias=m_fox_f_bias, fox_q_norm_w=m_fox_q_norm_w,
             fox_k_norm_w=m_fox_k_norm_w, w_out=m_w_out, norm_ffn_w=m_norm_ffn_w, w_up=m_w_up, ffn_conv_w=m_ffn_conv_w,
             ffn_conv_b=m_ffn_conv_b, w_down=m_w_down)
    v = dict(norm_mix_w=v_norm_mix_w, w_in=v_w_in, ssd_conv_w=v_ssd_conv_w, ssd_conv_b=v_ssd_conv_b, ssd_dt_bias=v_ssd_dt_bias,
             ssd_a_log=v_ssd_a_log, ssd_d=v_ssd_d, ssd_norm_w=v_ssd_norm_w, fox_f_bias=v_fox_f_bias, fox_q_norm_w=v_fox_q_norm_w,
             fox_k_norm_w=v_fox_k_norm_w, w_out=v_w_out, norm_ffn_w=v_norm_ffn_w, w_up=v_w_up, ffn_conv_w=v_ffn_conv_w,
             ffn_conv_b=v_ffn_conv_b, w_down=v_w_down)
    chip = 2 * lax.axis_index("x") + lax.axis_index("y")

    a_in, a_scw, a_fcw = gather_weights([w_in[0].astype(BF16), _pad_rows(ssd_conv_w[0], 16), _pad_rows(ffn_conv_w[0], 16)])
    later_shards = [w_out[0].astype(BF16), w_up[0].astype(BF16), w_down[0].astype(BF16)]
    wx = jnp.concatenate([p for lo, hi, _ in sorted(IN_SEGMENTS, key=lambda seg: seg[2]) for p in _in_cols(a_in, lo, hi)]
                         + [jnp.zeros((D_MODEL, PROJ_COLS - IN_COLS), BF16)], axis=1)
    ssd_cw8 = a_scw.transpose(1, 0, 2).reshape(16, 1536)[:8]
    ffn_cw8 = a_fcw.transpose(1, 0, 2).reshape(16, 2 * D_FF)[:8]
    gap = lambda n: jnp.zeros((n,), F32)
    smallp = jnp.concatenate([ssd_dt_bias[0], gap(112), ssd_a_log[0], gap(112), ssd_d[0], gap(112),
                              gap(F_LANE), fox_f_bias[0], gap(128 - F_LANE - N_HEADS), gap(4 * 128)]).reshape(8, 128)
    qw_t = jnp.tile(fox_q_norm_w[0], N_HEADS)[None]
    kw_t = jnp.tile(fox_k_norm_w[0], N_HEADS)[None]
    sel = jnp.asarray((np.arange(1024)[:, None] // HEAD_DIM == np.arange(128)[None, :]).astype(np.float32), BF16)

    res = local_step(x[0], loss_target[0], wx, later_shards, ssd_cw8, ssd_conv_b, smallp, ssd_norm_w, qw_t, kw_t,
                     sel, sel.T, norm_mix_w, norm_ffn_w, ffn_cw8, ffn_conv_b)

    full_shapes = [(1, 1024), (1, 4, 1536), (1, 1536), (1, 16), (1, 16), (1, 16), (1, 1024), (1, 16), (1, 64), (1, 64),
                   (1, 1024), (1, 3, 2 * D_FF), (1, 2 * D_FF), (1,)]
    local_small = [res["g_norm_mix"], res["g_ssd_cw"][:4], res["g_ssd_cw"][4], res["g_sp"][0, :16], res["g_sp"][1, :16],
                   res["g_sp"][2, :16], res["g_ssd_nw"], res["g_fb"][0, F_LANE:F_LANE + 16],
                   res["g_qw"].reshape(N_HEADS, HEAD_DIM).sum(0), res["g_kw"].reshape(N_HEADS, HEAD_DIM).sum(0),
                   res["g_norm_ffn"], res["g_ffn_cw"][:3], res["g_ffn_cw"][3], jnp.sum(res["sq"])]
    landed = res["landed"]
    core = lax.axis_index("c").astype(jnp.int32).reshape(1)
    halves = [sum_chips(p, core, name="sum_chips_" + n, tr=ADAM_ROWS[n]) for p, n in zip(landed, BIG_NAMES)]
    g_big = dict(zip(BIG_NAMES, pair_join_halves(halves)))

    grads, deltas, new_m, new_v = {}, {}, {}, {}
    for n in BIG_NAMES:
        out = adamw(w[n][0], g_big[n], m[n][0], v[n][0], name="adamw_" + n, tr=ADAM_ROWS[n],
                    allreduce=_pack(local_small) if n == BIG_NAMES[0] else None)
        if n == BIG_NAMES[0]:
            summed = _unpack(out[3], full_shapes)
        d, mn, vn = out[:3]
        grads[n], deltas[n], new_m[n], new_v[n] = g_big[n][None], d[None], mn[None], vn[None]
    loss = (0.5 / D_MODEL) * summed[-1][0]
    g_small = dict(zip(SMALL_NAMES, summed[:-1]))
    g_small["ssd_conv_w"] = lax.dynamic_slice(g_small["ssd_conv_w"], (0, 0, 384 * chip), (1, 4, 384))
    g_small["ffn_conv_w"] = lax.dynamic_slice(g_small["ffn_conv_w"], (0, 0, 1408 * chip), (1, 3, 1408))
    shapes = [w[n].shape for n in SMALL_NAMES]
    packed_w = _pack([w[n] for n in SMALL_NAMES])
    d, mn, vn = adamw(packed_w, _pack([g_small[n] for n in SMALL_NAMES]), _pack([m[n] for n in SMALL_NAMES]),
                      _pack([v[n] for n in SMALL_NAMES]), name="adamw_small", tr=packed_w.shape[0])
    for n, dd, mm, vv in zip(SMALL_NAMES, _unpack(d, shapes), _unpack(mn, shapes), _unpack(vn, shapes)):
        grads[n], deltas[n], new_m[n], new_v[n] = g_small[n].reshape(w[n].shape), dd, mm, vv
    return (loss, res["grad_x"][None], *[grads[n] for n in WEIGHT_ORDER], *[deltas[n] for n in WEIGHT_ORDER],
            *[new_m[n] for n in WEIGHT_ORDER], *[new_v[n] for n in WEIGHT_ORDER])
```

```python
import functools

import jax
import jax.numpy as jnp
import numpy as np
from jax import lax
from jax.experimental import pallas as pl
from jax.experimental.pallas import tpu as pltpu

F32 = jnp.float32
BF16 = jnp.bfloat16
MESH = pl.DeviceIdType.MESH

D_MODEL = 1024
HEAD_DIM = 64
N_HEADS = 16
N_PAIRS = N_HEADS // 2
SSD_CHUNK = 128
SSD_STATE = 128
SSD_CONV = 4
D_FF = 2816
FFN_CONV = 3
NORM_EPS = 1e-6
MAIN_COLS = 5632
SMALL_COLS = 128
PROJ_COLS = MAIN_COLS + SMALL_COLS
SMALL_BLOCK = MAIN_COLS // SMALL_COLS
PROJ_TILE = 1152
F_LANE = 16
IN_COLS = 5664

ADAM_LR = 0.001
ADAM_B1 = 0.9
ADAM_B2 = 0.999
ADAM_EPS = 1e-08
ADAM_WD = 0.01
ADAM_STEP = 10

VMEM_LIMIT_V7X = 56 * 1024 * 1024
NEG_BIG = -1e30


def _params(sem=None):
    return pltpu.CompilerParams(dimension_semantics=sem, vmem_limit_bytes=VMEM_LIMIT_V7X)


def _sigmoid(x):
    return 1.0 / (1.0 + jnp.exp(-x))


def _silu_and_grad(x):
    s = _sigmoid(x)
    return x * s, s * (1.0 + x * (1.0 - s))


def _shift_down(v, j):
    return v if j == 0 else pltpu.roll(v, j, 0)


def _shift_up(v, j):
    return v if j == 0 else pltpu.roll(v, v.shape[0] - j, 0)


def _row_iota(shape):
    return lax.broadcasted_iota(jnp.int32, shape, 0)


def _lane_iota(shape):
    return lax.broadcasted_iota(jnp.int32, shape, 1)


def _dot(a, b, mode="nn"):
    dims = {"nn": (((1,), (0,)), ((), ())), "nt": (((1,), (1,)), ((), ())), "tn": (((0,), (0,)), ((), ()))}[mode]
    return lax.dot_general(a.astype(BF16), b.astype(BF16), dims, preferred_element_type=F32)


def _dot_f32(a, b):
    return jnp.dot(a, b, precision=lax.Precision.HIGHEST, preferred_element_type=F32)


def matmul(a, b, *, mode, tm, tn, tk, out_dtype, name, layout=None):
    layout = layout or {}
    if layout:
        m, n, k = layout["m"], layout["n"], layout["k"]
    else:
        (m, k), n = a.shape, (b.shape[1] if mode == "nn" else b.shape[0])
    assert m % tm == 0 and n % tn == 0 and k % tk == 0, (name, m, n, k, tm, tn, tk)
    nk = k // tk
    a_spec = layout.get("a_spec") or pl.BlockSpec((tm, tk), lambda i, j, kk: (i, kk))
    b_spec = layout.get("b_spec") or (pl.BlockSpec((tn, tk), lambda i, j, kk: (j, kk)) if mode == "nt"
                                      else pl.BlockSpec((tk, tn), lambda i, j, kk: (kk, j)))
    o_spec = layout.get("o_spec") or pl.BlockSpec((tm, tn), lambda i, j, kk: (i, j))

    def body(a_ref, b_ref, o_ref, acc_ref):
        kk = pl.program_id(2)
        part = _dot(a_ref[...], b_ref[...], mode)
        if nk == 1:
            o_ref[...] = part.astype(out_dtype)
        else:
            @pl.when(kk == 0)
            def _():
                acc_ref[...] = part

            @pl.when(jnp.logical_and(kk > 0, kk < nk - 1))
            def _():
                acc_ref[...] += part

            @pl.when(kk == nk - 1)
            def _():
                o_ref[...] = (acc_ref[...] + part).astype(out_dtype)

    return pl.pallas_call(
        body, name=name, grid=(m // tm, n // tn, nk), in_specs=[a_spec, b_spec], out_specs=o_spec,
        out_shape=jax.ShapeDtypeStruct(layout.get("out_shape", (m, n)), out_dtype),
        scratch_shapes=[pltpu.VMEM((tm, tn) if nk > 1 else (8, 128), F32)],
        compiler_params=_params(("parallel", "parallel", "arbitrary")),
    )(a, b)


def rms_fwd(x, w, *, name, tm=1024):
    s, d = x.shape

    def body(x_ref, w_ref, h_ref, ht_ref):
        xv = x_ref[...]
        r = lax.rsqrt(jnp.mean(xv * xv, axis=-1, keepdims=True) + NORM_EPS)
        h = (xv * r) * w_ref[...]
        h_ref[...] = h.astype(BF16)
        ht_ref[...] = h.T.astype(BF16)

    return pl.pallas_call(
        body, name=name, grid=(s // tm,),
        in_specs=[pl.BlockSpec((tm, d), lambda i: (i, 0)), pl.BlockSpec((1, d), lambda i: (0, 0))],
        out_specs=[pl.BlockSpec((tm, d), lambda i: (i, 0)), pl.BlockSpec((d, tm), lambda i: (0, i))],
        out_shape=[jax.ShapeDtypeStruct((s, d), BF16), jax.ShapeDtypeStruct((d, s), BF16)],
        compiler_params=_params(("parallel",)),
    )(x, w)


def matmul_rms_bwd(dproj, wx, x, w, resid, *, scatter, tm=512):
    s, d = x.shape
    ns = len(scatter)
    nsteps = s // tm

    def body(*refs):
        a_ref, b_ref, x_ref, w_ref, res_ref = refs[:5]
        dx_ref, dw_ref = refs[5 + ns:7 + ns]
        step = pl.program_id(0)
        start, finish = _scatter_phases(refs[5:5 + ns], refs[7 + ns:7 + 2 * ns], *refs[7 + 2 * ns:])
        pl.when(step == 0)(start)
        part = jnp.zeros((1, d), F32)
        for r in range(0, tm, UP_ROWS):
            rows = slice(r, r + UP_ROWS)
            dhv = _dot(a_ref[rows, :], b_ref[...], "nt")
            xv = x_ref[rows, :]
            rstd = lax.rsqrt(jnp.mean(xv * xv, axis=-1, keepdims=True) + NORM_EPS)
            xh = xv * rstd
            g = dhv * w_ref[...]
            dx_ref[rows, :] = res_ref[rows, :] + rstd * (g - xh * jnp.mean(g * xh, axis=-1, keepdims=True))
            part = part + jnp.sum(dhv * xh, axis=0, keepdims=True)

        @pl.when(step == 0)
        def _():
            dw_ref[...] = part

        @pl.when(step > 0)
        def _():
            dw_ref[...] += part

        pl.when(step == nsteps - 1)(finish)

    row = pl.BlockSpec((tm, d), lambda i: (i, 0))
    vec = pl.BlockSpec((1, d), lambda i: (0, 0))
    outs = pl.pallas_call(
        body, name="mm_dh_rms_mix_bwd", grid=(nsteps,),
        in_specs=[pl.BlockSpec((tm, PROJ_COLS), lambda i: (i, 0)),
                  pl.BlockSpec((d, PROJ_COLS), lambda i: (0, 0), pipeline_mode=pl.Buffered(1)), row, vec, row] + [ANY] * ns,
        out_specs=[row, vec] + [ANY] * ns,
        out_shape=[jax.ShapeDtypeStruct((s, d), F32), jax.ShapeDtypeStruct((1, d), F32)]
        + [jax.ShapeDtypeStruct(p.shape, p.dtype) for p in scatter],
        scratch_shapes=_scatter_scratch(ns), compiler_params=_params(("arbitrary",)),
    )(dproj, wx, x, w, resid, *scatter)
    return outs[0], outs[1], _keep_own_blocks(outs[2:], scatter)


def out_proj_rms_fwd(y_ssd, y_fox, w_out, x, norm_w, *, tm=512):
    s, d = x.shape

    def body(ys_ref, yf_ref, w_ref, x_ref, nw_ref, x1_ref, h_ref, ht_ref):
        for r in range(0, tm, UP_ROWS):
            rows = slice(r, r + UP_ROWS)
            x1 = x_ref[rows, :] + _dot(ys_ref[rows, :], w_ref[0:d, :]) + _dot(yf_ref[rows, :], w_ref[d:2 * d, :])
            x1_ref[rows, :] = x1
            rstd = lax.rsqrt(jnp.mean(x1 * x1, axis=-1, keepdims=True) + NORM_EPS)
            h = (x1 * rstd) * nw_ref[...]
            h_ref[rows, :] = h.astype(BF16)
            ht_ref[:, rows] = h.T.astype(BF16)

    row = pl.BlockSpec((tm, d), lambda i: (i, 0))
    return pl.pallas_call(
        body, name="out_proj_rms_fwd", grid=(s // tm,),
        in_specs=[row, row, pl.BlockSpec((2 * d, d), lambda i: (0, 0)), row, pl.BlockSpec((1, d), lambda i: (0, 0))],
        out_specs=[row, row, pl.BlockSpec((d, tm), lambda i: (0, i))],
        out_shape=[jax.ShapeDtypeStruct((s, d), F32), jax.ShapeDtypeStruct((s, d), BF16), jax.ShapeDtypeStruct((d, s), BF16)],
        compiler_params=_params(("parallel",)),
    )(y_ssd, y_fox, w_out, x, norm_w)


def rms_bwd_matmul(dh, x, w, resid, b, *, name, tm=512):
    s, d = x.shape
    n = b.shape[0]

    def body(dh_ref, x_ref, w_ref, res_ref, b_ref, dx_ref, dw_ref, prod_ref):
        part = jnp.zeros((1, d), F32)
        for r in range(0, tm, UP_ROWS):
            rows = slice(r, r + UP_ROWS)
            xv, dhv = x_ref[rows, :], dh_ref[rows, :]
            rstd = lax.rsqrt(jnp.mean(xv * xv, axis=-1, keepdims=True) + NORM_EPS)
            xh = xv * rstd
            g = dhv * w_ref[...]
            dx = res_ref[rows, :] + rstd * (g - xh * jnp.mean(g * xh, axis=-1, keepdims=True))
            dx_ref[rows, :] = dx
            prod_ref[rows, :] = _dot(dx, b_ref[...], "nt")
            part = part + jnp.sum(dhv * xh, axis=0, keepdims=True)

        @pl.when(pl.program_id(0) == 0)
        def _():
            dw_ref[...] = part

        @pl.when(pl.program_id(0) > 0)
        def _():
            dw_ref[...] += part

    row = pl.BlockSpec((tm, d), lambda i: (i, 0))
    vec = pl.BlockSpec((1, d), lambda i: (0, 0))
    return pl.pallas_call(
        body, name=name, grid=(s // tm,), in_specs=[row, row, vec, row, pl.BlockSpec((n, d), lambda i: (0, 0))],
        out_specs=[row, vec, pl.BlockSpec((tm, n), lambda i: (i, 0))],
        out_shape=[jax.ShapeDtypeStruct((s, d), F32), jax.ShapeDtypeStruct((1, d), F32), jax.ShapeDtypeStruct((s, n), F32)],
        compiler_params=_params(("arbitrary",)),
    )(dh, x, w, resid, b)


def down_proj_loss(act, w_down, x1, target, *, tm=512):
    s, d = x1.shape

    def body(a_ref, w_ref, x1_ref, t_ref, dy_ref, sq_ref):
        part = jnp.zeros((1, d), F32)
        for r in range(0, tm, UP_ROWS):
            rows = slice(r, r + UP_ROWS)
            e = x1_ref[rows, :] + _dot(a_ref[rows, :], w_ref[...]) - t_ref[rows, :]
            dy_ref[rows, :] = e / float(d)
            part = part + jnp.sum(e * e, axis=0, keepdims=True)

        @pl.when(pl.program_id(0) == 0)
        def _():
            sq_ref[...] = part

        @pl.when(pl.program_id(0) > 0)
        def _():
            sq_ref[...] += part

    row = pl.BlockSpec((tm, d), lambda i: (i, 0))
    vec = pl.BlockSpec((1, d), lambda i: (0, 0))
    return pl.pallas_call(
        body, name="down_proj_loss", grid=(s // tm,),
        in_specs=[pl.BlockSpec((tm, D_FF), lambda i: (i, 0)), pl.BlockSpec((D_FF, d), lambda i: (0, 0)), row, row],
        out_specs=[row, vec], out_shape=[jax.ShapeDtypeStruct((s, d), F32), jax.ShapeDtypeStruct((1, d), F32)],
        compiler_params=_params(("arbitrary",)),
    )(act, w_down, x1, target)


def _row_shifts(ext, k_taps):
    return [_shift_down(ext, j) for j in range(k_taps)]


def _conv_rows(shifts, w):
    k_taps = len(shifts)
    acc = w[k_taps - 1:k_taps, :] * shifts[0]
    for k in range(k_taps - 1):
        acc = acc + w[k:k + 1, :] * shifts[k_taps - 1 - k]
    return acc


def _conv_weight_grad(dcur, shifts, rows, width):
    k_taps = len(shifts)
    out = [jnp.sum(dcur * shifts[k_taps - 1 - k][rows], axis=0, keepdims=True) for k in range(k_taps)]
    out.append(jnp.sum(dcur, axis=0, keepdims=True))
    return _stack_rows(out, width)


def _conv_rows_transposed(dext, w, k_taps):
    acc = w[k_taps - 1:k_taps, :] * dext
    for k in range(k_taps - 1):
        acc = acc + w[k:k + 1, :] * _shift_up(dext, k_taps - 1 - k)
    return acc


def _stack_rows(rows, width):
    ri = _row_iota((8, width))
    out = jnp.zeros((8, width), F32)
    for k, r in enumerate(rows):
        out = out + jnp.where(ri == k, r, 0.0)
    return out


UP_SHARD = 1408
UP_ROWS = 256


def up_ffn_fwd(hf, a_up, conv_w8, conv_b, *, tm=512):
    s = hf.shape[0]

    def body(a_ref, bg_ref, bv_ref, wg_ref, wv_ref, cbg_ref, cbv_ref, hu_ref, act_ref, actt_ref, carry):
        i, j = pl.program_id(0), pl.program_id(1)
        prev_g = jnp.where(i == 0, 0.0, carry[0, j])
        prev_v = jnp.where(i == 0, 0.0, carry[1, j])
        for r in range(0, tm, UP_ROWS):
            rows = slice(r, r + UP_ROWS)
            a = a_ref[rows, :]
            hg, hv = _dot(a, bg_ref[...]), _dot(a, bv_ref[...])
            hu_ref[0, rows, :] = hg
            hu_ref[1, rows, :] = hv
            gc = _conv_rows(_row_shifts(jnp.concatenate([prev_g, hg], axis=0), FFN_CONV), wg_ref[...])[8:] + cbg_ref[...]
            vc = _conv_rows(_row_shifts(jnp.concatenate([prev_v, hv], axis=0), FFN_CONV), wv_ref[...])[8:] + cbv_ref[...]
            act = gc * _sigmoid(gc) * vc
            act_ref[rows, :] = act.astype(BF16)
            actt_ref[:, rows] = act.T.astype(BF16)
            prev_g, prev_v = hg[UP_ROWS - 8:], hv[UP_ROWS - 8:]
        carry[0, j] = prev_g
        carry[1, j] = prev_v

    shard = lambda off: pl.BlockSpec((None, D_MODEL, UP_SHARD), lambda i, j: (j + off, 0, 0))
    taps = lambda off: pl.BlockSpec((8, UP_SHARD), lambda i, j: (0, j + off))
    bias = lambda off: pl.BlockSpec((1, UP_SHARD), lambda i, j: (0, j + off))
    return pl.pallas_call(
        body, name="up_ffn_fwd", grid=(s // tm, 2),
        in_specs=[pl.BlockSpec((tm, D_MODEL), lambda i, j: (i, 0)), shard(0), shard(2), taps(0), taps(2), bias(0), bias(2)],
        out_specs=[pl.BlockSpec((2, tm, UP_SHARD), lambda i, j: (0, i, j)), pl.BlockSpec((tm, UP_SHARD), lambda i, j: (i, j)),
                   pl.BlockSpec((UP_SHARD, tm), lambda i, j: (j, i))],
        out_shape=[jax.ShapeDtypeStruct((2, s, D_FF), F32), jax.ShapeDtypeStruct((s, D_FF), BF16),
                   jax.ShapeDtypeStruct((D_FF, s), BF16)],
        scratch_shapes=[pltpu.VMEM((2, 2, 8, UP_SHARD), F32)], compiler_params=_params(("arbitrary", "arbitrary")),
    )(hf, a_up, a_up, conv_w8, conv_w8, conv_b, conv_b)


def ffn_mid_bwd(hu, dact, conv_w8, conv_b, *, tm=1024, tc=256):
    s = hu.shape[1]
    ncol = D_FF // tc
    nrow = s // tm
    r8 = tm // 8

    def body(g_ref, v_ref, gp_ref, vp_ref, gn_ref, vn_ref, da_ref, dan_ref, wg_ref, wv_ref, bg_ref, bv_ref,
             dhu_ref, wgo_ref, wvo_ref):
        i = pl.program_id(1)
        first = i == 0
        last = i == nrow - 1

        def ext_of(cur_ref, prev_ref, next_ref):
            prev = jnp.where(first, 0.0, prev_ref[...])
            return jnp.concatenate([prev, cur_ref[...], next_ref[...]], axis=0)

        g_sh = _row_shifts(ext_of(g_ref, gp_ref, gn_ref), FFN_CONV)
        v_sh = _row_shifts(ext_of(v_ref, vp_ref, vn_ref), FFN_CONV)
        gc = _conv_rows(g_sh, wg_ref[...]) + bg_ref[...]
        vc = _conv_rows(v_sh, wv_ref[...]) + bv_ref[...]
        da_ext = jnp.concatenate([jnp.zeros((8, tc), F32), da_ref[...], jnp.where(last, 0.0, dan_ref[...])], axis=0)
        silu, dsilu = _silu_and_grad(gc)
        dgc = da_ext * vc * dsilu
        dvc = da_ext * silu
        dhu_ref[0] = _conv_rows_transposed(dgc, wg_ref[...], FFN_CONV)[8:8 + tm].astype(BF16)
        dhu_ref[1] = _conv_rows_transposed(dvc, wv_ref[...], FFN_CONV)[8:8 + tm].astype(BF16)

        cur = slice(8, 8 + tm)
        pg = _conv_weight_grad(dgc[cur], g_sh, cur, tc)
        pv = _conv_weight_grad(dvc[cur], v_sh, cur, tc)

        @pl.when(first)
        def _():
            wgo_ref[...] = pg
            wvo_ref[...] = pv

        @pl.when(i > 0)
        def _():
            wgo_ref[...] += pg
            wvo_ref[...] += pv

    def prev_idx(i):
        return jnp.maximum(i * r8 - 1, 0)

    def next_idx(i):
        return jnp.minimum((i + 1) * r8, s // 8 - 1)

    half = lambda k, rows, row_index: pl.BlockSpec((None, rows, tc), lambda j, i: (k, row_index(i), j))
    in_specs = [
        half(0, tm, lambda i: i), half(1, tm, lambda i: i),
        half(0, 8, prev_idx), half(1, 8, prev_idx),
        half(0, 8, next_idx), half(1, 8, next_idx),
        pl.BlockSpec((tm, tc), lambda j, i: (i, j)),
        pl.BlockSpec((8, tc), lambda j, i: (next_idx(i), j)),
        pl.BlockSpec((8, tc), lambda j, i: (0, j)),
        pl.BlockSpec((8, tc), lambda j, i: (0, j + ncol)),
        pl.BlockSpec((1, tc), lambda j, i: (0, j)),
        pl.BlockSpec((1, tc), lambda j, i: (0, j + ncol)),
    ]
    out_specs = [pl.BlockSpec((2, tm, tc), lambda j, i: (0, i, j)), pl.BlockSpec((8, tc), lambda j, i: (0, j)),
                 pl.BlockSpec((8, tc), lambda j, i: (0, j))]
    out_shape = [jax.ShapeDtypeStruct((2, s, D_FF), BF16),
                 jax.ShapeDtypeStruct((8, D_FF), F32), jax.ShapeDtypeStruct((8, D_FF), F32)]
    return pl.pallas_call(
        body, name="ffn_mid_bwd", grid=(ncol, nrow), in_specs=in_specs, out_specs=out_specs, out_shape=out_shape,
        compiler_params=_params(("parallel", "arbitrary")),
    )(hu, hu, hu, hu, hu, hu, dact, dact, conv_w8, conv_w8, conv_b, conv_b)


def _softplus(x):
    return jnp.maximum(x, 0.0) + jnp.log(1.0 + jnp.exp(-jnp.abs(x)))


def _cumsum_rows(v):
    n = v.shape[0]
    ri = _row_iota(v.shape)
    sh = 1
    while sh < n:
        v = v + jnp.where(ri >= sh, _shift_down(v, sh), 0.0)
        sh *= 2
    return v


def _rev_cumsum_rows(v):
    n = v.shape[0]
    ri = _row_iota(v.shape)
    sh = 1
    while sh < n:
        v = v + jnp.where(ri < n - sh, _shift_up(v, sh), 0.0)
        sh *= 2
    return v


def _total(v):
    return jnp.sum(jnp.sum(v, axis=1, keepdims=True), axis=0, keepdims=True)


def _ssd_in_specs(rev_nc=None):
    def ch(c):
        return c if rev_nc is None else rev_nc - 1 - c

    def prev(c):
        return jnp.maximum(ch(c) * (SSD_CHUNK // 8) - 1, 0)

    L = SSD_CHUNK
    return [
        pl.BlockSpec((L, 1024), lambda c: (ch(c), 0)),
        pl.BlockSpec((L, 1024), lambda c: (ch(c), 1)),
        pl.BlockSpec((L, 256), lambda c: (ch(c), 20)),
        pl.BlockSpec((L, 256), lambda c: (ch(c), 21)),
        pl.BlockSpec((8, 1024), lambda c: (prev(c), 1)),
        pl.BlockSpec((8, 256), lambda c: (prev(c), 20)),
        pl.BlockSpec((8, 256), lambda c: (prev(c), 21)),
        pl.BlockSpec((8, 1024), lambda c: (0, 0)),
        pl.BlockSpec((8, 256), lambda c: (0, 4)),
        pl.BlockSpec((8, 256), lambda c: (0, 5)),
        pl.BlockSpec((1, 1024), lambda c: (0, 0)),
        pl.BlockSpec((1, 256), lambda c: (0, 4)),
        pl.BlockSpec((1, 256), lambda c: (0, 5)),
        pl.BlockSpec((L, SMALL_COLS), lambda c: (ch(c), SMALL_BLOCK)),
        pl.BlockSpec((8, 128), lambda c: (0, 0)),
        pl.BlockSpec((1, 1024), lambda c: (0, 0)),
    ]


def _ssd_conv_pre(cur_ref, prev_ref, w_ref, b_ref, first):
    prev = jnp.where(first, 0.0, prev_ref[...])
    shifts = _row_shifts(jnp.concatenate([prev, cur_ref[...]], axis=0), SSD_CONV)
    return shifts, _conv_rows(shifts, w_ref[...])[8:] + b_ref[...]


def _ssd_time_consts(small_ref, sp_ref):
    dt_pre = small_ref[...] + sp_ref[0:1, :]
    dt = _softplus(dt_pre)
    a = -jnp.exp(sp_ref[1:2, :])
    acs = _cumsum_rows(dt * a)
    return dt_pre, dt, a, acs


def ssd_fwd(proj, conv_w8, conv_b, smallp, norm_w):
    s = proj.shape[0]
    nc = s // SSD_CHUNK
    L = SSD_CHUNK

    def body(z_ref, xs_ref, b_ref, c_ref, xsp_ref, bp_ref, cp_ref, wx_ref, wb_ref, wc_ref, bx_ref, bb_ref, bc_ref,
             small_ref, sp_ref, nw_ref, y_ref, yt_ref, ypre_ref, st_ref, state):
        first = pl.program_id(0) == 0

        @pl.when(first)
        def _():
            state[...] = jnp.zeros_like(state)

        xs = _ssd_conv_pre(xs_ref, xsp_ref, wx_ref, bx_ref, first)[1]
        xs = xs * _sigmoid(xs)
        bm = _ssd_conv_pre(b_ref, bp_ref, wb_ref, bb_ref, first)[1]
        bm = bm * _sigmoid(bm)
        cm = _ssd_conv_pre(c_ref, cp_ref, wc_ref, bc_ref, first)[1]
        cm = cm * _sigmoid(cm)
        _, dt, _, acs = _ssd_time_consts(small_ref, sp_ref)
        acs_t = acs.T
        li = _lane_iota((L, L))
        ri = _row_iota((L, L))
        tri = ri >= li
        lo = li < HEAD_DIM
        st_ref[0] = state[...]
        for g in range(2):
            bg = bm[:, 128 * g:128 * g + 128]
            cg = cm[:, 128 * g:128 * g + 128]
            gmat = _dot(cg, bg, "nt")
            for pp in range(4):
                p = 4 * g + pp
                h0, h1 = 2 * p, 2 * p + 1
                x = xs[:, 128 * p:128 * p + 128]
                a0, a1 = acs[:, h0:h0 + 1], acs[:, h1:h1 + 1]
                xdt = x * jnp.where(lo, dt[:, h0:h0 + 1], dt[:, h1:h1 + 1])
                m0 = gmat * jnp.exp(jnp.where(tri, a0 - acs_t[h0:h0 + 1, :], NEG_BIG))
                m1 = gmat * jnp.exp(jnp.where(tri, a1 - acs_t[h1:h1 + 1, :], NEG_BIG))
                yd = _dot(m0, jnp.where(lo, xdt, 0.0)) + _dot(m1, jnp.where(lo, 0.0, xdt))
                hin = state[p]
                yo = _dot(cg, hin, "nt") * jnp.exp(jnp.where(lo, a0, a1))
                dskip = jnp.where(lo[0:1], sp_ref[2:3, h0:h0 + 1], sp_ref[2:3, h1:h1 + 1])
                ypre_ref[:, 128 * p:128 * p + 128] = yd + yo + dskip * x
                al0, al1 = acs[L - 1:L, h0:h0 + 1], acs[L - 1:L, h1:h1 + 1]
                w = jnp.exp(jnp.where(lo, al0 - a0, al1 - a1))
                dec = jnp.exp(jnp.where(ri < HEAD_DIM, al0, al1))
                state[p] = dec * hin + _dot(xdt * w, bg, "tn")
        z = z_ref[...]
        yg = ypre_ref[...] * (z * _sigmoid(z))
        for g in range(2):
            seg = yg[:, 512 * g:512 * g + 512]
            r = lax.rsqrt(jnp.mean(seg * seg, axis=-1, keepdims=True) + NORM_EPS)
            out = (seg * r) * nw_ref[:, 512 * g:512 * g + 512]
            y_ref[:, 512 * g:512 * g + 512] = out.astype(BF16)
            yt_ref[512 * g:512 * g + 512, :] = out.T.astype(BF16)

    row = pl.BlockSpec((L, 1024), lambda c: (c, 0))
    return pl.pallas_call(
        body, name="ssd_fwd", grid=(nc,), in_specs=_ssd_in_specs(),
        out_specs=[row, pl.BlockSpec((1024, L), lambda c: (0, c)), row,
                   pl.BlockSpec((1, N_PAIRS, 128, 128), lambda c: (c, 0, 0, 0))],
        out_shape=[jax.ShapeDtypeStruct((s, 1024), BF16), jax.ShapeDtypeStruct((1024, s), BF16),
                   jax.ShapeDtypeStruct((s, 1024), F32), jax.ShapeDtypeStruct((nc, N_PAIRS, 128, 128), F32)],
        scratch_shapes=[pltpu.VMEM((N_PAIRS, 128, 128), F32)],
        compiler_params=_params(("arbitrary",)),
    )(proj, proj, proj, proj, proj, proj, proj, conv_w8, conv_w8, conv_w8, conv_b, conv_b, conv_b, proj, smallp, norm_w)


def ssd_bwd(proj, conv_w8, conv_b, smallp, norm_w, ypre, states, dy, sel, swap=()):
    s = proj.shape[0]
    nc = s // SSD_CHUNK
    L = SSD_CHUNK

    ns = len(swap)
    n_in, n_out, n_scratch = 20, 10, 11

    def body(*refs):
        own = refs[:n_in] + refs[n_in + ns:n_in + ns + n_out] + refs[n_in + 2 * ns + n_out:n_in + 2 * ns + n_out + n_scratch]
        if ns:
            start, finish = _pair_swap_phases(refs[n_in:n_in + ns], refs[n_in + ns + n_out:n_in + 2 * ns + n_out],
                                              *refs[n_in + 2 * ns + n_out + n_scratch:])
            pl.when(pl.program_id(0) == 0)(start)
        compute(*own)
        if ns:
            pl.when(pl.program_id(0) == nc - 1)(finish)

    def compute(z_ref, xs_ref, b_ref, c_ref, xsp_ref, bp_ref, cp_ref, wx_ref, wb_ref, wc_ref, bx_ref, bb_ref, bc_ref,
                small_ref, sp_ref, nw_ref, ypre_ref, st_ref, dy_ref, sel_ref,
                dz_ref, dxs_ref, db_ref, dc_ref, dsmall_ref, gwx_ref, gwb_ref, gwc_ref, gsp_ref, gnw_ref,
                dstate, carry_x, carry_b, carry_c, dxs_buf, dbm_buf, dcm_buf, qcs, col_sums, acs_terms, dt_terms):
        step = pl.program_id(0)
        col_sums[...] = jnp.zeros_like(col_sums)
        first_chunk = step == nc - 1
        start = step == 0

        @pl.when(start)
        def _():
            dstate[...] = jnp.zeros_like(dstate)
            carry_x[...] = jnp.zeros_like(carry_x)
            carry_b[...] = jnp.zeros_like(carry_b)
            carry_c[...] = jnp.zeros_like(carry_c)

        xs_sh, xs_pre = _ssd_conv_pre(xs_ref, xsp_ref, wx_ref, bx_ref, first_chunk)
        b_sh, b_pre = _ssd_conv_pre(b_ref, bp_ref, wb_ref, bb_ref, first_chunk)
        c_sh, c_pre = _ssd_conv_pre(c_ref, cp_ref, wc_ref, bc_ref, first_chunk)
        xs, xs_ds = _silu_and_grad(xs_pre)
        bm, b_ds = _silu_and_grad(b_pre)
        cm, c_ds = _silu_and_grad(c_pre)
        dt_pre, dt, a, acs = _ssd_time_consts(small_ref, sp_ref)
        acs_t = acs.T
        li = _lane_iota((L, L))
        ri = _row_iota((L, L))
        tri = ri >= li
        lo = li < HEAD_DIM
        lo_rows = ri < HEAD_DIM
        li1 = _lane_iota((1, L))

        z = z_ref[...]
        sz, dsz = _silu_and_grad(z)
        y = ypre_ref[...]
        yg = y * sz
        dout = dy_ref[...]
        dyg_parts = []
        gnw_parts = []
        for g in range(2):
            sl = slice(512 * g, 512 * g + 512)
            seg = yg[:, sl]
            r = lax.rsqrt(jnp.mean(seg * seg, axis=-1, keepdims=True) + NORM_EPS)
            n = seg * r
            gnw_parts.append(jnp.sum(dout[:, sl] * n, axis=0, keepdims=True))
            gg = dout[:, sl] * nw_ref[:, sl]
            dyg_parts.append(r * (gg - n * jnp.mean(gg * n, axis=-1, keepdims=True)))
        dyg = jnp.concatenate(dyg_parts, axis=1)
        gnw = jnp.concatenate(gnw_parts, axis=1)
        dz_ref[...] = (dyg * y * dsz).astype(BF16)
        dypre = dyg * sz

        qcs[...] = jnp.zeros_like(qcs)
        dalast = jnp.zeros((1, L), F32)
        for g in range(2):
            bg = bm[:, 128 * g:128 * g + 128]
            cg = cm[:, 128 * g:128 * g + 128]
            gmat = _dot(cg, bg, "nt")
            dgmat = jnp.zeros((L, L), F32)
            dbg = jnp.zeros((L, L), F32)
            dcg = jnp.zeros((L, L), F32)
            for pp in range(4):
                p = 4 * g + pp
                h0, h1 = 2 * p, 2 * p + 1
                lanes = slice(128 * p, 128 * p + 128)
                x = xs[:, lanes]
                dyp = dypre[:, lanes]
                a0, a1 = acs[:, h0:h0 + 1], acs[:, h1:h1 + 1]
                dtl = jnp.where(lo, dt[:, h0:h0 + 1], dt[:, h1:h1 + 1])
                xdt = x * dtl
                l0 = jnp.exp(jnp.where(tri, a0 - acs_t[h0:h0 + 1, :], NEG_BIG))
                l1 = jnp.exp(jnp.where(tri, a1 - acs_t[h1:h1 + 1, :], NEG_BIG))
                m0, m1 = gmat * l0, gmat * l1
                dskip = jnp.where(lo[0:1], sp_ref[2:3, h0:h0 + 1], sp_ref[2:3, h1:h1 + 1])
                col_sums[0:1, lanes] = jnp.sum(dyp * x, axis=0, keepdims=True)
                dx = dyp * dskip
                dy0, dy1 = jnp.where(lo, dyp, 0.0), jnp.where(lo, 0.0, dyp)
                x0, x1 = jnp.where(lo, xdt, 0.0), jnp.where(lo, 0.0, xdt)
                dm0, dm1 = _dot(dy0, x0, "nt"), _dot(dy1, x1, "nt")
                dxdt = _dot(m0, dy0, "tn") + _dot(m1, dy1, "tn")
                q0, q1 = dm0 * m0, dm1 * m1
                qcs[h0:h0 + 1, :] = jnp.sum(q0, axis=0, keepdims=True)
                qcs[h1:h1 + 1, :] = jnp.sum(q1, axis=0, keepdims=True)
                row_terms = jnp.where(lo, q0 + pltpu.roll(q0, HEAD_DIM, 1), q1 + pltpu.roll(q1, HEAD_DIM, 1))
                dgmat = dgmat + dm0 * l0 + dm1 * l1
                hin = st_ref[0, p]
                e = jnp.exp(jnp.where(lo, a0, a1))
                ch = _dot(cg, hin, "nt")
                dch = dyp * e
                dcg = dcg + _dot(dch, hin)
                dhin = _dot(dch, cg, "tn")
                dhout = dstate[p]
                al0, al1 = acs[L - 1:L, h0:h0 + 1], acs[L - 1:L, h1:h1 + 1]
                dec = jnp.exp(jnp.where(lo_rows, al0, al1))
                dhin = dhin + dec * dhout
                dal = dhout * hin * dec
                dal0 = _total(jnp.where(lo_rows, dal, 0.0))
                dal1 = _total(dal) - dal0
                dalast = dalast + jnp.where(li1 == h0, dal0, 0.0) + jnp.where(li1 == h1, dal1, 0.0)
                w = jnp.exp(jnp.where(lo, al0 - a0, al1 - a1))
                xw = xdt * w
                dxw = _dot(bg, dhout, "nt")
                dbg = dbg + _dot(xw, dhout)
                dxdt = dxdt + dxw * w
                dww = dxw * xw
                col_sums[1:2, lanes] = jnp.sum(dww, axis=0, keepdims=True)
                acs_terms[:, lanes] = row_terms + dch * ch - dww
                dx = dx + dxdt * dtl
                dt_terms[:, lanes] = dxdt * x
                dxs_buf[:, lanes] = dx
                dstate[p] = dhin
            dcg = dcg + _dot(dgmat, bg)
            dbg = dbg + _dot(dgmat, cg, "tn")
            dbm_buf[:, 128 * g:128 * g + 128] = dbg
            dcm_buf[:, 128 * g:128 * g + 128] = dcg

        head_sums = _split3_dot(col_sums[...], sel_ref[...])
        dskip_g = head_sums[0:1, :]
        dalast = dalast + head_sums[1:2, :]
        ddt = _split3_dot(dt_terms[...], sel_ref[...])
        dacs_tot = _split3_dot(acs_terms[...], sel_ref[...]) - qcs[...].T + jnp.where(ri == L - 1, dalast, 0.0)
        dstep = _rev_cumsum_rows(dacs_tot)
        ddt = ddt + dstep * a
        head_lane = li < N_HEADS
        ddt_pre = jnp.where(head_lane, ddt * _sigmoid(dt_pre), 0.0)
        dsmall_ref[...] = ddt_pre
        da = jnp.sum(jnp.where(head_lane, dstep * dt, 0.0), axis=0, keepdims=True)
        gsp = _stack_rows([jnp.sum(ddt_pre, axis=0, keepdims=True), da * a, dskip_g], L)

        def conv_back(dpost, ds, shifts, w_ref, carry, out_ref, width):
            dpre = dpost * ds
            dext = jnp.concatenate([dpre, carry[...]], axis=0)
            out_ref[...] = _conv_rows_transposed(dext, w_ref[...], SSD_CONV)[:L].astype(BF16)
            carry[...] = dpre[0:8]
            return _conv_weight_grad(dpre, shifts, slice(8, 8 + L), width)

        gwx = conv_back(dxs_buf[...], xs_ds, xs_sh, wx_ref, carry_x, dxs_ref, 1024)
        gwb = conv_back(dbm_buf[...], b_ds, b_sh, wb_ref, carry_b, db_ref, 256)
        gwc = conv_back(dcm_buf[...], c_ds, c_sh, wc_ref, carry_c, dc_ref, 256)

        @pl.when(start)
        def _():
            gwx_ref[...] = gwx
            gwb_ref[...] = gwb
            gwc_ref[...] = gwc
            gsp_ref[...] = gsp
            gnw_ref[...] = gnw

        @pl.when(step > 0)
        def _():
            gwx_ref[...] += gwx
            gwb_ref[...] += gwb
            gwc_ref[...] += gwc
            gsp_ref[...] += gsp
            gnw_ref[...] += gnw

    def ch(c):
        return nc - 1 - c

    row = pl.BlockSpec((L, 1024), lambda c: (ch(c), 0))
    row256 = pl.BlockSpec((L, 256), lambda c: (ch(c), 0))
    in_specs = _ssd_in_specs(rev_nc=nc) + [row, pl.BlockSpec((1, N_PAIRS, 128, 128), lambda c: (ch(c), 0, 0, 0)), row,
                                           pl.BlockSpec((1024, 128), lambda c: (0, 0))]
    out_specs = [row, row, row256, row256, pl.BlockSpec((L, 128), lambda c: (ch(c), 0)),
                 pl.BlockSpec((8, 1024), lambda c: (0, 0)), pl.BlockSpec((8, 256), lambda c: (0, 0)),
                 pl.BlockSpec((8, 256), lambda c: (0, 0)), pl.BlockSpec((8, 128), lambda c: (0, 0)),
                 pl.BlockSpec((1, 1024), lambda c: (0, 0))]
    out_shape = [jax.ShapeDtypeStruct((s, 1024), BF16), jax.ShapeDtypeStruct((s, 1024), BF16),
                 jax.ShapeDtypeStruct((s, 256), BF16), jax.ShapeDtypeStruct((s, 256), BF16),
                 jax.ShapeDtypeStruct((s, 128), F32),
                 jax.ShapeDtypeStruct((8, 1024), F32), jax.ShapeDtypeStruct((8, 256), F32),
                 jax.ShapeDtypeStruct((8, 256), F32), jax.ShapeDtypeStruct((8, 128), F32),
                 jax.ShapeDtypeStruct((1, 1024), F32)]
    scratch = [pltpu.VMEM((N_PAIRS, 128, 128), F32), pltpu.VMEM((8, 1024), F32), pltpu.VMEM((8, 256), F32),
               pltpu.VMEM((8, 256), F32), pltpu.VMEM((L, 1024), F32), pltpu.VMEM((L, 256), F32), pltpu.VMEM((L, 256), F32),
               pltpu.VMEM((L, L), F32), pltpu.VMEM((8, 1024), F32), pltpu.VMEM((L, 1024), F32), pltpu.VMEM((L, 1024), F32)]
    assert (len(in_specs), len(out_specs), len(scratch)) == (n_in, n_out, n_scratch)
    outs = pl.pallas_call(
        body, name="ssd_bwd", grid=(nc,), in_specs=in_specs + [ANY] * ns, out_specs=out_specs + [ANY] * ns,
        out_shape=out_shape + _pair_swap_out_shapes(swap), scratch_shapes=scratch + (_pair_swap_scratch(ns) if ns else []),
        compiler_params=_params(("arbitrary",)),
    )(proj, proj, proj, proj, proj, proj, proj, conv_w8, conv_w8, conv_w8, conv_b, conv_b, conv_b, proj, smallp, norm_w,
      ypre, states, dy, sel, *swap)
    return (*outs[:n_out], list(outs[n_out:]))


FOX_SCALE = HEAD_DIM ** -0.5
FOX_T = 256
Q_COL, K_COL, V_COL = 2, 3, 4


def _split_dot(v, m, terms):
    out, rest = None, v
    for i in range(terms):
        piece = rest.astype(BF16)
        out = _dot(piece, m) if out is None else out + _dot(piece, m)
        if i + 1 < terms:
            rest = rest - piece.astype(F32)
    return out


def _split3_dot(v, m):
    return _split_dot(v, m, 3)


def _head_mean(x, sel_ref, selt_ref):
    return _dot(x, sel_ref[...]) * (1.0 / HEAD_DIM)


def _head_spread(v, selt_ref):
    return _split_dot(v, selt_ref[...], 2)


def _head_rstd(x, sel_ref, selt_ref):
    return _head_spread(lax.rsqrt(_head_mean(x * x, sel_ref, selt_ref) + NORM_EPS), selt_ref)


def fox_tables():
    r = np.arange(3 * 128)
    piece, lane = r // 128, r % 128
    head = lane - F_LANE
    is_head = np.logical_and(head >= 0, head < N_HEADS)
    col = 128 * (head // 2) + HEAD_DIM * (1 - head % 2) + piece
    cols = np.arange(1024)
    place_q = np.logical_and(is_head[:, None], cols[None, :] == col[:, None])
    place_k = np.logical_and(is_head[:, None], cols[None, :] == (col + 3)[:, None])
    ones_q = np.logical_and(cols % HEAD_DIM >= 3, cols % HEAD_DIM < 6)[None]
    ones_k = (cols % HEAD_DIM < 3)[None]
    h = np.arange(128) - F_LANE
    ok = np.logical_and(h >= 0, h < N_HEADS)
    same_pair = cols[:, None] // 128 == (h // 2)[None, :]
    fold_even = np.logical_and(np.logical_and(ok, h % 2 == 0)[None, :], same_pair)
    fold_odd = np.logical_and(np.logical_and(ok, h % 2 == 1)[None, :], same_pair)
    as_bf16 = lambda t: jnp.asarray(t.astype(np.float32), BF16)
    return (as_bf16(place_q), as_bf16(place_k), jnp.asarray(ones_q, F32), jnp.asarray(ones_k, F32),
            as_bf16(fold_even), as_bf16(fold_odd))


def fox_prep(proj, smallp, qw, kw, sel, selt, place_q, place_k, ones_q, ones_k, *, tm=256):
    s = proj.shape[0]

    def body(q_ref, k_ref, v_ref, small_ref, sp_ref, qw_ref, kw_ref, sel_ref, selt_ref, pq_ref, pk_ref, oq_ref, ok_ref,
             qn_ref, kn_ref, aq_ref, ak_ref, vb_ref, knt_ref, akt_ref, vt_ref, carry):
        @pl.when(pl.program_id(0) == 0)
        def _():
            carry[...] = jnp.zeros_like(carry)

        q = q_ref[...]
        qn_ref[...] = (((q * _head_rstd(q, sel_ref, selt_ref)) * qw_ref[...]) * FOX_SCALE).astype(BF16)
        k = k_ref[...]
        kn = ((k * _head_rstd(k, sel_ref, selt_ref)) * kw_ref[...]).astype(BF16)
        kn_ref[...] = kn
        knt_ref[...] = kn.astype(F32).T.astype(BF16)
        vb_ref[...] = v_ref[...].astype(BF16)
        vt_ref[...] = v_ref[...].T.astype(BF16)
        li = _lane_iota((tm, 128))
        f_lane = jnp.logical_and(li >= F_LANE, li < F_LANE + N_HEADS)
        logf = jnp.where(f_lane, -_softplus(-(small_ref[...] + sp_ref[3:4, :])), 0.0)
        cum = _cumsum_rows(logf) + carry[...]
        carry[...] = cum[tm - 1:tm, :]
        hi = cum.astype(BF16)
        r1 = cum - hi.astype(F32)
        mid = r1.astype(BF16)
        lo = (r1 - mid.astype(F32)).astype(BF16)
        pieces = jnp.concatenate([hi, mid, lo], axis=1)
        aq_ref[...] = (_dot(pieces, pq_ref[...]) + oq_ref[...]).astype(BF16)
        ak = ok_ref[...] - _dot(pieces, pk_ref[...])
        ak_ref[...] = ak.astype(BF16)
        akt_ref[...] = ak.T.astype(BF16)

    row = pl.BlockSpec((tm, 1024), lambda i: (i, 0))
    col = pl.BlockSpec((1024, tm), lambda i: (0, i))
    vec = pl.BlockSpec((1, 1024), lambda i: (0, 0))
    table = pl.BlockSpec((384, 1024), lambda i: (0, 0))
    wide = jax.ShapeDtypeStruct((s, 1024), BF16)
    tall = jax.ShapeDtypeStruct((1024, s), BF16)
    return pl.pallas_call(
        body, name="fox_prep", grid=(s // tm,),
        in_specs=[pl.BlockSpec((tm, 1024), lambda i: (i, Q_COL)), pl.BlockSpec((tm, 1024), lambda i: (i, K_COL)),
                  pl.BlockSpec((tm, 1024), lambda i: (i, V_COL)),
                  pl.BlockSpec((tm, 128), lambda i: (i, SMALL_BLOCK)), pl.BlockSpec((8, 128), lambda i: (0, 0)), vec, vec,
                  pl.BlockSpec((1024, 128), lambda i: (0, 0)), pl.BlockSpec((128, 1024), lambda i: (0, 0)),
                  table, table, vec, vec],
        out_specs=[row, row, row, row, row, col, col, col],
        out_shape=[wide, wide, wide, wide, wide, tall, tall, tall],
        scratch_shapes=[pltpu.VMEM((1, 128), F32)], compiler_params=_params(("arbitrary",)),
    )(proj, proj, proj, proj, smallp, qw, kw, sel, selt, place_q, place_k, ones_q, ones_k)


def fox_fwd(qn, kn, aq, ak, vt, shards=()):
    s = qn.shape[0]
    t = FOX_T
    nq = s // t
    ng = len(shards)

    def body(*refs):
        q_ref, k_ref, aq_ref, ak_ref, vt_ref = refs[:5]
        o_ref, ot_ref, lse_ref = refs[5 + ng:8 + ng]
        p = pl.program_id(0)
        if ng:
            start, forward, finish = _gather_phases(refs[5:5 + ng], refs[8 + ng:8 + 2 * ng], *refs[8 + 2 * ng:])
            pl.when(p == 0)(start)
            pl.when(p == N_PAIRS // 2)(forward)

        @pl.when(p == 0)
        def _():
            lse_ref[...] = jnp.zeros_like(lse_ref)

        lo = _lane_iota((t, 128)) < HEAD_DIM
        lo_rows = _row_iota((128, t)) < HEAD_DIM
        causal_t = _lane_iota((t, t)) >= _row_iota((t, t))

        def q_loop(qi, _):
            q0 = pl.multiple_of(qi * t, t)
            qv, aqv = q_ref[pl.ds(q0, t), :], aq_ref[pl.ds(q0, t), :]
            qa, qb = jnp.where(lo, qv, aqv), jnp.where(lo, aqv, qv)

            def scores(kj):
                k0 = pl.multiple_of(kj * t, t)
                kv, akv = k_ref[pl.ds(k0, t), :], ak_ref[pl.ds(k0, t), :]
                return _dot(jnp.where(lo, kv, akv), qa, "nt"), _dot(jnp.where(lo, akv, kv), qb, "nt")

            def update(kj, stats, s0, s1):
                m0, l0, m1, l1, acc = stats
                vtv = vt_ref[:, pl.ds(pl.multiple_of(kj * t, t), t)]
                n0 = jnp.maximum(m0, jnp.max(s0, axis=0, keepdims=True))
                n1 = jnp.maximum(m1, jnp.max(s1, axis=0, keepdims=True))
                a0, a1 = jnp.exp(m0 - n0), jnp.exp(m1 - n1)
                p0, p1 = jnp.exp(s0 - n0), jnp.exp(s1 - n1)
                l0 = a0 * l0 + jnp.sum(p0, axis=0, keepdims=True)
                l1 = a1 * l1 + jnp.sum(p1, axis=0, keepdims=True)
                acc = (jnp.where(lo_rows, a0, a1) * acc + _dot(jnp.where(lo_rows, vtv, 0.0), p0)
                       + _dot(jnp.where(lo_rows, 0.0, vtv), p1))
                return n0, l0, n1, l1, acc

            def step(kj, carry):
                stats, (s0, s1) = carry[:5], carry[5:]
                nxt = scores(kj + 1)
                return (*update(kj, stats, s0, s1), *nxt)

            def row(val):
                return jnp.full((1, t), val, F32)

            init = (row(NEG_BIG), row(0.0), row(NEG_BIG), row(0.0), jnp.zeros((128, t), F32), *scores(0))
            carry = lax.fori_loop(0, qi, step, init)
            s0, s1 = jnp.where(causal_t, carry[5], NEG_BIG), jnp.where(causal_t, carry[6], NEG_BIG)
            m0, l0, m1, l1, acc = update(qi, carry[:5], s0, s1)
            out_t = acc / jnp.where(lo_rows, l0, l1)
            ot_ref[:, pl.ds(q0, t)] = out_t.astype(BF16)
            o_ref[pl.ds(q0, t), :] = out_t.T.astype(BF16)
            ri = _row_iota((N_HEADS, t))
            old = lse_ref[:, pl.ds(q0, t)]
            lse_ref[:, pl.ds(q0, t)] = jnp.where(
                ri == 2 * p, m0 + jnp.log(l0), jnp.where(ri == 2 * p + 1, m1 + jnp.log(l1), old))
            return 0

        lax.fori_loop(0, nq, q_loop, 0)
        if ng:
            pl.when(p == N_PAIRS - 1)(finish)

    pair = pl.BlockSpec((s, 128), lambda p: (0, p))
    outs = pl.pallas_call(
        body, name="fox_fwd", grid=(N_PAIRS,),
        in_specs=[pair] * 4 + [pl.BlockSpec((128, s), lambda p: (p, 0))] + [ANY] * ng,
        out_specs=[pair, pl.BlockSpec((128, s), lambda p: (p, 0)), pl.BlockSpec((N_HEADS, s), lambda p: (0, 0))] + [ANY] * ng,
        out_shape=[jax.ShapeDtypeStruct((s, 1024), BF16), jax.ShapeDtypeStruct((1024, s), BF16),
                   jax.ShapeDtypeStruct((N_HEADS, s), F32)] + _gather_out_shapes(shards),
        scratch_shapes=_gather_scratch(ng) if ng else [],
        compiler_params=_params(("arbitrary",)),
    )(qn, kn, aq, ak, vt, *shards)
    return outs[0], outs[1], outs[2], list(outs[3:])


def fox_bwd(qn, kn, aq, ak, knt, akt, vb, lse, dmixed, parts=()):
    s = qn.shape[0]
    t = FOX_T
    nq = s // t
    once = pl.Buffered(1)
    ns = len(parts)

    def body(*refs):
        q_ref, k_ref, aq_ref, ak_ref, kt_ref, akt_ref, v_ref, lse_ref, do_ref = refs[:9]
        dq_ref, dk_ref, dv_ref, dc0_ref, dc1_ref = refs[9 + ns:14 + ns]
        p_scr, dp_scr = refs[14 + 2 * ns:16 + 2 * ns]
        p = pl.program_id(0)
        if ns:
            start, finish = _scatter_phases(refs[9:9 + ns], refs[14 + ns:14 + 2 * ns], *refs[16 + 2 * ns:])
            pl.when(p == 0)(start)
        dk_ref[...] = jnp.zeros_like(dk_ref)
        dv_ref[...] = jnp.zeros_like(dv_ref)
        dc0_ref[...] = jnp.zeros_like(dc0_ref)
        dc1_ref[...] = jnp.zeros_like(dc1_ref)
        lo = _lane_iota((t, 128)) < HEAD_DIM
        lo_rows = _row_iota((128, t)) < HEAD_DIM
        causal_t = _lane_iota((t, t)) >= _row_iota((t, t))

        def q_loop(qi, _):
            q0 = pl.multiple_of(qi * t, t)
            qv, aqv = q_ref[pl.ds(q0, t), :], aq_ref[pl.ds(q0, t), :]
            qa, qb = jnp.where(lo, qv, aqv), jnp.where(lo, aqv, qv)
            do = do_ref[pl.ds(q0, t), :]
            doa, dob = jnp.where(lo, do, 0.0).astype(BF16), jnp.where(lo, 0.0, do).astype(BF16)
            lse_blk = lse_ref[:, pl.ds(q0, t)]
            ri = _row_iota((N_HEADS, t))
            lse0 = jnp.sum(jnp.where(ri == 2 * p, lse_blk, 0.0), axis=0, keepdims=True)
            lse1 = jnp.sum(jnp.where(ri == 2 * p + 1, lse_blk, 0.0), axis=0, keepdims=True)

            def scores(kj):
                k0 = pl.multiple_of(kj * t, t)
                kv, akv = k_ref[pl.ds(k0, t), :], ak_ref[pl.ds(k0, t), :]
                return _dot(jnp.where(lo, kv, akv), qa, "nt"), _dot(jnp.where(lo, akv, kv), qb, "nt")

            def pass1(kj, d0, d1, diagonal):
                k0 = pl.multiple_of(kj * t, t)
                vv = v_ref[pl.ds(k0, t), :]
                s0, s1 = scores(kj)
                if diagonal:
                    s0, s1 = jnp.where(causal_t, s0, NEG_BIG), jnp.where(causal_t, s1, NEG_BIG)
                p0, p1 = jnp.exp(s0 - lse0), jnp.exp(s1 - lse1)
                dp0, dp1 = _dot(vv, doa, "nt"), _dot(vv, dob, "nt")
                p_scr[0, kj], p_scr[1, kj] = p0, p1
                dp_scr[0, kj], dp_scr[1, kj] = dp0, dp1
                dv_ref[pl.ds(k0, t), :] += _dot(p0, doa) + _dot(p1, dob)
                return d0 + jnp.sum(p0 * dp0, axis=0, keepdims=True), d1 + jnp.sum(p1 * dp1, axis=0, keepdims=True)

            zero = jnp.zeros((1, t), F32)
            d0, d1 = lax.fori_loop(0, qi, lambda kj, c: pass1(kj, *c, False), (zero, zero))
            d0, d1 = pass1(qi, d0, d1, True)

            def fold_lanes(v):
                return functools.reduce(lambda a, b: a + b, [v[:, 128 * i:128 * (i + 1)] for i in range(t // 128)])

            def pass2(kj, carry):
                dq0, dq1 = carry
                k0 = pl.multiple_of(kj * t, t)
                p0, p1 = p_scr[0, kj], p_scr[1, kj]
                ds0, ds1 = p0 * (dp_scr[0, kj] - d0), p1 * (dp_scr[1, kj] - d1)
                dk_ref[pl.ds(k0, t), :] += jnp.where(lo, _dot(ds0, qa), _dot(ds1, qb))
                dc0_ref[pl.ds(k0, t), :] += fold_lanes(ds0)
                dc1_ref[pl.ds(k0, t), :] += fold_lanes(ds1)
                ktv, aktv = kt_ref[:, pl.ds(k0, t)], akt_ref[:, pl.ds(k0, t)]
                return dq0 + _dot(jnp.where(lo_rows, ktv, aktv), ds0), dq1 + _dot(jnp.where(lo_rows, aktv, ktv), ds1)

            zq = jnp.zeros((128, t), F32)
            dq0, dq1 = lax.fori_loop(0, qi + 1, pass2, (zq, zq))
            dq_ref[pl.ds(q0, t), :] = jnp.where(lo_rows, dq0, dq1).T
            return 0

        lax.fori_loop(0, nq, q_loop, 0)
        if ns:
            pl.when(p == N_PAIRS - 1)(finish)

    pair = pl.BlockSpec((s, 128), lambda p: (0, p))
    pair_t = pl.BlockSpec((128, s), lambda p: (p, 0))
    out = jax.ShapeDtypeStruct((s, 1024), F32)
    outs = pl.pallas_call(
        body, name="fox_bwd", grid=(N_PAIRS,),
        in_specs=[pair, pair, pair, pair, pair_t, pair_t, pair, pl.BlockSpec((N_HEADS, s), lambda p: (0, 0)),
                  pl.BlockSpec((s, 128), lambda p: (0, 8 + p))] + [ANY] * ns,
        out_specs=[pl.BlockSpec((s, 128), lambda p: (0, p), pipeline_mode=once)] * 5 + [ANY] * ns,
        out_shape=[out] * 5 + [jax.ShapeDtypeStruct(p.shape, p.dtype) for p in parts],
        scratch_shapes=[pltpu.VMEM((2, nq, t, t), F32), pltpu.VMEM((2, nq, t, t), F32)] + (_scatter_scratch(ns) if ns else []),
        compiler_params=_params(("arbitrary",)),
    )(qn, kn, aq, ak, knt, akt, vb, lse, dmixed, *parts)
    return (*outs[:5], _keep_own_blocks(outs[5:], parts))


def fox_post(dqn, dkn, dc0, dc1, proj, smallp, qw, kw, sel, selt, fold_even, fold_odd, *, tm=256):
    s = proj.shape[0]
    nrow = s // tm

    def body(dqn_ref, dkn_ref, dc0_ref, dc1_ref, q_ref, k_ref, small_ref, sp_ref, qw_ref, kw_ref, sel_ref, selt_ref,
             fe_ref, fo_ref, dq_ref, dk_ref, dsmall_ref, gqw_ref, gkw_ref, gfb_ref, carry):
        step = pl.program_id(0)

        @pl.when(step == 0)
        def _():
            carry[...] = jnp.zeros_like(carry)

        def norm_bwd(x_ref, w_ref, dn, out_ref):
            x = x_ref[...]
            rf = _head_rstd(x, sel_ref, selt_ref)
            xh = x * rf
            g = dn * w_ref[...]
            mean_gx = _head_spread(_head_mean(g * xh, sel_ref, selt_ref), selt_ref)
            out_ref[...] = (rf * (g - xh * mean_gx)).astype(BF16)
            return jnp.sum(dn * xh, axis=0, keepdims=True)

        gqw = norm_bwd(q_ref, qw_ref, dqn_ref[...] * FOX_SCALE, dq_ref)
        gkw = norm_bwd(k_ref, kw_ref, dkn_ref[...], dk_ref)
        li = _lane_iota((tm, 128))
        f_lane = jnp.logical_and(li >= F_LANE, li < F_LANE + N_HEADS)
        dcum = -(_split3_dot(dc0_ref[...], fe_ref[...]) + _split3_dot(dc1_ref[...], fo_ref[...]))
        dlogf = _rev_cumsum_rows(dcum) + carry[...]
        carry[...] = dlogf[0:1, :]
        dfr = jnp.where(f_lane, dlogf * _sigmoid(-(small_ref[...] + sp_ref[3:4, :])), 0.0)
        dsmall_ref[...] = dfr
        gfb = jnp.sum(dfr, axis=0, keepdims=True)

        @pl.when(step == 0)
        def _():
            gqw_ref[...] = gqw
            gkw_ref[...] = gkw
            gfb_ref[...] = gfb

        @pl.when(step > 0)
        def _():
            gqw_ref[...] += gqw
            gkw_ref[...] += gkw
            gfb_ref[...] += gfb

    def rb(i):
        return nrow - 1 - i

    row = pl.BlockSpec((tm, 1024), lambda i: (rb(i), 0))
    vec = pl.BlockSpec((1, 1024), lambda i: (0, 0))
    fold = pl.BlockSpec((1024, 128), lambda i: (0, 0))
    return pl.pallas_call(
        body, name="fox_post", grid=(nrow,),
        in_specs=[row, row, row, row, pl.BlockSpec((tm, 1024), lambda i: (rb(i), Q_COL)),
                  pl.BlockSpec((tm, 1024), lambda i: (rb(i), K_COL)),
                  pl.BlockSpec((tm, 128), lambda i: (rb(i), SMALL_BLOCK)), pl.BlockSpec((8, 128), lambda i: (0, 0)), vec, vec,
                  fold, pl.BlockSpec((128, 1024), lambda i: (0, 0)), fold, fold],
        out_specs=[row, row, pl.BlockSpec((tm, 128), lambda i: (rb(i), 0)), vec, vec, pl.BlockSpec((1, 128), lambda i: (0, 0))],
        out_shape=[jax.ShapeDtypeStruct((s, 1024), BF16), jax.ShapeDtypeStruct((s, 1024), BF16),
                   jax.ShapeDtypeStruct((s, 128), F32), jax.ShapeDtypeStruct((1, 1024), F32),
                   jax.ShapeDtypeStruct((1, 1024), F32), jax.ShapeDtypeStruct((1, 128), F32)],
        scratch_shapes=[pltpu.VMEM((1, 128), F32)], compiler_params=_params(("arbitrary",)),
    )(dqn, dkn, dc0, dc1, proj, proj, proj, smallp, qw, kw, sel, selt, fold_even, fold_odd)


def local_step(x, target, wx, later_shards, ssd_cw8, ssd_cb, smallp, ssd_nw, qw_t, kw_t, sel, selt,
               norm_mix_w, norm_ffn_w, ffn_cw8, ffn_cb):
    h, h_t = rms_fwd(x, norm_mix_w, name="rms_mix_fwd")
    proj = matmul(h, wx, mode="nn", tm=1024, tn=PROJ_TILE, tk=1024, out_dtype=F32, name="mm_in_proj")
    y_ssd, y_ssd_t, ypre, states = ssd_fwd(proj, ssd_cw8, ssd_cb, smallp, ssd_nw)
    place_q, place_k, ones_q, ones_k, fold_even, fold_odd = fox_tables()
    qn, kn, aq, ak, vb, knt, akt, vt = fox_prep(proj, smallp, qw_t, kw_t, sel, selt, place_q, place_k, ones_q, ones_k)
    y_fox, y_fox_t, lse, (a_out, a_up, a_down) = fox_fwd(qn, kn, aq, ak, vt, shards=later_shards)
    w_out = a_out.reshape(2048, D_MODEL)
    w_down = a_down.reshape(D_FF, D_MODEL)
    s = x.shape[0]
    shard = lambda index: pl.BlockSpec((None, 1024, 1408), index)
    x1, hf, hf_t = out_proj_rms_fwd(y_ssd, y_fox, w_out, x, norm_ffn_w)
    hu, act, act_t = up_ffn_fwd(hf, a_up, ffn_cw8, ffn_cb)
    dy, sq = down_proj_loss(act, w_down, x1, target)

    dact = matmul(dy, w_down, mode="nt", tm=1024, tn=1408, tk=1024, out_dtype=F32, name="mm_dact")
    g_down = matmul(act_t, dy, mode="nn", tm=1408, tn=1024, tk=1024, out_dtype=BF16, name="mm_dw_down")
    dhu, gcw_g, gcw_v = ffn_mid_bwd(hu, dact, ffn_cw8, ffn_cb)
    dhf = matmul(dhu, a_up, mode="nt", tm=1024, tn=1024, tk=1408, out_dtype=F32, name="mm_dhf",
                 layout=dict(m=s, n=D_MODEL, k=2 * D_FF, a_spec=shard(lambda i, j, kk: (kk // 2, i, kk % 2)),
                             b_spec=shard(lambda i, j, kk: (kk, 0, 0))))
    g_up = matmul(hf_t, dhu, mode="nn", tm=1024, tn=1408, tk=1024, out_dtype=BF16, name="mm_dw_up",
                  layout=dict(m=D_MODEL, n=2 * D_FF, k=s, b_spec=shard(lambda i, j, kk: (j // 2, kk, j % 2)),
                              o_spec=shard(lambda i, j, kk: (j, i, 0)), out_shape=(4, D_MODEL, 1408)))
    dx1, g_norm_ffn, dmixed = rms_bwd_matmul(dhf, x1, norm_ffn_w, dy, w_out, name="rms_ffn_bwd_dmixed")
    g_out_a = matmul(y_ssd_t, dx1, mode="nn", tm=1024, tn=1024, tk=1024, out_dtype=BF16, name="mm_dw_out_ssd")
    g_out_b = matmul(y_fox_t, dx1, mode="nn", tm=1024, tn=1024, tk=1024, out_dtype=BF16, name="mm_dw_out_fox")
    early = [jnp.concatenate([g_out_a, g_out_b], axis=0).reshape(4, 512, D_MODEL), g_up, g_down.reshape(4, 704, D_MODEL)]
    dz, dxs, db, dc, dsmall_ssd, gcw_x, gcw_b, gcw_c, g_sp, g_ssd_nw, theirs = ssd_bwd(
        proj, ssd_cw8, ssd_cb, smallp, ssd_nw, ypre, states, dmixed, sel, swap=early)
    core = lax.axis_index("c").astype(jnp.int32).reshape(1)
    parts = [add_pair(a, b, core, name="add_pair_" + n, tr=ADAM_ROWS[n]) for a, b, n in zip(early, theirs, BIG_NAMES[1:])]
    dqn, dkn, dv, dc0, dc1, landed_early = fox_bwd(qn, kn, aq, ak, knt, akt, vb, lse, dmixed, parts=parts)
    dq, dk, dsmall_fox, g_qw, g_kw, g_fb = fox_post(dqn, dkn, dc0, dc1, proj, smallp, qw_t, kw_t, sel, selt,
                                                    fold_even, fold_odd)
    dproj = jnp.concatenate([dz, dxs, dq, dk, dv.astype(BF16), db, dc, (dsmall_ssd + dsmall_fox).astype(BF16)], axis=1)
    g_wx = matmul(h_t, dproj, mode="nn", tm=1024, tn=PROJ_TILE, tk=1024, out_dtype=BF16, name="mm_dw_in")
    g_in = _in_grad_shards(g_wx)
    part_in = add_pair(g_in, pair_swap_halves([g_in], name="pair_swap_w_in")[0], core, name="add_pair_w_in",
                       tr=ADAM_ROWS["w_in"])
    grad_x, g_norm_mix, landed_in = matmul_rms_bwd(dproj, wx, x, norm_mix_w, dx1, scatter=[part_in])
    return dict(
        sq=sq, grad_x=grad_x, landed=landed_in + landed_early,
        g_norm_mix=g_norm_mix, g_norm_ffn=g_norm_ffn, g_ssd_nw=g_ssd_nw,
        g_ssd_cw=jnp.concatenate([gcw_x, gcw_b, gcw_c], axis=1), g_sp=g_sp, g_fb=g_fb, g_qw=g_qw, g_kw=g_kw,
        g_ffn_cw=jnp.concatenate([gcw_g, gcw_v], axis=1))


def adamw(w, g, m, v, *, name, tr, allreduce=None):
    rows, cols = w.shape
    nsteps = rows // tr

    def body(*refs):
        if allreduce is None:
            w_ref, g_ref, m_ref, v_ref, d_ref, mo_ref, vo_ref = refs
        else:
            w_ref, g_ref, m_ref, v_ref, packed_ref, d_ref, mo_ref, vo_ref, summed_ref = refs[:9]
            start, finish = _allreduce_phases(packed_ref, summed_ref, *refs[9:])
            pl.when(pl.program_id(0) == 0)(start)
        gv = g_ref[...]
        mn = ADAM_B1 * m_ref[...] + (1.0 - ADAM_B1) * gv
        vn = ADAM_B2 * v_ref[...] + (1.0 - ADAM_B2) * (gv * gv)
        m_hat = mn / (1.0 - ADAM_B1 ** ADAM_STEP)
        v_hat = vn / (1.0 - ADAM_B2 ** ADAM_STEP)
        d_ref[...] = -ADAM_LR * (m_hat / (jnp.sqrt(v_hat) + ADAM_EPS) + ADAM_WD * w_ref[...])
        mo_ref[...] = mn
        vo_ref[...] = vn
        if allreduce is not None:
            pl.when(pl.program_id(0) == nsteps - 1)(finish)

    blk = pl.BlockSpec((tr, cols), lambda i: (i, 0))
    shp = jax.ShapeDtypeStruct((rows, cols), F32)
    if allreduce is None:
        return pl.pallas_call(
            body, name=name, grid=(nsteps,), in_specs=[blk] * 4, out_specs=[blk] * 3, out_shape=[shp] * 3,
            compiler_params=_params(("parallel",)),
        )(w, g, m, v)
    whole = pl.BlockSpec(memory_space=pltpu.VMEM)
    return pl.pallas_call(
        body, name=name, grid=(nsteps,), in_specs=[blk] * 4 + [whole], out_specs=[blk] * 3 + [whole],
        out_shape=[shp] * 3 + [jax.ShapeDtypeStruct(allreduce.shape, F32)],
        scratch_shapes=_allreduce_scratch(allreduce.shape[0]), compiler_params=_params(("arbitrary",)),
    )(w, g, m, v, allreduce)


def add_pair(full, theirs, core, *, name, tr):
    _, rows, cols = theirs.shape
    nblk = rows // tr

    def body(c_ref, a_ref, b_ref, o_ref):
        o_ref[...] = (a_ref[...].astype(F32) + b_ref[...].astype(F32)).astype(BF16)

    blk = pl.BlockSpec((1, tr, cols), lambda j, i, c: (j, i, 0))
    grid_spec = pltpu.PrefetchScalarGridSpec(
        num_scalar_prefetch=1, grid=(4, nblk),
        in_specs=[pl.BlockSpec((1, tr, cols), lambda j, i, c: (j, c[0] * nblk + i, 0)), blk], out_specs=blk)
    return pl.pallas_call(
        body, name=name, grid_spec=grid_spec, out_shape=jax.ShapeDtypeStruct(theirs.shape, BF16),
        compiler_params=_params(("parallel", "parallel")),
    )(core, full, theirs)


def sum_chips(parts, core, *, name, tr):
    _, rows, cols = parts.shape
    nblk = rows // tr

    def body(c_ref, p_ref, o_ref):
        acc = p_ref[0].astype(F32)
        for k in range(1, 4):
            acc = acc + p_ref[k].astype(F32)
        o_ref[...] = acc

    grid_spec = pltpu.PrefetchScalarGridSpec(
        num_scalar_prefetch=1, grid=(nblk,), in_specs=[pl.BlockSpec((4, tr, cols), lambda i, c: (0, i, 0))],
        out_specs=pl.BlockSpec((tr, cols), lambda i, c: (c[0] * nblk + i, 0)))
    return pl.pallas_call(
        body, name=name, grid_spec=grid_spec, out_shape=jax.ShapeDtypeStruct((2 * rows, cols), F32),
        compiler_params=_params(("parallel",)),
    )(core, parts)


ANY = pl.BlockSpec(memory_space=pl.ANY)


def _place():
    x, y, c = lax.axis_index("x"), lax.axis_index("y"), lax.axis_index("c")
    chips = [(1 - x, y), (x, 1 - y), (1 - x, 1 - y)]
    return x, y, c, chips


def _chunks(rows):
    size = next((c for c in (128, 176, 64, 32, 16, 8) if rows % c == 0), rows)
    return [(r, size) for r in range(0, rows, size)]


def gather_weights(shards):
    n = len(shards)

    def body(*refs):
        start, forward, finish = _gather_phases(refs[:n], refs[n:2 * n], *refs[2 * n:])
        start()
        forward()
        finish()

    gathered = pl.pallas_call(
        body, name="gather_weights", in_specs=[ANY] * n, out_specs=[ANY] * n,
        out_shape=_gather_out_shapes(shards), scratch_shapes=_gather_scratch(n),
    )(*shards)
    return gathered


def _gather_out_shapes(shards):
    return [jax.ShapeDtypeStruct((4,) + s.shape, s.dtype) for s in shards]


def _gather_scratch(n):
    return [pltpu.SemaphoreType.DMA((n, 7)), pltpu.SemaphoreType.DMA((n, 7))]


def _gather_phases(ins, outs, send_sems, recv_sems):
    n = len(ins)
    x, y, c, chips = _place()
    me = 2 * x + y
    sibling = (x, y, 1 - c)
    blks = [2 * cx + cy for cx, cy in chips]

    def half(a, blk, r=0, nr=None):
        rows = ins[a].shape[0] // 2
        return outs[a].at[blk, pl.ds(c * rows + r, rows if nr is None else nr), :]

    def to_chip(a, t, r=0, nr=None):
        rows = ins[a].shape[0] // 2
        return pltpu.make_async_remote_copy(
            src_ref=ins[a].at[pl.ds(c * rows + r, rows if nr is None else nr), :], dst_ref=half(a, me, r, nr),
            send_sem=send_sems.at[a, t], recv_sem=recv_sems.at[a, t], device_id=(*chips[t], c), device_id_type=MESH)

    def from_chip(a, t):
        return pltpu.make_async_remote_copy(
            src_ref=half(a, blks[t]), dst_ref=half(a, blks[t]), send_sem=send_sems.at[a, t], recv_sem=recv_sems.at[a, t],
            device_id=(*chips[t], c), device_id_type=MESH)

    def to_sibling(a, t, r=0, nr=None):
        return pltpu.make_async_remote_copy(
            src_ref=half(a, blks[t], r, nr), dst_ref=half(a, blks[t], r, nr), send_sem=send_sems.at[a, 3 + t],
            recv_sem=recv_sems.at[a, 3 + t], device_id=sibling, device_id_type=MESH)

    def from_sibling(a, t):
        rows = ins[a].shape[0] // 2
        dst = outs[a].at[blks[t], pl.ds((1 - c) * rows, rows), :]
        return pltpu.make_async_remote_copy(
            src_ref=dst, dst_ref=dst, send_sem=send_sems.at[a, 3 + t], recv_sem=recv_sems.at[a, 3 + t],
            device_id=sibling, device_id_type=MESH)

    def own(a, r=0, nr=None):
        return pltpu.make_async_remote_copy(
            src_ref=ins[a].at[pl.ds(r, ins[a].shape[0] if nr is None else nr), :],
            dst_ref=outs[a].at[me, pl.ds(r, ins[a].shape[0] if nr is None else nr), :],
            send_sem=send_sems.at[a, 6], recv_sem=recv_sems.at[a, 6], device_id=sibling, device_id_type=MESH)

    def start():
        for a in range(n):
            for t in range(3):
                for r, nr in _chunks(ins[a].shape[0] // 2):
                    to_chip(a, t, r, nr).start()
            for r, nr in _chunks(ins[a].shape[0]):
                own(a, r, nr).start()

    def forward():
        for a in range(n):
            for t in range(3):
                from_chip(a, t).wait_recv()
                for r, nr in _chunks(ins[a].shape[0] // 2):
                    to_sibling(a, t, r, nr).start()

    def finish():
        for a in range(n):
            for t in range(3):
                from_sibling(a, t).wait_recv()
        for a in range(n):
            for t in range(3):
                to_chip(a, t).wait_send()
                to_sibling(a, t).wait_send()
            own(a).wait()

    return start, forward, finish


def pair_swap_halves(grads, *, name):
    n = len(grads)

    def body(*refs):
        start, finish = _pair_swap_phases(refs[:n], refs[n:2 * n], *refs[2 * n:])
        start()
        finish()

    return pl.pallas_call(
        body, name=name, in_specs=[ANY] * n, out_specs=[ANY] * n, out_shape=_pair_swap_out_shapes(grads),
        scratch_shapes=_pair_swap_scratch(n),
    )(*grads)


def _pair_swap_out_shapes(grads):
    return [jax.ShapeDtypeStruct((4, g.shape[1] // 2, g.shape[2]), g.dtype) for g in grads]


def _pair_swap_scratch(n):
    return [pltpu.SemaphoreType.DMA((n,)), pltpu.SemaphoreType.DMA((n,))]


def _pair_swap_phases(ins, theirs, send_sems, recv_sems):
    n = len(ins)
    x, y, c, _ = _place()
    sibling = (x, y, 1 - c)

    def start():
        for a in range(n):
            rows = ins[a].shape[1] // 2
            for j in range(4):
                for r, nr in _chunks(rows):
                    pltpu.make_async_remote_copy(
                        src_ref=ins[a].at[j, pl.ds((1 - c) * rows + r, nr), :], dst_ref=theirs[a].at[j, pl.ds(r, nr), :],
                        send_sem=send_sems.at[a], recv_sem=recv_sems.at[a], device_id=sibling, device_id_type=MESH).start()

    def finish():
        for a in range(n):
            pltpu.make_async_remote_copy(src_ref=theirs[a], dst_ref=theirs[a], send_sem=send_sems.at[a],
                                         recv_sem=recv_sems.at[a], device_id=sibling, device_id_type=MESH).wait()

    return start, finish


def _scatter_scratch(n):
    return [pltpu.SemaphoreType.DMA((n, 3)), pltpu.SemaphoreType.DMA((n, 3))]


def _keep_own_blocks(landed, parts):
    if not parts:
        return []
    chip = 2 * lax.axis_index("x") + lax.axis_index("y")
    return [lax.dynamic_update_slice(l, lax.dynamic_slice_in_dim(p, chip, 1, axis=0), (chip, 0, 0))
            for l, p in zip(landed, parts)]


def _scatter_phases(ins, outs, send_sems, recv_sems):
    n = len(ins)
    x, y, c, chips = _place()
    me = 2 * x + y
    blks = [2 * cx + cy for cx, cy in chips]

    def start():
        for a in range(n):
            for r, nr in _chunks(ins[a].shape[1]):
                for t in range(3):
                    pltpu.make_async_remote_copy(
                        src_ref=ins[a].at[blks[t], pl.ds(r, nr), :], dst_ref=outs[a].at[me, pl.ds(r, nr), :],
                        send_sem=send_sems.at[a, t], recv_sem=recv_sems.at[a, t],
                        device_id=(*chips[t], c), device_id_type=MESH).start()

    def finish():
        for a in range(n):
            for t in range(3):
                pltpu.make_async_remote_copy(
                    src_ref=outs[a].at[blks[t]], dst_ref=outs[a].at[blks[t]], send_sem=send_sems.at[a, t],
                    recv_sem=recv_sems.at[a, t], device_id=(*chips[t], c), device_id_type=MESH).wait()

    return start, finish


def pair_join_halves(bufs):
    n = len(bufs)

    def body(*refs):
        outs = refs[n:2 * n]
        send_sems, recv_sems = refs[2 * n:]
        x, y, c, _ = _place()
        sibling = (x, y, 1 - c)
        for a in range(n):
            rows = outs[a].shape[0] // 2
            for r, nr in _chunks(rows):
                mine = outs[a].at[pl.ds(c * rows + r, nr), :]
                pltpu.make_async_remote_copy(src_ref=mine, dst_ref=mine, send_sem=send_sems.at[a], recv_sem=recv_sems.at[a],
                                             device_id=sibling, device_id_type=MESH).start()
        for a in range(n):
            rows = outs[a].shape[0] // 2
            pltpu.make_async_remote_copy(
                src_ref=outs[a].at[pl.ds(c * rows, rows), :], dst_ref=outs[a].at[pl.ds((1 - c) * rows, rows), :],
                send_sem=send_sems.at[a], recv_sem=recv_sems.at[a], device_id=sibling, device_id_type=MESH).wait()

    return pl.pallas_call(
        body, name="pair_join_halves", in_specs=[ANY] * n, out_specs=[ANY] * n,
        out_shape=[jax.ShapeDtypeStruct(b.shape, b.dtype) for b in bufs], input_output_aliases={a: a for a in range(n)},
        scratch_shapes=[pltpu.SemaphoreType.DMA((n,)), pltpu.SemaphoreType.DMA((n,))],
    )(*bufs)


def _allreduce_scratch(rows):
    return [pltpu.VMEM((8, rows, 128), F32), pltpu.SemaphoreType.DMA((7,)), pltpu.SemaphoreType.DMA((7,))]


def _allreduce_phases(in_ref, out_ref, gathered, send_sems, recv_sems):
    x, y, c, _ = _place()
    me = 4 * x + 2 * y + c
    flips = [(fx, fy, fc) for fx in (0, 1) for fy in (0, 1) for fc in (0, 1)][1:]
    peers = [((1 - x) if fx else x, (1 - y) if fy else y, (1 - c) if fc else c) for fx, fy, fc in flips]

    def send(t):
        return pltpu.make_async_remote_copy(
            src_ref=in_ref, dst_ref=gathered.at[me], send_sem=send_sems.at[t], recv_sem=recv_sems.at[t],
            device_id=peers[t], device_id_type=MESH)

    def start():
        gathered[me] = in_ref[...]
        for t in range(7):
            send(t).start()

    def finish():
        for t, (px, py, pc) in enumerate(peers):
            slot = gathered.at[4 * px + 2 * py + pc]
            pltpu.make_async_remote_copy(
                src_ref=slot, dst_ref=slot, send_sem=send_sems.at[t], recv_sem=recv_sems.at[t],
                device_id=(px, py, pc), device_id_type=MESH).wait_recv()
        for t in range(7):
            send(t).wait_send()
        acc = gathered[0]
        for k in range(1, 8):
            acc = acc + gathered[k]
        out_ref[...] = acc

    return start, finish


SMALL_NAMES = ("norm_mix_w", "ssd_conv_w", "ssd_conv_b", "ssd_dt_bias", "ssd_a_log", "ssd_d", "ssd_norm_w", "fox_f_bias",
               "fox_q_norm_w", "fox_k_norm_w", "norm_ffn_w", "ffn_conv_w", "ffn_conv_b")
BIG_NAMES = ("w_in", "w_out", "w_up", "w_down")
WEIGHT_ORDER = ("norm_mix_w", "w_in", "ssd_conv_w", "ssd_conv_b", "ssd_dt_bias", "ssd_a_log", "ssd_d", "ssd_norm_w",
                "fox_f_bias", "fox_q_norm_w", "fox_k_norm_w", "w_out", "norm_ffn_w", "w_up", "ffn_conv_w", "ffn_conv_b", "w_down")
ADAM_ROWS = {"w_in": 256, "w_out": 256, "w_up": 256, "w_down": 176}


def _pack(arrays):
    pieces = []
    for a in arrays:
        flat = a.reshape(-1).astype(F32)
        pieces += [flat, jnp.zeros(((-flat.shape[0]) % 1024,), F32)]
    return jnp.concatenate(pieces).reshape(-1, 128)


def _unpack(packed, shapes):
    out, r = [], 0
    for shp in shapes:
        size = 1
        for d in shp:
            size *= d
        nrow = 8 * (-(-size // 1024))
        out.append(packed[r:r + nrow].reshape(-1)[:size].reshape(shp))
        r += nrow
    return out


IN_SHARD = IN_COLS // 4
IN_SEGMENTS = ((0, 2048, 0), (2048, 2560, 5120), (2560, 2576, MAIN_COLS), (2576, 5648, 2048), (5648, 5664, MAIN_COLS + F_LANE))


def _in_cols(shards, lo, hi):
    out = []
    for j in range(4):
        a, b = max(lo, IN_SHARD * j), min(hi, IN_SHARD * (j + 1))
        if a < b:
            out.append(shards[j][:, a - IN_SHARD * j:b - IN_SHARD * j])
    return out


def _in_grad_shards(g):
    shards = []
    for j in range(4):
        pieces = []
        for lo, hi, at in IN_SEGMENTS:
            a, b = max(lo, IN_SHARD * j), min(hi, IN_SHARD * (j + 1))
            if a < b:
                pieces.append(g[:, at + a - lo:at + b - lo])
        shards.append(jnp.concatenate(pieces, axis=1))
    return jnp.stack(shards)


def _pad_rows(a, rows):
    return jnp.pad(a, ((0, rows - a.shape[0]), (0, 0)))


def kernel(x, norm_mix_w, w_in, ssd_conv_w, ssd_conv_b, ssd_dt_bias, ssd_a_log, ssd_d, ssd_norm_w, fox_f_bias, fox_q_norm_w, fox_k_norm_w, w_out, norm_ffn_w, w_up, ffn_conv_w, ffn_conv_b, w_down, loss_target, m_norm_mix_w, m_w_in, m_ssd_conv_w, m_ssd_conv_b, m_ssd_dt_bias, m_ssd_a_log, m_ssd_d, m_ssd_norm_w, m_fox_f_bias, m_fox_q_norm_w, m_fox_k_norm_w, m_w_out, m_norm_ffn_w, m_w_up, m_ffn_conv_w, m_ffn_conv_b, m_w_down, v_norm_mix_w, v_w_in, v_ssd_conv_w, v_ssd_conv_b, v_ssd_dt_bias, v_ssd_a_log, v_ssd_d, v_ssd_norm_w, v_fox_f_bias, v_fox_q_norm_w, v_fox_k_norm_w, v_w_out, v_norm_ffn_w, v_w_up, v_ffn_conv_w, v_ffn_conv_b, v_w_down):
    w = dict(norm_mix_w=norm_mix_w, w_in=w_in, ssd_conv_w=ssd_conv_w, ssd_conv_b=ssd_conv_b, ssd_dt_bias=ssd_dt_bias,
             ssd_a_log=ssd_a_log, ssd_d=ssd_d, ssd_norm_w=ssd_norm_w, fox_f_bias=fox_f_bias, fox_q_norm_w=fox_q_norm_w,
             fox_k_norm_w=fox_k_norm_w, w_out=w_out, norm_ffn_w=norm_ffn_w, w_up=w_up, ffn_conv_w=ffn_conv_w,
             ffn_conv_b=ffn_conv_b, w_down=w_down)
    m = dict(norm_mix_w=m_norm_mix_w, w_in=m_w_in, ssd_conv_w=m_ssd_conv_w, ssd_conv_b=m_ssd_conv_b, ssd_dt_bias=m_ssd_dt_bias,
             ssd_a_log=m_ssd_a_log, ssd_d=m_ssd_d, ssd_norm_w=m_ssd_norm_w, fox_f_bias=m_fox_f_bias, fox_q_norm_w=m_fox_q_norm_w,
             fox_k_norm_w=m_fox_k_norm_w, w_out=m_w_out, norm_ffn_w=m_norm_ffn_w, w_up=m_w_up, ffn_conv_w=m_ffn_conv_w,
             ffn_conv_b=m_ffn_conv_b, w_down=m_w_down)
    v = dict(norm_mix_w=v_norm_mix_w, w_in=v_w_in, ssd_conv_w=v_ssd_conv_w, ssd_conv_b=v_ssd_conv_b, ssd_dt_bias=v_ssd_dt_bias,
             ssd_a_log=v_ssd_a_log, ssd_d=v_ssd_d, ssd_norm_w=v_ssd_norm_w, fox_f_bias=v_fox_f_bias, fox_q_norm_w=v_fox_q_norm_w,
             fox_k_norm_w=v_fox_k_norm_w, w_out=v_w_out, norm_ffn_w=v_norm_ffn_w, w_up=v_w_up, ffn_conv_w=v_ffn_conv_w,
             ffn_conv_b=v_ffn_conv_b, w_down=v_w_down)
    chip = 2 * lax.axis_index("x") + lax.axis_index("y")

    a_in, a_scw, a_fcw = gather_weights([w_in[0].astype(BF16), _pad_rows(ssd_conv_w[0], 16), _pad_rows(ffn_conv_w[0], 16)])
    later_shards = [w_out[0].astype(BF16), w_up[0].astype(BF16), w_down[0].astype(BF16)]
    wx = jnp.concatenate([p for lo, hi, _ in sorted(IN_SEGMENTS, key=lambda seg: seg[2]) for p in _in_cols(a_in, lo, hi)]
                         + [jnp.zeros((D_MODEL, PROJ_COLS - IN_COLS), BF16)], axis=1)
    ssd_cw8 = a_scw.transpose(1, 0, 2).reshape(16, 1536)[:8]
    ffn_cw8 = a_fcw.transpose(1, 0, 2).reshape(16, 2 * D_FF)[:8]
    gap = lambda n: jnp.zeros((n,), F32)
    smallp = jnp.concatenate([ssd_dt_bias[0], gap(112), ssd_a_log[0], gap(112), ssd_d[0], gap(112),
                              gap(F_LANE), fox_f_bias[0], gap(128 - F_LANE - N_HEADS), gap(4 * 128)]).reshape(8, 128)
    qw_t = jnp.tile(fox_q_norm_w[0], N_HEADS)[None]
    kw_t = jnp.tile(fox_k_norm_w[0], N_HEADS)[None]
    sel = jnp.asarray((np.arange(1024)[:, None] // HEAD_DIM == np.arange(128)[None, :]).astype(np.float32), BF16)

    res = local_step(x[0], loss_target[0], wx, later_shards, ssd_cw8, ssd_conv_b, smallp, ssd_norm_w, qw_t, kw_t,
                     sel, sel.T, norm_mix_w, norm_ffn_w, ffn_cw8, ffn_conv_b)

    full_shapes = [(1, 1024), (1, 4, 1536), (1, 1536), (1, 16), (1, 16), (1, 16), (1, 1024), (1, 16), (1, 64), (1, 64),
                   (1, 1024), (1, 3, 2 * D_FF), (1, 2 * D_FF), (1,)]
    local_small = [res["g_norm_mix"], res["g_ssd_cw"][:4], res["g_ssd_cw"][4], res["g_sp"][0, :16], res["g_sp"][1, :16],
                   res["g_sp"][2, :16], res["g_ssd_nw"], res["g_fb"][0, F_LANE:F_LANE + 16],
                   res["g_qw"].reshape(N_HEADS, HEAD_DIM).sum(0), res["g_kw"].reshape(N_HEADS, HEAD_DIM).sum(0),
                   res["g_norm_ffn"], res["g_ffn_cw"][:3], res["g_ffn_cw"][3], jnp.sum(res["sq"])]
    landed = res["landed"]
    core = lax.axis_index("c").astype(jnp.int32).reshape(1)
    halves = [sum_chips(p, core, name="sum_chips_" + n, tr=ADAM_ROWS[n]) for p, n in zip(landed, BIG_NAMES)]
    g_big = dict(zip(BIG_NAMES, pair_join_halves(halves)))

    grads, deltas, new_m, new_v = {}, {}, {}, {}
    for n in BIG_NAMES:
        out = adamw(w[n][0], g_big[n], m[n][0], v[n][0], name="adamw_" + n, tr=ADAM_ROWS[n],
                    allreduce=_pack(local_small) if n == BIG_NAMES[0] else None)
        if n == BIG_NAMES[0]:
            summed = _unpack(out[3], full_shapes)
        d, mn, vn = out[:3]
        grads[n], deltas[n], new_m[n], new_v[n] = g_big[n][None], d[None], mn[None], vn[None]
    loss = (0.5 / D_MODEL) * summed[-1][0]
    g_small = dict(zip(SMALL_NAMES, summed[:-1]))
    g_small["ssd_conv_w"] = lax.dynamic_slice(g_small["ssd_conv_w"], (0, 0, 384 * chip), (1, 4, 384))
    g_small["ffn_conv_w"] = lax.dynamic_slice(g_small["ffn_conv_w"], (0, 0, 1408 * chip), (1, 3, 1408))
    shapes = [w[n].shape for n in SMALL_NAMES]
    packed_w = _pack([w[n] for n in SMALL_NAMES])
    d, mn, vn = adamw(packed_w, _pack([g_small[n] for n in SMALL_NAMES]), _pack([m[n] for n in SMALL_NAMES]),
                      _pack([v[n] for n in SMALL_NAMES]), name="adamw_small", tr=packed_w.shape[0])
    for n, dd, mm, vv in zip(SMALL_NAMES, _unpack(d, shapes), _unpack(mn, shapes), _unpack(vn, shapes)):
        grads[n], deltas[n], new_m[n], new_v[n] = g_small[n].reshape(w[n].shape), dd, mm, vv
    return (loss, res["grad_x"][None], *[grads[n] for n in WEIGHT_ORDER], *[deltas[n] for n in WEIGHT_ORDER],
            *[new_m[n] for n in WEIGHT_ORDER], *[new_v[n] for n in WEIGHT_ORDER])
```

```python
import functools

import jax
import jax.numpy as jnp
import numpy as np
from jax import lax
from jax.experimental import pallas as pl
from jax.experimental.pallas import tpu as pltpu

F32 = jnp.float32
BF16 = jnp.bfloat16
MESH = pl.DeviceIdType.MESH

D_MODEL = 1024
HEAD_DIM = 64
N_HEADS = 16
N_PAIRS = N_HEADS // 2
SSD_CHUNK = 128
SSD_STATE = 128
SSD_CONV = 4
D_FF = 2816
FFN_CONV = 3
NORM_EPS = 1e-6
MAIN_COLS = 5632
SMALL_COLS = 128
PROJ_COLS = MAIN_COLS + SMALL_COLS
SMALL_BLOCK = MAIN_COLS // SMALL_COLS
PROJ_TILE = 1152
F_LANE = 16
IN_COLS = 5664

ADAM_LR = 0.001
ADAM_B1 = 0.9
ADAM_B2 = 0.999
ADAM_EPS = 1e-08
ADAM_WD = 0.01
ADAM_STEP = 10

VMEM_LIMIT_V7X = 56 * 1024 * 1024
NEG_BIG = -1e30


def _params(sem=None):
    return pltpu.CompilerParams(dimension_semantics=sem, vmem_limit_bytes=VMEM_LIMIT_V7X)


def _sigmoid(x):
    return 1.0 / (1.0 + jnp.exp(-x))


def _silu_and_grad(x):
    s = _sigmoid(x)
    return x * s, s * (1.0 + x * (1.0 - s))


def _shift_down(v, j):
    return v if j == 0 else pltpu.roll(v, j, 0)


def _shift_up(v, j):
    return v if j == 0 else pltpu.roll(v, v.shape[0] - j, 0)


def _row_iota(shape):
    return lax.broadcasted_iota(jnp.int32, shape, 0)


def _lane_iota(shape):
    return lax.broadcasted_iota(jnp.int32, shape, 1)


def _dot(a, b, mode="nn"):
    dims = {"nn": (((1,), (0,)), ((), ())), "nt": (((1,), (1,)), ((), ())), "tn": (((0,), (0,)), ((), ()))}[mode]
    return lax.dot_general(a.astype(BF16), b.astype(BF16), dims, preferred_element_type=F32)


def _dot_f32(a, b):
    return jnp.dot(a, b, precision=lax.Precision.HIGHEST, preferred_element_type=F32)


def matmul(a, b, *, mode, tm, tn, tk, out_dtype, name, layout=None):
    layout = layout or {}
    if layout:
        m, n, k = layout["m"], layout["n"], layout["k"]
    else:
        (m, k), n = a.shape, (b.shape[1] if mode == "nn" else b.shape[0])
    assert m % tm == 0 and n % tn == 0 and k % tk == 0, (name, m, n, k, tm, tn, tk)
    nk = k // tk
    a_spec = layout.get("a_spec") or pl.BlockSpec((tm, tk), lambda i, j, kk: (i, kk))
    b_spec = layout.get("b_spec") or (pl.BlockSpec((tn, tk), lambda i, j, kk: (j, kk)) if mode == "nt"
                                      else pl.BlockSpec((tk, tn), lambda i, j, kk: (kk, j)))
    o_spec = layout.get("o_spec") or pl.BlockSpec((tm, tn), lambda i, j, kk: (i, j))

    def body(a_ref, b_ref, o_ref, acc_ref):
        kk = pl.program_id(2)
        part = _dot(a_ref[...], b_ref[...], mode)
        if nk == 1:
            o_ref[...] = part.astype(out_dtype)
        else:
            @pl.when(kk == 0)
            def _():
                acc_ref[...] = part

            @pl.when(jnp.logical_and(kk > 0, kk < nk - 1))
            def _():
                acc_ref[...] += part

            @pl.when(kk == nk - 1)
            def _():
                o_ref[...] = (acc_ref[...] + part).astype(out_dtype)

    return pl.pallas_call(
        body, name=name, grid=(m // tm, n // tn, nk), in_specs=[a_spec, b_spec], out_specs=o_spec,
        out_shape=jax.ShapeDtypeStruct(layout.get("out_shape", (m, n)), out_dtype),
        scratch_shapes=[pltpu.VMEM((tm, tn) if nk > 1 else (8, 128), F32)],
        compiler_params=_params(("parallel", "parallel", "arbitrary")),
    )(a, b)


def rms_in_proj(x, w, wx, *, tm=512):
    s, d = x.shape

    def body(x_ref, w_ref, wx_ref, proj_ref, ht_ref):
        for r in range(0, tm, UP_ROWS):
            rows = slice(r, r + UP_ROWS)
            xv = x_ref[rows, :]
            rstd = lax.rsqrt(jnp.mean(xv * xv, axis=-1, keepdims=True) + NORM_EPS)
            h = (xv * rstd) * w_ref[...]
            ht_ref[:, rows] = h.T.astype(BF16)
            proj_ref[rows, :] = _dot(h, wx_ref[...])

    return pl.pallas_call(
        body, name="rms_in_proj", grid=(s // tm,),
        in_specs=[pl.BlockSpec((tm, d), lambda i: (i, 0)), pl.BlockSpec((1, d), lambda i: (0, 0)),
                  pl.BlockSpec((d, PROJ_COLS), lambda i: (0, 0), pipeline_mode=pl.Buffered(1))],
        out_specs=[pl.BlockSpec((tm, PROJ_COLS), lambda i: (i, 0)), pl.BlockSpec((d, tm), lambda i: (0, i))],
        out_shape=[jax.ShapeDtypeStruct((s, PROJ_COLS), F32), jax.ShapeDtypeStruct((d, s), BF16)],
        compiler_params=_params(("parallel",)),
    )(x, w, wx)


def matmul_rms_bwd(dproj, wx, x, w, resid, *, scatter, tm=512):
    s, d = x.shape
    ns = len(scatter)
    nsteps = s // tm

    def body(*refs):
        a_ref, b_ref, x_ref, w_ref, res_ref = refs[:5]
        dx_ref, dw_ref = refs[5 + ns:7 + ns]
        step = pl.program_id(0)
        start, finish = _scatter_phases(refs[5:5 + ns], refs[7 + ns:7 + 2 * ns], *refs[7 + 2 * ns:])
        pl.when(step == 0)(start)
        part = jnp.zeros((1, d), F32)
        for r in range(0, tm, UP_ROWS):
            rows = slice(r, r + UP_ROWS)
            dhv = _dot(a_ref[rows, :], b_ref[...], "nt")
            xv = x_ref[rows, :]
            rstd = lax.rsqrt(jnp.mean(xv * xv, axis=-1, keepdims=True) + NORM_EPS)
            xh = xv * rstd
            g = dhv * w_ref[...]
            dx_ref[rows, :] = res_ref[rows, :] + rstd * (g - xh * jnp.mean(g * xh, axis=-1, keepdims=True))
            part = part + jnp.sum(dhv * xh, axis=0, keepdims=True)

        @pl.when(step == 0)
        def _():
            dw_ref[...] = part

        @pl.when(step > 0)
        def _():
            dw_ref[...] += part

        pl.when(step == nsteps - 1)(finish)

    row = pl.BlockSpec((tm, d), lambda i: (i, 0))
    vec = pl.BlockSpec((1, d), lambda i: (0, 0))
    outs = pl.pallas_call(
        body, name="mm_dh_rms_mix_bwd", grid=(nsteps,),
        in_specs=[pl.BlockSpec((tm, PROJ_COLS), lambda i: (i, 0)),
                  pl.BlockSpec((d, PROJ_COLS), lambda i: (0, 0), pipeline_mode=pl.Buffered(1)), row, vec, row] + [ANY] * ns,
        out_specs=[row, vec] + [ANY] * ns,
        out_shape=[jax.ShapeDtypeStruct((s, d), F32), jax.ShapeDtypeStruct((1, d), F32)]
        + [jax.ShapeDtypeStruct(p.shape, p.dtype) for p in scatter],
        scratch_shapes=_scatter_scratch(ns), compiler_params=_params(("arbitrary",)),
    )(dproj, wx, x, w, resid, *scatter)
    return outs[0], outs[1], _keep_own_blocks(outs[2:], scatter)


def out_proj_rms_fwd(y_ssd, y_fox, w_out, x, norm_w, *, tm=512):
    s, d = x.shape

    def body(ys_ref, yf_ref, w_ref, x_ref, nw_ref, x1_ref, h_ref, ht_ref):
        for r in range(0, tm, UP_ROWS):
            rows = slice(r, r + UP_ROWS)
            x1 = x_ref[rows, :] + _dot(ys_ref[rows, :], w_ref[0:d, :]) + _dot(yf_ref[rows, :], w_ref[d:2 * d, :])
            x1_ref[rows, :] = x1
            rstd = lax.rsqrt(jnp.mean(x1 * x1, axis=-1, keepdims=True) + NORM_EPS)
            h = (x1 * rstd) * nw_ref[...]
            h_ref[rows, :] = h.astype(BF16)
            ht_ref[:, rows] = h.T.astype(BF16)

    row = pl.BlockSpec((tm, d), lambda i: (i, 0))
    return pl.pallas_call(
        body, name="out_proj_rms_fwd", grid=(s // tm,),
        in_specs=[row, row, pl.BlockSpec((2 * d, d), lambda i: (0, 0)), row, pl.BlockSpec((1, d), lambda i: (0, 0))],
        out_specs=[row, row, pl.BlockSpec((d, tm), lambda i: (0, i))],
        out_shape=[jax.ShapeDtypeStruct((s, d), F32), jax.ShapeDtypeStruct((s, d), BF16), jax.ShapeDtypeStruct((d, s), BF16)],
        compiler_params=_params(("parallel",)),
    )(y_ssd, y_fox, w_out, x, norm_w)


def rms_bwd_matmul(dh, x, w, resid, b, *, name, tm=512):
    s, d = x.shape
    n = b.shape[0]

    def body(dh_ref, x_ref, w_ref, res_ref, b_ref, dx_ref, dw_ref, prod_ref):
        part = jnp.zeros((1, d), F32)
        for r in range(0, tm, UP_ROWS):
            rows = slice(r, r + UP_ROWS)
            xv, dhv = x_ref[rows, :], dh_ref[rows, :]
            rstd = lax.rsqrt(jnp.mean(xv * xv, axis=-1, keepdims=True) + NORM_EPS)
            xh = xv * rstd
            g = dhv * w_ref[...]
            dx = res_ref[rows, :] + rstd * (g - xh * jnp.mean(g * xh, axis=-1, keepdims=True))
            dx_ref[rows, :] = dx
            prod_ref[rows, :] = _dot(dx, b_ref[...], "nt")
            part = part + jnp.sum(dhv * xh, axis=0, keepdims=True)

        @pl.when(pl.program_id(0) == 0)
        def _():
            dw_ref[...] = part

        @pl.when(pl.program_id(0) > 0)
        def _():
            dw_ref[...] += part

    row = pl.BlockSpec((tm, d), lambda i: (i, 0))
    vec = pl.BlockSpec((1, d), lambda i: (0, 0))
    return pl.pallas_call(
        body, name=name, grid=(s // tm,), in_specs=[row, row, vec, row, pl.BlockSpec((n, d), lambda i: (0, 0))],
        out_specs=[row, vec, pl.BlockSpec((tm, n), lambda i: (i, 0))],
        out_shape=[jax.ShapeDtypeStruct((s, d), F32), jax.ShapeDtypeStruct((1, d), F32), jax.ShapeDtypeStruct((s, n), F32)],
        compiler_params=_params(("arbitrary",)),
    )(dh, x, w, resid, b)


def down_proj_loss(act, w_down, x1, target, *, tm=512):
    s, d = x1.shape

    def body(a_ref, w_ref, x1_ref, t_ref, dy_ref, sq_ref):
        part = jnp.zeros((1, d), F32)
        for r in range(0, tm, UP_ROWS):
            rows = slice(r, r + UP_ROWS)
            e = x1_ref[rows, :] + _dot(a_ref[rows, :], w_ref[...]) - t_ref[rows, :]
            dy_ref[rows, :] = e / float(d)
            part = part + jnp.sum(e * e, axis=0, keepdims=True)

        @pl.when(pl.program_id(0) == 0)
        def _():
            sq_ref[...] = part

        @pl.when(pl.program_id(0) > 0)
        def _():
            sq_ref[...] += part

    row = pl.BlockSpec((tm, d), lambda i: (i, 0))
    vec = pl.BlockSpec((1, d), lambda i: (0, 0))
    return pl.pallas_call(
        body, name="down_proj_loss", grid=(s // tm,),
        in_specs=[pl.BlockSpec((tm, D_FF), lambda i: (i, 0)), pl.BlockSpec((D_FF, d), lambda i: (0, 0)), row, row],
        out_specs=[row, vec], out_shape=[jax.ShapeDtypeStruct((s, d), F32), jax.ShapeDtypeStruct((1, d), F32)],
        compiler_params=_params(("arbitrary",)),
    )(act, w_down, x1, target)


def _row_shifts(ext, k_taps):
    return [_shift_down(ext, j) for j in range(k_taps)]


def _conv_rows(shifts, w):
    k_taps = len(shifts)
    acc = w[k_taps - 1:k_taps, :] * shifts[0]
    for k in range(k_taps - 1):
        acc = acc + w[k:k + 1, :] * shifts[k_taps - 1 - k]
    return acc


def _conv_weight_grad(dcur, shifts, rows, width):
    k_taps = len(shifts)
    out = [jnp.sum(dcur * shifts[k_taps - 1 - k][rows], axis=0, keepdims=True) for k in range(k_taps)]
    out.append(jnp.sum(dcur, axis=0, keepdims=True))
    return _stack_rows(out, width)


def _conv_rows_transposed(dext, w, k_taps):
    acc = w[k_taps - 1:k_taps, :] * dext
    for k in range(k_taps - 1):
        acc = acc + w[k:k + 1, :] * _shift_up(dext, k_taps - 1 - k)
    return acc


def _stack_rows(rows, width):
    ri = _row_iota((8, width))
    out = jnp.zeros((8, width), F32)
    for k, r in enumerate(rows):
        out = out + jnp.where(ri == k, r, 0.0)
    return out


UP_SHARD = 1408
UP_ROWS = 256


def up_ffn_fwd(hf, a_up, conv_w8, conv_b, *, tm=512):
    s = hf.shape[0]

    def body(a_ref, bg_ref, bv_ref, wg_ref, wv_ref, cbg_ref, cbv_ref, hu_ref, act_ref, actt_ref, carry):
        i, j = pl.program_id(0), pl.program_id(1)
        prev_g = jnp.where(i == 0, 0.0, carry[0, j])
        prev_v = jnp.where(i == 0, 0.0, carry[1, j])
        for r in range(0, tm, UP_ROWS):
            rows = slice(r, r + UP_ROWS)
            a = a_ref[rows, :]
            hg, hv = _dot(a, bg_ref[...]), _dot(a, bv_ref[...])
            hu_ref[0, rows, :] = hg
            hu_ref[1, rows, :] = hv
            gc = _conv_rows(_row_shifts(jnp.concatenate([prev_g, hg], axis=0), FFN_CONV), wg_ref[...])[8:] + cbg_ref[...]
            vc = _conv_rows(_row_shifts(jnp.concatenate([prev_v, hv], axis=0), FFN_CONV), wv_ref[...])[8:] + cbv_ref[...]
            act = gc * _sigmoid(gc) * vc
            act_ref[rows, :] = act.astype(BF16)
            actt_ref[:, rows] = act.T.astype(BF16)
            prev_g, prev_v = hg[UP_ROWS - 8:], hv[UP_ROWS - 8:]
        carry[0, j] = prev_g
        carry[1, j] = prev_v

    shard = lambda off: pl.BlockSpec((None, D_MODEL, UP_SHARD), lambda i, j: (j + off, 0, 0))
    taps = lambda off: pl.BlockSpec((8, UP_SHARD), lambda i, j: (0, j + off))
    bias = lambda off: pl.BlockSpec((1, UP_SHARD), lambda i, j: (0, j + off))
    return pl.pallas_call(
        body, name="up_ffn_fwd", grid=(s // tm, 2),
        in_specs=[pl.BlockSpec((tm, D_MODEL), lambda i, j: (i, 0)), shard(0), shard(2), taps(0), taps(2), bias(0), bias(2)],
        out_specs=[pl.BlockSpec((2, tm, UP_SHARD), lambda i, j: (0, i, j)), pl.BlockSpec((tm, UP_SHARD), lambda i, j: (i, j)),
                   pl.BlockSpec((UP_SHARD, tm), lambda i, j: (j, i))],
        out_shape=[jax.ShapeDtypeStruct((2, s, D_FF), F32), jax.ShapeDtypeStruct((s, D_FF), BF16),
                   jax.ShapeDtypeStruct((D_FF, s), BF16)],
        scratch_shapes=[pltpu.VMEM((2, 2, 8, UP_SHARD), F32)], compiler_params=_params(("arbitrary", "arbitrary")),
    )(hf, a_up, a_up, conv_w8, conv_w8, conv_b, conv_b)


def ffn_mid_bwd(hu, dact, conv_w8, conv_b, *, tm=1024, tc=256):
    s = hu.shape[1]
    ncol = D_FF // tc
    nrow = s // tm
    r8 = tm // 8

    def body(g_ref, v_ref, gp_ref, vp_ref, gn_ref, vn_ref, da_ref, dan_ref, wg_ref, wv_ref, bg_ref, bv_ref,
             dhu_ref, wgo_ref, wvo_ref):
        i = pl.program_id(1)
        first = i == 0
        last = i == nrow - 1

        def ext_of(cur_ref, prev_ref, next_ref):
            prev = jnp.where(first, 0.0, prev_ref[...])
            return jnp.concatenate([prev, cur_ref[...], next_ref[...]], axis=0)

        g_sh = _row_shifts(ext_of(g_ref, gp_ref, gn_ref), FFN_CONV)
        v_sh = _row_shifts(ext_of(v_ref, vp_ref, vn_ref), FFN_CONV)
        gc = _conv_rows(g_sh, wg_ref[...]) + bg_ref[...]
        vc = _conv_rows(v_sh, wv_ref[...]) + bv_ref[...]
        da_ext = jnp.concatenate([jnp.zeros((8, tc), F32), da_ref[...], jnp.where(last, 0.0, dan_ref[...])], axis=0)
        silu, dsilu = _silu_and_grad(gc)
        dgc = da_ext * vc * dsilu
        dvc = da_ext * silu
        dhu_ref[0] = _conv_rows_transposed(dgc, wg_ref[...], FFN_CONV)[8:8 + tm].astype(BF16)
        dhu_ref[1] = _conv_rows_transposed(dvc, wv_ref[...], FFN_CONV)[8:8 + tm].astype(BF16)

        cur = slice(8, 8 + tm)
        pg = _conv_weight_grad(dgc[cur], g_sh, cur, tc)
        pv = _conv_weight_grad(dvc[cur], v_sh, cur, tc)

        @pl.when(first)
        def _():
            wgo_ref[...] = pg
            wvo_ref[...] = pv

        @pl.when(i > 0)
        def _():
            wgo_ref[...] += pg
            wvo_ref[...] += pv

    def prev_idx(i):
        return jnp.maximum(i * r8 - 1, 0)

    def next_idx(i):
        return jnp.minimum((i + 1) * r8, s // 8 - 1)

    half = lambda k, rows, row_index: pl.BlockSpec((None, rows, tc), lambda j, i: (k, row_index(i), j))
    in_specs = [
        half(0, tm, lambda i: i), half(1, tm, lambda i: i),
        half(0, 8, prev_idx), half(1, 8, prev_idx),
        half(0, 8, next_idx), half(1, 8, next_idx),
        pl.BlockSpec((tm, tc), lambda j, i: (i, j)),
        pl.BlockSpec((8, tc), lambda j, i: (next_idx(i), j)),
        pl.BlockSpec((8, tc), lambda j, i: (0, j)),
        pl.BlockSpec((8, tc), lambda j, i: (0, j + ncol)),
        pl.BlockSpec((1, tc), lambda j, i: (0, j)),
        pl.BlockSpec((1, tc), lambda j, i: (0, j + ncol)),
    ]
    out_specs = [pl.BlockSpec((2, tm, tc), lambda j, i: (0, i, j)), pl.BlockSpec((8, tc), lambda j, i: (0, j)),
                 pl.BlockSpec((8, tc), lambda j, i: (0, j))]
    out_shape = [jax.ShapeDtypeStruct((2, s, D_FF), BF16),
                 jax.ShapeDtypeStruct((8, D_FF), F32), jax.ShapeDtypeStruct((8, D_FF), F32)]
    return pl.pallas_call(
        body, name="ffn_mid_bwd", grid=(ncol, nrow), in_specs=in_specs, out_specs=out_specs, out_shape=out_shape,
        compiler_params=_params(("parallel", "arbitrary")),
    )(hu, hu, hu, hu, hu, hu, dact, dact, conv_w8, conv_w8, conv_b, conv_b)


def _softplus(x):
    return jnp.maximum(x, 0.0) + jnp.log(1.0 + jnp.exp(-jnp.abs(x)))


def _cumsum_rows(v):
    n = v.shape[0]
    ri = _row_iota(v.shape)
    sh = 1
    while sh < n:
        v = v + jnp.where(ri >= sh, _shift_down(v, sh), 0.0)
        sh *= 2
    return v


def _rev_cumsum_rows(v):
    n = v.shape[0]
    ri = _row_iota(v.shape)
    sh = 1
    while sh < n:
        v = v + jnp.where(ri < n - sh, _shift_up(v, sh), 0.0)
        sh *= 2
    return v


def _total(v):
    return jnp.sum(jnp.sum(v, axis=1, keepdims=True), axis=0, keepdims=True)


def _ssd_in_specs(rev_nc=None):
    def ch(c):
        return c if rev_nc is None else rev_nc - 1 - c

    def prev(c):
        return jnp.maximum(ch(c) * (SSD_CHUNK // 8) - 1, 0)

    L = SSD_CHUNK
    return [
        pl.BlockSpec((L, 1024), lambda c: (ch(c), 0)),
        pl.BlockSpec((L, 1024), lambda c: (ch(c), 1)),
        pl.BlockSpec((L, 256), lambda c: (ch(c), 20)),
        pl.BlockSpec((L, 256), lambda c: (ch(c), 21)),
        pl.BlockSpec((8, 1024), lambda c: (prev(c), 1)),
        pl.BlockSpec((8, 256), lambda c: (prev(c), 20)),
        pl.BlockSpec((8, 256), lambda c: (prev(c), 21)),
        pl.BlockSpec((8, 1024), lambda c: (0, 0)),
        pl.BlockSpec((8, 256), lambda c: (0, 4)),
        pl.BlockSpec((8, 256), lambda c: (0, 5)),
        pl.BlockSpec((1, 1024), lambda c: (0, 0)),
        pl.BlockSpec((1, 256), lambda c: (0, 4)),
        pl.BlockSpec((1, 256), lambda c: (0, 5)),
        pl.BlockSpec((L, SMALL_COLS), lambda c: (ch(c), SMALL_BLOCK)),
        pl.BlockSpec((8, 128), lambda c: (0, 0)),
        pl.BlockSpec((1, 1024), lambda c: (0, 0)),
    ]


def _ssd_conv_pre(cur_ref, prev_ref, w_ref, b_ref, first):
    prev = jnp.where(first, 0.0, prev_ref[...])
    shifts = _row_shifts(jnp.concatenate([prev, cur_ref[...]], axis=0), SSD_CONV)
    return shifts, _conv_rows(shifts, w_ref[...])[8:] + b_ref[...]


def _ssd_time_consts(small_ref, sp_ref):
    dt_pre = small_ref[...] + sp_ref[0:1, :]
    dt = _softplus(dt_pre)
    a = -jnp.exp(sp_ref[1:2, :])
    acs = _cumsum_rows(dt * a)
    return dt_pre, dt, a, acs


def ssd_fwd(proj, conv_w8, conv_b, smallp, norm_w):
    s = proj.shape[0]
    nc = s // SSD_CHUNK
    L = SSD_CHUNK

    def body(z_ref, xs_ref, b_ref, c_ref, xsp_ref, bp_ref, cp_ref, wx_ref, wb_ref, wc_ref, bx_ref, bb_ref, bc_ref,
             small_ref, sp_ref, nw_ref, y_ref, yt_ref, ypre_ref, st_ref, state):
        first = pl.program_id(0) == 0

        @pl.when(first)
        def _():
            state[...] = jnp.zeros_like(state)

        xs = _ssd_conv_pre(xs_ref, xsp_ref, wx_ref, bx_ref, first)[1]
        xs = xs * _sigmoid(xs)
        bm = _ssd_conv_pre(b_ref, bp_ref, wb_ref, bb_ref, first)[1]
        bm = bm * _sigmoid(bm)
        cm = _ssd_conv_pre(c_ref, cp_ref, wc_ref, bc_ref, first)[1]
        cm = cm * _sigmoid(cm)
        _, dt, _, acs = _ssd_time_consts(small_ref, sp_ref)
        acs_t = acs.T
        li = _lane_iota((L, L))
        ri = _row_iota((L, L))
        tri = ri >= li
        lo = li < HEAD_DIM
        st_ref[0] = state[...]
        for g in range(2):
            bg = bm[:, 128 * g:128 * g + 128]
            cg = cm[:, 128 * g:128 * g + 128]
            gmat = _dot(cg, bg, "nt")
            for pp in range(4):
                p = 4 * g + pp
                h0, h1 = 2 * p, 2 * p + 1
                x = xs[:, 128 * p:128 * p + 128]
                a0, a1 = acs[:, h0:h0 + 1], acs[:, h1:h1 + 1]
                xdt = x * jnp.where(lo, dt[:, h0:h0 + 1], dt[:, h1:h1 + 1])
                m0 = gmat * jnp.exp(jnp.where(tri, a0 - acs_t[h0:h0 + 1, :], NEG_BIG))
                m1 = gmat * jnp.exp(jnp.where(tri, a1 - acs_t[h1:h1 + 1, :], NEG_BIG))
                yd = _dot(m0, jnp.where(lo, xdt, 0.0)) + _dot(m1, jnp.where(lo, 0.0, xdt))
                hin = state[p]
                yo = _dot(cg, hin, "nt") * jnp.exp(jnp.where(lo, a0, a1))
                dskip = jnp.where(lo[0:1], sp_ref[2:3, h0:h0 + 1], sp_ref[2:3, h1:h1 + 1])
                ypre_ref[:, 128 * p:128 * p + 128] = yd + yo + dskip * x
                al0, al1 = acs[L - 1:L, h0:h0 + 1], acs[L - 1:L, h1:h1 + 1]
                w = jnp.exp(jnp.where(lo, al0 - a0, al1 - a1))
                dec = jnp.exp(jnp.where(ri < HEAD_DIM, al0, al1))
                state[p] = dec * hin + _dot(xdt * w, bg, "tn")
        z = z_ref[...]
        yg = ypre_ref[...] * (z * _sigmoid(z))
        for g in range(2):
            seg = yg[:, 512 * g:512 * g + 512]
            r = lax.rsqrt(jnp.mean(seg * seg, axis=-1, keepdims=True) + NORM_EPS)
            out = (seg * r) * nw_ref[:, 512 * g:512 * g + 512]
            y_ref[:, 512 * g:512 * g + 512] = out.astype(BF16)
            yt_ref[512 * g:512 * g + 512, :] = out.T.astype(BF16)

    row = pl.BlockSpec((L, 1024), lambda c: (c, 0))
    return pl.pallas_call(
        body, name="ssd_fwd", grid=(nc,), in_specs=_ssd_in_specs(),
        out_specs=[row, pl.BlockSpec((1024, L), lambda c: (0, c)), row,
                   pl.BlockSpec((1, N_PAIRS, 128, 128), lambda c: (c, 0, 0, 0))],
        out_shape=[jax.ShapeDtypeStruct((s, 1024), BF16), jax.ShapeDtypeStruct((1024, s), BF16),
                   jax.ShapeDtypeStruct((s, 1024), F32), jax.ShapeDtypeStruct((nc, N_PAIRS, 128, 128), F32)],
        scratch_shapes=[pltpu.VMEM((N_PAIRS, 128, 128), F32)],
        compiler_params=_params(("arbitrary",)),
    )(proj, proj, proj, proj, proj, proj, proj, conv_w8, conv_w8, conv_w8, conv_b, conv_b, conv_b, proj, smallp, norm_w)


def ssd_bwd(proj, conv_w8, conv_b, smallp, norm_w, ypre, states, dy, sel, swap=()):
    s = proj.shape[0]
    nc = s // SSD_CHUNK
    L = SSD_CHUNK

    ns = len(swap)
    n_in, n_out, n_scratch = 20, 10, 11

    def body(*refs):
        own = refs[:n_in] + refs[n_in + ns:n_in + ns + n_out] + refs[n_in + 2 * ns + n_out:n_in + 2 * ns + n_out + n_scratch]
        if ns:
            start, finish = _pair_swap_phases(refs[n_in:n_in + ns], refs[n_in + ns + n_out:n_in + 2 * ns + n_out],
                                              *refs[n_in + 2 * ns + n_out + n_scratch:])
            pl.when(pl.program_id(0) == 0)(start)
        compute(*own)
        if ns:
            pl.when(pl.program_id(0) == nc - 1)(finish)

    def compute(z_ref, xs_ref, b_ref, c_ref, xsp_ref, bp_ref, cp_ref, wx_ref, wb_ref, wc_ref, bx_ref, bb_ref, bc_ref,
                small_ref, sp_ref, nw_ref, ypre_ref, st_ref, dy_ref, sel_ref,
                dz_ref, dxs_ref, db_ref, dc_ref, dsmall_ref, gwx_ref, gwb_ref, gwc_ref, gsp_ref, gnw_ref,
                dstate, carry_x, carry_b, carry_c, dxs_buf, dbm_buf, dcm_buf, qcs, col_sums, acs_terms, dt_terms):
        step = pl.program_id(0)
        col_sums[...] = jnp.zeros_like(col_sums)
        first_chunk = step == nc - 1
        start = step == 0

        @pl.when(start)
        def _():
            dstate[...] = jnp.zeros_like(dstate)
            carry_x[...] = jnp.zeros_like(carry_x)
            carry_b[...] = jnp.zeros_like(carry_b)
            carry_c[...] = jnp.zeros_like(carry_c)

        xs_sh, xs_pre = _ssd_conv_pre(xs_ref, xsp_ref, wx_ref, bx_ref, first_chunk)
        b_sh, b_pre = _ssd_conv_pre(b_ref, bp_ref, wb_ref, bb_ref, first_chunk)
        c_sh, c_pre = _ssd_conv_pre(c_ref, cp_ref, wc_ref, bc_ref, first_chunk)
        xs, xs_ds = _silu_and_grad(xs_pre)
        bm, b_ds = _silu_and_grad(b_pre)
        cm, c_ds = _silu_and_grad(c_pre)
        dt_pre, dt, a, acs = _ssd_time_consts(small_ref, sp_ref)
        acs_t = acs.T
        li = _lane_iota((L, L))
        ri = _row_iota((L, L))
        tri = ri >= li
        lo = li < HEAD_DIM
        lo_rows = ri < HEAD_DIM
        li1 = _lane_iota((1, L))

        z = z_ref[...]
        sz, dsz = _silu_and_grad(z)
        y = ypre_ref[...]
        yg = y * sz
        dout = dy_ref[...]
        dyg_parts = []
        gnw_parts = []
        for g in range(2):
            sl = slice(512 * g, 512 * g + 512)
            seg = yg[:, sl]
            r = lax.rsqrt(jnp.mean(seg * seg, axis=-1, keepdims=True) + NORM_EPS)
            n = seg * r
            gnw_parts.append(jnp.sum(dout[:, sl] * n, axis=0, keepdims=True))
            gg = dout[:, sl] * nw_ref[:, sl]
            dyg_parts.append(r * (gg - n * jnp.mean(gg * n, axis=-1, keepdims=True)))
        dyg = jnp.concatenate(dyg_parts, axis=1)
        gnw = jnp.concatenate(gnw_parts, axis=1)
        dz_ref[...] = (dyg * y * dsz).astype(BF16)
        dypre = dyg * sz

        qcs[...] = jnp.zeros_like(qcs)
        dalast = jnp.zeros((1, L), F32)
        for g in range(2):
            bg = bm[:, 128 * g:128 * g + 128]
            cg = cm[:, 128 * g:128 * g + 128]
            gmat = _dot(cg, bg, "nt")
            dgmat = jnp.zeros((L, L), F32)
            dbg = jnp.zeros((L, L), F32)
            dcg = jnp.zeros((L, L), F32)
            for pp in range(4):
                p = 4 * g + pp
                h0, h1 = 2 * p, 2 * p + 1
                lanes = slice(128 * p, 128 * p + 128)
                x = xs[:, lanes]
                dyp = dypre[:, lanes]
                a0, a1 = acs[:, h0:h0 + 1], acs[:, h1:h1 + 1]
                dtl = jnp.where(lo, dt[:, h0:h0 + 1], dt[:, h1:h1 + 1])
                xdt = x * dtl
                l0 = jnp.exp(jnp.where(tri, a0 - acs_t[h0:h0 + 1, :], NEG_BIG))
                l1 = jnp.exp(jnp.where(tri, a1 - acs_t[h1:h1 + 1, :], NEG_BIG))
                m0, m1 = gmat * l0, gmat * l1
                dskip = jnp.where(lo[0:1], sp_ref[2:3, h0:h0 + 1], sp_ref[2:3, h1:h1 + 1])
                col_sums[0:1, lanes] = jnp.sum(dyp * x, axis=0, keepdims=True)
                dx = dyp * dskip
                dy0, dy1 = jnp.where(lo, dyp, 0.0), jnp.where(lo, 0.0, dyp)
                x0, x1 = jnp.where(lo, xdt, 0.0), jnp.where(lo, 0.0, xdt)
                dm0, dm1 = _dot(dy0, x0, "nt"), _dot(dy1, x1, "nt")
                dxdt = _dot(m0, dy0, "tn") + _dot(m1, dy1, "tn")
                q0, q1 = dm0 * m0, dm1 * m1
                qcs[h0:h0 + 1, :] = jnp.sum(q0, axis=0, keepdims=True)
                qcs[h1:h1 + 1, :] = jnp.sum(q1, axis=0, keepdims=True)
                row_terms = jnp.where(lo, q0 + pltpu.roll(q0, HEAD_DIM, 1), q1 + pltpu.roll(q1, HEAD_DIM, 1))
                dgmat = dgmat + dm0 * l0 + dm1 * l1
                hin = st_ref[0, p]
                e = jnp.exp(jnp.where(lo, a0, a1))
                ch = _dot(cg, hin, "nt")
                dch = dyp * e
                dcg = dcg + _dot(dch, hin)
                dhin = _dot(dch, cg, "tn")
                dhout = dstate[p]
                al0, al1 = acs[L - 1:L, h0:h0 + 1], acs[L - 1:L, h1:h1 + 1]
                dec = jnp.exp(jnp.where(lo_rows, al0, al1))
                dhin = dhin + dec * dhout
                dal = dhout * hin * dec
                dal0 = _total(jnp.where(lo_rows, dal, 0.0))
                dal1 = _total(dal) - dal0
                dalast = dalast + jnp.where(li1 == h0, dal0, 0.0) + jnp.where(li1 == h1, dal1, 0.0)
                w = jnp.exp(jnp.where(lo, al0 - a0, al1 - a1))
                xw = xdt * w
                dxw = _dot(bg, dhout, "nt")
                dbg = dbg + _dot(xw, dhout)
                dxdt = dxdt + dxw * w
                dww = dxw * xw
                col_sums[1:2, lanes] = jnp.sum(dww, axis=0, keepdims=True)
                acs_terms[:, lanes] = row_terms + dch * ch - dww
                dx = dx + dxdt * dtl
                dt_terms[:, lanes] = dxdt * x
                dxs_buf[:, lanes] = dx
                dstate[p] = dhin
            dcg = dcg + _dot(dgmat, bg)
            dbg = dbg + _dot(dgmat, cg, "tn")
            dbm_buf[:, 128 * g:128 * g + 128] = dbg
            dcm_buf[:, 128 * g:128 * g + 128] = dcg

        head_sums = _split3_dot(col_sums[...], sel_ref[...])
        dskip_g = head_sums[0:1, :]
        dalast = dalast + head_sums[1:2, :]
        ddt = _split3_dot(dt_terms[...], sel_ref[...])
        dacs_tot = _split3_dot(acs_terms[...], sel_ref[...]) - qcs[...].T + jnp.where(ri == L - 1, dalast, 0.0)
        dstep = _rev_cumsum_rows(dacs_tot)
        ddt = ddt + dstep * a
        head_lane = li < N_HEADS
        ddt_pre = jnp.where(head_lane, ddt * _sigmoid(dt_pre), 0.0)
        dsmall_ref[...] = ddt_pre
        da = jnp.sum(jnp.where(head_lane, dstep * dt, 0.0), axis=0, keepdims=True)
        gsp = _stack_rows([jnp.sum(ddt_pre, axis=0, keepdims=True), da * a, dskip_g], L)

        def conv_back(dpost, ds, shifts, w_ref, carry, out_ref, width):
            dpre = dpost * ds
            dext = jnp.concatenate([dpre, carry[...]], axis=0)
            out_ref[...] = _conv_rows_transposed(dext, w_ref[...], SSD_CONV)[:L].astype(BF16)
            carry[...] = dpre[0:8]
            return _conv_weight_grad(dpre, shifts, slice(8, 8 + L), width)

        gwx = conv_back(dxs_buf[...], xs_ds, xs_sh, wx_ref, carry_x, dxs_ref, 1024)
        gwb = conv_back(dbm_buf[...], b_ds, b_sh, wb_ref, carry_b, db_ref, 256)
        gwc = conv_back(dcm_buf[...], c_ds, c_sh, wc_ref, carry_c, dc_ref, 256)

        @pl.when(start)
        def _():
            gwx_ref[...] = gwx
            gwb_ref[...] = gwb
            gwc_ref[...] = gwc
            gsp_ref[...] = gsp
            gnw_ref[...] = gnw

        @pl.when(step > 0)
        def _():
            gwx_ref[...] += gwx
            gwb_ref[...] += gwb
            gwc_ref[...] += gwc
            gsp_ref[...] += gsp
            gnw_ref[...] += gnw

    def ch(c):
        return nc - 1 - c

    row = pl.BlockSpec((L, 1024), lambda c: (ch(c), 0))
    row256 = pl.BlockSpec((L, 256), lambda c: (ch(c), 0))
    in_specs = _ssd_in_specs(rev_nc=nc) + [row, pl.BlockSpec((1, N_PAIRS, 128, 128), lambda c: (ch(c), 0, 0, 0)), row,
                                           pl.BlockSpec((1024, 128), lambda c: (0, 0))]
    out_specs = [row, row, row256, row256, pl.BlockSpec((L, 128), lambda c: (ch(c), 0)),
                 pl.BlockSpec((8, 1024), lambda c: (0, 0)), pl.BlockSpec((8, 256), lambda c: (0, 0)),
                 pl.BlockSpec((8, 256), lambda c: (0, 0)), pl.BlockSpec((8, 128), lambda c: (0, 0)),
                 pl.BlockSpec((1, 1024), lambda c: (0, 0))]
    out_shape = [jax.ShapeDtypeStruct((s, 1024), BF16), jax.ShapeDtypeStruct((s, 1024), BF16),
                 jax.ShapeDtypeStruct((s, 256), BF16), jax.ShapeDtypeStruct((s, 256), BF16),
                 jax.ShapeDtypeStruct((s, 128), F32),
                 jax.ShapeDtypeStruct((8, 1024), F32), jax.ShapeDtypeStruct((8, 256), F32),
                 jax.ShapeDtypeStruct((8, 256), F32), jax.ShapeDtypeStruct((8, 128), F32),
                 jax.ShapeDtypeStruct((1, 1024), F32)]
    scratch = [pltpu.VMEM((N_PAIRS, 128, 128), F32), pltpu.VMEM((8, 1024), F32), pltpu.VMEM((8, 256), F32),
               pltpu.VMEM((8, 256), F32), pltpu.VMEM((L, 1024), F32), pltpu.VMEM((L, 256), F32), pltpu.VMEM((L, 256), F32),
               pltpu.VMEM((L, L), F32), pltpu.VMEM((8, 1024), F32), pltpu.VMEM((L, 1024), F32), pltpu.VMEM((L, 1024), F32)]
    assert (len(in_specs), len(out_specs), len(scratch)) == (n_in, n_out, n_scratch)
    outs = pl.pallas_call(
        body, name="ssd_bwd", grid=(nc,), in_specs=in_specs + [ANY] * ns, out_specs=out_specs + [ANY] * ns,
        out_shape=out_shape + _pair_swap_out_shapes(swap), scratch_shapes=scratch + (_pair_swap_scratch(ns) if ns else []),
        compiler_params=_params(("arbitrary",)),
    )(proj, proj, proj, proj, proj, proj, proj, conv_w8, conv_w8, conv_w8, conv_b, conv_b, conv_b, proj, smallp, norm_w,
      ypre, states, dy, sel, *swap)
    return (*outs[:n_out], list(outs[n_out:]))


FOX_SCALE = HEAD_DIM ** -0.5
FOX_T = 256
Q_COL, K_COL, V_COL = 2, 3, 4


def _split_dot(v, m, terms):
    out, rest = None, v
    for i in range(terms):
        piece = rest.astype(BF16)
        out = _dot(piece, m) if out is None else out + _dot(piece, m)
        if i + 1 < terms:
            rest = rest - piece.astype(F32)
    return out


def _split3_dot(v, m):
    return _split_dot(v, m, 3)


def _head_mean(x, sel_ref, selt_ref):
    return _dot(x, sel_ref[...]) * (1.0 / HEAD_DIM)


def _head_spread(v, selt_ref):
    return _split_dot(v, selt_ref[...], 2)


def _head_rstd(x, sel_ref, selt_ref):
    return _head_spread(lax.rsqrt(_head_mean(x * x, sel_ref, selt_ref) + NORM_EPS), selt_ref)


def fox_tables():
    r = np.arange(3 * 128)
    piece, lane = r // 128, r % 128
    head = lane - F_LANE
    is_head = np.logical_and(head >= 0, head < N_HEADS)
    col = 128 * (head // 2) + HEAD_DIM * (1 - head % 2) + piece
    cols = np.arange(1024)
    place_q = np.logical_and(is_head[:, None], cols[None, :] == col[:, None])
    place_k = np.logical_and(is_head[:, None], cols[None, :] == (col + 3)[:, None])
    ones_q = np.logical_and(cols % HEAD_DIM >= 3, cols % HEAD_DIM < 6)[None]
    ones_k = (cols % HEAD_DIM < 3)[None]
    h = np.arange(128) - F_LANE
    ok = np.logical_and(h >= 0, h < N_HEADS)
    same_pair = cols[:, None] // 128 == (h // 2)[None, :]
    fold_even = np.logical_and(np.logical_and(ok, h % 2 == 0)[None, :], same_pair)
    fold_odd = np.logical_and(np.logical_and(ok, h % 2 == 1)[None, :], same_pair)
    as_bf16 = lambda t: jnp.asarray(t.astype(np.float32), BF16)
    return (as_bf16(place_q), as_bf16(place_k), jnp.asarray(ones_q, F32), jnp.asarray(ones_k, F32),
            as_bf16(fold_even), as_bf16(fold_odd))


def fox_prep(proj, smallp, qw, kw, sel, selt, place_q, place_k, ones_q, ones_k, *, tm=256):
    s = proj.shape[0]

    def body(q_ref, k_ref, v_ref, small_ref, sp_ref, qw_ref, kw_ref, sel_ref, selt_ref, pq_ref, pk_ref, oq_ref, ok_ref,
             qn_ref, kn_ref, aq_ref, ak_ref, vb_ref, knt_ref, akt_ref, vt_ref, carry):
        @pl.when(pl.program_id(0) == 0)
        def _():
            carry[...] = jnp.zeros_like(carry)

        q = q_ref[...]
        qn_ref[...] = (((q * _head_rstd(q, sel_ref, selt_ref)) * qw_ref[...]) * FOX_SCALE).astype(BF16)
        k = k_ref[...]
        kn = ((k * _head_rstd(k, sel_ref, selt_ref)) * kw_ref[...]).astype(BF16)
        kn_ref[...] = kn
        knt_ref[...] = kn.astype(F32).T.astype(BF16)
        vb_ref[...] = v_ref[...].astype(BF16)
        vt_ref[...] = v_ref[...].T.astype(BF16)
        li = _lane_iota((tm, 128))
        f_lane = jnp.logical_and(li >= F_LANE, li < F_LANE + N_HEADS)
        logf = jnp.where(f_lane, -_softplus(-(small_ref[...] + sp_ref[3:4, :])), 0.0)
        cum = _cumsum_rows(logf) + carry[...]
        carry[...] = cum[tm - 1:tm, :]
        hi = cum.astype(BF16)
        r1 = cum - hi.astype(F32)
        mid = r1.astype(BF16)
        lo = (r1 - mid.astype(F32)).astype(BF16)
        pieces = jnp.concatenate([hi, mid, lo], axis=1)
        aq_ref[...] = (_dot(pieces, pq_ref[...]) + oq_ref[...]).astype(BF16)
        ak = ok_ref[...] - _dot(pieces, pk_ref[...])
        ak_ref[...] = ak.astype(BF16)
        akt_ref[...] = ak.T.astype(BF16)

    row = pl.BlockSpec((tm, 1024), lambda i: (i, 0))
    col = pl.BlockSpec((1024, tm), lambda i: (0, i))
    vec = pl.BlockSpec((1, 1024), lambda i: (0, 0))
    table = pl.BlockSpec((384, 1024), lambda i: (0, 0))
    wide = jax.ShapeDtypeStruct((s, 1024), BF16)
    tall = jax.ShapeDtypeStruct((1024, s), BF16)
    return pl.pallas_call(
        body, name="fox_prep", grid=(s // tm,),
        in_specs=[pl.BlockSpec((tm, 1024), lambda i: (i, Q_COL)), pl.BlockSpec((tm, 1024), lambda i: (i, K_COL)),
                  pl.BlockSpec((tm, 1024), lambda i: (i, V_COL)),
                  pl.BlockSpec((tm, 128), lambda i: (i, SMALL_BLOCK)), pl.BlockSpec((8, 128), lambda i: (0, 0)), vec, vec,
                  pl.BlockSpec((1024, 128), lambda i: (0, 0)), pl.BlockSpec((128, 1024), lambda i: (0, 0)),
                  table, table, vec, vec],
        out_specs=[row, row, row, row, row, col, col, col],
        out_shape=[wide, wide, wide, wide, wide, tall, tall, tall],
        scratch_shapes=[pltpu.VMEM((1, 128), F32)], compiler_params=_params(("arbitrary",)),
    )(proj, proj, proj, proj, smallp, qw, kw, sel, selt, place_q, place_k, ones_q, ones_k)


def fox_fwd(qn, kn, aq, ak, vt, shards=()):
    s = qn.shape[0]
    t = FOX_T
    nq = s // t
    ng = len(shards)

    def body(*refs):
        q_ref, k_ref, aq_ref, ak_ref, vt_ref = refs[:5]
        o_ref, ot_ref, lse_ref = refs[5 + ng:8 + ng]
        p = pl.program_id(0)
        if ng:
            start, forward, finish = _gather_phases(refs[5:5 + ng], refs[8 + ng:8 + 2 * ng], *refs[8 + 2 * ng:])
            pl.when(p == 0)(start)
            pl.when(p == N_PAIRS // 2)(forward)

        @pl.when(p == 0)
        def _():
            lse_ref[...] = jnp.zeros_like(lse_ref)

        lo = _lane_iota((t, 128)) < HEAD_DIM
        lo_rows = _row_iota((128, t)) < HEAD_DIM
        causal_t = _lane_iota((t, t)) >= _row_iota((t, t))

        def q_loop(qi, _):
            q0 = pl.multiple_of(qi * t, t)
            qv, aqv = q_ref[pl.ds(q0, t), :], aq_ref[pl.ds(q0, t), :]
            qa, qb = jnp.where(lo, qv, aqv), jnp.where(lo, aqv, qv)

            def scores(kj):
                k0 = pl.multiple_of(kj * t, t)
                kv, akv = k_ref[pl.ds(k0, t), :], ak_ref[pl.ds(k0, t), :]
                return _dot(jnp.where(lo, kv, akv), qa, "nt"), _dot(jnp.where(lo, akv, kv), qb, "nt")

            def update(kj, stats, s0, s1):
                m0, l0, m1, l1, acc = stats
                vtv = vt_ref[:, pl.ds(pl.multiple_of(kj * t, t), t)]
                n0 = jnp.maximum(m0, jnp.max(s0, axis=0, keepdims=True))
                n1 = jnp.maximum(m1, jnp.max(s1, axis=0, keepdims=True))
                a0, a1 = jnp.exp(m0 - n0), jnp.exp(m1 - n1)
                p0, p1 = jnp.exp(s0 - n0), jnp.exp(s1 - n1)
                l0 = a0 * l0 + jnp.sum(p0, axis=0, keepdims=True)
                l1 = a1 * l1 + jnp.sum(p1, axis=0, keepdims=True)
                acc = (jnp.where(lo_rows, a0, a1) * acc + _dot(jnp.where(lo_rows, vtv, 0.0), p0)
                       + _dot(jnp.where(lo_rows, 0.0, vtv), p1))
                return n0, l0, n1, l1, acc

            def step(kj, carry):
                stats, (s0, s1) = carry[:5], carry[5:]
                nxt = scores(kj + 1)
                return (*update(kj, stats, s0, s1), *nxt)

            def row(val):
                return jnp.full((1, t), val, F32)

            init = (row(NEG_BIG), row(0.0), row(NEG_BIG), row(0.0), jnp.zeros((128, t), F32), *scores(0))
            carry = lax.fori_loop(0, qi, step, init)
            s0, s1 = jnp.where(causal_t, carry[5], NEG_BIG), jnp.where(causal_t, carry[6], NEG_BIG)
            m0, l0, m1, l1, acc = update(qi, carry[:5], s0, s1)
            out_t = acc / jnp.where(lo_rows, l0, l1)
            ot_ref[:, pl.ds(q0, t)] = out_t.astype(BF16)
            o_ref[pl.ds(q0, t), :] = out_t.T.astype(BF16)
            ri = _row_iota((N_HEADS, t))
            old = lse_ref[:, pl.ds(q0, t)]
            lse_ref[:, pl.ds(q0, t)] = jnp.where(
                ri == 2 * p, m0 + jnp.log(l0), jnp.where(ri == 2 * p + 1, m1 + jnp.log(l1), old))
            return 0

        lax.fori_loop(0, nq, q_loop, 0)
        if ng:
            pl.when(p == N_PAIRS - 1)(finish)

    pair = pl.BlockSpec((s, 128), lambda p: (0, p))
    outs = pl.pallas_call(
        body, name="fox_fwd", grid=(N_PAIRS,),
        in_specs=[pair] * 4 + [pl.BlockSpec((128, s), lambda p: (p, 0))] + [ANY] * ng,
        out_specs=[pair, pl.BlockSpec((128, s), lambda p: (p, 0)), pl.BlockSpec((N_HEADS, s), lambda p: (0, 0))] + [ANY] * ng,
        out_shape=[jax.ShapeDtypeStruct((s, 1024), BF16), jax.ShapeDtypeStruct((1024, s), BF16),
                   jax.ShapeDtypeStruct((N_HEADS, s), F32)] + _gather_out_shapes(shards),
        scratch_shapes=_gather_scratch(ng) if ng else [],
        compiler_params=_params(("arbitrary",)),
    )(qn, kn, aq, ak, vt, *shards)
    return outs[0], outs[1], outs[2], list(outs[3:])


def fox_bwd(qn, kn, aq, ak, knt, akt, vb, lse, dmixed, parts=()):
    s = qn.shape[0]
    t = FOX_T
    nq = s // t
    once = pl.Buffered(1)
    ns = len(parts)

    def body(*refs):
        q_ref, k_ref, aq_ref, ak_ref, kt_ref, akt_ref, v_ref, lse_ref, do_ref = refs[:9]
        dq_ref, dk_ref, dv_ref, dc0_ref, dc1_ref = refs[9 + ns:14 + ns]
        p_scr, dp_scr = refs[14 + 2 * ns:16 + 2 * ns]
        p = pl.program_id(0)
        if ns:
            start, finish = _scatter_phases(refs[9:9 + ns], refs[14 + ns:14 + 2 * ns], *refs[16 + 2 * ns:])
            pl.when(p == 0)(start)
        dk_ref[...] = jnp.zeros_like(dk_ref)
        dv_ref[...] = jnp.zeros_like(dv_ref)
        dc0_ref[...] = jnp.zeros_like(dc0_ref)
        dc1_ref[...] = jnp.zeros_like(dc1_ref)
        lo = _lane_iota((t, 128)) < HEAD_DIM
        lo_rows = _row_iota((128, t)) < HEAD_DIM
        causal_t = _lane_iota((t, t)) >= _row_iota((t, t))

        def q_loop(qi, _):
            q0 = pl.multiple_of(qi * t, t)
            qv, aqv = q_ref[pl.ds(q0, t), :], aq_ref[pl.ds(q0, t), :]
            qa, qb = jnp.where(lo, qv, aqv), jnp.where(lo, aqv, qv)
            do = do_ref[pl.ds(q0, t), :]
            doa, dob = jnp.where(lo, do, 0.0).astype(BF16), jnp.where(lo, 0.0, do).astype(BF16)
            lse_blk = lse_ref[:, pl.ds(q0, t)]
            ri = _row_iota((N_HEADS, t))
            lse0 = jnp.sum(jnp.where(ri == 2 * p, lse_blk, 0.0), axis=0, keepdims=True)
            lse1 = jnp.sum(jnp.where(ri == 2 * p + 1, lse_blk, 0.0), axis=0, keepdims=True)

            def scores(kj):
                k0 = pl.multiple_of(kj * t, t)
                kv, akv = k_ref[pl.ds(k0, t), :], ak_ref[pl.ds(k0, t), :]
                return _dot(jnp.where(lo, kv, akv), qa, "nt"), _dot(jnp.where(lo, akv, kv), qb, "nt")

            def pass1(kj, d0, d1, diagonal):
                k0 = pl.multiple_of(kj * t, t)
                vv = v_ref[pl.ds(k0, t), :]
                s0, s1 = scores(kj)
                if diagonal:
                    s0, s1 = jnp.where(causal_t, s0, NEG_BIG), jnp.where(causal_t, s1, NEG_BIG)
                p0, p1 = jnp.exp(s0 - lse0), jnp.exp(s1 - lse1)
                dp0, dp1 = _dot(vv, doa, "nt"), _dot(vv, dob, "nt")
                p_scr[0, kj], p_scr[1, kj] = p0, p1
                dp_scr[0, kj], dp_scr[1, kj] = dp0, dp1
                dv_ref[pl.ds(k0, t), :] += _dot(p0, doa) + _dot(p1, dob)
                return d0 + jnp.sum(p0 * dp0, axis=0, keepdims=True), d1 + jnp.sum(p1 * dp1, axis=0, keepdims=True)

            zero = jnp.zeros((1, t), F32)
            d0, d1 = lax.fori_loop(0, qi, lambda kj, c: pass1(kj, *c, False), (zero, zero))
            d0, d1 = pass1(qi, d0, d1, True)

            def fold_lanes(v):
                return functools.reduce(lambda a, b: a + b, [v[:, 128 * i:128 * (i + 1)] for i in range(t // 128)])

            def pass2(kj, carry):
                dq0, dq1 = carry
                k0 = pl.multiple_of(kj * t, t)
                p0, p1 = p_scr[0, kj], p_scr[1, kj]
                ds0, ds1 = p0 * (dp_scr[0, kj] - d0), p1 * (dp_scr[1, kj] - d1)
                dk_ref[pl.ds(k0, t), :] += jnp.where(lo, _dot(ds0, qa), _dot(ds1, qb))
                dc0_ref[pl.ds(k0, t), :] += fold_lanes(ds0)
                dc1_ref[pl.ds(k0, t), :] += fold_lanes(ds1)
                ktv, aktv = kt_ref[:, pl.ds(k0, t)], akt_ref[:, pl.ds(k0, t)]
                return dq0 + _dot(jnp.where(lo_rows, ktv, aktv), ds0), dq1 + _dot(jnp.where(lo_rows, aktv, ktv), ds1)

            zq = jnp.zeros((128, t), F32)
            dq0, dq1 = lax.fori_loop(0, qi + 1, pass2, (zq, zq))
            dq_ref[pl.ds(q0, t), :] = jnp.where(lo_rows, dq0, dq1).T
            return 0

        lax.fori_loop(0, nq, q_loop, 0)
        if ns:
            pl.when(p == N_PAIRS - 1)(finish)

    pair = pl.BlockSpec((s, 128), lambda p: (0, p))
    pair_t = pl.BlockSpec((128, s), lambda p: (p, 0))
    out = jax.ShapeDtypeStruct((s, 1024), F32)
    outs = pl.pallas_call(
        body, name="fox_bwd", grid=(N_PAIRS,),
        in_specs=[pair, pair, pair, pair, pair_t, pair_t, pair, pl.BlockSpec((N_HEADS, s), lambda p: (0, 0)),
                  pl.BlockSpec((s, 128), lambda p: (0, 8 + p))] + [ANY] * ns,
        out_specs=[pl.BlockSpec((s, 128), lambda p: (0, p), pipeline_mode=once)] * 5 + [ANY] * ns,
        out_shape=[out] * 5 + [jax.ShapeDtypeStruct(p.shape, p.dtype) for p in parts],
        scratch_shapes=[pltpu.VMEM((2, nq, t, t), F32), pltpu.VMEM((2, nq, t, t), F32)] + (_scatter_scratch(ns) if ns else []),
        compiler_params=_params(("arbitrary",)),
    )(qn, kn, aq, ak, knt, akt, vb, lse, dmixed, *parts)
    return (*outs[:5], _keep_own_blocks(outs[5:], parts))


def fox_post(dqn, dkn, dc0, dc1, proj, smallp, qw, kw, sel, selt, fold_even, fold_odd, *, tm=256):
    s = proj.shape[0]
    nrow = s // tm

    def body(dqn_ref, dkn_ref, dc0_ref, dc1_ref, q_ref, k_ref, small_ref, sp_ref, qw_ref, kw_ref, sel_ref, selt_ref,
             fe_ref, fo_ref, dq_ref, dk_ref, dsmall_ref, gqw_ref, gkw_ref, gfb_ref, carry):
        step = pl.program_id(0)

        @pl.when(step == 0)
        def _():
            carry[...] = jnp.zeros_like(carry)

        def norm_bwd(x_ref, w_ref, dn, out_ref):
            x = x_ref[...]
            rf = _head_rstd(x, sel_ref, selt_ref)
            xh = x * rf
            g = dn * w_ref[...]
            mean_gx = _head_spread(_head_mean(g * xh, sel_ref, selt_ref), selt_ref)
            out_ref[...] = (rf * (g - xh * mean_gx)).astype(BF16)
            return jnp.sum(dn * xh, axis=0, keepdims=True)

        gqw = norm_bwd(q_ref, qw_ref, dqn_ref[...] * FOX_SCALE, dq_ref)
        gkw = norm_bwd(k_ref, kw_ref, dkn_ref[...], dk_ref)
        li = _lane_iota((tm, 128))
        f_lane = jnp.logical_and(li >= F_LANE, li < F_LANE + N_HEADS)
        dcum = -(_split3_dot(dc0_ref[...], fe_ref[...]) + _split3_dot(dc1_ref[...], fo_ref[...]))
        dlogf = _rev_cumsum_rows(dcum) + carry[...]
        carry[...] = dlogf[0:1, :]
        dfr = jnp.where(f_lane, dlogf * _sigmoid(-(small_ref[...] + sp_ref[3:4, :])), 0.0)
        dsmall_ref[...] = dfr
        gfb = jnp.sum(dfr, axis=0, keepdims=True)

        @pl.when(step == 0)
        def _():
            gqw_ref[...] = gqw
            gkw_ref[...] = gkw
            gfb_ref[...] = gfb

        @pl.when(step > 0)
        def _():
            gqw_ref[...] += gqw
            gkw_ref[...] += gkw
            gfb_ref[...] += gfb

    def rb(i):
        return nrow - 1 - i

    row = pl.BlockSpec((tm, 1024), lambda i: (rb(i), 0))
    vec = pl.BlockSpec((1, 1024), lambda i: (0, 0))
    fold = pl.BlockSpec((1024, 128), lambda i: (0, 0))
    return pl.pallas_call(
        body, name="fox_post", grid=(nrow,),
        in_specs=[row, row, row, row, pl.BlockSpec((tm, 1024), lambda i: (rb(i), Q_COL)),
                  pl.BlockSpec((tm, 1024), lambda i: (rb(i), K_COL)),
                  pl.BlockSpec((tm, 128), lambda i: (rb(i), SMALL_BLOCK)), pl.BlockSpec((8, 128), lambda i: (0, 0)), vec, vec,
                  fold, pl.BlockSpec((128, 1024), lambda i: (0, 0)), fold, fold],
        out_specs=[row, row, pl.BlockSpec((tm, 128), lambda i: (rb(i), 0)), vec, vec, pl.BlockSpec((1, 128), lambda i: (0, 0))],
        out_shape=[jax.ShapeDtypeStruct((s, 1024), BF16), jax.ShapeDtypeStruct((s, 1024), BF16),
                   jax.ShapeDtypeStruct((s, 128), F32), jax.ShapeDtypeStruct((1, 1024), F32),
                   jax.ShapeDtypeStruct((1, 1024), F32), jax.ShapeDtypeStruct((1, 128), F32)],
        scratch_shapes=[pltpu.VMEM((1, 128), F32)], compiler_params=_params(("arbitrary",)),
    )(dqn, dkn, dc0, dc1, proj, proj, proj, smallp, qw, kw, sel, selt, fold_even, fold_odd)


def local_step(x, target, wx, later_shards, ssd_cw8, ssd_cb, smallp, ssd_nw, qw_t, kw_t, sel, selt,
               norm_mix_w, norm_ffn_w, ffn_cw8, ffn_cb):
    proj, h_t = rms_in_proj(x, norm_mix_w, wx)
    y_ssd, y_ssd_t, ypre, states = ssd_fwd(proj, ssd_cw8, ssd_cb, smallp, ssd_nw)
    place_q, place_k, ones_q, ones_k, fold_even, fold_odd = fox_tables()
    qn, kn, aq, ak, vb, knt, akt, vt = fox_prep(proj, smallp, qw_t, kw_t, sel, selt, place_q, place_k, ones_q, ones_k)
    y_fox, y_fox_t, lse, (a_out, a_up, a_down) = fox_fwd(qn, kn, aq, ak, vt, shards=later_shards)
    w_out = a_out.reshape(2048, D_MODEL)
    w_down = a_down.reshape(D_FF, D_MODEL)
    s = x.shape[0]
    shard = lambda index: pl.BlockSpec((None, 1024, 1408), index)
    x1, hf, hf_t = out_proj_rms_fwd(y_ssd, y_fox, w_out, x, norm_ffn_w)
    hu, act, act_t = up_ffn_fwd(hf, a_up, ffn_cw8, ffn_cb)
    dy, sq = down_proj_loss(act, w_down, x1, target)

    dact = matmul(dy, w_down, mode="nt", tm=1024, tn=1408, tk=1024, out_dtype=F32, name="mm_dact")
    g_down = matmul(act_t, dy, mode="nn", tm=1408, tn=1024, tk=1024, out_dtype=BF16, name="mm_dw_down")
    dhu, gcw_g, gcw_v = ffn_mid_bwd(hu, dact, ffn_cw8, ffn_cb)
    dhf = matmul(dhu, a_up, mode="nt", tm=1024, tn=1024, tk=1408, out_dtype=F32, name="mm_dhf",
                 layout=dict(m=s, n=D_MODEL, k=2 * D_FF, a_spec=shard(lambda i, j, kk: (kk // 2, i, kk % 2)),
                             b_spec=shard(lambda i, j, kk: (kk, 0, 0))))
    g_up = matmul(hf_t, dhu, mode="nn", tm=1024, tn=1408, tk=1024, out_dtype=BF16, name="mm_dw_up",
                  layout=dict(m=D_MODEL, n=2 * D_FF, k=s, b_spec=shard(lambda i, j, kk: (j // 2, kk, j % 2)),
                              o_spec=shard(lambda i, j, kk: (j, i, 0)), out_shape=(4, D_MODEL, 1408)))
    dx1, g_norm_ffn, dmixed = rms_bwd_matmul(dhf, x1, norm_ffn_w, dy, w_out, name="rms_ffn_bwd_dmixed")
    g_out_a = matmul(y_ssd_t, dx1, mode="nn", tm=1024, tn=1024, tk=1024, out_dtype=BF16, name="mm_dw_out_ssd")
    g_out_b = matmul(y_fox_t, dx1, mode="nn", tm=1024, tn=1024, tk=1024, out_dtype=BF16, name="mm_dw_out_fox")
    early = [jnp.concatenate([g_out_a, g_out_b], axis=0).reshape(4, 512, D_MODEL), g_up, g_down.reshape(4, 704, D_MODEL)]
    dz, dxs, db, dc, dsmall_ssd, gcw_x, gcw_b, gcw_c, g_sp, g_ssd_nw, theirs = ssd_bwd(
        proj, ssd_cw8, ssd_cb, smallp, ssd_nw, ypre, states, dmixed, sel, swap=early)
    core = lax.axis_index("c").astype(jnp.int32).reshape(1)
    parts = [add_pair(a, b, core, name="add_pair_" + n, tr=ADAM_ROWS[n]) for a, b, n in zip(early, theirs, BIG_NAMES[1:])]
    dqn, dkn, dv, dc0, dc1, landed_early = fox_bwd(qn, kn, aq, ak, knt, akt, vb, lse, dmixed, parts=parts)
    dq, dk, dsmall_fox, g_qw, g_kw, g_fb = fox_post(dqn, dkn, dc0, dc1, proj, smallp, qw_t, kw_t, sel, selt,
                                                    fold_even, fold_odd)
    dproj = jnp.concatenate([dz, dxs, dq, dk, dv.astype(BF16), db, dc, (dsmall_ssd + dsmall_fox).astype(BF16)], axis=1)
    g_wx = matmul(h_t, dproj, mode="nn", tm=1024, tn=PROJ_TILE, tk=1024, out_dtype=BF16, name="mm_dw_in")
    g_in = _in_grad_shards(g_wx)
    part_in = add_pair(g_in, pair_swap_halves([g_in], name="pair_swap_w_in")[0], core, name="add_pair_w_in",
                       tr=ADAM_ROWS["w_in"])
    grad_x, g_norm_mix, landed_in = matmul_rms_bwd(dproj, wx, x, norm_mix_w, dx1, scatter=[part_in])
    return dict(
        sq=sq, grad_x=grad_x, landed=landed_in + landed_early,
        g_norm_mix=g_norm_mix, g_norm_ffn=g_norm_ffn, g_ssd_nw=g_ssd_nw,
        g_ssd_cw=jnp.concatenate([gcw_x, gcw_b, gcw_c], axis=1), g_sp=g_sp, g_fb=g_fb, g_qw=g_qw, g_kw=g_kw,
        g_ffn_cw=jnp.concatenate([gcw_g, gcw_v], axis=1))


def adamw(w, g, m, v, *, name, tr, allreduce=None):
    rows, cols = w.shape
    nsteps = rows // tr

    def body(*refs):
        if allreduce is None:
            w_ref, g_ref, m_ref, v_ref, d_ref, mo_ref, vo_ref = refs
        else:
            w_ref, g_ref, m_ref, v_ref, packed_ref, d_ref, mo_ref, vo_ref, summed_ref = refs[:9]
            start, finish = _allreduce_phases(packed_ref, summed_ref, *refs[9:])
            pl.when(pl.program_id(0) == 0)(start)
        gv = g_ref[...]
        mn = ADAM_B1 * m_ref[...] + (1.0 - ADAM_B1) * gv
        vn = ADAM_B2 * v_ref[...] + (1.0 - ADAM_B2) * (gv * gv)
        m_hat = mn / (1.0 - ADAM_B1 ** ADAM_STEP)
        v_hat = vn / (1.0 - ADAM_B2 ** ADAM_STEP)
        d_ref[...] = -ADAM_LR * (m_hat / (jnp.sqrt(v_hat) + ADAM_EPS) + ADAM_WD * w_ref[...])
        mo_ref[...] = mn
        vo_ref[...] = vn
        if allreduce is not None:
            pl.when(pl.program_id(0) == nsteps - 1)(finish)

    blk = pl.BlockSpec((tr, cols), lambda i: (i, 0))
    shp = jax.ShapeDtypeStruct((rows, cols), F32)
    if allreduce is None:
        return pl.pallas_call(
            body, name=name, grid=(nsteps,), in_specs=[blk] * 4, out_specs=[blk] * 3, out_shape=[shp] * 3,
            compiler_params=_params(("parallel",)),
        )(w, g, m, v)
    whole = pl.BlockSpec(memory_space=pltpu.VMEM)
    return pl.pallas_call(
        body, name=name, grid=(nsteps,), in_specs=[blk] * 4 + [whole], out_specs=[blk] * 3 + [whole],
        out_shape=[shp] * 3 + [jax.ShapeDtypeStruct(allreduce.shape, F32)],
        scratch_shapes=_allreduce_scratch(allreduce.shape[0]), compiler_params=_params(("arbitrary",)),
    )(w, g, m, v, allreduce)


def add_pair(full, theirs, core, *, name, tr):
    _, rows, cols = theirs.shape
    nblk = rows // tr

    def body(c_ref, a_ref, b_ref, o_ref):
        o_ref[...] = (a_ref[...].astype(F32) + b_ref[...].astype(F32)).astype(BF16)

    blk = pl.BlockSpec((1, tr, cols), lambda j, i, c: (j, i, 0))
    grid_spec = pltpu.PrefetchScalarGridSpec(
        num_scalar_prefetch=1, grid=(4, nblk),
        in_specs=[pl.BlockSpec((1, tr, cols), lambda j, i, c: (j, c[0] * nblk + i, 0)), blk], out_specs=blk)
    return pl.pallas_call(
        body, name=name, grid_spec=grid_spec, out_shape=jax.ShapeDtypeStruct(theirs.shape, BF16),
        compiler_params=_params(("parallel", "parallel")),
    )(core, full, theirs)


def sum_chips(parts, core, *, name, tr):
    _, rows, cols = parts.shape
    nblk = rows // tr

    def body(c_ref, p_ref, o_ref):
        acc = p_ref[0].astype(F32)
        for k in range(1, 4):
            acc = acc + p_ref[k].astype(F32)
        o_ref[...] = acc

    grid_spec = pltpu.PrefetchScalarGridSpec(
        num_scalar_prefetch=1, grid=(nblk,), in_specs=[pl.BlockSpec((4, tr, cols), lambda i, c: (0, i, 0))],
        out_specs=pl.BlockSpec((tr, cols), lambda i, c: (c[0] * nblk + i, 0)))
    return pl.pallas_call(
        body, name=name, grid_spec=grid_spec, out_shape=jax.ShapeDtypeStruct((2 * rows, cols), F32),
        compiler_params=_params(("parallel",)),
    )(core, parts)


ANY = pl.BlockSpec(memory_space=pl.ANY)


def _place():
    x, y, c = lax.axis_index("x"), lax.axis_index("y"), lax.axis_index("c")
    chips = [(1 - x, y), (x, 1 - y), (1 - x, 1 - y)]
    return x, y, c, chips


def _chunks(rows):
    size = next((c for c in (128, 176, 64, 32, 16, 8) if rows % c == 0), rows)
    return [(r, size) for r in range(0, rows, size)]


def gather_weights(shards):
    n = len(shards)

    def body(*refs):
        start, forward, finish = _gather_phases(refs[:n], refs[n:2 * n], *refs[2 * n:])
        start()
        forward()
        finish()

    gathered = pl.pallas_call(
        body, name="gather_weights", in_specs=[ANY] * n, out_specs=[ANY] * n,
        out_shape=_gather_out_shapes(shards), scratch_shapes=_gather_scratch(n),
    )(*shards)
    return gathered


def _gather_out_shapes(shards):
    return [jax.ShapeDtypeStruct((4,) + s.shape, s.dtype) for s in shards]


def _gather_scratch(n):
    return [pltpu.SemaphoreType.DMA((n, 7)), pltpu.SemaphoreType.DMA((n, 7))]


def _gather_phases(ins, outs, send_sems, recv_sems):
    n = len(ins)
    x, y, c, chips = _place()
    me = 2 * x + y
    sibling = (x, y, 1 - c)
    blks = [2 * cx + cy for cx, cy in chips]

    def half(a, blk, r=0, nr=None):
        rows = ins[a].shape[0] // 2
        return outs[a].at[blk, pl.ds(c * rows + r, rows if nr is None else nr), :]

    def to_chip(a, t, r=0, nr=None):
        rows = ins[a].shape[0] // 2
        return pltpu.make_async_remote_copy(
            src_ref=ins[a].at[pl.ds(c * rows + r, rows if nr is None else nr), :], dst_ref=half(a, me, r, nr),
            send_sem=send_sems.at[a, t], recv_sem=recv_sems.at[a, t], device_id=(*chips[t], c), device_id_type=MESH)

    def from_chip(a, t):
        return pltpu.make_async_remote_copy(
            src_ref=half(a, blks[t]), dst_ref=half(a, blks[t]), send_sem=send_sems.at[a, t], recv_sem=recv_sems.at[a, t],
            device_id=(*chips[t], c), device_id_type=MESH)

    def to_sibling(a, t, r=0, nr=None):
        return pltpu.make_async_remote_copy(
            src_ref=half(a, blks[t], r, nr), dst_ref=half(a, blks[t], r, nr), send_sem=send_sems.at[a, 3 + t],
            recv_sem=recv_sems.at[a, 3 + t], device_id=sibling, device_id_type=MESH)

    def from_sibling(a, t):
        rows = ins[a].shape[0] // 2
        dst = outs[a].at[blks[t], pl.ds((1 - c) * rows, rows), :]
        return pltpu.make_async_remote_copy(
            src_ref=dst, dst_ref=dst, send_sem=send_sems.at[a, 3 + t], recv_sem=recv_sems.at[a, 3 + t],
            device_id=sibling, device_id_type=MESH)

    def own(a, r=0, nr=None):
        return pltpu.make_async_remote_copy(
            src_ref=ins[a].at[pl.ds(r, ins[a].shape[0] if nr is None else nr), :],
            dst_ref=outs[a].at[me, pl.ds(r, ins[a].shape[0] if nr is None else nr), :],
            send_sem=send_sems.at[a, 6], recv_sem=recv_sems.at[a, 6], device_id=sibling, device_id_type=MESH)

    def start():
        for a in range(n):
            for t in range(3):
                for r, nr in _chunks(ins[a].shape[0] // 2):
                    to_chip(a, t, r, nr).start()
            for r, nr in _chunks(ins[a].shape[0]):
                own(a, r, nr).start()

    def forward():
        for a in range(n):
            for t in range(3):
                from_chip(a, t).wait_recv()
                for r, nr in _chunks(ins[a].shape[0] // 2):
                    to_sibling(a, t, r, nr).start()

    def finish():
        for a in range(n):
            for t in range(3):
                from_sibling(a, t).wait_recv()
        for a in range(n):
            for t in range(3):
                to_chip(a, t).wait_send()
                to_sibling(a, t).wait_send()
            own(a).wait()

    return start, forward, finish


def pair_swap_halves(grads, *, name):
    n = len(grads)

    def body(*refs):
        start, finish = _pair_swap_phases(refs[:n], refs[n:2 * n], *refs[2 * n:])
        start()
        finish()

    return pl.pallas_call(
        body, name=name, in_specs=[ANY] * n, out_specs=[ANY] * n, out_shape=_pair_swap_out_shapes(grads),
        scratch_shapes=_pair_swap_scratch(n),
    )(*grads)


def _pair_swap_out_shapes(grads):
    return [jax.ShapeDtypeStruct((4, g.shape[1] // 2, g.shape[2]), g.dtype) for g in grads]


def _pair_swap_scratch(n):
    return [pltpu.SemaphoreType.DMA((n,)), pltpu.SemaphoreType.DMA((n,))]


def _pair_swap_phases(ins, theirs, send_sems, recv_sems):
    n = len(ins)
    x, y, c, _ = _place()
    sibling = (x, y, 1 - c)

    def start():
        for a in range(n):
            rows = ins[a].shape[1] // 2
            for j in range(4):
                for r, nr in _chunks(rows):
                    pltpu.make_async_remote_copy(
                        src_ref=ins[a].at[j, pl.ds((1 - c) * rows + r, nr), :], dst_ref=theirs[a].at[j, pl.ds(r, nr), :],
                        send_sem=send_sems.at[a], recv_sem=recv_sems.at[a], device_id=sibling, device_id_type=MESH).start()

    def finish():
        for a in range(n):
            pltpu.make_async_remote_copy(src_ref=theirs[a], dst_ref=theirs[a], send_sem=send_sems.at[a],
                                         recv_sem=recv_sems.at[a], device_id=sibling, device_id_type=MESH).wait()

    return start, finish


def _scatter_scratch(n):
    return [pltpu.SemaphoreType.DMA((n, 3)), pltpu.SemaphoreType.DMA((n, 3))]


def _keep_own_blocks(landed, parts):
    if not parts:
        return []
    chip = 2 * lax.axis_index("x") + lax.axis_index("y")
    return [lax.dynamic_update_slice(l, lax.dynamic_slice_in_dim(p, chip, 1, axis=0), (chip, 0, 0))
            for l, p in zip(landed, parts)]


def _scatter_phases(ins, outs, send_sems, recv_sems):
    n = len(ins)
    x, y, c, chips = _place()
    me = 2 * x + y
    blks = [2 * cx + cy for cx, cy in chips]

    def start():
        for a in range(n):
            for r, nr in _chunks(ins[a].shape[1]):
                for t in range(3):
                    pltpu.make_async_remote_copy(
                        src_ref=ins[a].at[blks[t], pl.ds(r, nr), :], dst_ref=outs[a].at[me, pl.ds(r, nr), :],
                        send_sem=send_sems.at[a, t], recv_sem=recv_sems.at[a, t],
                        device_id=(*chips[t], c), device_id_type=MESH).start()

    def finish():
        for a in range(n):
            for t in range(3):
                pltpu.make_async_remote_copy(
                    src_ref=outs[a].at[blks[t]], dst_ref=outs[a].at[blks[t]], send_sem=send_sems.at[a, t],
                    recv_sem=recv_sems.at[a, t], device_id=(*chips[t], c), device_id_type=MESH).wait()

    return start, finish


def pair_join_halves(bufs):
    n = len(bufs)

    def body(*refs):
        outs = refs[n:2 * n]
        send_sems, recv_sems = refs[2 * n:]
        x, y, c, _ = _place()
        sibling = (x, y, 1 - c)
        for a in range(n):
            rows = outs[a].shape[0] // 2
            for r, nr in _chunks(rows):
                mine = outs[a].at[pl.ds(c * rows + r, nr), :]
                pltpu.make_async_remote_copy(src_ref=mine, dst_ref=mine, send_sem=send_sems.at[a], recv_sem=recv_sems.at[a],
                                             device_id=sibling, device_id_type=MESH).start()
        for a in range(n):
            rows = outs[a].shape[0] // 2
            pltpu.make_async_remote_copy(
                src_ref=outs[a].at[pl.ds(c * rows, rows), :], dst_ref=outs[a].at[pl.ds((1 - c) * rows, rows), :],
                send_sem=send_sems.at[a], recv_sem=recv_sems.at[a], device_id=sibling, device_id_type=MESH).wait()

    return pl.pallas_call(
        body, name="pair_join_halves", in_specs=[ANY] * n, out_specs=[ANY] * n,
        out_shape=[jax.ShapeDtypeStruct(b.shape, b.dtype) for b in bufs], input_output_aliases={a: a for a in range(n)},
        scratch_shapes=[pltpu.SemaphoreType.DMA((n,)), pltpu.SemaphoreType.DMA((n,))],
    )(*bufs)


def _allreduce_scratch(rows):
    return [pltpu.VMEM((8, rows, 128), F32), pltpu.SemaphoreType.DMA((7,)), pltpu.SemaphoreType.DMA((7,))]


def _allreduce_phases(in_ref, out_ref, gathered, send_sems, recv_sems):
    x, y, c, _ = _place()
    me = 4 * x + 2 * y + c
    flips = [(fx, fy, fc) for fx in (0, 1) for fy in (0, 1) for fc in (0, 1)][1:]
    peers = [((1 - x) if fx else x, (1 - y) if fy else y, (1 - c) if fc else c) for fx, fy, fc in flips]

    def send(t):
        return pltpu.make_async_remote_copy(
            src_ref=in_ref, dst_ref=gathered.at[me], send_sem=send_sems.at[t], recv_sem=recv_sems.at[t],
            device_id=peers[t], device_id_type=MESH)

    def start():
        gathered[me] = in_ref[...]
        for t in range(7):
            send(t).start()

    def finish():
        for t, (px, py, pc) in enumerate(peers):
            slot = gathered.at[4 * px + 2 * py + pc]
            pltpu.make_async_remote_copy(
                src_ref=slot, dst_ref=slot, send_sem=send_sems.at[t], recv_sem=recv_sems.at[t],
                device_id=(px, py, pc), device_id_type=MESH).wait_recv()
        for t in range(7):
            send(t).wait_send()
        acc = gathered[0]
        for k in range(1, 8):
            acc = acc + gathered[k]
        out_ref[...] = acc

    return start, finish


SMALL_NAMES = ("norm_mix_w", "ssd_conv_w", "ssd_conv_b", "ssd_dt_bias", "ssd_a_log", "ssd_d", "ssd_norm_w", "fox_f_bias",
               "fox_q_norm_w", "fox_k_norm_w", "norm_ffn_w", "ffn_conv_w", "ffn_conv_b")
BIG_NAMES = ("w_in", "w_out", "w_up", "w_down")
WEIGHT_ORDER = ("norm_mix_w", "w_in", "ssd_conv_w", "ssd_conv_b", "ssd_dt_bias", "ssd_a_log", "ssd_d", "ssd_norm_w",
                "fox_f_bias", "fox_q_norm_w", "fox_k_norm_w", "w_out", "norm_ffn_w", "w_up", "ffn_conv_w", "ffn_conv_b", "w_down")
ADAM_ROWS = {"w_in": 256, "w_out": 256, "w_up": 256, "w_down": 176}


def _pack(arrays):
    pieces = []
    for a in arrays:
        flat = a.reshape(-1).astype(F32)
        pieces += [flat, jnp.zeros(((-flat.shape[0]) % 1024,), F32)]
    return jnp.concatenate(pieces).reshape(-1, 128)


def _unpack(packed, shapes):
    out, r = [], 0
    for shp in shapes:
        size = 1
        for d in shp:
            size *= d
        nrow = 8 * (-(-size // 1024))
        out.append(packed[r:r + nrow].reshape(-1)[:size].reshape(shp))
        r += nrow
    return out


IN_SHARD = IN_COLS // 4
IN_SEGMENTS = ((0, 2048, 0), (2048, 2560, 5120), (2560, 2576, MAIN_COLS), (2576, 5648, 2048), (5648, 5664, MAIN_COLS + F_LANE))


def _in_cols(shards, lo, hi):
    out = []
    for j in range(4):
        a, b = max(lo, IN_SHARD * j), min(hi, IN_SHARD * (j + 1))
        if a < b:
            out.append(shards[j][:, a - IN_SHARD * j:b - IN_SHARD * j])
    return out


def _in_grad_shards(g):
    shards = []
    for j in range(4):
        pieces = []
        for lo, hi, at in IN_SEGMENTS:
            a, b = max(lo, IN_SHARD * j), min(hi, IN_SHARD * (j + 1))
            if a < b:
                pieces.append(g[:, at + a - lo:at + b - lo])
        shards.append(jnp.concatenate(pieces, axis=1))
    return jnp.stack(shards)


def _pad_rows(a, rows):
    return jnp.pad(a, ((0, rows - a.shape[0]), (0, 0)))


def kernel(x, norm_mix_w, w_in, ssd_conv_w, ssd_conv_b, ssd_dt_bias, ssd_a_log, ssd_d, ssd_norm_w, fox_f_bias, fox_q_norm_w, fox_k_norm_w, w_out, norm_ffn_w, w_up, ffn_conv_w, ffn_conv_b, w_down, loss_target, m_norm_mix_w, m_w_in, m_ssd_conv_w, m_ssd_conv_b, m_ssd_dt_bias, m_ssd_a_log, m_ssd_d, m_ssd_norm_w, m_fox_f_bias, m_fox_q_norm_w, m_fox_k_norm_w, m_w_out, m_norm_ffn_w, m_w_up, m_ffn_conv_w, m_ffn_conv_b, m_w_down, v_norm_mix_w, v_w_in, v_ssd_conv_w, v_ssd_conv_b, v_ssd_dt_bias, v_ssd_a_log, v_ssd_d, v_ssd_norm_w, v_fox_f_bias, v_fox_q_norm_w, v_fox_k_norm_w, v_w_out, v_norm_ffn_w, v_w_up, v_ffn_conv_w, v_ffn_conv_b, v_w_down):
    w = dict(norm_mix_w=norm_mix_w, w_in=w_in, ssd_conv_w=ssd_conv_w, ssd_conv_b=ssd_conv_b, ssd_dt_bias=ssd_dt_bias,
             ssd_a_log=ssd_a_log, ssd_d=ssd_d, ssd_norm_w=ssd_norm_w, fox_f_bias=fox_f_bias, fox_q_norm_w=fox_q_norm_w,
             fox_k_norm_w=fox_k_norm_w, w_out=w_out, norm_ffn_w=norm_ffn_w, w_up=w_up, ffn_conv_w=ffn_conv_w,
             ffn_conv_b=ffn_conv_b, w_down=w_down)
    m = dict(norm_mix_w=m_norm_mix_w, w_in=m_w_in, ssd_conv_w=m_ssd_conv_w, ssd_conv_b=m_ssd_conv_b, ssd_dt_bias=m_ssd_dt_bias,
             ssd_a_log=m_ssd_a_log, ssd_d=m_ssd_d, ssd_norm_w=m_ssd_norm_w, fox_f_bias=m_fox_f_bias, fox_q_norm_w=m_fox_q_norm_w,
             fox_k_norm_w=m_fox_k_norm_w, w_out=m_w_out, norm_ffn_w=m_norm_ffn_w, w_up=m_w_up, ffn_conv_w=m_ffn_conv_w,
             ffn_conv_b=m_ffn_conv_b, w_down=m_w_down)
    v = dict(norm_mix_w=v_norm_mix_w, w_in=v_w_in, ssd_conv_w=v_ssd_conv_w, ssd_conv_b=v_ssd_conv_b, ssd_dt_bias=v_ssd_dt_bias,
             ssd_a_log=v_ssd_a_log, ssd_d=v_ssd_d, ssd_norm_w=v_ssd_norm_w, fox_f_bias=v_fox_f_bias, fox_q_norm_w=v_fox_q_norm_w,
             fox_k_norm_w=v_fox_k_norm_w, w_out=v_w_out, norm_ffn_w=v_norm_ffn_w, w_up=v_w_up, ffn_conv_w=v_ffn_conv_w,
             ffn_conv_b=v_ffn_conv_b, w_down=v_w_down)
    chip = 2 * lax.axis_index("x") + lax.axis_index("y")

    a_in, a_scw, a_fcw = gather_weights([w_in[0].astype(BF16), _pad_rows(ssd_conv_w[0], 16), _pad_rows(ffn_conv_w[0], 16)])
    later_shards = [w_out[0].astype(BF16), w_up[0].astype(BF16), w_down[0].astype(BF16)]
    wx = jnp.concatenate([p for lo, hi, _ in sorted(IN_SEGMENTS, key=lambda seg: seg[2]) for p in _in_cols(a_in, lo, hi)]
                         + [jnp.zeros((D_MODEL, PROJ_COLS - IN_COLS), BF16)], axis=1)
    ssd_cw8 = a_scw.transpose(1, 0, 2).reshape(16, 1536)[:8]
    ffn_cw8 = a_fcw.transpose(1, 0, 2).reshape(16, 2 * D_FF)[:8]
    gap = lambda n: jnp.zeros((n,), F32)
    smallp = jnp.concatenate([ssd_dt_bias[0], gap(112), ssd_a_log[0], gap(112), ssd_d[0], gap(112),
                              gap(F_LANE), fox_f_bias[0], gap(128 - F_LANE - N_HEADS), gap(4 * 128)]).reshape(8, 128)
    qw_t = jnp.tile(fox_q_norm_w[0], N_HEADS)[None]
    kw_t = jnp.tile(fox_k_norm_w[0], N_HEADS)[None]
    sel = jnp.asarray((np.arange(1024)[:, None] // HEAD_DIM == np.arange(128)[None, :]).astype(np.float32), BF16)

    res = local_step(x[0], loss_target[0], wx, later_shards, ssd_cw8, ssd_conv_b, smallp, ssd_norm_w, qw_t, kw_t,
                     sel, sel.T, norm_mix_w, norm_ffn_w, ffn_cw8, ffn_conv_b)

    full_shapes = [(1, 1024), (1, 4, 1536), (1, 1536), (1, 16), (1, 16), (1, 16), (1, 1024), (1, 16), (1, 64), (1, 64),
                   (1, 1024), (1, 3, 2 * D_FF), (1, 2 * D_FF), (1,)]
    local_small = [res["g_norm_mix"], res["g_ssd_cw"][:4], res["g_ssd_cw"][4], res["g_sp"][0, :16], res["g_sp"][1, :16],
                   res["g_sp"][2, :16], res["g_ssd_nw"], res["g_fb"][0, F_LANE:F_LANE + 16],
                   res["g_qw"].reshape(N_HEADS, HEAD_DIM).sum(0), res["g_kw"].reshape(N_HEADS, HEAD_DIM).sum(0),
                   res["g_norm_ffn"], res["g_ffn_cw"][:3], res["g_ffn_cw"][3], jnp.sum(res["sq"])]
    landed = res["landed"]
    core = lax.axis_index("c").astype(jnp.int32).reshape(1)
    halves = [sum_chips(p, core, name="sum_chips_" + n, tr=ADAM_ROWS[n]) for p, n in zip(landed, BIG_NAMES)]
    g_big = dict(zip(BIG_NAMES, pair_join_halves(halves)))

    grads, deltas, new_m, new_v = {}, {}, {}, {}
    for n in BIG_NAMES:
        out = adamw(w[n][0], g_big[n], m[n][0], v[n][0], name="adamw_" + n, tr=ADAM_ROWS[n],
                    allreduce=_pack(local_small) if n == BIG_NAMES[0] else None)
        if n == BIG_NAMES[0]:
            summed = _unpack(out[3], full_shapes)
        d, mn, vn = out[:3]
        grads[n], deltas[n], new_m[n], new_v[n] = g_big[n][None], d[None], mn[None], vn[None]
    loss = (0.5 / D_MODEL) * summed[-1][0]
    g_small = dict(zip(SMALL_NAMES, summed[:-1]))
    g_small["ssd_conv_w"] = lax.dynamic_slice(g_small["ssd_conv_w"], (0, 0, 384 * chip), (1, 4, 384))
    g_small["ffn_conv_w"] = lax.dynamic_slice(g_small["ffn_conv_w"], (0, 0, 1408 * chip), (1, 3, 1408))
    shapes = [w[n].shape for n in SMALL_NAMES]
    packed_w = _pack([w[n] for n in SMALL_NAMES])
    d, mn, vn = adamw(packed_w, _pack([g_small[n] for n in SMALL_NAMES]), _pack([m[n] for n in SMALL_NAMES]),
                      _pack([v[n] for n in SMALL_NAMES]), name="adamw_small", tr=packed_w.shape[0])
    for n, dd, mm, vv in zip(SMALL_NAMES, _unpack(d, shapes), _unpack(mn, shapes), _unpack(vn, shapes)):
        grads[n], deltas[n], new_m[n], new_v[n] = g_small[n].reshape(w[n].shape), dd, mm, vv
    return (loss, res["grad_x"][None], *[grads[n] for n in WEIGHT_ORDER], *[deltas[n] for n in WEIGHT_ORDER],
            *[new_m[n] for n in WEIGHT_ORDER], *[new_v[n] for n in WEIGHT_ORDER])
```

```python
import functools

import jax
import jax.numpy as jnp
import numpy as np
from jax import lax
from jax.experimental import pallas as pl
from jax.experimental.pallas import tpu as pltpu

F32 = jnp.float32
BF16 = jnp.bfloat16
MESH = pl.DeviceIdType.MESH

D_MODEL = 1024
HEAD_DIM = 64
N_HEADS = 16
N_PAIRS = N_HEADS // 2
SSD_CHUNK = 128
SSD_STATE = 128
SSD_CONV = 4
D_FF = 2816
FFN_CONV = 3
NORM_EPS = 1e-6
MAIN_COLS = 5632
SMALL_COLS = 128
PROJ_COLS = MAIN_COLS + SMALL_COLS
SMALL_BLOCK = MAIN_COLS // SMALL_COLS
PROJ_TILE = 1152
F_LANE = 16
IN_COLS = 5664

ADAM_LR = 0.001
ADAM_B1 = 0.9
ADAM_B2 = 0.999
ADAM_EPS = 1e-08
ADAM_WD = 0.01
ADAM_STEP = 10

VMEM_LIMIT_V7X = 56 * 1024 * 1024
NEG_BIG = -1e30


def _params(sem=None):
    return pltpu.CompilerParams(dimension_semantics=sem, vmem_limit_bytes=VMEM_LIMIT_V7X)


def _sigmoid(x):
    return 1.0 / (1.0 + jnp.exp(-x))


def _silu_and_grad(x):
    s = _sigmoid(x)
    return x * s, s * (1.0 + x * (1.0 - s))


def _shift_down(v, j):
    return v if j == 0 else pltpu.roll(v, j, 0)


def _shift_up(v, j):
    return v if j == 0 else pltpu.roll(v, v.shape[0] - j, 0)


def _row_iota(shape):
    return lax.broadcasted_iota(jnp.int32, shape, 0)


def _lane_iota(shape):
    return lax.broadcasted_iota(jnp.int32, shape, 1)


def _dot(a, b, mode="nn"):
    dims = {"nn": (((1,), (0,)), ((), ())), "nt": (((1,), (1,)), ((), ())), "tn": (((0,), (0,)), ((), ()))}[mode]
    return lax.dot_general(a.astype(BF16), b.astype(BF16), dims, preferred_element_type=F32)


def _dot_f32(a, b):
    return jnp.dot(a, b, precision=lax.Precision.HIGHEST, preferred_element_type=F32)


def matmul(a, b, *, mode, tm, tn, tk, out_dtype, name, layout=None):
    layout = layout or {}
    if layout:
        m, n, k = layout["m"], layout["n"], layout["k"]
    else:
        (m, k), n = a.shape, (b.shape[1] if mode == "nn" else b.shape[0])
    assert m % tm == 0 and n % tn == 0 and k % tk == 0, (name, m, n, k, tm, tn, tk)
    nk = k // tk
    a_spec = layout.get("a_spec") or pl.BlockSpec((tm, tk), lambda i, j, kk: (i, kk))
    b_spec = layout.get("b_spec") or (pl.BlockSpec((tn, tk), lambda i, j, kk: (j, kk)) if mode == "nt"
                                      else pl.BlockSpec((tk, tn), lambda i, j, kk: (kk, j)))
    o_spec = layout.get("o_spec") or pl.BlockSpec((tm, tn), lambda i, j, kk: (i, j))

    def body(a_ref, b_ref, o_ref, acc_ref):
        kk = pl.program_id(2)
        part = _dot(a_ref[...], b_ref[...], mode)
        if nk == 1:
            o_ref[...] = part.astype(out_dtype)
        else:
            @pl.when(kk == 0)
            def _():
                acc_ref[...] = part

            @pl.when(jnp.logical_and(kk > 0, kk < nk - 1))
            def _():
                acc_ref[...] += part

            @pl.when(kk == nk - 1)
            def _():
                o_ref[...] = (acc_ref[...] + part).astype(out_dtype)

    return pl.pallas_call(
        body, name=name, grid=(m // tm, n // tn, nk), in_specs=[a_spec, b_spec], out_specs=o_spec,
        out_shape=jax.ShapeDtypeStruct(layout.get("out_shape", (m, n)), out_dtype),
        scratch_shapes=[pltpu.VMEM((tm, tn) if nk > 1 else (8, 128), F32)],
        compiler_params=_params(("parallel", "parallel", "arbitrary")),
    )(a, b)


def rms_in_proj(x, w, wx, *, tm=512):
    s, d = x.shape

    def body(x_ref, w_ref, wx_ref, proj_ref, ht_ref):
        for r in range(0, tm, UP_ROWS):
            rows = slice(r, r + UP_ROWS)
            xv = x_ref[rows, :]
            rstd = lax.rsqrt(jnp.mean(xv * xv, axis=-1, keepdims=True) + NORM_EPS)
            h = (xv * rstd) * w_ref[...]
            ht_ref[:, rows] = h.T.astype(BF16)
            proj_ref[rows, :] = _dot(h, wx_ref[...])

    return pl.pallas_call(
        body, name="rms_in_proj", grid=(s // tm,),
        in_specs=[pl.BlockSpec((tm, d), lambda i: (i, 0)), pl.BlockSpec((1, d), lambda i: (0, 0)),
                  pl.BlockSpec((d, PROJ_COLS), lambda i: (0, 0), pipeline_mode=pl.Buffered(1))],
        out_specs=[pl.BlockSpec((tm, PROJ_COLS), lambda i: (i, 0)), pl.BlockSpec((d, tm), lambda i: (0, i))],
        out_shape=[jax.ShapeDtypeStruct((s, PROJ_COLS), F32), jax.ShapeDtypeStruct((d, s), BF16)],
        compiler_params=_params(("parallel",)),
    )(x, w, wx)


def matmul_rms_bwd(dproj, wx, x, w, resid, *, scatter, tm=512):
    s, d = x.shape
    ns = len(scatter)
    nsteps = s // tm

    def body(*refs):
        a_ref, b_ref, x_ref, w_ref, res_ref = refs[:5]
        dx_ref, dw_ref = refs[5 + ns:7 + ns]
        step = pl.program_id(0)
        start, finish = _scatter_phases(refs[5:5 + ns], refs[7 + ns:7 + 2 * ns], *refs[7 + 2 * ns:])
        pl.when(step == 0)(start)
        part = jnp.zeros((1, d), F32)
        for r in range(0, tm, UP_ROWS):
            rows = slice(r, r + UP_ROWS)
            dhv = _dot(a_ref[rows, :], b_ref[...], "nt")
            xv = x_ref[rows, :]
            rstd = lax.rsqrt(jnp.mean(xv * xv, axis=-1, keepdims=True) + NORM_EPS)
            xh = xv * rstd
            g = dhv * w_ref[...]
            dx_ref[rows, :] = res_ref[rows, :] + rstd * (g - xh * jnp.mean(g * xh, axis=-1, keepdims=True))
            part = part + jnp.sum(dhv * xh, axis=0, keepdims=True)

        @pl.when(step == 0)
        def _():
            dw_ref[...] = part

        @pl.when(step > 0)
        def _():
            dw_ref[...] += part

        pl.when(step == nsteps - 1)(finish)

    row = pl.BlockSpec((tm, d), lambda i: (i, 0))
    vec = pl.BlockSpec((1, d), lambda i: (0, 0))
    outs = pl.pallas_call(
        body, name="mm_dh_rms_mix_bwd", grid=(nsteps,),
        in_specs=[pl.BlockSpec((tm, PROJ_COLS), lambda i: (i, 0)),
                  pl.BlockSpec((d, PROJ_COLS), lambda i: (0, 0), pipeline_mode=pl.Buffered(1)), row, vec, row] + [ANY] * ns,
        out_specs=[row, vec] + [ANY] * ns,
        out_shape=[jax.ShapeDtypeStruct((s, d), F32), jax.ShapeDtypeStruct((1, d), F32)]
        + [jax.ShapeDtypeStruct(p.shape, p.dtype) for p in scatter],
        scratch_shapes=_scatter_scratch(ns), compiler_params=_params(("arbitrary",)),
    )(dproj, wx, x, w, resid, *scatter)
    return outs[0], outs[1], _keep_own_blocks(outs[2:], scatter)


def out_proj_rms_fwd(y_ssd, y_fox, w_out, x, norm_w, *, tm=512):
    s, d = x.shape

    def body(ys_ref, yf_ref, w_ref, x_ref, nw_ref, x1_ref, h_ref, ht_ref):
        for r in range(0, tm, UP_ROWS):
            rows = slice(r, r + UP_ROWS)
            x1 = x_ref[rows, :] + _dot(ys_ref[rows, :], w_ref[0:d, :]) + _dot(yf_ref[rows, :], w_ref[d:2 * d, :])
            x1_ref[rows, :] = x1
            rstd = lax.rsqrt(jnp.mean(x1 * x1, axis=-1, keepdims=True) + NORM_EPS)
            h = (x1 * rstd) * nw_ref[...]
            h_ref[rows, :] = h.astype(BF16)
            ht_ref[:, rows] = h.T.astype(BF16)

    row = pl.BlockSpec((tm, d), lambda i: (i, 0))
    return pl.pallas_call(
        body, name="out_proj_rms_fwd", grid=(s // tm,),
        in_specs=[row, row, pl.BlockSpec((2 * d, d), lambda i: (0, 0)), row, pl.BlockSpec((1, d), lambda i: (0, 0))],
        out_specs=[row, row, pl.BlockSpec((d, tm), lambda i: (0, i))],
        out_shape=[jax.ShapeDtypeStruct((s, d), F32), jax.ShapeDtypeStruct((s, d), BF16), jax.ShapeDtypeStruct((d, s), BF16)],
        compiler_params=_params(("parallel",)),
    )(y_ssd, y_fox, w_out, x, norm_w)


def rms_bwd_matmul(dhu, a_up, x, w, resid, b, *, name, tm=512):
    s, d = x.shape
    n = b.shape[0]

    def body(dhu_ref, up_ref, x_ref, w_ref, res_ref, b_ref, dx_ref, dw_ref, prod_ref):
        part = jnp.zeros((1, d), F32)
        for r in range(0, tm, UP_ROWS):
            rows = slice(r, r + UP_ROWS)
            xv = x_ref[rows, :]
            dhv = functools.reduce(lambda p, q: p + q, [
                _dot(dhu_ref[k // 2, rows, (k % 2) * UP_SHARD:(k % 2 + 1) * UP_SHARD], up_ref[k], "nt") for k in range(4)])
            rstd = lax.rsqrt(jnp.mean(xv * xv, axis=-1, keepdims=True) + NORM_EPS)
            xh = xv * rstd
            g = dhv * w_ref[...]
            dx = res_ref[rows, :] + rstd * (g - xh * jnp.mean(g * xh, axis=-1, keepdims=True))
            dx_ref[rows, :] = dx
            prod_ref[rows, :] = _dot(dx, b_ref[...], "nt")
            part = part + jnp.sum(dhv * xh, axis=0, keepdims=True)

        @pl.when(pl.program_id(0) == 0)
        def _():
            dw_ref[...] = part

        @pl.when(pl.program_id(0) > 0)
        def _():
            dw_ref[...] += part

    row = pl.BlockSpec((tm, d), lambda i: (i, 0))
    vec = pl.BlockSpec((1, d), lambda i: (0, 0))
    once = pl.Buffered(1)
    return pl.pallas_call(
        body, name=name, grid=(s // tm,),
        in_specs=[pl.BlockSpec((2, tm, D_FF), lambda i: (0, i, 0)),
                  pl.BlockSpec((4, d, UP_SHARD), lambda i: (0, 0, 0), pipeline_mode=once), row, vec, row,
                  pl.BlockSpec((n, d), lambda i: (0, 0), pipeline_mode=once)],
        out_specs=[row, vec, pl.BlockSpec((tm, n), lambda i: (i, 0))],
        out_shape=[jax.ShapeDtypeStruct((s, d), F32), jax.ShapeDtypeStruct((1, d), F32), jax.ShapeDtypeStruct((s, n), F32)],
        compiler_params=_params(("arbitrary",)),
    )(dhu, a_up, x, w, resid, b)


def down_proj_loss(act, w_down, x1, target, *, tm=512):
    s, d = x1.shape

    def body(a_ref, w_ref, x1_ref, t_ref, dy_ref, sq_ref):
        part = jnp.zeros((1, d), F32)
        for r in range(0, tm, UP_ROWS):
            rows = slice(r, r + UP_ROWS)
            e = x1_ref[rows, :] + _dot(a_ref[rows, :], w_ref[...]) - t_ref[rows, :]
            dy_ref[rows, :] = e / float(d)
            part = part + jnp.sum(e * e, axis=0, keepdims=True)

        @pl.when(pl.program_id(0) == 0)
        def _():
            sq_ref[...] = part

        @pl.when(pl.program_id(0) > 0)
        def _():
            sq_ref[...] += part

    row = pl.BlockSpec((tm, d), lambda i: (i, 0))
    vec = pl.BlockSpec((1, d), lambda i: (0, 0))
    return pl.pallas_call(
        body, name="down_proj_loss", grid=(s // tm,),
        in_specs=[pl.BlockSpec((tm, D_FF), lambda i: (i, 0)), pl.BlockSpec((D_FF, d), lambda i: (0, 0)), row, row],
        out_specs=[row, vec], out_shape=[jax.ShapeDtypeStruct((s, d), F32), jax.ShapeDtypeStruct((1, d), F32)],
        compiler_params=_params(("arbitrary",)),
    )(act, w_down, x1, target)


def _row_shifts(ext, k_taps):
    return [_shift_down(ext, j) for j in range(k_taps)]


def _conv_rows(shifts, w):
    k_taps = len(shifts)
    acc = w[k_taps - 1:k_taps, :] * shifts[0]
    for k in range(k_taps - 1):
        acc = acc + w[k:k + 1, :] * shifts[k_taps - 1 - k]
    return acc


def _conv_weight_grad(dcur, shifts, rows, width):
    k_taps = len(shifts)
    out = [jnp.sum(dcur * shifts[k_taps - 1 - k][rows], axis=0, keepdims=True) for k in range(k_taps)]
    out.append(jnp.sum(dcur, axis=0, keepdims=True))
    return _stack_rows(out, width)


def _conv_rows_transposed(dext, w, k_taps):
    acc = w[k_taps - 1:k_taps, :] * dext
    for k in range(k_taps - 1):
        acc = acc + w[k:k + 1, :] * _shift_up(dext, k_taps - 1 - k)
    return acc


def _stack_rows(rows, width):
    ri = _row_iota((8, width))
    out = jnp.zeros((8, width), F32)
    for k, r in enumerate(rows):
        out = out + jnp.where(ri == k, r, 0.0)
    return out


UP_SHARD = 1408
UP_ROWS = 256


def up_ffn_fwd(hf, a_up, conv_w8, conv_b, *, tm=512):
    s = hf.shape[0]

    def body(a_ref, bg_ref, bv_ref, wg_ref, wv_ref, cbg_ref, cbv_ref, hu_ref, act_ref, actt_ref, carry):
        i, j = pl.program_id(0), pl.program_id(1)
        prev_g = jnp.where(i == 0, 0.0, carry[0, j])
        prev_v = jnp.where(i == 0, 0.0, carry[1, j])
        for r in range(0, tm, UP_ROWS):
            rows = slice(r, r + UP_ROWS)
            a = a_ref[rows, :]
            hg, hv = _dot(a, bg_ref[...]), _dot(a, bv_ref[...])
            hu_ref[0, rows, :] = hg
            hu_ref[1, rows, :] = hv
            gc = _conv_rows(_row_shifts(jnp.concatenate([prev_g, hg], axis=0), FFN_CONV), wg_ref[...])[8:] + cbg_ref[...]
            vc = _conv_rows(_row_shifts(jnp.concatenate([prev_v, hv], axis=0), FFN_CONV), wv_ref[...])[8:] + cbv_ref[...]
            act = gc * _sigmoid(gc) * vc
            act_ref[rows, :] = act.astype(BF16)
            actt_ref[:, rows] = act.T.astype(BF16)
            prev_g, prev_v = hg[UP_ROWS - 8:], hv[UP_ROWS - 8:]
        carry[0, j] = prev_g
        carry[1, j] = prev_v

    shard = lambda off: pl.BlockSpec((None, D_MODEL, UP_SHARD), lambda i, j: (j + off, 0, 0))
    taps = lambda off: pl.BlockSpec((8, UP_SHARD), lambda i, j: (0, j + off))
    bias = lambda off: pl.BlockSpec((1, UP_SHARD), lambda i, j: (0, j + off))
    return pl.pallas_call(
        body, name="up_ffn_fwd", grid=(s // tm, 2),
        in_specs=[pl.BlockSpec((tm, D_MODEL), lambda i, j: (i, 0)), shard(0), shard(2), taps(0), taps(2), bias(0), bias(2)],
        out_specs=[pl.BlockSpec((2, tm, UP_SHARD), lambda i, j: (0, i, j)), pl.BlockSpec((tm, UP_SHARD), lambda i, j: (i, j)),
                   pl.BlockSpec((UP_SHARD, tm), lambda i, j: (j, i))],
        out_shape=[jax.ShapeDtypeStruct((2, s, D_FF), F32), jax.ShapeDtypeStruct((s, D_FF), BF16),
                   jax.ShapeDtypeStruct((D_FF, s), BF16)],
        scratch_shapes=[pltpu.VMEM((2, 2, 8, UP_SHARD), F32)], compiler_params=_params(("arbitrary", "arbitrary")),
    )(hf, a_up, a_up, conv_w8, conv_w8, conv_b, conv_b)


def ffn_mid_bwd(hu, dact, conv_w8, conv_b, *, tm=1024, tc=256):
    s = hu.shape[1]
    ncol = D_FF // tc
    nrow = s // tm
    r8 = tm // 8

    def body(g_ref, v_ref, gp_ref, vp_ref, gn_ref, vn_ref, da_ref, dan_ref, wg_ref, wv_ref, bg_ref, bv_ref,
             dhu_ref, wgo_ref, wvo_ref):
        i = pl.program_id(1)
        first = i == 0
        last = i == nrow - 1

        def ext_of(cur_ref, prev_ref, next_ref):
            prev = jnp.where(first, 0.0, prev_ref[...])
            return jnp.concatenate([prev, cur_ref[...], next_ref[...]], axis=0)

        g_sh = _row_shifts(ext_of(g_ref, gp_ref, gn_ref), FFN_CONV)
        v_sh = _row_shifts(ext_of(v_ref, vp_ref, vn_ref), FFN_CONV)
        gc = _conv_rows(g_sh, wg_ref[...]) + bg_ref[...]
        vc = _conv_rows(v_sh, wv_ref[...]) + bv_ref[...]
        da_ext = jnp.concatenate([jnp.zeros((8, tc), F32), da_ref[...], jnp.where(last, 0.0, dan_ref[...])], axis=0)
        silu, dsilu = _silu_and_grad(gc)
        dgc = da_ext * vc * dsilu
        dvc = da_ext * silu
        dhu_ref[0] = _conv_rows_transposed(dgc, wg_ref[...], FFN_CONV)[8:8 + tm].astype(BF16)
        dhu_ref[1] = _conv_rows_transposed(dvc, wv_ref[...], FFN_CONV)[8:8 + tm].astype(BF16)

        cur = slice(8, 8 + tm)
        pg = _conv_weight_grad(dgc[cur], g_sh, cur, tc)
        pv = _conv_weight_grad(dvc[cur], v_sh, cur, tc)

        @pl.when(first)
        def _():
            wgo_ref[...] = pg
            wvo_ref[...] = pv

        @pl.when(i > 0)
        def _():
            wgo_ref[...] += pg
            wvo_ref[...] += pv

    def prev_idx(i):
        return jnp.maximum(i * r8 - 1, 0)

    def next_idx(i):
        return jnp.minimum((i + 1) * r8, s // 8 - 1)

    half = lambda k, rows, row_index: pl.BlockSpec((None, rows, tc), lambda j, i: (k, row_index(i), j))
    in_specs = [
        half(0, tm, lambda i: i), half(1, tm, lambda i: i),
        half(0, 8, prev_idx), half(1, 8, prev_idx),
        half(0, 8, next_idx), half(1, 8, next_idx),
        pl.BlockSpec((tm, tc), lambda j, i: (i, j)),
        pl.BlockSpec((8, tc), lambda j, i: (next_idx(i), j)),
        pl.BlockSpec((8, tc), lambda j, i: (0, j)),
        pl.BlockSpec((8, tc), lambda j, i: (0, j + ncol)),
        pl.BlockSpec((1, tc), lambda j, i: (0, j)),
        pl.BlockSpec((1, tc), lambda j, i: (0, j + ncol)),
    ]
    out_specs = [pl.BlockSpec((2, tm, tc), lambda j, i: (0, i, j)), pl.BlockSpec((8, tc), lambda j, i: (0, j)),
                 pl.BlockSpec((8, tc), lambda j, i: (0, j))]
    out_shape = [jax.ShapeDtypeStruct((2, s, D_FF), BF16),
                 jax.ShapeDtypeStruct((8, D_FF), F32), jax.ShapeDtypeStruct((8, D_FF), F32)]
    return pl.pallas_call(
        body, name="ffn_mid_bwd", grid=(ncol, nrow), in_specs=in_specs, out_specs=out_specs, out_shape=out_shape,
        compiler_params=_params(("parallel", "arbitrary")),
    )(hu, hu, hu, hu, hu, hu, dact, dact, conv_w8, conv_w8, conv_b, conv_b)


def _softplus(x):
    return jnp.maximum(x, 0.0) + jnp.log(1.0 + jnp.exp(-jnp.abs(x)))


def _cumsum_rows(v):
    n = v.shape[0]
    ri = _row_iota(v.shape)
    sh = 1
    while sh < n:
        v = v + jnp.where(ri >= sh, _shift_down(v, sh), 0.0)
        sh *= 2
    return v


def _rev_cumsum_rows(v):
    n = v.shape[0]
    ri = _row_iota(v.shape)
    sh = 1
    while sh < n:
        v = v + jnp.where(ri < n - sh, _shift_up(v, sh), 0.0)
        sh *= 2
    return v


def _total(v):
    return jnp.sum(jnp.sum(v, axis=1, keepdims=True), axis=0, keepdims=True)


def _ssd_in_specs(rev_nc=None):
    def ch(c):
        return c if rev_nc is None else rev_nc - 1 - c

    def prev(c):
        return jnp.maximum(ch(c) * (SSD_CHUNK // 8) - 1, 0)

    L = SSD_CHUNK
    return [
        pl.BlockSpec((L, 1024), lambda c: (ch(c), 0)),
        pl.BlockSpec((L, 1024), lambda c: (ch(c), 1)),
        pl.BlockSpec((L, 256), lambda c: (ch(c), 20)),
        pl.BlockSpec((L, 256), lambda c: (ch(c), 21)),
        pl.BlockSpec((8, 1024), lambda c: (prev(c), 1)),
        pl.BlockSpec((8, 256), lambda c: (prev(c), 20)),
        pl.BlockSpec((8, 256), lambda c: (prev(c), 21)),
        pl.BlockSpec((8, 1024), lambda c: (0, 0)),
        pl.BlockSpec((8, 256), lambda c: (0, 4)),
        pl.BlockSpec((8, 256), lambda c: (0, 5)),
        pl.BlockSpec((1, 1024), lambda c: (0, 0)),
        pl.BlockSpec((1, 256), lambda c: (0, 4)),
        pl.BlockSpec((1, 256), lambda c: (0, 5)),
        pl.BlockSpec((L, SMALL_COLS), lambda c: (ch(c), SMALL_BLOCK)),
        pl.BlockSpec((8, 128), lambda c: (0, 0)),
        pl.BlockSpec((1, 1024), lambda c: (0, 0)),
    ]


def _ssd_conv_pre(cur_ref, prev_ref, w_ref, b_ref, first):
    prev = jnp.where(first, 0.0, prev_ref[...])
    shifts = _row_shifts(jnp.concatenate([prev, cur_ref[...]], axis=0), SSD_CONV)
    return shifts, _conv_rows(shifts, w_ref[...])[8:] + b_ref[...]


def _ssd_time_consts(small_ref, sp_ref):
    dt_pre = small_ref[...] + sp_ref[0:1, :]
    dt = _softplus(dt_pre)
    a = -jnp.exp(sp_ref[1:2, :])
    acs = _cumsum_rows(dt * a)
    return dt_pre, dt, a, acs


def ssd_fwd(proj, conv_w8, conv_b, smallp, norm_w):
    s = proj.shape[0]
    nc = s // SSD_CHUNK
    L = SSD_CHUNK

    def body(z_ref, xs_ref, b_ref, c_ref, xsp_ref, bp_ref, cp_ref, wx_ref, wb_ref, wc_ref, bx_ref, bb_ref, bc_ref,
             small_ref, sp_ref, nw_ref, y_ref, yt_ref, ypre_ref, st_ref, state):
        first = pl.program_id(0) == 0

        @pl.when(first)
        def _():
            state[...] = jnp.zeros_like(state)

        xs = _ssd_conv_pre(xs_ref, xsp_ref, wx_ref, bx_ref, first)[1]
        xs = xs * _sigmoid(xs)
        bm = _ssd_conv_pre(b_ref, bp_ref, wb_ref, bb_ref, first)[1]
        bm = bm * _sigmoid(bm)
        cm = _ssd_conv_pre(c_ref, cp_ref, wc_ref, bc_ref, first)[1]
        cm = cm * _sigmoid(cm)
        _, dt, _, acs = _ssd_time_consts(small_ref, sp_ref)
        acs_t = acs.T
        li = _lane_iota((L, L))
        ri = _row_iota((L, L))
        tri = ri >= li
        lo = li < HEAD_DIM
        st_ref[0] = state[...]
        for g in range(2):
            bg = bm[:, 128 * g:128 * g + 128]
            cg = cm[:, 128 * g:128 * g + 128]
            gmat = _dot(cg, bg, "nt")
            for pp in range(4):
                p = 4 * g + pp
                h0, h1 = 2 * p, 2 * p + 1
                x = xs[:, 128 * p:128 * p + 128]
                a0, a1 = acs[:, h0:h0 + 1], acs[:, h1:h1 + 1]
                xdt = x * jnp.where(lo, dt[:, h0:h0 + 1], dt[:, h1:h1 + 1])
                m0 = gmat * jnp.exp(jnp.where(tri, a0 - acs_t[h0:h0 + 1, :], NEG_BIG))
                m1 = gmat * jnp.exp(jnp.where(tri, a1 - acs_t[h1:h1 + 1, :], NEG_BIG))
                yd = _dot(m0, jnp.where(lo, xdt, 0.0)) + _dot(m1, jnp.where(lo, 0.0, xdt))
                hin = state[p]
                yo = _dot(cg, hin, "nt") * jnp.exp(jnp.where(lo, a0, a1))
                dskip = jnp.where(lo[0:1], sp_ref[2:3, h0:h0 + 1], sp_ref[2:3, h1:h1 + 1])
                ypre_ref[:, 128 * p:128 * p + 128] = yd + yo + dskip * x
                al0, al1 = acs[L - 1:L, h0:h0 + 1], acs[L - 1:L, h1:h1 + 1]
                w = jnp.exp(jnp.where(lo, al0 - a0, al1 - a1))
                dec = jnp.exp(jnp.where(ri < HEAD_DIM, al0, al1))
                state[p] = dec * hin + _dot(xdt * w, bg, "tn")
        z = z_ref[...]
        yg = ypre_ref[...] * (z * _sigmoid(z))
        for g in range(2):
            seg = yg[:, 512 * g:512 * g + 512]
            r = lax.rsqrt(jnp.mean(seg * seg, axis=-1, keepdims=True) + NORM_EPS)
            out = (seg * r) * nw_ref[:, 512 * g:512 * g + 512]
            y_ref[:, 512 * g:512 * g + 512] = out.astype(BF16)
            yt_ref[512 * g:512 * g + 512, :] = out.T.astype(BF16)

    row = pl.BlockSpec((L, 1024), lambda c: (c, 0))
    return pl.pallas_call(
        body, name="ssd_fwd", grid=(nc,), in_specs=_ssd_in_specs(),
        out_specs=[row, pl.BlockSpec((1024, L), lambda c: (0, c)), row,
                   pl.BlockSpec((1, N_PAIRS, 128, 128), lambda c: (c, 0, 0, 0))],
        out_shape=[jax.ShapeDtypeStruct((s, 1024), BF16), jax.ShapeDtypeStruct((1024, s), BF16),
                   jax.ShapeDtypeStruct((s, 1024), F32), jax.ShapeDtypeStruct((nc, N_PAIRS, 128, 128), F32)],
        scratch_shapes=[pltpu.VMEM((N_PAIRS, 128, 128), F32)],
        compiler_params=_params(("arbitrary",)),
    )(proj, proj, proj, proj, proj, proj, proj, conv_w8, conv_w8, conv_w8, conv_b, conv_b, conv_b, proj, smallp, norm_w)


def ssd_bwd(proj, conv_w8, conv_b, smallp, norm_w, ypre, states, dy, sel, swap=()):
    s = proj.shape[0]
    nc = s // SSD_CHUNK
    L = SSD_CHUNK

    ns = len(swap)
    n_in, n_out, n_scratch = 20, 10, 11

    def body(*refs):
        own = refs[:n_in] + refs[n_in + ns:n_in + ns + n_out] + refs[n_in + 2 * ns + n_out:n_in + 2 * ns + n_out + n_scratch]
        if ns:
            start, finish = _pair_swap_phases(refs[n_in:n_in + ns], refs[n_in + ns + n_out:n_in + 2 * ns + n_out],
                                              *refs[n_in + 2 * ns + n_out + n_scratch:])
            pl.when(pl.program_id(0) == 0)(start)
        compute(*own)
        if ns:
            pl.when(pl.program_id(0) == nc - 1)(finish)

    def compute(z_ref, xs_ref, b_ref, c_ref, xsp_ref, bp_ref, cp_ref, wx_ref, wb_ref, wc_ref, bx_ref, bb_ref, bc_ref,
                small_ref, sp_ref, nw_ref, ypre_ref, st_ref, dy_ref, sel_ref,
                dz_ref, dxs_ref, db_ref, dc_ref, dsmall_ref, gwx_ref, gwb_ref, gwc_ref, gsp_ref, gnw_ref,
                dstate, carry_x, carry_b, carry_c, dxs_buf, dbm_buf, dcm_buf, qcs, col_sums, acs_terms, dt_terms):
        step = pl.program_id(0)
        col_sums[...] = jnp.zeros_like(col_sums)
        first_chunk = step == nc - 1
        start = step == 0

        @pl.when(start)
        def _():
            dstate[...] = jnp.zeros_like(dstate)
            carry_x[...] = jnp.zeros_like(carry_x)
            carry_b[...] = jnp.zeros_like(carry_b)
            carry_c[...] = jnp.zeros_like(carry_c)

        xs_sh, xs_pre = _ssd_conv_pre(xs_ref, xsp_ref, wx_ref, bx_ref, first_chunk)
        b_sh, b_pre = _ssd_conv_pre(b_ref, bp_ref, wb_ref, bb_ref, first_chunk)
        c_sh, c_pre = _ssd_conv_pre(c_ref, cp_ref, wc_ref, bc_ref, first_chunk)
        xs, xs_ds = _silu_and_grad(xs_pre)
        bm, b_ds = _silu_and_grad(b_pre)
        cm, c_ds = _silu_and_grad(c_pre)
        dt_pre, dt, a, acs = _ssd_time_consts(small_ref, sp_ref)
        acs_t = acs.T
        li = _lane_iota((L, L))
        ri = _row_iota((L, L))
        tri = ri >= li
        lo = li < HEAD_DIM
        lo_rows = ri < HEAD_DIM
        li1 = _lane_iota((1, L))

        z = z_ref[...]
        sz, dsz = _silu_and_grad(z)
        y = ypre_ref[...]
        yg = y * sz
        dout = dy_ref[...]
        dyg_parts = []
        gnw_parts = []
        for g in range(2):
            sl = slice(512 * g, 512 * g + 512)
            seg = yg[:, sl]
            r = lax.rsqrt(jnp.mean(seg * seg, axis=-1, keepdims=True) + NORM_EPS)
            n = seg * r
            gnw_parts.append(jnp.sum(dout[:, sl] * n, axis=0, keepdims=True))
            gg = dout[:, sl] * nw_ref[:, sl]
            dyg_parts.append(r * (gg - n * jnp.mean(gg * n, axis=-1, keepdims=True)))
        dyg = jnp.concatenate(dyg_parts, axis=1)
        gnw = jnp.concatenate(gnw_parts, axis=1)
        dz_ref[...] = (dyg * y * dsz).astype(BF16)
        dypre = dyg * sz

        qcs[...] = jnp.zeros_like(qcs)
        dalast = jnp.zeros((1, L), F32)
        for g in range(2):
            bg = bm[:, 128 * g:128 * g + 128]
            cg = cm[:, 128 * g:128 * g + 128]
            gmat = _dot(cg, bg, "nt")
            dgmat = jnp.zeros((L, L), F32)
            dbg = jnp.zeros((L, L), F32)
            dcg = jnp.zeros((L, L), F32)
            for pp in range(4):
                p = 4 * g + pp
                h0, h1 = 2 * p, 2 * p + 1
                lanes = slice(128 * p, 128 * p + 128)
                x = xs[:, lanes]
                dyp = dypre[:, lanes]
                a0, a1 = acs[:, h0:h0 + 1], acs[:, h1:h1 + 1]
                dtl = jnp.where(lo, dt[:, h0:h0 + 1], dt[:, h1:h1 + 1])
                xdt = x * dtl
                l0 = jnp.exp(jnp.where(tri, a0 - acs_t[h0:h0 + 1, :], NEG_BIG))
                l1 = jnp.exp(jnp.where(tri, a1 - acs_t[h1:h1 + 1, :], NEG_BIG))
                m0, m1 = gmat * l0, gmat * l1
                dskip = jnp.where(lo[0:1], sp_ref[2:3, h0:h0 + 1], sp_ref[2:3, h1:h1 + 1])
                col_sums[0:1, lanes] = jnp.sum(dyp * x, axis=0, keepdims=True)
                dx = dyp * dskip
                dy0, dy1 = jnp.where(lo, dyp, 0.0), jnp.where(lo, 0.0, dyp)
                x0, x1 = jnp.where(lo, xdt, 0.0), jnp.where(lo, 0.0, xdt)
                dm0, dm1 = _dot(dy0, x0, "nt"), _dot(dy1, x1, "nt")
                dxdt = _dot(m0, dy0, "tn") + _dot(m1, dy1, "tn")
                q0, q1 = dm0 * m0, dm1 * m1
                qcs[h0:h0 + 1, :] = jnp.sum(q0, axis=0, keepdims=True)
                qcs[h1:h1 + 1, :] = jnp.sum(q1, axis=0, keepdims=True)
                row_terms = jnp.where(lo, q0 + pltpu.roll(q0, HEAD_DIM, 1), q1 + pltpu.roll(q1, HEAD_DIM, 1))
                dgmat = dgmat + dm0 * l0 + dm1 * l1
                hin = st_ref[0, p]
                e = jnp.exp(jnp.where(lo, a0, a1))
                ch = _dot(cg, hin, "nt")
                dch = dyp * e
                dcg = dcg + _dot(dch, hin)
                dhin = _dot(dch, cg, "tn")
                dhout = dstate[p]
                al0, al1 = acs[L - 1:L, h0:h0 + 1], acs[L - 1:L, h1:h1 + 1]
                dec = jnp.exp(jnp.where(lo_rows, al0, al1))
                dhin = dhin + dec * dhout
                dal = dhout * hin * dec
                dal0 = _total(jnp.where(lo_rows, dal, 0.0))
                dal1 = _total(dal) - dal0
                dalast = dalast + jnp.where(li1 == h0, dal0, 0.0) + jnp.where(li1 == h1, dal1, 0.0)
                w = jnp.exp(jnp.where(lo, al0 - a0, al1 - a1))
                xw = xdt * w
                dxw = _dot(bg, dhout, "nt")
                dbg = dbg + _dot(xw, dhout)
                dxdt = dxdt + dxw * w
                dww = dxw * xw
                col_sums[1:2, lanes] = jnp.sum(dww, axis=0, keepdims=True)
                acs_terms[:, lanes] = row_terms + dch * ch - dww
                dx = dx + dxdt * dtl
                dt_terms[:, lanes] = dxdt * x
                dxs_buf[:, lanes] = dx
                dstate[p] = dhin
            dcg = dcg + _dot(dgmat, bg)
            dbg = dbg + _dot(dgmat, cg, "tn")
            dbm_buf[:, 128 * g:128 * g + 128] = dbg
            dcm_buf[:, 128 * g:128 * g + 128] = dcg

        head_sums = _split3_dot(col_sums[...], sel_ref[...])
        dskip_g = head_sums[0:1, :]
        dalast = dalast + head_sums[1:2, :]
        ddt = _split3_dot(dt_terms[...], sel_ref[...])
        dacs_tot = _split3_dot(acs_terms[...], sel_ref[...]) - qcs[...].T + jnp.where(ri == L - 1, dalast, 0.0)
        dstep = _rev_cumsum_rows(dacs_tot)
        ddt = ddt + dstep * a
        head_lane = li < N_HEADS
        ddt_pre = jnp.where(head_lane, ddt * _sigmoid(dt_pre), 0.0)
        dsmall_ref[...] = ddt_pre
        da = jnp.sum(jnp.where(head_lane, dstep * dt, 0.0), axis=0, keepdims=True)
        gsp = _stack_rows([jnp.sum(ddt_pre, axis=0, keepdims=True), da * a, dskip_g], L)

        def conv_back(dpost, ds, shifts, w_ref, carry, out_ref, width):
            dpre = dpost * ds
            dext = jnp.concatenate([dpre, carry[...]], axis=0)
            out_ref[...] = _conv_rows_transposed(dext, w_ref[...], SSD_CONV)[:L].astype(BF16)
            carry[...] = dpre[0:8]
            return _conv_weight_grad(dpre, shifts, slice(8, 8 + L), width)

        gwx = conv_back(dxs_buf[...], xs_ds, xs_sh, wx_ref, carry_x, dxs_ref, 1024)
        gwb = conv_back(dbm_buf[...], b_ds, b_sh, wb_ref, carry_b, db_ref, 256)
        gwc = conv_back(dcm_buf[...], c_ds, c_sh, wc_ref, carry_c, dc_ref, 256)

        @pl.when(start)
        def _():
            gwx_ref[...] = gwx
            gwb_ref[...] = gwb
            gwc_ref[...] = gwc
            gsp_ref[...] = gsp
            gnw_ref[...] = gnw

        @pl.when(step > 0)
        def _():
            gwx_ref[...] += gwx
            gwb_ref[...] += gwb
            gwc_ref[...] += gwc
            gsp_ref[...] += gsp
            gnw_ref[...] += gnw

    def ch(c):
        return nc - 1 - c

    row = pl.BlockSpec((L, 1024), lambda c: (ch(c), 0))
    row256 = pl.BlockSpec((L, 256), lambda c: (ch(c), 0))
    in_specs = _ssd_in_specs(rev_nc=nc) + [row, pl.BlockSpec((1, N_PAIRS, 128, 128), lambda c: (ch(c), 0, 0, 0)), row,
                                           pl.BlockSpec((1024, 128), lambda c: (0, 0))]
    out_specs = [row, row, row256, row256, pl.BlockSpec((L, 128), lambda c: (ch(c), 0)),
                 pl.BlockSpec((8, 1024), lambda c: (0, 0)), pl.BlockSpec((8, 256), lambda c: (0, 0)),
                 pl.BlockSpec((8, 256), lambda c: (0, 0)), pl.BlockSpec((8, 128), lambda c: (0, 0)),
                 pl.BlockSpec((1, 1024), lambda c: (0, 0))]
    out_shape = [jax.ShapeDtypeStruct((s, 1024), BF16), jax.ShapeDtypeStruct((s, 1024), BF16),
                 jax.ShapeDtypeStruct((s, 256), BF16), jax.ShapeDtypeStruct((s, 256), BF16),
                 jax.ShapeDtypeStruct((s, 128), F32),
                 jax.ShapeDtypeStruct((8, 1024), F32), jax.ShapeDtypeStruct((8, 256), F32),
                 jax.ShapeDtypeStruct((8, 256), F32), jax.ShapeDtypeStruct((8, 128), F32),
                 jax.ShapeDtypeStruct((1, 1024), F32)]
    scratch = [pltpu.VMEM((N_PAIRS, 128, 128), F32), pltpu.VMEM((8, 1024), F32), pltpu.VMEM((8, 256), F32),
               pltpu.VMEM((8, 256), F32), pltpu.VMEM((L, 1024), F32), pltpu.VMEM((L, 256), F32), pltpu.VMEM((L, 256), F32),
               pltpu.VMEM((L, L), F32), pltpu.VMEM((8, 1024), F32), pltpu.VMEM((L, 1024), F32), pltpu.VMEM((L, 1024), F32)]
    assert (len(in_specs), len(out_specs), len(scratch)) == (n_in, n_out, n_scratch)
    outs = pl.pallas_call(
        body, name="ssd_bwd", grid=(nc,), in_specs=in_specs + [ANY] * ns, out_specs=out_specs + [ANY] * ns,
        out_shape=out_shape + _pair_swap_out_shapes(swap), scratch_shapes=scratch + (_pair_swap_scratch(ns) if ns else []),
        compiler_params=_params(("arbitrary",)),
    )(proj, proj, proj, proj, proj, proj, proj, conv_w8, conv_w8, conv_w8, conv_b, conv_b, conv_b, proj, smallp, norm_w,
      ypre, states, dy, sel, *swap)
    return (*outs[:n_out], list(outs[n_out:]))


FOX_SCALE = HEAD_DIM ** -0.5
FOX_T = 256
Q_COL, K_COL, V_COL = 2, 3, 4


def _split_dot(v, m, terms):
    out, rest = None, v
    for i in range(terms):
        piece = rest.astype(BF16)
        out = _dot(piece, m) if out is None else out + _dot(piece, m)
        if i + 1 < terms:
            rest = rest - piece.astype(F32)
    return out


def _split3_dot(v, m):
    return _split_dot(v, m, 3)


def _head_mean(x, sel_ref, selt_ref):
    return _dot(x, sel_ref[...]) * (1.0 / HEAD_DIM)


def _head_spread(v, selt_ref):
    return _split_dot(v, selt_ref[...], 2)


def _head_rstd(x, sel_ref, selt_ref):
    return _head_spread(lax.rsqrt(_head_mean(x * x, sel_ref, selt_ref) + NORM_EPS), selt_ref)


def fox_tables():
    r = np.arange(3 * 128)
    piece, lane = r // 128, r % 128
    head = lane - F_LANE
    is_head = np.logical_and(head >= 0, head < N_HEADS)
    col = 128 * (head // 2) + HEAD_DIM * (1 - head % 2) + piece
    cols = np.arange(1024)
    place_q = np.logical_and(is_head[:, None], cols[None, :] == col[:, None])
    place_k = np.logical_and(is_head[:, None], cols[None, :] == (col + 3)[:, None])
    ones_q = np.logical_and(cols % HEAD_DIM >= 3, cols % HEAD_DIM < 6)[None]
    ones_k = (cols % HEAD_DIM < 3)[None]
    h = np.arange(128) - F_LANE
    ok = np.logical_and(h >= 0, h < N_HEADS)
    same_pair = cols[:, None] // 128 == (h // 2)[None, :]
    fold_even = np.logical_and(np.logical_and(ok, h % 2 == 0)[None, :], same_pair)
    fold_odd = np.logical_and(np.logical_and(ok, h % 2 == 1)[None, :], same_pair)
    as_bf16 = lambda t: jnp.asarray(t.astype(np.float32), BF16)
    return (as_bf16(place_q), as_bf16(place_k), jnp.asarray(ones_q, F32), jnp.asarray(ones_k, F32),
            as_bf16(fold_even), as_bf16(fold_odd))


def fox_prep(proj, smallp, qw, kw, sel, selt, place_q, place_k, ones_q, ones_k, *, tm=256):
    s = proj.shape[0]

    def body(q_ref, k_ref, v_ref, small_ref, sp_ref, qw_ref, kw_ref, sel_ref, selt_ref, pq_ref, pk_ref, oq_ref, ok_ref,
             qn_ref, kn_ref, aq_ref, ak_ref, vb_ref, knt_ref, akt_ref, vt_ref, carry):
        @pl.when(pl.program_id(0) == 0)
        def _():
            carry[...] = jnp.zeros_like(carry)

        q = q_ref[...]
        qn_ref[...] = (((q * _head_rstd(q, sel_ref, selt_ref)) * qw_ref[...]) * FOX_SCALE).astype(BF16)
        k = k_ref[...]
        kn = ((k * _head_rstd(k, sel_ref, selt_ref)) * kw_ref[...]).astype(BF16)
        kn_ref[...] = kn
        knt_ref[...] = kn.astype(F32).T.astype(BF16)
        vb_ref[...] = v_ref[...].astype(BF16)
        vt_ref[...] = v_ref[...].T.astype(BF16)
        li = _lane_iota((tm, 128))
        f_lane = jnp.logical_and(li >= F_LANE, li < F_LANE + N_HEADS)
        logf = jnp.where(f_lane, -_softplus(-(small_ref[...] + sp_ref[3:4, :])), 0.0)
        cum = _cumsum_rows(logf) + carry[...]
        carry[...] = cum[tm - 1:tm, :]
        hi = cum.astype(BF16)
        r1 = cum - hi.astype(F32)
        mid = r1.astype(BF16)
        lo = (r1 - mid.astype(F32)).astype(BF16)
        pieces = jnp.concatenate([hi, mid, lo], axis=1)
        aq_ref[...] = (_dot(pieces, pq_ref[...]) + oq_ref[...]).astype(BF16)
        ak = ok_ref[...] - _dot(pieces, pk_ref[...])
        ak_ref[...] = ak.astype(BF16)
        akt_ref[...] = ak.T.astype(BF16)

    row = pl.BlockSpec((tm, 1024), lambda i: (i, 0))
    col = pl.BlockSpec((1024, tm), lambda i: (0, i))
    vec = pl.BlockSpec((1, 1024), lambda i: (0, 0))
    table = pl.BlockSpec((384, 1024), lambda i: (0, 0))
    wide = jax.ShapeDtypeStruct((s, 1024), BF16)
    tall = jax.ShapeDtypeStruct((1024, s), BF16)
    return pl.pallas_call(
        body, name="fox_prep", grid=(s // tm,),
        in_specs=[pl.BlockSpec((tm, 1024), lambda i: (i, Q_COL)), pl.BlockSpec((tm, 1024), lambda i: (i, K_COL)),
                  pl.BlockSpec((tm, 1024), lambda i: (i, V_COL)),
                  pl.BlockSpec((tm, 128), lambda i: (i, SMALL_BLOCK)), pl.BlockSpec((8, 128), lambda i: (0, 0)), vec, vec,
                  pl.BlockSpec((1024, 128), lambda i: (0, 0)), pl.BlockSpec((128, 1024), lambda i: (0, 0)),
                  table, table, vec, vec],
        out_specs=[row, row, row, row, row, col, col, col],
        out_shape=[wide, wide, wide, wide, wide, tall, tall, tall],
        scratch_shapes=[pltpu.VMEM((1, 128), F32)], compiler_params=_params(("arbitrary",)),
    )(proj, proj, proj, proj, smallp, qw, kw, sel, selt, place_q, place_k, ones_q, ones_k)


def fox_fwd(qn, kn, aq, ak, vt, shards=()):
    s = qn.shape[0]
    t = FOX_T
    nq = s // t
    ng = len(shards)

    def body(*refs):
        q_ref, k_ref, aq_ref, ak_ref, vt_ref = refs[:5]
        o_ref, ot_ref, lse_ref = refs[5 + ng:8 + ng]
        p = pl.program_id(0)
        if ng:
            start, forward, finish = _gather_phases(refs[5:5 + ng], refs[8 + ng:8 + 2 * ng], *refs[8 + 2 * ng:])
            pl.when(p == 0)(start)
            pl.when(p == N_PAIRS // 2)(forward)

        @pl.when(p == 0)
        def _():
            lse_ref[...] = jnp.zeros_like(lse_ref)

        lo = _lane_iota((t, 128)) < HEAD_DIM
        lo_rows = _row_iota((128, t)) < HEAD_DIM
        causal_t = _lane_iota((t, t)) >= _row_iota((t, t))

        def q_loop(qi, _):
            q0 = pl.multiple_of(qi * t, t)
            qv, aqv = q_ref[pl.ds(q0, t), :], aq_ref[pl.ds(q0, t), :]
            qa, qb = jnp.where(lo, qv, aqv), jnp.where(lo, aqv, qv)

            def scores(kj):
                k0 = pl.multiple_of(kj * t, t)
                kv, akv = k_ref[pl.ds(k0, t), :], ak_ref[pl.ds(k0, t), :]
                return _dot(jnp.where(lo, kv, akv), qa, "nt"), _dot(jnp.where(lo, akv, kv), qb, "nt")

            def update(kj, stats, s0, s1):
                m0, l0, m1, l1, acc = stats
                vtv = vt_ref[:, pl.ds(pl.multiple_of(kj * t, t), t)]
                n0 = jnp.maximum(m0, jnp.max(s0, axis=0, keepdims=True))
                n1 = jnp.maximum(m1, jnp.max(s1, axis=0, keepdims=True))
                a0, a1 = jnp.exp(m0 - n0), jnp.exp(m1 - n1)
                p0, p1 = jnp.exp(s0 - n0), jnp.exp(s1 - n1)
                l0 = a0 * l0 + jnp.sum(p0, axis=0, keepdims=True)
                l1 = a1 * l1 + jnp.sum(p1, axis=0, keepdims=True)
                acc = (jnp.where(lo_rows, a0, a1) * acc + _dot(jnp.where(lo_rows, vtv, 0.0), p0)
                       + _dot(jnp.where(lo_rows, 0.0, vtv), p1))
                return n0, l0, n1, l1, acc

            def step(kj, carry):
                stats, (s0, s1) = carry[:5], carry[5:]
                nxt = scores(kj + 1)
                return (*update(kj, stats, s0, s1), *nxt)

            def row(val):
                return jnp.full((1, t), val, F32)

            init = (row(NEG_BIG), row(0.0), row(NEG_BIG), row(0.0), jnp.zeros((128, t), F32), *scores(0))
            carry = lax.fori_loop(0, qi, step, init)
            s0, s1 = jnp.where(causal_t, carry[5], NEG_BIG), jnp.where(causal_t, carry[6], NEG_BIG)
            m0, l0, m1, l1, acc = update(qi, carry[:5], s0, s1)
            out_t = acc / jnp.where(lo_rows, l0, l1)
            ot_ref[:, pl.ds(q0, t)] = out_t.astype(BF16)
            o_ref[pl.ds(q0, t), :] = out_t.T.astype(BF16)
            ri = _row_iota((N_HEADS, t))
            old = lse_ref[:, pl.ds(q0, t)]
            lse_ref[:, pl.ds(q0, t)] = jnp.where(
                ri == 2 * p, m0 + jnp.log(l0), jnp.where(ri == 2 * p + 1, m1 + jnp.log(l1), old))
            return 0

        lax.fori_loop(0, nq, q_loop, 0)
        if ng:
            pl.when(p == N_PAIRS - 1)(finish)

    pair = pl.BlockSpec((s, 128), lambda p: (0, p))
    outs = pl.pallas_call(
        body, name="fox_fwd", grid=(N_PAIRS,),
        in_specs=[pair] * 4 + [pl.BlockSpec((128, s), lambda p: (p, 0))] + [ANY] * ng,
        out_specs=[pair, pl.BlockSpec((128, s), lambda p: (p, 0)), pl.BlockSpec((N_HEADS, s), lambda p: (0, 0))] + [ANY] * ng,
        out_shape=[jax.ShapeDtypeStruct((s, 1024), BF16), jax.ShapeDtypeStruct((1024, s), BF16),
                   jax.ShapeDtypeStruct((N_HEADS, s), F32)] + _gather_out_shapes(shards),
        scratch_shapes=_gather_scratch(ng) if ng else [],
        compiler_params=_params(("arbitrary",)),
    )(qn, kn, aq, ak, vt, *shards)
    return outs[0], outs[1], outs[2], list(outs[3:])


def fox_bwd(qn, kn, aq, ak, knt, akt, vb, lse, dmixed, parts=()):
    s = qn.shape[0]
    t = FOX_T
    nq = s // t
    once = pl.Buffered(1)
    ns = len(parts)

    def body(*refs):
        q_ref, k_ref, aq_ref, ak_ref, kt_ref, akt_ref, v_ref, lse_ref, do_ref = refs[:9]
        dq_ref, dk_ref, dv_ref, dc0_ref, dc1_ref = refs[9 + ns:14 + ns]
        p_scr, dp_scr = refs[14 + 2 * ns:16 + 2 * ns]
        p = pl.program_id(0)
        if ns:
            start, finish = _scatter_phases(refs[9:9 + ns], refs[14 + ns:14 + 2 * ns], *refs[16 + 2 * ns:])
            pl.when(p == 0)(start)
        dk_ref[...] = jnp.zeros_like(dk_ref)
        dv_ref[...] = jnp.zeros_like(dv_ref)
        dc0_ref[...] = jnp.zeros_like(dc0_ref)
        dc1_ref[...] = jnp.zeros_like(dc1_ref)
        lo = _lane_iota((t, 128)) < HEAD_DIM
        lo_rows = _row_iota((128, t)) < HEAD_DIM
        causal_t = _lane_iota((t, t)) >= _row_iota((t, t))

        def q_loop(qi, _):
            q0 = pl.multiple_of(qi * t, t)
            qv, aqv = q_ref[pl.ds(q0, t), :], aq_ref[pl.ds(q0, t), :]
            qa, qb = jnp.where(lo, qv, aqv), jnp.where(lo, aqv, qv)
            do = do_ref[pl.ds(q0, t), :]
            doa, dob = jnp.where(lo, do, 0.0).astype(BF16), jnp.where(lo, 0.0, do).astype(BF16)
            lse_blk = lse_ref[:, pl.ds(q0, t)]
            ri = _row_iota((N_HEADS, t))
            lse0 = jnp.sum(jnp.where(ri == 2 * p, lse_blk, 0.0), axis=0, keepdims=True)
            lse1 = jnp.sum(jnp.where(ri == 2 * p + 1, lse_blk, 0.0), axis=0, keepdims=True)

            def scores(kj):
                k0 = pl.multiple_of(kj * t, t)
                kv, akv = k_ref[pl.ds(k0, t), :], ak_ref[pl.ds(k0, t), :]
                return _dot(jnp.where(lo, kv, akv), qa, "nt"), _dot(jnp.where(lo, akv, kv), qb, "nt")

            def pass1(kj, d0, d1, diagonal):
                k0 = pl.multiple_of(kj * t, t)
                vv = v_ref[pl.ds(k0, t), :]
                s0, s1 = scores(kj)
                if diagonal:
                    s0, s1 = jnp.where(causal_t, s0, NEG_BIG), jnp.where(causal_t, s1, NEG_BIG)
                p0, p1 = jnp.exp(s0 - lse0), jnp.exp(s1 - lse1)
                dp0, dp1 = _dot(vv, doa, "nt"), _dot(vv, dob, "nt")
                p_scr[0, kj], p_scr[1, kj] = p0, p1
                dp_scr[0, kj], dp_scr[1, kj] = dp0, dp1
                dv_ref[pl.ds(k0, t), :] += _dot(p0, doa) + _dot(p1, dob)
                return d0 + jnp.sum(p0 * dp0, axis=0, keepdims=True), d1 + jnp.sum(p1 * dp1, axis=0, keepdims=True)

            zero = jnp.zeros((1, t), F32)
            d0, d1 = lax.fori_loop(0, qi, lambda kj, c: pass1(kj, *c, False), (zero, zero))
            d0, d1 = pass1(qi, d0, d1, True)

            def fold_lanes(v):
                return functools.reduce(lambda a, b: a + b, [v[:, 128 * i:128 * (i + 1)] for i in range(t // 128)])

            def pass2(kj, carry):
                dq0, dq1 = carry
                k0 = pl.multiple_of(kj * t, t)
                p0, p1 = p_scr[0, kj], p_scr[1, kj]
                ds0, ds1 = p0 * (dp_scr[0, kj] - d0), p1 * (dp_scr[1, kj] - d1)
                dk_ref[pl.ds(k0, t), :] += jnp.where(lo, _dot(ds0, qa), _dot(ds1, qb))
                dc0_ref[pl.ds(k0, t), :] += fold_lanes(ds0)
                dc1_ref[pl.ds(k0, t), :] += fold_lanes(ds1)
                ktv, aktv = kt_ref[:, pl.ds(k0, t)], akt_ref[:, pl.ds(k0, t)]
                return dq0 + _dot(jnp.where(lo_rows, ktv, aktv), ds0), dq1 + _dot(jnp.where(lo_rows, aktv, ktv), ds1)

            zq = jnp.zeros((128, t), F32)
            dq0, dq1 = lax.fori_loop(0, qi + 1, pass2, (zq, zq))
            dq_ref[pl.ds(q0, t), :] = jnp.where(lo_rows, dq0, dq1).T
            return 0

        lax.fori_loop(0, nq, q_loop, 0)
        if ns:
            pl.when(p == N_PAIRS - 1)(finish)

    pair = pl.BlockSpec((s, 128), lambda p: (0, p))
    pair_t = pl.BlockSpec((128, s), lambda p: (p, 0))
    out = jax.ShapeDtypeStruct((s, 1024), F32)
    outs = pl.pallas_call(
        body, name="fox_bwd", grid=(N_PAIRS,),
        in_specs=[pair, pair, pair, pair, pair_t, pair_t, pair, pl.BlockSpec((N_HEADS, s), lambda p: (0, 0)),
                  pl.BlockSpec((s, 128), lambda p: (0, 8 + p))] + [ANY] * ns,
        out_specs=[pl.BlockSpec((s, 128), lambda p: (0, p), pipeline_mode=once)] * 5 + [ANY] * ns,
        out_shape=[out] * 5 + [jax.ShapeDtypeStruct(p.shape, p.dtype) for p in parts],
        scratch_shapes=[pltpu.VMEM((2, nq, t, t), F32), pltpu.VMEM((2, nq, t, t), F32)] + (_scatter_scratch(ns) if ns else []),
        compiler_params=_params(("arbitrary",)),
    )(qn, kn, aq, ak, knt, akt, vb, lse, dmixed, *parts)
    return (*outs[:5], _keep_own_blocks(outs[5:], parts))


def fox_post(dqn, dkn, dc0, dc1, proj, smallp, qw, kw, sel, selt, fold_even, fold_odd, *, tm=256):
    s = proj.shape[0]
    nrow = s // tm

    def body(dqn_ref, dkn_ref, dc0_ref, dc1_ref, q_ref, k_ref, small_ref, sp_ref, qw_ref, kw_ref, sel_ref, selt_ref,
             fe_ref, fo_ref, dq_ref, dk_ref, dsmall_ref, gqw_ref, gkw_ref, gfb_ref, carry):
        step = pl.program_id(0)

        @pl.when(step == 0)
        def _():
            carry[...] = jnp.zeros_like(carry)

        def norm_bwd(x_ref, w_ref, dn, out_ref):
            x = x_ref[...]
            rf = _head_rstd(x, sel_ref, selt_ref)
            xh = x * rf
            g = dn * w_ref[...]
            mean_gx = _head_spread(_head_mean(g * xh, sel_ref, selt_ref), selt_ref)
            out_ref[...] = (rf * (g - xh * mean_gx)).astype(BF16)
            return jnp.sum(dn * xh, axis=0, keepdims=True)

        gqw = norm_bwd(q_ref, qw_ref, dqn_ref[...] * FOX_SCALE, dq_ref)
        gkw = norm_bwd(k_ref, kw_ref, dkn_ref[...], dk_ref)
        li = _lane_iota((tm, 128))
        f_lane = jnp.logical_and(li >= F_LANE, li < F_LANE + N_HEADS)
        dcum = -(_split3_dot(dc0_ref[...], fe_ref[...]) + _split3_dot(dc1_ref[...], fo_ref[...]))
        dlogf = _rev_cumsum_rows(dcum) + carry[...]
        carry[...] = dlogf[0:1, :]
        dfr = jnp.where(f_lane, dlogf * _sigmoid(-(small_ref[...] + sp_ref[3:4, :])), 0.0)
        dsmall_ref[...] = dfr
        gfb = jnp.sum(dfr, axis=0, keepdims=True)

        @pl.when(step == 0)
        def _():
            gqw_ref[...] = gqw
            gkw_ref[...] = gkw
            gfb_ref[...] = gfb

        @pl.when(step > 0)
        def _():
            gqw_ref[...] += gqw
            gkw_ref[...] += gkw
            gfb_ref[...] += gfb

    def rb(i):
        return nrow - 1 - i

    row = pl.BlockSpec((tm, 1024), lambda i: (rb(i), 0))
    vec = pl.BlockSpec((1, 1024), lambda i: (0, 0))
    fold = pl.BlockSpec((1024, 128), lambda i: (0, 0))
    return pl.pallas_call(
        body, name="fox_post", grid=(nrow,),
        in_specs=[row, row, row, row, pl.BlockSpec((tm, 1024), lambda i: (rb(i), Q_COL)),
                  pl.BlockSpec((tm, 1024), lambda i: (rb(i), K_COL)),
                  pl.BlockSpec((tm, 128), lambda i: (rb(i), SMALL_BLOCK)), pl.BlockSpec((8, 128), lambda i: (0, 0)), vec, vec,
                  fold, pl.BlockSpec((128, 1024), lambda i: (0, 0)), fold, fold],
        out_specs=[row, row, pl.BlockSpec((tm, 128), lambda i: (rb(i), 0)), vec, vec, pl.BlockSpec((1, 128), lambda i: (0, 0))],
        out_shape=[jax.ShapeDtypeStruct((s, 1024), BF16), jax.ShapeDtypeStruct((s, 1024), BF16),
                   jax.ShapeDtypeStruct((s, 128), F32), jax.ShapeDtypeStruct((1, 1024), F32),
                   jax.ShapeDtypeStruct((1, 1024), F32), jax.ShapeDtypeStruct((1, 128), F32)],
        scratch_shapes=[pltpu.VMEM((1, 128), F32)], compiler_params=_params(("arbitrary",)),
    )(dqn, dkn, dc0, dc1, proj, proj, proj, smallp, qw, kw, sel, selt, fold_even, fold_odd)


def local_step(x, target, wx, later_shards, ssd_cw8, ssd_cb, smallp, ssd_nw, qw_t, kw_t, sel, selt,
               norm_mix_w, norm_ffn_w, ffn_cw8, ffn_cb):
    proj, h_t = rms_in_proj(x, norm_mix_w, wx)
    y_ssd, y_ssd_t, ypre, states = ssd_fwd(proj, ssd_cw8, ssd_cb, smallp, ssd_nw)
    place_q, place_k, ones_q, ones_k, fold_even, fold_odd = fox_tables()
    qn, kn, aq, ak, vb, knt, akt, vt = fox_prep(proj, smallp, qw_t, kw_t, sel, selt, place_q, place_k, ones_q, ones_k)
    y_fox, y_fox_t, lse, (a_out, a_up, a_down) = fox_fwd(qn, kn, aq, ak, vt, shards=later_shards)
    w_out = a_out.reshape(2048, D_MODEL)
    w_down = a_down.reshape(D_FF, D_MODEL)
    s = x.shape[0]
    shard = lambda index: pl.BlockSpec((None, 1024, 1408), index)
    x1, hf, hf_t = out_proj_rms_fwd(y_ssd, y_fox, w_out, x, norm_ffn_w)
    hu, act, act_t = up_ffn_fwd(hf, a_up, ffn_cw8, ffn_cb)
    dy, sq = down_proj_loss(act, w_down, x1, target)

    dact = matmul(dy, w_down, mode="nt", tm=1024, tn=1408, tk=1024, out_dtype=F32, name="mm_dact")
    g_down = matmul(act_t, dy, mode="nn", tm=1408, tn=1024, tk=1024, out_dtype=BF16, name="mm_dw_down")
    dhu, gcw_g, gcw_v = ffn_mid_bwd(hu, dact, ffn_cw8, ffn_cb)
    g_up = matmul(hf_t, dhu, mode="nn", tm=1024, tn=1408, tk=1024, out_dtype=BF16, name="mm_dw_up",
                  layout=dict(m=D_MODEL, n=2 * D_FF, k=s, b_spec=shard(lambda i, j, kk: (j // 2, kk, j % 2)),
                              o_spec=shard(lambda i, j, kk: (j, i, 0)), out_shape=(4, D_MODEL, 1408)))
    dx1, g_norm_ffn, dmixed = rms_bwd_matmul(dhu, a_up, x1, norm_ffn_w, dy, w_out, name="dhf_rms_ffn_bwd_dmixed")
    g_out_a = matmul(y_ssd_t, dx1, mode="nn", tm=1024, tn=1024, tk=1024, out_dtype=BF16, name="mm_dw_out_ssd")
    g_out_b = matmul(y_fox_t, dx1, mode="nn", tm=1024, tn=1024, tk=1024, out_dtype=BF16, name="mm_dw_out_fox")
    early = [jnp.concatenate([g_out_a, g_out_b], axis=0).reshape(4, 512, D_MODEL), g_up, g_down.reshape(4, 704, D_MODEL)]
    dz, dxs, db, dc, dsmall_ssd, gcw_x, gcw_b, gcw_c, g_sp, g_ssd_nw, theirs = ssd_bwd(
        proj, ssd_cw8, ssd_cb, smallp, ssd_nw, ypre, states, dmixed, sel, swap=early)
    core = lax.axis_index("c").astype(jnp.int32).reshape(1)
    parts = [add_pair(a, b, core, name="add_pair_" + n, tr=ADAM_ROWS[n]) for a, b, n in zip(early, theirs, BIG_NAMES[1:])]
    dqn, dkn, dv, dc0, dc1, landed_early = fox_bwd(qn, kn, aq, ak, knt, akt, vb, lse, dmixed, parts=parts)
    dq, dk, dsmall_fox, g_qw, g_kw, g_fb = fox_post(dqn, dkn, dc0, dc1, proj, smallp, qw_t, kw_t, sel, selt,
                                                    fold_even, fold_odd)
    dproj = jnp.concatenate([dz, dxs, dq, dk, dv.astype(BF16), db, dc, (dsmall_ssd + dsmall_fox).astype(BF16)], axis=1)
    g_wx = matmul(h_t, dproj, mode="nn", tm=1024, tn=PROJ_TILE, tk=1024, out_dtype=BF16, name="mm_dw_in")
    g_in = _in_grad_shards(g_wx)
    part_in = add_pair(g_in, pair_swap_halves([g_in], name="pair_swap_w_in")[0], core, name="add_pair_w_in",
                       tr=ADAM_ROWS["w_in"])
    grad_x, g_norm_mix, landed_in = matmul_rms_bwd(dproj, wx, x, norm_mix_w, dx1, scatter=[part_in])
    return dict(
        sq=sq, grad_x=grad_x, landed=landed_in + landed_early,
        g_norm_mix=g_norm_mix, g_norm_ffn=g_norm_ffn, g_ssd_nw=g_ssd_nw,
        g_ssd_cw=jnp.concatenate([gcw_x, gcw_b, gcw_c], axis=1), g_sp=g_sp, g_fb=g_fb, g_qw=g_qw, g_kw=g_kw,
        g_ffn_cw=jnp.concatenate([gcw_g, gcw_v], axis=1))


def adamw(w, g, m, v, *, name, tr, allreduce=None):
    rows, cols = w.shape
    nsteps = rows // tr

    def body(*refs):
        if allreduce is None:
            w_ref, g_ref, m_ref, v_ref, d_ref, mo_ref, vo_ref = refs
        else:
            w_ref, g_ref, m_ref, v_ref, packed_ref, d_ref, mo_ref, vo_ref, summed_ref = refs[:9]
            start, finish = _allreduce_phases(packed_ref, summed_ref, *refs[9:])
            pl.when(pl.program_id(0) == 0)(start)
        gv = g_ref[...]
        mn = ADAM_B1 * m_ref[...] + (1.0 - ADAM_B1) * gv
        vn = ADAM_B2 * v_ref[...] + (1.0 - ADAM_B2) * (gv * gv)
        m_hat = mn / (1.0 - ADAM_B1 ** ADAM_STEP)
        v_hat = vn / (1.0 - ADAM_B2 ** ADAM_STEP)
        d_ref[...] = -ADAM_LR * (m_hat / (jnp.sqrt(v_hat) + ADAM_EPS) + ADAM_WD * w_ref[...])
        mo_ref[...] = mn
        vo_ref[...] = vn
        if allreduce is not None:
            pl.when(pl.program_id(0) == nsteps - 1)(finish)

    blk = pl.BlockSpec((tr, cols), lambda i: (i, 0))
    shp = jax.ShapeDtypeStruct((rows, cols), F32)
    if allreduce is None:
        return pl.pallas_call(
            body, name=name, grid=(nsteps,), in_specs=[blk] * 4, out_specs=[blk] * 3, out_shape=[shp] * 3,
            compiler_params=_params(("parallel",)),
        )(w, g, m, v)
    whole = pl.BlockSpec(memory_space=pltpu.VMEM)
    return pl.pallas_call(
        body, name=name, grid=(nsteps,), in_specs=[blk] * 4 + [whole], out_specs=[blk] * 3 + [whole],
        out_shape=[shp] * 3 + [jax.ShapeDtypeStruct(allreduce.shape, F32)],
        scratch_shapes=_allreduce_scratch(allreduce.shape[0]), compiler_params=_params(("arbitrary",)),
    )(w, g, m, v, allreduce)


def add_pair(full, theirs, core, *, name, tr):
    _, rows, cols = theirs.shape
    nblk = rows // tr

    def body(c_ref, a_ref, b_ref, o_ref):
        o_ref[...] = (a_ref[...].astype(F32) + b_ref[...].astype(F32)).astype(BF16)

    blk = pl.BlockSpec((1, tr, cols), lambda j, i, c: (j, i, 0))
    grid_spec = pltpu.PrefetchScalarGridSpec(
        num_scalar_prefetch=1, grid=(4, nblk),
        in_specs=[pl.BlockSpec((1, tr, cols), lambda j, i, c: (j, c[0] * nblk + i, 0)), blk], out_specs=blk)
    return pl.pallas_call(
        body, name=name, grid_spec=grid_spec, out_shape=jax.ShapeDtypeStruct(theirs.shape, BF16),
        compiler_params=_params(("parallel", "parallel")),
    )(core, full, theirs)


def sum_chips(parts, core, *, name, tr):
    _, rows, cols = parts.shape
    nblk = rows // tr

    def body(c_ref, p_ref, o_ref):
        acc = p_ref[0].astype(F32)
        for k in range(1, 4):
            acc = acc + p_ref[k].astype(F32)
        o_ref[...] = acc

    grid_spec = pltpu.PrefetchScalarGridSpec(
        num_scalar_prefetch=1, grid=(nblk,), in_specs=[pl.BlockSpec((4, tr, cols), lambda i, c: (0, i, 0))],
        out_specs=pl.BlockSpec((tr, cols), lambda i, c: (c[0] * nblk + i, 0)))
    return pl.pallas_call(
        body, name=name, grid_spec=grid_spec, out_shape=jax.ShapeDtypeStruct((2 * rows, cols), F32),
        compiler_params=_params(("parallel",)),
    )(core, parts)


ANY = pl.BlockSpec(memory_space=pl.ANY)


def _place():
    x, y, c = lax.axis_index("x"), lax.axis_index("y"), lax.axis_index("c")
    chips = [(1 - x, y), (x, 1 - y), (1 - x, 1 - y)]
    return x, y, c, chips


def _chunks(rows):
    size = next((c for c in (128, 176, 64, 32, 16, 8) if rows % c == 0), rows)
    return [(r, size) for r in range(0, rows, size)]


def gather_weights(shards):
    n = len(shards)

    def body(*refs):
        start, forward, finish = _gather_phases(refs[:n], refs[n:2 * n], *refs[2 * n:])
        start()
        forward()
        finish()

    gathered = pl.pallas_call(
        body, name="gather_weights", in_specs=[ANY] * n, out_specs=[ANY] * n,
        out_shape=_gather_out_shapes(shards), scratch_shapes=_gather_scratch(n),
    )(*shards)
    return gathered


def _gather_out_shapes(shards):
    return [jax.ShapeDtypeStruct((4,) + s.shape, s.dtype) for s in shards]


def _gather_scratch(n):
    return [pltpu.SemaphoreType.DMA((n, 7)), pltpu.SemaphoreType.DMA((n, 7))]


def _gather_phases(ins, outs, send_sems, recv_sems):
    n = len(ins)
    x, y, c, chips = _place()
    me = 2 * x + y
    sibling = (x, y, 1 - c)
    blks = [2 * cx + cy for cx, cy in chips]

    def half(a, blk, r=0, nr=None):
        rows = ins[a].shape[0] // 2
        return outs[a].at[blk, pl.ds(c * rows + r, rows if nr is None else nr), :]

    def to_chip(a, t, r=0, nr=None):
        rows = ins[a].shape[0] // 2
        return pltpu.make_async_remote_copy(
            src_ref=ins[a].at[pl.ds(c * rows + r, rows if nr is None else nr), :], dst_ref=half(a, me, r, nr),
            send_sem=send_sems.at[a, t], recv_sem=recv_sems.at[a, t], device_id=(*chips[t], c), device_id_type=MESH)

    def from_chip(a, t):
        return pltpu.make_async_remote_copy(
            src_ref=half(a, blks[t]), dst_ref=half(a, blks[t]), send_sem=send_sems.at[a, t], recv_sem=recv_sems.at[a, t],
            device_id=(*chips[t], c), device_id_type=MESH)

    def to_sibling(a, t, r=0, nr=None):
        return pltpu.make_async_remote_copy(
            src_ref=half(a, blks[t], r, nr), dst_ref=half(a, blks[t], r, nr), send_sem=send_sems.at[a, 3 + t],
            recv_sem=recv_sems.at[a, 3 + t], device_id=sibling, device_id_type=MESH)

    def from_sibling(a, t):
        rows = ins[a].shape[0] // 2
        dst = outs[a].at[blks[t], pl.ds((1 - c) * rows, rows), :]
        return pltpu.make_async_remote_copy(
            src_ref=dst, dst_ref=dst, send_sem=send_sems.at[a, 3 + t], recv_sem=recv_sems.at[a, 3 + t],
            device_id=sibling, device_id_type=MESH)

    def own(a, r=0, nr=None):
        return pltpu.make_async_remote_copy(
            src_ref=ins[a].at[pl.ds(r, ins[a].shape[0] if nr is None else nr), :],
            dst_ref=outs[a].at[me, pl.ds(r, ins[a].shape[0] if nr is None else nr), :],
            send_sem=send_sems.at[a, 6], recv_sem=recv_sems.at[a, 6], device_id=sibling, device_id_type=MESH)

    def start():
        for a in range(n):
            for t in range(3):
                for r, nr in _chunks(ins[a].shape[0] // 2):
                    to_chip(a, t, r, nr).start()
            for r, nr in _chunks(ins[a].shape[0]):
                own(a, r, nr).start()

    def forward():
        for a in range(n):
            for t in range(3):
                from_chip(a, t).wait_recv()
                for r, nr in _chunks(ins[a].shape[0] // 2):
                    to_sibling(a, t, r, nr).start()

    def finish():
        for a in range(n):
            for t in range(3):
                from_sibling(a, t).wait_recv()
        for a in range(n):
            for t in range(3):
                to_chip(a, t).wait_send()
                to_sibling(a, t).wait_send()
            own(a).wait()

    return start, forward, finish


def pair_swap_halves(grads, *, name):
    n = len(grads)

    def body(*refs):
        start, finish = _pair_swap_phases(refs[:n], refs[n:2 * n], *refs[2 * n:])
        start()
        finish()

    return pl.pallas_call(
        body, name=name, in_specs=[ANY] * n, out_specs=[ANY] * n, out_shape=_pair_swap_out_shapes(grads),
        scratch_shapes=_pair_swap_scratch(n),
    )(*grads)


def _pair_swap_out_shapes(grads):
    return [jax.ShapeDtypeStruct((4, g.shape[1] // 2, g.shape[2]), g.dtype) for g in grads]


def _pair_swap_scratch(n):
    return [pltpu.SemaphoreType.DMA((n,)), pltpu.SemaphoreType.DMA((n,))]


def _pair_swap_phases(ins, theirs, send_sems, recv_sems):
    n = len(ins)
    x, y, c, _ = _place()
    sibling = (x, y, 1 - c)

    def start():
        for a in range(n):
            rows = ins[a].shape[1] // 2
            for j in range(4):
                for r, nr in _chunks(rows):
                    pltpu.make_async_remote_copy(
                        src_ref=ins[a].at[j, pl.ds((1 - c) * rows + r, nr), :], dst_ref=theirs[a].at[j, pl.ds(r, nr), :],
                        send_sem=send_sems.at[a], recv_sem=recv_sems.at[a], device_id=sibling, device_id_type=MESH).start()

    def finish():
        for a in range(n):
            pltpu.make_async_remote_copy(src_ref=theirs[a], dst_ref=theirs[a], send_sem=send_sems.at[a],
                                         recv_sem=recv_sems.at[a], device_id=sibling, device_id_type=MESH).wait()

    return start, finish


def _scatter_scratch(n):
    return [pltpu.SemaphoreType.DMA((n, 3)), pltpu.SemaphoreType.DMA((n, 3))]


def _keep_own_blocks(landed, parts):
    if not parts:
        return []
    chip = 2 * lax.axis_index("x") + lax.axis_index("y")
    return [lax.dynamic_update_slice(l, lax.dynamic_slice_in_dim(p, chip, 1, axis=0), (chip, 0, 0))
            for l, p in zip(landed, parts)]


def _scatter_phases(ins, outs, send_sems, recv_sems):
    n = len(ins)
    x, y, c, chips = _place()
    me = 2 * x + y
    blks = [2 * cx + cy for cx, cy in chips]

    def start():
        for a in range(n):
            for r, nr in _chunks(ins[a].shape[1]):
                for t in range(3):
                    pltpu.make_async_remote_copy(
                        src_ref=ins[a].at[blks[t], pl.ds(r, nr), :], dst_ref=outs[a].at[me, pl.ds(r, nr), :],
                        send_sem=send_sems.at[a, t], recv_sem=recv_sems.at[a, t],
                        device_id=(*chips[t], c), device_id_type=MESH).start()

    def finish():
        for a in range(n):
            for t in range(3):
                pltpu.make_async_remote_copy(
                    src_ref=outs[a].at[blks[t]], dst_ref=outs[a].at[blks[t]], send_sem=send_sems.at[a, t],
                    recv_sem=recv_sems.at[a, t], device_id=(*chips[t], c), device_id_type=MESH).wait()

    return start, finish


def pair_join_halves(bufs):
    n = len(bufs)

    def body(*refs):
        outs = refs[n:2 * n]
        send_sems, recv_sems = refs[2 * n:]
        x, y, c, _ = _place()
        sibling = (x, y, 1 - c)
        for a in range(n):
            rows = outs[a].shape[0] // 2
            for r, nr in _chunks(rows):
                mine = outs[a].at[pl.ds(c * rows + r, nr), :]
                pltpu.make_async_remote_copy(src_ref=mine, dst_ref=mine, send_sem=send_sems.at[a], recv_sem=recv_sems.at[a],
                                             device_id=sibling, device_id_type=MESH).start()
        for a in range(n):
            rows = outs[a].shape[0] // 2
            pltpu.make_async_remote_copy(
                src_ref=outs[a].at[pl.ds(c * rows, rows), :], dst_ref=outs[a].at[pl.ds((1 - c) * rows, rows), :],
                send_sem=send_sems.at[a], recv_sem=recv_sems.at[a], device_id=sibling, device_id_type=MESH).wait()

    return pl.pallas_call(
        body, name="pair_join_halves", in_specs=[ANY] * n, out_specs=[ANY] * n,
        out_shape=[jax.ShapeDtypeStruct(b.shape, b.dtype) for b in bufs], input_output_aliases={a: a for a in range(n)},
        scratch_shapes=[pltpu.SemaphoreType.DMA((n,)), pltpu.SemaphoreType.DMA((n,))],
    )(*bufs)


def _allreduce_scratch(rows):
    return [pltpu.VMEM((8, rows, 128), F32), pltpu.SemaphoreType.DMA((7,)), pltpu.SemaphoreType.DMA((7,))]


def _allreduce_phases(in_ref, out_ref, gathered, send_sems, recv_sems):
    x, y, c, _ = _place()
    me = 4 * x + 2 * y + c
    flips = [(fx, fy, fc) for fx in (0, 1) for fy in (0, 1) for fc in (0, 1)][1:]
    peers = [((1 - x) if fx else x, (1 - y) if fy else y, (1 - c) if fc else c) for fx, fy, fc in flips]

    def send(t):
        return pltpu.make_async_remote_copy(
            src_ref=in_ref, dst_ref=gathered.at[me], send_sem=send_sems.at[t], recv_sem=recv_sems.at[t],
            device_id=peers[t], device_id_type=MESH)

    def start():
        gathered[me] = in_ref[...]
        for t in range(7):
            send(t).start()

    def finish():
        for t, (px, py, pc) in enumerate(peers):
            slot = gathered.at[4 * px + 2 * py + pc]
            pltpu.make_async_remote_copy(
                src_ref=slot, dst_ref=slot, send_sem=send_sems.at[t], recv_sem=recv_sems.at[t],
                device_id=(px, py, pc), device_id_type=MESH).wait_recv()
        for t in range(7):
            send(t).wait_send()
        acc = gathered[0]
        for k in range(1, 8):
            acc = acc + gathered[k]
        out_ref[...] = acc

    return start, finish


SMALL_NAMES = ("norm_mix_w", "ssd_conv_w", "ssd_conv_b", "ssd_dt_bias", "ssd_a_log", "ssd_d", "ssd_norm_w", "fox_f_bias",
               "fox_q_norm_w", "fox_k_norm_w", "norm_ffn_w", "ffn_conv_w", "ffn_conv_b")
BIG_NAMES = ("w_in", "w_out", "w_up", "w_down")
WEIGHT_ORDER = ("norm_mix_w", "w_in", "ssd_conv_w", "ssd_conv_b", "ssd_dt_bias", "ssd_a_log", "ssd_d", "ssd_norm_w",
                "fox_f_bias", "fox_q_norm_w", "fox_k_norm_w", "w_out", "norm_ffn_w", "w_up", "ffn_conv_w", "ffn_conv_b", "w_down")
ADAM_ROWS = {"w_in": 256, "w_out": 256, "w_up": 256, "w_down": 176}


def _pack(arrays):
    pieces = []
    for a in arrays:
        flat = a.reshape(-1).astype(F32)
        pieces += [flat, jnp.zeros(((-flat.shape[0]) % 1024,), F32)]
    return jnp.concatenate(pieces).reshape(-1, 128)


def _unpack(packed, shapes):
    out, r = [], 0
    for shp in shapes:
        size = 1
        for d in shp:
            size *= d
        nrow = 8 * (-(-size // 1024))
        out.append(packed[r:r + nrow].reshape(-1)[:size].reshape(shp))
        r += nrow
    return out


IN_SHARD = IN_COLS // 4
IN_SEGMENTS = ((0, 2048, 0), (2048, 2560, 5120), (2560, 2576, MAIN_COLS), (2576, 5648, 2048), (5648, 5664, MAIN_COLS + F_LANE))


def _in_cols(shards, lo, hi):
    out = []
    for j in range(4):
        a, b = max(lo, IN_SHARD * j), min(hi, IN_SHARD * (j + 1))
        if a < b:
            out.append(shards[j][:, a - IN_SHARD * j:b - IN_SHARD * j])
    return out


def _in_grad_shards(g):
    shards = []
    for j in range(4):
        pieces = []
        for lo, hi, at in IN_SEGMENTS:
            a, b = max(lo, IN_SHARD * j), min(hi, IN_SHARD * (j + 1))
            if a < b:
                pieces.append(g[:, at + a - lo:at + b - lo])
        shards.append(jnp.concatenate(pieces, axis=1))
    return jnp.stack(shards)


def _pad_rows(a, rows):
    return jnp.pad(a, ((0, rows - a.shape[0]), (0, 0)))


def kernel(x, norm_mix_w, w_in, ssd_conv_w, ssd_conv_b, ssd_dt_bias, ssd_a_log, ssd_d, ssd_norm_w, fox_f_bias, fox_q_norm_w, fox_k_norm_w, w_out, norm_ffn_w, w_up, ffn_conv_w, ffn_conv_b, w_down, loss_target, m_norm_mix_w, m_w_in, m_ssd_conv_w, m_ssd_conv_b, m_ssd_dt_bias, m_ssd_a_log, m_ssd_d, m_ssd_norm_w, m_fox_f_bias, m_fox_q_norm_w, m_fox_k_norm_w, m_w_out, m_norm_ffn_w, m_w_up, m_ffn_conv_w, m_ffn_conv_b, m_w_down, v_norm_mix_w, v_w_in, v_ssd_conv_w, v_ssd_conv_b, v_ssd_dt_bias, v_ssd_a_log, v_ssd_d, v_ssd_norm_w, v_fox_f_bias, v_fox_q_norm_w, v_fox_k_norm_w, v_w_out, v_norm_ffn_w, v_w_up, v_ffn_conv_w, v_ffn_conv_b, v_w_down):
    w = dict(norm_mix_w=norm_mix_w, w_in=w_in, ssd_conv_w=ssd_conv_w, ssd_conv_b=ssd_conv_b, ssd_dt_bias=ssd_dt_bias,
             ssd_a_log=ssd_a_log, ssd_d=ssd_d, ssd_norm_w=ssd_norm_w, fox_f_bias=fox_f_bias, fox_q_norm_w=fox_q_norm_w,
             fox_k_norm_w=fox_k_norm_w, w_out=w_out, norm_ffn_w=norm_ffn_w, w_up=w_up, ffn_conv_w=ffn_conv_w,
             ffn_conv_b=ffn_conv_b, w_down=w_down)
    m = dict(norm_mix_w=m_norm_mix_w, w_in=m_w_in, ssd_conv_w=m_ssd_conv_w, ssd_conv_b=m_ssd_conv_b, ssd_dt_bias=m_ssd_dt_bias,
             ssd_a_log=m_ssd_a_log, ssd_d=m_ssd_d, ssd_norm_w=m_ssd_norm_w, fox_f_bias=m_fox_f_bias, fox_q_norm_w=m_fox_q_norm_w,
             fox_k_norm_w=m_fox_k_norm_w, w_out=m_w_out, norm_ffn_w=m_norm_ffn_w, w_up=m_w_up, ffn_conv_w=m_ffn_conv_w,
             ffn_conv_b=m_ffn_conv_b, w_down=m_w_down)
    v = dict(norm_mix_w=v_norm_mix_w, w_in=v_w_in, ssd_conv_w=v_ssd_conv_w, ssd_conv_b=v_ssd_conv_b, ssd_dt_bias=v_ssd_dt_bias,
             ssd_a_log=v_ssd_a_log, ssd_d=v_ssd_d, ssd_norm_w=v_ssd_norm_w, fox_f_bias=v_fox_f_bias, fox_q_norm_w=v_fox_q_norm_w,
             fox_k_norm_w=v_fox_k_norm_w, w_out=v_w_out, norm_ffn_w=v_norm_ffn_w, w_up=v_w_up, ffn_conv_w=v_ffn_conv_w,
             ffn_conv_b=v_ffn_conv_b, w_down=v_w_down)
    chip = 2 * lax.axis_index("x") + lax.axis_index("y")

    a_in, a_scw, a_fcw = gather_weights([w_in[0].astype(BF16), _pad_rows(ssd_conv_w[0], 16), _pad_rows(ffn_conv_w[0], 16)])
    later_shards = [w_out[0].astype(BF16), w_up[0].astype(BF16), w_down[0].astype(BF16)]
    wx = jnp.concatenate([p for lo, hi, _ in sorted(IN_SEGMENTS, key=lambda seg: seg[2]) for p in _in_cols(a_in, lo, hi)]
                         + [jnp.zeros((D_MODEL, PROJ_COLS - IN_COLS), BF16)], axis=1)
    ssd_cw8 = a_scw.transpose(1, 0, 2).reshape(16, 1536)[:8]
    ffn_cw8 = a_fcw.transpose(1, 0, 2).reshape(16, 2 * D_FF)[:8]
    gap = lambda n: jnp.zeros((n,), F32)
    smallp = jnp.concatenate([ssd_dt_bias[0], gap(112), ssd_a_log[0], gap(112), ssd_d[0], gap(112),
                              gap(F_LANE), fox_f_bias[0], gap(128 - F_LANE - N_HEADS), gap(4 * 128)]).reshape(8, 128)
    qw_t = jnp.tile(fox_q_norm_w[0], N_HEADS)[None]
    kw_t = jnp.tile(fox_k_norm_w[0], N_HEADS)[None]
    sel = jnp.asarray((np.arange(1024)[:, None] // HEAD_DIM == np.arange(128)[None, :]).astype(np.float32), BF16)

    res = local_step(x[0], loss_target[0], wx, later_shards, ssd_cw8, ssd_conv_b, smallp, ssd_norm_w, qw_t, kw_t,
                     sel, sel.T, norm_mix_w, norm_ffn_w, ffn_cw8, ffn_conv_b)

    full_shapes = [(1, 1024), (1, 4, 1536), (1, 1536), (1, 16), (1, 16), (1, 16), (1, 1024), (1, 16), (1, 64), (1, 64),
                   (1, 1024), (1, 3, 2 * D_FF), (1, 2 * D_FF), (1,)]
    local_small = [res["g_norm_mix"], res["g_ssd_cw"][:4], res["g_ssd_cw"][4], res["g_sp"][0, :16], res["g_sp"][1, :16],
                   res["g_sp"][2, :16], res["g_ssd_nw"], res["g_fb"][0, F_LANE:F_LANE + 16],
                   res["g_qw"].reshape(N_HEADS, HEAD_DIM).sum(0), res["g_kw"].reshape(N_HEADS, HEAD_DIM).sum(0),
                   res["g_norm_ffn"], res["g_ffn_cw"][:3], res["g_ffn_cw"][3], jnp.sum(res["sq"])]
    landed = res["landed"]
    core = lax.axis_index("c").astype(jnp.int32).reshape(1)
    halves = [sum_chips(p, core, name="sum_chips_" + n, tr=ADAM_ROWS[n]) for p, n in zip(landed, BIG_NAMES)]
    g_big = dict(zip(BIG_NAMES, pair_join_halves(halves)))

    grads, deltas, new_m, new_v = {}, {}, {}, {}
    for n in BIG_NAMES:
        out = adamw(w[n][0], g_big[n], m[n][0], v[n][0], name="adamw_" + n, tr=ADAM_ROWS[n],
                    allreduce=_pack(local_small) if n == BIG_NAMES[0] else None)
        if n == BIG_NAMES[0]:
            summed = _unpack(out[3], full_shapes)
        d, mn, vn = out[:3]
        grads[n], deltas[n], new_m[n], new_v[n] = g_big[n][None], d[None], mn[None], vn[None]
    loss = (0.5 / D_MODEL) * summed[-1][0]
    g_small = dict(zip(SMALL_NAMES, summed[:-1]))
    g_small["ssd_conv_w"] = lax.dynamic_slice(g_small["ssd_conv_w"], (0, 0, 384 * chip), (1, 4, 384))
    g_small["ffn_conv_w"] = lax.dynamic_slice(g_small["ffn_conv_w"], (0, 0, 1408 * chip), (1, 3, 1408))
    shapes = [w[n].shape for n in SMALL_NAMES]
    packed_w = _pack([w[n] for n in SMALL_NAMES])
    d, mn, vn = adamw(packed_w, _pack([g_small[n] for n in SMALL_NAMES]), _pack([m[n] for n in SMALL_NAMES]),
                      _pack([v[n] for n in SMALL_NAMES]), name="adamw_small", tr=packed_w.shape[0])
    for n, dd, mm, vv in zip(SMALL_NAMES, _unpack(d, shapes), _unpack(mn, shapes), _unpack(vn, shapes)):
        grads[n], deltas[n], new_m[n], new_v[n] = g_small[n].reshape(w[n].shape), dd, mm, vv
    return (loss, res["grad_x"][None], *[grads[n] for n in WEIGHT_ORDER], *[deltas[n] for n in WEIGHT_ORDER],
            *[new_m[n] for n in WEIGHT_ORDER], *[new_v[n] for n in WEIGHT_ORDER])
```

```python
import functools

import jax
import jax.numpy as jnp
import numpy as np
from jax import lax
from jax.experimental import pallas as pl
from jax.experimental.pallas import tpu as pltpu

F32 = jnp.float32
BF16 = jnp.bfloat16
MESH = pl.DeviceIdType.MESH

D_MODEL = 1024
HEAD_DIM = 64
N_HEADS = 16
N_PAIRS = N_HEADS // 2
SSD_CHUNK = 128
SSD_STATE = 128
SSD_CONV = 4
D_FF = 2816
FFN_CONV = 3
NORM_EPS = 1e-6
MAIN_COLS = 5632
SMALL_COLS = 128
PROJ_COLS = MAIN_COLS + SMALL_COLS
SMALL_BLOCK = MAIN_COLS // SMALL_COLS
PROJ_TILE = 1152
F_LANE = 16
IN_COLS = 5664

ADAM_LR = 0.001
ADAM_B1 = 0.9
ADAM_B2 = 0.999
ADAM_EPS = 1e-08
ADAM_WD = 0.01
ADAM_STEP = 10

VMEM_LIMIT_V7X = 56 * 1024 * 1024
NEG_BIG = -1e30


def _params(sem=None):
    return pltpu.CompilerParams(dimension_semantics=sem, vmem_limit_bytes=VMEM_LIMIT_V7X)


def _sigmoid(x):
    return pl.reciprocal(1.0 + jnp.exp(-x), approx=True)


def _silu_and_grad(x):
    s = _sigmoid(x)
    return x * s, s * (1.0 + x * (1.0 - s))


def _shift_down(v, j):
    return v if j == 0 else pltpu.roll(v, j, 0)


def _shift_up(v, j):
    return v if j == 0 else pltpu.roll(v, v.shape[0] - j, 0)


def _row_iota(shape):
    return lax.broadcasted_iota(jnp.int32, shape, 0)


def _lane_iota(shape):
    return lax.broadcasted_iota(jnp.int32, shape, 1)


def _dot(a, b, mode="nn"):
    dims = {"nn": (((1,), (0,)), ((), ())), "nt": (((1,), (1,)), ((), ())), "tn": (((0,), (0,)), ((), ()))}[mode]
    return lax.dot_general(a.astype(BF16), b.astype(BF16), dims, preferred_element_type=F32)


def _dot_f32(a, b):
    return jnp.dot(a, b, precision=lax.Precision.HIGHEST, preferred_element_type=F32)


def matmul(a, b, *, mode, tm, tn, tk, out_dtype, name, layout=None):
    layout = layout or {}
    if layout:
        m, n, k = layout["m"], layout["n"], layout["k"]
    else:
        (m, k), n = a.shape, (b.shape[1] if mode == "nn" else b.shape[0])
    assert m % tm == 0 and n % tn == 0 and k % tk == 0, (name, m, n, k, tm, tn, tk)
    nk = k // tk
    a_spec = layout.get("a_spec") or pl.BlockSpec((tm, tk), lambda i, j, kk: (i, kk))
    b_spec = layout.get("b_spec") or (pl.BlockSpec((tn, tk), lambda i, j, kk: (j, kk)) if mode == "nt"
                                      else pl.BlockSpec((tk, tn), lambda i, j, kk: (kk, j)))
    o_spec = layout.get("o_spec") or pl.BlockSpec((tm, tn), lambda i, j, kk: (i, j))

    def body(a_ref, b_ref, o_ref, acc_ref):
        kk = pl.program_id(2)
        part = _dot(a_ref[...], b_ref[...], mode)
        if nk == 1:
            o_ref[...] = part.astype(out_dtype)
        else:
            @pl.when(kk == 0)
            def _():
                acc_ref[...] = part

            @pl.when(jnp.logical_and(kk > 0, kk < nk - 1))
            def _():
                acc_ref[...] += part

            @pl.when(kk == nk - 1)
            def _():
                o_ref[...] = (acc_ref[...] + part).astype(out_dtype)

    return pl.pallas_call(
        body, name=name, grid=(m // tm, n // tn, nk), in_specs=[a_spec, b_spec], out_specs=o_spec,
        out_shape=jax.ShapeDtypeStruct(layout.get("out_shape", (m, n)), out_dtype),
        scratch_shapes=[pltpu.VMEM((tm, tn) if nk > 1 else (8, 128), F32)],
        compiler_params=_params(("parallel", "parallel", "arbitrary")),
    )(a, b)


def rms_in_proj(x, w, wx, *, tm=512):
    s, d = x.shape

    def body(x_ref, w_ref, wx_ref, proj_ref, ht_ref):
        for r in range(0, tm, UP_ROWS):
            rows = slice(r, r + UP_ROWS)
            xv = x_ref[rows, :]
            rstd = lax.rsqrt(jnp.mean(xv * xv, axis=-1, keepdims=True) + NORM_EPS)
            h = (xv * rstd) * w_ref[...]
            ht_ref[:, rows] = h.T.astype(BF16)
            proj_ref[rows, :] = _dot(h, wx_ref[...])

    return pl.pallas_call(
        body, name="rms_in_proj", grid=(s // tm,),
        in_specs=[pl.BlockSpec((tm, d), lambda i: (i, 0)), pl.BlockSpec((1, d), lambda i: (0, 0)),
                  pl.BlockSpec((d, PROJ_COLS), lambda i: (0, 0), pipeline_mode=pl.Buffered(1))],
        out_specs=[pl.BlockSpec((tm, PROJ_COLS), lambda i: (i, 0)), pl.BlockSpec((d, tm), lambda i: (0, i))],
        out_shape=[jax.ShapeDtypeStruct((s, PROJ_COLS), F32), jax.ShapeDtypeStruct((d, s), BF16)],
        compiler_params=_params(("parallel",)),
    )(x, w, wx)


def matmul_rms_bwd(dproj, wx, x, w, resid, *, scatter, tm=512):
    s, d = x.shape
    ns = len(scatter)
    nsteps = s // tm

    def body(*refs):
        a_ref, b_ref, x_ref, w_ref, res_ref = refs[:5]
        dx_ref, dw_ref = refs[5 + ns:7 + ns]
        step = pl.program_id(0)
        start, finish = _scatter_phases(refs[5:5 + ns], refs[7 + ns:7 + 2 * ns], *refs[7 + 2 * ns:])
        pl.when(step == 0)(start)
        part = jnp.zeros((1, d), F32)
        for r in range(0, tm, UP_ROWS):
            rows = slice(r, r + UP_ROWS)
            dhv = _dot(a_ref[rows, :], b_ref[...], "nt")
            xv = x_ref[rows, :]
            rstd = lax.rsqrt(jnp.mean(xv * xv, axis=-1, keepdims=True) + NORM_EPS)
            xh = xv * rstd
            g = dhv * w_ref[...]
            dx_ref[rows, :] = res_ref[rows, :] + rstd * (g - xh * jnp.mean(g * xh, axis=-1, keepdims=True))
            part = part + jnp.sum(dhv * xh, axis=0, keepdims=True)

        @pl.when(step == 0)
        def _():
            dw_ref[...] = part

        @pl.when(step > 0)
        def _():
            dw_ref[...] += part

        pl.when(step == nsteps - 1)(finish)

    row = pl.BlockSpec((tm, d), lambda i: (i, 0))
    vec = pl.BlockSpec((1, d), lambda i: (0, 0))
    outs = pl.pallas_call(
        body, name="mm_dh_rms_mix_bwd", grid=(nsteps,),
        in_specs=[pl.BlockSpec((tm, PROJ_COLS), lambda i: (i, 0)),
                  pl.BlockSpec((d, PROJ_COLS), lambda i: (0, 0), pipeline_mode=pl.Buffered(1)), row, vec, row] + [ANY] * ns,
        out_specs=[row, vec] + [ANY] * ns,
        out_shape=[jax.ShapeDtypeStruct((s, d), F32), jax.ShapeDtypeStruct((1, d), F32)]
        + [jax.ShapeDtypeStruct(p.shape, p.dtype) for p in scatter],
        scratch_shapes=_scatter_scratch(ns), compiler_params=_params(("arbitrary",)),
    )(dproj, wx, x, w, resid, *scatter)
    return outs[0], outs[1], _keep_own_blocks(outs[2:], scatter)


def out_proj_rms_fwd(y_ssd, y_fox, w_out, x, norm_w, *, tm=512):
    s, d = x.shape

    def body(ys_ref, yf_ref, w_ref, x_ref, nw_ref, x1_ref, h_ref, ht_ref):
        for r in range(0, tm, UP_ROWS):
            rows = slice(r, r + UP_ROWS)
            x1 = x_ref[rows, :] + _dot(ys_ref[rows, :], w_ref[0:d, :]) + _dot(yf_ref[rows, :], w_ref[d:2 * d, :])
            x1_ref[rows, :] = x1
            rstd = lax.rsqrt(jnp.mean(x1 * x1, axis=-1, keepdims=True) + NORM_EPS)
            h = (x1 * rstd) * nw_ref[...]
            h_ref[rows, :] = h.astype(BF16)
            ht_ref[:, rows] = h.T.astype(BF16)

    row = pl.BlockSpec((tm, d), lambda i: (i, 0))
    return pl.pallas_call(
        body, name="out_proj_rms_fwd", grid=(s // tm,),
        in_specs=[row, row, pl.BlockSpec((2 * d, d), lambda i: (0, 0)), row, pl.BlockSpec((1, d), lambda i: (0, 0))],
        out_specs=[row, row, pl.BlockSpec((d, tm), lambda i: (0, i))],
        out_shape=[jax.ShapeDtypeStruct((s, d), F32), jax.ShapeDtypeStruct((s, d), BF16), jax.ShapeDtypeStruct((d, s), BF16)],
        compiler_params=_params(("parallel",)),
    )(y_ssd, y_fox, w_out, x, norm_w)


def rms_bwd_matmul(dhu, a_up, x, w, resid, b, *, name, tm=512):
    s, d = x.shape
    n = b.shape[0]

    def body(dhu_ref, up_ref, x_ref, w_ref, res_ref, b_ref, dx_ref, dw_ref, prod_ref):
        part = jnp.zeros((1, d), F32)
        for r in range(0, tm, UP_ROWS):
            rows = slice(r, r + UP_ROWS)
            xv = x_ref[rows, :]
            dhv = functools.reduce(lambda p, q: p + q, [
                _dot(dhu_ref[k // 2, rows, (k % 2) * UP_SHARD:(k % 2 + 1) * UP_SHARD], up_ref[k], "nt") for k in range(4)])
            rstd = lax.rsqrt(jnp.mean(xv * xv, axis=-1, keepdims=True) + NORM_EPS)
            xh = xv * rstd
            g = dhv * w_ref[...]
            dx = res_ref[rows, :] + rstd * (g - xh * jnp.mean(g * xh, axis=-1, keepdims=True))
            dx_ref[rows, :] = dx
            prod_ref[rows, :] = _dot(dx, b_ref[...], "nt")
            part = part + jnp.sum(dhv * xh, axis=0, keepdims=True)

        @pl.when(pl.program_id(0) == 0)
        def _():
            dw_ref[...] = part

        @pl.when(pl.program_id(0) > 0)
        def _():
            dw_ref[...] += part

    row = pl.BlockSpec((tm, d), lambda i: (i, 0))
    vec = pl.BlockSpec((1, d), lambda i: (0, 0))
    once = pl.Buffered(1)
    return pl.pallas_call(
        body, name=name, grid=(s // tm,),
        in_specs=[pl.BlockSpec((2, tm, D_FF), lambda i: (0, i, 0)),
                  pl.BlockSpec((4, d, UP_SHARD), lambda i: (0, 0, 0), pipeline_mode=once), row, vec, row,
                  pl.BlockSpec((n, d), lambda i: (0, 0), pipeline_mode=once)],
        out_specs=[row, vec, pl.BlockSpec((tm, n), lambda i: (i, 0))],
        out_shape=[jax.ShapeDtypeStruct((s, d), F32), jax.ShapeDtypeStruct((1, d), F32), jax.ShapeDtypeStruct((s, n), F32)],
        compiler_params=_params(("arbitrary",)),
    )(dhu, a_up, x, w, resid, b)


def down_proj_loss(act, w_down, x1, target, *, tm=512):
    s, d = x1.shape

    def body(a_ref, w_ref, x1_ref, t_ref, dy_ref, sq_ref):
        part = jnp.zeros((1, d), F32)
        for r in range(0, tm, UP_ROWS):
            rows = slice(r, r + UP_ROWS)
            e = x1_ref[rows, :] + _dot(a_ref[rows, :], w_ref[...]) - t_ref[rows, :]
            dy_ref[rows, :] = e / float(d)
            part = part + jnp.sum(e * e, axis=0, keepdims=True)

        @pl.when(pl.program_id(0) == 0)
        def _():
            sq_ref[...] = part

        @pl.when(pl.program_id(0) > 0)
        def _():
            sq_ref[...] += part

    row = pl.BlockSpec((tm, d), lambda i: (i, 0))
    vec = pl.BlockSpec((1, d), lambda i: (0, 0))
    return pl.pallas_call(
        body, name="down_proj_loss", grid=(s // tm,),
        in_specs=[pl.BlockSpec((tm, D_FF), lambda i: (i, 0)), pl.BlockSpec((D_FF, d), lambda i: (0, 0)), row, row],
        out_specs=[row, vec], out_shape=[jax.ShapeDtypeStruct((s, d), F32), jax.ShapeDtypeStruct((1, d), F32)],
        compiler_params=_params(("arbitrary",)),
    )(act, w_down, x1, target)


def _row_shifts(ext, k_taps):
    return [_shift_down(ext, j) for j in range(k_taps)]


def _conv_rows(shifts, w):
    k_taps = len(shifts)
    acc = w[k_taps - 1:k_taps, :] * shifts[0]
    for k in range(k_taps - 1):
        acc = acc + w[k:k + 1, :] * shifts[k_taps - 1 - k]
    return acc


def _conv_weight_grad(dcur, shifts, rows, width):
    k_taps = len(shifts)
    out = [jnp.sum(dcur * shifts[k_taps - 1 - k][rows], axis=0, keepdims=True) for k in range(k_taps)]
    out.append(jnp.sum(dcur, axis=0, keepdims=True))
    return _stack_rows(out, width)


def _conv_rows_transposed(dext, w, k_taps):
    acc = w[k_taps - 1:k_taps, :] * dext
    for k in range(k_taps - 1):
        acc = acc + w[k:k + 1, :] * _shift_up(dext, k_taps - 1 - k)
    return acc


def _stack_rows(rows, width):
    ri = _row_iota((8, width))
    out = jnp.zeros((8, width), F32)
    for k, r in enumerate(rows):
        out = out + jnp.where(ri == k, r, 0.0)
    return out


UP_SHARD = 1408
UP_ROWS = 256


def up_ffn_fwd(hf, a_up, conv_w8, conv_b, *, tm=512):
    s = hf.shape[0]

    def body(a_ref, bg_ref, bv_ref, wg_ref, wv_ref, cbg_ref, cbv_ref, hu_ref, act_ref, actt_ref, carry):
        i, j = pl.program_id(0), pl.program_id(1)
        prev_g = jnp.where(i == 0, 0.0, carry[0, j])
        prev_v = jnp.where(i == 0, 0.0, carry[1, j])
        for r in range(0, tm, UP_ROWS):
            rows = slice(r, r + UP_ROWS)
            a = a_ref[rows, :]
            hg, hv = _dot(a, bg_ref[...]), _dot(a, bv_ref[...])
            hu_ref[0, rows, :] = hg
            hu_ref[1, rows, :] = hv
            gc = _conv_rows(_row_shifts(jnp.concatenate([prev_g, hg], axis=0), FFN_CONV), wg_ref[...])[8:] + cbg_ref[...]
            vc = _conv_rows(_row_shifts(jnp.concatenate([prev_v, hv], axis=0), FFN_CONV), wv_ref[...])[8:] + cbv_ref[...]
            act = gc * _sigmoid(gc) * vc
            act_ref[rows, :] = act.astype(BF16)
            actt_ref[:, rows] = act.T.astype(BF16)
            prev_g, prev_v = hg[UP_ROWS - 8:], hv[UP_ROWS - 8:]
        carry[0, j] = prev_g
        carry[1, j] = prev_v

    shard = lambda off: pl.BlockSpec((None, D_MODEL, UP_SHARD), lambda i, j: (j + off, 0, 0))
    taps = lambda off: pl.BlockSpec((8, UP_SHARD), lambda i, j: (0, j + off))
    bias = lambda off: pl.BlockSpec((1, UP_SHARD), lambda i, j: (0, j + off))
    return pl.pallas_call(
        body, name="up_ffn_fwd", grid=(s // tm, 2),
        in_specs=[pl.BlockSpec((tm, D_MODEL), lambda i, j: (i, 0)), shard(0), shard(2), taps(0), taps(2), bias(0), bias(2)],
        out_specs=[pl.BlockSpec((2, tm, UP_SHARD), lambda i, j: (0, i, j)), pl.BlockSpec((tm, UP_SHARD), lambda i, j: (i, j)),
                   pl.BlockSpec((UP_SHARD, tm), lambda i, j: (j, i))],
        out_shape=[jax.ShapeDtypeStruct((2, s, D_FF), F32), jax.ShapeDtypeStruct((s, D_FF), BF16),
                   jax.ShapeDtypeStruct((D_FF, s), BF16)],
        scratch_shapes=[pltpu.VMEM((2, 2, 8, UP_SHARD), F32)], compiler_params=_params(("arbitrary", "arbitrary")),
    )(hf, a_up, a_up, conv_w8, conv_w8, conv_b, conv_b)


def ffn_mid_bwd(hu, dact, conv_w8, conv_b, *, tm=1024, tc=256):
    s = hu.shape[1]
    ncol = D_FF // tc
    nrow = s // tm
    r8 = tm // 8

    def body(g_ref, v_ref, gp_ref, vp_ref, gn_ref, vn_ref, da_ref, dan_ref, wg_ref, wv_ref, bg_ref, bv_ref,
             dhu_ref, wgo_ref, wvo_ref):
        i = pl.program_id(1)
        first = i == 0
        last = i == nrow - 1

        def ext_of(cur_ref, prev_ref, next_ref):
            prev = jnp.where(first, 0.0, prev_ref[...])
            return jnp.concatenate([prev, cur_ref[...], next_ref[...]], axis=0)

        g_sh = _row_shifts(ext_of(g_ref, gp_ref, gn_ref), FFN_CONV)
        v_sh = _row_shifts(ext_of(v_ref, vp_ref, vn_ref), FFN_CONV)
        gc = _conv_rows(g_sh, wg_ref[...]) + bg_ref[...]
        vc = _conv_rows(v_sh, wv_ref[...]) + bv_ref[...]
        da_ext = jnp.concatenate([jnp.zeros((8, tc), F32), da_ref[...], jnp.where(last, 0.0, dan_ref[...])], axis=0)
        silu, dsilu = _silu_and_grad(gc)
        dgc = da_ext * vc * dsilu
        dvc = da_ext * silu
        dhu_ref[0] = _conv_rows_transposed(dgc, wg_ref[...], FFN_CONV)[8:8 + tm].astype(BF16)
        dhu_ref[1] = _conv_rows_transposed(dvc, wv_ref[...], FFN_CONV)[8:8 + tm].astype(BF16)

        cur = slice(8, 8 + tm)
        pg = _conv_weight_grad(dgc[cur], g_sh, cur, tc)
        pv = _conv_weight_grad(dvc[cur], v_sh, cur, tc)

        @pl.when(first)
        def _():
            wgo_ref[...] = pg
            wvo_ref[...] = pv

        @pl.when(i > 0)
        def _():
            wgo_ref[...] += pg
            wvo_ref[...] += pv

    def prev_idx(i):
        return jnp.maximum(i * r8 - 1, 0)

    def next_idx(i):
        return jnp.minimum((i + 1) * r8, s // 8 - 1)

    half = lambda k, rows, row_index: pl.BlockSpec((None, rows, tc), lambda j, i: (k, row_index(i), j))
    in_specs = [
        half(0, tm, lambda i: i), half(1, tm, lambda i: i),
        half(0, 8, prev_idx), half(1, 8, prev_idx),
        half(0, 8, next_idx), half(1, 8, next_idx),
        pl.BlockSpec((tm, tc), lambda j, i: (i, j)),
        pl.BlockSpec((8, tc), lambda j, i: (next_idx(i), j)),
        pl.BlockSpec((8, tc), lambda j, i: (0, j)),
        pl.BlockSpec((8, tc), lambda j, i: (0, j + ncol)),
        pl.BlockSpec((1, tc), lambda j, i: (0, j)),
        pl.BlockSpec((1, tc), lambda j, i: (0, j + ncol)),
    ]
    out_specs = [pl.BlockSpec((2, tm, tc), lambda j, i: (0, i, j)), pl.BlockSpec((8, tc), lambda j, i: (0, j)),
                 pl.BlockSpec((8, tc), lambda j, i: (0, j))]
    out_shape = [jax.ShapeDtypeStruct((2, s, D_FF), BF16),
                 jax.ShapeDtypeStruct((8, D_FF), F32), jax.ShapeDtypeStruct((8, D_FF), F32)]
    return pl.pallas_call(
        body, name="ffn_mid_bwd", grid=(ncol, nrow), in_specs=in_specs, out_specs=out_specs, out_shape=out_shape,
        compiler_params=_params(("parallel", "arbitrary")),
    )(hu, hu, hu, hu, hu, hu, dact, dact, conv_w8, conv_w8, conv_b, conv_b)


def _softplus(x):
    return jnp.maximum(x, 0.0) + jnp.log(1.0 + jnp.exp(-jnp.abs(x)))


def _cumsum_rows(v):
    n = v.shape[0]
    ri = _row_iota(v.shape)
    sh = 1
    while sh < n:
        v = v + jnp.where(ri >= sh, _shift_down(v, sh), 0.0)
        sh *= 2
    return v


def _rev_cumsum_rows(v):
    n = v.shape[0]
    ri = _row_iota(v.shape)
    sh = 1
    while sh < n:
        v = v + jnp.where(ri < n - sh, _shift_up(v, sh), 0.0)
        sh *= 2
    return v


def _total(v):
    return jnp.sum(jnp.sum(v, axis=1, keepdims=True), axis=0, keepdims=True)


def _ssd_in_specs(rev_nc=None):
    def ch(c):
        return c if rev_nc is None else rev_nc - 1 - c

    def prev(c):
        return jnp.maximum(ch(c) * (SSD_CHUNK // 8) - 1, 0)

    L = SSD_CHUNK
    return [
        pl.BlockSpec((L, 1024), lambda c: (ch(c), 0)),
        pl.BlockSpec((L, 1024), lambda c: (ch(c), 1)),
        pl.BlockSpec((L, 256), lambda c: (ch(c), 20)),
        pl.BlockSpec((L, 256), lambda c: (ch(c), 21)),
        pl.BlockSpec((8, 1024), lambda c: (prev(c), 1)),
        pl.BlockSpec((8, 256), lambda c: (prev(c), 20)),
        pl.BlockSpec((8, 256), lambda c: (prev(c), 21)),
        pl.BlockSpec((8, 1024), lambda c: (0, 0)),
        pl.BlockSpec((8, 256), lambda c: (0, 4)),
        pl.BlockSpec((8, 256), lambda c: (0, 5)),
        pl.BlockSpec((1, 1024), lambda c: (0, 0)),
        pl.BlockSpec((1, 256), lambda c: (0, 4)),
        pl.BlockSpec((1, 256), lambda c: (0, 5)),
        pl.BlockSpec((L, SMALL_COLS), lambda c: (ch(c), SMALL_BLOCK)),
        pl.BlockSpec((8, 128), lambda c: (0, 0)),
        pl.BlockSpec((1, 1024), lambda c: (0, 0)),
    ]


def _ssd_conv_pre(cur_ref, prev_ref, w_ref, b_ref, first):
    prev = jnp.where(first, 0.0, prev_ref[...])
    shifts = _row_shifts(jnp.concatenate([prev, cur_ref[...]], axis=0), SSD_CONV)
    return shifts, _conv_rows(shifts, w_ref[...])[8:] + b_ref[...]


def _ssd_time_consts(small_ref, sp_ref):
    dt_pre = small_ref[...] + sp_ref[0:1, :]
    dt = _softplus(dt_pre)
    a = -jnp.exp(sp_ref[1:2, :])
    acs = _cumsum_rows(dt * a)
    return dt_pre, dt, a, acs


def ssd_fwd(proj, conv_w8, conv_b, smallp, norm_w):
    s = proj.shape[0]
    nc = s // SSD_CHUNK
    L = SSD_CHUNK

    def body(z_ref, xs_ref, b_ref, c_ref, xsp_ref, bp_ref, cp_ref, wx_ref, wb_ref, wc_ref, bx_ref, bb_ref, bc_ref,
             small_ref, sp_ref, nw_ref, y_ref, yt_ref, ypre_ref, st_ref, state):
        first = pl.program_id(0) == 0

        @pl.when(first)
        def _():
            state[...] = jnp.zeros_like(state)

        xs = _ssd_conv_pre(xs_ref, xsp_ref, wx_ref, bx_ref, first)[1]
        xs = xs * _sigmoid(xs)
        bm = _ssd_conv_pre(b_ref, bp_ref, wb_ref, bb_ref, first)[1]
        bm = bm * _sigmoid(bm)
        cm = _ssd_conv_pre(c_ref, cp_ref, wc_ref, bc_ref, first)[1]
        cm = cm * _sigmoid(cm)
        _, dt, _, acs = _ssd_time_consts(small_ref, sp_ref)
        acs_t = acs.T
        li = _lane_iota((L, L))
        ri = _row_iota((L, L))
        tri = ri >= li
        lo = li < HEAD_DIM
        st_ref[0] = state[...]
        for g in range(2):
            bg = bm[:, 128 * g:128 * g + 128]
            cg = cm[:, 128 * g:128 * g + 128]
            gmat = _dot(cg, bg, "nt")
            for pp in range(4):
                p = 4 * g + pp
                h0, h1 = 2 * p, 2 * p + 1
                x = xs[:, 128 * p:128 * p + 128]
                a0, a1 = acs[:, h0:h0 + 1], acs[:, h1:h1 + 1]
                xdt = x * jnp.where(lo, dt[:, h0:h0 + 1], dt[:, h1:h1 + 1])
                m0 = gmat * jnp.exp(jnp.where(tri, a0 - acs_t[h0:h0 + 1, :], NEG_BIG))
                m1 = gmat * jnp.exp(jnp.where(tri, a1 - acs_t[h1:h1 + 1, :], NEG_BIG))
                yd = _dot(m0, jnp.where(lo, xdt, 0.0)) + _dot(m1, jnp.where(lo, 0.0, xdt))
                hin = state[p]
                yo = _dot(cg, hin, "nt") * jnp.exp(jnp.where(lo, a0, a1))
                dskip = jnp.where(lo[0:1], sp_ref[2:3, h0:h0 + 1], sp_ref[2:3, h1:h1 + 1])
                ypre_ref[:, 128 * p:128 * p + 128] = yd + yo + dskip * x
                al0, al1 = acs[L - 1:L, h0:h0 + 1], acs[L - 1:L, h1:h1 + 1]
                w = jnp.exp(jnp.where(lo, al0 - a0, al1 - a1))
                dec = jnp.exp(jnp.where(ri < HEAD_DIM, al0, al1))
                state[p] = dec * hin + _dot(xdt * w, bg, "tn")
        z = z_ref[...]
        yg = ypre_ref[...] * (z * _sigmoid(z))
        for g in range(2):
            seg = yg[:, 512 * g:512 * g + 512]
            r = lax.rsqrt(jnp.mean(seg * seg, axis=-1, keepdims=True) + NORM_EPS)
            out = (seg * r) * nw_ref[:, 512 * g:512 * g + 512]
            y_ref[:, 512 * g:512 * g + 512] = out.astype(BF16)
            yt_ref[512 * g:512 * g + 512, :] = out.T.astype(BF16)

    row = pl.BlockSpec((L, 1024), lambda c: (c, 0))
    return pl.pallas_call(
        body, name="ssd_fwd", grid=(nc,), in_specs=_ssd_in_specs(),
        out_specs=[row, pl.BlockSpec((1024, L), lambda c: (0, c)), row,
                   pl.BlockSpec((1, N_PAIRS, 128, 128), lambda c: (c, 0, 0, 0))],
        out_shape=[jax.ShapeDtypeStruct((s, 1024), BF16), jax.ShapeDtypeStruct((1024, s), BF16),
                   jax.ShapeDtypeStruct((s, 1024), F32), jax.ShapeDtypeStruct((nc, N_PAIRS, 128, 128), F32)],
        scratch_shapes=[pltpu.VMEM((N_PAIRS, 128, 128), F32)],
        compiler_params=_params(("arbitrary",)),
    )(proj, proj, proj, proj, proj, proj, proj, conv_w8, conv_w8, conv_w8, conv_b, conv_b, conv_b, proj, smallp, norm_w)


def ssd_bwd(proj, conv_w8, conv_b, smallp, norm_w, ypre, states, dy, sel, swap=()):
    s = proj.shape[0]
    nc = s // SSD_CHUNK
    L = SSD_CHUNK

    ns = len(swap)
    n_in, n_out, n_scratch = 20, 10, 11

    def body(*refs):
        own = refs[:n_in] + refs[n_in + ns:n_in + ns + n_out] + refs[n_in + 2 * ns + n_out:n_in + 2 * ns + n_out + n_scratch]
        if ns:
            start, finish = _pair_swap_phases(refs[n_in:n_in + ns], refs[n_in + ns + n_out:n_in + 2 * ns + n_out],
                                              *refs[n_in + 2 * ns + n_out + n_scratch:])
            pl.when(pl.program_id(0) == 0)(start)
        compute(*own)
        if ns:
            pl.when(pl.program_id(0) == nc - 1)(finish)

    def compute(z_ref, xs_ref, b_ref, c_ref, xsp_ref, bp_ref, cp_ref, wx_ref, wb_ref, wc_ref, bx_ref, bb_ref, bc_ref,
                small_ref, sp_ref, nw_ref, ypre_ref, st_ref, dy_ref, sel_ref,
                dz_ref, dxs_ref, db_ref, dc_ref, dsmall_ref, gwx_ref, gwb_ref, gwc_ref, gsp_ref, gnw_ref,
                dstate, carry_x, carry_b, carry_c, dxs_buf, dbm_buf, dcm_buf, qcs, col_sums, acs_terms, dt_terms):
        step = pl.program_id(0)
        col_sums[...] = jnp.zeros_like(col_sums)
        first_chunk = step == nc - 1
        start = step == 0

        @pl.when(start)
        def _():
            dstate[...] = jnp.zeros_like(dstate)
            carry_x[...] = jnp.zeros_like(carry_x)
            carry_b[...] = jnp.zeros_like(carry_b)
            carry_c[...] = jnp.zeros_like(carry_c)

        xs_sh, xs_pre = _ssd_conv_pre(xs_ref, xsp_ref, wx_ref, bx_ref, first_chunk)
        b_sh, b_pre = _ssd_conv_pre(b_ref, bp_ref, wb_ref, bb_ref, first_chunk)
        c_sh, c_pre = _ssd_conv_pre(c_ref, cp_ref, wc_ref, bc_ref, first_chunk)
        xs, xs_ds = _silu_and_grad(xs_pre)
        bm, b_ds = _silu_and_grad(b_pre)
        cm, c_ds = _silu_and_grad(c_pre)
        dt_pre, dt, a, acs = _ssd_time_consts(small_ref, sp_ref)
        acs_t = acs.T
        li = _lane_iota((L, L))
        ri = _row_iota((L, L))
        tri = ri >= li
        lo = li < HEAD_DIM
        lo_rows = ri < HEAD_DIM
        li1 = _lane_iota((1, L))

        z = z_ref[...]
        sz, dsz = _silu_and_grad(z)
        y = ypre_ref[...]
        yg = y * sz
        dout = dy_ref[...]
        dyg_parts = []
        gnw_parts = []
        for g in range(2):
            sl = slice(512 * g, 512 * g + 512)
            seg = yg[:, sl]
            r = lax.rsqrt(jnp.mean(seg * seg, axis=-1, keepdims=True) + NORM_EPS)
            n = seg * r
            gnw_parts.append(jnp.sum(dout[:, sl] * n, axis=0, keepdims=True))
            gg = dout[:, sl] * nw_ref[:, sl]
            dyg_parts.append(r * (gg - n * jnp.mean(gg * n, axis=-1, keepdims=True)))
        dyg = jnp.concatenate(dyg_parts, axis=1)
        gnw = jnp.concatenate(gnw_parts, axis=1)
        dz_ref[...] = (dyg * y * dsz).astype(BF16)
        dypre = dyg * sz

        qcs[...] = jnp.zeros_like(qcs)
        dalast = jnp.zeros((1, L), F32)
        for g in range(2):
            bg = bm[:, 128 * g:128 * g + 128]
            cg = cm[:, 128 * g:128 * g + 128]
            gmat = _dot(cg, bg, "nt")
            dgmat = jnp.zeros((L, L), F32)
            dbg = jnp.zeros((L, L), F32)
            dcg = jnp.zeros((L, L), F32)
            for pp in range(4):
                p = 4 * g + pp
                h0, h1 = 2 * p, 2 * p + 1
                lanes = slice(128 * p, 128 * p + 128)
                x = xs[:, lanes]
                dyp = dypre[:, lanes]
                a0, a1 = acs[:, h0:h0 + 1], acs[:, h1:h1 + 1]
                dtl = jnp.where(lo, dt[:, h0:h0 + 1], dt[:, h1:h1 + 1])
                xdt = x * dtl
                l0 = jnp.exp(jnp.where(tri, a0 - acs_t[h0:h0 + 1, :], NEG_BIG))
                l1 = jnp.exp(jnp.where(tri, a1 - acs_t[h1:h1 + 1, :], NEG_BIG))
                m0, m1 = gmat * l0, gmat * l1
                dskip = jnp.where(lo[0:1], sp_ref[2:3, h0:h0 + 1], sp_ref[2:3, h1:h1 + 1])
                col_sums[0:1, lanes] = jnp.sum(dyp * x, axis=0, keepdims=True)
                dx = dyp * dskip
                dy0, dy1 = jnp.where(lo, dyp, 0.0), jnp.where(lo, 0.0, dyp)
                x0, x1 = jnp.where(lo, xdt, 0.0), jnp.where(lo, 0.0, xdt)
                dm0, dm1 = _dot(dy0, x0, "nt"), _dot(dy1, x1, "nt")
                dxdt = _dot(m0, dy0, "tn") + _dot(m1, dy1, "tn")
                q0, q1 = dm0 * m0, dm1 * m1
                qcs[h0:h0 + 1, :] = jnp.sum(q0, axis=0, keepdims=True)
                qcs[h1:h1 + 1, :] = jnp.sum(q1, axis=0, keepdims=True)
                row_terms = jnp.where(lo, q0 + pltpu.roll(q0, HEAD_DIM, 1), q1 + pltpu.roll(q1, HEAD_DIM, 1))
                dgmat = dgmat + dm0 * l0 + dm1 * l1
                hin = st_ref[0, p]
                e = jnp.exp(jnp.where(lo, a0, a1))
                ch = _dot(cg, hin, "nt")
                dch = dyp * e
                dcg = dcg + _dot(dch, hin)
                dhin = _dot(dch, cg, "tn")
                dhout = dstate[p]
                al0, al1 = acs[L - 1:L, h0:h0 + 1], acs[L - 1:L, h1:h1 + 1]
                dec = jnp.exp(jnp.where(lo_rows, al0, al1))
                dhin = dhin + dec * dhout
                dal = dhout * hin * dec
                dal0 = _total(jnp.where(lo_rows, dal, 0.0))
                dal1 = _total(dal) - dal0
                dalast = dalast + jnp.where(li1 == h0, dal0, 0.0) + jnp.where(li1 == h1, dal1, 0.0)
                w = jnp.exp(jnp.where(lo, al0 - a0, al1 - a1))
                xw = xdt * w
                dxw = _dot(bg, dhout, "nt")
                dbg = dbg + _dot(xw, dhout)
                dxdt = dxdt + dxw * w
                dww = dxw * xw
                col_sums[1:2, lanes] = jnp.sum(dww, axis=0, keepdims=True)
                acs_terms[:, lanes] = row_terms + dch * ch - dww
                dx = dx + dxdt * dtl
                dt_terms[:, lanes] = dxdt * x
                dxs_buf[:, lanes] = dx
                dstate[p] = dhin
            dcg = dcg + _dot(dgmat, bg)
            dbg = dbg + _dot(dgmat, cg, "tn")
            dbm_buf[:, 128 * g:128 * g + 128] = dbg
            dcm_buf[:, 128 * g:128 * g + 128] = dcg

        head_sums = _split3_dot(col_sums[...], sel_ref[...])
        dskip_g = head_sums[0:1, :]
        dalast = dalast + head_sums[1:2, :]
        ddt = _split3_dot(dt_terms[...], sel_ref[...])
        dacs_tot = _split3_dot(acs_terms[...], sel_ref[...]) - qcs[...].T + jnp.where(ri == L - 1, dalast, 0.0)
        dstep = _rev_cumsum_rows(dacs_tot)
        ddt = ddt + dstep * a
        head_lane = li < N_HEADS
        ddt_pre = jnp.where(head_lane, ddt * _sigmoid(dt_pre), 0.0)
        dsmall_ref[...] = ddt_pre
        da = jnp.sum(jnp.where(head_lane, dstep * dt, 0.0), axis=0, keepdims=True)
        gsp = _stack_rows([jnp.sum(ddt_pre, axis=0, keepdims=True), da * a, dskip_g], L)

        def conv_back(dpost, ds, shifts, w_ref, carry, out_ref, width):
            dpre = dpost * ds
            dext = jnp.concatenate([dpre, carry[...]], axis=0)
            out_ref[...] = _conv_rows_transposed(dext, w_ref[...], SSD_CONV)[:L].astype(BF16)
            carry[...] = dpre[0:8]
            return _conv_weight_grad(dpre, shifts, slice(8, 8 + L), width)

        gwx = conv_back(dxs_buf[...], xs_ds, xs_sh, wx_ref, carry_x, dxs_ref, 1024)
        gwb = conv_back(dbm_buf[...], b_ds, b_sh, wb_ref, carry_b, db_ref, 256)
        gwc = conv_back(dcm_buf[...], c_ds, c_sh, wc_ref, carry_c, dc_ref, 256)

        @pl.when(start)
        def _():
            gwx_ref[...] = gwx
            gwb_ref[...] = gwb
            gwc_ref[...] = gwc
            gsp_ref[...] = gsp
            gnw_ref[...] = gnw

        @pl.when(step > 0)
        def _():
            gwx_ref[...] += gwx
            gwb_ref[...] += gwb
            gwc_ref[...] += gwc
            gsp_ref[...] += gsp
            gnw_ref[...] += gnw

    def ch(c):
        return nc - 1 - c

    row = pl.BlockSpec((L, 1024), lambda c: (ch(c), 0))
    row256 = pl.BlockSpec((L, 256), lambda c: (ch(c), 0))
    in_specs = _ssd_in_specs(rev_nc=nc) + [row, pl.BlockSpec((1, N_PAIRS, 128, 128), lambda c: (ch(c), 0, 0, 0)), row,
                                           pl.BlockSpec((1024, 128), lambda c: (0, 0))]
    out_specs = [row, row, row256, row256, pl.BlockSpec((L, 128), lambda c: (ch(c), 0)),
                 pl.BlockSpec((8, 1024), lambda c: (0, 0)), pl.BlockSpec((8, 256), lambda c: (0, 0)),
                 pl.BlockSpec((8, 256), lambda c: (0, 0)), pl.BlockSpec((8, 128), lambda c: (0, 0)),
                 pl.BlockSpec((1, 1024), lambda c: (0, 0))]
    out_shape = [jax.ShapeDtypeStruct((s, 1024), BF16), jax.ShapeDtypeStruct((s, 1024), BF16),
                 jax.ShapeDtypeStruct((s, 256), BF16), jax.ShapeDtypeStruct((s, 256), BF16),
                 jax.ShapeDtypeStruct((s, 128), F32),
                 jax.ShapeDtypeStruct((8, 1024), F32), jax.ShapeDtypeStruct((8, 256), F32),
                 jax.ShapeDtypeStruct((8, 256), F32), jax.ShapeDtypeStruct((8, 128), F32),
                 jax.ShapeDtypeStruct((1, 1024), F32)]
    scratch = [pltpu.VMEM((N_PAIRS, 128, 128), F32), pltpu.VMEM((8, 1024), F32), pltpu.VMEM((8, 256), F32),
               pltpu.VMEM((8, 256), F32), pltpu.VMEM((L, 1024), F32), pltpu.VMEM((L, 256), F32), pltpu.VMEM((L, 256), F32),
               pltpu.VMEM((L, L), F32), pltpu.VMEM((8, 1024), F32), pltpu.VMEM((L, 1024), F32), pltpu.VMEM((L, 1024), F32)]
    assert (len(in_specs), len(out_specs), len(scratch)) == (n_in, n_out, n_scratch)
    outs = pl.pallas_call(
        body, name="ssd_bwd", grid=(nc,), in_specs=in_specs + [ANY] * ns, out_specs=out_specs + [ANY] * ns,
        out_shape=out_shape + _pair_swap_out_shapes(swap), scratch_shapes=scratch + (_pair_swap_scratch(ns) if ns else []),
        compiler_params=_params(("arbitrary",)),
    )(proj, proj, proj, proj, proj, proj, proj, conv_w8, conv_w8, conv_w8, conv_b, conv_b, conv_b, proj, smallp, norm_w,
      ypre, states, dy, sel, *swap)
    return (*outs[:n_out], list(outs[n_out:]))


FOX_SCALE = HEAD_DIM ** -0.5
FOX_T = 256
Q_COL, K_COL, V_COL = 2, 3, 4


def _split_dot(v, m, terms):
    out, rest = None, v
    for i in range(terms):
        piece = rest.astype(BF16)
        out = _dot(piece, m) if out is None else out + _dot(piece, m)
        if i + 1 < terms:
            rest = rest - piece.astype(F32)
    return out


def _split3_dot(v, m):
    return _split_dot(v, m, 3)


def _head_mean(x, sel_ref, selt_ref):
    return _dot(x, sel_ref[...]) * (1.0 / HEAD_DIM)


def _head_spread(v, selt_ref):
    return _split_dot(v, selt_ref[...], 2)


def _head_rstd(x, sel_ref, selt_ref):
    return _head_spread(lax.rsqrt(_head_mean(x * x, sel_ref, selt_ref) + NORM_EPS), selt_ref)


def fox_tables():
    r = np.arange(3 * 128)
    piece, lane = r // 128, r % 128
    head = lane - F_LANE
    is_head = np.logical_and(head >= 0, head < N_HEADS)
    col = 128 * (head // 2) + HEAD_DIM * (1 - head % 2) + piece
    cols = np.arange(1024)
    place_q = np.logical_and(is_head[:, None], cols[None, :] == col[:, None])
    place_k = np.logical_and(is_head[:, None], cols[None, :] == (col + 3)[:, None])
    ones_q = np.logical_and(cols % HEAD_DIM >= 3, cols % HEAD_DIM < 6)[None]
    ones_k = (cols % HEAD_DIM < 3)[None]
    h = np.arange(128) - F_LANE
    ok = np.logical_and(h >= 0, h < N_HEADS)
    same_pair = cols[:, None] // 128 == (h // 2)[None, :]
    fold_even = np.logical_and(np.logical_and(ok, h % 2 == 0)[None, :], same_pair)
    fold_odd = np.logical_and(np.logical_and(ok, h % 2 == 1)[None, :], same_pair)
    as_bf16 = lambda t: jnp.asarray(t.astype(np.float32), BF16)
    return (as_bf16(place_q), as_bf16(place_k), jnp.asarray(ones_q, F32), jnp.asarray(ones_k, F32),
            as_bf16(fold_even), as_bf16(fold_odd))


def fox_prep(proj, smallp, qw, kw, sel, selt, place_q, place_k, ones_q, ones_k, *, tm=256):
    s = proj.shape[0]

    def body(q_ref, k_ref, v_ref, small_ref, sp_ref, qw_ref, kw_ref, sel_ref, selt_ref, pq_ref, pk_ref, oq_ref, ok_ref,
             qn_ref, kn_ref, aq_ref, ak_ref, vb_ref, knt_ref, akt_ref, vt_ref, carry):
        @pl.when(pl.program_id(0) == 0)
        def _():
            carry[...] = jnp.zeros_like(carry)

        q = q_ref[...]
        qn_ref[...] = (((q * _head_rstd(q, sel_ref, selt_ref)) * qw_ref[...]) * FOX_SCALE).astype(BF16)
        k = k_ref[...]
        kn = ((k * _head_rstd(k, sel_ref, selt_ref)) * kw_ref[...]).astype(BF16)
        kn_ref[...] = kn
        knt_ref[...] = kn.astype(F32).T.astype(BF16)
        vb_ref[...] = v_ref[...].astype(BF16)
        vt_ref[...] = v_ref[...].T.astype(BF16)
        li = _lane_iota((tm, 128))
        f_lane = jnp.logical_and(li >= F_LANE, li < F_LANE + N_HEADS)
        logf = jnp.where(f_lane, -_softplus(-(small_ref[...] + sp_ref[3:4, :])), 0.0)
        cum = _cumsum_rows(logf) + carry[...]
        carry[...] = cum[tm - 1:tm, :]
        hi = cum.astype(BF16)
        r1 = cum - hi.astype(F32)
        mid = r1.astype(BF16)
        lo = (r1 - mid.astype(F32)).astype(BF16)
        pieces = jnp.concatenate([hi, mid, lo], axis=1)
        aq_ref[...] = (_dot(pieces, pq_ref[...]) + oq_ref[...]).astype(BF16)
        ak = ok_ref[...] - _dot(pieces, pk_ref[...])
        ak_ref[...] = ak.astype(BF16)
        akt_ref[...] = ak.T.astype(BF16)

    row = pl.BlockSpec((tm, 1024), lambda i: (i, 0))
    col = pl.BlockSpec((1024, tm), lambda i: (0, i))
    vec = pl.BlockSpec((1, 1024), lambda i: (0, 0))
    table = pl.BlockSpec((384, 1024), lambda i: (0, 0))
    wide = jax.ShapeDtypeStruct((s, 1024), BF16)
    tall = jax.ShapeDtypeStruct((1024, s), BF16)
    return pl.pallas_call(
        body, name="fox_prep", grid=(s // tm,),
        in_specs=[pl.BlockSpec((tm, 1024), lambda i: (i, Q_COL)), pl.BlockSpec((tm, 1024), lambda i: (i, K_COL)),
                  pl.BlockSpec((tm, 1024), lambda i: (i, V_COL)),
                  pl.BlockSpec((tm, 128), lambda i: (i, SMALL_BLOCK)), pl.BlockSpec((8, 128), lambda i: (0, 0)), vec, vec,
                  pl.BlockSpec((1024, 128), lambda i: (0, 0)), pl.BlockSpec((128, 1024), lambda i: (0, 0)),
                  table, table, vec, vec],
        out_specs=[row, row, row, row, row, col, col, col],
        out_shape=[wide, wide, wide, wide, wide, tall, tall, tall],
        scratch_shapes=[pltpu.VMEM((1, 128), F32)], compiler_params=_params(("arbitrary",)),
    )(proj, proj, proj, proj, smallp, qw, kw, sel, selt, place_q, place_k, ones_q, ones_k)


def fox_fwd(qn, kn, aq, ak, vt, shards=()):
    s = qn.shape[0]
    t = FOX_T
    nq = s // t
    ng = len(shards)

    def body(*refs):
        q_ref, k_ref, aq_ref, ak_ref, vt_ref = refs[:5]
        o_ref, ot_ref, lse_ref = refs[5 + ng:8 + ng]
        p = pl.program_id(0)
        if ng:
            start, forward, finish = _gather_phases(refs[5:5 + ng], refs[8 + ng:8 + 2 * ng], *refs[8 + 2 * ng:])
            pl.when(p == 0)(start)
            pl.when(p == N_PAIRS // 2)(forward)

        @pl.when(p == 0)
        def _():
            lse_ref[...] = jnp.zeros_like(lse_ref)

        lo = _lane_iota((t, 128)) < HEAD_DIM
        lo_rows = _row_iota((128, t)) < HEAD_DIM
        causal_t = _lane_iota((t, t)) >= _row_iota((t, t))

        def q_loop(qi, _):
            q0 = pl.multiple_of(qi * t, t)
            qv, aqv = q_ref[pl.ds(q0, t), :], aq_ref[pl.ds(q0, t), :]
            qa, qb = jnp.where(lo, qv, aqv), jnp.where(lo, aqv, qv)

            def scores(kj):
                k0 = pl.multiple_of(kj * t, t)
                kv, akv = k_ref[pl.ds(k0, t), :], ak_ref[pl.ds(k0, t), :]
                return _dot(jnp.where(lo, kv, akv), qa, "nt"), _dot(jnp.where(lo, akv, kv), qb, "nt")

            def update(kj, stats, s0, s1):
                m0, l0, m1, l1, acc = stats
                vtv = vt_ref[:, pl.ds(pl.multiple_of(kj * t, t), t)]
                n0 = jnp.maximum(m0, jnp.max(s0, axis=0, keepdims=True))
                n1 = jnp.maximum(m1, jnp.max(s1, axis=0, keepdims=True))
                a0, a1 = jnp.exp(m0 - n0), jnp.exp(m1 - n1)
                p0, p1 = jnp.exp(s0 - n0), jnp.exp(s1 - n1)
                l0 = a0 * l0 + jnp.sum(p0, axis=0, keepdims=True)
                l1 = a1 * l1 + jnp.sum(p1, axis=0, keepdims=True)
                acc = (jnp.where(lo_rows, a0, a1) * acc + _dot(jnp.where(lo_rows, vtv, 0.0), p0)
                       + _dot(jnp.where(lo_rows, 0.0, vtv), p1))
                return n0, l0, n1, l1, acc

            def step(kj, carry):
                stats, (s0, s1) = carry[:5], carry[5:]
                nxt = scores(kj + 1)
                return (*update(kj, stats, s0, s1), *nxt)

            def row(val):
                return jnp.full((1, t), val, F32)

            init = (row(NEG_BIG), row(0.0), row(NEG_BIG), row(0.0), jnp.zeros((128, t), F32), *scores(0))
            carry = lax.fori_loop(0, qi, step, init)
            s0, s1 = jnp.where(causal_t, carry[5], NEG_BIG), jnp.where(causal_t, carry[6], NEG_BIG)
            m0, l0, m1, l1, acc = update(qi, carry[:5], s0, s1)
            out_t = acc / jnp.where(lo_rows, l0, l1)
            ot_ref[:, pl.ds(q0, t)] = out_t.astype(BF16)
            o_ref[pl.ds(q0, t), :] = out_t.T.astype(BF16)
            ri = _row_iota((N_HEADS, t))
            old = lse_ref[:, pl.ds(q0, t)]
            lse_ref[:, pl.ds(q0, t)] = jnp.where(
                ri == 2 * p, m0 + jnp.log(l0), jnp.where(ri == 2 * p + 1, m1 + jnp.log(l1), old))
            return 0

        lax.fori_loop(0, nq, q_loop, 0)
        if ng:
            pl.when(p == N_PAIRS - 1)(finish)

    pair = pl.BlockSpec((s, 128), lambda p: (0, p))
    outs = pl.pallas_call(
        body, name="fox_fwd", grid=(N_PAIRS,),
        in_specs=[pair] * 4 + [pl.BlockSpec((128, s), lambda p: (p, 0))] + [ANY] * ng,
        out_specs=[pair, pl.BlockSpec((128, s), lambda p: (p, 0)), pl.BlockSpec((N_HEADS, s), lambda p: (0, 0))] + [ANY] * ng,
        out_shape=[jax.ShapeDtypeStruct((s, 1024), BF16), jax.ShapeDtypeStruct((1024, s), BF16),
                   jax.ShapeDtypeStruct((N_HEADS, s), F32)] + _gather_out_shapes(shards),
        scratch_shapes=_gather_scratch(ng) if ng else [],
        compiler_params=_params(("arbitrary",)),
    )(qn, kn, aq, ak, vt, *shards)
    return outs[0], outs[1], outs[2], list(outs[3:])


def fox_bwd(qn, kn, aq, ak, knt, akt, vb, lse, dmixed, parts=()):
    s = qn.shape[0]
    t = FOX_T
    nq = s // t
    once = pl.Buffered(1)
    ns = len(parts)

    def body(*refs):
        q_ref, k_ref, aq_ref, ak_ref, kt_ref, akt_ref, v_ref, lse_ref, do_ref = refs[:9]
        dq_ref, dk_ref, dv_ref, dc0_ref, dc1_ref = refs[9 + ns:14 + ns]
        p_scr, dp_scr = refs[14 + 2 * ns:16 + 2 * ns]
        p = pl.program_id(0)
        if ns:
            start, finish = _scatter_phases(refs[9:9 + ns], refs[14 + ns:14 + 2 * ns], *refs[16 + 2 * ns:])
            pl.when(p == 0)(start)
        dk_ref[...] = jnp.zeros_like(dk_ref)
        dv_ref[...] = jnp.zeros_like(dv_ref)
        dc0_ref[...] = jnp.zeros_like(dc0_ref)
        dc1_ref[...] = jnp.zeros_like(dc1_ref)
        lo = _lane_iota((t, 128)) < HEAD_DIM
        lo_rows = _row_iota((128, t)) < HEAD_DIM
        causal_t = _lane_iota((t, t)) >= _row_iota((t, t))

        def q_loop(qi, _):
            q0 = pl.multiple_of(qi * t, t)
            qv, aqv = q_ref[pl.ds(q0, t), :], aq_ref[pl.ds(q0, t), :]
            qa, qb = jnp.where(lo, qv, aqv), jnp.where(lo, aqv, qv)
            do = do_ref[pl.ds(q0, t), :]
            doa, dob = jnp.where(lo, do, 0.0).astype(BF16), jnp.where(lo, 0.0, do).astype(BF16)
            lse_blk = lse_ref[:, pl.ds(q0, t)]
            ri = _row_iota((N_HEADS, t))
            lse0 = jnp.sum(jnp.where(ri == 2 * p, lse_blk, 0.0), axis=0, keepdims=True)
            lse1 = jnp.sum(jnp.where(ri == 2 * p + 1, lse_blk, 0.0), axis=0, keepdims=True)

            def scores(kj):
                k0 = pl.multiple_of(kj * t, t)
                kv, akv = k_ref[pl.ds(k0, t), :], ak_ref[pl.ds(k0, t), :]
                return _dot(jnp.where(lo, kv, akv), qa, "nt"), _dot(jnp.where(lo, akv, kv), qb, "nt")

            def pass1(kj, d0, d1, diagonal):
                k0 = pl.multiple_of(kj * t, t)
                vv = v_ref[pl.ds(k0, t), :]
                s0, s1 = scores(kj)
                if diagonal:
                    s0, s1 = jnp.where(causal_t, s0, NEG_BIG), jnp.where(causal_t, s1, NEG_BIG)
                p0, p1 = jnp.exp(s0 - lse0), jnp.exp(s1 - lse1)
                dp0, dp1 = _dot(vv, doa, "nt"), _dot(vv, dob, "nt")
                p_scr[0, kj], p_scr[1, kj] = p0, p1
                dp_scr[0, kj], dp_scr[1, kj] = dp0, dp1
                dv_ref[pl.ds(k0, t), :] += _dot(p0, doa) + _dot(p1, dob)
                return d0 + jnp.sum(p0 * dp0, axis=0, keepdims=True), d1 + jnp.sum(p1 * dp1, axis=0, keepdims=True)

            zero = jnp.zeros((1, t), F32)
            d0, d1 = lax.fori_loop(0, qi, lambda kj, c: pass1(kj, *c, False), (zero, zero))
            d0, d1 = pass1(qi, d0, d1, True)

            def fold_lanes(v):
                return functools.reduce(lambda a, b: a + b, [v[:, 128 * i:128 * (i + 1)] for i in range(t // 128)])

            def pass2(kj, carry):
                dq0, dq1 = carry
                k0 = pl.multiple_of(kj * t, t)
                p0, p1 = p_scr[0, kj], p_scr[1, kj]
                ds0, ds1 = p0 * (dp_scr[0, kj] - d0), p1 * (dp_scr[1, kj] - d1)
                dk_ref[pl.ds(k0, t), :] += jnp.where(lo, _dot(ds0, qa), _dot(ds1, qb))
                dc0_ref[pl.ds(k0, t), :] += fold_lanes(ds0)
                dc1_ref[pl.ds(k0, t), :] += fold_lanes(ds1)
                ktv, aktv = kt_ref[:, pl.ds(k0, t)], akt_ref[:, pl.ds(k0, t)]
                return dq0 + _dot(jnp.where(lo_rows, ktv, aktv), ds0), dq1 + _dot(jnp.where(lo_rows, aktv, ktv), ds1)

            zq = jnp.zeros((128, t), F32)
            dq0, dq1 = lax.fori_loop(0, qi + 1, pass2, (zq, zq))
            dq_ref[pl.ds(q0, t), :] = jnp.where(lo_rows, dq0, dq1).T
            return 0

        lax.fori_loop(0, nq, q_loop, 0)
        if ns:
            pl.when(p == N_PAIRS - 1)(finish)

    pair = pl.BlockSpec((s, 128), lambda p: (0, p))
    pair_t = pl.BlockSpec((128, s), lambda p: (p, 0))
    out = jax.ShapeDtypeStruct((s, 1024), F32)
    outs = pl.pallas_call(
        body, name="fox_bwd", grid=(N_PAIRS,),
        in_specs=[pair, pair, pair, pair, pair_t, pair_t, pair, pl.BlockSpec((N_HEADS, s), lambda p: (0, 0)),
                  pl.BlockSpec((s, 128), lambda p: (0, 8 + p))] + [ANY] * ns,
        out_specs=[pl.BlockSpec((s, 128), lambda p: (0, p), pipeline_mode=once)] * 5 + [ANY] * ns,
        out_shape=[out] * 5 + [jax.ShapeDtypeStruct(p.shape, p.dtype) for p in parts],
        scratch_shapes=[pltpu.VMEM((2, nq, t, t), F32), pltpu.VMEM((2, nq, t, t), F32)] + (_scatter_scratch(ns) if ns else []),
        compiler_params=_params(("arbitrary",)),
    )(qn, kn, aq, ak, knt, akt, vb, lse, dmixed, *parts)
    return (*outs[:5], _keep_own_blocks(outs[5:], parts))


def fox_post(dqn, dkn, dc0, dc1, proj, smallp, qw, kw, sel, selt, fold_even, fold_odd, *, tm=256):
    s = proj.shape[0]
    nrow = s // tm

    def body(dqn_ref, dkn_ref, dc0_ref, dc1_ref, q_ref, k_ref, small_ref, sp_ref, qw_ref, kw_ref, sel_ref, selt_ref,
             fe_ref, fo_ref, dq_ref, dk_ref, dsmall_ref, gqw_ref, gkw_ref, gfb_ref, carry):
        step = pl.program_id(0)

        @pl.when(step == 0)
        def _():
            carry[...] = jnp.zeros_like(carry)

        def norm_bwd(x_ref, w_ref, dn, out_ref):
            x = x_ref[...]
            rf = _head_rstd(x, sel_ref, selt_ref)
            xh = x * rf
            g = dn * w_ref[...]
            mean_gx = _head_spread(_head_mean(g * xh, sel_ref, selt_ref), selt_ref)
            out_ref[...] = (rf * (g - xh * mean_gx)).astype(BF16)
            return jnp.sum(dn * xh, axis=0, keepdims=True)

        gqw = norm_bwd(q_ref, qw_ref, dqn_ref[...] * FOX_SCALE, dq_ref)
        gkw = norm_bwd(k_ref, kw_ref, dkn_ref[...], dk_ref)
        li = _lane_iota((tm, 128))
        f_lane = jnp.logical_and(li >= F_LANE, li < F_LANE + N_HEADS)
        dcum = -(_split3_dot(dc0_ref[...], fe_ref[...]) + _split3_dot(dc1_ref[...], fo_ref[...]))
        dlogf = _rev_cumsum_rows(dcum) + carry[...]
        carry[...] = dlogf[0:1, :]
        dfr = jnp.where(f_lane, dlogf * _sigmoid(-(small_ref[...] + sp_ref[3:4, :])), 0.0)
        dsmall_ref[...] = dfr
        gfb = jnp.sum(dfr, axis=0, keepdims=True)

        @pl.when(step == 0)
        def _():
            gqw_ref[...] = gqw
            gkw_ref[...] = gkw
            gfb_ref[...] = gfb

        @pl.when(step > 0)
        def _():
            gqw_ref[...] += gqw
            gkw_ref[...] += gkw
            gfb_ref[...] += gfb

    def rb(i):
        return nrow - 1 - i

    row = pl.BlockSpec((tm, 1024), lambda i: (rb(i), 0))
    vec = pl.BlockSpec((1, 1024), lambda i: (0, 0))
    fold = pl.BlockSpec((1024, 128), lambda i: (0, 0))
    return pl.pallas_call(
        body, name="fox_post", grid=(nrow,),
        in_specs=[row, row, row, row, pl.BlockSpec((tm, 1024), lambda i: (rb(i), Q_COL)),
                  pl.BlockSpec((tm, 1024), lambda i: (rb(i), K_COL)),
                  pl.BlockSpec((tm, 128), lambda i: (rb(i), SMALL_BLOCK)), pl.BlockSpec((8, 128), lambda i: (0, 0)), vec, vec,
                  fold, pl.BlockSpec((128, 1024), lambda i: (0, 0)), fold, fold],
        out_specs=[row, row, pl.BlockSpec((tm, 128), lambda i: (rb(i), 0)), vec, vec, pl.BlockSpec((1, 128), lambda i: (0, 0))],
        out_shape=[jax.ShapeDtypeStruct((s, 1024), BF16), jax.ShapeDtypeStruct((s, 1024), BF16),
                   jax.ShapeDtypeStruct((s, 128), F32), jax.ShapeDtypeStruct((1, 1024), F32),
                   jax.ShapeDtypeStruct((1, 1024), F32), jax.ShapeDtypeStruct((1, 128), F32)],
        scratch_shapes=[pltpu.VMEM((1, 128), F32)], compiler_params=_params(("arbitrary",)),
    )(dqn, dkn, dc0, dc1, proj, proj, proj, smallp, qw, kw, sel, selt, fold_even, fold_odd)


def local_step(x, target, wx, later_shards, ssd_cw8, ssd_cb, smallp, ssd_nw, qw_t, kw_t, sel, selt,
               norm_mix_w, norm_ffn_w, ffn_cw8, ffn_cb):
    proj, h_t = rms_in_proj(x, norm_mix_w, wx)
    y_ssd, y_ssd_t, ypre, states = ssd_fwd(proj, ssd_cw8, ssd_cb, smallp, ssd_nw)
    place_q, place_k, ones_q, ones_k, fold_even, fold_odd = fox_tables()
    qn, kn, aq, ak, vb, knt, akt, vt = fox_prep(proj, smallp, qw_t, kw_t, sel, selt, place_q, place_k, ones_q, ones_k)
    y_fox, y_fox_t, lse, (a_out, a_up, a_down) = fox_fwd(qn, kn, aq, ak, vt, shards=later_shards)
    w_out = a_out.reshape(2048, D_MODEL)
    w_down = a_down.reshape(D_FF, D_MODEL)
    s = x.shape[0]
    shard = lambda index: pl.BlockSpec((None, 1024, 1408), index)
    x1, hf, hf_t = out_proj_rms_fwd(y_ssd, y_fox, w_out, x, norm_ffn_w)
    hu, act, act_t = up_ffn_fwd(hf, a_up, ffn_cw8, ffn_cb)
    dy, sq = down_proj_loss(act, w_down, x1, target)

    dact = matmul(dy, w_down, mode="nt", tm=1024, tn=1408, tk=1024, out_dtype=F32, name="mm_dact")
    g_down = matmul(act_t, dy, mode="nn", tm=1408, tn=1024, tk=1024, out_dtype=BF16, name="mm_dw_down")
    dhu, gcw_g, gcw_v = ffn_mid_bwd(hu, dact, ffn_cw8, ffn_cb)
    g_up = matmul(hf_t, dhu, mode="nn", tm=1024, tn=1408, tk=1024, out_dtype=BF16, name="mm_dw_up",
                  layout=dict(m=D_MODEL, n=2 * D_FF, k=s, b_spec=shard(lambda i, j, kk: (j // 2, kk, j % 2)),
                              o_spec=shard(lambda i, j, kk: (j, i, 0)), out_shape=(4, D_MODEL, 1408)))
    dx1, g_norm_ffn, dmixed = rms_bwd_matmul(dhu, a_up, x1, norm_ffn_w, dy, w_out, name="dhf_rms_ffn_bwd_dmixed")
    g_out_a = matmul(y_ssd_t, dx1, mode="nn", tm=1024, tn=1024, tk=1024, out_dtype=BF16, name="mm_dw_out_ssd")
    g_out_b = matmul(y_fox_t, dx1, mode="nn", tm=1024, tn=1024, tk=1024, out_dtype=BF16, name="mm_dw_out_fox")
    early = [jnp.concatenate([g_out_a, g_out_b], axis=0).reshape(4, 512, D_MODEL), g_up, g_down.reshape(4, 704, D_MODEL)]
    dz, dxs, db, dc, dsmall_ssd, gcw_x, gcw_b, gcw_c, g_sp, g_ssd_nw, theirs = ssd_bwd(
        proj, ssd_cw8, ssd_cb, smallp, ssd_nw, ypre, states, dmixed, sel, swap=early)
    core = lax.axis_index("c").astype(jnp.int32).reshape(1)
    parts = [add_pair(a, b, core, name="add_pair_" + n, tr=ADAM_ROWS[n]) for a, b, n in zip(early, theirs, BIG_NAMES[1:])]
    dqn, dkn, dv, dc0, dc1, landed_early = fox_bwd(qn, kn, aq, ak, knt, akt, vb, lse, dmixed, parts=parts)
    dq, dk, dsmall_fox, g_qw, g_kw, g_fb = fox_post(dqn, dkn, dc0, dc1, proj, smallp, qw_t, kw_t, sel, selt,
                                                    fold_even, fold_odd)
    dproj = jnp.concatenate([dz, dxs, dq, dk, dv.astype(BF16), db, dc, (dsmall_ssd + dsmall_fox).astype(BF16)], axis=1)
    g_wx = matmul(h_t, dproj, mode="nn", tm=1024, tn=PROJ_TILE, tk=1024, out_dtype=BF16, name="mm_dw_in")
    g_in = _in_grad_shards(g_wx)
    part_in = add_pair(g_in, pair_swap_halves([g_in], name="pair_swap_w_in")[0], core, name="add_pair_w_in",
                       tr=ADAM_ROWS["w_in"])
    grad_x, g_norm_mix, landed_in = matmul_rms_bwd(dproj, wx, x, norm_mix_w, dx1, scatter=[part_in])
    return dict(
        sq=sq, grad_x=grad_x, landed=landed_in + landed_early,
        g_norm_mix=g_norm_mix, g_norm_ffn=g_norm_ffn, g_ssd_nw=g_ssd_nw,
        g_ssd_cw=jnp.concatenate([gcw_x, gcw_b, gcw_c], axis=1), g_sp=g_sp, g_fb=g_fb, g_qw=g_qw, g_kw=g_kw,
        g_ffn_cw=jnp.concatenate([gcw_g, gcw_v], axis=1))


def adamw(w, g, m, v, *, name, tr, allreduce=None):
    rows, cols = w.shape
    nsteps = rows // tr

    def body(*refs):
        if allreduce is None:
            w_ref, g_ref, m_ref, v_ref, d_ref, mo_ref, vo_ref = refs
        else:
            w_ref, g_ref, m_ref, v_ref, packed_ref, d_ref, mo_ref, vo_ref, summed_ref = refs[:9]
            start, finish = _allreduce_phases(packed_ref, summed_ref, *refs[9:])
            pl.when(pl.program_id(0) == 0)(start)
        gv = g_ref[...]
        mn = ADAM_B1 * m_ref[...] + (1.0 - ADAM_B1) * gv
        vn = ADAM_B2 * v_ref[...] + (1.0 - ADAM_B2) * (gv * gv)
        m_hat = mn / (1.0 - ADAM_B1 ** ADAM_STEP)
        v_hat = vn / (1.0 - ADAM_B2 ** ADAM_STEP)
        d_ref[...] = -ADAM_LR * (m_hat / (jnp.sqrt(v_hat) + ADAM_EPS) + ADAM_WD * w_ref[...])
        mo_ref[...] = mn
        vo_ref[...] = vn
        if allreduce is not None:
            pl.when(pl.program_id(0) == nsteps - 1)(finish)

    blk = pl.BlockSpec((tr, cols), lambda i: (i, 0))
    shp = jax.ShapeDtypeStruct((rows, cols), F32)
    if allreduce is None:
        return pl.pallas_call(
            body, name=name, grid=(nsteps,), in_specs=[blk] * 4, out_specs=[blk] * 3, out_shape=[shp] * 3,
            compiler_params=_params(("parallel",)),
        )(w, g, m, v)
    whole = pl.BlockSpec(memory_space=pltpu.VMEM)
    return pl.pallas_call(
        body, name=name, grid=(nsteps,), in_specs=[blk] * 4 + [whole], out_specs=[blk] * 3 + [whole],
        out_shape=[shp] * 3 + [jax.ShapeDtypeStruct(allreduce.shape, F32)],
        scratch_shapes=_allreduce_scratch(allreduce.shape[0]), compiler_params=_params(("arbitrary",)),
    )(w, g, m, v, allreduce)


def add_pair(full, theirs, core, *, name, tr):
    _, rows, cols = theirs.shape
    nblk = rows // tr

    def body(c_ref, a_ref, b_ref, o_ref):
        o_ref[...] = (a_ref[...].astype(F32) + b_ref[...].astype(F32)).astype(BF16)

    blk = pl.BlockSpec((1, tr, cols), lambda j, i, c: (j, i, 0))
    grid_spec = pltpu.PrefetchScalarGridSpec(
        num_scalar_prefetch=1, grid=(4, nblk),
        in_specs=[pl.BlockSpec((1, tr, cols), lambda j, i, c: (j, c[0] * nblk + i, 0)), blk], out_specs=blk)
    return pl.pallas_call(
        body, name=name, grid_spec=grid_spec, out_shape=jax.ShapeDtypeStruct(theirs.shape, BF16),
        compiler_params=_params(("parallel", "parallel")),
    )(core, full, theirs)


def sum_chips(parts, core, *, name, tr):
    _, rows, cols = parts.shape
    nblk = rows // tr

    def body(c_ref, p_ref, o_ref):
        acc = p_ref[0].astype(F32)
        for k in range(1, 4):
            acc = acc + p_ref[k].astype(F32)
        o_ref[...] = acc

    grid_spec = pltpu.PrefetchScalarGridSpec(
        num_scalar_prefetch=1, grid=(nblk,), in_specs=[pl.BlockSpec((4, tr, cols), lambda i, c: (0, i, 0))],
        out_specs=pl.BlockSpec((tr, cols), lambda i, c: (c[0] * nblk + i, 0)))
    return pl.pallas_call(
        body, name=name, grid_spec=grid_spec, out_shape=jax.ShapeDtypeStruct((2 * rows, cols), F32),
        compiler_params=_params(("parallel",)),
    )(core, parts)


ANY = pl.BlockSpec(memory_space=pl.ANY)


def _place():
    x, y, c = lax.axis_index("x"), lax.axis_index("y"), lax.axis_index("c")
    chips = [(1 - x, y), (x, 1 - y), (1 - x, 1 - y)]
    return x, y, c, chips


def _chunks(rows):
    size = next((c for c in (128, 176, 64, 32, 16, 8) if rows % c == 0), rows)
    return [(r, size) for r in range(0, rows, size)]


def gather_weights(shards):
    n = len(shards)

    def body(*refs):
        start, forward, finish = _gather_phases(refs[:n], refs[n:2 * n], *refs[2 * n:])
        start()
        forward()
        finish()

    gathered = pl.pallas_call(
        body, name="gather_weights", in_specs=[ANY] * n, out_specs=[ANY] * n,
        out_shape=_gather_out_shapes(shards), scratch_shapes=_gather_scratch(n),
    )(*shards)
    return gathered


def _gather_out_shapes(shards):
    return [jax.ShapeDtypeStruct((4,) + s.shape, s.dtype) for s in shards]


def _gather_scratch(n):
    return [pltpu.SemaphoreType.DMA((n, 7)), pltpu.SemaphoreType.DMA((n, 7))]


def _gather_phases(ins, outs, send_sems, recv_sems):
    n = len(ins)
    x, y, c, chips = _place()
    me = 2 * x + y
    sibling = (x, y, 1 - c)
    blks = [2 * cx + cy for cx, cy in chips]

    def half(a, blk, r=0, nr=None):
        rows = ins[a].shape[0] // 2
        return outs[a].at[blk, pl.ds(c * rows + r, rows if nr is None else nr), :]

    def to_chip(a, t, r=0, nr=None):
        rows = ins[a].shape[0] // 2
        return pltpu.make_async_remote_copy(
            src_ref=ins[a].at[pl.ds(c * rows + r, rows if nr is None else nr), :], dst_ref=half(a, me, r, nr),
            send_sem=send_sems.at[a, t], recv_sem=recv_sems.at[a, t], device_id=(*chips[t], c), device_id_type=MESH)

    def from_chip(a, t):
        return pltpu.make_async_remote_copy(
            src_ref=half(a, blks[t]), dst_ref=half(a, blks[t]), send_sem=send_sems.at[a, t], recv_sem=recv_sems.at[a, t],
            device_id=(*chips[t], c), device_id_type=MESH)

    def to_sibling(a, t, r=0, nr=None):
        return pltpu.make_async_remote_copy(
            src_ref=half(a, blks[t], r, nr), dst_ref=half(a, blks[t], r, nr), send_sem=send_sems.at[a, 3 + t],
            recv_sem=recv_sems.at[a, 3 + t], device_id=sibling, device_id_type=MESH)

    def from_sibling(a, t):
        rows = ins[a].shape[0] // 2
        dst = outs[a].at[blks[t], pl.ds((1 - c) * rows, rows), :]
        return pltpu.make_async_remote_copy(
            src_ref=dst, dst_ref=dst, send_sem=send_sems.at[a, 3 + t], recv_sem=recv_sems.at[a, 3 + t],
            device_id=sibling, device_id_type=MESH)

    def own(a, r=0, nr=None):
        return pltpu.make_async_remote_copy(
            src_ref=ins[a].at[pl.ds(r, ins[a].shape[0] if nr is None else nr), :],
            dst_ref=outs[a].at[me, pl.ds(r, ins[a].shape[0] if nr is None else nr), :],
            send_sem=send_sems.at[a, 6], recv_sem=recv_sems.at[a, 6], device_id=sibling, device_id_type=MESH)

    def start():
        for a in range(n):
            for t in range(3):
                for r, nr in _chunks(ins[a].shape[0] // 2):
                    to_chip(a, t, r, nr).start()
            for r, nr in _chunks(ins[a].shape[0]):
                own(a, r, nr).start()

    def forward():
        for a in range(n):
            for t in range(3):
                from_chip(a, t).wait_recv()
                for r, nr in _chunks(ins[a].shape[0] // 2):
                    to_sibling(a, t, r, nr).start()

    def finish():
        for a in range(n):
            for t in range(3):
                from_sibling(a, t).wait_recv()
        for a in range(n):
            for t in range(3):
                to_chip(a, t).wait_send()
                to_sibling(a, t).wait_send()
            own(a).wait()

    return start, forward, finish


def pair_swap_halves(grads, *, name):
    n = len(grads)

    def body(*refs):
        start, finish = _pair_swap_phases(refs[:n], refs[n:2 * n], *refs[2 * n:])
        start()
        finish()

    return pl.pallas_call(
        body, name=name, in_specs=[ANY] * n, out_specs=[ANY] * n, out_shape=_pair_swap_out_shapes(grads),
        scratch_shapes=_pair_swap_scratch(n),
    )(*grads)


def _pair_swap_out_shapes(grads):
    return [jax.ShapeDtypeStruct((4, g.shape[1] // 2, g.shape[2]), g.dtype) for g in grads]


def _pair_swap_scratch(n):
    return [pltpu.SemaphoreType.DMA((n,)), pltpu.SemaphoreType.DMA((n,))]


def _pair_swap_phases(ins, theirs, send_sems, recv_sems):
    n = len(ins)
    x, y, c, _ = _place()
    sibling = (x, y, 1 - c)

    def start():
        for a in range(n):
            rows = ins[a].shape[1] // 2
            for j in range(4):
                for r, nr in _chunks(rows):
                    pltpu.make_async_remote_copy(
                        src_ref=ins[a].at[j, pl.ds((1 - c) * rows + r, nr), :], dst_ref=theirs[a].at[j, pl.ds(r, nr), :],
                        send_sem=send_sems.at[a], recv_sem=recv_sems.at[a], device_id=sibling, device_id_type=MESH).start()

    def finish():
        for a in range(n):
            pltpu.make_async_remote_copy(src_ref=theirs[a], dst_ref=theirs[a], send_sem=send_sems.at[a],
                                         recv_sem=recv_sems.at[a], device_id=sibling, device_id_type=MESH).wait()

    return start, finish


def _scatter_scratch(n):
    return [pltpu.SemaphoreType.DMA((n, 3)), pltpu.SemaphoreType.DMA((n, 3))]


def _keep_own_blocks(landed, parts):
    if not parts:
        return []
    chip = 2 * lax.axis_index("x") + lax.axis_index("y")
    return [lax.dynamic_update_slice(l, lax.dynamic_slice_in_dim(p, chip, 1, axis=0), (chip, 0, 0))
            for l, p in zip(landed, parts)]


def _scatter_phases(ins, outs, send_sems, recv_sems):
    n = len(ins)
    x, y, c, chips = _place()
    me = 2 * x + y
    blks = [2 * cx + cy for cx, cy in chips]

    def start():
        for a in range(n):
            for r, nr in _chunks(ins[a].shape[1]):
                for t in range(3):
                    pltpu.make_async_remote_copy(
                        src_ref=ins[a].at[blks[t], pl.ds(r, nr), :], dst_ref=outs[a].at[me, pl.ds(r, nr), :],
                        send_sem=send_sems.at[a, t], recv_sem=recv_sems.at[a, t],
                        device_id=(*chips[t], c), device_id_type=MESH).start()

    def finish():
        for a in range(n):
            for t in range(3):
                pltpu.make_async_remote_copy(
                    src_ref=outs[a].at[blks[t]], dst_ref=outs[a].at[blks[t]], send_sem=send_sems.at[a, t],
                    recv_sem=recv_sems.at[a, t], device_id=(*chips[t], c), device_id_type=MESH).wait()

    return start, finish


def pair_join_halves(bufs):
    n = len(bufs)

    def body(*refs):
        outs = refs[n:2 * n]
        send_sems, recv_sems = refs[2 * n:]
        x, y, c, _ = _place()
        sibling = (x, y, 1 - c)
        for a in range(n):
            rows = outs[a].shape[0] // 2
            for r, nr in _chunks(rows):
                mine = outs[a].at[pl.ds(c * rows + r, nr), :]
                pltpu.make_async_remote_copy(src_ref=mine, dst_ref=mine, send_sem=send_sems.at[a], recv_sem=recv_sems.at[a],
                                             device_id=sibling, device_id_type=MESH).start()
        for a in range(n):
            rows = outs[a].shape[0] // 2
            pltpu.make_async_remote_copy(
                src_ref=outs[a].at[pl.ds(c * rows, rows), :], dst_ref=outs[a].at[pl.ds((1 - c) * rows, rows), :],
                send_sem=send_sems.at[a], recv_sem=recv_sems.at[a], device_id=sibling, device_id_type=MESH).wait()

    return pl.pallas_call(
        body, name="pair_join_halves", in_specs=[ANY] * n, out_specs=[ANY] * n,
        out_shape=[jax.ShapeDtypeStruct(b.shape, b.dtype) for b in bufs], input_output_aliases={a: a for a in range(n)},
        scratch_shapes=[pltpu.SemaphoreType.DMA((n,)), pltpu.SemaphoreType.DMA((n,))],
    )(*bufs)


def _allreduce_scratch(rows):
    return [pltpu.VMEM((8, rows, 128), F32), pltpu.SemaphoreType.DMA((7,)), pltpu.SemaphoreType.DMA((7,))]


def _allreduce_phases(in_ref, out_ref, gathered, send_sems, recv_sems):
    x, y, c, _ = _place()
    me = 4 * x + 2 * y + c
    flips = [(fx, fy, fc) for fx in (0, 1) for fy in (0, 1) for fc in (0, 1)][1:]
    peers = [((1 - x) if fx else x, (1 - y) if fy else y, (1 - c) if fc else c) for fx, fy, fc in flips]

    def send(t):
        return pltpu.make_async_remote_copy(
            src_ref=in_ref, dst_ref=gathered.at[me], send_sem=send_sems.at[t], recv_sem=recv_sems.at[t],
            device_id=peers[t], device_id_type=MESH)

    def start():
        gathered[me] = in_ref[...]
        for t in range(7):
            send(t).start()

    def finish():
        for t, (px, py, pc) in enumerate(peers):
            slot = gathered.at[4 * px + 2 * py + pc]
            pltpu.make_async_remote_copy(
                src_ref=slot, dst_ref=slot, send_sem=send_sems.at[t], recv_sem=recv_sems.at[t],
                device_id=(px, py, pc), device_id_type=MESH).wait_recv()
        for t in range(7):
            send(t).wait_send()
        acc = gathered[0]
        for k in range(1, 8):
            acc = acc + gathered[k]
        out_ref[...] = acc

    return start, finish


SMALL_NAMES = ("norm_mix_w", "ssd_conv_w", "ssd_conv_b", "ssd_dt_bias", "ssd_a_log", "ssd_d", "ssd_norm_w", "fox_f_bias",
               "fox_q_norm_w", "fox_k_norm_w", "norm_ffn_w", "ffn_conv_w", "ffn_conv_b")
BIG_NAMES = ("w_in", "w_out", "w_up", "w_down")
WEIGHT_ORDER = ("norm_mix_w", "w_in", "ssd_conv_w", "ssd_conv_b", "ssd_dt_bias", "ssd_a_log", "ssd_d", "ssd_norm_w",
                "fox_f_bias", "fox_q_norm_w", "fox_k_norm_w", "w_out", "norm_ffn_w", "w_up", "ffn_conv_w", "ffn_conv_b", "w_down")
ADAM_ROWS = {"w_in": 256, "w_out": 256, "w_up": 256, "w_down": 176}


def _pack(arrays):
    pieces = []
    for a in arrays:
        flat = a.reshape(-1).astype(F32)
        pieces += [flat, jnp.zeros(((-flat.shape[0]) % 1024,), F32)]
    return jnp.concatenate(pieces).reshape(-1, 128)


def _unpack(packed, shapes):
    out, r = [], 0
    for shp in shapes:
        size = 1
        for d in shp:
            size *= d
        nrow = 8 * (-(-size // 1024))
        out.append(packed[r:r + nrow].reshape(-1)[:size].reshape(shp))
        r += nrow
    return out


IN_SHARD = IN_COLS // 4
IN_SEGMENTS = ((0, 2048, 0), (2048, 2560, 5120), (2560, 2576, MAIN_COLS), (2576, 5648, 2048), (5648, 5664, MAIN_COLS + F_LANE))


def _in_cols(shards, lo, hi):
    out = []
    for j in range(4):
        a, b = max(lo, IN_SHARD * j), min(hi, IN_SHARD * (j + 1))
        if a < b:
            out.append(shards[j][:, a - IN_SHARD * j:b - IN_SHARD * j])
    return out


def _in_grad_shards(g):
    shards = []
    for j in range(4):
        pieces = []
        for lo, hi, at in IN_SEGMENTS:
            a, b = max(lo, IN_SHARD * j), min(hi, IN_SHARD * (j + 1))
            if a < b:
                pieces.append(g[:, at + a - lo:at + b - lo])
        shards.append(jnp.concatenate(pieces, axis=1))
    return jnp.stack(shards)


def _pad_rows(a, rows):
    return jnp.pad(a, ((0, rows - a.shape[0]), (0, 0)))


def kernel(x, norm_mix_w, w_in, ssd_conv_w, ssd_conv_b, ssd_dt_bias, ssd_a_log, ssd_d, ssd_norm_w, fox_f_bias, fox_q_norm_w, fox_k_norm_w, w_out, norm_ffn_w, w_up, ffn_conv_w, ffn_conv_b, w_down, loss_target, m_norm_mix_w, m_w_in, m_ssd_conv_w, m_ssd_conv_b, m_ssd_dt_bias, m_ssd_a_log, m_ssd_d, m_ssd_norm_w, m_fox_f_bias, m_fox_q_norm_w, m_fox_k_norm_w, m_w_out, m_norm_ffn_w, m_w_up, m_ffn_conv_w, m_ffn_conv_b, m_w_down, v_norm_mix_w, v_w_in, v_ssd_conv_w, v_ssd_conv_b, v_ssd_dt_bias, v_ssd_a_log, v_ssd_d, v_ssd_norm_w, v_fox_f_bias, v_fox_q_norm_w, v_fox_k_norm_w, v_w_out, v_norm_ffn_w, v_w_up, v_ffn_conv_w, v_ffn_conv_b, v_w_down):
    w = dict(norm_mix_w=norm_mix_w, w_in=w_in, ssd_conv_w=ssd_conv_w, ssd_conv_b=ssd_conv_b, ssd_dt_bias=ssd_dt_bias,
             ssd_a_log=ssd_a_log, ssd_d=ssd_d, ssd_norm_w=ssd_norm_w, fox_f_bias=fox_f_bias, fox_q_norm_w=fox_q_norm_w,
             fox_k_norm_w=fox_k_norm_w, w_out=w_out, norm_ffn_w=norm_ffn_w, w_up=w_up, ffn_conv_w=ffn_conv_w,
             ffn_conv_b=ffn_conv_b, w_down=w_down)
    m = dict(norm_mix_w=m_norm_mix_w, w_in=m_w_in, ssd_conv_w=m_ssd_conv_w, ssd_conv_b=m_ssd_conv_b, ssd_dt_bias=m_ssd_dt_bias,
             ssd_a_log=m_ssd_a_log, ssd_d=m_ssd_d, ssd_norm_w=m_ssd_norm_w, fox_f_bias=m_fox_f_bias, fox_q_norm_w=m_fox_q_norm_w,
             fox_k_norm_w=m_fox_k_norm_w, w_out=m_w_out, norm_ffn_w=m_norm_ffn_w, w_up=m_w_up, ffn_conv_w=m_ffn_conv_w,
             ffn_conv_b=m_ffn_conv_b, w_down=m_w_down)
    v = dict(norm_mix_w=v_norm_mix_w, w_in=v_w_in, ssd_conv_w=v_ssd_conv_w, ssd_conv_b=v_ssd_conv_b, ssd_dt_bias=v_ssd_dt_bias,
             ssd_a_log=v_ssd_a_log, ssd_d=v_ssd_d, ssd_norm_w=v_ssd_norm_w, fox_f_bias=v_fox_f_bias, fox_q_norm_w=v_fox_q_norm_w,
             fox_k_norm_w=v_fox_k_norm_w, w_out=v_w_out, norm_ffn_w=v_norm_ffn_w, w_up=v_w_up, ffn_conv_w=v_ffn_conv_w,
             ffn_conv_b=v_ffn_conv_b, w_down=v_w_down)
    chip = 2 * lax.axis_index("x") + lax.axis_index("y")

    a_in, a_scw, a_fcw = gather_weights([w_in[0].astype(BF16), _pad_rows(ssd_conv_w[0], 16), _pad_rows(ffn_conv_w[0], 16)])
    later_shards = [w_out[0].astype(BF16), w_up[0].astype(BF16), w_down[0].astype(BF16)]
    wx = jnp.concatenate([p for lo, hi, _ in sorted(IN_SEGMENTS, key=lambda seg: seg[2]) for p in _in_cols(a_in, lo, hi)]
                         + [jnp.zeros((D_MODEL, PROJ_COLS - IN_COLS), BF16)], axis=1)
    ssd_cw8 = a_scw.transpose(1, 0, 2).reshape(16, 1536)[:8]
    ffn_cw8 = a_fcw.transpose(1, 0, 2).reshape(16, 2 * D_FF)[:8]
    gap = lambda n: jnp.zeros((n,), F32)
    smallp = jnp.concatenate([ssd_dt_bias[0], gap(112), ssd_a_log[0], gap(112), ssd_d[0], gap(112),
                              gap(F_LANE), fox_f_bias[0], gap(128 - F_LANE - N_HEADS), gap(4 * 128)]).reshape(8, 128)
    qw_t = jnp.tile(fox_q_norm_w[0], N_HEADS)[None]
    kw_t = jnp.tile(fox_k_norm_w[0], N_HEADS)[None]
    sel = jnp.asarray((np.arange(1024)[:, None] // HEAD_DIM == np.arange(128)[None, :]).astype(np.float32), BF16)

    res = local_step(x[0], loss_target[0], wx, later_shards, ssd_cw8, ssd_conv_b, smallp, ssd_norm_w, qw_t, kw_t,
                     sel, sel.T, norm_mix_w, norm_ffn_w, ffn_cw8, ffn_conv_b)

    full_shapes = [(1, 1024), (1, 4, 1536), (1, 1536), (1, 16), (1, 16), (1, 16), (1, 1024), (1, 16), (1, 64), (1, 64),
                   (1, 1024), (1, 3, 2 * D_FF), (1, 2 * D_FF), (1,)]
    local_small = [res["g_norm_mix"], res["g_ssd_cw"][:4], res["g_ssd_cw"][4], res["g_sp"][0, :16], res["g_sp"][1, :16],
                   res["g_sp"][2, :16], res["g_ssd_nw"], res["g_fb"][0, F_LANE:F_LANE + 16],
                   res["g_qw"].reshape(N_HEADS, HEAD_DIM).sum(0), res["g_kw"].reshape(N_HEADS, HEAD_DIM).sum(0),
                   res["g_norm_ffn"], res["g_ffn_cw"][:3], res["g_ffn_cw"][3], jnp.sum(res["sq"])]
    landed = res["landed"]
    core = lax.axis_index("c").astype(jnp.int32).reshape(1)
    halves = [sum_chips(p, core, name="sum_chips_" + n, tr=ADAM_ROWS[n]) for p, n in zip(landed, BIG_NAMES)]
    g_big = dict(zip(BIG_NAMES, pair_join_halves(halves)))

    grads, deltas, new_m, new_v = {}, {}, {}, {}
    for n in BIG_NAMES:
        out = adamw(w[n][0], g_big[n], m[n][0], v[n][0], name="adamw_" + n, tr=ADAM_ROWS[n],
                    allreduce=_pack(local_small) if n == BIG_NAMES[0] else None)
        if n == BIG_NAMES[0]:
            summed = _unpack(out[3], full_shapes)
        d, mn, vn = out[:3]
        grads[n], deltas[n], new_m[n], new_v[n] = g_big[n][None], d[None], mn[None], vn[None]
    loss = (0.5 / D_MODEL) * summed[-1][0]
    g_small = dict(zip(SMALL_NAMES, summed[:-1]))
    g_small["ssd_conv_w"] = lax.dynamic_slice(g_small["ssd_conv_w"], (0, 0, 384 * chip), (1, 4, 384))
    g_small["ffn_conv_w"] = lax.dynamic_slice(g_small["ffn_conv_w"], (0, 0, 1408 * chip), (1, 3, 1408))
    shapes = [w[n].shape for n in SMALL_NAMES]
    packed_w = _pack([w[n] for n in SMALL_NAMES])
    d, mn, vn = adamw(packed_w, _pack([g_small[n] for n in SMALL_NAMES]), _pack([m[n] for n in SMALL_NAMES]),
                      _pack([v[n] for n in SMALL_NAMES]), name="adamw_small", tr=packed_w.shape[0])
    for n, dd, mm, vv in zip(SMALL_NAMES, _unpack(d, shapes), _unpack(mn, shapes), _unpack(vn, shapes)):
        grads[n], deltas[n], new_m[n], new_v[n] = g_small[n].reshape(w[n].shape), dd, mm, vv
    return (loss, res["grad_x"][None], *[grads[n] for n in WEIGHT_ORDER], *[deltas[n] for n in WEIGHT_ORDER],
            *[new_m[n] for n in WEIGHT_ORDER], *[new_v[n] for n in WEIGHT_ORDER])
```

```python
import functools

import jax
import jax.numpy as jnp
import numpy as np
from jax import lax
from jax.experimental import pallas as pl
from jax.experimental.pallas import tpu as pltpu

F32 = jnp.float32
BF16 = jnp.bfloat16
MESH = pl.DeviceIdType.MESH

D_MODEL = 1024
HEAD_DIM = 64
N_HEADS = 16
N_PAIRS = N_HEADS // 2
SSD_CHUNK = 128
SSD_STATE = 128
SSD_CONV = 4
D_FF = 2816
FFN_CONV = 3
NORM_EPS = 1e-6
MAIN_COLS = 5632
SMALL_COLS = 128
PROJ_COLS = MAIN_COLS + SMALL_COLS
SMALL_BLOCK = MAIN_COLS // SMALL_COLS
PROJ_TILE = 1152
F_LANE = 16
IN_COLS = 5664

ADAM_LR = 0.001
ADAM_B1 = 0.9
ADAM_B2 = 0.999
ADAM_EPS = 1e-08
ADAM_WD = 0.01
ADAM_STEP = 10

VMEM_LIMIT_V7X = 56 * 1024 * 1024
NEG_BIG = -1e30


def _params(sem=None):
    return pltpu.CompilerParams(dimension_semantics=sem, vmem_limit_bytes=VMEM_LIMIT_V7X)


def _sigmoid(x):
    return pl.reciprocal(1.0 + jnp.exp(-x), approx=True)


def _silu_and_grad(x):
    s = _sigmoid(x)
    return x * s, s * (1.0 + x * (1.0 - s))


def _shift_down(v, j):
    return v if j == 0 else pltpu.roll(v, j, 0)


def _shift_up(v, j):
    return v if j == 0 else pltpu.roll(v, v.shape[0] - j, 0)


def _row_iota(shape):
    return lax.broadcasted_iota(jnp.int32, shape, 0)


def _lane_iota(shape):
    return lax.broadcasted_iota(jnp.int32, shape, 1)


def _dot(a, b, mode="nn"):
    dims = {"nn": (((1,), (0,)), ((), ())), "nt": (((1,), (1,)), ((), ())), "tn": (((0,), (0,)), ((), ()))}[mode]
    return lax.dot_general(a.astype(BF16), b.astype(BF16), dims, preferred_element_type=F32)


def _dot_f32(a, b):
    return jnp.dot(a, b, precision=lax.Precision.HIGHEST, preferred_element_type=F32)


def matmul(a, b, *, mode, tm, tn, tk, out_dtype, name, layout=None):
    layout = layout or {}
    if layout:
        m, n, k = layout["m"], layout["n"], layout["k"]
    else:
        (m, k), n = a.shape, (b.shape[1] if mode == "nn" else b.shape[0])
    assert m % tm == 0 and n % tn == 0 and k % tk == 0, (name, m, n, k, tm, tn, tk)
    nk = k // tk
    a_spec = layout.get("a_spec") or pl.BlockSpec((tm, tk), lambda i, j, kk: (i, kk))
    b_spec = layout.get("b_spec") or (pl.BlockSpec((tn, tk), lambda i, j, kk: (j, kk)) if mode == "nt"
                                      else pl.BlockSpec((tk, tn), lambda i, j, kk: (kk, j)))
    o_spec = layout.get("o_spec") or pl.BlockSpec((tm, tn), lambda i, j, kk: (i, j))

    def body(a_ref, b_ref, o_ref, acc_ref):
        kk = pl.program_id(2)
        part = _dot(a_ref[...], b_ref[...], mode)
        if nk == 1:
            o_ref[...] = part.astype(out_dtype)
        else:
            @pl.when(kk == 0)
            def _():
                acc_ref[...] = part

            @pl.when(jnp.logical_and(kk > 0, kk < nk - 1))
            def _():
                acc_ref[...] += part

            @pl.when(kk == nk - 1)
            def _():
                o_ref[...] = (acc_ref[...] + part).astype(out_dtype)

    return pl.pallas_call(
        body, name=name, grid=(m // tm, n // tn, nk), in_specs=[a_spec, b_spec], out_specs=o_spec,
        out_shape=jax.ShapeDtypeStruct(layout.get("out_shape", (m, n)), out_dtype),
        scratch_shapes=[pltpu.VMEM((tm, tn) if nk > 1 else (8, 128), F32)],
        compiler_params=_params(("parallel", "parallel", "arbitrary")),
    )(a, b)


def rms_in_proj(x, w, wx, *, tm=512):
    s, d = x.shape

    def body(x_ref, w_ref, wx_ref, proj_ref, ht_ref):
        for r in range(0, tm, UP_ROWS):
            rows = slice(r, r + UP_ROWS)
            xv = x_ref[rows, :]
            rstd = lax.rsqrt(jnp.mean(xv * xv, axis=-1, keepdims=True) + NORM_EPS)
            h = (xv * rstd) * w_ref[...]
            ht_ref[:, rows] = h.T.astype(BF16)
            proj_ref[rows, :] = _dot(h, wx_ref[...])

    return pl.pallas_call(
        body, name="rms_in_proj", grid=(s // tm,),
        in_specs=[pl.BlockSpec((tm, d), lambda i: (i, 0)), pl.BlockSpec((1, d), lambda i: (0, 0)),
                  pl.BlockSpec((d, PROJ_COLS), lambda i: (0, 0), pipeline_mode=pl.Buffered(1))],
        out_specs=[pl.BlockSpec((tm, PROJ_COLS), lambda i: (i, 0)), pl.BlockSpec((d, tm), lambda i: (0, i))],
        out_shape=[jax.ShapeDtypeStruct((s, PROJ_COLS), F32), jax.ShapeDtypeStruct((d, s), BF16)],
        compiler_params=_params(("parallel",)),
    )(x, w, wx)


def matmul_rms_bwd(dproj, wx, x, w, resid, *, scatter, tm=512):
    s, d = x.shape
    ns = len(scatter)
    nsteps = s // tm

    def body(*refs):
        a_ref, b_ref, x_ref, w_ref, res_ref = refs[:5]
        dx_ref, dw_ref = refs[5 + ns:7 + ns]
        step = pl.program_id(0)
        start, finish = _scatter_phases(refs[5:5 + ns], refs[7 + ns:7 + 2 * ns], *refs[7 + 2 * ns:])
        pl.when(step == 0)(start)
        part = jnp.zeros((1, d), F32)
        for r in range(0, tm, UP_ROWS):
            rows = slice(r, r + UP_ROWS)
            dhv = _dot(a_ref[rows, :], b_ref[...], "nt")
            xv = x_ref[rows, :]
            rstd = lax.rsqrt(jnp.mean(xv * xv, axis=-1, keepdims=True) + NORM_EPS)
            xh = xv * rstd
            g = dhv * w_ref[...]
            dx_ref[rows, :] = res_ref[rows, :] + rstd * (g - xh * jnp.mean(g * xh, axis=-1, keepdims=True))
            part = part + jnp.sum(dhv * xh, axis=0, keepdims=True)

        @pl.when(step == 0)
        def _():
            dw_ref[...] = part

        @pl.when(step > 0)
        def _():
            dw_ref[...] += part

        pl.when(step == nsteps - 1)(finish)

    row = pl.BlockSpec((tm, d), lambda i: (i, 0))
    vec = pl.BlockSpec((1, d), lambda i: (0, 0))
    outs = pl.pallas_call(
        body, name="mm_dh_rms_mix_bwd", grid=(nsteps,),
        in_specs=[pl.BlockSpec((tm, PROJ_COLS), lambda i: (i, 0)),
                  pl.BlockSpec((d, PROJ_COLS), lambda i: (0, 0), pipeline_mode=pl.Buffered(1)), row, vec, row] + [ANY] * ns,
        out_specs=[row, vec] + [ANY] * ns,
        out_shape=[jax.ShapeDtypeStruct((s, d), F32), jax.ShapeDtypeStruct((1, d), F32)]
        + [jax.ShapeDtypeStruct(p.shape, p.dtype) for p in scatter],
        scratch_shapes=_scatter_scratch(ns), compiler_params=_params(("arbitrary",)),
    )(dproj, wx, x, w, resid, *scatter)
    return outs[0], outs[1], _keep_own_blocks(outs[2:], scatter)


def out_proj_rms_fwd(y_ssd, y_fox, w_out, x, norm_w, *, tm=512):
    s, d = x.shape

    def body(ys_ref, yf_ref, w_ref, x_ref, nw_ref, x1_ref, h_ref, ht_ref):
        for r in range(0, tm, UP_ROWS):
            rows = slice(r, r + UP_ROWS)
            x1 = x_ref[rows, :] + _dot(ys_ref[rows, :], w_ref[0:d, :]) + _dot(yf_ref[rows, :], w_ref[d:2 * d, :])
            x1_ref[rows, :] = x1
            rstd = lax.rsqrt(jnp.mean(x1 * x1, axis=-1, keepdims=True) + NORM_EPS)
            h = (x1 * rstd) * nw_ref[...]
            h_ref[rows, :] = h.astype(BF16)
            ht_ref[:, rows] = h.T.astype(BF16)

    row = pl.BlockSpec((tm, d), lambda i: (i, 0))
    return pl.pallas_call(
        body, name="out_proj_rms_fwd", grid=(s // tm,),
        in_specs=[row, row, pl.BlockSpec((2 * d, d), lambda i: (0, 0)), row, pl.BlockSpec((1, d), lambda i: (0, 0))],
        out_specs=[row, row, pl.BlockSpec((d, tm), lambda i: (0, i))],
        out_shape=[jax.ShapeDtypeStruct((s, d), F32), jax.ShapeDtypeStruct((s, d), BF16), jax.ShapeDtypeStruct((d, s), BF16)],
        compiler_params=_params(("parallel",)),
    )(y_ssd, y_fox, w_out, x, norm_w)


def rms_bwd_matmul(dhu, a_up, x, w, resid, b, *, name, tm=512):
    s, d = x.shape
    n = b.shape[0]

    def body(dhu_ref, up_ref, x_ref, w_ref, res_ref, b_ref, dx_ref, dw_ref, prod_ref):
        part = jnp.zeros((1, d), F32)
        for r in range(0, tm, UP_ROWS):
            rows = slice(r, r + UP_ROWS)
            xv = x_ref[rows, :]
            dhv = functools.reduce(lambda p, q: p + q, [
                _dot(dhu_ref[k // 2, rows, (k % 2) * UP_SHARD:(k % 2 + 1) * UP_SHARD], up_ref[k], "nt") for k in range(4)])
            rstd = lax.rsqrt(jnp.mean(xv * xv, axis=-1, keepdims=True) + NORM_EPS)
            xh = xv * rstd
            g = dhv * w_ref[...]
            dx = res_ref[rows, :] + rstd * (g - xh * jnp.mean(g * xh, axis=-1, keepdims=True))
            dx_ref[rows, :] = dx
            prod_ref[rows, :] = _dot(dx, b_ref[...], "nt")
            part = part + jnp.sum(dhv * xh, axis=0, keepdims=True)

        @pl.when(pl.program_id(0) == 0)
        def _():
            dw_ref[...] = part

        @pl.when(pl.program_id(0) > 0)
        def _():
            dw_ref[...] += part

    row = pl.BlockSpec((tm, d), lambda i: (i, 0))
    vec = pl.BlockSpec((1, d), lambda i: (0, 0))
    once = pl.Buffered(1)
    return pl.pallas_call(
        body, name=name, grid=(s // tm,),
        in_specs=[pl.BlockSpec((2, tm, D_FF), lambda i: (0, i, 0)),
                  pl.BlockSpec((4, d, UP_SHARD), lambda i: (0, 0, 0), pipeline_mode=once), row, vec, row,
                  pl.BlockSpec((n, d), lambda i: (0, 0), pipeline_mode=once)],
        out_specs=[row, vec, pl.BlockSpec((tm, n), lambda i: (i, 0))],
        out_shape=[jax.ShapeDtypeStruct((s, d), F32), jax.ShapeDtypeStruct((1, d), F32), jax.ShapeDtypeStruct((s, n), F32)],
        compiler_params=_params(("arbitrary",)),
    )(dhu, a_up, x, w, resid, b)


def down_proj_loss(act, w_down, x1, target, *, tm=512):
    s, d = x1.shape

    def body(a_ref, w_ref, x1_ref, t_ref, dy_ref, sq_ref):
        part = jnp.zeros((1, d), F32)
        for r in range(0, tm, UP_ROWS):
            rows = slice(r, r + UP_ROWS)
            e = x1_ref[rows, :] + _dot(a_ref[rows, :], w_ref[...]) - t_ref[rows, :]
            dy_ref[rows, :] = e / float(d)
            part = part + jnp.sum(e * e, axis=0, keepdims=True)

        @pl.when(pl.program_id(0) == 0)
        def _():
            sq_ref[...] = part

        @pl.when(pl.program_id(0) > 0)
        def _():
            sq_ref[...] += part

    row = pl.BlockSpec((tm, d), lambda i: (i, 0))
    vec = pl.BlockSpec((1, d), lambda i: (0, 0))
    return pl.pallas_call(
        body, name="down_proj_loss", grid=(s // tm,),
        in_specs=[pl.BlockSpec((tm, D_FF), lambda i: (i, 0)), pl.BlockSpec((D_FF, d), lambda i: (0, 0)), row, row],
        out_specs=[row, vec], out_shape=[jax.ShapeDtypeStruct((s, d), F32), jax.ShapeDtypeStruct((1, d), F32)],
        compiler_params=_params(("arbitrary",)),
    )(act, w_down, x1, target)


def _row_shifts(ext, k_taps):
    return [_shift_down(ext, j) for j in range(k_taps)]


def _conv_rows(shifts, w):
    k_taps = len(shifts)
    acc = w[k_taps - 1:k_taps, :] * shifts[0]
    for k in range(k_taps - 1):
        acc = acc + w[k:k + 1, :] * shifts[k_taps - 1 - k]
    return acc


def _conv_weight_grad(dcur, shifts, rows, width):
    k_taps = len(shifts)
    out = [jnp.sum(dcur * shifts[k_taps - 1 - k][rows], axis=0, keepdims=True) for k in range(k_taps)]
    out.append(jnp.sum(dcur, axis=0, keepdims=True))
    return _stack_rows(out, width)


def _conv_rows_transposed(dext, w, k_taps):
    acc = w[k_taps - 1:k_taps, :] * dext
    for k in range(k_taps - 1):
        acc = acc + w[k:k + 1, :] * _shift_up(dext, k_taps - 1 - k)
    return acc


def _stack_rows(rows, width):
    ri = _row_iota((8, width))
    out = jnp.zeros((8, width), F32)
    for k, r in enumerate(rows):
        out = out + jnp.where(ri == k, r, 0.0)
    return out


UP_SHARD = 1408
UP_ROWS = 256


def up_ffn_fwd(hf, a_up, conv_w8, conv_b, *, tm=512):
    s = hf.shape[0]

    def body(a_ref, bg_ref, bv_ref, wg_ref, wv_ref, cbg_ref, cbv_ref, hu_ref, act_ref, actt_ref, carry):
        i, j = pl.program_id(0), pl.program_id(1)
        prev_g = jnp.where(i == 0, 0.0, carry[0, j])
        prev_v = jnp.where(i == 0, 0.0, carry[1, j])
        for r in range(0, tm, UP_ROWS):
            rows = slice(r, r + UP_ROWS)
            a = a_ref[rows, :]
            hg, hv = _dot(a, bg_ref[...]), _dot(a, bv_ref[...])
            hu_ref[0, rows, :] = hg
            hu_ref[1, rows, :] = hv
            gc = _conv_rows(_row_shifts(jnp.concatenate([prev_g, hg], axis=0), FFN_CONV), wg_ref[...])[8:] + cbg_ref[...]
            vc = _conv_rows(_row_shifts(jnp.concatenate([prev_v, hv], axis=0), FFN_CONV), wv_ref[...])[8:] + cbv_ref[...]
            act = gc * _sigmoid(gc) * vc
            act_ref[rows, :] = act.astype(BF16)
            actt_ref[:, rows] = act.T.astype(BF16)
            prev_g, prev_v = hg[UP_ROWS - 8:], hv[UP_ROWS - 8:]
        carry[0, j] = prev_g
        carry[1, j] = prev_v

    shard = lambda off: pl.BlockSpec((None, D_MODEL, UP_SHARD), lambda i, j: (j + off, 0, 0))
    taps = lambda off: pl.BlockSpec((8, UP_SHARD), lambda i, j: (0, j + off))
    bias = lambda off: pl.BlockSpec((1, UP_SHARD), lambda i, j: (0, j + off))
    return pl.pallas_call(
        body, name="up_ffn_fwd", grid=(s // tm, 2),
        in_specs=[pl.BlockSpec((tm, D_MODEL), lambda i, j: (i, 0)), shard(0), shard(2), taps(0), taps(2), bias(0), bias(2)],
        out_specs=[pl.BlockSpec((2, tm, UP_SHARD), lambda i, j: (0, i, j)), pl.BlockSpec((tm, UP_SHARD), lambda i, j: (i, j)),
                   pl.BlockSpec((UP_SHARD, tm), lambda i, j: (j, i))],
        out_shape=[jax.ShapeDtypeStruct((2, s, D_FF), F32), jax.ShapeDtypeStruct((s, D_FF), BF16),
                   jax.ShapeDtypeStruct((D_FF, s), BF16)],
        scratch_shapes=[pltpu.VMEM((2, 2, 8, UP_SHARD), F32)], compiler_params=_params(("arbitrary", "arbitrary")),
    )(hf, a_up, a_up, conv_w8, conv_w8, conv_b, conv_b)


def ffn_mid_bwd(hu, dact, conv_w8, conv_b, *, tm=1024, tc=256):
    s = hu.shape[1]
    ncol = D_FF // tc
    nrow = s // tm
    r8 = tm // 8

    def body(g_ref, v_ref, gp_ref, vp_ref, gn_ref, vn_ref, da_ref, dan_ref, wg_ref, wv_ref, bg_ref, bv_ref,
             dhu_ref, wgo_ref, wvo_ref):
        i = pl.program_id(1)
        first = i == 0
        last = i == nrow - 1

        def ext_of(cur_ref, prev_ref, next_ref):
            prev = jnp.where(first, 0.0, prev_ref[...])
            return jnp.concatenate([prev, cur_ref[...], next_ref[...]], axis=0)

        g_sh = _row_shifts(ext_of(g_ref, gp_ref, gn_ref), FFN_CONV)
        v_sh = _row_shifts(ext_of(v_ref, vp_ref, vn_ref), FFN_CONV)
        gc = _conv_rows(g_sh, wg_ref[...]) + bg_ref[...]
        vc = _conv_rows(v_sh, wv_ref[...]) + bv_ref[...]
        da_ext = jnp.concatenate([jnp.zeros((8, tc), F32), da_ref[...], jnp.where(last, 0.0, dan_ref[...])], axis=0)
        silu, dsilu = _silu_and_grad(gc)
        dgc = da_ext * vc * dsilu
        dvc = da_ext * silu
        dhu_ref[0] = _conv_rows_transposed(dgc, wg_ref[...], FFN_CONV)[8:8 + tm].astype(BF16)
        dhu_ref[1] = _conv_rows_transposed(dvc, wv_ref[...], FFN_CONV)[8:8 + tm].astype(BF16)

        cur = slice(8, 8 + tm)
        pg = _conv_weight_grad(dgc[cur], g_sh, cur, tc)
        pv = _conv_weight_grad(dvc[cur], v_sh, cur, tc)

        @pl.when(first)
        def _():
            wgo_ref[...] = pg
            wvo_ref[...] = pv

        @pl.when(i > 0)
        def _():
            wgo_ref[...] += pg
            wvo_ref[...] += pv

    def prev_idx(i):
        return jnp.maximum(i * r8 - 1, 0)

    def next_idx(i):
        return jnp.minimum((i + 1) * r8, s // 8 - 1)

    half = lambda k, rows, row_index: pl.BlockSpec((None, rows, tc), lambda j, i: (k, row_index(i), j))
    in_specs = [
        half(0, tm, lambda i: i), half(1, tm, lambda i: i),
        half(0, 8, prev_idx), half(1, 8, prev_idx),
        half(0, 8, next_idx), half(1, 8, next_idx),
        pl.BlockSpec((tm, tc), lambda j, i: (i, j)),
        pl.BlockSpec((8, tc), lambda j, i: (next_idx(i), j)),
        pl.BlockSpec((8, tc), lambda j, i: (0, j)),
        pl.BlockSpec((8, tc), lambda j, i: (0, j + ncol)),
        pl.BlockSpec((1, tc), lambda j, i: (0, j)),
        pl.BlockSpec((1, tc), lambda j, i: (0, j + ncol)),
    ]
    out_specs = [pl.BlockSpec((2, tm, tc), lambda j, i: (0, i, j)), pl.BlockSpec((8, tc), lambda j, i: (0, j)),
                 pl.BlockSpec((8, tc), lambda j, i: (0, j))]
    out_shape = [jax.ShapeDtypeStruct((2, s, D_FF), BF16),
                 jax.ShapeDtypeStruct((8, D_FF), F32), jax.ShapeDtypeStruct((8, D_FF), F32)]
    return pl.pallas_call(
        body, name="ffn_mid_bwd", grid=(ncol, nrow), in_specs=in_specs, out_specs=out_specs, out_shape=out_shape,
        compiler_params=_params(("parallel", "arbitrary")),
    )(hu, hu, hu, hu, hu, hu, dact, dact, conv_w8, conv_w8, conv_b, conv_b)


def _softplus(x):
    return jnp.maximum(x, 0.0) + jnp.log(1.0 + jnp.exp(-jnp.abs(x)))


def _cumsum_rows(v):
    n = v.shape[0]
    ri = _row_iota(v.shape)
    sh = 1
    while sh < n:
        v = v + jnp.where(ri >= sh, _shift_down(v, sh), 0.0)
        sh *= 2
    return v


def _rev_cumsum_rows(v):
    n = v.shape[0]
    ri = _row_iota(v.shape)
    sh = 1
    while sh < n:
        v = v + jnp.where(ri < n - sh, _shift_up(v, sh), 0.0)
        sh *= 2
    return v


def _total(v):
    return jnp.sum(jnp.sum(v, axis=1, keepdims=True), axis=0, keepdims=True)


def _ssd_in_specs(rev_nc=None):
    def ch(c):
        return c if rev_nc is None else rev_nc - 1 - c

    def prev(c):
        return jnp.maximum(ch(c) * (SSD_CHUNK // 8) - 1, 0)

    L = SSD_CHUNK
    return [
        pl.BlockSpec((L, 1024), lambda c: (ch(c), 0)),
        pl.BlockSpec((L, 1024), lambda c: (ch(c), 1)),
        pl.BlockSpec((L, 256), lambda c: (ch(c), 20)),
        pl.BlockSpec((L, 256), lambda c: (ch(c), 21)),
        pl.BlockSpec((8, 1024), lambda c: (prev(c), 1)),
        pl.BlockSpec((8, 256), lambda c: (prev(c), 20)),
        pl.BlockSpec((8, 256), lambda c: (prev(c), 21)),
        pl.BlockSpec((8, 1024), lambda c: (0, 0)),
        pl.BlockSpec((8, 256), lambda c: (0, 4)),
        pl.BlockSpec((8, 256), lambda c: (0, 5)),
        pl.BlockSpec((1, 1024), lambda c: (0, 0)),
        pl.BlockSpec((1, 256), lambda c: (0, 4)),
        pl.BlockSpec((1, 256), lambda c: (0, 5)),
        pl.BlockSpec((L, SMALL_COLS), lambda c: (ch(c), SMALL_BLOCK)),
        pl.BlockSpec((8, 128), lambda c: (0, 0)),
        pl.BlockSpec((1, 1024), lambda c: (0, 0)),
    ]


def _ssd_conv_pre(cur_ref, prev_ref, w_ref, b_ref, first):
    prev = jnp.where(first, 0.0, prev_ref[...])
    shifts = _row_shifts(jnp.concatenate([prev, cur_ref[...]], axis=0), SSD_CONV)
    return shifts, _conv_rows(shifts, w_ref[...])[8:] + b_ref[...]


def _ssd_time_consts(small_ref, sp_ref):
    dt_pre = small_ref[...] + sp_ref[0:1, :]
    dt = _softplus(dt_pre)
    a = -jnp.exp(sp_ref[1:2, :])
    acs = _cumsum_rows(dt * a)
    return dt_pre, dt, a, acs


def ssd_fwd(proj, conv_w8, conv_b, smallp, norm_w):
    s = proj.shape[0]
    nc = s // SSD_CHUNK
    L = SSD_CHUNK

    def body(z_ref, xs_ref, b_ref, c_ref, xsp_ref, bp_ref, cp_ref, wx_ref, wb_ref, wc_ref, bx_ref, bb_ref, bc_ref,
             small_ref, sp_ref, nw_ref, y_ref, yt_ref, ypre_ref, st_ref, state):
        first = pl.program_id(0) == 0

        @pl.when(first)
        def _():
            state[...] = jnp.zeros_like(state)

        xs = _ssd_conv_pre(xs_ref, xsp_ref, wx_ref, bx_ref, first)[1]
        xs = xs * _sigmoid(xs)
        bm = _ssd_conv_pre(b_ref, bp_ref, wb_ref, bb_ref, first)[1]
        bm = bm * _sigmoid(bm)
        cm = _ssd_conv_pre(c_ref, cp_ref, wc_ref, bc_ref, first)[1]
        cm = cm * _sigmoid(cm)
        _, dt, _, acs = _ssd_time_consts(small_ref, sp_ref)
        acs_t = acs.T
        li = _lane_iota((L, L))
        ri = _row_iota((L, L))
        tri = ri >= li
        lo = li < HEAD_DIM
        st_ref[0] = state[...]
        for g in range(2):
            bg = bm[:, 128 * g:128 * g + 128]
            cg = cm[:, 128 * g:128 * g + 128]
            gmat = _dot(cg, bg, "nt")
            for pp in range(4):
                p = 4 * g + pp
                h0, h1 = 2 * p, 2 * p + 1
                x = xs[:, 128 * p:128 * p + 128]
                a0, a1 = acs[:, h0:h0 + 1], acs[:, h1:h1 + 1]
                xdt = x * jnp.where(lo, dt[:, h0:h0 + 1], dt[:, h1:h1 + 1])
                m0 = gmat * jnp.exp(jnp.where(tri, a0 - acs_t[h0:h0 + 1, :], NEG_BIG))
                m1 = gmat * jnp.exp(jnp.where(tri, a1 - acs_t[h1:h1 + 1, :], NEG_BIG))
                yd = _dot(m0, jnp.where(lo, xdt, 0.0)) + _dot(m1, jnp.where(lo, 0.0, xdt))
                hin = state[p]
                yo = _dot(cg, hin, "nt") * jnp.exp(jnp.where(lo, a0, a1))
                dskip = jnp.where(lo[0:1], sp_ref[2:3, h0:h0 + 1], sp_ref[2:3, h1:h1 + 1])
                ypre_ref[:, 128 * p:128 * p + 128] = yd + yo + dskip * x
                al0, al1 = acs[L - 1:L, h0:h0 + 1], acs[L - 1:L, h1:h1 + 1]
                w = jnp.exp(jnp.where(lo, al0 - a0, al1 - a1))
                dec = jnp.exp(jnp.where(ri < HEAD_DIM, al0, al1))
                state[p] = dec * hin + _dot(xdt * w, bg, "tn")
        z = z_ref[...]
        yg = ypre_ref[...] * (z * _sigmoid(z))
        for g in range(2):
            seg = yg[:, 512 * g:512 * g + 512]
            r = lax.rsqrt(jnp.mean(seg * seg, axis=-1, keepdims=True) + NORM_EPS)
            out = (seg * r) * nw_ref[:, 512 * g:512 * g + 512]
            y_ref[:, 512 * g:512 * g + 512] = out.astype(BF16)
            yt_ref[512 * g:512 * g + 512, :] = out.T.astype(BF16)

    row = pl.BlockSpec((L, 1024), lambda c: (c, 0))
    return pl.pallas_call(
        body, name="ssd_fwd", grid=(nc,), in_specs=_ssd_in_specs(),
        out_specs=[row, pl.BlockSpec((1024, L), lambda c: (0, c)), row,
                   pl.BlockSpec((1, N_PAIRS, 128, 128), lambda c: (c, 0, 0, 0))],
        out_shape=[jax.ShapeDtypeStruct((s, 1024), BF16), jax.ShapeDtypeStruct((1024, s), BF16),
                   jax.ShapeDtypeStruct((s, 1024), F32), jax.ShapeDtypeStruct((nc, N_PAIRS, 128, 128), F32)],
        scratch_shapes=[pltpu.VMEM((N_PAIRS, 128, 128), F32)],
        compiler_params=_params(("arbitrary",)),
    )(proj, proj, proj, proj, proj, proj, proj, conv_w8, conv_w8, conv_w8, conv_b, conv_b, conv_b, proj, smallp, norm_w)


def ssd_bwd(proj, conv_w8, conv_b, smallp, norm_w, ypre, states, dy, sel, swap=()):
    s = proj.shape[0]
    nc = s // SSD_CHUNK
    L = SSD_CHUNK

    ns = len(swap)
    n_in, n_out, n_scratch = 20, 10, 11

    def body(*refs):
        own = refs[:n_in] + refs[n_in + ns:n_in + ns + n_out] + refs[n_in + 2 * ns + n_out:n_in + 2 * ns + n_out + n_scratch]
        if ns:
            start, finish = _pair_swap_phases(refs[n_in:n_in + ns], refs[n_in + ns + n_out:n_in + 2 * ns + n_out],
                                              *refs[n_in + 2 * ns + n_out + n_scratch:])
            pl.when(pl.program_id(0) == 0)(start)
        compute(*own)
        if ns:
            pl.when(pl.program_id(0) == nc - 1)(finish)

    def compute(z_ref, xs_ref, b_ref, c_ref, xsp_ref, bp_ref, cp_ref, wx_ref, wb_ref, wc_ref, bx_ref, bb_ref, bc_ref,
                small_ref, sp_ref, nw_ref, ypre_ref, st_ref, dy_ref, sel_ref,
                dz_ref, dxs_ref, db_ref, dc_ref, dsmall_ref, gwx_ref, gwb_ref, gwc_ref, gsp_ref, gnw_ref,
                dstate, carry_x, carry_b, carry_c, dxs_buf, dbm_buf, dcm_buf, qcs, col_sums, acs_terms, dt_terms):
        step = pl.program_id(0)
        col_sums[...] = jnp.zeros_like(col_sums)
        first_chunk = step == nc - 1
        start = step == 0

        @pl.when(start)
        def _():
            dstate[...] = jnp.zeros_like(dstate)
            carry_x[...] = jnp.zeros_like(carry_x)
            carry_b[...] = jnp.zeros_like(carry_b)
            carry_c[...] = jnp.zeros_like(carry_c)

        xs_sh, xs_pre = _ssd_conv_pre(xs_ref, xsp_ref, wx_ref, bx_ref, first_chunk)
        b_sh, b_pre = _ssd_conv_pre(b_ref, bp_ref, wb_ref, bb_ref, first_chunk)
        c_sh, c_pre = _ssd_conv_pre(c_ref, cp_ref, wc_ref, bc_ref, first_chunk)
        xs, xs_ds = _silu_and_grad(xs_pre)
        bm, b_ds = _silu_and_grad(b_pre)
        cm, c_ds = _silu_and_grad(c_pre)
        dt_pre, dt, a, acs = _ssd_time_consts(small_ref, sp_ref)
        acs_t = acs.T
        li = _lane_iota((L, L))
        ri = _row_iota((L, L))
        tri = ri >= li
        lo = li < HEAD_DIM
        lo_rows = ri < HEAD_DIM
        li1 = _lane_iota((1, L))

        z = z_ref[...]
        sz, dsz = _silu_and_grad(z)
        y = ypre_ref[...]
        yg = y * sz
        dout = dy_ref[...]
        dyg_parts = []
        gnw_parts = []
        for g in range(2):
            sl = slice(512 * g, 512 * g + 512)
            seg = yg[:, sl]
            r = lax.rsqrt(jnp.mean(seg * seg, axis=-1, keepdims=True) + NORM_EPS)
            n = seg * r
            gnw_parts.append(jnp.sum(dout[:, sl] * n, axis=0, keepdims=True))
            gg = dout[:, sl] * nw_ref[:, sl]
            dyg_parts.append(r * (gg - n * jnp.mean(gg * n, axis=-1, keepdims=True)))
        dyg = jnp.concatenate(dyg_parts, axis=1)
        gnw = jnp.concatenate(gnw_parts, axis=1)
        dz_ref[...] = (dyg * y * dsz).astype(BF16)
        dypre = dyg * sz

        qcs[...] = jnp.zeros_like(qcs)
        dalast = jnp.zeros((1, L), F32)
        for g in range(2):
            bg = bm[:, 128 * g:128 * g + 128]
            cg = cm[:, 128 * g:128 * g + 128]
            gmat = _dot(cg, bg, "nt")
            dgmat = jnp.zeros((L, L), F32)
            dbg = jnp.zeros((L, L), F32)
            dcg = jnp.zeros((L, L), F32)
            for pp in range(4):
                p = 4 * g + pp
                h0, h1 = 2 * p, 2 * p + 1
                lanes = slice(128 * p, 128 * p + 128)
                x = xs[:, lanes]
                dyp = dypre[:, lanes]
                a0, a1 = acs[:, h0:h0 + 1], acs[:, h1:h1 + 1]
                dtl = jnp.where(lo, dt[:, h0:h0 + 1], dt[:, h1:h1 + 1])
                xdt = x * dtl
                l0 = jnp.exp(jnp.where(tri, a0 - acs_t[h0:h0 + 1, :], NEG_BIG))
                l1 = jnp.exp(jnp.where(tri, a1 - acs_t[h1:h1 + 1, :], NEG_BIG))
                m0, m1 = gmat * l0, gmat * l1
                dskip = jnp.where(lo[0:1], sp_ref[2:3, h0:h0 + 1], sp_ref[2:3, h1:h1 + 1])
                col_sums[0:1, lanes] = jnp.sum(dyp * x, axis=0, keepdims=True)
                dx = dyp * dskip
                dy0, dy1 = jnp.where(lo, dyp, 0.0), jnp.where(lo, 0.0, dyp)
                x0, x1 = jnp.where(lo, xdt, 0.0), jnp.where(lo, 0.0, xdt)
                dm0, dm1 = _dot(dy0, x0, "nt"), _dot(dy1, x1, "nt")
                dxdt = _dot(m0, dy0, "tn") + _dot(m1, dy1, "tn")
                q0, q1 = dm0 * m0, dm1 * m1
                qcs[h0:h0 + 1, :] = jnp.sum(q0, axis=0, keepdims=True)
                qcs[h1:h1 + 1, :] = jnp.sum(q1, axis=0, keepdims=True)
                row_terms = jnp.where(lo, q0 + pltpu.roll(q0, HEAD_DIM, 1), q1 + pltpu.roll(q1, HEAD_DIM, 1))
                dgmat = dgmat + dm0 * l0 + dm1 * l1
                hin = st_ref[0, p]
                e = jnp.exp(jnp.where(lo, a0, a1))
                ch = _dot(cg, hin, "nt")
                dch = dyp * e
                dcg = dcg + _dot(dch, hin)
                dhin = _dot(dch, cg, "tn")
                dhout = dstate[p]
                al0, al1 = acs[L - 1:L, h0:h0 + 1], acs[L - 1:L, h1:h1 + 1]
                dec = jnp.exp(jnp.where(lo_rows, al0, al1))
                dhin = dhin + dec * dhout
                dal = dhout * hin * dec
                dal0 = _total(jnp.where(lo_rows, dal, 0.0))
                dal1 = _total(dal) - dal0
                dalast = dalast + jnp.where(li1 == h0, dal0, 0.0) + jnp.where(li1 == h1, dal1, 0.0)
                w = jnp.exp(jnp.where(lo, al0 - a0, al1 - a1))
                xw = xdt * w
                dxw = _dot(bg, dhout, "nt")
                dbg = dbg + _dot(xw, dhout)
                dxdt = dxdt + dxw * w
                dww = dxw * xw
                col_sums[1:2, lanes] = jnp.sum(dww, axis=0, keepdims=True)
                acs_terms[:, lanes] = row_terms + dch * ch - dww
                dx = dx + dxdt * dtl
                dt_terms[:, lanes] = dxdt * x
                dxs_buf[:, lanes] = dx
                dstate[p] = dhin
            dcg = dcg + _dot(dgmat, bg)
            dbg = dbg + _dot(dgmat, cg, "tn")
            dbm_buf[:, 128 * g:128 * g + 128] = dbg
            dcm_buf[:, 128 * g:128 * g + 128] = dcg

        head_sums = _split3_dot(col_sums[...], sel_ref[...])
        dskip_g = head_sums[0:1, :]
        dalast = dalast + head_sums[1:2, :]
        ddt = _split3_dot(dt_terms[...], sel_ref[...])
        dacs_tot = _split3_dot(acs_terms[...], sel_ref[...]) - qcs[...].T + jnp.where(ri == L - 1, dalast, 0.0)
        dstep = _rev_cumsum_rows(dacs_tot)
        ddt = ddt + dstep * a
        head_lane = li < N_HEADS
        ddt_pre = jnp.where(head_lane, ddt * _sigmoid(dt_pre), 0.0)
        dsmall_ref[...] = ddt_pre
        da = jnp.sum(jnp.where(head_lane, dstep * dt, 0.0), axis=0, keepdims=True)
        gsp = _stack_rows([jnp.sum(ddt_pre, axis=0, keepdims=True), da * a, dskip_g], L)

        def conv_back(dpost, ds, shifts, w_ref, carry, out_ref, width):
            dpre = dpost * ds
            dext = jnp.concatenate([dpre, carry[...]], axis=0)
            out_ref[...] = _conv_rows_transposed(dext, w_ref[...], SSD_CONV)[:L].astype(BF16)
            carry[...] = dpre[0:8]
            return _conv_weight_grad(dpre, shifts, slice(8, 8 + L), width)

        gwx = conv_back(dxs_buf[...], xs_ds, xs_sh, wx_ref, carry_x, dxs_ref, 1024)
        gwb = conv_back(dbm_buf[...], b_ds, b_sh, wb_ref, carry_b, db_ref, 256)
        gwc = conv_back(dcm_buf[...], c_ds, c_sh, wc_ref, carry_c, dc_ref, 256)

        @pl.when(start)
        def _():
            gwx_ref[...] = gwx
            gwb_ref[...] = gwb
            gwc_ref[...] = gwc
            gsp_ref[...] = gsp
            gnw_ref[...] = gnw

        @pl.when(step > 0)
        def _():
            gwx_ref[...] += gwx
            gwb_ref[...] += gwb
            gwc_ref[...] += gwc
            gsp_ref[...] += gsp
            gnw_ref[...] += gnw

    def ch(c):
        return nc - 1 - c

    row = pl.BlockSpec((L, 1024), lambda c: (ch(c), 0))
    row256 = pl.BlockSpec((L, 256), lambda c: (ch(c), 0))
    in_specs = _ssd_in_specs(rev_nc=nc) + [row, pl.BlockSpec((1, N_PAIRS, 128, 128), lambda c: (ch(c), 0, 0, 0)), row,
                                           pl.BlockSpec((1024, 128), lambda c: (0, 0))]
    out_specs = [row, row, row256, row256, pl.BlockSpec((L, 128), lambda c: (ch(c), 0)),
                 pl.BlockSpec((8, 1024), lambda c: (0, 0)), pl.BlockSpec((8, 256), lambda c: (0, 0)),
                 pl.BlockSpec((8, 256), lambda c: (0, 0)), pl.BlockSpec((8, 128), lambda c: (0, 0)),
                 pl.BlockSpec((1, 1024), lambda c: (0, 0))]
    out_shape = [jax.ShapeDtypeStruct((s, 1024), BF16), jax.ShapeDtypeStruct((s, 1024), BF16),
                 jax.ShapeDtypeStruct((s, 256), BF16), jax.ShapeDtypeStruct((s, 256), BF16),
                 jax.ShapeDtypeStruct((s, 128), F32),
                 jax.ShapeDtypeStruct((8, 1024), F32), jax.ShapeDtypeStruct((8, 256), F32),
                 jax.ShapeDtypeStruct((8, 256), F32), jax.ShapeDtypeStruct((8, 128), F32),
                 jax.ShapeDtypeStruct((1, 1024), F32)]
    scratch = [pltpu.VMEM((N_PAIRS, 128, 128), F32), pltpu.VMEM((8, 1024), F32), pltpu.VMEM((8, 256), F32),
               pltpu.VMEM((8, 256), F32), pltpu.VMEM((L, 1024), F32), pltpu.VMEM((L, 256), F32), pltpu.VMEM((L, 256), F32),
               pltpu.VMEM((L, L), F32), pltpu.VMEM((8, 1024), F32), pltpu.VMEM((L, 1024), F32), pltpu.VMEM((L, 1024), F32)]
    assert (len(in_specs), len(out_specs), len(scratch)) == (n_in, n_out, n_scratch)
    outs = pl.pallas_call(
        body, name="ssd_bwd", grid=(nc,), in_specs=in_specs + [ANY] * ns, out_specs=out_specs + [ANY] * ns,
        out_shape=out_shape + _pair_swap_out_shapes(swap), scratch_shapes=scratch + (_pair_swap_scratch(ns) if ns else []),
        compiler_params=_params(("arbitrary",)),
    )(proj, proj, proj, proj, proj, proj, proj, conv_w8, conv_w8, conv_w8, conv_b, conv_b, conv_b, proj, smallp, norm_w,
      ypre, states, dy, sel, *swap)
    return (*outs[:n_out], list(outs[n_out:]))


FOX_SCALE = HEAD_DIM ** -0.5
FOX_T = 256
Q_COL, K_COL, V_COL = 2, 3, 4


def _split_dot(v, m, terms):
    out, rest = None, v
    for i in range(terms):
        piece = rest.astype(BF16)
        out = _dot(piece, m) if out is None else out + _dot(piece, m)
        if i + 1 < terms:
            rest = rest - piece.astype(F32)
    return out


def _split3_dot(v, m):
    return _split_dot(v, m, 3)


def _head_mean(x, sel_ref, selt_ref):
    return _dot(x, sel_ref[...]) * (1.0 / HEAD_DIM)


def _head_spread(v, selt_ref):
    return _split_dot(v, selt_ref[...], 2)


def _head_rstd(x, sel_ref, selt_ref):
    return _head_spread(lax.rsqrt(_head_mean(x * x, sel_ref, selt_ref) + NORM_EPS), selt_ref)


def fox_tables():
    r = np.arange(3 * 128)
    piece, lane = r // 128, r % 128
    head = lane - F_LANE
    is_head = np.logical_and(head >= 0, head < N_HEADS)
    col = 128 * (head // 2) + HEAD_DIM * (1 - head % 2) + piece
    cols = np.arange(1024)
    place_q = np.logical_and(is_head[:, None], cols[None, :] == col[:, None])
    place_k = np.logical_and(is_head[:, None], cols[None, :] == (col + 3)[:, None])
    ones_q = np.logical_and(cols % HEAD_DIM >= 3, cols % HEAD_DIM < 6)[None]
    ones_k = (cols % HEAD_DIM < 3)[None]
    h = np.arange(128) - F_LANE
    ok = np.logical_and(h >= 0, h < N_HEADS)
    same_pair = cols[:, None] // 128 == (h // 2)[None, :]
    fold_even = np.logical_and(np.logical_and(ok, h % 2 == 0)[None, :], same_pair)
    fold_odd = np.logical_and(np.logical_and(ok, h % 2 == 1)[None, :], same_pair)
    as_bf16 = lambda t: jnp.asarray(t.astype(np.float32), BF16)
    return (as_bf16(place_q), as_bf16(place_k), jnp.asarray(ones_q, F32), jnp.asarray(ones_k, F32),
            as_bf16(fold_even), as_bf16(fold_odd))


def fox_prep(proj, smallp, qw, kw, sel, selt, place_q, place_k, ones_q, ones_k, *, tm=256):
    s = proj.shape[0]

    def body(q_ref, k_ref, v_ref, small_ref, sp_ref, qw_ref, kw_ref, sel_ref, selt_ref, pq_ref, pk_ref, oq_ref, ok_ref,
             qn_ref, kn_ref, aq_ref, ak_ref, vb_ref, knt_ref, akt_ref, vt_ref, carry):
        @pl.when(pl.program_id(0) == 0)
        def _():
            carry[...] = jnp.zeros_like(carry)

        q = q_ref[...]
        qn_ref[...] = (((q * _head_rstd(q, sel_ref, selt_ref)) * qw_ref[...]) * FOX_SCALE).astype(BF16)
        k = k_ref[...]
        kn = ((k * _head_rstd(k, sel_ref, selt_ref)) * kw_ref[...]).astype(BF16)
        kn_ref[...] = kn
        knt_ref[...] = kn.astype(F32).T.astype(BF16)
        vb_ref[...] = v_ref[...].astype(BF16)
        vt_ref[...] = v_ref[...].T.astype(BF16)
        li = _lane_iota((tm, 128))
        f_lane = jnp.logical_and(li >= F_LANE, li < F_LANE + N_HEADS)
        logf = jnp.where(f_lane, -_softplus(-(small_ref[...] + sp_ref[3:4, :])), 0.0)
        cum = _cumsum_rows(logf) + carry[...]
        carry[...] = cum[tm - 1:tm, :]
        hi = cum.astype(BF16)
        r1 = cum - hi.astype(F32)
        mid = r1.astype(BF16)
        lo = (r1 - mid.astype(F32)).astype(BF16)
        pieces = jnp.concatenate([hi, mid, lo], axis=1)
        aq_ref[...] = (_dot(pieces, pq_ref[...]) + oq_ref[...]).astype(BF16)
        ak = ok_ref[...] - _dot(pieces, pk_ref[...])
        ak_ref[...] = ak.astype(BF16)
        akt_ref[...] = ak.T.astype(BF16)

    row = pl.BlockSpec((tm, 1024), lambda i: (i, 0))
    col = pl.BlockSpec((1024, tm), lambda i: (0, i))
    vec = pl.BlockSpec((1, 1024), lambda i: (0, 0))
    table = pl.BlockSpec((384, 1024), lambda i: (0, 0))
    wide = jax.ShapeDtypeStruct((s, 1024), BF16)
    tall = jax.ShapeDtypeStruct((1024, s), BF16)
    return pl.pallas_call(
        body, name="fox_prep", grid=(s // tm,),
        in_specs=[pl.BlockSpec((tm, 1024), lambda i: (i, Q_COL)), pl.BlockSpec((tm, 1024), lambda i: (i, K_COL)),
                  pl.BlockSpec((tm, 1024), lambda i: (i, V_COL)),
                  pl.BlockSpec((tm, 128), lambda i: (i, SMALL_BLOCK)), pl.BlockSpec((8, 128), lambda i: (0, 0)), vec, vec,
                  pl.BlockSpec((1024, 128), lambda i: (0, 0)), pl.BlockSpec((128, 1024), lambda i: (0, 0)),
                  table, table, vec, vec],
        out_specs=[row, row, row, row, row, col, col, col],
        out_shape=[wide, wide, wide, wide, wide, tall, tall, tall],
        scratch_shapes=[pltpu.VMEM((1, 128), F32)], compiler_params=_params(("arbitrary",)),
    )(proj, proj, proj, proj, smallp, qw, kw, sel, selt, place_q, place_k, ones_q, ones_k)


def fox_fwd(qn, kn, aq, ak, vt, shards=()):
    s = qn.shape[0]
    t = FOX_T
    nq = s // t
    ng = len(shards)

    def body(*refs):
        q_ref, k_ref, aq_ref, ak_ref, vt_ref = refs[:5]
        o_ref, ot_ref, lse_ref = refs[5 + ng:8 + ng]
        p = pl.program_id(0)
        if ng:
            start, forward, finish = _gather_phases(refs[5:5 + ng], refs[8 + ng:8 + 2 * ng], *refs[8 + 2 * ng:])
            pl.when(p == 0)(start)
            pl.when(p == N_PAIRS // 2)(forward)

        @pl.when(p == 0)
        def _():
            lse_ref[...] = jnp.zeros_like(lse_ref)

        lo = _lane_iota((t, 128)) < HEAD_DIM
        lo_rows = _row_iota((128, t)) < HEAD_DIM
        causal_t = _lane_iota((t, t)) >= _row_iota((t, t))

        def q_loop(qi, _):
            q0 = pl.multiple_of(qi * t, t)
            qv, aqv = q_ref[pl.ds(q0, t), :], aq_ref[pl.ds(q0, t), :]
            qa, qb = jnp.where(lo, qv, aqv), jnp.where(lo, aqv, qv)

            def scores(kj):
                k0 = pl.multiple_of(kj * t, t)
                kv, akv = k_ref[pl.ds(k0, t), :], ak_ref[pl.ds(k0, t), :]
                return _dot(jnp.where(lo, kv, akv), qa, "nt"), _dot(jnp.where(lo, akv, kv), qb, "nt")

            def update(kj, stats, s0, s1):
                m0, l0, m1, l1, acc = stats
                vtv = vt_ref[:, pl.ds(pl.multiple_of(kj * t, t), t)]
                n0 = jnp.maximum(m0, jnp.max(s0, axis=0, keepdims=True))
                n1 = jnp.maximum(m1, jnp.max(s1, axis=0, keepdims=True))
                a0, a1 = jnp.exp(m0 - n0), jnp.exp(m1 - n1)
                p0, p1 = jnp.exp(s0 - n0), jnp.exp(s1 - n1)
                l0 = a0 * l0 + jnp.sum(p0, axis=0, keepdims=True)
                l1 = a1 * l1 + jnp.sum(p1, axis=0, keepdims=True)
                acc = (jnp.where(lo_rows, a0, a1) * acc + _dot(jnp.where(lo_rows, vtv, 0.0), p0)
                       + _dot(jnp.where(lo_rows, 0.0, vtv), p1))
                return n0, l0, n1, l1, acc

            def step(kj, carry):
                stats, (s0, s1) = carry[:5], carry[5:]
                nxt = scores(kj + 1)
                return (*update(kj, stats, s0, s1), *nxt)

            def row(val):
                return jnp.full((1, t), val, F32)

            init = (row(NEG_BIG), row(0.0), row(NEG_BIG), row(0.0), jnp.zeros((128, t), F32), *scores(0))
            carry = lax.fori_loop(0, qi, step, init)
            s0, s1 = jnp.where(causal_t, carry[5], NEG_BIG), jnp.where(causal_t, carry[6], NEG_BIG)
            m0, l0, m1, l1, acc = update(qi, carry[:5], s0, s1)
            out_t = acc / jnp.where(lo_rows, l0, l1)
            ot_ref[:, pl.ds(q0, t)] = out_t.astype(BF16)
            o_ref[pl.ds(q0, t), :] = out_t.T.astype(BF16)
            ri = _row_iota((N_HEADS, t))
            old = lse_ref[:, pl.ds(q0, t)]
            lse_ref[:, pl.ds(q0, t)] = jnp.where(
                ri == 2 * p, m0 + jnp.log(l0), jnp.where(ri == 2 * p + 1, m1 + jnp.log(l1), old))
            return 0

        lax.fori_loop(0, nq, q_loop, 0)
        if ng:
            pl.when(p == N_PAIRS - 1)(finish)

    pair = pl.BlockSpec((s, 128), lambda p: (0, p))
    outs = pl.pallas_call(
        body, name="fox_fwd", grid=(N_PAIRS,),
        in_specs=[pair] * 4 + [pl.BlockSpec((128, s), lambda p: (p, 0))] + [ANY] * ng,
        out_specs=[pair, pl.BlockSpec((128, s), lambda p: (p, 0)), pl.BlockSpec((N_HEADS, s), lambda p: (0, 0))] + [ANY] * ng,
        out_shape=[jax.ShapeDtypeStruct((s, 1024), BF16), jax.ShapeDtypeStruct((1024, s), BF16),
                   jax.ShapeDtypeStruct((N_HEADS, s), F32)] + _gather_out_shapes(shards),
        scratch_shapes=_gather_scratch(ng) if ng else [],
        compiler_params=_params(("arbitrary",)),
    )(qn, kn, aq, ak, vt, *shards)
    return outs[0], outs[1], outs[2], list(outs[3:])


def fox_bwd(qn, kn, aq, ak, knt, akt, vb, lse, dmixed, parts=()):
    s = qn.shape[0]
    t = FOX_T
    nq = s // t
    once = pl.Buffered(1)
    ns = len(parts)

    def body(*refs):
        q_ref, k_ref, aq_ref, ak_ref, kt_ref, akt_ref, v_ref, lse_ref, do_ref = refs[:9]
        dq_ref, dk_ref, dv_ref, dc0_ref, dc1_ref = refs[9 + ns:14 + ns]
        p_scr, dp_scr = refs[14 + 2 * ns:16 + 2 * ns]
        p = pl.program_id(0)
        if ns:
            start, finish = _scatter_phases(refs[9:9 + ns], refs[14 + ns:14 + 2 * ns], *refs[16 + 2 * ns:])
            pl.when(p == 0)(start)
        dk_ref[...] = jnp.zeros_like(dk_ref)
        dv_ref[...] = jnp.zeros_like(dv_ref)
        dc0_ref[...] = jnp.zeros_like(dc0_ref)
        dc1_ref[...] = jnp.zeros_like(dc1_ref)
        lo = _lane_iota((t, 128)) < HEAD_DIM
        lo_rows = _row_iota((128, t)) < HEAD_DIM
        causal_t = _lane_iota((t, t)) >= _row_iota((t, t))

        def q_loop(qi, _):
            q0 = pl.multiple_of(qi * t, t)
            qv, aqv = q_ref[pl.ds(q0, t), :], aq_ref[pl.ds(q0, t), :]
            qa, qb = jnp.where(lo, qv, aqv), jnp.where(lo, aqv, qv)
            do = do_ref[pl.ds(q0, t), :]
            doa, dob = jnp.where(lo, do, 0.0).astype(BF16), jnp.where(lo, 0.0, do).astype(BF16)
            lse_blk = lse_ref[:, pl.ds(q0, t)]
            ri = _row_iota((N_HEADS, t))
            lse0 = jnp.sum(jnp.where(ri == 2 * p, lse_blk, 0.0), axis=0, keepdims=True)
            lse1 = jnp.sum(jnp.where(ri == 2 * p + 1, lse_blk, 0.0), axis=0, keepdims=True)

            def scores(kj):
                k0 = pl.multiple_of(kj * t, t)
                kv, akv = k_ref[pl.ds(k0, t), :], ak_ref[pl.ds(k0, t), :]
                return _dot(jnp.where(lo, kv, akv), qa, "nt"), _dot(jnp.where(lo, akv, kv), qb, "nt")

            def pass1(kj, d0, d1, diagonal):
                k0 = pl.multiple_of(kj * t, t)
                vv = v_ref[pl.ds(k0, t), :]
                s0, s1 = scores(kj)
                if diagonal:
                    s0, s1 = jnp.where(causal_t, s0, NEG_BIG), jnp.where(causal_t, s1, NEG_BIG)
                p0, p1 = jnp.exp(s0 - lse0), jnp.exp(s1 - lse1)
                dp0, dp1 = _dot(vv, doa, "nt"), _dot(vv, dob, "nt")
                p_scr[0, kj], p_scr[1, kj] = p0, p1
                dp_scr[0, kj], dp_scr[1, kj] = dp0, dp1
                dv_ref[pl.ds(k0, t), :] += _dot(p0, doa) + _dot(p1, dob)
                return d0 + jnp.sum(p0 * dp0, axis=0, keepdims=True), d1 + jnp.sum(p1 * dp1, axis=0, keepdims=True)

            zero = jnp.zeros((1, t), F32)
            d0, d1 = lax.fori_loop(0, qi, lambda kj, c: pass1(kj, *c, False), (zero, zero))
            d0, d1 = pass1(qi, d0, d1, True)

            def fold_lanes(v):
                return functools.reduce(lambda a, b: a + b, [v[:, 128 * i:128 * (i + 1)] for i in range(t // 128)])

            def pass2(kj, carry):
                dq0, dq1 = carry
                k0 = pl.multiple_of(kj * t, t)
                p0, p1 = p_scr[0, kj], p_scr[1, kj]
                ds0, ds1 = p0 * (dp_scr[0, kj] - d0), p1 * (dp_scr[1, kj] - d1)
                dk_ref[pl.ds(k0, t), :] += jnp.where(lo, _dot(ds0, qa), _dot(ds1, qb))
                dc0_ref[pl.ds(k0, t), :] += fold_lanes(ds0)
                dc1_ref[pl.ds(k0, t), :] += fold_lanes(ds1)
                ktv, aktv = kt_ref[:, pl.ds(k0, t)], akt_ref[:, pl.ds(k0, t)]
                return dq0 + _dot(jnp.where(lo_rows, ktv, aktv), ds0), dq1 + _dot(jnp.where(lo_rows, aktv, ktv), ds1)

            zq = jnp.zeros((128, t), F32)
            dq0, dq1 = lax.fori_loop(0, qi + 1, pass2, (zq, zq))
            dq_ref[pl.ds(q0, t), :] = jnp.where(lo_rows, dq0, dq1).T
            return 0

        lax.fori_loop(0, nq, q_loop, 0)
        if ns:
            pl.when(p == N_PAIRS - 1)(finish)

    pair = pl.BlockSpec((s, 128), lambda p: (0, p))
    pair_t = pl.BlockSpec((128, s), lambda p: (p, 0))
    out = jax.ShapeDtypeStruct((s, 1024), F32)
    outs = pl.pallas_call(
        body, name="fox_bwd", grid=(N_PAIRS,),
        in_specs=[pair, pair, pair, pair, pair_t, pair_t, pair, pl.BlockSpec((N_HEADS, s), lambda p: (0, 0)),
                  pl.BlockSpec((s, 128), lambda p: (0, 8 + p))] + [ANY] * ns,
        out_specs=[pl.BlockSpec((s, 128), lambda p: (0, p), pipeline_mode=once)] * 5 + [ANY] * ns,
        out_shape=[out] * 5 + [jax.ShapeDtypeStruct(p.shape, p.dtype) for p in parts],
        scratch_shapes=[pltpu.VMEM((2, nq, t, t), F32), pltpu.VMEM((2, nq, t, t), F32)] + (_scatter_scratch(ns) if ns else []),
        compiler_params=_params(("arbitrary",)),
    )(qn, kn, aq, ak, knt, akt, vb, lse, dmixed, *parts)
    return (*outs[:5], _keep_own_blocks(outs[5:], parts))


def fox_post(dqn, dkn, dc0, dc1, proj, smallp, qw, kw, sel, selt, fold_even, fold_odd, *, tm=256):
    s = proj.shape[0]
    nrow = s // tm

    def body(dqn_ref, dkn_ref, dc0_ref, dc1_ref, q_ref, k_ref, small_ref, sp_ref, qw_ref, kw_ref, sel_ref, selt_ref,
             fe_ref, fo_ref, dq_ref, dk_ref, dsmall_ref, gqw_ref, gkw_ref, gfb_ref, carry):
        step = pl.program_id(0)

        @pl.when(step == 0)
        def _():
            carry[...] = jnp.zeros_like(carry)

        def norm_bwd(x_ref, w_ref, dn, out_ref):
            x = x_ref[...]
            rf = _head_rstd(x, sel_ref, selt_ref)
            xh = x * rf
            g = dn * w_ref[...]
            mean_gx = _head_spread(_head_mean(g * xh, sel_ref, selt_ref), selt_ref)
            out_ref[...] = (rf * (g - xh * mean_gx)).astype(BF16)
            return jnp.sum(dn * xh, axis=0, keepdims=True)

        gqw = norm_bwd(q_ref, qw_ref, dqn_ref[...] * FOX_SCALE, dq_ref)
        gkw = norm_bwd(k_ref, kw_ref, dkn_ref[...], dk_ref)
        li = _lane_iota((tm, 128))
        f_lane = jnp.logical_and(li >= F_LANE, li < F_LANE + N_HEADS)
        dcum = -(_split3_dot(dc0_ref[...], fe_ref[...]) + _split3_dot(dc1_ref[...], fo_ref[...]))
        dlogf = _rev_cumsum_rows(dcum) + carry[...]
        carry[...] = dlogf[0:1, :]
        dfr = jnp.where(f_lane, dlogf * _sigmoid(-(small_ref[...] + sp_ref[3:4, :])), 0.0)
        dsmall_ref[...] = dfr
        gfb = jnp.sum(dfr, axis=0, keepdims=True)

        @pl.when(step == 0)
        def _():
            gqw_ref[...] = gqw
            gkw_ref[...] = gkw
            gfb_ref[...] = gfb

        @pl.when(step > 0)
        def _():
            gqw_ref[...] += gqw
            gkw_ref[...] += gkw
            gfb_ref[...] += gfb

    def rb(i):
        return nrow - 1 - i

    row = pl.BlockSpec((tm, 1024), lambda i: (rb(i), 0))
    vec = pl.BlockSpec((1, 1024), lambda i: (0, 0))
    fold = pl.BlockSpec((1024, 128), lambda i: (0, 0))
    return pl.pallas_call(
        body, name="fox_post", grid=(nrow,),
        in_specs=[row, row, row, row, pl.BlockSpec((tm, 1024), lambda i: (rb(i), Q_COL)),
                  pl.BlockSpec((tm, 1024), lambda i: (rb(i), K_COL)),
                  pl.BlockSpec((tm, 128), lambda i: (rb(i), SMALL_BLOCK)), pl.BlockSpec((8, 128), lambda i: (0, 0)), vec, vec,
                  fold, pl.BlockSpec((128, 1024), lambda i: (0, 0)), fold, fold],
        out_specs=[row, row, pl.BlockSpec((tm, 128), lambda i: (rb(i), 0)), vec, vec, pl.BlockSpec((1, 128), lambda i: (0, 0))],
        out_shape=[jax.ShapeDtypeStruct((s, 1024), BF16), jax.ShapeDtypeStruct((s, 1024), BF16),
                   jax.ShapeDtypeStruct((s, 128), F32), jax.ShapeDtypeStruct((1, 1024), F32),
                   jax.ShapeDtypeStruct((1, 1024), F32), jax.ShapeDtypeStruct((1, 128), F32)],
        scratch_shapes=[pltpu.VMEM((1, 128), F32)], compiler_params=_params(("arbitrary",)),
    )(dqn, dkn, dc0, dc1, proj, proj, proj, smallp, qw, kw, sel, selt, fold_even, fold_odd)


def local_step(x, target, wx, later_shards, ssd_cw8, ssd_cb, smallp, ssd_nw, qw_t, kw_t, sel, selt,
               norm_mix_w, norm_ffn_w, ffn_cw8, ffn_cb):
    proj, h_t = rms_in_proj(x, norm_mix_w, wx)
    y_ssd, y_ssd_t, ypre, states = ssd_fwd(proj, ssd_cw8, ssd_cb, smallp, ssd_nw)
    place_q, place_k, ones_q, ones_k, fold_even, fold_odd = fox_tables()
    qn, kn, aq, ak, vb, knt, akt, vt = fox_prep(proj, smallp, qw_t, kw_t, sel, selt, place_q, place_k, ones_q, ones_k)
    y_fox, y_fox_t, lse, (a_out, a_up, a_down) = fox_fwd(qn, kn, aq, ak, vt, shards=later_shards)
    w_out = a_out.reshape(2048, D_MODEL)
    w_down = a_down.reshape(D_FF, D_MODEL)
    s = x.shape[0]
    shard = lambda index: pl.BlockSpec((None, 1024, 1408), index)
    x1, hf, hf_t = out_proj_rms_fwd(y_ssd, y_fox, w_out, x, norm_ffn_w)
    hu, act, act_t = up_ffn_fwd(hf, a_up, ffn_cw8, ffn_cb)
    dy, sq = down_proj_loss(act, w_down, x1, target)

    dact = matmul(dy, w_down, mode="nt", tm=1024, tn=1408, tk=1024, out_dtype=F32, name="mm_dact")
    g_down = matmul(act_t, dy, mode="nn", tm=1408, tn=1024, tk=1024, out_dtype=BF16, name="mm_dw_down")
    dhu, gcw_g, gcw_v = ffn_mid_bwd(hu, dact, ffn_cw8, ffn_cb)
    g_up = matmul(hf_t, dhu, mode="nn", tm=1024, tn=1408, tk=1024, out_dtype=BF16, name="mm_dw_up",
                  layout=dict(m=D_MODEL, n=2 * D_FF, k=s, b_spec=shard(lambda i, j, kk: (j // 2, kk, j % 2)),
                              o_spec=shard(lambda i, j, kk: (j, i, 0)), out_shape=(4, D_MODEL, 1408)))
    dx1, g_norm_ffn, dmixed = rms_bwd_matmul(dhu, a_up, x1, norm_ffn_w, dy, w_out, name="dhf_rms_ffn_bwd_dmixed")
    g_out_a = matmul(y_ssd_t, dx1, mode="nn", tm=1024, tn=1024, tk=1024, out_dtype=BF16, name="mm_dw_out_ssd")
    g_out_b = matmul(y_fox_t, dx1, mode="nn", tm=1024, tn=1024, tk=1024, out_dtype=BF16, name="mm_dw_out_fox")
    early = [jnp.concatenate([g_out_a, g_out_b], axis=0).reshape(4, 512, D_MODEL), g_up, g_down.reshape(4, 704, D_MODEL)]
    dz, dxs, db, dc, dsmall_ssd, gcw_x, gcw_b, gcw_c, g_sp, g_ssd_nw, theirs = ssd_bwd(
        proj, ssd_cw8, ssd_cb, smallp, ssd_nw, ypre, states, dmixed, sel, swap=early)
    core = lax.axis_index("c").astype(jnp.int32).reshape(1)
    parts = [add_pair(a, b, core, name="add_pair_" + n, tr=ADAM_ROWS[n]) for a, b, n in zip(early, theirs, BIG_NAMES[1:])]
    dqn, dkn, dv, dc0, dc1, landed_early = fox_bwd(qn, kn, aq, ak, knt, akt, vb, lse, dmixed, parts=parts)
    dq, dk, dsmall_fox, g_qw, g_kw, g_fb = fox_post(dqn, dkn, dc0, dc1, proj, smallp, qw_t, kw_t, sel, selt,
                                                    fold_even, fold_odd)
    dproj = jnp.concatenate([dz, dxs, dq, dk, dv.astype(BF16), db, dc, (dsmall_ssd + dsmall_fox).astype(BF16)], axis=1)
    g_wx = matmul(h_t, dproj, mode="nn", tm=1024, tn=PROJ_TILE, tk=1024, out_dtype=BF16, name="mm_dw_in")
    g_in = _in_grad_shards(g_wx)
    part_in = add_pair(g_in, pair_swap_halves([g_in], name="pair_swap_w_in")[0], core, name="add_pair_w_in",
                       tr=ADAM_ROWS["w_in"])
    grad_x, g_norm_mix, landed_in = matmul_rms_bwd(dproj, wx, x, norm_mix_w, dx1, scatter=[part_in])
    return dict(
        sq=sq, grad_x=grad_x, landed=landed_in + landed_early,
        g_norm_mix=g_norm_mix, g_norm_ffn=g_norm_ffn, g_ssd_nw=g_ssd_nw,
        g_ssd_cw=jnp.concatenate([gcw_x, gcw_b, gcw_c], axis=1), g_sp=g_sp, g_fb=g_fb, g_qw=g_qw, g_kw=g_kw,
        g_ffn_cw=jnp.concatenate([gcw_g, gcw_v], axis=1))


def adamw(w, g, m, v, *, name, tr, allreduce=None):
    rows, cols = w.shape
    nsteps = rows // tr

    def body(*refs):
        if allreduce is None:
            w_ref, g_ref, m_ref, v_ref, d_ref, mo_ref, vo_ref = refs
        else:
            w_ref, g_ref, m_ref, v_ref, packed_ref, d_ref, mo_ref, vo_ref, summed_ref = refs[:9]
            start, finish = _allreduce_phases(packed_ref, summed_ref, *refs[9:])
            pl.when(pl.program_id(0) == 0)(start)
        gv = g_ref[...]
        mn = ADAM_B1 * m_ref[...] + (1.0 - ADAM_B1) * gv
        vn = ADAM_B2 * v_ref[...] + (1.0 - ADAM_B2) * (gv * gv)
        m_hat = mn * (1.0 / (1.0 - ADAM_B1 ** ADAM_STEP))
        v_hat = vn * (1.0 / (1.0 - ADAM_B2 ** ADAM_STEP))
        inv = pl.reciprocal(jnp.sqrt(v_hat) + ADAM_EPS, approx=True)
        d_ref[...] = -ADAM_LR * (m_hat * inv + ADAM_WD * w_ref[...])
        mo_ref[...] = mn
        vo_ref[...] = vn
        if allreduce is not None:
            pl.when(pl.program_id(0) == nsteps - 1)(finish)

    blk = pl.BlockSpec((tr, cols), lambda i: (i, 0))
    shp = jax.ShapeDtypeStruct((rows, cols), F32)
    if allreduce is None:
        return pl.pallas_call(
            body, name=name, grid=(nsteps,), in_specs=[blk] * 4, out_specs=[blk] * 3, out_shape=[shp] * 3,
            compiler_params=_params(("parallel",)),
        )(w, g, m, v)
    whole = pl.BlockSpec(memory_space=pltpu.VMEM)
    return pl.pallas_call(
        body, name=name, grid=(nsteps,), in_specs=[blk] * 4 + [whole], out_specs=[blk] * 3 + [whole],
        out_shape=[shp] * 3 + [jax.ShapeDtypeStruct(allreduce.shape, F32)],
        scratch_shapes=_allreduce_scratch(allreduce.shape[0]), compiler_params=_params(("arbitrary",)),
    )(w, g, m, v, allreduce)


def add_pair(full, theirs, core, *, name, tr):
    _, rows, cols = theirs.shape
    nblk = rows // tr

    def body(c_ref, a_ref, b_ref, o_ref):
        o_ref[...] = (a_ref[...].astype(F32) + b_ref[...].astype(F32)).astype(BF16)

    blk = pl.BlockSpec((1, tr, cols), lambda j, i, c: (j, i, 0))
    grid_spec = pltpu.PrefetchScalarGridSpec(
        num_scalar_prefetch=1, grid=(4, nblk),
        in_specs=[pl.BlockSpec((1, tr, cols), lambda j, i, c: (j, c[0] * nblk + i, 0)), blk], out_specs=blk)
    return pl.pallas_call(
        body, name=name, grid_spec=grid_spec, out_shape=jax.ShapeDtypeStruct(theirs.shape, BF16),
        compiler_params=_params(("parallel", "parallel")),
    )(core, full, theirs)


def sum_chips(parts, core, *, name, tr):
    _, rows, cols = parts.shape
    nblk = rows // tr

    def body(c_ref, p_ref, o_ref):
        acc = p_ref[0].astype(F32)
        for k in range(1, 4):
            acc = acc + p_ref[k].astype(F32)
        o_ref[...] = acc

    grid_spec = pltpu.PrefetchScalarGridSpec(
        num_scalar_prefetch=1, grid=(nblk,), in_specs=[pl.BlockSpec((4, tr, cols), lambda i, c: (0, i, 0))],
        out_specs=pl.BlockSpec((tr, cols), lambda i, c: (c[0] * nblk + i, 0)))
    return pl.pallas_call(
        body, name=name, grid_spec=grid_spec, out_shape=jax.ShapeDtypeStruct((2 * rows, cols), F32),
        compiler_params=_params(("parallel",)),
    )(core, parts)


ANY = pl.BlockSpec(memory_space=pl.ANY)


def _place():
    x, y, c = lax.axis_index("x"), lax.axis_index("y"), lax.axis_index("c")
    chips = [(1 - x, y), (x, 1 - y), (1 - x, 1 - y)]
    return x, y, c, chips


def _chunks(rows):
    size = next((c for c in (128, 176, 64, 32, 16, 8) if rows % c == 0), rows)
    return [(r, size) for r in range(0, rows, size)]


def gather_weights(shards):
    n = len(shards)

    def body(*refs):
        start, forward, finish = _gather_phases(refs[:n], refs[n:2 * n], *refs[2 * n:])
        start()
        forward()
        finish()

    gathered = pl.pallas_call(
        body, name="gather_weights", in_specs=[ANY] * n, out_specs=[ANY] * n,
        out_shape=_gather_out_shapes(shards), scratch_shapes=_gather_scratch(n),
    )(*shards)
    return gathered


def _gather_out_shapes(shards):
    return [jax.ShapeDtypeStruct((4,) + s.shape, s.dtype) for s in shards]


def _gather_scratch(n):
    return [pltpu.SemaphoreType.DMA((n, 7)), pltpu.SemaphoreType.DMA((n, 7))]


def _gather_phases(ins, outs, send_sems, recv_sems):
    n = len(ins)
    x, y, c, chips = _place()
    me = 2 * x + y
    sibling = (x, y, 1 - c)
    blks = [2 * cx + cy for cx, cy in chips]

    def half(a, blk, r=0, nr=None):
        rows = ins[a].shape[0] // 2
        return outs[a].at[blk, pl.ds(c * rows + r, rows if nr is None else nr), :]

    def to_chip(a, t, r=0, nr=None):
        rows = ins[a].shape[0] // 2
        return pltpu.make_async_remote_copy(
            src_ref=ins[a].at[pl.ds(c * rows + r, rows if nr is None else nr), :], dst_ref=half(a, me, r, nr),
            send_sem=send_sems.at[a, t], recv_sem=recv_sems.at[a, t], device_id=(*chips[t], c), device_id_type=MESH)

    def from_chip(a, t):
        return pltpu.make_async_remote_copy(
            src_ref=half(a, blks[t]), dst_ref=half(a, blks[t]), send_sem=send_sems.at[a, t], recv_sem=recv_sems.at[a, t],
            device_id=(*chips[t], c), device_id_type=MESH)

    def to_sibling(a, t, r=0, nr=None):
        return pltpu.make_async_remote_copy(
            src_ref=half(a, blks[t], r, nr), dst_ref=half(a, blks[t], r, nr), send_sem=send_sems.at[a, 3 + t],
            recv_sem=recv_sems.at[a, 3 + t], device_id=sibling, device_id_type=MESH)

    def from_sibling(a, t):
        rows = ins[a].shape[0] // 2
        dst = outs[a].at[blks[t], pl.ds((1 - c) * rows, rows), :]
        return pltpu.make_async_remote_copy(
            src_ref=dst, dst_ref=dst, send_sem=send_sems.at[a, 3 + t], recv_sem=recv_sems.at[a, 3 + t],
            device_id=sibling, device_id_type=MESH)

    def own(a, r=0, nr=None):
        return pltpu.make_async_remote_copy(
            src_ref=ins[a].at[pl.ds(r, ins[a].shape[0] if nr is None else nr), :],
            dst_ref=outs[a].at[me, pl.ds(r, ins[a].shape[0] if nr is None else nr), :],
            send_sem=send_sems.at[a, 6], recv_sem=recv_sems.at[a, 6], device_id=sibling, device_id_type=MESH)

    def start():
        for a in range(n):
            for t in range(3):
                for r, nr in _chunks(ins[a].shape[0] // 2):
                    to_chip(a, t, r, nr).start()
            for r, nr in _chunks(ins[a].shape[0]):
                own(a, r, nr).start()

    def forward():
        for a in range(n):
            for t in range(3):
                from_chip(a, t).wait_recv()
                for r, nr in _chunks(ins[a].shape[0] // 2):
                    to_sibling(a, t, r, nr).start()

    def finish():
        for a in range(n):
            for t in range(3):
                from_sibling(a, t).wait_recv()
        for a in range(n):
            for t in range(3):
                to_chip(a, t).wait_send()
                to_sibling(a, t).wait_send()
            own(a).wait()

    return start, forward, finish


def pair_swap_halves(grads, *, name):
    n = len(grads)

    def body(*refs):
        start, finish = _pair_swap_phases(refs[:n], refs[n:2 * n], *refs[2 * n:])
        start()
        finish()

    return pl.pallas_call(
        body, name=name, in_specs=[ANY] * n, out_specs=[ANY] * n, out_shape=_pair_swap_out_shapes(grads),
        scratch_shapes=_pair_swap_scratch(n),
    )(*grads)


def _pair_swap_out_shapes(grads):
    return [jax.ShapeDtypeStruct((4, g.shape[1] // 2, g.shape[2]), g.dtype) for g in grads]


def _pair_swap_scratch(n):
    return [pltpu.SemaphoreType.DMA((n,)), pltpu.SemaphoreType.DMA((n,))]


def _pair_swap_phases(ins, theirs, send_sems, recv_sems):
    n = len(ins)
    x, y, c, _ = _place()
    sibling = (x, y, 1 - c)

    def start():
        for a in range(n):
            rows = ins[a].shape[1] // 2
            for j in range(4):
                for r, nr in _chunks(rows):
                    pltpu.make_async_remote_copy(
                        src_ref=ins[a].at[j, pl.ds((1 - c) * rows + r, nr), :], dst_ref=theirs[a].at[j, pl.ds(r, nr), :],
                        send_sem=send_sems.at[a], recv_sem=recv_sems.at[a], device_id=sibling, device_id_type=MESH).start()

    def finish():
        for a in range(n):
            pltpu.make_async_remote_copy(src_ref=theirs[a], dst_ref=theirs[a], send_sem=send_sems.at[a],
                                         recv_sem=recv_sems.at[a], device_id=sibling, device_id_type=MESH).wait()

    return start, finish


def _scatter_scratch(n):
    return [pltpu.SemaphoreType.DMA((n, 3)), pltpu.SemaphoreType.DMA((n, 3))]


def _keep_own_blocks(landed, parts):
    if not parts:
        return []
    chip = 2 * lax.axis_index("x") + lax.axis_index("y")
    return [lax.dynamic_update_slice(l, lax.dynamic_slice_in_dim(p, chip, 1, axis=0), (chip, 0, 0))
            for l, p in zip(landed, parts)]


def _scatter_phases(ins, outs, send_sems, recv_sems):
    n = len(ins)
    x, y, c, chips = _place()
    me = 2 * x + y
    blks = [2 * cx + cy for cx, cy in chips]

    def start():
        for a in range(n):
            for r, nr in _chunks(ins[a].shape[1]):
                for t in range(3):
                    pltpu.make_async_remote_copy(
                        src_ref=ins[a].at[blks[t], pl.ds(r, nr), :], dst_ref=outs[a].at[me, pl.ds(r, nr), :],
                        send_sem=send_sems.at[a, t], recv_sem=recv_sems.at[a, t],
                        device_id=(*chips[t], c), device_id_type=MESH).start()

    def finish():
        for a in range(n):
            for t in range(3):
                pltpu.make_async_remote_copy(
                    src_ref=outs[a].at[blks[t]], dst_ref=outs[a].at[blks[t]], send_sem=send_sems.at[a, t],
                    recv_sem=recv_sems.at[a, t], device_id=(*chips[t], c), device_id_type=MESH).wait()

    return start, finish


def pair_join_halves(bufs):
    n = len(bufs)

    def body(*refs):
        outs = refs[n:2 * n]
        send_sems, recv_sems = refs[2 * n:]
        x, y, c, _ = _place()
        sibling = (x, y, 1 - c)
        for a in range(n):
            rows = outs[a].shape[0] // 2
            for r, nr in _chunks(rows):
                mine = outs[a].at[pl.ds(c * rows + r, nr), :]
                pltpu.make_async_remote_copy(src_ref=mine, dst_ref=mine, send_sem=send_sems.at[a], recv_sem=recv_sems.at[a],
                                             device_id=sibling, device_id_type=MESH).start()
        for a in range(n):
            rows = outs[a].shape[0] // 2
            pltpu.make_async_remote_copy(
                src_ref=outs[a].at[pl.ds(c * rows, rows), :], dst_ref=outs[a].at[pl.ds((1 - c) * rows, rows), :],
                send_sem=send_sems.at[a], recv_sem=recv_sems.at[a], device_id=sibling, device_id_type=MESH).wait()

    return pl.pallas_call(
        body, name="pair_join_halves", in_specs=[ANY] * n, out_specs=[ANY] * n,
        out_shape=[jax.ShapeDtypeStruct(b.shape, b.dtype) for b in bufs], input_output_aliases={a: a for a in range(n)},
        scratch_shapes=[pltpu.SemaphoreType.DMA((n,)), pltpu.SemaphoreType.DMA((n,))],
    )(*bufs)


def _allreduce_scratch(rows):
    return [pltpu.VMEM((8, rows, 128), F32), pltpu.SemaphoreType.DMA((7,)), pltpu.SemaphoreType.DMA((7,))]


def _allreduce_phases(in_ref, out_ref, gathered, send_sems, recv_sems):
    x, y, c, _ = _place()
    me = 4 * x + 2 * y + c
    flips = [(fx, fy, fc) for fx in (0, 1) for fy in (0, 1) for fc in (0, 1)][1:]
    peers = [((1 - x) if fx else x, (1 - y) if fy else y, (1 - c) if fc else c) for fx, fy, fc in flips]

    def send(t):
        return pltpu.make_async_remote_copy(
            src_ref=in_ref, dst_ref=gathered.at[me], send_sem=send_sems.at[t], recv_sem=recv_sems.at[t],
            device_id=peers[t], device_id_type=MESH)

    def start():
        gathered[me] = in_ref[...]
        for t in range(7):
            send(t).start()

    def finish():
        for t, (px, py, pc) in enumerate(peers):
            slot = gathered.at[4 * px + 2 * py + pc]
            pltpu.make_async_remote_copy(
                src_ref=slot, dst_ref=slot, send_sem=send_sems.at[t], recv_sem=recv_sems.at[t],
                device_id=(px, py, pc), device_id_type=MESH).wait_recv()
        for t in range(7):
            send(t).wait_send()
        acc = gathered[0]
        for k in range(1, 8):
            acc = acc + gathered[k]
        out_ref[...] = acc

    return start, finish


SMALL_NAMES = ("norm_mix_w", "ssd_conv_w", "ssd_conv_b", "ssd_dt_bias", "ssd_a_log", "ssd_d", "ssd_norm_w", "fox_f_bias",
               "fox_q_norm_w", "fox_k_norm_w", "norm_ffn_w", "ffn_conv_w", "ffn_conv_b")
BIG_NAMES = ("w_in", "w_out", "w_up", "w_down")
WEIGHT_ORDER = ("norm_mix_w", "w_in", "ssd_conv_w", "ssd_conv_b", "ssd_dt_bias", "ssd_a_log", "ssd_d", "ssd_norm_w",
                "fox_f_bias", "fox_q_norm_w", "fox_k_norm_w", "w_out", "norm_ffn_w", "w_up", "ffn_conv_w", "ffn_conv_b", "w_down")
ADAM_ROWS = {"w_in": 256, "w_out": 256, "w_up": 256, "w_down": 176}


def _pack(arrays):
    pieces = []
    for a in arrays:
        flat = a.reshape(-1).astype(F32)
        pieces += [flat, jnp.zeros(((-flat.shape[0]) % 1024,), F32)]
    return jnp.concatenate(pieces).reshape(-1, 128)


def _unpack(packed, shapes):
    out, r = [], 0
    for shp in shapes:
        size = 1
        for d in shp:
            size *= d
        nrow = 8 * (-(-size // 1024))
        out.append(packed[r:r + nrow].reshape(-1)[:size].reshape(shp))
        r += nrow
    return out


IN_SHARD = IN_COLS // 4
IN_SEGMENTS = ((0, 2048, 0), (2048, 2560, 5120), (2560, 2576, MAIN_COLS), (2576, 5648, 2048), (5648, 5664, MAIN_COLS + F_LANE))


def _in_cols(shards, lo, hi):
    out = []
    for j in range(4):
        a, b = max(lo, IN_SHARD * j), min(hi, IN_SHARD * (j + 1))
        if a < b:
            out.append(shards[j][:, a - IN_SHARD * j:b - IN_SHARD * j])
    return out


def _in_grad_shards(g):
    shards = []
    for j in range(4):
        pieces = []
        for lo, hi, at in IN_SEGMENTS:
            a, b = max(lo, IN_SHARD * j), min(hi, IN_SHARD * (j + 1))
            if a < b:
                pieces.append(g[:, at + a - lo:at + b - lo])
        shards.append(jnp.concatenate(pieces, axis=1))
    return jnp.stack(shards)


def _pad_rows(a, rows):
    return jnp.pad(a, ((0, rows - a.shape[0]), (0, 0)))


def kernel(x, norm_mix_w, w_in, ssd_conv_w, ssd_conv_b, ssd_dt_bias, ssd_a_log, ssd_d, ssd_norm_w, fox_f_bias, fox_q_norm_w, fox_k_norm_w, w_out, norm_ffn_w, w_up, ffn_conv_w, ffn_conv_b, w_down, loss_target, m_norm_mix_w, m_w_in, m_ssd_conv_w, m_ssd_conv_b, m_ssd_dt_bias, m_ssd_a_log, m_ssd_d, m_ssd_norm_w, m_fox_f_bias, m_fox_q_norm_w, m_fox_k_norm_w, m_w_out, m_norm_ffn_w, m_w_up, m_ffn_conv_w, m_ffn_conv_b, m_w_down, v_norm_mix_w, v_w_in, v_ssd_conv_w, v_ssd_conv_b, v_ssd_dt_bias, v_ssd_a_log, v_ssd_d, v_ssd_norm_w, v_fox_f_bias, v_fox_q_norm_w, v_fox_k_norm_w, v_w_out, v_norm_ffn_w, v_w_up, v_ffn_conv_w, v_ffn_conv_b, v_w_down):
    w = dict(norm_mix_w=norm_mix_w, w_in=w_in, ssd_conv_w=ssd_conv_w, ssd_conv_b=ssd_conv_b, ssd_dt_bias=ssd_dt_bias,
             ssd_a_log=ssd_a_log, ssd_d=ssd_d, ssd_norm_w=ssd_norm_w, fox_f_bias=fox_f_bias, fox_q_norm_w=fox_q_norm_w,
             fox_k_norm_w=fox_k_norm_w, w_out=w_out, norm_ffn_w=norm_ffn_w, w_up=w_up, ffn_conv_w=ffn_conv_w,
             ffn_conv_b=ffn_conv_b, w_down=w_down)
    m = dict(norm_mix_w=m_norm_mix_w, w_in=m_w_in, ssd_conv_w=m_ssd_conv_w, ssd_conv_b=m_ssd_conv_b, ssd_dt_bias=m_ssd_dt_bias,
             ssd_a_log=m_ssd_a_log, ssd_d=m_ssd_d, ssd_norm_w=m_ssd_norm_w, fox_f_bias=m_fox_f_bias, fox_q_norm_w=m_fox_q_norm_w,
             fox_k_norm_w=m_fox_k_norm_w, w_out=m_w_out, norm_ffn_w=m_norm_ffn_w, w_up=m_w_up, ffn_conv_w=m_ffn_conv_w,
             ffn_conv_b=m_ffn_conv_b, w_down=m_w_down)
    v = dict(norm_mix_w=v_norm_mix_w, w_in=v_w_in, ssd_conv_w=v_ssd_conv_w, ssd_conv_b=v_ssd_conv_b, ssd_dt_bias=v_ssd_dt_bias,
             ssd_a_log=v_ssd_a_log, ssd_d=v_ssd_d, ssd_norm_w=v_ssd_norm_w, fox_f_bias=v_fox_f_bias, fox_q_norm_w=v_fox_q_norm_w,
             fox_k_norm_w=v_fox_k_norm_w, w_out=v_w_out, norm_ffn_w=v_norm_ffn_w, w_up=v_w_up, ffn_conv_w=v_ffn_conv_w,
             ffn_conv_b=v_ffn_conv_b, w_down=v_w_down)
    chip = 2 * lax.axis_index("x") + lax.axis_index("y")

    a_in, a_scw, a_fcw = gather_weights([w_in[0].astype(BF16), _pad_rows(ssd_conv_w[0], 16), _pad_rows(ffn_conv_w[0], 16)])
    later_shards = [w_out[0].astype(BF16), w_up[0].astype(BF16), w_down[0].astype(BF16)]
    wx = jnp.concatenate([p for lo, hi, _ in sorted(IN_SEGMENTS, key=lambda seg: seg[2]) for p in _in_cols(a_in, lo, hi)]
                         + [jnp.zeros((D_MODEL, PROJ_COLS - IN_COLS), BF16)], axis=1)
    ssd_cw8 = a_scw.transpose(1, 0, 2).reshape(16, 1536)[:8]
    ffn_cw8 = a_fcw.transpose(1, 0, 2).reshape(16, 2 * D_FF)[:8]
    gap = lambda n: jnp.zeros((n,), F32)
    smallp = jnp.concatenate([ssd_dt_bias[0], gap(112), ssd_a_log[0], gap(112), ssd_d[0], gap(112),
                              gap(F_LANE), fox_f_bias[0], gap(128 - F_LANE - N_HEADS), gap(4 * 128)]).reshape(8, 128)
    qw_t = jnp.tile(fox_q_norm_w[0], N_HEADS)[None]
    kw_t = jnp.tile(fox_k_norm_w[0], N_HEADS)[None]
    sel = jnp.asarray((np.arange(1024)[:, None] // HEAD_DIM == np.arange(128)[None, :]).astype(np.float32), BF16)

    res = local_step(x[0], loss_target[0], wx, later_shards, ssd_cw8, ssd_conv_b, smallp, ssd_norm_w, qw_t, kw_t,
                     sel, sel.T, norm_mix_w, norm_ffn_w, ffn_cw8, ffn_conv_b)

    full_shapes = [(1, 1024), (1, 4, 1536), (1, 1536), (1, 16), (1, 16), (1, 16), (1, 1024), (1, 16), (1, 64), (1, 64),
                   (1, 1024), (1, 3, 2 * D_FF), (1, 2 * D_FF), (1,)]
    local_small = [res["g_norm_mix"], res["g_ssd_cw"][:4], res["g_ssd_cw"][4], res["g_sp"][0, :16], res["g_sp"][1, :16],
                   res["g_sp"][2, :16], res["g_ssd_nw"], res["g_fb"][0, F_LANE:F_LANE + 16],
                   res["g_qw"].reshape(N_HEADS, HEAD_DIM).sum(0), res["g_kw"].reshape(N_HEADS, HEAD_DIM).sum(0),
                   res["g_norm_ffn"], res["g_ffn_cw"][:3], res["g_ffn_cw"][3], jnp.sum(res["sq"])]
    landed = res["landed"]
    core = lax.axis_index("c").astype(jnp.int32).reshape(1)
    halves = [sum_chips(p, core, name="sum_chips_" + n, tr=ADAM_ROWS[n]) for p, n in zip(landed, BIG_NAMES)]
    g_big = dict(zip(BIG_NAMES, pair_join_halves(halves)))

    grads, deltas, new_m, new_v = {}, {}, {}, {}
    for n in BIG_NAMES:
        out = adamw(w[n][0], g_big[n], m[n][0], v[n][0], name="adamw_" + n, tr=ADAM_ROWS[n],
                    allreduce=_pack(local_small) if n == BIG_NAMES[0] else None)
        if n == BIG_NAMES[0]:
            summed = _unpack(out[3], full_shapes)
        d, mn, vn = out[:3]
        grads[n], deltas[n], new_m[n], new_v[n] = g_big[n][None], d[None], mn[None], vn[None]
    loss = (0.5 / D_MODEL) * summed[-1][0]
    g_small = dict(zip(SMALL_NAMES, summed[:-1]))
    g_small["ssd_conv_w"] = lax.dynamic_slice(g_small["ssd_conv_w"], (0, 0, 384 * chip), (1, 4, 384))
    g_small["ffn_conv_w"] = lax.dynamic_slice(g_small["ffn_conv_w"], (0, 0, 1408 * chip), (1, 3, 1408))
    shapes = [w[n].shape for n in SMALL_NAMES]
    packed_w = _pack([w[n] for n in SMALL_NAMES])
    d, mn, vn = adamw(packed_w, _pack([g_small[n] for n in SMALL_NAMES]), _pack([m[n] for n in SMALL_NAMES]),
                      _pack([v[n] for n in SMALL_NAMES]), name="adamw_small", tr=packed_w.shape[0])
    for n, dd, mm, vv in zip(SMALL_NAMES, _unpack(d, shapes), _unpack(mn, shapes), _unpack(vn, shapes)):
        grads[n], deltas[n], new_m[n], new_v[n] = g_small[n].reshape(w[n].shape), dd, mm, vv
    return (loss, res["grad_x"][None], *[grads[n] for n in WEIGHT_ORDER], *[deltas[n] for n in WEIGHT_ORDER],
            *[new_m[n] for n in WEIGHT_ORDER], *[new_v[n] for n in WEIGHT_ORDER])
```
